```python
import math
import jax, jax.numpy as jnp
from jax import lax
import numpy as np

D_MODEL = 1024
BATCH = 8
SEQ = 4096
DEPTH = 1

N_META = 16
BLOCK_Q = 128
RMS_EPS = 1e-6

MLA_HEADS = 16
MLA_Q_RANK = 256
MLA_KV_RANK = 128
MLA_NOPE_DIM = 64
MLA_ROPE_DIM = 32
MLA_V_DIM = 64
MLA_WIDTH = MLA_HEADS * MLA_V_DIM
MLA_SCALE = 1.0 / math.sqrt(MLA_NOPE_DIM + MLA_ROPE_DIM)
ROPE_THETA = 10000.0

FOX_HEADS = 16
FOX_HEAD_DIM = 64
FOX_WIDTH = FOX_HEADS * FOX_HEAD_DIM
FOX_SCALE = 1.0 / math.sqrt(FOX_HEAD_DIM)

IN_SPLITS = (MLA_Q_RANK, MLA_KV_RANK, MLA_ROPE_DIM, MLA_WIDTH,
             FOX_WIDTH, FOX_WIDTH, FOX_WIDTH, FOX_HEADS, FOX_WIDTH,
             D_MODEL, D_MODEL)
IN_WIDTH = sum(IN_SPLITS)

kernel_name = 'hybrid_mla_fox_gated_merge'


def rmsnorm(x, g):
    xf = x.astype(jnp.float32)
    y = xf * lax.rsqrt(jnp.mean(xf * xf, axis=-1, keepdims=True) + RMS_EPS)
    return (y * g.astype(jnp.float32)).astype(x.dtype)


def rope(x, pos):
    half = x.shape[-1] // 2
    inv_freq = ROPE_THETA ** (-jnp.arange(half, dtype=jnp.float32) / half)
    ang = pos.astype(jnp.float32)[:, None] * inv_freq[None, :]
    cos, sin = jnp.cos(ang), jnp.sin(ang)
    x1 = x[..., :half].astype(jnp.float32)
    x2 = x[..., half:].astype(jnp.float32)
    return jnp.concatenate([x1 * cos - x2 * sin, x1 * sin + x2 * cos], axis=-1).astype(x.dtype)


def causal_block_attention(q, k, v, scale, cum=None):
    B, H, L, _ = q.shape
    n_real = L - N_META
    nb = n_real // BLOCK_Q
    key_pos = jnp.arange(L)

    def attend(qb, qpos, cq=None):
        s = jnp.einsum('bhqd,bhkd->bhqk', qb, k, preferred_element_type=jnp.float32) * scale
        if cq is not None:
            s = s + (cq[..., :, None] - cum[:, :, None, :])
        s = jnp.where(key_pos[None, :] <= qpos[:, None], s, -jnp.inf)
        p = jax.nn.softmax(s, axis=-1)
        return jnp.einsum('bhqk,bhkd->bhqd', p.astype(v.dtype), v)

    out_meta = attend(q[:, :, :N_META], key_pos[:N_META],
                      None if cum is None else cum[:, :, :N_META])

    def to_blocks(a):
        a = a[:, :, N_META:]
        a = a.reshape((B, H, nb, BLOCK_Q) + a.shape[3:])
        return jnp.moveaxis(a, 2, 0)

    xs = (to_blocks(q), key_pos[N_META:].reshape(nb, BLOCK_Q))
    if cum is not None:
        xs = xs + (to_blocks(cum),)
    out_real = lax.map(lambda args: attend(*args), xs)
    out_real = jnp.moveaxis(out_real, 0, 2).reshape(B, H, n_real, v.shape[-1])
    return jnp.concatenate([out_meta, out_real], axis=2)


def _fwd_setup_inputs(seed: int = 0) -> dict:
    key = jax.random.key(seed)
    ks = jax.random.split(key, 13)
    f32 = jnp.float32

    def gain(k, n):
        return 1.0 + 0.1 * jax.random.normal(k, (DEPTH, n), f32)

    def dense(k, fan_in, fan_out):
        return jax.random.normal(k, (DEPTH, fan_in, fan_out), f32) * fan_in ** -0.5

    return {
        'x': jax.random.normal(ks[0], (BATCH, SEQ, D_MODEL), f32),
        'meta_tokens': jax.random.normal(ks[1], (N_META, D_MODEL), f32),
        'pre_norm_g': gain(ks[2], D_MODEL),
        'w_in': dense(ks[3], D_MODEL, IN_WIDTH),
        'fox_forget_b': jax.random.uniform(ks[4], (DEPTH, FOX_HEADS), f32, 1.0, 4.0),
        'mla_q_norm_g': gain(ks[5], MLA_Q_RANK),
        'mla_kv_norm_g': gain(ks[6], MLA_KV_RANK),
        'w_uq': dense(ks[7], MLA_Q_RANK, MLA_HEADS * (MLA_NOPE_DIM + MLA_ROPE_DIM)),
        'w_ukv': dense(ks[8], MLA_KV_RANK, MLA_HEADS * (MLA_NOPE_DIM + MLA_V_DIM)),
        'w_br_mla': dense(ks[9], MLA_WIDTH, D_MODEL),
        'w_br_fox': dense(ks[10], FOX_WIDTH, D_MODEL),
        'w_out': dense(ks[11], D_MODEL, D_MODEL),
        'post_norm_g': gain(ks[12], D_MODEL),
    }


def _fwd_reference(x, meta_tokens, pre_norm_g, w_in, fox_forget_b, mla_q_norm_g, mla_kv_norm_g,
              w_uq, w_ukv, w_br_mla, w_br_fox, w_out, post_norm_g):
    B, S, D = x.shape
    L = S + N_META
    pos = jnp.arange(L)
    h = jnp.concatenate([jnp.broadcast_to(meta_tokens[None].astype(x.dtype), (B, N_META, D)), x], axis=1)
    split_idx = np.cumsum(IN_SPLITS)[:-1].tolist()

    def fox_heads(t):
        return t.reshape(B, L, FOX_HEADS, FOX_HEAD_DIM).transpose(0, 2, 1, 3)

    for l in range(DEPTH):
        u = rmsnorm(h, pre_norm_g[l])
        proj = u @ w_in[l]
        (cq, ckv, k_pe_raw, z_mla, fq, fk, fv, f_logit, z_fox, gate_a, gate_b) = jnp.split(proj, split_idx, axis=-1)

        q = (rmsnorm(cq, mla_q_norm_g[l]) @ w_uq[l]).reshape(B, L, MLA_HEADS, MLA_NOPE_DIM + MLA_ROPE_DIM).transpose(0, 2, 1, 3)
        kv = (rmsnorm(ckv, mla_kv_norm_g[l]) @ w_ukv[l]).reshape(B, L, MLA_HEADS, MLA_NOPE_DIM + MLA_V_DIM).transpose(0, 2, 1, 3)
        q_nope, q_pe = q[..., :MLA_NOPE_DIM], q[..., MLA_NOPE_DIM:]
        k_nope, v_mla = kv[..., :MLA_NOPE_DIM], kv[..., MLA_NOPE_DIM:]
        k_pe = rope(k_pe_raw, pos)[:, None]
        q_m = jnp.concatenate([q_nope, rope(q_pe, pos)], axis=-1)
        k_m = jnp.concatenate([k_nope, jnp.broadcast_to(k_pe, (B, MLA_HEADS, L, MLA_ROPE_DIM))], axis=-1)
        o_mla = causal_block_attention(q_m, k_m, v_mla, MLA_SCALE)
        o_mla = o_mla.transpose(0, 2, 1, 3).reshape(B, L, MLA_WIDTH)
        y_mla = (o_mla * jax.nn.silu(z_mla)) @ w_br_mla[l]

        log_f = jax.nn.log_sigmoid((f_logit + fox_forget_b[l]).astype(jnp.float32)).transpose(0, 2, 1)
        cum = jnp.cumsum(log_f, axis=-1)
        o_fox = causal_block_attention(fox_heads(fq), fox_heads(fk), fox_heads(fv), FOX_SCALE, cum)
        o_fox = o_fox.transpose(0, 2, 1, 3).reshape(B, L, FOX_WIDTH)
        y_fox = (o_fox * jax.nn.silu(z_fox)) @ w_br_fox[l]

        mixed = (jax.nn.sigmoid(gate_a) * y_mla + jax.nn.sigmoid(gate_b) * y_fox) @ w_out[l]
        h = h + rmsnorm(mixed, post_norm_g[l])

    return h[:, N_META:]


import jax as _jax
import jax.numpy as _jnp

TWIN_FORMAT = 'train_step'
FWD_PARAMS = ['x', 'meta_tokens', 'pre_norm_g', 'w_in', 'fox_forget_b', 'mla_q_norm_g', 'mla_kv_norm_g', 'w_uq', 'w_ukv', 'w_br_mla', 'w_br_fox', 'w_out', 'post_norm_g']
TWIN_WEIGHTS = ['meta_tokens', 'pre_norm_g', 'w_in', 'fox_forget_b', 'mla_q_norm_g', 'mla_kv_norm_g', 'w_uq', 'w_ukv', 'w_br_mla', 'w_br_fox', 'w_out', 'post_norm_g']
TWIN_DIFF_INPUT = 'x'
TWIN_INPUTS = ['x', 'meta_tokens', 'pre_norm_g', 'w_in', 'fox_forget_b', 'mla_q_norm_g', 'mla_kv_norm_g', 'w_uq', 'w_ukv', 'w_br_mla', 'w_br_fox', 'w_out', 'post_norm_g', 'loss_target', 'm_meta_tokens', 'm_pre_norm_g', 'm_w_in', 'm_fox_forget_b', 'm_mla_q_norm_g', 'm_mla_kv_norm_g', 'm_w_uq', 'm_w_ukv', 'm_w_br_mla', 'm_w_br_fox', 'm_w_out', 'm_post_norm_g', 'v_meta_tokens', 'v_pre_norm_g', 'v_w_in', 'v_fox_forget_b', 'v_mla_q_norm_g', 'v_mla_kv_norm_g', 'v_w_uq', 'v_w_ukv', 'v_w_br_mla', 'v_w_br_fox', 'v_w_out', 'v_post_norm_g']
TWIN_OUTPUTS = ['loss', 'grad_x', 'grad_meta_tokens', 'grad_pre_norm_g', 'grad_w_in', 'grad_fox_forget_b', 'grad_mla_q_norm_g', 'grad_mla_kv_norm_g', 'grad_w_uq', 'grad_w_ukv', 'grad_w_br_mla', 'grad_w_br_fox', 'grad_w_out', 'grad_post_norm_g', 'delta_meta_tokens', 'delta_pre_norm_g', 'delta_w_in', 'delta_fox_forget_b', 'delta_mla_q_norm_g', 'delta_mla_kv_norm_g', 'delta_w_uq', 'delta_w_ukv', 'delta_w_br_mla', 'delta_w_br_fox', 'delta_w_out', 'delta_post_norm_g', 'new_m_meta_tokens', 'new_m_pre_norm_g', 'new_m_w_in', 'new_m_fox_forget_b', 'new_m_mla_q_norm_g', 'new_m_mla_kv_norm_g', 'new_m_w_uq', 'new_m_w_ukv', 'new_m_w_br_mla', 'new_m_w_br_fox', 'new_m_w_out', 'new_m_post_norm_g', 'new_v_meta_tokens', 'new_v_pre_norm_g', 'new_v_w_in', 'new_v_fox_forget_b', 'new_v_mla_q_norm_g', 'new_v_mla_kv_norm_g', 'new_v_w_uq', 'new_v_w_ukv', 'new_v_w_br_mla', 'new_v_w_br_fox', 'new_v_w_out', 'new_v_post_norm_g']
TWIN_LEAF_KINDS = {'loss': 'loss', 'grad_x': 'grad_x', 'grad_meta_tokens': 'grad_w', 'grad_pre_norm_g': 'grad_w', 'grad_w_in': 'grad_w', 'grad_fox_forget_b': 'grad_w', 'grad_mla_q_norm_g': 'grad_w', 'grad_mla_kv_norm_g': 'grad_w', 'grad_w_uq': 'grad_w', 'grad_w_ukv': 'grad_w', 'grad_w_br_mla': 'grad_w', 'grad_w_br_fox': 'grad_w', 'grad_w_out': 'grad_w', 'grad_post_norm_g': 'grad_w', 'delta_meta_tokens': 'delta_w', 'delta_pre_norm_g': 'delta_w', 'delta_w_in': 'delta_w', 'delta_fox_forget_b': 'delta_w', 'delta_mla_q_norm_g': 'delta_w', 'delta_mla_kv_norm_g': 'delta_w', 'delta_w_uq': 'delta_w', 'delta_w_ukv': 'delta_w', 'delta_w_br_mla': 'delta_w', 'delta_w_br_fox': 'delta_w', 'delta_w_out': 'delta_w', 'delta_post_norm_g': 'delta_w', 'new_m_meta_tokens': 'new_m', 'new_m_pre_norm_g': 'new_m', 'new_m_w_in': 'new_m', 'new_m_fox_forget_b': 'new_m', 'new_m_mla_q_norm_g': 'new_m', 'new_m_mla_kv_norm_g': 'new_m', 'new_m_w_uq': 'new_m', 'new_m_w_ukv': 'new_m', 'new_m_w_br_mla': 'new_m', 'new_m_w_br_fox': 'new_m', 'new_m_w_out': 'new_m', 'new_m_post_norm_g': 'new_m', 'new_v_meta_tokens': 'new_v', 'new_v_pre_norm_g': 'new_v', 'new_v_w_in': 'new_v', 'new_v_fox_forget_b': 'new_v', 'new_v_mla_q_norm_g': 'new_v', 'new_v_mla_kv_norm_g': 'new_v', 'new_v_w_uq': 'new_v', 'new_v_w_ukv': 'new_v', 'new_v_w_br_mla': 'new_v', 'new_v_w_br_fox': 'new_v', 'new_v_w_out': 'new_v', 'new_v_post_norm_g': 'new_v'}


def _forward(args):
    return _fwd_reference(*[args[k] for k in FWD_PARAMS])


def _output_shape():
    def fwd():
        inp = _fwd_setup_inputs(0)
        return _fwd_reference(*[inp[k] for k in FWD_PARAMS])
    out = _jax.eval_shape(fwd)
    return out.shape, out.dtype

N_MICROBATCH = 1
ADAM_LR = 0.001
ADAM_B1 = 0.9
ADAM_B2 = 0.999
ADAM_EPS = 1e-08
ADAM_WD = 0.01
ADAM_STEP = 10
PER_EXAMPLE_BATCH_AXIS = {'x': 0, 'loss_target': 0}
SHARED_INPUTS = []
_WEIGHT_DTYPES = {'meta_tokens': _jnp.float32, 'pre_norm_g': _jnp.float32, 'w_in': _jnp.float32, 'fox_forget_b': _jnp.float32, 'mla_q_norm_g': _jnp.float32, 'mla_kv_norm_g': _jnp.float32, 'w_uq': _jnp.float32, 'w_ukv': _jnp.float32, 'w_br_mla': _jnp.float32, 'w_br_fox': _jnp.float32, 'w_out': _jnp.float32, 'post_norm_g': _jnp.float32}
MOMENT_SCALE = {'meta_tokens': 1.907531e-02, 'pre_norm_g': 4.834727e-01, 'w_in': 1.777002e-01, 'fox_forget_b': 1.235797e+00, 'mla_q_norm_g': 2.047243e-01, 'mla_kv_norm_g': 4.863478e-01, 'w_uq': 8.510131e-02, 'w_ukv': 1.013753e-01, 'w_br_mla': 1.105996e-01, 'w_br_fox': 2.952386e-01, 'w_out': 3.290625e-01, 'post_norm_g': 3.232703e+01}


def _to_microbatches(a, axis):
    t = _jnp.moveaxis(a, axis, 0)
    t = t.reshape((N_MICROBATCH, t.shape[0] // N_MICROBATCH) + t.shape[1:])
    return _jnp.moveaxis(t, 1, axis + 1)


def setup_inputs(seed: int = 0) -> dict:
    inp = _fwd_setup_inputs(seed)
    key = _jax.random.fold_in(_jax.random.key(seed), 7919)
    shape, _ = _output_shape()
    out = dict(inp)
    out["loss_target"] = _jax.random.normal(_jax.random.fold_in(key, 0), shape, _jnp.float32)
    for i, name in enumerate(TWIN_WEIGHTS):
        w = inp[name].astype(_jnp.float32)
        if MOMENT_SCALE is None:
            s = _jnp.sqrt(_jnp.mean(_jnp.square(w)) + 1e-30)
        else:
            s = MOMENT_SCALE[name]
        km, kv = _jax.random.split(_jax.random.fold_in(key, i + 1))
        out[name] = w
        out["m_" + name] = s * _jax.random.normal(km, w.shape, _jnp.float32)
        out["v_" + name] = (s * s) * _jax.random.uniform(kv, w.shape, _jnp.float32, 0.5, 1.5)
    if N_MICROBATCH > 1:
        for name, axis in PER_EXAMPLE_BATCH_AXIS.items():
            out[name] = _to_microbatches(out[name], axis)
    return {'x': out['x'], 'meta_tokens': out['meta_tokens'], 'pre_norm_g': out['pre_norm_g'], 'w_in': out['w_in'], 'fox_forget_b': out['fox_forget_b'], 'mla_q_norm_g': out['mla_q_norm_g'], 'mla_kv_norm_g': out['mla_kv_norm_g'], 'w_uq': out['w_uq'], 'w_ukv': out['w_ukv'], 'w_br_mla': out['w_br_mla'], 'w_br_fox': out['w_br_fox'], 'w_out': out['w_out'], 'post_norm_g': out['post_norm_g'], 'loss_target': out['loss_target'], 'm_meta_tokens': out['m_meta_tokens'], 'm_pre_norm_g': out['m_pre_norm_g'], 'm_w_in': out['m_w_in'], 'm_fox_forget_b': out['m_fox_forget_b'], 'm_mla_q_norm_g': out['m_mla_q_norm_g'], 'm_mla_kv_norm_g': out['m_mla_kv_norm_g'], 'm_w_uq': out['m_w_uq'], 'm_w_ukv': out['m_w_ukv'], 'm_w_br_mla': out['m_w_br_mla'], 'm_w_br_fox': out['m_w_br_fox'], 'm_w_out': out['m_w_out'], 'm_post_norm_g': out['m_post_norm_g'], 'v_meta_tokens': out['v_meta_tokens'], 'v_pre_norm_g': out['v_pre_norm_g'], 'v_w_in': out['v_w_in'], 'v_fox_forget_b': out['v_fox_forget_b'], 'v_mla_q_norm_g': out['v_mla_q_norm_g'], 'v_mla_kv_norm_g': out['v_mla_kv_norm_g'], 'v_w_uq': out['v_w_uq'], 'v_w_ukv': out['v_w_ukv'], 'v_w_br_mla': out['v_w_br_mla'], 'v_w_br_fox': out['v_w_br_fox'], 'v_w_out': out['v_w_out'], 'v_post_norm_g': out['v_post_norm_g']}


def _loss(weights, diff, rest, loss_target):
    with _jax.named_scope("forward"):
        args = {**rest, TWIN_DIFF_INPUT: diff, **{k: w.astype(_WEIGHT_DTYPES[k]) for k, w in weights.items()}}
        y = _forward(args)
    with _jax.named_scope("loss_head"):
        err = _jnp.square(y.astype(_jnp.float32) - loss_target)
        return 0.5 * _jnp.sum(_jnp.mean(err, axis=-1)) if err.ndim else 0.5 * err


def _adamw(w, g, m, v):
    m = ADAM_B1 * m + (1.0 - ADAM_B1) * g
    v = ADAM_B2 * v + (1.0 - ADAM_B2) * _jnp.square(g)
    m_hat = m / (1.0 - ADAM_B1 ** ADAM_STEP)
    v_hat = v / (1.0 - ADAM_B2 ** ADAM_STEP)
    delta = -ADAM_LR * (m_hat / (_jnp.sqrt(v_hat) + ADAM_EPS) + ADAM_WD * w)
    return delta, m, v


def reference(x, meta_tokens, pre_norm_g, w_in, fox_forget_b, mla_q_norm_g, mla_kv_norm_g, w_uq, w_ukv, w_br_mla, w_br_fox, w_out, post_norm_g, loss_target, m_meta_tokens, m_pre_norm_g, m_w_in, m_fox_forget_b, m_mla_q_norm_g, m_mla_kv_norm_g, m_w_uq, m_w_ukv, m_w_br_mla, m_w_br_fox, m_w_out, m_post_norm_g, v_meta_tokens, v_pre_norm_g, v_w_in, v_fox_forget_b, v_mla_q_norm_g, v_mla_kv_norm_g, v_w_uq, v_w_ukv, v_w_br_mla, v_w_br_fox, v_w_out, v_post_norm_g):
    given = dict(x=x, meta_tokens=meta_tokens, pre_norm_g=pre_norm_g, w_in=w_in, fox_forget_b=fox_forget_b, mla_q_norm_g=mla_q_norm_g, mla_kv_norm_g=mla_kv_norm_g, w_uq=w_uq, w_ukv=w_ukv, w_br_mla=w_br_mla, w_br_fox=w_br_fox, w_out=w_out, post_norm_g=post_norm_g, loss_target=loss_target, m_meta_tokens=m_meta_tokens, m_pre_norm_g=m_pre_norm_g, m_w_in=m_w_in, m_fox_forget_b=m_fox_forget_b, m_mla_q_norm_g=m_mla_q_norm_g, m_mla_kv_norm_g=m_mla_kv_norm_g, m_w_uq=m_w_uq, m_w_ukv=m_w_ukv, m_w_br_mla=m_w_br_mla, m_w_br_fox=m_w_br_fox, m_w_out=m_w_out, m_post_norm_g=m_post_norm_g, v_meta_tokens=v_meta_tokens, v_pre_norm_g=v_pre_norm_g, v_w_in=v_w_in, v_fox_forget_b=v_fox_forget_b, v_mla_q_norm_g=v_mla_q_norm_g, v_mla_kv_norm_g=v_mla_kv_norm_g, v_w_uq=v_w_uq, v_w_ukv=v_w_ukv, v_w_br_mla=v_w_br_mla, v_w_br_fox=v_w_br_fox, v_w_out=v_w_out, v_post_norm_g=v_post_norm_g)
    weights = {n: given[n] for n in TWIN_WEIGHTS}
    shared = {n: given[n] for n in SHARED_INPUTS}
    per_example = {n: given[n] for n in ['x']}
    grad_fn = _jax.value_and_grad(_loss, argnums=(0, 1))

    def one_microbatch(ex, loss_target):
        ex = dict(ex)
        diff = ex.pop(TWIN_DIFF_INPUT)
        return grad_fn(weights, diff, {**shared, **ex}, loss_target)

    if N_MICROBATCH == 1:
        loss, (grad_w, grad_x) = one_microbatch(per_example, given["loss_target"])
    else:
        def body(carry, xs):
            loss_sum, grad_sum = carry
            l_k, (gw_k, gx_k) = one_microbatch(xs[0], xs[1])
            with _jax.named_scope("update"):
                return (loss_sum + l_k, _jax.tree.map(_jnp.add, grad_sum, gw_k)), gx_k

        init = (_jnp.zeros((), _jnp.float32), _jax.tree.map(_jnp.zeros_like, weights))
        (loss, grad_w), grad_x = _jax.lax.scan(body, init, (per_example, given["loss_target"]))
    with _jax.named_scope("update"):
        delta_w, new_m, new_v = {}, {}, {}
        for n in TWIN_WEIGHTS:
            delta_w[n], new_m[n], new_v[n] = _adamw(weights[n], grad_w[n], given["m_" + n], given["v_" + n])
    return (loss, grad_x, *[grad_w[n] for n in TWIN_WEIGHTS], *[delta_w[n] for n in TWIN_WEIGHTS],
            *[new_m[n] for n in TWIN_WEIGHTS], *[new_v[n] for n in TWIN_WEIGHTS])
```

```python
import math

import jax
import jax.numpy as jnp
from jax import lax
from jax.experimental import pallas as pl
from jax.experimental.pallas import tpu as pltpu

F32 = jnp.float32
BF16 = jnp.bfloat16

D_MODEL = 1024
N_META = 16
RMS_EPS = 1e-6
HEADS = 16
PAIRS = HEADS // 2
HEAD_DIM = 64
LANES = 128
MLA_Q_RANK = 256
MLA_KV_RANK = 128
MLA_ROPE = 32
MLA_SCALE = 1.0 / math.sqrt(64 + 32)
FOX_SCALE = 1.0 / math.sqrt(64)
ROPE_THETA = 10000.0

PAD = 256
BLK = 256
NEG = -1e30

C_CQ, C_CKV, C_KPE, C_ZMLA, C_FQ, C_FK, C_FV, C_FL, C_ZFOX, C_GA, C_GB, C_END = (
    0, 256, 384, 416, 1440, 2464, 3488, 4512, 4528, 5552, 6576, 7600)
SMALL_W = 640

PK_ROWS = 2848
PK_HALF = PK_ROWS // 2
N_CHIPS = 4

ADAM_LR = 0.001
ADAM_B1 = 0.9
ADAM_B2 = 0.999
ADAM_EPS = 1e-08
ADAM_WD = 0.01
ADAM_STEP = 10

VMEM_BIG = 56 * 1024 * 1024
MESH = pl.DeviceIdType.MESH


def _cp(dims, vmem=None):
    return pltpu.CompilerParams(dimension_semantics=dims, vmem_limit_bytes=vmem)


def _dot(a, b, ca, cb):
    return lax.dot_general(a, b, (((ca,), (cb,)), ((), ())), preferred_element_type=F32)


def _sigmoid(x):
    return 1.0 / (1.0 + jnp.exp(-x))


def _tile(n, cands):
    for c in cands:
        if n % c == 0:
            return c
    return n


def _mm(a, b, *, mode, out_dtype, name, acc=None):
    if mode == "nn":
        (M, K), N = a.shape, b.shape[1]
    elif mode == "nt":
        (M, K), N = a.shape, b.shape[0]
    else:
        (K, M), N = a.shape, b.shape[1]
    tm = _tile(M, (1088, 1024)) if M > 1024 else M
    tn = _tile(N, (1024,)) if N > 1024 else N
    tk = _tile(K, (1088, 1024)) if K > 1088 else K
    nk = K // tk
    ca, cb = {"nn": (1, 0), "nt": (1, 1), "tn": (0, 0)}[mode]
    a_spec = (pl.BlockSpec((tk, tm), lambda j, i, k: (k, i)) if mode == "tn"
              else pl.BlockSpec((tm, tk), lambda j, i, k: (i, k)))
    b_spec = (pl.BlockSpec((tn, tk), lambda j, i, k: (j, k)) if mode == "nt"
              else pl.BlockSpec((tk, tn), lambda j, i, k: (k, j)))
    o_spec = pl.BlockSpec((tm, tn), lambda j, i, k: (i, j))
    has_acc = acc is not None

    def body(*refs):
        a_ref, b_ref = refs[0], refs[1]
        acc_ref = refs[2] if has_acc else None
        o_ref = refs[3] if has_acc else refs[2]
        part = _dot(a_ref[...].astype(BF16), b_ref[...].astype(BF16), ca, cb)
        if nk == 1:
            if has_acc:
                part = part + acc_ref[...]
            o_ref[...] = part.astype(out_dtype)
        else:
            sc = refs[-1]
            k = pl.program_id(2)

            @pl.when(k == 0)
            def _():
                sc[...] = part + acc_ref[...] if has_acc else part

            @pl.when(k > 0)
            def _():
                sc[...] += part

            @pl.when(k == nk - 1)
            def _():
                o_ref[...] = sc[...].astype(out_dtype)

    ins = [a, b] + ([acc] if has_acc else [])
    in_specs = [a_spec, b_spec] + ([o_spec] if has_acc else [])
    return pl.pallas_call(
        body, name=name, grid=(N // tn, M // tm, nk), in_specs=in_specs, out_specs=o_spec,
        out_shape=jax.ShapeDtypeStruct((M, N), out_dtype),
        scratch_shapes=[pltpu.VMEM((tm, tn), F32)] if nk > 1 else [],
        compiler_params=_cp(("parallel", "parallel", "arbitrary"), VMEM_BIG))(*ins)


def _row(w):
    return pl.BlockSpec((BLK, w), lambda i: (i, 0))


def _rowc(w, c):
    return pl.BlockSpec((BLK, w), lambda i: (i, c))


def _full(shape):
    return pl.BlockSpec(shape, lambda i: tuple(0 for _ in shape))


def _rope(x, c, s):
    lane = lax.broadcasted_iota(jnp.int32, x.shape, 1)
    is_x1 = ((lane >> 4) & 1) == 0
    partner = jnp.where(is_x1, pltpu.roll(x, LANES - 16, 1), pltpu.roll(x, 16, 1))
    return x * c + partner * s


def _row_valid(i):
    rows = i * BLK + lax.broadcasted_iota(jnp.int32, (BLK, 1), 0)
    return (rows < N_META) | (rows >= PAD)


def _rms_pre(h, g):
    lp = h.shape[0]

    def body(h_ref, g_ref, u_ref):
        hv = h_ref[...]
        r = lax.rsqrt(jnp.mean(hv * hv, axis=-1, keepdims=True) + RMS_EPS)
        u_ref[...] = (hv * r * g_ref[...]).astype(BF16)

    return pl.pallas_call(
        body, name="rms_pre", grid=(lp // BLK,),
        in_specs=[_row(D_MODEL), _full((1, D_MODEL))], out_specs=_row(D_MODEL),
        out_shape=jax.ShapeDtypeStruct((lp, D_MODEL), BF16),
        compiler_params=_cp(("parallel",)))(h, g)


def _split3(x):
    hi = x.astype(BF16)
    r1 = x - hi.astype(F32)
    mid = r1.astype(BF16)
    lo = (r1 - mid.astype(F32)).astype(BF16)
    return hi, mid, lo


def _small_prep(small, gq, gkv, fb, ctab, stab, tri):
    lp = small.shape[0]

    def body(sm_ref, gq_ref, gkv_ref, fb_ref, c_ref, s_ref, tri_ref, qn_ref, kvn_ref, kr_ref, ncum_ref, carry):
        i = pl.program_id(0)

        @pl.when(i == 0)
        def _():
            carry[...] = jnp.zeros_like(carry)

        cq = sm_ref[:, 0:256]
        r = lax.rsqrt(jnp.mean(cq * cq, axis=-1, keepdims=True) + RMS_EPS)
        qn_ref[...] = (cq * r * gq_ref[...]).astype(BF16)
        ckv = sm_ref[:, 256:384]
        r = lax.rsqrt(jnp.mean(ckv * ckv, axis=-1, keepdims=True) + RMS_EPS)
        kvn_ref[...] = (ckv * r * gkv_ref[...]).astype(BF16)
        kr_ref[...] = _rope(sm_ref[:, 384:512], c_ref[...], s_ref[...]).astype(BF16)
        fl = sm_ref[:, 512:640] + fb_ref[...]
        lf = jnp.minimum(fl, 0.0) - jnp.log(1.0 + jnp.exp(-jnp.abs(fl)))
        lf = jnp.where(_row_valid(i), lf, 0.0)
        hi, mid, lo = _split3(lf)
        t = tri_ref[...]
        cum = (_dot(t, hi, 1, 0) + _dot(t, mid, 1, 0)) + _dot(t, lo, 1, 0) + carry[...]
        ncum_ref[...] = -cum
        carry[...] = -ncum_ref[BLK - 1:BLK, :]

    return pl.pallas_call(
        body, name="small_prep", grid=(lp // BLK,),
        in_specs=[_row(SMALL_W), _full((1, 256)), _full((1, 128)), _full((1, 128)), _row(128), _row(128),
                  _full((BLK, BLK))],
        out_specs=[_row(256), _row(128), _row(128), _row(128)],
        out_shape=[jax.ShapeDtypeStruct((lp, 256), BF16), jax.ShapeDtypeStruct((lp, 128), BF16),
                   jax.ShapeDtypeStruct((lp, 128), BF16), jax.ShapeDtypeStruct((lp, 128), F32)],
        scratch_shapes=[pltpu.VMEM((1, 128), F32)],
        compiler_params=_cp(("arbitrary",)))(small, gq, gkv, fb, ctab, stab, tri)


def _rope_q(qraw, ctab, stab, *, inverse, out_dtype, name):
    lp = qraw.shape[0]

    def body(q_ref, c_ref, s_ref, o_ref):
        c = c_ref[...]
        s = -s_ref[...] if inverse else s_ref[...]
        for p in range(PAIRS):
            lo = p * 256
            o_ref[:, lo:lo + 128] = q_ref[:, lo:lo + 128].astype(out_dtype)
            o_ref[:, lo + 128:lo + 256] = _rope(q_ref[:, lo + 128:lo + 256].astype(F32), c, s).astype(out_dtype)

    return pl.pallas_call(
        body, name=name, grid=(lp // BLK,),
        in_specs=[_row(PAIRS * 256), _row(128), _row(128)], out_specs=_row(PAIRS * 256),
        out_shape=jax.ShapeDtypeStruct((lp, PAIRS * 256), out_dtype),
        compiler_params=_cp(("parallel",)))(qraw, ctab, stab)


def _gate_fwd(o_mla, o_fox, gate):
    lp = o_mla.shape[0]

    def body(om_ref, of_ref, zm_ref, zf_ref, am_ref, af_ref):
        zm = zm_ref[...].astype(F32)
        am_ref[...] = (om_ref[...] * (zm * _sigmoid(zm))).astype(BF16)
        zf = zf_ref[...].astype(F32)
        af_ref[...] = (of_ref[...] * (zf * _sigmoid(zf))).astype(BF16)

    return pl.pallas_call(
        body, name="gate_fwd", grid=(lp // BLK,),
        in_specs=[_row(D_MODEL), _row(D_MODEL), _rowc(D_MODEL, 0), _rowc(D_MODEL, 1)],
        out_specs=[_row(D_MODEL), _row(D_MODEL)],
        out_shape=[jax.ShapeDtypeStruct((lp, D_MODEL), BF16)] * 2,
        compiler_params=_cp(("parallel",)))(o_mla, o_fox, gate, gate)


def _merge_fwd(gate, y_mla, y_fox):
    lp = y_mla.shape[0]

    def body(ga_ref, gb_ref, ym_ref, yf_ref, m_ref):
        sa = _sigmoid(ga_ref[...].astype(F32))
        sb = _sigmoid(gb_ref[...].astype(F32))
        m_ref[...] = (sa * ym_ref[...] + sb * yf_ref[...]).astype(BF16)

    return pl.pallas_call(
        body, name="merge_fwd", grid=(lp // BLK,),
        in_specs=[_rowc(D_MODEL, 2), _rowc(D_MODEL, 3), _row(D_MODEL), _row(D_MODEL)],
        out_specs=_row(D_MODEL), out_shape=jax.ShapeDtypeStruct((lp, D_MODEL), BF16),
        compiler_params=_cp(("parallel",)))(gate, gate, y_mla, y_fox)


def _tail(h, mixed, tgt, gpost):
    lp = h.shape[0]
    shift = pl.BlockSpec((BLK, D_MODEL), lambda i: (jnp.maximum(i - 1, 0), 0))

    def body(h_ref, mx_ref, t_ref, g_ref, dmx_ref, dy_ref, loss_ref, dg_ref):
        i = pl.program_id(0)

        @pl.when(i == 0)
        def _():
            loss_ref[...] = jnp.zeros_like(loss_ref)
            dg_ref[...] = jnp.zeros_like(dg_ref)
            dmx_ref[...] = jnp.zeros_like(dmx_ref)
            dy_ref[...] = jnp.zeros_like(dy_ref)

        @pl.when(i > 0)
        def _():
            mx = mx_ref[...]
            g = g_ref[...]
            r = lax.rsqrt(jnp.mean(mx * mx, axis=-1, keepdims=True) + RMS_EPS)
            nrm = mx * r
            e = (h_ref[...] + nrm * g) - t_ref[...]
            loss_ref[...] += jnp.sum(0.5 * jnp.sum(e * e, axis=-1, keepdims=True) * (1.0 / D_MODEL),
                                     axis=0, keepdims=True)
            dy = e * (1.0 / D_MODEL)
            dy_ref[...] = dy
            dg_ref[...] += jnp.sum(dy * nrm, axis=0, keepdims=True)
            w = dy * g
            dot = jnp.mean(w * mx, axis=-1, keepdims=True)
            dmx_ref[...] = (r * w - mx * (r * r * r * dot)).astype(BF16)

    return pl.pallas_call(
        body, name="tail", grid=(lp // BLK,),
        in_specs=[_row(D_MODEL), _row(D_MODEL), shift, _full((1, D_MODEL))],
        out_specs=[_row(D_MODEL), _row(D_MODEL), _full((1, 1)), _full((1, D_MODEL))],
        out_shape=[jax.ShapeDtypeStruct((lp, D_MODEL), BF16), jax.ShapeDtypeStruct((lp, D_MODEL), F32),
                   jax.ShapeDtypeStruct((1, 1), F32), jax.ShapeDtypeStruct((1, D_MODEL), F32)],
        compiler_params=_cp(("arbitrary",)))(h, mixed, tgt, gpost)


def _merge_bwd(dm, gate, y_mla, y_fox):
    lp = dm.shape[0]

    def body(dm_ref, ga_ref, gb_ref, ym_ref, yf_ref, dym_ref, dyf_ref, dg_ref):
        dm_v = dm_ref[...]
        sa = _sigmoid(ga_ref[...].astype(F32))
        sb = _sigmoid(gb_ref[...].astype(F32))
        dym_ref[...] = (dm_v * sa).astype(BF16)
        dyf_ref[...] = (dm_v * sb).astype(BF16)
        dg_ref[:, 0:D_MODEL] = (dm_v * ym_ref[...] * (sa * (1.0 - sa))).astype(BF16)
        dg_ref[:, D_MODEL:2 * D_MODEL] = (dm_v * yf_ref[...] * (sb * (1.0 - sb))).astype(BF16)

    return pl.pallas_call(
        body, name="merge_bwd", grid=(lp // BLK,),
        in_specs=[_row(D_MODEL), _rowc(D_MODEL, 2), _rowc(D_MODEL, 3), _row(D_MODEL), _row(D_MODEL)],
        out_specs=[_row(D_MODEL), _row(D_MODEL), _row(2 * D_MODEL)],
        out_shape=[jax.ShapeDtypeStruct((lp, D_MODEL), BF16), jax.ShapeDtypeStruct((lp, D_MODEL), BF16),
                   jax.ShapeDtypeStruct((lp, 2 * D_MODEL), BF16)],
        compiler_params=_cp(("parallel",)))(dm, gate, gate, y_mla, y_fox)


def _gate_bwd(da_mla, da_fox, o_mla, o_fox, gate):
    lp = da_mla.shape[0]

    def one(da, o, z):
        sg = _sigmoid(z)
        do = da * (z * sg)
        dz = da * o * (sg * (1.0 + z * (1.0 - sg)))
        return do.astype(BF16), dz.astype(BF16)

    def body(dam_ref, daf_ref, om_ref, of_ref, zm_ref, zf_ref, dom_ref, dof_ref, dz_ref):
        dom_ref[...], dz_ref[:, 0:D_MODEL] = one(dam_ref[...], om_ref[...], zm_ref[...].astype(F32))
        dof_ref[...], dz_ref[:, D_MODEL:2 * D_MODEL] = one(daf_ref[...], of_ref[...], zf_ref[...].astype(F32))

    return pl.pallas_call(
        body, name="gate_bwd", grid=(lp // BLK,),
        in_specs=[_row(D_MODEL)] * 4 + [_rowc(D_MODEL, 0), _rowc(D_MODEL, 1)],
        out_specs=[_row(D_MODEL), _row(D_MODEL), _row(2 * D_MODEL)],
        out_shape=[jax.ShapeDtypeStruct((lp, D_MODEL), BF16), jax.ShapeDtypeStruct((lp, D_MODEL), BF16),
                   jax.ShapeDtypeStruct((lp, 2 * D_MODEL), BF16)],
        compiler_params=_cp(("parallel",)))(da_mla, da_fox, o_mla, o_fox, gate, gate)


def _small_bwd(small, dqn, dkvn, dkr, dnb_t, drow_t, gq, gkv, fb, ctab, stab, triu):
    lp = small.shape[0]
    nb = lp // BLK

    def rrow(w):
        return pl.BlockSpec((BLK, w), lambda i: (nb - 1 - i, 0))

    def body(sm_ref, dqn_ref, dkvn_ref, dkr_ref, dnb_ref, drow_ref, gq_ref, gkv_ref, fb_ref, c_ref, s_ref, tri_ref,
             ds_ref, dgq_ref, dgkv_ref, dfb_ref, carry):
        i = pl.program_id(0)

        @pl.when(i == 0)
        def _():
            carry[...] = jnp.zeros_like(carry)
            dgq_ref[...] = jnp.zeros_like(dgq_ref)
            dgkv_ref[...] = jnp.zeros_like(dgkv_ref)
            dfb_ref[...] = jnp.zeros_like(dfb_ref)

        def norm_bwd(x, dn, g, dg_ref):
            r = lax.rsqrt(jnp.mean(x * x, axis=-1, keepdims=True) + RMS_EPS)
            dg_ref[...] += jnp.sum(dn * (x * r), axis=0, keepdims=True)
            w = dn * g
            dot = jnp.mean(w * x, axis=-1, keepdims=True)
            return r * w - x * (r * r * r * dot)

        ds_ref[:, 0:256] = norm_bwd(sm_ref[:, 0:256], dqn_ref[...], gq_ref[...], dgq_ref).astype(BF16)
        ds_ref[:, 256:384] = norm_bwd(sm_ref[:, 256:384], dkvn_ref[...], gkv_ref[...], dgkv_ref).astype(BF16)

        dk = dkr_ref[0]
        for p in range(1, PAIRS):
            dk = dk + dkr_ref[p]
        dk = _rope(dk, c_ref[...], -s_ref[...])
        lane = lax.broadcasted_iota(jnp.int32, dk.shape, 1)
        dk = jnp.where(lane < MLA_ROPE, dk + pltpu.roll(dk, LANES - MLA_ROPE, 1), 0.0)
        ds_ref[:, 384:512] = dk.astype(BF16)

        dcr = dnb_ref[...] - drow_ref[...]
        hi, mid, lo = _split3(dcr)
        t = tri_ref[...]
        suf = (_dot(t, hi, 1, 0) + _dot(t, mid, 1, 0)) + _dot(t, lo, 1, 0) + carry[...]
        fl = sm_ref[:, 512:640] + fb_ref[...]
        dfl = jnp.where(_row_valid(nb - 1 - i), -suf * _sigmoid(-fl), 0.0)
        ds_ref[:, 512:640] = dfl.astype(BF16)
        dfb_ref[...] += jnp.sum(dfl, axis=0, keepdims=True)
        carry[...] += jnp.sum(dcr, axis=0, keepdims=True)

    return pl.pallas_call(
        body, name="small_bwd", grid=(nb,),
        in_specs=[rrow(SMALL_W), rrow(256), rrow(128),
                  pl.BlockSpec((PAIRS, BLK, 128), lambda i: (0, nb - 1 - i, 0)), rrow(128), rrow(128),
                  _full((1, 256)), _full((1, 128)), _full((1, 128)), rrow(128), rrow(128), _full((BLK, BLK))],
        out_specs=[rrow(SMALL_W), _full((1, 256)), _full((1, 128)), _full((1, 128))],
        out_shape=[jax.ShapeDtypeStruct((lp, SMALL_W), BF16), jax.ShapeDtypeStruct((1, 256), F32),
                   jax.ShapeDtypeStruct((1, 128), F32), jax.ShapeDtypeStruct((1, 128), F32)],
        scratch_shapes=[pltpu.VMEM((1, 128), F32)],
        compiler_params=_cp(("arbitrary",)))(small, dqn, dkvn, dkr, dnb_t, drow_t, gq, gkv, fb, ctab, stab, triu)


def _pre_bwd(du, h, dy, gpre, s_rows):
    lp = h.shape[0]
    shift = pl.BlockSpec((BLK, D_MODEL), lambda i: (jnp.maximum(i - 1, 0), 0))

    def body(du_ref, h_ref, dy_ref, g_ref, dx_ref, dmeta_ref, dg_ref):
        i = pl.program_id(0)

        @pl.when(i == 0)
        def _():
            dg_ref[...] = jnp.zeros_like(dg_ref)

        hv = h_ref[...]
        duv = du_ref[...]
        r = lax.rsqrt(jnp.mean(hv * hv, axis=-1, keepdims=True) + RMS_EPS)
        dg_ref[...] += jnp.sum(duv * (hv * r), axis=0, keepdims=True)
        w = duv * g_ref[...]
        dot = jnp.mean(w * hv, axis=-1, keepdims=True)
        dh = dy_ref[...] + (r * w - hv * (r * r * r * dot))
        dx_ref[...] = dh

        @pl.when(i == 0)
        def _():
            dmeta_ref[...] = dh[0:N_META, :]

    return pl.pallas_call(
        body, name="pre_bwd", grid=(lp // BLK,),
        in_specs=[_row(D_MODEL), _row(D_MODEL), _row(D_MODEL), _full((1, D_MODEL))],
        out_specs=[shift, _full((N_META, D_MODEL)), _full((1, D_MODEL))],
        out_shape=[jax.ShapeDtypeStruct((s_rows, D_MODEL), F32), jax.ShapeDtypeStruct((N_META, D_MODEL), F32),
                   jax.ShapeDtypeStruct((1, D_MODEL), F32)],
        compiler_params=_cp(("arbitrary",)))(du, h, dy, gpre)


def _head_masks(h):
    lane = lax.broadcasted_iota(jnp.int32, (1, LANES), 1)
    if h == 0:
        return lane < HEAD_DIM, lane < MLA_ROPE
    return lane >= HEAD_DIM, (lane >= MLA_ROPE) & (lane < 2 * MLA_ROPE)


def _valid(q0, k0):
    row = q0 + lax.broadcasted_iota(jnp.int32, (BLK, BLK), 0)
    col = k0 + lax.broadcasted_iota(jnp.int32, (BLK, BLK), 1)
    return (col <= row) & ((col < N_META) | (col >= PAD))


def _flash_fwd(q, k, v, *, kr=None, nbias=None, scale, qcol, kcol, vcol, name):
    lp = q.shape[0]
    nb = lp // BLK
    rope = kr is not None
    bias = nbias is not None
    qw = 256 if rope else 128

    def body(*refs):
        it = iter(refs)
        q_ref, k_ref, v_ref = next(it), next(it), next(it)
        kr_ref = next(it) if rope else None
        nb_ref = next(it) if bias else None
        o_ref, lse_ref = next(it), next(it)
        i = pl.program_id(1)
        q0 = i * BLK
        qa = q_ref[:, 0:128]
        qb = q_ref[:, 128:256] if rope else None
        outs = []
        for h in range(2):
            ma, mb = _head_masks(h)
            qah = jnp.where(ma, qa, jnp.zeros_like(qa))
            qbh = jnp.where(mb, qb, jnp.zeros_like(qb)) if rope else None

            def step(kc, carry, qah=qah, qbh=qbh, h=h):
                m_prev, l_prev, acc = carry
                k0 = pl.multiple_of(kc * BLK, BLK)
                s = _dot(qah, k_ref[pl.ds(k0, BLK), :], 1, 1)
                if rope:
                    s = s + _dot(qbh, kr_ref[pl.ds(k0, BLK), :], 1, 1)
                s = s * scale
                if bias:
                    s = s + nb_ref[0, h:h + 1, pl.ds(k0, BLK)]
                s = jnp.where(_valid(q0, k0), s, NEG)
                m_new = jnp.maximum(m_prev, jnp.max(s, axis=1, keepdims=True))
                alpha = jnp.exp(m_prev - m_new)
                p = jnp.exp(s - m_new)
                l_new = alpha * l_prev + jnp.sum(p, axis=1, keepdims=True)
                acc = alpha * acc + _dot(p.astype(BF16), v_ref[pl.ds(k0, BLK), :], 1, 0)
                return m_new, l_new, acc

            init = (jnp.full((BLK, 1), NEG, F32), jnp.zeros((BLK, 1), F32), jnp.zeros((BLK, LANES), F32))
            m_f, l_f, acc = lax.fori_loop(0, i + 1, step, init)
            outs.append(acc / l_f)
            lse_ref[h] = jnp.broadcast_to(m_f + jnp.log(l_f), (BLK, LANES))
        ma, _ = _head_masks(0)
        o_ref[...] = jnp.where(ma, outs[0], outs[1])

    in_specs = [pl.BlockSpec((BLK, qw), lambda p, i: (i, qcol + p)),
                pl.BlockSpec((lp, 128), lambda p, i: (0, kcol(p))),
                pl.BlockSpec((lp, 128), lambda p, i: (0, vcol(p)))]
    ins = [q, k, v]
    if rope:
        in_specs.append(pl.BlockSpec((lp, 128), lambda p, i: (0, 0)))
        ins.append(kr)
    if bias:
        in_specs.append(pl.BlockSpec((1, 2, lp), lambda p, i: (p, 0, 0)))
        ins.append(nbias)
    return pl.pallas_call(
        body, name=name, grid=(PAIRS, nb), in_specs=in_specs,
        out_specs=[pl.BlockSpec((BLK, 128), lambda p, i: (i, p)),
                   pl.BlockSpec((2, BLK, 128), lambda p, i: (p, i, 0))],
        out_shape=[jax.ShapeDtypeStruct((lp, D_MODEL), F32), jax.ShapeDtypeStruct((HEADS, lp, 128), F32)],
        compiler_params=_cp(("parallel", "arbitrary"), VMEM_BIG))(*ins)


def _flash_bwd(q, k, v, do, o, lse, *, kr=None, nbias=None, scale, qcol, kcol, vcol, name):
    lp = q.shape[0]
    nb = lp // BLK
    rope = kr is not None
    bias = nbias is not None
    qw = 256 if rope else 128

    def body(*refs):
        it = iter(refs)
        q_ref, k_ref, v_ref = next(it), next(it), next(it)
        kr_ref = next(it) if rope else None
        nb_ref = next(it) if bias else None
        do_ref, o_ref, lse_ref = next(it), next(it), next(it)
        dq_ref, dk_ref, dv_ref = next(it), next(it), next(it)
        x_ref = next(it)
        drow_ref = next(it) if bias else None
        delta = next(it)
        kb = pl.program_id(1)
        k0 = kb * BLK
        lane = lax.broadcasted_iota(jnp.int32, (1, LANES), 1)

        @pl.when(kb == 0)
        def _():
            dq_ref[...] = jnp.zeros_like(dq_ref)
            if bias:
                drow_ref[...] = jnp.zeros_like(drow_ref)

            def dstep(c, carry):
                r0 = pl.multiple_of(c * BLK, BLK)
                prod = do_ref[pl.ds(r0, BLK), :].astype(F32) * o_ref[pl.ds(r0, BLK), :]
                d0 = jnp.sum(jnp.where(lane < HEAD_DIM, prod, 0.0), axis=1, keepdims=True)
                d1 = jnp.sum(jnp.where(lane >= HEAD_DIM, prod, 0.0), axis=1, keepdims=True)
                delta[0, pl.ds(r0, BLK), :] = jnp.broadcast_to(d0, (BLK, LANES))
                delta[1, pl.ds(r0, BLK), :] = jnp.broadcast_to(d1, (BLK, LANES))
                return carry

            lax.fori_loop(0, nb, dstep, 0)

        ka = k_ref[...]
        vv = v_ref[...]
        kbv = kr_ref[...] if rope else None
        res = []
        for h in range(2):
            ma, mb = _head_masks(h)
            kah = jnp.where(ma, ka, jnp.zeros_like(ka))
            vh = jnp.where(ma, vv, jnp.zeros_like(vv))
            kbh = jnp.where(mb, kbv, jnp.zeros_like(kbv)) if rope else None
            nbh = nb_ref[0, h:h + 1, :] if bias else None

            def step(qc, carry, kah=kah, vh=vh, kbh=kbh, nbh=nbh, h=h):
                q0 = pl.multiple_of(qc * BLK, BLK)
                qa = q_ref[pl.ds(q0, BLK), 0:128]
                s = _dot(qa, kah, 1, 1)
                if rope:
                    qb = q_ref[pl.ds(q0, BLK), 128:256]
                    s = s + _dot(qb, kbh, 1, 1)
                s = s * scale
                if bias:
                    s = s + nbh
                lse_t = lse_ref[h, pl.ds(q0, BLK), :]
                dl_t = delta[h, pl.ds(q0, BLK), :]
                p = jnp.where(_valid(q0, k0), jnp.exp(s - jnp.concatenate([lse_t, lse_t], axis=1)), 0.0)
                dov = do_ref[pl.ds(q0, BLK), :]
                dp = _dot(dov, vh, 1, 1)
                ds = p * (dp - jnp.concatenate([dl_t, dl_t], axis=1))
                dsc = (ds * scale).astype(BF16)
                out = [carry[0] + _dot(dsc, qa, 0, 0), carry[1] + _dot(p.astype(BF16), dov, 0, 0)]
                dq_ref[pl.ds(q0, BLK), 0:128] += _dot(dsc, kah, 1, 0)
                if rope:
                    out.append(carry[2] + _dot(dsc, qb, 0, 0))
                    dq_ref[pl.ds(q0, BLK), 128:256] += _dot(dsc, kbh, 1, 0)
                if bias:
                    out.append(carry[2] + jnp.sum(ds, axis=0, keepdims=True))
                    drow_ref[0, pl.ds(q0, BLK), :] += jnp.where(lane == h, jnp.sum(ds, axis=1, keepdims=True), 0.0)
                return tuple(out)

            init = [jnp.zeros((BLK, LANES), F32), jnp.zeros((BLK, LANES), F32)]
            if rope:
                init.append(jnp.zeros((BLK, LANES), F32))
            if bias:
                init.append(jnp.zeros((1, BLK), F32))
            res.append(lax.fori_loop(kb, nb, step, tuple(init)))
        ma, mb = _head_masks(0)
        dk_ref[...] = jnp.where(ma, res[0][0], res[1][0]).astype(BF16)
        dv_ref[...] = jnp.where(ma, res[0][1], res[1][1]).astype(BF16)
        if rope:
            _, mb1 = _head_masks(1)
            x_ref[0] = jnp.where(mb, res[0][2], jnp.where(mb1, res[1][2], 0.0))
        if bias:
            x_ref[0, 0:1, :] = res[0][2]
            x_ref[0, 1:2, :] = res[1][2]

    in_specs = [pl.BlockSpec((lp, qw), lambda p, j: (0, qcol + p)),
                pl.BlockSpec((BLK, 128), lambda p, j: (j, kcol(p))),
                pl.BlockSpec((BLK, 128), lambda p, j: (j, vcol(p)))]
    ins = [q, k, v]
    if rope:
        in_specs.append(pl.BlockSpec((BLK, 128), lambda p, j: (j, 0)))
        ins.append(kr)
    if bias:
        in_specs.append(pl.BlockSpec((1, 2, BLK), lambda p, j: (p, 0, j)))
        ins.append(nbias)
    in_specs += [pl.BlockSpec((lp, 128), lambda p, j: (0, p)), pl.BlockSpec((lp, 128), lambda p, j: (0, p)),
                 pl.BlockSpec((2, lp, 128), lambda p, j: (p, 0, 0))]
    ins += [do, o, lse]
    out_specs = [pl.BlockSpec((lp, qw), lambda p, j: (0, p)),
                 pl.BlockSpec((BLK, 128), lambda p, j: (j, p)),
                 pl.BlockSpec((BLK, 128), lambda p, j: (j, p))]
    out_shape = [jax.ShapeDtypeStruct((lp, PAIRS * qw), F32), jax.ShapeDtypeStruct((lp, D_MODEL), BF16),
                 jax.ShapeDtypeStruct((lp, D_MODEL), BF16)]
    if rope:
        out_specs.append(pl.BlockSpec((1, BLK, 128), lambda p, j: (p, j, 0)))
        out_shape.append(jax.ShapeDtypeStruct((PAIRS, lp, 128), F32))
    else:
        out_specs.append(pl.BlockSpec((1, 2, BLK), lambda p, j: (p, 0, j)))
        out_shape.append(jax.ShapeDtypeStruct((PAIRS, 2, lp), F32))
        out_specs.append(pl.BlockSpec((1, lp, 128), lambda p, j: (p, 0, 0)))
        out_shape.append(jax.ShapeDtypeStruct((PAIRS, lp, 128), F32))
    return pl.pallas_call(
        body, name=name, grid=(PAIRS, nb), in_specs=in_specs, out_specs=out_specs, out_shape=out_shape,
        scratch_shapes=[pltpu.VMEM((2, lp, 128), F32)],
        compiler_params=_cp(("parallel", "arbitrary"), VMEM_BIG))(*ins)


def _adamw(w, g, m, v, name):
    rows, cols = w.shape
    tr = 128 if rows * cols > 512 * 1024 else rows

    def body(w_ref, g_ref, m_ref, v_ref, d_ref, nm_ref, nv_ref):
        gv = g_ref[...]
        nm = ADAM_B1 * m_ref[...] + (1.0 - ADAM_B1) * gv
        nv = ADAM_B2 * v_ref[...] + (1.0 - ADAM_B2) * (gv * gv)
        m_hat = nm / (1.0 - ADAM_B1 ** ADAM_STEP)
        v_hat = nv / (1.0 - ADAM_B2 ** ADAM_STEP)
        d_ref[...] = -ADAM_LR * (m_hat / (jnp.sqrt(v_hat) + ADAM_EPS) + ADAM_WD * w_ref[...])
        nm_ref[...] = nm
        nv_ref[...] = nv

    spec = pl.BlockSpec((tr, cols), lambda i: (i, 0))
    return pl.pallas_call(
        body, name=name, grid=(rows // tr,), in_specs=[spec] * 4, out_specs=[spec] * 3,
        out_shape=[jax.ShapeDtypeStruct((rows, cols), F32)] * 3,
        compiler_params=_cp(("parallel",), VMEM_BIG))(w, g, m, v)


def _add_pair(a, b, name):
    rows, cols = a.shape
    tr = _tile(rows, (712, 356))
    spec = pl.BlockSpec((tr, cols), lambda i: (i, 0))

    def body(a_ref, b_ref, o_ref):
        o_ref[...] = a_ref[...] + b_ref[...]

    return pl.pallas_call(
        body, name=name, grid=(rows // tr,), in_specs=[spec, spec], out_specs=spec,
        out_shape=jax.ShapeDtypeStruct((rows, cols), F32), compiler_params=_cp(("parallel",), VMEM_BIG))(a, b)


def _add_four(x, name):
    _, rows, cols = x.shape
    tr = _tile(rows, (712, 356))

    def body(x_ref, o_ref):
        o_ref[...] = ((x_ref[0] + x_ref[1]) + x_ref[2]) + x_ref[3]

    return pl.pallas_call(
        body, name=name, grid=(rows // tr,),
        in_specs=[pl.BlockSpec((N_CHIPS, tr, cols), lambda i: (0, i, 0))],
        out_specs=pl.BlockSpec((tr, cols), lambda i: (i, 0)),
        out_shape=jax.ShapeDtypeStruct((rows, cols), F32), compiler_params=_cp(("parallel",), VMEM_BIG))(x)


def _axes():
    return lax.axis_index("x"), lax.axis_index("y"), lax.axis_index("c")


def _other_chips(x, y):
    return [(1 - x, y), (x, 1 - y), (1 - x, 1 - y)]


ANY = pl.BlockSpec(memory_space=pl.ANY)


def _gather_weights(pk, meta):
    def body(pk_ref, meta_ref, out_ref, mout_ref, send_sems, recv_sems, local_sems):
        x, y, c = _axes()
        me = 2 * x + y
        sib = (x, y, 1 - c)
        chips = _other_chips(x, y)

        def half(ref, chip_idx, cc):
            return ref.at[chip_idx, pl.ds(cc * PK_HALF, PK_HALF), :]

        def copy(k, src, dst, to):
            return pltpu.make_async_remote_copy(src_ref=src, dst_ref=dst, send_sem=send_sems.at[k],
                                                recv_sem=recv_sems.at[k], device_id=to, device_id_type=MESH)

        mine = pltpu.make_async_copy(pk_ref, out_ref.at[me], local_sems.at[0])
        mine_meta = pltpu.make_async_copy(meta_ref, mout_ref.at[me], local_sems.at[1])
        mine.start()
        mine_meta.start()
        first = []
        for j, (px, py) in enumerate(chips):
            first.append(copy(j, pk_ref.at[pl.ds(c * PK_HALF, PK_HALF), :], half(out_ref, me, c), (px, py, c)))
            first.append(copy(3 + j, meta_ref, mout_ref.at[me], (px, py, c)))
        for cp in first:
            cp.start()
        passed = []
        for j, (px, py) in enumerate(chips):
            src_chip = 2 * px + py
            copy(j, half(out_ref, src_chip, c), half(out_ref, src_chip, c), sib).wait_recv()
            fwd = copy(6 + j, half(out_ref, src_chip, c), half(out_ref, src_chip, c), sib)
            fwd.start()
            passed.append(fwd)
            copy(3 + j, mout_ref.at[src_chip], mout_ref.at[src_chip], sib).wait_recv()
        for j, (px, py) in enumerate(chips):
            src_chip = 2 * px + py
            copy(6 + j, half(out_ref, src_chip, 1 - c), half(out_ref, src_chip, 1 - c), sib).wait_recv()
        for cp in first + passed:
            cp.wait_send()
        mine.wait()
        mine_meta.wait()

    return pl.pallas_call(
        body, name="gather_weights", in_specs=[ANY, ANY], out_specs=[ANY, ANY],
        out_shape=[jax.ShapeDtypeStruct((N_CHIPS, PK_ROWS, D_MODEL), BF16),
                   jax.ShapeDtypeStruct((N_CHIPS, N_META, 256), F32)],
        scratch_shapes=[pltpu.SemaphoreType.DMA((9,)), pltpu.SemaphoreType.DMA((9,)),
                        pltpu.SemaphoreType.DMA((2,))])(pk, meta)


def _swap_halves(g):
    def body(g_ref, out_ref, send_sems, recv_sems):
        x, y, c = _axes()
        sib = (x, y, 1 - c)
        cps = []
        for j in range(N_CHIPS):
            cps.append(pltpu.make_async_remote_copy(
                src_ref=g_ref.at[j, pl.ds((1 - c) * PK_HALF, PK_HALF), :], dst_ref=out_ref.at[j],
                send_sem=send_sems.at[j], recv_sem=recv_sems.at[j], device_id=sib, device_id_type=MESH))
        for cp in cps:
            cp.start()
        for cp in cps:
            cp.wait()

    return pl.pallas_call(
        body, name="swap_halves", in_specs=[ANY], out_specs=ANY,
        out_shape=jax.ShapeDtypeStruct((N_CHIPS, PK_HALF, D_MODEL), F32),
        scratch_shapes=[pltpu.SemaphoreType.DMA((N_CHIPS,)), pltpu.SemaphoreType.DMA((N_CHIPS,))])(g)


def _scatter_chips(part):
    def body(p_ref, out_ref, send_sems, recv_sems, local_sem):
        x, y, c = _axes()
        me = 2 * x + y
        mine = pltpu.make_async_copy(p_ref.at[me], out_ref.at[me], local_sem)
        mine.start()
        cps = []
        for j, (px, py) in enumerate(_other_chips(x, y)):
            cps.append(pltpu.make_async_remote_copy(
                src_ref=p_ref.at[2 * px + py], dst_ref=out_ref.at[me], send_sem=send_sems.at[j],
                recv_sem=recv_sems.at[j], device_id=(px, py, c), device_id_type=MESH))
        for cp in cps:
            cp.start()
        for cp in cps:
            cp.wait()
        mine.wait()

    return pl.pallas_call(
        body, name="scatter_chips", in_specs=[ANY], out_specs=ANY,
        out_shape=jax.ShapeDtypeStruct((N_CHIPS, PK_HALF, D_MODEL), F32),
        scratch_shapes=[pltpu.SemaphoreType.DMA((3,)), pltpu.SemaphoreType.DMA((3,)),
                        pltpu.SemaphoreType.DMA])(part)


def _join_halves(r):
    def body(r_ref, out_ref, send_sem, recv_sem, local_sem):
        x, y, c = _axes()
        rows = out_ref.at[pl.ds(c * PK_HALF, PK_HALF), :]
        mine = pltpu.make_async_copy(r_ref, rows, local_sem)
        mine.start()
        cp = pltpu.make_async_remote_copy(src_ref=r_ref, dst_ref=rows, send_sem=send_sem, recv_sem=recv_sem,
                                          device_id=(x, y, 1 - c), device_id_type=MESH)
        cp.start()
        cp.wait_send()
        other = out_ref.at[pl.ds((1 - c) * PK_HALF, PK_HALF), :]
        pltpu.make_async_remote_copy(src_ref=r_ref, dst_ref=other, send_sem=send_sem, recv_sem=recv_sem,
                                     device_id=(x, y, 1 - c), device_id_type=MESH).wait_recv()
        mine.wait()

    return pl.pallas_call(
        body, name="join_halves", in_specs=[ANY], out_specs=ANY,
        out_shape=jax.ShapeDtypeStruct((PK_ROWS, D_MODEL), F32),
        scratch_shapes=[pltpu.SemaphoreType.DMA, pltpu.SemaphoreType.DMA, pltpu.SemaphoreType.DMA])(r)


SMALL_ROWS = 24


def _allreduce_small(vec):
    def body(v_ref, out_ref, slots, send_sems, recv_sems):
        x, y, c = _axes()
        me = 4 * x + 2 * y + c
        slots[me] = v_ref[...]
        cps = []
        for k in range(1, 8):
            kx, ky, kc = (k >> 2) & 1, (k >> 1) & 1, k & 1
            peer = (1 - x if kx else x, 1 - y if ky else y, 1 - c if kc else c)
            cps.append(pltpu.make_async_remote_copy(
                src_ref=v_ref, dst_ref=slots.at[me], send_sem=send_sems.at[k - 1], recv_sem=recv_sems.at[k - 1],
                device_id=peer, device_id_type=MESH))
        for cp in cps:
            cp.start()
        for cp in cps:
            cp.wait()
        tot = slots[0]
        for k in range(1, 8):
            tot = tot + slots[k]
        out_ref[...] = tot

    return pl.pallas_call(
        body, name="allreduce_small",
        in_specs=[pl.BlockSpec(memory_space=pltpu.VMEM)], out_specs=pl.BlockSpec(memory_space=pltpu.VMEM),
        out_shape=jax.ShapeDtypeStruct((SMALL_ROWS, 128), F32),
        scratch_shapes=[pltpu.VMEM((8, SMALL_ROWS, 128), F32), pltpu.SemaphoreType.DMA((7,)),
                        pltpu.SemaphoreType.DMA((7,))])(vec)


def _pack_shard(w_in, w_uq, w_ukv, w_br_mla, w_br_fox, w_out, meta, dtype):
    parts = [w_in.reshape(1900, D_MODEL), w_uq.reshape(96, D_MODEL), w_ukv.reshape(64, D_MODEL),
             w_br_mla.reshape(256, D_MODEL), w_br_fox.reshape(256, D_MODEL), w_out.reshape(256, D_MODEL),
             meta.reshape(4, D_MODEL), jnp.zeros((PK_ROWS - 2832, D_MODEL), meta.dtype)]
    return jnp.concatenate([p.astype(dtype) for p in parts], axis=0)


def _unpack_shard(pk):
    return (pk[0:1900].reshape(D_MODEL, 1900), pk[1900:1996].reshape(256, 384), pk[1996:2060].reshape(128, 512),
            pk[2060:2316], pk[2316:2572], pk[2572:2828], pk[2828:2832].reshape(N_META, 256))


def _uq_arrange(w):
    w3 = w.reshape(256, HEADS, 96)
    nope = w3[:, :, :64].reshape(256, PAIRS, 128)
    pe = w3[:, :, 64:].reshape(256, PAIRS, 64)
    return jnp.concatenate([nope, pe, jnp.zeros((256, PAIRS, 64), w.dtype)], axis=2).reshape(256, PAIRS * 256)


def _uq_restore(g):
    g3 = g.reshape(256, PAIRS, 256)
    nope = g3[:, :, :128].reshape(256, HEADS, 64)
    pe = g3[:, :, 128:192].reshape(256, HEADS, 32)
    return jnp.concatenate([nope, pe], axis=2).reshape(256, HEADS * 96)


def _ukv_arrange(w):
    w3 = w.reshape(128, HEADS, 128)
    return jnp.concatenate([w3[:, :, :64].reshape(128, 1024), w3[:, :, 64:].reshape(128, 1024)], axis=1)


def _ukv_restore(g):
    kn = g[:, :1024].reshape(128, HEADS, 64)
    vv = g[:, 1024:].reshape(128, HEADS, 64)
    return jnp.concatenate([kn, vv], axis=2).reshape(128, HEADS * 128)


def _rope_tables(lp):
    r = jnp.arange(lp)
    pos = jnp.where(r < N_META, r, jnp.where(r >= PAD, r - PAD + N_META, 0))
    half = MLA_ROPE // 2
    inv_freq = ROPE_THETA ** (-jnp.arange(half, dtype=F32) / half)
    ang = pos.astype(F32)[:, None] * inv_freq[None, :]
    cos, sin = jnp.cos(ang), jnp.sin(ang)
    one, zero = jnp.ones((lp, 64), F32), jnp.zeros((lp, 64), F32)
    return (jnp.concatenate([cos, cos, cos, cos, one], axis=1),
            jnp.concatenate([-sin, sin, -sin, sin, zero], axis=1))


def _pad_lanes(v, n=128):
    return jnp.pad(v, ((0, 0), (0, n - v.shape[1])))


def _local_step(x2, tgt2, meta_f, w_in_f, w_uq_f, w_ukv_f, w_bm, w_bf, w_o, pre_norm_g, post_norm_g, mla_q_norm_g,
                mla_kv_norm_g, fox_forget_b):
    s_rows = x2.shape[0]
    lp = PAD + s_rows

    kpe = w_in_f[:, C_KPE:C_ZMLA]
    w_small = jnp.concatenate([w_in_f[:, C_CQ:C_KPE], kpe, kpe, jnp.zeros((D_MODEL, 64), BF16),
                               w_in_f[:, C_FL:C_ZFOX], jnp.zeros((D_MODEL, 112), BF16)], axis=1)
    w_attn = w_in_f[:, C_FQ:C_FL]
    w_gate = jnp.concatenate([w_in_f[:, C_ZMLA:C_FQ], w_in_f[:, C_ZFOX:C_END]], axis=1)
    w_uq_a = _uq_arrange(w_uq_f)
    w_ukv_a = _ukv_arrange(w_ukv_f)

    ctab, stab = _rope_tables(lp)
    ii = jnp.arange(BLK)
    tri_lo = (ii[:, None] >= ii[None, :]).astype(BF16)
    tri_up = (ii[:, None] <= ii[None, :]).astype(BF16)
    fb128 = _pad_lanes(fox_forget_b)

    h = jnp.concatenate([meta_f, jnp.zeros((PAD - N_META, D_MODEL), F32), x2], axis=0)
    u = _rms_pre(h, pre_norm_g)
    small = _mm(u, w_small, mode="nn", out_dtype=F32, name="proj_small")
    attn = _mm(u, w_attn, mode="nn", out_dtype=BF16, name="proj_attn")
    gate = _mm(u, w_gate, mode="nn", out_dtype=BF16, name="proj_gate")
    qn, kvn, kr, ncum = _small_prep(small, mla_q_norm_g, mla_kv_norm_g, fb128, ctab, stab, tri_lo)
    qraw = _mm(qn, w_uq_a, mode="nn", out_dtype=F32, name="mla_q")
    qcat = _rope_q(qraw, ctab, stab, inverse=False, out_dtype=BF16, name="rope_q")
    kv = _mm(kvn, w_ukv_a, mode="nn", out_dtype=BF16, name="mla_kv")
    nbias = ncum[:, :HEADS].T.reshape(PAIRS, 2, lp)

    mla_cols = dict(qcol=0, kcol=lambda p: p, vcol=lambda p: PAIRS + p)
    fox_cols = dict(qcol=0, kcol=lambda p: PAIRS + p, vcol=lambda p: 2 * PAIRS + p)
    o_mla, lse_mla = _flash_fwd(qcat, kv, kv, kr=kr, scale=MLA_SCALE, name="mla_fwd", **mla_cols)
    o_fox, lse_fox = _flash_fwd(attn, attn, attn, nbias=nbias, scale=FOX_SCALE, name="fox_fwd", **fox_cols)

    a_mla, a_fox = _gate_fwd(o_mla, o_fox, gate)
    y_mla = _mm(a_mla, w_bm, mode="nn", out_dtype=F32, name="br_mla")
    y_fox = _mm(a_fox, w_bf, mode="nn", out_dtype=F32, name="br_fox")
    mg = _merge_fwd(gate, y_mla, y_fox)
    mixed = _mm(mg, w_o, mode="nn", out_dtype=F32, name="out_proj")
    dmixed, dy, loss_p, dg_post = _tail(h, mixed, tgt2, post_norm_g)

    d_w_out = _mm(mg, dmixed, mode="tn", out_dtype=F32, name="d_w_out")
    dm = _mm(dmixed, w_o, mode="nt", out_dtype=F32, name="d_merge")
    dy_mla, dy_fox, dgate_ab = _merge_bwd(dm, gate, y_mla, y_fox)
    d_w_bm = _mm(a_mla, dy_mla, mode="tn", out_dtype=F32, name="d_w_br_mla")
    d_w_bf = _mm(a_fox, dy_fox, mode="tn", out_dtype=F32, name="d_w_br_fox")
    da_mla = _mm(dy_mla, w_bm, mode="nt", out_dtype=F32, name="d_a_mla")
    da_fox = _mm(dy_fox, w_bf, mode="nt", out_dtype=F32, name="d_a_fox")
    do_mla, do_fox, dgate_z = _gate_bwd(da_mla, da_fox, o_mla, o_fox, gate)

    dqcat, dkn, dvm, dkr = _flash_bwd(qcat, kv, kv, do_mla, o_mla, lse_mla, kr=kr, scale=MLA_SCALE,
                                      name="mla_bwd", **mla_cols)
    dfq, dfk, dfv, dnb, drow = _flash_bwd(attn, attn, attn, do_fox, o_fox, lse_fox, nbias=nbias, scale=FOX_SCALE,
                                          name="fox_bwd", **fox_cols)

    dq_a = _rope_q(dqcat, ctab, stab, inverse=True, out_dtype=BF16, name="rope_q_bwd")
    d_w_uq_a = _mm(qn, dq_a, mode="tn", out_dtype=F32, name="d_w_uq")
    dqn = _mm(dq_a, w_uq_a, mode="nt", out_dtype=F32, name="d_qn")
    d_w_ukv_a = jnp.concatenate([_mm(kvn, dkn, mode="tn", out_dtype=F32, name="d_w_uk"),
                                 _mm(kvn, dvm, mode="tn", out_dtype=F32, name="d_w_uv")], axis=1)
    dkvn = _mm(dkn, w_ukv_a[:, :1024], mode="nt", out_dtype=F32, name="d_kvn_k")
    dkvn = _mm(dvm, w_ukv_a[:, 1024:], mode="nt", out_dtype=F32, name="d_kvn_v", acc=dkvn)
    dnb_t = _pad_lanes(dnb.reshape(HEADS, lp).T)
    drow_t = _pad_lanes(jnp.transpose(drow[:, :, 0:2], (1, 0, 2)).reshape(lp, HEADS))
    dsmall, dg_q, dg_kv, dfb = _small_bwd(small, dqn, dkvn, dkr, dnb_t, drow_t, mla_q_norm_g, mla_kv_norm_g,
                                          fb128, ctab, stab, tri_up)

    dw_small = _mm(u, dsmall, mode="tn", out_dtype=F32, name="d_w_small")
    dw_fq = _mm(u, dfq, mode="tn", out_dtype=F32, name="d_w_fq")
    dw_fk = _mm(u, dfk, mode="tn", out_dtype=F32, name="d_w_fk")
    dw_fv = _mm(u, dfv, mode="tn", out_dtype=F32, name="d_w_fv")
    dw_z = _mm(u, dgate_z, mode="tn", out_dtype=F32, name="d_w_z")
    dw_g = _mm(u, dgate_ab, mode="tn", out_dtype=F32, name="d_w_g")
    du = _mm(dsmall, w_small, mode="nt", out_dtype=F32, name="d_u_small")
    du = _mm(dfq, w_attn[:, 0:1024], mode="nt", out_dtype=F32, name="d_u_fq", acc=du)
    du = _mm(dfk, w_attn[:, 1024:2048], mode="nt", out_dtype=F32, name="d_u_fk", acc=du)
    du = _mm(dfv, w_attn[:, 2048:3072], mode="nt", out_dtype=F32, name="d_u_fv", acc=du)
    du = _mm(dgate_z, w_gate[:, 0:2048], mode="nt", out_dtype=F32, name="d_u_z", acc=du)
    du = _mm(dgate_ab, w_gate[:, 2048:4096], mode="nt", out_dtype=F32, name="d_u_g", acc=du)
    dx, dmeta, dg_pre = _pre_bwd(du, h, dy, pre_norm_g, s_rows)

    d_w_in = jnp.concatenate([dw_small[:, 0:416], dw_z[:, 0:1024], dw_fq, dw_fk, dw_fv, dw_small[:, 512:528],
                              dw_z[:, 1024:2048], dw_g], axis=1)
    d_w_uq = _uq_restore(d_w_uq_a)
    d_w_ukv = _ukv_restore(d_w_ukv_a)
    return (loss_p, dx, dmeta, d_w_in, d_w_uq, d_w_ukv, d_w_bm, d_w_bf, d_w_out, dg_pre, dg_post, dg_q, dg_kv, dfb)


def kernel(x, meta_tokens, pre_norm_g, w_in, fox_forget_b, mla_q_norm_g, mla_kv_norm_g, w_uq, w_ukv, w_br_mla, w_br_fox, w_out, post_norm_g, loss_target, m_meta_tokens, m_pre_norm_g, m_w_in, m_fox_forget_b, m_mla_q_norm_g, m_mla_kv_norm_g, m_w_uq, m_w_ukv, m_w_br_mla, m_w_br_fox, m_w_out, m_post_norm_g, v_meta_tokens, v_pre_norm_g, v_w_in, v_fox_forget_b, v_mla_q_norm_g, v_mla_kv_norm_g, v_w_uq, v_w_ukv, v_w_br_mla, v_w_br_fox, v_w_out, v_post_norm_g):
    pk = _pack_shard(w_in[0], w_uq[0], w_ukv[0], w_br_mla[0], w_br_fox[0], w_out[0],
                     jnp.zeros((N_META, 256), F32), BF16)
    wg, meta_g = _gather_weights(pk, meta_tokens)
    pieces = [_unpack_shard(wg[j]) for j in range(N_CHIPS)]
    w_in_f = jnp.concatenate([p[0] for p in pieces], axis=1)
    w_uq_f = jnp.concatenate([p[1] for p in pieces], axis=1)
    w_ukv_f = jnp.concatenate([p[2] for p in pieces], axis=1)
    w_bm = jnp.concatenate([p[3] for p in pieces], axis=0)
    w_bf = jnp.concatenate([p[4] for p in pieces], axis=0)
    w_o = jnp.concatenate([p[5] for p in pieces], axis=0)
    meta_f = jnp.concatenate([meta_g[j] for j in range(N_CHIPS)], axis=1)

    (loss_p, dx, dmeta, d_w_in, d_w_uq, d_w_ukv, d_w_bm, d_w_bf, d_w_out, dg_pre, dg_post, dg_q, dg_kv,
     dfb) = _local_step(x[0], loss_target[0], meta_f, w_in_f, w_uq_f, w_ukv_f, w_bm, w_bf, w_o, pre_norm_g,
                        post_norm_g, mla_q_norm_g, mla_kv_norm_g, fox_forget_b)

    def shard(j):
        return _pack_shard(d_w_in[:, 1900 * j:1900 * (j + 1)], d_w_uq[:, 384 * j:384 * (j + 1)],
                           d_w_ukv[:, 512 * j:512 * (j + 1)], d_w_bm[256 * j:256 * (j + 1)],
                           d_w_bf[256 * j:256 * (j + 1)], d_w_out[256 * j:256 * (j + 1)],
                           dmeta[:, 256 * j:256 * (j + 1)], F32)

    g_all = jnp.stack([shard(j) for j in range(N_CHIPS)], axis=0)
    c = lax.axis_index("c")
    from_sib = _swap_halves(g_all)
    own_half = lax.dynamic_slice_in_dim(g_all, c * PK_HALF, PK_HALF, axis=1)
    part = _add_pair(own_half.reshape(N_CHIPS * PK_HALF, D_MODEL), from_sib.reshape(N_CHIPS * PK_HALF, D_MODEL),
                     "add_cores").reshape(N_CHIPS, PK_HALF, D_MODEL)
    landed = _scatter_chips(part)
    red_half = _add_four(landed, "add_chips")
    g_red = _join_halves(red_half)
    g_w_in, g_w_uq, g_w_ukv, g_w_bm, g_w_bf, g_w_out, g_meta = _unpack_shard(g_red)

    vec = jnp.concatenate([dg_pre.reshape(8, 128), dg_post.reshape(8, 128), dg_q.reshape(2, 128), dg_kv,
                           dfb, _pad_lanes(loss_p), jnp.zeros((3, 128), F32)], axis=0)
    tot = _allreduce_small(vec)
    loss = tot[20, 0]

    def small_pack(pre, post, gq_, gkv_, fb_):
        return jnp.concatenate([pre.reshape(8, 128), post.reshape(8, 128), gq_.reshape(2, 128), gkv_,
                                _pad_lanes(fb_), jnp.zeros((4, 128), F32)], axis=0)

    def small_unpack(t):
        return (t[0:8].reshape(1, 1024), t[8:16].reshape(1, 1024), t[16:18].reshape(1, 256), t[18:19],
                t[19:20, 0:HEADS])

    g_small = jnp.concatenate([tot[0:20], jnp.zeros((4, 128), F32)], axis=0)
    sm = _adamw(small_pack(pre_norm_g, post_norm_g, mla_q_norm_g, mla_kv_norm_g, fox_forget_b), g_small,
                small_pack(m_pre_norm_g, m_post_norm_g, m_mla_q_norm_g, m_mla_kv_norm_g, m_fox_forget_b),
                small_pack(v_pre_norm_g, v_post_norm_g, v_mla_q_norm_g, v_mla_kv_norm_g, v_fox_forget_b),
                "adamw_small")
    g_pre, g_post, g_q, g_kv, g_fb = small_unpack(g_small)
    (d_pre, d_post, d_q, d_kv, d_fb), (nm_pre, nm_post, nm_q, nm_kv, nm_fb), (nv_pre, nv_post, nv_q, nv_kv, nv_fb) = (
        small_unpack(t) for t in sm)

    d_meta, nm_meta, nv_meta = _adamw(meta_tokens, g_meta, m_meta_tokens, v_meta_tokens, "adamw_meta")
    d_win, nm_win, nv_win = _adamw(w_in[0], g_w_in, m_w_in[0], v_w_in[0], "adamw_w_in")
    d_wuq, nm_wuq, nv_wuq = _adamw(w_uq[0], g_w_uq, m_w_uq[0], v_w_uq[0], "adamw_w_uq")
    d_wukv, nm_wukv, nv_wukv = _adamw(w_ukv[0], g_w_ukv, m_w_ukv[0], v_w_ukv[0], "adamw_w_ukv")
    d_wbm, nm_wbm, nv_wbm = _adamw(w_br_mla[0], g_w_bm, m_w_br_mla[0], v_w_br_mla[0], "adamw_w_br_mla")
    d_wbf, nm_wbf, nv_wbf = _adamw(w_br_fox[0], g_w_bf, m_w_br_fox[0], v_w_br_fox[0], "adamw_w_br_fox")
    d_wo, nm_wo, nv_wo = _adamw(w_out[0], g_w_out, m_w_out[0], v_w_out[0], "adamw_w_out")

    def group(meta_, pre, win, fb_, q_, kv_, wuq, wukv, wbm, wbf, wo, post):
        return (meta_, pre, win[None], fb_, q_, kv_, wuq[None], wukv[None], wbm[None], wbf[None], wo[None], post)

    grads = group(g_meta, g_pre, g_w_in, g_fb, g_q, g_kv, g_w_uq, g_w_ukv, g_w_bm, g_w_bf, g_w_out, g_post)
    deltas = group(d_meta, d_pre, d_win, d_fb, d_q, d_kv, d_wuq, d_wukv, d_wbm, d_wbf, d_wo, d_post)
    new_m = group(nm_meta, nm_pre, nm_win, nm_fb, nm_q, nm_kv, nm_wuq, nm_wukv, nm_wbm, nm_wbf, nm_wo, nm_post)
    new_v = group(nv_meta, nv_pre, nv_win, nv_fb, nv_q, nv_kv, nv_wuq, nv_wukv, nv_wbm, nv_wbf, nv_wo, nv_post)
    return (loss, dx[None], *grads, *deltas, *new_m, *new_v)
```

```python
import math

import jax
import jax.numpy as jnp
from jax import lax
from jax.experimental import pallas as pl
from jax.experimental.pallas import tpu as pltpu

F32 = jnp.float32
BF16 = jnp.bfloat16

D_MODEL = 1024
N_META = 16
RMS_EPS = 1e-6
HEADS = 16
PAIRS = HEADS // 2
HEAD_DIM = 64
LANES = 128
MLA_Q_RANK = 256
MLA_KV_RANK = 128
MLA_ROPE = 32
MLA_SCALE = 1.0 / math.sqrt(64 + 32)
FOX_SCALE = 1.0 / math.sqrt(64)
ROPE_THETA = 10000.0

PAD = 256
BLK = 256
NEG = -1e30

C_CQ, C_CKV, C_KPE, C_ZMLA, C_FQ, C_FK, C_FV, C_FL, C_ZFOX, C_GA, C_GB, C_END = (
    0, 256, 384, 416, 1440, 2464, 3488, 4512, 4528, 5552, 6576, 7600)
SMALL_W = 640

PK_ROWS = 2848
PK_HALF = PK_ROWS // 2
N_CHIPS = 4

ADAM_LR = 0.001
ADAM_B1 = 0.9
ADAM_B2 = 0.999
ADAM_EPS = 1e-08
ADAM_WD = 0.01
ADAM_STEP = 10

VMEM_BIG = 56 * 1024 * 1024
MESH = pl.DeviceIdType.MESH


def _cp(dims, vmem=None):
    return pltpu.CompilerParams(dimension_semantics=dims, vmem_limit_bytes=vmem)


def _dot(a, b, ca, cb):
    return lax.dot_general(a, b, (((ca,), (cb,)), ((), ())), preferred_element_type=F32)


def _sigmoid(x):
    return 1.0 / (1.0 + jnp.exp(-x))


def _tile(n, cands):
    for c in cands:
        if n % c == 0:
            return c
    return n


def _mm(a, b, *, mode, out_dtype, name, acc=None):
    if mode == "nn":
        (M, K), N = a.shape, b.shape[1]
    elif mode == "nt":
        (M, K), N = a.shape, b.shape[0]
    else:
        (K, M), N = a.shape, b.shape[1]
    tm = _tile(M, (1088, 1024)) if M > 1024 else M
    tn = _tile(N, (1024,)) if N > 1024 else N
    tk = _tile(K, (1088, 1024)) if K > 1088 else K
    nk = K // tk
    ca, cb = {"nn": (1, 0), "nt": (1, 1), "tn": (0, 0)}[mode]
    a_spec = (pl.BlockSpec((tk, tm), lambda j, i, k: (k, i)) if mode == "tn"
              else pl.BlockSpec((tm, tk), lambda j, i, k: (i, k)))
    b_spec = (pl.BlockSpec((tn, tk), lambda j, i, k: (j, k)) if mode == "nt"
              else pl.BlockSpec((tk, tn), lambda j, i, k: (k, j)))
    o_spec = pl.BlockSpec((tm, tn), lambda j, i, k: (i, j))
    has_acc = acc is not None

    def body(*refs):
        a_ref, b_ref = refs[0], refs[1]
        acc_ref = refs[2] if has_acc else None
        o_ref = refs[3] if has_acc else refs[2]
        part = _dot(a_ref[...].astype(BF16), b_ref[...].astype(BF16), ca, cb)
        if nk == 1:
            if has_acc:
                part = part + acc_ref[...]
            o_ref[...] = part.astype(out_dtype)
        else:
            sc = refs[-1]
            k = pl.program_id(2)

            @pl.when(k == 0)
            def _():
                sc[...] = part + acc_ref[...] if has_acc else part

            @pl.when(k > 0)
            def _():
                sc[...] += part

            @pl.when(k == nk - 1)
            def _():
                o_ref[...] = sc[...].astype(out_dtype)

    ins = [a, b] + ([acc] if has_acc else [])
    in_specs = [a_spec, b_spec] + ([o_spec] if has_acc else [])
    return pl.pallas_call(
        body, name=name, grid=(N // tn, M // tm, nk), in_specs=in_specs, out_specs=o_spec,
        out_shape=jax.ShapeDtypeStruct((M, N), out_dtype),
        scratch_shapes=[pltpu.VMEM((tm, tn), F32)] if nk > 1 else [],
        compiler_params=_cp(("parallel", "parallel", "arbitrary"), VMEM_BIG))(*ins)


def _row(w):
    return pl.BlockSpec((BLK, w), lambda i: (i, 0))


def _rowc(w, c):
    return pl.BlockSpec((BLK, w), lambda i: (i, c))


def _full(shape):
    return pl.BlockSpec(shape, lambda i: tuple(0 for _ in shape))


def _rope(x, c, s):
    lane = lax.broadcasted_iota(jnp.int32, x.shape, 1)
    is_x1 = ((lane >> 4) & 1) == 0
    partner = jnp.where(is_x1, pltpu.roll(x, LANES - 16, 1), pltpu.roll(x, 16, 1))
    return x * c + partner * s


def _row_valid(i):
    rows = i * BLK + lax.broadcasted_iota(jnp.int32, (BLK, 1), 0)
    return (rows < N_META) | (rows >= PAD)


def _rms_pre(h, g):
    lp = h.shape[0]

    def body(h_ref, g_ref, u_ref):
        hv = h_ref[...]
        r = lax.rsqrt(jnp.mean(hv * hv, axis=-1, keepdims=True) + RMS_EPS)
        u_ref[...] = (hv * r * g_ref[...]).astype(BF16)

    return pl.pallas_call(
        body, name="rms_pre", grid=(lp // BLK,),
        in_specs=[_row(D_MODEL), _full((1, D_MODEL))], out_specs=_row(D_MODEL),
        out_shape=jax.ShapeDtypeStruct((lp, D_MODEL), BF16),
        compiler_params=_cp(("parallel",)))(h, g)


def _split3(x):
    hi = x.astype(BF16)
    r1 = x - hi.astype(F32)
    mid = r1.astype(BF16)
    lo = (r1 - mid.astype(F32)).astype(BF16)
    return hi, mid, lo


def _small_prep(small, gq, gkv, fb, ctab, stab, tri):
    lp = small.shape[0]

    def body(sm_ref, gq_ref, gkv_ref, fb_ref, c_ref, s_ref, tri_ref, qn_ref, kvn_ref, kr_ref, ncum_ref, carry):
        i = pl.program_id(0)

        @pl.when(i == 0)
        def _():
            carry[...] = jnp.zeros_like(carry)

        cq = sm_ref[:, 0:256]
        r = lax.rsqrt(jnp.mean(cq * cq, axis=-1, keepdims=True) + RMS_EPS)
        qn_ref[...] = (cq * r * gq_ref[...]).astype(BF16)
        ckv = sm_ref[:, 256:384]
        r = lax.rsqrt(jnp.mean(ckv * ckv, axis=-1, keepdims=True) + RMS_EPS)
        kvn_ref[...] = (ckv * r * gkv_ref[...]).astype(BF16)
        kr_ref[...] = _rope(sm_ref[:, 384:512], c_ref[...], s_ref[...]).astype(BF16)
        fl = sm_ref[:, 512:640] + fb_ref[...]
        lf = jnp.minimum(fl, 0.0) - jnp.log(1.0 + jnp.exp(-jnp.abs(fl)))
        lf = jnp.where(_row_valid(i), lf, 0.0)
        hi, mid, lo = _split3(lf)
        t = tri_ref[...]
        cum = (_dot(t, hi, 1, 0) + _dot(t, mid, 1, 0)) + _dot(t, lo, 1, 0) + carry[...]
        ncum_ref[...] = -cum
        carry[...] = -ncum_ref[BLK - 1:BLK, :]

    return pl.pallas_call(
        body, name="small_prep", grid=(lp // BLK,),
        in_specs=[_row(SMALL_W), _full((1, 256)), _full((1, 128)), _full((1, 128)), _row(128), _row(128),
                  _full((BLK, BLK))],
        out_specs=[_row(256), _row(128), _row(128), _row(128)],
        out_shape=[jax.ShapeDtypeStruct((lp, 256), BF16), jax.ShapeDtypeStruct((lp, 128), BF16),
                   jax.ShapeDtypeStruct((lp, 128), BF16), jax.ShapeDtypeStruct((lp, 128), F32)],
        scratch_shapes=[pltpu.VMEM((1, 128), F32)],
        compiler_params=_cp(("arbitrary",)))(small, gq, gkv, fb, ctab, stab, tri)


def _rope_q(qraw, ctab, stab, *, inverse, out_dtype, name):
    lp = qraw.shape[0]

    def body(q_ref, c_ref, s_ref, o_ref):
        c = c_ref[...]
        s = -s_ref[...] if inverse else s_ref[...]
        for p in range(PAIRS):
            lo = p * 256
            o_ref[:, lo:lo + 128] = q_ref[:, lo:lo + 128].astype(out_dtype)
            o_ref[:, lo + 128:lo + 256] = _rope(q_ref[:, lo + 128:lo + 256].astype(F32), c, s).astype(out_dtype)

    return pl.pallas_call(
        body, name=name, grid=(lp // BLK,),
        in_specs=[_row(PAIRS * 256), _row(128), _row(128)], out_specs=_row(PAIRS * 256),
        out_shape=jax.ShapeDtypeStruct((lp, PAIRS * 256), out_dtype),
        compiler_params=_cp(("parallel",)))(qraw, ctab, stab)


def _gate_fwd(o_mla, o_fox, gate):
    lp = o_mla.shape[0]

    def body(om_ref, of_ref, zm_ref, zf_ref, am_ref, af_ref):
        zm = zm_ref[...].astype(F32)
        am_ref[...] = (om_ref[...] * (zm * _sigmoid(zm))).astype(BF16)
        zf = zf_ref[...].astype(F32)
        af_ref[...] = (of_ref[...] * (zf * _sigmoid(zf))).astype(BF16)

    return pl.pallas_call(
        body, name="gate_fwd", grid=(lp // BLK,),
        in_specs=[_row(D_MODEL), _row(D_MODEL), _rowc(D_MODEL, 0), _rowc(D_MODEL, 1)],
        out_specs=[_row(D_MODEL), _row(D_MODEL)],
        out_shape=[jax.ShapeDtypeStruct((lp, D_MODEL), BF16)] * 2,
        compiler_params=_cp(("parallel",)))(o_mla, o_fox, gate, gate)


def _merge_fwd(gate, y_mla, y_fox):
    lp = y_mla.shape[0]

    def body(ga_ref, gb_ref, ym_ref, yf_ref, m_ref):
        sa = _sigmoid(ga_ref[...].astype(F32))
        sb = _sigmoid(gb_ref[...].astype(F32))
        m_ref[...] = (sa * ym_ref[...] + sb * yf_ref[...]).astype(BF16)

    return pl.pallas_call(
        body, name="merge_fwd", grid=(lp // BLK,),
        in_specs=[_rowc(D_MODEL, 2), _rowc(D_MODEL, 3), _row(D_MODEL), _row(D_MODEL)],
        out_specs=_row(D_MODEL), out_shape=jax.ShapeDtypeStruct((lp, D_MODEL), BF16),
        compiler_params=_cp(("parallel",)))(gate, gate, y_mla, y_fox)


def _tail(h, mixed, tgt, gpost):
    lp = h.shape[0]
    shift = pl.BlockSpec((BLK, D_MODEL), lambda i: (jnp.maximum(i - 1, 0), 0))

    def body(h_ref, mx_ref, t_ref, g_ref, dmx_ref, dy_ref, loss_ref, dg_ref):
        i = pl.program_id(0)

        @pl.when(i == 0)
        def _():
            loss_ref[...] = jnp.zeros_like(loss_ref)
            dg_ref[...] = jnp.zeros_like(dg_ref)
            dmx_ref[...] = jnp.zeros_like(dmx_ref)
            dy_ref[...] = jnp.zeros_like(dy_ref)

        @pl.when(i > 0)
        def _():
            mx = mx_ref[...]
            g = g_ref[...]
            r = lax.rsqrt(jnp.mean(mx * mx, axis=-1, keepdims=True) + RMS_EPS)
            nrm = mx * r
            e = (h_ref[...] + nrm * g) - t_ref[...]
            loss_ref[...] += jnp.sum(0.5 * jnp.sum(e * e, axis=-1, keepdims=True) * (1.0 / D_MODEL),
                                     axis=0, keepdims=True)
            dy = e * (1.0 / D_MODEL)
            dy_ref[...] = dy
            dg_ref[...] += jnp.sum(dy * nrm, axis=0, keepdims=True)
            w = dy * g
            dot = jnp.mean(w * mx, axis=-1, keepdims=True)
            dmx_ref[...] = (r * w - mx * (r * r * r * dot)).astype(BF16)

    return pl.pallas_call(
        body, name="tail", grid=(lp // BLK,),
        in_specs=[_row(D_MODEL), _row(D_MODEL), shift, _full((1, D_MODEL))],
        out_specs=[_row(D_MODEL), _row(D_MODEL), _full((1, 1)), _full((1, D_MODEL))],
        out_shape=[jax.ShapeDtypeStruct((lp, D_MODEL), BF16), jax.ShapeDtypeStruct((lp, D_MODEL), F32),
                   jax.ShapeDtypeStruct((1, 1), F32), jax.ShapeDtypeStruct((1, D_MODEL), F32)],
        compiler_params=_cp(("arbitrary",)))(h, mixed, tgt, gpost)


def _merge_bwd(dm, gate, y_mla, y_fox):
    lp = dm.shape[0]

    def body(dm_ref, ga_ref, gb_ref, ym_ref, yf_ref, dym_ref, dyf_ref, dg_ref):
        dm_v = dm_ref[...]
        sa = _sigmoid(ga_ref[...].astype(F32))
        sb = _sigmoid(gb_ref[...].astype(F32))
        dym_ref[...] = (dm_v * sa).astype(BF16)
        dyf_ref[...] = (dm_v * sb).astype(BF16)
        dg_ref[:, 0:D_MODEL] = (dm_v * ym_ref[...] * (sa * (1.0 - sa))).astype(BF16)
        dg_ref[:, D_MODEL:2 * D_MODEL] = (dm_v * yf_ref[...] * (sb * (1.0 - sb))).astype(BF16)

    return pl.pallas_call(
        body, name="merge_bwd", grid=(lp // BLK,),
        in_specs=[_row(D_MODEL), _rowc(D_MODEL, 2), _rowc(D_MODEL, 3), _row(D_MODEL), _row(D_MODEL)],
        out_specs=[_row(D_MODEL), _row(D_MODEL), _row(2 * D_MODEL)],
        out_shape=[jax.ShapeDtypeStruct((lp, D_MODEL), BF16), jax.ShapeDtypeStruct((lp, D_MODEL), BF16),
                   jax.ShapeDtypeStruct((lp, 2 * D_MODEL), BF16)],
        compiler_params=_cp(("parallel",)))(dm, gate, gate, y_mla, y_fox)


def _gate_bwd(da_mla, da_fox, o_mla, o_fox, gate):
    lp = da_mla.shape[0]

    def one(da, o, z):
        sg = _sigmoid(z)
        do = da * (z * sg)
        dz = da * o * (sg * (1.0 + z * (1.0 - sg)))
        return do.astype(BF16), dz.astype(BF16)

    def body(dam_ref, daf_ref, om_ref, of_ref, zm_ref, zf_ref, dom_ref, dof_ref, dz_ref):
        dom_ref[...], dz_ref[:, 0:D_MODEL] = one(dam_ref[...], om_ref[...], zm_ref[...].astype(F32))
        dof_ref[...], dz_ref[:, D_MODEL:2 * D_MODEL] = one(daf_ref[...], of_ref[...], zf_ref[...].astype(F32))

    return pl.pallas_call(
        body, name="gate_bwd", grid=(lp // BLK,),
        in_specs=[_row(D_MODEL)] * 4 + [_rowc(D_MODEL, 0), _rowc(D_MODEL, 1)],
        out_specs=[_row(D_MODEL), _row(D_MODEL), _row(2 * D_MODEL)],
        out_shape=[jax.ShapeDtypeStruct((lp, D_MODEL), BF16), jax.ShapeDtypeStruct((lp, D_MODEL), BF16),
                   jax.ShapeDtypeStruct((lp, 2 * D_MODEL), BF16)],
        compiler_params=_cp(("parallel",)))(da_mla, da_fox, o_mla, o_fox, gate, gate)


def _small_bwd(small, dqn, dkvn, dkr, dnb_t, drow_t, gq, gkv, fb, ctab, stab, triu):
    lp = small.shape[0]
    nb = lp // BLK

    def rrow(w):
        return pl.BlockSpec((BLK, w), lambda i: (nb - 1 - i, 0))

    def body(sm_ref, dqn_ref, dkvn_ref, dkr_ref, dnb_ref, drow_ref, gq_ref, gkv_ref, fb_ref, c_ref, s_ref, tri_ref,
             ds_ref, dgq_ref, dgkv_ref, dfb_ref, carry):
        i = pl.program_id(0)

        @pl.when(i == 0)
        def _():
            carry[...] = jnp.zeros_like(carry)
            dgq_ref[...] = jnp.zeros_like(dgq_ref)
            dgkv_ref[...] = jnp.zeros_like(dgkv_ref)
            dfb_ref[...] = jnp.zeros_like(dfb_ref)

        def norm_bwd(x, dn, g, dg_ref):
            r = lax.rsqrt(jnp.mean(x * x, axis=-1, keepdims=True) + RMS_EPS)
            dg_ref[...] += jnp.sum(dn * (x * r), axis=0, keepdims=True)
            w = dn * g
            dot = jnp.mean(w * x, axis=-1, keepdims=True)
            return r * w - x * (r * r * r * dot)

        ds_ref[:, 0:256] = norm_bwd(sm_ref[:, 0:256], dqn_ref[...], gq_ref[...], dgq_ref).astype(BF16)
        ds_ref[:, 256:384] = norm_bwd(sm_ref[:, 256:384], dkvn_ref[...], gkv_ref[...], dgkv_ref).astype(BF16)

        dk = dkr_ref[0]
        for p in range(1, PAIRS):
            dk = dk + dkr_ref[p]
        dk = _rope(dk, c_ref[...], -s_ref[...])
        lane = lax.broadcasted_iota(jnp.int32, dk.shape, 1)
        dk = jnp.where(lane < MLA_ROPE, dk + pltpu.roll(dk, LANES - MLA_ROPE, 1), 0.0)
        ds_ref[:, 384:512] = dk.astype(BF16)

        dcr = dnb_ref[...] - drow_ref[...]
        hi, mid, lo = _split3(dcr)
        t = tri_ref[...]
        suf = (_dot(t, hi, 1, 0) + _dot(t, mid, 1, 0)) + _dot(t, lo, 1, 0) + carry[...]
        fl = sm_ref[:, 512:640] + fb_ref[...]
        dfl = jnp.where(_row_valid(nb - 1 - i), -suf * _sigmoid(-fl), 0.0)
        ds_ref[:, 512:640] = dfl.astype(BF16)
        dfb_ref[...] += jnp.sum(dfl, axis=0, keepdims=True)
        carry[...] += jnp.sum(dcr, axis=0, keepdims=True)

    return pl.pallas_call(
        body, name="small_bwd", grid=(nb,),
        in_specs=[rrow(SMALL_W), rrow(256), rrow(128),
                  pl.BlockSpec((PAIRS, BLK, 128), lambda i: (0, nb - 1 - i, 0)), rrow(128), rrow(128),
                  _full((1, 256)), _full((1, 128)), _full((1, 128)), rrow(128), rrow(128), _full((BLK, BLK))],
        out_specs=[rrow(SMALL_W), _full((1, 256)), _full((1, 128)), _full((1, 128))],
        out_shape=[jax.ShapeDtypeStruct((lp, SMALL_W), BF16), jax.ShapeDtypeStruct((1, 256), F32),
                   jax.ShapeDtypeStruct((1, 128), F32), jax.ShapeDtypeStruct((1, 128), F32)],
        scratch_shapes=[pltpu.VMEM((1, 128), F32)],
        compiler_params=_cp(("arbitrary",)))(small, dqn, dkvn, dkr, dnb_t, drow_t, gq, gkv, fb, ctab, stab, triu)


def _pre_bwd(du, h, dy, gpre, s_rows):
    lp = h.shape[0]
    shift = pl.BlockSpec((BLK, D_MODEL), lambda i: (jnp.maximum(i - 1, 0), 0))

    def body(du_ref, h_ref, dy_ref, g_ref, dx_ref, dmeta_ref, dg_ref):
        i = pl.program_id(0)

        @pl.when(i == 0)
        def _():
            dg_ref[...] = jnp.zeros_like(dg_ref)

        hv = h_ref[...]
        duv = du_ref[...]
        r = lax.rsqrt(jnp.mean(hv * hv, axis=-1, keepdims=True) + RMS_EPS)
        dg_ref[...] += jnp.sum(duv * (hv * r), axis=0, keepdims=True)
        w = duv * g_ref[...]
        dot = jnp.mean(w * hv, axis=-1, keepdims=True)
        dh = dy_ref[...] + (r * w - hv * (r * r * r * dot))
        dx_ref[...] = dh

        @pl.when(i == 0)
        def _():
            dmeta_ref[...] = dh[0:N_META, :]

    return pl.pallas_call(
        body, name="pre_bwd", grid=(lp // BLK,),
        in_specs=[_row(D_MODEL), _row(D_MODEL), _row(D_MODEL), _full((1, D_MODEL))],
        out_specs=[shift, _full((N_META, D_MODEL)), _full((1, D_MODEL))],
        out_shape=[jax.ShapeDtypeStruct((s_rows, D_MODEL), F32), jax.ShapeDtypeStruct((N_META, D_MODEL), F32),
                   jax.ShapeDtypeStruct((1, D_MODEL), F32)],
        compiler_params=_cp(("arbitrary",)))(du, h, dy, gpre)


def _head_masks(h):
    lane = lax.broadcasted_iota(jnp.int32, (1, LANES), 1)
    if h == 0:
        return lane < HEAD_DIM, lane < MLA_ROPE
    return lane >= HEAD_DIM, (lane >= MLA_ROPE) & (lane < 2 * MLA_ROPE)


def _valid(q0, k0):
    row = q0 + lax.broadcasted_iota(jnp.int32, (BLK, BLK), 0)
    col = k0 + lax.broadcasted_iota(jnp.int32, (BLK, BLK), 1)
    return (col <= row) & ((col < N_META) | (col >= PAD))


def _flash_fwd(q, k, v, *, kr=None, nbias=None, scale, qcol, kcol, vcol, name):
    lp = q.shape[0]
    nb = lp // BLK
    rope = kr is not None
    bias = nbias is not None
    qw = 256 if rope else 128

    def body(*refs):
        it = iter(refs)
        q_ref, k_ref, v_ref = next(it), next(it), next(it)
        kr_ref = next(it) if rope else None
        nb_ref = next(it) if bias else None
        o_ref, lse_ref = next(it), next(it)
        i = pl.program_id(1)
        q0 = i * BLK
        qa = q_ref[:, 0:128]
        qb = q_ref[:, 128:256] if rope else None
        outs = []
        for h in range(2):
            ma, mb = _head_masks(h)
            qah = jnp.where(ma, qa, jnp.zeros_like(qa))
            qbh = jnp.where(mb, qb, jnp.zeros_like(qb)) if rope else None

            def step(kc, carry, qah=qah, qbh=qbh, h=h):
                m_prev, l_prev, acc = carry
                k0 = pl.multiple_of(kc * BLK, BLK)
                s = _dot(qah, k_ref[pl.ds(k0, BLK), :], 1, 1)
                if rope:
                    s = s + _dot(qbh, kr_ref[pl.ds(k0, BLK), :], 1, 1)
                s = s * scale
                if bias:
                    s = s + nb_ref[0, h:h + 1, pl.ds(k0, BLK)]
                s = jnp.where(_valid(q0, k0), s, NEG)
                m_new = jnp.maximum(m_prev, jnp.max(s, axis=1, keepdims=True))
                alpha = jnp.exp(m_prev - m_new)
                p = jnp.exp(s - m_new)
                l_new = alpha * l_prev + jnp.sum(p, axis=1, keepdims=True)
                acc = alpha * acc + _dot(p.astype(BF16), v_ref[pl.ds(k0, BLK), :], 1, 0)
                return m_new, l_new, acc

            init = (jnp.full((BLK, 1), NEG, F32), jnp.zeros((BLK, 1), F32), jnp.zeros((BLK, LANES), F32))
            m_f, l_f, acc = lax.fori_loop(0, i + 1, step, init)
            outs.append(acc / l_f)
            lse_ref[h] = jnp.broadcast_to(m_f + jnp.log(l_f), (BLK, LANES))
        ma, _ = _head_masks(0)
        o_ref[...] = jnp.where(ma, outs[0], outs[1])

    in_specs = [pl.BlockSpec((BLK, qw), lambda p, i: (i, qcol + p)),
                pl.BlockSpec((lp, 128), lambda p, i: (0, kcol(p))),
                pl.BlockSpec((lp, 128), lambda p, i: (0, vcol(p)))]
    ins = [q, k, v]
    if rope:
        in_specs.append(pl.BlockSpec((lp, 128), lambda p, i: (0, 0)))
        ins.append(kr)
    if bias:
        in_specs.append(pl.BlockSpec((1, 2, lp), lambda p, i: (p, 0, 0)))
        ins.append(nbias)
    return pl.pallas_call(
        body, name=name, grid=(PAIRS, nb), in_specs=in_specs,
        out_specs=[pl.BlockSpec((BLK, 128), lambda p, i: (i, p)),
                   pl.BlockSpec((2, BLK, 128), lambda p, i: (p, i, 0))],
        out_shape=[jax.ShapeDtypeStruct((lp, D_MODEL), F32), jax.ShapeDtypeStruct((HEADS, lp, 128), F32)],
        compiler_params=_cp(("parallel", "arbitrary"), VMEM_BIG))(*ins)


def _flash_bwd(q, k, v, do, o, lse, *, kr=None, nbias=None, scale, qcol, kcol, vcol, name):
    lp = q.shape[0]
    nb = lp // BLK
    rope = kr is not None
    bias = nbias is not None
    qw = 256 if rope else 128

    def body(*refs):
        it = iter(refs)
        q_ref, k_ref, v_ref = next(it), next(it), next(it)
        kr_ref = next(it) if rope else None
        nb_ref = next(it) if bias else None
        do_ref, o_ref, lse_ref = next(it), next(it), next(it)
        dq_ref, dk_ref, dv_ref = next(it), next(it), next(it)
        x_ref = next(it)
        drow_ref = next(it) if bias else None
        delta = next(it)
        kb = pl.program_id(1)
        k0 = kb * BLK
        lane = lax.broadcasted_iota(jnp.int32, (1, LANES), 1)

        @pl.when(kb == 0)
        def _():
            dq_ref[...] = jnp.zeros_like(dq_ref)
            if bias:
                drow_ref[...] = jnp.zeros_like(drow_ref)

            def dstep(c, carry):
                r0 = pl.multiple_of(c * BLK, BLK)
                prod = do_ref[pl.ds(r0, BLK), :].astype(F32) * o_ref[pl.ds(r0, BLK), :]
                d0 = jnp.sum(jnp.where(lane < HEAD_DIM, prod, 0.0), axis=1, keepdims=True)
                d1 = jnp.sum(jnp.where(lane >= HEAD_DIM, prod, 0.0), axis=1, keepdims=True)
                delta[0, pl.ds(r0, BLK), :] = jnp.broadcast_to(d0, (BLK, LANES))
                delta[1, pl.ds(r0, BLK), :] = jnp.broadcast_to(d1, (BLK, LANES))
                return carry

            lax.fori_loop(0, nb, dstep, 0)

        ka = k_ref[...]
        vv = v_ref[...]
        kbv = kr_ref[...] if rope else None
        res = []
        for h in range(2):
            ma, mb = _head_masks(h)
            kah = jnp.where(ma, ka, jnp.zeros_like(ka))
            vh = jnp.where(ma, vv, jnp.zeros_like(vv))
            kbh = jnp.where(mb, kbv, jnp.zeros_like(kbv)) if rope else None
            nbh = nb_ref[0, h:h + 1, :] if bias else None

            def step(qc, carry, kah=kah, vh=vh, kbh=kbh, nbh=nbh, h=h):
                q0 = pl.multiple_of(qc * BLK, BLK)
                qa = q_ref[pl.ds(q0, BLK), 0:128]
                s = _dot(qa, kah, 1, 1)
                if rope:
                    qb = q_ref[pl.ds(q0, BLK), 128:256]
                    s = s + _dot(qb, kbh, 1, 1)
                s = s * scale
                if bias:
                    s = s + nbh
                lse_t = lse_ref[h, pl.ds(q0, BLK), :]
                dl_t = delta[h, pl.ds(q0, BLK), :]
                p = jnp.where(_valid(q0, k0), jnp.exp(s - jnp.concatenate([lse_t, lse_t], axis=1)), 0.0)
                dov = do_ref[pl.ds(q0, BLK), :]
                dp = _dot(dov, vh, 1, 1)
                ds = p * (dp - jnp.concatenate([dl_t, dl_t], axis=1))
                dsc = (ds * scale).astype(BF16)
                out = [carry[0] + _dot(dsc, qa, 0, 0), carry[1] + _dot(p.astype(BF16), dov, 0, 0)]
                dq_ref[pl.ds(q0, BLK), 0:128] += _dot(dsc, kah, 1, 0)
                if rope:
                    out.append(carry[2] + _dot(dsc, qb, 0, 0))
                    dq_ref[pl.ds(q0, BLK), 128:256] += _dot(dsc, kbh, 1, 0)
                if bias:
                    out.append(carry[2] + jnp.sum(ds, axis=0, keepdims=True))
                    drow_ref[0, pl.ds(q0, BLK), :] += jnp.where(lane == h, jnp.sum(ds, axis=1, keepdims=True), 0.0)
                return tuple(out)

            init = [jnp.zeros((BLK, LANES), F32), jnp.zeros((BLK, LANES), F32)]
            if rope:
                init.append(jnp.zeros((BLK, LANES), F32))
            if bias:
                init.append(jnp.zeros((1, BLK), F32))
            res.append(lax.fori_loop(kb, nb, step, tuple(init)))
        ma, mb = _head_masks(0)
        dk_ref[...] = jnp.where(ma, res[0][0], res[1][0]).astype(BF16)
        dv_ref[...] = jnp.where(ma, res[0][1], res[1][1]).astype(BF16)
        if rope:
            _, mb1 = _head_masks(1)
            x_ref[0] = jnp.where(mb, res[0][2], jnp.where(mb1, res[1][2], 0.0))
        if bias:
            x_ref[0, 0:1, :] = res[0][2]
            x_ref[0, 1:2, :] = res[1][2]

    in_specs = [pl.BlockSpec((lp, qw), lambda p, j: (0, qcol + p)),
                pl.BlockSpec((BLK, 128), lambda p, j: (j, kcol(p))),
                pl.BlockSpec((BLK, 128), lambda p, j: (j, vcol(p)))]
    ins = [q, k, v]
    if rope:
        in_specs.append(pl.BlockSpec((BLK, 128), lambda p, j: (j, 0)))
        ins.append(kr)
    if bias:
        in_specs.append(pl.BlockSpec((1, 2, BLK), lambda p, j: (p, 0, j)))
        ins.append(nbias)
    in_specs += [pl.BlockSpec((lp, 128), lambda p, j: (0, p)), pl.BlockSpec((lp, 128), lambda p, j: (0, p)),
                 pl.BlockSpec((2, lp, 128), lambda p, j: (p, 0, 0))]
    ins += [do, o, lse]
    out_specs = [pl.BlockSpec((lp, qw), lambda p, j: (0, p)),
                 pl.BlockSpec((BLK, 128), lambda p, j: (j, p)),
                 pl.BlockSpec((BLK, 128), lambda p, j: (j, p))]
    out_shape = [jax.ShapeDtypeStruct((lp, PAIRS * qw), F32), jax.ShapeDtypeStruct((lp, D_MODEL), BF16),
                 jax.ShapeDtypeStruct((lp, D_MODEL), BF16)]
    if rope:
        out_specs.append(pl.BlockSpec((1, BLK, 128), lambda p, j: (p, j, 0)))
        out_shape.append(jax.ShapeDtypeStruct((PAIRS, lp, 128), F32))
    else:
        out_specs.append(pl.BlockSpec((1, 2, BLK), lambda p, j: (p, 0, j)))
        out_shape.append(jax.ShapeDtypeStruct((PAIRS, 2, lp), F32))
        out_specs.append(pl.BlockSpec((1, lp, 128), lambda p, j: (p, 0, 0)))
        out_shape.append(jax.ShapeDtypeStruct((PAIRS, lp, 128), F32))
    return pl.pallas_call(
        body, name=name, grid=(PAIRS, nb), in_specs=in_specs, out_specs=out_specs, out_shape=out_shape,
        scratch_shapes=[pltpu.VMEM((2, lp, 128), F32)],
        compiler_params=_cp(("parallel", "arbitrary"), VMEM_BIG))(*ins)


def _pair_masks():
    lane = lax.broadcasted_iota(jnp.int32, (1, LANES), 1)
    return ([lane < HEAD_DIM, lane >= HEAD_DIM],
            [lane < MLA_ROPE, (lane >= MLA_ROPE) & (lane < 2 * MLA_ROPE)])


def _mask2(x, masks):
    return [jnp.where(m, x, jnp.zeros_like(x)) for m in masks]


def _attn_fwd(q, k, v, *, kr=None, nbrep=None, scale, qcol, kcol, vcol, name):
    lp = q.shape[0]
    nb = lp // BLK
    rope = kr is not None
    bias = nbrep is not None
    qw = 256 if rope else 128

    def body(*refs):
        it = iter(refs)
        q_ref, k_ref, v_ref = next(it), next(it), next(it)
        kr_ref = next(it) if rope else None
        nb_ref = next(it) if bias else None
        o_ref, lse_ref = next(it), next(it)
        i = pl.program_id(1)
        mas, mbs = _pair_masks()
        qah = _mask2(q_ref[:, 0:128], mas)
        if bias:
            qah = [x * scale for x in qah]
        qbh = _mask2(q_ref[:, 128:256], mbs) if rope else None
        key_l = lax.broadcasted_iota(jnp.int32, (BLK, BLK), 0)
        qry_l = lax.broadcasted_iota(jnp.int32, (BLK, BLK), 1)
        meta_mask = (key_l < N_META) & ((key_l <= qry_l) | (i > 0))
        diag_mask = (key_l <= qry_l) & (i > 0)

        def chunk(kc, carry, mask):
            stats, acc = carry[:4], carry[4]
            k0 = pl.multiple_of(kc * BLK, BLK)
            kk = k_ref[pl.ds(k0, BLK), :]
            vv = v_ref[pl.ds(k0, BLK), :]
            new_stats, alphas, ps = [], [], []
            for h in range(2):
                m_prev, l_prev = stats[2 * h], stats[2 * h + 1]
                s = _dot(kk, qah[h], 1, 1)
                if rope:
                    s = (s + _dot(kr_ref[pl.ds(k0, BLK), :], qbh[h], 1, 1)) * scale
                if bias:
                    nbc = nb_ref[h, pl.ds(k0, BLK), :]
                    s = s + jnp.concatenate([nbc, nbc], axis=1)
                if mask is not None:
                    s = jnp.where(mask, s, NEG)
                m_new = jnp.maximum(m_prev, jnp.max(s, axis=0, keepdims=True))
                alpha = jnp.exp(m_prev - m_new)
                p = jnp.exp(s - m_new)
                new_stats += [m_new, alpha * l_prev + jnp.sum(p, axis=0, keepdims=True)]
                alphas.append(alpha)
                ps.append(p.astype(BF16))
            vcat = jnp.concatenate(_mask2(vv, mas), axis=0)
            pv = _dot(vcat, jnp.concatenate(ps, axis=0), 0, 0)
            a_full = jnp.concatenate([jnp.broadcast_to(a, (HEAD_DIM, BLK)) for a in alphas], axis=0)
            return (*new_stats, a_full * acc + pv)

        neg = jnp.full((1, BLK), NEG, F32)
        zero = jnp.zeros((1, BLK), F32)
        c = chunk(0, (neg, zero, neg, zero, jnp.zeros((LANES, BLK), F32)), meta_mask)
        c = lax.fori_loop(1, i, lambda kc, cr: chunk(kc, cr, None), c)
        c = chunk(i, c, diag_mask)
        inv = jnp.concatenate([jnp.broadcast_to(1.0 / c[1], (HEAD_DIM, BLK)),
                               jnp.broadcast_to(1.0 / c[3], (HEAD_DIM, BLK))], axis=0)
        o_ref[...] = (c[4] * inv).T
        lse_ref[0, 0:1, :] = c[0] + jnp.log(c[1])
        lse_ref[0, 1:2, :] = c[2] + jnp.log(c[3])

    in_specs = [pl.BlockSpec((BLK, qw), lambda p, i: (i, qcol + p)),
                pl.BlockSpec((lp, 128), lambda p, i: (0, kcol(p))),
                pl.BlockSpec((lp, 128), lambda p, i: (0, vcol(p)))]
    ins = [q, k, v]
    if rope:
        in_specs.append(pl.BlockSpec((lp, 128), lambda p, i: (0, 0)))
        ins.append(kr)
    if bias:
        in_specs.append(pl.BlockSpec((2, lp, 128), lambda p, i: (p, 0, 0)))
        ins.append(nbrep)
    return pl.pallas_call(
        body, name=name, grid=(PAIRS, nb), in_specs=in_specs,
        out_specs=[pl.BlockSpec((BLK, 128), lambda p, i: (i, p)),
                   pl.BlockSpec((1, 2, BLK), lambda p, i: (p, 0, i))],
        out_shape=[jax.ShapeDtypeStruct((lp, D_MODEL), F32), jax.ShapeDtypeStruct((PAIRS, 2, lp), F32)],
        compiler_params=_cp(("parallel", "arbitrary"), VMEM_BIG))(*ins)


def _attn_bwd(q, k, v, do, o, lse, *, kr=None, nbrep=None, scale, qcol, kcol, vcol, name):
    lp = q.shape[0]
    nb = lp // BLK
    rope = kr is not None
    bias = nbrep is not None
    qw = 256 if rope else 128

    def body(*refs):
        it = iter(refs)
        q_ref, k_ref, v_ref = next(it), next(it), next(it)
        kr_ref = next(it) if rope else None
        nb_ref = next(it) if bias else None
        do_ref, o_ref, lse_ref = next(it), next(it), next(it)
        dq_ref, dk_ref, dv_ref = next(it), next(it), next(it)
        x_ref = next(it)
        drow_ref = next(it) if bias else None
        delta = next(it)
        kb = pl.program_id(1)
        mas, mbs = _pair_masks()
        lane = lax.broadcasted_iota(jnp.int32, (1, LANES), 1)

        @pl.when(kb == 0)
        def _():
            dq_ref[...] = jnp.zeros_like(dq_ref)
            if bias:
                drow_ref[...] = jnp.zeros_like(drow_ref)
            sub = lax.broadcasted_iota(jnp.int32, (8, LANES), 0)
            sel = (((sub == 0) & mas[0]) | ((sub == 1) & mas[1])).astype(BF16)

            def dstep(c, carry):
                r0 = pl.multiple_of(c * BLK, BLK)
                prod = do_ref[pl.ds(r0, BLK), :].astype(F32) * o_ref[pl.ds(r0, BLK), :]
                hi, mid, lo = _split3(prod)
                delta[:, pl.ds(r0, BLK)] = (_dot(sel, hi, 1, 1) + _dot(sel, mid, 1, 1)) + _dot(sel, lo, 1, 1)
                return carry

            lax.fori_loop(0, nb, dstep, 0)

        kk = k_ref[...]
        vh = _mask2(v_ref[...], mas)
        kcat = jnp.concatenate(_mask2(kk, mas), axis=0)
        if bias:
            kcat = kcat * scale
            nbc = [jnp.concatenate([nb_ref[h], nb_ref[h]], axis=1) for h in range(2)]
        if rope:
            krr = kr_ref[...]
            krcat = jnp.concatenate(_mask2(krr, mbs), axis=0)
        key_l = lax.broadcasted_iota(jnp.int32, (BLK, BLK), 0)
        qry_l = lax.broadcasted_iota(jnp.int32, (BLK, BLK), 1)
        diag_mask = (key_l <= qry_l) & ((kb > 0) | (key_l < N_META))
        meta_mask = key_l < N_META

        def chunk(qc, carry, mask):
            carry = list(carry)
            q0 = pl.multiple_of(qc * BLK, BLK)
            dov = do_ref[pl.ds(q0, BLK), :]
            doh = _mask2(dov, mas)
            qah = _mask2(q_ref[pl.ds(q0, BLK), 0:128], mas)
            if bias:
                qah = [x * scale for x in qah]
            qbh = _mask2(q_ref[pl.ds(q0, BLK), 128:256], mbs) if rope else None
            pbs, dss = [], []
            for h in range(2):
                s = _dot(kk, qah[h], 1, 1)
                if rope:
                    s = (s + _dot(krr, qbh[h], 1, 1)) * scale
                if bias:
                    s = s + nbc[h]
                p = jnp.exp(s - lse_ref[0, h:h + 1, pl.ds(q0, BLK)])
                if mask is not None:
                    p = jnp.where(mask, p, 0.0)
                dp = _dot(vh[h], dov, 1, 1)
                ds = p * (dp - delta[h:h + 1, pl.ds(q0, BLK)])
                if bias:
                    drow_ref[0, h:h + 1, pl.ds(q0, BLK)] += jnp.sum(ds, axis=0, keepdims=True)
                    carry[2 + h] = carry[2 + h] + jnp.sum(ds, axis=1, keepdims=True)
                else:
                    ds = ds * scale
                pbs.append(p.astype(BF16))
                dss.append(ds.astype(BF16))
            ds_lanes = jnp.concatenate(dss, axis=1)
            ds_rows = jnp.concatenate(dss, axis=0)
            carry[0] = carry[0] + _dot(ds_lanes, jnp.concatenate(qah, axis=0), 1, 0)
            carry[1] = carry[1] + _dot(jnp.concatenate(pbs, axis=1), jnp.concatenate(doh, axis=0), 1, 0)
            dq_ref[pl.ds(q0, BLK), 0:128] += _dot(ds_rows, kcat, 0, 0)
            if rope:
                carry[2] = carry[2] + _dot(ds_lanes, jnp.concatenate(qbh, axis=0), 1, 0)
                dq_ref[pl.ds(q0, BLK), 128:256] += _dot(ds_rows, krcat, 0, 0)
            return tuple(carry)

        init = [jnp.zeros((BLK, LANES), F32), jnp.zeros((BLK, LANES), F32)]
        if rope:
            init.append(jnp.zeros((BLK, LANES), F32))
        if bias:
            init += [jnp.zeros((BLK, 1), F32), jnp.zeros((BLK, 1), F32)]
        c = chunk(kb, tuple(init), diag_mask)
        first = kb == 0
        c = lax.fori_loop(kb + 1, jnp.where(first, nb, kb + 1), lambda qc, cr: chunk(qc, cr, meta_mask), c)
        c = lax.fori_loop(kb + 1, jnp.where(first, kb + 1, nb), lambda qc, cr: chunk(qc, cr, None), c)
        dk_ref[...] = c[0].astype(BF16)
        dv_ref[...] = c[1].astype(BF16)
        if rope:
            x_ref[0] = c[2]
        if bias:
            x_ref[0] = jnp.where(lane == 0, c[2], jnp.where(lane == 1, c[3], 0.0))

    in_specs = [pl.BlockSpec((lp, qw), lambda p, j: (0, qcol + p)),
                pl.BlockSpec((BLK, 128), lambda p, j: (j, kcol(p))),
                pl.BlockSpec((BLK, 128), lambda p, j: (j, vcol(p)))]
    ins = [q, k, v]
    if rope:
        in_specs.append(pl.BlockSpec((BLK, 128), lambda p, j: (j, 0)))
        ins.append(kr)
    if bias:
        in_specs.append(pl.BlockSpec((2, BLK, 128), lambda p, j: (p, j, 0)))
        ins.append(nbrep)
    in_specs += [pl.BlockSpec((lp, 128), lambda p, j: (0, p)), pl.BlockSpec((lp, 128), lambda p, j: (0, p)),
                 pl.BlockSpec((1, 2, lp), lambda p, j: (p, 0, 0))]
    ins += [do, o, lse]
    out_specs = [pl.BlockSpec((lp, qw), lambda p, j: (0, p)),
                 pl.BlockSpec((BLK, 128), lambda p, j: (j, p)),
                 pl.BlockSpec((BLK, 128), lambda p, j: (j, p)),
                 pl.BlockSpec((1, BLK, 128), lambda p, j: (p, j, 0))]
    out_shape = [jax.ShapeDtypeStruct((lp, PAIRS * qw), F32), jax.ShapeDtypeStruct((lp, D_MODEL), BF16),
                 jax.ShapeDtypeStruct((lp, D_MODEL), BF16), jax.ShapeDtypeStruct((PAIRS, lp, 128), F32)]
    if bias:
        out_specs.append(pl.BlockSpec((1, 2, lp), lambda p, j: (p, 0, 0)))
        out_shape.append(jax.ShapeDtypeStruct((PAIRS, 2, lp), F32))
    return pl.pallas_call(
        body, name=name, grid=(PAIRS, nb), in_specs=in_specs, out_specs=out_specs, out_shape=out_shape,
        scratch_shapes=[pltpu.VMEM((8, lp), F32)],
        compiler_params=_cp(("parallel", "arbitrary"), VMEM_BIG))(*ins)


def _adamw(w, g, m, v, name):
    rows, cols = w.shape
    tr = 128 if rows * cols > 512 * 1024 else rows

    def body(w_ref, g_ref, m_ref, v_ref, d_ref, nm_ref, nv_ref):
        gv = g_ref[...]
        nm = ADAM_B1 * m_ref[...] + (1.0 - ADAM_B1) * gv
        nv = ADAM_B2 * v_ref[...] + (1.0 - ADAM_B2) * (gv * gv)
        m_hat = nm / (1.0 - ADAM_B1 ** ADAM_STEP)
        v_hat = nv / (1.0 - ADAM_B2 ** ADAM_STEP)
        d_ref[...] = -ADAM_LR * (m_hat / (jnp.sqrt(v_hat) + ADAM_EPS) + ADAM_WD * w_ref[...])
        nm_ref[...] = nm
        nv_ref[...] = nv

    spec = pl.BlockSpec((tr, cols), lambda i: (i, 0))
    return pl.pallas_call(
        body, name=name, grid=(rows // tr,), in_specs=[spec] * 4, out_specs=[spec] * 3,
        out_shape=[jax.ShapeDtypeStruct((rows, cols), F32)] * 3,
        compiler_params=_cp(("parallel",), VMEM_BIG))(w, g, m, v)


def _add_pair(a, b, name):
    rows, cols = a.shape
    tr = _tile(rows, (712, 356))
    spec = pl.BlockSpec((tr, cols), lambda i: (i, 0))

    def body(a_ref, b_ref, o_ref):
        o_ref[...] = a_ref[...] + b_ref[...]

    return pl.pallas_call(
        body, name=name, grid=(rows // tr,), in_specs=[spec, spec], out_specs=spec,
        out_shape=jax.ShapeDtypeStruct((rows, cols), F32), compiler_params=_cp(("parallel",), VMEM_BIG))(a, b)


def _add_four(x, name):
    _, rows, cols = x.shape
    tr = _tile(rows, (712, 356))

    def body(x_ref, o_ref):
        o_ref[...] = ((x_ref[0] + x_ref[1]) + x_ref[2]) + x_ref[3]

    return pl.pallas_call(
        body, name=name, grid=(rows // tr,),
        in_specs=[pl.BlockSpec((N_CHIPS, tr, cols), lambda i: (0, i, 0))],
        out_specs=pl.BlockSpec((tr, cols), lambda i: (i, 0)),
        out_shape=jax.ShapeDtypeStruct((rows, cols), F32), compiler_params=_cp(("parallel",), VMEM_BIG))(x)


def _axes():
    return lax.axis_index("x"), lax.axis_index("y"), lax.axis_index("c")


def _other_chips(x, y):
    return [(1 - x, y), (x, 1 - y), (1 - x, 1 - y)]


ANY = pl.BlockSpec(memory_space=pl.ANY)


def _gather_weights(pk, meta):
    def body(pk_ref, meta_ref, out_ref, mout_ref, send_sems, recv_sems, local_sems):
        x, y, c = _axes()
        me = 2 * x + y
        sib = (x, y, 1 - c)
        chips = _other_chips(x, y)

        def half(ref, chip_idx, cc):
            return ref.at[chip_idx, pl.ds(cc * PK_HALF, PK_HALF), :]

        def copy(k, src, dst, to):
            return pltpu.make_async_remote_copy(src_ref=src, dst_ref=dst, send_sem=send_sems.at[k],
                                                recv_sem=recv_sems.at[k], device_id=to, device_id_type=MESH)

        mine = pltpu.make_async_copy(pk_ref, out_ref.at[me], local_sems.at[0])
        mine_meta = pltpu.make_async_copy(meta_ref, mout_ref.at[me], local_sems.at[1])
        mine.start()
        mine_meta.start()
        first = []
        for j, (px, py) in enumerate(chips):
            first.append(copy(j, pk_ref.at[pl.ds(c * PK_HALF, PK_HALF), :], half(out_ref, me, c), (px, py, c)))
            first.append(copy(3 + j, meta_ref, mout_ref.at[me], (px, py, c)))
        for cp in first:
            cp.start()
        passed = []
        for j, (px, py) in enumerate(chips):
            src_chip = 2 * px + py
            copy(j, half(out_ref, src_chip, c), half(out_ref, src_chip, c), sib).wait_recv()
            fwd = copy(6 + j, half(out_ref, src_chip, c), half(out_ref, src_chip, c), sib)
            fwd.start()
            passed.append(fwd)
            copy(3 + j, mout_ref.at[src_chip], mout_ref.at[src_chip], sib).wait_recv()
        for j, (px, py) in enumerate(chips):
            src_chip = 2 * px + py
            copy(6 + j, half(out_ref, src_chip, 1 - c), half(out_ref, src_chip, 1 - c), sib).wait_recv()
        for cp in first + passed:
            cp.wait_send()
        mine.wait()
        mine_meta.wait()

    return pl.pallas_call(
        body, name="gather_weights", in_specs=[ANY, ANY], out_specs=[ANY, ANY],
        out_shape=[jax.ShapeDtypeStruct((N_CHIPS, PK_ROWS, D_MODEL), BF16),
                   jax.ShapeDtypeStruct((N_CHIPS, N_META, 256), F32)],
        scratch_shapes=[pltpu.SemaphoreType.DMA((9,)), pltpu.SemaphoreType.DMA((9,)),
                        pltpu.SemaphoreType.DMA((2,))])(pk, meta)


def _swap_halves(g):
    def body(g_ref, out_ref, send_sems, recv_sems):
        x, y, c = _axes()
        sib = (x, y, 1 - c)
        cps = []
        for j in range(N_CHIPS):
            cps.append(pltpu.make_async_remote_copy(
                src_ref=g_ref.at[j, pl.ds((1 - c) * PK_HALF, PK_HALF), :], dst_ref=out_ref.at[j],
                send_sem=send_sems.at[j], recv_sem=recv_sems.at[j], device_id=sib, device_id_type=MESH))
        for cp in cps:
            cp.start()
        for cp in cps:
            cp.wait()

    return pl.pallas_call(
        body, name="swap_halves", in_specs=[ANY], out_specs=ANY,
        out_shape=jax.ShapeDtypeStruct((N_CHIPS, PK_HALF, D_MODEL), F32),
        scratch_shapes=[pltpu.SemaphoreType.DMA((N_CHIPS,)), pltpu.SemaphoreType.DMA((N_CHIPS,))])(g)


def _scatter_chips(part):
    def body(p_ref, out_ref, send_sems, recv_sems, local_sem):
        x, y, c = _axes()
        me = 2 * x + y
        mine = pltpu.make_async_copy(p_ref.at[me], out_ref.at[me], local_sem)
        mine.start()
        cps = []
        for j, (px, py) in enumerate(_other_chips(x, y)):
            cps.append(pltpu.make_async_remote_copy(
                src_ref=p_ref.at[2 * px + py], dst_ref=out_ref.at[me], send_sem=send_sems.at[j],
                recv_sem=recv_sems.at[j], device_id=(px, py, c), device_id_type=MESH))
        for cp in cps:
            cp.start()
        for cp in cps:
            cp.wait()
        mine.wait()

    return pl.pallas_call(
        body, name="scatter_chips", in_specs=[ANY], out_specs=ANY,
        out_shape=jax.ShapeDtypeStruct((N_CHIPS, PK_HALF, D_MODEL), F32),
        scratch_shapes=[pltpu.SemaphoreType.DMA((3,)), pltpu.SemaphoreType.DMA((3,)),
                        pltpu.SemaphoreType.DMA])(part)


def _join_halves(r):
    def body(r_ref, out_ref, send_sem, recv_sem, local_sem):
        x, y, c = _axes()
        rows = out_ref.at[pl.ds(c * PK_HALF, PK_HALF), :]
        mine = pltpu.make_async_copy(r_ref, rows, local_sem)
        mine.start()
        cp = pltpu.make_async_remote_copy(src_ref=r_ref, dst_ref=rows, send_sem=send_sem, recv_sem=recv_sem,
                                          device_id=(x, y, 1 - c), device_id_type=MESH)
        cp.start()
        cp.wait_send()
        other = out_ref.at[pl.ds((1 - c) * PK_HALF, PK_HALF), :]
        pltpu.make_async_remote_copy(src_ref=r_ref, dst_ref=other, send_sem=send_sem, recv_sem=recv_sem,
                                     device_id=(x, y, 1 - c), device_id_type=MESH).wait_recv()
        mine.wait()

    return pl.pallas_call(
        body, name="join_halves", in_specs=[ANY], out_specs=ANY,
        out_shape=jax.ShapeDtypeStruct((PK_ROWS, D_MODEL), F32),
        scratch_shapes=[pltpu.SemaphoreType.DMA, pltpu.SemaphoreType.DMA, pltpu.SemaphoreType.DMA])(r)


SMALL_ROWS = 24


def _allreduce_small(vec):
    def body(v_ref, out_ref, slots, send_sems, recv_sems):
        x, y, c = _axes()
        me = 4 * x + 2 * y + c
        slots[me] = v_ref[...]
        cps = []
        for k in range(1, 8):
            kx, ky, kc = (k >> 2) & 1, (k >> 1) & 1, k & 1
            peer = (1 - x if kx else x, 1 - y if ky else y, 1 - c if kc else c)
            cps.append(pltpu.make_async_remote_copy(
                src_ref=v_ref, dst_ref=slots.at[me], send_sem=send_sems.at[k - 1], recv_sem=recv_sems.at[k - 1],
                device_id=peer, device_id_type=MESH))
        for cp in cps:
            cp.start()
        for cp in cps:
            cp.wait()
        tot = slots[0]
        for k in range(1, 8):
            tot = tot + slots[k]
        out_ref[...] = tot

    return pl.pallas_call(
        body, name="allreduce_small",
        in_specs=[pl.BlockSpec(memory_space=pltpu.VMEM)], out_specs=pl.BlockSpec(memory_space=pltpu.VMEM),
        out_shape=jax.ShapeDtypeStruct((SMALL_ROWS, 128), F32),
        scratch_shapes=[pltpu.VMEM((8, SMALL_ROWS, 128), F32), pltpu.SemaphoreType.DMA((7,)),
                        pltpu.SemaphoreType.DMA((7,))])(vec)


def _pack_shard(w_in, w_uq, w_ukv, w_br_mla, w_br_fox, w_out, meta, dtype):
    parts = [w_in.reshape(1900, D_MODEL), w_uq.reshape(96, D_MODEL), w_ukv.reshape(64, D_MODEL),
             w_br_mla.reshape(256, D_MODEL), w_br_fox.reshape(256, D_MODEL), w_out.reshape(256, D_MODEL),
             meta.reshape(4, D_MODEL), jnp.zeros((PK_ROWS - 2832, D_MODEL), meta.dtype)]
    return jnp.concatenate([p.astype(dtype) for p in parts], axis=0)


def _unpack_shard(pk):
    return (pk[0:1900].reshape(D_MODEL, 1900), pk[1900:1996].reshape(256, 384), pk[1996:2060].reshape(128, 512),
            pk[2060:2316], pk[2316:2572], pk[2572:2828], pk[2828:2832].reshape(N_META, 256))


def _uq_arrange(w):
    w3 = w.reshape(256, HEADS, 96)
    nope = w3[:, :, :64].reshape(256, PAIRS, 128)
    pe = w3[:, :, 64:].reshape(256, PAIRS, 64)
    return jnp.concatenate([nope, pe, jnp.zeros((256, PAIRS, 64), w.dtype)], axis=2).reshape(256, PAIRS * 256)


def _uq_restore(g):
    g3 = g.reshape(256, PAIRS, 256)
    nope = g3[:, :, :128].reshape(256, HEADS, 64)
    pe = g3[:, :, 128:192].reshape(256, HEADS, 32)
    return jnp.concatenate([nope, pe], axis=2).reshape(256, HEADS * 96)


def _ukv_arrange(w):
    w3 = w.reshape(128, HEADS, 128)
    return jnp.concatenate([w3[:, :, :64].reshape(128, 1024), w3[:, :, 64:].reshape(128, 1024)], axis=1)


def _ukv_restore(g):
    kn = g[:, :1024].reshape(128, HEADS, 64)
    vv = g[:, 1024:].reshape(128, HEADS, 64)
    return jnp.concatenate([kn, vv], axis=2).reshape(128, HEADS * 128)


def _rope_tables(lp):
    r = jnp.arange(lp)
    pos = jnp.where(r < N_META, r, jnp.where(r >= PAD, r - PAD + N_META, 0))
    half = MLA_ROPE // 2
    inv_freq = ROPE_THETA ** (-jnp.arange(half, dtype=F32) / half)
    ang = pos.astype(F32)[:, None] * inv_freq[None, :]
    cos, sin = jnp.cos(ang), jnp.sin(ang)
    one, zero = jnp.ones((lp, 64), F32), jnp.zeros((lp, 64), F32)
    return (jnp.concatenate([cos, cos, cos, cos, one], axis=1),
            jnp.concatenate([-sin, sin, -sin, sin, zero], axis=1))


def _pad_lanes(v, n=128):
    return jnp.pad(v, ((0, 0), (0, n - v.shape[1])))


def _local_step(x2, tgt2, meta_f, w_in_f, w_uq_f, w_ukv_f, w_bm, w_bf, w_o, pre_norm_g, post_norm_g, mla_q_norm_g,
                mla_kv_norm_g, fox_forget_b):
    s_rows = x2.shape[0]
    lp = PAD + s_rows

    kpe = w_in_f[:, C_KPE:C_ZMLA]
    w_small = jnp.concatenate([w_in_f[:, C_CQ:C_KPE], kpe, kpe, jnp.zeros((D_MODEL, 64), BF16),
                               w_in_f[:, C_FL:C_ZFOX], jnp.zeros((D_MODEL, 112), BF16)], axis=1)
    w_attn = w_in_f[:, C_FQ:C_FL]
    w_gate = jnp.concatenate([w_in_f[:, C_ZMLA:C_FQ], w_in_f[:, C_ZFOX:C_END]], axis=1)
    w_uq_a = _uq_arrange(w_uq_f)
    w_ukv_a = _ukv_arrange(w_ukv_f)

    ctab, stab = _rope_tables(lp)
    ii = jnp.arange(BLK)
    tri_lo = (ii[:, None] >= ii[None, :]).astype(BF16)
    tri_up = (ii[:, None] <= ii[None, :]).astype(BF16)
    fb128 = _pad_lanes(fox_forget_b)

    h = jnp.concatenate([meta_f, jnp.zeros((PAD - N_META, D_MODEL), F32), x2], axis=0)
    u = _rms_pre(h, pre_norm_g)
    small = _mm(u, w_small, mode="nn", out_dtype=F32, name="proj_small")
    attn = _mm(u, w_attn, mode="nn", out_dtype=BF16, name="proj_attn")
    gate = _mm(u, w_gate, mode="nn", out_dtype=BF16, name="proj_gate")
    qn, kvn, kr, ncum = _small_prep(small, mla_q_norm_g, mla_kv_norm_g, fb128, ctab, stab, tri_lo)
    qraw = _mm(qn, w_uq_a, mode="nn", out_dtype=F32, name="mla_q")
    qcat = _rope_q(qraw, ctab, stab, inverse=False, out_dtype=BF16, name="rope_q")
    kv = _mm(kvn, w_ukv_a, mode="nn", out_dtype=BF16, name="mla_kv")
    nbrep = jnp.broadcast_to(ncum[:, :HEADS].T[:, :, None], (HEADS, lp, LANES))

    mla_cols = dict(qcol=0, kcol=lambda p: p, vcol=lambda p: PAIRS + p)
    fox_cols = dict(qcol=0, kcol=lambda p: PAIRS + p, vcol=lambda p: 2 * PAIRS + p)
    o_mla, lse_mla = _attn_fwd(qcat, kv, kv, kr=kr, scale=MLA_SCALE, name="mla_fwd", **mla_cols)
    o_fox, lse_fox = _attn_fwd(attn, attn, attn, nbrep=nbrep, scale=FOX_SCALE, name="fox_fwd", **fox_cols)

    a_mla, a_fox = _gate_fwd(o_mla, o_fox, gate)
    y_mla = _mm(a_mla, w_bm, mode="nn", out_dtype=F32, name="br_mla")
    y_fox = _mm(a_fox, w_bf, mode="nn", out_dtype=F32, name="br_fox")
    mg = _merge_fwd(gate, y_mla, y_fox)
    mixed = _mm(mg, w_o, mode="nn", out_dtype=F32, name="out_proj")
    dmixed, dy, loss_p, dg_post = _tail(h, mixed, tgt2, post_norm_g)

    d_w_out = _mm(mg, dmixed, mode="tn", out_dtype=F32, name="d_w_out")
    dm = _mm(dmixed, w_o, mode="nt", out_dtype=F32, name="d_merge")
    dy_mla, dy_fox, dgate_ab = _merge_bwd(dm, gate, y_mla, y_fox)
    d_w_bm = _mm(a_mla, dy_mla, mode="tn", out_dtype=F32, name="d_w_br_mla")
    d_w_bf = _mm(a_fox, dy_fox, mode="tn", out_dtype=F32, name="d_w_br_fox")
    da_mla = _mm(dy_mla, w_bm, mode="nt", out_dtype=F32, name="d_a_mla")
    da_fox = _mm(dy_fox, w_bf, mode="nt", out_dtype=F32, name="d_a_fox")
    do_mla, do_fox, dgate_z = _gate_bwd(da_mla, da_fox, o_mla, o_fox, gate)

    dqcat, dkn, dvm, dkr = _attn_bwd(qcat, kv, kv, do_mla, o_mla, lse_mla, kr=kr, scale=MLA_SCALE,
                                     name="mla_bwd", **mla_cols)
    dfq, dfk, dfv, dcol, drow = _attn_bwd(attn, attn, attn, do_fox, o_fox, lse_fox, nbrep=nbrep, scale=FOX_SCALE,
                                          name="fox_bwd", **fox_cols)

    dq_a = _rope_q(dqcat, ctab, stab, inverse=True, out_dtype=BF16, name="rope_q_bwd")
    d_w_uq_a = _mm(qn, dq_a, mode="tn", out_dtype=F32, name="d_w_uq")
    dqn = _mm(dq_a, w_uq_a, mode="nt", out_dtype=F32, name="d_qn")
    d_w_ukv_a = jnp.concatenate([_mm(kvn, dkn, mode="tn", out_dtype=F32, name="d_w_uk"),
                                 _mm(kvn, dvm, mode="tn", out_dtype=F32, name="d_w_uv")], axis=1)
    dkvn = _mm(dkn, w_ukv_a[:, :1024], mode="nt", out_dtype=F32, name="d_kvn_k")
    dkvn = _mm(dvm, w_ukv_a[:, 1024:], mode="nt", out_dtype=F32, name="d_kvn_v", acc=dkvn)
    dnb_t = _pad_lanes(jnp.transpose(dcol[:, :, 0:2], (1, 0, 2)).reshape(lp, HEADS))
    drow_t = _pad_lanes(drow.reshape(HEADS, lp).T)
    dsmall, dg_q, dg_kv, dfb = _small_bwd(small, dqn, dkvn, dkr, dnb_t, drow_t, mla_q_norm_g, mla_kv_norm_g,
                                          fb128, ctab, stab, tri_up)

    dw_small = _mm(u, dsmall, mode="tn", out_dtype=F32, name="d_w_small")
    dw_fq = _mm(u, dfq, mode="tn", out_dtype=F32, name="d_w_fq")
    dw_fk = _mm(u, dfk, mode="tn", out_dtype=F32, name="d_w_fk")
    dw_fv = _mm(u, dfv, mode="tn", out_dtype=F32, name="d_w_fv")
    dw_z = _mm(u, dgate_z, mode="tn", out_dtype=F32, name="d_w_z")
    dw_g = _mm(u, dgate_ab, mode="tn", out_dtype=F32, name="d_w_g")
    du = _mm(dsmall, w_small, mode="nt", out_dtype=F32, name="d_u_small")
    du = _mm(dfq, w_attn[:, 0:1024], mode="nt", out_dtype=F32, name="d_u_fq", acc=du)
    du = _mm(dfk, w_attn[:, 1024:2048], mode="nt", out_dtype=F32, name="d_u_fk", acc=du)
    du = _mm(dfv, w_attn[:, 2048:3072], mode="nt", out_dtype=F32, name="d_u_fv", acc=du)
    du = _mm(dgate_z, w_gate[:, 0:2048], mode="nt", out_dtype=F32, name="d_u_z", acc=du)
    du = _mm(dgate_ab, w_gate[:, 2048:4096], mode="nt", out_dtype=F32, name="d_u_g", acc=du)
    dx, dmeta, dg_pre = _pre_bwd(du, h, dy, pre_norm_g, s_rows)

    d_w_in = jnp.concatenate([dw_small[:, 0:416], dw_z[:, 0:1024], dw_fq, dw_fk, dw_fv, dw_small[:, 512:528],
                              dw_z[:, 1024:2048], dw_g], axis=1)
    d_w_uq = _uq_restore(d_w_uq_a)
    d_w_ukv = _ukv_restore(d_w_ukv_a)
    return (loss_p, dx, dmeta, d_w_in, d_w_uq, d_w_ukv, d_w_bm, d_w_bf, d_w_out, dg_pre, dg_post, dg_q, dg_kv, dfb)


def kernel(x, meta_tokens, pre_norm_g, w_in, fox_forget_b, mla_q_norm_g, mla_kv_norm_g, w_uq, w_ukv, w_br_mla, w_br_fox, w_out, post_norm_g, loss_target, m_meta_tokens, m_pre_norm_g, m_w_in, m_fox_forget_b, m_mla_q_norm_g, m_mla_kv_norm_g, m_w_uq, m_w_ukv, m_w_br_mla, m_w_br_fox, m_w_out, m_post_norm_g, v_meta_tokens, v_pre_norm_g, v_w_in, v_fox_forget_b, v_mla_q_norm_g, v_mla_kv_norm_g, v_w_uq, v_w_ukv, v_w_br_mla, v_w_br_fox, v_w_out, v_post_norm_g):
    pk = _pack_shard(w_in[0], w_uq[0], w_ukv[0], w_br_mla[0], w_br_fox[0], w_out[0],
                     jnp.zeros((N_META, 256), F32), BF16)
    wg, meta_g = _gather_weights(pk, meta_tokens)
    pieces = [_unpack_shard(wg[j]) for j in range(N_CHIPS)]
    w_in_f = jnp.concatenate([p[0] for p in pieces], axis=1)
    w_uq_f = jnp.concatenate([p[1] for p in pieces], axis=1)
    w_ukv_f = jnp.concatenate([p[2] for p in pieces], axis=1)
    w_bm = jnp.concatenate([p[3] for p in pieces], axis=0)
    w_bf = jnp.concatenate([p[4] for p in pieces], axis=0)
    w_o = jnp.concatenate([p[5] for p in pieces], axis=0)
    meta_f = jnp.concatenate([meta_g[j] for j in range(N_CHIPS)], axis=1)

    (loss_p, dx, dmeta, d_w_in, d_w_uq, d_w_ukv, d_w_bm, d_w_bf, d_w_out, dg_pre, dg_post, dg_q, dg_kv,
     dfb) = _local_step(x[0], loss_target[0], meta_f, w_in_f, w_uq_f, w_ukv_f, w_bm, w_bf, w_o, pre_norm_g,
                        post_norm_g, mla_q_norm_g, mla_kv_norm_g, fox_forget_b)

    def shard(j):
        return _pack_shard(d_w_in[:, 1900 * j:1900 * (j + 1)], d_w_uq[:, 384 * j:384 * (j + 1)],
                           d_w_ukv[:, 512 * j:512 * (j + 1)], d_w_bm[256 * j:256 * (j + 1)],
                           d_w_bf[256 * j:256 * (j + 1)], d_w_out[256 * j:256 * (j + 1)],
                           dmeta[:, 256 * j:256 * (j + 1)], F32)

    g_all = jnp.stack([shard(j) for j in range(N_CHIPS)], axis=0)
    c = lax.axis_index("c")
    from_sib = _swap_halves(g_all)
    own_half = lax.dynamic_slice_in_dim(g_all, c * PK_HALF, PK_HALF, axis=1)
    part = _add_pair(own_half.reshape(N_CHIPS * PK_HALF, D_MODEL), from_sib.reshape(N_CHIPS * PK_HALF, D_MODEL),
                     "add_cores").reshape(N_CHIPS, PK_HALF, D_MODEL)
    landed = _scatter_chips(part)
    red_half = _add_four(landed, "add_chips")
    g_red = _join_halves(red_half)
    g_w_in, g_w_uq, g_w_ukv, g_w_bm, g_w_bf, g_w_out, g_meta = _unpack_shard(g_red)

    vec = jnp.concatenate([dg_pre.reshape(8, 128), dg_post.reshape(8, 128), dg_q.reshape(2, 128), dg_kv,
                           dfb, _pad_lanes(loss_p), jnp.zeros((3, 128), F32)], axis=0)
    tot = _allreduce_small(vec)
    loss = tot[20, 0]

    def small_pack(pre, post, gq_, gkv_, fb_):
        return jnp.concatenate([pre.reshape(8, 128), post.reshape(8, 128), gq_.reshape(2, 128), gkv_,
                                _pad_lanes(fb_), jnp.zeros((4, 128), F32)], axis=0)

    def small_unpack(t):
        return (t[0:8].reshape(1, 1024), t[8:16].reshape(1, 1024), t[16:18].reshape(1, 256), t[18:19],
                t[19:20, 0:HEADS])

    g_small = jnp.concatenate([tot[0:20], jnp.zeros((4, 128), F32)], axis=0)
    sm = _adamw(small_pack(pre_norm_g, post_norm_g, mla_q_norm_g, mla_kv_norm_g, fox_forget_b), g_small,
                small_pack(m_pre_norm_g, m_post_norm_g, m_mla_q_norm_g, m_mla_kv_norm_g, m_fox_forget_b),
                small_pack(v_pre_norm_g, v_post_norm_g, v_mla_q_norm_g, v_mla_kv_norm_g, v_fox_forget_b),
                "adamw_small")
    g_pre, g_post, g_q, g_kv, g_fb = small_unpack(g_small)
    (d_pre, d_post, d_q, d_kv, d_fb), (nm_pre, nm_post, nm_q, nm_kv, nm_fb), (nv_pre, nv_post, nv_q, nv_kv, nv_fb) = (
        small_unpack(t) for t in sm)

    d_meta, nm_meta, nv_meta = _adamw(meta_tokens, g_meta, m_meta_tokens, v_meta_tokens, "adamw_meta")
    d_win, nm_win, nv_win = _adamw(w_in[0], g_w_in, m_w_in[0], v_w_in[0], "adamw_w_in")
    d_wuq, nm_wuq, nv_wuq = _adamw(w_uq[0], g_w_uq, m_w_uq[0], v_w_uq[0], "adamw_w_uq")
    d_wukv, nm_wukv, nv_wukv = _adamw(w_ukv[0], g_w_ukv, m_w_ukv[0], v_w_ukv[0], "adamw_w_ukv")
    d_wbm, nm_wbm, nv_wbm = _adamw(w_br_mla[0], g_w_bm, m_w_br_mla[0], v_w_br_mla[0], "adamw_w_br_mla")
    d_wbf, nm_wbf, nv_wbf = _adamw(w_br_fox[0], g_w_bf, m_w_br_fox[0], v_w_br_fox[0], "adamw_w_br_fox")
    d_wo, nm_wo, nv_wo = _adamw(w_out[0], g_w_out, m_w_out[0], v_w_out[0], "adamw_w_out")

    def group(meta_, pre, win, fb_, q_, kv_, wuq, wukv, wbm, wbf, wo, post):
        return (meta_, pre, win[None], fb_, q_, kv_, wuq[None], wukv[None], wbm[None], wbf[None], wo[None], post)

    grads = group(g_meta, g_pre, g_w_in, g_fb, g_q, g_kv, g_w_uq, g_w_ukv, g_w_bm, g_w_bf, g_w_out, g_post)
    deltas = group(d_meta, d_pre, d_win, d_fb, d_q, d_kv, d_wuq, d_wukv, d_wbm, d_wbf, d_wo, d_post)
    new_m = group(nm_meta, nm_pre, nm_win, nm_fb, nm_q, nm_kv, nm_wuq, nm_wukv, nm_wbm, nm_wbf, nm_wo, nm_post)
    new_v = group(nv_meta, nv_pre, nv_win, nv_fb, nv_q, nv_kv, nv_wuq, nv_wukv, nv_wbm, nv_wbf, nv_wo, nv_post)
    return (loss, dx[None], *grads, *deltas, *new_m, *new_v)
```

```python
import math

import jax
import jax.numpy as jnp
from jax import lax
from jax.experimental import pallas as pl
from jax.experimental.pallas import tpu as pltpu

F32 = jnp.float32
BF16 = jnp.bfloat16

D_MODEL = 1024
N_META = 16
RMS_EPS = 1e-6
HEADS = 16
PAIRS = HEADS // 2
HEAD_DIM = 64
LANES = 128
MLA_ROPE = 32
MLA_SCALE = 1.0 / math.sqrt(64 + 32)
FOX_SCALE = 1.0 / math.sqrt(64)
ROPE_THETA = 10000.0

PAD = 256
BLK = 256
NEG = -1e30

C_CQ, C_CKV, C_KPE, C_ZMLA, C_FQ, C_FK, C_FV, C_FL, C_ZFOX, C_GA, C_GB, C_END = (
    0, 256, 384, 416, 1440, 2464, 3488, 4512, 4528, 5552, 6576, 7600)
SMALL_W = 640
W_IN_SHARD = 1900

P2_ROWS = 960
N_CHIPS = 4

ADAM_LR = 0.001
ADAM_B1 = 0.9
ADAM_B2 = 0.999
ADAM_EPS = 1e-08
ADAM_WD = 0.01
ADAM_STEP = 10

VMEM_BIG = 56 * 1024 * 1024
MESH = pl.DeviceIdType.MESH


def _cp(dims, vmem=None):
    return pltpu.CompilerParams(dimension_semantics=dims, vmem_limit_bytes=vmem)


def _dot(a, b, ca, cb):
    return lax.dot_general(a, b, (((ca,), (cb,)), ((), ())), preferred_element_type=F32)


def _sigmoid(x):
    return 1.0 / (1.0 + jnp.exp(-x))


def _tile(n, cands):
    for c in cands:
        if n % c == 0:
            return c
    return n


def _mm(a, b, *, mode, out_dtype, name, acc=None):
    if mode == "nn":
        (M, K), N = a.shape, b.shape[1]
    elif mode == "nt":
        (M, K), N = a.shape, b.shape[0]
    else:
        (K, M), N = a.shape, b.shape[1]
    tm = _tile(M, (1088, 1024)) if M > 1024 else M
    tn = _tile(N, (1024,)) if N > 1024 else N
    tk = _tile(K, (1088, 1024)) if K > 1088 else K
    nk = K // tk
    ca, cb = {"nn": (1, 0), "nt": (1, 1), "tn": (0, 0)}[mode]
    a_spec = (pl.BlockSpec((tk, tm), lambda j, i, k: (k, i)) if mode == "tn"
              else pl.BlockSpec((tm, tk), lambda j, i, k: (i, k)))
    b_spec = (pl.BlockSpec((tn, tk), lambda j, i, k: (j, k)) if mode == "nt"
              else pl.BlockSpec((tk, tn), lambda j, i, k: (k, j)))
    o_spec = pl.BlockSpec((tm, tn), lambda j, i, k: (i, j))
    has_acc = acc is not None

    def body(*refs):
        a_ref, b_ref = refs[0], refs[1]
        acc_ref = refs[2] if has_acc else None
        o_ref = refs[3] if has_acc else refs[2]
        part = _dot(a_ref[...].astype(BF16), b_ref[...].astype(BF16), ca, cb)
        if nk == 1:
            if has_acc:
                part = part + acc_ref[...]
            o_ref[...] = part.astype(out_dtype)
        else:
            sc = refs[-1]
            k = pl.program_id(2)

            @pl.when(k == 0)
            def _():
                sc[...] = part + acc_ref[...] if has_acc else part

            @pl.when(k > 0)
            def _():
                sc[...] += part

            @pl.when(k == nk - 1)
            def _():
                o_ref[...] = sc[...].astype(out_dtype)

    ins = [a, b] + ([acc] if has_acc else [])
    in_specs = [a_spec, b_spec] + ([o_spec] if has_acc else [])
    return pl.pallas_call(
        body, name=name, grid=(N // tn, M // tm, nk), in_specs=in_specs, out_specs=o_spec,
        out_shape=jax.ShapeDtypeStruct((M, N), out_dtype),
        scratch_shapes=[pltpu.VMEM((tm, tn), F32)] if nk > 1 else [],
        compiler_params=_cp(("parallel", "parallel", "arbitrary"), VMEM_BIG))(*ins)


def _row(w):
    return pl.BlockSpec((BLK, w), lambda i: (i, 0))


def _rowc(w, c):
    return pl.BlockSpec((BLK, w), lambda i: (i, c))


def _full(shape):
    return pl.BlockSpec(shape, lambda i: tuple(0 for _ in shape))


def _rope(x, c, s):
    lane = lax.broadcasted_iota(jnp.int32, x.shape, 1)
    is_x1 = ((lane >> 4) & 1) == 0
    partner = jnp.where(is_x1, pltpu.roll(x, LANES - 16, 1), pltpu.roll(x, 16, 1))
    return x * c + partner * s


def _row_valid(i):
    rows = i * BLK + lax.broadcasted_iota(jnp.int32, (BLK, 1), 0)
    return (rows < N_META) | (rows >= PAD)


def _rms_pre(h, g):
    lp = h.shape[0]

    def body(h_ref, g_ref, u_ref):
        hv = h_ref[...]
        r = lax.rsqrt(jnp.mean(hv * hv, axis=-1, keepdims=True) + RMS_EPS)
        u_ref[...] = (hv * r * g_ref[...]).astype(BF16)

    return pl.pallas_call(
        body, name="rms_pre", grid=(lp // BLK,),
        in_specs=[_row(D_MODEL), _full((1, D_MODEL))], out_specs=_row(D_MODEL),
        out_shape=jax.ShapeDtypeStruct((lp, D_MODEL), BF16),
        compiler_params=_cp(("parallel",)))(h, g)


def _split3(x):
    hi = x.astype(BF16)
    r1 = x - hi.astype(F32)
    mid = r1.astype(BF16)
    lo = (r1 - mid.astype(F32)).astype(BF16)
    return hi, mid, lo


def _small_prep(small, gq, gkv, fb, ctab, stab, tri):
    lp = small.shape[0]

    def body(sm_ref, gq_ref, gkv_ref, fb_ref, c_ref, s_ref, tri_ref, qn_ref, kvn_ref, kr_ref, ncum_ref, carry):
        i = pl.program_id(0)

        @pl.when(i == 0)
        def _():
            carry[...] = jnp.zeros_like(carry)

        cq = sm_ref[:, 0:256]
        r = lax.rsqrt(jnp.mean(cq * cq, axis=-1, keepdims=True) + RMS_EPS)
        qn_ref[...] = (cq * r * gq_ref[...]).astype(BF16)
        ckv = sm_ref[:, 256:384]
        r = lax.rsqrt(jnp.mean(ckv * ckv, axis=-1, keepdims=True) + RMS_EPS)
        kvn_ref[...] = (ckv * r * gkv_ref[...]).astype(BF16)
        kr_ref[...] = _rope(sm_ref[:, 384:512], c_ref[...], s_ref[...]).astype(BF16)
        fl = sm_ref[:, 512:640] + fb_ref[...]
        lf = jnp.minimum(fl, 0.0) - jnp.log(1.0 + jnp.exp(-jnp.abs(fl)))
        lf = jnp.where(_row_valid(i), lf, 0.0)
        hi, mid, lo = _split3(lf)
        t = tri_ref[...]
        cum = (_dot(t, hi, 1, 0) + _dot(t, mid, 1, 0)) + _dot(t, lo, 1, 0) + carry[...]
        ncum_ref[...] = -cum
        carry[...] = -ncum_ref[BLK - 1:BLK, :]

    return pl.pallas_call(
        body, name="small_prep", grid=(lp // BLK,),
        in_specs=[_row(SMALL_W), _full((1, 256)), _full((1, 128)), _full((1, 128)), _row(128), _row(128),
                  _full((BLK, BLK))],
        out_specs=[_row(256), _row(128), _row(128), _row(128)],
        out_shape=[jax.ShapeDtypeStruct((lp, 256), BF16), jax.ShapeDtypeStruct((lp, 128), BF16),
                   jax.ShapeDtypeStruct((lp, 128), BF16), jax.ShapeDtypeStruct((lp, 128), F32)],
        scratch_shapes=[pltpu.VMEM((1, 128), F32)],
        compiler_params=_cp(("arbitrary",)))(small, gq, gkv, fb, ctab, stab, tri)


def _rope_q(qraw, ctab, stab, *, inverse, out_dtype, name):
    lp = qraw.shape[0]

    def body(q_ref, c_ref, s_ref, o_ref):
        c = c_ref[...]
        s = -s_ref[...] if inverse else s_ref[...]
        for p in range(PAIRS):
            lo = p * 256
            o_ref[:, lo:lo + 128] = q_ref[:, lo:lo + 128].astype(out_dtype)
            o_ref[:, lo + 128:lo + 256] = _rope(q_ref[:, lo + 128:lo + 256].astype(F32), c, s).astype(out_dtype)

    return pl.pallas_call(
        body, name=name, grid=(lp // BLK,),
        in_specs=[_row(PAIRS * 256), _row(128), _row(128)], out_specs=_row(PAIRS * 256),
        out_shape=jax.ShapeDtypeStruct((lp, PAIRS * 256), out_dtype),
        compiler_params=_cp(("parallel",)))(qraw, ctab, stab)


def _gate_fwd(o_mla, o_fox, gate):
    lp = o_mla.shape[0]

    def body(om_ref, of_ref, zm_ref, zf_ref, am_ref, af_ref):
        zm = zm_ref[...].astype(F32)
        am_ref[...] = (om_ref[...] * (zm * _sigmoid(zm))).astype(BF16)
        zf = zf_ref[...].astype(F32)
        af_ref[...] = (of_ref[...] * (zf * _sigmoid(zf))).astype(BF16)

    return pl.pallas_call(
        body, name="gate_fwd", grid=(lp // BLK,),
        in_specs=[_row(D_MODEL), _row(D_MODEL), _rowc(D_MODEL, 0), _rowc(D_MODEL, 1)],
        out_specs=[_row(D_MODEL), _row(D_MODEL)],
        out_shape=[jax.ShapeDtypeStruct((lp, D_MODEL), BF16)] * 2,
        compiler_params=_cp(("parallel",)))(o_mla, o_fox, gate, gate)


def _merge_fwd(gate, y_mla, y_fox):
    lp = y_mla.shape[0]

    def body(ga_ref, gb_ref, ym_ref, yf_ref, m_ref):
        sa = _sigmoid(ga_ref[...].astype(F32))
        sb = _sigmoid(gb_ref[...].astype(F32))
        m_ref[...] = (sa * ym_ref[...] + sb * yf_ref[...]).astype(BF16)

    return pl.pallas_call(
        body, name="merge_fwd", grid=(lp // BLK,),
        in_specs=[_rowc(D_MODEL, 2), _rowc(D_MODEL, 3), _row(D_MODEL), _row(D_MODEL)],
        out_specs=_row(D_MODEL), out_shape=jax.ShapeDtypeStruct((lp, D_MODEL), BF16),
        compiler_params=_cp(("parallel",)))(gate, gate, y_mla, y_fox)


def _tail(h, mixed, tgt, gpost):
    lp = h.shape[0]
    shift = pl.BlockSpec((BLK, D_MODEL), lambda i: (jnp.maximum(i - 1, 0), 0))

    def body(h_ref, mx_ref, t_ref, g_ref, dmx_ref, dy_ref, loss_ref, dg_ref):
        i = pl.program_id(0)

        @pl.when(i == 0)
        def _():
            loss_ref[...] = jnp.zeros_like(loss_ref)
            dg_ref[...] = jnp.zeros_like(dg_ref)
            dmx_ref[...] = jnp.zeros_like(dmx_ref)
            dy_ref[...] = jnp.zeros_like(dy_ref)

        @pl.when(i > 0)
        def _():
            mx = mx_ref[...]
            g = g_ref[...]
            r = lax.rsqrt(jnp.mean(mx * mx, axis=-1, keepdims=True) + RMS_EPS)
            nrm = mx * r
            e = (h_ref[...] + nrm * g) - t_ref[...]
            loss_ref[...] += jnp.sum(0.5 * jnp.sum(e * e, axis=-1, keepdims=True) * (1.0 / D_MODEL),
                                     axis=0, keepdims=True)
            dy = e * (1.0 / D_MODEL)
            dy_ref[...] = dy
            dg_ref[...] += jnp.sum(dy * nrm, axis=0, keepdims=True)
            w = dy * g
            dot = jnp.mean(w * mx, axis=-1, keepdims=True)
            dmx_ref[...] = (r * w - mx * (r * r * r * dot)).astype(BF16)

    return pl.pallas_call(
        body, name="tail", grid=(lp // BLK,),
        in_specs=[_row(D_MODEL), _row(D_MODEL), shift, _full((1, D_MODEL))],
        out_specs=[_row(D_MODEL), _row(D_MODEL), _full((1, 1)), _full((1, D_MODEL))],
        out_shape=[jax.ShapeDtypeStruct((lp, D_MODEL), BF16), jax.ShapeDtypeStruct((lp, D_MODEL), F32),
                   jax.ShapeDtypeStruct((1, 1), F32), jax.ShapeDtypeStruct((1, D_MODEL), F32)],
        compiler_params=_cp(("arbitrary",)))(h, mixed, tgt, gpost)


def _merge_bwd(dm, gate, y_mla, y_fox):
    lp = dm.shape[0]

    def body(dm_ref, ga_ref, gb_ref, ym_ref, yf_ref, dym_ref, dyf_ref, dg_ref):
        dm_v = dm_ref[...]
        sa = _sigmoid(ga_ref[...].astype(F32))
        sb = _sigmoid(gb_ref[...].astype(F32))
        dym_ref[...] = (dm_v * sa).astype(BF16)
        dyf_ref[...] = (dm_v * sb).astype(BF16)
        dg_ref[:, 0:D_MODEL] = (dm_v * ym_ref[...] * (sa * (1.0 - sa))).astype(BF16)
        dg_ref[:, D_MODEL:2 * D_MODEL] = (dm_v * yf_ref[...] * (sb * (1.0 - sb))).astype(BF16)

    return pl.pallas_call(
        body, name="merge_bwd", grid=(lp // BLK,),
        in_specs=[_row(D_MODEL), _rowc(D_MODEL, 2), _rowc(D_MODEL, 3), _row(D_MODEL), _row(D_MODEL)],
        out_specs=[_row(D_MODEL), _row(D_MODEL), _row(2 * D_MODEL)],
        out_shape=[jax.ShapeDtypeStruct((lp, D_MODEL), BF16), jax.ShapeDtypeStruct((lp, D_MODEL), BF16),
                   jax.ShapeDtypeStruct((lp, 2 * D_MODEL), BF16)],
        compiler_params=_cp(("parallel",)))(dm, gate, gate, y_mla, y_fox)


def _gate_bwd(da_mla, da_fox, o_mla, o_fox, gate):
    lp = da_mla.shape[0]

    def one(da, o, z):
        sg = _sigmoid(z)
        do = da * (z * sg)
        dz = da * o * (sg * (1.0 + z * (1.0 - sg)))
        return do.astype(BF16), dz.astype(BF16)

    def body(dam_ref, daf_ref, om_ref, of_ref, zm_ref, zf_ref, dom_ref, dof_ref, dz_ref):
        dom_ref[...], dz_ref[:, 0:D_MODEL] = one(dam_ref[...], om_ref[...], zm_ref[...].astype(F32))
        dof_ref[...], dz_ref[:, D_MODEL:2 * D_MODEL] = one(daf_ref[...], of_ref[...], zf_ref[...].astype(F32))

    return pl.pallas_call(
        body, name="gate_bwd", grid=(lp // BLK,),
        in_specs=[_row(D_MODEL)] * 4 + [_rowc(D_MODEL, 0), _rowc(D_MODEL, 1)],
        out_specs=[_row(D_MODEL), _row(D_MODEL), _row(2 * D_MODEL)],
        out_shape=[jax.ShapeDtypeStruct((lp, D_MODEL), BF16), jax.ShapeDtypeStruct((lp, D_MODEL), BF16),
                   jax.ShapeDtypeStruct((lp, 2 * D_MODEL), BF16)],
        compiler_params=_cp(("parallel",)))(da_mla, da_fox, o_mla, o_fox, gate, gate)


def _small_bwd(small, dqn, dkvn, dkr, dcol_t, drow_t, gq, gkv, fb, ctab, stab, triu):
    lp = small.shape[0]
    nb = lp // BLK

    def rrow(w):
        return pl.BlockSpec((BLK, w), lambda i: (nb - 1 - i, 0))

    def body(sm_ref, dqn_ref, dkvn_ref, dkr_ref, dcol_ref, drow_ref, gq_ref, gkv_ref, fb_ref, c_ref, s_ref, tri_ref,
             ds_ref, dgq_ref, dgkv_ref, dfb_ref, carry):
        i = pl.program_id(0)

        @pl.when(i == 0)
        def _():
            carry[...] = jnp.zeros_like(carry)
            dgq_ref[...] = jnp.zeros_like(dgq_ref)
            dgkv_ref[...] = jnp.zeros_like(dgkv_ref)
            dfb_ref[...] = jnp.zeros_like(dfb_ref)

        def norm_bwd(x, dn, g, dg_ref):
            r = lax.rsqrt(jnp.mean(x * x, axis=-1, keepdims=True) + RMS_EPS)
            dg_ref[...] += jnp.sum(dn * (x * r), axis=0, keepdims=True)
            w = dn * g
            dot = jnp.mean(w * x, axis=-1, keepdims=True)
            return r * w - x * (r * r * r * dot)

        ds_ref[:, 0:256] = norm_bwd(sm_ref[:, 0:256], dqn_ref[...], gq_ref[...], dgq_ref).astype(BF16)
        ds_ref[:, 256:384] = norm_bwd(sm_ref[:, 256:384], dkvn_ref[...], gkv_ref[...], dgkv_ref).astype(BF16)

        dk = dkr_ref[0]
        for p in range(1, PAIRS):
            dk = dk + dkr_ref[p]
        dk = _rope(dk, c_ref[...], -s_ref[...])
        lane = lax.broadcasted_iota(jnp.int32, dk.shape, 1)
        dk = jnp.where(lane < MLA_ROPE, dk + pltpu.roll(dk, LANES - MLA_ROPE, 1), 0.0)
        ds_ref[:, 384:512] = dk.astype(BF16)

        dcr = dcol_ref[...] - drow_ref[...]
        hi, mid, lo = _split3(dcr)
        t = tri_ref[...]
        suf = (_dot(t, hi, 1, 0) + _dot(t, mid, 1, 0)) + _dot(t, lo, 1, 0) + carry[...]
        fl = sm_ref[:, 512:640] + fb_ref[...]
        dfl = jnp.where(_row_valid(nb - 1 - i), -suf * _sigmoid(-fl), 0.0)
        ds_ref[:, 512:640] = dfl.astype(BF16)
        dfb_ref[...] += jnp.sum(dfl, axis=0, keepdims=True)
        carry[...] += jnp.sum(dcr, axis=0, keepdims=True)

    return pl.pallas_call(
        body, name="small_bwd", grid=(nb,),
        in_specs=[rrow(SMALL_W), rrow(256), rrow(128),
                  pl.BlockSpec((PAIRS, BLK, 128), lambda i: (0, nb - 1 - i, 0)), rrow(128), rrow(128),
                  _full((1, 256)), _full((1, 128)), _full((1, 128)), rrow(128), rrow(128), _full((BLK, BLK))],
        out_specs=[rrow(SMALL_W), _full((1, 256)), _full((1, 128)), _full((1, 128))],
        out_shape=[jax.ShapeDtypeStruct((lp, SMALL_W), BF16), jax.ShapeDtypeStruct((1, 256), F32),
                   jax.ShapeDtypeStruct((1, 128), F32), jax.ShapeDtypeStruct((1, 128), F32)],
        scratch_shapes=[pltpu.VMEM((1, 128), F32)],
        compiler_params=_cp(("arbitrary",)))(small, dqn, dkvn, dkr, dcol_t, drow_t, gq, gkv, fb, ctab, stab, triu)


def _pre_bwd(du, h, dy, gpre, s_rows):
    lp = h.shape[0]
    shift = pl.BlockSpec((BLK, D_MODEL), lambda i: (jnp.maximum(i - 1, 0), 0))

    def body(du_ref, h_ref, dy_ref, g_ref, dx_ref, dmeta_ref, dg_ref):
        i = pl.program_id(0)

        @pl.when(i == 0)
        def _():
            dg_ref[...] = jnp.zeros_like(dg_ref)

        hv = h_ref[...]
        duv = du_ref[...]
        r = lax.rsqrt(jnp.mean(hv * hv, axis=-1, keepdims=True) + RMS_EPS)
        dg_ref[...] += jnp.sum(duv * (hv * r), axis=0, keepdims=True)
        w = duv * g_ref[...]
        dot = jnp.mean(w * hv, axis=-1, keepdims=True)
        dh = dy_ref[...] + (r * w - hv * (r * r * r * dot))
        dx_ref[...] = dh

        @pl.when(i == 0)
        def _():
            dmeta_ref[...] = dh[0:N_META, :]

    return pl.pallas_call(
        body, name="pre_bwd", grid=(lp // BLK,),
        in_specs=[_row(D_MODEL), _row(D_MODEL), _row(D_MODEL), _full((1, D_MODEL))],
        out_specs=[shift, _full((N_META, D_MODEL)), _full((1, D_MODEL))],
        out_shape=[jax.ShapeDtypeStruct((s_rows, D_MODEL), F32), jax.ShapeDtypeStruct((N_META, D_MODEL), F32),
                   jax.ShapeDtypeStruct((1, D_MODEL), F32)],
        compiler_params=_cp(("arbitrary",)))(du, h, dy, gpre)


def _pair_masks():
    lane = lax.broadcasted_iota(jnp.int32, (1, LANES), 1)
    return ([lane < HEAD_DIM, lane >= HEAD_DIM],
            [lane < MLA_ROPE, (lane >= MLA_ROPE) & (lane < 2 * MLA_ROPE)])


def _mask2(x, masks):
    return [jnp.where(m, x, jnp.zeros_like(x)) for m in masks]


def _attn_fwd(q, k, v, *, kr=None, nbrep=None, scale, qcol, kcol, vcol, name):
    lp = q.shape[0]
    nb = lp // BLK
    rope = kr is not None
    bias = nbrep is not None
    qw = 256 if rope else 128

    def body(*refs):
        it = iter(refs)
        q_ref, k_ref, v_ref = next(it), next(it), next(it)
        kr_ref = next(it) if rope else None
        nb_ref = next(it) if bias else None
        o_ref, lse_ref = next(it), next(it)
        i = pl.program_id(1)
        mas, mbs = _pair_masks()
        qah = _mask2(q_ref[:, 0:128], mas)
        if bias:
            qah = [x * scale for x in qah]
        qbh = _mask2(q_ref[:, 128:256], mbs) if rope else None
        key_l = lax.broadcasted_iota(jnp.int32, (BLK, BLK), 0)
        qry_l = lax.broadcasted_iota(jnp.int32, (BLK, BLK), 1)
        meta_mask = (key_l < N_META) & ((key_l <= qry_l) | (i > 0))
        diag_mask = (key_l <= qry_l) & (i > 0)

        def scores(kc):
            k0 = pl.multiple_of(kc * BLK, BLK)
            kk = k_ref[pl.ds(k0, BLK), :]
            out = []
            for h in range(2):
                s = _dot(kk, qah[h], 1, 1)
                if rope:
                    s = (s + _dot(kr_ref[pl.ds(k0, BLK), :], qbh[h], 1, 1)) * scale
                out.append(s)
            return out

        def update(kc, ss, carry, mask):
            stats, acc = carry[:4], carry[4]
            k0 = pl.multiple_of(kc * BLK, BLK)
            vv = v_ref[pl.ds(k0, BLK), :]
            new_stats, alphas, ps = [], [], []
            for h in range(2):
                m_prev, l_prev = stats[2 * h], stats[2 * h + 1]
                s = ss[h]
                if bias:
                    nbc = nb_ref[h, pl.ds(k0, BLK), :]
                    s = s + jnp.concatenate([nbc, nbc], axis=1)
                if mask is not None:
                    s = jnp.where(mask, s, NEG)
                m_new = jnp.maximum(m_prev, jnp.max(s, axis=0, keepdims=True))
                alpha = jnp.exp(m_prev - m_new)
                p = jnp.exp(s - m_new)
                new_stats += [m_new, alpha * l_prev + jnp.sum(p, axis=0, keepdims=True)]
                alphas.append(alpha)
                ps.append(p.astype(BF16))
            vcat = jnp.concatenate(_mask2(vv, mas), axis=0)
            pv = _dot(vcat, jnp.concatenate(ps, axis=0), 0, 0)
            a_full = jnp.concatenate([jnp.broadcast_to(a, (HEAD_DIM, BLK)) for a in alphas], axis=0)
            return (*new_stats, a_full * acc + pv)

        neg = jnp.full((1, BLK), NEG, F32)
        zero = jnp.zeros((1, BLK), F32)
        s_cur = scores(0)
        s_nxt = scores(jnp.minimum(1, i))
        c = update(0, s_cur, (neg, zero, neg, zero, jnp.zeros((LANES, BLK), F32)), meta_mask)

        def step(kc, cr):
            nxt = scores(kc + 1)
            return (*nxt, *update(kc, cr[:2], cr[2:], None))

        cr = lax.fori_loop(1, i, step, (*s_nxt, *c))
        c = update(i, cr[:2], cr[2:], diag_mask)
        inv = jnp.concatenate([jnp.broadcast_to(1.0 / c[1], (HEAD_DIM, BLK)),
                               jnp.broadcast_to(1.0 / c[3], (HEAD_DIM, BLK))], axis=0)
        o_ref[...] = (c[4] * inv).T
        lse_ref[0, 0:1, :] = c[0] + jnp.log(c[1])
        lse_ref[0, 1:2, :] = c[2] + jnp.log(c[3])

    in_specs = [pl.BlockSpec((BLK, qw), lambda p, i: (i, qcol + p)),
                pl.BlockSpec((lp, 128), lambda p, i: (0, kcol(p))),
                pl.BlockSpec((lp, 128), lambda p, i: (0, vcol(p)))]
    ins = [q, k, v]
    if rope:
        in_specs.append(pl.BlockSpec((lp, 128), lambda p, i: (0, 0)))
        ins.append(kr)
    if bias:
        in_specs.append(pl.BlockSpec((2, lp, 128), lambda p, i: (p, 0, 0)))
        ins.append(nbrep)
    return pl.pallas_call(
        body, name=name, grid=(PAIRS, nb), in_specs=in_specs,
        out_specs=[pl.BlockSpec((BLK, 128), lambda p, i: (i, p)),
                   pl.BlockSpec((1, 2, BLK), lambda p, i: (p, 0, i))],
        out_shape=[jax.ShapeDtypeStruct((lp, D_MODEL), F32), jax.ShapeDtypeStruct((PAIRS, 2, lp), F32)],
        compiler_params=_cp(("parallel", "arbitrary"), VMEM_BIG))(*ins)


def _attn_bwd(q, k, v, do, o, lse, *, kr=None, nbrep=None, scale, qcol, kcol, vcol, name):
    lp = q.shape[0]
    nb = lp // BLK
    rope = kr is not None
    bias = nbrep is not None
    qw = 256 if rope else 128

    def body(*refs):
        it = iter(refs)
        q_ref, k_ref, v_ref = next(it), next(it), next(it)
        kr_ref = next(it) if rope else None
        nb_ref = next(it) if bias else None
        do_ref, o_ref, lse_ref = next(it), next(it), next(it)
        dq_ref, dk_ref, dv_ref = next(it), next(it), next(it)
        x_ref = next(it)
        drow_ref = next(it) if bias else None
        delta = next(it)
        kb = pl.program_id(1)
        mas, mbs = _pair_masks()
        lane = lax.broadcasted_iota(jnp.int32, (1, LANES), 1)

        @pl.when(kb == 0)
        def _():
            dq_ref[...] = jnp.zeros_like(dq_ref)
            if bias:
                drow_ref[...] = jnp.zeros_like(drow_ref)
            sub = lax.broadcasted_iota(jnp.int32, (8, LANES), 0)
            sel = (((sub == 0) & mas[0]) | ((sub == 1) & mas[1])).astype(BF16)

            def dstep(c, carry):
                r0 = pl.multiple_of(c * BLK, BLK)
                prod = do_ref[pl.ds(r0, BLK), :].astype(F32) * o_ref[pl.ds(r0, BLK), :]
                hi, mid, lo = _split3(prod)
                delta[:, pl.ds(r0, BLK)] = (_dot(sel, hi, 1, 1) + _dot(sel, mid, 1, 1)) + _dot(sel, lo, 1, 1)
                return carry

            lax.fori_loop(0, nb, dstep, 0)

        kk = k_ref[...]
        vh = _mask2(v_ref[...], mas)
        kcat = jnp.concatenate(_mask2(kk, mas), axis=0)
        if bias:
            kcat = kcat * scale
            nbc = [jnp.concatenate([nb_ref[h], nb_ref[h]], axis=1) for h in range(2)]
        if rope:
            krr = kr_ref[...]
            krcat = jnp.concatenate(_mask2(krr, mbs), axis=0)
        key_l = lax.broadcasted_iota(jnp.int32, (BLK, BLK), 0)
        qry_l = lax.broadcasted_iota(jnp.int32, (BLK, BLK), 1)
        diag_mask = (key_l <= qry_l) & ((kb > 0) | (key_l < N_META))
        meta_mask = key_l < N_META

        def chunk(qc, carry, mask):
            carry = list(carry)
            q0 = pl.multiple_of(qc * BLK, BLK)
            dov = do_ref[pl.ds(q0, BLK), :]
            doh = _mask2(dov, mas)
            qah = _mask2(q_ref[pl.ds(q0, BLK), 0:128], mas)
            if bias:
                qah = [x * scale for x in qah]
            qbh = _mask2(q_ref[pl.ds(q0, BLK), 128:256], mbs) if rope else None
            pbs, dss = [], []
            for h in range(2):
                s = _dot(kk, qah[h], 1, 1)
                if rope:
                    s = (s + _dot(krr, qbh[h], 1, 1)) * scale
                if bias:
                    s = s + nbc[h]
                p = jnp.exp(s - lse_ref[0, h:h + 1, pl.ds(q0, BLK)])
                if mask is not None:
                    p = jnp.where(mask, p, 0.0)
                dp = _dot(vh[h], dov, 1, 1)
                ds = p * (dp - delta[h:h + 1, pl.ds(q0, BLK)])
                if bias:
                    drow_ref[0, h:h + 1, pl.ds(q0, BLK)] += jnp.sum(ds, axis=0, keepdims=True)
                    carry[2 + h] = carry[2 + h] + jnp.sum(ds, axis=1, keepdims=True)
                else:
                    ds = ds * scale
                pbs.append(p.astype(BF16))
                dss.append(ds.astype(BF16))
            ds_lanes = jnp.concatenate(dss, axis=1)
            ds_rows = jnp.concatenate(dss, axis=0)
            carry[0] = carry[0] + _dot(ds_lanes, jnp.concatenate(qah, axis=0), 1, 0)
            carry[1] = carry[1] + _dot(jnp.concatenate(pbs, axis=1), jnp.concatenate(doh, axis=0), 1, 0)
            dq_ref[pl.ds(q0, BLK), 0:128] += _dot(ds_rows, kcat, 0, 0)
            if rope:
                carry[2] = carry[2] + _dot(ds_lanes, jnp.concatenate(qbh, axis=0), 1, 0)
                dq_ref[pl.ds(q0, BLK), 128:256] += _dot(ds_rows, krcat, 0, 0)
            return tuple(carry)

        init = [jnp.zeros((BLK, LANES), F32), jnp.zeros((BLK, LANES), F32)]
        if rope:
            init.append(jnp.zeros((BLK, LANES), F32))
        if bias:
            init += [jnp.zeros((BLK, 1), F32), jnp.zeros((BLK, 1), F32)]
        c = chunk(kb, tuple(init), diag_mask)
        first = kb == 0
        c = lax.fori_loop(kb + 1, jnp.where(first, nb, kb + 1), lambda qc, cr: chunk(qc, cr, meta_mask), c)
        c = lax.fori_loop(kb + 1, jnp.where(first, kb + 1, nb), lambda qc, cr: chunk(qc, cr, None), c)
        dk_ref[...] = c[0].astype(BF16)
        dv_ref[...] = c[1].astype(BF16)
        if rope:
            x_ref[0] = c[2]
        if bias:
            x_ref[0] = jnp.where(lane == 0, c[2], jnp.where(lane == 1, c[3], 0.0))

    in_specs = [pl.BlockSpec((lp, qw), lambda p, j: (0, qcol + p)),
                pl.BlockSpec((BLK, 128), lambda p, j: (j, kcol(p))),
                pl.BlockSpec((BLK, 128), lambda p, j: (j, vcol(p)))]
    ins = [q, k, v]
    if rope:
        in_specs.append(pl.BlockSpec((BLK, 128), lambda p, j: (j, 0)))
        ins.append(kr)
    if bias:
        in_specs.append(pl.BlockSpec((2, BLK, 128), lambda p, j: (p, j, 0)))
        ins.append(nbrep)
    in_specs += [pl.BlockSpec((lp, 128), lambda p, j: (0, p)), pl.BlockSpec((lp, 128), lambda p, j: (0, p)),
                 pl.BlockSpec((1, 2, lp), lambda p, j: (p, 0, 0))]
    ins += [do, o, lse]
    out_specs = [pl.BlockSpec((lp, qw), lambda p, j: (0, p)),
                 pl.BlockSpec((BLK, 128), lambda p, j: (j, p)),
                 pl.BlockSpec((BLK, 128), lambda p, j: (j, p)),
                 pl.BlockSpec((1, BLK, 128), lambda p, j: (p, j, 0))]
    out_shape = [jax.ShapeDtypeStruct((lp, PAIRS * qw), F32), jax.ShapeDtypeStruct((lp, D_MODEL), BF16),
                 jax.ShapeDtypeStruct((lp, D_MODEL), BF16), jax.ShapeDtypeStruct((PAIRS, lp, 128), F32)]
    if bias:
        out_specs.append(pl.BlockSpec((1, 2, lp), lambda p, j: (p, 0, 0)))
        out_shape.append(jax.ShapeDtypeStruct((PAIRS, 2, lp), F32))
    return pl.pallas_call(
        body, name=name, grid=(PAIRS, nb), in_specs=in_specs, out_specs=out_specs, out_shape=out_shape,
        scratch_shapes=[pltpu.VMEM((8, lp), F32)],
        compiler_params=_cp(("parallel", "arbitrary"), VMEM_BIG))(*ins)


def _adamw(w, g, m, v, name):
    rows, cols = w.shape
    tr = 128 if rows * cols > 512 * 1024 else rows

    def body(w_ref, g_ref, m_ref, v_ref, d_ref, nm_ref, nv_ref):
        gv = g_ref[...]
        nm = ADAM_B1 * m_ref[...] + (1.0 - ADAM_B1) * gv
        nv = ADAM_B2 * v_ref[...] + (1.0 - ADAM_B2) * (gv * gv)
        m_hat = nm / (1.0 - ADAM_B1 ** ADAM_STEP)
        v_hat = nv / (1.0 - ADAM_B2 ** ADAM_STEP)
        d_ref[...] = -ADAM_LR * (m_hat / (jnp.sqrt(v_hat) + ADAM_EPS) + ADAM_WD * w_ref[...])
        nm_ref[...] = nm
        nv_ref[...] = nv

    spec = pl.BlockSpec((tr, cols), lambda i: (i, 0))
    return pl.pallas_call(
        body, name=name, grid=(rows // tr,), in_specs=[spec] * 4, out_specs=[spec] * 3,
        out_shape=[jax.ShapeDtypeStruct((rows, cols), F32)] * 3,
        compiler_params=_cp(("parallel",), VMEM_BIG))(w, g, m, v)


def _add_cores(g, from_sib, name):
    n, rows, cols = g.shape
    half = rows // 2
    tr = _tile(half, (256, 240))
    nt = half // tr

    def body(lo_ref, hi_ref, s_ref, o_ref):
        mine = jnp.where(lax.axis_index("c") == 0, lo_ref[0], hi_ref[0])
        o_ref[0] = (mine + s_ref[0]).astype(BF16)

    return pl.pallas_call(
        body, name=name, grid=(n, nt),
        in_specs=[pl.BlockSpec((1, tr, cols), lambda j, i: (j, i, 0)),
                  pl.BlockSpec((1, tr, cols), lambda j, i: (j, nt + i, 0)),
                  pl.BlockSpec((1, tr, cols), lambda j, i: (j, i, 0))],
        out_specs=pl.BlockSpec((1, tr, cols), lambda j, i: (j, i, 0)),
        out_shape=jax.ShapeDtypeStruct((n, half, cols), BF16),
        compiler_params=_cp(("parallel", "parallel"), VMEM_BIG))(g, g, from_sib)


def _add_chips(x, name):
    n, rows, cols = x.shape
    tr = _tile(rows, (256, 240))

    def body(x_ref, o_ref):
        o_ref[...] = ((x_ref[0].astype(F32) + x_ref[1].astype(F32)) + x_ref[2].astype(F32)) + x_ref[3].astype(F32)

    return pl.pallas_call(
        body, name=name, grid=(rows // tr,),
        in_specs=[pl.BlockSpec((n, tr, cols), lambda i: (0, i, 0))],
        out_specs=pl.BlockSpec((tr, cols), lambda i: (i, 0)),
        out_shape=jax.ShapeDtypeStruct((rows, cols), F32), compiler_params=_cp(("parallel",), VMEM_BIG))(x)


def _axes():
    return lax.axis_index("x"), lax.axis_index("y"), lax.axis_index("c")


def _other_chips(x, y):
    return [(1 - x, y), (x, 1 - y), (1 - x, 1 - y)]


ANY = pl.BlockSpec(memory_space=pl.ANY)


def _rcopy(src, dst, send_sems, recv_sems, k, to):
    return pltpu.make_async_remote_copy(src_ref=src, dst_ref=dst, send_sem=send_sems.at[k], recv_sem=recv_sems.at[k],
                                        device_id=to, device_id_type=MESH)


def _gather_weights(shards, meta):
    n = len(shards)

    def body(*refs):
        srcs, meta_ref = refs[:n], refs[n]
        outs, mout_ref = refs[n + 1:2 * n + 1], refs[2 * n + 1]
        send_sems, recv_sems, local_sems = refs[2 * n + 2:]
        x, y, c = _axes()
        me = 2 * x + y
        sib = (x, y, 1 - c)
        chips = _other_chips(x, y)

        def half(t, chip_idx, cc):
            hr = shards[t].shape[0] // 2
            return outs[t].at[chip_idx, pl.ds(cc * hr, hr), :]

        local = [pltpu.make_async_copy(srcs[t], outs[t].at[me], local_sems.at[t]) for t in range(n)]
        local.append(pltpu.make_async_copy(meta_ref, mout_ref.at[me], local_sems.at[n]))
        for cp in local:
            cp.start()
        first = []
        for j, (px, py) in enumerate(chips):
            for t in range(n):
                hr = shards[t].shape[0] // 2
                first.append(_rcopy(srcs[t].at[pl.ds(c * hr, hr), :], half(t, me, c), send_sems, recv_sems,
                                    3 * t + j, (px, py, c)))
            first.append(_rcopy(meta_ref, mout_ref.at[me], send_sems, recv_sems, 3 * n + j, (px, py, c)))
        for cp in first:
            cp.start()
        passed = []
        for j, (px, py) in enumerate(chips):
            src_chip = 2 * px + py
            for t in range(n):
                _rcopy(half(t, src_chip, c), half(t, src_chip, c), send_sems, recv_sems, 3 * t + j, sib).wait_recv()
                fwd = _rcopy(half(t, src_chip, c), half(t, src_chip, c), send_sems, recv_sems, 3 * (n + 1 + t) + j, sib)
                fwd.start()
                passed.append(fwd)
            _rcopy(mout_ref.at[src_chip], mout_ref.at[src_chip], send_sems, recv_sems, 3 * n + j, sib).wait_recv()
        for j, (px, py) in enumerate(chips):
            src_chip = 2 * px + py
            for t in range(n):
                _rcopy(half(t, src_chip, 1 - c), half(t, src_chip, 1 - c), send_sems, recv_sems,
                       3 * (n + 1 + t) + j, sib).wait_recv()
        for cp in first + passed:
            cp.wait_send()
        for cp in local:
            cp.wait()

    nsem = 3 * (2 * n + 1)
    return pl.pallas_call(
        body, name="gather_weights", in_specs=[ANY] * (n + 1), out_specs=[ANY] * (n + 1),
        out_shape=[jax.ShapeDtypeStruct((N_CHIPS,) + s.shape, s.dtype) for s in shards]
        + [jax.ShapeDtypeStruct((N_CHIPS,) + meta.shape, meta.dtype)],
        scratch_shapes=[pltpu.SemaphoreType.DMA((nsem,)), pltpu.SemaphoreType.DMA((nsem,)),
                        pltpu.SemaphoreType.DMA((n + 1,))])(*shards, meta)


def _swap_halves(gs):
    n = len(gs)

    def body(*refs):
        srcs, outs = refs[:n], refs[n:2 * n]
        send_sems, recv_sems = refs[2 * n:]
        x, y, c = _axes()
        cps = []
        for t in range(n):
            hr = gs[t].shape[1] // 2
            for j in range(N_CHIPS):
                cps.append(_rcopy(srcs[t].at[j, pl.ds((1 - c) * hr, hr), :], outs[t].at[j], send_sems, recv_sems,
                                  N_CHIPS * t + j, (x, y, 1 - c)))
        for cp in cps:
            cp.start()
        for cp in cps:
            cp.wait()

    return pl.pallas_call(
        body, name="swap_halves", in_specs=[ANY] * n, out_specs=[ANY] * n,
        out_shape=[jax.ShapeDtypeStruct((N_CHIPS, g.shape[1] // 2, g.shape[2]), g.dtype) for g in gs],
        scratch_shapes=[pltpu.SemaphoreType.DMA((N_CHIPS * n,)), pltpu.SemaphoreType.DMA((N_CHIPS * n,))])(*gs)


def _scatter_chips(parts):
    n = len(parts)

    def body(*refs):
        srcs, outs = refs[:n], refs[n:2 * n]
        send_sems, recv_sems, local_sems = refs[2 * n:]
        x, y, c = _axes()
        me = 2 * x + y
        local = [pltpu.make_async_copy(srcs[t].at[me], outs[t].at[me], local_sems.at[t]) for t in range(n)]
        for cp in local:
            cp.start()
        cps = []
        for j, (px, py) in enumerate(_other_chips(x, y)):
            for t in range(n):
                cps.append(_rcopy(srcs[t].at[2 * px + py], outs[t].at[me], send_sems, recv_sems, 3 * t + j,
                                  (px, py, c)))
        for cp in cps:
            cp.start()
        for cp in cps:
            cp.wait()
        for cp in local:
            cp.wait()

    return pl.pallas_call(
        body, name="scatter_chips", in_specs=[ANY] * n, out_specs=[ANY] * n,
        out_shape=[jax.ShapeDtypeStruct(p.shape, p.dtype) for p in parts],
        scratch_shapes=[pltpu.SemaphoreType.DMA((3 * n,)), pltpu.SemaphoreType.DMA((3 * n,)),
                        pltpu.SemaphoreType.DMA((n,))])(*parts)


def _join_halves(rs):
    n = len(rs)

    def body(*refs):
        srcs, outs = refs[:n], refs[n:2 * n]
        send_sems, recv_sems, local_sems = refs[2 * n:]
        x, y, c = _axes()
        local, cps, waits = [], [], []
        for t in range(n):
            hr = rs[t].shape[0]
            mine = outs[t].at[pl.ds(c * hr, hr), :]
            other = outs[t].at[pl.ds((1 - c) * hr, hr), :]
            local.append(pltpu.make_async_copy(srcs[t], mine, local_sems.at[t]))
            cps.append(_rcopy(srcs[t], mine, send_sems, recv_sems, t, (x, y, 1 - c)))
            waits.append(_rcopy(srcs[t], other, send_sems, recv_sems, t, (x, y, 1 - c)))
        for cp in local + cps:
            cp.start()
        for cp in cps:
            cp.wait_send()
        for cp in waits:
            cp.wait_recv()
        for cp in local:
            cp.wait()

    return pl.pallas_call(
        body, name="join_halves", in_specs=[ANY] * n, out_specs=[ANY] * n,
        out_shape=[jax.ShapeDtypeStruct((2 * r.shape[0], r.shape[1]), r.dtype) for r in rs],
        scratch_shapes=[pltpu.SemaphoreType.DMA((n,)), pltpu.SemaphoreType.DMA((n,)),
                        pltpu.SemaphoreType.DMA((n,))])(*rs)


SMALL_ROWS = 24


def _allreduce_small(vec):
    def body(v_ref, out_ref, slots, send_sems, recv_sems):
        x, y, c = _axes()
        me = 4 * x + 2 * y + c
        slots[me] = v_ref[...]
        cps = []
        for k in range(1, 8):
            kx, ky, kc = (k >> 2) & 1, (k >> 1) & 1, k & 1
            peer = (1 - x if kx else x, 1 - y if ky else y, 1 - c if kc else c)
            cps.append(_rcopy(v_ref, slots.at[me], send_sems, recv_sems, k - 1, peer))
        for cp in cps:
            cp.start()
        for cp in cps:
            cp.wait()
        tot = slots[0]
        for k in range(1, 8):
            tot = tot + slots[k]
        out_ref[...] = tot

    return pl.pallas_call(
        body, name="allreduce_small",
        in_specs=[pl.BlockSpec(memory_space=pltpu.VMEM)], out_specs=pl.BlockSpec(memory_space=pltpu.VMEM),
        out_shape=jax.ShapeDtypeStruct((SMALL_ROWS, 128), F32),
        scratch_shapes=[pltpu.VMEM((8, SMALL_ROWS, 128), F32), pltpu.SemaphoreType.DMA((7,)),
                        pltpu.SemaphoreType.DMA((7,))])(vec)


def _pack_p2(w_uq, w_ukv, w_br_mla, w_br_fox, w_out, meta, dtype):
    parts = [w_uq.reshape(96, D_MODEL), w_ukv.reshape(64, D_MODEL), w_br_mla, w_br_fox, w_out,
             meta.reshape(4, D_MODEL), jnp.zeros((P2_ROWS - 932, D_MODEL), meta.dtype)]
    return jnp.concatenate([p.astype(dtype) for p in parts], axis=0)


def _unpack_p2(pk):
    return (pk[0:96].reshape(256, 384), pk[96:160].reshape(128, 512), pk[160:416], pk[416:672], pk[672:928],
            pk[928:932].reshape(N_META, 256))


def _uq_arrange(w):
    w3 = w.reshape(256, HEADS, 96)
    nope = w3[:, :, :64].reshape(256, PAIRS, 128)
    pe = w3[:, :, 64:].reshape(256, PAIRS, 64)
    return jnp.concatenate([nope, pe, jnp.zeros((256, PAIRS, 64), w.dtype)], axis=2).reshape(256, PAIRS * 256)


def _uq_restore(g):
    g3 = g.reshape(256, PAIRS, 256)
    nope = g3[:, :, :128].reshape(256, HEADS, 64)
    pe = g3[:, :, 128:192].reshape(256, HEADS, 32)
    return jnp.concatenate([nope, pe], axis=2).reshape(256, HEADS * 96)


def _ukv_arrange(w):
    w3 = w.reshape(128, HEADS, 128)
    return jnp.concatenate([w3[:, :, :64].reshape(128, 1024), w3[:, :, 64:].reshape(128, 1024)], axis=1)


def _ukv_restore(g):
    kn = g[:, :1024].reshape(128, HEADS, 64)
    vv = g[:, 1024:].reshape(128, HEADS, 64)
    return jnp.concatenate([kn, vv], axis=2).reshape(128, HEADS * 128)


def _rope_tables(lp):
    r = jnp.arange(lp)
    pos = jnp.where(r < N_META, r, jnp.where(r >= PAD, r - PAD + N_META, 0))
    half = MLA_ROPE // 2
    inv_freq = ROPE_THETA ** (-jnp.arange(half, dtype=F32) / half)
    ang = pos.astype(F32)[:, None] * inv_freq[None, :]
    cos, sin = jnp.cos(ang), jnp.sin(ang)
    one, zero = jnp.ones((lp, 64), F32), jnp.zeros((lp, 64), F32)
    return (jnp.concatenate([cos, cos, cos, cos, one], axis=1),
            jnp.concatenate([-sin, sin, -sin, sin, zero], axis=1))


def _pad_lanes(v, n=128):
    return jnp.pad(v, ((0, 0), (0, n - v.shape[1])))


def _in_cols(slabs, a, b):
    out = []
    for j in range(N_CHIPS):
        lo, hi = max(a, W_IN_SHARD * j), min(b, W_IN_SHARD * (j + 1))
        if lo < hi:
            out.append(slabs[j][:, lo - W_IN_SHARD * j:hi - W_IN_SHARD * j])
    return out


def _local_step(x2, tgt2, meta_f, w_small, w_attn, w_gate, w_uq_f, w_ukv_f, w_bm, w_bf, w_o, pre_norm_g,
                post_norm_g, mla_q_norm_g, mla_kv_norm_g, fox_forget_b):
    s_rows = x2.shape[0]
    lp = PAD + s_rows
    w_uq_a = _uq_arrange(w_uq_f)
    w_ukv_a = _ukv_arrange(w_ukv_f)

    ctab, stab = _rope_tables(lp)
    ii = jnp.arange(BLK)
    tri_lo = (ii[:, None] >= ii[None, :]).astype(BF16)
    tri_up = (ii[:, None] <= ii[None, :]).astype(BF16)
    fb128 = _pad_lanes(fox_forget_b)

    h = jnp.concatenate([meta_f, jnp.zeros((PAD - N_META, D_MODEL), F32), x2], axis=0)
    u = _rms_pre(h, pre_norm_g)
    small = _mm(u, w_small, mode="nn", out_dtype=F32, name="proj_small")
    attn = _mm(u, w_attn, mode="nn", out_dtype=BF16, name="proj_attn")
    gate = _mm(u, w_gate, mode="nn", out_dtype=BF16, name="proj_gate")
    qn, kvn, kr, ncum = _small_prep(small, mla_q_norm_g, mla_kv_norm_g, fb128, ctab, stab, tri_lo)
    qraw = _mm(qn, w_uq_a, mode="nn", out_dtype=F32, name="mla_q")
    qcat = _rope_q(qraw, ctab, stab, inverse=False, out_dtype=BF16, name="rope_q")
    kv = _mm(kvn, w_ukv_a, mode="nn", out_dtype=BF16, name="mla_kv")
    nbrep = jnp.broadcast_to(ncum[:, :HEADS].T[:, :, None], (HEADS, lp, LANES))

    mla_cols = dict(qcol=0, kcol=lambda p: p, vcol=lambda p: PAIRS + p)
    fox_cols = dict(qcol=0, kcol=lambda p: PAIRS + p, vcol=lambda p: 2 * PAIRS + p)
    o_mla, lse_mla = _attn_fwd(qcat, kv, kv, kr=kr, scale=MLA_SCALE, name="mla_fwd", **mla_cols)
    o_fox, lse_fox = _attn_fwd(attn, attn, attn, nbrep=nbrep, scale=FOX_SCALE, name="fox_fwd", **fox_cols)

    a_mla, a_fox = _gate_fwd(o_mla, o_fox, gate)
    y_mla = _mm(a_mla, w_bm, mode="nn", out_dtype=F32, name="br_mla")
    y_fox = _mm(a_fox, w_bf, mode="nn", out_dtype=F32, name="br_fox")
    mg = _merge_fwd(gate, y_mla, y_fox)
    mixed = _mm(mg, w_o, mode="nn", out_dtype=F32, name="out_proj")
    dmixed, dy, loss_p, dg_post = _tail(h, mixed, tgt2, post_norm_g)

    d_w_out = _mm(mg, dmixed, mode="tn", out_dtype=F32, name="d_w_out")
    dm = _mm(dmixed, w_o, mode="nt", out_dtype=F32, name="d_merge")
    dy_mla, dy_fox, dgate_ab = _merge_bwd(dm, gate, y_mla, y_fox)
    d_w_bm = _mm(a_mla, dy_mla, mode="tn", out_dtype=F32, name="d_w_br_mla")
    d_w_bf = _mm(a_fox, dy_fox, mode="tn", out_dtype=F32, name="d_w_br_fox")
    da_mla = _mm(dy_mla, w_bm, mode="nt", out_dtype=F32, name="d_a_mla")
    da_fox = _mm(dy_fox, w_bf, mode="nt", out_dtype=F32, name="d_a_fox")
    do_mla, do_fox, dgate_z = _gate_bwd(da_mla, da_fox, o_mla, o_fox, gate)

    dqcat, dkn, dvm, dkr = _attn_bwd(qcat, kv, kv, do_mla, o_mla, lse_mla, kr=kr, scale=MLA_SCALE,
                                     name="mla_bwd", **mla_cols)
    dfq, dfk, dfv, dcol, drow = _attn_bwd(attn, attn, attn, do_fox, o_fox, lse_fox, nbrep=nbrep, scale=FOX_SCALE,
                                          name="fox_bwd", **fox_cols)

    dq_a = _rope_q(dqcat, ctab, stab, inverse=True, out_dtype=BF16, name="rope_q_bwd")
    d_w_uq_a = _mm(qn, dq_a, mode="tn", out_dtype=F32, name="d_w_uq")
    dqn = _mm(dq_a, w_uq_a, mode="nt", out_dtype=F32, name="d_qn")
    d_w_ukv_a = jnp.concatenate([_mm(kvn, dkn, mode="tn", out_dtype=F32, name="d_w_uk"),
                                 _mm(kvn, dvm, mode="tn", out_dtype=F32, name="d_w_uv")], axis=1)
    dkvn = _mm(dkn, w_ukv_a[:, :1024], mode="nt", out_dtype=F32, name="d_kvn_k")
    dkvn = _mm(dvm, w_ukv_a[:, 1024:], mode="nt", out_dtype=F32, name="d_kvn_v", acc=dkvn)
    dcol_t = _pad_lanes(jnp.transpose(dcol[:, :, 0:2], (1, 0, 2)).reshape(lp, HEADS))
    drow_t = _pad_lanes(drow.reshape(HEADS, lp).T)
    dsmall, dg_q, dg_kv, dfb = _small_bwd(small, dqn, dkvn, dkr, dcol_t, drow_t, mla_q_norm_g, mla_kv_norm_g,
                                          fb128, ctab, stab, tri_up)

    dw_small = _mm(u, dsmall, mode="tn", out_dtype=F32, name="d_w_small")
    dw_fq = _mm(u, dfq, mode="tn", out_dtype=F32, name="d_w_fq")
    dw_fk = _mm(u, dfk, mode="tn", out_dtype=F32, name="d_w_fk")
    dw_fv = _mm(u, dfv, mode="tn", out_dtype=F32, name="d_w_fv")
    dw_z = _mm(u, dgate_z, mode="tn", out_dtype=F32, name="d_w_z")
    dw_g = _mm(u, dgate_ab, mode="tn", out_dtype=F32, name="d_w_g")
    du = _mm(dsmall, w_small, mode="nt", out_dtype=F32, name="d_u_small")
    du = _mm(dfq, w_attn[:, 0:1024], mode="nt", out_dtype=F32, name="d_u_fq", acc=du)
    du = _mm(dfk, w_attn[:, 1024:2048], mode="nt", out_dtype=F32, name="d_u_fk", acc=du)
    du = _mm(dfv, w_attn[:, 2048:3072], mode="nt", out_dtype=F32, name="d_u_fv", acc=du)
    du = _mm(dgate_z, w_gate[:, 0:2048], mode="nt", out_dtype=F32, name="d_u_z", acc=du)
    du = _mm(dgate_ab, w_gate[:, 2048:4096], mode="nt", out_dtype=F32, name="d_u_g", acc=du)
    dx, dmeta, dg_pre = _pre_bwd(du, h, dy, pre_norm_g, s_rows)

    d_w_in = jnp.concatenate([dw_small[:, 0:416], dw_z[:, 0:1024], dw_fq, dw_fk, dw_fv, dw_small[:, 512:528],
                              dw_z[:, 1024:2048], dw_g], axis=1)
    d_w_uq = _uq_restore(d_w_uq_a)
    d_w_ukv = _ukv_restore(d_w_ukv_a)
    return (loss_p, dx, dmeta, d_w_in, d_w_uq, d_w_ukv, d_w_bm, d_w_bf, d_w_out, dg_pre, dg_post, dg_q, dg_kv, dfb)


def kernel(x, meta_tokens, pre_norm_g, w_in, fox_forget_b, mla_q_norm_g, mla_kv_norm_g, w_uq, w_ukv, w_br_mla, w_br_fox, w_out, post_norm_g, loss_target, m_meta_tokens, m_pre_norm_g, m_w_in, m_fox_forget_b, m_mla_q_norm_g, m_mla_kv_norm_g, m_w_uq, m_w_ukv, m_w_br_mla, m_w_br_fox, m_w_out, m_post_norm_g, v_meta_tokens, v_pre_norm_g, v_w_in, v_fox_forget_b, v_mla_q_norm_g, v_mla_kv_norm_g, v_w_uq, v_w_ukv, v_w_br_mla, v_w_br_fox, v_w_out, v_post_norm_g):
    p2 = _pack_p2(w_uq[0], w_ukv[0], w_br_mla[0], w_br_fox[0], w_out[0], jnp.zeros((N_META, 256), F32), BF16)
    w_in_g, p2_g, meta_g = _gather_weights([w_in[0].astype(BF16), p2], meta_tokens)
    slabs = [w_in_g[j] for j in range(N_CHIPS)]
    pieces = [_unpack_p2(p2_g[j]) for j in range(N_CHIPS)]
    w_uq_f = jnp.concatenate([p[0] for p in pieces], axis=1)
    w_ukv_f = jnp.concatenate([p[1] for p in pieces], axis=1)
    w_bm = jnp.concatenate([p[2] for p in pieces], axis=0)
    w_bf = jnp.concatenate([p[3] for p in pieces], axis=0)
    w_o = jnp.concatenate([p[4] for p in pieces], axis=0)
    meta_f = jnp.concatenate([meta_g[j] for j in range(N_CHIPS)], axis=1)
    kpe = _in_cols(slabs, C_KPE, C_ZMLA)
    w_small = jnp.concatenate(_in_cols(slabs, C_CQ, C_KPE) + kpe + kpe + [jnp.zeros((D_MODEL, 64), BF16)]
                              + _in_cols(slabs, C_FL, C_ZFOX) + [jnp.zeros((D_MODEL, 112), BF16)], axis=1)
    w_attn = jnp.concatenate(_in_cols(slabs, C_FQ, C_FL), axis=1)
    w_gate = jnp.concatenate(_in_cols(slabs, C_ZMLA, C_FQ) + _in_cols(slabs, C_ZFOX, C_END), axis=1)

    (loss_p, dx, dmeta, d_w_in, d_w_uq, d_w_ukv, d_w_bm, d_w_bf, d_w_out, dg_pre, dg_post, dg_q, dg_kv,
     dfb) = _local_step(x[0], loss_target[0], meta_f, w_small, w_attn, w_gate, w_uq_f, w_ukv_f, w_bm, w_bf, w_o,
                        pre_norm_g, post_norm_g, mla_q_norm_g, mla_kv_norm_g, fox_forget_b)

    g1 = jnp.transpose(d_w_in.reshape(D_MODEL, N_CHIPS, W_IN_SHARD), (1, 0, 2))
    g2 = jnp.stack([_pack_p2(d_w_uq[:, 384 * j:384 * (j + 1)], d_w_ukv[:, 512 * j:512 * (j + 1)],
                             d_w_bm[256 * j:256 * (j + 1)], d_w_bf[256 * j:256 * (j + 1)],
                             d_w_out[256 * j:256 * (j + 1)], dmeta[:, 256 * j:256 * (j + 1)], F32)
                    for j in range(N_CHIPS)], axis=0)
    s1, s2 = _swap_halves([g1, g2])
    landed = _scatter_chips([_add_cores(g1, s1, "add_cores_w_in"), _add_cores(g2, s2, "add_cores_rest")])
    g_w_in, g_p2 = _join_halves([_add_chips(landed[0], "add_chips_w_in"), _add_chips(landed[1], "add_chips_rest")])
    g_w_uq, g_w_ukv, g_w_bm, g_w_bf, g_w_out, g_meta = _unpack_p2(g_p2)

    vec = jnp.concatenate([dg_pre.reshape(8, 128), dg_post.reshape(8, 128), dg_q.reshape(2, 128), dg_kv,
                           dfb, _pad_lanes(loss_p), jnp.zeros((3, 128), F32)], axis=0)
    tot = _allreduce_small(vec)
    loss = tot[20, 0]

    def small_pack(pre, post, gq_, gkv_, fb_):
        return jnp.concatenate([pre.reshape(8, 128), post.reshape(8, 128), gq_.reshape(2, 128), gkv_,
                                _pad_lanes(fb_), jnp.zeros((4, 128), F32)], axis=0)

    def small_unpack(t):
        return (t[0:8].reshape(1, 1024), t[8:16].reshape(1, 1024), t[16:18].reshape(1, 256), t[18:19],
                t[19:20, 0:HEADS])

    g_small = jnp.concatenate([tot[0:20], jnp.zeros((4, 128), F32)], axis=0)
    sm = _adamw(small_pack(pre_norm_g, post_norm_g, mla_q_norm_g, mla_kv_norm_g, fox_forget_b), g_small,
                small_pack(m_pre_norm_g, m_post_norm_g, m_mla_q_norm_g, m_mla_kv_norm_g, m_fox_forget_b),
                small_pack(v_pre_norm_g, v_post_norm_g, v_mla_q_norm_g, v_mla_kv_norm_g, v_fox_forget_b),
                "adamw_small")
    g_pre, g_post, g_q, g_kv, g_fb = small_unpack(g_small)
    (d_pre, d_post, d_q, d_kv, d_fb), (nm_pre, nm_post, nm_q, nm_kv, nm_fb), (nv_pre, nv_post, nv_q, nv_kv, nv_fb) = (
        small_unpack(t) for t in sm)

    d_meta, nm_meta, nv_meta = _adamw(meta_tokens, g_meta, m_meta_tokens, v_meta_tokens, "adamw_meta")
    d_win, nm_win, nv_win = _adamw(w_in[0], g_w_in, m_w_in[0], v_w_in[0], "adamw_w_in")
    d_wuq, nm_wuq, nv_wuq = _adamw(w_uq[0], g_w_uq, m_w_uq[0], v_w_uq[0], "adamw_w_uq")
    d_wukv, nm_wukv, nv_wukv = _adamw(w_ukv[0], g_w_ukv, m_w_ukv[0], v_w_ukv[0], "adamw_w_ukv")
    d_wbm, nm_wbm, nv_wbm = _adamw(w_br_mla[0], g_w_bm, m_w_br_mla[0], v_w_br_mla[0], "adamw_w_br_mla")
    d_wbf, nm_wbf, nv_wbf = _adamw(w_br_fox[0], g_w_bf, m_w_br_fox[0], v_w_br_fox[0], "adamw_w_br_fox")
    d_wo, nm_wo, nv_wo = _adamw(w_out[0], g_w_out, m_w_out[0], v_w_out[0], "adamw_w_out")

    def group(meta_, pre, win, fb_, q_, kv_, wuq, wukv, wbm, wbf, wo, post):
        return (meta_, pre, win[None], fb_, q_, kv_, wuq[None], wukv[None], wbm[None], wbf[None], wo[None], post)

    grads = group(g_meta, g_pre, g_w_in, g_fb, g_q, g_kv, g_w_uq, g_w_ukv, g_w_bm, g_w_bf, g_w_out, g_post)
    deltas = group(d_meta, d_pre, d_win, d_fb, d_q, d_kv, d_wuq, d_wukv, d_wbm, d_wbf, d_wo, d_post)
    new_m = group(nm_meta, nm_pre, nm_win, nm_fb, nm_q, nm_kv, nm_wuq, nm_wukv, nm_wbm, nm_wbf, nm_wo, nm_post)
    new_v = group(nv_meta, nv_pre, nv_win, nv_fb, nv_q, nv_kv, nv_wuq, nv_wukv, nv_wbm, nv_wbf, nv_wo, nv_post)
    return (loss, dx[None], *grads, *deltas, *new_m, *new_v)
```

```python
import math

import jax
import jax.numpy as jnp
from jax import lax
from jax.experimental import pallas as pl
from jax.experimental.pallas import tpu as pltpu

F32 = jnp.float32
BF16 = jnp.bfloat16

D_MODEL = 1024
N_META = 16
RMS_EPS = 1e-6
HEADS = 16
PAIRS = HEADS // 2
HEAD_DIM = 64
LANES = 128
MLA_ROPE = 32
MLA_SCALE = 1.0 / math.sqrt(64 + 32)
FOX_SCALE = 1.0 / math.sqrt(64)
ROPE_THETA = 10000.0

PAD = 256
BLK = 256
NEG = -1e30

C_CQ, C_CKV, C_KPE, C_ZMLA, C_FQ, C_FK, C_FV, C_FL, C_ZFOX, C_GA, C_GB, C_END = (
    0, 256, 384, 416, 1440, 2464, 3488, 4512, 4528, 5552, 6576, 7600)
SMALL_W = 640
W_IN_SHARD = 1900

P2_ROWS = 960
N_CHIPS = 4

ADAM_LR = 0.001
ADAM_B1 = 0.9
ADAM_B2 = 0.999
ADAM_EPS = 1e-08
ADAM_WD = 0.01
ADAM_STEP = 10

VMEM_BIG = 56 * 1024 * 1024
MESH = pl.DeviceIdType.MESH


def _cp(dims, vmem=None):
    return pltpu.CompilerParams(dimension_semantics=dims, vmem_limit_bytes=vmem)


def _dot(a, b, ca, cb):
    return lax.dot_general(a, b, (((ca,), (cb,)), ((), ())), preferred_element_type=F32)


def _sigmoid(x):
    return 1.0 / (1.0 + jnp.exp(-x))


def _tile(n, cands):
    for c in cands:
        if n % c == 0:
            return c
    return n


def _mm(a, b, *, mode, out_dtype, name, acc=None):
    if mode == "nn":
        (M, K), N = a.shape, b.shape[1]
    elif mode == "nt":
        (M, K), N = a.shape, b.shape[0]
    else:
        (K, M), N = a.shape, b.shape[1]
    tm = _tile(M, (1088, 1024)) if M > 1024 else M
    tn = _tile(N, (1024,)) if N > 1024 else N
    tk = _tile(K, (1088, 1024)) if K > 1088 else K
    nk = K // tk
    ca, cb = {"nn": (1, 0), "nt": (1, 1), "tn": (0, 0)}[mode]
    a_spec = (pl.BlockSpec((tk, tm), lambda j, i, k: (k, i)) if mode == "tn"
              else pl.BlockSpec((tm, tk), lambda j, i, k: (i, k)))
    b_spec = (pl.BlockSpec((tn, tk), lambda j, i, k: (j, k)) if mode == "nt"
              else pl.BlockSpec((tk, tn), lambda j, i, k: (k, j)))
    o_spec = pl.BlockSpec((tm, tn), lambda j, i, k: (i, j))
    has_acc = acc is not None

    def body(*refs):
        a_ref, b_ref = refs[0], refs[1]
        acc_ref = refs[2] if has_acc else None
        o_ref = refs[3] if has_acc else refs[2]
        part = _dot(a_ref[...].astype(BF16), b_ref[...].astype(BF16), ca, cb)
        if nk == 1:
            if has_acc:
                part = part + acc_ref[...]
            o_ref[...] = part.astype(out_dtype)
        else:
            sc = refs[-1]
            k = pl.program_id(2)

            @pl.when(k == 0)
            def _():
                sc[...] = part + acc_ref[...] if has_acc else part

            @pl.when(k > 0)
            def _():
                sc[...] += part

            @pl.when(k == nk - 1)
            def _():
                o_ref[...] = sc[...].astype(out_dtype)

    ins = [a, b] + ([acc] if has_acc else [])
    in_specs = [a_spec, b_spec] + ([o_spec] if has_acc else [])
    return pl.pallas_call(
        body, name=name, grid=(N // tn, M // tm, nk), in_specs=in_specs, out_specs=o_spec,
        out_shape=jax.ShapeDtypeStruct((M, N), out_dtype),
        scratch_shapes=[pltpu.VMEM((tm, tn), F32)] if nk > 1 else [],
        compiler_params=_cp(("parallel", "parallel", "arbitrary"), VMEM_BIG))(*ins)


def _row(w):
    return pl.BlockSpec((BLK, w), lambda i: (i, 0))


def _rowc(w, c):
    return pl.BlockSpec((BLK, w), lambda i: (i, c))


def _full(shape):
    return pl.BlockSpec(shape, lambda i: tuple(0 for _ in shape))


def _rope(x, c, s):
    lane = lax.broadcasted_iota(jnp.int32, x.shape, 1)
    is_x1 = ((lane >> 4) & 1) == 0
    partner = jnp.where(is_x1, pltpu.roll(x, LANES - 16, 1), pltpu.roll(x, 16, 1))
    return x * c + partner * s


def _row_valid(i):
    rows = i * BLK + lax.broadcasted_iota(jnp.int32, (BLK, 1), 0)
    return (rows < N_META) | (rows >= PAD)


def _rms_pre(h, g):
    lp = h.shape[0]

    def body(h_ref, g_ref, u_ref):
        hv = h_ref[...]
        r = lax.rsqrt(jnp.mean(hv * hv, axis=-1, keepdims=True) + RMS_EPS)
        u_ref[...] = (hv * r * g_ref[...]).astype(BF16)

    return pl.pallas_call(
        body, name="rms_pre", grid=(lp // BLK,),
        in_specs=[_row(D_MODEL), _full((1, D_MODEL))], out_specs=_row(D_MODEL),
        out_shape=jax.ShapeDtypeStruct((lp, D_MODEL), BF16),
        compiler_params=_cp(("parallel",)))(h, g)


def _split3(x):
    hi = x.astype(BF16)
    r1 = x - hi.astype(F32)
    mid = r1.astype(BF16)
    lo = (r1 - mid.astype(F32)).astype(BF16)
    return hi, mid, lo


def _small_prep(small, gq, gkv, fb, ctab, stab, tri):
    lp = small.shape[0]

    def body(sm_ref, gq_ref, gkv_ref, fb_ref, c_ref, s_ref, tri_ref, qn_ref, kvn_ref, kr_ref, ncum_ref, carry):
        i = pl.program_id(0)

        @pl.when(i == 0)
        def _():
            carry[...] = jnp.zeros_like(carry)

        cq = sm_ref[:, 0:256]
        r = lax.rsqrt(jnp.mean(cq * cq, axis=-1, keepdims=True) + RMS_EPS)
        qn_ref[...] = (cq * r * gq_ref[...]).astype(BF16)
        ckv = sm_ref[:, 256:384]
        r = lax.rsqrt(jnp.mean(ckv * ckv, axis=-1, keepdims=True) + RMS_EPS)
        kvn_ref[...] = (ckv * r * gkv_ref[...]).astype(BF16)
        kr_ref[...] = _rope(sm_ref[:, 384:512], c_ref[...], s_ref[...]).astype(BF16)
        fl = sm_ref[:, 512:640] + fb_ref[...]
        lf = jnp.minimum(fl, 0.0) - jnp.log(1.0 + jnp.exp(-jnp.abs(fl)))
        lf = jnp.where(_row_valid(i), lf, 0.0)
        hi, mid, lo = _split3(lf)
        t = tri_ref[...]
        cum = (_dot(t, hi, 1, 0) + _dot(t, mid, 1, 0)) + _dot(t, lo, 1, 0) + carry[...]
        ncum_ref[...] = -cum
        carry[...] = -ncum_ref[BLK - 1:BLK, :]

    return pl.pallas_call(
        body, name="small_prep", grid=(lp // BLK,),
        in_specs=[_row(SMALL_W), _full((1, 256)), _full((1, 128)), _full((1, 128)), _row(128), _row(128),
                  _full((BLK, BLK))],
        out_specs=[_row(256), _row(128), _row(128), _row(128)],
        out_shape=[jax.ShapeDtypeStruct((lp, 256), BF16), jax.ShapeDtypeStruct((lp, 128), BF16),
                   jax.ShapeDtypeStruct((lp, 128), BF16), jax.ShapeDtypeStruct((lp, 128), F32)],
        scratch_shapes=[pltpu.VMEM((1, 128), F32)],
        compiler_params=_cp(("arbitrary",)))(small, gq, gkv, fb, ctab, stab, tri)


def _rope_q(qraw, ctab, stab, *, inverse, out_dtype, name):
    lp = qraw.shape[0]

    def body(q_ref, c_ref, s_ref, o_ref):
        c = c_ref[...]
        s = -s_ref[...] if inverse else s_ref[...]
        for p in range(PAIRS):
            lo = p * 256
            o_ref[:, lo:lo + 128] = q_ref[:, lo:lo + 128].astype(out_dtype)
            o_ref[:, lo + 128:lo + 256] = _rope(q_ref[:, lo + 128:lo + 256].astype(F32), c, s).astype(out_dtype)

    return pl.pallas_call(
        body, name=name, grid=(lp // BLK,),
        in_specs=[_row(PAIRS * 256), _row(128), _row(128)], out_specs=_row(PAIRS * 256),
        out_shape=jax.ShapeDtypeStruct((lp, PAIRS * 256), out_dtype),
        compiler_params=_cp(("parallel",)))(qraw, ctab, stab)


def _gate_fwd(o_mla, o_fox, gate):
    lp = o_mla.shape[0]

    def body(om_ref, of_ref, zm_ref, zf_ref, am_ref, af_ref):
        zm = zm_ref[...].astype(F32)
        am_ref[...] = (om_ref[...] * (zm * _sigmoid(zm))).astype(BF16)
        zf = zf_ref[...].astype(F32)
        af_ref[...] = (of_ref[...] * (zf * _sigmoid(zf))).astype(BF16)

    return pl.pallas_call(
        body, name="gate_fwd", grid=(lp // BLK,),
        in_specs=[_row(D_MODEL), _row(D_MODEL), _rowc(D_MODEL, 0), _rowc(D_MODEL, 1)],
        out_specs=[_row(D_MODEL), _row(D_MODEL)],
        out_shape=[jax.ShapeDtypeStruct((lp, D_MODEL), BF16)] * 2,
        compiler_params=_cp(("parallel",)))(o_mla, o_fox, gate, gate)


def _merge_fwd(gate, y_mla, y_fox):
    lp = y_mla.shape[0]

    def body(ga_ref, gb_ref, ym_ref, yf_ref, m_ref):
        sa = _sigmoid(ga_ref[...].astype(F32))
        sb = _sigmoid(gb_ref[...].astype(F32))
        m_ref[...] = (sa * ym_ref[...] + sb * yf_ref[...]).astype(BF16)

    return pl.pallas_call(
        body, name="merge_fwd", grid=(lp // BLK,),
        in_specs=[_rowc(D_MODEL, 2), _rowc(D_MODEL, 3), _row(D_MODEL), _row(D_MODEL)],
        out_specs=_row(D_MODEL), out_shape=jax.ShapeDtypeStruct((lp, D_MODEL), BF16),
        compiler_params=_cp(("parallel",)))(gate, gate, y_mla, y_fox)


def _tail(h, mixed, tgt, gpost):
    lp = h.shape[0]
    shift = pl.BlockSpec((BLK, D_MODEL), lambda i: (jnp.maximum(i - 1, 0), 0))

    def body(h_ref, mx_ref, t_ref, g_ref, dmx_ref, dy_ref, loss_ref, dg_ref):
        i = pl.program_id(0)

        @pl.when(i == 0)
        def _():
            loss_ref[...] = jnp.zeros_like(loss_ref)
            dg_ref[...] = jnp.zeros_like(dg_ref)
            dmx_ref[...] = jnp.zeros_like(dmx_ref)
            dy_ref[...] = jnp.zeros_like(dy_ref)

        @pl.when(i > 0)
        def _():
            mx = mx_ref[...]
            g = g_ref[...]
            r = lax.rsqrt(jnp.mean(mx * mx, axis=-1, keepdims=True) + RMS_EPS)
            nrm = mx * r
            e = (h_ref[...] + nrm * g) - t_ref[...]
            loss_ref[...] += jnp.sum(0.5 * jnp.sum(e * e, axis=-1, keepdims=True) * (1.0 / D_MODEL),
                                     axis=0, keepdims=True)
            dy = e * (1.0 / D_MODEL)
            dy_ref[...] = dy
            dg_ref[...] += jnp.sum(dy * nrm, axis=0, keepdims=True)
            w = dy * g
            dot = jnp.mean(w * mx, axis=-1, keepdims=True)
            dmx_ref[...] = (r * w - mx * (r * r * r * dot)).astype(BF16)

    return pl.pallas_call(
        body, name="tail", grid=(lp // BLK,),
        in_specs=[_row(D_MODEL), _row(D_MODEL), shift, _full((1, D_MODEL))],
        out_specs=[_row(D_MODEL), _row(D_MODEL), _full((1, 1)), _full((1, D_MODEL))],
        out_shape=[jax.ShapeDtypeStruct((lp, D_MODEL), BF16), jax.ShapeDtypeStruct((lp, D_MODEL), F32),
                   jax.ShapeDtypeStruct((1, 1), F32), jax.ShapeDtypeStruct((1, D_MODEL), F32)],
        compiler_params=_cp(("arbitrary",)))(h, mixed, tgt, gpost)


def _merge_bwd(dm, gate, y_mla, y_fox):
    lp = dm.shape[0]

    def body(dm_ref, ga_ref, gb_ref, ym_ref, yf_ref, dym_ref, dyf_ref, dg_ref):
        dm_v = dm_ref[...]
        sa = _sigmoid(ga_ref[...].astype(F32))
        sb = _sigmoid(gb_ref[...].astype(F32))
        dym_ref[...] = (dm_v * sa).astype(BF16)
        dyf_ref[...] = (dm_v * sb).astype(BF16)
        dg_ref[:, 0:D_MODEL] = (dm_v * ym_ref[...] * (sa * (1.0 - sa))).astype(BF16)
        dg_ref[:, D_MODEL:2 * D_MODEL] = (dm_v * yf_ref[...] * (sb * (1.0 - sb))).astype(BF16)

    return pl.pallas_call(
        body, name="merge_bwd", grid=(lp // BLK,),
        in_specs=[_row(D_MODEL), _rowc(D_MODEL, 2), _rowc(D_MODEL, 3), _row(D_MODEL), _row(D_MODEL)],
        out_specs=[_row(D_MODEL), _row(D_MODEL), _row(2 * D_MODEL)],
        out_shape=[jax.ShapeDtypeStruct((lp, D_MODEL), BF16), jax.ShapeDtypeStruct((lp, D_MODEL), BF16),
                   jax.ShapeDtypeStruct((lp, 2 * D_MODEL), BF16)],
        compiler_params=_cp(("parallel",)))(dm, gate, gate, y_mla, y_fox)


def _gate_bwd(da_mla, da_fox, o_mla, o_fox, gate):
    lp = da_mla.shape[0]

    def one(da, o, z):
        sg = _sigmoid(z)
        do = da * (z * sg)
        dz = da * o * (sg * (1.0 + z * (1.0 - sg)))
        return do.astype(BF16), dz.astype(BF16)

    def body(dam_ref, daf_ref, om_ref, of_ref, zm_ref, zf_ref, dom_ref, dof_ref, dz_ref):
        dom_ref[...], dz_ref[:, 0:D_MODEL] = one(dam_ref[...], om_ref[...], zm_ref[...].astype(F32))
        dof_ref[...], dz_ref[:, D_MODEL:2 * D_MODEL] = one(daf_ref[...], of_ref[...], zf_ref[...].astype(F32))

    return pl.pallas_call(
        body, name="gate_bwd", grid=(lp // BLK,),
        in_specs=[_row(D_MODEL)] * 4 + [_rowc(D_MODEL, 0), _rowc(D_MODEL, 1)],
        out_specs=[_row(D_MODEL), _row(D_MODEL), _row(2 * D_MODEL)],
        out_shape=[jax.ShapeDtypeStruct((lp, D_MODEL), BF16), jax.ShapeDtypeStruct((lp, D_MODEL), BF16),
                   jax.ShapeDtypeStruct((lp, 2 * D_MODEL), BF16)],
        compiler_params=_cp(("parallel",)))(da_mla, da_fox, o_mla, o_fox, gate, gate)


def _small_bwd(small, dqn, dkvn, dkr, dcol_t, drow_t, gq, gkv, fb, ctab, stab, triu):
    lp = small.shape[0]
    nb = lp // BLK

    def rrow(w):
        return pl.BlockSpec((BLK, w), lambda i: (nb - 1 - i, 0))

    def body(sm_ref, dqn_ref, dkvn_ref, dkr_ref, dcol_ref, drow_ref, gq_ref, gkv_ref, fb_ref, c_ref, s_ref, tri_ref,
             ds_ref, dgq_ref, dgkv_ref, dfb_ref, carry):
        i = pl.program_id(0)

        @pl.when(i == 0)
        def _():
            carry[...] = jnp.zeros_like(carry)
            dgq_ref[...] = jnp.zeros_like(dgq_ref)
            dgkv_ref[...] = jnp.zeros_like(dgkv_ref)
            dfb_ref[...] = jnp.zeros_like(dfb_ref)

        def norm_bwd(x, dn, g, dg_ref):
            r = lax.rsqrt(jnp.mean(x * x, axis=-1, keepdims=True) + RMS_EPS)
            dg_ref[...] += jnp.sum(dn * (x * r), axis=0, keepdims=True)
            w = dn * g
            dot = jnp.mean(w * x, axis=-1, keepdims=True)
            return r * w - x * (r * r * r * dot)

        ds_ref[:, 0:256] = norm_bwd(sm_ref[:, 0:256], dqn_ref[...], gq_ref[...], dgq_ref).astype(BF16)
        ds_ref[:, 256:384] = norm_bwd(sm_ref[:, 256:384], dkvn_ref[...], gkv_ref[...], dgkv_ref).astype(BF16)

        dk = dkr_ref[0]
        for p in range(1, PAIRS):
            dk = dk + dkr_ref[p]
        dk = _rope(dk, c_ref[...], -s_ref[...])
        lane = lax.broadcasted_iota(jnp.int32, dk.shape, 1)
        dk = jnp.where(lane < MLA_ROPE, dk + pltpu.roll(dk, LANES - MLA_ROPE, 1), 0.0)
        ds_ref[:, 384:512] = dk.astype(BF16)

        dcr = dcol_ref[...] - drow_ref[...]
        hi, mid, lo = _split3(dcr)
        t = tri_ref[...]
        suf = (_dot(t, hi, 1, 0) + _dot(t, mid, 1, 0)) + _dot(t, lo, 1, 0) + carry[...]
        fl = sm_ref[:, 512:640] + fb_ref[...]
        dfl = jnp.where(_row_valid(nb - 1 - i), -suf * _sigmoid(-fl), 0.0)
        ds_ref[:, 512:640] = dfl.astype(BF16)
        dfb_ref[...] += jnp.sum(dfl, axis=0, keepdims=True)
        carry[...] += jnp.sum(dcr, axis=0, keepdims=True)

    return pl.pallas_call(
        body, name="small_bwd", grid=(nb,),
        in_specs=[rrow(SMALL_W), rrow(256), rrow(128),
                  pl.BlockSpec((PAIRS, BLK, 128), lambda i: (0, nb - 1 - i, 0)), rrow(128), rrow(128),
                  _full((1, 256)), _full((1, 128)), _full((1, 128)), rrow(128), rrow(128), _full((BLK, BLK))],
        out_specs=[rrow(SMALL_W), _full((1, 256)), _full((1, 128)), _full((1, 128))],
        out_shape=[jax.ShapeDtypeStruct((lp, SMALL_W), BF16), jax.ShapeDtypeStruct((1, 256), F32),
                   jax.ShapeDtypeStruct((1, 128), F32), jax.ShapeDtypeStruct((1, 128), F32)],
        scratch_shapes=[pltpu.VMEM((1, 128), F32)],
        compiler_params=_cp(("arbitrary",)))(small, dqn, dkvn, dkr, dcol_t, drow_t, gq, gkv, fb, ctab, stab, triu)


def _pre_bwd(du, h, dy, gpre, s_rows):
    lp = h.shape[0]
    shift = pl.BlockSpec((BLK, D_MODEL), lambda i: (jnp.maximum(i - 1, 0), 0))

    def body(du_ref, h_ref, dy_ref, g_ref, dx_ref, dmeta_ref, dg_ref):
        i = pl.program_id(0)

        @pl.when(i == 0)
        def _():
            dg_ref[...] = jnp.zeros_like(dg_ref)

        hv = h_ref[...]
        duv = du_ref[...]
        r = lax.rsqrt(jnp.mean(hv * hv, axis=-1, keepdims=True) + RMS_EPS)
        dg_ref[...] += jnp.sum(duv * (hv * r), axis=0, keepdims=True)
        w = duv * g_ref[...]
        dot = jnp.mean(w * hv, axis=-1, keepdims=True)
        dh = dy_ref[...] + (r * w - hv * (r * r * r * dot))
        dx_ref[...] = dh

        @pl.when(i == 0)
        def _():
            dmeta_ref[...] = dh[0:N_META, :]

    return pl.pallas_call(
        body, name="pre_bwd", grid=(lp // BLK,),
        in_specs=[_row(D_MODEL), _row(D_MODEL), _row(D_MODEL), _full((1, D_MODEL))],
        out_specs=[shift, _full((N_META, D_MODEL)), _full((1, D_MODEL))],
        out_shape=[jax.ShapeDtypeStruct((s_rows, D_MODEL), F32), jax.ShapeDtypeStruct((N_META, D_MODEL), F32),
                   jax.ShapeDtypeStruct((1, D_MODEL), F32)],
        compiler_params=_cp(("arbitrary",)))(du, h, dy, gpre)


def _pair_masks(rope):
    lane = lax.broadcasted_iota(jnp.int32, (1, LANES), 1)
    mas = [lane < HEAD_DIM, lane >= HEAD_DIM]
    if not rope:
        return mas, mas
    wide = lax.broadcasted_iota(jnp.int32, (1, 2 * LANES), 1)
    rope_lo = LANES + MLA_ROPE
    return mas, [(wide < HEAD_DIM) | ((wide >= LANES) & (wide < rope_lo)),
                 ((wide >= HEAD_DIM) & (wide < LANES)) | ((wide >= rope_lo) & (wide < rope_lo + MLA_ROPE))]


def _mask2(x, masks):
    return [jnp.where(m, x, jnp.zeros_like(x)) for m in masks]


def _attn_fwd(q, k, v, *, kr=None, nbrep=None, scale, qcol, kcol, vcol, name):
    lp = q.shape[0]
    nb = lp // BLK
    rope = kr is not None
    bias = nbrep is not None
    qw = 256 if rope else 128

    def body(*refs):
        it = iter(refs)
        q_ref, k_ref, v_ref = next(it), next(it), next(it)
        kr_ref = next(it) if rope else None
        nb_ref = next(it) if bias else None
        o_ref, lse_ref = next(it), next(it)
        i = pl.program_id(1)
        mas, hmask = _pair_masks(rope)
        qh = _mask2(q_ref[...], hmask)
        if bias:
            qh = [x * scale for x in qh]
        def causal(n):
            return (lax.broadcasted_iota(jnp.int32, (n, BLK), 0) <= lax.broadcasted_iota(jnp.int32, (n, BLK), 1))

        meta_mask = causal(N_META) | (i > 0)
        diag_mask = causal(BLK) & (i > 0)

        def scores(kc, n=BLK):
            k0 = pl.multiple_of(kc * BLK, BLK)
            kk = k_ref[pl.ds(k0, n), :]
            if rope:
                kk = jnp.concatenate([kk, kr_ref[pl.ds(k0, n), :]], axis=1)
            out = []
            for h in range(2):
                s = _dot(kk, qh[h], 1, 1)
                out.append(s * scale if rope else s)
            return out

        def update(kc, ss, carry, mask, n=BLK):
            stats, acc = carry[:4], carry[4]
            k0 = pl.multiple_of(kc * BLK, BLK)
            vv = v_ref[pl.ds(k0, n), :]
            new_stats, alphas, ps = [], [], []
            for h in range(2):
                m_prev, l_prev = stats[2 * h], stats[2 * h + 1]
                s = ss[h]
                if bias:
                    nbc = nb_ref[h, pl.ds(k0, n), :]
                    s = s + jnp.concatenate([nbc, nbc], axis=1)
                if mask is not None:
                    s = jnp.where(mask, s, NEG)
                m_new = jnp.maximum(m_prev, jnp.max(s, axis=0, keepdims=True))
                alpha = jnp.exp(m_prev - m_new)
                p = jnp.exp(s - m_new)
                new_stats += [m_new, alpha * l_prev + jnp.sum(p, axis=0, keepdims=True)]
                alphas.append(alpha)
                ps.append(p.astype(BF16))
            vcat = jnp.concatenate(_mask2(vv, mas), axis=0)
            pv = _dot(vcat, jnp.concatenate(ps, axis=0), 0, 0)
            a_full = jnp.concatenate([jnp.broadcast_to(a, (HEAD_DIM, BLK)) for a in alphas], axis=0)
            return (*new_stats, a_full * acc + pv)

        neg = jnp.full((1, BLK), NEG, F32)
        zero = jnp.zeros((1, BLK), F32)
        s_nxt = scores(jnp.minimum(1, i))
        c = update(0, scores(0, N_META), (neg, zero, neg, zero, jnp.zeros((LANES, BLK), F32)), meta_mask, N_META)

        def step(kc, cr):
            nxt = scores(kc + 1)
            return (*nxt, *update(kc, cr[:2], cr[2:], None))

        cr = lax.fori_loop(1, i, step, (*s_nxt, *c))
        c = update(i, cr[:2], cr[2:], diag_mask)
        inv = jnp.concatenate([jnp.broadcast_to(1.0 / c[1], (HEAD_DIM, BLK)),
                               jnp.broadcast_to(1.0 / c[3], (HEAD_DIM, BLK))], axis=0)
        o_ref[...] = (c[4] * inv).T
        lse_ref[0, 0:1, :] = c[0] + jnp.log(c[1])
        lse_ref[0, 1:2, :] = c[2] + jnp.log(c[3])

    in_specs = [pl.BlockSpec((BLK, qw), lambda p, i: (i, qcol + p)),
                pl.BlockSpec((lp, 128), lambda p, i: (0, kcol(p))),
                pl.BlockSpec((lp, 128), lambda p, i: (0, vcol(p)))]
    ins = [q, k, v]
    if rope:
        in_specs.append(pl.BlockSpec((lp, 128), lambda p, i: (0, 0)))
        ins.append(kr)
    if bias:
        in_specs.append(pl.BlockSpec((2, lp, 128), lambda p, i: (p, 0, 0)))
        ins.append(nbrep)
    return pl.pallas_call(
        body, name=name, grid=(PAIRS, nb), in_specs=in_specs,
        out_specs=[pl.BlockSpec((BLK, 128), lambda p, i: (i, p)),
                   pl.BlockSpec((1, 2, BLK), lambda p, i: (p, 0, i))],
        out_shape=[jax.ShapeDtypeStruct((lp, D_MODEL), F32), jax.ShapeDtypeStruct((PAIRS, 2, lp), F32)],
        compiler_params=_cp(("parallel", "arbitrary"), VMEM_BIG))(*ins)


def _attn_bwd(q, k, v, do, o, lse, *, kr=None, nbrep=None, scale, qcol, kcol, vcol, name):
    lp = q.shape[0]
    nb = lp // BLK
    rope = kr is not None
    bias = nbrep is not None
    qw = 256 if rope else 128

    def body(*refs):
        it = iter(refs)
        q_ref, k_ref, v_ref = next(it), next(it), next(it)
        kr_ref = next(it) if rope else None
        nb_ref = next(it) if bias else None
        do_ref, o_ref, lse_ref = next(it), next(it), next(it)
        dq_ref, dk_ref, dv_ref = next(it), next(it), next(it)
        x_ref = next(it)
        drow_ref = next(it) if bias else None
        delta = next(it)
        kb = pl.program_id(1)
        mas, hmask = _pair_masks(rope)
        lane = lax.broadcasted_iota(jnp.int32, (1, LANES), 1)

        @pl.when(kb == 0)
        def _():
            dq_ref[...] = jnp.zeros_like(dq_ref)
            if bias:
                drow_ref[...] = jnp.zeros_like(drow_ref)
            sub = lax.broadcasted_iota(jnp.int32, (8, LANES), 0)
            sel = (((sub == 0) & mas[0]) | ((sub == 1) & mas[1])).astype(BF16)

            def dstep(c, carry):
                r0 = pl.multiple_of(c * BLK, BLK)
                prod = do_ref[pl.ds(r0, BLK), :].astype(F32) * o_ref[pl.ds(r0, BLK), :]
                hi, mid, lo = _split3(prod)
                delta[:, pl.ds(r0, BLK)] = (_dot(sel, hi, 1, 1) + _dot(sel, mid, 1, 1)) + _dot(sel, lo, 1, 1)
                return carry

            lax.fori_loop(0, nb, dstep, 0)

        def masked_q(q0):
            qh = _mask2(q_ref[pl.ds(q0, BLK), :], hmask)
            return [x * scale for x in qh] if bias else qh

        def key_pass(n):
            kk = k_ref[0:n, :]
            if rope:
                kk = jnp.concatenate([kk, kr_ref[0:n, :]], axis=1)
            vh = _mask2(v_ref[0:n, :], mas)
            kcat = jnp.concatenate(_mask2(kk, hmask), axis=0)
            if bias:
                kcat = kcat * scale
                nbc = [jnp.concatenate([nb_ref[h, 0:n, :], nb_ref[h, 0:n, :]], axis=1) for h in range(2)]
            diag_mask = (lax.broadcasted_iota(jnp.int32, (n, BLK), 0) <= lax.broadcasted_iota(jnp.int32, (n, BLK), 1))

            def chunk(qc, carry, mask):
                carry = list(carry)
                q0 = pl.multiple_of(qc * BLK, BLK)
                dov = do_ref[pl.ds(q0, BLK), :]
                doh = _mask2(dov, mas)
                qh = masked_q(q0)
                pbs, dss = [], []
                for h in range(2):
                    s = _dot(kk, qh[h], 1, 1)
                    if rope:
                        s = s * scale
                    if bias:
                        s = s + nbc[h]
                    p = jnp.exp(s - lse_ref[0, h:h + 1, pl.ds(q0, BLK)])
                    if mask is not None:
                        p = jnp.where(mask, p, 0.0)
                    ds = p * (_dot(vh[h], dov, 1, 1) - delta[h:h + 1, pl.ds(q0, BLK)])
                    if bias:
                        drow_ref[0, h:h + 1, pl.ds(q0, BLK)] += jnp.sum(ds, axis=0, keepdims=True)
                        carry[2 + h] = carry[2 + h] + jnp.sum(ds, axis=1, keepdims=True)
                    else:
                        ds = ds * scale
                    pbs.append(p.astype(BF16))
                    dss.append(ds.astype(BF16))
                ds_lanes = jnp.concatenate(dss, axis=1)
                ds_rows = jnp.concatenate(dss, axis=0)
                carry[0] = carry[0] + _dot(ds_lanes, jnp.concatenate(qh, axis=0), 1, 0)
                carry[1] = carry[1] + _dot(jnp.concatenate(pbs, axis=1), jnp.concatenate(doh, axis=0), 1, 0)
                dq_ref[pl.ds(q0, BLK), :] += _dot(ds_rows, kcat, 0, 0)
                return tuple(carry)

            init = [jnp.zeros((n, qw), F32), jnp.zeros((n, LANES), F32)]
            if bias:
                init += [jnp.zeros((n, 1), F32), jnp.zeros((n, 1), F32)]
            c = chunk(kb, tuple(init), diag_mask)
            c = lax.fori_loop(kb + 1, nb, lambda qc, cr: chunk(qc, cr, None), c)

            def rows(a, dtype):
                a = a.astype(dtype)
                return a if n == BLK else jnp.concatenate([a, jnp.zeros((BLK - n, a.shape[1]), dtype)], axis=0)

            dk_ref[...] = rows(c[0][:, 0:LANES], BF16)
            dv_ref[...] = rows(c[1], BF16)
            if rope:
                x_ref[0] = rows(c[0][:, LANES:2 * LANES], F32)
            if bias:
                x_ref[0] = rows(jnp.where(lane == 0, c[2], jnp.where(lane == 1, c[3], 0.0)), F32)

        @pl.when(kb == 0)
        def _():
            key_pass(N_META)

        @pl.when(kb > 0)
        def _():
            key_pass(BLK)

    in_specs = [pl.BlockSpec((lp, qw), lambda p, j: (0, qcol + p)),
                pl.BlockSpec((BLK, 128), lambda p, j: (j, kcol(p))),
                pl.BlockSpec((BLK, 128), lambda p, j: (j, vcol(p)))]
    ins = [q, k, v]
    if rope:
        in_specs.append(pl.BlockSpec((BLK, 128), lambda p, j: (j, 0)))
        ins.append(kr)
    if bias:
        in_specs.append(pl.BlockSpec((2, BLK, 128), lambda p, j: (p, j, 0)))
        ins.append(nbrep)
    in_specs += [pl.BlockSpec((lp, 128), lambda p, j: (0, p)), pl.BlockSpec((lp, 128), lambda p, j: (0, p)),
                 pl.BlockSpec((1, 2, lp), lambda p, j: (p, 0, 0))]
    ins += [do, o, lse]
    out_specs = [pl.BlockSpec((lp, qw), lambda p, j: (0, p)),
                 pl.BlockSpec((BLK, 128), lambda p, j: (j, p)),
                 pl.BlockSpec((BLK, 128), lambda p, j: (j, p)),
                 pl.BlockSpec((1, BLK, 128), lambda p, j: (p, j, 0))]
    out_shape = [jax.ShapeDtypeStruct((lp, PAIRS * qw), F32), jax.ShapeDtypeStruct((lp, D_MODEL), BF16),
                 jax.ShapeDtypeStruct((lp, D_MODEL), BF16), jax.ShapeDtypeStruct((PAIRS, lp, 128), F32)]
    if bias:
        out_specs.append(pl.BlockSpec((1, 2, lp), lambda p, j: (p, 0, 0)))
        out_shape.append(jax.ShapeDtypeStruct((PAIRS, 2, lp), F32))
    return pl.pallas_call(
        body, name=name, grid=(PAIRS, nb), in_specs=in_specs, out_specs=out_specs, out_shape=out_shape,
        scratch_shapes=[pltpu.VMEM((8, lp), F32)],
        compiler_params=_cp(("parallel", "arbitrary"), VMEM_BIG))(*ins)


def _adamw(w, g, m, v, name):
    rows, cols = w.shape
    tr = 128 if rows * cols > 512 * 1024 else rows

    def body(w_ref, g_ref, m_ref, v_ref, d_ref, nm_ref, nv_ref):
        gv = g_ref[...]
        nm = ADAM_B1 * m_ref[...] + (1.0 - ADAM_B1) * gv
        nv = ADAM_B2 * v_ref[...] + (1.0 - ADAM_B2) * (gv * gv)
        m_hat = nm / (1.0 - ADAM_B1 ** ADAM_STEP)
        v_hat = nv / (1.0 - ADAM_B2 ** ADAM_STEP)
        d_ref[...] = -ADAM_LR * (m_hat / (jnp.sqrt(v_hat) + ADAM_EPS) + ADAM_WD * w_ref[...])
        nm_ref[...] = nm
        nv_ref[...] = nv

    spec = pl.BlockSpec((tr, cols), lambda i: (i, 0))
    return pl.pallas_call(
        body, name=name, grid=(rows // tr,), in_specs=[spec] * 4, out_specs=[spec] * 3,
        out_shape=[jax.ShapeDtypeStruct((rows, cols), F32)] * 3,
        compiler_params=_cp(("parallel",), VMEM_BIG))(w, g, m, v)


def _add_cores(g, from_sib, name):
    n, rows, cols = g.shape
    half = rows // 2
    tr = _tile(half, (256, 240))
    nt = half // tr

    def body(lo_ref, hi_ref, s_ref, o_ref):
        mine = jnp.where(lax.axis_index("c") == 0, lo_ref[0], hi_ref[0])
        o_ref[0] = (mine + s_ref[0]).astype(BF16)

    return pl.pallas_call(
        body, name=name, grid=(n, nt),
        in_specs=[pl.BlockSpec((1, tr, cols), lambda j, i: (j, i, 0)),
                  pl.BlockSpec((1, tr, cols), lambda j, i: (j, nt + i, 0)),
                  pl.BlockSpec((1, tr, cols), lambda j, i: (j, i, 0))],
        out_specs=pl.BlockSpec((1, tr, cols), lambda j, i: (j, i, 0)),
        out_shape=jax.ShapeDtypeStruct((n, half, cols), BF16),
        compiler_params=_cp(("parallel", "parallel"), VMEM_BIG))(g, g, from_sib)


def _add_chips(x, name):
    n, rows, cols = x.shape
    tr = _tile(rows, (256, 240))

    def body(x_ref, o_ref):
        o_ref[...] = ((x_ref[0].astype(F32) + x_ref[1].astype(F32)) + x_ref[2].astype(F32)) + x_ref[3].astype(F32)

    return pl.pallas_call(
        body, name=name, grid=(rows // tr,),
        in_specs=[pl.BlockSpec((n, tr, cols), lambda i: (0, i, 0))],
        out_specs=pl.BlockSpec((tr, cols), lambda i: (i, 0)),
        out_shape=jax.ShapeDtypeStruct((rows, cols), F32), compiler_params=_cp(("parallel",), VMEM_BIG))(x)


def _axes():
    return lax.axis_index("x"), lax.axis_index("y"), lax.axis_index("c")


def _other_chips(x, y):
    return [(1 - x, y), (x, 1 - y), (1 - x, 1 - y)]


ANY = pl.BlockSpec(memory_space=pl.ANY)


def _rcopy(src, dst, send_sems, recv_sems, k, to):
    return pltpu.make_async_remote_copy(src_ref=src, dst_ref=dst, send_sem=send_sems.at[k], recv_sem=recv_sems.at[k],
                                        device_id=to, device_id_type=MESH)


def _gather_weights(shards, meta):
    n = len(shards)

    def body(*refs):
        srcs, meta_ref = refs[:n], refs[n]
        outs, mout_ref = refs[n + 1:2 * n + 1], refs[2 * n + 1]
        send_sems, recv_sems, local_sems = refs[2 * n + 2:]
        x, y, c = _axes()
        me = 2 * x + y
        sib = (x, y, 1 - c)
        chips = _other_chips(x, y)

        def half(t, chip_idx, cc):
            hr = shards[t].shape[0] // 2
            return outs[t].at[chip_idx, pl.ds(cc * hr, hr), :]

        local = [pltpu.make_async_copy(srcs[t], outs[t].at[me], local_sems.at[t]) for t in range(n)]
        local.append(pltpu.make_async_copy(meta_ref, mout_ref.at[me], local_sems.at[n]))
        for cp in local:
            cp.start()
        first = []
        for j, (px, py) in enumerate(chips):
            for t in range(n):
                hr = shards[t].shape[0] // 2
                first.append(_rcopy(srcs[t].at[pl.ds(c * hr, hr), :], half(t, me, c), send_sems, recv_sems,
                                    3 * t + j, (px, py, c)))
            first.append(_rcopy(meta_ref, mout_ref.at[me], send_sems, recv_sems, 3 * n + j, (px, py, c)))
        for cp in first:
            cp.start()
        passed = []
        for j, (px, py) in enumerate(chips):
            src_chip = 2 * px + py
            for t in range(n):
                _rcopy(half(t, src_chip, c), half(t, src_chip, c), send_sems, recv_sems, 3 * t + j, sib).wait_recv()
                fwd = _rcopy(half(t, src_chip, c), half(t, src_chip, c), send_sems, recv_sems, 3 * (n + 1 + t) + j, sib)
                fwd.start()
                passed.append(fwd)
            _rcopy(mout_ref.at[src_chip], mout_ref.at[src_chip], send_sems, recv_sems, 3 * n + j, sib).wait_recv()
        for j, (px, py) in enumerate(chips):
            src_chip = 2 * px + py
            for t in range(n):
                _rcopy(half(t, src_chip, 1 - c), half(t, src_chip, 1 - c), send_sems, recv_sems,
                       3 * (n + 1 + t) + j, sib).wait_recv()
        for cp in first + passed:
            cp.wait_send()
        for cp in local:
            cp.wait()

    nsem = 3 * (2 * n + 1)
    return pl.pallas_call(
        body, name="gather_weights", in_specs=[ANY] * (n + 1), out_specs=[ANY] * (n + 1),
        out_shape=[jax.ShapeDtypeStruct((N_CHIPS,) + s.shape, s.dtype) for s in shards]
        + [jax.ShapeDtypeStruct((N_CHIPS,) + meta.shape, meta.dtype)],
        scratch_shapes=[pltpu.SemaphoreType.DMA((nsem,)), pltpu.SemaphoreType.DMA((nsem,)),
                        pltpu.SemaphoreType.DMA((n + 1,))])(*shards, meta)


def _swap_halves(gs):
    n = len(gs)

    def body(*refs):
        srcs, outs = refs[:n], refs[n:2 * n]
        send_sems, recv_sems = refs[2 * n:]
        x, y, c = _axes()
        cps = []
        for t in range(n):
            hr = gs[t].shape[1] // 2
            for j in range(N_CHIPS):
                cps.append(_rcopy(srcs[t].at[j, pl.ds((1 - c) * hr, hr), :], outs[t].at[j], send_sems, recv_sems,
                                  N_CHIPS * t + j, (x, y, 1 - c)))
        for cp in cps:
            cp.start()
        for cp in cps:
            cp.wait()

    return pl.pallas_call(
        body, name="swap_halves", in_specs=[ANY] * n, out_specs=[ANY] * n,
        out_shape=[jax.ShapeDtypeStruct((N_CHIPS, g.shape[1] // 2, g.shape[2]), g.dtype) for g in gs],
        scratch_shapes=[pltpu.SemaphoreType.DMA((N_CHIPS * n,)), pltpu.SemaphoreType.DMA((N_CHIPS * n,))])(*gs)


def _scatter_chips(parts):
    n = len(parts)

    def body(*refs):
        srcs, outs = refs[:n], refs[n:2 * n]
        send_sems, recv_sems, local_sems = refs[2 * n:]
        x, y, c = _axes()
        me = 2 * x + y
        local = [pltpu.make_async_copy(srcs[t].at[me], outs[t].at[me], local_sems.at[t]) for t in range(n)]
        for cp in local:
            cp.start()
        cps = []
        for j, (px, py) in enumerate(_other_chips(x, y)):
            for t in range(n):
                cps.append(_rcopy(srcs[t].at[2 * px + py], outs[t].at[me], send_sems, recv_sems, 3 * t + j,
                                  (px, py, c)))
        for cp in cps:
            cp.start()
        for cp in cps:
            cp.wait()
        for cp in local:
            cp.wait()

    return pl.pallas_call(
        body, name="scatter_chips", in_specs=[ANY] * n, out_specs=[ANY] * n,
        out_shape=[jax.ShapeDtypeStruct(p.shape, p.dtype) for p in parts],
        scratch_shapes=[pltpu.SemaphoreType.DMA((3 * n,)), pltpu.SemaphoreType.DMA((3 * n,)),
                        pltpu.SemaphoreType.DMA((n,))])(*parts)


def _join_halves(rs):
    n = len(rs)

    def body(*refs):
        srcs, outs = refs[:n], refs[n:2 * n]
        send_sems, recv_sems, local_sems = refs[2 * n:]
        x, y, c = _axes()
        local, cps, waits = [], [], []
        for t in range(n):
            hr = rs[t].shape[0]
            mine = outs[t].at[pl.ds(c * hr, hr), :]
            other = outs[t].at[pl.ds((1 - c) * hr, hr), :]
            local.append(pltpu.make_async_copy(srcs[t], mine, local_sems.at[t]))
            cps.append(_rcopy(srcs[t], mine, send_sems, recv_sems, t, (x, y, 1 - c)))
            waits.append(_rcopy(srcs[t], other, send_sems, recv_sems, t, (x, y, 1 - c)))
        for cp in local + cps:
            cp.start()
        for cp in cps:
            cp.wait_send()
        for cp in waits:
            cp.wait_recv()
        for cp in local:
            cp.wait()

    return pl.pallas_call(
        body, name="join_halves", in_specs=[ANY] * n, out_specs=[ANY] * n,
        out_shape=[jax.ShapeDtypeStruct((2 * r.shape[0], r.shape[1]), r.dtype) for r in rs],
        scratch_shapes=[pltpu.SemaphoreType.DMA((n,)), pltpu.SemaphoreType.DMA((n,)),
                        pltpu.SemaphoreType.DMA((n,))])(*rs)


SMALL_ROWS = 24


def _allreduce_small(vec):
    def body(v_ref, out_ref, slots, send_sems, recv_sems):
        x, y, c = _axes()
        me = 4 * x + 2 * y + c
        slots[me] = v_ref[...]
        cps = []
        for k in range(1, 8):
            kx, ky, kc = (k >> 2) & 1, (k >> 1) & 1, k & 1
            peer = (1 - x if kx else x, 1 - y if ky else y, 1 - c if kc else c)
            cps.append(_rcopy(v_ref, slots.at[me], send_sems, recv_sems, k - 1, peer))
        for cp in cps:
            cp.start()
        for cp in cps:
            cp.wait()
        tot = slots[0]
        for k in range(1, 8):
            tot = tot + slots[k]
        out_ref[...] = tot

    return pl.pallas_call(
        body, name="allreduce_small",
        in_specs=[pl.BlockSpec(memory_space=pltpu.VMEM)], out_specs=pl.BlockSpec(memory_space=pltpu.VMEM),
        out_shape=jax.ShapeDtypeStruct((SMALL_ROWS, 128), F32),
        scratch_shapes=[pltpu.VMEM((8, SMALL_ROWS, 128), F32), pltpu.SemaphoreType.DMA((7,)),
                        pltpu.SemaphoreType.DMA((7,))])(vec)


def _pack_p2(w_uq, w_ukv, w_br_mla, w_br_fox, w_out, meta, dtype):
    parts = [w_uq.reshape(96, D_MODEL), w_ukv.reshape(64, D_MODEL), w_br_mla, w_br_fox, w_out,
             meta.reshape(4, D_MODEL), jnp.zeros((P2_ROWS - 932, D_MODEL), meta.dtype)]
    return jnp.concatenate([p.astype(dtype) for p in parts], axis=0)


def _unpack_p2(pk):
    return (pk[0:96].reshape(256, 384), pk[96:160].reshape(128, 512), pk[160:416], pk[416:672], pk[672:928],
            pk[928:932].reshape(N_META, 256))


def _uq_arrange(w):
    w3 = w.reshape(256, HEADS, 96)
    nope = w3[:, :, :64].reshape(256, PAIRS, 128)
    pe = w3[:, :, 64:].reshape(256, PAIRS, 64)
    return jnp.concatenate([nope, pe, jnp.zeros((256, PAIRS, 64), w.dtype)], axis=2).reshape(256, PAIRS * 256)


def _uq_restore(g):
    g3 = g.reshape(256, PAIRS, 256)
    nope = g3[:, :, :128].reshape(256, HEADS, 64)
    pe = g3[:, :, 128:192].reshape(256, HEADS, 32)
    return jnp.concatenate([nope, pe], axis=2).reshape(256, HEADS * 96)


def _ukv_arrange(w):
    w3 = w.reshape(128, HEADS, 128)
    return jnp.concatenate([w3[:, :, :64].reshape(128, 1024), w3[:, :, 64:].reshape(128, 1024)], axis=1)


def _ukv_restore(g):
    kn = g[:, :1024].reshape(128, HEADS, 64)
    vv = g[:, 1024:].reshape(128, HEADS, 64)
    return jnp.concatenate([kn, vv], axis=2).reshape(128, HEADS * 128)


def _rope_tables(lp):
    r = jnp.arange(lp)
    pos = jnp.where(r < N_META, r, jnp.where(r >= PAD, r - PAD + N_META, 0))
    half = MLA_ROPE // 2
    inv_freq = ROPE_THETA ** (-jnp.arange(half, dtype=F32) / half)
    ang = pos.astype(F32)[:, None] * inv_freq[None, :]
    cos, sin = jnp.cos(ang), jnp.sin(ang)
    one, zero = jnp.ones((lp, 64), F32), jnp.zeros((lp, 64), F32)
    return (jnp.concatenate([cos, cos, cos, cos, one], axis=1),
            jnp.concatenate([-sin, sin, -sin, sin, zero], axis=1))


def _pad_lanes(v, n=128):
    return jnp.pad(v, ((0, 0), (0, n - v.shape[1])))


def _in_cols(slabs, a, b):
    out = []
    for j in range(N_CHIPS):
        lo, hi = max(a, W_IN_SHARD * j), min(b, W_IN_SHARD * (j + 1))
        if lo < hi:
            out.append(slabs[j][:, lo - W_IN_SHARD * j:hi - W_IN_SHARD * j])
    return out


def _local_step(x2, tgt2, meta_f, w_small, w_attn, w_gate, w_uq_f, w_ukv_f, w_bm, w_bf, w_o, pre_norm_g,
                post_norm_g, mla_q_norm_g, mla_kv_norm_g, fox_forget_b):
    s_rows = x2.shape[0]
    lp = PAD + s_rows
    w_uq_a = _uq_arrange(w_uq_f)
    w_ukv_a = _ukv_arrange(w_ukv_f)

    ctab, stab = _rope_tables(lp)
    ii = jnp.arange(BLK)
    tri_lo = (ii[:, None] >= ii[None, :]).astype(BF16)
    tri_up = (ii[:, None] <= ii[None, :]).astype(BF16)
    fb128 = _pad_lanes(fox_forget_b)

    h = jnp.concatenate([meta_f, jnp.zeros((PAD - N_META, D_MODEL), F32), x2], axis=0)
    u = _rms_pre(h, pre_norm_g)
    small = _mm(u, w_small, mode="nn", out_dtype=F32, name="proj_small")
    attn = _mm(u, w_attn, mode="nn", out_dtype=BF16, name="proj_attn")
    gate = _mm(u, w_gate, mode="nn", out_dtype=BF16, name="proj_gate")
    qn, kvn, kr, ncum = _small_prep(small, mla_q_norm_g, mla_kv_norm_g, fb128, ctab, stab, tri_lo)
    qraw = _mm(qn, w_uq_a, mode="nn", out_dtype=F32, name="mla_q")
    qcat = _rope_q(qraw, ctab, stab, inverse=False, out_dtype=BF16, name="rope_q")
    kv = _mm(kvn, w_ukv_a, mode="nn", out_dtype=BF16, name="mla_kv")
    nbrep = jnp.broadcast_to(ncum[:, :HEADS].T[:, :, None], (HEADS, lp, LANES))

    mla_cols = dict(qcol=0, kcol=lambda p: p, vcol=lambda p: PAIRS + p)
    fox_cols = dict(qcol=0, kcol=lambda p: PAIRS + p, vcol=lambda p: 2 * PAIRS + p)
    o_mla, lse_mla = _attn_fwd(qcat, kv, kv, kr=kr, scale=MLA_SCALE, name="mla_fwd", **mla_cols)
    o_fox, lse_fox = _attn_fwd(attn, attn, attn, nbrep=nbrep, scale=FOX_SCALE, name="fox_fwd", **fox_cols)

    a_mla, a_fox = _gate_fwd(o_mla, o_fox, gate)
    y_mla = _mm(a_mla, w_bm, mode="nn", out_dtype=F32, name="br_mla")
    y_fox = _mm(a_fox, w_bf, mode="nn", out_dtype=F32, name="br_fox")
    mg = _merge_fwd(gate, y_mla, y_fox)
    mixed = _mm(mg, w_o, mode="nn", out_dtype=F32, name="out_proj")
    dmixed, dy, loss_p, dg_post = _tail(h, mixed, tgt2, post_norm_g)

    d_w_out = _mm(mg, dmixed, mode="tn", out_dtype=F32, name="d_w_out")
    dm = _mm(dmixed, w_o, mode="nt", out_dtype=F32, name="d_merge")
    dy_mla, dy_fox, dgate_ab = _merge_bwd(dm, gate, y_mla, y_fox)
    d_w_bm = _mm(a_mla, dy_mla, mode="tn", out_dtype=F32, name="d_w_br_mla")
    d_w_bf = _mm(a_fox, dy_fox, mode="tn", out_dtype=F32, name="d_w_br_fox")
    da_mla = _mm(dy_mla, w_bm, mode="nt", out_dtype=F32, name="d_a_mla")
    da_fox = _mm(dy_fox, w_bf, mode="nt", out_dtype=F32, name="d_a_fox")
    do_mla, do_fox, dgate_z = _gate_bwd(da_mla, da_fox, o_mla, o_fox, gate)

    dqcat, dkn, dvm, dkr = _attn_bwd(qcat, kv, kv, do_mla, o_mla, lse_mla, kr=kr, scale=MLA_SCALE,
                                     name="mla_bwd", **mla_cols)
    dfq, dfk, dfv, dcol, drow = _attn_bwd(attn, attn, attn, do_fox, o_fox, lse_fox, nbrep=nbrep, scale=FOX_SCALE,
                                          name="fox_bwd", **fox_cols)

    dq_a = _rope_q(dqcat, ctab, stab, inverse=True, out_dtype=BF16, name="rope_q_bwd")
    d_w_uq_a = _mm(qn, dq_a, mode="tn", out_dtype=F32, name="d_w_uq")
    dqn = _mm(dq_a, w_uq_a, mode="nt", out_dtype=F32, name="d_qn")
    d_w_ukv_a = jnp.concatenate([_mm(kvn, dkn, mode="tn", out_dtype=F32, name="d_w_uk"),
                                 _mm(kvn, dvm, mode="tn", out_dtype=F32, name="d_w_uv")], axis=1)
    dkvn = _mm(dkn, w_ukv_a[:, :1024], mode="nt", out_dtype=F32, name="d_kvn_k")
    dkvn = _mm(dvm, w_ukv_a[:, 1024:], mode="nt", out_dtype=F32, name="d_kvn_v", acc=dkvn)
    dcol_t = _pad_lanes(jnp.transpose(dcol[:, :, 0:2], (1, 0, 2)).reshape(lp, HEADS))
    drow_t = _pad_lanes(drow.reshape(HEADS, lp).T)
    dsmall, dg_q, dg_kv, dfb = _small_bwd(small, dqn, dkvn, dkr, dcol_t, drow_t, mla_q_norm_g, mla_kv_norm_g,
                                          fb128, ctab, stab, tri_up)

    dw_small = _mm(u, dsmall, mode="tn", out_dtype=F32, name="d_w_small")
    dw_fq = _mm(u, dfq, mode="tn", out_dtype=F32, name="d_w_fq")
    dw_fk = _mm(u, dfk, mode="tn", out_dtype=F32, name="d_w_fk")
    dw_fv = _mm(u, dfv, mode="tn", out_dtype=F32, name="d_w_fv")
    dw_z = _mm(u, dgate_z, mode="tn", out_dtype=F32, name="d_w_z")
    dw_g = _mm(u, dgate_ab, mode="tn", out_dtype=F32, name="d_w_g")
    du = _mm(dsmall, w_small, mode="nt", out_dtype=F32, name="d_u_small")
    du = _mm(dfq, w_attn[:, 0:1024], mode="nt", out_dtype=F32, name="d_u_fq", acc=du)
    du = _mm(dfk, w_attn[:, 1024:2048], mode="nt", out_dtype=F32, name="d_u_fk", acc=du)
    du = _mm(dfv, w_attn[:, 2048:3072], mode="nt", out_dtype=F32, name="d_u_fv", acc=du)
    du = _mm(dgate_z, w_gate[:, 0:2048], mode="nt", out_dtype=F32, name="d_u_z", acc=du)
    du = _mm(dgate_ab, w_gate[:, 2048:4096], mode="nt", out_dtype=F32, name="d_u_g", acc=du)
    dx, dmeta, dg_pre = _pre_bwd(du, h, dy, pre_norm_g, s_rows)

    d_w_in = jnp.concatenate([dw_small[:, 0:416], dw_z[:, 0:1024], dw_fq, dw_fk, dw_fv, dw_small[:, 512:528],
                              dw_z[:, 1024:2048], dw_g], axis=1)
    d_w_uq = _uq_restore(d_w_uq_a)
    d_w_ukv = _ukv_restore(d_w_ukv_a)
    return (loss_p, dx, dmeta, d_w_in, d_w_uq, d_w_ukv, d_w_bm, d_w_bf, d_w_out, dg_pre, dg_post, dg_q, dg_kv, dfb)


def kernel(x, meta_tokens, pre_norm_g, w_in, fox_forget_b, mla_q_norm_g, mla_kv_norm_g, w_uq, w_ukv, w_br_mla, w_br_fox, w_out, post_norm_g, loss_target, m_meta_tokens, m_pre_norm_g, m_w_in, m_fox_forget_b, m_mla_q_norm_g, m_mla_kv_norm_g, m_w_uq, m_w_ukv, m_w_br_mla, m_w_br_fox, m_w_out, m_post_norm_g, v_meta_tokens, v_pre_norm_g, v_w_in, v_fox_forget_b, v_mla_q_norm_g, v_mla_kv_norm_g, v_w_uq, v_w_ukv, v_w_br_mla, v_w_br_fox, v_w_out, v_post_norm_g):
    p2 = _pack_p2(w_uq[0], w_ukv[0], w_br_mla[0], w_br_fox[0], w_out[0], jnp.zeros((N_META, 256), F32), BF16)
    w_in_g, p2_g, meta_g = _gather_weights([w_in[0].astype(BF16), p2], meta_tokens)
    slabs = [w_in_g[j] for j in range(N_CHIPS)]
    pieces = [_unpack_p2(p2_g[j]) for j in range(N_CHIPS)]
    w_uq_f = jnp.concatenate([p[0] for p in pieces], axis=1)
    w_ukv_f = jnp.concatenate([p[1] for p in pieces], axis=1)
    w_bm = jnp.concatenate([p[2] for p in pieces], axis=0)
    w_bf = jnp.concatenate([p[3] for p in pieces], axis=0)
    w_o = jnp.concatenate([p[4] for p in pieces], axis=0)
    meta_f = jnp.concatenate([meta_g[j] for j in range(N_CHIPS)], axis=1)
    kpe = _in_cols(slabs, C_KPE, C_ZMLA)
    w_small = jnp.concatenate(_in_cols(slabs, C_CQ, C_KPE) + kpe + kpe + [jnp.zeros((D_MODEL, 64), BF16)]
                              + _in_cols(slabs, C_FL, C_ZFOX) + [jnp.zeros((D_MODEL, 112), BF16)], axis=1)
    w_attn = jnp.concatenate(_in_cols(slabs, C_FQ, C_FL), axis=1)
    w_gate = jnp.concatenate(_in_cols(slabs, C_ZMLA, C_FQ) + _in_cols(slabs, C_ZFOX, C_END), axis=1)

    (loss_p, dx, dmeta, d_w_in, d_w_uq, d_w_ukv, d_w_bm, d_w_bf, d_w_out, dg_pre, dg_post, dg_q, dg_kv,
     dfb) = _local_step(x[0], loss_target[0], meta_f, w_small, w_attn, w_gate, w_uq_f, w_ukv_f, w_bm, w_bf, w_o,
                        pre_norm_g, post_norm_g, mla_q_norm_g, mla_kv_norm_g, fox_forget_b)

    g1 = jnp.transpose(d_w_in.reshape(D_MODEL, N_CHIPS, W_IN_SHARD), (1, 0, 2))
    g2 = jnp.stack([_pack_p2(d_w_uq[:, 384 * j:384 * (j + 1)], d_w_ukv[:, 512 * j:512 * (j + 1)],
                             d_w_bm[256 * j:256 * (j + 1)], d_w_bf[256 * j:256 * (j + 1)],
                             d_w_out[256 * j:256 * (j + 1)], dmeta[:, 256 * j:256 * (j + 1)], F32)
                    for j in range(N_CHIPS)], axis=0)
    s1, s2 = _swap_halves([g1, g2])
    landed = _scatter_chips([_add_cores(g1, s1, "add_cores_w_in"), _add_cores(g2, s2, "add_cores_rest")])
    g_w_in, g_p2 = _join_halves([_add_chips(landed[0], "add_chips_w_in"), _add_chips(landed[1], "add_chips_rest")])
    g_w_uq, g_w_ukv, g_w_bm, g_w_bf, g_w_out, g_meta = _unpack_p2(g_p2)

    vec = jnp.concatenate([dg_pre.reshape(8, 128), dg_post.reshape(8, 128), dg_q.reshape(2, 128), dg_kv,
                           dfb, _pad_lanes(loss_p), jnp.zeros((3, 128), F32)], axis=0)
    tot = _allreduce_small(vec)
    loss = tot[20, 0]

    def small_pack(pre, post, gq_, gkv_, fb_):
        return jnp.concatenate([pre.reshape(8, 128), post.reshape(8, 128), gq_.reshape(2, 128), gkv_,
                                _pad_lanes(fb_), jnp.zeros((4, 128), F32)], axis=0)

    def small_unpack(t):
        return (t[0:8].reshape(1, 1024), t[8:16].reshape(1, 1024), t[16:18].reshape(1, 256), t[18:19],
                t[19:20, 0:HEADS])

    g_small = jnp.concatenate([tot[0:20], jnp.zeros((4, 128), F32)], axis=0)
    sm = _adamw(small_pack(pre_norm_g, post_norm_g, mla_q_norm_g, mla_kv_norm_g, fox_forget_b), g_small,
                small_pack(m_pre_norm_g, m_post_norm_g, m_mla_q_norm_g, m_mla_kv_norm_g, m_fox_forget_b),
                small_pack(v_pre_norm_g, v_post_norm_g, v_mla_q_norm_g, v_mla_kv_norm_g, v_fox_forget_b),
                "adamw_small")
    g_pre, g_post, g_q, g_kv, g_fb = small_unpack(g_small)
    (d_pre, d_post, d_q, d_kv, d_fb), (nm_pre, nm_post, nm_q, nm_kv, nm_fb), (nv_pre, nv_post, nv_q, nv_kv, nv_fb) = (
        small_unpack(t) for t in sm)

    d_meta, nm_meta, nv_meta = _adamw(meta_tokens, g_meta, m_meta_tokens, v_meta_tokens, "adamw_meta")
    d_win, nm_win, nv_win = _adamw(w_in[0], g_w_in, m_w_in[0], v_w_in[0], "adamw_w_in")
    d_wuq, nm_wuq, nv_wuq = _adamw(w_uq[0], g_w_uq, m_w_uq[0], v_w_uq[0], "adamw_w_uq")
    d_wukv, nm_wukv, nv_wukv = _adamw(w_ukv[0], g_w_ukv, m_w_ukv[0], v_w_ukv[0], "adamw_w_ukv")
    d_wbm, nm_wbm, nv_wbm = _adamw(w_br_mla[0], g_w_bm, m_w_br_mla[0], v_w_br_mla[0], "adamw_w_br_mla")
    d_wbf, nm_wbf, nv_wbf = _adamw(w_br_fox[0], g_w_bf, m_w_br_fox[0], v_w_br_fox[0], "adamw_w_br_fox")
    d_wo, nm_wo, nv_wo = _adamw(w_out[0], g_w_out, m_w_out[0], v_w_out[0], "adamw_w_out")

    def group(meta_, pre, win, fb_, q_, kv_, wuq, wukv, wbm, wbf, wo, post):
        return (meta_, pre, win[None], fb_, q_, kv_, wuq[None], wukv[None], wbm[None], wbf[None], wo[None], post)

    grads = group(g_meta, g_pre, g_w_in, g_fb, g_q, g_kv, g_w_uq, g_w_ukv, g_w_bm, g_w_bf, g_w_out, g_post)
    deltas = group(d_meta, d_pre, d_win, d_fb, d_q, d_kv, d_wuq, d_wukv, d_wbm, d_wbf, d_wo, d_post)
    new_m = group(nm_meta, nm_pre, nm_win, nm_fb, nm_q, nm_kv, nm_wuq, nm_wukv, nm_wbm, nm_wbf, nm_wo, nm_post)
    new_v = group(nv_meta, nv_pre, nv_win, nv_fb, nv_q, nv_kv, nv_wuq, nv_wukv, nv_wbm, nv_wbf, nv_wo, nv_post)
    return (loss, dx[None], *grads, *deltas, *new_m, *new_v)
```

```python
import math

import jax
import jax.numpy as jnp
from jax import lax
from jax.experimental import pallas as pl
from jax.experimental.pallas import tpu as pltpu

F32 = jnp.float32
BF16 = jnp.bfloat16

D_MODEL = 1024
N_META = 16
RMS_EPS = 1e-6
HEADS = 16
PAIRS = HEADS // 2
HEAD_DIM = 64
LANES = 128
MLA_ROPE = 32
MLA_SCALE = 1.0 / math.sqrt(64 + 32)
FOX_SCALE = 1.0 / math.sqrt(64)
ROPE_THETA = 10000.0

PAD = 256
BLK = 256
NEG = -1e30

C_CQ, C_CKV, C_KPE, C_ZMLA, C_FQ, C_FK, C_FV, C_FL, C_ZFOX, C_GA, C_GB, C_END = (
    0, 256, 384, 416, 1440, 2464, 3488, 4512, 4528, 5552, 6576, 7600)
SMALL_W = 640
W_IN_SHARD = 1900

P2_ROWS = 960
N_CHIPS = 4

ADAM_LR = 0.001
ADAM_B1 = 0.9
ADAM_B2 = 0.999
ADAM_EPS = 1e-08
ADAM_WD = 0.01
ADAM_STEP = 10

VMEM_BIG = 56 * 1024 * 1024
MESH = pl.DeviceIdType.MESH


def _cp(dims, vmem=None):
    return pltpu.CompilerParams(dimension_semantics=dims, vmem_limit_bytes=vmem)


def _dot(a, b, ca, cb):
    return lax.dot_general(a, b, (((ca,), (cb,)), ((), ())), preferred_element_type=F32)


def _sigmoid(x):
    return 1.0 / (1.0 + jnp.exp(-x))


def _tile(n, cands):
    for c in cands:
        if n % c == 0:
            return c
    return n


def _mm(a, b, *, mode, out_dtype, name, acc=None):
    if mode == "nn":
        (M, K), N = a.shape, b.shape[1]
    elif mode == "nt":
        (M, K), N = a.shape, b.shape[0]
    else:
        (K, M), N = a.shape, b.shape[1]
    tm = _tile(M, (1088, 1024)) if M > 1024 else M
    tn = _tile(N, (1024,)) if N > 1024 else N
    tk = _tile(K, (1088, 1024)) if K > 1088 else K
    nk = K // tk
    ca, cb = {"nn": (1, 0), "nt": (1, 1), "tn": (0, 0)}[mode]
    a_spec = (pl.BlockSpec((tk, tm), lambda j, i, k: (k, i)) if mode == "tn"
              else pl.BlockSpec((tm, tk), lambda j, i, k: (i, k)))
    b_spec = (pl.BlockSpec((tn, tk), lambda j, i, k: (j, k)) if mode == "nt"
              else pl.BlockSpec((tk, tn), lambda j, i, k: (k, j)))
    o_spec = pl.BlockSpec((tm, tn), lambda j, i, k: (i, j))
    has_acc = acc is not None

    def body(*refs):
        a_ref, b_ref = refs[0], refs[1]
        acc_ref = refs[2] if has_acc else None
        o_ref = refs[3] if has_acc else refs[2]
        part = _dot(a_ref[...].astype(BF16), b_ref[...].astype(BF16), ca, cb)
        if nk == 1:
            if has_acc:
                part = part + acc_ref[...]
            o_ref[...] = part.astype(out_dtype)
        else:
            sc = refs[-1]
            k = pl.program_id(2)

            @pl.when(k == 0)
            def _():
                sc[...] = part + acc_ref[...] if has_acc else part

            @pl.when(k > 0)
            def _():
                sc[...] += part

            @pl.when(k == nk - 1)
            def _():
                o_ref[...] = sc[...].astype(out_dtype)

    ins = [a, b] + ([acc] if has_acc else [])
    in_specs = [a_spec, b_spec] + ([o_spec] if has_acc else [])
    return pl.pallas_call(
        body, name=name, grid=(N // tn, M // tm, nk), in_specs=in_specs, out_specs=o_spec,
        out_shape=jax.ShapeDtypeStruct((M, N), out_dtype),
        scratch_shapes=[pltpu.VMEM((tm, tn), F32)] if nk > 1 else [],
        compiler_params=_cp(("parallel", "parallel", "arbitrary"), VMEM_BIG))(*ins)


def _row(w):
    return pl.BlockSpec((BLK, w), lambda i: (i, 0))


def _rowc(w, c):
    return pl.BlockSpec((BLK, w), lambda i: (i, c))


def _full(shape):
    return pl.BlockSpec(shape, lambda i: tuple(0 for _ in shape))


def _rope(x, c, s):
    lane = lax.broadcasted_iota(jnp.int32, x.shape, 1)
    is_x1 = ((lane >> 4) & 1) == 0
    partner = jnp.where(is_x1, pltpu.roll(x, LANES - 16, 1), pltpu.roll(x, 16, 1))
    return x * c + partner * s


def _row_valid(i):
    rows = i * BLK + lax.broadcasted_iota(jnp.int32, (BLK, 1), 0)
    return (rows < N_META) | (rows >= PAD)


def _rms_pre(h, g):
    lp = h.shape[0]

    def body(h_ref, g_ref, u_ref):
        hv = h_ref[...]
        r = lax.rsqrt(jnp.mean(hv * hv, axis=-1, keepdims=True) + RMS_EPS)
        u_ref[...] = (hv * r * g_ref[...]).astype(BF16)

    return pl.pallas_call(
        body, name="rms_pre", grid=(lp // BLK,),
        in_specs=[_row(D_MODEL), _full((1, D_MODEL))], out_specs=_row(D_MODEL),
        out_shape=jax.ShapeDtypeStruct((lp, D_MODEL), BF16),
        compiler_params=_cp(("parallel",)))(h, g)


def _split3(x):
    hi = x.astype(BF16)
    r1 = x - hi.astype(F32)
    mid = r1.astype(BF16)
    lo = (r1 - mid.astype(F32)).astype(BF16)
    return hi, mid, lo


def _small_prep(small, gq, gkv, fb, ctab, stab, tri):
    lp = small.shape[0]

    def body(sm_ref, gq_ref, gkv_ref, fb_ref, c_ref, s_ref, tri_ref, qn_ref, kvn_ref, kr_ref, ncum_ref, carry):
        i = pl.program_id(0)

        @pl.when(i == 0)
        def _():
            carry[...] = jnp.zeros_like(carry)

        cq = sm_ref[:, 0:256]
        r = lax.rsqrt(jnp.mean(cq * cq, axis=-1, keepdims=True) + RMS_EPS)
        qn_ref[...] = (cq * r * gq_ref[...]).astype(BF16)
        ckv = sm_ref[:, 256:384]
        r = lax.rsqrt(jnp.mean(ckv * ckv, axis=-1, keepdims=True) + RMS_EPS)
        kvn_ref[...] = (ckv * r * gkv_ref[...]).astype(BF16)
        kr_ref[...] = _rope(sm_ref[:, 384:512], c_ref[...], s_ref[...]).astype(BF16)
        fl = sm_ref[:, 512:640] + fb_ref[...]
        lf = jnp.minimum(fl, 0.0) - jnp.log(1.0 + jnp.exp(-jnp.abs(fl)))
        lf = jnp.where(_row_valid(i), lf, 0.0)
        hi, mid, lo = _split3(lf)
        t = tri_ref[...]
        cum = (_dot(t, hi, 1, 0) + _dot(t, mid, 1, 0)) + _dot(t, lo, 1, 0) + carry[...]
        ncum_ref[...] = -cum
        carry[...] = -ncum_ref[BLK - 1:BLK, :]

    return pl.pallas_call(
        body, name="small_prep", grid=(lp // BLK,),
        in_specs=[_row(SMALL_W), _full((1, 256)), _full((1, 128)), _full((1, 128)), _row(128), _row(128),
                  _full((BLK, BLK))],
        out_specs=[_row(256), _row(128), _row(128), _row(128)],
        out_shape=[jax.ShapeDtypeStruct((lp, 256), BF16), jax.ShapeDtypeStruct((lp, 128), BF16),
                   jax.ShapeDtypeStruct((lp, 128), BF16), jax.ShapeDtypeStruct((lp, 128), F32)],
        scratch_shapes=[pltpu.VMEM((1, 128), F32)],
        compiler_params=_cp(("arbitrary",)))(small, gq, gkv, fb, ctab, stab, tri)


def _rope_q(qraw, ctab, stab, *, inverse, out_dtype, name):
    lp = qraw.shape[0]

    def body(q_ref, c_ref, s_ref, o_ref):
        c = c_ref[...]
        s = -s_ref[...] if inverse else s_ref[...]
        for p in range(PAIRS):
            lo = p * 256
            o_ref[:, lo:lo + 128] = q_ref[:, lo:lo + 128].astype(out_dtype)
            o_ref[:, lo + 128:lo + 256] = _rope(q_ref[:, lo + 128:lo + 256].astype(F32), c, s).astype(out_dtype)

    return pl.pallas_call(
        body, name=name, grid=(lp // BLK,),
        in_specs=[_row(PAIRS * 256), _row(128), _row(128)], out_specs=_row(PAIRS * 256),
        out_shape=jax.ShapeDtypeStruct((lp, PAIRS * 256), out_dtype),
        compiler_params=_cp(("parallel",)))(qraw, ctab, stab)


def _gate_fwd(o_mla, o_fox, gate):
    lp = o_mla.shape[0]

    def body(om_ref, of_ref, zm_ref, zf_ref, am_ref, af_ref):
        zm = zm_ref[...].astype(F32)
        am_ref[...] = (om_ref[...] * (zm * _sigmoid(zm))).astype(BF16)
        zf = zf_ref[...].astype(F32)
        af_ref[...] = (of_ref[...] * (zf * _sigmoid(zf))).astype(BF16)

    return pl.pallas_call(
        body, name="gate_fwd", grid=(lp // BLK,),
        in_specs=[_row(D_MODEL), _row(D_MODEL), _rowc(D_MODEL, 0), _rowc(D_MODEL, 1)],
        out_specs=[_row(D_MODEL), _row(D_MODEL)],
        out_shape=[jax.ShapeDtypeStruct((lp, D_MODEL), BF16)] * 2,
        compiler_params=_cp(("parallel",)))(o_mla, o_fox, gate, gate)


def _merge_fwd(gate, y_mla, y_fox):
    lp = y_mla.shape[0]

    def body(ga_ref, gb_ref, ym_ref, yf_ref, m_ref):
        sa = _sigmoid(ga_ref[...].astype(F32))
        sb = _sigmoid(gb_ref[...].astype(F32))
        m_ref[...] = (sa * ym_ref[...] + sb * yf_ref[...]).astype(BF16)

    return pl.pallas_call(
        body, name="merge_fwd", grid=(lp // BLK,),
        in_specs=[_rowc(D_MODEL, 2), _rowc(D_MODEL, 3), _row(D_MODEL), _row(D_MODEL)],
        out_specs=_row(D_MODEL), out_shape=jax.ShapeDtypeStruct((lp, D_MODEL), BF16),
        compiler_params=_cp(("parallel",)))(gate, gate, y_mla, y_fox)


def _tail(h, mixed, tgt, gpost):
    lp = h.shape[0]
    shift = pl.BlockSpec((BLK, D_MODEL), lambda i: (jnp.maximum(i - 1, 0), 0))

    def body(h_ref, mx_ref, t_ref, g_ref, dmx_ref, dy_ref, loss_ref, dg_ref):
        i = pl.program_id(0)

        @pl.when(i == 0)
        def _():
            loss_ref[...] = jnp.zeros_like(loss_ref)
            dg_ref[...] = jnp.zeros_like(dg_ref)
            dmx_ref[...] = jnp.zeros_like(dmx_ref)
            dy_ref[...] = jnp.zeros_like(dy_ref)

        @pl.when(i > 0)
        def _():
            mx = mx_ref[...]
            g = g_ref[...]
            r = lax.rsqrt(jnp.mean(mx * mx, axis=-1, keepdims=True) + RMS_EPS)
            nrm = mx * r
            e = (h_ref[...] + nrm * g) - t_ref[...]
            loss_ref[...] += jnp.sum(0.5 * jnp.sum(e * e, axis=-1, keepdims=True) * (1.0 / D_MODEL),
                                     axis=0, keepdims=True)
            dy = e * (1.0 / D_MODEL)
            dy_ref[...] = dy
            dg_ref[...] += jnp.sum(dy * nrm, axis=0, keepdims=True)
            w = dy * g
            dot = jnp.mean(w * mx, axis=-1, keepdims=True)
            dmx_ref[...] = (r * w - mx * (r * r * r * dot)).astype(BF16)

    return pl.pallas_call(
        body, name="tail", grid=(lp // BLK,),
        in_specs=[_row(D_MODEL), _row(D_MODEL), shift, _full((1, D_MODEL))],
        out_specs=[_row(D_MODEL), _row(D_MODEL), _full((1, 1)), _full((1, D_MODEL))],
        out_shape=[jax.ShapeDtypeStruct((lp, D_MODEL), BF16), jax.ShapeDtypeStruct((lp, D_MODEL), F32),
                   jax.ShapeDtypeStruct((1, 1), F32), jax.ShapeDtypeStruct((1, D_MODEL), F32)],
        compiler_params=_cp(("arbitrary",)))(h, mixed, tgt, gpost)


def _merge_bwd(dm, gate, y_mla, y_fox):
    lp = dm.shape[0]

    def body(dm_ref, ga_ref, gb_ref, ym_ref, yf_ref, dym_ref, dyf_ref, dg_ref):
        dm_v = dm_ref[...]
        sa = _sigmoid(ga_ref[...].astype(F32))
        sb = _sigmoid(gb_ref[...].astype(F32))
        dym_ref[...] = (dm_v * sa).astype(BF16)
        dyf_ref[...] = (dm_v * sb).astype(BF16)
        dg_ref[:, 0:D_MODEL] = (dm_v * ym_ref[...] * (sa * (1.0 - sa))).astype(BF16)
        dg_ref[:, D_MODEL:2 * D_MODEL] = (dm_v * yf_ref[...] * (sb * (1.0 - sb))).astype(BF16)

    return pl.pallas_call(
        body, name="merge_bwd", grid=(lp // BLK,),
        in_specs=[_row(D_MODEL), _rowc(D_MODEL, 2), _rowc(D_MODEL, 3), _row(D_MODEL), _row(D_MODEL)],
        out_specs=[_row(D_MODEL), _row(D_MODEL), _row(2 * D_MODEL)],
        out_shape=[jax.ShapeDtypeStruct((lp, D_MODEL), BF16), jax.ShapeDtypeStruct((lp, D_MODEL), BF16),
                   jax.ShapeDtypeStruct((lp, 2 * D_MODEL), BF16)],
        compiler_params=_cp(("parallel",)))(dm, gate, gate, y_mla, y_fox)


def _gate_bwd(da_mla, da_fox, o_mla, o_fox, gate):
    lp = da_mla.shape[0]

    def one(da, o, z):
        sg = _sigmoid(z)
        do = da * (z * sg)
        dz = da * o * (sg * (1.0 + z * (1.0 - sg)))
        return do.astype(BF16), dz.astype(BF16)

    def body(dam_ref, daf_ref, om_ref, of_ref, zm_ref, zf_ref, dom_ref, dof_ref, dz_ref):
        dom_ref[...], dz_ref[:, 0:D_MODEL] = one(dam_ref[...], om_ref[...], zm_ref[...].astype(F32))
        dof_ref[...], dz_ref[:, D_MODEL:2 * D_MODEL] = one(daf_ref[...], of_ref[...], zf_ref[...].astype(F32))

    return pl.pallas_call(
        body, name="gate_bwd", grid=(lp // BLK,),
        in_specs=[_row(D_MODEL)] * 4 + [_rowc(D_MODEL, 0), _rowc(D_MODEL, 1)],
        out_specs=[_row(D_MODEL), _row(D_MODEL), _row(2 * D_MODEL)],
        out_shape=[jax.ShapeDtypeStruct((lp, D_MODEL), BF16), jax.ShapeDtypeStruct((lp, D_MODEL), BF16),
                   jax.ShapeDtypeStruct((lp, 2 * D_MODEL), BF16)],
        compiler_params=_cp(("parallel",)))(da_mla, da_fox, o_mla, o_fox, gate, gate)


def _small_bwd(small, dqn, dkvn, dkr, dcol_t, drow_t, gq, gkv, fb, ctab, stab, triu):
    lp = small.shape[0]
    nb = lp // BLK

    def rrow(w):
        return pl.BlockSpec((BLK, w), lambda i: (nb - 1 - i, 0))

    def body(sm_ref, dqn_ref, dkvn_ref, dkr_ref, dcol_ref, drow_ref, gq_ref, gkv_ref, fb_ref, c_ref, s_ref, tri_ref,
             ds_ref, dgq_ref, dgkv_ref, dfb_ref, carry):
        i = pl.program_id(0)

        @pl.when(i == 0)
        def _():
            carry[...] = jnp.zeros_like(carry)
            dgq_ref[...] = jnp.zeros_like(dgq_ref)
            dgkv_ref[...] = jnp.zeros_like(dgkv_ref)
            dfb_ref[...] = jnp.zeros_like(dfb_ref)

        def norm_bwd(x, dn, g, dg_ref):
            r = lax.rsqrt(jnp.mean(x * x, axis=-1, keepdims=True) + RMS_EPS)
            dg_ref[...] += jnp.sum(dn * (x * r), axis=0, keepdims=True)
            w = dn * g
            dot = jnp.mean(w * x, axis=-1, keepdims=True)
            return r * w - x * (r * r * r * dot)

        ds_ref[:, 0:256] = norm_bwd(sm_ref[:, 0:256], dqn_ref[...], gq_ref[...], dgq_ref).astype(BF16)
        ds_ref[:, 256:384] = norm_bwd(sm_ref[:, 256:384], dkvn_ref[...], gkv_ref[...], dgkv_ref).astype(BF16)

        dk = dkr_ref[0]
        for p in range(1, PAIRS):
            dk = dk + dkr_ref[p]
        dk = _rope(dk, c_ref[...], -s_ref[...])
        lane = lax.broadcasted_iota(jnp.int32, dk.shape, 1)
        dk = jnp.where(lane < MLA_ROPE, dk + pltpu.roll(dk, LANES - MLA_ROPE, 1), 0.0)
        ds_ref[:, 384:512] = dk.astype(BF16)

        dcr = dcol_ref[...] - drow_ref[...]
        hi, mid, lo = _split3(dcr)
        t = tri_ref[...]
        suf = (_dot(t, hi, 1, 0) + _dot(t, mid, 1, 0)) + _dot(t, lo, 1, 0) + carry[...]
        fl = sm_ref[:, 512:640] + fb_ref[...]
        dfl = jnp.where(_row_valid(nb - 1 - i), -suf * _sigmoid(-fl), 0.0)
        ds_ref[:, 512:640] = dfl.astype(BF16)
        dfb_ref[...] += jnp.sum(dfl, axis=0, keepdims=True)
        carry[...] += jnp.sum(dcr, axis=0, keepdims=True)

    return pl.pallas_call(
        body, name="small_bwd", grid=(nb,),
        in_specs=[rrow(SMALL_W), rrow(256), rrow(128),
                  pl.BlockSpec((PAIRS, BLK, 128), lambda i: (0, nb - 1 - i, 0)), rrow(128), rrow(128),
                  _full((1, 256)), _full((1, 128)), _full((1, 128)), rrow(128), rrow(128), _full((BLK, BLK))],
        out_specs=[rrow(SMALL_W), _full((1, 256)), _full((1, 128)), _full((1, 128))],
        out_shape=[jax.ShapeDtypeStruct((lp, SMALL_W), BF16), jax.ShapeDtypeStruct((1, 256), F32),
                   jax.ShapeDtypeStruct((1, 128), F32), jax.ShapeDtypeStruct((1, 128), F32)],
        scratch_shapes=[pltpu.VMEM((1, 128), F32)],
        compiler_params=_cp(("arbitrary",)))(small, dqn, dkvn, dkr, dcol_t, drow_t, gq, gkv, fb, ctab, stab, triu)


def _pre_bwd(du, h, dy, gpre, s_rows):
    lp = h.shape[0]
    shift = pl.BlockSpec((BLK, D_MODEL), lambda i: (jnp.maximum(i - 1, 0), 0))

    def body(du_ref, h_ref, dy_ref, g_ref, dx_ref, dmeta_ref, dg_ref):
        i = pl.program_id(0)

        @pl.when(i == 0)
        def _():
            dg_ref[...] = jnp.zeros_like(dg_ref)

        hv = h_ref[...]
        duv = du_ref[...]
        r = lax.rsqrt(jnp.mean(hv * hv, axis=-1, keepdims=True) + RMS_EPS)
        dg_ref[...] += jnp.sum(duv * (hv * r), axis=0, keepdims=True)
        w = duv * g_ref[...]
        dot = jnp.mean(w * hv, axis=-1, keepdims=True)
        dh = dy_ref[...] + (r * w - hv * (r * r * r * dot))
        dx_ref[...] = dh

        @pl.when(i == 0)
        def _():
            dmeta_ref[...] = dh[0:N_META, :]

    return pl.pallas_call(
        body, name="pre_bwd", grid=(lp // BLK,),
        in_specs=[_row(D_MODEL), _row(D_MODEL), _row(D_MODEL), _full((1, D_MODEL))],
        out_specs=[shift, _full((N_META, D_MODEL)), _full((1, D_MODEL))],
        out_shape=[jax.ShapeDtypeStruct((s_rows, D_MODEL), F32), jax.ShapeDtypeStruct((N_META, D_MODEL), F32),
                   jax.ShapeDtypeStruct((1, D_MODEL), F32)],
        compiler_params=_cp(("arbitrary",)))(du, h, dy, gpre)


def _pair_masks(rope):
    lane = lax.broadcasted_iota(jnp.int32, (1, LANES), 1)
    mas = [lane < HEAD_DIM, lane >= HEAD_DIM]
    if not rope:
        return mas, mas
    wide = lax.broadcasted_iota(jnp.int32, (1, 2 * LANES), 1)
    rope_lo = LANES + MLA_ROPE
    return mas, [(wide < HEAD_DIM) | ((wide >= LANES) & (wide < rope_lo)),
                 ((wide >= HEAD_DIM) & (wide < LANES)) | ((wide >= rope_lo) & (wide < rope_lo + MLA_ROPE))]


def _mask2(x, masks):
    return [jnp.where(m, x, jnp.zeros_like(x)) for m in masks]


def _attn_fwd(q, k, v, *, kr=None, nbrep=None, scale, qcol, kcol, vcol, name):
    lp = q.shape[0]
    nb = lp // BLK
    rope = kr is not None
    bias = nbrep is not None
    qw = 256 if rope else 128

    def body(*refs):
        it = iter(refs)
        q_ref, k_ref, v_ref = next(it), next(it), next(it)
        kr_ref = next(it) if rope else None
        nb_ref = next(it) if bias else None
        o_ref, lse_ref = next(it), next(it)
        i = pl.program_id(1)
        mas, hmask = _pair_masks(rope)
        qh = _mask2(q_ref[...], hmask)
        if bias:
            qh = [x * scale for x in qh]
        def causal(n):
            return (lax.broadcasted_iota(jnp.int32, (n, BLK), 0) <= lax.broadcasted_iota(jnp.int32, (n, BLK), 1))

        meta_mask = causal(N_META) | (i > 0)
        diag_mask = causal(BLK) & (i > 0)

        def scores(kc, n=BLK):
            k0 = pl.multiple_of(kc * BLK, BLK)
            kk = k_ref[pl.ds(k0, n), :]
            if rope:
                kk = jnp.concatenate([kk, kr_ref[pl.ds(k0, n), :]], axis=1)
            out = []
            for h in range(2):
                s = _dot(kk, qh[h], 1, 1)
                out.append(s * scale if rope else s)
            return out

        def update(kc, ss, carry, mask, n=BLK):
            stats, acc = carry[:4], carry[4]
            k0 = pl.multiple_of(kc * BLK, BLK)
            vv = v_ref[pl.ds(k0, n), :]
            new_stats, alphas, ps = [], [], []
            for h in range(2):
                m_prev, l_prev = stats[2 * h], stats[2 * h + 1]
                s = ss[h]
                if bias:
                    nbc = nb_ref[h, pl.ds(k0, n), :]
                    s = s + jnp.concatenate([nbc, nbc], axis=1)
                if mask is not None:
                    s = jnp.where(mask, s, NEG)
                m_new = jnp.maximum(m_prev, jnp.max(s, axis=0, keepdims=True))
                alpha = jnp.exp(m_prev - m_new)
                p = jnp.exp(s - m_new)
                new_stats += [m_new, alpha * l_prev + jnp.sum(p, axis=0, keepdims=True)]
                alphas.append(alpha)
                ps.append(p.astype(BF16))
            vcat = jnp.concatenate(_mask2(vv, mas), axis=0)
            pv = _dot(vcat, jnp.concatenate(ps, axis=0), 0, 0)
            a_full = jnp.concatenate([jnp.broadcast_to(a, (HEAD_DIM, BLK)) for a in alphas], axis=0)
            return (*new_stats, a_full * acc + pv)

        neg = jnp.full((1, BLK), NEG, F32)
        zero = jnp.zeros((1, BLK), F32)
        s_nxt = scores(jnp.minimum(1, i))
        c = update(0, scores(0, N_META), (neg, zero, neg, zero, jnp.zeros((LANES, BLK), F32)), meta_mask, N_META)

        def step(kc, cr):
            nxt = scores(kc + 1)
            return (*nxt, *update(kc, cr[:2], cr[2:], None))

        cr = lax.fori_loop(1, i, step, (*s_nxt, *c))
        c = update(i, cr[:2], cr[2:], diag_mask)
        inv = jnp.concatenate([jnp.broadcast_to(1.0 / c[1], (HEAD_DIM, BLK)),
                               jnp.broadcast_to(1.0 / c[3], (HEAD_DIM, BLK))], axis=0)
        o_ref[...] = (c[4] * inv).T
        lse_ref[0, 0:1, :] = c[0] + jnp.log(c[1])
        lse_ref[0, 1:2, :] = c[2] + jnp.log(c[3])

    in_specs = [pl.BlockSpec((BLK, qw), lambda p, i: (i, qcol + p)),
                pl.BlockSpec((lp, 128), lambda p, i: (0, kcol(p))),
                pl.BlockSpec((lp, 128), lambda p, i: (0, vcol(p)))]
    ins = [q, k, v]
    if rope:
        in_specs.append(pl.BlockSpec((lp, 128), lambda p, i: (0, 0)))
        ins.append(kr)
    if bias:
        in_specs.append(pl.BlockSpec((2, lp, 128), lambda p, i: (p, 0, 0)))
        ins.append(nbrep)
    return pl.pallas_call(
        body, name=name, grid=(PAIRS, nb), in_specs=in_specs,
        out_specs=[pl.BlockSpec((BLK, 128), lambda p, i: (i, p)),
                   pl.BlockSpec((1, 2, BLK), lambda p, i: (p, 0, i))],
        out_shape=[jax.ShapeDtypeStruct((lp, D_MODEL), F32), jax.ShapeDtypeStruct((PAIRS, 2, lp), F32)],
        compiler_params=_cp(("parallel", "arbitrary"), VMEM_BIG))(*ins)


def _attn_bwd(q, k, v, do, o, lse, *, kr=None, nbrep=None, scale, qcol, kcol, vcol, name):
    lp = q.shape[0]
    nb = lp // BLK
    rope = kr is not None
    bias = nbrep is not None
    qw = 256 if rope else 128

    def body(*refs):
        it = iter(refs)
        q_ref, k_ref, v_ref = next(it), next(it), next(it)
        kr_ref = next(it) if rope else None
        nb_ref = next(it) if bias else None
        do_ref, o_ref, lse_ref = next(it), next(it), next(it)
        dq_ref, dk_ref, dv_ref = next(it), next(it), next(it)
        x_ref = next(it)
        drow_ref = next(it) if bias else None
        delta = next(it)
        kb = pl.program_id(1)
        mas, hmask = _pair_masks(rope)
        lane = lax.broadcasted_iota(jnp.int32, (1, LANES), 1)

        @pl.when(kb == 0)
        def _():
            dq_ref[...] = jnp.zeros_like(dq_ref)
            if bias:
                drow_ref[...] = jnp.zeros_like(drow_ref)
            sub = lax.broadcasted_iota(jnp.int32, (8, LANES), 0)
            sel = (((sub == 0) & mas[0]) | ((sub == 1) & mas[1])).astype(BF16)

            def dstep(c, carry):
                r0 = pl.multiple_of(c * BLK, BLK)
                prod = do_ref[pl.ds(r0, BLK), :].astype(F32) * o_ref[pl.ds(r0, BLK), :]
                hi, mid, lo = _split3(prod)
                delta[:, pl.ds(r0, BLK)] = (_dot(sel, hi, 1, 1) + _dot(sel, mid, 1, 1)) + _dot(sel, lo, 1, 1)
                return carry

            lax.fori_loop(0, nb, dstep, 0)

        def masked_q(q0):
            qh = _mask2(q_ref[pl.ds(q0, BLK), :], hmask)
            return [x * scale for x in qh] if bias else qh

        def key_pass(n):
            kk = k_ref[0:n, :]
            if rope:
                kk = jnp.concatenate([kk, kr_ref[0:n, :]], axis=1)
            vh = _mask2(v_ref[0:n, :], mas)
            kcat = jnp.concatenate(_mask2(kk, hmask), axis=0)
            if bias:
                kcat = kcat * scale
                nbc = [jnp.concatenate([nb_ref[h, 0:n, :], nb_ref[h, 0:n, :]], axis=1) for h in range(2)]
            diag_mask = (lax.broadcasted_iota(jnp.int32, (n, BLK), 0) <= lax.broadcasted_iota(jnp.int32, (n, BLK), 1))

            def chunk(qc, carry, mask):
                carry = list(carry)
                q0 = pl.multiple_of(qc * BLK, BLK)
                dov = do_ref[pl.ds(q0, BLK), :]
                doh = _mask2(dov, mas)
                qh = masked_q(q0)
                pbs, dss = [], []
                for h in range(2):
                    s = _dot(kk, qh[h], 1, 1)
                    if rope:
                        s = s * scale
                    if bias:
                        s = s + nbc[h]
                    p = jnp.exp(s - lse_ref[0, h:h + 1, pl.ds(q0, BLK)])
                    if mask is not None:
                        p = jnp.where(mask, p, 0.0)
                    ds = p * (_dot(vh[h], dov, 1, 1) - delta[h:h + 1, pl.ds(q0, BLK)])
                    if bias:
                        drow_ref[0, h:h + 1, pl.ds(q0, BLK)] += jnp.sum(ds, axis=0, keepdims=True)
                        carry[2 + h] = carry[2 + h] + jnp.sum(ds, axis=1, keepdims=True)
                    else:
                        ds = ds * scale
                    pbs.append(p.astype(BF16))
                    dss.append(ds.astype(BF16))
                ds_lanes = jnp.concatenate(dss, axis=1)
                ds_rows = jnp.concatenate(dss, axis=0)
                carry[0] = carry[0] + _dot(ds_lanes, jnp.concatenate(qh, axis=0), 1, 0)
                carry[1] = carry[1] + _dot(jnp.concatenate(pbs, axis=1), jnp.concatenate(doh, axis=0), 1, 0)
                dq_ref[pl.ds(q0, BLK), :] += _dot(ds_rows, kcat, 0, 0)
                return tuple(carry)

            init = [jnp.zeros((n, qw), F32), jnp.zeros((n, LANES), F32)]
            if bias:
                init += [jnp.zeros((n, 1), F32), jnp.zeros((n, 1), F32)]
            c = chunk(kb, tuple(init), diag_mask)
            c = lax.fori_loop(kb + 1, nb, lambda qc, cr: chunk(qc, cr, None), c)

            def rows(a, dtype):
                a = a.astype(dtype)
                return a if n == BLK else jnp.concatenate([a, jnp.zeros((BLK - n, a.shape[1]), dtype)], axis=0)

            dk_ref[...] = rows(c[0][:, 0:LANES], BF16)
            dv_ref[...] = rows(c[1], BF16)
            if rope:
                x_ref[0] = rows(c[0][:, LANES:2 * LANES], F32)
            if bias:
                x_ref[0] = rows(jnp.where(lane == 0, c[2], jnp.where(lane == 1, c[3], 0.0)), F32)

        @pl.when(kb == 0)
        def _():
            key_pass(N_META)

        @pl.when(kb > 0)
        def _():
            key_pass(BLK)

    in_specs = [pl.BlockSpec((lp, qw), lambda p, j: (0, qcol + p)),
                pl.BlockSpec((BLK, 128), lambda p, j: (j, kcol(p))),
                pl.BlockSpec((BLK, 128), lambda p, j: (j, vcol(p)))]
    ins = [q, k, v]
    if rope:
        in_specs.append(pl.BlockSpec((BLK, 128), lambda p, j: (j, 0)))
        ins.append(kr)
    if bias:
        in_specs.append(pl.BlockSpec((2, BLK, 128), lambda p, j: (p, j, 0)))
        ins.append(nbrep)
    in_specs += [pl.BlockSpec((lp, 128), lambda p, j: (0, p)), pl.BlockSpec((lp, 128), lambda p, j: (0, p)),
                 pl.BlockSpec((1, 2, lp), lambda p, j: (p, 0, 0))]
    ins += [do, o, lse]
    out_specs = [pl.BlockSpec((lp, qw), lambda p, j: (0, p)),
                 pl.BlockSpec((BLK, 128), lambda p, j: (j, p)),
                 pl.BlockSpec((BLK, 128), lambda p, j: (j, p)),
                 pl.BlockSpec((1, BLK, 128), lambda p, j: (p, j, 0))]
    out_shape = [jax.ShapeDtypeStruct((lp, PAIRS * qw), F32), jax.ShapeDtypeStruct((lp, D_MODEL), BF16),
                 jax.ShapeDtypeStruct((lp, D_MODEL), BF16), jax.ShapeDtypeStruct((PAIRS, lp, 128), F32)]
    if bias:
        out_specs.append(pl.BlockSpec((1, 2, lp), lambda p, j: (p, 0, 0)))
        out_shape.append(jax.ShapeDtypeStruct((PAIRS, 2, lp), F32))
    return pl.pallas_call(
        body, name=name, grid=(PAIRS, nb), in_specs=in_specs, out_specs=out_specs, out_shape=out_shape,
        scratch_shapes=[pltpu.VMEM((8, lp), F32)],
        compiler_params=_cp(("parallel", "arbitrary"), VMEM_BIG))(*ins)


def _adamw(w, g, m, v, name):
    lead = w.ndim - 2
    rows, cols = w.shape[lead:]
    tr = 128 if rows * cols > 512 * 1024 else rows

    def body(w_ref, g_ref, m_ref, v_ref, d_ref, nm_ref, nv_ref):
        gv = g_ref[...]
        nm = ADAM_B1 * m_ref[...] + (1.0 - ADAM_B1) * gv
        nv = ADAM_B2 * v_ref[...] + (1.0 - ADAM_B2) * (gv * gv)
        m_hat = nm / (1.0 - ADAM_B1 ** ADAM_STEP)
        v_hat = nv / (1.0 - ADAM_B2 ** ADAM_STEP)
        d_ref[...] = -ADAM_LR * (m_hat / (jnp.sqrt(v_hat) + ADAM_EPS) + ADAM_WD * w_ref[...])
        nm_ref[...] = nm
        nv_ref[...] = nv

    spec = pl.BlockSpec((1,) * lead + (tr, cols), lambda i: (0,) * lead + (i, 0))
    return pl.pallas_call(
        body, name=name, grid=(rows // tr,), in_specs=[spec] * 4, out_specs=[spec] * 3,
        out_shape=[jax.ShapeDtypeStruct(w.shape, F32)] * 3,
        compiler_params=_cp(("parallel",), VMEM_BIG))(w, g, m, v)


def _add_cores(g, from_sib, name):
    n, rows, cols = g.shape
    half = rows // 2
    tr = _tile(half, (256, 240))
    nt = half // tr

    def body(lo_ref, hi_ref, s_ref, o_ref):
        mine = jnp.where(lax.axis_index("c") == 0, lo_ref[0], hi_ref[0])
        o_ref[0] = (mine + s_ref[0]).astype(BF16)

    return pl.pallas_call(
        body, name=name, grid=(n, nt),
        in_specs=[pl.BlockSpec((1, tr, cols), lambda j, i: (j, i, 0)),
                  pl.BlockSpec((1, tr, cols), lambda j, i: (j, nt + i, 0)),
                  pl.BlockSpec((1, tr, cols), lambda j, i: (j, i, 0))],
        out_specs=pl.BlockSpec((1, tr, cols), lambda j, i: (j, i, 0)),
        out_shape=jax.ShapeDtypeStruct((n, half, cols), BF16),
        compiler_params=_cp(("parallel", "parallel"), VMEM_BIG))(g, g, from_sib)


def _add_chips(x, own, name):
    n, rows, cols = x.shape
    tr = _tile(rows, (256, 240))

    def body(x_ref, own_ref, o_ref):
        me = 2 * lax.axis_index("x") + lax.axis_index("y")
        v = [jnp.where(me == k, own_ref[...], x_ref[k]).astype(F32) for k in range(N_CHIPS)]
        o_ref[...] = ((v[0] + v[1]) + v[2]) + v[3]

    return pl.pallas_call(
        body, name=name, grid=(rows // tr,),
        in_specs=[pl.BlockSpec((n, tr, cols), lambda i: (0, i, 0)), pl.BlockSpec((tr, cols), lambda i: (i, 0))],
        out_specs=pl.BlockSpec((tr, cols), lambda i: (i, 0)),
        out_shape=jax.ShapeDtypeStruct((rows, cols), F32), compiler_params=_cp(("parallel",), VMEM_BIG))(x, own)


def _axes():
    return lax.axis_index("x"), lax.axis_index("y"), lax.axis_index("c")


def _other_chips(x, y):
    return [(1 - x, y), (x, 1 - y), (1 - x, 1 - y)]


ANY = pl.BlockSpec(memory_space=pl.ANY)


def _rcopy(src, dst, send_sems, recv_sems, k, to):
    return pltpu.make_async_remote_copy(src_ref=src, dst_ref=dst, send_sem=send_sems.at[k], recv_sem=recv_sems.at[k],
                                        device_id=to, device_id_type=MESH)


def _gather_weights(shards, meta):
    n = len(shards)

    def body(*refs):
        srcs, meta_ref = refs[:n], refs[n]
        outs, mout_ref = refs[n + 1:2 * n + 1], refs[2 * n + 1]
        send_sems, recv_sems = refs[2 * n + 2:]
        x, y, c = _axes()
        me = 2 * x + y
        sib = (x, y, 1 - c)
        chips = _other_chips(x, y)

        def half(t, chip_idx, cc):
            hr = shards[t].shape[0] // 2
            return outs[t].at[chip_idx, pl.ds(cc * hr, hr), :]

        first = []
        for j, (px, py) in enumerate(chips):
            for t in range(n):
                hr = shards[t].shape[0] // 2
                first.append(_rcopy(srcs[t].at[pl.ds(c * hr, hr), :], half(t, me, c), send_sems, recv_sems,
                                    3 * t + j, (px, py, c)))
            first.append(_rcopy(meta_ref, mout_ref.at[me], send_sems, recv_sems, 3 * n + j, (px, py, c)))
        for cp in first:
            cp.start()
        passed = []
        for j, (px, py) in enumerate(chips):
            src_chip = 2 * px + py
            for t in range(n):
                _rcopy(half(t, src_chip, c), half(t, src_chip, c), send_sems, recv_sems, 3 * t + j, sib).wait_recv()
                fwd = _rcopy(half(t, src_chip, c), half(t, src_chip, c), send_sems, recv_sems, 3 * (n + 1 + t) + j, sib)
                fwd.start()
                passed.append(fwd)
            _rcopy(mout_ref.at[src_chip], mout_ref.at[src_chip], send_sems, recv_sems, 3 * n + j, sib).wait_recv()
        for j, (px, py) in enumerate(chips):
            src_chip = 2 * px + py
            for t in range(n):
                _rcopy(half(t, src_chip, 1 - c), half(t, src_chip, 1 - c), send_sems, recv_sems,
                       3 * (n + 1 + t) + j, sib).wait_recv()
        for cp in first + passed:
            cp.wait_send()

    nsem = 3 * (2 * n + 1)
    return pl.pallas_call(
        body, name="gather_weights", in_specs=[ANY] * (n + 1), out_specs=[ANY] * (n + 1),
        out_shape=[jax.ShapeDtypeStruct((N_CHIPS,) + s.shape, s.dtype) for s in shards]
        + [jax.ShapeDtypeStruct((N_CHIPS,) + meta.shape, meta.dtype)],
        scratch_shapes=[pltpu.SemaphoreType.DMA((nsem,)), pltpu.SemaphoreType.DMA((nsem,))])(*shards, meta)


def _swap_halves(gs):
    n = len(gs)

    def body(*refs):
        srcs, outs = refs[:n], refs[n:2 * n]
        send_sems, recv_sems = refs[2 * n:]
        x, y, c = _axes()
        cps = []
        for t in range(n):
            hr = gs[t].shape[1] // 2
            for j in range(N_CHIPS):
                cps.append(_rcopy(srcs[t].at[j, pl.ds((1 - c) * hr, hr), :], outs[t].at[j], send_sems, recv_sems,
                                  N_CHIPS * t + j, (x, y, 1 - c)))
        for cp in cps:
            cp.start()
        for cp in cps:
            cp.wait()

    return pl.pallas_call(
        body, name="swap_halves", in_specs=[ANY] * n, out_specs=[ANY] * n,
        out_shape=[jax.ShapeDtypeStruct((N_CHIPS, g.shape[1] // 2, g.shape[2]), g.dtype) for g in gs],
        scratch_shapes=[pltpu.SemaphoreType.DMA((N_CHIPS * n,)), pltpu.SemaphoreType.DMA((N_CHIPS * n,))])(*gs)


def _scatter_chips(parts):
    n = len(parts)

    def body(*refs):
        srcs, outs = refs[:n], refs[n:2 * n]
        send_sems, recv_sems = refs[2 * n:]
        x, y, c = _axes()
        me = 2 * x + y
        cps = []
        for j, (px, py) in enumerate(_other_chips(x, y)):
            for t in range(n):
                cps.append(_rcopy(srcs[t].at[2 * px + py], outs[t].at[me], send_sems, recv_sems, 3 * t + j,
                                  (px, py, c)))
        for cp in cps:
            cp.start()
        for cp in cps:
            cp.wait()

    return pl.pallas_call(
        body, name="scatter_chips", in_specs=[ANY] * n, out_specs=[ANY] * n,
        out_shape=[jax.ShapeDtypeStruct(p.shape, p.dtype) for p in parts],
        scratch_shapes=[pltpu.SemaphoreType.DMA((3 * n,)), pltpu.SemaphoreType.DMA((3 * n,))])(*parts)


def _swap_reduced(rs):
    n = len(rs)

    def body(*refs):
        srcs, outs = refs[:n], refs[n:2 * n]
        send_sems, recv_sems = refs[2 * n:]
        x, y, c = _axes()
        cps = [_rcopy(srcs[t], outs[t], send_sems, recv_sems, t, (x, y, 1 - c)) for t in range(n)]
        for cp in cps:
            cp.start()
        for cp in cps:
            cp.wait()

    return pl.pallas_call(
        body, name="swap_reduced", in_specs=[ANY] * n, out_specs=[ANY] * n,
        out_shape=[jax.ShapeDtypeStruct(r.shape, r.dtype) for r in rs],
        scratch_shapes=[pltpu.SemaphoreType.DMA((n,)), pltpu.SemaphoreType.DMA((n,))])(*rs)


SMALL_ROWS = 24


def _allreduce_small(vec):
    def body(v_ref, out_ref, slots, send_sems, recv_sems):
        x, y, c = _axes()
        me = 4 * x + 2 * y + c
        slots[me] = v_ref[...]
        cps = []
        for k in range(1, 8):
            kx, ky, kc = (k >> 2) & 1, (k >> 1) & 1, k & 1
            peer = (1 - x if kx else x, 1 - y if ky else y, 1 - c if kc else c)
            cps.append(_rcopy(v_ref, slots.at[me], send_sems, recv_sems, k - 1, peer))
        for cp in cps:
            cp.start()
        for cp in cps:
            cp.wait()
        tot = slots[0]
        for k in range(1, 8):
            tot = tot + slots[k]
        out_ref[...] = tot

    return pl.pallas_call(
        body, name="allreduce_small",
        in_specs=[pl.BlockSpec(memory_space=pltpu.VMEM)], out_specs=pl.BlockSpec(memory_space=pltpu.VMEM),
        out_shape=jax.ShapeDtypeStruct((SMALL_ROWS, 128), F32),
        scratch_shapes=[pltpu.VMEM((8, SMALL_ROWS, 128), F32), pltpu.SemaphoreType.DMA((7,)),
                        pltpu.SemaphoreType.DMA((7,))])(vec)


def _pack_p2(w_uq, w_ukv, w_br_mla, w_br_fox, w_out, meta, dtype):
    parts = [w_uq.reshape(96, D_MODEL), w_ukv.reshape(64, D_MODEL), w_br_mla, w_br_fox, w_out,
             meta.reshape(4, D_MODEL), jnp.zeros((P2_ROWS - 932, D_MODEL), meta.dtype)]
    return jnp.concatenate([p.astype(dtype) for p in parts], axis=0)


def _unpack_p2(pk):
    return (pk[0:96].reshape(256, 384), pk[96:160].reshape(128, 512), pk[160:416], pk[416:672], pk[672:928],
            pk[928:932].reshape(N_META, 256))


def _uq_arrange(w):
    w3 = w.reshape(256, HEADS, 96)
    nope = w3[:, :, :64].reshape(256, PAIRS, 128)
    pe = w3[:, :, 64:].reshape(256, PAIRS, 64)
    return jnp.concatenate([nope, pe, jnp.zeros((256, PAIRS, 64), w.dtype)], axis=2).reshape(256, PAIRS * 256)


def _uq_restore(g):
    g3 = g.reshape(256, PAIRS, 256)
    nope = g3[:, :, :128].reshape(256, HEADS, 64)
    pe = g3[:, :, 128:192].reshape(256, HEADS, 32)
    return jnp.concatenate([nope, pe], axis=2).reshape(256, HEADS * 96)


def _ukv_arrange(w):
    w3 = w.reshape(128, HEADS, 128)
    return jnp.concatenate([w3[:, :, :64].reshape(128, 1024), w3[:, :, 64:].reshape(128, 1024)], axis=1)


def _ukv_restore(g):
    kn = g[:, :1024].reshape(128, HEADS, 64)
    vv = g[:, 1024:].reshape(128, HEADS, 64)
    return jnp.concatenate([kn, vv], axis=2).reshape(128, HEADS * 128)


def _rope_tables(lp):
    r = jnp.arange(lp)
    pos = jnp.where(r < N_META, r, jnp.where(r >= PAD, r - PAD + N_META, 0))
    half = MLA_ROPE // 2
    inv_freq = ROPE_THETA ** (-jnp.arange(half, dtype=F32) / half)
    ang = pos.astype(F32)[:, None] * inv_freq[None, :]
    cos, sin = jnp.cos(ang), jnp.sin(ang)
    one, zero = jnp.ones((lp, 64), F32), jnp.zeros((lp, 64), F32)
    return (jnp.concatenate([cos, cos, cos, cos, one], axis=1),
            jnp.concatenate([-sin, sin, -sin, sin, zero], axis=1))


def _pad_lanes(v, n=128):
    return jnp.pad(v, ((0, 0), (0, n - v.shape[1])))


def _in_cols(slabs, a, b):
    out = []
    for j in range(N_CHIPS):
        lo, hi = max(a, W_IN_SHARD * j), min(b, W_IN_SHARD * (j + 1))
        if lo < hi:
            out.append(slabs[j][:, lo - W_IN_SHARD * j:hi - W_IN_SHARD * j])
    return out


def _local_step(x2, tgt2, meta_f, w_small, w_attn, w_gate, w_uq_f, w_ukv_f, w_bm, w_bf, w_o, pre_norm_g,
                post_norm_g, mla_q_norm_g, mla_kv_norm_g, fox_forget_b):
    s_rows = x2.shape[0]
    lp = PAD + s_rows
    w_uq_a = _uq_arrange(w_uq_f)
    w_ukv_a = _ukv_arrange(w_ukv_f)

    ctab, stab = _rope_tables(lp)
    ii = jnp.arange(BLK)
    tri_lo = (ii[:, None] >= ii[None, :]).astype(BF16)
    tri_up = (ii[:, None] <= ii[None, :]).astype(BF16)
    fb128 = _pad_lanes(fox_forget_b)

    h = jnp.concatenate([meta_f, jnp.zeros((PAD - N_META, D_MODEL), F32), x2], axis=0)
    u = _rms_pre(h, pre_norm_g)
    small = _mm(u, w_small, mode="nn", out_dtype=F32, name="proj_small")
    attn = _mm(u, w_attn, mode="nn", out_dtype=BF16, name="proj_attn")
    gate = _mm(u, w_gate, mode="nn", out_dtype=BF16, name="proj_gate")
    qn, kvn, kr, ncum = _small_prep(small, mla_q_norm_g, mla_kv_norm_g, fb128, ctab, stab, tri_lo)
    qraw = _mm(qn, w_uq_a, mode="nn", out_dtype=F32, name="mla_q")
    qcat = _rope_q(qraw, ctab, stab, inverse=False, out_dtype=BF16, name="rope_q")
    kv = _mm(kvn, w_ukv_a, mode="nn", out_dtype=BF16, name="mla_kv")
    nbrep = jnp.broadcast_to(ncum[:, :HEADS].T[:, :, None], (HEADS, lp, LANES))

    mla_cols = dict(qcol=0, kcol=lambda p: p, vcol=lambda p: PAIRS + p)
    fox_cols = dict(qcol=0, kcol=lambda p: PAIRS + p, vcol=lambda p: 2 * PAIRS + p)
    o_mla, lse_mla = _attn_fwd(qcat, kv, kv, kr=kr, scale=MLA_SCALE, name="mla_fwd", **mla_cols)
    o_fox, lse_fox = _attn_fwd(attn, attn, attn, nbrep=nbrep, scale=FOX_SCALE, name="fox_fwd", **fox_cols)

    a_mla, a_fox = _gate_fwd(o_mla, o_fox, gate)
    y_mla = _mm(a_mla, w_bm, mode="nn", out_dtype=F32, name="br_mla")
    y_fox = _mm(a_fox, w_bf, mode="nn", out_dtype=F32, name="br_fox")
    mg = _merge_fwd(gate, y_mla, y_fox)
    mixed = _mm(mg, w_o, mode="nn", out_dtype=F32, name="out_proj")
    dmixed, dy, loss_p, dg_post = _tail(h, mixed, tgt2, post_norm_g)

    d_w_out = _mm(mg, dmixed, mode="tn", out_dtype=F32, name="d_w_out")
    dm = _mm(dmixed, w_o, mode="nt", out_dtype=F32, name="d_merge")
    dy_mla, dy_fox, dgate_ab = _merge_bwd(dm, gate, y_mla, y_fox)
    d_w_bm = _mm(a_mla, dy_mla, mode="tn", out_dtype=F32, name="d_w_br_mla")
    d_w_bf = _mm(a_fox, dy_fox, mode="tn", out_dtype=F32, name="d_w_br_fox")
    da_mla = _mm(dy_mla, w_bm, mode="nt", out_dtype=F32, name="d_a_mla")
    da_fox = _mm(dy_fox, w_bf, mode="nt", out_dtype=F32, name="d_a_fox")
    do_mla, do_fox, dgate_z = _gate_bwd(da_mla, da_fox, o_mla, o_fox, gate)

    dqcat, dkn, dvm, dkr = _attn_bwd(qcat, kv, kv, do_mla, o_mla, lse_mla, kr=kr, scale=MLA_SCALE,
                                     name="mla_bwd", **mla_cols)
    dfq, dfk, dfv, dcol, drow = _attn_bwd(attn, attn, attn, do_fox, o_fox, lse_fox, nbrep=nbrep, scale=FOX_SCALE,
                                          name="fox_bwd", **fox_cols)

    dq_a = _rope_q(dqcat, ctab, stab, inverse=True, out_dtype=BF16, name="rope_q_bwd")
    d_w_uq_a = _mm(qn, dq_a, mode="tn", out_dtype=F32, name="d_w_uq")
    dqn = _mm(dq_a, w_uq_a, mode="nt", out_dtype=F32, name="d_qn")
    d_w_ukv_a = jnp.concatenate([_mm(kvn, dkn, mode="tn", out_dtype=F32, name="d_w_uk"),
                                 _mm(kvn, dvm, mode="tn", out_dtype=F32, name="d_w_uv")], axis=1)
    dkvn = _mm(dkn, w_ukv_a[:, :1024], mode="nt", out_dtype=F32, name="d_kvn_k")
    dkvn = _mm(dvm, w_ukv_a[:, 1024:], mode="nt", out_dtype=F32, name="d_kvn_v", acc=dkvn)
    dcol_t = _pad_lanes(jnp.transpose(dcol[:, :, 0:2], (1, 0, 2)).reshape(lp, HEADS))
    drow_t = _pad_lanes(drow.reshape(HEADS, lp).T)
    dsmall, dg_q, dg_kv, dfb = _small_bwd(small, dqn, dkvn, dkr, dcol_t, drow_t, mla_q_norm_g, mla_kv_norm_g,
                                          fb128, ctab, stab, tri_up)

    dw_small = _mm(u, dsmall, mode="tn", out_dtype=F32, name="d_w_small")
    dw_fq = _mm(u, dfq, mode="tn", out_dtype=F32, name="d_w_fq")
    dw_fk = _mm(u, dfk, mode="tn", out_dtype=F32, name="d_w_fk")
    dw_fv = _mm(u, dfv, mode="tn", out_dtype=F32, name="d_w_fv")
    dw_z = _mm(u, dgate_z, mode="tn", out_dtype=F32, name="d_w_z")
    dw_g = _mm(u, dgate_ab, mode="tn", out_dtype=F32, name="d_w_g")
    du = _mm(dsmall, w_small, mode="nt", out_dtype=F32, name="d_u_small")
    du = _mm(dfq, w_attn[:, 0:1024], mode="nt", out_dtype=F32, name="d_u_fq", acc=du)
    du = _mm(dfk, w_attn[:, 1024:2048], mode="nt", out_dtype=F32, name="d_u_fk", acc=du)
    du = _mm(dfv, w_attn[:, 2048:3072], mode="nt", out_dtype=F32, name="d_u_fv", acc=du)
    du = _mm(dgate_z, w_gate[:, 0:2048], mode="nt", out_dtype=F32, name="d_u_z", acc=du)
    du = _mm(dgate_ab, w_gate[:, 2048:4096], mode="nt", out_dtype=F32, name="d_u_g", acc=du)
    dx, dmeta, dg_pre = _pre_bwd(du, h, dy, pre_norm_g, s_rows)

    runs = [(dw_small[:, 0:416], C_CQ), (dw_z[:, 0:1024], C_ZMLA), (dw_fq, C_FQ), (dw_fk, C_FK), (dw_fv, C_FV),
            (dw_small[:, 512:528], C_FL), (dw_z[:, 1024:2048], C_ZFOX), (dw_g, C_GA)]
    slabs = []
    for j in range(N_CHIPS):
        lo, hi = W_IN_SHARD * j, W_IN_SHARD * (j + 1)
        cols = [a[:, max(lo, c0) - c0:min(hi, c0 + a.shape[1]) - c0] for a, c0 in runs
                if max(lo, c0) < min(hi, c0 + a.shape[1])]
        slabs.append(jnp.concatenate(cols, axis=1))
    d_w_in = jnp.stack(slabs, axis=0)
    d_w_uq = _uq_restore(d_w_uq_a)
    d_w_ukv = _ukv_restore(d_w_ukv_a)
    return (loss_p, dx, dmeta, d_w_in, d_w_uq, d_w_ukv, d_w_bm, d_w_bf, d_w_out, dg_pre, dg_post, dg_q, dg_kv, dfb)


def kernel(x, meta_tokens, pre_norm_g, w_in, fox_forget_b, mla_q_norm_g, mla_kv_norm_g, w_uq, w_ukv, w_br_mla, w_br_fox, w_out, post_norm_g, loss_target, m_meta_tokens, m_pre_norm_g, m_w_in, m_fox_forget_b, m_mla_q_norm_g, m_mla_kv_norm_g, m_w_uq, m_w_ukv, m_w_br_mla, m_w_br_fox, m_w_out, m_post_norm_g, v_meta_tokens, v_pre_norm_g, v_w_in, v_fox_forget_b, v_mla_q_norm_g, v_mla_kv_norm_g, v_w_uq, v_w_ukv, v_w_br_mla, v_w_br_fox, v_w_out, v_post_norm_g):
    me = 2 * lax.axis_index("x") + lax.axis_index("y")
    core = lax.axis_index("c")
    w_in_b = w_in.astype(BF16).reshape(D_MODEL, W_IN_SHARD)
    p2 = _pack_p2(w_uq[0], w_ukv[0], w_br_mla[0], w_br_fox[0], w_out[0], jnp.zeros((N_META, 256), F32), BF16)
    w_in_g, p2_g, meta_g = _gather_weights([w_in_b, p2], meta_tokens)
    slabs = [jnp.where(me == j, w_in_b, w_in_g[j]) for j in range(N_CHIPS)]
    pieces = [_unpack_p2(jnp.where(me == j, p2, p2_g[j])) for j in range(N_CHIPS)]
    w_uq_f = jnp.concatenate([p[0] for p in pieces], axis=1)
    w_ukv_f = jnp.concatenate([p[1] for p in pieces], axis=1)
    w_bm = jnp.concatenate([p[2] for p in pieces], axis=0)
    w_bf = jnp.concatenate([p[3] for p in pieces], axis=0)
    w_o = jnp.concatenate([p[4] for p in pieces], axis=0)
    meta_f = jnp.concatenate([jnp.where(me == j, meta_tokens, meta_g[j]) for j in range(N_CHIPS)], axis=1)
    kpe = _in_cols(slabs, C_KPE, C_ZMLA)
    w_small = jnp.concatenate(_in_cols(slabs, C_CQ, C_KPE) + kpe + kpe + [jnp.zeros((D_MODEL, 64), BF16)]
                              + _in_cols(slabs, C_FL, C_ZFOX) + [jnp.zeros((D_MODEL, 112), BF16)], axis=1)
    w_attn = jnp.concatenate(_in_cols(slabs, C_FQ, C_FL), axis=1)
    w_gate = jnp.concatenate(_in_cols(slabs, C_ZMLA, C_FQ) + _in_cols(slabs, C_ZFOX, C_END), axis=1)

    (loss_p, dx, dmeta, d_w_in, d_w_uq, d_w_ukv, d_w_bm, d_w_bf, d_w_out, dg_pre, dg_post, dg_q, dg_kv,
     dfb) = _local_step(x[0], loss_target[0], meta_f, w_small, w_attn, w_gate, w_uq_f, w_ukv_f, w_bm, w_bf, w_o,
                        pre_norm_g, post_norm_g, mla_q_norm_g, mla_kv_norm_g, fox_forget_b)

    g1 = d_w_in
    g2 = jnp.stack([_pack_p2(d_w_uq[:, 384 * j:384 * (j + 1)], d_w_ukv[:, 512 * j:512 * (j + 1)],
                             d_w_bm[256 * j:256 * (j + 1)], d_w_bf[256 * j:256 * (j + 1)],
                             d_w_out[256 * j:256 * (j + 1)], dmeta[:, 256 * j:256 * (j + 1)], F32)
                    for j in range(N_CHIPS)], axis=0)
    s1, s2 = _swap_halves([g1, g2])
    part1, part2 = _add_cores(g1, s1, "add_cores_w_in"), _add_cores(g2, s2, "add_cores_rest")
    landed = _scatter_chips([part1, part2])
    mine = [_add_chips(l, lax.dynamic_index_in_dim(p, me, 0, keepdims=False), nm)
            for l, p, nm in zip(landed, (part1, part2), ("add_chips_w_in", "add_chips_rest"))]
    theirs = _swap_reduced(mine)
    g_w_in, g_p2 = [jnp.concatenate([jnp.where(core == 0, a, b), jnp.where(core == 0, b, a)], axis=0)
                    for a, b in zip(mine, theirs)]
    g_w_uq, g_w_ukv, g_w_bm, g_w_bf, g_w_out, g_meta = _unpack_p2(g_p2)
    g_w_in = g_w_in[None]

    vec = jnp.concatenate([dg_pre.reshape(8, 128), dg_post.reshape(8, 128), dg_q.reshape(2, 128), dg_kv,
                           dfb, _pad_lanes(loss_p), jnp.zeros((3, 128), F32)], axis=0)
    tot = _allreduce_small(vec)
    loss = tot[20, 0]

    def small_pack(pre, post, gq_, gkv_, fb_):
        return jnp.concatenate([pre.reshape(8, 128), post.reshape(8, 128), gq_.reshape(2, 128), gkv_,
                                _pad_lanes(fb_), jnp.zeros((4, 128), F32)], axis=0)

    def small_unpack(t):
        return (t[0:8].reshape(1, 1024), t[8:16].reshape(1, 1024), t[16:18].reshape(1, 256), t[18:19],
                t[19:20, 0:HEADS])

    g_small = jnp.concatenate([tot[0:20], jnp.zeros((4, 128), F32)], axis=0)
    sm = _adamw(small_pack(pre_norm_g, post_norm_g, mla_q_norm_g, mla_kv_norm_g, fox_forget_b), g_small,
                small_pack(m_pre_norm_g, m_post_norm_g, m_mla_q_norm_g, m_mla_kv_norm_g, m_fox_forget_b),
                small_pack(v_pre_norm_g, v_post_norm_g, v_mla_q_norm_g, v_mla_kv_norm_g, v_fox_forget_b),
                "adamw_small")
    g_pre, g_post, g_q, g_kv, g_fb = small_unpack(g_small)
    (d_pre, d_post, d_q, d_kv, d_fb), (nm_pre, nm_post, nm_q, nm_kv, nm_fb), (nv_pre, nv_post, nv_q, nv_kv, nv_fb) = (
        small_unpack(t) for t in sm)

    d_meta, nm_meta, nv_meta = _adamw(meta_tokens, g_meta, m_meta_tokens, v_meta_tokens, "adamw_meta")
    d_win, nm_win, nv_win = _adamw(w_in, g_w_in, m_w_in, v_w_in, "adamw_w_in")
    d_wuq, nm_wuq, nv_wuq = _adamw(w_uq[0], g_w_uq, m_w_uq[0], v_w_uq[0], "adamw_w_uq")
    d_wukv, nm_wukv, nv_wukv = _adamw(w_ukv[0], g_w_ukv, m_w_ukv[0], v_w_ukv[0], "adamw_w_ukv")
    d_wbm, nm_wbm, nv_wbm = _adamw(w_br_mla[0], g_w_bm, m_w_br_mla[0], v_w_br_mla[0], "adamw_w_br_mla")
    d_wbf, nm_wbf, nv_wbf = _adamw(w_br_fox[0], g_w_bf, m_w_br_fox[0], v_w_br_fox[0], "adamw_w_br_fox")
    d_wo, nm_wo, nv_wo = _adamw(w_out[0], g_w_out, m_w_out[0], v_w_out[0], "adamw_w_out")

    def group(meta_, pre, win, fb_, q_, kv_, wuq, wukv, wbm, wbf, wo, post):
        return (meta_, pre, win, fb_, q_, kv_, wuq[None], wukv[None], wbm[None], wbf[None], wo[None], post)

    grads = group(g_meta, g_pre, g_w_in, g_fb, g_q, g_kv, g_w_uq, g_w_ukv, g_w_bm, g_w_bf, g_w_out, g_post)
    deltas = group(d_meta, d_pre, d_win, d_fb, d_q, d_kv, d_wuq, d_wukv, d_wbm, d_wbf, d_wo, d_post)
    new_m = group(nm_meta, nm_pre, nm_win, nm_fb, nm_q, nm_kv, nm_wuq, nm_wukv, nm_wbm, nm_wbf, nm_wo, nm_post)
    new_v = group(nv_meta, nv_pre, nv_win, nv_fb, nv_q, nv_kv, nv_wuq, nv_wukv, nv_wbm, nv_wbf, nv_wo, nv_post)
    return (loss, dx[None], *grads, *deltas, *new_m, *new_v)
```

```python
import math

import jax
import jax.numpy as jnp
from jax import lax
from jax.experimental import pallas as pl
from jax.experimental.pallas import tpu as pltpu

F32 = jnp.float32
BF16 = jnp.bfloat16

D_MODEL = 1024
N_META = 16
RMS_EPS = 1e-6
HEADS = 16
PAIRS = HEADS // 2
HEAD_DIM = 64
LANES = 128
MLA_ROPE = 32
MLA_SCALE = 1.0 / math.sqrt(64 + 32)
FOX_SCALE = 1.0 / math.sqrt(64)
ROPE_THETA = 10000.0

PAD = 256
BLK = 256
QB = 512
UNROLL = 4
NEG = -1e30

C_CQ, C_CKV, C_KPE, C_ZMLA, C_FQ, C_FK, C_FV, C_FL, C_ZFOX, C_GA, C_GB, C_END = (
    0, 256, 384, 416, 1440, 2464, 3488, 4512, 4528, 5552, 6576, 7600)
SMALL_W = 640
W_IN_SHARD = 1900

P2_ROWS = 960
N_CHIPS = 4

ADAM_LR = 0.001
ADAM_B1 = 0.9
ADAM_B2 = 0.999
ADAM_EPS = 1e-08
ADAM_WD = 0.01
ADAM_STEP = 10

VMEM_BIG = 56 * 1024 * 1024
MESH = pl.DeviceIdType.MESH


def _cp(dims, vmem=None):
    return pltpu.CompilerParams(dimension_semantics=dims, vmem_limit_bytes=vmem)


def _dot(a, b, ca, cb):
    return lax.dot_general(a, b, (((ca,), (cb,)), ((), ())), preferred_element_type=F32)


def _sigmoid(x):
    return 1.0 / (1.0 + jnp.exp(-x))


def _tile(n, cands):
    for c in cands:
        if n % c == 0:
            return c
    return n


def _mm(a, b, *, mode, out_dtype, name, acc=None):
    if mode == "nn":
        (M, K), N = a.shape, b.shape[1]
    elif mode == "nt":
        (M, K), N = a.shape, b.shape[0]
    else:
        (K, M), N = a.shape, b.shape[1]
    tm = _tile(M, (1088, 1024)) if M > 1024 else M
    tn = _tile(N, (1024,)) if N > 1024 else N
    tk = _tile(K, (1088, 1024)) if K > 1088 else K
    nk = K // tk
    ca, cb = {"nn": (1, 0), "nt": (1, 1), "tn": (0, 0)}[mode]
    a_spec = (pl.BlockSpec((tk, tm), lambda j, i, k: (k, i)) if mode == "tn"
              else pl.BlockSpec((tm, tk), lambda j, i, k: (i, k)))
    b_spec = (pl.BlockSpec((tn, tk), lambda j, i, k: (j, k)) if mode == "nt"
              else pl.BlockSpec((tk, tn), lambda j, i, k: (k, j)))
    o_spec = pl.BlockSpec((tm, tn), lambda j, i, k: (i, j))
    has_acc = acc is not None

    def body(*refs):
        a_ref, b_ref = refs[0], refs[1]
        acc_ref = refs[2] if has_acc else None
        o_ref = refs[3] if has_acc else refs[2]
        part = _dot(a_ref[...].astype(BF16), b_ref[...].astype(BF16), ca, cb)
        if nk == 1:
            if has_acc:
                part = part + acc_ref[...]
            o_ref[...] = part.astype(out_dtype)
        else:
            sc = refs[-1]
            k = pl.program_id(2)

            @pl.when(k == 0)
            def _():
                sc[...] = part + acc_ref[...] if has_acc else part

            @pl.when(k > 0)
            def _():
                sc[...] += part

            @pl.when(k == nk - 1)
            def _():
                o_ref[...] = sc[...].astype(out_dtype)

    ins = [a, b] + ([acc] if has_acc else [])
    in_specs = [a_spec, b_spec] + ([o_spec] if has_acc else [])
    return pl.pallas_call(
        body, name=name, grid=(N // tn, M // tm, nk), in_specs=in_specs, out_specs=o_spec,
        out_shape=jax.ShapeDtypeStruct((M, N), out_dtype),
        scratch_shapes=[pltpu.VMEM((tm, tn), F32)] if nk > 1 else [],
        compiler_params=_cp(("parallel", "parallel", "arbitrary"), VMEM_BIG))(*ins)


def _row(w):
    return pl.BlockSpec((BLK, w), lambda i: (i, 0))


def _rowc(w, c):
    return pl.BlockSpec((BLK, w), lambda i: (i, c))


def _full(shape):
    return pl.BlockSpec(shape, lambda i: tuple(0 for _ in shape))


def _rope(x, c, s):
    lane = lax.broadcasted_iota(jnp.int32, x.shape, 1)
    is_x1 = ((lane >> 4) & 1) == 0
    partner = jnp.where(is_x1, pltpu.roll(x, LANES - 16, 1), pltpu.roll(x, 16, 1))
    return x * c + partner * s


def _row_valid(i):
    rows = i * BLK + lax.broadcasted_iota(jnp.int32, (BLK, 1), 0)
    return (rows < N_META) | (rows >= PAD)


def _rms_pre(h, g):
    lp = h.shape[0]

    def body(h_ref, g_ref, u_ref):
        hv = h_ref[...]
        r = lax.rsqrt(jnp.mean(hv * hv, axis=-1, keepdims=True) + RMS_EPS)
        u_ref[...] = (hv * r * g_ref[...]).astype(BF16)

    return pl.pallas_call(
        body, name="rms_pre", grid=(lp // BLK,),
        in_specs=[_row(D_MODEL), _full((1, D_MODEL))], out_specs=_row(D_MODEL),
        out_shape=jax.ShapeDtypeStruct((lp, D_MODEL), BF16),
        compiler_params=_cp(("parallel",)))(h, g)


def _split3(x):
    hi = x.astype(BF16)
    r1 = x - hi.astype(F32)
    mid = r1.astype(BF16)
    lo = (r1 - mid.astype(F32)).astype(BF16)
    return hi, mid, lo


def _small_prep(small, gq, gkv, fb, ctab, stab, tri):
    lp = small.shape[0]

    def body(sm_ref, gq_ref, gkv_ref, fb_ref, c_ref, s_ref, tri_ref, qn_ref, kvn_ref, kr_ref, ncum_ref, carry):
        i = pl.program_id(0)

        @pl.when(i == 0)
        def _():
            carry[...] = jnp.zeros_like(carry)

        cq = sm_ref[:, 0:256]
        r = lax.rsqrt(jnp.mean(cq * cq, axis=-1, keepdims=True) + RMS_EPS)
        qn_ref[...] = (cq * r * gq_ref[...]).astype(BF16)
        ckv = sm_ref[:, 256:384]
        r = lax.rsqrt(jnp.mean(ckv * ckv, axis=-1, keepdims=True) + RMS_EPS)
        kvn_ref[...] = (ckv * r * gkv_ref[...]).astype(BF16)
        kr_ref[...] = _rope(sm_ref[:, 384:512], c_ref[...], s_ref[...]).astype(BF16)
        fl = sm_ref[:, 512:640] + fb_ref[...]
        lf = jnp.minimum(fl, 0.0) - jnp.log(1.0 + jnp.exp(-jnp.abs(fl)))
        lf = jnp.where(_row_valid(i), lf, 0.0)
        hi, mid, lo = _split3(lf)
        t = tri_ref[...]
        cum = (_dot(t, hi, 1, 0) + _dot(t, mid, 1, 0)) + _dot(t, lo, 1, 0) + carry[...]
        ncum_ref[...] = -cum
        carry[...] = -ncum_ref[BLK - 1:BLK, :]

    return pl.pallas_call(
        body, name="small_prep", grid=(lp // BLK,),
        in_specs=[_row(SMALL_W), _full((1, 256)), _full((1, 128)), _full((1, 128)), _row(128), _row(128),
                  _full((BLK, BLK))],
        out_specs=[_row(256), _row(128), _row(128), _row(128)],
        out_shape=[jax.ShapeDtypeStruct((lp, 256), BF16), jax.ShapeDtypeStruct((lp, 128), BF16),
                   jax.ShapeDtypeStruct((lp, 128), BF16), jax.ShapeDtypeStruct((lp, 128), F32)],
        scratch_shapes=[pltpu.VMEM((1, 128), F32)],
        compiler_params=_cp(("arbitrary",)))(small, gq, gkv, fb, ctab, stab, tri)


def _rope_q(qraw, ctab, stab, *, inverse, out_dtype, name):
    lp = qraw.shape[0]

    def body(q_ref, c_ref, s_ref, o_ref):
        c = c_ref[...]
        s = -s_ref[...] if inverse else s_ref[...]
        for p in range(PAIRS):
            lo = p * 256
            o_ref[:, lo:lo + 128] = q_ref[:, lo:lo + 128].astype(out_dtype)
            o_ref[:, lo + 128:lo + 256] = _rope(q_ref[:, lo + 128:lo + 256].astype(F32), c, s).astype(out_dtype)

    return pl.pallas_call(
        body, name=name, grid=(lp // BLK,),
        in_specs=[_row(PAIRS * 256), _row(128), _row(128)], out_specs=_row(PAIRS * 256),
        out_shape=jax.ShapeDtypeStruct((lp, PAIRS * 256), out_dtype),
        compiler_params=_cp(("parallel",)))(qraw, ctab, stab)


def _gate_fwd(o_mla, o_fox, gate):
    lp = o_mla.shape[0]

    def body(om_ref, of_ref, zm_ref, zf_ref, am_ref, af_ref):
        zm = zm_ref[...].astype(F32)
        am_ref[...] = (om_ref[...] * (zm * _sigmoid(zm))).astype(BF16)
        zf = zf_ref[...].astype(F32)
        af_ref[...] = (of_ref[...] * (zf * _sigmoid(zf))).astype(BF16)

    return pl.pallas_call(
        body, name="gate_fwd", grid=(lp // BLK,),
        in_specs=[_row(D_MODEL), _row(D_MODEL), _rowc(D_MODEL, 0), _rowc(D_MODEL, 1)],
        out_specs=[_row(D_MODEL), _row(D_MODEL)],
        out_shape=[jax.ShapeDtypeStruct((lp, D_MODEL), BF16)] * 2,
        compiler_params=_cp(("parallel",)))(o_mla, o_fox, gate, gate)


def _merge_fwd(gate, y_mla, y_fox):
    lp = y_mla.shape[0]

    def body(ga_ref, gb_ref, ym_ref, yf_ref, m_ref):
        sa = _sigmoid(ga_ref[...].astype(F32))
        sb = _sigmoid(gb_ref[...].astype(F32))
        m_ref[...] = (sa * ym_ref[...] + sb * yf_ref[...]).astype(BF16)

    return pl.pallas_call(
        body, name="merge_fwd", grid=(lp // BLK,),
        in_specs=[_rowc(D_MODEL, 2), _rowc(D_MODEL, 3), _row(D_MODEL), _row(D_MODEL)],
        out_specs=_row(D_MODEL), out_shape=jax.ShapeDtypeStruct((lp, D_MODEL), BF16),
        compiler_params=_cp(("parallel",)))(gate, gate, y_mla, y_fox)


def _tail(h, mixed, tgt, gpost):
    lp = h.shape[0]
    shift = pl.BlockSpec((BLK, D_MODEL), lambda i: (jnp.maximum(i - 1, 0), 0))

    def body(h_ref, mx_ref, t_ref, g_ref, dmx_ref, dy_ref, loss_ref, dg_ref):
        i = pl.program_id(0)

        @pl.when(i == 0)
        def _():
            loss_ref[...] = jnp.zeros_like(loss_ref)
            dg_ref[...] = jnp.zeros_like(dg_ref)
            dmx_ref[...] = jnp.zeros_like(dmx_ref)
            dy_ref[...] = jnp.zeros_like(dy_ref)

        @pl.when(i > 0)
        def _():
            mx = mx_ref[...]
            g = g_ref[...]
            r = lax.rsqrt(jnp.mean(mx * mx, axis=-1, keepdims=True) + RMS_EPS)
            nrm = mx * r
            e = (h_ref[...] + nrm * g) - t_ref[...]
            loss_ref[...] += jnp.sum(0.5 * jnp.sum(e * e, axis=-1, keepdims=True) * (1.0 / D_MODEL),
                                     axis=0, keepdims=True)
            dy = e * (1.0 / D_MODEL)
            dy_ref[...] = dy
            dg_ref[...] += jnp.sum(dy * nrm, axis=0, keepdims=True)
            w = dy * g
            dot = jnp.mean(w * mx, axis=-1, keepdims=True)
            dmx_ref[...] = (r * w - mx * (r * r * r * dot)).astype(BF16)

    return pl.pallas_call(
        body, name="tail", grid=(lp // BLK,),
        in_specs=[_row(D_MODEL), _row(D_MODEL), shift, _full((1, D_MODEL))],
        out_specs=[_row(D_MODEL), _row(D_MODEL), _full((1, 1)), _full((1, D_MODEL))],
        out_shape=[jax.ShapeDtypeStruct((lp, D_MODEL), BF16), jax.ShapeDtypeStruct((lp, D_MODEL), F32),
                   jax.ShapeDtypeStruct((1, 1), F32), jax.ShapeDtypeStruct((1, D_MODEL), F32)],
        compiler_params=_cp(("arbitrary",)))(h, mixed, tgt, gpost)


def _merge_bwd(dm, gate, y_mla, y_fox):
    lp = dm.shape[0]

    def body(dm_ref, ga_ref, gb_ref, ym_ref, yf_ref, dym_ref, dyf_ref, dg_ref):
        dm_v = dm_ref[...]
        sa = _sigmoid(ga_ref[...].astype(F32))
        sb = _sigmoid(gb_ref[...].astype(F32))
        dym_ref[...] = (dm_v * sa).astype(BF16)
        dyf_ref[...] = (dm_v * sb).astype(BF16)
        dg_ref[:, 0:D_MODEL] = (dm_v * ym_ref[...] * (sa * (1.0 - sa))).astype(BF16)
        dg_ref[:, D_MODEL:2 * D_MODEL] = (dm_v * yf_ref[...] * (sb * (1.0 - sb))).astype(BF16)

    return pl.pallas_call(
        body, name="merge_bwd", grid=(lp // BLK,),
        in_specs=[_row(D_MODEL), _rowc(D_MODEL, 2), _rowc(D_MODEL, 3), _row(D_MODEL), _row(D_MODEL)],
        out_specs=[_row(D_MODEL), _row(D_MODEL), _row(2 * D_MODEL)],
        out_shape=[jax.ShapeDtypeStruct((lp, D_MODEL), BF16), jax.ShapeDtypeStruct((lp, D_MODEL), BF16),
                   jax.ShapeDtypeStruct((lp, 2 * D_MODEL), BF16)],
        compiler_params=_cp(("parallel",)))(dm, gate, gate, y_mla, y_fox)


def _gate_bwd(da_mla, da_fox, o_mla, o_fox, gate):
    lp = da_mla.shape[0]

    def one(da, o, z):
        sg = _sigmoid(z)
        do = da * (z * sg)
        dz = da * o * (sg * (1.0 + z * (1.0 - sg)))
        return do.astype(BF16), dz.astype(BF16)

    def body(dam_ref, daf_ref, om_ref, of_ref, zm_ref, zf_ref, dom_ref, dof_ref, dz_ref):
        dom_ref[...], dz_ref[:, 0:D_MODEL] = one(dam_ref[...], om_ref[...], zm_ref[...].astype(F32))
        dof_ref[...], dz_ref[:, D_MODEL:2 * D_MODEL] = one(daf_ref[...], of_ref[...], zf_ref[...].astype(F32))

    return pl.pallas_call(
        body, name="gate_bwd", grid=(lp // BLK,),
        in_specs=[_row(D_MODEL)] * 4 + [_rowc(D_MODEL, 0), _rowc(D_MODEL, 1)],
        out_specs=[_row(D_MODEL), _row(D_MODEL), _row(2 * D_MODEL)],
        out_shape=[jax.ShapeDtypeStruct((lp, D_MODEL), BF16), jax.ShapeDtypeStruct((lp, D_MODEL), BF16),
                   jax.ShapeDtypeStruct((lp, 2 * D_MODEL), BF16)],
        compiler_params=_cp(("parallel",)))(da_mla, da_fox, o_mla, o_fox, gate, gate)


def _small_bwd(small, dqn, dkvn, dkr, dcol_t, drow_t, gq, gkv, fb, ctab, stab, triu):
    lp = small.shape[0]
    nb = lp // BLK

    def rrow(w):
        return pl.BlockSpec((BLK, w), lambda i: (nb - 1 - i, 0))

    def body(sm_ref, dqn_ref, dkvn_ref, dkr_ref, dcol_ref, drow_ref, gq_ref, gkv_ref, fb_ref, c_ref, s_ref, tri_ref,
             ds_ref, dgq_ref, dgkv_ref, dfb_ref, carry):
        i = pl.program_id(0)

        @pl.when(i == 0)
        def _():
            carry[...] = jnp.zeros_like(carry)
            dgq_ref[...] = jnp.zeros_like(dgq_ref)
            dgkv_ref[...] = jnp.zeros_like(dgkv_ref)
            dfb_ref[...] = jnp.zeros_like(dfb_ref)

        def norm_bwd(x, dn, g, dg_ref):
            r = lax.rsqrt(jnp.mean(x * x, axis=-1, keepdims=True) + RMS_EPS)
            dg_ref[...] += jnp.sum(dn * (x * r), axis=0, keepdims=True)
            w = dn * g
            dot = jnp.mean(w * x, axis=-1, keepdims=True)
            return r * w - x * (r * r * r * dot)

        ds_ref[:, 0:256] = norm_bwd(sm_ref[:, 0:256], dqn_ref[...], gq_ref[...], dgq_ref).astype(BF16)
        ds_ref[:, 256:384] = norm_bwd(sm_ref[:, 256:384], dkvn_ref[...], gkv_ref[...], dgkv_ref).astype(BF16)

        dk = dkr_ref[0]
        for p in range(1, PAIRS):
            dk = dk + dkr_ref[p]
        dk = _rope(dk, c_ref[...], -s_ref[...])
        lane = lax.broadcasted_iota(jnp.int32, dk.shape, 1)
        dk = jnp.where(lane < MLA_ROPE, dk + pltpu.roll(dk, LANES - MLA_ROPE, 1), 0.0)
        ds_ref[:, 384:512] = dk.astype(BF16)

        dcr = dcol_ref[...] - drow_ref[...]
        hi, mid, lo = _split3(dcr)
        t = tri_ref[...]
        suf = (_dot(t, hi, 1, 0) + _dot(t, mid, 1, 0)) + _dot(t, lo, 1, 0) + carry[...]
        fl = sm_ref[:, 512:640] + fb_ref[...]
        dfl = jnp.where(_row_valid(nb - 1 - i), -suf * _sigmoid(-fl), 0.0)
        ds_ref[:, 512:640] = dfl.astype(BF16)
        dfb_ref[...] += jnp.sum(dfl, axis=0, keepdims=True)
        carry[...] += jnp.sum(dcr, axis=0, keepdims=True)

    return pl.pallas_call(
        body, name="small_bwd", grid=(nb,),
        in_specs=[rrow(SMALL_W), rrow(256), rrow(128),
                  pl.BlockSpec((PAIRS, BLK, 128), lambda i: (0, nb - 1 - i, 0)), rrow(128), rrow(128),
                  _full((1, 256)), _full((1, 128)), _full((1, 128)), rrow(128), rrow(128), _full((BLK, BLK))],
        out_specs=[rrow(SMALL_W), _full((1, 256)), _full((1, 128)), _full((1, 128))],
        out_shape=[jax.ShapeDtypeStruct((lp, SMALL_W), BF16), jax.ShapeDtypeStruct((1, 256), F32),
                   jax.ShapeDtypeStruct((1, 128), F32), jax.ShapeDtypeStruct((1, 128), F32)],
        scratch_shapes=[pltpu.VMEM((1, 128), F32)],
        compiler_params=_cp(("arbitrary",)))(small, dqn, dkvn, dkr, dcol_t, drow_t, gq, gkv, fb, ctab, stab, triu)


def _pre_bwd(du, h, dy, gpre, s_rows):
    lp = h.shape[0]
    shift = pl.BlockSpec((BLK, D_MODEL), lambda i: (jnp.maximum(i - 1, 0), 0))

    def body(du_ref, h_ref, dy_ref, g_ref, dx_ref, dmeta_ref, dg_ref):
        i = pl.program_id(0)

        @pl.when(i == 0)
        def _():
            dg_ref[...] = jnp.zeros_like(dg_ref)

        hv = h_ref[...]
        duv = du_ref[...]
        r = lax.rsqrt(jnp.mean(hv * hv, axis=-1, keepdims=True) + RMS_EPS)
        dg_ref[...] += jnp.sum(duv * (hv * r), axis=0, keepdims=True)
        w = duv * g_ref[...]
        dot = jnp.mean(w * hv, axis=-1, keepdims=True)
        dh = dy_ref[...] + (r * w - hv * (r * r * r * dot))
        dx_ref[...] = dh

        @pl.when(i == 0)
        def _():
            dmeta_ref[...] = dh[0:N_META, :]

    return pl.pallas_call(
        body, name="pre_bwd", grid=(lp // BLK,),
        in_specs=[_row(D_MODEL), _row(D_MODEL), _row(D_MODEL), _full((1, D_MODEL))],
        out_specs=[shift, _full((N_META, D_MODEL)), _full((1, D_MODEL))],
        out_shape=[jax.ShapeDtypeStruct((s_rows, D_MODEL), F32), jax.ShapeDtypeStruct((N_META, D_MODEL), F32),
                   jax.ShapeDtypeStruct((1, D_MODEL), F32)],
        compiler_params=_cp(("arbitrary",)))(du, h, dy, gpre)


def _pair_masks(rope):
    lane = lax.broadcasted_iota(jnp.int32, (1, LANES), 1)
    mas = [lane < HEAD_DIM, lane >= HEAD_DIM]
    if not rope:
        return mas, mas
    wide = lax.broadcasted_iota(jnp.int32, (1, 2 * LANES), 1)
    rope_lo = LANES + MLA_ROPE
    return mas, [(wide < HEAD_DIM) | ((wide >= LANES) & (wide < rope_lo)),
                 ((wide >= HEAD_DIM) & (wide < LANES)) | ((wide >= rope_lo) & (wide < rope_lo + MLA_ROPE))]


def _mask2(x, masks):
    return [jnp.where(m, x, jnp.zeros_like(x)) for m in masks]


def _attn_fwd(q, k, v, *, kr=None, nbrep=None, scale, qcol, kcol, vcol, name):
    lp = q.shape[0]
    nq = 1 + (lp - PAD) // QB
    rope = kr is not None
    bias = nbrep is not None
    qw = 256 if rope else 128

    def body(*refs):
        it = iter(refs)
        q_ref, k_ref, v_ref = next(it), next(it), next(it)
        kr_ref = next(it) if rope else None
        nb_ref = next(it) if bias else None
        o_ref, lse_ref = next(it), next(it)
        i = pl.program_id(1)
        r0 = pl.multiple_of(jnp.where(i == 0, 0, PAD + QB * (i - 1)), BLK)
        b0 = r0 // BLK
        mas, hmask = _pair_masks(rope)
        qh = _mask2(q_ref[pl.ds(r0, QB), :], hmask)
        if bias:
            qh = [x * scale for x in qh]

        def causal(kc, n):
            key = kc * BLK + lax.broadcasted_iota(jnp.int32, (n, QB), 0)
            return (key <= r0 + lax.broadcasted_iota(jnp.int32, (n, QB), 1)) & ((kc > 0) | (n == N_META))

        def scores(kc, n=BLK):
            k0 = pl.multiple_of(kc * BLK, BLK)
            kk = k_ref[pl.ds(k0, n), :]
            if rope:
                kk = jnp.concatenate([kk, kr_ref[pl.ds(k0, n), :]], axis=1)
            out = []
            for h in range(2):
                s = _dot(kk, qh[h], 1, 1)
                out.append(s * scale if rope else s)
            return out

        def update(kc, ss, carry, mask, n=BLK):
            stats, acc = carry[:4], carry[4]
            k0 = pl.multiple_of(kc * BLK, BLK)
            vv = v_ref[pl.ds(k0, n), :]
            new_stats, alphas, ps = [], [], []
            for h in range(2):
                m_prev, l_prev = stats[2 * h], stats[2 * h + 1]
                s = ss[h]
                if bias:
                    nbc = nb_ref[h, pl.ds(k0, n), :]
                    s = s + jnp.concatenate([nbc] * (QB // LANES), axis=1)
                if mask is not None:
                    s = jnp.where(mask, s, NEG)
                m_new = jnp.maximum(m_prev, jnp.max(s, axis=0, keepdims=True))
                alpha = jnp.exp(m_prev - m_new)
                p = jnp.exp(s - m_new)
                new_stats += [m_new, alpha * l_prev + jnp.sum(p, axis=0, keepdims=True)]
                alphas.append(alpha)
                ps.append(p.astype(BF16))
            vcat = jnp.concatenate(_mask2(vv, mas), axis=0)
            pv = _dot(vcat, jnp.concatenate(ps, axis=0), 0, 0)
            a_full = jnp.concatenate([jnp.broadcast_to(a, (HEAD_DIM, QB)) for a in alphas], axis=0)
            return (*new_stats, a_full * acc + pv)

        neg = jnp.full((1, QB), NEG, F32)
        zero = jnp.zeros((1, QB), F32)
        s_nxt = scores(jnp.minimum(1, b0))
        c = update(0, scores(0, N_META), (neg, zero, neg, zero, jnp.zeros((LANES, QB), F32)),
                   causal(0, N_META), N_META)

        def step(kc, cr):
            nxt = scores(kc + 1)
            return (*nxt, *update(kc, cr[:2], cr[2:], None))

        cr = lax.fori_loop(1, b0, step, (*s_nxt, *c))
        s_last = scores(b0 + 1)
        c = update(b0, cr[:2], cr[2:], causal(b0, BLK))
        c = update(b0 + 1, s_last, c, causal(b0 + 1, BLK))
        inv = jnp.concatenate([jnp.broadcast_to(1.0 / c[1], (HEAD_DIM, QB)),
                               jnp.broadcast_to(1.0 / c[3], (HEAD_DIM, QB))], axis=0)
        o_t = (c[4] * inv).T
        lses = [c[0] + jnp.log(c[1]), c[2] + jnp.log(c[3])]
        o_ref[pl.ds(r0, BLK), :] = o_t[0:BLK]
        for h in range(2):
            lse_ref[0, h:h + 1, pl.ds(r0, BLK)] = lses[h][:, 0:BLK]

        @pl.when(i > 0)
        def _():
            r1 = pl.multiple_of(r0 + BLK, BLK)
            o_ref[pl.ds(r1, QB - BLK), :] = o_t[BLK:QB]
            for h in range(2):
                lse_ref[0, h:h + 1, pl.ds(r1, QB - BLK)] = lses[h][:, BLK:QB]

    in_specs = [pl.BlockSpec((lp, qw), lambda p, i: (0, qcol + p)),
                pl.BlockSpec((lp, 128), lambda p, i: (0, kcol(p))),
                pl.BlockSpec((lp, 128), lambda p, i: (0, vcol(p)))]
    ins = [q, k, v]
    if rope:
        in_specs.append(pl.BlockSpec((lp, 128), lambda p, i: (0, 0)))
        ins.append(kr)
    if bias:
        in_specs.append(pl.BlockSpec((2, lp, 128), lambda p, i: (p, 0, 0)))
        ins.append(nbrep)
    return pl.pallas_call(
        body, name=name, grid=(PAIRS, nq), in_specs=in_specs,
        out_specs=[pl.BlockSpec((lp, 128), lambda p, i: (0, p)),
                   pl.BlockSpec((1, 2, lp), lambda p, i: (p, 0, 0))],
        out_shape=[jax.ShapeDtypeStruct((lp, D_MODEL), F32), jax.ShapeDtypeStruct((PAIRS, 2, lp), F32)],
        compiler_params=_cp(("parallel", "arbitrary"), VMEM_BIG))(*ins)


def _attn_bwd(q, k, v, do, o, lse, *, kr=None, nbrep=None, scale, qcol, kcol, vcol, name):
    lp = q.shape[0]
    nb = lp // BLK
    rope = kr is not None
    bias = nbrep is not None
    qw = 256 if rope else 128

    def body(*refs):
        it = iter(refs)
        q_ref, k_ref, v_ref = next(it), next(it), next(it)
        kr_ref = next(it) if rope else None
        nb_ref = next(it) if bias else None
        do_ref, o_ref, lse_ref = next(it), next(it), next(it)
        dq_ref, dk_ref, dv_ref = next(it), next(it), next(it)
        x_ref = next(it)
        drow_ref = next(it) if bias else None
        delta = next(it)
        kb = pl.program_id(1)
        mas, hmask = _pair_masks(rope)
        lane = lax.broadcasted_iota(jnp.int32, (1, LANES), 1)

        @pl.when(kb == 0)
        def _():
            dq_ref[...] = jnp.zeros_like(dq_ref)
            if bias:
                drow_ref[...] = jnp.zeros_like(drow_ref)
            sub = lax.broadcasted_iota(jnp.int32, (8, LANES), 0)
            sel = (((sub == 0) & mas[0]) | ((sub == 1) & mas[1])).astype(BF16)

            def dstep(c, carry):
                r0 = pl.multiple_of(c * BLK, BLK)
                prod = do_ref[pl.ds(r0, BLK), :].astype(F32) * o_ref[pl.ds(r0, BLK), :]
                hi, mid, lo = _split3(prod)
                delta[:, pl.ds(r0, BLK)] = (_dot(sel, hi, 1, 1) + _dot(sel, mid, 1, 1)) + _dot(sel, lo, 1, 1)
                return carry

            lax.fori_loop(0, nb, dstep, 0)

        def masked_q(q0):
            qh = _mask2(q_ref[pl.ds(q0, BLK), :], hmask)
            return [x * scale for x in qh] if bias else qh

        def key_pass(n):
            kk = k_ref[0:n, :]
            if rope:
                kk = jnp.concatenate([kk, kr_ref[0:n, :]], axis=1)
            vh = _mask2(v_ref[0:n, :], mas)
            kcat = jnp.concatenate(_mask2(kk, hmask), axis=0)
            if bias:
                kcat = kcat * scale
                nbc = [jnp.concatenate([nb_ref[h, 0:n, :], nb_ref[h, 0:n, :]], axis=1) for h in range(2)]
            diag_mask = (lax.broadcasted_iota(jnp.int32, (n, BLK), 0) <= lax.broadcasted_iota(jnp.int32, (n, BLK), 1))

            def chunk(qc, carry, mask):
                carry = list(carry)
                q0 = pl.multiple_of(qc * BLK, BLK)
                dov = do_ref[pl.ds(q0, BLK), :]
                doh = _mask2(dov, mas)
                qh = masked_q(q0)
                pbs, dss = [], []
                for h in range(2):
                    s = _dot(kk, qh[h], 1, 1)
                    if rope:
                        s = s * scale
                    if bias:
                        s = s + nbc[h]
                    p = jnp.exp(s - lse_ref[0, h:h + 1, pl.ds(q0, BLK)])
                    if mask is not None:
                        p = jnp.where(mask, p, 0.0)
                    ds = p * (_dot(vh[h], dov, 1, 1) - delta[h:h + 1, pl.ds(q0, BLK)])
                    if bias:
                        drow_ref[0, h:h + 1, pl.ds(q0, BLK)] += jnp.sum(ds, axis=0, keepdims=True)
                        carry[2 + h] = carry[2 + h] + jnp.sum(ds, axis=1, keepdims=True)
                    else:
                        ds = ds * scale
                    pbs.append(p.astype(BF16))
                    dss.append(ds.astype(BF16))
                ds_lanes = jnp.concatenate(dss, axis=1)
                ds_rows = jnp.concatenate(dss, axis=0)
                carry[0] = carry[0] + _dot(ds_lanes, jnp.concatenate(qh, axis=0), 1, 0)
                carry[1] = carry[1] + _dot(jnp.concatenate(pbs, axis=1), jnp.concatenate(doh, axis=0), 1, 0)
                dq_ref[pl.ds(q0, BLK), :] += _dot(ds_rows, kcat, 0, 0)
                return tuple(carry)

            init = [jnp.zeros((n, qw), F32), jnp.zeros((n, LANES), F32)]
            if bias:
                init += [jnp.zeros((n, 1), F32), jnp.zeros((n, 1), F32)]
            c = chunk(kb, tuple(init), diag_mask)
            rest = nb - 1 - kb

            def several(t, cr):
                for u in range(UNROLL):
                    cr = chunk(kb + 1 + UNROLL * t + u, cr, None)
                return cr

            c = lax.fori_loop(0, rest // UNROLL, several, c)
            c = lax.fori_loop(nb - rest % UNROLL, nb, lambda qc, cr: chunk(qc, cr, None), c)

            def rows(a, dtype):
                a = a.astype(dtype)
                return a if n == BLK else jnp.concatenate([a, jnp.zeros((BLK - n, a.shape[1]), dtype)], axis=0)

            dk_ref[...] = rows(c[0][:, 0:LANES], BF16)
            dv_ref[...] = rows(c[1], BF16)
            if rope:
                x_ref[0] = rows(c[0][:, LANES:2 * LANES], F32)
            if bias:
                x_ref[0] = rows(jnp.where(lane == 0, c[2], jnp.where(lane == 1, c[3], 0.0)), F32)

        @pl.when(kb == 0)
        def _():
            key_pass(N_META)

        @pl.when(kb > 0)
        def _():
            key_pass(BLK)

    in_specs = [pl.BlockSpec((lp, qw), lambda p, j: (0, qcol + p)),
                pl.BlockSpec((BLK, 128), lambda p, j: (j, kcol(p))),
                pl.BlockSpec((BLK, 128), lambda p, j: (j, vcol(p)))]
    ins = [q, k, v]
    if rope:
        in_specs.append(pl.BlockSpec((BLK, 128), lambda p, j: (j, 0)))
        ins.append(kr)
    if bias:
        in_specs.append(pl.BlockSpec((2, BLK, 128), lambda p, j: (p, j, 0)))
        ins.append(nbrep)
    in_specs += [pl.BlockSpec((lp, 128), lambda p, j: (0, p)), pl.BlockSpec((lp, 128), lambda p, j: (0, p)),
                 pl.BlockSpec((1, 2, lp), lambda p, j: (p, 0, 0))]
    ins += [do, o, lse]
    out_specs = [pl.BlockSpec((lp, qw), lambda p, j: (0, p)),
                 pl.BlockSpec((BLK, 128), lambda p, j: (j, p)),
                 pl.BlockSpec((BLK, 128), lambda p, j: (j, p)),
                 pl.BlockSpec((1, BLK, 128), lambda p, j: (p, j, 0))]
    out_shape = [jax.ShapeDtypeStruct((lp, PAIRS * qw), F32), jax.ShapeDtypeStruct((lp, D_MODEL), BF16),
                 jax.ShapeDtypeStruct((lp, D_MODEL), BF16), jax.ShapeDtypeStruct((PAIRS, lp, 128), F32)]
    if bias:
        out_specs.append(pl.BlockSpec((1, 2, lp), lambda p, j: (p, 0, 0)))
        out_shape.append(jax.ShapeDtypeStruct((PAIRS, 2, lp), F32))
    return pl.pallas_call(
        body, name=name, grid=(PAIRS, nb), in_specs=in_specs, out_specs=out_specs, out_shape=out_shape,
        scratch_shapes=[pltpu.VMEM((8, lp), F32)],
        compiler_params=_cp(("parallel", "arbitrary"), VMEM_BIG))(*ins)


def _adamw(w, g, m, v, name):
    lead = w.ndim - 2
    rows, cols = w.shape[lead:]
    tr = 128 if rows * cols > 512 * 1024 else rows

    def body(w_ref, g_ref, m_ref, v_ref, d_ref, nm_ref, nv_ref):
        gv = g_ref[...]
        nm = ADAM_B1 * m_ref[...] + (1.0 - ADAM_B1) * gv
        nv = ADAM_B2 * v_ref[...] + (1.0 - ADAM_B2) * (gv * gv)
        m_hat = nm / (1.0 - ADAM_B1 ** ADAM_STEP)
        v_hat = nv / (1.0 - ADAM_B2 ** ADAM_STEP)
        d_ref[...] = -ADAM_LR * (m_hat / (jnp.sqrt(v_hat) + ADAM_EPS) + ADAM_WD * w_ref[...])
        nm_ref[...] = nm
        nv_ref[...] = nv

    spec = pl.BlockSpec((1,) * lead + (tr, cols), lambda i: (0,) * lead + (i, 0))
    return pl.pallas_call(
        body, name=name, grid=(rows // tr,), in_specs=[spec] * 4, out_specs=[spec] * 3,
        out_shape=[jax.ShapeDtypeStruct(w.shape, F32)] * 3,
        compiler_params=_cp(("parallel",), VMEM_BIG))(w, g, m, v)


def _add_cores(g, from_sib, name):
    n, rows, cols = g.shape
    half = rows // 2
    tr = _tile(half, (256, 240))
    nt = half // tr

    def body(lo_ref, hi_ref, s_ref, o_ref):
        mine = jnp.where(lax.axis_index("c") == 0, lo_ref[0], hi_ref[0])
        o_ref[0] = (mine + s_ref[0]).astype(BF16)

    return pl.pallas_call(
        body, name=name, grid=(n, nt),
        in_specs=[pl.BlockSpec((1, tr, cols), lambda j, i: (j, i, 0)),
                  pl.BlockSpec((1, tr, cols), lambda j, i: (j, nt + i, 0)),
                  pl.BlockSpec((1, tr, cols), lambda j, i: (j, i, 0))],
        out_specs=pl.BlockSpec((1, tr, cols), lambda j, i: (j, i, 0)),
        out_shape=jax.ShapeDtypeStruct((n, half, cols), BF16),
        compiler_params=_cp(("parallel", "parallel"), VMEM_BIG))(g, g, from_sib)


def _add_chips(x, own, name):
    n, rows, cols = x.shape
    tr = _tile(rows, (256, 240))

    def body(x_ref, own_ref, o_ref):
        me = 2 * lax.axis_index("x") + lax.axis_index("y")
        v = [jnp.where(me == k, own_ref[...], x_ref[k]).astype(F32) for k in range(N_CHIPS)]
        o_ref[...] = ((v[0] + v[1]) + v[2]) + v[3]

    return pl.pallas_call(
        body, name=name, grid=(rows // tr,),
        in_specs=[pl.BlockSpec((n, tr, cols), lambda i: (0, i, 0)), pl.BlockSpec((tr, cols), lambda i: (i, 0))],
        out_specs=pl.BlockSpec((tr, cols), lambda i: (i, 0)),
        out_shape=jax.ShapeDtypeStruct((rows, cols), F32), compiler_params=_cp(("parallel",), VMEM_BIG))(x, own)


def _axes():
    return lax.axis_index("x"), lax.axis_index("y"), lax.axis_index("c")


def _other_chips(x, y):
    return [(1 - x, y), (x, 1 - y), (1 - x, 1 - y)]


ANY = pl.BlockSpec(memory_space=pl.ANY)


def _rcopy(src, dst, send_sems, recv_sems, k, to):
    return pltpu.make_async_remote_copy(src_ref=src, dst_ref=dst, send_sem=send_sems.at[k], recv_sem=recv_sems.at[k],
                                        device_id=to, device_id_type=MESH)


def _gather_weights(shards, meta):
    n = len(shards)

    def body(*refs):
        srcs, meta_ref = refs[:n], refs[n]
        outs, mout_ref = refs[n + 1:2 * n + 1], refs[2 * n + 1]
        send_sems, recv_sems = refs[2 * n + 2:]
        x, y, c = _axes()
        me = 2 * x + y
        sib = (x, y, 1 - c)
        chips = _other_chips(x, y)

        def half(t, chip_idx, cc):
            hr = shards[t].shape[0] // 2
            return outs[t].at[chip_idx, pl.ds(cc * hr, hr), :]

        first = []
        for j, (px, py) in enumerate(chips):
            for t in range(n):
                hr = shards[t].shape[0] // 2
                first.append(_rcopy(srcs[t].at[pl.ds(c * hr, hr), :], half(t, me, c), send_sems, recv_sems,
                                    3 * t + j, (px, py, c)))
            first.append(_rcopy(meta_ref, mout_ref.at[me], send_sems, recv_sems, 3 * n + j, (px, py, c)))
        for cp in first:
            cp.start()
        passed = []
        for j, (px, py) in enumerate(chips):
            src_chip = 2 * px + py
            for t in range(n):
                _rcopy(half(t, src_chip, c), half(t, src_chip, c), send_sems, recv_sems, 3 * t + j, sib).wait_recv()
                fwd = _rcopy(half(t, src_chip, c), half(t, src_chip, c), send_sems, recv_sems, 3 * (n + 1 + t) + j, sib)
                fwd.start()
                passed.append(fwd)
            _rcopy(mout_ref.at[src_chip], mout_ref.at[src_chip], send_sems, recv_sems, 3 * n + j, sib).wait_recv()
        for j, (px, py) in enumerate(chips):
            src_chip = 2 * px + py
            for t in range(n):
                _rcopy(half(t, src_chip, 1 - c), half(t, src_chip, 1 - c), send_sems, recv_sems,
                       3 * (n + 1 + t) + j, sib).wait_recv()
        for cp in first + passed:
            cp.wait_send()

    nsem = 3 * (2 * n + 1)
    return pl.pallas_call(
        body, name="gather_weights", in_specs=[ANY] * (n + 1), out_specs=[ANY] * (n + 1),
        out_shape=[jax.ShapeDtypeStruct((N_CHIPS,) + s.shape, s.dtype) for s in shards]
        + [jax.ShapeDtypeStruct((N_CHIPS,) + meta.shape, meta.dtype)],
        scratch_shapes=[pltpu.SemaphoreType.DMA((nsem,)), pltpu.SemaphoreType.DMA((nsem,))])(*shards, meta)


def _swap_halves(gs):
    n = len(gs)

    def body(*refs):
        srcs, outs = refs[:n], refs[n:2 * n]
        send_sems, recv_sems = refs[2 * n:]
        x, y, c = _axes()
        cps = []
        for t in range(n):
            hr = gs[t].shape[1] // 2
            for j in range(N_CHIPS):
                cps.append(_rcopy(srcs[t].at[j, pl.ds((1 - c) * hr, hr), :], outs[t].at[j], send_sems, recv_sems,
                                  N_CHIPS * t + j, (x, y, 1 - c)))
        for cp in cps:
            cp.start()
        for cp in cps:
            cp.wait()

    return pl.pallas_call(
        body, name="swap_halves", in_specs=[ANY] * n, out_specs=[ANY] * n,
        out_shape=[jax.ShapeDtypeStruct((N_CHIPS, g.shape[1] // 2, g.shape[2]), g.dtype) for g in gs],
        scratch_shapes=[pltpu.SemaphoreType.DMA((N_CHIPS * n,)), pltpu.SemaphoreType.DMA((N_CHIPS * n,))])(*gs)


def _scatter_chips(parts):
    n = len(parts)

    def body(*refs):
        srcs, outs = refs[:n], refs[n:2 * n]
        send_sems, recv_sems = refs[2 * n:]
        x, y, c = _axes()
        me = 2 * x + y
        cps = []
        for j, (px, py) in enumerate(_other_chips(x, y)):
            for t in range(n):
                cps.append(_rcopy(srcs[t].at[2 * px + py], outs[t].at[me], send_sems, recv_sems, 3 * t + j,
                                  (px, py, c)))
        for cp in cps:
            cp.start()
        for cp in cps:
            cp.wait()

    return pl.pallas_call(
        body, name="scatter_chips", in_specs=[ANY] * n, out_specs=[ANY] * n,
        out_shape=[jax.ShapeDtypeStruct(p.shape, p.dtype) for p in parts],
        scratch_shapes=[pltpu.SemaphoreType.DMA((3 * n,)), pltpu.SemaphoreType.DMA((3 * n,))])(*parts)


def _swap_reduced(rs):
    n = len(rs)

    def body(*refs):
        srcs, outs = refs[:n], refs[n:2 * n]
        send_sems, recv_sems = refs[2 * n:]
        x, y, c = _axes()
        cps = [_rcopy(srcs[t], outs[t], send_sems, recv_sems, t, (x, y, 1 - c)) for t in range(n)]
        for cp in cps:
            cp.start()
        for cp in cps:
            cp.wait()

    return pl.pallas_call(
        body, name="swap_reduced", in_specs=[ANY] * n, out_specs=[ANY] * n,
        out_shape=[jax.ShapeDtypeStruct(r.shape, r.dtype) for r in rs],
        scratch_shapes=[pltpu.SemaphoreType.DMA((n,)), pltpu.SemaphoreType.DMA((n,))])(*rs)


SMALL_ROWS = 24


def _allreduce_small(vec):
    def body(v_ref, out_ref, slots, send_sems, recv_sems):
        x, y, c = _axes()
        me = 4 * x + 2 * y + c
        slots[me] = v_ref[...]
        cps = []
        for k in range(1, 8):
            kx, ky, kc = (k >> 2) & 1, (k >> 1) & 1, k & 1
            peer = (1 - x if kx else x, 1 - y if ky else y, 1 - c if kc else c)
            cps.append(_rcopy(v_ref, slots.at[me], send_sems, recv_sems, k - 1, peer))
        for cp in cps:
            cp.start()
        for cp in cps:
            cp.wait()
        tot = slots[0]
        for k in range(1, 8):
            tot = tot + slots[k]
        out_ref[...] = tot

    return pl.pallas_call(
        body, name="allreduce_small",
        in_specs=[pl.BlockSpec(memory_space=pltpu.VMEM)], out_specs=pl.BlockSpec(memory_space=pltpu.VMEM),
        out_shape=jax.ShapeDtypeStruct((SMALL_ROWS, 128), F32),
        scratch_shapes=[pltpu.VMEM((8, SMALL_ROWS, 128), F32), pltpu.SemaphoreType.DMA((7,)),
                        pltpu.SemaphoreType.DMA((7,))])(vec)


def _pack_p2(w_uq, w_ukv, w_br_mla, w_br_fox, w_out, meta, dtype):
    parts = [w_uq.reshape(96, D_MODEL), w_ukv.reshape(64, D_MODEL), w_br_mla, w_br_fox, w_out,
             meta.reshape(4, D_MODEL), jnp.zeros((P2_ROWS - 932, D_MODEL), meta.dtype)]
    return jnp.concatenate([p.astype(dtype) for p in parts], axis=0)


def _unpack_p2(pk):
    return (pk[0:96].reshape(256, 384), pk[96:160].reshape(128, 512), pk[160:416], pk[416:672], pk[672:928],
            pk[928:932].reshape(N_META, 256))


def _uq_arrange(w):
    w3 = w.reshape(256, HEADS, 96)
    nope = w3[:, :, :64].reshape(256, PAIRS, 128)
    pe = w3[:, :, 64:].reshape(256, PAIRS, 64)
    return jnp.concatenate([nope, pe, jnp.zeros((256, PAIRS, 64), w.dtype)], axis=2).reshape(256, PAIRS * 256)


def _uq_restore(g):
    g3 = g.reshape(256, PAIRS, 256)
    nope = g3[:, :, :128].reshape(256, HEADS, 64)
    pe = g3[:, :, 128:192].reshape(256, HEADS, 32)
    return jnp.concatenate([nope, pe], axis=2).reshape(256, HEADS * 96)


def _ukv_arrange(w):
    w3 = w.reshape(128, HEADS, 128)
    return jnp.concatenate([w3[:, :, :64].reshape(128, 1024), w3[:, :, 64:].reshape(128, 1024)], axis=1)


def _ukv_restore(g):
    kn = g[:, :1024].reshape(128, HEADS, 64)
    vv = g[:, 1024:].reshape(128, HEADS, 64)
    return jnp.concatenate([kn, vv], axis=2).reshape(128, HEADS * 128)


def _rope_tables(lp):
    r = jnp.arange(lp)
    pos = jnp.where(r < N_META, r, jnp.where(r >= PAD, r - PAD + N_META, 0))
    half = MLA_ROPE // 2
    inv_freq = ROPE_THETA ** (-jnp.arange(half, dtype=F32) / half)
    ang = pos.astype(F32)[:, None] * inv_freq[None, :]
    cos, sin = jnp.cos(ang), jnp.sin(ang)
    one, zero = jnp.ones((lp, 64), F32), jnp.zeros((lp, 64), F32)
    return (jnp.concatenate([cos, cos, cos, cos, one], axis=1),
            jnp.concatenate([-sin, sin, -sin, sin, zero], axis=1))


def _pad_lanes(v, n=128):
    return jnp.pad(v, ((0, 0), (0, n - v.shape[1])))


def _in_cols(slabs, a, b):
    out = []
    for j in range(N_CHIPS):
        lo, hi = max(a, W_IN_SHARD * j), min(b, W_IN_SHARD * (j + 1))
        if lo < hi:
            out.append(slabs[j][:, lo - W_IN_SHARD * j:hi - W_IN_SHARD * j])
    return out


def _local_step(x2, tgt2, meta_f, w_small, w_attn, w_gate, w_uq_f, w_ukv_f, w_bm, w_bf, w_o, pre_norm_g,
                post_norm_g, mla_q_norm_g, mla_kv_norm_g, fox_forget_b):
    s_rows = x2.shape[0]
    lp = PAD + s_rows
    w_uq_a = _uq_arrange(w_uq_f)
    w_ukv_a = _ukv_arrange(w_ukv_f)

    ctab, stab = _rope_tables(lp)
    ii = jnp.arange(BLK)
    tri_lo = (ii[:, None] >= ii[None, :]).astype(BF16)
    tri_up = (ii[:, None] <= ii[None, :]).astype(BF16)
    fb128 = _pad_lanes(fox_forget_b)

    h = jnp.concatenate([meta_f, jnp.zeros((PAD - N_META, D_MODEL), F32), x2], axis=0)
    u = _rms_pre(h, pre_norm_g)
    small = _mm(u, w_small, mode="nn", out_dtype=F32, name="proj_small")
    attn = _mm(u, w_attn, mode="nn", out_dtype=BF16, name="proj_attn")
    gate = _mm(u, w_gate, mode="nn", out_dtype=BF16, name="proj_gate")
    qn, kvn, kr, ncum = _small_prep(small, mla_q_norm_g, mla_kv_norm_g, fb128, ctab, stab, tri_lo)
    qraw = _mm(qn, w_uq_a, mode="nn", out_dtype=F32, name="mla_q")
    qcat = _rope_q(qraw, ctab, stab, inverse=False, out_dtype=BF16, name="rope_q")
    kv = _mm(kvn, w_ukv_a, mode="nn", out_dtype=BF16, name="mla_kv")
    nbrep = jnp.broadcast_to(ncum[:, :HEADS].T[:, :, None], (HEADS, lp, LANES))

    mla_cols = dict(qcol=0, kcol=lambda p: p, vcol=lambda p: PAIRS + p)
    fox_cols = dict(qcol=0, kcol=lambda p: PAIRS + p, vcol=lambda p: 2 * PAIRS + p)
    o_mla, lse_mla = _attn_fwd(qcat, kv, kv, kr=kr, scale=MLA_SCALE, name="mla_fwd", **mla_cols)
    o_fox, lse_fox = _attn_fwd(attn, attn, attn, nbrep=nbrep, scale=FOX_SCALE, name="fox_fwd", **fox_cols)

    a_mla, a_fox = _gate_fwd(o_mla, o_fox, gate)
    y_mla = _mm(a_mla, w_bm, mode="nn", out_dtype=F32, name="br_mla")
    y_fox = _mm(a_fox, w_bf, mode="nn", out_dtype=F32, name="br_fox")
    mg = _merge_fwd(gate, y_mla, y_fox)
    mixed = _mm(mg, w_o, mode="nn", out_dtype=F32, name="out_proj")
    dmixed, dy, loss_p, dg_post = _tail(h, mixed, tgt2, post_norm_g)

    d_w_out = _mm(mg, dmixed, mode="tn", out_dtype=F32, name="d_w_out")
    dm = _mm(dmixed, w_o, mode="nt", out_dtype=F32, name="d_merge")
    dy_mla, dy_fox, dgate_ab = _merge_bwd(dm, gate, y_mla, y_fox)
    d_w_bm = _mm(a_mla, dy_mla, mode="tn", out_dtype=F32, name="d_w_br_mla")
    d_w_bf = _mm(a_fox, dy_fox, mode="tn", out_dtype=F32, name="d_w_br_fox")
    da_mla = _mm(dy_mla, w_bm, mode="nt", out_dtype=F32, name="d_a_mla")
    da_fox = _mm(dy_fox, w_bf, mode="nt", out_dtype=F32, name="d_a_fox")
    do_mla, do_fox, dgate_z = _gate_bwd(da_mla, da_fox, o_mla, o_fox, gate)

    dqcat, dkn, dvm, dkr = _attn_bwd(qcat, kv, kv, do_mla, o_mla, lse_mla, kr=kr, scale=MLA_SCALE,
                                     name="mla_bwd", **mla_cols)
    dfq, dfk, dfv, dcol, drow = _attn_bwd(attn, attn, attn, do_fox, o_fox, lse_fox, nbrep=nbrep, scale=FOX_SCALE,
                                          name="fox_bwd", **fox_cols)

    dq_a = _rope_q(dqcat, ctab, stab, inverse=True, out_dtype=BF16, name="rope_q_bwd")
    d_w_uq_a = _mm(qn, dq_a, mode="tn", out_dtype=F32, name="d_w_uq")
    dqn = _mm(dq_a, w_uq_a, mode="nt", out_dtype=F32, name="d_qn")
    d_w_ukv_a = jnp.concatenate([_mm(kvn, dkn, mode="tn", out_dtype=F32, name="d_w_uk"),
                                 _mm(kvn, dvm, mode="tn", out_dtype=F32, name="d_w_uv")], axis=1)
    dkvn = _mm(dkn, w_ukv_a[:, :1024], mode="nt", out_dtype=F32, name="d_kvn_k")
    dkvn = _mm(dvm, w_ukv_a[:, 1024:], mode="nt", out_dtype=F32, name="d_kvn_v", acc=dkvn)
    dcol_t = _pad_lanes(jnp.transpose(dcol[:, :, 0:2], (1, 0, 2)).reshape(lp, HEADS))
    drow_t = _pad_lanes(drow.reshape(HEADS, lp).T)
    dsmall, dg_q, dg_kv, dfb = _small_bwd(small, dqn, dkvn, dkr, dcol_t, drow_t, mla_q_norm_g, mla_kv_norm_g,
                                          fb128, ctab, stab, tri_up)

    dw_small = _mm(u, dsmall, mode="tn", out_dtype=F32, name="d_w_small")
    dw_fq = _mm(u, dfq, mode="tn", out_dtype=F32, name="d_w_fq")
    dw_fk = _mm(u, dfk, mode="tn", out_dtype=F32, name="d_w_fk")
    dw_fv = _mm(u, dfv, mode="tn", out_dtype=F32, name="d_w_fv")
    dw_z = _mm(u, dgate_z, mode="tn", out_dtype=F32, name="d_w_z")
    dw_g = _mm(u, dgate_ab, mode="tn", out_dtype=F32, name="d_w_g")
    du = _mm(dsmall, w_small, mode="nt", out_dtype=F32, name="d_u_small")
    du = _mm(dfq, w_attn[:, 0:1024], mode="nt", out_dtype=F32, name="d_u_fq", acc=du)
    du = _mm(dfk, w_attn[:, 1024:2048], mode="nt", out_dtype=F32, name="d_u_fk", acc=du)
    du = _mm(dfv, w_attn[:, 2048:3072], mode="nt", out_dtype=F32, name="d_u_fv", acc=du)
    du = _mm(dgate_z, w_gate[:, 0:2048], mode="nt", out_dtype=F32, name="d_u_z", acc=du)
    du = _mm(dgate_ab, w_gate[:, 2048:4096], mode="nt", out_dtype=F32, name="d_u_g", acc=du)
    dx, dmeta, dg_pre = _pre_bwd(du, h, dy, pre_norm_g, s_rows)

    runs = [(dw_small[:, 0:416], C_CQ), (dw_z[:, 0:1024], C_ZMLA), (dw_fq, C_FQ), (dw_fk, C_FK), (dw_fv, C_FV),
            (dw_small[:, 512:528], C_FL), (dw_z[:, 1024:2048], C_ZFOX), (dw_g, C_GA)]
    slabs = []
    for j in range(N_CHIPS):
        lo, hi = W_IN_SHARD * j, W_IN_SHARD * (j + 1)
        cols = [a[:, max(lo, c0) - c0:min(hi, c0 + a.shape[1]) - c0] for a, c0 in runs
                if max(lo, c0) < min(hi, c0 + a.shape[1])]
        slabs.append(jnp.concatenate(cols, axis=1))
    d_w_in = jnp.stack(slabs, axis=0)
    d_w_uq = _uq_restore(d_w_uq_a)
    d_w_ukv = _ukv_restore(d_w_ukv_a)
    return (loss_p, dx, dmeta, d_w_in, d_w_uq, d_w_ukv, d_w_bm, d_w_bf, d_w_out, dg_pre, dg_post, dg_q, dg_kv, dfb)


def kernel(x, meta_tokens, pre_norm_g, w_in, fox_forget_b, mla_q_norm_g, mla_kv_norm_g, w_uq, w_ukv, w_br_mla, w_br_fox, w_out, post_norm_g, loss_target, m_meta_tokens, m_pre_norm_g, m_w_in, m_fox_forget_b, m_mla_q_norm_g, m_mla_kv_norm_g, m_w_uq, m_w_ukv, m_w_br_mla, m_w_br_fox, m_w_out, m_post_norm_g, v_meta_tokens, v_pre_norm_g, v_w_in, v_fox_forget_b, v_mla_q_norm_g, v_mla_kv_norm_g, v_w_uq, v_w_ukv, v_w_br_mla, v_w_br_fox, v_w_out, v_post_norm_g):
    me = 2 * lax.axis_index("x") + lax.axis_index("y")
    core = lax.axis_index("c")
    w_in_b = w_in.astype(BF16).reshape(D_MODEL, W_IN_SHARD)
    p2 = _pack_p2(w_uq[0], w_ukv[0], w_br_mla[0], w_br_fox[0], w_out[0], jnp.zeros((N_META, 256), F32), BF16)
    w_in_g, p2_g, meta_g = _gather_weights([w_in_b, p2], meta_tokens)
    slabs = [jnp.where(me == j, w_in_b, w_in_g[j]) for j in range(N_CHIPS)]
    pieces = [_unpack_p2(jnp.where(me == j, p2, p2_g[j])) for j in range(N_CHIPS)]
    w_uq_f = jnp.concatenate([p[0] for p in pieces], axis=1)
    w_ukv_f = jnp.concatenate([p[1] for p in pieces], axis=1)
    w_bm = jnp.concatenate([p[2] for p in pieces], axis=0)
    w_bf = jnp.concatenate([p[3] for p in pieces], axis=0)
    w_o = jnp.concatenate([p[4] for p in pieces], axis=0)
    meta_f = jnp.concatenate([jnp.where(me == j, meta_tokens, meta_g[j]) for j in range(N_CHIPS)], axis=1)
    kpe = _in_cols(slabs, C_KPE, C_ZMLA)
    w_small = jnp.concatenate(_in_cols(slabs, C_CQ, C_KPE) + kpe + kpe + [jnp.zeros((D_MODEL, 64), BF16)]
                              + _in_cols(slabs, C_FL, C_ZFOX) + [jnp.zeros((D_MODEL, 112), BF16)], axis=1)
    w_attn = jnp.concatenate(_in_cols(slabs, C_FQ, C_FL), axis=1)
    w_gate = jnp.concatenate(_in_cols(slabs, C_ZMLA, C_FQ) + _in_cols(slabs, C_ZFOX, C_END), axis=1)

    (loss_p, dx, dmeta, d_w_in, d_w_uq, d_w_ukv, d_w_bm, d_w_bf, d_w_out, dg_pre, dg_post, dg_q, dg_kv,
     dfb) = _local_step(x[0], loss_target[0], meta_f, w_small, w_attn, w_gate, w_uq_f, w_ukv_f, w_bm, w_bf, w_o,
                        pre_norm_g, post_norm_g, mla_q_norm_g, mla_kv_norm_g, fox_forget_b)

    g1 = d_w_in
    g2 = jnp.stack([_pack_p2(d_w_uq[:, 384 * j:384 * (j + 1)], d_w_ukv[:, 512 * j:512 * (j + 1)],
                             d_w_bm[256 * j:256 * (j + 1)], d_w_bf[256 * j:256 * (j + 1)],
                             d_w_out[256 * j:256 * (j + 1)], dmeta[:, 256 * j:256 * (j + 1)], F32)
                    for j in range(N_CHIPS)], axis=0)
    s1, s2 = _swap_halves([g1, g2])
    part1, part2 = _add_cores(g1, s1, "add_cores_w_in"), _add_cores(g2, s2, "add_cores_rest")
    landed = _scatter_chips([part1, part2])
    mine = [_add_chips(l, lax.dynamic_index_in_dim(p, me, 0, keepdims=False), nm)
            for l, p, nm in zip(landed, (part1, part2), ("add_chips_w_in", "add_chips_rest"))]
    theirs = _swap_reduced(mine)
    g_w_in, g_p2 = [jnp.concatenate([jnp.where(core == 0, a, b), jnp.where(core == 0, b, a)], axis=0)
                    for a, b in zip(mine, theirs)]
    g_w_uq, g_w_ukv, g_w_bm, g_w_bf, g_w_out, g_meta = _unpack_p2(g_p2)
    g_w_in = g_w_in[None]

    vec = jnp.concatenate([dg_pre.reshape(8, 128), dg_post.reshape(8, 128), dg_q.reshape(2, 128), dg_kv,
                           dfb, _pad_lanes(loss_p), jnp.zeros((3, 128), F32)], axis=0)
    tot = _allreduce_small(vec)
    loss = tot[20, 0]

    def small_pack(pre, post, gq_, gkv_, fb_):
        return jnp.concatenate([pre.reshape(8, 128), post.reshape(8, 128), gq_.reshape(2, 128), gkv_,
                                _pad_lanes(fb_), jnp.zeros((4, 128), F32)], axis=0)

    def small_unpack(t):
        return (t[0:8].reshape(1, 1024), t[8:16].reshape(1, 1024), t[16:18].reshape(1, 256), t[18:19],
                t[19:20, 0:HEADS])

    g_small = jnp.concatenate([tot[0:20], jnp.zeros((4, 128), F32)], axis=0)
    sm = _adamw(small_pack(pre_norm_g, post_norm_g, mla_q_norm_g, mla_kv_norm_g, fox_forget_b), g_small,
                small_pack(m_pre_norm_g, m_post_norm_g, m_mla_q_norm_g, m_mla_kv_norm_g, m_fox_forget_b),
                small_pack(v_pre_norm_g, v_post_norm_g, v_mla_q_norm_g, v_mla_kv_norm_g, v_fox_forget_b),
                "adamw_small")
    g_pre, g_post, g_q, g_kv, g_fb = small_unpack(g_small)
    (d_pre, d_post, d_q, d_kv, d_fb), (nm_pre, nm_post, nm_q, nm_kv, nm_fb), (nv_pre, nv_post, nv_q, nv_kv, nv_fb) = (
        small_unpack(t) for t in sm)

    d_meta, nm_meta, nv_meta = _adamw(meta_tokens, g_meta, m_meta_tokens, v_meta_tokens, "adamw_meta")
    d_win, nm_win, nv_win = _adamw(w_in, g_w_in, m_w_in, v_w_in, "adamw_w_in")
    d_wuq, nm_wuq, nv_wuq = _adamw(w_uq[0], g_w_uq, m_w_uq[0], v_w_uq[0], "adamw_w_uq")
    d_wukv, nm_wukv, nv_wukv = _adamw(w_ukv[0], g_w_ukv, m_w_ukv[0], v_w_ukv[0], "adamw_w_ukv")
    d_wbm, nm_wbm, nv_wbm = _adamw(w_br_mla[0], g_w_bm, m_w_br_mla[0], v_w_br_mla[0], "adamw_w_br_mla")
    d_wbf, nm_wbf, nv_wbf = _adamw(w_br_fox[0], g_w_bf, m_w_br_fox[0], v_w_br_fox[0], "adamw_w_br_fox")
    d_wo, nm_wo, nv_wo = _adamw(w_out[0], g_w_out, m_w_out[0], v_w_out[0], "adamw_w_out")

    def group(meta_, pre, win, fb_, q_, kv_, wuq, wukv, wbm, wbf, wo, post):
        return (meta_, pre, win, fb_, q_, kv_, wuq[None], wukv[None], wbm[None], wbf[None], wo[None], post)

    grads = group(g_meta, g_pre, g_w_in, g_fb, g_q, g_kv, g_w_uq, g_w_ukv, g_w_bm, g_w_bf, g_w_out, g_post)
    deltas = group(d_meta, d_pre, d_win, d_fb, d_q, d_kv, d_wuq, d_wukv, d_wbm, d_wbf, d_wo, d_post)
    new_m = group(nm_meta, nm_pre, nm_win, nm_fb, nm_q, nm_kv, nm_wuq, nm_wukv, nm_wbm, nm_wbf, nm_wo, nm_post)
    new_v = group(nv_meta, nv_pre, nv_win, nv_fb, nv_q, nv_kv, nv_wuq, nv_wukv, nv_wbm, nv_wbf, nv_wo, nv_post)
    return (loss, dx[None], *grads, *deltas, *new_m, *new_v)
```

```python
import math

import jax
import jax.numpy as jnp
from jax import lax
from jax.experimental import pallas as pl
from jax.experimental.pallas import tpu as pltpu

F32 = jnp.float32
BF16 = jnp.bfloat16

D_MODEL = 1024
N_META = 16
RMS_EPS = 1e-6
HEADS = 16
PAIRS = HEADS // 2
HEAD_DIM = 64
LANES = 128
MLA_ROPE = 32
MLA_SCALE = 1.0 / math.sqrt(64 + 32)
FOX_SCALE = 1.0 / math.sqrt(64)
ROPE_THETA = 10000.0

PAD = 256
BLK = 256
QB = 512
UNROLL = 4
NEG = -1e30

C_CQ, C_CKV, C_KPE, C_ZMLA, C_FQ, C_FK, C_FV, C_FL, C_ZFOX, C_GA, C_GB, C_END = (
    0, 256, 384, 416, 1440, 2464, 3488, 4512, 4528, 5552, 6576, 7600)
SMALL_W = 640
W_IN_SHARD = 1900

P2_ROWS = 960
N_CHIPS = 4

ADAM_LR = 0.001
ADAM_B1 = 0.9
ADAM_B2 = 0.999
ADAM_EPS = 1e-08
ADAM_WD = 0.01
ADAM_STEP = 10

VMEM_BIG = 56 * 1024 * 1024
MESH = pl.DeviceIdType.MESH


def _cp(dims, vmem=None):
    return pltpu.CompilerParams(dimension_semantics=dims, vmem_limit_bytes=vmem)


def _dot(a, b, ca, cb):
    return lax.dot_general(a, b, (((ca,), (cb,)), ((), ())), preferred_element_type=F32)


def _sigmoid(x):
    return 1.0 / (1.0 + jnp.exp(-x))


def _tile(n, cands):
    for c in cands:
        if n % c == 0:
            return c
    return n


def _mm(a, b, *, mode, out_dtype, name, acc=None):
    if mode == "nn":
        (M, K), N = a.shape, b.shape[1]
    elif mode == "nt":
        (M, K), N = a.shape, b.shape[0]
    else:
        (K, M), N = a.shape, b.shape[1]
    tm = _tile(M, (1088, 1024)) if M > 1024 else M
    tn = _tile(N, (1024,)) if N > 1024 else N
    tk = _tile(K, (1088, 1024)) if K > 1088 else K
    nk = K // tk
    ca, cb = {"nn": (1, 0), "nt": (1, 1), "tn": (0, 0)}[mode]
    a_spec = (pl.BlockSpec((tk, tm), lambda j, i, k: (k, i)) if mode == "tn"
              else pl.BlockSpec((tm, tk), lambda j, i, k: (i, k)))
    b_spec = (pl.BlockSpec((tn, tk), lambda j, i, k: (j, k)) if mode == "nt"
              else pl.BlockSpec((tk, tn), lambda j, i, k: (k, j)))
    o_spec = pl.BlockSpec((tm, tn), lambda j, i, k: (i, j))
    has_acc = acc is not None

    def body(*refs):
        a_ref, b_ref = refs[0], refs[1]
        acc_ref = refs[2] if has_acc else None
        o_ref = refs[3] if has_acc else refs[2]
        part = _dot(a_ref[...].astype(BF16), b_ref[...].astype(BF16), ca, cb)
        if nk == 1:
            if has_acc:
                part = part + acc_ref[...]
            o_ref[...] = part.astype(out_dtype)
        else:
            sc = refs[-1]
            k = pl.program_id(2)

            @pl.when(k == 0)
            def _():
                sc[...] = part + acc_ref[...] if has_acc else part

            @pl.when(k > 0)
            def _():
                sc[...] += part

            @pl.when(k == nk - 1)
            def _():
                o_ref[...] = sc[...].astype(out_dtype)

    ins = [a, b] + ([acc] if has_acc else [])
    in_specs = [a_spec, b_spec] + ([o_spec] if has_acc else [])
    return pl.pallas_call(
        body, name=name, grid=(N // tn, M // tm, nk), in_specs=in_specs, out_specs=o_spec,
        out_shape=jax.ShapeDtypeStruct((M, N), out_dtype),
        scratch_shapes=[pltpu.VMEM((tm, tn), F32)] if nk > 1 else [],
        compiler_params=_cp(("parallel", "parallel", "arbitrary"), VMEM_BIG))(*ins)


def _row(w):
    return pl.BlockSpec((BLK, w), lambda i: (i, 0))


def _rowc(w, c):
    return pl.BlockSpec((BLK, w), lambda i: (i, c))


def _full(shape):
    return pl.BlockSpec(shape, lambda i: tuple(0 for _ in shape))


def _rope(x, c, s):
    lane = lax.broadcasted_iota(jnp.int32, x.shape, 1)
    is_x1 = ((lane >> 4) & 1) == 0
    partner = jnp.where(is_x1, pltpu.roll(x, LANES - 16, 1), pltpu.roll(x, 16, 1))
    return x * c + partner * s


def _row_valid(i):
    rows = i * BLK + lax.broadcasted_iota(jnp.int32, (BLK, 1), 0)
    return (rows < N_META) | (rows >= PAD)


def _rms_pre(h, g):
    lp = h.shape[0]

    def body(h_ref, g_ref, u_ref):
        hv = h_ref[...]
        r = lax.rsqrt(jnp.mean(hv * hv, axis=-1, keepdims=True) + RMS_EPS)
        u_ref[...] = (hv * r * g_ref[...]).astype(BF16)

    return pl.pallas_call(
        body, name="rms_pre", grid=(lp // BLK,),
        in_specs=[_row(D_MODEL), _full((1, D_MODEL))], out_specs=_row(D_MODEL),
        out_shape=jax.ShapeDtypeStruct((lp, D_MODEL), BF16),
        compiler_params=_cp(("parallel",)))(h, g)


def _split3(x):
    hi = x.astype(BF16)
    r1 = x - hi.astype(F32)
    mid = r1.astype(BF16)
    lo = (r1 - mid.astype(F32)).astype(BF16)
    return hi, mid, lo


def _small_prep(small, gq, gkv, fb, ctab, stab, tri):
    lp = small.shape[0]

    def body(sm_ref, gq_ref, gkv_ref, fb_ref, c_ref, s_ref, tri_ref, qn_ref, kvn_ref, kr_ref, ncum_ref, carry):
        i = pl.program_id(0)

        @pl.when(i == 0)
        def _():
            carry[...] = jnp.zeros_like(carry)

        cq = sm_ref[:, 0:256]
        r = lax.rsqrt(jnp.mean(cq * cq, axis=-1, keepdims=True) + RMS_EPS)
        qn_ref[...] = (cq * r * gq_ref[...]).astype(BF16)
        ckv = sm_ref[:, 256:384]
        r = lax.rsqrt(jnp.mean(ckv * ckv, axis=-1, keepdims=True) + RMS_EPS)
        kvn_ref[...] = (ckv * r * gkv_ref[...]).astype(BF16)
        kr_ref[...] = _rope(sm_ref[:, 384:512], c_ref[...], s_ref[...]).astype(BF16)
        fl = sm_ref[:, 512:640] + fb_ref[...]
        lf = jnp.minimum(fl, 0.0) - jnp.log(1.0 + jnp.exp(-jnp.abs(fl)))
        lf = jnp.where(_row_valid(i), lf, 0.0)
        hi, mid, lo = _split3(lf)
        t = tri_ref[...]
        cum = (_dot(t, hi, 1, 0) + _dot(t, mid, 1, 0)) + _dot(t, lo, 1, 0) + carry[...]
        ncum_ref[...] = -cum
        carry[...] = -ncum_ref[BLK - 1:BLK, :]

    return pl.pallas_call(
        body, name="small_prep", grid=(lp // BLK,),
        in_specs=[_row(SMALL_W), _full((1, 256)), _full((1, 128)), _full((1, 128)), _row(128), _row(128),
                  _full((BLK, BLK))],
        out_specs=[_row(256), _row(128), _row(128), _row(128)],
        out_shape=[jax.ShapeDtypeStruct((lp, 256), BF16), jax.ShapeDtypeStruct((lp, 128), BF16),
                   jax.ShapeDtypeStruct((lp, 128), BF16), jax.ShapeDtypeStruct((lp, 128), F32)],
        scratch_shapes=[pltpu.VMEM((1, 128), F32)],
        compiler_params=_cp(("arbitrary",)))(small, gq, gkv, fb, ctab, stab, tri)


def _rope_q(qraw, ctab, stab, *, inverse, out_dtype, name):
    lp = qraw.shape[0]

    def body(q_ref, c_ref, s_ref, o_ref):
        c = c_ref[...]
        s = -s_ref[...] if inverse else s_ref[...]
        for p in range(PAIRS):
            lo = p * 256
            o_ref[:, lo:lo + 128] = q_ref[:, lo:lo + 128].astype(out_dtype)
            o_ref[:, lo + 128:lo + 256] = _rope(q_ref[:, lo + 128:lo + 256].astype(F32), c, s).astype(out_dtype)

    return pl.pallas_call(
        body, name=name, grid=(lp // BLK,),
        in_specs=[_row(PAIRS * 256), _row(128), _row(128)], out_specs=_row(PAIRS * 256),
        out_shape=jax.ShapeDtypeStruct((lp, PAIRS * 256), out_dtype),
        compiler_params=_cp(("parallel",)))(qraw, ctab, stab)


def _gate_fwd(o_mla, o_fox, gate):
    lp = o_mla.shape[0]

    def body(om_ref, of_ref, zm_ref, zf_ref, am_ref, af_ref):
        zm = zm_ref[...].astype(F32)
        am_ref[...] = (om_ref[...] * (zm * _sigmoid(zm))).astype(BF16)
        zf = zf_ref[...].astype(F32)
        af_ref[...] = (of_ref[...] * (zf * _sigmoid(zf))).astype(BF16)

    return pl.pallas_call(
        body, name="gate_fwd", grid=(lp // BLK,),
        in_specs=[_row(D_MODEL), _row(D_MODEL), _rowc(D_MODEL, 0), _rowc(D_MODEL, 1)],
        out_specs=[_row(D_MODEL), _row(D_MODEL)],
        out_shape=[jax.ShapeDtypeStruct((lp, D_MODEL), BF16)] * 2,
        compiler_params=_cp(("parallel",)))(o_mla, o_fox, gate, gate)


def _merge_fwd(gate, y_mla, y_fox):
    lp = y_mla.shape[0]

    def body(ga_ref, gb_ref, ym_ref, yf_ref, m_ref):
        sa = _sigmoid(ga_ref[...].astype(F32))
        sb = _sigmoid(gb_ref[...].astype(F32))
        m_ref[...] = (sa * ym_ref[...] + sb * yf_ref[...]).astype(BF16)

    return pl.pallas_call(
        body, name="merge_fwd", grid=(lp // BLK,),
        in_specs=[_rowc(D_MODEL, 2), _rowc(D_MODEL, 3), _row(D_MODEL), _row(D_MODEL)],
        out_specs=_row(D_MODEL), out_shape=jax.ShapeDtypeStruct((lp, D_MODEL), BF16),
        compiler_params=_cp(("parallel",)))(gate, gate, y_mla, y_fox)


def _tail(h, mixed, tgt, gpost):
    lp = h.shape[0]
    shift = pl.BlockSpec((BLK, D_MODEL), lambda i: (jnp.maximum(i - 1, 0), 0))

    def body(h_ref, mx_ref, t_ref, g_ref, dmx_ref, dy_ref, loss_ref, dg_ref):
        i = pl.program_id(0)

        @pl.when(i == 0)
        def _():
            loss_ref[...] = jnp.zeros_like(loss_ref)
            dg_ref[...] = jnp.zeros_like(dg_ref)
            dmx_ref[...] = jnp.zeros_like(dmx_ref)
            dy_ref[...] = jnp.zeros_like(dy_ref)

        @pl.when(i > 0)
        def _():
            mx = mx_ref[...]
            g = g_ref[...]
            r = lax.rsqrt(jnp.mean(mx * mx, axis=-1, keepdims=True) + RMS_EPS)
            nrm = mx * r
            e = (h_ref[...] + nrm * g) - t_ref[...]
            loss_ref[...] += jnp.sum(0.5 * jnp.sum(e * e, axis=-1, keepdims=True) * (1.0 / D_MODEL),
                                     axis=0, keepdims=True)
            dy = e * (1.0 / D_MODEL)
            dy_ref[...] = dy
            dg_ref[...] += jnp.sum(dy * nrm, axis=0, keepdims=True)
            w = dy * g
            dot = jnp.mean(w * mx, axis=-1, keepdims=True)
            dmx_ref[...] = (r * w - mx * (r * r * r * dot)).astype(BF16)

    return pl.pallas_call(
        body, name="tail", grid=(lp // BLK,),
        in_specs=[_row(D_MODEL), _row(D_MODEL), shift, _full((1, D_MODEL))],
        out_specs=[_row(D_MODEL), _row(D_MODEL), _full((1, 1)), _full((1, D_MODEL))],
        out_shape=[jax.ShapeDtypeStruct((lp, D_MODEL), BF16), jax.ShapeDtypeStruct((lp, D_MODEL), F32),
                   jax.ShapeDtypeStruct((1, 1), F32), jax.ShapeDtypeStruct((1, D_MODEL), F32)],
        compiler_params=_cp(("arbitrary",)))(h, mixed, tgt, gpost)


def _merge_bwd(dm, gate, y_mla, y_fox):
    lp = dm.shape[0]

    def body(dm_ref, ga_ref, gb_ref, ym_ref, yf_ref, dym_ref, dyf_ref, dg_ref):
        dm_v = dm_ref[...]
        sa = _sigmoid(ga_ref[...].astype(F32))
        sb = _sigmoid(gb_ref[...].astype(F32))
        dym_ref[...] = (dm_v * sa).astype(BF16)
        dyf_ref[...] = (dm_v * sb).astype(BF16)
        dg_ref[:, 0:D_MODEL] = (dm_v * ym_ref[...] * (sa * (1.0 - sa))).astype(BF16)
        dg_ref[:, D_MODEL:2 * D_MODEL] = (dm_v * yf_ref[...] * (sb * (1.0 - sb))).astype(BF16)

    return pl.pallas_call(
        body, name="merge_bwd", grid=(lp // BLK,),
        in_specs=[_row(D_MODEL), _rowc(D_MODEL, 2), _rowc(D_MODEL, 3), _row(D_MODEL), _row(D_MODEL)],
        out_specs=[_row(D_MODEL), _row(D_MODEL), _row(2 * D_MODEL)],
        out_shape=[jax.ShapeDtypeStruct((lp, D_MODEL), BF16), jax.ShapeDtypeStruct((lp, D_MODEL), BF16),
                   jax.ShapeDtypeStruct((lp, 2 * D_MODEL), BF16)],
        compiler_params=_cp(("parallel",)))(dm, gate, gate, y_mla, y_fox)


def _gate_bwd(da_mla, da_fox, o_mla, o_fox, gate):
    lp = da_mla.shape[0]

    def one(da, o, z):
        sg = _sigmoid(z)
        do = da * (z * sg)
        dz = da * o * (sg * (1.0 + z * (1.0 - sg)))
        return do.astype(BF16), dz.astype(BF16)

    def body(dam_ref, daf_ref, om_ref, of_ref, zm_ref, zf_ref, dom_ref, dof_ref, dz_ref):
        dom_ref[...], dz_ref[:, 0:D_MODEL] = one(dam_ref[...], om_ref[...], zm_ref[...].astype(F32))
        dof_ref[...], dz_ref[:, D_MODEL:2 * D_MODEL] = one(daf_ref[...], of_ref[...], zf_ref[...].astype(F32))

    return pl.pallas_call(
        body, name="gate_bwd", grid=(lp // BLK,),
        in_specs=[_row(D_MODEL)] * 4 + [_rowc(D_MODEL, 0), _rowc(D_MODEL, 1)],
        out_specs=[_row(D_MODEL), _row(D_MODEL), _row(2 * D_MODEL)],
        out_shape=[jax.ShapeDtypeStruct((lp, D_MODEL), BF16), jax.ShapeDtypeStruct((lp, D_MODEL), BF16),
                   jax.ShapeDtypeStruct((lp, 2 * D_MODEL), BF16)],
        compiler_params=_cp(("parallel",)))(da_mla, da_fox, o_mla, o_fox, gate, gate)


def _small_bwd(small, dqn, dkvn, dkr, dcol_t, drow_t, gq, gkv, fb, ctab, stab, triu):
    lp = small.shape[0]
    nb = lp // BLK

    def rrow(w):
        return pl.BlockSpec((BLK, w), lambda i: (nb - 1 - i, 0))

    def body(sm_ref, dqn_ref, dkvn_ref, dkr_ref, dcol_ref, drow_ref, gq_ref, gkv_ref, fb_ref, c_ref, s_ref, tri_ref,
             ds_ref, dgq_ref, dgkv_ref, dfb_ref, carry):
        i = pl.program_id(0)

        @pl.when(i == 0)
        def _():
            carry[...] = jnp.zeros_like(carry)
            dgq_ref[...] = jnp.zeros_like(dgq_ref)
            dgkv_ref[...] = jnp.zeros_like(dgkv_ref)
            dfb_ref[...] = jnp.zeros_like(dfb_ref)

        def norm_bwd(x, dn, g, dg_ref):
            r = lax.rsqrt(jnp.mean(x * x, axis=-1, keepdims=True) + RMS_EPS)
            dg_ref[...] += jnp.sum(dn * (x * r), axis=0, keepdims=True)
            w = dn * g
            dot = jnp.mean(w * x, axis=-1, keepdims=True)
            return r * w - x * (r * r * r * dot)

        ds_ref[:, 0:256] = norm_bwd(sm_ref[:, 0:256], dqn_ref[...], gq_ref[...], dgq_ref).astype(BF16)
        ds_ref[:, 256:384] = norm_bwd(sm_ref[:, 256:384], dkvn_ref[...], gkv_ref[...], dgkv_ref).astype(BF16)

        dk = dkr_ref[0]
        for p in range(1, PAIRS):
            dk = dk + dkr_ref[p]
        dk = _rope(dk, c_ref[...], -s_ref[...])
        lane = lax.broadcasted_iota(jnp.int32, dk.shape, 1)
        dk = jnp.where(lane < MLA_ROPE, dk + pltpu.roll(dk, LANES - MLA_ROPE, 1), 0.0)
        ds_ref[:, 384:512] = dk.astype(BF16)

        dcr = dcol_ref[...] - drow_ref[...]
        hi, mid, lo = _split3(dcr)
        t = tri_ref[...]
        suf = (_dot(t, hi, 1, 0) + _dot(t, mid, 1, 0)) + _dot(t, lo, 1, 0) + carry[...]
        fl = sm_ref[:, 512:640] + fb_ref[...]
        dfl = jnp.where(_row_valid(nb - 1 - i), -suf * _sigmoid(-fl), 0.0)
        ds_ref[:, 512:640] = dfl.astype(BF16)
        dfb_ref[...] += jnp.sum(dfl, axis=0, keepdims=True)
        carry[...] += jnp.sum(dcr, axis=0, keepdims=True)

    return pl.pallas_call(
        body, name="small_bwd", grid=(nb,),
        in_specs=[rrow(SMALL_W), rrow(256), rrow(128),
                  pl.BlockSpec((PAIRS, BLK, 128), lambda i: (0, nb - 1 - i, 0)), rrow(128), rrow(128),
                  _full((1, 256)), _full((1, 128)), _full((1, 128)), rrow(128), rrow(128), _full((BLK, BLK))],
        out_specs=[rrow(SMALL_W), _full((1, 256)), _full((1, 128)), _full((1, 128))],
        out_shape=[jax.ShapeDtypeStruct((lp, SMALL_W), BF16), jax.ShapeDtypeStruct((1, 256), F32),
                   jax.ShapeDtypeStruct((1, 128), F32), jax.ShapeDtypeStruct((1, 128), F32)],
        scratch_shapes=[pltpu.VMEM((1, 128), F32)],
        compiler_params=_cp(("arbitrary",)))(small, dqn, dkvn, dkr, dcol_t, drow_t, gq, gkv, fb, ctab, stab, triu)


def _pre_bwd(du, h, dy, gpre, s_rows):
    lp = h.shape[0]
    shift = pl.BlockSpec((BLK, D_MODEL), lambda i: (jnp.maximum(i - 1, 0), 0))

    def body(du_ref, h_ref, dy_ref, g_ref, dx_ref, dmeta_ref, dg_ref):
        i = pl.program_id(0)

        @pl.when(i == 0)
        def _():
            dg_ref[...] = jnp.zeros_like(dg_ref)

        hv = h_ref[...]
        duv = du_ref[...]
        r = lax.rsqrt(jnp.mean(hv * hv, axis=-1, keepdims=True) + RMS_EPS)
        dg_ref[...] += jnp.sum(duv * (hv * r), axis=0, keepdims=True)
        w = duv * g_ref[...]
        dot = jnp.mean(w * hv, axis=-1, keepdims=True)
        dh = dy_ref[...] + (r * w - hv * (r * r * r * dot))
        dx_ref[...] = dh

        @pl.when(i == 0)
        def _():
            dmeta_ref[...] = dh[0:N_META, :]

    return pl.pallas_call(
        body, name="pre_bwd", grid=(lp // BLK,),
        in_specs=[_row(D_MODEL), _row(D_MODEL), _row(D_MODEL), _full((1, D_MODEL))],
        out_specs=[shift, _full((N_META, D_MODEL)), _full((1, D_MODEL))],
        out_shape=[jax.ShapeDtypeStruct((s_rows, D_MODEL), F32), jax.ShapeDtypeStruct((N_META, D_MODEL), F32),
                   jax.ShapeDtypeStruct((1, D_MODEL), F32)],
        compiler_params=_cp(("arbitrary",)))(du, h, dy, gpre)


def _pair_masks(rope):
    lane = lax.broadcasted_iota(jnp.int32, (1, LANES), 1)
    mas = [lane < HEAD_DIM, lane >= HEAD_DIM]
    if not rope:
        return mas, mas
    wide = lax.broadcasted_iota(jnp.int32, (1, 2 * LANES), 1)
    rope_lo = LANES + MLA_ROPE
    return mas, [(wide < HEAD_DIM) | ((wide >= LANES) & (wide < rope_lo)),
                 ((wide >= HEAD_DIM) & (wide < LANES)) | ((wide >= rope_lo) & (wide < rope_lo + MLA_ROPE))]


def _mask2(x, masks):
    return [jnp.where(m, x, jnp.zeros_like(x)) for m in masks]


def _attn_fwd(q, k, v, *, kr=None, nbrep=None, scale, qcol, kcol, vcol, name):
    lp = q.shape[0]
    nq = 1 + (lp - PAD) // QB
    rope = kr is not None
    bias = nbrep is not None
    qw = 256 if rope else 128

    def body(*refs):
        it = iter(refs)
        q_ref, k_ref, v_ref = next(it), next(it), next(it)
        kr_ref = next(it) if rope else None
        nb_ref = next(it) if bias else None
        o_ref, lse_ref = next(it), next(it)
        i = pl.program_id(1)
        r0 = pl.multiple_of(jnp.where(i == 0, 0, PAD + QB * (i - 1)), BLK)
        b0 = r0 // BLK
        mas, hmask = _pair_masks(rope)
        qh = _mask2(q_ref[pl.ds(r0, QB), :], hmask)
        if bias:
            qh = [x * scale for x in qh]

        def causal(kc, n):
            key = kc * BLK + lax.broadcasted_iota(jnp.int32, (n, QB), 0)
            return (key <= r0 + lax.broadcasted_iota(jnp.int32, (n, QB), 1)) & ((kc > 0) | (n == N_META))

        def update(kcs, carry, masks, n=BLK):
            stats, acc = carry[:4], carry[4]
            k0s = [pl.multiple_of(kc * BLK, BLK) for kc in kcs]
            kks = [k_ref[pl.ds(k0, n), :] for k0 in k0s]
            if rope:
                kks = [jnp.concatenate([kk, kr_ref[pl.ds(k0, n), :]], axis=1) for kk, k0 in zip(kks, k0s)]
            new_stats, alphas, ps = [], [], [[] for _ in kcs]
            for h in range(2):
                m_prev, l_prev = stats[2 * h], stats[2 * h + 1]
                ss = []
                for kk, k0, mask in zip(kks, k0s, masks):
                    s = _dot(kk, qh[h], 1, 1)
                    if rope:
                        s = s * scale
                    if bias:
                        nbc = nb_ref[h, pl.ds(k0, n), :]
                        s = s + jnp.concatenate([nbc] * (QB // LANES), axis=1)
                    if mask is not None:
                        s = jnp.where(mask, s, NEG)
                    ss.append(s)
                m_new = m_prev
                for s in ss:
                    m_new = jnp.maximum(m_new, jnp.max(s, axis=0, keepdims=True))
                alpha = jnp.exp(m_prev - m_new)
                l_new = alpha * l_prev
                for j, s in enumerate(ss):
                    p = jnp.exp(s - m_new)
                    l_new = l_new + jnp.sum(p, axis=0, keepdims=True)
                    ps[j].append(p.astype(BF16))
                new_stats += [m_new, l_new]
                alphas.append(alpha)
            vcat = jnp.concatenate([x for k0 in k0s for x in _mask2(v_ref[pl.ds(k0, n), :], mas)], axis=0)
            pv = _dot(vcat, jnp.concatenate([p for pj in ps for p in pj], axis=0), 0, 0)
            a_full = jnp.concatenate([jnp.broadcast_to(a, (HEAD_DIM, QB)) for a in alphas], axis=0)
            return (*new_stats, a_full * acc + pv)

        neg = jnp.full((1, QB), NEG, F32)
        zero = jnp.zeros((1, QB), F32)
        c = update([0], (neg, zero, neg, zero, jnp.zeros((LANES, QB), F32)), [causal(0, N_META)], N_META)
        c = lax.fori_loop(0, (b0 - 1) // 2, lambda t, cr: update([1 + 2 * t, 2 + 2 * t], cr, [None, None]), c)
        c = update([b0, b0 + 1], c, [causal(b0, BLK), causal(b0 + 1, BLK)])
        inv =jnp.concatenate([jnp.broadcast_to(1.0 / c[1], (HEAD_DIM, QB)),
                               jnp.broadcast_to(1.0 / c[3], (HEAD_DIM, QB))], axis=0)
        o_t = (c[4] * inv).T
        lses = [c[0] + jnp.log(c[1]), c[2] + jnp.log(c[3])]
        o_ref[pl.ds(r0, BLK), :] = o_t[0:BLK]
        for h in range(2):
            lse_ref[0, h:h + 1, pl.ds(r0, BLK)] = lses[h][:, 0:BLK]

        @pl.when(i > 0)
        def _():
            r1 = pl.multiple_of(r0 + BLK, BLK)
            o_ref[pl.ds(r1, QB - BLK), :] = o_t[BLK:QB]
            for h in range(2):
                lse_ref[0, h:h + 1, pl.ds(r1, QB - BLK)] = lses[h][:, BLK:QB]

    in_specs = [pl.BlockSpec((lp, qw), lambda p, i: (0, qcol + p)),
                pl.BlockSpec((lp, 128), lambda p, i: (0, kcol(p))),
                pl.BlockSpec((lp, 128), lambda p, i: (0, vcol(p)))]
    ins = [q, k, v]
    if rope:
        in_specs.append(pl.BlockSpec((lp, 128), lambda p, i: (0, 0)))
        ins.append(kr)
    if bias:
        in_specs.append(pl.BlockSpec((2, lp, 128), lambda p, i: (p, 0, 0)))
        ins.append(nbrep)
    return pl.pallas_call(
        body, name=name, grid=(PAIRS, nq), in_specs=in_specs,
        out_specs=[pl.BlockSpec((lp, 128), lambda p, i: (0, p)),
                   pl.BlockSpec((1, 2, lp), lambda p, i: (p, 0, 0))],
        out_shape=[jax.ShapeDtypeStruct((lp, D_MODEL), F32), jax.ShapeDtypeStruct((PAIRS, 2, lp), F32)],
        compiler_params=_cp(("parallel", "arbitrary"), VMEM_BIG))(*ins)


def _attn_bwd(q, k, v, do, o, lse, *, kr=None, nbrep=None, scale, qcol, kcol, vcol, name):
    lp = q.shape[0]
    nb = lp // BLK
    rope = kr is not None
    bias = nbrep is not None
    qw = 256 if rope else 128

    def body(*refs):
        it = iter(refs)
        q_ref, k_ref, v_ref = next(it), next(it), next(it)
        kr_ref = next(it) if rope else None
        nb_ref = next(it) if bias else None
        do_ref, o_ref, lse_ref = next(it), next(it), next(it)
        dq_ref, dk_ref, dv_ref = next(it), next(it), next(it)
        x_ref = next(it)
        drow_ref = next(it) if bias else None
        delta = next(it)
        kb = pl.program_id(1)
        mas, hmask = _pair_masks(rope)
        lane = lax.broadcasted_iota(jnp.int32, (1, LANES), 1)

        @pl.when(kb == 0)
        def _():
            dq_ref[...] = jnp.zeros_like(dq_ref)
            if bias:
                drow_ref[...] = jnp.zeros_like(drow_ref)
            sub = lax.broadcasted_iota(jnp.int32, (8, LANES), 0)
            sel = (((sub == 0) & mas[0]) | ((sub == 1) & mas[1])).astype(BF16)

            def dstep(c, carry):
                r0 = pl.multiple_of(c * BLK, BLK)
                prod = do_ref[pl.ds(r0, BLK), :].astype(F32) * o_ref[pl.ds(r0, BLK), :]
                hi, mid, lo = _split3(prod)
                delta[:, pl.ds(r0, BLK)] = (_dot(sel, hi, 1, 1) + _dot(sel, mid, 1, 1)) + _dot(sel, lo, 1, 1)
                return carry

            lax.fori_loop(0, nb, dstep, 0)

        def masked_q(q0):
            qh = _mask2(q_ref[pl.ds(q0, BLK), :], hmask)
            return [x * scale for x in qh] if bias else qh

        def key_pass(n):
            kk = k_ref[0:n, :]
            if rope:
                kk = jnp.concatenate([kk, kr_ref[0:n, :]], axis=1)
            vh = _mask2(v_ref[0:n, :], mas)
            kcat = jnp.concatenate(_mask2(kk, hmask), axis=0)
            if bias:
                kcat = kcat * scale
                nbc = [jnp.concatenate([nb_ref[h, 0:n, :], nb_ref[h, 0:n, :]], axis=1) for h in range(2)]
            diag_mask = (lax.broadcasted_iota(jnp.int32, (n, BLK), 0) <= lax.broadcasted_iota(jnp.int32, (n, BLK), 1))

            def chunk(qc, carry, mask):
                carry = list(carry)
                q0 = pl.multiple_of(qc * BLK, BLK)
                dov = do_ref[pl.ds(q0, BLK), :]
                doh = _mask2(dov, mas)
                qh = masked_q(q0)
                pbs, dss = [], []
                for h in range(2):
                    s = _dot(kk, qh[h], 1, 1)
                    if rope:
                        s = s * scale
                    if bias:
                        s = s + nbc[h]
                    p = jnp.exp(s - lse_ref[0, h:h + 1, pl.ds(q0, BLK)])
                    if mask is not None:
                        p = jnp.where(mask, p, 0.0)
                    ds = p * (_dot(vh[h], dov, 1, 1) - delta[h:h + 1, pl.ds(q0, BLK)])
                    if bias:
                        drow_ref[0, h:h + 1, pl.ds(q0, BLK)] += jnp.sum(ds, axis=0, keepdims=True)
                        carry[2 + h] = carry[2 + h] + jnp.sum(ds, axis=1, keepdims=True)
                    else:
                        ds = ds * scale
                    pbs.append(p.astype(BF16))
                    dss.append(ds.astype(BF16))
                ds_lanes = jnp.concatenate(dss, axis=1)
                ds_rows = jnp.concatenate(dss, axis=0)
                carry[0] = carry[0] + _dot(ds_lanes, jnp.concatenate(qh, axis=0), 1, 0)
                carry[1] = carry[1] + _dot(jnp.concatenate(pbs, axis=1), jnp.concatenate(doh, axis=0), 1, 0)
                dq_ref[pl.ds(q0, BLK), :] += _dot(ds_rows, kcat, 0, 0)
                return tuple(carry)

            init = [jnp.zeros((n, qw), F32), jnp.zeros((n, LANES), F32)]
            if bias:
                init += [jnp.zeros((n, 1), F32), jnp.zeros((n, 1), F32)]
            c = chunk(kb, tuple(init), diag_mask)
            rest = nb - 1 - kb

            def several(t, cr):
                for u in range(UNROLL):
                    cr = chunk(kb + 1 + UNROLL * t + u, cr, None)
                return cr

            c = lax.fori_loop(0, rest // UNROLL, several, c)
            c = lax.fori_loop(nb - rest % UNROLL, nb, lambda qc, cr: chunk(qc, cr, None), c)

            def rows(a, dtype):
                a = a.astype(dtype)
                return a if n == BLK else jnp.concatenate([a, jnp.zeros((BLK - n, a.shape[1]), dtype)], axis=0)

            dk_ref[...] = rows(c[0][:, 0:LANES], BF16)
            dv_ref[...] = rows(c[1], BF16)
            if rope:
                x_ref[0] = rows(c[0][:, LANES:2 * LANES], F32)
            if bias:
                x_ref[0] = rows(jnp.where(lane == 0, c[2], jnp.where(lane == 1, c[3], 0.0)), F32)

        @pl.when(kb == 0)
        def _():
            key_pass(N_META)

        @pl.when(kb > 0)
        def _():
            key_pass(BLK)

    in_specs = [pl.BlockSpec((lp, qw), lambda p, j: (0, qcol + p)),
                pl.BlockSpec((BLK, 128), lambda p, j: (j, kcol(p))),
                pl.BlockSpec((BLK, 128), lambda p, j: (j, vcol(p)))]
    ins = [q, k, v]
    if rope:
        in_specs.append(pl.BlockSpec((BLK, 128), lambda p, j: (j, 0)))
        ins.append(kr)
    if bias:
        in_specs.append(pl.BlockSpec((2, BLK, 128), lambda p, j: (p, j, 0)))
        ins.append(nbrep)
    in_specs += [pl.BlockSpec((lp, 128), lambda p, j: (0, p)), pl.BlockSpec((lp, 128), lambda p, j: (0, p)),
                 pl.BlockSpec((1, 2, lp), lambda p, j: (p, 0, 0))]
    ins += [do, o, lse]
    out_specs = [pl.BlockSpec((lp, qw), lambda p, j: (0, p)),
                 pl.BlockSpec((BLK, 128), lambda p, j: (j, p)),
                 pl.BlockSpec((BLK, 128), lambda p, j: (j, p)),
                 pl.BlockSpec((1, BLK, 128), lambda p, j: (p, j, 0))]
    out_shape = [jax.ShapeDtypeStruct((lp, PAIRS * qw), F32), jax.ShapeDtypeStruct((lp, D_MODEL), BF16),
                 jax.ShapeDtypeStruct((lp, D_MODEL), BF16), jax.ShapeDtypeStruct((PAIRS, lp, 128), F32)]
    if bias:
        out_specs.append(pl.BlockSpec((1, 2, lp), lambda p, j: (p, 0, 0)))
        out_shape.append(jax.ShapeDtypeStruct((PAIRS, 2, lp), F32))
    return pl.pallas_call(
        body, name=name, grid=(PAIRS, nb), in_specs=in_specs, out_specs=out_specs, out_shape=out_shape,
        scratch_shapes=[pltpu.VMEM((8, lp), F32)],
        compiler_params=_cp(("parallel", "arbitrary"), VMEM_BIG))(*ins)


def _adamw(w, g, m, v, name):
    lead = w.ndim - 2
    rows, cols = w.shape[lead:]
    tr = 128 if rows * cols > 512 * 1024 else rows

    def body(w_ref, g_ref, m_ref, v_ref, d_ref, nm_ref, nv_ref):
        gv = g_ref[...]
        nm = ADAM_B1 * m_ref[...] + (1.0 - ADAM_B1) * gv
        nv = ADAM_B2 * v_ref[...] + (1.0 - ADAM_B2) * (gv * gv)
        m_hat = nm / (1.0 - ADAM_B1 ** ADAM_STEP)
        v_hat = nv / (1.0 - ADAM_B2 ** ADAM_STEP)
        d_ref[...] = -ADAM_LR * (m_hat / (jnp.sqrt(v_hat) + ADAM_EPS) + ADAM_WD * w_ref[...])
        nm_ref[...] = nm
        nv_ref[...] = nv

    spec = pl.BlockSpec((1,) * lead + (tr, cols), lambda i: (0,) * lead + (i, 0))
    return pl.pallas_call(
        body, name=name, grid=(rows // tr,), in_specs=[spec] * 4, out_specs=[spec] * 3,
        out_shape=[jax.ShapeDtypeStruct(w.shape, F32)] * 3,
        compiler_params=_cp(("parallel",), VMEM_BIG))(w, g, m, v)


def _add_cores(g, from_sib, name):
    n, rows, cols = g.shape
    half = rows // 2
    tr = _tile(half, (256, 240))
    nt = half // tr

    def body(lo_ref, hi_ref, s_ref, o_ref):
        mine = jnp.where(lax.axis_index("c") == 0, lo_ref[0], hi_ref[0])
        o_ref[0] = (mine + s_ref[0]).astype(BF16)

    return pl.pallas_call(
        body, name=name, grid=(n, nt),
        in_specs=[pl.BlockSpec((1, tr, cols), lambda j, i: (j, i, 0)),
                  pl.BlockSpec((1, tr, cols), lambda j, i: (j, nt + i, 0)),
                  pl.BlockSpec((1, tr, cols), lambda j, i: (j, i, 0))],
        out_specs=pl.BlockSpec((1, tr, cols), lambda j, i: (j, i, 0)),
        out_shape=jax.ShapeDtypeStruct((n, half, cols), BF16),
        compiler_params=_cp(("parallel", "parallel"), VMEM_BIG))(g, g, from_sib)


def _add_chips(x, own, name):
    n, rows, cols = x.shape
    tr = _tile(rows, (256, 240))

    def body(x_ref, own_ref, o_ref):
        me = 2 * lax.axis_index("x") + lax.axis_index("y")
        v = [jnp.where(me == k, own_ref[...], x_ref[k]).astype(F32) for k in range(N_CHIPS)]
        o_ref[...] = ((v[0] + v[1]) + v[2]) + v[3]

    return pl.pallas_call(
        body, name=name, grid=(rows // tr,),
        in_specs=[pl.BlockSpec((n, tr, cols), lambda i: (0, i, 0)), pl.BlockSpec((tr, cols), lambda i: (i, 0))],
        out_specs=pl.BlockSpec((tr, cols), lambda i: (i, 0)),
        out_shape=jax.ShapeDtypeStruct((rows, cols), F32), compiler_params=_cp(("parallel",), VMEM_BIG))(x, own)


def _axes():
    return lax.axis_index("x"), lax.axis_index("y"), lax.axis_index("c")


def _other_chips(x, y):
    return [(1 - x, y), (x, 1 - y), (1 - x, 1 - y)]


ANY = pl.BlockSpec(memory_space=pl.ANY)


def _rcopy(src, dst, send_sems, recv_sems, k, to):
    return pltpu.make_async_remote_copy(src_ref=src, dst_ref=dst, send_sem=send_sems.at[k], recv_sem=recv_sems.at[k],
                                        device_id=to, device_id_type=MESH)


def _gather_weights(shards, meta):
    n = len(shards)

    def body(*refs):
        srcs, meta_ref = refs[:n], refs[n]
        outs, mout_ref = refs[n + 1:2 * n + 1], refs[2 * n + 1]
        send_sems, recv_sems = refs[2 * n + 2:]
        x, y, c = _axes()
        me = 2 * x + y
        sib = (x, y, 1 - c)
        chips = _other_chips(x, y)

        def half(t, chip_idx, cc):
            hr = shards[t].shape[0] // 2
            return outs[t].at[chip_idx, pl.ds(cc * hr, hr), :]

        first = []
        for j, (px, py) in enumerate(chips):
            for t in range(n):
                hr = shards[t].shape[0] // 2
                first.append(_rcopy(srcs[t].at[pl.ds(c * hr, hr), :], half(t, me, c), send_sems, recv_sems,
                                    3 * t + j, (px, py, c)))
            first.append(_rcopy(meta_ref, mout_ref.at[me], send_sems, recv_sems, 3 * n + j, (px, py, c)))
        for cp in first:
            cp.start()
        passed = []
        for j, (px, py) in enumerate(chips):
            src_chip = 2 * px + py
            for t in range(n):
                _rcopy(half(t, src_chip, c), half(t, src_chip, c), send_sems, recv_sems, 3 * t + j, sib).wait_recv()
                fwd = _rcopy(half(t, src_chip, c), half(t, src_chip, c), send_sems, recv_sems, 3 * (n + 1 + t) + j, sib)
                fwd.start()
                passed.append(fwd)
            _rcopy(mout_ref.at[src_chip], mout_ref.at[src_chip], send_sems, recv_sems, 3 * n + j, sib).wait_recv()
        for j, (px, py) in enumerate(chips):
            src_chip = 2 * px + py
            for t in range(n):
                _rcopy(half(t, src_chip, 1 - c), half(t, src_chip, 1 - c), send_sems, recv_sems,
                       3 * (n + 1 + t) + j, sib).wait_recv()
        for cp in first + passed:
            cp.wait_send()

    nsem = 3 * (2 * n + 1)
    return pl.pallas_call(
        body, name="gather_weights", in_specs=[ANY] * (n + 1), out_specs=[ANY] * (n + 1),
        out_shape=[jax.ShapeDtypeStruct((N_CHIPS,) + s.shape, s.dtype) for s in shards]
        + [jax.ShapeDtypeStruct((N_CHIPS,) + meta.shape, meta.dtype)],
        scratch_shapes=[pltpu.SemaphoreType.DMA((nsem,)), pltpu.SemaphoreType.DMA((nsem,))])(*shards, meta)


def _swap_halves(gs):
    n = len(gs)

    def body(*refs):
        srcs, outs = refs[:n], refs[n:2 * n]
        send_sems, recv_sems = refs[2 * n:]
        x, y, c = _axes()
        cps = []
        for t in range(n):
            hr = gs[t].shape[1] // 2
            for j in range(N_CHIPS):
                cps.append(_rcopy(srcs[t].at[j, pl.ds((1 - c) * hr, hr), :], outs[t].at[j], send_sems, recv_sems,
                                  N_CHIPS * t + j, (x, y, 1 - c)))
        for cp in cps:
            cp.start()
        for cp in cps:
            cp.wait()

    return pl.pallas_call(
        body, name="swap_halves", in_specs=[ANY] * n, out_specs=[ANY] * n,
        out_shape=[jax.ShapeDtypeStruct((N_CHIPS, g.shape[1] // 2, g.shape[2]), g.dtype) for g in gs],
        scratch_shapes=[pltpu.SemaphoreType.DMA((N_CHIPS * n,)), pltpu.SemaphoreType.DMA((N_CHIPS * n,))])(*gs)


def _scatter_chips(parts):
    n = len(parts)

    def body(*refs):
        srcs, outs = refs[:n], refs[n:2 * n]
        send_sems, recv_sems = refs[2 * n:]
        x, y, c = _axes()
        me = 2 * x + y
        cps = []
        for j, (px, py) in enumerate(_other_chips(x, y)):
            for t in range(n):
                cps.append(_rcopy(srcs[t].at[2 * px + py], outs[t].at[me], send_sems, recv_sems, 3 * t + j,
                                  (px, py, c)))
        for cp in cps:
            cp.start()
        for cp in cps:
            cp.wait()

    return pl.pallas_call(
        body, name="scatter_chips", in_specs=[ANY] * n, out_specs=[ANY] * n,
        out_shape=[jax.ShapeDtypeStruct(p.shape, p.dtype) for p in parts],
        scratch_shapes=[pltpu.SemaphoreType.DMA((3 * n,)), pltpu.SemaphoreType.DMA((3 * n,))])(*parts)


def _swap_reduced(rs):
    n = len(rs)

    def body(*refs):
        srcs, outs = refs[:n], refs[n:2 * n]
        send_sems, recv_sems = refs[2 * n:]
        x, y, c = _axes()
        cps = [_rcopy(srcs[t], outs[t], send_sems, recv_sems, t, (x, y, 1 - c)) for t in range(n)]
        for cp in cps:
            cp.start()
        for cp in cps:
            cp.wait()

    return pl.pallas_call(
        body, name="swap_reduced", in_specs=[ANY] * n, out_specs=[ANY] * n,
        out_shape=[jax.ShapeDtypeStruct(r.shape, r.dtype) for r in rs],
        scratch_shapes=[pltpu.SemaphoreType.DMA((n,)), pltpu.SemaphoreType.DMA((n,))])(*rs)


SMALL_ROWS = 24


def _allreduce_small(vec):
    def body(v_ref, out_ref, slots, send_sems, recv_sems):
        x, y, c = _axes()
        me = 4 * x + 2 * y + c
        slots[me] = v_ref[...]
        cps = []
        for k in range(1, 8):
            kx, ky, kc = (k >> 2) & 1, (k >> 1) & 1, k & 1
            peer = (1 - x if kx else x, 1 - y if ky else y, 1 - c if kc else c)
            cps.append(_rcopy(v_ref, slots.at[me], send_sems, recv_sems, k - 1, peer))
        for cp in cps:
            cp.start()
        for cp in cps:
            cp.wait()
        tot = slots[0]
        for k in range(1, 8):
            tot = tot + slots[k]
        out_ref[...] = tot

    return pl.pallas_call(
        body, name="allreduce_small",
        in_specs=[pl.BlockSpec(memory_space=pltpu.VMEM)], out_specs=pl.BlockSpec(memory_space=pltpu.VMEM),
        out_shape=jax.ShapeDtypeStruct((SMALL_ROWS, 128), F32),
        scratch_shapes=[pltpu.VMEM((8, SMALL_ROWS, 128), F32), pltpu.SemaphoreType.DMA((7,)),
                        pltpu.SemaphoreType.DMA((7,))])(vec)


def _pack_p2(w_uq, w_ukv, w_br_mla, w_br_fox, w_out, meta, dtype):
    parts = [w_uq.reshape(96, D_MODEL), w_ukv.reshape(64, D_MODEL), w_br_mla, w_br_fox, w_out,
             meta.reshape(4, D_MODEL), jnp.zeros((P2_ROWS - 932, D_MODEL), meta.dtype)]
    return jnp.concatenate([p.astype(dtype) for p in parts], axis=0)


def _unpack_p2(pk):
    return (pk[0:96].reshape(256, 384), pk[96:160].reshape(128, 512), pk[160:416], pk[416:672], pk[672:928],
            pk[928:932].reshape(N_META, 256))


def _uq_arrange(w):
    w3 = w.reshape(256, HEADS, 96)
    nope = w3[:, :, :64].reshape(256, PAIRS, 128)
    pe = w3[:, :, 64:].reshape(256, PAIRS, 64)
    return jnp.concatenate([nope, pe, jnp.zeros((256, PAIRS, 64), w.dtype)], axis=2).reshape(256, PAIRS * 256)


def _uq_restore(g):
    g3 = g.reshape(256, PAIRS, 256)
    nope = g3[:, :, :128].reshape(256, HEADS, 64)
    pe = g3[:, :, 128:192].reshape(256, HEADS, 32)
    return jnp.concatenate([nope, pe], axis=2).reshape(256, HEADS * 96)


def _ukv_arrange(w):
    w3 = w.reshape(128, HEADS, 128)
    return jnp.concatenate([w3[:, :, :64].reshape(128, 1024), w3[:, :, 64:].reshape(128, 1024)], axis=1)


def _ukv_restore(g):
    kn = g[:, :1024].reshape(128, HEADS, 64)
    vv = g[:, 1024:].reshape(128, HEADS, 64)
    return jnp.concatenate([kn, vv], axis=2).reshape(128, HEADS * 128)


def _rope_tables(lp):
    r = jnp.arange(lp)
    pos = jnp.where(r < N_META, r, jnp.where(r >= PAD, r - PAD + N_META, 0))
    half = MLA_ROPE // 2
    inv_freq = ROPE_THETA ** (-jnp.arange(half, dtype=F32) / half)
    ang = pos.astype(F32)[:, None] * inv_freq[None, :]
    cos, sin = jnp.cos(ang), jnp.sin(ang)
    one, zero = jnp.ones((lp, 64), F32), jnp.zeros((lp, 64), F32)
    return (jnp.concatenate([cos, cos, cos, cos, one], axis=1),
            jnp.concatenate([-sin, sin, -sin, sin, zero], axis=1))


def _pad_lanes(v, n=128):
    return jnp.pad(v, ((0, 0), (0, n - v.shape[1])))


def _in_cols(slabs, a, b):
    out = []
    for j in range(N_CHIPS):
        lo, hi = max(a, W_IN_SHARD * j), min(b, W_IN_SHARD * (j + 1))
        if lo < hi:
            out.append(slabs[j][:, lo - W_IN_SHARD * j:hi - W_IN_SHARD * j])
    return out


def _local_step(x2, tgt2, meta_f, w_small, w_attn, w_gate, w_uq_f, w_ukv_f, w_bm, w_bf, w_o, pre_norm_g,
                post_norm_g, mla_q_norm_g, mla_kv_norm_g, fox_forget_b):
    s_rows = x2.shape[0]
    lp = PAD + s_rows
    w_uq_a = _uq_arrange(w_uq_f)
    w_ukv_a = _ukv_arrange(w_ukv_f)

    ctab, stab = _rope_tables(lp)
    ii = jnp.arange(BLK)
    tri_lo = (ii[:, None] >= ii[None, :]).astype(BF16)
    tri_up = (ii[:, None] <= ii[None, :]).astype(BF16)
    fb128 = _pad_lanes(fox_forget_b)

    h = jnp.concatenate([meta_f, jnp.zeros((PAD - N_META, D_MODEL), F32), x2], axis=0)
    u = _rms_pre(h, pre_norm_g)
    small = _mm(u, w_small, mode="nn", out_dtype=F32, name="proj_small")
    attn = _mm(u, w_attn, mode="nn", out_dtype=BF16, name="proj_attn")
    gate = _mm(u, w_gate, mode="nn", out_dtype=BF16, name="proj_gate")
    qn, kvn, kr, ncum = _small_prep(small, mla_q_norm_g, mla_kv_norm_g, fb128, ctab, stab, tri_lo)
    qraw = _mm(qn, w_uq_a, mode="nn", out_dtype=F32, name="mla_q")
    qcat = _rope_q(qraw, ctab, stab, inverse=False, out_dtype=BF16, name="rope_q")
    kv = _mm(kvn, w_ukv_a, mode="nn", out_dtype=BF16, name="mla_kv")
    nbrep = jnp.broadcast_to(ncum[:, :HEADS].T[:, :, None], (HEADS, lp, LANES))

    mla_cols = dict(qcol=0, kcol=lambda p: p, vcol=lambda p: PAIRS + p)
    fox_cols = dict(qcol=0, kcol=lambda p: PAIRS + p, vcol=lambda p: 2 * PAIRS + p)
    o_mla, lse_mla = _attn_fwd(qcat, kv, kv, kr=kr, scale=MLA_SCALE, name="mla_fwd", **mla_cols)
    o_fox, lse_fox = _attn_fwd(attn, attn, attn, nbrep=nbrep, scale=FOX_SCALE, name="fox_fwd", **fox_cols)

    a_mla, a_fox = _gate_fwd(o_mla, o_fox, gate)
    y_mla = _mm(a_mla, w_bm, mode="nn", out_dtype=F32, name="br_mla")
    y_fox = _mm(a_fox, w_bf, mode="nn", out_dtype=F32, name="br_fox")
    mg = _merge_fwd(gate, y_mla, y_fox)
    mixed = _mm(mg, w_o, mode="nn", out_dtype=F32, name="out_proj")
    dmixed, dy, loss_p, dg_post = _tail(h, mixed, tgt2, post_norm_g)

    d_w_out = _mm(mg, dmixed, mode="tn", out_dtype=F32, name="d_w_out")
    dm = _mm(dmixed, w_o, mode="nt", out_dtype=F32, name="d_merge")
    dy_mla, dy_fox, dgate_ab = _merge_bwd(dm, gate, y_mla, y_fox)
    d_w_bm = _mm(a_mla, dy_mla, mode="tn", out_dtype=F32, name="d_w_br_mla")
    d_w_bf = _mm(a_fox, dy_fox, mode="tn", out_dtype=F32, name="d_w_br_fox")
    da_mla = _mm(dy_mla, w_bm, mode="nt", out_dtype=F32, name="d_a_mla")
    da_fox = _mm(dy_fox, w_bf, mode="nt", out_dtype=F32, name="d_a_fox")
    do_mla, do_fox, dgate_z = _gate_bwd(da_mla, da_fox, o_mla, o_fox, gate)

    dqcat, dkn, dvm, dkr = _attn_bwd(qcat, kv, kv, do_mla, o_mla, lse_mla, kr=kr, scale=MLA_SCALE,
                                     name="mla_bwd", **mla_cols)
    dfq, dfk, dfv, dcol, drow = _attn_bwd(attn, attn, attn, do_fox, o_fox, lse_fox, nbrep=nbrep, scale=FOX_SCALE,
                                          name="fox_bwd", **fox_cols)

    dq_a = _rope_q(dqcat, ctab, stab, inverse=True, out_dtype=BF16, name="rope_q_bwd")
    d_w_uq_a = _mm(qn, dq_a, mode="tn", out_dtype=F32, name="d_w_uq")
    dqn = _mm(dq_a, w_uq_a, mode="nt", out_dtype=F32, name="d_qn")
    d_w_ukv_a = jnp.concatenate([_mm(kvn, dkn, mode="tn", out_dtype=F32, name="d_w_uk"),
                                 _mm(kvn, dvm, mode="tn", out_dtype=F32, name="d_w_uv")], axis=1)
    dkvn = _mm(dkn, w_ukv_a[:, :1024], mode="nt", out_dtype=F32, name="d_kvn_k")
    dkvn = _mm(dvm, w_ukv_a[:, 1024:], mode="nt", out_dtype=F32, name="d_kvn_v", acc=dkvn)
    dcol_t = _pad_lanes(jnp.transpose(dcol[:, :, 0:2], (1, 0, 2)).reshape(lp, HEADS))
    drow_t = _pad_lanes(drow.reshape(HEADS, lp).T)
    dsmall, dg_q, dg_kv, dfb = _small_bwd(small, dqn, dkvn, dkr, dcol_t, drow_t, mla_q_norm_g, mla_kv_norm_g,
                                          fb128, ctab, stab, tri_up)

    dw_small = _mm(u, dsmall, mode="tn", out_dtype=F32, name="d_w_small")
    dw_fq = _mm(u, dfq, mode="tn", out_dtype=F32, name="d_w_fq")
    dw_fk = _mm(u, dfk, mode="tn", out_dtype=F32, name="d_w_fk")
    dw_fv = _mm(u, dfv, mode="tn", out_dtype=F32, name="d_w_fv")
    dw_z = _mm(u, dgate_z, mode="tn", out_dtype=F32, name="d_w_z")
    dw_g = _mm(u, dgate_ab, mode="tn", out_dtype=F32, name="d_w_g")
    du = _mm(dsmall, w_small, mode="nt", out_dtype=F32, name="d_u_small")
    du = _mm(dfq, w_attn[:, 0:1024], mode="nt", out_dtype=F32, name="d_u_fq", acc=du)
    du = _mm(dfk, w_attn[:, 1024:2048], mode="nt", out_dtype=F32, name="d_u_fk", acc=du)
    du = _mm(dfv, w_attn[:, 2048:3072], mode="nt", out_dtype=F32, name="d_u_fv", acc=du)
    du = _mm(dgate_z, w_gate[:, 0:2048], mode="nt", out_dtype=F32, name="d_u_z", acc=du)
    du = _mm(dgate_ab, w_gate[:, 2048:4096], mode="nt", out_dtype=F32, name="d_u_g", acc=du)
    dx, dmeta, dg_pre = _pre_bwd(du, h, dy, pre_norm_g, s_rows)

    runs = [(dw_small[:, 0:416], C_CQ), (dw_z[:, 0:1024], C_ZMLA), (dw_fq, C_FQ), (dw_fk, C_FK), (dw_fv, C_FV),
            (dw_small[:, 512:528], C_FL), (dw_z[:, 1024:2048], C_ZFOX), (dw_g, C_GA)]
    slabs = []
    for j in range(N_CHIPS):
        lo, hi = W_IN_SHARD * j, W_IN_SHARD * (j + 1)
        cols = [a[:, max(lo, c0) - c0:min(hi, c0 + a.shape[1]) - c0] for a, c0 in runs
                if max(lo, c0) < min(hi, c0 + a.shape[1])]
        slabs.append(jnp.concatenate(cols, axis=1))
    d_w_in = jnp.stack(slabs, axis=0)
    d_w_uq = _uq_restore(d_w_uq_a)
    d_w_ukv = _ukv_restore(d_w_ukv_a)
    return (loss_p, dx, dmeta, d_w_in, d_w_uq, d_w_ukv, d_w_bm, d_w_bf, d_w_out, dg_pre, dg_post, dg_q, dg_kv, dfb)


def kernel(x, meta_tokens, pre_norm_g, w_in, fox_forget_b, mla_q_norm_g, mla_kv_norm_g, w_uq, w_ukv, w_br_mla, w_br_fox, w_out, post_norm_g, loss_target, m_meta_tokens, m_pre_norm_g, m_w_in, m_fox_forget_b, m_mla_q_norm_g, m_mla_kv_norm_g, m_w_uq, m_w_ukv, m_w_br_mla, m_w_br_fox, m_w_out, m_post_norm_g, v_meta_tokens, v_pre_norm_g, v_w_in, v_fox_forget_b, v_mla_q_norm_g, v_mla_kv_norm_g, v_w_uq, v_w_ukv, v_w_br_mla, v_w_br_fox, v_w_out, v_post_norm_g):
    me = 2 * lax.axis_index("x") + lax.axis_index("y")
    core = lax.axis_index("c")
    w_in_b = w_in.astype(BF16).reshape(D_MODEL, W_IN_SHARD)
    p2 = _pack_p2(w_uq[0], w_ukv[0], w_br_mla[0], w_br_fox[0], w_out[0], jnp.zeros((N_META, 256), F32), BF16)
    w_in_g, p2_g, meta_g = _gather_weights([w_in_b, p2], meta_tokens)
    slabs = [jnp.where(me == j, w_in_b, w_in_g[j]) for j in range(N_CHIPS)]
    pieces = [_unpack_p2(jnp.where(me == j, p2, p2_g[j])) for j in range(N_CHIPS)]
    w_uq_f = jnp.concatenate([p[0] for p in pieces], axis=1)
    w_ukv_f = jnp.concatenate([p[1] for p in pieces], axis=1)
    w_bm = jnp.concatenate([p[2] for p in pieces], axis=0)
    w_bf = jnp.concatenate([p[3] for p in pieces], axis=0)
    w_o = jnp.concatenate([p[4] for p in pieces], axis=0)
    meta_f = jnp.concatenate([jnp.where(me == j, meta_tokens, meta_g[j]) for j in range(N_CHIPS)], axis=1)
    kpe = _in_cols(slabs, C_KPE, C_ZMLA)
    w_small = jnp.concatenate(_in_cols(slabs, C_CQ, C_KPE) + kpe + kpe + [jnp.zeros((D_MODEL, 64), BF16)]
                              + _in_cols(slabs, C_FL, C_ZFOX) + [jnp.zeros((D_MODEL, 112), BF16)], axis=1)
    w_attn = jnp.concatenate(_in_cols(slabs, C_FQ, C_FL), axis=1)
    w_gate = jnp.concatenate(_in_cols(slabs, C_ZMLA, C_FQ) + _in_cols(slabs, C_ZFOX, C_END), axis=1)

    (loss_p, dx, dmeta, d_w_in, d_w_uq, d_w_ukv, d_w_bm, d_w_bf, d_w_out, dg_pre, dg_post, dg_q, dg_kv,
     dfb) = _local_step(x[0], loss_target[0], meta_f, w_small, w_attn, w_gate, w_uq_f, w_ukv_f, w_bm, w_bf, w_o,
                        pre_norm_g, post_norm_g, mla_q_norm_g, mla_kv_norm_g, fox_forget_b)

    g1 = d_w_in
    g2 = jnp.stack([_pack_p2(d_w_uq[:, 384 * j:384 * (j + 1)], d_w_ukv[:, 512 * j:512 * (j + 1)],
                             d_w_bm[256 * j:256 * (j + 1)], d_w_bf[256 * j:256 * (j + 1)],
                             d_w_out[256 * j:256 * (j + 1)], dmeta[:, 256 * j:256 * (j + 1)], F32)
                    for j in range(N_CHIPS)], axis=0)
    s1, s2 = _swap_halves([g1, g2])
    part1, part2 = _add_cores(g1, s1, "add_cores_w_in"), _add_cores(g2, s2, "add_cores_rest")
    landed = _scatter_chips([part1, part2])
    mine = [_add_chips(l, lax.dynamic_index_in_dim(p, me, 0, keepdims=False), nm)
            for l, p, nm in zip(landed, (part1, part2), ("add_chips_w_in", "add_chips_rest"))]
    theirs = _swap_reduced(mine)
    g_w_in, g_p2 = [jnp.concatenate([jnp.where(core == 0, a, b), jnp.where(core == 0, b, a)], axis=0)
                    for a, b in zip(mine, theirs)]
    g_w_uq, g_w_ukv, g_w_bm, g_w_bf, g_w_out, g_meta = _unpack_p2(g_p2)
    g_w_in = g_w_in[None]

    vec = jnp.concatenate([dg_pre.reshape(8, 128), dg_post.reshape(8, 128), dg_q.reshape(2, 128), dg_kv,
                           dfb, _pad_lanes(loss_p), jnp.zeros((3, 128), F32)], axis=0)
    tot = _allreduce_small(vec)
    loss = tot[20, 0]

    def small_pack(pre, post, gq_, gkv_, fb_):
        return jnp.concatenate([pre.reshape(8, 128), post.reshape(8, 128), gq_.reshape(2, 128), gkv_,
                                _pad_lanes(fb_), jnp.zeros((4, 128), F32)], axis=0)

    def small_unpack(t):
        return (t[0:8].reshape(1, 1024), t[8:16].reshape(1, 1024), t[16:18].reshape(1, 256), t[18:19],
                t[19:20, 0:HEADS])

    g_small = jnp.concatenate([tot[0:20], jnp.zeros((4, 128), F32)], axis=0)
    sm = _adamw(small_pack(pre_norm_g, post_norm_g, mla_q_norm_g, mla_kv_norm_g, fox_forget_b), g_small,
                small_pack(m_pre_norm_g, m_post_norm_g, m_mla_q_norm_g, m_mla_kv_norm_g, m_fox_forget_b),
                small_pack(v_pre_norm_g, v_post_norm_g, v_mla_q_norm_g, v_mla_kv_norm_g, v_fox_forget_b),
                "adamw_small")
    g_pre, g_post, g_q, g_kv, g_fb = small_unpack(g_small)
    (d_pre, d_post, d_q, d_kv, d_fb), (nm_pre, nm_post, nm_q, nm_kv, nm_fb), (nv_pre, nv_post, nv_q, nv_kv, nv_fb) = (
        small_unpack(t) for t in sm)

    d_meta, nm_meta, nv_meta = _adamw(meta_tokens, g_meta, m_meta_tokens, v_meta_tokens, "adamw_meta")
    d_win, nm_win, nv_win = _adamw(w_in, g_w_in, m_w_in, v_w_in, "adamw_w_in")
    d_wuq, nm_wuq, nv_wuq = _adamw(w_uq[0], g_w_uq, m_w_uq[0], v_w_uq[0], "adamw_w_uq")
    d_wukv, nm_wukv, nv_wukv = _adamw(w_ukv[0], g_w_ukv, m_w_ukv[0], v_w_ukv[0], "adamw_w_ukv")
    d_wbm, nm_wbm, nv_wbm = _adamw(w_br_mla[0], g_w_bm, m_w_br_mla[0], v_w_br_mla[0], "adamw_w_br_mla")
    d_wbf, nm_wbf, nv_wbf = _adamw(w_br_fox[0], g_w_bf, m_w_br_fox[0], v_w_br_fox[0], "adamw_w_br_fox")
    d_wo, nm_wo, nv_wo = _adamw(w_out[0], g_w_out, m_w_out[0], v_w_out[0], "adamw_w_out")

    def group(meta_, pre, win, fb_, q_, kv_, wuq, wukv, wbm, wbf, wo, post):
        return (meta_, pre, win, fb_, q_, kv_, wuq[None], wukv[None], wbm[None], wbf[None], wo[None], post)

    grads = group(g_meta, g_pre, g_w_in, g_fb, g_q, g_kv, g_w_uq, g_w_ukv, g_w_bm, g_w_bf, g_w_out, g_post)
    deltas = group(d_meta, d_pre, d_win, d_fb, d_q, d_kv, d_wuq, d_wukv, d_wbm, d_wbf, d_wo, d_post)
    new_m = group(nm_meta, nm_pre, nm_win, nm_fb, nm_q, nm_kv, nm_wuq, nm_wukv, nm_wbm, nm_wbf, nm_wo, nm_post)
    new_v = group(nv_meta, nv_pre, nv_win, nv_fb, nv_q, nv_kv, nv_wuq, nv_wukv, nv_wbm, nv_wbf, nv_wo, nv_post)
    return (loss, dx[None], *grads, *deltas, *new_m, *new_v)
```

```python
import math

import jax
import jax.numpy as jnp
import numpy as np
from jax import lax
from jax.experimental import pallas as pl
from jax.experimental.pallas import tpu as pltpu

F32 = jnp.float32
BF16 = jnp.bfloat16

D_MODEL = 1024
N_META = 16
RMS_EPS = 1e-6
HEADS = 16
PAIRS = HEADS // 2
HEAD_DIM = 64
LANES = 128
MLA_ROPE = 32
MLA_SCALE = 1.0 / math.sqrt(64 + 32)
FOX_SCALE = 1.0 / math.sqrt(64)
ROPE_THETA = 10000.0

PAD = 256
BLK = 256
QB = 512
UNROLL = 4
NEG = -1e30

C_CQ, C_CKV, C_KPE, C_ZMLA, C_FQ, C_FK, C_FV, C_FL, C_ZFOX, C_GA, C_GB, C_END = (
    0, 256, 384, 416, 1440, 2464, 3488, 4512, 4528, 5552, 6576, 7600)
SMALL_W = 640
W_IN_SHARD = 1900

P2_ROWS = 960
N_CHIPS = 4

ADAM_LR = 0.001
ADAM_B1 = 0.9
ADAM_B2 = 0.999
ADAM_EPS = 1e-08
ADAM_WD = 0.01
ADAM_STEP = 10

VMEM_BIG = 56 * 1024 * 1024
MM_VMEM_BUDGET = 44 * 1024 * 1024
MESH = pl.DeviceIdType.MESH


def _cp(dims, vmem=None):
    return pltpu.CompilerParams(dimension_semantics=dims, vmem_limit_bytes=vmem)


def _dot(a, b, ca, cb):
    return lax.dot_general(a, b, (((ca,), (cb,)), ((), ())), preferred_element_type=F32)


def _sigmoid(x):
    return 1.0 / (1.0 + jnp.exp(-x))


def _tile(n, cands):
    for c in cands:
        if n % c == 0:
            return c
    return n


def _mm(a, b, *, mode, out_dtype, name, acc=None):
    if mode == "nn":
        (M, K), N = a.shape, b.shape[1]
    elif mode == "nt":
        (M, K), N = a.shape, b.shape[0]
    else:
        (K, M), N = a.shape, b.shape[1]
    tm = _tile(M, (1088, 1024)) if M > 1024 else M
    tn = _tile(N, (1024,)) if N > 1024 else N
    nk = 1
    while True:
        tk = K // nk
        need = 2 * tk * (tm * a.dtype.itemsize + tn * b.dtype.itemsize) + tm * tn * (
            2 * jnp.dtype(out_dtype).itemsize + (8 if acc is not None else 0) + (4 if nk > 1 else 0))
        if need <= MM_VMEM_BUDGET or (tk // 2) % (16 if mode == "tn" else LANES) or tk <= 512:
            break
        nk *= 2
    ca, cb = {"nn": (1, 0), "nt": (1, 1), "tn": (0, 0)}[mode]
    a_spec = (pl.BlockSpec((tk, tm), lambda j, i, k: (k, i)) if mode == "tn"
              else pl.BlockSpec((tm, tk), lambda j, i, k: (i, k)))
    b_spec = (pl.BlockSpec((tn, tk), lambda j, i, k: (j, k)) if mode == "nt"
              else pl.BlockSpec((tk, tn), lambda j, i, k: (k, j)))
    o_spec = pl.BlockSpec((tm, tn), lambda j, i, k: (i, j))
    has_acc = acc is not None

    def body(*refs):
        a_ref, b_ref = refs[0], refs[1]
        acc_ref = refs[2] if has_acc else None
        o_ref = refs[3] if has_acc else refs[2]
        part = _dot(a_ref[...].astype(BF16), b_ref[...].astype(BF16), ca, cb)
        if nk == 1:
            if has_acc:
                part = part + acc_ref[...]
            o_ref[...] = part.astype(out_dtype)
        else:
            sc = refs[-1]
            k = pl.program_id(2)

            @pl.when(k == 0)
            def _():
                sc[...] = part + acc_ref[...] if has_acc else part

            @pl.when(k > 0)
            def _():
                sc[...] += part

            @pl.when(k == nk - 1)
            def _():
                o_ref[...] = sc[...].astype(out_dtype)

    ins = [a, b] + ([acc] if has_acc else [])
    in_specs = [a_spec, b_spec] + ([o_spec] if has_acc else [])
    return pl.pallas_call(
        body, name=name, grid=(N // tn, M // tm, nk), in_specs=in_specs, out_specs=o_spec,
        out_shape=jax.ShapeDtypeStruct((M, N), out_dtype),
        scratch_shapes=[pltpu.VMEM((tm, tn), F32)] if nk > 1 else [],
        compiler_params=_cp(("parallel", "parallel", "arbitrary"), VMEM_BIG))(*ins)


def _row(w):
    return pl.BlockSpec((BLK, w), lambda i: (i, 0))


def _rowc(w, c):
    return pl.BlockSpec((BLK, w), lambda i: (i, c))


def _full(shape):
    return pl.BlockSpec(shape, lambda i: tuple(0 for _ in shape))


def _rope(x, c, s):
    lane = lax.broadcasted_iota(jnp.int32, x.shape, 1)
    is_x1 = ((lane >> 4) & 1) == 0
    partner = jnp.where(is_x1, pltpu.roll(x, LANES - 16, 1), pltpu.roll(x, 16, 1))
    return x * c + partner * s


def _row_valid(i):
    rows = i * BLK + lax.broadcasted_iota(jnp.int32, (BLK, 1), 0)
    return (rows < N_META) | (rows >= PAD)


def _shift_rows(w):
    return pl.BlockSpec((BLK, w), lambda i: (jnp.maximum(i - 1, 0), 0))


def _h_block(i, x_ref, meta_ref):
    head = jnp.concatenate([meta_ref[...], jnp.zeros((BLK - N_META, D_MODEL), F32)], axis=0)
    return jnp.where(i == 0, head, x_ref[...])


def _rms_pre(x2, meta, g):
    lp = PAD + x2.shape[0]

    def body(x_ref, meta_ref, g_ref, u_ref):
        hv = _h_block(pl.program_id(0), x_ref, meta_ref)
        r = lax.rsqrt(jnp.mean(hv * hv, axis=-1, keepdims=True) + RMS_EPS)
        u_ref[...] = (hv * r * g_ref[...]).astype(BF16)

    return pl.pallas_call(
        body, name="rms_pre", grid=(lp // BLK,),
        in_specs=[_shift_rows(D_MODEL), _full((N_META, D_MODEL)), _full((1, D_MODEL))], out_specs=_row(D_MODEL),
        out_shape=jax.ShapeDtypeStruct((lp, D_MODEL), BF16),
        compiler_params=_cp(("parallel",)))(x2, meta, g)


def _split3(x):
    hi = x.astype(BF16)
    r1 = x - hi.astype(F32)
    mid = r1.astype(BF16)
    lo = (r1 - mid.astype(F32)).astype(BF16)
    return hi, mid, lo


def _small_prep(small, gq, gkv, fb, ctab, stab, tri):
    lp = small.shape[0]

    def body(sm_ref, gq_ref, gkv_ref, fb_ref, c_ref, s_ref, tri_ref, qn_ref, kvn_ref, kr_ref, ncum_ref, carry):
        i = pl.program_id(0)

        @pl.when(i == 0)
        def _():
            carry[...] = jnp.zeros_like(carry)

        cq = sm_ref[:, 0:256]
        r = lax.rsqrt(jnp.mean(cq * cq, axis=-1, keepdims=True) + RMS_EPS)
        qn_ref[...] = (cq * r * gq_ref[...]).astype(BF16)
        ckv = sm_ref[:, 256:384]
        r = lax.rsqrt(jnp.mean(ckv * ckv, axis=-1, keepdims=True) + RMS_EPS)
        kvn_ref[...] = (ckv * r * gkv_ref[...]).astype(BF16)
        kr_ref[...] = _rope(sm_ref[:, 384:512], c_ref[...], s_ref[...]).astype(BF16)
        fl = sm_ref[:, 512:640] + fb_ref[...]
        lf = jnp.minimum(fl, 0.0) - jnp.log(1.0 + jnp.exp(-jnp.abs(fl)))
        lf = jnp.where(_row_valid(i), lf, 0.0)
        hi, mid, lo = _split3(lf)
        t = tri_ref[...]
        cum = (_dot(t, hi, 1, 0) + _dot(t, mid, 1, 0)) + _dot(t, lo, 1, 0) + carry[...]
        ncum_ref[...] = -cum
        carry[...] = -ncum_ref[BLK - 1:BLK, :]

    return pl.pallas_call(
        body, name="small_prep", grid=(lp // BLK,),
        in_specs=[_row(SMALL_W), _full((1, 256)), _full((1, 128)), _full((1, 128)), _row(128), _row(128),
                  _full((BLK, BLK))],
        out_specs=[_row(256), _row(128), _row(128), _row(128)],
        out_shape=[jax.ShapeDtypeStruct((lp, 256), BF16), jax.ShapeDtypeStruct((lp, 128), BF16),
                   jax.ShapeDtypeStruct((lp, 128), BF16), jax.ShapeDtypeStruct((lp, 128), F32)],
        scratch_shapes=[pltpu.VMEM((1, 128), F32)],
        compiler_params=_cp(("arbitrary",)))(small, gq, gkv, fb, ctab, stab, tri)


def _rope_q(qraw, ctab, stab, *, inverse, out_dtype, name):
    lp = qraw.shape[0]

    def body(q_ref, c_ref, s_ref, o_ref):
        c = c_ref[...]
        s = -s_ref[...] if inverse else s_ref[...]
        for p in range(PAIRS):
            lo = p * 256
            o_ref[:, lo:lo + 128] = q_ref[:, lo:lo + 128].astype(out_dtype)
            o_ref[:, lo + 128:lo + 256] = _rope(q_ref[:, lo + 128:lo + 256].astype(F32), c, s).astype(out_dtype)

    return pl.pallas_call(
        body, name=name, grid=(lp // BLK,),
        in_specs=[_row(PAIRS * 256), _row(128), _row(128)], out_specs=_row(PAIRS * 256),
        out_shape=jax.ShapeDtypeStruct((lp, PAIRS * 256), out_dtype),
        compiler_params=_cp(("parallel",)))(qraw, ctab, stab)


def _gate_fwd(o_mla, o_fox, gate):
    lp = o_mla.shape[0]

    def body(om_ref, of_ref, zm_ref, zf_ref, am_ref, af_ref):
        zm = zm_ref[...].astype(F32)
        am_ref[...] = (om_ref[...] * (zm * _sigmoid(zm))).astype(BF16)
        zf = zf_ref[...].astype(F32)
        af_ref[...] = (of_ref[...] * (zf * _sigmoid(zf))).astype(BF16)

    return pl.pallas_call(
        body, name="gate_fwd", grid=(lp // BLK,),
        in_specs=[_row(D_MODEL), _row(D_MODEL), _rowc(D_MODEL, 0), _rowc(D_MODEL, 1)],
        out_specs=[_row(D_MODEL), _row(D_MODEL)],
        out_shape=[jax.ShapeDtypeStruct((lp, D_MODEL), BF16)] * 2,
        compiler_params=_cp(("parallel",)))(o_mla, o_fox, gate, gate)


def _merge_fwd(gate, y_mla, y_fox):
    lp = y_mla.shape[0]

    def body(ga_ref, gb_ref, ym_ref, yf_ref, m_ref):
        sa = _sigmoid(ga_ref[...].astype(F32))
        sb = _sigmoid(gb_ref[...].astype(F32))
        m_ref[...] = (sa * ym_ref[...] + sb * yf_ref[...]).astype(BF16)

    return pl.pallas_call(
        body, name="merge_fwd", grid=(lp // BLK,),
        in_specs=[_rowc(D_MODEL, 2), _rowc(D_MODEL, 3), _row(D_MODEL), _row(D_MODEL)],
        out_specs=_row(D_MODEL), out_shape=jax.ShapeDtypeStruct((lp, D_MODEL), BF16),
        compiler_params=_cp(("parallel",)))(gate, gate, y_mla, y_fox)


def _tail(x2, mixed, tgt, gpost):
    lp = mixed.shape[0]
    shift = _shift_rows(D_MODEL)

    def body(h_ref, mx_ref, t_ref, g_ref, dmx_ref, dy_ref, loss_ref, dg_ref):
        i = pl.program_id(0)

        @pl.when(i == 0)
        def _():
            loss_ref[...] = jnp.zeros_like(loss_ref)
            dg_ref[...] = jnp.zeros_like(dg_ref)
            dmx_ref[...] = jnp.zeros_like(dmx_ref)
            dy_ref[...] = jnp.zeros_like(dy_ref)

        @pl.when(i > 0)
        def _():
            mx = mx_ref[...]
            g = g_ref[...]
            r = lax.rsqrt(jnp.mean(mx * mx, axis=-1, keepdims=True) + RMS_EPS)
            nrm = mx * r
            e = (h_ref[...] + nrm * g) - t_ref[...]
            loss_ref[...] += jnp.sum(0.5 * jnp.sum(e * e, axis=-1, keepdims=True) * (1.0 / D_MODEL),
                                     axis=0, keepdims=True)
            dy = e * (1.0 / D_MODEL)
            dy_ref[...] = dy
            dg_ref[...] += jnp.sum(dy * nrm, axis=0, keepdims=True)
            w = dy * g
            dot = jnp.mean(w * mx, axis=-1, keepdims=True)
            dmx_ref[...] = (r * w - mx * (r * r * r * dot)).astype(BF16)

    return pl.pallas_call(
        body, name="tail", grid=(lp // BLK,),
        in_specs=[shift, _row(D_MODEL), shift, _full((1, D_MODEL))],
        out_specs=[_row(D_MODEL), _row(D_MODEL), _full((1, 1)), _full((1, D_MODEL))],
        out_shape=[jax.ShapeDtypeStruct((lp, D_MODEL), BF16), jax.ShapeDtypeStruct((lp, D_MODEL), F32),
                   jax.ShapeDtypeStruct((1, 1), F32), jax.ShapeDtypeStruct((1, D_MODEL), F32)],
        compiler_params=_cp(("arbitrary",)))(x2, mixed, tgt, gpost)


def _merge_bwd(dm, gate, y_mla, y_fox):
    lp = dm.shape[0]

    def body(dm_ref, ga_ref, gb_ref, ym_ref, yf_ref, dym_ref, dyf_ref, dg_ref):
        dm_v = dm_ref[...]
        sa = _sigmoid(ga_ref[...].astype(F32))
        sb = _sigmoid(gb_ref[...].astype(F32))
        dym_ref[...] = (dm_v * sa).astype(BF16)
        dyf_ref[...] = (dm_v * sb).astype(BF16)
        dg_ref[:, 0:D_MODEL] = (dm_v * ym_ref[...] * (sa * (1.0 - sa))).astype(BF16)
        dg_ref[:, D_MODEL:2 * D_MODEL] = (dm_v * yf_ref[...] * (sb * (1.0 - sb))).astype(BF16)

    return pl.pallas_call(
        body, name="merge_bwd", grid=(lp // BLK,),
        in_specs=[_row(D_MODEL), _rowc(D_MODEL, 2), _rowc(D_MODEL, 3), _row(D_MODEL), _row(D_MODEL)],
        out_specs=[_row(D_MODEL), _row(D_MODEL), _row(2 * D_MODEL)],
        out_shape=[jax.ShapeDtypeStruct((lp, D_MODEL), BF16), jax.ShapeDtypeStruct((lp, D_MODEL), BF16),
                   jax.ShapeDtypeStruct((lp, 2 * D_MODEL), BF16)],
        compiler_params=_cp(("parallel",)))(dm, gate, gate, y_mla, y_fox)


def _gate_bwd(da_mla, da_fox, o_mla, o_fox, gate):
    lp = da_mla.shape[0]

    def one(da, o, z):
        sg = _sigmoid(z)
        do = da * (z * sg)
        dz = da * o * (sg * (1.0 + z * (1.0 - sg)))
        return do.astype(BF16), dz.astype(BF16)

    def body(dam_ref, daf_ref, om_ref, of_ref, zm_ref, zf_ref, dom_ref, dof_ref, dz_ref):
        dom_ref[...], dz_ref[:, 0:D_MODEL] = one(dam_ref[...], om_ref[...], zm_ref[...].astype(F32))
        dof_ref[...], dz_ref[:, D_MODEL:2 * D_MODEL] = one(daf_ref[...], of_ref[...], zf_ref[...].astype(F32))

    return pl.pallas_call(
        body, name="gate_bwd", grid=(lp // BLK,),
        in_specs=[_row(D_MODEL)] * 4 + [_rowc(D_MODEL, 0), _rowc(D_MODEL, 1)],
        out_specs=[_row(D_MODEL), _row(D_MODEL), _row(2 * D_MODEL)],
        out_shape=[jax.ShapeDtypeStruct((lp, D_MODEL), BF16), jax.ShapeDtypeStruct((lp, D_MODEL), BF16),
                   jax.ShapeDtypeStruct((lp, 2 * D_MODEL), BF16)],
        compiler_params=_cp(("parallel",)))(da_mla, da_fox, o_mla, o_fox, gate, gate)


def _small_bwd(small, dqn, dkvn, dkr, dcol_t, drow_t, gq, gkv, fb, ctab, stab, triu):
    lp = small.shape[0]
    nb = lp // BLK

    def rrow(w):
        return pl.BlockSpec((BLK, w), lambda i: (nb - 1 - i, 0))

    def body(sm_ref, dqn_ref, dkvn_ref, dkr_ref, dcol_ref, drow_ref, gq_ref, gkv_ref, fb_ref, c_ref, s_ref, tri_ref,
             ds_ref, dgq_ref, dgkv_ref, dfb_ref, carry):
        i = pl.program_id(0)

        @pl.when(i == 0)
        def _():
            carry[...] = jnp.zeros_like(carry)
            dgq_ref[...] = jnp.zeros_like(dgq_ref)
            dgkv_ref[...] = jnp.zeros_like(dgkv_ref)
            dfb_ref[...] = jnp.zeros_like(dfb_ref)

        def norm_bwd(x, dn, g, dg_ref):
            r = lax.rsqrt(jnp.mean(x * x, axis=-1, keepdims=True) + RMS_EPS)
            dg_ref[...] += jnp.sum(dn * (x * r), axis=0, keepdims=True)
            w = dn * g
            dot = jnp.mean(w * x, axis=-1, keepdims=True)
            return r * w - x * (r * r * r * dot)

        ds_ref[:, 0:256] = norm_bwd(sm_ref[:, 0:256], dqn_ref[...], gq_ref[...], dgq_ref).astype(BF16)
        ds_ref[:, 256:384] = norm_bwd(sm_ref[:, 256:384], dkvn_ref[...], gkv_ref[...], dgkv_ref).astype(BF16)

        dk = dkr_ref[0]
        for p in range(1, PAIRS):
            dk = dk + dkr_ref[p]
        dk = _rope(dk, c_ref[...], -s_ref[...])
        lane = lax.broadcasted_iota(jnp.int32, dk.shape, 1)
        dk = jnp.where(lane < MLA_ROPE, dk + pltpu.roll(dk, LANES - MLA_ROPE, 1), 0.0)
        ds_ref[:, 384:512] = dk.astype(BF16)

        dcol = dcol_ref[0]
        for p in range(1, PAIRS):
            dcol = dcol + pltpu.roll(dcol_ref[p], 2 * p, 1)
        rows16 = jnp.concatenate([drow_ref[p, h:h + 1, :] for p in range(PAIRS) for h in range(2)], axis=0)
        eye = (lax.broadcasted_iota(jnp.int32, (HEADS, LANES), 0)
               == lax.broadcasted_iota(jnp.int32, (HEADS, LANES), 1)).astype(BF16)
        drow = sum(_dot(part, eye, 0, 0) for part in _split3(rows16))
        dcr = dcol - drow
        hi, mid, lo = _split3(dcr)
        t = tri_ref[...]
        suf = (_dot(t, hi, 1, 0) + _dot(t, mid, 1, 0)) + _dot(t, lo, 1, 0) + carry[...]
        fl = sm_ref[:, 512:640] + fb_ref[...]
        dfl = jnp.where(_row_valid(nb - 1 - i), -suf * _sigmoid(-fl), 0.0)
        ds_ref[:, 512:640] = dfl.astype(BF16)
        dfb_ref[...] += jnp.sum(dfl, axis=0, keepdims=True)
        carry[...] += jnp.sum(dcr, axis=0, keepdims=True)

    return pl.pallas_call(
        body, name="small_bwd", grid=(nb,),
        in_specs=[rrow(SMALL_W), rrow(256), rrow(128),
                  pl.BlockSpec((PAIRS, BLK, 128), lambda i: (0, nb - 1 - i, 0)),
                  pl.BlockSpec((PAIRS, BLK, 128), lambda i: (0, nb - 1 - i, 0)),
                  pl.BlockSpec((PAIRS, 2, BLK), lambda i: (0, 0, nb - 1 - i)),
                  _full((1, 256)), _full((1, 128)), _full((1, 128)), rrow(128), rrow(128), _full((BLK, BLK))],
        out_specs=[rrow(SMALL_W), _full((1, 256)), _full((1, 128)), _full((1, 128))],
        out_shape=[jax.ShapeDtypeStruct((lp, SMALL_W), BF16), jax.ShapeDtypeStruct((1, 256), F32),
                   jax.ShapeDtypeStruct((1, 128), F32), jax.ShapeDtypeStruct((1, 128), F32)],
        scratch_shapes=[pltpu.VMEM((1, 128), F32)],
        compiler_params=_cp(("arbitrary",)))(small, dqn, dkvn, dkr, dcol_t, drow_t, gq, gkv, fb, ctab, stab, triu)


def _pre_bwd(du, x2, meta, dy, gpre):
    s_rows = x2.shape[0]
    lp = PAD + s_rows
    shift = _shift_rows(D_MODEL)

    def body(du_ref, x_ref, meta_ref, dy_ref, g_ref, dx_ref, dmeta_ref, dg_ref):
        i = pl.program_id(0)

        @pl.when(i == 0)
        def _():
            dg_ref[...] = jnp.zeros_like(dg_ref)

        hv = _h_block(i, x_ref, meta_ref)
        duv = du_ref[...]
        r = lax.rsqrt(jnp.mean(hv * hv, axis=-1, keepdims=True) + RMS_EPS)
        dg_ref[...] += jnp.sum(duv * (hv * r), axis=0, keepdims=True)
        w = duv * g_ref[...]
        dot = jnp.mean(w * hv, axis=-1, keepdims=True)
        dh = dy_ref[...] + (r * w - hv * (r * r * r * dot))
        dx_ref[...] = dh

        @pl.when(i == 0)
        def _():
            dmeta_ref[...] = dh[0:N_META, :]

    return pl.pallas_call(
        body, name="pre_bwd", grid=(lp // BLK,),
        in_specs=[_row(D_MODEL), shift, _full((N_META, D_MODEL)), _row(D_MODEL), _full((1, D_MODEL))],
        out_specs=[shift, _full((N_META, D_MODEL)), _full((1, D_MODEL))],
        out_shape=[jax.ShapeDtypeStruct((s_rows, D_MODEL), F32), jax.ShapeDtypeStruct((N_META, D_MODEL), F32),
                   jax.ShapeDtypeStruct((1, D_MODEL), F32)],
        compiler_params=_cp(("arbitrary",)))(du, x2, meta, dy, gpre)


def _pair_masks(rope):
    lane = lax.broadcasted_iota(jnp.int32, (1, LANES), 1)
    mas = [lane < HEAD_DIM, lane >= HEAD_DIM]
    if not rope:
        return mas, mas
    wide = lax.broadcasted_iota(jnp.int32, (1, 2 * LANES), 1)
    rope_lo = LANES + MLA_ROPE
    return mas, [(wide < HEAD_DIM) | ((wide >= LANES) & (wide < rope_lo)),
                 ((wide >= HEAD_DIM) & (wide < LANES)) | ((wide >= rope_lo) & (wide < rope_lo + MLA_ROPE))]


def _mask2(x, masks):
    return [jnp.where(m, x, jnp.zeros_like(x)) for m in masks]


def _attn_fwd(q, k, v, *, kr=None, nbrep=None, scale, qcol, kcol, vcol, name):
    lp = q.shape[0]
    nq = 1 + (lp - PAD) // QB
    rope = kr is not None
    bias = nbrep is not None
    qw = 256 if rope else 128

    def body(*refs):
        it = iter(refs)
        q_ref, k_ref, v_ref = next(it), next(it), next(it)
        kr_ref = next(it) if rope else None
        nb_ref = next(it) if bias else None
        o_ref, lse_ref = next(it), next(it)
        i = pl.program_id(1)
        r0 = pl.multiple_of(jnp.where(i == 0, 0, PAD + QB * (i - 1)), BLK)
        b0 = r0 // BLK
        mas, hmask = _pair_masks(rope)
        qh = _mask2(q_ref[pl.ds(r0, QB), :], hmask)
        if bias:
            qh = [x * scale for x in qh]

        def causal(kc, n):
            key = kc * BLK + lax.broadcasted_iota(jnp.int32, (n, QB), 0)
            return (key <= r0 + lax.broadcasted_iota(jnp.int32, (n, QB), 1)) & ((kc > 0) | (n == N_META))

        def update(kcs, carry, masks, n=BLK):
            stats, acc = carry[:4], carry[4]
            k0s = [pl.multiple_of(kc * BLK, BLK) for kc in kcs]
            kks = [k_ref[pl.ds(k0, n), :] for k0 in k0s]
            if rope:
                kks = [jnp.concatenate([kk, kr_ref[pl.ds(k0, n), :]], axis=1) for kk, k0 in zip(kks, k0s)]
            new_stats, alphas, ps = [], [], [[] for _ in kcs]
            for h in range(2):
                m_prev, l_prev = stats[2 * h], stats[2 * h + 1]
                ss = []
                for kk, k0, mask in zip(kks, k0s, masks):
                    s = _dot(kk, qh[h], 1, 1)
                    if rope:
                        s = s * scale
                    if bias:
                        nbc = nb_ref[h, pl.ds(k0, n), :]
                        s = s + jnp.concatenate([nbc] * (QB // LANES), axis=1)
                    if mask is not None:
                        s = jnp.where(mask, s, NEG)
                    ss.append(s)
                m_new = m_prev
                for s in ss:
                    m_new = jnp.maximum(m_new, jnp.max(s, axis=0, keepdims=True))
                alpha = jnp.exp(m_prev - m_new)
                l_new = alpha * l_prev
                for j, s in enumerate(ss):
                    p = jnp.exp(s - m_new)
                    l_new = l_new + jnp.sum(p, axis=0, keepdims=True)
                    ps[j].append(p.astype(BF16))
                new_stats += [m_new, l_new]
                alphas.append(alpha)
            vcat = jnp.concatenate([x for k0 in k0s for x in _mask2(v_ref[pl.ds(k0, n), :], mas)], axis=0)
            pv = _dot(vcat, jnp.concatenate([p for pj in ps for p in pj], axis=0), 0, 0)
            a_full = jnp.concatenate([jnp.broadcast_to(a, (HEAD_DIM, QB)) for a in alphas], axis=0)
            return (*new_stats, a_full * acc + pv)

        neg = jnp.full((1, QB), NEG, F32)
        zero = jnp.zeros((1, QB), F32)
        c = update([0], (neg, zero, neg, zero, jnp.zeros((LANES, QB), F32)), [causal(0, N_META)], N_META)
        c = lax.fori_loop(0, (b0 - 1) // 2, lambda t, cr: update([1 + 2 * t, 2 + 2 * t], cr, [None, None]), c)
        c = update([b0, b0 + 1], c, [causal(b0, BLK), causal(b0 + 1, BLK)])
        inv =jnp.concatenate([jnp.broadcast_to(1.0 / c[1], (HEAD_DIM, QB)),
                               jnp.broadcast_to(1.0 / c[3], (HEAD_DIM, QB))], axis=0)
        o_t = (c[4] * inv).T
        lses = [c[0] + jnp.log(c[1]), c[2] + jnp.log(c[3])]
        o_ref[pl.ds(r0, BLK), :] = o_t[0:BLK]
        for h in range(2):
            lse_ref[0, h:h + 1, pl.ds(r0, BLK)] = lses[h][:, 0:BLK]

        @pl.when(i > 0)
        def _():
            r1 = pl.multiple_of(r0 + BLK, BLK)
            o_ref[pl.ds(r1, QB - BLK), :] = o_t[BLK:QB]
            for h in range(2):
                lse_ref[0, h:h + 1, pl.ds(r1, QB - BLK)] = lses[h][:, BLK:QB]

    in_specs = [pl.BlockSpec((lp, qw), lambda p, i: (0, qcol + p)),
                pl.BlockSpec((lp, 128), lambda p, i: (0, kcol(p))),
                pl.BlockSpec((lp, 128), lambda p, i: (0, vcol(p)))]
    ins = [q, k, v]
    if rope:
        in_specs.append(pl.BlockSpec((lp, 128), lambda p, i: (0, 0)))
        ins.append(kr)
    if bias:
        in_specs.append(pl.BlockSpec((2, lp, 128), lambda p, i: (p, 0, 0)))
        ins.append(nbrep)
    return pl.pallas_call(
        body, name=name, grid=(PAIRS, nq), in_specs=in_specs,
        out_specs=[pl.BlockSpec((lp, 128), lambda p, i: (0, p)),
                   pl.BlockSpec((1, 2, lp), lambda p, i: (p, 0, 0))],
        out_shape=[jax.ShapeDtypeStruct((lp, D_MODEL), F32), jax.ShapeDtypeStruct((PAIRS, 2, lp), F32)],
        compiler_params=_cp(("parallel", "arbitrary"), VMEM_BIG))(*ins)


def _attn_bwd(q, k, v, do, o, lse, *, kr=None, nbrep=None, scale, qcol, kcol, vcol, name):
    lp = q.shape[0]
    nb = lp // BLK
    rope = kr is not None
    bias = nbrep is not None
    qw = 256 if rope else 128

    def body(*refs):
        it = iter(refs)
        q_ref, k_ref, v_ref = next(it), next(it), next(it)
        kr_ref = next(it) if rope else None
        nb_ref = next(it) if bias else None
        do_ref, o_ref, lse_ref = next(it), next(it), next(it)
        dq_ref, dk_ref, dv_ref = next(it), next(it), next(it)
        x_ref = next(it)
        drow_ref = next(it) if bias else None
        delta = next(it)
        kb = pl.program_id(1)
        mas, hmask = _pair_masks(rope)
        lane = lax.broadcasted_iota(jnp.int32, (1, LANES), 1)

        @pl.when(kb == 0)
        def _():
            dq_ref[...] = jnp.zeros_like(dq_ref)
            if bias:
                drow_ref[...] = jnp.zeros_like(drow_ref)
            sub = lax.broadcasted_iota(jnp.int32, (8, LANES), 0)
            sel = (((sub == 0) & mas[0]) | ((sub == 1) & mas[1])).astype(BF16)

            def dstep(c, carry):
                r0 = pl.multiple_of(c * BLK, BLK)
                prod = do_ref[pl.ds(r0, BLK), :].astype(F32) * o_ref[pl.ds(r0, BLK), :]
                hi, mid, lo = _split3(prod)
                delta[:, pl.ds(r0, BLK)] = (_dot(sel, hi, 1, 1) + _dot(sel, mid, 1, 1)) + _dot(sel, lo, 1, 1)
                return carry

            lax.fori_loop(0, nb, dstep, 0)

        def masked_q(q0):
            qh = _mask2(q_ref[pl.ds(q0, BLK), :], hmask)
            return [x * scale for x in qh] if bias else qh

        def key_pass(n):
            kk = k_ref[0:n, :]
            if rope:
                kk = jnp.concatenate([kk, kr_ref[0:n, :]], axis=1)
            vh = _mask2(v_ref[0:n, :], mas)
            kcat = jnp.concatenate(_mask2(kk, hmask), axis=0)
            if bias:
                kcat = kcat * scale
                nbc = [jnp.concatenate([nb_ref[h, 0:n, :], nb_ref[h, 0:n, :]], axis=1) for h in range(2)]
            diag_mask = (lax.broadcasted_iota(jnp.int32, (n, BLK), 0) <= lax.broadcasted_iota(jnp.int32, (n, BLK), 1))

            def chunk(qc, carry, mask):
                carry = list(carry)
                q0 = pl.multiple_of(qc * BLK, BLK)
                dov = do_ref[pl.ds(q0, BLK), :]
                doh = _mask2(dov, mas)
                qh = masked_q(q0)
                pbs, dss = [], []
                for h in range(2):
                    s = _dot(kk, qh[h], 1, 1)
                    if rope:
                        s = s * scale
                    if bias:
                        s = s + nbc[h]
                    p = jnp.exp(s - lse_ref[0, h:h + 1, pl.ds(q0, BLK)])
                    if mask is not None:
                        p = jnp.where(mask, p, 0.0)
                    ds = p * (_dot(vh[h], dov, 1, 1) - delta[h:h + 1, pl.ds(q0, BLK)])
                    if bias:
                        drow_ref[0, h:h + 1, pl.ds(q0, BLK)] += jnp.sum(ds, axis=0, keepdims=True)
                        carry[2 + h] = carry[2 + h] + jnp.sum(ds, axis=1, keepdims=True)
                    else:
                        ds = ds * scale
                    pbs.append(p.astype(BF16))
                    dss.append(ds.astype(BF16))
                ds_lanes = jnp.concatenate(dss, axis=1)
                ds_rows = jnp.concatenate(dss, axis=0)
                carry[0] = carry[0] + _dot(ds_lanes, jnp.concatenate(qh, axis=0), 1, 0)
                carry[1] = carry[1] + _dot(jnp.concatenate(pbs, axis=1), jnp.concatenate(doh, axis=0), 1, 0)
                dq_ref[pl.ds(q0, BLK), :] += _dot(ds_rows, kcat, 0, 0)
                return tuple(carry)

            init = [jnp.zeros((n, qw), F32), jnp.zeros((n, LANES), F32)]
            if bias:
                init += [jnp.zeros((n, 1), F32), jnp.zeros((n, 1), F32)]
            c = chunk(kb, tuple(init), diag_mask)
            rest = nb - 1 - kb

            def several(t, cr):
                for u in range(UNROLL):
                    cr = chunk(kb + 1 + UNROLL * t + u, cr, None)
                return cr

            c = lax.fori_loop(0, rest // UNROLL, several, c)
            c = lax.fori_loop(nb - rest % UNROLL, nb, lambda qc, cr: chunk(qc, cr, None), c)

            def rows(a, dtype):
                a = a.astype(dtype)
                return a if n == BLK else jnp.concatenate([a, jnp.zeros((BLK - n, a.shape[1]), dtype)], axis=0)

            dk_ref[...] = rows(c[0][:, 0:LANES], BF16)
            dv_ref[...] = rows(c[1], BF16)
            if rope:
                x_ref[0] = rows(c[0][:, LANES:2 * LANES], F32)
            if bias:
                x_ref[0] = rows(jnp.where(lane == 0, c[2], jnp.where(lane == 1, c[3], 0.0)), F32)

        @pl.when(kb == 0)
        def _():
            key_pass(N_META)

        @pl.when(kb > 0)
        def _():
            key_pass(BLK)

    in_specs = [pl.BlockSpec((lp, qw), lambda p, j: (0, qcol + p)),
                pl.BlockSpec((BLK, 128), lambda p, j: (j, kcol(p))),
                pl.BlockSpec((BLK, 128), lambda p, j: (j, vcol(p)))]
    ins = [q, k, v]
    if rope:
        in_specs.append(pl.BlockSpec((BLK, 128), lambda p, j: (j, 0)))
        ins.append(kr)
    if bias:
        in_specs.append(pl.BlockSpec((2, BLK, 128), lambda p, j: (p, j, 0)))
        ins.append(nbrep)
    in_specs += [pl.BlockSpec((lp, 128), lambda p, j: (0, p)), pl.BlockSpec((lp, 128), lambda p, j: (0, p)),
                 pl.BlockSpec((1, 2, lp), lambda p, j: (p, 0, 0))]
    ins += [do, o, lse]
    out_specs = [pl.BlockSpec((lp, qw), lambda p, j: (0, p)),
                 pl.BlockSpec((BLK, 128), lambda p, j: (j, p)),
                 pl.BlockSpec((BLK, 128), lambda p, j: (j, p)),
                 pl.BlockSpec((1, BLK, 128), lambda p, j: (p, j, 0))]
    out_shape = [jax.ShapeDtypeStruct((lp, PAIRS * qw), F32), jax.ShapeDtypeStruct((lp, D_MODEL), BF16),
                 jax.ShapeDtypeStruct((lp, D_MODEL), BF16), jax.ShapeDtypeStruct((PAIRS, lp, 128), F32)]
    if bias:
        out_specs.append(pl.BlockSpec((1, 2, lp), lambda p, j: (p, 0, 0)))
        out_shape.append(jax.ShapeDtypeStruct((PAIRS, 2, lp), F32))
    return pl.pallas_call(
        body, name=name, grid=(PAIRS, nb), in_specs=in_specs, out_specs=out_specs, out_shape=out_shape,
        scratch_shapes=[pltpu.VMEM((8, lp), F32)],
        compiler_params=_cp(("parallel", "arbitrary"), VMEM_BIG))(*ins)


def _adamw(w, g, m, v, name):
    lead = w.ndim - 2
    rows, cols = w.shape[lead:]
    tr = 128 if rows * cols > 512 * 1024 else rows

    def body(w_ref, g_ref, m_ref, v_ref, d_ref, nm_ref, nv_ref):
        gv = g_ref[...]
        nm = ADAM_B1 * m_ref[...] + (1.0 - ADAM_B1) * gv
        nv = ADAM_B2 * v_ref[...] + (1.0 - ADAM_B2) * (gv * gv)
        m_hat = nm / (1.0 - ADAM_B1 ** ADAM_STEP)
        v_hat = nv / (1.0 - ADAM_B2 ** ADAM_STEP)
        d_ref[...] = -ADAM_LR * (m_hat / (jnp.sqrt(v_hat) + ADAM_EPS) + ADAM_WD * w_ref[...])
        nm_ref[...] = nm
        nv_ref[...] = nv

    spec = pl.BlockSpec((1,) * lead + (tr, cols), lambda i: (0,) * lead + (i, 0))
    return pl.pallas_call(
        body, name=name, grid=(rows // tr,), in_specs=[spec] * 4, out_specs=[spec] * 3,
        out_shape=[jax.ShapeDtypeStruct(w.shape, F32)] * 3,
        compiler_params=_cp(("parallel",), VMEM_BIG))(w, g, m, v)


def _add_cores(g, from_sib, name):
    n, rows, cols = g.shape
    half = rows // 2
    tr = _tile(half, (256, 240))
    nt = half // tr

    def body(lo_ref, hi_ref, s_ref, o_ref):
        mine = jnp.where(lax.axis_index("c") == 0, lo_ref[0], hi_ref[0])
        o_ref[0] = (mine + s_ref[0]).astype(BF16)

    return pl.pallas_call(
        body, name=name, grid=(n, nt),
        in_specs=[pl.BlockSpec((1, tr, cols), lambda j, i: (j, i, 0)),
                  pl.BlockSpec((1, tr, cols), lambda j, i: (j, nt + i, 0)),
                  pl.BlockSpec((1, tr, cols), lambda j, i: (j, i, 0))],
        out_specs=pl.BlockSpec((1, tr, cols), lambda j, i: (j, i, 0)),
        out_shape=jax.ShapeDtypeStruct((n, half, cols), BF16),
        compiler_params=_cp(("parallel", "parallel"), VMEM_BIG))(g, g, from_sib)


def _add_chips(x, own, name):
    n, rows, cols = x.shape
    tr = _tile(rows, (256, 240))

    def body(x_ref, own_ref, o_ref):
        me = 2 * lax.axis_index("x") + lax.axis_index("y")
        v = [jnp.where(me == k, own_ref[...], x_ref[k]).astype(F32) for k in range(N_CHIPS)]
        o_ref[...] = ((v[0] + v[1]) + v[2]) + v[3]

    return pl.pallas_call(
        body, name=name, grid=(rows // tr,),
        in_specs=[pl.BlockSpec((n, tr, cols), lambda i: (0, i, 0)), pl.BlockSpec((tr, cols), lambda i: (i, 0))],
        out_specs=pl.BlockSpec((tr, cols), lambda i: (i, 0)),
        out_shape=jax.ShapeDtypeStruct((rows, cols), F32), compiler_params=_cp(("parallel",), VMEM_BIG))(x, own)


def _axes():
    return lax.axis_index("x"), lax.axis_index("y"), lax.axis_index("c")


def _other_chips(x, y):
    return [(1 - x, y), (x, 1 - y), (1 - x, 1 - y)]


ANY = pl.BlockSpec(memory_space=pl.ANY)


def _rcopy(src, dst, send_sems, recv_sems, k, to):
    return pltpu.make_async_remote_copy(src_ref=src, dst_ref=dst, send_sem=send_sems.at[k], recv_sem=recv_sems.at[k],
                                        device_id=to, device_id_type=MESH)


def _gather_weights(shards, meta):
    n = len(shards)

    def body(*refs):
        srcs, meta_ref = refs[:n], refs[n]
        outs, mout_ref = refs[n + 1:2 * n + 1], refs[2 * n + 1]
        send_sems, recv_sems = refs[2 * n + 2:]
        x, y, c = _axes()
        me = 2 * x + y
        sib = (x, y, 1 - c)
        chips = _other_chips(x, y)

        def half(t, chip_idx, cc):
            hr = shards[t].shape[0] // 2
            return outs[t].at[chip_idx, pl.ds(cc * hr, hr), :]

        first = []
        for j, (px, py) in enumerate(chips):
            for t in range(n):
                hr = shards[t].shape[0] // 2
                first.append(_rcopy(srcs[t].at[pl.ds(c * hr, hr), :], half(t, me, c), send_sems, recv_sems,
                                    3 * t + j, (px, py, c)))
            first.append(_rcopy(meta_ref, mout_ref.at[me], send_sems, recv_sems, 3 * n + j, (px, py, c)))
        for cp in first:
            cp.start()
        passed = []
        for j, (px, py) in enumerate(chips):
            src_chip = 2 * px + py
            for t in range(n):
                _rcopy(half(t, src_chip, c), half(t, src_chip, c), send_sems, recv_sems, 3 * t + j, sib).wait_recv()
                fwd = _rcopy(half(t, src_chip, c), half(t, src_chip, c), send_sems, recv_sems, 3 * (n + 1 + t) + j, sib)
                fwd.start()
                passed.append(fwd)
            _rcopy(mout_ref.at[src_chip], mout_ref.at[src_chip], send_sems, recv_sems, 3 * n + j, sib).wait_recv()
        for j, (px, py) in enumerate(chips):
            src_chip = 2 * px + py
            for t in range(n):
                _rcopy(half(t, src_chip, 1 - c), half(t, src_chip, 1 - c), send_sems, recv_sems,
                       3 * (n + 1 + t) + j, sib).wait_recv()
        for cp in first + passed:
            cp.wait_send()

    nsem = 3 * (2 * n + 1)
    return pl.pallas_call(
        body, name="gather_weights", in_specs=[ANY] * (n + 1), out_specs=[ANY] * (n + 1),
        out_shape=[jax.ShapeDtypeStruct((N_CHIPS,) + s.shape, s.dtype) for s in shards]
        + [jax.ShapeDtypeStruct((N_CHIPS,) + meta.shape, meta.dtype)],
        scratch_shapes=[pltpu.SemaphoreType.DMA((nsem,)), pltpu.SemaphoreType.DMA((nsem,))])(*shards, meta)


def _swap_halves(gs):
    n = len(gs)

    def body(*refs):
        srcs, outs = refs[:n], refs[n:2 * n]
        send_sems, recv_sems = refs[2 * n:]
        x, y, c = _axes()
        cps = []
        for t in range(n):
            hr = gs[t].shape[1] // 2
            for j in range(N_CHIPS):
                cps.append(_rcopy(srcs[t].at[j, pl.ds((1 - c) * hr, hr), :], outs[t].at[j], send_sems, recv_sems,
                                  N_CHIPS * t + j, (x, y, 1 - c)))
        for cp in cps:
            cp.start()
        for cp in cps:
            cp.wait()

    return pl.pallas_call(
        body, name="swap_halves", in_specs=[ANY] * n, out_specs=[ANY] * n,
        out_shape=[jax.ShapeDtypeStruct((N_CHIPS, g.shape[1] // 2, g.shape[2]), g.dtype) for g in gs],
        scratch_shapes=[pltpu.SemaphoreType.DMA((N_CHIPS * n,)), pltpu.SemaphoreType.DMA((N_CHIPS * n,))])(*gs)


def _scatter_chips(parts):
    n = len(parts)

    def body(*refs):
        srcs, outs = refs[:n], refs[n:2 * n]
        send_sems, recv_sems = refs[2 * n:]
        x, y, c = _axes()
        me = 2 * x + y
        cps = []
        for j, (px, py) in enumerate(_other_chips(x, y)):
            for t in range(n):
                cps.append(_rcopy(srcs[t].at[2 * px + py], outs[t].at[me], send_sems, recv_sems, 3 * t + j,
                                  (px, py, c)))
        for cp in cps:
            cp.start()
        for cp in cps:
            cp.wait()

    return pl.pallas_call(
        body, name="scatter_chips", in_specs=[ANY] * n, out_specs=[ANY] * n,
        out_shape=[jax.ShapeDtypeStruct(p.shape, p.dtype) for p in parts],
        scratch_shapes=[pltpu.SemaphoreType.DMA((3 * n,)), pltpu.SemaphoreType.DMA((3 * n,))])(*parts)


def _swap_reduced(rs):
    n = len(rs)

    def body(*refs):
        srcs, outs = refs[:n], refs[n:2 * n]
        send_sems, recv_sems = refs[2 * n:]
        x, y, c = _axes()
        cps = [_rcopy(srcs[t], outs[t], send_sems, recv_sems, t, (x, y, 1 - c)) for t in range(n)]
        for cp in cps:
            cp.start()
        for cp in cps:
            cp.wait()

    return pl.pallas_call(
        body, name="swap_reduced", in_specs=[ANY] * n, out_specs=[ANY] * n,
        out_shape=[jax.ShapeDtypeStruct(r.shape, r.dtype) for r in rs],
        scratch_shapes=[pltpu.SemaphoreType.DMA((n,)), pltpu.SemaphoreType.DMA((n,))])(*rs)


SMALL_ROWS = 24


def _allreduce_small(vec):
    def body(v_ref, out_ref, slots, send_sems, recv_sems):
        x, y, c = _axes()
        me = 4 * x + 2 * y + c
        slots[me] = v_ref[...]
        cps = []
        for k in range(1, 8):
            kx, ky, kc = (k >> 2) & 1, (k >> 1) & 1, k & 1
            peer = (1 - x if kx else x, 1 - y if ky else y, 1 - c if kc else c)
            cps.append(_rcopy(v_ref, slots.at[me], send_sems, recv_sems, k - 1, peer))
        for cp in cps:
            cp.start()
        for cp in cps:
            cp.wait()
        tot = slots[0]
        for k in range(1, 8):
            tot = tot + slots[k]
        out_ref[...] = tot

    return pl.pallas_call(
        body, name="allreduce_small",
        in_specs=[pl.BlockSpec(memory_space=pltpu.VMEM)], out_specs=pl.BlockSpec(memory_space=pltpu.VMEM),
        out_shape=jax.ShapeDtypeStruct((SMALL_ROWS, 128), F32),
        scratch_shapes=[pltpu.VMEM((8, SMALL_ROWS, 128), F32), pltpu.SemaphoreType.DMA((7,)),
                        pltpu.SemaphoreType.DMA((7,))])(vec)


def _pack_p2(w_uq, w_ukv, w_br_mla, w_br_fox, w_out, meta, dtype):
    parts = [w_uq.reshape(96, D_MODEL), w_ukv.reshape(64, D_MODEL), w_br_mla, w_br_fox, w_out,
             meta.reshape(4, D_MODEL), jnp.zeros((P2_ROWS - 932, D_MODEL), meta.dtype)]
    return jnp.concatenate([p.astype(dtype) for p in parts], axis=0)


def _unpack_p2(pk):
    return (pk[0:96].reshape(256, 384), pk[96:160].reshape(128, 512), pk[160:416], pk[416:672], pk[672:928],
            pk[928:932].reshape(N_META, 256))


def _uq_arrange(w):
    w3 = w.reshape(256, HEADS, 96)
    nope = w3[:, :, :64].reshape(256, PAIRS, 128)
    pe = w3[:, :, 64:].reshape(256, PAIRS, 64)
    return jnp.concatenate([nope, pe, jnp.zeros((256, PAIRS, 64), w.dtype)], axis=2).reshape(256, PAIRS * 256)


def _uq_restore(g):
    g3 = g.reshape(256, PAIRS, 256)
    nope = g3[:, :, :128].reshape(256, HEADS, 64)
    pe = g3[:, :, 128:192].reshape(256, HEADS, 32)
    return jnp.concatenate([nope, pe], axis=2).reshape(256, HEADS * 96)


def _ukv_arrange(w):
    w3 = w.reshape(128, HEADS, 128)
    return jnp.concatenate([w3[:, :, :64].reshape(128, 1024), w3[:, :, 64:].reshape(128, 1024)], axis=1)


def _ukv_restore(g):
    kn = g[:, :1024].reshape(128, HEADS, 64)
    vv = g[:, 1024:].reshape(128, HEADS, 64)
    return jnp.concatenate([kn, vv], axis=2).reshape(128, HEADS * 128)


def _rope_tables(lp):
    r = np.arange(lp)
    pos = np.where(r < N_META, r, np.where(r >= PAD, r - PAD + N_META, 0)).astype(np.float32)
    half = MLA_ROPE // 2
    inv_freq = np.float32(ROPE_THETA) ** (-np.arange(half, dtype=np.float32) / np.float32(half))
    ang = (pos[:, None] * inv_freq[None, :]).astype(np.float32)
    cos, sin = np.cos(ang).astype(np.float32), np.sin(ang).astype(np.float32)
    one, zero = np.ones((lp, 64), np.float32), np.zeros((lp, 64), np.float32)
    return (jnp.asarray(np.concatenate([cos, cos, cos, cos, one], axis=1)),
            jnp.asarray(np.concatenate([-sin, sin, -sin, sin, zero], axis=1)))


def _pad_lanes(v, n=128):
    return jnp.pad(v, ((0, 0), (0, n - v.shape[1])))


def _in_cols(slabs, a, b):
    out = []
    for j in range(N_CHIPS):
        lo, hi = max(a, W_IN_SHARD * j), min(b, W_IN_SHARD * (j + 1))
        if lo < hi:
            out.append(slabs[j][:, lo - W_IN_SHARD * j:hi - W_IN_SHARD * j])
    return out


def _local_step(x2, tgt2, meta_f, w_small, w_attn, w_gate, w_uq_f, w_ukv_f, w_bm, w_bf, w_o, pre_norm_g,
                post_norm_g, mla_q_norm_g, mla_kv_norm_g, fox_forget_b):
    s_rows = x2.shape[0]
    lp = PAD + s_rows
    w_uq_a = _uq_arrange(w_uq_f)
    w_ukv_a = _ukv_arrange(w_ukv_f)

    ctab, stab = _rope_tables(lp)
    ii = jnp.arange(BLK)
    tri_lo = (ii[:, None] >= ii[None, :]).astype(BF16)
    tri_up = (ii[:, None] <= ii[None, :]).astype(BF16)
    fb128 = _pad_lanes(fox_forget_b)

    u = _rms_pre(x2, meta_f, pre_norm_g)
    small = _mm(u, w_small, mode="nn", out_dtype=F32, name="proj_small")
    attn = _mm(u, w_attn, mode="nn", out_dtype=BF16, name="proj_attn")
    gate = _mm(u, w_gate, mode="nn", out_dtype=BF16, name="proj_gate")
    qn, kvn, kr, ncum = _small_prep(small, mla_q_norm_g, mla_kv_norm_g, fb128, ctab, stab, tri_lo)
    qraw = _mm(qn, w_uq_a, mode="nn", out_dtype=F32, name="mla_q")
    qcat = _rope_q(qraw, ctab, stab, inverse=False, out_dtype=BF16, name="rope_q")
    kv = _mm(kvn, w_ukv_a, mode="nn", out_dtype=BF16, name="mla_kv")
    nbrep = jnp.broadcast_to(ncum[:, :HEADS].T[:, :, None], (HEADS, lp, LANES))

    mla_cols = dict(qcol=0, kcol=lambda p: p, vcol=lambda p: PAIRS + p)
    fox_cols = dict(qcol=0, kcol=lambda p: PAIRS + p, vcol=lambda p: 2 * PAIRS + p)
    o_mla, lse_mla = _attn_fwd(qcat, kv, kv, kr=kr, scale=MLA_SCALE, name="mla_fwd", **mla_cols)
    o_fox, lse_fox = _attn_fwd(attn, attn, attn, nbrep=nbrep, scale=FOX_SCALE, name="fox_fwd", **fox_cols)

    a_mla, a_fox = _gate_fwd(o_mla, o_fox, gate)
    y_mla = _mm(a_mla, w_bm, mode="nn", out_dtype=F32, name="br_mla")
    y_fox = _mm(a_fox, w_bf, mode="nn", out_dtype=F32, name="br_fox")
    mg = _merge_fwd(gate, y_mla, y_fox)
    mixed = _mm(mg, w_o, mode="nn", out_dtype=F32, name="out_proj")
    dmixed, dy, loss_p, dg_post = _tail(x2, mixed, tgt2, post_norm_g)

    d_w_out = _mm(mg, dmixed, mode="tn", out_dtype=F32, name="d_w_out")
    dm = _mm(dmixed, w_o, mode="nt", out_dtype=F32, name="d_merge")
    dy_mla, dy_fox, dgate_ab = _merge_bwd(dm, gate, y_mla, y_fox)
    d_w_bm = _mm(a_mla, dy_mla, mode="tn", out_dtype=F32, name="d_w_br_mla")
    d_w_bf = _mm(a_fox, dy_fox, mode="tn", out_dtype=F32, name="d_w_br_fox")
    da_mla = _mm(dy_mla, w_bm, mode="nt", out_dtype=F32, name="d_a_mla")
    da_fox = _mm(dy_fox, w_bf, mode="nt", out_dtype=F32, name="d_a_fox")
    do_mla, do_fox, dgate_z = _gate_bwd(da_mla, da_fox, o_mla, o_fox, gate)

    dqcat, dkn, dvm, dkr = _attn_bwd(qcat, kv, kv, do_mla, o_mla, lse_mla, kr=kr, scale=MLA_SCALE,
                                     name="mla_bwd", **mla_cols)
    dfq, dfk, dfv, dcol, drow = _attn_bwd(attn, attn, attn, do_fox, o_fox, lse_fox, nbrep=nbrep, scale=FOX_SCALE,
                                          name="fox_bwd", **fox_cols)

    dq_a = _rope_q(dqcat, ctab, stab, inverse=True, out_dtype=BF16, name="rope_q_bwd")
    d_w_uq_a = _mm(qn, dq_a, mode="tn", out_dtype=F32, name="d_w_uq")
    dqn = _mm(dq_a, w_uq_a, mode="nt", out_dtype=F32, name="d_qn")
    d_w_ukv_a = jnp.concatenate([_mm(kvn, dkn, mode="tn", out_dtype=F32, name="d_w_uk"),
                                 _mm(kvn, dvm, mode="tn", out_dtype=F32, name="d_w_uv")], axis=1)
    dkvn = _mm(dkn, w_ukv_a[:, :1024], mode="nt", out_dtype=F32, name="d_kvn_k")
    dkvn = _mm(dvm, w_ukv_a[:, 1024:], mode="nt", out_dtype=F32, name="d_kvn_v", acc=dkvn)
    dsmall, dg_q, dg_kv, dfb = _small_bwd(small, dqn, dkvn, dkr, dcol, drow, mla_q_norm_g, mla_kv_norm_g,
                                          fb128, ctab, stab, tri_up)

    dw_small = _mm(u, dsmall, mode="tn", out_dtype=F32, name="d_w_small")
    dw_fq = _mm(u, dfq, mode="tn", out_dtype=F32, name="d_w_fq")
    dw_fk = _mm(u, dfk, mode="tn", out_dtype=F32, name="d_w_fk")
    dw_fv = _mm(u, dfv, mode="tn", out_dtype=F32, name="d_w_fv")
    dw_z = _mm(u, dgate_z, mode="tn", out_dtype=F32, name="d_w_z")
    dw_g = _mm(u, dgate_ab, mode="tn", out_dtype=F32, name="d_w_g")
    du = _mm(dsmall, w_small, mode="nt", out_dtype=F32, name="d_u_small")
    du = _mm(dfq, w_attn[:, 0:1024], mode="nt", out_dtype=F32, name="d_u_fq", acc=du)
    du = _mm(dfk, w_attn[:, 1024:2048], mode="nt", out_dtype=F32, name="d_u_fk", acc=du)
    du = _mm(dfv, w_attn[:, 2048:3072], mode="nt", out_dtype=F32, name="d_u_fv", acc=du)
    du = _mm(dgate_z, w_gate[:, 0:2048], mode="nt", out_dtype=F32, name="d_u_z", acc=du)
    du = _mm(dgate_ab, w_gate[:, 2048:4096], mode="nt", out_dtype=F32, name="d_u_g", acc=du)
    dx, dmeta, dg_pre = _pre_bwd(du, x2, meta_f, dy, pre_norm_g)

    runs = [(dw_small[:, 0:416], C_CQ), (dw_z[:, 0:1024], C_ZMLA), (dw_fq, C_FQ), (dw_fk, C_FK), (dw_fv, C_FV),
            (dw_small[:, 512:528], C_FL), (dw_z[:, 1024:2048], C_ZFOX), (dw_g, C_GA)]
    slabs = []
    for j in range(N_CHIPS):
        lo, hi = W_IN_SHARD * j, W_IN_SHARD * (j + 1)
        cols = [a[:, max(lo, c0) - c0:min(hi, c0 + a.shape[1]) - c0] for a, c0 in runs
                if max(lo, c0) < min(hi, c0 + a.shape[1])]
        slabs.append(jnp.concatenate(cols, axis=1))
    d_w_in = jnp.stack(slabs, axis=0)
    d_w_uq = _uq_restore(d_w_uq_a)
    d_w_ukv = _ukv_restore(d_w_ukv_a)
    return (loss_p, dx, dmeta, d_w_in, d_w_uq, d_w_ukv, d_w_bm, d_w_bf, d_w_out, dg_pre, dg_post, dg_q, dg_kv, dfb)


def kernel(x, meta_tokens, pre_norm_g, w_in, fox_forget_b, mla_q_norm_g, mla_kv_norm_g, w_uq, w_ukv, w_br_mla, w_br_fox, w_out, post_norm_g, loss_target, m_meta_tokens, m_pre_norm_g, m_w_in, m_fox_forget_b, m_mla_q_norm_g, m_mla_kv_norm_g, m_w_uq, m_w_ukv, m_w_br_mla, m_w_br_fox, m_w_out, m_post_norm_g, v_meta_tokens, v_pre_norm_g, v_w_in, v_fox_forget_b, v_mla_q_norm_g, v_mla_kv_norm_g, v_w_uq, v_w_ukv, v_w_br_mla, v_w_br_fox, v_w_out, v_post_norm_g):
    me = 2 * lax.axis_index("x") + lax.axis_index("y")
    core = lax.axis_index("c")
    w_in_b = w_in.astype(BF16).reshape(D_MODEL, W_IN_SHARD)
    p2 = _pack_p2(w_uq[0], w_ukv[0], w_br_mla[0], w_br_fox[0], w_out[0], jnp.zeros((N_META, 256), F32), BF16)
    w_in_g, p2_g, meta_g = _gather_weights([w_in_b, p2], meta_tokens)
    slabs = [jnp.where(me == j, w_in_b, w_in_g[j]) for j in range(N_CHIPS)]
    pieces = [_unpack_p2(jnp.where(me == j, p2, p2_g[j])) for j in range(N_CHIPS)]
    w_uq_f = jnp.concatenate([p[0] for p in pieces], axis=1)
    w_ukv_f = jnp.concatenate([p[1] for p in pieces], axis=1)
    w_bm = jnp.concatenate([p[2] for p in pieces], axis=0)
    w_bf = jnp.concatenate([p[3] for p in pieces], axis=0)
    w_o = jnp.concatenate([p[4] for p in pieces], axis=0)
    meta_f = jnp.concatenate([jnp.where(me == j, meta_tokens, meta_g[j]) for j in range(N_CHIPS)], axis=1)
    kpe = _in_cols(slabs, C_KPE, C_ZMLA)
    w_small = jnp.concatenate(_in_cols(slabs, C_CQ, C_KPE) + kpe + kpe + [jnp.zeros((D_MODEL, 64), BF16)]
                              + _in_cols(slabs, C_FL, C_ZFOX) + [jnp.zeros((D_MODEL, 112), BF16)], axis=1)
    w_attn = jnp.concatenate(_in_cols(slabs, C_FQ, C_FL), axis=1)
    w_gate = jnp.concatenate(_in_cols(slabs, C_ZMLA, C_FQ) + _in_cols(slabs, C_ZFOX, C_END), axis=1)

    (loss_p, dx, dmeta, d_w_in, d_w_uq, d_w_ukv, d_w_bm, d_w_bf, d_w_out, dg_pre, dg_post, dg_q, dg_kv,
     dfb) = _local_step(x[0], loss_target[0], meta_f, w_small, w_attn, w_gate, w_uq_f, w_ukv_f, w_bm, w_bf, w_o,
                        pre_norm_g, post_norm_g, mla_q_norm_g, mla_kv_norm_g, fox_forget_b)

    g1 = d_w_in
    g2 = jnp.stack([_pack_p2(d_w_uq[:, 384 * j:384 * (j + 1)], d_w_ukv[:, 512 * j:512 * (j + 1)],
                             d_w_bm[256 * j:256 * (j + 1)], d_w_bf[256 * j:256 * (j + 1)],
                             d_w_out[256 * j:256 * (j + 1)], dmeta[:, 256 * j:256 * (j + 1)], F32)
                    for j in range(N_CHIPS)], axis=0)
    s1, s2 = _swap_halves([g1, g2])
    part1, part2 = _add_cores(g1, s1, "add_cores_w_in"), _add_cores(g2, s2, "add_cores_rest")
    landed = _scatter_chips([part1, part2])
    mine = [_add_chips(l, lax.dynamic_index_in_dim(p, me, 0, keepdims=False), nm)
            for l, p, nm in zip(landed, (part1, part2), ("add_chips_w_in", "add_chips_rest"))]
    theirs = _swap_reduced(mine)
    g_w_in, g_p2 = [jnp.concatenate([jnp.where(core == 0, a, b), jnp.where(core == 0, b, a)], axis=0)
                    for a, b in zip(mine, theirs)]
    g_w_uq, g_w_ukv, g_w_bm, g_w_bf, g_w_out, g_meta = _unpack_p2(g_p2)
    g_w_in = g_w_in[None]

    vec = jnp.concatenate([dg_pre.reshape(8, 128), dg_post.reshape(8, 128), dg_q.reshape(2, 128), dg_kv,
                           dfb, _pad_lanes(loss_p), jnp.zeros((3, 128), F32)], axis=0)
    tot = _allreduce_small(vec)
    loss = tot[20, 0]

    def small_pack(pre, post, gq_, gkv_, fb_):
        return jnp.concatenate([pre.reshape(8, 128), post.reshape(8, 128), gq_.reshape(2, 128), gkv_,
                                _pad_lanes(fb_), jnp.zeros((4, 128), F32)], axis=0)

    def small_unpack(t):
        return (t[0:8].reshape(1, 1024), t[8:16].reshape(1, 1024), t[16:18].reshape(1, 256), t[18:19],
                t[19:20, 0:HEADS])

    g_small = jnp.concatenate([tot[0:20], jnp.zeros((4, 128), F32)], axis=0)
    sm = _adamw(small_pack(pre_norm_g, post_norm_g, mla_q_norm_g, mla_kv_norm_g, fox_forget_b), g_small,
                small_pack(m_pre_norm_g, m_post_norm_g, m_mla_q_norm_g, m_mla_kv_norm_g, m_fox_forget_b),
                small_pack(v_pre_norm_g, v_post_norm_g, v_mla_q_norm_g, v_mla_kv_norm_g, v_fox_forget_b),
                "adamw_small")
    g_pre, g_post, g_q, g_kv, g_fb = small_unpack(g_small)
    (d_pre, d_post, d_q, d_kv, d_fb), (nm_pre, nm_post, nm_q, nm_kv, nm_fb), (nv_pre, nv_post, nv_q, nv_kv, nv_fb) = (
        small_unpack(t) for t in sm)

    d_meta, nm_meta, nv_meta = _adamw(meta_tokens, g_meta, m_meta_tokens, v_meta_tokens, "adamw_meta")
    d_win, nm_win, nv_win = _adamw(w_in, g_w_in, m_w_in, v_w_in, "adamw_w_in")
    d_wuq, nm_wuq, nv_wuq = _adamw(w_uq[0], g_w_uq, m_w_uq[0], v_w_uq[0], "adamw_w_uq")
    d_wukv, nm_wukv, nv_wukv = _adamw(w_ukv[0], g_w_ukv, m_w_ukv[0], v_w_ukv[0], "adamw_w_ukv")
    d_wbm, nm_wbm, nv_wbm = _adamw(w_br_mla[0], g_w_bm, m_w_br_mla[0], v_w_br_mla[0], "adamw_w_br_mla")
    d_wbf, nm_wbf, nv_wbf = _adamw(w_br_fox[0], g_w_bf, m_w_br_fox[0], v_w_br_fox[0], "adamw_w_br_fox")
    d_wo, nm_wo, nv_wo = _adamw(w_out[0], g_w_out, m_w_out[0], v_w_out[0], "adamw_w_out")

    def group(meta_, pre, win, fb_, q_, kv_, wuq, wukv, wbm, wbf, wo, post):
        return (meta_, pre, win, fb_, q_, kv_, wuq[None], wukv[None], wbm[None], wbf[None], wo[None], post)

    grads = group(g_meta, g_pre, g_w_in, g_fb, g_q, g_kv, g_w_uq, g_w_ukv, g_w_bm, g_w_bf, g_w_out, g_post)
    deltas = group(d_meta, d_pre, d_win, d_fb, d_q, d_kv, d_wuq, d_wukv, d_wbm, d_wbf, d_wo, d_post)
    new_m = group(nm_meta, nm_pre, nm_win, nm_fb, nm_q, nm_kv, nm_wuq, nm_wukv, nm_wbm, nm_wbf, nm_wo, nm_post)
    new_v = group(nv_meta, nv_pre, nv_win, nv_fb, nv_q, nv_kv, nv_wuq, nv_wukv, nv_wbm, nv_wbf, nv_wo, nv_post)
    return (loss, dx[None], *grads, *deltas, *new_m, *new_v)
```

```python
import math

import jax
import jax.numpy as jnp
import numpy as np
from jax import lax
from jax.experimental import pallas as pl
from jax.experimental.pallas import tpu as pltpu

F32 = jnp.float32
BF16 = jnp.bfloat16

D_MODEL = 1024
N_META = 16
RMS_EPS = 1e-6
HEADS = 16
PAIRS = HEADS // 2
HEAD_DIM = 64
LANES = 128
MLA_ROPE = 32
MLA_SCALE = 1.0 / math.sqrt(64 + 32)
FOX_SCALE = 1.0 / math.sqrt(64)
ROPE_THETA = 10000.0

PAD = 256
BLK = 256
QB = 512
UNROLL = 4
NEG = -1e30

C_CQ, C_CKV, C_KPE, C_ZMLA, C_FQ, C_FK, C_FV, C_FL, C_ZFOX, C_GA, C_GB, C_END = (
    0, 256, 384, 416, 1440, 2464, 3488, 4512, 4528, 5552, 6576, 7600)
SMALL_W = 640
W_IN_SHARD = 1900

P2_ROWS = 960
N_CHIPS = 4

ADAM_LR = 0.001
ADAM_B1 = 0.9
ADAM_B2 = 0.999
ADAM_EPS = 1e-08
ADAM_WD = 0.01
ADAM_STEP = 10

VMEM_BIG = 56 * 1024 * 1024
MM_VMEM_BUDGET = 44 * 1024 * 1024
MESH = pl.DeviceIdType.MESH


def _cp(dims, vmem=None):
    return pltpu.CompilerParams(dimension_semantics=dims, vmem_limit_bytes=vmem)


def _dot(a, b, ca, cb):
    return lax.dot_general(a, b, (((ca,), (cb,)), ((), ())), preferred_element_type=F32)


def _sigmoid(x):
    return 1.0 / (1.0 + jnp.exp(-x))


def _tile(n, cands):
    for c in cands:
        if n % c == 0:
            return c
    return n


def _mm(a, b, *, mode, out_dtype, name, acc=None):
    if mode == "nn":
        (M, K), N = a.shape, b.shape[1]
    elif mode == "nt":
        (M, K), N = a.shape, b.shape[0]
    else:
        (K, M), N = a.shape, b.shape[1]
    tm = _tile(M, (1088, 1024)) if M > 1024 else M
    tn = _tile(N, (1024,)) if N > 1024 else N
    nk = 1
    while True:
        tk = K // nk
        need = 2 * tk * (tm * a.dtype.itemsize + tn * b.dtype.itemsize) + tm * tn * (
            2 * jnp.dtype(out_dtype).itemsize + (8 if acc is not None else 0) + (4 if nk > 1 else 0))
        if need <= MM_VMEM_BUDGET or (tk // 2) % (16 if mode == "tn" else LANES) or tk <= 512:
            break
        nk *= 2
    ca, cb = {"nn": (1, 0), "nt": (1, 1), "tn": (0, 0)}[mode]
    a_spec = (pl.BlockSpec((tk, tm), lambda j, i, k: (k, i)) if mode == "tn"
              else pl.BlockSpec((tm, tk), lambda j, i, k: (i, k)))
    b_spec = (pl.BlockSpec((tn, tk), lambda j, i, k: (j, k)) if mode == "nt"
              else pl.BlockSpec((tk, tn), lambda j, i, k: (k, j)))
    o_spec = pl.BlockSpec((tm, tn), lambda j, i, k: (i, j))
    has_acc = acc is not None

    def body(*refs):
        a_ref, b_ref = refs[0], refs[1]
        acc_ref = refs[2] if has_acc else None
        o_ref = refs[3] if has_acc else refs[2]
        part = _dot(a_ref[...].astype(BF16), b_ref[...].astype(BF16), ca, cb)
        if nk == 1:
            if has_acc:
                part = part + acc_ref[...]
            o_ref[...] = part.astype(out_dtype)
        else:
            sc = refs[-1]
            k = pl.program_id(2)

            @pl.when(k == 0)
            def _():
                sc[...] = part + acc_ref[...] if has_acc else part

            @pl.when(k > 0)
            def _():
                sc[...] += part

            @pl.when(k == nk - 1)
            def _():
                o_ref[...] = sc[...].astype(out_dtype)

    ins = [a, b] + ([acc] if has_acc else [])
    in_specs = [a_spec, b_spec] + ([o_spec] if has_acc else [])
    return pl.pallas_call(
        body, name=name, grid=(N // tn, M // tm, nk), in_specs=in_specs, out_specs=o_spec,
        out_shape=jax.ShapeDtypeStruct((M, N), out_dtype),
        scratch_shapes=[pltpu.VMEM((tm, tn), F32)] if nk > 1 else [],
        compiler_params=_cp(("parallel", "parallel", "arbitrary"), VMEM_BIG))(*ins)


def _row(w):
    return pl.BlockSpec((BLK, w), lambda i: (i, 0))


def _rowc(w, c):
    return pl.BlockSpec((BLK, w), lambda i: (i, c))


def _full(shape):
    return pl.BlockSpec(shape, lambda i: tuple(0 for _ in shape))


def _rope(x, c, s):
    lane = lax.broadcasted_iota(jnp.int32, x.shape, 1)
    is_x1 = ((lane >> 4) & 1) == 0
    partner = jnp.where(is_x1, pltpu.roll(x, LANES - 16, 1), pltpu.roll(x, 16, 1))
    return x * c + partner * s


def _row_valid(i):
    rows = i * BLK + lax.broadcasted_iota(jnp.int32, (BLK, 1), 0)
    return (rows < N_META) | (rows >= PAD)


def _shift_rows(w):
    return pl.BlockSpec((BLK, w), lambda i: (jnp.maximum(i - 1, 0), 0))


def _h_block(i, x_ref, meta_ref):
    head = jnp.concatenate([meta_ref[...], jnp.zeros((BLK - N_META, D_MODEL), F32)], axis=0)
    return jnp.where(i == 0, head, x_ref[...])


def _rms_pre(x2, meta, g):
    lp = PAD + x2.shape[0]

    def body(x_ref, meta_ref, g_ref, u_ref):
        hv = _h_block(pl.program_id(0), x_ref, meta_ref)
        r = lax.rsqrt(jnp.mean(hv * hv, axis=-1, keepdims=True) + RMS_EPS)
        u_ref[...] = (hv * r * g_ref[...]).astype(BF16)

    return pl.pallas_call(
        body, name="rms_pre", grid=(lp // BLK,),
        in_specs=[_shift_rows(D_MODEL), _full((N_META, D_MODEL)), _full((1, D_MODEL))], out_specs=_row(D_MODEL),
        out_shape=jax.ShapeDtypeStruct((lp, D_MODEL), BF16),
        compiler_params=_cp(("parallel",)))(x2, meta, g)


def _split3(x):
    hi = x.astype(BF16)
    r1 = x - hi.astype(F32)
    mid = r1.astype(BF16)
    lo = (r1 - mid.astype(F32)).astype(BF16)
    return hi, mid, lo


def _small_prep(small, gq, gkv, fb, ctab, stab, tri):
    lp = small.shape[0]

    def body(sm_ref, gq_ref, gkv_ref, fb_ref, c_ref, s_ref, tri_ref, qn_ref, kvn_ref, kr_ref, ncum_ref, carry):
        i = pl.program_id(0)

        @pl.when(i == 0)
        def _():
            carry[...] = jnp.zeros_like(carry)

        cq = sm_ref[:, 0:256]
        r = lax.rsqrt(jnp.mean(cq * cq, axis=-1, keepdims=True) + RMS_EPS)
        qn_ref[...] = (cq * r * gq_ref[...]).astype(BF16)
        ckv = sm_ref[:, 256:384]
        r = lax.rsqrt(jnp.mean(ckv * ckv, axis=-1, keepdims=True) + RMS_EPS)
        kvn_ref[...] = (ckv * r * gkv_ref[...]).astype(BF16)
        kr_ref[...] = _rope(sm_ref[:, 384:512], c_ref[...], s_ref[...]).astype(BF16)
        fl = sm_ref[:, 512:640] + fb_ref[...]
        lf = jnp.minimum(fl, 0.0) - jnp.log(1.0 + jnp.exp(-jnp.abs(fl)))
        lf = jnp.where(_row_valid(i), lf, 0.0)
        hi, mid, lo = _split3(lf)
        t = tri_ref[...]
        cum = (_dot(t, hi, 1, 0) + _dot(t, mid, 1, 0)) + _dot(t, lo, 1, 0) + carry[...]
        ncum_ref[...] = -cum
        carry[...] = -ncum_ref[BLK - 1:BLK, :]

    return pl.pallas_call(
        body, name="small_prep", grid=(lp // BLK,),
        in_specs=[_row(SMALL_W), _full((1, 256)), _full((1, 128)), _full((1, 128)), _row(128), _row(128),
                  _full((BLK, BLK))],
        out_specs=[_row(256), _row(128), _row(128), _row(128)],
        out_shape=[jax.ShapeDtypeStruct((lp, 256), BF16), jax.ShapeDtypeStruct((lp, 128), BF16),
                   jax.ShapeDtypeStruct((lp, 128), BF16), jax.ShapeDtypeStruct((lp, 128), F32)],
        scratch_shapes=[pltpu.VMEM((1, 128), F32)],
        compiler_params=_cp(("arbitrary",)))(small, gq, gkv, fb, ctab, stab, tri)


def _rope_q(qraw, ctab, stab):
    lp = qraw.shape[0]

    def body(q_ref, c_ref, s_ref, o_ref):
        c, s = c_ref[...], s_ref[...]
        for p in range(PAIRS):
            lo = p * 256
            o_ref[:, lo:lo + 128] = q_ref[:, lo:lo + 128].astype(BF16)
            o_ref[:, lo + 128:lo + 256] = _rope(q_ref[:, lo + 128:lo + 256], c, s).astype(BF16)

    return pl.pallas_call(
        body, name="rope_q", grid=(lp // BLK,),
        in_specs=[_row(PAIRS * 256), _row(128), _row(128)], out_specs=_row(PAIRS * 256),
        out_shape=jax.ShapeDtypeStruct((lp, PAIRS * 256), BF16),
        compiler_params=_cp(("parallel",)))(qraw, ctab, stab)


def _gate_fwd(o_mla, o_fox, gate):
    lp = o_mla.shape[0]

    def body(om_ref, of_ref, zm_ref, zf_ref, am_ref, af_ref):
        zm = zm_ref[...].astype(F32)
        am_ref[...] = (om_ref[...] * (zm * _sigmoid(zm))).astype(BF16)
        zf = zf_ref[...].astype(F32)
        af_ref[...] = (of_ref[...] * (zf * _sigmoid(zf))).astype(BF16)

    return pl.pallas_call(
        body, name="gate_fwd", grid=(lp // BLK,),
        in_specs=[_row(D_MODEL), _row(D_MODEL), _rowc(D_MODEL, 0), _rowc(D_MODEL, 1)],
        out_specs=[_row(D_MODEL), _row(D_MODEL)],
        out_shape=[jax.ShapeDtypeStruct((lp, D_MODEL), BF16)] * 2,
        compiler_params=_cp(("parallel",)))(o_mla, o_fox, gate, gate)


def _merge_fwd(gate, y_mla, y_fox):
    lp = y_mla.shape[0]

    def body(ga_ref, gb_ref, ym_ref, yf_ref, m_ref):
        sa = _sigmoid(ga_ref[...].astype(F32))
        sb = _sigmoid(gb_ref[...].astype(F32))
        m_ref[...] = (sa * ym_ref[...] + sb * yf_ref[...]).astype(BF16)

    return pl.pallas_call(
        body, name="merge_fwd", grid=(lp // BLK,),
        in_specs=[_rowc(D_MODEL, 2), _rowc(D_MODEL, 3), _row(D_MODEL), _row(D_MODEL)],
        out_specs=_row(D_MODEL), out_shape=jax.ShapeDtypeStruct((lp, D_MODEL), BF16),
        compiler_params=_cp(("parallel",)))(gate, gate, y_mla, y_fox)


def _tail(x2, mixed, tgt, gpost):
    lp = mixed.shape[0]
    shift = _shift_rows(D_MODEL)

    def body(h_ref, mx_ref, t_ref, g_ref, dmx_ref, dy_ref, loss_ref, dg_ref):
        i = pl.program_id(0)

        @pl.when(i == 0)
        def _():
            loss_ref[...] = jnp.zeros_like(loss_ref)
            dg_ref[...] = jnp.zeros_like(dg_ref)
            dmx_ref[...] = jnp.zeros_like(dmx_ref)
            dy_ref[...] = jnp.zeros_like(dy_ref)

        @pl.when(i > 0)
        def _():
            mx = mx_ref[...]
            g = g_ref[...]
            r = lax.rsqrt(jnp.mean(mx * mx, axis=-1, keepdims=True) + RMS_EPS)
            nrm = mx * r
            e = (h_ref[...] + nrm * g) - t_ref[...]
            loss_ref[...] += jnp.sum(0.5 * jnp.sum(e * e, axis=-1, keepdims=True) * (1.0 / D_MODEL),
                                     axis=0, keepdims=True)
            dy = e * (1.0 / D_MODEL)
            dy_ref[...] = dy
            dg_ref[...] += jnp.sum(dy * nrm, axis=0, keepdims=True)
            w = dy * g
            dot = jnp.mean(w * mx, axis=-1, keepdims=True)
            dmx_ref[...] = (r * w - mx * (r * r * r * dot)).astype(BF16)

    return pl.pallas_call(
        body, name="tail", grid=(lp // BLK,),
        in_specs=[shift, _row(D_MODEL), shift, _full((1, D_MODEL))],
        out_specs=[_row(D_MODEL), _row(D_MODEL), _full((1, 1)), _full((1, D_MODEL))],
        out_shape=[jax.ShapeDtypeStruct((lp, D_MODEL), BF16), jax.ShapeDtypeStruct((lp, D_MODEL), F32),
                   jax.ShapeDtypeStruct((1, 1), F32), jax.ShapeDtypeStruct((1, D_MODEL), F32)],
        compiler_params=_cp(("arbitrary",)))(x2, mixed, tgt, gpost)


def _merge_bwd(dm, gate, y_mla, y_fox):
    lp = dm.shape[0]

    def body(dm_ref, ga_ref, gb_ref, ym_ref, yf_ref, dym_ref, dyf_ref, dg_ref):
        dm_v = dm_ref[...]
        sa = _sigmoid(ga_ref[...].astype(F32))
        sb = _sigmoid(gb_ref[...].astype(F32))
        dym_ref[...] = (dm_v * sa).astype(BF16)
        dyf_ref[...] = (dm_v * sb).astype(BF16)
        dg_ref[:, 0:D_MODEL] = (dm_v * ym_ref[...] * (sa * (1.0 - sa))).astype(BF16)
        dg_ref[:, D_MODEL:2 * D_MODEL] = (dm_v * yf_ref[...] * (sb * (1.0 - sb))).astype(BF16)

    return pl.pallas_call(
        body, name="merge_bwd", grid=(lp // BLK,),
        in_specs=[_row(D_MODEL), _rowc(D_MODEL, 2), _rowc(D_MODEL, 3), _row(D_MODEL), _row(D_MODEL)],
        out_specs=[_row(D_MODEL), _row(D_MODEL), _row(2 * D_MODEL)],
        out_shape=[jax.ShapeDtypeStruct((lp, D_MODEL), BF16), jax.ShapeDtypeStruct((lp, D_MODEL), BF16),
                   jax.ShapeDtypeStruct((lp, 2 * D_MODEL), BF16)],
        compiler_params=_cp(("parallel",)))(dm, gate, gate, y_mla, y_fox)


def _gate_bwd(da_mla, da_fox, o_mla, o_fox, gate):
    lp = da_mla.shape[0]

    def one(da, o, z):
        sg = _sigmoid(z)
        do = da * (z * sg)
        dz = da * o * (sg * (1.0 + z * (1.0 - sg)))
        return do.astype(BF16), dz.astype(BF16)

    def body(dam_ref, daf_ref, om_ref, of_ref, zm_ref, zf_ref, dom_ref, dof_ref, dz_ref):
        dom_ref[...], dz_ref[:, 0:D_MODEL] = one(dam_ref[...], om_ref[...], zm_ref[...].astype(F32))
        dof_ref[...], dz_ref[:, D_MODEL:2 * D_MODEL] = one(daf_ref[...], of_ref[...], zf_ref[...].astype(F32))

    return pl.pallas_call(
        body, name="gate_bwd", grid=(lp // BLK,),
        in_specs=[_row(D_MODEL)] * 4 + [_rowc(D_MODEL, 0), _rowc(D_MODEL, 1)],
        out_specs=[_row(D_MODEL), _row(D_MODEL), _row(2 * D_MODEL)],
        out_shape=[jax.ShapeDtypeStruct((lp, D_MODEL), BF16), jax.ShapeDtypeStruct((lp, D_MODEL), BF16),
                   jax.ShapeDtypeStruct((lp, 2 * D_MODEL), BF16)],
        compiler_params=_cp(("parallel",)))(da_mla, da_fox, o_mla, o_fox, gate, gate)


def _small_bwd(small, dqn, dkvn, dkr, dcol_t, drow_t, gq, gkv, fb, ctab, stab, triu):
    lp = small.shape[0]
    nb = lp // BLK

    def rrow(w):
        return pl.BlockSpec((BLK, w), lambda i: (nb - 1 - i, 0))

    def body(sm_ref, dqn_ref, dkvn_ref, dkr_ref, dcol_ref, drow_ref, gq_ref, gkv_ref, fb_ref, c_ref, s_ref, tri_ref,
             ds_ref, dgq_ref, dgkv_ref, dfb_ref, carry):
        i = pl.program_id(0)

        @pl.when(i == 0)
        def _():
            carry[...] = jnp.zeros_like(carry)
            dgq_ref[...] = jnp.zeros_like(dgq_ref)
            dgkv_ref[...] = jnp.zeros_like(dgkv_ref)
            dfb_ref[...] = jnp.zeros_like(dfb_ref)

        def norm_bwd(x, dn, g, dg_ref):
            r = lax.rsqrt(jnp.mean(x * x, axis=-1, keepdims=True) + RMS_EPS)
            dg_ref[...] += jnp.sum(dn * (x * r), axis=0, keepdims=True)
            w = dn * g
            dot = jnp.mean(w * x, axis=-1, keepdims=True)
            return r * w - x * (r * r * r * dot)

        ds_ref[:, 0:256] = norm_bwd(sm_ref[:, 0:256], dqn_ref[...], gq_ref[...], dgq_ref).astype(BF16)
        ds_ref[:, 256:384] = norm_bwd(sm_ref[:, 256:384], dkvn_ref[...], gkv_ref[...], dgkv_ref).astype(BF16)

        dk = dkr_ref[0]
        for p in range(1, PAIRS):
            dk = dk + dkr_ref[p]
        dk = _rope(dk, c_ref[...], -s_ref[...])
        lane = lax.broadcasted_iota(jnp.int32, dk.shape, 1)
        dk = jnp.where(lane < MLA_ROPE, dk + pltpu.roll(dk, LANES - MLA_ROPE, 1), 0.0)
        ds_ref[:, 384:512] = dk.astype(BF16)

        dcol = dcol_ref[0]
        for p in range(1, PAIRS):
            dcol = dcol + pltpu.roll(dcol_ref[p], 2 * p, 1)
        rows16 = jnp.concatenate([drow_ref[p, h:h + 1, :] for p in range(PAIRS) for h in range(2)], axis=0)
        eye = (lax.broadcasted_iota(jnp.int32, (HEADS, LANES), 0)
               == lax.broadcasted_iota(jnp.int32, (HEADS, LANES), 1)).astype(BF16)
        drow = sum(_dot(part, eye, 0, 0) for part in _split3(rows16))
        dcr = dcol - drow
        hi, mid, lo = _split3(dcr)
        t = tri_ref[...]
        suf = (_dot(t, hi, 1, 0) + _dot(t, mid, 1, 0)) + _dot(t, lo, 1, 0) + carry[...]
        fl = sm_ref[:, 512:640] + fb_ref[...]
        dfl = jnp.where(_row_valid(nb - 1 - i), -suf * _sigmoid(-fl), 0.0)
        ds_ref[:, 512:640] = dfl.astype(BF16)
        dfb_ref[...] += jnp.sum(dfl, axis=0, keepdims=True)
        carry[...] += jnp.sum(dcr, axis=0, keepdims=True)

    return pl.pallas_call(
        body, name="small_bwd", grid=(nb,),
        in_specs=[rrow(SMALL_W), rrow(256), rrow(128),
                  pl.BlockSpec((PAIRS, BLK, 128), lambda i: (0, nb - 1 - i, 0)),
                  pl.BlockSpec((PAIRS, BLK, 128), lambda i: (0, nb - 1 - i, 0)),
                  pl.BlockSpec((PAIRS, 2, BLK), lambda i: (0, 0, nb - 1 - i)),
                  _full((1, 256)), _full((1, 128)), _full((1, 128)), rrow(128), rrow(128), _full((BLK, BLK))],
        out_specs=[rrow(SMALL_W), _full((1, 256)), _full((1, 128)), _full((1, 128))],
        out_shape=[jax.ShapeDtypeStruct((lp, SMALL_W), BF16), jax.ShapeDtypeStruct((1, 256), F32),
                   jax.ShapeDtypeStruct((1, 128), F32), jax.ShapeDtypeStruct((1, 128), F32)],
        scratch_shapes=[pltpu.VMEM((1, 128), F32)],
        compiler_params=_cp(("arbitrary",)))(small, dqn, dkvn, dkr, dcol_t, drow_t, gq, gkv, fb, ctab, stab, triu)


def _pre_bwd(du, x2, meta, dy, gpre):
    s_rows = x2.shape[0]
    lp = PAD + s_rows
    shift = _shift_rows(D_MODEL)

    def body(du_ref, x_ref, meta_ref, dy_ref, g_ref, dx_ref, dmeta_ref, dg_ref):
        i = pl.program_id(0)

        @pl.when(i == 0)
        def _():
            dg_ref[...] = jnp.zeros_like(dg_ref)

        hv = _h_block(i, x_ref, meta_ref)
        duv = du_ref[...]
        r = lax.rsqrt(jnp.mean(hv * hv, axis=-1, keepdims=True) + RMS_EPS)
        dg_ref[...] += jnp.sum(duv * (hv * r), axis=0, keepdims=True)
        w = duv * g_ref[...]
        dot = jnp.mean(w * hv, axis=-1, keepdims=True)
        dh = dy_ref[...] + (r * w - hv * (r * r * r * dot))
        dx_ref[...] = dh

        @pl.when(i == 0)
        def _():
            dmeta_ref[...] = dh[0:N_META, :]

    return pl.pallas_call(
        body, name="pre_bwd", grid=(lp // BLK,),
        in_specs=[_row(D_MODEL), shift, _full((N_META, D_MODEL)), _row(D_MODEL), _full((1, D_MODEL))],
        out_specs=[shift, _full((N_META, D_MODEL)), _full((1, D_MODEL))],
        out_shape=[jax.ShapeDtypeStruct((s_rows, D_MODEL), F32), jax.ShapeDtypeStruct((N_META, D_MODEL), F32),
                   jax.ShapeDtypeStruct((1, D_MODEL), F32)],
        compiler_params=_cp(("arbitrary",)))(du, x2, meta, dy, gpre)


def _pair_masks(rope):
    lane = lax.broadcasted_iota(jnp.int32, (1, LANES), 1)
    mas = [lane < HEAD_DIM, lane >= HEAD_DIM]
    if not rope:
        return mas, mas
    wide = lax.broadcasted_iota(jnp.int32, (1, 2 * LANES), 1)
    rope_lo = LANES + MLA_ROPE
    return mas, [(wide < HEAD_DIM) | ((wide >= LANES) & (wide < rope_lo)),
                 ((wide >= HEAD_DIM) & (wide < LANES)) | ((wide >= rope_lo) & (wide < rope_lo + MLA_ROPE))]


def _mask2(x, masks):
    return [jnp.where(m, x, jnp.zeros_like(x)) for m in masks]


def _attn_fwd(q, k, v, *, kr=None, nbrep=None, scale, qcol, kcol, vcol, name):
    lp = q.shape[0]
    nq = 1 + (lp - PAD) // QB
    rope = kr is not None
    bias = nbrep is not None
    qw = 256 if rope else 128

    def body(*refs):
        it = iter(refs)
        q_ref, k_ref, v_ref = next(it), next(it), next(it)
        kr_ref = next(it) if rope else None
        nb_ref = next(it) if bias else None
        o_ref, lse_ref = next(it), next(it)
        i = pl.program_id(1)
        r0 = pl.multiple_of(jnp.where(i == 0, 0, PAD + QB * (i - 1)), BLK)
        b0 = r0 // BLK
        mas, hmask = _pair_masks(rope)
        qh = _mask2(q_ref[pl.ds(r0, QB), :], hmask)
        if bias:
            qh = [x * scale for x in qh]

        def causal(kc, n):
            key = kc * BLK + lax.broadcasted_iota(jnp.int32, (n, QB), 0)
            return (key <= r0 + lax.broadcasted_iota(jnp.int32, (n, QB), 1)) & ((kc > 0) | (n == N_META))

        def update(kcs, carry, masks, n=BLK):
            stats, acc = carry[:4], carry[4]
            k0s = [pl.multiple_of(kc * BLK, BLK) for kc in kcs]
            kks = [k_ref[pl.ds(k0, n), :] for k0 in k0s]
            if rope:
                kks = [jnp.concatenate([kk, kr_ref[pl.ds(k0, n), :]], axis=1) for kk, k0 in zip(kks, k0s)]
            new_stats, alphas, ps = [], [], [[] for _ in kcs]
            for h in range(2):
                m_prev, l_prev = stats[2 * h], stats[2 * h + 1]
                ss = []
                for kk, k0, mask in zip(kks, k0s, masks):
                    s = _dot(kk, qh[h], 1, 1)
                    if rope:
                        s = s * scale
                    if bias:
                        nbc = nb_ref[h, pl.ds(k0, n), :]
                        s = s + jnp.concatenate([nbc] * (QB // LANES), axis=1)
                    if mask is not None:
                        s = jnp.where(mask, s, NEG)
                    ss.append(s)
                m_new = m_prev
                for s in ss:
                    m_new = jnp.maximum(m_new, jnp.max(s, axis=0, keepdims=True))
                alpha = jnp.exp(m_prev - m_new)
                l_new = alpha * l_prev
                for j, s in enumerate(ss):
                    p = jnp.exp(s - m_new)
                    l_new = l_new + jnp.sum(p, axis=0, keepdims=True)
                    ps[j].append(p.astype(BF16))
                new_stats += [m_new, l_new]
                alphas.append(alpha)
            vcat = jnp.concatenate([x for k0 in k0s for x in _mask2(v_ref[pl.ds(k0, n), :], mas)], axis=0)
            pv = _dot(vcat, jnp.concatenate([p for pj in ps for p in pj], axis=0), 0, 0)
            a_full = jnp.concatenate([jnp.broadcast_to(a, (HEAD_DIM, QB)) for a in alphas], axis=0)
            return (*new_stats, a_full * acc + pv)

        neg = jnp.full((1, QB), NEG, F32)
        zero = jnp.zeros((1, QB), F32)
        c = update([0], (neg, zero, neg, zero, jnp.zeros((LANES, QB), F32)), [causal(0, N_META)], N_META)
        n_mid = jnp.maximum(b0 - 1, 0)
        c = lax.fori_loop(0, n_mid // 4, lambda t, cr: update([4 * t + u for u in (1, 2, 3, 4)], cr, [None] * 4), c)
        c = lax.fori_loop(0, (n_mid % 4) // 2, lambda t, cr: update([n_mid - 1, n_mid], cr, [None, None]), c)
        c = update([b0, b0 + 1], c, [causal(b0, BLK), causal(b0 + 1, BLK)])
        inv =jnp.concatenate([jnp.broadcast_to(1.0 / c[1], (HEAD_DIM, QB)),
                               jnp.broadcast_to(1.0 / c[3], (HEAD_DIM, QB))], axis=0)
        o_t = (c[4] * inv).T
        lses = [c[0] + jnp.log(c[1]), c[2] + jnp.log(c[3])]
        o_ref[pl.ds(r0, BLK), :] = o_t[0:BLK]
        for h in range(2):
            lse_ref[0, h:h + 1, pl.ds(r0, BLK)] = lses[h][:, 0:BLK]

        @pl.when(i > 0)
        def _():
            r1 = pl.multiple_of(r0 + BLK, BLK)
            o_ref[pl.ds(r1, QB - BLK), :] = o_t[BLK:QB]
            for h in range(2):
                lse_ref[0, h:h + 1, pl.ds(r1, QB - BLK)] = lses[h][:, BLK:QB]

    in_specs = [pl.BlockSpec((lp, qw), lambda p, i: (0, qcol + p)),
                pl.BlockSpec((lp, 128), lambda p, i: (0, kcol(p))),
                pl.BlockSpec((lp, 128), lambda p, i: (0, vcol(p)))]
    ins = [q, k, v]
    if rope:
        in_specs.append(pl.BlockSpec((lp, 128), lambda p, i: (0, 0)))
        ins.append(kr)
    if bias:
        in_specs.append(pl.BlockSpec((2, lp, 128), lambda p, i: (p, 0, 0)))
        ins.append(nbrep)
    return pl.pallas_call(
        body, name=name, grid=(PAIRS, nq), in_specs=in_specs,
        out_specs=[pl.BlockSpec((lp, 128), lambda p, i: (0, p)),
                   pl.BlockSpec((1, 2, lp), lambda p, i: (p, 0, 0))],
        out_shape=[jax.ShapeDtypeStruct((lp, D_MODEL), F32), jax.ShapeDtypeStruct((PAIRS, 2, lp), F32)],
        compiler_params=_cp(("parallel", "arbitrary"), VMEM_BIG))(*ins)


def _attn_bwd(q, k, v, do, o, lse, *, kr=None, rtabs=None, nbrep=None, scale, qcol, kcol, vcol, name):
    lp = q.shape[0]
    nb = lp // BLK
    rope = kr is not None
    bias = nbrep is not None
    qw = 256 if rope else 128

    def body(*refs):
        it = iter(refs)
        q_ref, k_ref, v_ref = next(it), next(it), next(it)
        kr_ref = next(it) if rope else None
        nb_ref = next(it) if bias else None
        do_ref, o_ref, lse_ref = next(it), next(it), next(it)
        ct_ref, st_ref = (next(it), next(it)) if rope else (None, None)
        dq_out, dk_ref, dv_ref = next(it), next(it), next(it)
        x_ref = next(it)
        drow_ref = next(it) if bias else None
        delta, dq_ref = next(it), next(it)
        kb = pl.program_id(1)
        mas, hmask = _pair_masks(rope)
        lane = lax.broadcasted_iota(jnp.int32, (1, LANES), 1)

        @pl.when(kb == 0)
        def _():
            dq_ref[...] = jnp.zeros_like(dq_ref)
            if bias:
                drow_ref[...] = jnp.zeros_like(drow_ref)
            sub = lax.broadcasted_iota(jnp.int32, (8, LANES), 0)
            sel = (((sub == 0) & mas[0]) | ((sub == 1) & mas[1])).astype(BF16)

            def dstep(c, carry):
                r0 = pl.multiple_of(c * BLK, BLK)
                prod = do_ref[pl.ds(r0, BLK), :].astype(F32) * o_ref[pl.ds(r0, BLK), :]
                hi, mid, lo = _split3(prod)
                delta[:, pl.ds(r0, BLK)] = (_dot(sel, hi, 1, 1) + _dot(sel, mid, 1, 1)) + _dot(sel, lo, 1, 1)
                return carry

            lax.fori_loop(0, nb, dstep, 0)

        def masked_q(q0):
            qh = _mask2(q_ref[pl.ds(q0, BLK), :], hmask)
            return [x * scale for x in qh] if bias else qh

        def key_pass(n):
            kk = k_ref[0:n, :]
            if rope:
                kk = jnp.concatenate([kk, kr_ref[0:n, :]], axis=1)
            vh = _mask2(v_ref[0:n, :], mas)
            kcat = jnp.concatenate(_mask2(kk, hmask), axis=0)
            if bias:
                kcat = kcat * scale
                nbc = [jnp.concatenate([nb_ref[h, 0:n, :], nb_ref[h, 0:n, :]], axis=1) for h in range(2)]
            diag_mask = (lax.broadcasted_iota(jnp.int32, (n, BLK), 0) <= lax.broadcasted_iota(jnp.int32, (n, BLK), 1))

            def chunk(qc, carry, mask):
                carry = list(carry)
                q0 = pl.multiple_of(qc * BLK, BLK)
                dov = do_ref[pl.ds(q0, BLK), :]
                doh = _mask2(dov, mas)
                qh = masked_q(q0)
                pbs, dss = [], []
                for h in range(2):
                    s = _dot(kk, qh[h], 1, 1)
                    if rope:
                        s = s * scale
                    if bias:
                        s = s + nbc[h]
                    p = jnp.exp(s - lse_ref[0, h:h + 1, pl.ds(q0, BLK)])
                    if mask is not None:
                        p = jnp.where(mask, p, 0.0)
                    ds = p * (_dot(vh[h], dov, 1, 1) - delta[h:h + 1, pl.ds(q0, BLK)])
                    if bias:
                        drow_ref[0, h:h + 1, pl.ds(q0, BLK)] += jnp.sum(ds, axis=0, keepdims=True)
                        carry[2 + h] = carry[2 + h] + jnp.sum(ds, axis=1, keepdims=True)
                    else:
                        ds = ds * scale
                    pbs.append(p.astype(BF16))
                    dss.append(ds.astype(BF16))
                ds_lanes = jnp.concatenate(dss, axis=1)
                ds_rows = jnp.concatenate(dss, axis=0)
                carry[0] = carry[0] + _dot(ds_lanes, jnp.concatenate(qh, axis=0), 1, 0)
                carry[1] = carry[1] + _dot(jnp.concatenate(pbs, axis=1), jnp.concatenate(doh, axis=0), 1, 0)
                dq_ref[pl.ds(q0, BLK), :] += _dot(ds_rows, kcat, 0, 0)
                return tuple(carry)

            init = [jnp.zeros((n, qw), F32), jnp.zeros((n, LANES), F32)]
            if bias:
                init += [jnp.zeros((n, 1), F32), jnp.zeros((n, 1), F32)]
            c = chunk(kb, tuple(init), diag_mask)
            rest = nb - 1 - kb

            def several(t, cr):
                for u in range(UNROLL):
                    cr = chunk(kb + 1 + UNROLL * t + u, cr, None)
                return cr

            c = lax.fori_loop(0, rest // UNROLL, several, c)
            c = lax.fori_loop(nb - rest % UNROLL, nb, lambda qc, cr: chunk(qc, cr, None), c)

            def rows(a, dtype):
                a = a.astype(dtype)
                return a if n == BLK else jnp.concatenate([a, jnp.zeros((BLK - n, a.shape[1]), dtype)], axis=0)

            dk_ref[...] = rows(c[0][:, 0:LANES], BF16)
            dv_ref[...] = rows(c[1], BF16)
            if rope:
                x_ref[0] = rows(c[0][:, LANES:2 * LANES], F32)
            if bias:
                x_ref[0] = rows(jnp.where(lane == 0, c[2], jnp.where(lane == 1, c[3], 0.0)), F32)

        @pl.when(kb == 0)
        def _():
            key_pass(N_META)

        @pl.when(kb > 0)
        def _():
            key_pass(BLK)

        @pl.when(kb == nb - 1)
        def _():
            def fin(c, carry):
                r0 = pl.multiple_of(c * BLK, BLK)
                dq = dq_ref[pl.ds(r0, BLK), :]
                if rope:
                    back = _rope(dq[:, LANES:2 * LANES], ct_ref[pl.ds(r0, BLK), :], -st_ref[pl.ds(r0, BLK), :])
                    dq = jnp.concatenate([dq[:, 0:LANES], back], axis=1)
                dq_out[pl.ds(r0, BLK), :] = dq.astype(BF16)
                return carry

            lax.fori_loop(0, nb, fin, 0)

    in_specs = [pl.BlockSpec((lp, qw), lambda p, j: (0, qcol + p)),
                pl.BlockSpec((BLK, 128), lambda p, j: (j, kcol(p))),
                pl.BlockSpec((BLK, 128), lambda p, j: (j, vcol(p)))]
    ins = [q, k, v]
    if rope:
        in_specs.append(pl.BlockSpec((BLK, 128), lambda p, j: (j, 0)))
        ins.append(kr)
    if bias:
        in_specs.append(pl.BlockSpec((2, BLK, 128), lambda p, j: (p, j, 0)))
        ins.append(nbrep)
    in_specs += [pl.BlockSpec((lp, 128), lambda p, j: (0, p)), pl.BlockSpec((lp, 128), lambda p, j: (0, p)),
                 pl.BlockSpec((1, 2, lp), lambda p, j: (p, 0, 0))]
    ins += [do, o, lse]
    if rope:
        in_specs += [pl.BlockSpec((lp, 128), lambda p, j: (0, 0))] * 2
        ins += list(rtabs)
    out_specs = [pl.BlockSpec((lp, qw), lambda p, j: (0, p)),
                 pl.BlockSpec((BLK, 128), lambda p, j: (j, p)),
                 pl.BlockSpec((BLK, 128), lambda p, j: (j, p)),
                 pl.BlockSpec((1, BLK, 128), lambda p, j: (p, j, 0))]
    out_shape = [jax.ShapeDtypeStruct((lp, PAIRS * qw), BF16), jax.ShapeDtypeStruct((lp, D_MODEL), BF16),
                 jax.ShapeDtypeStruct((lp, D_MODEL), BF16), jax.ShapeDtypeStruct((PAIRS, lp, 128), F32)]
    if bias:
        out_specs.append(pl.BlockSpec((1, 2, lp), lambda p, j: (p, 0, 0)))
        out_shape.append(jax.ShapeDtypeStruct((PAIRS, 2, lp), F32))
    return pl.pallas_call(
        body, name=name, grid=(PAIRS, nb), in_specs=in_specs, out_specs=out_specs, out_shape=out_shape,
        scratch_shapes=[pltpu.VMEM((8, lp), F32), pltpu.VMEM((lp, qw), F32)],
        compiler_params=_cp(("parallel", "arbitrary"), VMEM_BIG))(*ins)


def _adamw(w, g, m, v, name):
    lead = w.ndim - 2
    rows, cols = w.shape[lead:]
    big = rows * cols > 512 * 1024
    tr = 128 if big and rows % 128 == 0 else rows
    tc = 256 if big and tr == rows else cols

    def body(w_ref, g_ref, m_ref, v_ref, d_ref, nm_ref, nv_ref):
        gv = g_ref[...]
        nm = ADAM_B1 * m_ref[...] + (1.0 - ADAM_B1) * gv
        nv = ADAM_B2 * v_ref[...] + (1.0 - ADAM_B2) * (gv * gv)
        m_hat = nm / (1.0 - ADAM_B1 ** ADAM_STEP)
        v_hat = nv / (1.0 - ADAM_B2 ** ADAM_STEP)
        d_ref[...] = -ADAM_LR * (m_hat / (jnp.sqrt(v_hat) + ADAM_EPS) + ADAM_WD * w_ref[...])
        nm_ref[...] = nm
        nv_ref[...] = nv

    spec = pl.BlockSpec((1,) * lead + (tr, tc), lambda i, j: (0,) * lead + (i, j))
    return pl.pallas_call(
        body, name=name, grid=(rows // tr, cols // tc), in_specs=[spec] * 4, out_specs=[spec] * 3,
        out_shape=[jax.ShapeDtypeStruct(w.shape, F32)] * 3,
        compiler_params=_cp(("parallel", "parallel"), VMEM_BIG))(w, g, m, v)


def _add_cores(g, from_sib, name):
    n, rows, cols = g.shape
    half = rows // 2
    tr = _tile(half, (256, 240))
    nt = half // tr

    def body(lo_ref, hi_ref, s_ref, o_ref):
        mine = jnp.where(lax.axis_index("c") == 0, lo_ref[0], hi_ref[0])
        o_ref[0] = (mine + s_ref[0]).astype(BF16)

    return pl.pallas_call(
        body, name=name, grid=(n, nt),
        in_specs=[pl.BlockSpec((1, tr, cols), lambda j, i: (j, i, 0)),
                  pl.BlockSpec((1, tr, cols), lambda j, i: (j, nt + i, 0)),
                  pl.BlockSpec((1, tr, cols), lambda j, i: (j, i, 0))],
        out_specs=pl.BlockSpec((1, tr, cols), lambda j, i: (j, i, 0)),
        out_shape=jax.ShapeDtypeStruct((n, half, cols), BF16),
        compiler_params=_cp(("parallel", "parallel"), VMEM_BIG))(g, g, from_sib)


def _add_chips(x, own, name):
    n, rows, cols = x.shape
    tr = _tile(rows, (256, 240))

    def body(x_ref, own_ref, o_ref):
        me = 2 * lax.axis_index("x") + lax.axis_index("y")
        v = [jnp.where(me == k, own_ref[...], x_ref[k]).astype(F32) for k in range(N_CHIPS)]
        o_ref[...] = ((v[0] + v[1]) + v[2]) + v[3]

    return pl.pallas_call(
        body, name=name, grid=(rows // tr,),
        in_specs=[pl.BlockSpec((n, tr, cols), lambda i: (0, i, 0)), pl.BlockSpec((tr, cols), lambda i: (i, 0))],
        out_specs=pl.BlockSpec((tr, cols), lambda i: (i, 0)),
        out_shape=jax.ShapeDtypeStruct((rows, cols), F32), compiler_params=_cp(("parallel",), VMEM_BIG))(x, own)


def _axes():
    return lax.axis_index("x"), lax.axis_index("y"), lax.axis_index("c")


def _other_chips(x, y):
    return [(1 - x, y), (x, 1 - y), (1 - x, 1 - y)]


ANY = pl.BlockSpec(memory_space=pl.ANY)


def _rcopy(src, dst, send_sems, recv_sems, k, to):
    return pltpu.make_async_remote_copy(src_ref=src, dst_ref=dst, send_sem=send_sems.at[k], recv_sem=recv_sems.at[k],
                                        device_id=to, device_id_type=MESH)


def _gather_weights(shards, meta):
    n = len(shards)

    def body(*refs):
        srcs, meta_ref = refs[:n], refs[n]
        outs, mout_ref = refs[n + 1:2 * n + 1], refs[2 * n + 1]
        send_sems, recv_sems = refs[2 * n + 2:]
        x, y, c = _axes()
        me = 2 * x + y
        sib = (x, y, 1 - c)
        chips = _other_chips(x, y)

        def half(t, chip_idx, cc):
            hr = shards[t].shape[0] // 2
            return outs[t].at[chip_idx, pl.ds(cc * hr, hr), :]

        first = []
        for j, (px, py) in enumerate(chips):
            for t in range(n):
                hr = shards[t].shape[0] // 2
                first.append(_rcopy(srcs[t].at[pl.ds(c * hr, hr), :], half(t, me, c), send_sems, recv_sems,
                                    3 * t + j, (px, py, c)))
            first.append(_rcopy(meta_ref, mout_ref.at[me], send_sems, recv_sems, 3 * n + j, (px, py, c)))
        for cp in first:
            cp.start()
        passed = []
        for j, (px, py) in enumerate(chips):
            src_chip = 2 * px + py
            for t in range(n):
                _rcopy(half(t, src_chip, c), half(t, src_chip, c), send_sems, recv_sems, 3 * t + j, sib).wait_recv()
                fwd = _rcopy(half(t, src_chip, c), half(t, src_chip, c), send_sems, recv_sems, 3 * (n + 1 + t) + j, sib)
                fwd.start()
                passed.append(fwd)
            _rcopy(mout_ref.at[src_chip], mout_ref.at[src_chip], send_sems, recv_sems, 3 * n + j, sib).wait_recv()
        for j, (px, py) in enumerate(chips):
            src_chip = 2 * px + py
            for t in range(n):
                _rcopy(half(t, src_chip, 1 - c), half(t, src_chip, 1 - c), send_sems, recv_sems,
                       3 * (n + 1 + t) + j, sib).wait_recv()
        for cp in first + passed:
            cp.wait_send()

    nsem = 3 * (2 * n + 1)
    return pl.pallas_call(
        body, name="gather_weights", in_specs=[ANY] * (n + 1), out_specs=[ANY] * (n + 1),
        out_shape=[jax.ShapeDtypeStruct((N_CHIPS,) + s.shape, s.dtype) for s in shards]
        + [jax.ShapeDtypeStruct((N_CHIPS,) + meta.shape, meta.dtype)],
        scratch_shapes=[pltpu.SemaphoreType.DMA((nsem,)), pltpu.SemaphoreType.DMA((nsem,))])(*shards, meta)


def _swap_halves(gs):
    n = len(gs)

    def body(*refs):
        srcs, outs = refs[:n], refs[n:2 * n]
        send_sems, recv_sems = refs[2 * n:]
        x, y, c = _axes()
        cps = []
        for t in range(n):
            hr = gs[t].shape[1] // 2
            for j in range(N_CHIPS):
                cps.append(_rcopy(srcs[t].at[j, pl.ds((1 - c) * hr, hr), :], outs[t].at[j], send_sems, recv_sems,
                                  N_CHIPS * t + j, (x, y, 1 - c)))
        for cp in cps:
            cp.start()
        for cp in cps:
            cp.wait()

    return pl.pallas_call(
        body, name="swap_halves", in_specs=[ANY] * n, out_specs=[ANY] * n,
        out_shape=[jax.ShapeDtypeStruct((N_CHIPS, g.shape[1] // 2, g.shape[2]), g.dtype) for g in gs],
        scratch_shapes=[pltpu.SemaphoreType.DMA((N_CHIPS * n,)), pltpu.SemaphoreType.DMA((N_CHIPS * n,))])(*gs)


def _scatter_chips(parts):
    n = len(parts)

    def body(*refs):
        srcs, outs = refs[:n], refs[n:2 * n]
        send_sems, recv_sems = refs[2 * n:]
        x, y, c = _axes()
        me = 2 * x + y
        cps = []
        for j, (px, py) in enumerate(_other_chips(x, y)):
            for t in range(n):
                cps.append(_rcopy(srcs[t].at[2 * px + py], outs[t].at[me], send_sems, recv_sems, 3 * t + j,
                                  (px, py, c)))
        for cp in cps:
            cp.start()
        for cp in cps:
            cp.wait()

    return pl.pallas_call(
        body, name="scatter_chips", in_specs=[ANY] * n, out_specs=[ANY] * n,
        out_shape=[jax.ShapeDtypeStruct(p.shape, p.dtype) for p in parts],
        scratch_shapes=[pltpu.SemaphoreType.DMA((3 * n,)), pltpu.SemaphoreType.DMA((3 * n,))])(*parts)


def _swap_reduced(rs):
    n = len(rs)

    def body(*refs):
        srcs, outs = refs[:n], refs[n:2 * n]
        send_sems, recv_sems = refs[2 * n:]
        x, y, c = _axes()
        cps = [_rcopy(srcs[t], outs[t], send_sems, recv_sems, t, (x, y, 1 - c)) for t in range(n)]
        for cp in cps:
            cp.start()
        for cp in cps:
            cp.wait()

    return pl.pallas_call(
        body, name="swap_reduced", in_specs=[ANY] * n, out_specs=[ANY] * n,
        out_shape=[jax.ShapeDtypeStruct(r.shape, r.dtype) for r in rs],
        scratch_shapes=[pltpu.SemaphoreType.DMA((n,)), pltpu.SemaphoreType.DMA((n,))])(*rs)


SMALL_ROWS = 24


def _allreduce_small(vec):
    def body(v_ref, out_ref, slots, send_sems, recv_sems):
        x, y, c = _axes()
        me = 4 * x + 2 * y + c
        slots[me] = v_ref[...]
        cps = []
        for k in range(1, 8):
            kx, ky, kc = (k >> 2) & 1, (k >> 1) & 1, k & 1
            peer = (1 - x if kx else x, 1 - y if ky else y, 1 - c if kc else c)
            cps.append(_rcopy(v_ref, slots.at[me], send_sems, recv_sems, k - 1, peer))
        for cp in cps:
            cp.start()
        for cp in cps:
            cp.wait()
        tot = slots[0]
        for k in range(1, 8):
            tot = tot + slots[k]
        out_ref[...] = tot

    return pl.pallas_call(
        body, name="allreduce_small",
        in_specs=[pl.BlockSpec(memory_space=pltpu.VMEM)], out_specs=pl.BlockSpec(memory_space=pltpu.VMEM),
        out_shape=jax.ShapeDtypeStruct((SMALL_ROWS, 128), F32),
        scratch_shapes=[pltpu.VMEM((8, SMALL_ROWS, 128), F32), pltpu.SemaphoreType.DMA((7,)),
                        pltpu.SemaphoreType.DMA((7,))])(vec)


def _pack_p2(w_uq, w_ukv, w_br_mla, w_br_fox, w_out, meta, dtype):
    parts = [w_uq.reshape(96, D_MODEL), w_ukv.reshape(64, D_MODEL), w_br_mla, w_br_fox, w_out,
             meta.reshape(4, D_MODEL), jnp.zeros((P2_ROWS - 932, D_MODEL), meta.dtype)]
    return jnp.concatenate([p.astype(dtype) for p in parts], axis=0)


def _unpack_p2(pk):
    return (pk[0:96].reshape(256, 384), pk[96:160].reshape(128, 512), pk[160:416], pk[416:672], pk[672:928],
            pk[928:932].reshape(N_META, 256))


def _uq_arrange(w):
    w3 = w.reshape(256, HEADS, 96)
    nope = w3[:, :, :64].reshape(256, PAIRS, 128)
    pe = w3[:, :, 64:].reshape(256, PAIRS, 64)
    return jnp.concatenate([nope, pe, jnp.zeros((256, PAIRS, 64), w.dtype)], axis=2).reshape(256, PAIRS * 256)


def _uq_restore(g):
    g3 = g.reshape(256, PAIRS, 256)
    nope = g3[:, :, :128].reshape(256, HEADS, 64)
    pe = g3[:, :, 128:192].reshape(256, HEADS, 32)
    return jnp.concatenate([nope, pe], axis=2).reshape(256, HEADS * 96)


def _ukv_arrange(w):
    w3 = w.reshape(128, HEADS, 128)
    return jnp.concatenate([w3[:, :, :64].reshape(128, 1024), w3[:, :, 64:].reshape(128, 1024)], axis=1)


def _ukv_restore(g):
    kn = g[:, :1024].reshape(128, HEADS, 64)
    vv = g[:, 1024:].reshape(128, HEADS, 64)
    return jnp.concatenate([kn, vv], axis=2).reshape(128, HEADS * 128)


def _rope_tables(lp):
    r = np.arange(lp)
    pos = np.where(r < N_META, r, np.where(r >= PAD, r - PAD + N_META, 0)).astype(np.float32)
    half = MLA_ROPE // 2
    inv_freq = np.float32(ROPE_THETA) ** (-np.arange(half, dtype=np.float32) / np.float32(half))
    ang = (pos[:, None] * inv_freq[None, :]).astype(np.float32)
    cos, sin = np.cos(ang).astype(np.float32), np.sin(ang).astype(np.float32)
    one, zero = np.ones((lp, 64), np.float32), np.zeros((lp, 64), np.float32)
    return (jnp.asarray(np.concatenate([cos, cos, cos, cos, one], axis=1)),
            jnp.asarray(np.concatenate([-sin, sin, -sin, sin, zero], axis=1)))


def _pad_lanes(v, n=128):
    return jnp.pad(v, ((0, 0), (0, n - v.shape[1])))


def _in_cols(slabs, a, b):
    out = []
    for j in range(N_CHIPS):
        lo, hi = max(a, W_IN_SHARD * j), min(b, W_IN_SHARD * (j + 1))
        if lo < hi:
            out.append(slabs[j][:, lo - W_IN_SHARD * j:hi - W_IN_SHARD * j])
    return out


def _local_step(x2, tgt2, meta_f, w_small, w_attn, w_gate, w_uq_f, w_ukv_f, w_bm, w_bf, w_o, pre_norm_g,
                post_norm_g, mla_q_norm_g, mla_kv_norm_g, fox_forget_b):
    s_rows = x2.shape[0]
    lp = PAD + s_rows
    w_uq_a = _uq_arrange(w_uq_f)
    w_ukv_a = _ukv_arrange(w_ukv_f)

    ctab, stab = _rope_tables(lp)
    ii = jnp.arange(BLK)
    tri_lo = (ii[:, None] >= ii[None, :]).astype(BF16)
    tri_up = (ii[:, None] <= ii[None, :]).astype(BF16)
    fb128 = _pad_lanes(fox_forget_b)

    u = _rms_pre(x2, meta_f, pre_norm_g)
    small = _mm(u, w_small, mode="nn", out_dtype=F32, name="proj_small")
    attn = _mm(u, w_attn, mode="nn", out_dtype=BF16, name="proj_attn")
    gate = _mm(u, w_gate, mode="nn", out_dtype=BF16, name="proj_gate")
    qn, kvn, kr, ncum = _small_prep(small, mla_q_norm_g, mla_kv_norm_g, fb128, ctab, stab, tri_lo)
    qraw = _mm(qn, w_uq_a, mode="nn", out_dtype=F32, name="mla_q")
    qcat = _rope_q(qraw, ctab, stab)
    kv = _mm(kvn, w_ukv_a, mode="nn", out_dtype=BF16, name="mla_kv")
    nbrep = jnp.broadcast_to(ncum[:, :HEADS].T[:, :, None], (HEADS, lp, LANES))

    mla_cols = dict(qcol=0, kcol=lambda p: p, vcol=lambda p: PAIRS + p)
    fox_cols = dict(qcol=0, kcol=lambda p: PAIRS + p, vcol=lambda p: 2 * PAIRS + p)
    o_mla, lse_mla = _attn_fwd(qcat, kv, kv, kr=kr, scale=MLA_SCALE, name="mla_fwd", **mla_cols)
    o_fox, lse_fox = _attn_fwd(attn, attn, attn, nbrep=nbrep, scale=FOX_SCALE, name="fox_fwd", **fox_cols)

    a_mla, a_fox = _gate_fwd(o_mla, o_fox, gate)
    y_mla = _mm(a_mla, w_bm, mode="nn", out_dtype=F32, name="br_mla")
    y_fox = _mm(a_fox, w_bf, mode="nn", out_dtype=F32, name="br_fox")
    mg = _merge_fwd(gate, y_mla, y_fox)
    mixed = _mm(mg, w_o, mode="nn", out_dtype=F32, name="out_proj")
    dmixed, dy, loss_p, dg_post = _tail(x2, mixed, tgt2, post_norm_g)

    d_w_out = _mm(mg, dmixed, mode="tn", out_dtype=F32, name="d_w_out")
    dm = _mm(dmixed, w_o, mode="nt", out_dtype=F32, name="d_merge")
    dy_mla, dy_fox, dgate_ab = _merge_bwd(dm, gate, y_mla, y_fox)
    d_w_bm = _mm(a_mla, dy_mla, mode="tn", out_dtype=F32, name="d_w_br_mla")
    d_w_bf = _mm(a_fox, dy_fox, mode="tn", out_dtype=F32, name="d_w_br_fox")
    da_mla = _mm(dy_mla, w_bm, mode="nt", out_dtype=F32, name="d_a_mla")
    da_fox = _mm(dy_fox, w_bf, mode="nt", out_dtype=F32, name="d_a_fox")
    do_mla, do_fox, dgate_z = _gate_bwd(da_mla, da_fox, o_mla, o_fox, gate)

    dq_a, dkn, dvm, dkr = _attn_bwd(qcat, kv, kv, do_mla, o_mla, lse_mla, kr=kr, rtabs=(ctab, stab), scale=MLA_SCALE,
                                    name="mla_bwd", **mla_cols)
    dfq, dfk, dfv, dcol, drow = _attn_bwd(attn, attn, attn, do_fox, o_fox, lse_fox, nbrep=nbrep, scale=FOX_SCALE,
                                          name="fox_bwd", **fox_cols)

    d_w_uq_a = _mm(qn, dq_a, mode="tn", out_dtype=F32, name="d_w_uq")
    dqn = _mm(dq_a, w_uq_a, mode="nt", out_dtype=F32, name="d_qn")
    d_w_ukv_a = jnp.concatenate([_mm(kvn, dkn, mode="tn", out_dtype=F32, name="d_w_uk"),
                                 _mm(kvn, dvm, mode="tn", out_dtype=F32, name="d_w_uv")], axis=1)
    dkvn = _mm(dkn, w_ukv_a[:, :1024], mode="nt", out_dtype=F32, name="d_kvn_k")
    dkvn = _mm(dvm, w_ukv_a[:, 1024:], mode="nt", out_dtype=F32, name="d_kvn_v", acc=dkvn)
    dsmall, dg_q, dg_kv, dfb = _small_bwd(small, dqn, dkvn, dkr, dcol, drow, mla_q_norm_g, mla_kv_norm_g,
                                          fb128, ctab, stab, tri_up)

    dw_small = _mm(u, dsmall, mode="tn", out_dtype=F32, name="d_w_small")
    dw_fq = _mm(u, dfq, mode="tn", out_dtype=F32, name="d_w_fq")
    dw_fk = _mm(u, dfk, mode="tn", out_dtype=F32, name="d_w_fk")
    dw_fv = _mm(u, dfv, mode="tn", out_dtype=F32, name="d_w_fv")
    dw_z = _mm(u, dgate_z, mode="tn", out_dtype=F32, name="d_w_z")
    dw_g = _mm(u, dgate_ab, mode="tn", out_dtype=F32, name="d_w_g")
    du = _mm(dsmall, w_small, mode="nt", out_dtype=F32, name="d_u_small")
    du = _mm(dfq, w_attn[:, 0:1024], mode="nt", out_dtype=F32, name="d_u_fq", acc=du)
    du = _mm(dfk, w_attn[:, 1024:2048], mode="nt", out_dtype=F32, name="d_u_fk", acc=du)
    du = _mm(dfv, w_attn[:, 2048:3072], mode="nt", out_dtype=F32, name="d_u_fv", acc=du)
    du = _mm(dgate_z, w_gate[:, 0:2048], mode="nt", out_dtype=F32, name="d_u_z", acc=du)
    du = _mm(dgate_ab, w_gate[:, 2048:4096], mode="nt", out_dtype=F32, name="d_u_g", acc=du)
    dx, dmeta, dg_pre = _pre_bwd(du, x2, meta_f, dy, pre_norm_g)

    runs = [(dw_small[:, 0:416], C_CQ), (dw_z[:, 0:1024], C_ZMLA), (dw_fq, C_FQ), (dw_fk, C_FK), (dw_fv, C_FV),
            (dw_small[:, 512:528], C_FL), (dw_z[:, 1024:2048], C_ZFOX), (dw_g, C_GA)]
    slabs = []
    for j in range(N_CHIPS):
        lo, hi = W_IN_SHARD * j, W_IN_SHARD * (j + 1)
        cols = [a[:, max(lo, c0) - c0:min(hi, c0 + a.shape[1]) - c0] for a, c0 in runs
                if max(lo, c0) < min(hi, c0 + a.shape[1])]
        slabs.append(jnp.concatenate(cols, axis=1))
    d_w_in = jnp.stack(slabs, axis=0)
    d_w_uq = _uq_restore(d_w_uq_a)
    d_w_ukv = _ukv_restore(d_w_ukv_a)
    return (loss_p, dx, dmeta, d_w_in, d_w_uq, d_w_ukv, d_w_bm, d_w_bf, d_w_out, dg_pre, dg_post, dg_q, dg_kv, dfb)


def kernel(x, meta_tokens, pre_norm_g, w_in, fox_forget_b, mla_q_norm_g, mla_kv_norm_g, w_uq, w_ukv, w_br_mla, w_br_fox, w_out, post_norm_g, loss_target, m_meta_tokens, m_pre_norm_g, m_w_in, m_fox_forget_b, m_mla_q_norm_g, m_mla_kv_norm_g, m_w_uq, m_w_ukv, m_w_br_mla, m_w_br_fox, m_w_out, m_post_norm_g, v_meta_tokens, v_pre_norm_g, v_w_in, v_fox_forget_b, v_mla_q_norm_g, v_mla_kv_norm_g, v_w_uq, v_w_ukv, v_w_br_mla, v_w_br_fox, v_w_out, v_post_norm_g):
    me = 2 * lax.axis_index("x") + lax.axis_index("y")
    core = lax.axis_index("c")
    w_in_b = w_in.astype(BF16).reshape(D_MODEL, W_IN_SHARD)
    p2 = _pack_p2(w_uq[0], w_ukv[0], w_br_mla[0], w_br_fox[0], w_out[0], jnp.zeros((N_META, 256), F32), BF16)
    w_in_g, p2_g, meta_g = _gather_weights([w_in_b, p2], meta_tokens)
    slabs = [jnp.where(me == j, w_in_b, w_in_g[j]) for j in range(N_CHIPS)]
    pieces = [_unpack_p2(jnp.where(me == j, p2, p2_g[j])) for j in range(N_CHIPS)]
    w_uq_f = jnp.concatenate([p[0] for p in pieces], axis=1)
    w_ukv_f = jnp.concatenate([p[1] for p in pieces], axis=1)
    w_bm = jnp.concatenate([p[2] for p in pieces], axis=0)
    w_bf = jnp.concatenate([p[3] for p in pieces], axis=0)
    w_o = jnp.concatenate([p[4] for p in pieces], axis=0)
    meta_f = jnp.concatenate([jnp.where(me == j, meta_tokens, meta_g[j]) for j in range(N_CHIPS)], axis=1)
    kpe = _in_cols(slabs, C_KPE, C_ZMLA)
    w_small = jnp.concatenate(_in_cols(slabs, C_CQ, C_KPE) + kpe + kpe + [jnp.zeros((D_MODEL, 64), BF16)]
                              + _in_cols(slabs, C_FL, C_ZFOX) + [jnp.zeros((D_MODEL, 112), BF16)], axis=1)
    w_attn = jnp.concatenate(_in_cols(slabs, C_FQ, C_FL), axis=1)
    w_gate = jnp.concatenate(_in_cols(slabs, C_ZMLA, C_FQ) + _in_cols(slabs, C_ZFOX, C_END), axis=1)

    (loss_p, dx, dmeta, d_w_in, d_w_uq, d_w_ukv, d_w_bm, d_w_bf, d_w_out, dg_pre, dg_post, dg_q, dg_kv,
     dfb) = _local_step(x[0], loss_target[0], meta_f, w_small, w_attn, w_gate, w_uq_f, w_ukv_f, w_bm, w_bf, w_o,
                        pre_norm_g, post_norm_g, mla_q_norm_g, mla_kv_norm_g, fox_forget_b)

    g1 = d_w_in
    g2 = jnp.stack([_pack_p2(d_w_uq[:, 384 * j:384 * (j + 1)], d_w_ukv[:, 512 * j:512 * (j + 1)],
                             d_w_bm[256 * j:256 * (j + 1)], d_w_bf[256 * j:256 * (j + 1)],
                             d_w_out[256 * j:256 * (j + 1)], dmeta[:, 256 * j:256 * (j + 1)], F32)
                    for j in range(N_CHIPS)], axis=0)
    s1, s2 = _swap_halves([g1, g2])
    part1, part2 = _add_cores(g1, s1, "add_cores_w_in"), _add_cores(g2, s2, "add_cores_rest")
    landed = _scatter_chips([part1, part2])
    mine = [_add_chips(l, lax.dynamic_index_in_dim(p, me, 0, keepdims=False), nm)
            for l, p, nm in zip(landed, (part1, part2), ("add_chips_w_in", "add_chips_rest"))]
    theirs = _swap_reduced(mine)
    g_w_in, g_p2 = [jnp.concatenate([jnp.where(core == 0, a, b), jnp.where(core == 0, b, a)], axis=0)
                    for a, b in zip(mine, theirs)]
    g_w_uq, g_w_ukv, g_w_bm, g_w_bf, g_w_out, g_meta = _unpack_p2(g_p2)
    g_w_in = g_w_in[None]

    vec = jnp.concatenate([dg_pre.reshape(8, 128), dg_post.reshape(8, 128), dg_q.reshape(2, 128), dg_kv,
                           dfb, _pad_lanes(loss_p), jnp.zeros((3, 128), F32)], axis=0)
    tot = _allreduce_small(vec)
    loss = tot[20, 0]

    def small_pack(pre, post, gq_, gkv_, fb_):
        return jnp.concatenate([pre.reshape(8, 128), post.reshape(8, 128), gq_.reshape(2, 128), gkv_,
                                _pad_lanes(fb_), jnp.zeros((4, 128), F32)], axis=0)

    def small_unpack(t):
        return (t[0:8].reshape(1, 1024), t[8:16].reshape(1, 1024), t[16:18].reshape(1, 256), t[18:19],
                t[19:20, 0:HEADS])

    g_small = jnp.concatenate([tot[0:20], jnp.zeros((4, 128), F32)], axis=0)
    sm = _adamw(small_pack(pre_norm_g, post_norm_g, mla_q_norm_g, mla_kv_norm_g, fox_forget_b), g_small,
                small_pack(m_pre_norm_g, m_post_norm_g, m_mla_q_norm_g, m_mla_kv_norm_g, m_fox_forget_b),
                small_pack(v_pre_norm_g, v_post_norm_g, v_mla_q_norm_g, v_mla_kv_norm_g, v_fox_forget_b),
                "adamw_small")
    g_pre, g_post, g_q, g_kv, g_fb = small_unpack(g_small)
    (d_pre, d_post, d_q, d_kv, d_fb), (nm_pre, nm_post, nm_q, nm_kv, nm_fb), (nv_pre, nv_post, nv_q, nv_kv, nv_fb) = (
        small_unpack(t) for t in sm)

    d_meta, nm_meta, nv_meta = _adamw(meta_tokens, g_meta, m_meta_tokens, v_meta_tokens, "adamw_meta")
    d_win, nm_win, nv_win = (t.T[None] for t in _adamw(w_in[0].T, g_w_in[0].T, m_w_in[0].T, v_w_in[0].T,
                                                       "adamw_w_in"))
    d_wuq, nm_wuq, nv_wuq = _adamw(w_uq[0], g_w_uq, m_w_uq[0], v_w_uq[0], "adamw_w_uq")
    d_wukv, nm_wukv, nv_wukv = _adamw(w_ukv[0], g_w_ukv, m_w_ukv[0], v_w_ukv[0], "adamw_w_ukv")
    d_wbm, nm_wbm, nv_wbm = _adamw(w_br_mla[0], g_w_bm, m_w_br_mla[0], v_w_br_mla[0], "adamw_w_br_mla")
    d_wbf, nm_wbf, nv_wbf = _adamw(w_br_fox[0], g_w_bf, m_w_br_fox[0], v_w_br_fox[0], "adamw_w_br_fox")
    d_wo, nm_wo, nv_wo = _adamw(w_out[0], g_w_out, m_w_out[0], v_w_out[0], "adamw_w_out")

    def group(meta_, pre, win, fb_, q_, kv_, wuq, wukv, wbm, wbf, wo, post):
        return (meta_, pre, win, fb_, q_, kv_, wuq[None], wukv[None], wbm[None], wbf[None], wo[None], post)

    grads = group(g_meta, g_pre, g_w_in, g_fb, g_q, g_kv, g_w_uq, g_w_ukv, g_w_bm, g_w_bf, g_w_out, g_post)
    deltas = group(d_meta, d_pre, d_win, d_fb, d_q, d_kv, d_wuq, d_wukv, d_wbm, d_wbf, d_wo, d_post)
    new_m = group(nm_meta, nm_pre, nm_win, nm_fb, nm_q, nm_kv, nm_wuq, nm_wukv, nm_wbm, nm_wbf, nm_wo, nm_post)
    new_v = group(nv_meta, nv_pre, nv_win, nv_fb, nv_q, nv_kv, nv_wuq, nv_wukv, nv_wbm, nv_wbf, nv_wo, nv_post)
    return (loss, dx[None], *grads, *deltas, *new_m, *new_v)
```

```python
import math

import jax
import jax.numpy as jnp
import numpy as np
from jax import lax
from jax.experimental import pallas as pl
from jax.experimental.pallas import tpu as pltpu

F32 = jnp.float32
BF16 = jnp.bfloat16

D_MODEL = 1024
N_META = 16
RMS_EPS = 1e-6
HEADS = 16
PAIRS = HEADS // 2
HEAD_DIM = 64
LANES = 128
MLA_ROPE = 32
MLA_SCALE = 1.0 / math.sqrt(64 + 32)
FOX_SCALE = 1.0 / math.sqrt(64)
ROPE_THETA = 10000.0

PAD = 256
BLK = 256
QB = 512
UNROLL = 4
NEG = -1e30

C_CQ, C_CKV, C_KPE, C_ZMLA, C_FQ, C_FK, C_FV, C_FL, C_ZFOX, C_GA, C_GB, C_END = (
    0, 256, 384, 416, 1440, 2464, 3488, 4512, 4528, 5552, 6576, 7600)
SMALL_W = 640
W_IN_SHARD = 1900

P2_ROWS = 960
N_CHIPS = 4

ADAM_LR = 0.001
ADAM_B1 = 0.9
ADAM_B2 = 0.999
ADAM_EPS = 1e-08
ADAM_WD = 0.01
ADAM_STEP = 10

VMEM_BIG = 56 * 1024 * 1024
MM_VMEM_BUDGET = 44 * 1024 * 1024
MESH = pl.DeviceIdType.MESH


def _cp(dims, vmem=None):
    return pltpu.CompilerParams(dimension_semantics=dims, vmem_limit_bytes=vmem)


def _dot(a, b, ca, cb):
    return lax.dot_general(a, b, (((ca,), (cb,)), ((), ())), preferred_element_type=F32)


def _sigmoid(x):
    return 1.0 / (1.0 + jnp.exp(-x))


def _tile(n, cands):
    for c in cands:
        if n % c == 0:
            return c
    return n


def _mm(a, b, *, mode, out_dtype, name, acc=None):
    if mode == "nn":
        (M, K), N = a.shape, b.shape[1]
    elif mode == "nt":
        (M, K), N = a.shape, b.shape[0]
    else:
        (K, M), N = a.shape, b.shape[1]
    tm = _tile(M, (1088, 1024)) if M > 1024 else M
    tn = _tile(N, (1024,)) if N > 1024 else N
    nk = 1
    while True:
        tk = K // nk
        need = 2 * tk * (tm * a.dtype.itemsize + tn * b.dtype.itemsize) + tm * tn * (
            2 * jnp.dtype(out_dtype).itemsize + (8 if acc is not None else 0) + (4 if nk > 1 else 0))
        if need <= MM_VMEM_BUDGET or (tk // 2) % (16 if mode == "tn" else LANES) or tk <= 512:
            break
        nk *= 2
    ca, cb = {"nn": (1, 0), "nt": (1, 1), "tn": (0, 0)}[mode]
    a_spec = (pl.BlockSpec((tk, tm), lambda j, i, k: (k, i)) if mode == "tn"
              else pl.BlockSpec((tm, tk), lambda j, i, k: (i, k)))
    b_spec = (pl.BlockSpec((tn, tk), lambda j, i, k: (j, k)) if mode == "nt"
              else pl.BlockSpec((tk, tn), lambda j, i, k: (k, j)))
    o_spec = pl.BlockSpec((tm, tn), lambda j, i, k: (i, j))
    has_acc = acc is not None

    def body(*refs):
        a_ref, b_ref = refs[0], refs[1]
        acc_ref = refs[2] if has_acc else None
        o_ref = refs[3] if has_acc else refs[2]
        part = _dot(a_ref[...].astype(BF16), b_ref[...].astype(BF16), ca, cb)
        if nk == 1:
            if has_acc:
                part = part + acc_ref[...]
            o_ref[...] = part.astype(out_dtype)
        else:
            sc = refs[-1]
            k = pl.program_id(2)

            @pl.when(k == 0)
            def _():
                sc[...] = part + acc_ref[...] if has_acc else part

            @pl.when(k > 0)
            def _():
                sc[...] += part

            @pl.when(k == nk - 1)
            def _():
                o_ref[...] = sc[...].astype(out_dtype)

    ins = [a, b] + ([acc] if has_acc else [])
    in_specs = [a_spec, b_spec] + ([o_spec] if has_acc else [])
    return pl.pallas_call(
        body, name=name, grid=(N // tn, M // tm, nk), in_specs=in_specs, out_specs=o_spec,
        out_shape=jax.ShapeDtypeStruct((M, N), out_dtype),
        scratch_shapes=[pltpu.VMEM((tm, tn), F32)] if nk > 1 else [],
        compiler_params=_cp(("parallel", "parallel", "arbitrary"), VMEM_BIG))(*ins)


def _row(w):
    return pl.BlockSpec((BLK, w), lambda i: (i, 0))


def _rowc(w, c):
    return pl.BlockSpec((BLK, w), lambda i: (i, c))


def _full(shape):
    return pl.BlockSpec(shape, lambda i: tuple(0 for _ in shape))


def _rope(x, c, s):
    lane = lax.broadcasted_iota(jnp.int32, x.shape, 1)
    is_x1 = ((lane >> 4) & 1) == 0
    partner = jnp.where(is_x1, pltpu.roll(x, LANES - 16, 1), pltpu.roll(x, 16, 1))
    return x * c + partner * s


def _row_valid(i):
    rows = i * BLK + lax.broadcasted_iota(jnp.int32, (BLK, 1), 0)
    return (rows < N_META) | (rows >= PAD)


def _shift_rows(w):
    return pl.BlockSpec((BLK, w), lambda i: (jnp.maximum(i - 1, 0), 0))


def _h_block(i, x_ref, meta_ref):
    head = jnp.concatenate([meta_ref[...], jnp.zeros((BLK - N_META, D_MODEL), F32)], axis=0)
    return jnp.where(i == 0, head, x_ref[...])


def _rms_pre(x2, meta, g):
    lp = PAD + x2.shape[0]

    def body(x_ref, meta_ref, g_ref, u_ref):
        hv = _h_block(pl.program_id(0), x_ref, meta_ref)
        r = lax.rsqrt(jnp.mean(hv * hv, axis=-1, keepdims=True) + RMS_EPS)
        u_ref[...] = (hv * r * g_ref[...]).astype(BF16)

    return pl.pallas_call(
        body, name="rms_pre", grid=(lp // BLK,),
        in_specs=[_shift_rows(D_MODEL), _full((N_META, D_MODEL)), _full((1, D_MODEL))], out_specs=_row(D_MODEL),
        out_shape=jax.ShapeDtypeStruct((lp, D_MODEL), BF16),
        compiler_params=_cp(("parallel",)))(x2, meta, g)


def _split3(x):
    hi = x.astype(BF16)
    r1 = x - hi.astype(F32)
    mid = r1.astype(BF16)
    lo = (r1 - mid.astype(F32)).astype(BF16)
    return hi, mid, lo


def _small_prep(small, gq, gkv, fb, ctab, stab, tri):
    lp = small.shape[0]

    def body(sm_ref, gq_ref, gkv_ref, fb_ref, c_ref, s_ref, tri_ref, qn_ref, kvn_ref, kr_ref, ncum_ref, carry):
        i = pl.program_id(0)

        @pl.when(i == 0)
        def _():
            carry[...] = jnp.zeros_like(carry)

        cq = sm_ref[:, 0:256]
        r = lax.rsqrt(jnp.mean(cq * cq, axis=-1, keepdims=True) + RMS_EPS)
        qn_ref[...] = (cq * r * gq_ref[...]).astype(BF16)
        ckv = sm_ref[:, 256:384]
        r = lax.rsqrt(jnp.mean(ckv * ckv, axis=-1, keepdims=True) + RMS_EPS)
        kvn_ref[...] = (ckv * r * gkv_ref[...]).astype(BF16)
        kr_ref[...] = _rope(sm_ref[:, 384:512], c_ref[...], s_ref[...]).astype(BF16)
        fl = sm_ref[:, 512:640] + fb_ref[...]
        lf = jnp.minimum(fl, 0.0) - jnp.log(1.0 + jnp.exp(-jnp.abs(fl)))
        lf = jnp.where(_row_valid(i), lf, 0.0)
        hi, mid, lo = _split3(lf)
        t = tri_ref[...]
        cum = (_dot(t, hi, 1, 0) + _dot(t, mid, 1, 0)) + _dot(t, lo, 1, 0) + carry[...]
        ncum_ref[...] = -cum
        carry[...] = -ncum_ref[BLK - 1:BLK, :]

    return pl.pallas_call(
        body, name="small_prep", grid=(lp // BLK,),
        in_specs=[_row(SMALL_W), _full((1, 256)), _full((1, 128)), _full((1, 128)), _row(128), _row(128),
                  _full((BLK, BLK))],
        out_specs=[_row(256), _row(128), _row(128), _row(128)],
        out_shape=[jax.ShapeDtypeStruct((lp, 256), BF16), jax.ShapeDtypeStruct((lp, 128), BF16),
                   jax.ShapeDtypeStruct((lp, 128), BF16), jax.ShapeDtypeStruct((lp, 128), F32)],
        scratch_shapes=[pltpu.VMEM((1, 128), F32)],
        compiler_params=_cp(("arbitrary",)))(small, gq, gkv, fb, ctab, stab, tri)


def _rope_q(qraw, ctab, stab):
    lp = qraw.shape[0]

    def body(q_ref, c_ref, s_ref, o_ref):
        c, s = c_ref[...], s_ref[...]
        for p in range(PAIRS):
            lo = p * 256
            o_ref[:, lo:lo + 128] = q_ref[:, lo:lo + 128].astype(BF16)
            o_ref[:, lo + 128:lo + 256] = _rope(q_ref[:, lo + 128:lo + 256], c, s).astype(BF16)

    return pl.pallas_call(
        body, name="rope_q", grid=(lp // BLK,),
        in_specs=[_row(PAIRS * 256), _row(128), _row(128)], out_specs=_row(PAIRS * 256),
        out_shape=jax.ShapeDtypeStruct((lp, PAIRS * 256), BF16),
        compiler_params=_cp(("parallel",)))(qraw, ctab, stab)


def _gate_fwd(o_mla, o_fox, gate):
    lp = o_mla.shape[0]

    def body(om_ref, of_ref, zm_ref, zf_ref, am_ref, af_ref):
        zm = zm_ref[...].astype(F32)
        am_ref[...] = (om_ref[...] * (zm * _sigmoid(zm))).astype(BF16)
        zf = zf_ref[...].astype(F32)
        af_ref[...] = (of_ref[...] * (zf * _sigmoid(zf))).astype(BF16)

    return pl.pallas_call(
        body, name="gate_fwd", grid=(lp // BLK,),
        in_specs=[_row(D_MODEL), _row(D_MODEL), _rowc(D_MODEL, 0), _rowc(D_MODEL, 1)],
        out_specs=[_row(D_MODEL), _row(D_MODEL)],
        out_shape=[jax.ShapeDtypeStruct((lp, D_MODEL), BF16)] * 2,
        compiler_params=_cp(("parallel",)))(o_mla, o_fox, gate, gate)


def _merge_fwd(gate, y_mla, y_fox):
    lp = y_mla.shape[0]

    def body(ga_ref, gb_ref, ym_ref, yf_ref, m_ref):
        sa = _sigmoid(ga_ref[...].astype(F32))
        sb = _sigmoid(gb_ref[...].astype(F32))
        m_ref[...] = (sa * ym_ref[...] + sb * yf_ref[...]).astype(BF16)

    return pl.pallas_call(
        body, name="merge_fwd", grid=(lp // BLK,),
        in_specs=[_rowc(D_MODEL, 2), _rowc(D_MODEL, 3), _row(D_MODEL), _row(D_MODEL)],
        out_specs=_row(D_MODEL), out_shape=jax.ShapeDtypeStruct((lp, D_MODEL), BF16),
        compiler_params=_cp(("parallel",)))(gate, gate, y_mla, y_fox)


def _tail(x2, mixed, tgt, gpost):
    lp = mixed.shape[0]
    shift = _shift_rows(D_MODEL)

    def body(h_ref, mx_ref, t_ref, g_ref, dmx_ref, dy_ref, loss_ref, dg_ref):
        i = pl.program_id(0)

        @pl.when(i == 0)
        def _():
            loss_ref[...] = jnp.zeros_like(loss_ref)
            dg_ref[...] = jnp.zeros_like(dg_ref)
            dmx_ref[...] = jnp.zeros_like(dmx_ref)
            dy_ref[...] = jnp.zeros_like(dy_ref)

        @pl.when(i > 0)
        def _():
            mx = mx_ref[...]
            g = g_ref[...]
            r = lax.rsqrt(jnp.mean(mx * mx, axis=-1, keepdims=True) + RMS_EPS)
            nrm = mx * r
            e = (h_ref[...] + nrm * g) - t_ref[...]
            loss_ref[...] += jnp.sum(0.5 * jnp.sum(e * e, axis=-1, keepdims=True) * (1.0 / D_MODEL),
                                     axis=0, keepdims=True)
            dy = e * (1.0 / D_MODEL)
            dy_ref[...] = dy
            dg_ref[...] += jnp.sum(dy * nrm, axis=0, keepdims=True)
            w = dy * g
            dot = jnp.mean(w * mx, axis=-1, keepdims=True)
            dmx_ref[...] = (r * w - mx * (r * r * r * dot)).astype(BF16)

    return pl.pallas_call(
        body, name="tail", grid=(lp // BLK,),
        in_specs=[shift, _row(D_MODEL), shift, _full((1, D_MODEL))],
        out_specs=[_row(D_MODEL), _row(D_MODEL), _full((1, 1)), _full((1, D_MODEL))],
        out_shape=[jax.ShapeDtypeStruct((lp, D_MODEL), BF16), jax.ShapeDtypeStruct((lp, D_MODEL), F32),
                   jax.ShapeDtypeStruct((1, 1), F32), jax.ShapeDtypeStruct((1, D_MODEL), F32)],
        compiler_params=_cp(("arbitrary",)))(x2, mixed, tgt, gpost)


def _merge_bwd(dm, gate, y_mla, y_fox):
    lp = dm.shape[0]

    def body(dm_ref, ga_ref, gb_ref, ym_ref, yf_ref, dym_ref, dyf_ref, dg_ref):
        dm_v = dm_ref[...].astype(F32)
        sa = _sigmoid(ga_ref[...].astype(F32))
        sb = _sigmoid(gb_ref[...].astype(F32))
        dym_ref[...] = (dm_v * sa).astype(BF16)
        dyf_ref[...] = (dm_v * sb).astype(BF16)
        dg_ref[:, 0:D_MODEL] = (dm_v * ym_ref[...] * (sa * (1.0 - sa))).astype(BF16)
        dg_ref[:, D_MODEL:2 * D_MODEL] = (dm_v * yf_ref[...] * (sb * (1.0 - sb))).astype(BF16)

    return pl.pallas_call(
        body, name="merge_bwd", grid=(lp // BLK,),
        in_specs=[_row(D_MODEL), _rowc(D_MODEL, 2), _rowc(D_MODEL, 3), _row(D_MODEL), _row(D_MODEL)],
        out_specs=[_row(D_MODEL), _row(D_MODEL), _row(2 * D_MODEL)],
        out_shape=[jax.ShapeDtypeStruct((lp, D_MODEL), BF16), jax.ShapeDtypeStruct((lp, D_MODEL), BF16),
                   jax.ShapeDtypeStruct((lp, 2 * D_MODEL), BF16)],
        compiler_params=_cp(("parallel",)))(dm, gate, gate, y_mla, y_fox)


def _gate_bwd(da_mla, da_fox, o_mla, o_fox, gate):
    lp = da_mla.shape[0]

    def one(da, o, z):
        sg = _sigmoid(z)
        do = da * (z * sg)
        dz = da * o * (sg * (1.0 + z * (1.0 - sg)))
        return do.astype(BF16), dz.astype(BF16)

    def body(dam_ref, daf_ref, om_ref, of_ref, zm_ref, zf_ref, dom_ref, dof_ref, dz_ref):
        f32 = lambda r: r[...].astype(F32)
        dom_ref[...], dz_ref[:, 0:D_MODEL] = one(f32(dam_ref), f32(om_ref), f32(zm_ref))
        dof_ref[...], dz_ref[:, D_MODEL:2 * D_MODEL] = one(f32(daf_ref), f32(of_ref), f32(zf_ref))

    return pl.pallas_call(
        body, name="gate_bwd", grid=(lp // BLK,),
        in_specs=[_row(D_MODEL)] * 4 + [_rowc(D_MODEL, 0), _rowc(D_MODEL, 1)],
        out_specs=[_row(D_MODEL), _row(D_MODEL), _row(2 * D_MODEL)],
        out_shape=[jax.ShapeDtypeStruct((lp, D_MODEL), BF16), jax.ShapeDtypeStruct((lp, D_MODEL), BF16),
                   jax.ShapeDtypeStruct((lp, 2 * D_MODEL), BF16)],
        compiler_params=_cp(("parallel",)))(da_mla, da_fox, o_mla, o_fox, gate, gate)


def _small_bwd(small, dqn, dkvn, dkr, dcol_t, drow_t, gq, gkv, fb, ctab, stab, triu):
    lp = small.shape[0]
    nb = lp // BLK

    def rrow(w):
        return pl.BlockSpec((BLK, w), lambda i: (nb - 1 - i, 0))

    def body(sm_ref, dqn_ref, dkvn_ref, dkr_ref, dcol_ref, drow_ref, gq_ref, gkv_ref, fb_ref, c_ref, s_ref, tri_ref,
             ds_ref, dgq_ref, dgkv_ref, dfb_ref, carry):
        i = pl.program_id(0)

        @pl.when(i == 0)
        def _():
            carry[...] = jnp.zeros_like(carry)
            dgq_ref[...] = jnp.zeros_like(dgq_ref)
            dgkv_ref[...] = jnp.zeros_like(dgkv_ref)
            dfb_ref[...] = jnp.zeros_like(dfb_ref)

        def norm_bwd(x, dn, g, dg_ref):
            r = lax.rsqrt(jnp.mean(x * x, axis=-1, keepdims=True) + RMS_EPS)
            dg_ref[...] += jnp.sum(dn * (x * r), axis=0, keepdims=True)
            w = dn * g
            dot = jnp.mean(w * x, axis=-1, keepdims=True)
            return r * w - x * (r * r * r * dot)

        ds_ref[:, 0:256] = norm_bwd(sm_ref[:, 0:256], dqn_ref[...], gq_ref[...], dgq_ref).astype(BF16)
        ds_ref[:, 256:384] = norm_bwd(sm_ref[:, 256:384], dkvn_ref[...], gkv_ref[...], dgkv_ref).astype(BF16)

        dk = dkr_ref[0]
        for p in range(1, PAIRS):
            dk = dk + dkr_ref[p]
        dk = _rope(dk, c_ref[...], -s_ref[...])
        lane = lax.broadcasted_iota(jnp.int32, dk.shape, 1)
        dk = jnp.where(lane < MLA_ROPE, dk + pltpu.roll(dk, LANES - MLA_ROPE, 1), 0.0)
        ds_ref[:, 384:512] = dk.astype(BF16)

        dcol = dcol_ref[0]
        for p in range(1, PAIRS):
            dcol = dcol + pltpu.roll(dcol_ref[p], 2 * p, 1)
        rows16 = jnp.concatenate([drow_ref[p, h:h + 1, :] for p in range(PAIRS) for h in range(2)], axis=0)
        eye = (lax.broadcasted_iota(jnp.int32, (HEADS, LANES), 0)
               == lax.broadcasted_iota(jnp.int32, (HEADS, LANES), 1)).astype(BF16)
        drow = sum(_dot(part, eye, 0, 0) for part in _split3(rows16))
        dcr = dcol - drow
        hi, mid, lo = _split3(dcr)
        t = tri_ref[...]
        suf = (_dot(t, hi, 1, 0) + _dot(t, mid, 1, 0)) + _dot(t, lo, 1, 0) + carry[...]
        fl = sm_ref[:, 512:640] + fb_ref[...]
        dfl = jnp.where(_row_valid(nb - 1 - i), -suf * _sigmoid(-fl), 0.0)
        ds_ref[:, 512:640] = dfl.astype(BF16)
        dfb_ref[...] += jnp.sum(dfl, axis=0, keepdims=True)
        carry[...] += jnp.sum(dcr, axis=0, keepdims=True)

    return pl.pallas_call(
        body, name="small_bwd", grid=(nb,),
        in_specs=[rrow(SMALL_W), rrow(256), rrow(128),
                  pl.BlockSpec((PAIRS, BLK, 128), lambda i: (0, nb - 1 - i, 0)),
                  pl.BlockSpec((PAIRS, BLK, 128), lambda i: (0, nb - 1 - i, 0)),
                  pl.BlockSpec((PAIRS, 2, BLK), lambda i: (0, 0, nb - 1 - i)),
                  _full((1, 256)), _full((1, 128)), _full((1, 128)), rrow(128), rrow(128), _full((BLK, BLK))],
        out_specs=[rrow(SMALL_W), _full((1, 256)), _full((1, 128)), _full((1, 128))],
        out_shape=[jax.ShapeDtypeStruct((lp, SMALL_W), BF16), jax.ShapeDtypeStruct((1, 256), F32),
                   jax.ShapeDtypeStruct((1, 128), F32), jax.ShapeDtypeStruct((1, 128), F32)],
        scratch_shapes=[pltpu.VMEM((1, 128), F32)],
        compiler_params=_cp(("arbitrary",)))(small, dqn, dkvn, dkr, dcol_t, drow_t, gq, gkv, fb, ctab, stab, triu)


def _pre_bwd(du, x2, meta, dy, gpre):
    s_rows = x2.shape[0]
    lp = PAD + s_rows
    shift = _shift_rows(D_MODEL)

    def body(du_ref, x_ref, meta_ref, dy_ref, g_ref, dx_ref, dmeta_ref, dg_ref):
        i = pl.program_id(0)

        @pl.when(i == 0)
        def _():
            dg_ref[...] = jnp.zeros_like(dg_ref)

        hv = _h_block(i, x_ref, meta_ref)
        duv = du_ref[...]
        r = lax.rsqrt(jnp.mean(hv * hv, axis=-1, keepdims=True) + RMS_EPS)
        dg_ref[...] += jnp.sum(duv * (hv * r), axis=0, keepdims=True)
        w = duv * g_ref[...]
        dot = jnp.mean(w * hv, axis=-1, keepdims=True)
        dh = dy_ref[...] + (r * w - hv * (r * r * r * dot))
        dx_ref[...] = dh

        @pl.when(i == 0)
        def _():
            dmeta_ref[...] = dh[0:N_META, :]

    return pl.pallas_call(
        body, name="pre_bwd", grid=(lp // BLK,),
        in_specs=[_row(D_MODEL), shift, _full((N_META, D_MODEL)), _row(D_MODEL), _full((1, D_MODEL))],
        out_specs=[shift, _full((N_META, D_MODEL)), _full((1, D_MODEL))],
        out_shape=[jax.ShapeDtypeStruct((s_rows, D_MODEL), F32), jax.ShapeDtypeStruct((N_META, D_MODEL), F32),
                   jax.ShapeDtypeStruct((1, D_MODEL), F32)],
        compiler_params=_cp(("arbitrary",)))(du, x2, meta, dy, gpre)


def _pair_masks(rope):
    lane = lax.broadcasted_iota(jnp.int32, (1, LANES), 1)
    mas = [lane < HEAD_DIM, lane >= HEAD_DIM]
    if not rope:
        return mas, mas
    wide = lax.broadcasted_iota(jnp.int32, (1, 2 * LANES), 1)
    rope_lo = LANES + MLA_ROPE
    return mas, [(wide < HEAD_DIM) | ((wide >= LANES) & (wide < rope_lo)),
                 ((wide >= HEAD_DIM) & (wide < LANES)) | ((wide >= rope_lo) & (wide < rope_lo + MLA_ROPE))]


def _mask2(x, masks):
    return [jnp.where(m, x, jnp.zeros_like(x)) for m in masks]


def _attn_fwd(q, k, v, *, kr=None, nbrep=None, scale, qcol, kcol, vcol, name):
    lp = q.shape[0]
    nq = 1 + (lp - PAD) // QB
    rope = kr is not None
    bias = nbrep is not None
    qw = 256 if rope else 128
    exp2_c = scale * math.log2(math.e)

    def body(*refs):
        it = iter(refs)
        q_ref, k_ref, v_ref = next(it), next(it), next(it)
        kr_ref = next(it) if rope else None
        nb_ref = next(it) if bias else None
        o_ref, lse_ref = next(it), next(it)
        i = pl.program_id(1)
        r0 = pl.multiple_of(jnp.where(i == 0, 0, PAD + QB * (i - 1)), BLK)
        b0 = r0 // BLK
        mas, hmask = _pair_masks(rope)
        qh = _mask2(q_ref[pl.ds(r0, QB), :], hmask)
        if bias:
            qh = [x * scale for x in qh]

        def causal(kc, n):
            key = kc * BLK + lax.broadcasted_iota(jnp.int32, (n, QB), 0)
            return (key <= r0 + lax.broadcasted_iota(jnp.int32, (n, QB), 1)) & ((kc > 0) | (n == N_META))

        def update(kcs, carry, masks, n=BLK):
            stats, acc = carry[:4], carry[4]
            k0s = [pl.multiple_of(kc * BLK, BLK) for kc in kcs]
            kks = [k_ref[pl.ds(k0, n), :] for k0 in k0s]
            if rope:
                kks = [jnp.concatenate([kk, kr_ref[pl.ds(k0, n), :]], axis=1) for kk, k0 in zip(kks, k0s)]
            new_ms, alphas, ps = [], [], [[] for _ in kcs]
            for h in range(2):
                m_prev = stats[2 * h]
                ss = []
                for kk, k0, mask in zip(kks, k0s, masks):
                    s = _dot(kk, qh[h], 1, 1)
                    if bias:
                        nbc = nb_ref[h, pl.ds(k0, n), :]
                        s = s + jnp.concatenate([nbc] * (QB // LANES), axis=1)
                    if mask is not None:
                        s = jnp.where(mask, s, NEG)
                    ss.append(s)
                m_new = m_prev
                for s in ss:
                    m_new = jnp.maximum(m_new, jnp.max(s, axis=0, keepdims=True))
                alphas.append(jnp.exp2((m_prev - m_new) * exp2_c) if rope else jnp.exp(m_prev - m_new))
                for j, s in enumerate(ss):
                    p = jnp.exp2((s - m_new) * exp2_c) if rope else jnp.exp(s - m_new)
                    ps[j].append(p.astype(BF16))
                new_ms.append(m_new)
            pcat = jnp.concatenate([p for pj in ps for p in pj], axis=0)
            vcat = jnp.concatenate([x for k0 in k0s for x in _mask2(v_ref[pl.ds(k0, n), :], mas)], axis=0)
            pv = _dot(vcat, pcat, 0, 0)
            col = lax.broadcasted_iota(jnp.int32, (8, pcat.shape[0]), 1)
            sel = ((col // n) % 2 == lax.broadcasted_iota(jnp.int32, (8, pcat.shape[0]), 0)).astype(BF16)
            sums = _dot(sel, pcat, 1, 0)
            a_full = jnp.concatenate([jnp.broadcast_to(a, (HEAD_DIM, QB)) for a in alphas], axis=0)
            return (new_ms[0], alphas[0] * stats[1] + sums[0:1], new_ms[1], alphas[1] * stats[3] + sums[1:2],
                    a_full * acc + pv)

        neg = jnp.full((1, QB), NEG, F32)
        zero = jnp.zeros((1, QB), F32)
        c = update([0], (neg, zero, neg, zero, jnp.zeros((LANES, QB), F32)), [causal(0, N_META)], N_META)
        n_mid = jnp.maximum(b0 - 1, 0)
        c = lax.fori_loop(0, n_mid // 4, lambda t, cr: update([4 * t + u for u in (1, 2, 3, 4)], cr, [None] * 4), c)
        c = lax.fori_loop(0, (n_mid % 4) // 2, lambda t, cr: update([n_mid - 1, n_mid], cr, [None, None]), c)
        c = update([b0, b0 + 1], c, [causal(b0, BLK), causal(b0 + 1, BLK)])
        inv =jnp.concatenate([jnp.broadcast_to(1.0 / c[1], (HEAD_DIM, QB)),
                               jnp.broadcast_to(1.0 / c[3], (HEAD_DIM, QB))], axis=0)
        o_t = (c[4] * inv).T.astype(BF16)
        lses = [(c[2 * h] * scale if rope else c[2 * h]) + jnp.log(c[2 * h + 1]) for h in range(2)]
        o_ref[pl.ds(r0, BLK), :] = o_t[0:BLK]
        for h in range(2):
            lse_ref[0, h:h + 1, pl.ds(r0, BLK)] = lses[h][:, 0:BLK]

        @pl.when(i > 0)
        def _():
            r1 = pl.multiple_of(r0 + BLK, BLK)
            o_ref[pl.ds(r1, QB - BLK), :] = o_t[BLK:QB]
            for h in range(2):
                lse_ref[0, h:h + 1, pl.ds(r1, QB - BLK)] = lses[h][:, BLK:QB]

    in_specs = [pl.BlockSpec((lp, qw), lambda p, i: (0, qcol + p)),
                pl.BlockSpec((lp, 128), lambda p, i: (0, kcol(p))),
                pl.BlockSpec((lp, 128), lambda p, i: (0, vcol(p)))]
    ins = [q, k, v]
    if rope:
        in_specs.append(pl.BlockSpec((lp, 128), lambda p, i: (0, 0)))
        ins.append(kr)
    if bias:
        in_specs.append(pl.BlockSpec((2, lp, 128), lambda p, i: (p, 0, 0)))
        ins.append(nbrep)
    return pl.pallas_call(
        body, name=name, grid=(PAIRS, nq), in_specs=in_specs,
        out_specs=[pl.BlockSpec((lp, 128), lambda p, i: (0, p)),
                   pl.BlockSpec((1, 2, lp), lambda p, i: (p, 0, 0))],
        out_shape=[jax.ShapeDtypeStruct((lp, D_MODEL), BF16), jax.ShapeDtypeStruct((PAIRS, 2, lp), F32)],
        compiler_params=_cp(("parallel", "arbitrary"), VMEM_BIG))(*ins)


def _attn_bwd(q, k, v, do, o, lse, *, kr=None, rtabs=None, nbrep=None, scale, qcol, kcol, vcol, name):
    lp = q.shape[0]
    nb = lp // BLK
    rope = kr is not None
    bias = nbrep is not None
    qw = 256 if rope else 128

    def body(*refs):
        it = iter(refs)
        q_ref, k_ref, v_ref = next(it), next(it), next(it)
        kr_ref = next(it) if rope else None
        nb_ref = next(it) if bias else None
        do_ref, o_ref, lse_ref = next(it), next(it), next(it)
        ct_ref, st_ref = (next(it), next(it)) if rope else (None, None)
        dq_out, dk_ref, dv_ref = next(it), next(it), next(it)
        x_ref = next(it)
        drow_ref = next(it) if bias else None
        delta, dq_ref = next(it), next(it)
        kb = pl.program_id(1)
        mas, hmask = _pair_masks(rope)
        lane = lax.broadcasted_iota(jnp.int32, (1, LANES), 1)

        @pl.when(kb == 0)
        def _():
            dq_ref[...] = jnp.zeros_like(dq_ref)
            if bias:
                drow_ref[...] = jnp.zeros_like(drow_ref)
            sub = lax.broadcasted_iota(jnp.int32, (8, LANES), 0)
            sel = (((sub == 0) & mas[0]) | ((sub == 1) & mas[1])).astype(BF16)

            def dstep(c, carry):
                r0 = pl.multiple_of(c * BLK, BLK)
                prod = do_ref[pl.ds(r0, BLK), :].astype(F32) * o_ref[pl.ds(r0, BLK), :]
                hi, mid, lo = _split3(prod)
                delta[:, pl.ds(r0, BLK)] = (_dot(sel, hi, 1, 1) + _dot(sel, mid, 1, 1)) + _dot(sel, lo, 1, 1)
                return carry

            lax.fori_loop(0, nb, dstep, 0)

        def masked_q(q0):
            qh = _mask2(q_ref[pl.ds(q0, BLK), :], hmask)
            return [x * scale for x in qh] if bias else qh

        def key_pass(n):
            kk = k_ref[0:n, :]
            if rope:
                kk = jnp.concatenate([kk, kr_ref[0:n, :]], axis=1)
            vh = _mask2(v_ref[0:n, :], mas)
            kcat = jnp.concatenate(_mask2(kk, hmask), axis=0)
            if bias:
                kcat = kcat * scale
                nbc = [jnp.concatenate([nb_ref[h, 0:n, :], nb_ref[h, 0:n, :]], axis=1) for h in range(2)]
            diag_mask = (lax.broadcasted_iota(jnp.int32, (n, BLK), 0) <= lax.broadcasted_iota(jnp.int32, (n, BLK), 1))

            def chunk(qc, carry, mask):
                carry = list(carry)
                q0 = pl.multiple_of(qc * BLK, BLK)
                dov = do_ref[pl.ds(q0, BLK), :]
                doh = _mask2(dov, mas)
                qh = masked_q(q0)
                pbs, dss = [], []
                for h in range(2):
                    s = _dot(kk, qh[h], 1, 1)
                    if rope:
                        s = s * scale
                    if bias:
                        s = s + nbc[h]
                    p = jnp.exp(s - lse_ref[0, h:h + 1, pl.ds(q0, BLK)])
                    if mask is not None:
                        p = jnp.where(mask, p, 0.0)
                    ds = p * (_dot(vh[h], dov, 1, 1) - delta[h:h + 1, pl.ds(q0, BLK)])
                    if bias:
                        drow_ref[0, h:h + 1, pl.ds(q0, BLK)] += jnp.sum(ds, axis=0, keepdims=True)
                        carry[2 + h] = carry[2 + h] + jnp.sum(ds, axis=1, keepdims=True)
                    else:
                        ds = ds * scale
                    pbs.append(p.astype(BF16))
                    dss.append(ds.astype(BF16))
                ds_lanes = jnp.concatenate(dss, axis=1)
                ds_rows = jnp.concatenate(dss, axis=0)
                carry[0] = carry[0] + _dot(ds_lanes, jnp.concatenate(qh, axis=0), 1, 0)
                carry[1] = carry[1] + _dot(jnp.concatenate(pbs, axis=1), jnp.concatenate(doh, axis=0), 1, 0)
                dq_ref[pl.ds(q0, BLK), :] += _dot(ds_rows, kcat, 0, 0)
                return tuple(carry)

            init = [jnp.zeros((n, qw), F32), jnp.zeros((n, LANES), F32)]
            if bias:
                init += [jnp.zeros((n, 1), F32), jnp.zeros((n, 1), F32)]
            groups = (nb - kb) // UNROLL

            def several(t, cr):
                for u in range(UNROLL):
                    cr = chunk(kb + UNROLL * t + u, cr, (diag_mask | (t > 0)) if u == 0 else None)
                return cr

            c = lax.fori_loop(0, groups, several, tuple(init))
            c = lax.fori_loop(kb + UNROLL * groups, nb, lambda qc, cr: chunk(qc, cr, diag_mask | (qc > kb)), c)

            def rows(a, dtype):
                a = a.astype(dtype)
                return a if n == BLK else jnp.concatenate([a, jnp.zeros((BLK - n, a.shape[1]), dtype)], axis=0)

            dk_ref[...] = rows(c[0][:, 0:LANES], BF16)
            dv_ref[...] = rows(c[1], BF16)
            if rope:
                x_ref[0] = rows(c[0][:, LANES:2 * LANES], F32)
            if bias:
                x_ref[0] = rows(jnp.where(lane == 0, c[2], jnp.where(lane == 1, c[3], 0.0)), F32)

        @pl.when(kb == 0)
        def _():
            key_pass(N_META)

        @pl.when(kb > 0)
        def _():
            key_pass(BLK)

        @pl.when(kb == nb - 1)
        def _():
            def fin(c, carry):
                r0 = pl.multiple_of(c * BLK, BLK)
                dq = dq_ref[pl.ds(r0, BLK), :]
                if rope:
                    back = _rope(dq[:, LANES:2 * LANES], ct_ref[pl.ds(r0, BLK), :], -st_ref[pl.ds(r0, BLK), :])
                    dq = jnp.concatenate([dq[:, 0:LANES], back], axis=1)
                dq_out[pl.ds(r0, BLK), :] = dq.astype(BF16)
                return carry

            lax.fori_loop(0, nb, fin, 0)

    in_specs = [pl.BlockSpec((lp, qw), lambda p, j: (0, qcol + p)),
                pl.BlockSpec((BLK, 128), lambda p, j: (j, kcol(p))),
                pl.BlockSpec((BLK, 128), lambda p, j: (j, vcol(p)))]
    ins = [q, k, v]
    if rope:
        in_specs.append(pl.BlockSpec((BLK, 128), lambda p, j: (j, 0)))
        ins.append(kr)
    if bias:
        in_specs.append(pl.BlockSpec((2, BLK, 128), lambda p, j: (p, j, 0)))
        ins.append(nbrep)
    in_specs += [pl.BlockSpec((lp, 128), lambda p, j: (0, p)), pl.BlockSpec((lp, 128), lambda p, j: (0, p)),
                 pl.BlockSpec((1, 2, lp), lambda p, j: (p, 0, 0))]
    ins += [do, o, lse]
    if rope:
        in_specs += [pl.BlockSpec((lp, 128), lambda p, j: (0, 0))] * 2
        ins += list(rtabs)
    out_specs = [pl.BlockSpec((lp, qw), lambda p, j: (0, p)),
                 pl.BlockSpec((BLK, 128), lambda p, j: (j, p)),
                 pl.BlockSpec((BLK, 128), lambda p, j: (j, p)),
                 pl.BlockSpec((1, BLK, 128), lambda p, j: (p, j, 0))]
    out_shape = [jax.ShapeDtypeStruct((lp, PAIRS * qw), BF16), jax.ShapeDtypeStruct((lp, D_MODEL), BF16),
                 jax.ShapeDtypeStruct((lp, D_MODEL), BF16), jax.ShapeDtypeStruct((PAIRS, lp, 128), F32)]
    if bias:
        out_specs.append(pl.BlockSpec((1, 2, lp), lambda p, j: (p, 0, 0)))
        out_shape.append(jax.ShapeDtypeStruct((PAIRS, 2, lp), F32))
    return pl.pallas_call(
        body, name=name, grid=(PAIRS, nb), in_specs=in_specs, out_specs=out_specs, out_shape=out_shape,
        scratch_shapes=[pltpu.VMEM((8, lp), F32), pltpu.VMEM((lp, qw), F32)],
        compiler_params=_cp(("parallel", "arbitrary"), VMEM_BIG))(*ins)


def _adamw(w, g, m, v, name):
    lead = w.ndim - 2
    rows, cols = w.shape[lead:]
    big = rows * cols > 512 * 1024
    tr = 128 if big and rows % 128 == 0 else rows
    tc = 256 if big and tr == rows else cols

    def body(w_ref, g_ref, m_ref, v_ref, d_ref, nm_ref, nv_ref):
        gv = g_ref[...]
        nm = ADAM_B1 * m_ref[...] + (1.0 - ADAM_B1) * gv
        nv = ADAM_B2 * v_ref[...] + (1.0 - ADAM_B2) * (gv * gv)
        m_hat = nm / (1.0 - ADAM_B1 ** ADAM_STEP)
        v_hat = nv / (1.0 - ADAM_B2 ** ADAM_STEP)
        d_ref[...] = -ADAM_LR * (m_hat / (jnp.sqrt(v_hat) + ADAM_EPS) + ADAM_WD * w_ref[...])
        nm_ref[...] = nm
        nv_ref[...] = nv

    spec = pl.BlockSpec((1,) * lead + (tr, tc), lambda i, j: (0,) * lead + (i, j))
    return pl.pallas_call(
        body, name=name, grid=(rows // tr, cols // tc), in_specs=[spec] * 4, out_specs=[spec] * 3,
        out_shape=[jax.ShapeDtypeStruct(w.shape, F32)] * 3,
        compiler_params=_cp(("parallel", "parallel"), VMEM_BIG))(w, g, m, v)


def _add_cores(g, from_sib, name):
    n, rows, cols = g.shape
    half = rows // 2
    tr = _tile(half, (256, 240))
    nt = half // tr

    def body(lo_ref, hi_ref, s_ref, o_ref):
        mine = jnp.where(lax.axis_index("c") == 0, lo_ref[0], hi_ref[0])
        o_ref[0] = (mine + s_ref[0]).astype(BF16)

    return pl.pallas_call(
        body, name=name, grid=(n, nt),
        in_specs=[pl.BlockSpec((1, tr, cols), lambda j, i: (j, i, 0)),
                  pl.BlockSpec((1, tr, cols), lambda j, i: (j, nt + i, 0)),
                  pl.BlockSpec((1, tr, cols), lambda j, i: (j, i, 0))],
        out_specs=pl.BlockSpec((1, tr, cols), lambda j, i: (j, i, 0)),
        out_shape=jax.ShapeDtypeStruct((n, half, cols), BF16),
        compiler_params=_cp(("parallel", "parallel"), VMEM_BIG))(g, g, from_sib)


def _add_chips(x, own, name):
    n, rows, cols = x.shape
    tr = _tile(rows, (256, 240))

    def body(x_ref, own_ref, o_ref):
        me = 2 * lax.axis_index("x") + lax.axis_index("y")
        v = [jnp.where(me == k, own_ref[...], x_ref[k]).astype(F32) for k in range(N_CHIPS)]
        o_ref[...] = ((v[0] + v[1]) + v[2]) + v[3]

    return pl.pallas_call(
        body, name=name, grid=(rows // tr,),
        in_specs=[pl.BlockSpec((n, tr, cols), lambda i: (0, i, 0)), pl.BlockSpec((tr, cols), lambda i: (i, 0))],
        out_specs=pl.BlockSpec((tr, cols), lambda i: (i, 0)),
        out_shape=jax.ShapeDtypeStruct((rows, cols), F32), compiler_params=_cp(("parallel",), VMEM_BIG))(x, own)


def _axes():
    return lax.axis_index("x"), lax.axis_index("y"), lax.axis_index("c")


def _other_chips(x, y):
    return [(1 - x, y), (x, 1 - y), (1 - x, 1 - y)]


ANY = pl.BlockSpec(memory_space=pl.ANY)


def _rcopy(src, dst, send_sems, recv_sems, k, to):
    return pltpu.make_async_remote_copy(src_ref=src, dst_ref=dst, send_sem=send_sems.at[k], recv_sem=recv_sems.at[k],
                                        device_id=to, device_id_type=MESH)


def _gather_weights(shards, meta):
    n = len(shards)

    def body(*refs):
        srcs, meta_ref = refs[:n], refs[n]
        outs, mout_ref = refs[n + 1:2 * n + 1], refs[2 * n + 1]
        send_sems, recv_sems = refs[2 * n + 2:]
        x, y, c = _axes()
        me = 2 * x + y
        sib = (x, y, 1 - c)
        chips = _other_chips(x, y)

        def half(t, chip_idx, cc):
            hr = shards[t].shape[0] // 2
            return outs[t].at[chip_idx, pl.ds(cc * hr, hr), :]

        first = []
        for j, (px, py) in enumerate(chips):
            for t in range(n):
                hr = shards[t].shape[0] // 2
                first.append(_rcopy(srcs[t].at[pl.ds(c * hr, hr), :], half(t, me, c), send_sems, recv_sems,
                                    3 * t + j, (px, py, c)))
            first.append(_rcopy(meta_ref, mout_ref.at[me], send_sems, recv_sems, 3 * n + j, (px, py, c)))
        for cp in first:
            cp.start()
        passed = []
        for j, (px, py) in enumerate(chips):
            src_chip = 2 * px + py
            for t in range(n):
                _rcopy(half(t, src_chip, c), half(t, src_chip, c), send_sems, recv_sems, 3 * t + j, sib).wait_recv()
                fwd = _rcopy(half(t, src_chip, c), half(t, src_chip, c), send_sems, recv_sems, 3 * (n + 1 + t) + j, sib)
                fwd.start()
                passed.append(fwd)
            _rcopy(mout_ref.at[src_chip], mout_ref.at[src_chip], send_sems, recv_sems, 3 * n + j, sib).wait_recv()
        for j, (px, py) in enumerate(chips):
            src_chip = 2 * px + py
            for t in range(n):
                _rcopy(half(t, src_chip, 1 - c), half(t, src_chip, 1 - c), send_sems, recv_sems,
                       3 * (n + 1 + t) + j, sib).wait_recv()
        for cp in first + passed:
            cp.wait_send()

    nsem = 3 * (2 * n + 1)
    return pl.pallas_call(
        body, name="gather_weights", in_specs=[ANY] * (n + 1), out_specs=[ANY] * (n + 1),
        out_shape=[jax.ShapeDtypeStruct((N_CHIPS,) + s.shape, s.dtype) for s in shards]
        + [jax.ShapeDtypeStruct((N_CHIPS,) + meta.shape, meta.dtype)],
        scratch_shapes=[pltpu.SemaphoreType.DMA((nsem,)), pltpu.SemaphoreType.DMA((nsem,))])(*shards, meta)


def _swap_halves(gs):
    n = len(gs)

    def body(*refs):
        srcs, outs = refs[:n], refs[n:2 * n]
        send_sems, recv_sems = refs[2 * n:]
        x, y, c = _axes()
        cps = []
        for t in range(n):
            hr = gs[t].shape[1] // 2
            for j in range(N_CHIPS):
                cps.append(_rcopy(srcs[t].at[j, pl.ds((1 - c) * hr, hr), :], outs[t].at[j], send_sems, recv_sems,
                                  N_CHIPS * t + j, (x, y, 1 - c)))
        for cp in cps:
            cp.start()
        for cp in cps:
            cp.wait()

    return pl.pallas_call(
        body, name="swap_halves", in_specs=[ANY] * n, out_specs=[ANY] * n,
        out_shape=[jax.ShapeDtypeStruct((N_CHIPS, g.shape[1] // 2, g.shape[2]), g.dtype) for g in gs],
        scratch_shapes=[pltpu.SemaphoreType.DMA((N_CHIPS * n,)), pltpu.SemaphoreType.DMA((N_CHIPS * n,))])(*gs)


def _scatter_chips(parts):
    n = len(parts)

    def body(*refs):
        srcs, outs = refs[:n], refs[n:2 * n]
        send_sems, recv_sems = refs[2 * n:]
        x, y, c = _axes()
        me = 2 * x + y
        cps = []
        for j, (px, py) in enumerate(_other_chips(x, y)):
            for t in range(n):
                cps.append(_rcopy(srcs[t].at[2 * px + py], outs[t].at[me], send_sems, recv_sems, 3 * t + j,
                                  (px, py, c)))
        for cp in cps:
            cp.start()
        for cp in cps:
            cp.wait()

    return pl.pallas_call(
        body, name="scatter_chips", in_specs=[ANY] * n, out_specs=[ANY] * n,
        out_shape=[jax.ShapeDtypeStruct(p.shape, p.dtype) for p in parts],
        scratch_shapes=[pltpu.SemaphoreType.DMA((3 * n,)), pltpu.SemaphoreType.DMA((3 * n,))])(*parts)


def _swap_reduced(rs):
    n = len(rs)

    def body(*refs):
        srcs, outs = refs[:n], refs[n:2 * n]
        send_sems, recv_sems = refs[2 * n:]
        x, y, c = _axes()
        cps = [_rcopy(srcs[t], outs[t], send_sems, recv_sems, t, (x, y, 1 - c)) for t in range(n)]
        for cp in cps:
            cp.start()
        for cp in cps:
            cp.wait()

    return pl.pallas_call(
        body, name="swap_reduced", in_specs=[ANY] * n, out_specs=[ANY] * n,
        out_shape=[jax.ShapeDtypeStruct(r.shape, r.dtype) for r in rs],
        scratch_shapes=[pltpu.SemaphoreType.DMA((n,)), pltpu.SemaphoreType.DMA((n,))])(*rs)


SMALL_ROWS = 24


def _allreduce_small(vec):
    def body(v_ref, out_ref, slots, send_sems, recv_sems):
        x, y, c = _axes()
        me = 4 * x + 2 * y + c
        slots[me] = v_ref[...]
        cps = []
        for k in range(1, 8):
            kx, ky, kc = (k >> 2) & 1, (k >> 1) & 1, k & 1
            peer = (1 - x if kx else x, 1 - y if ky else y, 1 - c if kc else c)
            cps.append(_rcopy(v_ref, slots.at[me], send_sems, recv_sems, k - 1, peer))
        for cp in cps:
            cp.start()
        for cp in cps:
            cp.wait()
        tot = slots[0]
        for k in range(1, 8):
            tot = tot + slots[k]
        out_ref[...] = tot

    return pl.pallas_call(
        body, name="allreduce_small",
        in_specs=[pl.BlockSpec(memory_space=pltpu.VMEM)], out_specs=pl.BlockSpec(memory_space=pltpu.VMEM),
        out_shape=jax.ShapeDtypeStruct((SMALL_ROWS, 128), F32),
        scratch_shapes=[pltpu.VMEM((8, SMALL_ROWS, 128), F32), pltpu.SemaphoreType.DMA((7,)),
                        pltpu.SemaphoreType.DMA((7,))])(vec)


def _pack_p2(w_uq, w_ukv, w_br_mla, w_br_fox, w_out, meta, dtype):
    parts = [w_uq.reshape(96, D_MODEL), w_ukv.reshape(64, D_MODEL), w_br_mla, w_br_fox, w_out,
             meta.reshape(4, D_MODEL), jnp.zeros((P2_ROWS - 932, D_MODEL), meta.dtype)]
    return jnp.concatenate([p.astype(dtype) for p in parts], axis=0)


def _unpack_p2(pk):
    return (pk[0:96].reshape(256, 384), pk[96:160].reshape(128, 512), pk[160:416], pk[416:672], pk[672:928],
            pk[928:932].reshape(N_META, 256))


def _uq_arrange(w):
    w3 = w.reshape(256, HEADS, 96)
    nope = w3[:, :, :64].reshape(256, PAIRS, 128)
    pe = w3[:, :, 64:].reshape(256, PAIRS, 64)
    return jnp.concatenate([nope, pe, jnp.zeros((256, PAIRS, 64), w.dtype)], axis=2).reshape(256, PAIRS * 256)


def _uq_restore(g):
    g3 = g.reshape(256, PAIRS, 256)
    nope = g3[:, :, :128].reshape(256, HEADS, 64)
    pe = g3[:, :, 128:192].reshape(256, HEADS, 32)
    return jnp.concatenate([nope, pe], axis=2).reshape(256, HEADS * 96)


def _ukv_arrange(w):
    w3 = w.reshape(128, HEADS, 128)
    return jnp.concatenate([w3[:, :, :64].reshape(128, 1024), w3[:, :, 64:].reshape(128, 1024)], axis=1)


def _ukv_restore(g):
    kn = g[:, :1024].reshape(128, HEADS, 64)
    vv = g[:, 1024:].reshape(128, HEADS, 64)
    return jnp.concatenate([kn, vv], axis=2).reshape(128, HEADS * 128)


def _rope_tables(lp):
    r = np.arange(lp)
    pos = np.where(r < N_META, r, np.where(r >= PAD, r - PAD + N_META, 0)).astype(np.float32)
    half = MLA_ROPE // 2
    inv_freq = np.float32(ROPE_THETA) ** (-np.arange(half, dtype=np.float32) / np.float32(half))
    ang = (pos[:, None] * inv_freq[None, :]).astype(np.float32)
    cos, sin = np.cos(ang).astype(np.float32), np.sin(ang).astype(np.float32)
    one, zero = np.ones((lp, 64), np.float32), np.zeros((lp, 64), np.float32)
    return (jnp.asarray(np.concatenate([cos, cos, cos, cos, one], axis=1)),
            jnp.asarray(np.concatenate([-sin, sin, -sin, sin, zero], axis=1)))


def _pad_lanes(v, n=128):
    return jnp.pad(v, ((0, 0), (0, n - v.shape[1])))


def _in_cols(slabs, a, b):
    out = []
    for j in range(N_CHIPS):
        lo, hi = max(a, W_IN_SHARD * j), min(b, W_IN_SHARD * (j + 1))
        if lo < hi:
            out.append(slabs[j][:, lo - W_IN_SHARD * j:hi - W_IN_SHARD * j])
    return out


def _local_step(x2, tgt2, meta_f, w_small, w_attn, w_gate, w_uq_f, w_ukv_f, w_bm, w_bf, w_o, pre_norm_g,
                post_norm_g, mla_q_norm_g, mla_kv_norm_g, fox_forget_b):
    s_rows = x2.shape[0]
    lp = PAD + s_rows
    w_uq_a = _uq_arrange(w_uq_f)
    w_ukv_a = _ukv_arrange(w_ukv_f)

    ctab, stab = _rope_tables(lp)
    ii = jnp.arange(BLK)
    tri_lo = (ii[:, None] >= ii[None, :]).astype(BF16)
    tri_up = (ii[:, None] <= ii[None, :]).astype(BF16)
    fb128 = _pad_lanes(fox_forget_b)

    u = _rms_pre(x2, meta_f, pre_norm_g)
    small = _mm(u, w_small, mode="nn", out_dtype=F32, name="proj_small")
    attn = _mm(u, w_attn, mode="nn", out_dtype=BF16, name="proj_attn")
    gate = _mm(u, w_gate, mode="nn", out_dtype=BF16, name="proj_gate")
    qn, kvn, kr, ncum = _small_prep(small, mla_q_norm_g, mla_kv_norm_g, fb128, ctab, stab, tri_lo)
    qraw = _mm(qn, w_uq_a, mode="nn", out_dtype=F32, name="mla_q")
    qcat = _rope_q(qraw, ctab, stab)
    kv = _mm(kvn, w_ukv_a, mode="nn", out_dtype=BF16, name="mla_kv")
    nbrep = jnp.broadcast_to(ncum[:, :HEADS].T[:, :, None], (HEADS, lp, LANES))

    mla_cols = dict(qcol=0, kcol=lambda p: p, vcol=lambda p: PAIRS + p)
    fox_cols = dict(qcol=0, kcol=lambda p: PAIRS + p, vcol=lambda p: 2 * PAIRS + p)
    o_mla, lse_mla = _attn_fwd(qcat, kv, kv, kr=kr, scale=MLA_SCALE, name="mla_fwd", **mla_cols)
    o_fox, lse_fox = _attn_fwd(attn, attn, attn, nbrep=nbrep, scale=FOX_SCALE, name="fox_fwd", **fox_cols)

    a_mla, a_fox = _gate_fwd(o_mla, o_fox, gate)
    y_mla = _mm(a_mla, w_bm, mode="nn", out_dtype=BF16, name="br_mla")
    y_fox = _mm(a_fox, w_bf, mode="nn", out_dtype=BF16, name="br_fox")
    mg = _merge_fwd(gate, y_mla, y_fox)
    mixed = _mm(mg, w_o, mode="nn", out_dtype=F32, name="out_proj")
    dmixed, dy, loss_p, dg_post = _tail(x2, mixed, tgt2, post_norm_g)

    d_w_out = _mm(mg, dmixed, mode="tn", out_dtype=F32, name="d_w_out")
    dm = _mm(dmixed, w_o, mode="nt", out_dtype=BF16, name="d_merge")
    dy_mla, dy_fox, dgate_ab = _merge_bwd(dm, gate, y_mla, y_fox)
    d_w_bm = _mm(a_mla, dy_mla, mode="tn", out_dtype=F32, name="d_w_br_mla")
    d_w_bf = _mm(a_fox, dy_fox, mode="tn", out_dtype=F32, name="d_w_br_fox")
    da_mla = _mm(dy_mla, w_bm, mode="nt", out_dtype=BF16, name="d_a_mla")
    da_fox = _mm(dy_fox, w_bf, mode="nt", out_dtype=BF16, name="d_a_fox")
    do_mla, do_fox, dgate_z = _gate_bwd(da_mla, da_fox, o_mla, o_fox, gate)

    dq_a, dkn, dvm, dkr = _attn_bwd(qcat, kv, kv, do_mla, o_mla, lse_mla, kr=kr, rtabs=(ctab, stab), scale=MLA_SCALE,
                                    name="mla_bwd", **mla_cols)
    dfq, dfk, dfv, dcol, drow = _attn_bwd(attn, attn, attn, do_fox, o_fox, lse_fox, nbrep=nbrep, scale=FOX_SCALE,
                                          name="fox_bwd", **fox_cols)

    d_w_uq_a = _mm(qn, dq_a, mode="tn", out_dtype=F32, name="d_w_uq")
    dqn = _mm(dq_a, w_uq_a, mode="nt", out_dtype=F32, name="d_qn")
    d_w_ukv_a = jnp.concatenate([_mm(kvn, dkn, mode="tn", out_dtype=F32, name="d_w_uk"),
                                 _mm(kvn, dvm, mode="tn", out_dtype=F32, name="d_w_uv")], axis=1)
    dkvn = _mm(dkn, w_ukv_a[:, :1024], mode="nt", out_dtype=F32, name="d_kvn_k")
    dkvn = _mm(dvm, w_ukv_a[:, 1024:], mode="nt", out_dtype=F32, name="d_kvn_v", acc=dkvn)
    dsmall, dg_q, dg_kv, dfb = _small_bwd(small, dqn, dkvn, dkr, dcol, drow, mla_q_norm_g, mla_kv_norm_g,
                                          fb128, ctab, stab, tri_up)

    dw_small = _mm(u, dsmall, mode="tn", out_dtype=F32, name="d_w_small")
    dw_fq = _mm(u, dfq, mode="tn", out_dtype=F32, name="d_w_fq")
    dw_fk = _mm(u, dfk, mode="tn", out_dtype=F32, name="d_w_fk")
    dw_fv = _mm(u, dfv, mode="tn", out_dtype=F32, name="d_w_fv")
    dw_z = _mm(u, dgate_z, mode="tn", out_dtype=F32, name="d_w_z")
    dw_g = _mm(u, dgate_ab, mode="tn", out_dtype=F32, name="d_w_g")
    du = _mm(dsmall, w_small, mode="nt", out_dtype=F32, name="d_u_small")
    du = _mm(dfq, w_attn[:, 0:1024], mode="nt", out_dtype=F32, name="d_u_fq", acc=du)
    du = _mm(dfk, w_attn[:, 1024:2048], mode="nt", out_dtype=F32, name="d_u_fk", acc=du)
    du = _mm(dfv, w_attn[:, 2048:3072], mode="nt", out_dtype=F32, name="d_u_fv", acc=du)
    du = _mm(dgate_z, w_gate[:, 0:2048], mode="nt", out_dtype=F32, name="d_u_z", acc=du)
    du = _mm(dgate_ab, w_gate[:, 2048:4096], mode="nt", out_dtype=F32, name="d_u_g", acc=du)
    dx, dmeta, dg_pre = _pre_bwd(du, x2, meta_f, dy, pre_norm_g)

    runs = [(dw_small[:, 0:416], C_CQ), (dw_z[:, 0:1024], C_ZMLA), (dw_fq, C_FQ), (dw_fk, C_FK), (dw_fv, C_FV),
            (dw_small[:, 512:528], C_FL), (dw_z[:, 1024:2048], C_ZFOX), (dw_g, C_GA)]
    slabs = []
    for j in range(N_CHIPS):
        lo, hi = W_IN_SHARD * j, W_IN_SHARD * (j + 1)
        cols = [a[:, max(lo, c0) - c0:min(hi, c0 + a.shape[1]) - c0] for a, c0 in runs
                if max(lo, c0) < min(hi, c0 + a.shape[1])]
        slabs.append(jnp.concatenate(cols, axis=1))
    d_w_in = jnp.stack(slabs, axis=0)
    d_w_uq = _uq_restore(d_w_uq_a)
    d_w_ukv = _ukv_restore(d_w_ukv_a)
    return (loss_p, dx, dmeta, d_w_in, d_w_uq, d_w_ukv, d_w_bm, d_w_bf, d_w_out, dg_pre, dg_post, dg_q, dg_kv, dfb)


def kernel(x, meta_tokens, pre_norm_g, w_in, fox_forget_b, mla_q_norm_g, mla_kv_norm_g, w_uq, w_ukv, w_br_mla, w_br_fox, w_out, post_norm_g, loss_target, m_meta_tokens, m_pre_norm_g, m_w_in, m_fox_forget_b, m_mla_q_norm_g, m_mla_kv_norm_g, m_w_uq, m_w_ukv, m_w_br_mla, m_w_br_fox, m_w_out, m_post_norm_g, v_meta_tokens, v_pre_norm_g, v_w_in, v_fox_forget_b, v_mla_q_norm_g, v_mla_kv_norm_g, v_w_uq, v_w_ukv, v_w_br_mla, v_w_br_fox, v_w_out, v_post_norm_g):
    me = 2 * lax.axis_index("x") + lax.axis_index("y")
    core = lax.axis_index("c")
    w_in_b = w_in.astype(BF16).reshape(D_MODEL, W_IN_SHARD)
    p2 = _pack_p2(w_uq[0], w_ukv[0], w_br_mla[0], w_br_fox[0], w_out[0], jnp.zeros((N_META, 256), F32), BF16)
    w_in_g, p2_g, meta_g = _gather_weights([w_in_b, p2], meta_tokens)
    slabs = [jnp.where(me == j, w_in_b, w_in_g[j]) for j in range(N_CHIPS)]
    pieces = [_unpack_p2(jnp.where(me == j, p2, p2_g[j])) for j in range(N_CHIPS)]
    w_uq_f = jnp.concatenate([p[0] for p in pieces], axis=1)
    w_ukv_f = jnp.concatenate([p[1] for p in pieces], axis=1)
    w_bm = jnp.concatenate([p[2] for p in pieces], axis=0)
    w_bf = jnp.concatenate([p[3] for p in pieces], axis=0)
    w_o = jnp.concatenate([p[4] for p in pieces], axis=0)
    meta_f = jnp.concatenate([jnp.where(me == j, meta_tokens, meta_g[j]) for j in range(N_CHIPS)], axis=1)
    kpe = _in_cols(slabs, C_KPE, C_ZMLA)
    w_small = jnp.concatenate(_in_cols(slabs, C_CQ, C_KPE) + kpe + kpe + [jnp.zeros((D_MODEL, 64), BF16)]
                              + _in_cols(slabs, C_FL, C_ZFOX) + [jnp.zeros((D_MODEL, 112), BF16)], axis=1)
    w_attn = jnp.concatenate(_in_cols(slabs, C_FQ, C_FL), axis=1)
    w_gate = jnp.concatenate(_in_cols(slabs, C_ZMLA, C_FQ) + _in_cols(slabs, C_ZFOX, C_END), axis=1)

    (loss_p, dx, dmeta, d_w_in, d_w_uq, d_w_ukv, d_w_bm, d_w_bf, d_w_out, dg_pre, dg_post, dg_q, dg_kv,
     dfb) = _local_step(x[0], loss_target[0], meta_f, w_small, w_attn, w_gate, w_uq_f, w_ukv_f, w_bm, w_bf, w_o,
                        pre_norm_g, post_norm_g, mla_q_norm_g, mla_kv_norm_g, fox_forget_b)

    g1 = d_w_in
    g2 = jnp.stack([_pack_p2(d_w_uq[:, 384 * j:384 * (j + 1)], d_w_ukv[:, 512 * j:512 * (j + 1)],
                             d_w_bm[256 * j:256 * (j + 1)], d_w_bf[256 * j:256 * (j + 1)],
                             d_w_out[256 * j:256 * (j + 1)], dmeta[:, 256 * j:256 * (j + 1)], F32)
                    for j in range(N_CHIPS)], axis=0)
    s1, s2 = _swap_halves([g1, g2])
    part1, part2 = _add_cores(g1, s1, "add_cores_w_in"), _add_cores(g2, s2, "add_cores_rest")
    landed = _scatter_chips([part1, part2])
    mine = [_add_chips(l, lax.dynamic_index_in_dim(p, me, 0, keepdims=False), nm)
            for l, p, nm in zip(landed, (part1, part2), ("add_chips_w_in", "add_chips_rest"))]
    theirs = _swap_reduced(mine)
    g_w_in, g_p2 = [jnp.concatenate([jnp.where(core == 0, a, b), jnp.where(core == 0, b, a)], axis=0)
                    for a, b in zip(mine, theirs)]
    g_w_uq, g_w_ukv, g_w_bm, g_w_bf, g_w_out, g_meta = _unpack_p2(g_p2)
    g_w_in = g_w_in[None]

    vec = jnp.concatenate([dg_pre.reshape(8, 128), dg_post.reshape(8, 128), dg_q.reshape(2, 128), dg_kv,
                           dfb, _pad_lanes(loss_p), jnp.zeros((3, 128), F32)], axis=0)
    tot = _allreduce_small(vec)
    loss = tot[20, 0]

    def small_pack(pre, post, gq_, gkv_, fb_):
        return jnp.concatenate([pre.reshape(8, 128), post.reshape(8, 128), gq_.reshape(2, 128), gkv_,
                                _pad_lanes(fb_), jnp.zeros((4, 128), F32)], axis=0)

    def small_unpack(t):
        return (t[0:8].reshape(1, 1024), t[8:16].reshape(1, 1024), t[16:18].reshape(1, 256), t[18:19],
                t[19:20, 0:HEADS])

    g_small = jnp.concatenate([tot[0:20], jnp.zeros((4, 128), F32)], axis=0)
    sm = _adamw(small_pack(pre_norm_g, post_norm_g, mla_q_norm_g, mla_kv_norm_g, fox_forget_b), g_small,
                small_pack(m_pre_norm_g, m_post_norm_g, m_mla_q_norm_g, m_mla_kv_norm_g, m_fox_forget_b),
                small_pack(v_pre_norm_g, v_post_norm_g, v_mla_q_norm_g, v_mla_kv_norm_g, v_fox_forget_b),
                "adamw_small")
    g_pre, g_post, g_q, g_kv, g_fb = small_unpack(g_small)
    (d_pre, d_post, d_q, d_kv, d_fb), (nm_pre, nm_post, nm_q, nm_kv, nm_fb), (nv_pre, nv_post, nv_q, nv_kv, nv_fb) = (
        small_unpack(t) for t in sm)

    d_meta, nm_meta, nv_meta = _adamw(meta_tokens, g_meta, m_meta_tokens, v_meta_tokens, "adamw_meta")
    d_win, nm_win, nv_win = (t.T[None] for t in _adamw(w_in[0].T, g_w_in[0].T, m_w_in[0].T, v_w_in[0].T,
                                                       "adamw_w_in"))
    d_wuq, nm_wuq, nv_wuq = _adamw(w_uq[0], g_w_uq, m_w_uq[0], v_w_uq[0], "adamw_w_uq")
    d_wukv, nm_wukv, nv_wukv = _adamw(w_ukv[0], g_w_ukv, m_w_ukv[0], v_w_ukv[0], "adamw_w_ukv")
    d_wbm, nm_wbm, nv_wbm = _adamw(w_br_mla[0], g_w_bm, m_w_br_mla[0], v_w_br_mla[0], "adamw_w_br_mla")
    d_wbf, nm_wbf, nv_wbf = _adamw(w_br_fox[0], g_w_bf, m_w_br_fox[0], v_w_br_fox[0], "adamw_w_br_fox")
    d_wo, nm_wo, nv_wo = _adamw(w_out[0], g_w_out, m_w_out[0], v_w_out[0], "adamw_w_out")

    def group(meta_, pre, win, fb_, q_, kv_, wuq, wukv, wbm, wbf, wo, post):
        return (meta_, pre, win, fb_, q_, kv_, wuq[None], wukv[None], wbm[None], wbf[None], wo[None], post)

    grads = group(g_meta, g_pre, g_w_in, g_fb, g_q, g_kv, g_w_uq, g_w_ukv, g_w_bm, g_w_bf, g_w_out, g_post)
    deltas = group(d_meta, d_pre, d_win, d_fb, d_q, d_kv, d_wuq, d_wukv, d_wbm, d_wbf, d_wo, d_post)
    new_m = group(nm_meta, nm_pre, nm_win, nm_fb, nm_q, nm_kv, nm_wuq, nm_wukv, nm_wbm, nm_wbf, nm_wo, nm_post)
    new_v = group(nv_meta, nv_pre, nv_win, nv_fb, nv_q, nv_kv, nv_wuq, nv_wukv, nv_wbm, nv_wbf, nv_wo, nv_post)
    return (loss, dx[None], *grads, *deltas, *new_m, *new_v)
```

```python
import math

import jax
import jax.numpy as jnp
import numpy as np
from jax import lax
from jax.experimental import pallas as pl
from jax.experimental.pallas import tpu as pltpu

F32 = jnp.float32
BF16 = jnp.bfloat16

D_MODEL = 1024
N_META = 16
RMS_EPS = 1e-6
HEADS = 16
PAIRS = HEADS // 2
HEAD_DIM = 64
LANES = 128
MLA_ROPE = 32
MLA_SCALE = 1.0 / math.sqrt(64 + 32)
FOX_SCALE = 1.0 / math.sqrt(64)
ROPE_THETA = 10000.0

PAD = 256
BLK = 256
QB = 512
UNROLL = 4
NEG = -1e30

C_CQ, C_CKV, C_KPE, C_ZMLA, C_FQ, C_FK, C_FV, C_FL, C_ZFOX, C_GA, C_GB, C_END = (
    0, 256, 384, 416, 1440, 2464, 3488, 4512, 4528, 5552, 6576, 7600)
SMALL_W = 640
W_IN_SHARD = 1900

P2_ROWS = 960
N_CHIPS = 4

ADAM_LR = 0.001
ADAM_B1 = 0.9
ADAM_B2 = 0.999
ADAM_EPS = 1e-08
ADAM_WD = 0.01
ADAM_STEP = 10

VMEM_BIG = 56 * 1024 * 1024
MM_VMEM_BUDGET = 44 * 1024 * 1024
MESH = pl.DeviceIdType.MESH


def _cp(dims, vmem=None):
    return pltpu.CompilerParams(dimension_semantics=dims, vmem_limit_bytes=vmem)


def _dot(a, b, ca, cb):
    return lax.dot_general(a, b, (((ca,), (cb,)), ((), ())), preferred_element_type=F32)


def _sigmoid(x):
    return 1.0 / (1.0 + jnp.exp(-x))


def _tile(n, cands):
    for c in cands:
        if n % c == 0:
            return c
    return n


def _mm(a, b, *, mode, out_dtype, name, acc=None):
    if mode == "nn":
        (M, K), N = a.shape, b.shape[1]
    elif mode == "nt":
        (M, K), N = a.shape, b.shape[0]
    else:
        (K, M), N = a.shape, b.shape[1]
    tm = _tile(M, (1088, 1024)) if M > 1024 else M
    tn = _tile(N, (1024,)) if N > 1024 else N
    nk = 1
    while True:
        tk = K // nk
        need = 2 * tk * (tm * a.dtype.itemsize + tn * b.dtype.itemsize) + tm * tn * (
            2 * jnp.dtype(out_dtype).itemsize + (8 if acc is not None else 0) + (4 if nk > 1 else 0))
        if need <= MM_VMEM_BUDGET or (tk // 2) % (16 if mode == "tn" else LANES) or tk <= 512:
            break
        nk *= 2
    ca, cb = {"nn": (1, 0), "nt": (1, 1), "tn": (0, 0)}[mode]
    a_spec = (pl.BlockSpec((tk, tm), lambda j, i, k: (k, i)) if mode == "tn"
              else pl.BlockSpec((tm, tk), lambda j, i, k: (i, k)))
    b_spec = (pl.BlockSpec((tn, tk), lambda j, i, k: (j, k)) if mode == "nt"
              else pl.BlockSpec((tk, tn), lambda j, i, k: (k, j)))
    o_spec = pl.BlockSpec((tm, tn), lambda j, i, k: (i, j))
    has_acc = acc is not None

    def body(*refs):
        a_ref, b_ref = refs[0], refs[1]
        acc_ref = refs[2] if has_acc else None
        o_ref = refs[3] if has_acc else refs[2]
        part = _dot(a_ref[...].astype(BF16), b_ref[...].astype(BF16), ca, cb)
        if nk == 1:
            if has_acc:
                part = part + acc_ref[...]
            o_ref[...] = part.astype(out_dtype)
        else:
            sc = refs[-1]
            k = pl.program_id(2)

            @pl.when(k == 0)
            def _():
                sc[...] = part + acc_ref[...] if has_acc else part

            @pl.when(k > 0)
            def _():
                sc[...] += part

            @pl.when(k == nk - 1)
            def _():
                o_ref[...] = sc[...].astype(out_dtype)

    ins = [a, b] + ([acc] if has_acc else [])
    in_specs = [a_spec, b_spec] + ([o_spec] if has_acc else [])
    return pl.pallas_call(
        body, name=name, grid=(N // tn, M // tm, nk), in_specs=in_specs, out_specs=o_spec,
        out_shape=jax.ShapeDtypeStruct((M, N), out_dtype),
        scratch_shapes=[pltpu.VMEM((tm, tn), F32)] if nk > 1 else [],
        compiler_params=_cp(("parallel", "parallel", "arbitrary"), VMEM_BIG))(*ins)


def _row(w):
    return pl.BlockSpec((BLK, w), lambda i: (i, 0))


def _rowc(w, c):
    return pl.BlockSpec((BLK, w), lambda i: (i, c))


def _full(shape):
    return pl.BlockSpec(shape, lambda i: tuple(0 for _ in shape))


def _rope(x, c, s):
    lane = lax.broadcasted_iota(jnp.int32, x.shape, 1)
    is_x1 = ((lane >> 4) & 1) == 0
    partner = jnp.where(is_x1, pltpu.roll(x, LANES - 16, 1), pltpu.roll(x, 16, 1))
    return x * c + partner * s


def _row_valid(i):
    rows = i * BLK + lax.broadcasted_iota(jnp.int32, (BLK, 1), 0)
    return (rows < N_META) | (rows >= PAD)


def _shift_rows(w):
    return pl.BlockSpec((BLK, w), lambda i: (jnp.maximum(i - 1, 0), 0))


def _h_block(i, x_ref, meta_ref):
    head = jnp.concatenate([meta_ref[...], jnp.zeros((BLK - N_META, D_MODEL), F32)], axis=0)
    return jnp.where(i == 0, head, x_ref[...])


def _rms_pre(x2, meta, g):
    lp = PAD + x2.shape[0]

    def body(x_ref, meta_ref, g_ref, u_ref):
        hv = _h_block(pl.program_id(0), x_ref, meta_ref)
        r = lax.rsqrt(jnp.mean(hv * hv, axis=-1, keepdims=True) + RMS_EPS)
        u_ref[...] = (hv * r * g_ref[...]).astype(BF16)

    return pl.pallas_call(
        body, name="rms_pre", grid=(lp // BLK,),
        in_specs=[_shift_rows(D_MODEL), _full((N_META, D_MODEL)), _full((1, D_MODEL))], out_specs=_row(D_MODEL),
        out_shape=jax.ShapeDtypeStruct((lp, D_MODEL), BF16),
        compiler_params=_cp(("parallel",)))(x2, meta, g)


def _split3(x):
    hi = x.astype(BF16)
    r1 = x - hi.astype(F32)
    mid = r1.astype(BF16)
    lo = (r1 - mid.astype(F32)).astype(BF16)
    return hi, mid, lo


def _small_prep(small, gq, gkv, fb, ctab, stab, tri):
    lp = small.shape[0]

    def body(sm_ref, gq_ref, gkv_ref, fb_ref, c_ref, s_ref, tri_ref, qn_ref, kvn_ref, kr_ref, ncum_ref, carry):
        i = pl.program_id(0)

        @pl.when(i == 0)
        def _():
            carry[...] = jnp.zeros_like(carry)

        cq = sm_ref[:, 0:256]
        r = lax.rsqrt(jnp.mean(cq * cq, axis=-1, keepdims=True) + RMS_EPS)
        qn_ref[...] = (cq * r * gq_ref[...]).astype(BF16)
        ckv = sm_ref[:, 256:384]
        r = lax.rsqrt(jnp.mean(ckv * ckv, axis=-1, keepdims=True) + RMS_EPS)
        kvn_ref[...] = (ckv * r * gkv_ref[...]).astype(BF16)
        kr_ref[...] = _rope(sm_ref[:, 384:512], c_ref[...], s_ref[...]).astype(BF16)
        fl = sm_ref[:, 512:640] + fb_ref[...]
        lf = jnp.minimum(fl, 0.0) - jnp.log(1.0 + jnp.exp(-jnp.abs(fl)))
        lf = jnp.where(_row_valid(i), lf, 0.0)
        hi, mid, lo = _split3(lf)
        t = tri_ref[...]
        cum = (_dot(t, hi, 1, 0) + _dot(t, mid, 1, 0)) + _dot(t, lo, 1, 0) + carry[...]
        ncum_ref[...] = -cum
        carry[...] = -ncum_ref[BLK - 1:BLK, :]

    return pl.pallas_call(
        body, name="small_prep", grid=(lp // BLK,),
        in_specs=[_row(SMALL_W), _full((1, 256)), _full((1, 128)), _full((1, 128)), _row(128), _row(128),
                  _full((BLK, BLK))],
        out_specs=[_row(256), _row(128), _row(128), _row(128)],
        out_shape=[jax.ShapeDtypeStruct((lp, 256), BF16), jax.ShapeDtypeStruct((lp, 128), BF16),
                   jax.ShapeDtypeStruct((lp, 128), BF16), jax.ShapeDtypeStruct((lp, 128), F32)],
        scratch_shapes=[pltpu.VMEM((1, 128), F32)],
        compiler_params=_cp(("arbitrary",)))(small, gq, gkv, fb, ctab, stab, tri)


def _rope_q(qraw, ctab, stab):
    lp = qraw.shape[0]

    def body(q_ref, c_ref, s_ref, o_ref):
        c, s = c_ref[...], s_ref[...]
        for p in range(PAIRS):
            lo = p * 256
            o_ref[:, lo:lo + 128] = q_ref[:, lo:lo + 128].astype(BF16)
            o_ref[:, lo + 128:lo + 256] = _rope(q_ref[:, lo + 128:lo + 256], c, s).astype(BF16)

    return pl.pallas_call(
        body, name="rope_q", grid=(lp // BLK,),
        in_specs=[_row(PAIRS * 256), _row(128), _row(128)], out_specs=_row(PAIRS * 256),
        out_shape=jax.ShapeDtypeStruct((lp, PAIRS * 256), BF16),
        compiler_params=_cp(("parallel",)))(qraw, ctab, stab)


def _gate_fwd(o_mla, o_fox, gate):
    lp = o_mla.shape[0]

    def body(om_ref, of_ref, zm_ref, zf_ref, am_ref, af_ref):
        zm = zm_ref[...].astype(F32)
        am_ref[...] = (om_ref[...] * (zm * _sigmoid(zm))).astype(BF16)
        zf = zf_ref[...].astype(F32)
        af_ref[...] = (of_ref[...] * (zf * _sigmoid(zf))).astype(BF16)

    return pl.pallas_call(
        body, name="gate_fwd", grid=(lp // BLK,),
        in_specs=[_row(D_MODEL), _row(D_MODEL), _rowc(D_MODEL, 0), _rowc(D_MODEL, 1)],
        out_specs=[_row(D_MODEL), _row(D_MODEL)],
        out_shape=[jax.ShapeDtypeStruct((lp, D_MODEL), BF16)] * 2,
        compiler_params=_cp(("parallel",)))(o_mla, o_fox, gate, gate)


def _merge_fwd(gate, y_mla, y_fox):
    lp = y_mla.shape[0]

    def body(ga_ref, gb_ref, ym_ref, yf_ref, m_ref):
        sa = _sigmoid(ga_ref[...].astype(F32))
        sb = _sigmoid(gb_ref[...].astype(F32))
        m_ref[...] = (sa * ym_ref[...] + sb * yf_ref[...]).astype(BF16)

    return pl.pallas_call(
        body, name="merge_fwd", grid=(lp // BLK,),
        in_specs=[_rowc(D_MODEL, 2), _rowc(D_MODEL, 3), _row(D_MODEL), _row(D_MODEL)],
        out_specs=_row(D_MODEL), out_shape=jax.ShapeDtypeStruct((lp, D_MODEL), BF16),
        compiler_params=_cp(("parallel",)))(gate, gate, y_mla, y_fox)


def _tail(x2, mixed, tgt, gpost):
    lp = mixed.shape[0]
    shift = _shift_rows(D_MODEL)

    def body(h_ref, mx_ref, t_ref, g_ref, dmx_ref, dy_ref, loss_ref, dg_ref):
        i = pl.program_id(0)

        @pl.when(i == 0)
        def _():
            loss_ref[...] = jnp.zeros_like(loss_ref)
            dg_ref[...] = jnp.zeros_like(dg_ref)
            dmx_ref[...] = jnp.zeros_like(dmx_ref)
            dy_ref[...] = jnp.zeros_like(dy_ref)

        @pl.when(i > 0)
        def _():
            mx = mx_ref[...]
            g = g_ref[...]
            r = lax.rsqrt(jnp.mean(mx * mx, axis=-1, keepdims=True) + RMS_EPS)
            nrm = mx * r
            e = (h_ref[...] + nrm * g) - t_ref[...]
            loss_ref[...] += jnp.sum(0.5 * jnp.sum(e * e, axis=-1, keepdims=True) * (1.0 / D_MODEL),
                                     axis=0, keepdims=True)
            dy = e * (1.0 / D_MODEL)
            dy_ref[...] = dy
            dg_ref[...] += jnp.sum(dy * nrm, axis=0, keepdims=True)
            w = dy * g
            dot = jnp.mean(w * mx, axis=-1, keepdims=True)
            dmx_ref[...] = (r * w - mx * (r * r * r * dot)).astype(BF16)

    return pl.pallas_call(
        body, name="tail", grid=(lp // BLK,),
        in_specs=[shift, _row(D_MODEL), shift, _full((1, D_MODEL))],
        out_specs=[_row(D_MODEL), _row(D_MODEL), _full((1, 1)), _full((1, D_MODEL))],
        out_shape=[jax.ShapeDtypeStruct((lp, D_MODEL), BF16), jax.ShapeDtypeStruct((lp, D_MODEL), F32),
                   jax.ShapeDtypeStruct((1, 1), F32), jax.ShapeDtypeStruct((1, D_MODEL), F32)],
        compiler_params=_cp(("arbitrary",)))(x2, mixed, tgt, gpost)


def _merge_bwd(dm, gate, y_mla, y_fox):
    lp = dm.shape[0]

    def body(dm_ref, ga_ref, gb_ref, ym_ref, yf_ref, dym_ref, dyf_ref, dg_ref):
        dm_v = dm_ref[...].astype(F32)
        sa = _sigmoid(ga_ref[...].astype(F32))
        sb = _sigmoid(gb_ref[...].astype(F32))
        dym_ref[...] = (dm_v * sa).astype(BF16)
        dyf_ref[...] = (dm_v * sb).astype(BF16)
        dg_ref[:, 0:D_MODEL] = (dm_v * ym_ref[...] * (sa * (1.0 - sa))).astype(BF16)
        dg_ref[:, D_MODEL:2 * D_MODEL] = (dm_v * yf_ref[...] * (sb * (1.0 - sb))).astype(BF16)

    return pl.pallas_call(
        body, name="merge_bwd", grid=(lp // BLK,),
        in_specs=[_row(D_MODEL), _rowc(D_MODEL, 2), _rowc(D_MODEL, 3), _row(D_MODEL), _row(D_MODEL)],
        out_specs=[_row(D_MODEL), _row(D_MODEL), _row(2 * D_MODEL)],
        out_shape=[jax.ShapeDtypeStruct((lp, D_MODEL), BF16), jax.ShapeDtypeStruct((lp, D_MODEL), BF16),
                   jax.ShapeDtypeStruct((lp, 2 * D_MODEL), BF16)],
        compiler_params=_cp(("parallel",)))(dm, gate, gate, y_mla, y_fox)


def _gate_bwd(da_mla, da_fox, o_mla, o_fox, gate):
    lp = da_mla.shape[0]

    def one(da, o, z):
        sg = _sigmoid(z)
        do = da * (z * sg)
        dz = da * o * (sg * (1.0 + z * (1.0 - sg)))
        return do.astype(BF16), dz.astype(BF16)

    def body(dam_ref, daf_ref, om_ref, of_ref, zm_ref, zf_ref, dom_ref, dof_ref, dz_ref):
        f32 = lambda r: r[...].astype(F32)
        dom_ref[...], dz_ref[:, 0:D_MODEL] = one(f32(dam_ref), f32(om_ref), f32(zm_ref))
        dof_ref[...], dz_ref[:, D_MODEL:2 * D_MODEL] = one(f32(daf_ref), f32(of_ref), f32(zf_ref))

    return pl.pallas_call(
        body, name="gate_bwd", grid=(lp // BLK,),
        in_specs=[_row(D_MODEL)] * 4 + [_rowc(D_MODEL, 0), _rowc(D_MODEL, 1)],
        out_specs=[_row(D_MODEL), _row(D_MODEL), _row(2 * D_MODEL)],
        out_shape=[jax.ShapeDtypeStruct((lp, D_MODEL), BF16), jax.ShapeDtypeStruct((lp, D_MODEL), BF16),
                   jax.ShapeDtypeStruct((lp, 2 * D_MODEL), BF16)],
        compiler_params=_cp(("parallel",)))(da_mla, da_fox, o_mla, o_fox, gate, gate)


def _small_bwd(small, dqn, dkvn, dkr, dcol_t, drow_t, gq, gkv, fb, ctab, stab, triu):
    lp = small.shape[0]
    nb = lp // BLK

    def rrow(w):
        return pl.BlockSpec((BLK, w), lambda i: (nb - 1 - i, 0))

    def body(sm_ref, dqn_ref, dkvn_ref, dkr_ref, dcol_ref, drow_ref, gq_ref, gkv_ref, fb_ref, c_ref, s_ref, tri_ref,
             ds_ref, dgq_ref, dgkv_ref, dfb_ref, carry):
        i = pl.program_id(0)

        @pl.when(i == 0)
        def _():
            carry[...] = jnp.zeros_like(carry)
            dgq_ref[...] = jnp.zeros_like(dgq_ref)
            dgkv_ref[...] = jnp.zeros_like(dgkv_ref)
            dfb_ref[...] = jnp.zeros_like(dfb_ref)

        def norm_bwd(x, dn, g, dg_ref):
            r = lax.rsqrt(jnp.mean(x * x, axis=-1, keepdims=True) + RMS_EPS)
            dg_ref[...] += jnp.sum(dn * (x * r), axis=0, keepdims=True)
            w = dn * g
            dot = jnp.mean(w * x, axis=-1, keepdims=True)
            return r * w - x * (r * r * r * dot)

        ds_ref[:, 0:256] = norm_bwd(sm_ref[:, 0:256], dqn_ref[...], gq_ref[...], dgq_ref).astype(BF16)
        ds_ref[:, 256:384] = norm_bwd(sm_ref[:, 256:384], dkvn_ref[...], gkv_ref[...], dgkv_ref).astype(BF16)

        dk = dkr_ref[0]
        for p in range(1, PAIRS):
            dk = dk + dkr_ref[p]
        dk = _rope(dk, c_ref[...], -s_ref[...])
        lane = lax.broadcasted_iota(jnp.int32, dk.shape, 1)
        dk = jnp.where(lane < MLA_ROPE, dk + pltpu.roll(dk, LANES - MLA_ROPE, 1), 0.0)
        ds_ref[:, 384:512] = dk.astype(BF16)

        dcol = dcol_ref[0]
        for p in range(1, PAIRS):
            dcol = dcol + pltpu.roll(dcol_ref[p], 2 * p, 1)
        rows16 = jnp.concatenate([drow_ref[p, h:h + 1, :] for p in range(PAIRS) for h in range(2)], axis=0)
        eye = (lax.broadcasted_iota(jnp.int32, (HEADS, LANES), 0)
               == lax.broadcasted_iota(jnp.int32, (HEADS, LANES), 1)).astype(BF16)
        drow = sum(_dot(part, eye, 0, 0) for part in _split3(rows16))
        dcr = dcol - drow
        hi, mid, lo = _split3(dcr)
        t = tri_ref[...]
        suf = (_dot(t, hi, 1, 0) + _dot(t, mid, 1, 0)) + _dot(t, lo, 1, 0) + carry[...]
        fl = sm_ref[:, 512:640] + fb_ref[...]
        dfl = jnp.where(_row_valid(nb - 1 - i), -suf * _sigmoid(-fl), 0.0)
        ds_ref[:, 512:640] = dfl.astype(BF16)
        dfb_ref[...] += jnp.sum(dfl, axis=0, keepdims=True)
        carry[...] += jnp.sum(dcr, axis=0, keepdims=True)

    return pl.pallas_call(
        body, name="small_bwd", grid=(nb,),
        in_specs=[rrow(SMALL_W), rrow(256), rrow(128),
                  pl.BlockSpec((PAIRS, BLK, 128), lambda i: (0, nb - 1 - i, 0)),
                  pl.BlockSpec((PAIRS, BLK, 128), lambda i: (0, nb - 1 - i, 0)),
                  pl.BlockSpec((PAIRS, 2, BLK), lambda i: (0, 0, nb - 1 - i)),
                  _full((1, 256)), _full((1, 128)), _full((1, 128)), rrow(128), rrow(128), _full((BLK, BLK))],
        out_specs=[rrow(SMALL_W), _full((1, 256)), _full((1, 128)), _full((1, 128))],
        out_shape=[jax.ShapeDtypeStruct((lp, SMALL_W), BF16), jax.ShapeDtypeStruct((1, 256), F32),
                   jax.ShapeDtypeStruct((1, 128), F32), jax.ShapeDtypeStruct((1, 128), F32)],
        scratch_shapes=[pltpu.VMEM((1, 128), F32)],
        compiler_params=_cp(("arbitrary",)))(small, dqn, dkvn, dkr, dcol_t, drow_t, gq, gkv, fb, ctab, stab, triu)


def _pre_bwd(du, x2, meta, dy, gpre):
    s_rows = x2.shape[0]
    lp = PAD + s_rows
    shift = _shift_rows(D_MODEL)

    def body(du_ref, x_ref, meta_ref, dy_ref, g_ref, dx_ref, dmeta_ref, dg_ref):
        i = pl.program_id(0)

        @pl.when(i == 0)
        def _():
            dg_ref[...] = jnp.zeros_like(dg_ref)

        hv = _h_block(i, x_ref, meta_ref)
        duv = du_ref[...]
        r = lax.rsqrt(jnp.mean(hv * hv, axis=-1, keepdims=True) + RMS_EPS)
        dg_ref[...] += jnp.sum(duv * (hv * r), axis=0, keepdims=True)
        w = duv * g_ref[...]
        dot = jnp.mean(w * hv, axis=-1, keepdims=True)
        dh = dy_ref[...] + (r * w - hv * (r * r * r * dot))
        dx_ref[...] = dh

        @pl.when(i == 0)
        def _():
            dmeta_ref[...] = dh[0:N_META, :]

    return pl.pallas_call(
        body, name="pre_bwd", grid=(lp // BLK,),
        in_specs=[_row(D_MODEL), shift, _full((N_META, D_MODEL)), _row(D_MODEL), _full((1, D_MODEL))],
        out_specs=[shift, _full((N_META, D_MODEL)), _full((1, D_MODEL))],
        out_shape=[jax.ShapeDtypeStruct((s_rows, D_MODEL), F32), jax.ShapeDtypeStruct((N_META, D_MODEL), F32),
                   jax.ShapeDtypeStruct((1, D_MODEL), F32)],
        compiler_params=_cp(("arbitrary",)))(du, x2, meta, dy, gpre)


def _pair_masks(rope):
    lane = lax.broadcasted_iota(jnp.int32, (1, LANES), 1)
    mas = [lane < HEAD_DIM, lane >= HEAD_DIM]
    if not rope:
        return mas, mas
    wide = lax.broadcasted_iota(jnp.int32, (1, 2 * LANES), 1)
    rope_lo = LANES + MLA_ROPE
    return mas, [(wide < HEAD_DIM) | ((wide >= LANES) & (wide < rope_lo)),
                 ((wide >= HEAD_DIM) & (wide < LANES)) | ((wide >= rope_lo) & (wide < rope_lo + MLA_ROPE))]


def _mask2(x, masks):
    return [jnp.where(m, x, jnp.zeros_like(x)) for m in masks]


def _attn_fwd(q, k, v, *, kr=None, nbrep=None, scale, qcol, kcol, vcol, name):
    lp = q.shape[0]
    nq = 1 + (lp - PAD) // QB
    rope = kr is not None
    bias = nbrep is not None
    qw = 256 if rope else 128

    def body(*refs):
        it = iter(refs)
        q_ref, k_ref, v_ref = next(it), next(it), next(it)
        kr_ref = next(it) if rope else None
        nb_ref = next(it) if bias else None
        o_ref, lse_ref = next(it), next(it)
        i = pl.program_id(1)
        r0 = pl.multiple_of(jnp.where(i == 0, 0, PAD + QB * (i - 1)), BLK)
        b0 = r0 // BLK
        mas, hmask = _pair_masks(rope)
        qh = _mask2(q_ref[pl.ds(r0, QB), :], hmask)
        if bias:
            qh = [x * scale for x in qh]

        def causal(kc, n):
            key = kc * BLK + lax.broadcasted_iota(jnp.int32, (n, QB), 0)
            return (key <= r0 + lax.broadcasted_iota(jnp.int32, (n, QB), 1)) & ((kc > 0) | (n == N_META))

        def update(kcs, carry, masks, n=BLK):
            stats, acc = carry[:4], carry[4]
            k0s = [pl.multiple_of(kc * BLK, BLK) for kc in kcs]
            kks = [k_ref[pl.ds(k0, n), :] for k0 in k0s]
            if rope:
                kks = [jnp.concatenate([kk, kr_ref[pl.ds(k0, n), :]], axis=1) for kk, k0 in zip(kks, k0s)]
            new_stats, alphas, ps = [], [], [[] for _ in kcs]
            for h in range(2):
                m_prev, l_prev = stats[2 * h], stats[2 * h + 1]
                ss = []
                for kk, k0, mask in zip(kks, k0s, masks):
                    s = _dot(kk, qh[h], 1, 1)
                    if rope:
                        s = s * scale
                    if bias:
                        nbc = nb_ref[h, pl.ds(k0, n), :]
                        s = s + jnp.concatenate([nbc] * (QB // LANES), axis=1)
                    if mask is not None:
                        s = jnp.where(mask, s, NEG)
                    ss.append(s)
                m_new = m_prev
                for s in ss:
                    m_new = jnp.maximum(m_new, jnp.max(s, axis=0, keepdims=True))
                alpha = jnp.exp(m_prev - m_new)
                l_new = alpha * l_prev
                for j, s in enumerate(ss):
                    p = jnp.exp(s - m_new)
                    l_new = l_new + jnp.sum(p, axis=0, keepdims=True)
                    ps[j].append(p.astype(BF16))
                new_stats += [m_new, l_new]
                alphas.append(alpha)
            vcat = jnp.concatenate([x for k0 in k0s for x in _mask2(v_ref[pl.ds(k0, n), :], mas)], axis=0)
            pv = _dot(vcat, jnp.concatenate([p for pj in ps for p in pj], axis=0), 0, 0)
            a_full = jnp.concatenate([jnp.broadcast_to(a, (HEAD_DIM, QB)) for a in alphas], axis=0)
            return (*new_stats, a_full * acc + pv)

        neg = jnp.full((1, QB), NEG, F32)
        zero = jnp.zeros((1, QB), F32)
        c = update([0], (neg, zero, neg, zero, jnp.zeros((LANES, QB), F32)), [causal(0, N_META)], N_META)
        n_mid = jnp.maximum(b0 - 1, 0)
        c = lax.fori_loop(0, n_mid // 4, lambda t, cr: update([4 * t + u for u in (1, 2, 3, 4)], cr, [None] * 4), c)
        c = lax.fori_loop(0, (n_mid % 4) // 2, lambda t, cr: update([n_mid - 1, n_mid], cr, [None, None]), c)
        c = update([b0, b0 + 1], c, [causal(b0, BLK), causal(b0 + 1, BLK)])
        inv =jnp.concatenate([jnp.broadcast_to(1.0 / c[1], (HEAD_DIM, QB)),
                               jnp.broadcast_to(1.0 / c[3], (HEAD_DIM, QB))], axis=0)
        o_t = (c[4] * inv).T.astype(BF16)
        lses = [c[0] + jnp.log(c[1]), c[2] + jnp.log(c[3])]
        o_ref[pl.ds(r0, BLK), :] = o_t[0:BLK]
        for h in range(2):
            lse_ref[0, h:h + 1, pl.ds(r0, BLK)] = lses[h][:, 0:BLK]

        @pl.when(i > 0)
        def _():
            r1 = pl.multiple_of(r0 + BLK, BLK)
            o_ref[pl.ds(r1, QB - BLK), :] = o_t[BLK:QB]
            for h in range(2):
                lse_ref[0, h:h + 1, pl.ds(r1, QB - BLK)] = lses[h][:, BLK:QB]

    in_specs = [pl.BlockSpec((lp, qw), lambda p, i: (0, qcol + p)),
                pl.BlockSpec((lp, 128), lambda p, i: (0, kcol(p))),
                pl.BlockSpec((lp, 128), lambda p, i: (0, vcol(p)))]
    ins = [q, k, v]
    if rope:
        in_specs.append(pl.BlockSpec((lp, 128), lambda p, i: (0, 0)))
        ins.append(kr)
    if bias:
        in_specs.append(pl.BlockSpec((2, lp, 128), lambda p, i: (p, 0, 0)))
        ins.append(nbrep)
    return pl.pallas_call(
        body, name=name, grid=(PAIRS, nq), in_specs=in_specs,
        out_specs=[pl.BlockSpec((lp, 128), lambda p, i: (0, p)),
                   pl.BlockSpec((1, 2, lp), lambda p, i: (p, 0, 0))],
        out_shape=[jax.ShapeDtypeStruct((lp, D_MODEL), BF16), jax.ShapeDtypeStruct((PAIRS, 2, lp), F32)],
        compiler_params=_cp(("parallel", "arbitrary"), VMEM_BIG))(*ins)


def _attn_bwd(q, k, v, do, o, lse, *, kr=None, rtabs=None, nbrep=None, scale, qcol, kcol, vcol, name):
    lp = q.shape[0]
    nb = lp // BLK
    rope = kr is not None
    bias = nbrep is not None
    qw = 256 if rope else 128

    def body(*refs):
        it = iter(refs)
        q_ref, k_ref, v_ref = next(it), next(it), next(it)
        kr_ref = next(it) if rope else None
        nb_ref = next(it) if bias else None
        do_ref, o_ref, lse_ref = next(it), next(it), next(it)
        ct_ref, st_ref = (next(it), next(it)) if rope else (None, None)
        dq_out, dk_ref, dv_ref = next(it), next(it), next(it)
        x_ref = next(it)
        drow_ref = next(it) if bias else None
        delta, dq_ref = next(it), next(it)
        kb = pl.program_id(1)
        mas, hmask = _pair_masks(rope)
        lane = lax.broadcasted_iota(jnp.int32, (1, LANES), 1)

        @pl.when(kb == 0)
        def _():
            dq_ref[...] = jnp.zeros_like(dq_ref)
            if bias:
                drow_ref[...] = jnp.zeros_like(drow_ref)
            sub = lax.broadcasted_iota(jnp.int32, (8, LANES), 0)
            sel = (((sub == 0) & mas[0]) | ((sub == 1) & mas[1])).astype(BF16)

            def dstep(c, carry):
                r0 = pl.multiple_of(c * BLK, BLK)
                prod = do_ref[pl.ds(r0, BLK), :].astype(F32) * o_ref[pl.ds(r0, BLK), :]
                hi, mid, lo = _split3(prod)
                delta[:, pl.ds(r0, BLK)] = (_dot(sel, hi, 1, 1) + _dot(sel, mid, 1, 1)) + _dot(sel, lo, 1, 1)
                return carry

            lax.fori_loop(0, nb, dstep, 0)

        def masked_q(q0):
            qh = _mask2(q_ref[pl.ds(q0, BLK), :], hmask)
            return [x * scale for x in qh] if bias else qh

        def key_pass(n):
            kk = k_ref[0:n, :]
            if rope:
                kk = jnp.concatenate([kk, kr_ref[0:n, :]], axis=1)
            vh = _mask2(v_ref[0:n, :], mas)
            kcat = jnp.concatenate(_mask2(kk, hmask), axis=0)
            if bias:
                kcat = kcat * scale
                nbc = [jnp.concatenate([nb_ref[h, 0:n, :], nb_ref[h, 0:n, :]], axis=1) for h in range(2)]
            diag_mask = (lax.broadcasted_iota(jnp.int32, (n, BLK), 0) <= lax.broadcasted_iota(jnp.int32, (n, BLK), 1))

            def chunk(qc, carry, mask):
                carry = list(carry)
                q0 = pl.multiple_of(qc * BLK, BLK)
                dov = do_ref[pl.ds(q0, BLK), :]
                doh = _mask2(dov, mas)
                qh = masked_q(q0)
                pbs, dss = [], []
                for h in range(2):
                    s = _dot(kk, qh[h], 1, 1)
                    if rope:
                        s = s * scale
                    if bias:
                        s = s + nbc[h]
                    p = jnp.exp(s - lse_ref[0, h:h + 1, pl.ds(q0, BLK)])
                    if mask is not None:
                        p = jnp.where(mask, p, 0.0)
                    ds = p * (_dot(vh[h], dov, 1, 1) - delta[h:h + 1, pl.ds(q0, BLK)])
                    if bias:
                        drow_ref[0, h:h + 1, pl.ds(q0, BLK)] += jnp.sum(ds, axis=0, keepdims=True)
                        carry[2 + h] = carry[2 + h] + jnp.sum(ds, axis=1, keepdims=True)
                    else:
                        ds = ds * scale
                    pbs.append(p.astype(BF16))
                    dss.append(ds.astype(BF16))
                ds_lanes = jnp.concatenate(dss, axis=1)
                ds_rows = jnp.concatenate(dss, axis=0)
                carry[0] = carry[0] + _dot(ds_lanes, jnp.concatenate(qh, axis=0), 1, 0)
                carry[1] = carry[1] + _dot(jnp.concatenate(pbs, axis=1), jnp.concatenate(doh, axis=0), 1, 0)
                dq_ref[pl.ds(q0, BLK), :] += _dot(ds_rows, kcat, 0, 0)
                return tuple(carry)

            init = [jnp.zeros((n, qw), F32), jnp.zeros((n, LANES), F32)]
            if bias:
                init += [jnp.zeros((n, 1), F32), jnp.zeros((n, 1), F32)]
            groups = (nb - kb) // UNROLL

            def several(t, cr):
                for u in range(UNROLL):
                    cr = chunk(kb + UNROLL * t + u, cr, (diag_mask | (t > 0)) if u == 0 else None)
                return cr

            c = lax.fori_loop(0, groups, several, tuple(init))
            c = lax.fori_loop(kb + UNROLL * groups, nb, lambda qc, cr: chunk(qc, cr, diag_mask | (qc > kb)), c)

            def rows(a, dtype):
                a = a.astype(dtype)
                return a if n == BLK else jnp.concatenate([a, jnp.zeros((BLK - n, a.shape[1]), dtype)], axis=0)

            dk_ref[...] = rows(c[0][:, 0:LANES], BF16)
            dv_ref[...] = rows(c[1], BF16)
            if rope:
                x_ref[0] = rows(c[0][:, LANES:2 * LANES], F32)
            if bias:
                x_ref[0] = rows(jnp.where(lane == 0, c[2], jnp.where(lane == 1, c[3], 0.0)), F32)

        @pl.when(kb == 0)
        def _():
            key_pass(N_META)

        @pl.when(kb > 0)
        def _():
            key_pass(BLK)

        @pl.when(kb == nb - 1)
        def _():
            def fin(c, carry):
                r0 = pl.multiple_of(c * BLK, BLK)
                dq = dq_ref[pl.ds(r0, BLK), :]
                if rope:
                    back = _rope(dq[:, LANES:2 * LANES], ct_ref[pl.ds(r0, BLK), :], -st_ref[pl.ds(r0, BLK), :])
                    dq = jnp.concatenate([dq[:, 0:LANES], back], axis=1)
                dq_out[pl.ds(r0, BLK), :] = dq.astype(BF16)
                return carry

            lax.fori_loop(0, nb, fin, 0)

    in_specs = [pl.BlockSpec((lp, qw), lambda p, j: (0, qcol + p)),
                pl.BlockSpec((BLK, 128), lambda p, j: (j, kcol(p))),
                pl.BlockSpec((BLK, 128), lambda p, j: (j, vcol(p)))]
    ins = [q, k, v]
    if rope:
        in_specs.append(pl.BlockSpec((BLK, 128), lambda p, j: (j, 0)))
        ins.append(kr)
    if bias:
        in_specs.append(pl.BlockSpec((2, BLK, 128), lambda p, j: (p, j, 0)))
        ins.append(nbrep)
    in_specs += [pl.BlockSpec((lp, 128), lambda p, j: (0, p)), pl.BlockSpec((lp, 128), lambda p, j: (0, p)),
                 pl.BlockSpec((1, 2, lp), lambda p, j: (p, 0, 0))]
    ins += [do, o, lse]
    if rope:
        in_specs += [pl.BlockSpec((lp, 128), lambda p, j: (0, 0))] * 2
        ins += list(rtabs)
    out_specs = [pl.BlockSpec((lp, qw), lambda p, j: (0, p)),
                 pl.BlockSpec((BLK, 128), lambda p, j: (j, p)),
                 pl.BlockSpec((BLK, 128), lambda p, j: (j, p)),
                 pl.BlockSpec((1, BLK, 128), lambda p, j: (p, j, 0))]
    out_shape = [jax.ShapeDtypeStruct((lp, PAIRS * qw), BF16), jax.ShapeDtypeStruct((lp, D_MODEL), BF16),
                 jax.ShapeDtypeStruct((lp, D_MODEL), BF16), jax.ShapeDtypeStruct((PAIRS, lp, 128), F32)]
    if bias:
        out_specs.append(pl.BlockSpec((1, 2, lp), lambda p, j: (p, 0, 0)))
        out_shape.append(jax.ShapeDtypeStruct((PAIRS, 2, lp), F32))
    return pl.pallas_call(
        body, name=name, grid=(PAIRS, nb), in_specs=in_specs, out_specs=out_specs, out_shape=out_shape,
        scratch_shapes=[pltpu.VMEM((8, lp), F32), pltpu.VMEM((lp, qw), F32)],
        compiler_params=_cp(("parallel", "arbitrary"), VMEM_BIG))(*ins)


def _adamw(w, g, m, v, name):
    lead = w.ndim - 2
    rows, cols = w.shape[lead:]
    big = rows * cols > 512 * 1024
    tr = 128 if big and rows % 128 == 0 else rows
    tc = 256 if big and tr == rows else cols

    def body(w_ref, g_ref, m_ref, v_ref, d_ref, nm_ref, nv_ref):
        gv = g_ref[...]
        nm = ADAM_B1 * m_ref[...] + (1.0 - ADAM_B1) * gv
        nv = ADAM_B2 * v_ref[...] + (1.0 - ADAM_B2) * (gv * gv)
        m_hat = nm / (1.0 - ADAM_B1 ** ADAM_STEP)
        v_hat = nv / (1.0 - ADAM_B2 ** ADAM_STEP)
        d_ref[...] = -ADAM_LR * (m_hat / (jnp.sqrt(v_hat) + ADAM_EPS) + ADAM_WD * w_ref[...])
        nm_ref[...] = nm
        nv_ref[...] = nv

    spec = pl.BlockSpec((1,) * lead + (tr, tc), lambda i, j: (0,) * lead + (i, j))
    return pl.pallas_call(
        body, name=name, grid=(rows // tr, cols // tc), in_specs=[spec] * 4, out_specs=[spec] * 3,
        out_shape=[jax.ShapeDtypeStruct(w.shape, F32)] * 3,
        compiler_params=_cp(("parallel", "parallel"), VMEM_BIG))(w, g, m, v)


def _add_cores(g, from_sib, name):
    n, rows, cols = g.shape
    half = rows // 2
    tr = _tile(half, (256, 240))
    nt = half // tr

    def body(lo_ref, hi_ref, s_ref, o_ref):
        mine = jnp.where(lax.axis_index("c") == 0, lo_ref[0], hi_ref[0])
        o_ref[0] = (mine + s_ref[0]).astype(BF16)

    return pl.pallas_call(
        body, name=name, grid=(n, nt),
        in_specs=[pl.BlockSpec((1, tr, cols), lambda j, i: (j, i, 0)),
                  pl.BlockSpec((1, tr, cols), lambda j, i: (j, nt + i, 0)),
                  pl.BlockSpec((1, tr, cols), lambda j, i: (j, i, 0))],
        out_specs=pl.BlockSpec((1, tr, cols), lambda j, i: (j, i, 0)),
        out_shape=jax.ShapeDtypeStruct((n, half, cols), BF16),
        compiler_params=_cp(("parallel", "parallel"), VMEM_BIG))(g, g, from_sib)


def _add_chips(x, own, name):
    n, rows, cols = x.shape
    tr = _tile(rows, (256, 240))

    def body(x_ref, own_ref, o_ref):
        me = 2 * lax.axis_index("x") + lax.axis_index("y")
        v = [jnp.where(me == k, own_ref[...], x_ref[k]).astype(F32) for k in range(N_CHIPS)]
        o_ref[...] = ((v[0] + v[1]) + v[2]) + v[3]

    return pl.pallas_call(
        body, name=name, grid=(rows // tr,),
        in_specs=[pl.BlockSpec((n, tr, cols), lambda i: (0, i, 0)), pl.BlockSpec((tr, cols), lambda i: (i, 0))],
        out_specs=pl.BlockSpec((tr, cols), lambda i: (i, 0)),
        out_shape=jax.ShapeDtypeStruct((rows, cols), F32), compiler_params=_cp(("parallel",), VMEM_BIG))(x, own)


def _axes():
    return lax.axis_index("x"), lax.axis_index("y"), lax.axis_index("c")


def _other_chips(x, y):
    return [(1 - x, y), (x, 1 - y), (1 - x, 1 - y)]


ANY = pl.BlockSpec(memory_space=pl.ANY)


def _rcopy(src, dst, send_sems, recv_sems, k, to):
    return pltpu.make_async_remote_copy(src_ref=src, dst_ref=dst, send_sem=send_sems.at[k], recv_sem=recv_sems.at[k],
                                        device_id=to, device_id_type=MESH)


def _gather_weights(shards, meta):
    n = len(shards)

    def body(*refs):
        srcs, meta_ref = refs[:n], refs[n]
        outs, mout_ref = refs[n + 1:2 * n + 1], refs[2 * n + 1]
        send_sems, recv_sems = refs[2 * n + 2:]
        x, y, c = _axes()
        me = 2 * x + y
        sib = (x, y, 1 - c)
        chips = _other_chips(x, y)

        def half(t, chip_idx, cc):
            hr = shards[t].shape[0] // 2
            return outs[t].at[chip_idx, pl.ds(cc * hr, hr), :]

        first = []
        for j, (px, py) in enumerate(chips):
            for t in range(n):
                hr = shards[t].shape[0] // 2
                first.append(_rcopy(srcs[t].at[pl.ds(c * hr, hr), :], half(t, me, c), send_sems, recv_sems,
                                    3 * t + j, (px, py, c)))
            first.append(_rcopy(meta_ref, mout_ref.at[me], send_sems, recv_sems, 3 * n + j, (px, py, c)))
        for cp in first:
            cp.start()
        passed = []
        for j, (px, py) in enumerate(chips):
            src_chip = 2 * px + py
            for t in range(n):
                _rcopy(half(t, src_chip, c), half(t, src_chip, c), send_sems, recv_sems, 3 * t + j, sib).wait_recv()
                fwd = _rcopy(half(t, src_chip, c), half(t, src_chip, c), send_sems, recv_sems, 3 * (n + 1 + t) + j, sib)
                fwd.start()
                passed.append(fwd)
            _rcopy(mout_ref.at[src_chip], mout_ref.at[src_chip], send_sems, recv_sems, 3 * n + j, sib).wait_recv()
        for j, (px, py) in enumerate(chips):
            src_chip = 2 * px + py
            for t in range(n):
                _rcopy(half(t, src_chip, 1 - c), half(t, src_chip, 1 - c), send_sems, recv_sems,
                       3 * (n + 1 + t) + j, sib).wait_recv()
        for cp in first + passed:
            cp.wait_send()

    nsem = 3 * (2 * n + 1)
    return pl.pallas_call(
        body, name="gather_weights", in_specs=[ANY] * (n + 1), out_specs=[ANY] * (n + 1),
        out_shape=[jax.ShapeDtypeStruct((N_CHIPS,) + s.shape, s.dtype) for s in shards]
        + [jax.ShapeDtypeStruct((N_CHIPS,) + meta.shape, meta.dtype)],
        scratch_shapes=[pltpu.SemaphoreType.DMA((nsem,)), pltpu.SemaphoreType.DMA((nsem,))])(*shards, meta)


def _swap_halves(gs):
    n = len(gs)
    ncopies = sum(g.shape[0] for g in gs)

    def body(*refs):
        srcs, outs = refs[:n], refs[n:2 * n]
        send_sems, recv_sems = refs[2 * n:]
        x, y, c = _axes()
        cps = []
        for t in range(n):
            hr = gs[t].shape[1] // 2
            for j in range(gs[t].shape[0]):
                cps.append(_rcopy(srcs[t].at[j, pl.ds((1 - c) * hr, hr), :], outs[t].at[j], send_sems, recv_sems,
                                  len(cps), (x, y, 1 - c)))
        for cp in cps:
            cp.start()
        for cp in cps:
            cp.wait()

    return pl.pallas_call(
        body, name="swap_halves", in_specs=[ANY] * n, out_specs=[ANY] * n,
        out_shape=[jax.ShapeDtypeStruct((g.shape[0], g.shape[1] // 2, g.shape[2]), g.dtype) for g in gs],
        scratch_shapes=[pltpu.SemaphoreType.DMA((ncopies,)), pltpu.SemaphoreType.DMA((ncopies,))])(*gs)


def _scatter_chips(parts):
    n = len(parts)

    def body(*refs):
        srcs, outs = refs[:n], refs[n:2 * n]
        send_sems, recv_sems = refs[2 * n:]
        x, y, c = _axes()
        me = 2 * x + y
        cps = []
        for j, (px, py) in enumerate(_other_chips(x, y)):
            for t in range(n):
                cps.append(_rcopy(srcs[t].at[2 * px + py], outs[t].at[me], send_sems, recv_sems, 3 * t + j,
                                  (px, py, c)))
        for cp in cps:
            cp.start()
        for cp in cps:
            cp.wait()

    return pl.pallas_call(
        body, name="scatter_chips", in_specs=[ANY] * n, out_specs=[ANY] * n,
        out_shape=[jax.ShapeDtypeStruct(p.shape, p.dtype) for p in parts],
        scratch_shapes=[pltpu.SemaphoreType.DMA((3 * n,)), pltpu.SemaphoreType.DMA((3 * n,))])(*parts)


def _swap_reduced(rs):
    n = len(rs)

    def body(*refs):
        srcs, outs = refs[:n], refs[n:2 * n]
        send_sems, recv_sems = refs[2 * n:]
        x, y, c = _axes()
        cps = [_rcopy(srcs[t], outs[t], send_sems, recv_sems, t, (x, y, 1 - c)) for t in range(n)]
        for cp in cps:
            cp.start()
        for cp in cps:
            cp.wait()

    return pl.pallas_call(
        body, name="swap_reduced", in_specs=[ANY] * n, out_specs=[ANY] * n,
        out_shape=[jax.ShapeDtypeStruct(r.shape, r.dtype) for r in rs],
        scratch_shapes=[pltpu.SemaphoreType.DMA((n,)), pltpu.SemaphoreType.DMA((n,))])(*rs)


SMALL_ROWS = 24


def _allreduce_small(vec):
    def body(v_ref, out_ref, slots, send_sems, recv_sems):
        x, y, c = _axes()
        me = 4 * x + 2 * y + c
        slots[me] = v_ref[...]
        cps = []
        for k in range(1, 8):
            kx, ky, kc = (k >> 2) & 1, (k >> 1) & 1, k & 1
            peer = (1 - x if kx else x, 1 - y if ky else y, 1 - c if kc else c)
            cps.append(_rcopy(v_ref, slots.at[me], send_sems, recv_sems, k - 1, peer))
        for cp in cps:
            cp.start()
        for cp in cps:
            cp.wait()
        tot = slots[0]
        for k in range(1, 8):
            tot = tot + slots[k]
        out_ref[...] = tot

    return pl.pallas_call(
        body, name="allreduce_small",
        in_specs=[pl.BlockSpec(memory_space=pltpu.VMEM)], out_specs=pl.BlockSpec(memory_space=pltpu.VMEM),
        out_shape=jax.ShapeDtypeStruct((SMALL_ROWS, 128), F32),
        scratch_shapes=[pltpu.VMEM((8, SMALL_ROWS, 128), F32), pltpu.SemaphoreType.DMA((7,)),
                        pltpu.SemaphoreType.DMA((7,))])(vec)


def _pack_p2(w_uq, w_ukv, w_br_mla, w_br_fox, w_out, meta, dtype):
    parts = [w_uq.reshape(96, D_MODEL), w_ukv.reshape(64, D_MODEL), w_br_mla, w_br_fox, w_out,
             jnp.pad(meta.reshape(4, D_MODEL), ((0, P2_ROWS - 932), (0, 0)))]
    return jnp.concatenate([p.astype(dtype) for p in parts], axis=0)


def _unpack_p2(pk):
    return (pk[0:96].reshape(256, 384), pk[96:160].reshape(128, 512), pk[160:416], pk[416:672], pk[672:928],
            pk[928:932].reshape(N_META, 256))


def _uq_arrange(w):
    w3 = w.reshape(256, HEADS, 96)
    nope = w3[:, :, :64].reshape(256, PAIRS, 128)
    pe = w3[:, :, 64:].reshape(256, PAIRS, 64)
    return jnp.concatenate([nope, pe, jnp.zeros((256, PAIRS, 64), w.dtype)], axis=2).reshape(256, PAIRS * 256)


def _uq_restore(g):
    g3 = g.reshape(256, PAIRS, 256)
    nope = g3[:, :, :128].reshape(256, HEADS, 64)
    pe = g3[:, :, 128:192].reshape(256, HEADS, 32)
    return jnp.concatenate([nope, pe], axis=2).reshape(256, HEADS * 96)


def _ukv_arrange(w):
    w3 = w.reshape(128, HEADS, 128)
    return jnp.concatenate([w3[:, :, :64].reshape(128, 1024), w3[:, :, 64:].reshape(128, 1024)], axis=1)


def _ukv_restore(g):
    kn = g[:, :1024].reshape(128, HEADS, 64)
    vv = g[:, 1024:].reshape(128, HEADS, 64)
    return jnp.concatenate([kn, vv], axis=2).reshape(128, HEADS * 128)


def _rope_tables(lp):
    r = np.arange(lp)
    pos = np.where(r < N_META, r, np.where(r >= PAD, r - PAD + N_META, 0)).astype(np.float32)
    half = MLA_ROPE // 2
    inv_freq = np.float32(ROPE_THETA) ** (-np.arange(half, dtype=np.float32) / np.float32(half))
    ang = (pos[:, None] * inv_freq[None, :]).astype(np.float32)
    cos, sin = np.cos(ang).astype(np.float32), np.sin(ang).astype(np.float32)
    one, zero = np.ones((lp, 64), np.float32), np.zeros((lp, 64), np.float32)
    return (jnp.asarray(np.concatenate([cos, cos, cos, cos, one], axis=1)),
            jnp.asarray(np.concatenate([-sin, sin, -sin, sin, zero], axis=1)))


def _pad_lanes(v, n=128):
    return jnp.pad(v, ((0, 0), (0, n - v.shape[1])))


def _in_cols(slabs, a, b):
    out = []
    for j in range(N_CHIPS):
        lo, hi = max(a, W_IN_SHARD * j), min(b, W_IN_SHARD * (j + 1))
        if lo < hi:
            out.append(slabs[j][:, lo - W_IN_SHARD * j:hi - W_IN_SHARD * j])
    return out


def _local_step(x2, tgt2, meta_f, w_small, w_attn, w_gate, w_uq_f, w_ukv_f, w_bm, w_bf, w_o, pre_norm_g,
                post_norm_g, mla_q_norm_g, mla_kv_norm_g, fox_forget_b):
    s_rows = x2.shape[0]
    lp = PAD + s_rows
    w_uq_a = _uq_arrange(w_uq_f)
    w_ukv_a = _ukv_arrange(w_ukv_f)

    ctab, stab = _rope_tables(lp)
    ii = jnp.arange(BLK)
    tri_lo = (ii[:, None] >= ii[None, :]).astype(BF16)
    tri_up = (ii[:, None] <= ii[None, :]).astype(BF16)
    fb128 = _pad_lanes(fox_forget_b)

    u = _rms_pre(x2, meta_f, pre_norm_g)
    small = _mm(u, w_small, mode="nn", out_dtype=F32, name="proj_small")
    attn = _mm(u, w_attn, mode="nn", out_dtype=BF16, name="proj_attn")
    gate = _mm(u, w_gate, mode="nn", out_dtype=BF16, name="proj_gate")
    qn, kvn, kr, ncum = _small_prep(small, mla_q_norm_g, mla_kv_norm_g, fb128, ctab, stab, tri_lo)
    qraw = _mm(qn, w_uq_a, mode="nn", out_dtype=F32, name="mla_q")
    qcat = _rope_q(qraw, ctab, stab)
    kv = _mm(kvn, w_ukv_a, mode="nn", out_dtype=BF16, name="mla_kv")
    nbrep = jnp.broadcast_to(ncum[:, :HEADS].T[:, :, None], (HEADS, lp, LANES))

    mla_cols = dict(qcol=0, kcol=lambda p: p, vcol=lambda p: PAIRS + p)
    fox_cols = dict(qcol=0, kcol=lambda p: PAIRS + p, vcol=lambda p: 2 * PAIRS + p)
    o_mla, lse_mla = _attn_fwd(qcat, kv, kv, kr=kr, scale=MLA_SCALE, name="mla_fwd", **mla_cols)
    o_fox, lse_fox = _attn_fwd(attn, attn, attn, nbrep=nbrep, scale=FOX_SCALE, name="fox_fwd", **fox_cols)

    a_mla, a_fox = _gate_fwd(o_mla, o_fox, gate)
    y_mla = _mm(a_mla, w_bm, mode="nn", out_dtype=BF16, name="br_mla")
    y_fox = _mm(a_fox, w_bf, mode="nn", out_dtype=BF16, name="br_fox")
    mg = _merge_fwd(gate, y_mla, y_fox)
    mixed = _mm(mg, w_o, mode="nn", out_dtype=F32, name="out_proj")
    dmixed, dy, loss_p, dg_post = _tail(x2, mixed, tgt2, post_norm_g)

    d_w_out = _mm(mg, dmixed, mode="tn", out_dtype=F32, name="d_w_out")
    dm = _mm(dmixed, w_o, mode="nt", out_dtype=BF16, name="d_merge")
    dy_mla, dy_fox, dgate_ab = _merge_bwd(dm, gate, y_mla, y_fox)
    d_w_bm = _mm(a_mla, dy_mla, mode="tn", out_dtype=F32, name="d_w_br_mla")
    d_w_bf = _mm(a_fox, dy_fox, mode="tn", out_dtype=F32, name="d_w_br_fox")
    da_mla = _mm(dy_mla, w_bm, mode="nt", out_dtype=BF16, name="d_a_mla")
    da_fox = _mm(dy_fox, w_bf, mode="nt", out_dtype=BF16, name="d_a_fox")
    do_mla, do_fox, dgate_z = _gate_bwd(da_mla, da_fox, o_mla, o_fox, gate)

    dq_a, dkn, dvm, dkr = _attn_bwd(qcat, kv, kv, do_mla, o_mla, lse_mla, kr=kr, rtabs=(ctab, stab), scale=MLA_SCALE,
                                    name="mla_bwd", **mla_cols)
    dfq, dfk, dfv, dcol, drow = _attn_bwd(attn, attn, attn, do_fox, o_fox, lse_fox, nbrep=nbrep, scale=FOX_SCALE,
                                          name="fox_bwd", **fox_cols)

    d_w_uq_a = _mm(qn, dq_a, mode="tn", out_dtype=F32, name="d_w_uq")
    dqn = _mm(dq_a, w_uq_a, mode="nt", out_dtype=F32, name="d_qn")
    d_w_ukv_a = jnp.concatenate([_mm(kvn, dkn, mode="tn", out_dtype=F32, name="d_w_uk"),
                                 _mm(kvn, dvm, mode="tn", out_dtype=F32, name="d_w_uv")], axis=1)
    dkvn = _mm(dkn, w_ukv_a[:, :1024], mode="nt", out_dtype=F32, name="d_kvn_k")
    dkvn = _mm(dvm, w_ukv_a[:, 1024:], mode="nt", out_dtype=F32, name="d_kvn_v", acc=dkvn)
    dsmall, dg_q, dg_kv, dfb = _small_bwd(small, dqn, dkvn, dkr, dcol, drow, mla_q_norm_g, mla_kv_norm_g,
                                          fb128, ctab, stab, tri_up)

    dw_small = _mm(u, dsmall, mode="tn", out_dtype=F32, name="d_w_small")
    dw_fq = _mm(u, dfq, mode="tn", out_dtype=F32, name="d_w_fq")
    dw_fk = _mm(u, dfk, mode="tn", out_dtype=F32, name="d_w_fk")
    dw_fv = _mm(u, dfv, mode="tn", out_dtype=F32, name="d_w_fv")
    dw_z = _mm(u, dgate_z, mode="tn", out_dtype=F32, name="d_w_z")
    dw_g = _mm(u, dgate_ab, mode="tn", out_dtype=F32, name="d_w_g")
    du = _mm(dsmall, w_small, mode="nt", out_dtype=F32, name="d_u_small")
    du = _mm(dfq, w_attn[:, 0:1024], mode="nt", out_dtype=F32, name="d_u_fq", acc=du)
    du = _mm(dfk, w_attn[:, 1024:2048], mode="nt", out_dtype=F32, name="d_u_fk", acc=du)
    du = _mm(dfv, w_attn[:, 2048:3072], mode="nt", out_dtype=F32, name="d_u_fv", acc=du)
    du = _mm(dgate_z, w_gate[:, 0:2048], mode="nt", out_dtype=F32, name="d_u_z", acc=du)
    du = _mm(dgate_ab, w_gate[:, 2048:4096], mode="nt", out_dtype=F32, name="d_u_g", acc=du)
    dx, dmeta, dg_pre = _pre_bwd(du, x2, meta_f, dy, pre_norm_g)

    d_w_in = (dw_small, dw_z, dw_fq, dw_fk, dw_fv, dw_g)
    d_w_uq = _uq_restore(d_w_uq_a)
    d_w_ukv = _ukv_restore(d_w_ukv_a)
    return (loss_p, dx, dmeta, d_w_in, d_w_uq, d_w_ukv, d_w_bm, d_w_bf, d_w_out, dg_pre, dg_post, dg_q, dg_kv, dfb)


def _w_in_slabs(pieces):
    dw_small, dw_z, dw_fq, dw_fk, dw_fv, dw_g = pieces
    runs = [(dw_small[:, 0:416], C_CQ), (dw_z[:, 0:1024], C_ZMLA), (dw_fq, C_FQ), (dw_fk, C_FK), (dw_fv, C_FV),
            (dw_small[:, 512:528], C_FL), (dw_z[:, 1024:2048], C_ZFOX), (dw_g, C_GA)]
    slabs = []
    for j in range(N_CHIPS):
        lo, hi = W_IN_SHARD * j, W_IN_SHARD * (j + 1)
        cols = [a[:, max(lo, c0) - c0:min(hi, c0 + a.shape[1]) - c0] for a, c0 in runs
                if max(lo, c0) < min(hi, c0 + a.shape[1])]
        slabs.append(jnp.concatenate(cols, axis=1))
    return jnp.stack(slabs, axis=0)


def kernel(x, meta_tokens, pre_norm_g, w_in, fox_forget_b, mla_q_norm_g, mla_kv_norm_g, w_uq, w_ukv, w_br_mla, w_br_fox, w_out, post_norm_g, loss_target, m_meta_tokens, m_pre_norm_g, m_w_in, m_fox_forget_b, m_mla_q_norm_g, m_mla_kv_norm_g, m_w_uq, m_w_ukv, m_w_br_mla, m_w_br_fox, m_w_out, m_post_norm_g, v_meta_tokens, v_pre_norm_g, v_w_in, v_fox_forget_b, v_mla_q_norm_g, v_mla_kv_norm_g, v_w_uq, v_w_ukv, v_w_br_mla, v_w_br_fox, v_w_out, v_post_norm_g):
    me = 2 * lax.axis_index("x") + lax.axis_index("y")
    core = lax.axis_index("c")
    w_in_b = w_in.astype(BF16).reshape(D_MODEL, W_IN_SHARD)
    p2 = _pack_p2(w_uq[0], w_ukv[0], w_br_mla[0], w_br_fox[0], w_out[0], jnp.zeros((N_META, 256), F32), BF16)
    w_in_g, p2_g, meta_g = _gather_weights([w_in_b, p2], meta_tokens)
    slabs = [jnp.where(me == j, w_in_b, w_in_g[j]) for j in range(N_CHIPS)]
    pieces = [_unpack_p2(jnp.where(me == j, p2, p2_g[j])) for j in range(N_CHIPS)]
    w_uq_f = jnp.concatenate([p[0] for p in pieces], axis=1)
    w_ukv_f = jnp.concatenate([p[1] for p in pieces], axis=1)
    w_bm = jnp.concatenate([p[2] for p in pieces], axis=0)
    w_bf = jnp.concatenate([p[3] for p in pieces], axis=0)
    w_o = jnp.concatenate([p[4] for p in pieces], axis=0)
    meta_f = jnp.concatenate([jnp.where(me == j, meta_tokens, meta_g[j]) for j in range(N_CHIPS)], axis=1)
    kpe = _in_cols(slabs, C_KPE, C_ZMLA)
    w_small = jnp.concatenate(_in_cols(slabs, C_CQ, C_KPE) + kpe + kpe + [jnp.zeros((D_MODEL, 64), BF16)]
                              + _in_cols(slabs, C_FL, C_ZFOX) + [jnp.zeros((D_MODEL, 112), BF16)], axis=1)
    w_attn = jnp.concatenate(_in_cols(slabs, C_FQ, C_FL), axis=1)
    w_gate = jnp.concatenate(_in_cols(slabs, C_ZMLA, C_FQ) + _in_cols(slabs, C_ZFOX, C_END), axis=1)

    (loss_p, dx, dmeta, d_w_in, d_w_uq, d_w_ukv, d_w_bm, d_w_bf, d_w_out, dg_pre, dg_post, dg_q, dg_kv,
     dfb) = _local_step(x[0], loss_target[0], meta_f, w_small, w_attn, w_gate, w_uq_f, w_ukv_f, w_bm, w_bf, w_o,
                        pre_norm_g, post_norm_g, mla_q_norm_g, mla_kv_norm_g, fox_forget_b)

    g2 = jnp.stack([_pack_p2(d_w_uq[:, 384 * j:384 * (j + 1)], d_w_ukv[:, 512 * j:512 * (j + 1)],
                             d_w_bm[256 * j:256 * (j + 1)], d_w_bf[256 * j:256 * (j + 1)],
                             d_w_out[256 * j:256 * (j + 1)], dmeta[:, 256 * j:256 * (j + 1)], F32)
                    for j in range(N_CHIPS)], axis=0)
    pieces = [p[None] for p in d_w_in]
    from_sib = _swap_halves(pieces + [g2])
    halves = [_add_cores(p, s, "add_cores_" + nm)[0]
              for p, s, nm in zip(pieces, from_sib, ("small", "z", "fq", "fk", "fv", "g"))]
    part1, part2 = _w_in_slabs(halves), _add_cores(g2, from_sib[-1], "add_cores_rest")
    landed = _scatter_chips([part1, part2])
    mine = [_add_chips(l, lax.dynamic_index_in_dim(p, me, 0, keepdims=False), nm)
            for l, p, nm in zip(landed, (part1, part2), ("add_chips_w_in", "add_chips_rest"))]
    theirs = _swap_reduced(mine)
    g_w_in, g_p2 = [jnp.concatenate([jnp.where(core == 0, a, b), jnp.where(core == 0, b, a)], axis=0)
                    for a, b in zip(mine, theirs)]
    g_w_uq, g_w_ukv, g_w_bm, g_w_bf, g_w_out, g_meta = _unpack_p2(g_p2)
    g_w_in = g_w_in[None]

    vec = jnp.concatenate([dg_pre.reshape(8, 128), dg_post.reshape(8, 128), dg_q.reshape(2, 128), dg_kv,
                           dfb, _pad_lanes(loss_p), jnp.zeros((3, 128), F32)], axis=0)
    tot = _allreduce_small(vec)
    loss = tot[20, 0]

    def small_pack(pre, post, gq_, gkv_, fb_):
        return jnp.concatenate([pre.reshape(8, 128), post.reshape(8, 128), gq_.reshape(2, 128), gkv_,
                                _pad_lanes(fb_), jnp.zeros((4, 128), F32)], axis=0)

    def small_unpack(t):
        return (t[0:8].reshape(1, 1024), t[8:16].reshape(1, 1024), t[16:18].reshape(1, 256), t[18:19],
                t[19:20, 0:HEADS])

    g_small = jnp.concatenate([tot[0:20], jnp.zeros((4, 128), F32)], axis=0)
    sm = _adamw(small_pack(pre_norm_g, post_norm_g, mla_q_norm_g, mla_kv_norm_g, fox_forget_b), g_small,
                small_pack(m_pre_norm_g, m_post_norm_g, m_mla_q_norm_g, m_mla_kv_norm_g, m_fox_forget_b),
                small_pack(v_pre_norm_g, v_post_norm_g, v_mla_q_norm_g, v_mla_kv_norm_g, v_fox_forget_b),
                "adamw_small")
    g_pre, g_post, g_q, g_kv, g_fb = small_unpack(g_small)
    (d_pre, d_post, d_q, d_kv, d_fb), (nm_pre, nm_post, nm_q, nm_kv, nm_fb), (nv_pre, nv_post, nv_q, nv_kv, nv_fb) = (
        small_unpack(t) for t in sm)

    d_meta, nm_meta, nv_meta = _adamw(meta_tokens, g_meta, m_meta_tokens, v_meta_tokens, "adamw_meta")
    d_win, nm_win, nv_win = (t.T[None] for t in _adamw(w_in[0].T, g_w_in[0].T, m_w_in[0].T, v_w_in[0].T,
                                                       "adamw_w_in"))
    d_wuq, nm_wuq, nv_wuq = _adamw(w_uq[0], g_w_uq, m_w_uq[0], v_w_uq[0], "adamw_w_uq")
    d_wukv, nm_wukv, nv_wukv = _adamw(w_ukv[0], g_w_ukv, m_w_ukv[0], v_w_ukv[0], "adamw_w_ukv")
    d_wbm, nm_wbm, nv_wbm = _adamw(w_br_mla[0], g_w_bm, m_w_br_mla[0], v_w_br_mla[0], "adamw_w_br_mla")
    d_wbf, nm_wbf, nv_wbf = _adamw(w_br_fox[0], g_w_bf, m_w_br_fox[0], v_w_br_fox[0], "adamw_w_br_fox")
    d_wo, nm_wo, nv_wo = _adamw(w_out[0], g_w_out, m_w_out[0], v_w_out[0], "adamw_w_out")

    def group(meta_, pre, win, fb_, q_, kv_, wuq, wukv, wbm, wbf, wo, post):
        return (meta_, pre, win, fb_, q_, kv_, wuq[None], wukv[None], wbm[None], wbf[None], wo[None], post)

    grads = group(g_meta, g_pre, g_w_in, g_fb, g_q, g_kv, g_w_uq, g_w_ukv, g_w_bm, g_w_bf, g_w_out, g_post)
    deltas = group(d_meta, d_pre, d_win, d_fb, d_q, d_kv, d_wuq, d_wukv, d_wbm, d_wbf, d_wo, d_post)
    new_m = group(nm_meta, nm_pre, nm_win, nm_fb, nm_q, nm_kv, nm_wuq, nm_wukv, nm_wbm, nm_wbf, nm_wo, nm_post)
    new_v = group(nv_meta, nv_pre, nv_win, nv_fb, nv_q, nv_kv, nv_wuq, nv_wukv, nv_wbm, nv_wbf, nv_wo, nv_post)
    return (loss, dx[None], *grads, *deltas, *new_m, *new_v)
```

```python
import math

import jax
import jax.numpy as jnp
import numpy as np
from jax import lax
from jax.experimental import pallas as pl
from jax.experimental.pallas import tpu as pltpu
from jax.experimental.pallas import tpu_sc as plsc

F32 = jnp.float32
BF16 = jnp.bfloat16

D_MODEL = 1024
N_META = 16
RMS_EPS = 1e-6
HEADS = 16
PAIRS = HEADS // 2
HEAD_DIM = 64
LANES = 128
MLA_ROPE = 32
MLA_SCALE = 1.0 / math.sqrt(64 + 32)
FOX_SCALE = 1.0 / math.sqrt(64)
ROPE_THETA = 10000.0

PAD = 256
BLK = 256
QB = 512
UNROLL = 4
NEG = -1e30

C_CQ, C_CKV, C_KPE, C_ZMLA, C_FQ, C_FK, C_FV, C_FL, C_ZFOX, C_GA, C_GB, C_END = (
    0, 256, 384, 416, 1440, 2464, 3488, 4512, 4528, 5552, 6576, 7600)
SMALL_W = 640
W_IN_SHARD = 1900

P2_ROWS = 928
N_CHIPS = 4

ADAM_LR = 0.001
ADAM_B1 = 0.9
ADAM_B2 = 0.999
ADAM_EPS = 1e-08
ADAM_WD = 0.01
ADAM_STEP = 10

VMEM_BIG = 56 * 1024 * 1024
MM_VMEM_BUDGET = 44 * 1024 * 1024
MESH = pl.DeviceIdType.MESH


def _cp(dims, vmem=None):
    return pltpu.CompilerParams(dimension_semantics=dims, vmem_limit_bytes=vmem)


def _dot(a, b, ca, cb):
    return lax.dot_general(a, b, (((ca,), (cb,)), ((), ())), preferred_element_type=F32)


def _sigmoid(x):
    return 1.0 / (1.0 + jnp.exp(-x))


def _tile(n, cands):
    for c in cands:
        if n % c == 0:
            return c
    return n


def _mm(a, b, *, mode, out_dtype, name, acc=None, epilogue=None, row_ins=(), after=None):
    if mode == "nn":
        (M, K), N = a.shape, b.shape[1]
    elif mode == "nt":
        (M, K), N = a.shape, b.shape[0]
    else:
        (K, M), N = a.shape, b.shape[1]
    tm = _tile(M, (1088, 1024)) if M > 1024 else M
    tn = _tile(N, (1024,)) if N > 1024 else N
    nk = 1
    while True:
        tk = K // nk
        need = 2 * tk * (tm * a.dtype.itemsize + tn * b.dtype.itemsize) + tm * tn * (
            2 * jnp.dtype(out_dtype).itemsize + (8 if acc is not None else 0) + (4 if nk > 1 else 0))
        if need <= MM_VMEM_BUDGET or (tk // 2) % (16 if mode == "tn" else LANES) or tk <= 512:
            break
        nk *= 2
    ca, cb = {"nn": (1, 0), "nt": (1, 1), "tn": (0, 0)}[mode]
    a_spec = (pl.BlockSpec((tk, tm), lambda j, i, k: (k, i)) if mode == "tn"
              else pl.BlockSpec((tm, tk), lambda j, i, k: (i, k)))
    b_spec = (pl.BlockSpec((tn, tk), lambda j, i, k: (j, k)) if mode == "nt"
              else pl.BlockSpec((tk, tn), lambda j, i, k: (k, j)))
    o_spec = pl.BlockSpec((tm, tn), lambda j, i, k: (i, j))
    has_acc = acc is not None

    nrow = len(row_ins)

    def body(*refs):
        a_ref, b_ref = refs[0], refs[1]
        acc_ref = refs[2] if has_acc else None
        rows = refs[2 + has_acc:2 + has_acc + nrow]
        o_ref = refs[2 + has_acc + nrow + (after is not None)]

        def store(tile):
            if epilogue is not None:
                tile = epilogue(tile, *[r[...] for r in rows])
            o_ref[...] = tile.astype(out_dtype)

        part = _dot(a_ref[...].astype(BF16), b_ref[...].astype(BF16), ca, cb)
        if nk == 1:
            store(part + acc_ref[...] if has_acc else part)
        else:
            sc = refs[-1]
            k = pl.program_id(2)

            @pl.when(k == 0)
            def _():
                sc[...] = part + acc_ref[...] if has_acc else part

            @pl.when(k > 0)
            def _():
                sc[...] += part

            @pl.when(k == nk - 1)
            def _():
                store(sc[...])

    ins = [a, b] + ([acc] if has_acc else []) + list(row_ins)
    in_specs = ([a_spec, b_spec] + ([o_spec] if has_acc else [])
                + [pl.BlockSpec((tm, r.shape[1]), lambda j, i, k: (i, 0)) for r in row_ins])
    if after is not None:
        ins.append(after)
        in_specs.append(pl.BlockSpec(after.shape, lambda j, i, k: (0,) * after.ndim))
    return pl.pallas_call(
        body, name=name, grid=(N // tn, M // tm, nk), in_specs=in_specs, out_specs=o_spec,
        out_shape=jax.ShapeDtypeStruct((M, N), out_dtype),
        scratch_shapes=[pltpu.VMEM((tm, tn), F32)] if nk > 1 else [],
        compiler_params=_cp(("parallel", "parallel", "arbitrary"), VMEM_BIG))(*ins)


def _row(w):
    return pl.BlockSpec((BLK, w), lambda i: (i, 0))


def _rowc(w, c):
    return pl.BlockSpec((BLK, w), lambda i: (i, c))


def _full(shape):
    return pl.BlockSpec(shape, lambda i: tuple(0 for _ in shape))


def _rope(x, c, s):
    lane = lax.broadcasted_iota(jnp.int32, x.shape, 1)
    is_x1 = ((lane >> 4) & 1) == 0
    partner = jnp.where(is_x1, pltpu.roll(x, LANES - 16, 1), pltpu.roll(x, 16, 1))
    return x * c + partner * s


def _row_valid(i):
    rows = i * BLK + lax.broadcasted_iota(jnp.int32, (BLK, 1), 0)
    return (rows < N_META) | (rows >= PAD)


def _shift_rows(w):
    return pl.BlockSpec((BLK, w), lambda i: (jnp.maximum(i - 1, 0), 0))


def _h_block(i, x_ref, meta_ref):
    head = jnp.concatenate([meta_ref[...], jnp.zeros((BLK - N_META, D_MODEL), F32)], axis=0)
    return jnp.where(i == 0, head, x_ref[...])


def _rms_pre(x2, meta, g):
    lp = PAD + x2.shape[0]

    def body(x_ref, meta_ref, g_ref, u_ref):
        hv = _h_block(pl.program_id(0), x_ref, meta_ref)
        r = lax.rsqrt(jnp.mean(hv * hv, axis=-1, keepdims=True) + RMS_EPS)
        u_ref[...] = (hv * r * g_ref[...]).astype(BF16)

    return pl.pallas_call(
        body, name="rms_pre", grid=(lp // BLK,),
        in_specs=[_shift_rows(D_MODEL), _full((N_META, D_MODEL)), _full((1, D_MODEL))], out_specs=_row(D_MODEL),
        out_shape=jax.ShapeDtypeStruct((lp, D_MODEL), BF16),
        compiler_params=_cp(("parallel",)))(x2, meta, g)


def _split3(x):
    hi = x.astype(BF16)
    r1 = x - hi.astype(F32)
    mid = r1.astype(BF16)
    lo = (r1 - mid.astype(F32)).astype(BF16)
    return hi, mid, lo


def _small_prep(small, gq, gkv, fb, ctab, stab, tri):
    lp = small.shape[0]

    def body(sm_ref, gq_ref, gkv_ref, fb_ref, c_ref, s_ref, tri_ref, qn_ref, kvn_ref, kr_ref, ncum_ref, carry):
        i = pl.program_id(0)

        @pl.when(i == 0)
        def _():
            carry[...] = jnp.zeros_like(carry)

        cq = sm_ref[:, 0:256]
        r = lax.rsqrt(jnp.mean(cq * cq, axis=-1, keepdims=True) + RMS_EPS)
        qn_ref[...] = (cq * r * gq_ref[...]).astype(BF16)
        ckv = sm_ref[:, 256:384]
        r = lax.rsqrt(jnp.mean(ckv * ckv, axis=-1, keepdims=True) + RMS_EPS)
        kvn_ref[...] = (ckv * r * gkv_ref[...]).astype(BF16)
        kr_ref[...] = _rope(sm_ref[:, 384:512], c_ref[...], s_ref[...]).astype(BF16)
        fl = sm_ref[:, 512:640] + fb_ref[...]
        lf = jnp.minimum(fl, 0.0) - jnp.log(1.0 + jnp.exp(-jnp.abs(fl)))
        lf = jnp.where(_row_valid(i), lf, 0.0)
        hi, mid, lo = _split3(lf)
        t = tri_ref[...]
        cum = (_dot(t, hi, 1, 0) + _dot(t, mid, 1, 0)) + _dot(t, lo, 1, 0) + carry[...]
        ncum_ref[...] = -cum
        carry[...] = -ncum_ref[BLK - 1:BLK, :]

    return pl.pallas_call(
        body, name="small_prep", grid=(lp // BLK,),
        in_specs=[_row(SMALL_W), _full((1, 256)), _full((1, 128)), _full((1, 128)), _row(128), _row(128),
                  _full((BLK, BLK))],
        out_specs=[_row(256), _row(128), _row(128), _row(128)],
        out_shape=[jax.ShapeDtypeStruct((lp, 256), BF16), jax.ShapeDtypeStruct((lp, 128), BF16),
                   jax.ShapeDtypeStruct((lp, 128), BF16), jax.ShapeDtypeStruct((lp, 128), F32)],
        scratch_shapes=[pltpu.VMEM((1, 128), F32)],
        compiler_params=_cp(("arbitrary",)))(small, gq, gkv, fb, ctab, stab, tri)


def _rope_pairs(tile, c, s):
    out = []
    for lo in range(0, tile.shape[1], 256):
        out += [tile[:, lo:lo + 128], _rope(tile[:, lo + 128:lo + 256], c, s)]
    return jnp.concatenate(out, axis=1)


def _gate_fwd(o_mla, o_fox, gate):
    lp = o_mla.shape[0]

    def body(om_ref, of_ref, zm_ref, zf_ref, am_ref, af_ref):
        zm = zm_ref[...].astype(F32)
        am_ref[...] = (om_ref[...] * (zm * _sigmoid(zm))).astype(BF16)
        zf = zf_ref[...].astype(F32)
        af_ref[...] = (of_ref[...] * (zf * _sigmoid(zf))).astype(BF16)

    return pl.pallas_call(
        body, name="gate_fwd", grid=(lp // BLK,),
        in_specs=[_row(D_MODEL), _row(D_MODEL), _rowc(D_MODEL, 0), _rowc(D_MODEL, 1)],
        out_specs=[_row(D_MODEL), _row(D_MODEL)],
        out_shape=[jax.ShapeDtypeStruct((lp, D_MODEL), BF16)] * 2,
        compiler_params=_cp(("parallel",)))(o_mla, o_fox, gate, gate)


def _merge_fwd(gate, y_mla, y_fox):
    lp = y_mla.shape[0]

    def body(ga_ref, gb_ref, ym_ref, yf_ref, m_ref):
        sa = _sigmoid(ga_ref[...].astype(F32))
        sb = _sigmoid(gb_ref[...].astype(F32))
        m_ref[...] = (sa * ym_ref[...] + sb * yf_ref[...]).astype(BF16)

    return pl.pallas_call(
        body, name="merge_fwd", grid=(lp // BLK,),
        in_specs=[_rowc(D_MODEL, 2), _rowc(D_MODEL, 3), _row(D_MODEL), _row(D_MODEL)],
        out_specs=_row(D_MODEL), out_shape=jax.ShapeDtypeStruct((lp, D_MODEL), BF16),
        compiler_params=_cp(("parallel",)))(gate, gate, y_mla, y_fox)


def _tail(x2, mixed, tgt, gpost):
    lp = mixed.shape[0]
    shift = _shift_rows(D_MODEL)

    def body(h_ref, mx_ref, t_ref, g_ref, dmx_ref, dy_ref, loss_ref, dg_ref):
        i = pl.program_id(0)

        @pl.when(i == 0)
        def _():
            loss_ref[...] = jnp.zeros_like(loss_ref)
            dg_ref[...] = jnp.zeros_like(dg_ref)
            dmx_ref[...] = jnp.zeros_like(dmx_ref)
            dy_ref[...] = jnp.zeros_like(dy_ref)

        @pl.when(i > 0)
        def _():
            mx = mx_ref[...]
            g = g_ref[...]
            r = lax.rsqrt(jnp.mean(mx * mx, axis=-1, keepdims=True) + RMS_EPS)
            nrm = mx * r
            e = (h_ref[...] + nrm * g) - t_ref[...]
            loss_ref[...] += jnp.sum(0.5 * jnp.sum(e * e, axis=-1, keepdims=True) * (1.0 / D_MODEL),
                                     axis=0, keepdims=True)
            dy = e * (1.0 / D_MODEL)
            dy_ref[...] = dy
            dg_ref[...] += jnp.sum(dy * nrm, axis=0, keepdims=True)
            w = dy * g
            dot = jnp.mean(w * mx, axis=-1, keepdims=True)
            dmx_ref[...] = (r * w - mx * (r * r * r * dot)).astype(BF16)

    return pl.pallas_call(
        body, name="tail", grid=(lp // BLK,),
        in_specs=[shift, _row(D_MODEL), shift, _full((1, D_MODEL))],
        out_specs=[_row(D_MODEL), _row(D_MODEL), _full((1, 1)), _full((1, D_MODEL))],
        out_shape=[jax.ShapeDtypeStruct((lp, D_MODEL), BF16), jax.ShapeDtypeStruct((lp, D_MODEL), F32),
                   jax.ShapeDtypeStruct((1, 1), F32), jax.ShapeDtypeStruct((1, D_MODEL), F32)],
        compiler_params=_cp(("arbitrary",)))(x2, mixed, tgt, gpost)


def _merge_bwd(dm, gate, y_mla, y_fox):
    lp = dm.shape[0]

    def body(dm_ref, ga_ref, gb_ref, ym_ref, yf_ref, dym_ref, dyf_ref, dg_ref):
        dm_v = dm_ref[...].astype(F32)
        sa = _sigmoid(ga_ref[...].astype(F32))
        sb = _sigmoid(gb_ref[...].astype(F32))
        dym_ref[...] = (dm_v * sa).astype(BF16)
        dyf_ref[...] = (dm_v * sb).astype(BF16)
        dg_ref[:, 0:D_MODEL] = (dm_v * ym_ref[...] * (sa * (1.0 - sa))).astype(BF16)
        dg_ref[:, D_MODEL:2 * D_MODEL] = (dm_v * yf_ref[...] * (sb * (1.0 - sb))).astype(BF16)

    return pl.pallas_call(
        body, name="merge_bwd", grid=(lp // BLK,),
        in_specs=[_row(D_MODEL), _rowc(D_MODEL, 2), _rowc(D_MODEL, 3), _row(D_MODEL), _row(D_MODEL)],
        out_specs=[_row(D_MODEL), _row(D_MODEL), _row(2 * D_MODEL)],
        out_shape=[jax.ShapeDtypeStruct((lp, D_MODEL), BF16), jax.ShapeDtypeStruct((lp, D_MODEL), BF16),
                   jax.ShapeDtypeStruct((lp, 2 * D_MODEL), BF16)],
        compiler_params=_cp(("parallel",)))(dm, gate, gate, y_mla, y_fox)


def _gate_bwd(da_mla, da_fox, o_mla, o_fox, gate):
    lp = da_mla.shape[0]

    def one(da, o, z):
        sg = _sigmoid(z)
        do = da * (z * sg)
        dz = da * o * (sg * (1.0 + z * (1.0 - sg)))
        return do.astype(BF16), dz.astype(BF16)

    def body(dam_ref, daf_ref, om_ref, of_ref, zm_ref, zf_ref, dom_ref, dof_ref, dz_ref):
        f32 = lambda r: r[...].astype(F32)
        dom_ref[...], dz_ref[:, 0:D_MODEL] = one(f32(dam_ref), f32(om_ref), f32(zm_ref))
        dof_ref[...], dz_ref[:, D_MODEL:2 * D_MODEL] = one(f32(daf_ref), f32(of_ref), f32(zf_ref))

    return pl.pallas_call(
        body, name="gate_bwd", grid=(lp // BLK,),
        in_specs=[_row(D_MODEL)] * 4 + [_rowc(D_MODEL, 0), _rowc(D_MODEL, 1)],
        out_specs=[_row(D_MODEL), _row(D_MODEL), _row(2 * D_MODEL)],
        out_shape=[jax.ShapeDtypeStruct((lp, D_MODEL), BF16), jax.ShapeDtypeStruct((lp, D_MODEL), BF16),
                   jax.ShapeDtypeStruct((lp, 2 * D_MODEL), BF16)],
        compiler_params=_cp(("parallel",)))(da_mla, da_fox, o_mla, o_fox, gate, gate)


def _small_bwd(small, dqn, dkvn, dkr, dcol_t, drow_t, gq, gkv, fb, ctab, stab, triu):
    lp = small.shape[0]
    nb = lp // BLK

    def rrow(w):
        return pl.BlockSpec((BLK, w), lambda i: (nb - 1 - i, 0))

    def body(sm_ref, dqn_ref, dkvn_ref, dkr_ref, dcol_ref, drow_ref, gq_ref, gkv_ref, fb_ref, c_ref, s_ref, tri_ref,
             ds_ref, dgq_ref, dgkv_ref, dfb_ref, carry):
        i = pl.program_id(0)

        @pl.when(i == 0)
        def _():
            carry[...] = jnp.zeros_like(carry)
            dgq_ref[...] = jnp.zeros_like(dgq_ref)
            dgkv_ref[...] = jnp.zeros_like(dgkv_ref)
            dfb_ref[...] = jnp.zeros_like(dfb_ref)

        def norm_bwd(x, dn, g, dg_ref):
            r = lax.rsqrt(jnp.mean(x * x, axis=-1, keepdims=True) + RMS_EPS)
            dg_ref[...] += jnp.sum(dn * (x * r), axis=0, keepdims=True)
            w = dn * g
            dot = jnp.mean(w * x, axis=-1, keepdims=True)
            return r * w - x * (r * r * r * dot)

        ds_ref[:, 0:256] = norm_bwd(sm_ref[:, 0:256], dqn_ref[...], gq_ref[...], dgq_ref).astype(BF16)
        ds_ref[:, 256:384] = norm_bwd(sm_ref[:, 256:384], dkvn_ref[...], gkv_ref[...], dgkv_ref).astype(BF16)

        dk = dkr_ref[0]
        for p in range(1, PAIRS):
            dk = dk + dkr_ref[p]
        dk = _rope(dk, c_ref[...], -s_ref[...])
        lane = lax.broadcasted_iota(jnp.int32, dk.shape, 1)
        dk = jnp.where(lane < MLA_ROPE, dk + pltpu.roll(dk, LANES - MLA_ROPE, 1), 0.0)
        ds_ref[:, 384:512] = dk.astype(BF16)

        dcol = dcol_ref[0]
        for p in range(1, PAIRS):
            dcol = dcol + pltpu.roll(dcol_ref[p], 2 * p, 1)
        rows16 = jnp.concatenate([drow_ref[p, h:h + 1, :] for p in range(PAIRS) for h in range(2)], axis=0)
        eye = (lax.broadcasted_iota(jnp.int32, (HEADS, LANES), 0)
               == lax.broadcasted_iota(jnp.int32, (HEADS, LANES), 1)).astype(BF16)
        drow = sum(_dot(part, eye, 0, 0) for part in _split3(rows16))
        dcr = dcol - drow
        hi, mid, lo = _split3(dcr)
        t = tri_ref[...]
        suf = (_dot(t, hi, 1, 0) + _dot(t, mid, 1, 0)) + _dot(t, lo, 1, 0) + carry[...]
        fl = sm_ref[:, 512:640] + fb_ref[...]
        dfl = jnp.where(_row_valid(nb - 1 - i), -suf * _sigmoid(-fl), 0.0)
        ds_ref[:, 512:640] = dfl.astype(BF16)
        dfb_ref[...] += jnp.sum(dfl, axis=0, keepdims=True)
        carry[...] += jnp.sum(dcr, axis=0, keepdims=True)

    return pl.pallas_call(
        body, name="small_bwd", grid=(nb,),
        in_specs=[rrow(SMALL_W), rrow(256), rrow(128),
                  pl.BlockSpec((PAIRS, BLK, 128), lambda i: (0, nb - 1 - i, 0)),
                  pl.BlockSpec((PAIRS, BLK, 128), lambda i: (0, nb - 1 - i, 0)),
                  pl.BlockSpec((PAIRS, 2, BLK), lambda i: (0, 0, nb - 1 - i)),
                  _full((1, 256)), _full((1, 128)), _full((1, 128)), rrow(128), rrow(128), _full((BLK, BLK))],
        out_specs=[rrow(SMALL_W), _full((1, 256)), _full((1, 128)), _full((1, 128))],
        out_shape=[jax.ShapeDtypeStruct((lp, SMALL_W), BF16), jax.ShapeDtypeStruct((1, 256), F32),
                   jax.ShapeDtypeStruct((1, 128), F32), jax.ShapeDtypeStruct((1, 128), F32)],
        scratch_shapes=[pltpu.VMEM((1, 128), F32)],
        compiler_params=_cp(("arbitrary",)))(small, dqn, dkvn, dkr, dcol_t, drow_t, gq, gkv, fb, ctab, stab, triu)


def _pre_bwd(du, x2, meta, dy, gpre):
    s_rows = x2.shape[0]
    lp = PAD + s_rows
    shift = _shift_rows(D_MODEL)

    def body(du_ref, x_ref, meta_ref, dy_ref, g_ref, dx_ref, dmeta_ref, dg_ref):
        i = pl.program_id(0)

        @pl.when(i == 0)
        def _():
            dg_ref[...] = jnp.zeros_like(dg_ref)

        hv = _h_block(i, x_ref, meta_ref)
        duv = du_ref[...]
        r = lax.rsqrt(jnp.mean(hv * hv, axis=-1, keepdims=True) + RMS_EPS)
        dg_ref[...] += jnp.sum(duv * (hv * r), axis=0, keepdims=True)
        w = duv * g_ref[...]
        dot = jnp.mean(w * hv, axis=-1, keepdims=True)
        dh = dy_ref[...] + (r * w - hv * (r * r * r * dot))
        dx_ref[...] = dh

        @pl.when(i == 0)
        def _():
            dmeta_ref[...] = dh[0:N_META, :]

    return pl.pallas_call(
        body, name="pre_bwd", grid=(lp // BLK,),
        in_specs=[_row(D_MODEL), shift, _full((N_META, D_MODEL)), _row(D_MODEL), _full((1, D_MODEL))],
        out_specs=[shift, _full((N_META, D_MODEL)), _full((1, D_MODEL))],
        out_shape=[jax.ShapeDtypeStruct((s_rows, D_MODEL), F32), jax.ShapeDtypeStruct((N_META, D_MODEL), F32),
                   jax.ShapeDtypeStruct((1, D_MODEL), F32)],
        compiler_params=_cp(("arbitrary",)))(du, x2, meta, dy, gpre)


def _pair_masks(rope):
    lane = lax.broadcasted_iota(jnp.int32, (1, LANES), 1)
    mas = [lane < HEAD_DIM, lane >= HEAD_DIM]
    if not rope:
        return mas, mas
    wide = lax.broadcasted_iota(jnp.int32, (1, 2 * LANES), 1)
    rope_lo = LANES + MLA_ROPE
    return mas, [(wide < HEAD_DIM) | ((wide >= LANES) & (wide < rope_lo)),
                 ((wide >= HEAD_DIM) & (wide < LANES)) | ((wide >= rope_lo) & (wide < rope_lo + MLA_ROPE))]


def _mask2(x, masks):
    return [jnp.where(m, x, jnp.zeros_like(x)) for m in masks]


def _attn_fwd(q, k, v, *, kr=None, nbrep=None, scale, qcol, kcol, vcol, name):
    lp = q.shape[0]
    nq = 1 + (lp - PAD) // QB
    rope = kr is not None
    bias = nbrep is not None
    qw = 256 if rope else 128

    def body(*refs):
        it = iter(refs)
        q_ref, k_ref, v_ref = next(it), next(it), next(it)
        kr_ref = next(it) if rope else None
        nb_ref = next(it) if bias else None
        o_ref, lse_ref = next(it), next(it)
        i = pl.program_id(1)
        r0 = pl.multiple_of(jnp.where(i == 0, 0, PAD + QB * (i - 1)), BLK)
        b0 = r0 // BLK
        mas, hmask = _pair_masks(rope)
        qh = _mask2(q_ref[pl.ds(r0, QB), :], hmask)
        if bias:
            qh = [x * scale for x in qh]

        def causal(kc, n):
            key = kc * BLK + lax.broadcasted_iota(jnp.int32, (n, QB), 0)
            return (key <= r0 + lax.broadcasted_iota(jnp.int32, (n, QB), 1)) & ((kc > 0) | (n == N_META))

        def update(kcs, carry, masks, n=BLK):
            stats, acc = carry[:4], carry[4]
            k0s = [pl.multiple_of(kc * BLK, BLK) for kc in kcs]
            kks = [k_ref[pl.ds(k0, n), :] for k0 in k0s]
            if rope:
                kks = [jnp.concatenate([kk, kr_ref[pl.ds(k0, n), :]], axis=1) for kk, k0 in zip(kks, k0s)]
            new_stats, alphas, ps = [], [], [[] for _ in kcs]
            for h in range(2):
                m_prev, l_prev = stats[2 * h], stats[2 * h + 1]
                ss = []
                for kk, k0, mask in zip(kks, k0s, masks):
                    s = _dot(kk, qh[h], 1, 1)
                    if rope:
                        s = s * scale
                    if bias:
                        nbc = nb_ref[h, pl.ds(k0, n), :]
                        s = s + jnp.concatenate([nbc] * (QB // LANES), axis=1)
                    if mask is not None:
                        s = jnp.where(mask, s, NEG)
                    ss.append(s)
                m_new = m_prev
                for s in ss:
                    m_new = jnp.maximum(m_new, jnp.max(s, axis=0, keepdims=True))
                alpha = jnp.exp(m_prev - m_new)
                l_new = alpha * l_prev
                for j, s in enumerate(ss):
                    p = jnp.exp(s - m_new)
                    l_new = l_new + jnp.sum(p, axis=0, keepdims=True)
                    ps[j].append(p.astype(BF16))
                new_stats += [m_new, l_new]
                alphas.append(alpha)
            vcat = jnp.concatenate([x for k0 in k0s for x in _mask2(v_ref[pl.ds(k0, n), :], mas)], axis=0)
            pv = _dot(vcat, jnp.concatenate([p for pj in ps for p in pj], axis=0), 0, 0)
            a_full = jnp.concatenate([jnp.broadcast_to(a, (HEAD_DIM, QB)) for a in alphas], axis=0)
            return (*new_stats, a_full * acc + pv)

        neg = jnp.full((1, QB), NEG, F32)
        zero = jnp.zeros((1, QB), F32)
        c = update([0], (neg, zero, neg, zero, jnp.zeros((LANES, QB), F32)), [causal(0, N_META)], N_META)
        n_mid = jnp.maximum(b0 - 1, 0)
        c = lax.fori_loop(0, n_mid // 4, lambda t, cr: update([4 * t + u for u in (1, 2, 3, 4)], cr, [None] * 4), c)
        c = lax.fori_loop(0, (n_mid % 4) // 2, lambda t, cr: update([n_mid - 1, n_mid], cr, [None, None]), c)
        c = update([b0, b0 + 1], c, [causal(b0, BLK), causal(b0 + 1, BLK)])
        inv =jnp.concatenate([jnp.broadcast_to(1.0 / c[1], (HEAD_DIM, QB)),
                               jnp.broadcast_to(1.0 / c[3], (HEAD_DIM, QB))], axis=0)
        o_t = (c[4] * inv).T.astype(BF16)
        lses = [c[0] + jnp.log(c[1]), c[2] + jnp.log(c[3])]
        o_ref[pl.ds(r0, BLK), :] = o_t[0:BLK]
        for h in range(2):
            lse_ref[0, h:h + 1, pl.ds(r0, BLK)] = lses[h][:, 0:BLK]

        @pl.when(i > 0)
        def _():
            r1 = pl.multiple_of(r0 + BLK, BLK)
            o_ref[pl.ds(r1, QB - BLK), :] = o_t[BLK:QB]
            for h in range(2):
                lse_ref[0, h:h + 1, pl.ds(r1, QB - BLK)] = lses[h][:, BLK:QB]

    in_specs = [pl.BlockSpec((lp, qw), lambda p, i: (0, qcol + p)),
                pl.BlockSpec((lp, 128), lambda p, i: (0, kcol(p))),
                pl.BlockSpec((lp, 128), lambda p, i: (0, vcol(p)))]
    ins = [q, k, v]
    if rope:
        in_specs.append(pl.BlockSpec((lp, 128), lambda p, i: (0, 0)))
        ins.append(kr)
    if bias:
        in_specs.append(pl.BlockSpec((2, lp, 128), lambda p, i: (p, 0, 0)))
        ins.append(nbrep)
    return pl.pallas_call(
        body, name=name, grid=(PAIRS, nq), in_specs=in_specs,
        out_specs=[pl.BlockSpec((lp, 128), lambda p, i: (0, p)),
                   pl.BlockSpec((1, 2, lp), lambda p, i: (p, 0, 0))],
        out_shape=[jax.ShapeDtypeStruct((lp, D_MODEL), BF16), jax.ShapeDtypeStruct((PAIRS, 2, lp), F32)],
        compiler_params=_cp(("parallel", "arbitrary"), VMEM_BIG))(*ins)


def _attn_bwd(q, k, v, do, o, lse, *, kr=None, rtabs=None, nbrep=None, scale, qcol, kcol, vcol, name):
    lp = q.shape[0]
    nb = lp // BLK
    rope = kr is not None
    bias = nbrep is not None
    qw = 256 if rope else 128

    def body(*refs):
        it = iter(refs)
        q_ref, k_ref, v_ref = next(it), next(it), next(it)
        kr_ref = next(it) if rope else None
        nb_ref = next(it) if bias else None
        do_ref, o_ref, lse_ref = next(it), next(it), next(it)
        ct_ref, st_ref = (next(it), next(it)) if rope else (None, None)
        dq_out, dk_ref, dv_ref = next(it), next(it), next(it)
        x_ref = next(it)
        drow_ref = next(it) if bias else None
        delta, dq_ref = next(it), next(it)
        kb = pl.program_id(1)
        mas, hmask = _pair_masks(rope)
        lane = lax.broadcasted_iota(jnp.int32, (1, LANES), 1)

        @pl.when(kb == 0)
        def _():
            dq_ref[...] = jnp.zeros_like(dq_ref)
            if bias:
                drow_ref[...] = jnp.zeros_like(drow_ref)
            sub = lax.broadcasted_iota(jnp.int32, (8, LANES), 0)
            sel = (((sub == 0) & mas[0]) | ((sub == 1) & mas[1])).astype(BF16)

            def dstep(c, carry):
                r0 = pl.multiple_of(c * BLK, BLK)
                prod = do_ref[pl.ds(r0, BLK), :].astype(F32) * o_ref[pl.ds(r0, BLK), :]
                hi, mid, lo = _split3(prod)
                delta[:, pl.ds(r0, BLK)] = (_dot(sel, hi, 1, 1) + _dot(sel, mid, 1, 1)) + _dot(sel, lo, 1, 1)
                return carry

            lax.fori_loop(0, nb, dstep, 0)

        def masked_q(q0):
            qh = _mask2(q_ref[pl.ds(q0, BLK), :], hmask)
            return [x * scale for x in qh] if bias else qh

        def key_pass(n):
            kk = k_ref[0:n, :]
            if rope:
                kk = jnp.concatenate([kk, kr_ref[0:n, :]], axis=1)
            vh = _mask2(v_ref[0:n, :], mas)
            kcat = jnp.concatenate(_mask2(kk, hmask), axis=0)
            if bias:
                kcat = kcat * scale
                nbc = [jnp.concatenate([nb_ref[h, 0:n, :], nb_ref[h, 0:n, :]], axis=1) for h in range(2)]
            diag_mask = (lax.broadcasted_iota(jnp.int32, (n, BLK), 0) <= lax.broadcasted_iota(jnp.int32, (n, BLK), 1))

            def chunk(qc, carry, mask):
                carry = list(carry)
                q0 = pl.multiple_of(qc * BLK, BLK)
                dov = do_ref[pl.ds(q0, BLK), :]
                doh = _mask2(dov, mas)
                qh = masked_q(q0)
                pbs, dss = [], []
                for h in range(2):
                    s = _dot(kk, qh[h], 1, 1)
                    if rope:
                        s = s * scale
                    if bias:
                        s = s + nbc[h]
                    p = jnp.exp(s - lse_ref[0, h:h + 1, pl.ds(q0, BLK)])
                    if mask is not None:
                        p = jnp.where(mask, p, 0.0)
                    ds = p * (_dot(vh[h], dov, 1, 1) - delta[h:h + 1, pl.ds(q0, BLK)])
                    if bias:
                        drow_ref[0, h:h + 1, pl.ds(q0, BLK)] += jnp.sum(ds, axis=0, keepdims=True)
                        carry[2 + h] = carry[2 + h] + jnp.sum(ds, axis=1, keepdims=True)
                    else:
                        ds = ds * scale
                    pbs.append(p.astype(BF16))
                    dss.append(ds.astype(BF16))
                ds_lanes = jnp.concatenate(dss, axis=1)
                ds_rows = jnp.concatenate(dss, axis=0)
                carry[0] = carry[0] + _dot(ds_lanes, jnp.concatenate(qh, axis=0), 1, 0)
                carry[1] = carry[1] + _dot(jnp.concatenate(pbs, axis=1), jnp.concatenate(doh, axis=0), 1, 0)
                dq_ref[pl.ds(q0, BLK), :] += _dot(ds_rows, kcat, 0, 0)
                return tuple(carry)

            init = [jnp.zeros((n, qw), F32), jnp.zeros((n, LANES), F32)]
            if bias:
                init += [jnp.zeros((n, 1), F32), jnp.zeros((n, 1), F32)]
            groups = (nb - kb) // UNROLL

            def several(t, cr):
                for u in range(UNROLL):
                    cr = chunk(kb + UNROLL * t + u, cr, (diag_mask | (t > 0)) if u == 0 else None)
                return cr

            c = lax.fori_loop(0, groups, several, tuple(init))
            c = lax.fori_loop(kb + UNROLL * groups, nb, lambda qc, cr: chunk(qc, cr, diag_mask | (qc > kb)), c)

            def rows(a, dtype):
                a = a.astype(dtype)
                return a if n == BLK else jnp.concatenate([a, jnp.zeros((BLK - n, a.shape[1]), dtype)], axis=0)

            dk_ref[...] = rows(c[0][:, 0:LANES], BF16)
            dv_ref[...] = rows(c[1], BF16)
            if rope:
                x_ref[0] = rows(c[0][:, LANES:2 * LANES], F32)
            if bias:
                x_ref[0] = rows(jnp.where(lane == 0, c[2], jnp.where(lane == 1, c[3], 0.0)), F32)

        @pl.when(kb == 0)
        def _():
            key_pass(N_META)

        @pl.when(kb > 0)
        def _():
            key_pass(BLK)

        @pl.when(kb == nb - 1)
        def _():
            def fin(c, carry):
                r0 = pl.multiple_of(c * BLK, BLK)
                dq = dq_ref[pl.ds(r0, BLK), :]
                if rope:
                    back = _rope(dq[:, LANES:2 * LANES], ct_ref[pl.ds(r0, BLK), :], -st_ref[pl.ds(r0, BLK), :])
                    dq = jnp.concatenate([dq[:, 0:LANES], back], axis=1)
                dq_out[pl.ds(r0, BLK), :] = dq.astype(BF16)
                return carry

            lax.fori_loop(0, nb, fin, 0)

    in_specs = [pl.BlockSpec((lp, qw), lambda p, j: (0, qcol + p)),
                pl.BlockSpec((BLK, 128), lambda p, j: (j, kcol(p))),
                pl.BlockSpec((BLK, 128), lambda p, j: (j, vcol(p)))]
    ins = [q, k, v]
    if rope:
        in_specs.append(pl.BlockSpec((BLK, 128), lambda p, j: (j, 0)))
        ins.append(kr)
    if bias:
        in_specs.append(pl.BlockSpec((2, BLK, 128), lambda p, j: (p, j, 0)))
        ins.append(nbrep)
    in_specs += [pl.BlockSpec((lp, 128), lambda p, j: (0, p)), pl.BlockSpec((lp, 128), lambda p, j: (0, p)),
                 pl.BlockSpec((1, 2, lp), lambda p, j: (p, 0, 0))]
    ins += [do, o, lse]
    if rope:
        in_specs += [pl.BlockSpec((lp, 128), lambda p, j: (0, 0))] * 2
        ins += list(rtabs)
    out_specs = [pl.BlockSpec((lp, qw), lambda p, j: (0, p)),
                 pl.BlockSpec((BLK, 128), lambda p, j: (j, p)),
                 pl.BlockSpec((BLK, 128), lambda p, j: (j, p)),
                 pl.BlockSpec((1, BLK, 128), lambda p, j: (p, j, 0))]
    out_shape = [jax.ShapeDtypeStruct((lp, PAIRS * qw), BF16), jax.ShapeDtypeStruct((lp, D_MODEL), BF16),
                 jax.ShapeDtypeStruct((lp, D_MODEL), BF16), jax.ShapeDtypeStruct((PAIRS, lp, 128), F32)]
    if bias:
        out_specs.append(pl.BlockSpec((1, 2, lp), lambda p, j: (p, 0, 0)))
        out_shape.append(jax.ShapeDtypeStruct((PAIRS, 2, lp), F32))
    return pl.pallas_call(
        body, name=name, grid=(PAIRS, nb), in_specs=in_specs, out_specs=out_specs, out_shape=out_shape,
        scratch_shapes=[pltpu.VMEM((8, lp), F32), pltpu.VMEM((lp, qw), F32)],
        compiler_params=_cp(("parallel", "arbitrary"), VMEM_BIG))(*ins)


def _adamw(w, g, m, v, name):
    lead = w.ndim - 2
    rows, cols = w.shape[lead:]
    big = rows * cols > 512 * 1024
    tr = 128 if big and rows % 128 == 0 else rows
    tc = 256 if big and tr == rows else cols

    def body(w_ref, g_ref, m_ref, v_ref, d_ref, nm_ref, nv_ref):
        gv = g_ref[...]
        nm = ADAM_B1 * m_ref[...] + (1.0 - ADAM_B1) * gv
        nv = ADAM_B2 * v_ref[...] + (1.0 - ADAM_B2) * (gv * gv)
        m_hat = nm / (1.0 - ADAM_B1 ** ADAM_STEP)
        v_hat = nv / (1.0 - ADAM_B2 ** ADAM_STEP)
        d_ref[...] = -ADAM_LR * (m_hat / (jnp.sqrt(v_hat) + ADAM_EPS) + ADAM_WD * w_ref[...])
        nm_ref[...] = nm
        nv_ref[...] = nv

    spec = pl.BlockSpec((1,) * lead + (tr, tc), lambda i, j: (0,) * lead + (i, j))
    return pl.pallas_call(
        body, name=name, grid=(rows // tr, cols // tc), in_specs=[spec] * 4, out_specs=[spec] * 3,
        out_shape=[jax.ShapeDtypeStruct(w.shape, F32)] * 3,
        compiler_params=_cp(("parallel", "parallel"), VMEM_BIG))(w, g, m, v)


def _add_cores(g, from_sib, name):
    n, rows, cols = g.shape
    half = rows // 2
    tr = _tile(half, (256, 240))
    nt = half // tr

    def body(lo_ref, hi_ref, s_ref, o_ref):
        mine = jnp.where(lax.axis_index("c") == 0, lo_ref[0], hi_ref[0])
        o_ref[0] = (mine + s_ref[0]).astype(BF16)

    return pl.pallas_call(
        body, name=name, grid=(n, nt),
        in_specs=[pl.BlockSpec((1, tr, cols), lambda j, i: (j, i, 0)),
                  pl.BlockSpec((1, tr, cols), lambda j, i: (j, nt + i, 0)),
                  pl.BlockSpec((1, tr, cols), lambda j, i: (j, i, 0))],
        out_specs=pl.BlockSpec((1, tr, cols), lambda j, i: (j, i, 0)),
        out_shape=jax.ShapeDtypeStruct((n, half, cols), BF16),
        compiler_params=_cp(("parallel", "parallel"), VMEM_BIG))(g, g, from_sib)


def _add_chips(x, own, name):
    n, rows, cols = x.shape
    tr = _tile(rows, (256, 240))

    def body(x_ref, own_ref, o_ref):
        me = 2 * lax.axis_index("x") + lax.axis_index("y")
        v = [jnp.where(me == k, own_ref[...], x_ref[k]).astype(F32) for k in range(N_CHIPS)]
        o_ref[...] = ((v[0] + v[1]) + v[2]) + v[3]

    return pl.pallas_call(
        body, name=name, grid=(rows // tr,),
        in_specs=[pl.BlockSpec((n, tr, cols), lambda i: (0, i, 0)), pl.BlockSpec((tr, cols), lambda i: (i, 0))],
        out_specs=pl.BlockSpec((tr, cols), lambda i: (i, 0)),
        out_shape=jax.ShapeDtypeStruct((rows, cols), F32), compiler_params=_cp(("parallel",), VMEM_BIG))(x, own)


def _axes():
    return lax.axis_index("x"), lax.axis_index("y"), lax.axis_index("c")


def _other_chips(x, y):
    return [(1 - x, y), (x, 1 - y), (1 - x, 1 - y)]


ANY = pl.BlockSpec(memory_space=pl.ANY)


def _rcopy(src, dst, send_sems, recv_sems, k, to):
    return pltpu.make_async_remote_copy(src_ref=src, dst_ref=dst, send_sem=send_sems.at[k], recv_sem=recv_sems.at[k],
                                        device_id=to, device_id_type=MESH)


def _gather_weights(shards, meta):
    n = len(shards)

    def body(*refs):
        srcs, meta_ref = refs[:n], refs[n]
        outs, mout_ref = refs[n + 1:2 * n + 1], refs[2 * n + 1]
        send_sems, recv_sems = refs[2 * n + 2:]
        x, y, c = _axes()
        me = 2 * x + y
        sib = (x, y, 1 - c)
        chips = _other_chips(x, y)

        def half(t, chip_idx, cc):
            hr = shards[t].shape[0] // 2
            return outs[t].at[chip_idx, pl.ds(cc * hr, hr), :]

        first = []
        for j, (px, py) in enumerate(chips):
            for t in range(n):
                hr = shards[t].shape[0] // 2
                first.append(_rcopy(srcs[t].at[pl.ds(c * hr, hr), :], half(t, me, c), send_sems, recv_sems,
                                    3 * t + j, (px, py, c)))
            first.append(_rcopy(meta_ref, mout_ref.at[me], send_sems, recv_sems, 3 * n + j, (px, py, c)))
        for cp in first:
            cp.start()
        passed = []
        for j, (px, py) in enumerate(chips):
            src_chip = 2 * px + py
            for t in range(n):
                _rcopy(half(t, src_chip, c), half(t, src_chip, c), send_sems, recv_sems, 3 * t + j, sib).wait_recv()
                fwd = _rcopy(half(t, src_chip, c), half(t, src_chip, c), send_sems, recv_sems, 3 * (n + 1 + t) + j, sib)
                fwd.start()
                passed.append(fwd)
            _rcopy(mout_ref.at[src_chip], mout_ref.at[src_chip], send_sems, recv_sems, 3 * n + j, sib).wait_recv()
        for j, (px, py) in enumerate(chips):
            src_chip = 2 * px + py
            for t in range(n):
                _rcopy(half(t, src_chip, 1 - c), half(t, src_chip, 1 - c), send_sems, recv_sems,
                       3 * (n + 1 + t) + j, sib).wait_recv()
        for cp in first + passed:
            cp.wait_send()

    nsem = 3 * (2 * n + 1)
    return pl.pallas_call(
        body, name="gather_weights", in_specs=[ANY] * (n + 1), out_specs=[ANY] * (n + 1),
        out_shape=[jax.ShapeDtypeStruct((N_CHIPS,) + s.shape, s.dtype) for s in shards]
        + [jax.ShapeDtypeStruct((N_CHIPS,) + meta.shape, meta.dtype)],
        scratch_shapes=[pltpu.SemaphoreType.DMA((nsem,)), pltpu.SemaphoreType.DMA((nsem,))])(*shards, meta)


def _swap_halves(gs):
    n = len(gs)
    ncopies = sum(g.shape[0] for g in gs)

    def body(*refs):
        srcs, outs = refs[:n], refs[n:2 * n]
        send_sems, recv_sems = refs[2 * n:]
        x, y, c = _axes()
        cps = []
        for t in range(n):
            hr = gs[t].shape[1] // 2
            for j in range(gs[t].shape[0]):
                cps.append(_rcopy(srcs[t].at[j, pl.ds((1 - c) * hr, hr), :], outs[t].at[j], send_sems, recv_sems,
                                  len(cps), (x, y, 1 - c)))
        for cp in cps:
            cp.start()
        for cp in cps:
            cp.wait()

    return pl.pallas_call(
        body, name="swap_halves", in_specs=[ANY] * n, out_specs=[ANY] * n,
        out_shape=[jax.ShapeDtypeStruct((g.shape[0], g.shape[1] // 2, g.shape[2]), g.dtype) for g in gs],
        scratch_shapes=[pltpu.SemaphoreType.DMA((ncopies,)), pltpu.SemaphoreType.DMA((ncopies,))])(*gs)


def _scatter_chips(parts):
    n = len(parts)
    srcs = [jax.new_ref(p, memory_space=pltpu.MemorySpace.HBM) for p in parts]
    outs = [jax.empty_ref(jax.ShapeDtypeStruct(p.shape, p.dtype), memory_space=pltpu.MemorySpace.HBM) for p in parts]

    @pl.kernel(mesh=plsc.ScalarSubcoreMesh(axis_name="seq", num_cores=1), name="scatter_chips",
               scratch_types=(pltpu.SemaphoreType.DMA((3 * n,)), pltpu.SemaphoreType.DMA((3 * n,))),
               compiler_params=pltpu.CompilerParams(collective_id=0))
    def launch(send_sems, recv_sems):
        x, y, c = _axes()
        me = 2 * x + y
        chips = _other_chips(x, y)
        barrier = pltpu.get_barrier_semaphore()
        for px, py in chips:
            pl.semaphore_signal(barrier, inc=1, device_id=(px, py, c), device_id_type=MESH)
        pl.semaphore_wait(barrier, 3)
        cps = []
        for j, (px, py) in enumerate(chips):
            for t in range(n):
                cps.append(_rcopy(srcs[t].at[2 * px + py], outs[t].at[me], send_sems, recv_sems, 3 * t + j,
                                  (px, py, c)))
        for cp in cps:
            cp.start()
        for cp in cps:
            cp.wait()

    launch()
    return [o[...] for o in outs]


def _swap_reduced(rs):
    n = len(rs)

    def body(*refs):
        srcs, outs = refs[:n], refs[n:2 * n]
        send_sems, recv_sems = refs[2 * n:]
        x, y, c = _axes()
        cps = [_rcopy(srcs[t], outs[t], send_sems, recv_sems, t, (x, y, 1 - c)) for t in range(n)]
        for cp in cps:
            cp.start()
        for cp in cps:
            cp.wait()

    return pl.pallas_call(
        body, name="swap_reduced", in_specs=[ANY] * n, out_specs=[ANY] * n,
        out_shape=[jax.ShapeDtypeStruct(r.shape, r.dtype) for r in rs],
        scratch_shapes=[pltpu.SemaphoreType.DMA((n,)), pltpu.SemaphoreType.DMA((n,))])(*rs)


SMALL_ROWS = 24 + 128


def _allreduce_small(vec):
    def body(v_ref, out_ref, slots, send_sems, recv_sems):
        x, y, c = _axes()
        me = 4 * x + 2 * y + c
        slots[me] = v_ref[...]
        cps = []
        for k in range(1, 8):
            kx, ky, kc = (k >> 2) & 1, (k >> 1) & 1, k & 1
            peer = (1 - x if kx else x, 1 - y if ky else y, 1 - c if kc else c)
            cps.append(_rcopy(v_ref, slots.at[me], send_sems, recv_sems, k - 1, peer))
        for cp in cps:
            cp.start()
        for cp in cps:
            cp.wait()
        tot = slots[0]
        for k in range(1, 8):
            tot = tot + slots[k]
        out_ref[...] = tot

    return pl.pallas_call(
        body, name="allreduce_small",
        in_specs=[pl.BlockSpec(memory_space=pltpu.VMEM)], out_specs=pl.BlockSpec(memory_space=pltpu.VMEM),
        out_shape=jax.ShapeDtypeStruct((SMALL_ROWS, 128), F32),
        scratch_shapes=[pltpu.VMEM((8, SMALL_ROWS, 128), F32), pltpu.SemaphoreType.DMA((7,)),
                        pltpu.SemaphoreType.DMA((7,))])(vec)


def _pack_p2(w_uq, w_ukv, w_br_mla, w_br_fox, w_out, dtype):
    parts = [w_uq.reshape(96, D_MODEL), w_ukv.reshape(64, D_MODEL), w_br_mla, w_br_fox, w_out]
    return jnp.concatenate([p.astype(dtype) for p in parts], axis=0)


def _unpack_p2(pk):
    return pk[0:96].reshape(256, 384), pk[96:160].reshape(128, 512), pk[160:416], pk[416:672], pk[672:928]


def _uq_arrange(w):
    w3 = w.reshape(256, HEADS, 96)
    nope = w3[:, :, :64].reshape(256, PAIRS, 128)
    pe = w3[:, :, 64:].reshape(256, PAIRS, 64)
    return jnp.concatenate([nope, pe, jnp.zeros((256, PAIRS, 64), w.dtype)], axis=2).reshape(256, PAIRS * 256)


def _uq_restore(g):
    g3 = g.reshape(256, PAIRS, 256)
    nope = g3[:, :, :128].reshape(256, HEADS, 64)
    pe = g3[:, :, 128:192].reshape(256, HEADS, 32)
    return jnp.concatenate([nope, pe], axis=2).reshape(256, HEADS * 96)


def _ukv_arrange(w):
    w3 = w.reshape(128, HEADS, 128)
    return jnp.concatenate([w3[:, :, :64].reshape(128, 1024), w3[:, :, 64:].reshape(128, 1024)], axis=1)


def _ukv_restore(g):
    kn = g[:, :1024].reshape(128, HEADS, 64)
    vv = g[:, 1024:].reshape(128, HEADS, 64)
    return jnp.concatenate([kn, vv], axis=2).reshape(128, HEADS * 128)


def _rope_tables(lp):
    r = np.arange(lp)
    pos = np.where(r < N_META, r, np.where(r >= PAD, r - PAD + N_META, 0)).astype(np.float32)
    half = MLA_ROPE // 2
    inv_freq = np.float32(ROPE_THETA) ** (-np.arange(half, dtype=np.float32) / np.float32(half))
    ang = (pos[:, None] * inv_freq[None, :]).astype(np.float32)
    cos, sin = np.cos(ang).astype(np.float32), np.sin(ang).astype(np.float32)
    one, zero = np.ones((lp, 64), np.float32), np.zeros((lp, 64), np.float32)
    return (jnp.asarray(np.concatenate([cos, cos, cos, cos, one], axis=1)),
            jnp.asarray(np.concatenate([-sin, sin, -sin, sin, zero], axis=1)))


def _pad_lanes(v, n=128):
    return jnp.pad(v, ((0, 0), (0, n - v.shape[1])))


def _in_cols(slabs, a, b):
    out = []
    for j in range(N_CHIPS):
        lo, hi = max(a, W_IN_SHARD * j), min(b, W_IN_SHARD * (j + 1))
        if lo < hi:
            out.append(slabs[j][:, lo - W_IN_SHARD * j:hi - W_IN_SHARD * j])
    return out


def _local_step(x2, tgt2, meta_f, w_small, w_attn, w_gate, w_uq_f, w_ukv_f, w_bm, w_bf, w_o, pre_norm_g,
                post_norm_g, mla_q_norm_g, mla_kv_norm_g, fox_forget_b, start_exchange=None):
    s_rows = x2.shape[0]
    lp = PAD + s_rows
    w_uq_a = _uq_arrange(w_uq_f)
    w_ukv_a = _ukv_arrange(w_ukv_f)

    ctab, stab = _rope_tables(lp)
    ii = jnp.arange(BLK)
    tri_lo = (ii[:, None] >= ii[None, :]).astype(BF16)
    tri_up = (ii[:, None] <= ii[None, :]).astype(BF16)
    fb128 = _pad_lanes(fox_forget_b)

    u = _rms_pre(x2, meta_f, pre_norm_g)
    small = _mm(u, w_small, mode="nn", out_dtype=F32, name="proj_small")
    attn = _mm(u, w_attn, mode="nn", out_dtype=BF16, name="proj_attn")
    gate = _mm(u, w_gate, mode="nn", out_dtype=BF16, name="proj_gate")
    qn, kvn, kr, ncum = _small_prep(small, mla_q_norm_g, mla_kv_norm_g, fb128, ctab, stab, tri_lo)
    qcat = _mm(qn, w_uq_a, mode="nn", out_dtype=BF16, name="mla_q", epilogue=_rope_pairs, row_ins=(ctab, stab))
    kv = _mm(kvn, w_ukv_a, mode="nn", out_dtype=BF16, name="mla_kv")
    nbrep = jnp.broadcast_to(ncum[:, :HEADS].T[:, :, None], (HEADS, lp, LANES))

    mla_cols = dict(qcol=0, kcol=lambda p: p, vcol=lambda p: PAIRS + p)
    fox_cols = dict(qcol=0, kcol=lambda p: PAIRS + p, vcol=lambda p: 2 * PAIRS + p)
    o_mla, lse_mla = _attn_fwd(qcat, kv, kv, kr=kr, scale=MLA_SCALE, name="mla_fwd", **mla_cols)
    o_fox, lse_fox = _attn_fwd(attn, attn, attn, nbrep=nbrep, scale=FOX_SCALE, name="fox_fwd", **fox_cols)

    a_mla, a_fox = _gate_fwd(o_mla, o_fox, gate)
    y_mla = _mm(a_mla, w_bm, mode="nn", out_dtype=BF16, name="br_mla")
    y_fox = _mm(a_fox, w_bf, mode="nn", out_dtype=BF16, name="br_fox")
    mg = _merge_fwd(gate, y_mla, y_fox)
    mixed = _mm(mg, w_o, mode="nn", out_dtype=F32, name="out_proj")
    dmixed, dy, loss_p, dg_post = _tail(x2, mixed, tgt2, post_norm_g)

    d_w_out = _mm(mg, dmixed, mode="tn", out_dtype=F32, name="d_w_out")
    dm = _mm(dmixed, w_o, mode="nt", out_dtype=BF16, name="d_merge")
    dy_mla, dy_fox, dgate_ab = _merge_bwd(dm, gate, y_mla, y_fox)
    d_w_bm = _mm(a_mla, dy_mla, mode="tn", out_dtype=F32, name="d_w_br_mla")
    d_w_bf = _mm(a_fox, dy_fox, mode="tn", out_dtype=F32, name="d_w_br_fox")
    da_mla = _mm(dy_mla, w_bm, mode="nt", out_dtype=BF16, name="d_a_mla")
    da_fox = _mm(dy_fox, w_bf, mode="nt", out_dtype=BF16, name="d_a_fox")
    do_mla, do_fox, dgate_z = _gate_bwd(da_mla, da_fox, o_mla, o_fox, gate)

    dq_a, dkn, dvm, dkr = _attn_bwd(qcat, kv, kv, do_mla, o_mla, lse_mla, kr=kr, rtabs=(ctab, stab), scale=MLA_SCALE,
                                    name="mla_bwd", **mla_cols)
    dfq, dfk, dfv, dcol, drow = _attn_bwd(attn, attn, attn, do_fox, o_fox, lse_fox, nbrep=nbrep, scale=FOX_SCALE,
                                          name="fox_bwd", **fox_cols)

    d_w_uq_a = _mm(qn, dq_a, mode="tn", out_dtype=F32, name="d_w_uq")
    dqn = _mm(dq_a, w_uq_a, mode="nt", out_dtype=F32, name="d_qn")
    d_w_ukv_a = jnp.concatenate([_mm(kvn, dkn, mode="tn", out_dtype=F32, name="d_w_uk"),
                                 _mm(kvn, dvm, mode="tn", out_dtype=F32, name="d_w_uv")], axis=1)
    dkvn = _mm(dkn, w_ukv_a[:, :1024], mode="nt", out_dtype=F32, name="d_kvn_k")
    dkvn = _mm(dvm, w_ukv_a[:, 1024:], mode="nt", out_dtype=F32, name="d_kvn_v", acc=dkvn)
    dsmall, dg_q, dg_kv, dfb = _small_bwd(small, dqn, dkvn, dkr, dcol, drow, mla_q_norm_g, mla_kv_norm_g,
                                          fb128, ctab, stab, tri_up)

    dw_small = _mm(u, dsmall, mode="tn", out_dtype=F32, name="d_w_small")
    dw_fq = _mm(u, dfq, mode="tn", out_dtype=F32, name="d_w_fq")
    dw_fk = _mm(u, dfk, mode="tn", out_dtype=F32, name="d_w_fk")
    dw_fv = _mm(u, dfv, mode="tn", out_dtype=F32, name="d_w_fv")
    dw_z = _mm(u, dgate_z, mode="tn", out_dtype=F32, name="d_w_z")
    dw_g = _mm(u, dgate_ab, mode="tn", out_dtype=F32, name="d_w_g")
    d_w_in = (dw_small, dw_z, dw_fq, dw_fk, dw_fv, dw_g)
    d_w_uq = _uq_restore(d_w_uq_a)
    d_w_ukv = _ukv_restore(d_w_ukv_a)
    token = start_exchange(d_w_in, d_w_uq, d_w_ukv, d_w_bm, d_w_bf, d_w_out) if start_exchange else None
    du = _mm(dsmall, w_small, mode="nt", out_dtype=F32, name="d_u_small", after=token)
    du = _mm(dfq, w_attn[:, 0:1024], mode="nt", out_dtype=F32, name="d_u_fq", acc=du)
    du = _mm(dfk, w_attn[:, 1024:2048], mode="nt", out_dtype=F32, name="d_u_fk", acc=du)
    du = _mm(dfv, w_attn[:, 2048:3072], mode="nt", out_dtype=F32, name="d_u_fv", acc=du)
    du = _mm(dgate_z, w_gate[:, 0:2048], mode="nt", out_dtype=F32, name="d_u_z", acc=du)
    du = _mm(dgate_ab, w_gate[:, 2048:4096], mode="nt", out_dtype=F32, name="d_u_g", acc=du)
    dx, dmeta, dg_pre = _pre_bwd(du, x2, meta_f, dy, pre_norm_g)
    return (loss_p, dx, dmeta, d_w_in, d_w_uq, d_w_ukv, d_w_bm, d_w_bf, d_w_out, dg_pre, dg_post, dg_q, dg_kv, dfb)


def _w_in_slabs(pieces):
    dw_small, dw_z, dw_fq, dw_fk, dw_fv, dw_g = pieces
    runs = [(dw_small[:, 0:416], C_CQ), (dw_z[:, 0:1024], C_ZMLA), (dw_fq, C_FQ), (dw_fk, C_FK), (dw_fv, C_FV),
            (dw_small[:, 512:528], C_FL), (dw_z[:, 1024:2048], C_ZFOX), (dw_g, C_GA)]
    slabs = []
    for j in range(N_CHIPS):
        lo, hi = W_IN_SHARD * j, W_IN_SHARD * (j + 1)
        cols = [a[:, max(lo, c0) - c0:min(hi, c0 + a.shape[1]) - c0] for a, c0 in runs
                if max(lo, c0) < min(hi, c0 + a.shape[1])]
        slabs.append(jnp.concatenate(cols, axis=1))
    return jnp.stack(slabs, axis=0)


def kernel(x, meta_tokens, pre_norm_g, w_in, fox_forget_b, mla_q_norm_g, mla_kv_norm_g, w_uq, w_ukv, w_br_mla, w_br_fox, w_out, post_norm_g, loss_target, m_meta_tokens, m_pre_norm_g, m_w_in, m_fox_forget_b, m_mla_q_norm_g, m_mla_kv_norm_g, m_w_uq, m_w_ukv, m_w_br_mla, m_w_br_fox, m_w_out, m_post_norm_g, v_meta_tokens, v_pre_norm_g, v_w_in, v_fox_forget_b, v_mla_q_norm_g, v_mla_kv_norm_g, v_w_uq, v_w_ukv, v_w_br_mla, v_w_br_fox, v_w_out, v_post_norm_g):
    me = 2 * lax.axis_index("x") + lax.axis_index("y")
    core = lax.axis_index("c")
    w_in_b = w_in.astype(BF16).reshape(D_MODEL, W_IN_SHARD)
    p2 = _pack_p2(w_uq[0], w_ukv[0], w_br_mla[0], w_br_fox[0], w_out[0], BF16)
    w_in_g, p2_g, meta_g = _gather_weights([w_in_b, p2], meta_tokens)
    slabs = [jnp.where(me == j, w_in_b, w_in_g[j]) for j in range(N_CHIPS)]
    pieces = [_unpack_p2(jnp.where(me == j, p2, p2_g[j])) for j in range(N_CHIPS)]
    w_uq_f = jnp.concatenate([p[0] for p in pieces], axis=1)
    w_ukv_f = jnp.concatenate([p[1] for p in pieces], axis=1)
    w_bm = jnp.concatenate([p[2] for p in pieces], axis=0)
    w_bf = jnp.concatenate([p[3] for p in pieces], axis=0)
    w_o = jnp.concatenate([p[4] for p in pieces], axis=0)
    meta_f = jnp.concatenate([jnp.where(me == j, meta_tokens, meta_g[j]) for j in range(N_CHIPS)], axis=1)
    kpe = _in_cols(slabs, C_KPE, C_ZMLA)
    w_small = jnp.concatenate(_in_cols(slabs, C_CQ, C_KPE) + kpe + kpe + [jnp.zeros((D_MODEL, 64), BF16)]
                              + _in_cols(slabs, C_FL, C_ZFOX) + [jnp.zeros((D_MODEL, 112), BF16)], axis=1)
    w_attn = jnp.concatenate(_in_cols(slabs, C_FQ, C_FL), axis=1)
    w_gate = jnp.concatenate(_in_cols(slabs, C_ZMLA, C_FQ) + _in_cols(slabs, C_ZFOX, C_END), axis=1)

    exchange = {}

    def start_exchange(d_w_in, d_w_uq, d_w_ukv, d_w_bm, d_w_bf, d_w_out):
        g2 = jnp.stack([_pack_p2(d_w_uq[:, 384 * j:384 * (j + 1)], d_w_ukv[:, 512 * j:512 * (j + 1)],
                                 d_w_bm[256 * j:256 * (j + 1)], d_w_bf[256 * j:256 * (j + 1)],
                                 d_w_out[256 * j:256 * (j + 1)], F32) for j in range(N_CHIPS)], axis=0)
        pieces = [p[None] for p in d_w_in]
        from_sib = _swap_halves(pieces + [g2])
        halves = [_add_cores(p, s, "add_cores_" + nm)[0]
                  for p, s, nm in zip(pieces, from_sib, ("small", "z", "fq", "fk", "fv", "g"))]
        parts = [_w_in_slabs(halves), _add_cores(g2, from_sib[-1], "add_cores_rest")]
        exchange.update(parts=parts, landed=_scatter_chips(parts))
        return parts[0][0, 0:16, 0:LANES]

    (loss_p, dx, dmeta, _, _, _, _, _, _, dg_pre, dg_post, dg_q, dg_kv,
     dfb) = _local_step(x[0], loss_target[0], meta_f, w_small, w_attn, w_gate, w_uq_f, w_ukv_f, w_bm, w_bf, w_o,
                        pre_norm_g, post_norm_g, mla_q_norm_g, mla_kv_norm_g, fox_forget_b, start_exchange)

    mine = [_add_chips(l, lax.dynamic_index_in_dim(p, me, 0, keepdims=False), nm)
            for l, p, nm in zip(exchange["landed"], exchange["parts"], ("add_chips_w_in", "add_chips_rest"))]
    theirs = _swap_reduced(mine)
    g_w_in, g_p2 = [jnp.concatenate([jnp.where(core == 0, a, b), jnp.where(core == 0, b, a)], axis=0)
                    for a, b in zip(mine, theirs)]
    g_w_uq, g_w_ukv, g_w_bm, g_w_bf, g_w_out = _unpack_p2(g_p2)
    g_w_in = g_w_in[None]

    vec = jnp.concatenate([dg_pre.reshape(8, 128), dg_post.reshape(8, 128), dg_q.reshape(2, 128), dg_kv,
                           dfb, _pad_lanes(loss_p), jnp.zeros((3, 128), F32), dmeta.reshape(128, 128)], axis=0)
    tot = _allreduce_small(vec)
    loss = tot[20, 0]
    g_meta = lax.dynamic_slice_in_dim(tot[24:].reshape(N_META, D_MODEL), 256 * me, 256, axis=1)

    def small_pack(pre, post, gq_, gkv_, fb_):
        return jnp.concatenate([pre.reshape(8, 128), post.reshape(8, 128), gq_.reshape(2, 128), gkv_,
                                _pad_lanes(fb_), jnp.zeros((4, 128), F32)], axis=0)

    def small_unpack(t):
        return (t[0:8].reshape(1, 1024), t[8:16].reshape(1, 1024), t[16:18].reshape(1, 256), t[18:19],
                t[19:20, 0:HEADS])

    g_small = jnp.concatenate([tot[0:20], jnp.zeros((4, 128), F32)], axis=0)
    sm = _adamw(small_pack(pre_norm_g, post_norm_g, mla_q_norm_g, mla_kv_norm_g, fox_forget_b), g_small,
                small_pack(m_pre_norm_g, m_post_norm_g, m_mla_q_norm_g, m_mla_kv_norm_g, m_fox_forget_b),
                small_pack(v_pre_norm_g, v_post_norm_g, v_mla_q_norm_g, v_mla_kv_norm_g, v_fox_forget_b),
                "adamw_small")
    g_pre, g_post, g_q, g_kv, g_fb = small_unpack(g_small)
    (d_pre, d_post, d_q, d_kv, d_fb), (nm_pre, nm_post, nm_q, nm_kv, nm_fb), (nv_pre, nv_post, nv_q, nv_kv, nv_fb) = (
        small_unpack(t) for t in sm)

    d_meta, nm_meta, nv_meta = _adamw(meta_tokens, g_meta, m_meta_tokens, v_meta_tokens, "adamw_meta")
    d_win, nm_win, nv_win = (t.T[None] for t in _adamw(w_in[0].T, g_w_in[0].T, m_w_in[0].T, v_w_in[0].T,
                                                       "adamw_w_in"))
    d_wuq, nm_wuq, nv_wuq = _adamw(w_uq[0], g_w_uq, m_w_uq[0], v_w_uq[0], "adamw_w_uq")
    d_wukv, nm_wukv, nv_wukv = _adamw(w_ukv[0], g_w_ukv, m_w_ukv[0], v_w_ukv[0], "adamw_w_ukv")
    d_wbm, nm_wbm, nv_wbm = _adamw(w_br_mla[0], g_w_bm, m_w_br_mla[0], v_w_br_mla[0], "adamw_w_br_mla")
    d_wbf, nm_wbf, nv_wbf = _adamw(w_br_fox[0], g_w_bf, m_w_br_fox[0], v_w_br_fox[0], "adamw_w_br_fox")
    d_wo, nm_wo, nv_wo = _adamw(w_out[0], g_w_out, m_w_out[0], v_w_out[0], "adamw_w_out")

    def group(meta_, pre, win, fb_, q_, kv_, wuq, wukv, wbm, wbf, wo, post):
        return (meta_, pre, win, fb_, q_, kv_, wuq[None], wukv[None], wbm[None], wbf[None], wo[None], post)

    grads = group(g_meta, g_pre, g_w_in, g_fb, g_q, g_kv, g_w_uq, g_w_ukv, g_w_bm, g_w_bf, g_w_out, g_post)
    deltas = group(d_meta, d_pre, d_win, d_fb, d_q, d_kv, d_wuq, d_wukv, d_wbm, d_wbf, d_wo, d_post)
    new_m = group(nm_meta, nm_pre, nm_win, nm_fb, nm_q, nm_kv, nm_wuq, nm_wukv, nm_wbm, nm_wbf, nm_wo, nm_post)
    new_v = group(nv_meta, nv_pre, nv_win, nv_fb, nv_q, nv_kv, nv_wuq, nv_wukv, nv_wbm, nv_wbf, nv_wo, nv_post)
    return (loss, dx[None], *grads, *deltas, *new_m, *new_v)
```

```python
import math

import jax
import jax.numpy as jnp
import numpy as np
from jax import lax
from jax.experimental import pallas as pl
from jax.experimental.pallas import tpu as pltpu
from jax.experimental.pallas import tpu_sc as plsc

F32 = jnp.float32
BF16 = jnp.bfloat16

D_MODEL = 1024
N_META = 16
RMS_EPS = 1e-6
HEADS = 16
PAIRS = HEADS // 2
HEAD_DIM = 64
LANES = 128
MLA_ROPE = 32
MLA_SCALE = 1.0 / math.sqrt(64 + 32)
FOX_SCALE = 1.0 / math.sqrt(64)
ROPE_THETA = 10000.0

PAD = 256
BLK = 256
QB = 512
UNROLL = 4
NEG = -1e30

C_CQ, C_CKV, C_KPE, C_ZMLA, C_FQ, C_FK, C_FV, C_FL, C_ZFOX, C_GA, C_GB, C_END = (
    0, 256, 384, 416, 1440, 2464, 3488, 4512, 4528, 5552, 6576, 7600)
SMALL_W = 640
W_IN_SHARD = 1900

P2_ROWS = 928
N_CHIPS = 4

ADAM_LR = 0.001
ADAM_B1 = 0.9
ADAM_B2 = 0.999
ADAM_EPS = 1e-08
ADAM_WD = 0.01
ADAM_STEP = 10

VMEM_BIG = 56 * 1024 * 1024
MM_VMEM_BUDGET = 44 * 1024 * 1024
MESH = pl.DeviceIdType.MESH


def _cp(dims, vmem=None):
    return pltpu.CompilerParams(dimension_semantics=dims, vmem_limit_bytes=vmem)


def _dot(a, b, ca, cb):
    return lax.dot_general(a, b, (((ca,), (cb,)), ((), ())), preferred_element_type=F32)


def _sigmoid(x):
    return 1.0 / (1.0 + jnp.exp(-x))


def _tile(n, cands):
    for c in cands:
        if n % c == 0:
            return c
    return n


def _mm(a, b, *, mode, out_dtype, name, acc=None, epilogue=None, row_ins=(), after=None):
    if mode == "nn":
        (M, K), N = a.shape, b.shape[1]
    elif mode == "nt":
        (M, K), N = a.shape, b.shape[0]
    else:
        (K, M), N = a.shape, b.shape[1]
    tm = _tile(M, (1088, 1024)) if M > 1024 else M
    tn = _tile(N, (1024,)) if N > 1024 else N
    nk = 1
    while True:
        tk = K // nk
        need = 2 * tk * (tm * a.dtype.itemsize + tn * b.dtype.itemsize) + tm * tn * (
            2 * jnp.dtype(out_dtype).itemsize + (8 if acc is not None else 0) + (4 if nk > 1 else 0))
        if need <= MM_VMEM_BUDGET or (tk // 2) % (16 if mode == "tn" else LANES) or tk <= 512:
            break
        nk *= 2
    ca, cb = {"nn": (1, 0), "nt": (1, 1), "tn": (0, 0)}[mode]
    a_spec = (pl.BlockSpec((tk, tm), lambda j, i, k: (k, i)) if mode == "tn"
              else pl.BlockSpec((tm, tk), lambda j, i, k: (i, k)))
    b_spec = (pl.BlockSpec((tn, tk), lambda j, i, k: (j, k)) if mode == "nt"
              else pl.BlockSpec((tk, tn), lambda j, i, k: (k, j)))
    o_spec = pl.BlockSpec((tm, tn), lambda j, i, k: (i, j))
    has_acc = acc is not None

    nrow = len(row_ins)

    def body(*refs):
        a_ref, b_ref = refs[0], refs[1]
        acc_ref = refs[2] if has_acc else None
        rows = refs[2 + has_acc:2 + has_acc + nrow]
        o_ref = refs[2 + has_acc + nrow + (after is not None)]

        def store(tile):
            if epilogue is not None:
                tile = epilogue(tile, *[r[...] for r in rows])
            o_ref[...] = tile.astype(out_dtype)

        part = _dot(a_ref[...].astype(BF16), b_ref[...].astype(BF16), ca, cb)
        if nk == 1:
            store(part + acc_ref[...] if has_acc else part)
        else:
            sc = refs[-1]
            k = pl.program_id(2)

            @pl.when(k == 0)
            def _():
                sc[...] = part + acc_ref[...] if has_acc else part

            @pl.when(k > 0)
            def _():
                sc[...] += part

            @pl.when(k == nk - 1)
            def _():
                store(sc[...])

    ins = [a, b] + ([acc] if has_acc else []) + list(row_ins)
    in_specs = ([a_spec, b_spec] + ([o_spec] if has_acc else [])
                + [pl.BlockSpec((tm, r.shape[1]), lambda j, i, k: (i, 0)) for r in row_ins])
    if after is not None:
        ins.append(after)
        in_specs.append(pl.BlockSpec(after.shape, lambda j, i, k: (0,) * after.ndim))
    return pl.pallas_call(
        body, name=name, grid=(N // tn, M // tm, nk), in_specs=in_specs, out_specs=o_spec,
        out_shape=jax.ShapeDtypeStruct((M, N), out_dtype),
        scratch_shapes=[pltpu.VMEM((tm, tn), F32)] if nk > 1 else [],
        compiler_params=_cp(("parallel", "parallel", "arbitrary"), VMEM_BIG))(*ins)


def _row(w):
    return pl.BlockSpec((BLK, w), lambda i: (i, 0))


def _rowc(w, c):
    return pl.BlockSpec((BLK, w), lambda i: (i, c))


def _full(shape):
    return pl.BlockSpec(shape, lambda i: tuple(0 for _ in shape))


def _rope(x, c, s):
    lane = lax.broadcasted_iota(jnp.int32, x.shape, 1)
    is_x1 = ((lane >> 4) & 1) == 0
    partner = jnp.where(is_x1, pltpu.roll(x, LANES - 16, 1), pltpu.roll(x, 16, 1))
    return x * c + partner * s


def _row_valid(i):
    rows = i * BLK + lax.broadcasted_iota(jnp.int32, (BLK, 1), 0)
    return (rows < N_META) | (rows >= PAD)


def _shift_rows(w):
    return pl.BlockSpec((BLK, w), lambda i: (jnp.maximum(i - 1, 0), 0))


def _h_block(i, x_ref, meta_ref):
    head = jnp.concatenate([meta_ref[...], jnp.zeros((BLK - N_META, D_MODEL), F32)], axis=0)
    return jnp.where(i == 0, head, x_ref[...])


def _rms_pre(x2, meta, g):
    lp = PAD + x2.shape[0]

    def body(x_ref, meta_ref, g_ref, u_ref):
        hv = _h_block(pl.program_id(0), x_ref, meta_ref)
        r = lax.rsqrt(jnp.mean(hv * hv, axis=-1, keepdims=True) + RMS_EPS)
        u_ref[...] = (hv * r * g_ref[...]).astype(BF16)

    return pl.pallas_call(
        body, name="rms_pre", grid=(lp // BLK,),
        in_specs=[_shift_rows(D_MODEL), _full((N_META, D_MODEL)), _full((1, D_MODEL))], out_specs=_row(D_MODEL),
        out_shape=jax.ShapeDtypeStruct((lp, D_MODEL), BF16),
        compiler_params=_cp(("parallel",)))(x2, meta, g)


def _split3(x):
    hi = x.astype(BF16)
    r1 = x - hi.astype(F32)
    mid = r1.astype(BF16)
    lo = (r1 - mid.astype(F32)).astype(BF16)
    return hi, mid, lo


def _small_prep(small, gq, gkv, fb, ctab, stab, tri):
    lp = small.shape[0]

    def body(sm_ref, gq_ref, gkv_ref, fb_ref, c_ref, s_ref, tri_ref, qn_ref, kvn_ref, kr_ref, ncum_ref, carry):
        i = pl.program_id(0)

        @pl.when(i == 0)
        def _():
            carry[...] = jnp.zeros_like(carry)

        cq = sm_ref[:, 0:256]
        r = lax.rsqrt(jnp.mean(cq * cq, axis=-1, keepdims=True) + RMS_EPS)
        qn_ref[...] = (cq * r * gq_ref[...]).astype(BF16)
        ckv = sm_ref[:, 256:384]
        r = lax.rsqrt(jnp.mean(ckv * ckv, axis=-1, keepdims=True) + RMS_EPS)
        kvn_ref[...] = (ckv * r * gkv_ref[...]).astype(BF16)
        kr_ref[...] = _rope(sm_ref[:, 384:512], c_ref[...], s_ref[...]).astype(BF16)
        fl = sm_ref[:, 512:640] + fb_ref[...]
        lf = jnp.minimum(fl, 0.0) - jnp.log(1.0 + jnp.exp(-jnp.abs(fl)))
        lf = jnp.where(_row_valid(i), lf, 0.0)
        hi, mid, lo = _split3(lf)
        t = tri_ref[...]
        cum = (_dot(t, hi, 1, 0) + _dot(t, mid, 1, 0)) + _dot(t, lo, 1, 0) + carry[...]
        ncum_ref[...] = -cum
        carry[...] = -ncum_ref[BLK - 1:BLK, :]

    return pl.pallas_call(
        body, name="small_prep", grid=(lp // BLK,),
        in_specs=[_row(SMALL_W), _full((1, 256)), _full((1, 128)), _full((1, 128)), _row(128), _row(128),
                  _full((BLK, BLK))],
        out_specs=[_row(256), _row(128), _row(128), _row(128)],
        out_shape=[jax.ShapeDtypeStruct((lp, 256), BF16), jax.ShapeDtypeStruct((lp, 128), BF16),
                   jax.ShapeDtypeStruct((lp, 128), BF16), jax.ShapeDtypeStruct((lp, 128), F32)],
        scratch_shapes=[pltpu.VMEM((1, 128), F32)],
        compiler_params=_cp(("arbitrary",)))(small, gq, gkv, fb, ctab, stab, tri)


def _rope_pairs(tile, c, s):
    out = []
    for lo in range(0, tile.shape[1], 256):
        out += [tile[:, lo:lo + 128], _rope(tile[:, lo + 128:lo + 256], c, s)]
    return jnp.concatenate(out, axis=1)


def _gate_fwd(o_mla, o_fox, gate):
    lp = o_mla.shape[0]

    def body(om_ref, of_ref, zm_ref, zf_ref, am_ref, af_ref):
        zm = zm_ref[...].astype(F32)
        am_ref[...] = (om_ref[...] * (zm * _sigmoid(zm))).astype(BF16)
        zf = zf_ref[...].astype(F32)
        af_ref[...] = (of_ref[...] * (zf * _sigmoid(zf))).astype(BF16)

    return pl.pallas_call(
        body, name="gate_fwd", grid=(lp // BLK,),
        in_specs=[_row(D_MODEL), _row(D_MODEL), _rowc(D_MODEL, 0), _rowc(D_MODEL, 1)],
        out_specs=[_row(D_MODEL), _row(D_MODEL)],
        out_shape=[jax.ShapeDtypeStruct((lp, D_MODEL), BF16)] * 2,
        compiler_params=_cp(("parallel",)))(o_mla, o_fox, gate, gate)


def _merge_fwd(gate, y_mla, y_fox):
    lp = y_mla.shape[0]

    def body(ga_ref, gb_ref, ym_ref, yf_ref, m_ref):
        sa = _sigmoid(ga_ref[...].astype(F32))
        sb = _sigmoid(gb_ref[...].astype(F32))
        m_ref[...] = (sa * ym_ref[...] + sb * yf_ref[...]).astype(BF16)

    return pl.pallas_call(
        body, name="merge_fwd", grid=(lp // BLK,),
        in_specs=[_rowc(D_MODEL, 2), _rowc(D_MODEL, 3), _row(D_MODEL), _row(D_MODEL)],
        out_specs=_row(D_MODEL), out_shape=jax.ShapeDtypeStruct((lp, D_MODEL), BF16),
        compiler_params=_cp(("parallel",)))(gate, gate, y_mla, y_fox)


def _tail(x2, mixed, tgt, gpost):
    lp = mixed.shape[0]
    shift = _shift_rows(D_MODEL)

    def body(h_ref, mx_ref, t_ref, g_ref, dmx_ref, dy_ref, loss_ref, dg_ref):
        i = pl.program_id(0)

        @pl.when(i == 0)
        def _():
            loss_ref[...] = jnp.zeros_like(loss_ref)
            dg_ref[...] = jnp.zeros_like(dg_ref)
            dmx_ref[...] = jnp.zeros_like(dmx_ref)
            dy_ref[...] = jnp.zeros_like(dy_ref)

        @pl.when(i > 0)
        def _():
            mx = mx_ref[...]
            g = g_ref[...]
            r = lax.rsqrt(jnp.mean(mx * mx, axis=-1, keepdims=True) + RMS_EPS)
            nrm = mx * r
            e = (h_ref[...] + nrm * g) - t_ref[...]
            loss_ref[...] += jnp.sum(0.5 * jnp.sum(e * e, axis=-1, keepdims=True) * (1.0 / D_MODEL),
                                     axis=0, keepdims=True)
            dy = e * (1.0 / D_MODEL)
            dy_ref[...] = dy
            dg_ref[...] += jnp.sum(dy * nrm, axis=0, keepdims=True)
            w = dy * g
            dot = jnp.mean(w * mx, axis=-1, keepdims=True)
            dmx_ref[...] = (r * w - mx * (r * r * r * dot)).astype(BF16)

    return pl.pallas_call(
        body, name="tail", grid=(lp // BLK,),
        in_specs=[shift, _row(D_MODEL), shift, _full((1, D_MODEL))],
        out_specs=[_row(D_MODEL), _row(D_MODEL), _full((1, 1)), _full((1, D_MODEL))],
        out_shape=[jax.ShapeDtypeStruct((lp, D_MODEL), BF16), jax.ShapeDtypeStruct((lp, D_MODEL), F32),
                   jax.ShapeDtypeStruct((1, 1), F32), jax.ShapeDtypeStruct((1, D_MODEL), F32)],
        compiler_params=_cp(("arbitrary",)))(x2, mixed, tgt, gpost)


def _merge_bwd(dm, gate, y_mla, y_fox):
    lp = dm.shape[0]

    def body(dm_ref, ga_ref, gb_ref, ym_ref, yf_ref, dym_ref, dyf_ref, dg_ref):
        dm_v = dm_ref[...].astype(F32)
        sa = _sigmoid(ga_ref[...].astype(F32))
        sb = _sigmoid(gb_ref[...].astype(F32))
        dym_ref[...] = (dm_v * sa).astype(BF16)
        dyf_ref[...] = (dm_v * sb).astype(BF16)
        dg_ref[:, 0:D_MODEL] = (dm_v * ym_ref[...] * (sa * (1.0 - sa))).astype(BF16)
        dg_ref[:, D_MODEL:2 * D_MODEL] = (dm_v * yf_ref[...] * (sb * (1.0 - sb))).astype(BF16)

    return pl.pallas_call(
        body, name="merge_bwd", grid=(lp // BLK,),
        in_specs=[_row(D_MODEL), _rowc(D_MODEL, 2), _rowc(D_MODEL, 3), _row(D_MODEL), _row(D_MODEL)],
        out_specs=[_row(D_MODEL), _row(D_MODEL), _row(2 * D_MODEL)],
        out_shape=[jax.ShapeDtypeStruct((lp, D_MODEL), BF16), jax.ShapeDtypeStruct((lp, D_MODEL), BF16),
                   jax.ShapeDtypeStruct((lp, 2 * D_MODEL), BF16)],
        compiler_params=_cp(("parallel",)))(dm, gate, gate, y_mla, y_fox)


def _gate_bwd(da_mla, da_fox, o_mla, o_fox, gate):
    lp = da_mla.shape[0]

    def one(da, o, z):
        sg = _sigmoid(z)
        do = da * (z * sg)
        dz = da * o * (sg * (1.0 + z * (1.0 - sg)))
        return do.astype(BF16), dz.astype(BF16)

    def body(dam_ref, daf_ref, om_ref, of_ref, zm_ref, zf_ref, dom_ref, dof_ref, dz_ref):
        f32 = lambda r: r[...].astype(F32)
        dom_ref[...], dz_ref[:, 0:D_MODEL] = one(f32(dam_ref), f32(om_ref), f32(zm_ref))
        dof_ref[...], dz_ref[:, D_MODEL:2 * D_MODEL] = one(f32(daf_ref), f32(of_ref), f32(zf_ref))

    return pl.pallas_call(
        body, name="gate_bwd", grid=(lp // BLK,),
        in_specs=[_row(D_MODEL)] * 4 + [_rowc(D_MODEL, 0), _rowc(D_MODEL, 1)],
        out_specs=[_row(D_MODEL), _row(D_MODEL), _row(2 * D_MODEL)],
        out_shape=[jax.ShapeDtypeStruct((lp, D_MODEL), BF16), jax.ShapeDtypeStruct((lp, D_MODEL), BF16),
                   jax.ShapeDtypeStruct((lp, 2 * D_MODEL), BF16)],
        compiler_params=_cp(("parallel",)))(da_mla, da_fox, o_mla, o_fox, gate, gate)


def _small_bwd(small, dqn, dkvn, dkr, dcol_t, drow_t, gq, gkv, fb, ctab, stab, triu):
    lp = small.shape[0]
    nb = lp // BLK

    def rrow(w):
        return pl.BlockSpec((BLK, w), lambda i: (nb - 1 - i, 0))

    def body(sm_ref, dqn_ref, dkvn_ref, dkr_ref, dcol_ref, drow_ref, gq_ref, gkv_ref, fb_ref, c_ref, s_ref, tri_ref,
             ds_ref, dgq_ref, dgkv_ref, dfb_ref, carry):
        i = pl.program_id(0)

        @pl.when(i == 0)
        def _():
            carry[...] = jnp.zeros_like(carry)
            dgq_ref[...] = jnp.zeros_like(dgq_ref)
            dgkv_ref[...] = jnp.zeros_like(dgkv_ref)
            dfb_ref[...] = jnp.zeros_like(dfb_ref)

        def norm_bwd(x, dn, g, dg_ref):
            r = lax.rsqrt(jnp.mean(x * x, axis=-1, keepdims=True) + RMS_EPS)
            dg_ref[...] += jnp.sum(dn * (x * r), axis=0, keepdims=True)
            w = dn * g
            dot = jnp.mean(w * x, axis=-1, keepdims=True)
            return r * w - x * (r * r * r * dot)

        ds_ref[:, 0:256] = norm_bwd(sm_ref[:, 0:256], dqn_ref[...], gq_ref[...], dgq_ref).astype(BF16)
        ds_ref[:, 256:384] = norm_bwd(sm_ref[:, 256:384], dkvn_ref[...], gkv_ref[...], dgkv_ref).astype(BF16)

        dk = dkr_ref[0]
        for p in range(1, PAIRS):
            dk = dk + dkr_ref[p]
        dk = _rope(dk, c_ref[...], -s_ref[...])
        lane = lax.broadcasted_iota(jnp.int32, dk.shape, 1)
        dk = jnp.where(lane < MLA_ROPE, dk + pltpu.roll(dk, LANES - MLA_ROPE, 1), 0.0)
        ds_ref[:, 384:512] = dk.astype(BF16)

        dcol = dcol_ref[0]
        for p in range(1, PAIRS):
            dcol = dcol + pltpu.roll(dcol_ref[p], 2 * p, 1)
        rows16 = jnp.concatenate([drow_ref[p, h:h + 1, :] for p in range(PAIRS) for h in range(2)], axis=0)
        eye = (lax.broadcasted_iota(jnp.int32, (HEADS, LANES), 0)
               == lax.broadcasted_iota(jnp.int32, (HEADS, LANES), 1)).astype(BF16)
        drow = sum(_dot(part, eye, 0, 0) for part in _split3(rows16))
        dcr = dcol - drow
        hi, mid, lo = _split3(dcr)
        t = tri_ref[...]
        suf = (_dot(t, hi, 1, 0) + _dot(t, mid, 1, 0)) + _dot(t, lo, 1, 0) + carry[...]
        fl = sm_ref[:, 512:640] + fb_ref[...]
        dfl = jnp.where(_row_valid(nb - 1 - i), -suf * _sigmoid(-fl), 0.0)
        ds_ref[:, 512:640] = dfl.astype(BF16)
        dfb_ref[...] += jnp.sum(dfl, axis=0, keepdims=True)
        carry[...] += jnp.sum(dcr, axis=0, keepdims=True)

    return pl.pallas_call(
        body, name="small_bwd", grid=(nb,),
        in_specs=[rrow(SMALL_W), rrow(256), rrow(128),
                  pl.BlockSpec((PAIRS, BLK, 128), lambda i: (0, nb - 1 - i, 0)),
                  pl.BlockSpec((PAIRS, BLK, 128), lambda i: (0, nb - 1 - i, 0)),
                  pl.BlockSpec((PAIRS, 2, BLK), lambda i: (0, 0, nb - 1 - i)),
                  _full((1, 256)), _full((1, 128)), _full((1, 128)), rrow(128), rrow(128), _full((BLK, BLK))],
        out_specs=[rrow(SMALL_W), _full((1, 256)), _full((1, 128)), _full((1, 128))],
        out_shape=[jax.ShapeDtypeStruct((lp, SMALL_W), BF16), jax.ShapeDtypeStruct((1, 256), F32),
                   jax.ShapeDtypeStruct((1, 128), F32), jax.ShapeDtypeStruct((1, 128), F32)],
        scratch_shapes=[pltpu.VMEM((1, 128), F32)],
        compiler_params=_cp(("arbitrary",)))(small, dqn, dkvn, dkr, dcol_t, drow_t, gq, gkv, fb, ctab, stab, triu)


def _pre_bwd(du, x2, meta, dy, gpre):
    s_rows = x2.shape[0]
    lp = PAD + s_rows
    shift = _shift_rows(D_MODEL)

    def body(du_ref, x_ref, meta_ref, dy_ref, g_ref, dx_ref, dmeta_ref, dg_ref):
        i = pl.program_id(0)

        @pl.when(i == 0)
        def _():
            dg_ref[...] = jnp.zeros_like(dg_ref)

        hv = _h_block(i, x_ref, meta_ref)
        duv = du_ref[...]
        r = lax.rsqrt(jnp.mean(hv * hv, axis=-1, keepdims=True) + RMS_EPS)
        dg_ref[...] += jnp.sum(duv * (hv * r), axis=0, keepdims=True)
        w = duv * g_ref[...]
        dot = jnp.mean(w * hv, axis=-1, keepdims=True)
        dh = dy_ref[...] + (r * w - hv * (r * r * r * dot))
        dx_ref[...] = dh

        @pl.when(i == 0)
        def _():
            dmeta_ref[...] = dh[0:N_META, :]

    return pl.pallas_call(
        body, name="pre_bwd", grid=(lp // BLK,),
        in_specs=[_row(D_MODEL), shift, _full((N_META, D_MODEL)), _row(D_MODEL), _full((1, D_MODEL))],
        out_specs=[shift, _full((N_META, D_MODEL)), _full((1, D_MODEL))],
        out_shape=[jax.ShapeDtypeStruct((s_rows, D_MODEL), F32), jax.ShapeDtypeStruct((N_META, D_MODEL), F32),
                   jax.ShapeDtypeStruct((1, D_MODEL), F32)],
        compiler_params=_cp(("arbitrary",)))(du, x2, meta, dy, gpre)


def _pair_masks(rope):
    lane = lax.broadcasted_iota(jnp.int32, (1, LANES), 1)
    mas = [lane < HEAD_DIM, lane >= HEAD_DIM]
    if not rope:
        return mas, mas
    wide = lax.broadcasted_iota(jnp.int32, (1, 2 * LANES), 1)
    rope_lo = LANES + MLA_ROPE
    return mas, [(wide < HEAD_DIM) | ((wide >= LANES) & (wide < rope_lo)),
                 ((wide >= HEAD_DIM) & (wide < LANES)) | ((wide >= rope_lo) & (wide < rope_lo + MLA_ROPE))]


def _mask2(x, masks):
    return [jnp.where(m, x, jnp.zeros_like(x)) for m in masks]


def _attn_fwd(q, k, v, *, kr=None, nbrep=None, scale, qcol, kcol, vcol, name):
    lp = q.shape[0]
    nq = 1 + (lp - PAD) // QB
    rope = kr is not None
    bias = nbrep is not None
    qw = 256 if rope else 128

    def body(*refs):
        it = iter(refs)
        q_ref, k_ref, v_ref = next(it), next(it), next(it)
        kr_ref = next(it) if rope else None
        nb_ref = next(it) if bias else None
        o_ref, lse_ref = next(it), next(it)
        i = pl.program_id(1)
        r0 = pl.multiple_of(jnp.where(i == 0, 0, PAD + QB * (i - 1)), BLK)
        b0 = r0 // BLK
        mas, hmask = _pair_masks(rope)
        qh = _mask2(q_ref[pl.ds(r0, QB), :], hmask)
        if bias:
            qh = [x * scale for x in qh]

        def causal(kc, n):
            key = kc * BLK + lax.broadcasted_iota(jnp.int32, (n, QB), 0)
            return (key <= r0 + lax.broadcasted_iota(jnp.int32, (n, QB), 1)) & ((kc > 0) | (n == N_META))

        def update(kcs, carry, masks, n=BLK):
            stats, acc = carry[:4], carry[4]
            k0s = [pl.multiple_of(kc * BLK, BLK) for kc in kcs]
            kks = [k_ref[pl.ds(k0, n), :] for k0 in k0s]
            if rope:
                kks = [jnp.concatenate([kk, kr_ref[pl.ds(k0, n), :]], axis=1) for kk, k0 in zip(kks, k0s)]
            new_stats, alphas, ps = [], [], [[] for _ in kcs]
            for h in range(2):
                m_prev, l_prev = stats[2 * h], stats[2 * h + 1]
                ss = []
                for kk, k0, mask in zip(kks, k0s, masks):
                    s = _dot(kk, qh[h], 1, 1)
                    if rope:
                        s = s * scale
                    if bias:
                        nbc = nb_ref[h, pl.ds(k0, n), :]
                        s = s + jnp.concatenate([nbc] * (QB // LANES), axis=1)
                    if mask is not None:
                        s = jnp.where(mask, s, NEG)
                    ss.append(s)
                m_new = m_prev
                for s in ss:
                    m_new = jnp.maximum(m_new, jnp.max(s, axis=0, keepdims=True))
                alpha = jnp.exp(m_prev - m_new)
                l_new = alpha * l_prev
                for j, s in enumerate(ss):
                    p = jnp.exp(s - m_new)
                    l_new = l_new + jnp.sum(p, axis=0, keepdims=True)
                    ps[j].append(p.astype(BF16))
                new_stats += [m_new, l_new]
                alphas.append(alpha)
            vcat = jnp.concatenate([x for k0 in k0s for x in _mask2(v_ref[pl.ds(k0, n), :], mas)], axis=0)
            pv = _dot(vcat, jnp.concatenate([p for pj in ps for p in pj], axis=0), 0, 0)
            a_full = jnp.concatenate([jnp.broadcast_to(a, (HEAD_DIM, QB)) for a in alphas], axis=0)
            return (*new_stats, a_full * acc + pv)

        neg = jnp.full((1, QB), NEG, F32)
        zero = jnp.zeros((1, QB), F32)
        c = update([0], (neg, zero, neg, zero, jnp.zeros((LANES, QB), F32)), [causal(0, N_META)], N_META)
        n_mid = jnp.maximum(b0 - 1, 0)
        c = lax.fori_loop(0, n_mid // 4, lambda t, cr: update([4 * t + u for u in (1, 2, 3, 4)], cr, [None] * 4), c)
        c = lax.fori_loop(0, (n_mid % 4) // 2, lambda t, cr: update([n_mid - 1, n_mid], cr, [None, None]), c)
        c = update([b0, b0 + 1], c, [causal(b0, BLK), causal(b0 + 1, BLK)])
        inv =jnp.concatenate([jnp.broadcast_to(1.0 / c[1], (HEAD_DIM, QB)),
                               jnp.broadcast_to(1.0 / c[3], (HEAD_DIM, QB))], axis=0)
        o_t = (c[4] * inv).T.astype(BF16)
        lses = [c[0] + jnp.log(c[1]), c[2] + jnp.log(c[3])]
        o_ref[pl.ds(r0, BLK), :] = o_t[0:BLK]
        for h in range(2):
            lse_ref[0, h:h + 1, pl.ds(r0, BLK)] = lses[h][:, 0:BLK]

        @pl.when(i > 0)
        def _():
            r1 = pl.multiple_of(r0 + BLK, BLK)
            o_ref[pl.ds(r1, QB - BLK), :] = o_t[BLK:QB]
            for h in range(2):
                lse_ref[0, h:h + 1, pl.ds(r1, QB - BLK)] = lses[h][:, BLK:QB]

    in_specs = [pl.BlockSpec((lp, qw), lambda p, i: (0, qcol + p)),
                pl.BlockSpec((lp, 128), lambda p, i: (0, kcol(p))),
                pl.BlockSpec((lp, 128), lambda p, i: (0, vcol(p)))]
    ins = [q, k, v]
    if rope:
        in_specs.append(pl.BlockSpec((lp, 128), lambda p, i: (0, 0)))
        ins.append(kr)
    if bias:
        in_specs.append(pl.BlockSpec((2, lp, 128), lambda p, i: (p, 0, 0)))
        ins.append(nbrep)
    return pl.pallas_call(
        body, name=name, grid=(PAIRS, nq), in_specs=in_specs,
        out_specs=[pl.BlockSpec((lp, 128), lambda p, i: (0, p)),
                   pl.BlockSpec((1, 2, lp), lambda p, i: (p, 0, 0))],
        out_shape=[jax.ShapeDtypeStruct((lp, D_MODEL), BF16), jax.ShapeDtypeStruct((PAIRS, 2, lp), F32)],
        compiler_params=_cp(("parallel", "arbitrary"), VMEM_BIG))(*ins)


def _attn_bwd(q, k, v, do, o, lse, *, kr=None, rtabs=None, nbrep=None, scale, qcol, kcol, vcol, name):
    lp = q.shape[0]
    nb = lp // BLK
    rope = kr is not None
    bias = nbrep is not None
    qw = 256 if rope else 128

    def body(*refs):
        it = iter(refs)
        q_ref, k_ref, v_ref = next(it), next(it), next(it)
        kr_ref = next(it) if rope else None
        nb_ref = next(it) if bias else None
        do_ref, o_ref, lse_ref = next(it), next(it), next(it)
        ct_ref, st_ref = (next(it), next(it)) if rope else (None, None)
        dq_out, dk_ref, dv_ref = next(it), next(it), next(it)
        x_ref = next(it)
        drow_ref = next(it) if bias else None
        delta, dq_ref = next(it), next(it)
        kb = pl.program_id(1)
        mas, hmask = _pair_masks(rope)
        lane = lax.broadcasted_iota(jnp.int32, (1, LANES), 1)

        @pl.when(kb == 0)
        def _():
            dq_ref[...] = jnp.zeros_like(dq_ref)
            if bias:
                drow_ref[...] = jnp.zeros_like(drow_ref)
            sub = lax.broadcasted_iota(jnp.int32, (8, LANES), 0)
            sel = (((sub == 0) & mas[0]) | ((sub == 1) & mas[1])).astype(BF16)

            def dstep(c, carry):
                r0 = pl.multiple_of(c * BLK, BLK)
                prod = do_ref[pl.ds(r0, BLK), :].astype(F32) * o_ref[pl.ds(r0, BLK), :]
                hi, mid, lo = _split3(prod)
                delta[:, pl.ds(r0, BLK)] = (_dot(sel, hi, 1, 1) + _dot(sel, mid, 1, 1)) + _dot(sel, lo, 1, 1)
                return carry

            lax.fori_loop(0, nb, dstep, 0)

        def masked_q(q0):
            qh = _mask2(q_ref[pl.ds(q0, BLK), :], hmask)
            return [x * scale for x in qh] if bias else qh

        def key_pass(n):
            kk = k_ref[0:n, :]
            if rope:
                kk = jnp.concatenate([kk, kr_ref[0:n, :]], axis=1)
            vh = _mask2(v_ref[0:n, :], mas)
            kcat = jnp.concatenate(_mask2(kk, hmask), axis=0)
            if bias:
                kcat = kcat * scale
                nbc = [jnp.concatenate([nb_ref[h, 0:n, :], nb_ref[h, 0:n, :]], axis=1) for h in range(2)]
            diag_mask = (lax.broadcasted_iota(jnp.int32, (n, BLK), 0) <= lax.broadcasted_iota(jnp.int32, (n, BLK), 1))

            def chunk(qc, carry, mask):
                carry = list(carry)
                q0 = pl.multiple_of(qc * BLK, BLK)
                dov = do_ref[pl.ds(q0, BLK), :]
                doh = _mask2(dov, mas)
                qh = masked_q(q0)
                pbs, dss = [], []
                for h in range(2):
                    s = _dot(kk, qh[h], 1, 1)
                    if rope:
                        s = s * scale
                    if bias:
                        s = s + nbc[h]
                    p = jnp.exp(s - lse_ref[0, h:h + 1, pl.ds(q0, BLK)])
                    if mask is not None:
                        p = jnp.where(mask, p, 0.0)
                    ds = p * (_dot(vh[h], dov, 1, 1) - delta[h:h + 1, pl.ds(q0, BLK)])
                    if bias:
                        drow_ref[0, h:h + 1, pl.ds(q0, BLK)] += jnp.sum(ds, axis=0, keepdims=True)
                        carry[2 + h] = carry[2 + h] + jnp.sum(ds, axis=1, keepdims=True)
                    else:
                        ds = ds * scale
                    pbs.append(p.astype(BF16))
                    dss.append(ds.astype(BF16))
                ds_lanes = jnp.concatenate(dss, axis=1)
                ds_rows = jnp.concatenate(dss, axis=0)
                carry[0] = carry[0] + _dot(ds_lanes, jnp.concatenate(qh, axis=0), 1, 0)
                carry[1] = carry[1] + _dot(jnp.concatenate(pbs, axis=1), jnp.concatenate(doh, axis=0), 1, 0)
                dq_ref[pl.ds(q0, BLK), :] += _dot(ds_rows, kcat, 0, 0)
                return tuple(carry)

            init = [jnp.zeros((n, qw), F32), jnp.zeros((n, LANES), F32)]
            if bias:
                init += [jnp.zeros((n, 1), F32), jnp.zeros((n, 1), F32)]
            groups = (nb - kb) // UNROLL

            def several(t, cr):
                for u in range(UNROLL):
                    cr = chunk(kb + UNROLL * t + u, cr, (diag_mask | (t > 0)) if u == 0 else None)
                return cr

            c = lax.fori_loop(0, groups, several, tuple(init))
            c = lax.fori_loop(kb + UNROLL * groups, nb, lambda qc, cr: chunk(qc, cr, diag_mask | (qc > kb)), c)

            def rows(a, dtype):
                a = a.astype(dtype)
                return a if n == BLK else jnp.concatenate([a, jnp.zeros((BLK - n, a.shape[1]), dtype)], axis=0)

            dk_ref[...] = rows(c[0][:, 0:LANES], BF16)
            dv_ref[...] = rows(c[1], BF16)
            if rope:
                x_ref[0] = rows(c[0][:, LANES:2 * LANES], F32)
            if bias:
                x_ref[0] = rows(jnp.where(lane == 0, c[2], jnp.where(lane == 1, c[3], 0.0)), F32)

        @pl.when(kb == 0)
        def _():
            key_pass(N_META)

        @pl.when(kb > 0)
        def _():
            key_pass(BLK)

        @pl.when(kb == nb - 1)
        def _():
            def fin(c, carry):
                r0 = pl.multiple_of(c * BLK, BLK)
                dq = dq_ref[pl.ds(r0, BLK), :]
                if rope:
                    back = _rope(dq[:, LANES:2 * LANES], ct_ref[pl.ds(r0, BLK), :], -st_ref[pl.ds(r0, BLK), :])
                    dq = jnp.concatenate([dq[:, 0:LANES], back], axis=1)
                dq_out[pl.ds(r0, BLK), :] = dq.astype(BF16)
                return carry

            lax.fori_loop(0, nb, fin, 0)

    in_specs = [pl.BlockSpec((lp, qw), lambda p, j: (0, qcol + p)),
                pl.BlockSpec((BLK, 128), lambda p, j: (j, kcol(p))),
                pl.BlockSpec((BLK, 128), lambda p, j: (j, vcol(p)))]
    ins = [q, k, v]
    if rope:
        in_specs.append(pl.BlockSpec((BLK, 128), lambda p, j: (j, 0)))
        ins.append(kr)
    if bias:
        in_specs.append(pl.BlockSpec((2, BLK, 128), lambda p, j: (p, j, 0)))
        ins.append(nbrep)
    in_specs += [pl.BlockSpec((lp, 128), lambda p, j: (0, p)), pl.BlockSpec((lp, 128), lambda p, j: (0, p)),
                 pl.BlockSpec((1, 2, lp), lambda p, j: (p, 0, 0))]
    ins += [do, o, lse]
    if rope:
        in_specs += [pl.BlockSpec((lp, 128), lambda p, j: (0, 0))] * 2
        ins += list(rtabs)
    out_specs = [pl.BlockSpec((lp, qw), lambda p, j: (0, p)),
                 pl.BlockSpec((BLK, 128), lambda p, j: (j, p)),
                 pl.BlockSpec((BLK, 128), lambda p, j: (j, p)),
                 pl.BlockSpec((1, BLK, 128), lambda p, j: (p, j, 0))]
    out_shape = [jax.ShapeDtypeStruct((lp, PAIRS * qw), BF16), jax.ShapeDtypeStruct((lp, D_MODEL), BF16),
                 jax.ShapeDtypeStruct((lp, D_MODEL), BF16), jax.ShapeDtypeStruct((PAIRS, lp, 128), F32)]
    if bias:
        out_specs.append(pl.BlockSpec((1, 2, lp), lambda p, j: (p, 0, 0)))
        out_shape.append(jax.ShapeDtypeStruct((PAIRS, 2, lp), F32))
    return pl.pallas_call(
        body, name=name, grid=(PAIRS, nb), in_specs=in_specs, out_specs=out_specs, out_shape=out_shape,
        scratch_shapes=[pltpu.VMEM((8, lp), F32), pltpu.VMEM((lp, qw), F32)],
        compiler_params=_cp(("parallel", "arbitrary"), VMEM_BIG))(*ins)


def _adamw(w, g, m, v, name):
    lead = w.ndim - 2
    rows, cols = w.shape[lead:]
    big = rows * cols > 512 * 1024
    tr = 128 if big and rows % 128 == 0 else rows
    tc = 256 if big and tr == rows else cols

    def body(w_ref, g_ref, m_ref, v_ref, d_ref, nm_ref, nv_ref):
        gv = g_ref[...]
        nm = ADAM_B1 * m_ref[...] + (1.0 - ADAM_B1) * gv
        nv = ADAM_B2 * v_ref[...] + (1.0 - ADAM_B2) * (gv * gv)
        m_hat = nm / (1.0 - ADAM_B1 ** ADAM_STEP)
        v_hat = nv / (1.0 - ADAM_B2 ** ADAM_STEP)
        d_ref[...] = -ADAM_LR * (m_hat / (jnp.sqrt(v_hat) + ADAM_EPS) + ADAM_WD * w_ref[...])
        nm_ref[...] = nm
        nv_ref[...] = nv

    spec = pl.BlockSpec((1,) * lead + (tr, tc), lambda i, j: (0,) * lead + (i, j))
    return pl.pallas_call(
        body, name=name, grid=(rows // tr, cols // tc), in_specs=[spec] * 4, out_specs=[spec] * 3,
        out_shape=[jax.ShapeDtypeStruct(w.shape, F32)] * 3,
        compiler_params=_cp(("parallel", "parallel"), VMEM_BIG))(w, g, m, v)


def _add_cores(g, from_sib, name):
    n, rows, cols = g.shape
    half = rows // 2
    tr = _tile(half, (256, 240))
    nt = half // tr

    def body(lo_ref, hi_ref, s_ref, o_ref):
        mine = jnp.where(lax.axis_index("c") == 0, lo_ref[0], hi_ref[0])
        o_ref[0] = (mine + s_ref[0]).astype(BF16)

    return pl.pallas_call(
        body, name=name, grid=(n, nt),
        in_specs=[pl.BlockSpec((1, tr, cols), lambda j, i: (j, i, 0)),
                  pl.BlockSpec((1, tr, cols), lambda j, i: (j, nt + i, 0)),
                  pl.BlockSpec((1, tr, cols), lambda j, i: (j, i, 0))],
        out_specs=pl.BlockSpec((1, tr, cols), lambda j, i: (j, i, 0)),
        out_shape=jax.ShapeDtypeStruct((n, half, cols), BF16),
        compiler_params=_cp(("parallel", "parallel"), VMEM_BIG))(g, g, from_sib)


def _add_chips(x, own, name):
    n, rows, cols = x.shape
    tr = _tile(rows, (256, 240))

    def body(x_ref, own_ref, o_ref):
        me = 2 * lax.axis_index("x") + lax.axis_index("y")
        v = [jnp.where(me == k, own_ref[...], x_ref[k]).astype(F32) for k in range(N_CHIPS)]
        o_ref[...] = ((v[0] + v[1]) + v[2]) + v[3]

    return pl.pallas_call(
        body, name=name, grid=(rows // tr,),
        in_specs=[pl.BlockSpec((n, tr, cols), lambda i: (0, i, 0)), pl.BlockSpec((tr, cols), lambda i: (i, 0))],
        out_specs=pl.BlockSpec((tr, cols), lambda i: (i, 0)),
        out_shape=jax.ShapeDtypeStruct((rows, cols), F32), compiler_params=_cp(("parallel",), VMEM_BIG))(x, own)


def _axes():
    return lax.axis_index("x"), lax.axis_index("y"), lax.axis_index("c")


def _other_chips(x, y):
    return [(1 - x, y), (x, 1 - y), (1 - x, 1 - y)]


ANY = pl.BlockSpec(memory_space=pl.ANY)


def _rcopy(src, dst, send_sems, recv_sems, k, to):
    return pltpu.make_async_remote_copy(src_ref=src, dst_ref=dst, send_sem=send_sems.at[k], recv_sem=recv_sems.at[k],
                                        device_id=to, device_id_type=MESH)


def _gather_weights(shards, meta):
    n = len(shards)

    def body(*refs):
        srcs, meta_ref = refs[:n], refs[n]
        outs, mout_ref = refs[n + 1:2 * n + 1], refs[2 * n + 1]
        send_sems, recv_sems = refs[2 * n + 2:]
        x, y, c = _axes()
        me = 2 * x + y
        sib = (x, y, 1 - c)
        chips = _other_chips(x, y)

        def half(t, chip_idx, cc):
            hr = shards[t].shape[0] // 2
            return outs[t].at[chip_idx, pl.ds(cc * hr, hr), :]

        first = []
        for j, (px, py) in enumerate(chips):
            for t in range(n):
                hr = shards[t].shape[0] // 2
                first.append(_rcopy(srcs[t].at[pl.ds(c * hr, hr), :], half(t, me, c), send_sems, recv_sems,
                                    3 * t + j, (px, py, c)))
            first.append(_rcopy(meta_ref, mout_ref.at[me], send_sems, recv_sems, 3 * n + j, (px, py, c)))
        for cp in first:
            cp.start()
        passed = []
        for j, (px, py) in enumerate(chips):
            src_chip = 2 * px + py
            for t in range(n):
                _rcopy(half(t, src_chip, c), half(t, src_chip, c), send_sems, recv_sems, 3 * t + j, sib).wait_recv()
                fwd = _rcopy(half(t, src_chip, c), half(t, src_chip, c), send_sems, recv_sems, 3 * (n + 1 + t) + j, sib)
                fwd.start()
                passed.append(fwd)
            _rcopy(mout_ref.at[src_chip], mout_ref.at[src_chip], send_sems, recv_sems, 3 * n + j, sib).wait_recv()
        for j, (px, py) in enumerate(chips):
            src_chip = 2 * px + py
            for t in range(n):
                _rcopy(half(t, src_chip, 1 - c), half(t, src_chip, 1 - c), send_sems, recv_sems,
                       3 * (n + 1 + t) + j, sib).wait_recv()
        for cp in first + passed:
            cp.wait_send()

    nsem = 3 * (2 * n + 1)
    return pl.pallas_call(
        body, name="gather_weights", in_specs=[ANY] * (n + 1), out_specs=[ANY] * (n + 1),
        out_shape=[jax.ShapeDtypeStruct((N_CHIPS,) + s.shape, s.dtype) for s in shards]
        + [jax.ShapeDtypeStruct((N_CHIPS,) + meta.shape, meta.dtype)],
        scratch_shapes=[pltpu.SemaphoreType.DMA((nsem,)), pltpu.SemaphoreType.DMA((nsem,))])(*shards, meta)


def _swap_halves(gs):
    n = len(gs)
    ncopies = sum(g.shape[0] for g in gs)
    srcs = [jax.new_ref(g, memory_space=pltpu.MemorySpace.HBM) for g in gs]
    outs = [jax.empty_ref(jax.ShapeDtypeStruct((g.shape[0], g.shape[1] // 2, g.shape[2]), g.dtype),
                          memory_space=pltpu.MemorySpace.HBM) for g in gs]

    @pl.kernel(mesh=plsc.ScalarSubcoreMesh(axis_name="seq", num_cores=1), name="swap_halves",
               scratch_types=(pltpu.SemaphoreType.DMA((ncopies,)), pltpu.SemaphoreType.DMA((ncopies,))),
               compiler_params=pltpu.CompilerParams(collective_id=1))
    def launch(send_sems, recv_sems):
        x, y, c = _axes()
        barrier = pltpu.get_barrier_semaphore()
        pl.semaphore_signal(barrier, inc=1, device_id=(x, y, 1 - c), device_id_type=MESH)
        pl.semaphore_wait(barrier, 1)
        cps = []
        for t in range(n):
            hr = gs[t].shape[1] // 2
            for j in range(gs[t].shape[0]):
                cps.append(_rcopy(srcs[t].at[j, pl.ds((1 - c) * hr, hr), :], outs[t].at[j], send_sems, recv_sems,
                                  len(cps), (x, y, 1 - c)))
        for cp in cps:
            cp.start()
        for cp in cps:
            cp.wait()

    launch()
    return [o[...] for o in outs]


def _scatter_chips(parts):
    n = len(parts)
    srcs = [jax.new_ref(p, memory_space=pltpu.MemorySpace.HBM) for p in parts]
    outs = [jax.empty_ref(jax.ShapeDtypeStruct(p.shape, p.dtype), memory_space=pltpu.MemorySpace.HBM) for p in parts]

    @pl.kernel(mesh=plsc.ScalarSubcoreMesh(axis_name="seq", num_cores=1), name="scatter_chips",
               scratch_types=(pltpu.SemaphoreType.DMA((3 * n,)), pltpu.SemaphoreType.DMA((3 * n,))),
               compiler_params=pltpu.CompilerParams(collective_id=0))
    def launch(send_sems, recv_sems):
        x, y, c = _axes()
        me = 2 * x + y
        chips = _other_chips(x, y)
        barrier = pltpu.get_barrier_semaphore()
        for px, py in chips:
            pl.semaphore_signal(barrier, inc=1, device_id=(px, py, c), device_id_type=MESH)
        pl.semaphore_wait(barrier, 3)
        cps = []
        for j, (px, py) in enumerate(chips):
            for t in range(n):
                cps.append(_rcopy(srcs[t].at[2 * px + py], outs[t].at[me], send_sems, recv_sems, 3 * t + j,
                                  (px, py, c)))
        for cp in cps:
            cp.start()
        for cp in cps:
            cp.wait()

    launch()
    return [o[...] for o in outs]


def _swap_reduced(rs):
    n = len(rs)

    def body(*refs):
        srcs, outs = refs[:n], refs[n:2 * n]
        send_sems, recv_sems = refs[2 * n:]
        x, y, c = _axes()
        cps = [_rcopy(srcs[t], outs[t], send_sems, recv_sems, t, (x, y, 1 - c)) for t in range(n)]
        for cp in cps:
            cp.start()
        for cp in cps:
            cp.wait()

    return pl.pallas_call(
        body, name="swap_reduced", in_specs=[ANY] * n, out_specs=[ANY] * n,
        out_shape=[jax.ShapeDtypeStruct(r.shape, r.dtype) for r in rs],
        scratch_shapes=[pltpu.SemaphoreType.DMA((n,)), pltpu.SemaphoreType.DMA((n,))])(*rs)


SMALL_ROWS = 24 + 128


def _allreduce_small(vec):
    def body(v_ref, out_ref, slots, send_sems, recv_sems):
        x, y, c = _axes()
        me = 4 * x + 2 * y + c
        slots[me] = v_ref[...]
        cps = []
        for k in range(1, 8):
            kx, ky, kc = (k >> 2) & 1, (k >> 1) & 1, k & 1
            peer = (1 - x if kx else x, 1 - y if ky else y, 1 - c if kc else c)
            cps.append(_rcopy(v_ref, slots.at[me], send_sems, recv_sems, k - 1, peer))
        for cp in cps:
            cp.start()
        for cp in cps:
            cp.wait()
        tot = slots[0]
        for k in range(1, 8):
            tot = tot + slots[k]
        out_ref[...] = tot

    return pl.pallas_call(
        body, name="allreduce_small",
        in_specs=[pl.BlockSpec(memory_space=pltpu.VMEM)], out_specs=pl.BlockSpec(memory_space=pltpu.VMEM),
        out_shape=jax.ShapeDtypeStruct((SMALL_ROWS, 128), F32),
        scratch_shapes=[pltpu.VMEM((8, SMALL_ROWS, 128), F32), pltpu.SemaphoreType.DMA((7,)),
                        pltpu.SemaphoreType.DMA((7,))])(vec)


def _pack_p2(w_uq, w_ukv, w_br_mla, w_br_fox, w_out, dtype):
    parts = [w_uq.reshape(96, D_MODEL), w_ukv.reshape(64, D_MODEL), w_br_mla, w_br_fox, w_out]
    return jnp.concatenate([p.astype(dtype) for p in parts], axis=0)


def _unpack_p2(pk):
    return pk[0:96].reshape(256, 384), pk[96:160].reshape(128, 512), pk[160:416], pk[416:672], pk[672:928]


def _uq_arrange(w):
    w3 = w.reshape(256, HEADS, 96)
    nope = w3[:, :, :64].reshape(256, PAIRS, 128)
    pe = w3[:, :, 64:].reshape(256, PAIRS, 64)
    return jnp.concatenate([nope, pe, jnp.zeros((256, PAIRS, 64), w.dtype)], axis=2).reshape(256, PAIRS * 256)


def _uq_restore(g):
    g3 = g.reshape(256, PAIRS, 256)
    nope = g3[:, :, :128].reshape(256, HEADS, 64)
    pe = g3[:, :, 128:192].reshape(256, HEADS, 32)
    return jnp.concatenate([nope, pe], axis=2).reshape(256, HEADS * 96)


def _ukv_arrange(w):
    w3 = w.reshape(128, HEADS, 128)
    return jnp.concatenate([w3[:, :, :64].reshape(128, 1024), w3[:, :, 64:].reshape(128, 1024)], axis=1)


def _ukv_restore(g):
    kn = g[:, :1024].reshape(128, HEADS, 64)
    vv = g[:, 1024:].reshape(128, HEADS, 64)
    return jnp.concatenate([kn, vv], axis=2).reshape(128, HEADS * 128)


def _rope_tables(lp):
    r = np.arange(lp)
    pos = np.where(r < N_META, r, np.where(r >= PAD, r - PAD + N_META, 0)).astype(np.float32)
    half = MLA_ROPE // 2
    inv_freq = np.float32(ROPE_THETA) ** (-np.arange(half, dtype=np.float32) / np.float32(half))
    ang = (pos[:, None] * inv_freq[None, :]).astype(np.float32)
    cos, sin = np.cos(ang).astype(np.float32), np.sin(ang).astype(np.float32)
    one, zero = np.ones((lp, 64), np.float32), np.zeros((lp, 64), np.float32)
    return (jnp.asarray(np.concatenate([cos, cos, cos, cos, one], axis=1)),
            jnp.asarray(np.concatenate([-sin, sin, -sin, sin, zero], axis=1)))


def _pad_lanes(v, n=128):
    return jnp.pad(v, ((0, 0), (0, n - v.shape[1])))


def _in_cols(slabs, a, b):
    out = []
    for j in range(N_CHIPS):
        lo, hi = max(a, W_IN_SHARD * j), min(b, W_IN_SHARD * (j + 1))
        if lo < hi:
            out.append(slabs[j][:, lo - W_IN_SHARD * j:hi - W_IN_SHARD * j])
    return out


def _local_step(x2, tgt2, meta_f, w_small, w_attn, w_gate, w_uq_f, w_ukv_f, w_bm, w_bf, w_o, pre_norm_g,
                post_norm_g, mla_q_norm_g, mla_kv_norm_g, fox_forget_b, start_exchange=None):
    s_rows = x2.shape[0]
    lp = PAD + s_rows
    w_uq_a = _uq_arrange(w_uq_f)
    w_ukv_a = _ukv_arrange(w_ukv_f)

    ctab, stab = _rope_tables(lp)
    ii = jnp.arange(BLK)
    tri_lo = (ii[:, None] >= ii[None, :]).astype(BF16)
    tri_up = (ii[:, None] <= ii[None, :]).astype(BF16)
    fb128 = _pad_lanes(fox_forget_b)

    u = _rms_pre(x2, meta_f, pre_norm_g)
    small = _mm(u, w_small, mode="nn", out_dtype=F32, name="proj_small")
    attn = _mm(u, w_attn, mode="nn", out_dtype=BF16, name="proj_attn")
    gate = _mm(u, w_gate, mode="nn", out_dtype=BF16, name="proj_gate")
    qn, kvn, kr, ncum = _small_prep(small, mla_q_norm_g, mla_kv_norm_g, fb128, ctab, stab, tri_lo)
    qcat = _mm(qn, w_uq_a, mode="nn", out_dtype=BF16, name="mla_q", epilogue=_rope_pairs, row_ins=(ctab, stab))
    kv = _mm(kvn, w_ukv_a, mode="nn", out_dtype=BF16, name="mla_kv")
    nbrep = jnp.broadcast_to(ncum[:, :HEADS].T[:, :, None], (HEADS, lp, LANES))

    mla_cols = dict(qcol=0, kcol=lambda p: p, vcol=lambda p: PAIRS + p)
    fox_cols = dict(qcol=0, kcol=lambda p: PAIRS + p, vcol=lambda p: 2 * PAIRS + p)
    o_mla, lse_mla = _attn_fwd(qcat, kv, kv, kr=kr, scale=MLA_SCALE, name="mla_fwd", **mla_cols)
    o_fox, lse_fox = _attn_fwd(attn, attn, attn, nbrep=nbrep, scale=FOX_SCALE, name="fox_fwd", **fox_cols)

    a_mla, a_fox = _gate_fwd(o_mla, o_fox, gate)
    y_mla = _mm(a_mla, w_bm, mode="nn", out_dtype=BF16, name="br_mla")
    y_fox = _mm(a_fox, w_bf, mode="nn", out_dtype=BF16, name="br_fox")
    mg = _merge_fwd(gate, y_mla, y_fox)
    mixed = _mm(mg, w_o, mode="nn", out_dtype=F32, name="out_proj")
    dmixed, dy, loss_p, dg_post = _tail(x2, mixed, tgt2, post_norm_g)

    d_w_out = _mm(mg, dmixed, mode="tn", out_dtype=F32, name="d_w_out")
    dm = _mm(dmixed, w_o, mode="nt", out_dtype=BF16, name="d_merge")
    dy_mla, dy_fox, dgate_ab = _merge_bwd(dm, gate, y_mla, y_fox)
    d_w_bm = _mm(a_mla, dy_mla, mode="tn", out_dtype=F32, name="d_w_br_mla")
    d_w_bf = _mm(a_fox, dy_fox, mode="tn", out_dtype=F32, name="d_w_br_fox")
    da_mla = _mm(dy_mla, w_bm, mode="nt", out_dtype=BF16, name="d_a_mla")
    da_fox = _mm(dy_fox, w_bf, mode="nt", out_dtype=BF16, name="d_a_fox")
    do_mla, do_fox, dgate_z = _gate_bwd(da_mla, da_fox, o_mla, o_fox, gate)

    dq_a, dkn, dvm, dkr = _attn_bwd(qcat, kv, kv, do_mla, o_mla, lse_mla, kr=kr, rtabs=(ctab, stab), scale=MLA_SCALE,
                                    name="mla_bwd", **mla_cols)
    dfq, dfk, dfv, dcol, drow = _attn_bwd(attn, attn, attn, do_fox, o_fox, lse_fox, nbrep=nbrep, scale=FOX_SCALE,
                                          name="fox_bwd", **fox_cols)

    d_w_uq_a = _mm(qn, dq_a, mode="tn", out_dtype=F32, name="d_w_uq")
    dqn = _mm(dq_a, w_uq_a, mode="nt", out_dtype=F32, name="d_qn")
    d_w_ukv_a = jnp.concatenate([_mm(kvn, dkn, mode="tn", out_dtype=F32, name="d_w_uk"),
                                 _mm(kvn, dvm, mode="tn", out_dtype=F32, name="d_w_uv")], axis=1)
    dkvn = _mm(dkn, w_ukv_a[:, :1024], mode="nt", out_dtype=F32, name="d_kvn_k")
    dkvn = _mm(dvm, w_ukv_a[:, 1024:], mode="nt", out_dtype=F32, name="d_kvn_v", acc=dkvn)
    dsmall, dg_q, dg_kv, dfb = _small_bwd(small, dqn, dkvn, dkr, dcol, drow, mla_q_norm_g, mla_kv_norm_g,
                                          fb128, ctab, stab, tri_up)

    dw_small = _mm(u, dsmall, mode="tn", out_dtype=F32, name="d_w_small")
    dw_fq = _mm(u, dfq, mode="tn", out_dtype=F32, name="d_w_fq")
    dw_fk = _mm(u, dfk, mode="tn", out_dtype=F32, name="d_w_fk")
    dw_fv = _mm(u, dfv, mode="tn", out_dtype=F32, name="d_w_fv")
    dw_z = _mm(u, dgate_z, mode="tn", out_dtype=F32, name="d_w_z")
    dw_g = _mm(u, dgate_ab, mode="tn", out_dtype=F32, name="d_w_g")
    d_w_in = (dw_small, dw_z, dw_fq, dw_fk, dw_fv, dw_g)
    d_w_uq = _uq_restore(d_w_uq_a)
    d_w_ukv = _ukv_restore(d_w_ukv_a)
    token = start_exchange(d_w_in, d_w_uq, d_w_ukv, d_w_bm, d_w_bf, d_w_out) if start_exchange else None
    du = _mm(dsmall, w_small, mode="nt", out_dtype=F32, name="d_u_small", after=dw_g[0:8, 0:LANES])
    du = _mm(dfq, w_attn[:, 0:1024], mode="nt", out_dtype=F32, name="d_u_fq", acc=du)
    du = _mm(dfk, w_attn[:, 1024:2048], mode="nt", out_dtype=F32, name="d_u_fk", acc=du, after=token)
    du = _mm(dfv, w_attn[:, 2048:3072], mode="nt", out_dtype=F32, name="d_u_fv", acc=du)
    du = _mm(dgate_z, w_gate[:, 0:2048], mode="nt", out_dtype=F32, name="d_u_z", acc=du)
    du = _mm(dgate_ab, w_gate[:, 2048:4096], mode="nt", out_dtype=F32, name="d_u_g", acc=du)
    dx, dmeta, dg_pre = _pre_bwd(du, x2, meta_f, dy, pre_norm_g)
    return (loss_p, dx, dmeta, d_w_in, d_w_uq, d_w_ukv, d_w_bm, d_w_bf, d_w_out, dg_pre, dg_post, dg_q, dg_kv, dfb)


def _w_in_slabs(pieces):
    dw_small, dw_z, dw_fq, dw_fk, dw_fv, dw_g = pieces
    runs = [(dw_small[:, 0:416], C_CQ), (dw_z[:, 0:1024], C_ZMLA), (dw_fq, C_FQ), (dw_fk, C_FK), (dw_fv, C_FV),
            (dw_small[:, 512:528], C_FL), (dw_z[:, 1024:2048], C_ZFOX), (dw_g, C_GA)]
    slabs = []
    for j in range(N_CHIPS):
        lo, hi = W_IN_SHARD * j, W_IN_SHARD * (j + 1)
        cols = [a[:, max(lo, c0) - c0:min(hi, c0 + a.shape[1]) - c0] for a, c0 in runs
                if max(lo, c0) < min(hi, c0 + a.shape[1])]
        slabs.append(jnp.concatenate(cols, axis=1))
    return jnp.stack(slabs, axis=0)


def kernel(x, meta_tokens, pre_norm_g, w_in, fox_forget_b, mla_q_norm_g, mla_kv_norm_g, w_uq, w_ukv, w_br_mla, w_br_fox, w_out, post_norm_g, loss_target, m_meta_tokens, m_pre_norm_g, m_w_in, m_fox_forget_b, m_mla_q_norm_g, m_mla_kv_norm_g, m_w_uq, m_w_ukv, m_w_br_mla, m_w_br_fox, m_w_out, m_post_norm_g, v_meta_tokens, v_pre_norm_g, v_w_in, v_fox_forget_b, v_mla_q_norm_g, v_mla_kv_norm_g, v_w_uq, v_w_ukv, v_w_br_mla, v_w_br_fox, v_w_out, v_post_norm_g):
    me = 2 * lax.axis_index("x") + lax.axis_index("y")
    core = lax.axis_index("c")
    w_in_b = w_in.astype(BF16).reshape(D_MODEL, W_IN_SHARD)
    p2 = _pack_p2(w_uq[0], w_ukv[0], w_br_mla[0], w_br_fox[0], w_out[0], BF16)
    w_in_g, p2_g, meta_g = _gather_weights([w_in_b, p2], meta_tokens)
    slabs = [jnp.where(me == j, w_in_b, w_in_g[j]) for j in range(N_CHIPS)]
    pieces = [_unpack_p2(jnp.where(me == j, p2, p2_g[j])) for j in range(N_CHIPS)]
    w_uq_f = jnp.concatenate([p[0] for p in pieces], axis=1)
    w_ukv_f = jnp.concatenate([p[1] for p in pieces], axis=1)
    w_bm = jnp.concatenate([p[2] for p in pieces], axis=0)
    w_bf = jnp.concatenate([p[3] for p in pieces], axis=0)
    w_o = jnp.concatenate([p[4] for p in pieces], axis=0)
    meta_f = jnp.concatenate([jnp.where(me == j, meta_tokens, meta_g[j]) for j in range(N_CHIPS)], axis=1)
    kpe = _in_cols(slabs, C_KPE, C_ZMLA)
    w_small = jnp.concatenate(_in_cols(slabs, C_CQ, C_KPE) + kpe + kpe + [jnp.zeros((D_MODEL, 64), BF16)]
                              + _in_cols(slabs, C_FL, C_ZFOX) + [jnp.zeros((D_MODEL, 112), BF16)], axis=1)
    w_attn = jnp.concatenate(_in_cols(slabs, C_FQ, C_FL), axis=1)
    w_gate = jnp.concatenate(_in_cols(slabs, C_ZMLA, C_FQ) + _in_cols(slabs, C_ZFOX, C_END), axis=1)

    exchange = {}

    def start_exchange(d_w_in, d_w_uq, d_w_ukv, d_w_bm, d_w_bf, d_w_out):
        g2 = jnp.stack([_pack_p2(d_w_uq[:, 384 * j:384 * (j + 1)], d_w_ukv[:, 512 * j:512 * (j + 1)],
                                 d_w_bm[256 * j:256 * (j + 1)], d_w_bf[256 * j:256 * (j + 1)],
                                 d_w_out[256 * j:256 * (j + 1)], F32) for j in range(N_CHIPS)], axis=0)
        pieces = [p[None] for p in d_w_in]
        from_sib = _swap_halves(pieces + [g2])
        halves = [_add_cores(p, s, "add_cores_" + nm)[0]
                  for p, s, nm in zip(pieces, from_sib, ("small", "z", "fq", "fk", "fv", "g"))]
        parts = [_w_in_slabs(halves), _add_cores(g2, from_sib[-1], "add_cores_rest")]
        exchange.update(parts=parts, landed=_scatter_chips(parts))
        return parts[0][0, 0:16, 0:LANES]

    (loss_p, dx, dmeta, _, _, _, _, _, _, dg_pre, dg_post, dg_q, dg_kv,
     dfb) = _local_step(x[0], loss_target[0], meta_f, w_small, w_attn, w_gate, w_uq_f, w_ukv_f, w_bm, w_bf, w_o,
                        pre_norm_g, post_norm_g, mla_q_norm_g, mla_kv_norm_g, fox_forget_b, start_exchange)

    mine = [_add_chips(l, lax.dynamic_index_in_dim(p, me, 0, keepdims=False), nm)
            for l, p, nm in zip(exchange["landed"], exchange["parts"], ("add_chips_w_in", "add_chips_rest"))]
    theirs = _swap_reduced(mine)
    g_w_in, g_p2 = [jnp.concatenate([jnp.where(core == 0, a, b), jnp.where(core == 0, b, a)], axis=0)
                    for a, b in zip(mine, theirs)]
    g_w_uq, g_w_ukv, g_w_bm, g_w_bf, g_w_out = _unpack_p2(g_p2)
    g_w_in = g_w_in[None]

    vec = jnp.concatenate([dg_pre.reshape(8, 128), dg_post.reshape(8, 128), dg_q.reshape(2, 128), dg_kv,
                           dfb, _pad_lanes(loss_p), jnp.zeros((3, 128), F32), dmeta.reshape(128, 128)], axis=0)
    tot = _allreduce_small(vec)
    loss = tot[20, 0]
    g_meta = lax.dynamic_slice_in_dim(tot[24:].reshape(N_META, D_MODEL), 256 * me, 256, axis=1)

    def small_pack(pre, post, gq_, gkv_, fb_):
        return jnp.concatenate([pre.reshape(8, 128), post.reshape(8, 128), gq_.reshape(2, 128), gkv_,
                                _pad_lanes(fb_), jnp.zeros((4, 128), F32)], axis=0)

    def small_unpack(t):
        return (t[0:8].reshape(1, 1024), t[8:16].reshape(1, 1024), t[16:18].reshape(1, 256), t[18:19],
                t[19:20, 0:HEADS])

    g_small = jnp.concatenate([tot[0:20], jnp.zeros((4, 128), F32)], axis=0)
    sm = _adamw(small_pack(pre_norm_g, post_norm_g, mla_q_norm_g, mla_kv_norm_g, fox_forget_b), g_small,
                small_pack(m_pre_norm_g, m_post_norm_g, m_mla_q_norm_g, m_mla_kv_norm_g, m_fox_forget_b),
                small_pack(v_pre_norm_g, v_post_norm_g, v_mla_q_norm_g, v_mla_kv_norm_g, v_fox_forget_b),
                "adamw_small")
    g_pre, g_post, g_q, g_kv, g_fb = small_unpack(g_small)
    (d_pre, d_post, d_q, d_kv, d_fb), (nm_pre, nm_post, nm_q, nm_kv, nm_fb), (nv_pre, nv_post, nv_q, nv_kv, nv_fb) = (
        small_unpack(t) for t in sm)

    d_meta, nm_meta, nv_meta = _adamw(meta_tokens, g_meta, m_meta_tokens, v_meta_tokens, "adamw_meta")
    d_win, nm_win, nv_win = (t.T[None] for t in _adamw(w_in[0].T, g_w_in[0].T, m_w_in[0].T, v_w_in[0].T,
                                                       "adamw_w_in"))
    d_wuq, nm_wuq, nv_wuq = _adamw(w_uq[0], g_w_uq, m_w_uq[0], v_w_uq[0], "adamw_w_uq")
    d_wukv, nm_wukv, nv_wukv = _adamw(w_ukv[0], g_w_ukv, m_w_ukv[0], v_w_ukv[0], "adamw_w_ukv")
    d_wbm, nm_wbm, nv_wbm = _adamw(w_br_mla[0], g_w_bm, m_w_br_mla[0], v_w_br_mla[0], "adamw_w_br_mla")
    d_wbf, nm_wbf, nv_wbf = _adamw(w_br_fox[0], g_w_bf, m_w_br_fox[0], v_w_br_fox[0], "adamw_w_br_fox")
    d_wo, nm_wo, nv_wo = _adamw(w_out[0], g_w_out, m_w_out[0], v_w_out[0], "adamw_w_out")

    def group(meta_, pre, win, fb_, q_, kv_, wuq, wukv, wbm, wbf, wo, post):
        return (meta_, pre, win, fb_, q_, kv_, wuq[None], wukv[None], wbm[None], wbf[None], wo[None], post)

    grads = group(g_meta, g_pre, g_w_in, g_fb, g_q, g_kv, g_w_uq, g_w_ukv, g_w_bm, g_w_bf, g_w_out, g_post)
    deltas = group(d_meta, d_pre, d_win, d_fb, d_q, d_kv, d_wuq, d_wukv, d_wbm, d_wbf, d_wo, d_post)
    new_m = group(nm_meta, nm_pre, nm_win, nm_fb, nm_q, nm_kv, nm_wuq, nm_wukv, nm_wbm, nm_wbf, nm_wo, nm_post)
    new_v = group(nv_meta, nv_pre, nv_win, nv_fb, nv_q, nv_kv, nv_wuq, nv_wukv, nv_wbm, nv_wbf, nv_wo, nv_post)
    return (loss, dx[None], *grads, *deltas, *new_m, *new_v)
```

```python
import math

import jax
import jax.numpy as jnp
import numpy as np
from jax import lax
from jax.experimental import pallas as pl
from jax.experimental.pallas import tpu as pltpu
from jax.experimental.pallas import tpu_sc as plsc

F32 = jnp.float32
BF16 = jnp.bfloat16

D_MODEL = 1024
N_META = 16
RMS_EPS = 1e-6
HEADS = 16
PAIRS = HEADS // 2
HEAD_DIM = 64
LANES = 128
MLA_ROPE = 32
MLA_SCALE = 1.0 / math.sqrt(64 + 32)
FOX_SCALE = 1.0 / math.sqrt(64)
ROPE_THETA = 10000.0

PAD = 256
BLK = 256
QB = 512
UNROLL = 4
NEG = -1e30

C_CQ, C_CKV, C_KPE, C_ZMLA, C_FQ, C_FK, C_FV, C_FL, C_ZFOX, C_GA, C_GB, C_END = (
    0, 256, 384, 416, 1440, 2464, 3488, 4512, 4528, 5552, 6576, 7600)
SMALL_W = 640
W_IN_SHARD = 1900

P2_ROWS = 928
N_CHIPS = 4

ADAM_LR = 0.001
ADAM_B1 = 0.9
ADAM_B2 = 0.999
ADAM_EPS = 1e-08
ADAM_WD = 0.01
ADAM_STEP = 10

VMEM_BIG = 56 * 1024 * 1024
MM_VMEM_BUDGET = 44 * 1024 * 1024
MESH = pl.DeviceIdType.MESH


def _cp(dims, vmem=None):
    return pltpu.CompilerParams(dimension_semantics=dims, vmem_limit_bytes=vmem)


def _dot(a, b, ca, cb):
    return lax.dot_general(a, b, (((ca,), (cb,)), ((), ())), preferred_element_type=F32)


def _sigmoid(x):
    return 1.0 / (1.0 + jnp.exp(-x))


def _tile(n, cands):
    for c in cands:
        if n % c == 0:
            return c
    return n


def _mm(a, b, *, mode, out_dtype, name, acc=None, epilogue=None, row_ins=(), after=None):
    if mode == "nn":
        (M, K), N = a.shape, b.shape[1]
    elif mode == "nt":
        (M, K), N = a.shape, b.shape[0]
    else:
        (K, M), N = a.shape, b.shape[1]
    tm = _tile(M, (1088, 1024)) if M > 1024 else M
    tn = _tile(N, (1024,)) if N > 1024 else N
    nk = 1
    while True:
        tk = K // nk
        need = 2 * tk * (tm * a.dtype.itemsize + tn * b.dtype.itemsize) + tm * tn * (
            2 * jnp.dtype(out_dtype).itemsize + (8 if acc is not None else 0) + (4 if nk > 1 else 0))
        if need <= MM_VMEM_BUDGET or (tk // 2) % (16 if mode == "tn" else LANES) or tk <= 512:
            break
        nk *= 2
    ca, cb = {"nn": (1, 0), "nt": (1, 1), "tn": (0, 0)}[mode]
    a_spec = (pl.BlockSpec((tk, tm), lambda j, i, k: (k, i)) if mode == "tn"
              else pl.BlockSpec((tm, tk), lambda j, i, k: (i, k)))
    b_spec = (pl.BlockSpec((tn, tk), lambda j, i, k: (j, k)) if mode == "nt"
              else pl.BlockSpec((tk, tn), lambda j, i, k: (k, j)))
    o_spec = pl.BlockSpec((tm, tn), lambda j, i, k: (i, j))
    has_acc = acc is not None

    nrow = len(row_ins)

    def body(*refs):
        a_ref, b_ref = refs[0], refs[1]
        acc_ref = refs[2] if has_acc else None
        rows = refs[2 + has_acc:2 + has_acc + nrow]
        o_ref = refs[2 + has_acc + nrow + (after is not None)]

        def store(tile):
            if epilogue is not None:
                tile = epilogue(tile, *[r[...] for r in rows])
            o_ref[...] = tile.astype(out_dtype)

        part = _dot(a_ref[...].astype(BF16), b_ref[...].astype(BF16), ca, cb)
        if nk == 1:
            store(part + acc_ref[...] if has_acc else part)
        else:
            sc = refs[-1]
            k = pl.program_id(2)

            @pl.when(k == 0)
            def _():
                sc[...] = part + acc_ref[...] if has_acc else part

            @pl.when(k > 0)
            def _():
                sc[...] += part

            @pl.when(k == nk - 1)
            def _():
                store(sc[...])

    ins = [a, b] + ([acc] if has_acc else []) + list(row_ins)
    in_specs = ([a_spec, b_spec] + ([o_spec] if has_acc else [])
                + [pl.BlockSpec((tm, r.shape[1]), lambda j, i, k: (i, 0)) for r in row_ins])
    if after is not None:
        ins.append(after)
        in_specs.append(pl.BlockSpec(after.shape, lambda j, i, k: (0,) * after.ndim))
    return pl.pallas_call(
        body, name=name, grid=(N // tn, M // tm, nk), in_specs=in_specs, out_specs=o_spec,
        out_shape=jax.ShapeDtypeStruct((M, N), out_dtype),
        scratch_shapes=[pltpu.VMEM((tm, tn), F32)] if nk > 1 else [],
        compiler_params=_cp(("parallel", "parallel", "arbitrary"), VMEM_BIG))(*ins)


def _mm_sum_nt(pairs, *, name, after=None):
    n = len(pairs)
    M, N = pairs[0][0].shape[0], pairs[0][1].shape[0]
    tm = _tile(M, (272,))

    def body(*refs):
        o_ref = refs[2 * n + (after is not None)]
        tot = _dot(refs[0][...].astype(BF16), refs[n][...].astype(BF16), 1, 1)
        for i in range(1, n):
            tot = tot + _dot(refs[i][...].astype(BF16), refs[n + i][...].astype(BF16), 1, 1)
        o_ref[...] = tot

    ins = [a for a, _ in pairs] + [b for _, b in pairs]
    in_specs = ([pl.BlockSpec((tm, a.shape[1]), lambda i: (i, 0)) for a, _ in pairs]
                + [pl.BlockSpec(b.shape, lambda i: (0, 0)) for _, b in pairs])
    if after is not None:
        ins.append(after)
        in_specs.append(pl.BlockSpec(after.shape, lambda i: (0,) * after.ndim))
    return pl.pallas_call(
        body, name=name, grid=(M // tm,), in_specs=in_specs, out_specs=pl.BlockSpec((tm, N), lambda i: (i, 0)),
        out_shape=jax.ShapeDtypeStruct((M, N), F32), compiler_params=_cp(("parallel",), VMEM_BIG))(*ins)


def _row(w):
    return pl.BlockSpec((BLK, w), lambda i: (i, 0))


def _rowc(w, c):
    return pl.BlockSpec((BLK, w), lambda i: (i, c))


def _full(shape):
    return pl.BlockSpec(shape, lambda i: tuple(0 for _ in shape))


def _rope(x, c, s):
    lane = lax.broadcasted_iota(jnp.int32, x.shape, 1)
    is_x1 = ((lane >> 4) & 1) == 0
    partner = jnp.where(is_x1, pltpu.roll(x, LANES - 16, 1), pltpu.roll(x, 16, 1))
    return x * c + partner * s


def _row_valid(i):
    rows = i * BLK + lax.broadcasted_iota(jnp.int32, (BLK, 1), 0)
    return (rows < N_META) | (rows >= PAD)


def _shift_rows(w):
    return pl.BlockSpec((BLK, w), lambda i: (jnp.maximum(i - 1, 0), 0))


def _h_block(i, x_ref, meta_ref):
    head = jnp.concatenate([meta_ref[...], jnp.zeros((BLK - N_META, D_MODEL), F32)], axis=0)
    return jnp.where(i == 0, head, x_ref[...])


def _rms_pre(x2, meta, g):
    lp = PAD + x2.shape[0]

    def body(x_ref, meta_ref, g_ref, u_ref):
        hv = _h_block(pl.program_id(0), x_ref, meta_ref)
        r = lax.rsqrt(jnp.mean(hv * hv, axis=-1, keepdims=True) + RMS_EPS)
        u_ref[...] = (hv * r * g_ref[...]).astype(BF16)

    return pl.pallas_call(
        body, name="rms_pre", grid=(lp // BLK,),
        in_specs=[_shift_rows(D_MODEL), _full((N_META, D_MODEL)), _full((1, D_MODEL))], out_specs=_row(D_MODEL),
        out_shape=jax.ShapeDtypeStruct((lp, D_MODEL), BF16),
        compiler_params=_cp(("parallel",)))(x2, meta, g)


def _split3(x):
    hi = x.astype(BF16)
    r1 = x - hi.astype(F32)
    mid = r1.astype(BF16)
    lo = (r1 - mid.astype(F32)).astype(BF16)
    return hi, mid, lo


def _small_prep(small, gq, gkv, fb, ctab, stab, tri):
    lp = small.shape[0]

    def body(sm_ref, gq_ref, gkv_ref, fb_ref, c_ref, s_ref, tri_ref, qn_ref, kvn_ref, kr_ref, ncum_ref, carry):
        i = pl.program_id(0)

        @pl.when(i == 0)
        def _():
            carry[...] = jnp.zeros_like(carry)

        cq = sm_ref[:, 0:256]
        r = lax.rsqrt(jnp.mean(cq * cq, axis=-1, keepdims=True) + RMS_EPS)
        qn_ref[...] = (cq * r * gq_ref[...]).astype(BF16)
        ckv = sm_ref[:, 256:384]
        r = lax.rsqrt(jnp.mean(ckv * ckv, axis=-1, keepdims=True) + RMS_EPS)
        kvn_ref[...] = (ckv * r * gkv_ref[...]).astype(BF16)
        kr_ref[...] = _rope(sm_ref[:, 384:512], c_ref[...], s_ref[...]).astype(BF16)
        fl = sm_ref[:, 512:640] + fb_ref[...]
        lf = jnp.minimum(fl, 0.0) - jnp.log(1.0 + jnp.exp(-jnp.abs(fl)))
        lf = jnp.where(_row_valid(i), lf, 0.0)
        hi, mid, lo = _split3(lf)
        t = tri_ref[...]
        cum = (_dot(t, hi, 1, 0) + _dot(t, mid, 1, 0)) + _dot(t, lo, 1, 0) + carry[...]
        ncum_ref[...] = -cum
        carry[...] = -ncum_ref[BLK - 1:BLK, :]

    return pl.pallas_call(
        body, name="small_prep", grid=(lp // BLK,),
        in_specs=[_row(SMALL_W), _full((1, 256)), _full((1, 128)), _full((1, 128)), _row(128), _row(128),
                  _full((BLK, BLK))],
        out_specs=[_row(256), _row(128), _row(128), _row(128)],
        out_shape=[jax.ShapeDtypeStruct((lp, 256), BF16), jax.ShapeDtypeStruct((lp, 128), BF16),
                   jax.ShapeDtypeStruct((lp, 128), BF16), jax.ShapeDtypeStruct((lp, 128), F32)],
        scratch_shapes=[pltpu.VMEM((1, 128), F32)],
        compiler_params=_cp(("arbitrary",)))(small, gq, gkv, fb, ctab, stab, tri)


def _rope_pairs(tile, c, s):
    out = []
    for lo in range(0, tile.shape[1], 256):
        out += [tile[:, lo:lo + 128], _rope(tile[:, lo + 128:lo + 256], c, s)]
    return jnp.concatenate(out, axis=1)


def _gate_fwd(o_mla, o_fox, gate):
    lp = o_mla.shape[0]

    def body(om_ref, of_ref, zm_ref, zf_ref, am_ref, af_ref):
        zm = zm_ref[...].astype(F32)
        am_ref[...] = (om_ref[...] * (zm * _sigmoid(zm))).astype(BF16)
        zf = zf_ref[...].astype(F32)
        af_ref[...] = (of_ref[...] * (zf * _sigmoid(zf))).astype(BF16)

    return pl.pallas_call(
        body, name="gate_fwd", grid=(lp // BLK,),
        in_specs=[_row(D_MODEL), _row(D_MODEL), _rowc(D_MODEL, 0), _rowc(D_MODEL, 1)],
        out_specs=[_row(D_MODEL), _row(D_MODEL)],
        out_shape=[jax.ShapeDtypeStruct((lp, D_MODEL), BF16)] * 2,
        compiler_params=_cp(("parallel",)))(o_mla, o_fox, gate, gate)


def _merge_fwd(gate, y_mla, y_fox):
    lp = y_mla.shape[0]

    def body(ga_ref, gb_ref, ym_ref, yf_ref, m_ref):
        sa = _sigmoid(ga_ref[...].astype(F32))
        sb = _sigmoid(gb_ref[...].astype(F32))
        m_ref[...] = (sa * ym_ref[...] + sb * yf_ref[...]).astype(BF16)

    return pl.pallas_call(
        body, name="merge_fwd", grid=(lp // BLK,),
        in_specs=[_rowc(D_MODEL, 2), _rowc(D_MODEL, 3), _row(D_MODEL), _row(D_MODEL)],
        out_specs=_row(D_MODEL), out_shape=jax.ShapeDtypeStruct((lp, D_MODEL), BF16),
        compiler_params=_cp(("parallel",)))(gate, gate, y_mla, y_fox)


def _tail(x2, mixed, tgt, gpost):
    lp = mixed.shape[0]
    shift = _shift_rows(D_MODEL)

    def body(h_ref, mx_ref, t_ref, g_ref, dmx_ref, dy_ref, loss_ref, dg_ref):
        i = pl.program_id(0)

        @pl.when(i == 0)
        def _():
            loss_ref[...] = jnp.zeros_like(loss_ref)
            dg_ref[...] = jnp.zeros_like(dg_ref)
            dmx_ref[...] = jnp.zeros_like(dmx_ref)
            dy_ref[...] = jnp.zeros_like(dy_ref)

        @pl.when(i > 0)
        def _():
            mx = mx_ref[...]
            g = g_ref[...]
            r = lax.rsqrt(jnp.mean(mx * mx, axis=-1, keepdims=True) + RMS_EPS)
            nrm = mx * r
            e = (h_ref[...] + nrm * g) - t_ref[...]
            loss_ref[...] += jnp.sum(0.5 * jnp.sum(e * e, axis=-1, keepdims=True) * (1.0 / D_MODEL),
                                     axis=0, keepdims=True)
            dy = e * (1.0 / D_MODEL)
            dy_ref[...] = dy
            dg_ref[...] += jnp.sum(dy * nrm, axis=0, keepdims=True)
            w = dy * g
            dot = jnp.mean(w * mx, axis=-1, keepdims=True)
            dmx_ref[...] = (r * w - mx * (r * r * r * dot)).astype(BF16)

    return pl.pallas_call(
        body, name="tail", grid=(lp // BLK,),
        in_specs=[shift, _row(D_MODEL), shift, _full((1, D_MODEL))],
        out_specs=[_row(D_MODEL), _row(D_MODEL), _full((1, 1)), _full((1, D_MODEL))],
        out_shape=[jax.ShapeDtypeStruct((lp, D_MODEL), BF16), jax.ShapeDtypeStruct((lp, D_MODEL), F32),
                   jax.ShapeDtypeStruct((1, 1), F32), jax.ShapeDtypeStruct((1, D_MODEL), F32)],
        compiler_params=_cp(("arbitrary",)))(x2, mixed, tgt, gpost)


def _merge_bwd(dm, gate, y_mla, y_fox):
    lp = dm.shape[0]

    def body(dm_ref, ga_ref, gb_ref, ym_ref, yf_ref, dym_ref, dyf_ref, dg_ref):
        dm_v = dm_ref[...].astype(F32)
        sa = _sigmoid(ga_ref[...].astype(F32))
        sb = _sigmoid(gb_ref[...].astype(F32))
        dym_ref[...] = (dm_v * sa).astype(BF16)
        dyf_ref[...] = (dm_v * sb).astype(BF16)
        dg_ref[:, 0:D_MODEL] = (dm_v * ym_ref[...] * (sa * (1.0 - sa))).astype(BF16)
        dg_ref[:, D_MODEL:2 * D_MODEL] = (dm_v * yf_ref[...] * (sb * (1.0 - sb))).astype(BF16)

    return pl.pallas_call(
        body, name="merge_bwd", grid=(lp // BLK,),
        in_specs=[_row(D_MODEL), _rowc(D_MODEL, 2), _rowc(D_MODEL, 3), _row(D_MODEL), _row(D_MODEL)],
        out_specs=[_row(D_MODEL), _row(D_MODEL), _row(2 * D_MODEL)],
        out_shape=[jax.ShapeDtypeStruct((lp, D_MODEL), BF16), jax.ShapeDtypeStruct((lp, D_MODEL), BF16),
                   jax.ShapeDtypeStruct((lp, 2 * D_MODEL), BF16)],
        compiler_params=_cp(("parallel",)))(dm, gate, gate, y_mla, y_fox)


def _gate_bwd(da_mla, da_fox, o_mla, o_fox, gate):
    lp = da_mla.shape[0]

    def one(da, o, z):
        sg = _sigmoid(z)
        do = da * (z * sg)
        dz = da * o * (sg * (1.0 + z * (1.0 - sg)))
        return do.astype(BF16), dz.astype(BF16)

    def body(dam_ref, daf_ref, om_ref, of_ref, zm_ref, zf_ref, dom_ref, dof_ref, dz_ref):
        f32 = lambda r: r[...].astype(F32)
        dom_ref[...], dz_ref[:, 0:D_MODEL] = one(f32(dam_ref), f32(om_ref), f32(zm_ref))
        dof_ref[...], dz_ref[:, D_MODEL:2 * D_MODEL] = one(f32(daf_ref), f32(of_ref), f32(zf_ref))

    return pl.pallas_call(
        body, name="gate_bwd", grid=(lp // BLK,),
        in_specs=[_row(D_MODEL)] * 4 + [_rowc(D_MODEL, 0), _rowc(D_MODEL, 1)],
        out_specs=[_row(D_MODEL), _row(D_MODEL), _row(2 * D_MODEL)],
        out_shape=[jax.ShapeDtypeStruct((lp, D_MODEL), BF16), jax.ShapeDtypeStruct((lp, D_MODEL), BF16),
                   jax.ShapeDtypeStruct((lp, 2 * D_MODEL), BF16)],
        compiler_params=_cp(("parallel",)))(da_mla, da_fox, o_mla, o_fox, gate, gate)


def _small_bwd(small, dqn, dkvn, dkr, dcol_t, drow_t, gq, gkv, fb, ctab, stab, triu):
    lp = small.shape[0]
    nb = lp // BLK

    def rrow(w):
        return pl.BlockSpec((BLK, w), lambda i: (nb - 1 - i, 0))

    def body(sm_ref, dqn_ref, dkvn_ref, dkr_ref, dcol_ref, drow_ref, gq_ref, gkv_ref, fb_ref, c_ref, s_ref, tri_ref,
             ds_ref, dgq_ref, dgkv_ref, dfb_ref, carry):
        i = pl.program_id(0)

        @pl.when(i == 0)
        def _():
            carry[...] = jnp.zeros_like(carry)
            dgq_ref[...] = jnp.zeros_like(dgq_ref)
            dgkv_ref[...] = jnp.zeros_like(dgkv_ref)
            dfb_ref[...] = jnp.zeros_like(dfb_ref)

        def norm_bwd(x, dn, g, dg_ref):
            r = lax.rsqrt(jnp.mean(x * x, axis=-1, keepdims=True) + RMS_EPS)
            dg_ref[...] += jnp.sum(dn * (x * r), axis=0, keepdims=True)
            w = dn * g
            dot = jnp.mean(w * x, axis=-1, keepdims=True)
            return r * w - x * (r * r * r * dot)

        ds_ref[:, 0:256] = norm_bwd(sm_ref[:, 0:256], dqn_ref[...], gq_ref[...], dgq_ref).astype(BF16)
        ds_ref[:, 256:384] = norm_bwd(sm_ref[:, 256:384], dkvn_ref[...], gkv_ref[...], dgkv_ref).astype(BF16)

        dk = dkr_ref[0]
        for p in range(1, PAIRS):
            dk = dk + dkr_ref[p]
        dk = _rope(dk, c_ref[...], -s_ref[...])
        lane = lax.broadcasted_iota(jnp.int32, dk.shape, 1)
        dk = jnp.where(lane < MLA_ROPE, dk + pltpu.roll(dk, LANES - MLA_ROPE, 1), 0.0)
        ds_ref[:, 384:512] = dk.astype(BF16)

        dcol = dcol_ref[0]
        for p in range(1, PAIRS):
            dcol = dcol + pltpu.roll(dcol_ref[p], 2 * p, 1)
        rows16 = jnp.concatenate([drow_ref[p, h:h + 1, :] for p in range(PAIRS) for h in range(2)], axis=0)
        eye = (lax.broadcasted_iota(jnp.int32, (HEADS, LANES), 0)
               == lax.broadcasted_iota(jnp.int32, (HEADS, LANES), 1)).astype(BF16)
        drow = sum(_dot(part, eye, 0, 0) for part in _split3(rows16))
        dcr = dcol - drow
        hi, mid, lo = _split3(dcr)
        t = tri_ref[...]
        suf = (_dot(t, hi, 1, 0) + _dot(t, mid, 1, 0)) + _dot(t, lo, 1, 0) + carry[...]
        fl = sm_ref[:, 512:640] + fb_ref[...]
        dfl = jnp.where(_row_valid(nb - 1 - i), -suf * _sigmoid(-fl), 0.0)
        ds_ref[:, 512:640] = dfl.astype(BF16)
        dfb_ref[...] += jnp.sum(dfl, axis=0, keepdims=True)
        carry[...] += jnp.sum(dcr, axis=0, keepdims=True)

    return pl.pallas_call(
        body, name="small_bwd", grid=(nb,),
        in_specs=[rrow(SMALL_W), rrow(256), rrow(128),
                  pl.BlockSpec((PAIRS, BLK, 128), lambda i: (0, nb - 1 - i, 0)),
                  pl.BlockSpec((PAIRS, BLK, 128), lambda i: (0, nb - 1 - i, 0)),
                  pl.BlockSpec((PAIRS, 2, BLK), lambda i: (0, 0, nb - 1 - i)),
                  _full((1, 256)), _full((1, 128)), _full((1, 128)), rrow(128), rrow(128), _full((BLK, BLK))],
        out_specs=[rrow(SMALL_W), _full((1, 256)), _full((1, 128)), _full((1, 128))],
        out_shape=[jax.ShapeDtypeStruct((lp, SMALL_W), BF16), jax.ShapeDtypeStruct((1, 256), F32),
                   jax.ShapeDtypeStruct((1, 128), F32), jax.ShapeDtypeStruct((1, 128), F32)],
        scratch_shapes=[pltpu.VMEM((1, 128), F32)],
        compiler_params=_cp(("arbitrary",)))(small, dqn, dkvn, dkr, dcol_t, drow_t, gq, gkv, fb, ctab, stab, triu)


def _pre_bwd(du, x2, meta, dy, gpre):
    s_rows = x2.shape[0]
    lp = PAD + s_rows
    shift = _shift_rows(D_MODEL)

    def body(du_ref, x_ref, meta_ref, dy_ref, g_ref, dx_ref, dmeta_ref, dg_ref):
        i = pl.program_id(0)

        @pl.when(i == 0)
        def _():
            dg_ref[...] = jnp.zeros_like(dg_ref)

        hv = _h_block(i, x_ref, meta_ref)
        duv = du_ref[...]
        r = lax.rsqrt(jnp.mean(hv * hv, axis=-1, keepdims=True) + RMS_EPS)
        dg_ref[...] += jnp.sum(duv * (hv * r), axis=0, keepdims=True)
        w = duv * g_ref[...]
        dot = jnp.mean(w * hv, axis=-1, keepdims=True)
        dh = dy_ref[...] + (r * w - hv * (r * r * r * dot))
        dx_ref[...] = dh

        @pl.when(i == 0)
        def _():
            dmeta_ref[...] = dh[0:N_META, :]

    return pl.pallas_call(
        body, name="pre_bwd", grid=(lp // BLK,),
        in_specs=[_row(D_MODEL), shift, _full((N_META, D_MODEL)), _row(D_MODEL), _full((1, D_MODEL))],
        out_specs=[shift, _full((N_META, D_MODEL)), _full((1, D_MODEL))],
        out_shape=[jax.ShapeDtypeStruct((s_rows, D_MODEL), F32), jax.ShapeDtypeStruct((N_META, D_MODEL), F32),
                   jax.ShapeDtypeStruct((1, D_MODEL), F32)],
        compiler_params=_cp(("arbitrary",)))(du, x2, meta, dy, gpre)


def _pair_masks(rope):
    lane = lax.broadcasted_iota(jnp.int32, (1, LANES), 1)
    mas = [lane < HEAD_DIM, lane >= HEAD_DIM]
    if not rope:
        return mas, mas
    wide = lax.broadcasted_iota(jnp.int32, (1, 2 * LANES), 1)
    rope_lo = LANES + MLA_ROPE
    return mas, [(wide < HEAD_DIM) | ((wide >= LANES) & (wide < rope_lo)),
                 ((wide >= HEAD_DIM) & (wide < LANES)) | ((wide >= rope_lo) & (wide < rope_lo + MLA_ROPE))]


def _mask2(x, masks):
    return [jnp.where(m, x, jnp.zeros_like(x)) for m in masks]


def _attn_fwd(q, k, v, *, kr=None, nbrep=None, scale, qcol, kcol, vcol, name):
    lp = q.shape[0]
    nq = 1 + (lp - PAD) // QB
    rope = kr is not None
    bias = nbrep is not None
    qw = 256 if rope else 128

    def body(*refs):
        it = iter(refs)
        q_ref, k_ref, v_ref = next(it), next(it), next(it)
        kr_ref = next(it) if rope else None
        nb_ref = next(it) if bias else None
        o_ref, lse_ref = next(it), next(it)
        i = pl.program_id(1)
        r0 = pl.multiple_of(jnp.where(i == 0, 0, PAD + QB * (i - 1)), BLK)
        b0 = r0 // BLK
        mas, hmask = _pair_masks(rope)
        qh = _mask2(q_ref[pl.ds(r0, QB), :], hmask)
        if bias:
            qh = [x * scale for x in qh]

        def causal(kc, n):
            key = kc * BLK + lax.broadcasted_iota(jnp.int32, (n, QB), 0)
            return (key <= r0 + lax.broadcasted_iota(jnp.int32, (n, QB), 1)) & ((kc > 0) | (n == N_META))

        def update(kcs, carry, masks, n=BLK):
            stats, acc = carry[:4], carry[4]
            k0s = [pl.multiple_of(kc * BLK, BLK) for kc in kcs]
            kks = [k_ref[pl.ds(k0, n), :] for k0 in k0s]
            if rope:
                kks = [jnp.concatenate([kk, kr_ref[pl.ds(k0, n), :]], axis=1) for kk, k0 in zip(kks, k0s)]
            new_stats, alphas, ps = [], [], [[] for _ in kcs]
            for h in range(2):
                m_prev, l_prev = stats[2 * h], stats[2 * h + 1]
                ss = []
                for kk, k0, mask in zip(kks, k0s, masks):
                    s = _dot(kk, qh[h], 1, 1)
                    if rope:
                        s = s * scale
                    if bias:
                        nbc = nb_ref[h, pl.ds(k0, n), :]
                        s = s + jnp.concatenate([nbc] * (QB // LANES), axis=1)
                    if mask is not None:
                        s = jnp.where(mask, s, NEG)
                    ss.append(s)
                m_new = m_prev
                for s in ss:
                    m_new = jnp.maximum(m_new, jnp.max(s, axis=0, keepdims=True))
                alpha = jnp.exp(m_prev - m_new)
                l_new = alpha * l_prev
                for j, s in enumerate(ss):
                    p = jnp.exp(s - m_new)
                    l_new = l_new + jnp.sum(p, axis=0, keepdims=True)
                    ps[j].append(p.astype(BF16))
                new_stats += [m_new, l_new]
                alphas.append(alpha)
            vcat = jnp.concatenate([x for k0 in k0s for x in _mask2(v_ref[pl.ds(k0, n), :], mas)], axis=0)
            pv = _dot(vcat, jnp.concatenate([p for pj in ps for p in pj], axis=0), 0, 0)
            a_full = jnp.concatenate([jnp.broadcast_to(a, (HEAD_DIM, QB)) for a in alphas], axis=0)
            return (*new_stats, a_full * acc + pv)

        neg = jnp.full((1, QB), NEG, F32)
        zero = jnp.zeros((1, QB), F32)
        c = update([0], (neg, zero, neg, zero, jnp.zeros((LANES, QB), F32)), [causal(0, N_META)], N_META)
        n_mid = jnp.maximum(b0 - 1, 0)
        c = lax.fori_loop(0, n_mid // 4, lambda t, cr: update([4 * t + u for u in (1, 2, 3, 4)], cr, [None] * 4), c)
        c = lax.fori_loop(0, (n_mid % 4) // 2, lambda t, cr: update([n_mid - 1, n_mid], cr, [None, None]), c)
        c = update([b0, b0 + 1], c, [causal(b0, BLK), causal(b0 + 1, BLK)])
        inv =jnp.concatenate([jnp.broadcast_to(1.0 / c[1], (HEAD_DIM, QB)),
                               jnp.broadcast_to(1.0 / c[3], (HEAD_DIM, QB))], axis=0)
        o_t = (c[4] * inv).T.astype(BF16)
        lses = [c[0] + jnp.log(c[1]), c[2] + jnp.log(c[3])]
        o_ref[pl.ds(r0, BLK), :] = o_t[0:BLK]
        for h in range(2):
            lse_ref[0, h:h + 1, pl.ds(r0, BLK)] = lses[h][:, 0:BLK]

        @pl.when(i > 0)
        def _():
            r1 = pl.multiple_of(r0 + BLK, BLK)
            o_ref[pl.ds(r1, QB - BLK), :] = o_t[BLK:QB]
            for h in range(2):
                lse_ref[0, h:h + 1, pl.ds(r1, QB - BLK)] = lses[h][:, BLK:QB]

    in_specs = [pl.BlockSpec((lp, qw), lambda p, i: (0, qcol + p)),
                pl.BlockSpec((lp, 128), lambda p, i: (0, kcol(p))),
                pl.BlockSpec((lp, 128), lambda p, i: (0, vcol(p)))]
    ins = [q, k, v]
    if rope:
        in_specs.append(pl.BlockSpec((lp, 128), lambda p, i: (0, 0)))
        ins.append(kr)
    if bias:
        in_specs.append(pl.BlockSpec((2, lp, 128), lambda p, i: (p, 0, 0)))
        ins.append(nbrep)
    return pl.pallas_call(
        body, name=name, grid=(PAIRS, nq), in_specs=in_specs,
        out_specs=[pl.BlockSpec((lp, 128), lambda p, i: (0, p)),
                   pl.BlockSpec((1, 2, lp), lambda p, i: (p, 0, 0))],
        out_shape=[jax.ShapeDtypeStruct((lp, D_MODEL), BF16), jax.ShapeDtypeStruct((PAIRS, 2, lp), F32)],
        compiler_params=_cp(("parallel", "arbitrary"), VMEM_BIG))(*ins)


def _attn_bwd(q, k, v, do, o, lse, *, kr=None, rtabs=None, nbrep=None, scale, qcol, kcol, vcol, name):
    lp = q.shape[0]
    nb = lp // BLK
    rope = kr is not None
    bias = nbrep is not None
    qw = 256 if rope else 128

    def body(*refs):
        it = iter(refs)
        q_ref, k_ref, v_ref = next(it), next(it), next(it)
        kr_ref = next(it) if rope else None
        nb_ref = next(it) if bias else None
        do_ref, o_ref, lse_ref = next(it), next(it), next(it)
        ct_ref, st_ref = (next(it), next(it)) if rope else (None, None)
        dq_out, dk_ref, dv_ref = next(it), next(it), next(it)
        x_ref = next(it)
        drow_ref = next(it) if bias else None
        delta, dq_ref = next(it), next(it)
        kb = pl.program_id(1)
        mas, hmask = _pair_masks(rope)
        lane = lax.broadcasted_iota(jnp.int32, (1, LANES), 1)

        @pl.when(kb == 0)
        def _():
            dq_ref[...] = jnp.zeros_like(dq_ref)
            if bias:
                drow_ref[...] = jnp.zeros_like(drow_ref)
            sub = lax.broadcasted_iota(jnp.int32, (8, LANES), 0)
            sel = (((sub == 0) & mas[0]) | ((sub == 1) & mas[1])).astype(BF16)

            def dstep(c, carry):
                r0 = pl.multiple_of(c * BLK, BLK)
                prod = do_ref[pl.ds(r0, BLK), :].astype(F32) * o_ref[pl.ds(r0, BLK), :]
                hi, mid, lo = _split3(prod)
                delta[:, pl.ds(r0, BLK)] = (_dot(sel, hi, 1, 1) + _dot(sel, mid, 1, 1)) + _dot(sel, lo, 1, 1)
                return carry

            lax.fori_loop(0, nb, dstep, 0)

        def masked_q(q0):
            qh = _mask2(q_ref[pl.ds(q0, BLK), :], hmask)
            return [x * scale for x in qh] if bias else qh

        def key_pass(n):
            kk = k_ref[0:n, :]
            if rope:
                kk = jnp.concatenate([kk, kr_ref[0:n, :]], axis=1)
            vh = _mask2(v_ref[0:n, :], mas)
            kcat = jnp.concatenate(_mask2(kk, hmask), axis=0)
            if bias:
                kcat = kcat * scale
                nbc = [jnp.concatenate([nb_ref[h, 0:n, :], nb_ref[h, 0:n, :]], axis=1) for h in range(2)]
            diag_mask = (lax.broadcasted_iota(jnp.int32, (n, BLK), 0) <= lax.broadcasted_iota(jnp.int32, (n, BLK), 1))

            def chunk(qc, carry, mask):
                carry = list(carry)
                q0 = pl.multiple_of(qc * BLK, BLK)
                dov = do_ref[pl.ds(q0, BLK), :]
                doh = _mask2(dov, mas)
                qh = masked_q(q0)
                pbs, dss = [], []
                for h in range(2):
                    s = _dot(kk, qh[h], 1, 1)
                    if rope:
                        s = s * scale
                    if bias:
                        s = s + nbc[h]
                    p = jnp.exp(s - lse_ref[0, h:h + 1, pl.ds(q0, BLK)])
                    if mask is not None:
                        p = jnp.where(mask, p, 0.0)
                    ds = p * (_dot(vh[h], dov, 1, 1) - delta[h:h + 1, pl.ds(q0, BLK)])
                    if bias:
                        drow_ref[0, h:h + 1, pl.ds(q0, BLK)] += jnp.sum(ds, axis=0, keepdims=True)
                        carry[2 + h] = carry[2 + h] + jnp.sum(ds, axis=1, keepdims=True)
                    else:
                        ds = ds * scale
                    pbs.append(p.astype(BF16))
                    dss.append(ds.astype(BF16))
                ds_lanes = jnp.concatenate(dss, axis=1)
                ds_rows = jnp.concatenate(dss, axis=0)
                carry[0] = carry[0] + _dot(ds_lanes, jnp.concatenate(qh, axis=0), 1, 0)
                carry[1] = carry[1] + _dot(jnp.concatenate(pbs, axis=1), jnp.concatenate(doh, axis=0), 1, 0)
                dq_ref[pl.ds(q0, BLK), :] += _dot(ds_rows, kcat, 0, 0)
                return tuple(carry)

            init = [jnp.zeros((n, qw), F32), jnp.zeros((n, LANES), F32)]
            if bias:
                init += [jnp.zeros((n, 1), F32), jnp.zeros((n, 1), F32)]
            groups = (nb - kb) // UNROLL

            def several(t, cr):
                for u in range(UNROLL):
                    cr = chunk(kb + UNROLL * t + u, cr, (diag_mask | (t > 0)) if u == 0 else None)
                return cr

            c = lax.fori_loop(0, groups, several, tuple(init))
            start = kb + UNROLL * groups
            pairs = (nb - start) // 2

            def two(t, cr):
                qc = start + 2 * t
                return chunk(qc + 1, chunk(qc, cr, diag_mask | (qc > kb)), None)

            c = lax.fori_loop(0, pairs, two, c)
            c = lax.fori_loop(start + 2 * pairs, nb, lambda qc, cr: chunk(qc, cr, diag_mask | (qc > kb)), c)

            def rows(a, dtype):
                a = a.astype(dtype)
                return a if n == BLK else jnp.concatenate([a, jnp.zeros((BLK - n, a.shape[1]), dtype)], axis=0)

            dk_ref[...] = rows(c[0][:, 0:LANES], BF16)
            dv_ref[...] = rows(c[1], BF16)
            if rope:
                x_ref[0] = rows(c[0][:, LANES:2 * LANES], F32)
            if bias:
                x_ref[0] = rows(jnp.where(lane == 0, c[2], jnp.where(lane == 1, c[3], 0.0)), F32)

        @pl.when(kb == 0)
        def _():
            key_pass(N_META)

        @pl.when(kb > 0)
        def _():
            key_pass(BLK)

        @pl.when(kb == nb - 1)
        def _():
            def fin(c, carry):
                r0 = pl.multiple_of(c * BLK, BLK)
                dq = dq_ref[pl.ds(r0, BLK), :]
                if rope:
                    back = _rope(dq[:, LANES:2 * LANES], ct_ref[pl.ds(r0, BLK), :], -st_ref[pl.ds(r0, BLK), :])
                    dq = jnp.concatenate([dq[:, 0:LANES], back], axis=1)
                dq_out[pl.ds(r0, BLK), :] = dq.astype(BF16)
                return carry

            lax.fori_loop(0, nb, fin, 0)

    in_specs = [pl.BlockSpec((lp, qw), lambda p, j: (0, qcol + p)),
                pl.BlockSpec((BLK, 128), lambda p, j: (j, kcol(p))),
                pl.BlockSpec((BLK, 128), lambda p, j: (j, vcol(p)))]
    ins = [q, k, v]
    if rope:
        in_specs.append(pl.BlockSpec((BLK, 128), lambda p, j: (j, 0)))
        ins.append(kr)
    if bias:
        in_specs.append(pl.BlockSpec((2, BLK, 128), lambda p, j: (p, j, 0)))
        ins.append(nbrep)
    in_specs += [pl.BlockSpec((lp, 128), lambda p, j: (0, p)), pl.BlockSpec((lp, 128), lambda p, j: (0, p)),
                 pl.BlockSpec((1, 2, lp), lambda p, j: (p, 0, 0))]
    ins += [do, o, lse]
    if rope:
        in_specs += [pl.BlockSpec((lp, 128), lambda p, j: (0, 0))] * 2
        ins += list(rtabs)
    out_specs = [pl.BlockSpec((lp, qw), lambda p, j: (0, p)),
                 pl.BlockSpec((BLK, 128), lambda p, j: (j, p)),
                 pl.BlockSpec((BLK, 128), lambda p, j: (j, p)),
                 pl.BlockSpec((1, BLK, 128), lambda p, j: (p, j, 0))]
    out_shape = [jax.ShapeDtypeStruct((lp, PAIRS * qw), BF16), jax.ShapeDtypeStruct((lp, D_MODEL), BF16),
                 jax.ShapeDtypeStruct((lp, D_MODEL), BF16), jax.ShapeDtypeStruct((PAIRS, lp, 128), F32)]
    if bias:
        out_specs.append(pl.BlockSpec((1, 2, lp), lambda p, j: (p, 0, 0)))
        out_shape.append(jax.ShapeDtypeStruct((PAIRS, 2, lp), F32))
    return pl.pallas_call(
        body, name=name, grid=(PAIRS, nb), in_specs=in_specs, out_specs=out_specs, out_shape=out_shape,
        scratch_shapes=[pltpu.VMEM((8, lp), F32), pltpu.VMEM((lp, qw), F32)],
        compiler_params=_cp(("parallel", "arbitrary"), VMEM_BIG))(*ins)


def _adamw(w, g, m, v, name):
    lead = w.ndim - 2
    rows, cols = w.shape[lead:]
    big = rows * cols > 512 * 1024
    tr = 128 if big and rows % 128 == 0 else rows
    tc = 256 if big and tr == rows else cols

    def body(w_ref, g_ref, m_ref, v_ref, d_ref, nm_ref, nv_ref):
        gv = g_ref[...]
        nm = ADAM_B1 * m_ref[...] + (1.0 - ADAM_B1) * gv
        nv = ADAM_B2 * v_ref[...] + (1.0 - ADAM_B2) * (gv * gv)
        m_hat = nm / (1.0 - ADAM_B1 ** ADAM_STEP)
        v_hat = nv / (1.0 - ADAM_B2 ** ADAM_STEP)
        d_ref[...] = -ADAM_LR * (m_hat / (jnp.sqrt(v_hat) + ADAM_EPS) + ADAM_WD * w_ref[...])
        nm_ref[...] = nm
        nv_ref[...] = nv

    spec = pl.BlockSpec((1,) * lead + (tr, tc), lambda i, j: (0,) * lead + (i, j))
    return pl.pallas_call(
        body, name=name, grid=(rows // tr, cols // tc), in_specs=[spec] * 4, out_specs=[spec] * 3,
        out_shape=[jax.ShapeDtypeStruct(w.shape, F32)] * 3,
        compiler_params=_cp(("parallel", "parallel"), VMEM_BIG))(w, g, m, v)


def _add_cores(g, from_sib, name):
    n, rows, cols = g.shape
    half = rows // 2
    tr = _tile(half, (256, 240))
    nt = half // tr

    def body(lo_ref, hi_ref, s_ref, o_ref):
        mine = jnp.where(lax.axis_index("c") == 0, lo_ref[0], hi_ref[0])
        o_ref[0] = (mine + s_ref[0]).astype(BF16)

    return pl.pallas_call(
        body, name=name, grid=(n, nt),
        in_specs=[pl.BlockSpec((1, tr, cols), lambda j, i: (j, i, 0)),
                  pl.BlockSpec((1, tr, cols), lambda j, i: (j, nt + i, 0)),
                  pl.BlockSpec((1, tr, cols), lambda j, i: (j, i, 0))],
        out_specs=pl.BlockSpec((1, tr, cols), lambda j, i: (j, i, 0)),
        out_shape=jax.ShapeDtypeStruct((n, half, cols), BF16),
        compiler_params=_cp(("parallel", "parallel"), VMEM_BIG))(g, g, from_sib)


def _add_chips(x, own, name):
    n, rows, cols = x.shape
    tr = _tile(rows, (256, 240))

    def body(x_ref, own_ref, o_ref):
        me = 2 * lax.axis_index("x") + lax.axis_index("y")
        v = [jnp.where(me == k, own_ref[...], x_ref[k]).astype(F32) for k in range(N_CHIPS)]
        o_ref[...] = ((v[0] + v[1]) + v[2]) + v[3]

    return pl.pallas_call(
        body, name=name, grid=(rows // tr,),
        in_specs=[pl.BlockSpec((n, tr, cols), lambda i: (0, i, 0)), pl.BlockSpec((tr, cols), lambda i: (i, 0))],
        out_specs=pl.BlockSpec((tr, cols), lambda i: (i, 0)),
        out_shape=jax.ShapeDtypeStruct((rows, cols), F32), compiler_params=_cp(("parallel",), VMEM_BIG))(x, own)


def _axes():
    return lax.axis_index("x"), lax.axis_index("y"), lax.axis_index("c")


def _other_chips(x, y):
    return [(1 - x, y), (x, 1 - y), (1 - x, 1 - y)]


ANY = pl.BlockSpec(memory_space=pl.ANY)


def _rcopy(src, dst, send_sems, recv_sems, k, to):
    return pltpu.make_async_remote_copy(src_ref=src, dst_ref=dst, send_sem=send_sems.at[k], recv_sem=recv_sems.at[k],
                                        device_id=to, device_id_type=MESH)


def _gather_weights(shards, meta):
    n = len(shards)

    def body(*refs):
        srcs, meta_ref = refs[:n], refs[n]
        outs, mout_ref = refs[n + 1:2 * n + 1], refs[2 * n + 1]
        send_sems, recv_sems = refs[2 * n + 2:]
        x, y, c = _axes()
        me = 2 * x + y
        sib = (x, y, 1 - c)
        chips = _other_chips(x, y)

        def half(t, chip_idx, cc):
            hr = shards[t].shape[0] // 2
            return outs[t].at[chip_idx, pl.ds(cc * hr, hr), :]

        first = []
        for j, (px, py) in enumerate(chips):
            for t in range(n):
                hr = shards[t].shape[0] // 2
                first.append(_rcopy(srcs[t].at[pl.ds(c * hr, hr), :], half(t, me, c), send_sems, recv_sems,
                                    3 * t + j, (px, py, c)))
            first.append(_rcopy(meta_ref, mout_ref.at[me], send_sems, recv_sems, 3 * n + j, (px, py, c)))
        for cp in first:
            cp.start()
        passed = []
        for j, (px, py) in enumerate(chips):
            src_chip = 2 * px + py
            for t in range(n):
                _rcopy(half(t, src_chip, c), half(t, src_chip, c), send_sems, recv_sems, 3 * t + j, sib).wait_recv()
                fwd = _rcopy(half(t, src_chip, c), half(t, src_chip, c), send_sems, recv_sems, 3 * (n + 1 + t) + j, sib)
                fwd.start()
                passed.append(fwd)
            _rcopy(mout_ref.at[src_chip], mout_ref.at[src_chip], send_sems, recv_sems, 3 * n + j, sib).wait_recv()
        for j, (px, py) in enumerate(chips):
            src_chip = 2 * px + py
            for t in range(n):
                _rcopy(half(t, src_chip, 1 - c), half(t, src_chip, 1 - c), send_sems, recv_sems,
                       3 * (n + 1 + t) + j, sib).wait_recv()
        for cp in first + passed:
            cp.wait_send()

    nsem = 3 * (2 * n + 1)
    return pl.pallas_call(
        body, name="gather_weights", in_specs=[ANY] * (n + 1), out_specs=[ANY] * (n + 1),
        out_shape=[jax.ShapeDtypeStruct((N_CHIPS,) + s.shape, s.dtype) for s in shards]
        + [jax.ShapeDtypeStruct((N_CHIPS,) + meta.shape, meta.dtype)],
        scratch_shapes=[pltpu.SemaphoreType.DMA((nsem,)), pltpu.SemaphoreType.DMA((nsem,))])(*shards, meta)


def _gather_late(shard):
    rows, cols = shard.shape
    hr = rows // 2
    src = jax.new_ref(shard, memory_space=pltpu.MemorySpace.HBM)
    out = jax.empty_ref(jax.ShapeDtypeStruct((N_CHIPS, rows, cols), shard.dtype), memory_space=pltpu.MemorySpace.HBM)

    @pl.kernel(mesh=plsc.ScalarSubcoreMesh(axis_name="seq", num_cores=1), name="gather_late",
               scratch_types=(pltpu.SemaphoreType.DMA((6,)), pltpu.SemaphoreType.DMA((6,))),
               compiler_params=pltpu.CompilerParams(collective_id=1))
    def launch(send_sems, recv_sems):
        x, y, c = _axes()
        me = 2 * x + y
        sib = (x, y, 1 - c)
        chips = _other_chips(x, y)
        barrier = pltpu.get_barrier_semaphore()
        for px, py in chips:
            pl.semaphore_signal(barrier, inc=1, device_id=(px, py, c), device_id_type=MESH)
        pl.semaphore_signal(barrier, inc=1, device_id=sib, device_id_type=MESH)
        pl.semaphore_wait(barrier, 4)

        def half(chip_idx, cc):
            return out.at[chip_idx, pl.ds(cc * hr, hr), :]

        first = [_rcopy(src.at[pl.ds(c * hr, hr), :], half(me, c), send_sems, recv_sems, j, (px, py, c))
                 for j, (px, py) in enumerate(chips)]
        for cp in first:
            cp.start()
        passed = []
        for j, (px, py) in enumerate(chips):
            land = half(2 * px + py, c)
            _rcopy(land, land, send_sems, recv_sems, j, sib).wait_recv()
            fwd = _rcopy(land, land, send_sems, recv_sems, 3 + j, sib)
            fwd.start()
            passed.append(fwd)
        for j, (px, py) in enumerate(chips):
            land = half(2 * px + py, 1 - c)
            _rcopy(land, land, send_sems, recv_sems, 3 + j, sib).wait_recv()
        for cp in first + passed:
            cp.wait_send()

    launch()
    return out[...]


def _swap_halves(gs):
    n = len(gs)
    ncopies = sum(g.shape[0] for g in gs)

    def body(*refs):
        srcs, outs = refs[:n], refs[n:2 * n]
        send_sems, recv_sems = refs[2 * n:]
        x, y, c = _axes()
        cps = []
        for t in range(n):
            hr = gs[t].shape[1] // 2
            for j in range(gs[t].shape[0]):
                cps.append(_rcopy(srcs[t].at[j, pl.ds((1 - c) * hr, hr), :], outs[t].at[j], send_sems, recv_sems,
                                  len(cps), (x, y, 1 - c)))
        for cp in cps:
            cp.start()
        for cp in cps:
            cp.wait()

    return pl.pallas_call(
        body, name="swap_halves", in_specs=[ANY] * n, out_specs=[ANY] * n,
        out_shape=[jax.ShapeDtypeStruct((g.shape[0], g.shape[1] // 2, g.shape[2]), g.dtype) for g in gs],
        scratch_shapes=[pltpu.SemaphoreType.DMA((ncopies,)), pltpu.SemaphoreType.DMA((ncopies,))])(*gs)


def _scatter_chips(parts):
    n = len(parts)
    srcs = [jax.new_ref(p, memory_space=pltpu.MemorySpace.HBM) for p in parts]
    outs = [jax.empty_ref(jax.ShapeDtypeStruct(p.shape, p.dtype), memory_space=pltpu.MemorySpace.HBM) for p in parts]

    @pl.kernel(mesh=plsc.ScalarSubcoreMesh(axis_name="seq", num_cores=1), name="scatter_chips",
               scratch_types=(pltpu.SemaphoreType.DMA((3 * n,)), pltpu.SemaphoreType.DMA((3 * n,))),
               compiler_params=pltpu.CompilerParams(collective_id=0))
    def launch(send_sems, recv_sems):
        x, y, c = _axes()
        me = 2 * x + y
        chips = _other_chips(x, y)
        barrier = pltpu.get_barrier_semaphore()
        for px, py in chips:
            pl.semaphore_signal(barrier, inc=1, device_id=(px, py, c), device_id_type=MESH)
        pl.semaphore_wait(barrier, 3)
        cps = []
        for j, (px, py) in enumerate(chips):
            for t in range(n):
                cps.append(_rcopy(srcs[t].at[2 * px + py], outs[t].at[me], send_sems, recv_sems, 3 * t + j,
                                  (px, py, c)))
        for cp in cps:
            cp.start()
        for cp in cps:
            cp.wait()

    launch()
    return [o[...] for o in outs]


def _swap_reduced(rs):
    n = len(rs)

    def body(*refs):
        srcs, outs = refs[:n], refs[n:2 * n]
        send_sems, recv_sems = refs[2 * n:]
        x, y, c = _axes()
        cps = [_rcopy(srcs[t], outs[t], send_sems, recv_sems, t, (x, y, 1 - c)) for t in range(n)]
        for cp in cps:
            cp.start()
        for cp in cps:
            cp.wait()

    return pl.pallas_call(
        body, name="swap_reduced", in_specs=[ANY] * n, out_specs=[ANY] * n,
        out_shape=[jax.ShapeDtypeStruct(r.shape, r.dtype) for r in rs],
        scratch_shapes=[pltpu.SemaphoreType.DMA((n,)), pltpu.SemaphoreType.DMA((n,))])(*rs)


SMALL_ROWS = 24 + 128


def _allreduce_small(vec):
    def body(v_ref, out_ref, slots, send_sems, recv_sems):
        x, y, c = _axes()
        me = 4 * x + 2 * y + c
        slots[me] = v_ref[...]
        cps = []
        for k in range(1, 8):
            kx, ky, kc = (k >> 2) & 1, (k >> 1) & 1, k & 1
            peer = (1 - x if kx else x, 1 - y if ky else y, 1 - c if kc else c)
            cps.append(_rcopy(v_ref, slots.at[me], send_sems, recv_sems, k - 1, peer))
        for cp in cps:
            cp.start()
        for cp in cps:
            cp.wait()
        tot = slots[0]
        for k in range(1, 8):
            tot = tot + slots[k]
        out_ref[...] = tot

    return pl.pallas_call(
        body, name="allreduce_small",
        in_specs=[pl.BlockSpec(memory_space=pltpu.VMEM)], out_specs=pl.BlockSpec(memory_space=pltpu.VMEM),
        out_shape=jax.ShapeDtypeStruct((SMALL_ROWS, 128), F32),
        scratch_shapes=[pltpu.VMEM((8, SMALL_ROWS, 128), F32), pltpu.SemaphoreType.DMA((7,)),
                        pltpu.SemaphoreType.DMA((7,))])(vec)


def _pack_p2(w_uq, w_ukv, w_br_mla, w_br_fox, w_out, dtype):
    parts = [w_uq.reshape(96, D_MODEL), w_ukv.reshape(64, D_MODEL), w_br_mla, w_br_fox, w_out]
    return jnp.concatenate([p.astype(dtype) for p in parts], axis=0)


def _unpack_p2(pk):
    return pk[0:96].reshape(256, 384), pk[96:160].reshape(128, 512), pk[160:416], pk[416:672], pk[672:928]


def _uq_arrange(w):
    w3 = w.reshape(256, HEADS, 96)
    nope = w3[:, :, :64].reshape(256, PAIRS, 128)
    pe = w3[:, :, 64:].reshape(256, PAIRS, 64)
    return jnp.concatenate([nope, pe, jnp.zeros((256, PAIRS, 64), w.dtype)], axis=2).reshape(256, PAIRS * 256)


def _uq_restore(g):
    g3 = g.reshape(256, PAIRS, 256)
    nope = g3[:, :, :128].reshape(256, HEADS, 64)
    pe = g3[:, :, 128:192].reshape(256, HEADS, 32)
    return jnp.concatenate([nope, pe], axis=2).reshape(256, HEADS * 96)


def _ukv_arrange(w):
    w3 = w.reshape(128, HEADS, 128)
    return jnp.concatenate([w3[:, :, :64].reshape(128, 1024), w3[:, :, 64:].reshape(128, 1024)], axis=1)


def _ukv_restore(g):
    kn = g[:, :1024].reshape(128, HEADS, 64)
    vv = g[:, 1024:].reshape(128, HEADS, 64)
    return jnp.concatenate([kn, vv], axis=2).reshape(128, HEADS * 128)


def _rope_tables(lp):
    r = np.arange(lp)
    pos = np.where(r < N_META, r, np.where(r >= PAD, r - PAD + N_META, 0)).astype(np.float32)
    half = MLA_ROPE // 2
    inv_freq = np.float32(ROPE_THETA) ** (-np.arange(half, dtype=np.float32) / np.float32(half))
    ang = (pos[:, None] * inv_freq[None, :]).astype(np.float32)
    cos, sin = np.cos(ang).astype(np.float32), np.sin(ang).astype(np.float32)
    one, zero = np.ones((lp, 64), np.float32), np.zeros((lp, 64), np.float32)
    return (jnp.asarray(np.concatenate([cos, cos, cos, cos, one], axis=1)),
            jnp.asarray(np.concatenate([-sin, sin, -sin, sin, zero], axis=1)))


def _pad_lanes(v, n=128):
    return jnp.pad(v, ((0, 0), (0, n - v.shape[1])))


def _in_cols(slabs, a, b):
    out = []
    for j in range(N_CHIPS):
        lo, hi = max(a, W_IN_SHARD * j), min(b, W_IN_SHARD * (j + 1))
        if lo < hi:
            out.append(slabs[j][:, lo - W_IN_SHARD * j:hi - W_IN_SHARD * j])
    return out


def _local_step(x2, tgt2, meta_f, w_small, w_attn, w_gate, w_uq_f, w_ukv_f, w_bm, w_bf, w_o, pre_norm_g,
                post_norm_g, mla_q_norm_g, mla_kv_norm_g, fox_forget_b, start_exchange=None):
    s_rows = x2.shape[0]
    lp = PAD + s_rows
    w_uq_a = _uq_arrange(w_uq_f)
    w_ukv_a = _ukv_arrange(w_ukv_f)

    ctab, stab = _rope_tables(lp)
    ii = jnp.arange(BLK)
    tri_lo = (ii[:, None] >= ii[None, :]).astype(BF16)
    tri_up = (ii[:, None] <= ii[None, :]).astype(BF16)
    fb128 = _pad_lanes(fox_forget_b)

    u = _rms_pre(x2, meta_f, pre_norm_g)
    small = _mm(u, w_small, mode="nn", out_dtype=F32, name="proj_small")
    attn = _mm(u, w_attn, mode="nn", out_dtype=BF16, name="proj_attn")
    gate = _mm(u, w_gate, mode="nn", out_dtype=BF16, name="proj_gate")
    qn, kvn, kr, ncum = _small_prep(small, mla_q_norm_g, mla_kv_norm_g, fb128, ctab, stab, tri_lo)
    qcat = _mm(qn, w_uq_a, mode="nn", out_dtype=BF16, name="mla_q", epilogue=_rope_pairs, row_ins=(ctab, stab))
    kv = _mm(kvn, w_ukv_a, mode="nn", out_dtype=BF16, name="mla_kv")
    nbrep = jnp.broadcast_to(ncum[:, :HEADS].T[:, :, None], (HEADS, lp, LANES))

    mla_cols = dict(qcol=0, kcol=lambda p: p, vcol=lambda p: PAIRS + p)
    fox_cols = dict(qcol=0, kcol=lambda p: PAIRS + p, vcol=lambda p: 2 * PAIRS + p)
    o_mla, lse_mla = _attn_fwd(qcat, kv, kv, kr=kr, scale=MLA_SCALE, name="mla_fwd", **mla_cols)
    o_fox, lse_fox = _attn_fwd(attn, attn, attn, nbrep=nbrep, scale=FOX_SCALE, name="fox_fwd", **fox_cols)

    a_mla, a_fox = _gate_fwd(o_mla, o_fox, gate)
    y_mla = _mm(a_mla, w_bm, mode="nn", out_dtype=BF16, name="br_mla")
    y_fox = _mm(a_fox, w_bf, mode="nn", out_dtype=BF16, name="br_fox")
    mg = _merge_fwd(gate, y_mla, y_fox)
    mixed = _mm(mg, w_o, mode="nn", out_dtype=F32, name="out_proj")
    dmixed, dy, loss_p, dg_post = _tail(x2, mixed, tgt2, post_norm_g)

    d_w_out = _mm(mg, dmixed, mode="tn", out_dtype=F32, name="d_w_out")
    dm = _mm(dmixed, w_o, mode="nt", out_dtype=BF16, name="d_merge")
    dy_mla, dy_fox, dgate_ab = _merge_bwd(dm, gate, y_mla, y_fox)
    d_w_bm = _mm(a_mla, dy_mla, mode="tn", out_dtype=F32, name="d_w_br_mla")
    d_w_bf = _mm(a_fox, dy_fox, mode="tn", out_dtype=F32, name="d_w_br_fox")
    da_mla = _mm(dy_mla, w_bm, mode="nt", out_dtype=BF16, name="d_a_mla")
    da_fox = _mm(dy_fox, w_bf, mode="nt", out_dtype=BF16, name="d_a_fox")
    do_mla, do_fox, dgate_z = _gate_bwd(da_mla, da_fox, o_mla, o_fox, gate)

    dq_a, dkn, dvm, dkr = _attn_bwd(qcat, kv, kv, do_mla, o_mla, lse_mla, kr=kr, rtabs=(ctab, stab), scale=MLA_SCALE,
                                    name="mla_bwd", **mla_cols)
    dfq, dfk, dfv, dcol, drow = _attn_bwd(attn, attn, attn, do_fox, o_fox, lse_fox, nbrep=nbrep, scale=FOX_SCALE,
                                          name="fox_bwd", **fox_cols)

    d_w_uq_a = _mm(qn, dq_a, mode="tn", out_dtype=F32, name="d_w_uq")
    dqn = _mm(dq_a, w_uq_a, mode="nt", out_dtype=F32, name="d_qn")
    d_w_ukv_a = jnp.concatenate([_mm(kvn, dkn, mode="tn", out_dtype=F32, name="d_w_uk"),
                                 _mm(kvn, dvm, mode="tn", out_dtype=F32, name="d_w_uv")], axis=1)
    dkvn = _mm(dkn, w_ukv_a[:, :1024], mode="nt", out_dtype=F32, name="d_kvn_k")
    dkvn = _mm(dvm, w_ukv_a[:, 1024:], mode="nt", out_dtype=F32, name="d_kvn_v", acc=dkvn)
    dsmall, dg_q, dg_kv, dfb = _small_bwd(small, dqn, dkvn, dkr, dcol, drow, mla_q_norm_g, mla_kv_norm_g,
                                          fb128, ctab, stab, tri_up)

    dw_small = _mm(u, dsmall, mode="tn", out_dtype=F32, name="d_w_small")
    dw_fq = _mm(u, dfq, mode="tn", out_dtype=F32, name="d_w_fq")
    dw_fk = _mm(u, dfk, mode="tn", out_dtype=F32, name="d_w_fk")
    dw_fv = _mm(u, dfv, mode="tn", out_dtype=F32, name="d_w_fv")
    dw_z = _mm(u, dgate_z, mode="tn", out_dtype=F32, name="d_w_z")
    dw_g = _mm(u, dgate_ab, mode="tn", out_dtype=F32, name="d_w_g")
    d_w_in = (dw_small, dw_z, dw_fq, dw_fk, dw_fv, dw_g)
    d_w_uq = _uq_restore(d_w_uq_a)
    d_w_ukv = _ukv_restore(d_w_ukv_a)
    token = start_exchange(d_w_in, d_w_uq, d_w_ukv, d_w_bm, d_w_bf, d_w_out) if start_exchange else None
    du = _mm_sum_nt([(dsmall, w_small), (dfq, w_attn[:, 0:1024]), (dfk, w_attn[:, 1024:2048]),
                     (dfv, w_attn[:, 2048:3072]), (dgate_z, w_gate[:, 0:2048]), (dgate_ab, w_gate[:, 2048:4096])],
                    name="d_u", after=token)
    dx, dmeta, dg_pre = _pre_bwd(du, x2, meta_f, dy, pre_norm_g)
    return (loss_p, dx, dmeta, d_w_in, d_w_uq, d_w_ukv, d_w_bm, d_w_bf, d_w_out, dg_pre, dg_post, dg_q, dg_kv, dfb)


def _w_in_slabs(pieces):
    dw_small, dw_z, dw_fq, dw_fk, dw_fv, dw_g = pieces
    runs = [(dw_small[:, 0:416], C_CQ), (dw_z[:, 0:1024], C_ZMLA), (dw_fq, C_FQ), (dw_fk, C_FK), (dw_fv, C_FV),
            (dw_small[:, 512:528], C_FL), (dw_z[:, 1024:2048], C_ZFOX), (dw_g, C_GA)]
    slabs = []
    for j in range(N_CHIPS):
        lo, hi = W_IN_SHARD * j, W_IN_SHARD * (j + 1)
        cols = [a[:, max(lo, c0) - c0:min(hi, c0 + a.shape[1]) - c0] for a, c0 in runs
                if max(lo, c0) < min(hi, c0 + a.shape[1])]
        slabs.append(jnp.concatenate(cols, axis=1))
    return jnp.stack(slabs, axis=0)


def kernel(x, meta_tokens, pre_norm_g, w_in, fox_forget_b, mla_q_norm_g, mla_kv_norm_g, w_uq, w_ukv, w_br_mla, w_br_fox, w_out, post_norm_g, loss_target, m_meta_tokens, m_pre_norm_g, m_w_in, m_fox_forget_b, m_mla_q_norm_g, m_mla_kv_norm_g, m_w_uq, m_w_ukv, m_w_br_mla, m_w_br_fox, m_w_out, m_post_norm_g, v_meta_tokens, v_pre_norm_g, v_w_in, v_fox_forget_b, v_mla_q_norm_g, v_mla_kv_norm_g, v_w_uq, v_w_ukv, v_w_br_mla, v_w_br_fox, v_w_out, v_post_norm_g):
    me = 2 * lax.axis_index("x") + lax.axis_index("y")
    core = lax.axis_index("c")
    w_in_b = w_in.astype(BF16).reshape(D_MODEL, W_IN_SHARD)
    p2 = _pack_p2(w_uq[0], w_ukv[0], w_br_mla[0], w_br_fox[0], w_out[0], BF16)
    w_in_g, meta_g = _gather_weights([w_in_b], meta_tokens)
    p2_g = _gather_late(lax.optimization_barrier((p2, w_in_g))[0])
    slabs = [jnp.where(me == j, w_in_b, w_in_g[j]) for j in range(N_CHIPS)]
    pieces = [_unpack_p2(jnp.where(me == j, p2, p2_g[j])) for j in range(N_CHIPS)]
    w_uq_f = jnp.concatenate([p[0] for p in pieces], axis=1)
    w_ukv_f = jnp.concatenate([p[1] for p in pieces], axis=1)
    w_bm = jnp.concatenate([p[2] for p in pieces], axis=0)
    w_bf = jnp.concatenate([p[3] for p in pieces], axis=0)
    w_o = jnp.concatenate([p[4] for p in pieces], axis=0)
    meta_f = jnp.concatenate([jnp.where(me == j, meta_tokens, meta_g[j]) for j in range(N_CHIPS)], axis=1)
    kpe = _in_cols(slabs, C_KPE, C_ZMLA)
    w_small = jnp.concatenate(_in_cols(slabs, C_CQ, C_KPE) + kpe + kpe + [jnp.zeros((D_MODEL, 64), BF16)]
                              + _in_cols(slabs, C_FL, C_ZFOX) + [jnp.zeros((D_MODEL, 112), BF16)], axis=1)
    w_attn = jnp.concatenate(_in_cols(slabs, C_FQ, C_FL), axis=1)
    w_gate = jnp.concatenate(_in_cols(slabs, C_ZMLA, C_FQ) + _in_cols(slabs, C_ZFOX, C_END), axis=1)

    exchange = {}

    def start_exchange(d_w_in, d_w_uq, d_w_ukv, d_w_bm, d_w_bf, d_w_out):
        g2 = jnp.stack([_pack_p2(d_w_uq[:, 384 * j:384 * (j + 1)], d_w_ukv[:, 512 * j:512 * (j + 1)],
                                 d_w_bm[256 * j:256 * (j + 1)], d_w_bf[256 * j:256 * (j + 1)],
                                 d_w_out[256 * j:256 * (j + 1)], F32) for j in range(N_CHIPS)], axis=0)
        pieces = [p[None] for p in d_w_in]
        from_sib = _swap_halves(pieces + [g2])
        halves = [_add_cores(p, s, "add_cores_" + nm)[0]
                  for p, s, nm in zip(pieces, from_sib, ("small", "z", "fq", "fk", "fv", "g"))]
        parts = [_w_in_slabs(halves), _add_cores(g2, from_sib[-1], "add_cores_rest")]
        exchange.update(parts=parts, landed=_scatter_chips(parts))
        return parts[0][0, 0:16, 0:LANES]

    (loss_p, dx, dmeta, _, _, _, _, _, _, dg_pre, dg_post, dg_q, dg_kv,
     dfb) = _local_step(x[0], loss_target[0], meta_f, w_small, w_attn, w_gate, w_uq_f, w_ukv_f, w_bm, w_bf, w_o,
                        pre_norm_g, post_norm_g, mla_q_norm_g, mla_kv_norm_g, fox_forget_b, start_exchange)

    mine = [_add_chips(l, lax.dynamic_index_in_dim(p, me, 0, keepdims=False), nm)
            for l, p, nm in zip(exchange["landed"], exchange["parts"], ("add_chips_w_in", "add_chips_rest"))]
    theirs = _swap_reduced(mine)
    g_w_in, g_p2 = [jnp.concatenate([jnp.where(core == 0, a, b), jnp.where(core == 0, b, a)], axis=0)
                    for a, b in zip(mine, theirs)]
    g_w_uq, g_w_ukv, g_w_bm, g_w_bf, g_w_out = _unpack_p2(g_p2)
    g_w_in = g_w_in[None]

    vec = jnp.concatenate([dg_pre.reshape(8, 128), dg_post.reshape(8, 128), dg_q.reshape(2, 128), dg_kv,
                           dfb, _pad_lanes(loss_p), jnp.zeros((3, 128), F32), dmeta.reshape(128, 128)], axis=0)
    tot = _allreduce_small(vec)
    loss = tot[20, 0]
    g_meta = lax.dynamic_slice_in_dim(tot[24:].reshape(N_META, D_MODEL), 256 * me, 256, axis=1)

    def small_pack(pre, post, gq_, gkv_, fb_):
        return jnp.concatenate([pre.reshape(8, 128), post.reshape(8, 128), gq_.reshape(2, 128), gkv_,
                                _pad_lanes(fb_), jnp.zeros((4, 128), F32)], axis=0)

    def small_unpack(t):
        return (t[0:8].reshape(1, 1024), t[8:16].reshape(1, 1024), t[16:18].reshape(1, 256), t[18:19],
                t[19:20, 0:HEADS])

    g_small = jnp.concatenate([tot[0:20], jnp.zeros((4, 128), F32)], axis=0)
    sm = _adamw(small_pack(pre_norm_g, post_norm_g, mla_q_norm_g, mla_kv_norm_g, fox_forget_b), g_small,
                small_pack(m_pre_norm_g, m_post_norm_g, m_mla_q_norm_g, m_mla_kv_norm_g, m_fox_forget_b),
                small_pack(v_pre_norm_g, v_post_norm_g, v_mla_q_norm_g, v_mla_kv_norm_g, v_fox_forget_b),
                "adamw_small")
    g_pre, g_post, g_q, g_kv, g_fb = small_unpack(g_small)
    (d_pre, d_post, d_q, d_kv, d_fb), (nm_pre, nm_post, nm_q, nm_kv, nm_fb), (nv_pre, nv_post, nv_q, nv_kv, nv_fb) = (
        small_unpack(t) for t in sm)

    d_meta, nm_meta, nv_meta = _adamw(meta_tokens, g_meta, m_meta_tokens, v_meta_tokens, "adamw_meta")
    d_win, nm_win, nv_win = (t.T[None] for t in _adamw(w_in[0].T, g_w_in[0].T, m_w_in[0].T, v_w_in[0].T,
                                                       "adamw_w_in"))
    d_wuq, nm_wuq, nv_wuq = _adamw(w_uq[0], g_w_uq, m_w_uq[0], v_w_uq[0], "adamw_w_uq")
    d_wukv, nm_wukv, nv_wukv = _adamw(w_ukv[0], g_w_ukv, m_w_ukv[0], v_w_ukv[0], "adamw_w_ukv")
    d_wbm, nm_wbm, nv_wbm = _adamw(w_br_mla[0], g_w_bm, m_w_br_mla[0], v_w_br_mla[0], "adamw_w_br_mla")
    d_wbf, nm_wbf, nv_wbf = _adamw(w_br_fox[0], g_w_bf, m_w_br_fox[0], v_w_br_fox[0], "adamw_w_br_fox")
    d_wo, nm_wo, nv_wo = _adamw(w_out[0], g_w_out, m_w_out[0], v_w_out[0], "adamw_w_out")

    def group(meta_, pre, win, fb_, q_, kv_, wuq, wukv, wbm, wbf, wo, post):
        return (meta_, pre, win, fb_, q_, kv_, wuq[None], wukv[None], wbm[None], wbf[None], wo[None], post)

    grads = group(g_meta, g_pre, g_w_in, g_fb, g_q, g_kv, g_w_uq, g_w_ukv, g_w_bm, g_w_bf, g_w_out, g_post)
    deltas = group(d_meta, d_pre, d_win, d_fb, d_q, d_kv, d_wuq, d_wukv, d_wbm, d_wbf, d_wo, d_post)
    new_m = group(nm_meta, nm_pre, nm_win, nm_fb, nm_q, nm_kv, nm_wuq, nm_wukv, nm_wbm, nm_wbf, nm_wo, nm_post)
    new_v = group(nv_meta, nv_pre, nv_win, nv_fb, nv_q, nv_kv, nv_wuq, nv_wukv, nv_wbm, nv_wbf, nv_wo, nv_post)
    return (loss, dx[None], *grads, *deltas, *new_m, *new_v)
```

```python
import math

import jax
import jax.numpy as jnp
import numpy as np
from jax import lax
from jax.experimental import pallas as pl
from jax.experimental.pallas import tpu as pltpu
from jax.experimental.pallas import tpu_sc as plsc

F32 = jnp.float32
BF16 = jnp.bfloat16

D_MODEL = 1024
N_META = 16
RMS_EPS = 1e-6
HEADS = 16
PAIRS = HEADS // 2
HEAD_DIM = 64
LANES = 128
MLA_ROPE = 32
MLA_SCALE = 1.0 / math.sqrt(64 + 32)
FOX_SCALE = 1.0 / math.sqrt(64)
ROPE_THETA = 10000.0

PAD = 256
BLK = 256
QB = 512
UNROLL = 4
NEG = -1e30

C_CQ, C_CKV, C_KPE, C_ZMLA, C_FQ, C_FK, C_FV, C_FL, C_ZFOX, C_GA, C_GB, C_END = (
    0, 256, 384, 416, 1440, 2464, 3488, 4512, 4528, 5552, 6576, 7600)
SMALL_W = 640
W_IN_SHARD = 1900

P2_ROWS = 928
N_CHIPS = 4

ADAM_LR = 0.001
ADAM_B1 = 0.9
ADAM_B2 = 0.999
ADAM_EPS = 1e-08
ADAM_WD = 0.01
ADAM_STEP = 10

VMEM_BIG = 56 * 1024 * 1024
MM_VMEM_BUDGET = 44 * 1024 * 1024
MESH = pl.DeviceIdType.MESH


def _cp(dims, vmem=None):
    return pltpu.CompilerParams(dimension_semantics=dims, vmem_limit_bytes=vmem)


def _dot(a, b, ca, cb):
    return lax.dot_general(a, b, (((ca,), (cb,)), ((), ())), preferred_element_type=F32)


def _sigmoid(x):
    return 1.0 / (1.0 + jnp.exp(-x))


def _tile(n, cands):
    for c in cands:
        if n % c == 0:
            return c
    return n


def _mm(a, b, *, mode, out_dtype, name, acc=None, epilogue=None, row_ins=(), after=None):
    if mode == "nn":
        (M, K), N = a.shape, b.shape[1]
    elif mode == "nt":
        (M, K), N = a.shape, b.shape[0]
    else:
        (K, M), N = a.shape, b.shape[1]
    tm = _tile(M, (1088, 1024)) if M > 1024 else M
    tn = _tile(N, (1024,)) if N > 1024 else N
    nk = 1
    while True:
        tk = K // nk
        need = 2 * tk * (tm * a.dtype.itemsize + tn * b.dtype.itemsize) + tm * tn * (
            2 * jnp.dtype(out_dtype).itemsize + (8 if acc is not None else 0) + (4 if nk > 1 else 0))
        if need <= MM_VMEM_BUDGET or (tk // 2) % (16 if mode == "tn" else LANES) or tk <= 512:
            break
        nk *= 2
    ca, cb = {"nn": (1, 0), "nt": (1, 1), "tn": (0, 0)}[mode]
    a_spec = (pl.BlockSpec((tk, tm), lambda j, i, k: (k, i)) if mode == "tn"
              else pl.BlockSpec((tm, tk), lambda j, i, k: (i, k)))
    b_spec = (pl.BlockSpec((tn, tk), lambda j, i, k: (j, k)) if mode == "nt"
              else pl.BlockSpec((tk, tn), lambda j, i, k: (k, j)))
    o_spec = pl.BlockSpec((tm, tn), lambda j, i, k: (i, j))
    has_acc = acc is not None

    nrow = len(row_ins)

    def body(*refs):
        a_ref, b_ref = refs[0], refs[1]
        acc_ref = refs[2] if has_acc else None
        rows = refs[2 + has_acc:2 + has_acc + nrow]
        o_ref = refs[2 + has_acc + nrow + (after is not None)]

        def store(tile):
            if epilogue is not None:
                tile = epilogue(tile, *[r[...] for r in rows])
            o_ref[...] = tile.astype(out_dtype)

        part = _dot(a_ref[...].astype(BF16), b_ref[...].astype(BF16), ca, cb)
        if nk == 1:
            store(part + acc_ref[...] if has_acc else part)
        else:
            sc = refs[-1]
            k = pl.program_id(2)

            @pl.when(k == 0)
            def _():
                sc[...] = part + acc_ref[...] if has_acc else part

            @pl.when(k > 0)
            def _():
                sc[...] += part

            @pl.when(k == nk - 1)
            def _():
                store(sc[...])

    ins = [a, b] + ([acc] if has_acc else []) + list(row_ins)
    in_specs = ([a_spec, b_spec] + ([o_spec] if has_acc else [])
                + [pl.BlockSpec((tm, r.shape[1]), lambda j, i, k: (i, 0)) for r in row_ins])
    if after is not None:
        ins.append(after)
        in_specs.append(pl.BlockSpec(after.shape, lambda j, i, k: (0,) * after.ndim))
    return pl.pallas_call(
        body, name=name, grid=(N // tn, M // tm, nk), in_specs=in_specs, out_specs=o_spec,
        out_shape=jax.ShapeDtypeStruct((M, N), out_dtype),
        scratch_shapes=[pltpu.VMEM((tm, tn), F32)] if nk > 1 else [],
        compiler_params=_cp(("parallel", "parallel", "arbitrary"), VMEM_BIG))(*ins)


def _mm_sum_nt(pairs, *, name, after=None):
    n = len(pairs)
    M, N = pairs[0][0].shape[0], pairs[0][1].shape[0]
    tm = _tile(M, (272,))

    def body(*refs):
        o_ref = refs[2 * n + (after is not None)]
        tot = _dot(refs[0][...].astype(BF16), refs[n][...].astype(BF16), 1, 1)
        for i in range(1, n):
            tot = tot + _dot(refs[i][...].astype(BF16), refs[n + i][...].astype(BF16), 1, 1)
        o_ref[...] = tot

    ins = [a for a, _ in pairs] + [b for _, b in pairs]
    in_specs = ([pl.BlockSpec((tm, a.shape[1]), lambda i: (i, 0)) for a, _ in pairs]
                + [pl.BlockSpec(b.shape, lambda i: (0, 0)) for _, b in pairs])
    if after is not None:
        ins.append(after)
        in_specs.append(pl.BlockSpec(after.shape, lambda i: (0,) * after.ndim))
    return pl.pallas_call(
        body, name=name, grid=(M // tm,), in_specs=in_specs, out_specs=pl.BlockSpec((tm, N), lambda i: (i, 0)),
        out_shape=jax.ShapeDtypeStruct((M, N), F32), compiler_params=_cp(("parallel",), VMEM_BIG))(*ins)


def _row(w):
    return pl.BlockSpec((BLK, w), lambda i: (i, 0))


def _rowc(w, c):
    return pl.BlockSpec((BLK, w), lambda i: (i, c))


def _full(shape):
    return pl.BlockSpec(shape, lambda i: tuple(0 for _ in shape))


def _rope(x, c, s):
    lane = lax.broadcasted_iota(jnp.int32, x.shape, 1)
    is_x1 = ((lane >> 4) & 1) == 0
    partner = jnp.where(is_x1, pltpu.roll(x, LANES - 16, 1), pltpu.roll(x, 16, 1))
    return x * c + partner * s


def _row_valid(i):
    rows = i * BLK + lax.broadcasted_iota(jnp.int32, (BLK, 1), 0)
    return (rows < N_META) | (rows >= PAD)


def _shift_rows(w):
    return pl.BlockSpec((BLK, w), lambda i: (jnp.maximum(i - 1, 0), 0))


def _h_block(i, x_ref, meta_ref):
    head = jnp.concatenate([meta_ref[...], jnp.zeros((BLK - N_META, D_MODEL), F32)], axis=0)
    return jnp.where(i == 0, head, x_ref[...])


def _rms_pre(x2, meta, g):
    lp = PAD + x2.shape[0]

    def body(x_ref, meta_ref, g_ref, u_ref):
        hv = _h_block(pl.program_id(0), x_ref, meta_ref)
        r = lax.rsqrt(jnp.mean(hv * hv, axis=-1, keepdims=True) + RMS_EPS)
        u_ref[...] = (hv * r * g_ref[...]).astype(BF16)

    return pl.pallas_call(
        body, name="rms_pre", grid=(lp // BLK,),
        in_specs=[_shift_rows(D_MODEL), _full((N_META, D_MODEL)), _full((1, D_MODEL))], out_specs=_row(D_MODEL),
        out_shape=jax.ShapeDtypeStruct((lp, D_MODEL), BF16),
        compiler_params=_cp(("parallel",)))(x2, meta, g)


def _split3(x):
    hi = x.astype(BF16)
    r1 = x - hi.astype(F32)
    mid = r1.astype(BF16)
    lo = (r1 - mid.astype(F32)).astype(BF16)
    return hi, mid, lo


def _small_prep(small, gq, gkv, fb, ctab, stab, tri):
    lp = small.shape[0]

    def body(sm_ref, gq_ref, gkv_ref, fb_ref, c_ref, s_ref, tri_ref, qn_ref, kvn_ref, kr_ref, ncum_ref, carry):
        i = pl.program_id(0)

        @pl.when(i == 0)
        def _():
            carry[...] = jnp.zeros_like(carry)

        cq = sm_ref[:, 0:256]
        r = lax.rsqrt(jnp.mean(cq * cq, axis=-1, keepdims=True) + RMS_EPS)
        qn_ref[...] = (cq * r * gq_ref[...]).astype(BF16)
        ckv = sm_ref[:, 256:384]
        r = lax.rsqrt(jnp.mean(ckv * ckv, axis=-1, keepdims=True) + RMS_EPS)
        kvn_ref[...] = (ckv * r * gkv_ref[...]).astype(BF16)
        kr_ref[...] = _rope(sm_ref[:, 384:512], c_ref[...], s_ref[...]).astype(BF16)
        fl = sm_ref[:, 512:640] + fb_ref[...]
        lf = jnp.minimum(fl, 0.0) - jnp.log(1.0 + jnp.exp(-jnp.abs(fl)))
        lf = jnp.where(_row_valid(i), lf, 0.0)
        hi, mid, lo = _split3(lf)
        t = tri_ref[...]
        cum = (_dot(t, hi, 1, 0) + _dot(t, mid, 1, 0)) + _dot(t, lo, 1, 0) + carry[...]
        ncum_ref[...] = -cum
        carry[...] = -ncum_ref[BLK - 1:BLK, :]

    return pl.pallas_call(
        body, name="small_prep", grid=(lp // BLK,),
        in_specs=[_row(SMALL_W), _full((1, 256)), _full((1, 128)), _full((1, 128)), _row(128), _row(128),
                  _full((BLK, BLK))],
        out_specs=[_row(256), _row(128), _row(128), _row(128)],
        out_shape=[jax.ShapeDtypeStruct((lp, 256), BF16), jax.ShapeDtypeStruct((lp, 128), BF16),
                   jax.ShapeDtypeStruct((lp, 128), BF16), jax.ShapeDtypeStruct((lp, 128), F32)],
        scratch_shapes=[pltpu.VMEM((1, 128), F32)],
        compiler_params=_cp(("arbitrary",)))(small, gq, gkv, fb, ctab, stab, tri)


def _rope_pairs(tile, c, s):
    out = []
    for lo in range(0, tile.shape[1], 256):
        out += [tile[:, lo:lo + 128], _rope(tile[:, lo + 128:lo + 256], c, s)]
    return jnp.concatenate(out, axis=1)


def _gate_fwd(o_mla, o_fox, gate):
    lp = o_mla.shape[0]

    def body(om_ref, of_ref, zm_ref, zf_ref, am_ref, af_ref):
        zm = zm_ref[...].astype(F32)
        am_ref[...] = (om_ref[...] * (zm * _sigmoid(zm))).astype(BF16)
        zf = zf_ref[...].astype(F32)
        af_ref[...] = (of_ref[...] * (zf * _sigmoid(zf))).astype(BF16)

    return pl.pallas_call(
        body, name="gate_fwd", grid=(lp // BLK,),
        in_specs=[_row(D_MODEL), _row(D_MODEL), _rowc(D_MODEL, 0), _rowc(D_MODEL, 1)],
        out_specs=[_row(D_MODEL), _row(D_MODEL)],
        out_shape=[jax.ShapeDtypeStruct((lp, D_MODEL), BF16)] * 2,
        compiler_params=_cp(("parallel",)))(o_mla, o_fox, gate, gate)


def _merge_fwd(gate, y_mla, y_fox):
    lp = y_mla.shape[0]

    def body(ga_ref, gb_ref, ym_ref, yf_ref, m_ref):
        sa = _sigmoid(ga_ref[...].astype(F32))
        sb = _sigmoid(gb_ref[...].astype(F32))
        m_ref[...] = (sa * ym_ref[...] + sb * yf_ref[...]).astype(BF16)

    return pl.pallas_call(
        body, name="merge_fwd", grid=(lp // BLK,),
        in_specs=[_rowc(D_MODEL, 2), _rowc(D_MODEL, 3), _row(D_MODEL), _row(D_MODEL)],
        out_specs=_row(D_MODEL), out_shape=jax.ShapeDtypeStruct((lp, D_MODEL), BF16),
        compiler_params=_cp(("parallel",)))(gate, gate, y_mla, y_fox)


def _tail(x2, mixed, tgt, gpost):
    lp = mixed.shape[0]
    shift = _shift_rows(D_MODEL)

    def body(h_ref, mx_ref, t_ref, g_ref, dmx_ref, dy_ref, loss_ref, dg_ref):
        i = pl.program_id(0)

        @pl.when(i == 0)
        def _():
            loss_ref[...] = jnp.zeros_like(loss_ref)
            dg_ref[...] = jnp.zeros_like(dg_ref)
            dmx_ref[...] = jnp.zeros_like(dmx_ref)
            dy_ref[...] = jnp.zeros_like(dy_ref)

        @pl.when(i > 0)
        def _():
            mx = mx_ref[...]
            g = g_ref[...]
            r = lax.rsqrt(jnp.mean(mx * mx, axis=-1, keepdims=True) + RMS_EPS)
            nrm = mx * r
            e = (h_ref[...] + nrm * g) - t_ref[...]
            loss_ref[...] += jnp.sum(0.5 * jnp.sum(e * e, axis=-1, keepdims=True) * (1.0 / D_MODEL),
                                     axis=0, keepdims=True)
            dy = e * (1.0 / D_MODEL)
            dy_ref[...] = dy
            dg_ref[...] += jnp.sum(dy * nrm, axis=0, keepdims=True)
            w = dy * g
            dot = jnp.mean(w * mx, axis=-1, keepdims=True)
            dmx_ref[...] = (r * w - mx * (r * r * r * dot)).astype(BF16)

    return pl.pallas_call(
        body, name="tail", grid=(lp // BLK,),
        in_specs=[shift, _row(D_MODEL), shift, _full((1, D_MODEL))],
        out_specs=[_row(D_MODEL), _row(D_MODEL), _full((1, 1)), _full((1, D_MODEL))],
        out_shape=[jax.ShapeDtypeStruct((lp, D_MODEL), BF16), jax.ShapeDtypeStruct((lp, D_MODEL), F32),
                   jax.ShapeDtypeStruct((1, 1), F32), jax.ShapeDtypeStruct((1, D_MODEL), F32)],
        compiler_params=_cp(("arbitrary",)))(x2, mixed, tgt, gpost)


def _merge_bwd(dm, gate, y_mla, y_fox):
    lp = dm.shape[0]

    def body(dm_ref, ga_ref, gb_ref, ym_ref, yf_ref, dym_ref, dyf_ref, dg_ref):
        dm_v = dm_ref[...].astype(F32)
        sa = _sigmoid(ga_ref[...].astype(F32))
        sb = _sigmoid(gb_ref[...].astype(F32))
        dym_ref[...] = (dm_v * sa).astype(BF16)
        dyf_ref[...] = (dm_v * sb).astype(BF16)
        dg_ref[:, 0:D_MODEL] = (dm_v * ym_ref[...] * (sa * (1.0 - sa))).astype(BF16)
        dg_ref[:, D_MODEL:2 * D_MODEL] = (dm_v * yf_ref[...] * (sb * (1.0 - sb))).astype(BF16)

    return pl.pallas_call(
        body, name="merge_bwd", grid=(lp // BLK,),
        in_specs=[_row(D_MODEL), _rowc(D_MODEL, 2), _rowc(D_MODEL, 3), _row(D_MODEL), _row(D_MODEL)],
        out_specs=[_row(D_MODEL), _row(D_MODEL), _row(2 * D_MODEL)],
        out_shape=[jax.ShapeDtypeStruct((lp, D_MODEL), BF16), jax.ShapeDtypeStruct((lp, D_MODEL), BF16),
                   jax.ShapeDtypeStruct((lp, 2 * D_MODEL), BF16)],
        compiler_params=_cp(("parallel",)))(dm, gate, gate, y_mla, y_fox)


def _gate_bwd(da_mla, da_fox, o_mla, o_fox, gate):
    lp = da_mla.shape[0]

    def one(da, o, z):
        sg = _sigmoid(z)
        do = da * (z * sg)
        dz = da * o * (sg * (1.0 + z * (1.0 - sg)))
        return do.astype(BF16), dz.astype(BF16)

    def body(dam_ref, daf_ref, om_ref, of_ref, zm_ref, zf_ref, dom_ref, dof_ref, dz_ref):
        f32 = lambda r: r[...].astype(F32)
        dom_ref[...], dz_ref[:, 0:D_MODEL] = one(f32(dam_ref), f32(om_ref), f32(zm_ref))
        dof_ref[...], dz_ref[:, D_MODEL:2 * D_MODEL] = one(f32(daf_ref), f32(of_ref), f32(zf_ref))

    return pl.pallas_call(
        body, name="gate_bwd", grid=(lp // BLK,),
        in_specs=[_row(D_MODEL)] * 4 + [_rowc(D_MODEL, 0), _rowc(D_MODEL, 1)],
        out_specs=[_row(D_MODEL), _row(D_MODEL), _row(2 * D_MODEL)],
        out_shape=[jax.ShapeDtypeStruct((lp, D_MODEL), BF16), jax.ShapeDtypeStruct((lp, D_MODEL), BF16),
                   jax.ShapeDtypeStruct((lp, 2 * D_MODEL), BF16)],
        compiler_params=_cp(("parallel",)))(da_mla, da_fox, o_mla, o_fox, gate, gate)


def _small_bwd(small, dqn, dkvn, dkr, dcol_t, drow_t, gq, gkv, fb, ctab, stab, triu):
    lp = small.shape[0]
    nb = lp // BLK

    def rrow(w):
        return pl.BlockSpec((BLK, w), lambda i: (nb - 1 - i, 0))

    def body(sm_ref, dqn_ref, dkvn_ref, dkr_ref, dcol_ref, drow_ref, gq_ref, gkv_ref, fb_ref, c_ref, s_ref, tri_ref,
             ds_ref, dgq_ref, dgkv_ref, dfb_ref, carry):
        i = pl.program_id(0)

        @pl.when(i == 0)
        def _():
            carry[...] = jnp.zeros_like(carry)
            dgq_ref[...] = jnp.zeros_like(dgq_ref)
            dgkv_ref[...] = jnp.zeros_like(dgkv_ref)
            dfb_ref[...] = jnp.zeros_like(dfb_ref)

        def norm_bwd(x, dn, g, dg_ref):
            r = lax.rsqrt(jnp.mean(x * x, axis=-1, keepdims=True) + RMS_EPS)
            dg_ref[...] += jnp.sum(dn * (x * r), axis=0, keepdims=True)
            w = dn * g
            dot = jnp.mean(w * x, axis=-1, keepdims=True)
            return r * w - x * (r * r * r * dot)

        ds_ref[:, 0:256] = norm_bwd(sm_ref[:, 0:256], dqn_ref[...], gq_ref[...], dgq_ref).astype(BF16)
        ds_ref[:, 256:384] = norm_bwd(sm_ref[:, 256:384], dkvn_ref[...], gkv_ref[...], dgkv_ref).astype(BF16)

        dk = dkr_ref[0]
        for p in range(1, PAIRS):
            dk = dk + dkr_ref[p]
        dk = _rope(dk, c_ref[...], -s_ref[...])
        lane = lax.broadcasted_iota(jnp.int32, dk.shape, 1)
        dk = jnp.where(lane < MLA_ROPE, dk + pltpu.roll(dk, LANES - MLA_ROPE, 1), 0.0)
        ds_ref[:, 384:512] = dk.astype(BF16)

        dcol = dcol_ref[0]
        for p in range(1, PAIRS):
            dcol = dcol + pltpu.roll(dcol_ref[p], 2 * p, 1)
        rows16 = jnp.concatenate([drow_ref[p, h:h + 1, :] for p in range(PAIRS) for h in range(2)], axis=0)
        eye = (lax.broadcasted_iota(jnp.int32, (HEADS, LANES), 0)
               == lax.broadcasted_iota(jnp.int32, (HEADS, LANES), 1)).astype(BF16)
        drow = sum(_dot(part, eye, 0, 0) for part in _split3(rows16))
        dcr = dcol - drow
        hi, mid, lo = _split3(dcr)
        t = tri_ref[...]
        suf = (_dot(t, hi, 1, 0) + _dot(t, mid, 1, 0)) + _dot(t, lo, 1, 0) + carry[...]
        fl = sm_ref[:, 512:640] + fb_ref[...]
        dfl = jnp.where(_row_valid(nb - 1 - i), -suf * _sigmoid(-fl), 0.0)
        ds_ref[:, 512:640] = dfl.astype(BF16)
        dfb_ref[...] += jnp.sum(dfl, axis=0, keepdims=True)
        carry[...] += jnp.sum(dcr, axis=0, keepdims=True)

    return pl.pallas_call(
        body, name="small_bwd", grid=(nb,),
        in_specs=[rrow(SMALL_W), rrow(256), rrow(128),
                  pl.BlockSpec((PAIRS, BLK, 128), lambda i: (0, nb - 1 - i, 0)),
                  pl.BlockSpec((PAIRS, BLK, 128), lambda i: (0, nb - 1 - i, 0)),
                  pl.BlockSpec((PAIRS, 2, BLK), lambda i: (0, 0, nb - 1 - i)),
                  _full((1, 256)), _full((1, 128)), _full((1, 128)), rrow(128), rrow(128), _full((BLK, BLK))],
        out_specs=[rrow(SMALL_W), _full((1, 256)), _full((1, 128)), _full((1, 128))],
        out_shape=[jax.ShapeDtypeStruct((lp, SMALL_W), BF16), jax.ShapeDtypeStruct((1, 256), F32),
                   jax.ShapeDtypeStruct((1, 128), F32), jax.ShapeDtypeStruct((1, 128), F32)],
        scratch_shapes=[pltpu.VMEM((1, 128), F32)],
        compiler_params=_cp(("arbitrary",)))(small, dqn, dkvn, dkr, dcol_t, drow_t, gq, gkv, fb, ctab, stab, triu)


def _pre_bwd(du, x2, meta, dy, gpre):
    s_rows = x2.shape[0]
    lp = PAD + s_rows
    shift = _shift_rows(D_MODEL)

    def body(du_ref, x_ref, meta_ref, dy_ref, g_ref, dx_ref, dmeta_ref, dg_ref):
        i = pl.program_id(0)

        @pl.when(i == 0)
        def _():
            dg_ref[...] = jnp.zeros_like(dg_ref)

        hv = _h_block(i, x_ref, meta_ref)
        duv = du_ref[...]
        r = lax.rsqrt(jnp.mean(hv * hv, axis=-1, keepdims=True) + RMS_EPS)
        dg_ref[...] += jnp.sum(duv * (hv * r), axis=0, keepdims=True)
        w = duv * g_ref[...]
        dot = jnp.mean(w * hv, axis=-1, keepdims=True)
        dh = dy_ref[...] + (r * w - hv * (r * r * r * dot))
        dx_ref[...] = dh

        @pl.when(i == 0)
        def _():
            dmeta_ref[...] = dh[0:N_META, :]

    return pl.pallas_call(
        body, name="pre_bwd", grid=(lp // BLK,),
        in_specs=[_row(D_MODEL), shift, _full((N_META, D_MODEL)), _row(D_MODEL), _full((1, D_MODEL))],
        out_specs=[shift, _full((N_META, D_MODEL)), _full((1, D_MODEL))],
        out_shape=[jax.ShapeDtypeStruct((s_rows, D_MODEL), F32), jax.ShapeDtypeStruct((N_META, D_MODEL), F32),
                   jax.ShapeDtypeStruct((1, D_MODEL), F32)],
        compiler_params=_cp(("arbitrary",)))(du, x2, meta, dy, gpre)


def _pair_masks(rope):
    lane = lax.broadcasted_iota(jnp.int32, (1, LANES), 1)
    mas = [lane < HEAD_DIM, lane >= HEAD_DIM]
    if not rope:
        return mas, mas
    wide = lax.broadcasted_iota(jnp.int32, (1, 2 * LANES), 1)
    rope_lo = LANES + MLA_ROPE
    return mas, [(wide < HEAD_DIM) | ((wide >= LANES) & (wide < rope_lo)),
                 ((wide >= HEAD_DIM) & (wide < LANES)) | ((wide >= rope_lo) & (wide < rope_lo + MLA_ROPE))]


def _mask2(x, masks):
    return [jnp.where(m, x, jnp.zeros_like(x)) for m in masks]


def _attn_fwd(q, k, v, *, kr=None, nbrep=None, scale, qcol, kcol, vcol, name):
    lp = q.shape[0]
    nq = 1 + (lp - PAD) // QB
    rope = kr is not None
    bias = nbrep is not None
    qw = 256 if rope else 128

    def body(*refs):
        it = iter(refs)
        q_ref, k_ref, v_ref = next(it), next(it), next(it)
        kr_ref = next(it) if rope else None
        nb_ref = next(it) if bias else None
        o_ref, lse_ref = next(it), next(it)
        i = pl.program_id(1)
        r0 = pl.multiple_of(jnp.where(i == 0, 0, PAD + QB * (i - 1)), BLK)
        b0 = r0 // BLK
        mas, hmask = _pair_masks(rope)
        qh = _mask2(q_ref[pl.ds(r0, QB), :], hmask)
        if bias:
            qh = [x * scale for x in qh]

        def causal(kc, n):
            key = kc * BLK + lax.broadcasted_iota(jnp.int32, (n, QB), 0)
            return (key <= r0 + lax.broadcasted_iota(jnp.int32, (n, QB), 1)) & ((kc > 0) | (n == N_META))

        def update(kcs, carry, masks, n=BLK):
            stats, acc = carry[:4], carry[4]
            k0s = [pl.multiple_of(kc * BLK, BLK) for kc in kcs]
            kks = [k_ref[pl.ds(k0, n), :] for k0 in k0s]
            if rope:
                kks = [jnp.concatenate([kk, kr_ref[pl.ds(k0, n), :]], axis=1) for kk, k0 in zip(kks, k0s)]
            new_stats, alphas, ps = [], [], [[] for _ in kcs]
            for h in range(2):
                m_prev, l_prev = stats[2 * h], stats[2 * h + 1]
                ss = []
                for kk, k0, mask in zip(kks, k0s, masks):
                    s = _dot(kk, qh[h], 1, 1)
                    if rope:
                        s = s * scale
                    if bias:
                        nbc = nb_ref[h, pl.ds(k0, n), :]
                        s = s + jnp.concatenate([nbc] * (QB // LANES), axis=1)
                    if mask is not None:
                        s = jnp.where(mask, s, NEG)
                    ss.append(s)
                m_new = m_prev
                for s in ss:
                    m_new = jnp.maximum(m_new, jnp.max(s, axis=0, keepdims=True))
                alpha = jnp.exp(m_prev - m_new)
                l_new = alpha * l_prev
                for j, s in enumerate(ss):
                    p = jnp.exp(s - m_new)
                    l_new = l_new + jnp.sum(p, axis=0, keepdims=True)
                    ps[j].append(p.astype(BF16))
                new_stats += [m_new, l_new]
                alphas.append(alpha)
            vcat = jnp.concatenate([x for k0 in k0s for x in _mask2(v_ref[pl.ds(k0, n), :], mas)], axis=0)
            pv = _dot(vcat, jnp.concatenate([p for pj in ps for p in pj], axis=0), 0, 0)
            a_full = jnp.concatenate([jnp.broadcast_to(a, (HEAD_DIM, QB)) for a in alphas], axis=0)
            return (*new_stats, a_full * acc + pv)

        neg = jnp.full((1, QB), NEG, F32)
        zero = jnp.zeros((1, QB), F32)
        c = update([0], (neg, zero, neg, zero, jnp.zeros((LANES, QB), F32)), [causal(0, N_META)], N_META)
        n_mid = jnp.maximum(b0 - 1, 0)
        c = lax.fori_loop(0, n_mid // 4, lambda t, cr: update([4 * t + u for u in (1, 2, 3, 4)], cr, [None] * 4), c)
        c = lax.fori_loop(0, (n_mid % 4) // 2, lambda t, cr: update([n_mid - 1, n_mid], cr, [None, None]), c)
        c = update([b0, b0 + 1], c, [causal(b0, BLK), causal(b0 + 1, BLK)])
        inv =jnp.concatenate([jnp.broadcast_to(1.0 / c[1], (HEAD_DIM, QB)),
                               jnp.broadcast_to(1.0 / c[3], (HEAD_DIM, QB))], axis=0)
        o_t = (c[4] * inv).T.astype(BF16)
        lses = [c[0] + jnp.log(c[1]), c[2] + jnp.log(c[3])]
        o_ref[pl.ds(r0, BLK), :] = o_t[0:BLK]
        for h in range(2):
            lse_ref[0, h:h + 1, pl.ds(r0, BLK)] = lses[h][:, 0:BLK]

        @pl.when(i > 0)
        def _():
            r1 = pl.multiple_of(r0 + BLK, BLK)
            o_ref[pl.ds(r1, QB - BLK), :] = o_t[BLK:QB]
            for h in range(2):
                lse_ref[0, h:h + 1, pl.ds(r1, QB - BLK)] = lses[h][:, BLK:QB]

    in_specs = [pl.BlockSpec((lp, qw), lambda p, i: (0, qcol + p)),
                pl.BlockSpec((lp, 128), lambda p, i: (0, kcol(p))),
                pl.BlockSpec((lp, 128), lambda p, i: (0, vcol(p)))]
    ins = [q, k, v]
    if rope:
        in_specs.append(pl.BlockSpec((lp, 128), lambda p, i: (0, 0)))
        ins.append(kr)
    if bias:
        in_specs.append(pl.BlockSpec((2, lp, 128), lambda p, i: (p, 0, 0)))
        ins.append(nbrep)
    return pl.pallas_call(
        body, name=name, grid=(PAIRS, nq), in_specs=in_specs,
        out_specs=[pl.BlockSpec((lp, 128), lambda p, i: (0, p)),
                   pl.BlockSpec((1, 2, lp), lambda p, i: (p, 0, 0))],
        out_shape=[jax.ShapeDtypeStruct((lp, D_MODEL), BF16), jax.ShapeDtypeStruct((PAIRS, 2, lp), F32)],
        compiler_params=_cp(("parallel", "arbitrary"), VMEM_BIG))(*ins)


def _attn_bwd(q, k, v, do, o, lse, *, kr=None, rtabs=None, nbrep=None, scale, qcol, kcol, vcol, name):
    lp = q.shape[0]
    nb = lp // BLK
    rope = kr is not None
    bias = nbrep is not None
    qw = 256 if rope else 128

    def body(*refs):
        it = iter(refs)
        q_ref, k_ref, v_ref = next(it), next(it), next(it)
        kr_ref = next(it) if rope else None
        nb_ref = next(it) if bias else None
        do_ref, o_ref, lse_ref = next(it), next(it), next(it)
        ct_ref, st_ref = (next(it), next(it)) if rope else (None, None)
        dq_out, dk_ref, dv_ref = next(it), next(it), next(it)
        x_ref = next(it)
        drow_ref = next(it) if bias else None
        delta, dq_ref = next(it), next(it)
        kb = pl.program_id(1)
        mas, hmask = _pair_masks(rope)
        lane = lax.broadcasted_iota(jnp.int32, (1, LANES), 1)

        @pl.when(kb == 0)
        def _():
            dq_ref[...] = jnp.zeros_like(dq_ref)
            if bias:
                drow_ref[...] = jnp.zeros_like(drow_ref)
            sub = lax.broadcasted_iota(jnp.int32, (8, LANES), 0)
            sel = (((sub == 0) & mas[0]) | ((sub == 1) & mas[1])).astype(BF16)

            def dstep(c, carry):
                r0 = pl.multiple_of(c * BLK, BLK)
                prod = do_ref[pl.ds(r0, BLK), :].astype(F32) * o_ref[pl.ds(r0, BLK), :]
                hi, mid, lo = _split3(prod)
                delta[:, pl.ds(r0, BLK)] = (_dot(sel, hi, 1, 1) + _dot(sel, mid, 1, 1)) + _dot(sel, lo, 1, 1)
                return carry

            lax.fori_loop(0, nb, dstep, 0)

        def key_pass(n, w):
            kk = k_ref[0:n, :]
            if rope:
                kk = jnp.concatenate([kk, kr_ref[0:n, :]], axis=1)
            vh = _mask2(v_ref[0:n, :], mas)
            kcat = jnp.concatenate(_mask2(kk, hmask), axis=0)
            if bias:
                kcat = kcat * scale
                nbc = [jnp.concatenate([nb_ref[h, 0:n, :]] * (w // LANES), axis=1) for h in range(2)]
            diag_mask = (lax.broadcasted_iota(jnp.int32, (n, w), 0) <= lax.broadcasted_iota(jnp.int32, (n, w), 1))

            def chunk(qc, carry, mask):
                carry = list(carry)
                q0 = qc * w if isinstance(qc, int) else pl.multiple_of(qc * w, w)
                dov = do_ref[pl.ds(q0, w), :]
                doh = _mask2(dov, mas)
                qh = _mask2(q_ref[pl.ds(q0, w), :], hmask)
                if bias:
                    qh = [x * scale for x in qh]
                pbs, dss = [], []
                for h in range(2):
                    s = _dot(kk, qh[h], 1, 1)
                    if rope:
                        s = s * scale
                    if bias:
                        s = s + nbc[h]
                    p = jnp.exp(s - lse_ref[0, h:h + 1, pl.ds(q0, w)])
                    if mask is not None:
                        p = jnp.where(mask, p, 0.0)
                    ds = p * (_dot(vh[h], dov, 1, 1) - delta[h:h + 1, pl.ds(q0, w)])
                    if bias:
                        drow_ref[0, h:h + 1, pl.ds(q0, w)] += jnp.sum(ds, axis=0, keepdims=True)
                        carry[2 + h] = carry[2 + h] + jnp.sum(ds, axis=1, keepdims=True)
                    else:
                        ds = ds * scale
                    pbs.append(p.astype(BF16))
                    dss.append(ds.astype(BF16))
                ds_lanes = jnp.concatenate(dss, axis=1)
                ds_rows = jnp.concatenate(dss, axis=0)
                carry[0] = carry[0] + _dot(ds_lanes, jnp.concatenate(qh, axis=0), 1, 0)
                carry[1] = carry[1] + _dot(jnp.concatenate(pbs, axis=1), jnp.concatenate(doh, axis=0), 1, 0)
                dq_ref[pl.ds(q0, w), :] += _dot(ds_rows, kcat, 0, 0)
                return tuple(carry)

            c = [jnp.zeros((n, qw), F32), jnp.zeros((n, LANES), F32)]
            if bias:
                c += [jnp.zeros((n, 1), F32), jnp.zeros((n, 1), F32)]
            c = tuple(c)
            if w != BLK:
                for qc in range(lp // w):
                    c = chunk(qc, c, diag_mask if qc == 0 else None)
            else:
                groups = (nb - kb) // UNROLL

                def several(t, cr):
                    for u in range(UNROLL):
                        cr = chunk(kb + UNROLL * t + u, cr, (diag_mask | (t > 0)) if u == 0 else None)
                    return cr

                c = lax.fori_loop(0, groups, several, c)
                start = kb + UNROLL * groups
                pairs = (nb - start) // 2

                def two(t, cr):
                    qc = start + 2 * t
                    return chunk(qc + 1, chunk(qc, cr, diag_mask | (qc > kb)), None)

                c = lax.fori_loop(0, pairs, two, c)
                c = lax.fori_loop(start + 2 * pairs, nb, lambda qc, cr: chunk(qc, cr, diag_mask | (qc > kb)), c)

            def rows(a, dtype):
                a = a.astype(dtype)
                return a if n == BLK else jnp.concatenate([a, jnp.zeros((BLK - n, a.shape[1]), dtype)], axis=0)

            dk_ref[...] = rows(c[0][:, 0:LANES], BF16)
            dv_ref[...] = rows(c[1], BF16)
            if rope:
                x_ref[0] = rows(c[0][:, LANES:2 * LANES], F32)
            if bias:
                x_ref[0] = rows(jnp.where(lane == 0, c[2], jnp.where(lane == 1, c[3], 0.0)), F32)

        @pl.when(kb == 0)
        def _():
            key_pass(N_META, lp // 2)

        @pl.when(kb > 0)
        def _():
            key_pass(BLK, BLK)

        @pl.when(kb == nb - 1)
        def _():
            def fin(c, carry):
                r0 = pl.multiple_of(c * BLK, BLK)
                dq = dq_ref[pl.ds(r0, BLK), :]
                if rope:
                    back = _rope(dq[:, LANES:2 * LANES], ct_ref[pl.ds(r0, BLK), :], -st_ref[pl.ds(r0, BLK), :])
                    dq = jnp.concatenate([dq[:, 0:LANES], back], axis=1)
                dq_out[pl.ds(r0, BLK), :] = dq.astype(BF16)
                return carry

            lax.fori_loop(0, nb, fin, 0)

    in_specs = [pl.BlockSpec((lp, qw), lambda p, j: (0, qcol + p)),
                pl.BlockSpec((BLK, 128), lambda p, j: (j, kcol(p))),
                pl.BlockSpec((BLK, 128), lambda p, j: (j, vcol(p)))]
    ins = [q, k, v]
    if rope:
        in_specs.append(pl.BlockSpec((BLK, 128), lambda p, j: (j, 0)))
        ins.append(kr)
    if bias:
        in_specs.append(pl.BlockSpec((2, BLK, 128), lambda p, j: (p, j, 0)))
        ins.append(nbrep)
    in_specs += [pl.BlockSpec((lp, 128), lambda p, j: (0, p)), pl.BlockSpec((lp, 128), lambda p, j: (0, p)),
                 pl.BlockSpec((1, 2, lp), lambda p, j: (p, 0, 0))]
    ins += [do, o, lse]
    if rope:
        in_specs += [pl.BlockSpec((lp, 128), lambda p, j: (0, 0))] * 2
        ins += list(rtabs)
    out_specs = [pl.BlockSpec((lp, qw), lambda p, j: (0, p)),
                 pl.BlockSpec((BLK, 128), lambda p, j: (j, p)),
                 pl.BlockSpec((BLK, 128), lambda p, j: (j, p)),
                 pl.BlockSpec((1, BLK, 128), lambda p, j: (p, j, 0))]
    out_shape = [jax.ShapeDtypeStruct((lp, PAIRS * qw), BF16), jax.ShapeDtypeStruct((lp, D_MODEL), BF16),
                 jax.ShapeDtypeStruct((lp, D_MODEL), BF16), jax.ShapeDtypeStruct((PAIRS, lp, 128), F32)]
    if bias:
        out_specs.append(pl.BlockSpec((1, 2, lp), lambda p, j: (p, 0, 0)))
        out_shape.append(jax.ShapeDtypeStruct((PAIRS, 2, lp), F32))
    return pl.pallas_call(
        body, name=name, grid=(PAIRS, nb), in_specs=in_specs, out_specs=out_specs, out_shape=out_shape,
        scratch_shapes=[pltpu.VMEM((8, lp), F32), pltpu.VMEM((lp, qw), F32)],
        compiler_params=_cp(("parallel", "arbitrary"), VMEM_BIG))(*ins)


def _adamw(w, g, m, v, name):
    lead = w.ndim - 2
    rows, cols = w.shape[lead:]
    big = rows * cols > 512 * 1024
    tr = 128 if big and rows % 128 == 0 else rows
    tc = 256 if big and tr == rows else cols

    def body(w_ref, g_ref, m_ref, v_ref, d_ref, nm_ref, nv_ref):
        gv = g_ref[...]
        nm = ADAM_B1 * m_ref[...] + (1.0 - ADAM_B1) * gv
        nv = ADAM_B2 * v_ref[...] + (1.0 - ADAM_B2) * (gv * gv)
        m_hat = nm / (1.0 - ADAM_B1 ** ADAM_STEP)
        v_hat = nv / (1.0 - ADAM_B2 ** ADAM_STEP)
        d_ref[...] = -ADAM_LR * (m_hat / (jnp.sqrt(v_hat) + ADAM_EPS) + ADAM_WD * w_ref[...])
        nm_ref[...] = nm
        nv_ref[...] = nv

    spec = pl.BlockSpec((1,) * lead + (tr, tc), lambda i, j: (0,) * lead + (i, j))
    return pl.pallas_call(
        body, name=name, grid=(rows // tr, cols // tc), in_specs=[spec] * 4, out_specs=[spec] * 3,
        out_shape=[jax.ShapeDtypeStruct(w.shape, F32)] * 3,
        compiler_params=_cp(("parallel", "parallel"), VMEM_BIG))(w, g, m, v)


def _add_cores(g, from_sib, name):
    n, rows, cols = g.shape
    half = rows // 2
    tr = _tile(half, (256, 240))
    nt = half // tr

    def body(lo_ref, hi_ref, s_ref, o_ref):
        mine = jnp.where(lax.axis_index("c") == 0, lo_ref[0], hi_ref[0])
        o_ref[0] = (mine + s_ref[0]).astype(BF16)

    return pl.pallas_call(
        body, name=name, grid=(n, nt),
        in_specs=[pl.BlockSpec((1, tr, cols), lambda j, i: (j, i, 0)),
                  pl.BlockSpec((1, tr, cols), lambda j, i: (j, nt + i, 0)),
                  pl.BlockSpec((1, tr, cols), lambda j, i: (j, i, 0))],
        out_specs=pl.BlockSpec((1, tr, cols), lambda j, i: (j, i, 0)),
        out_shape=jax.ShapeDtypeStruct((n, half, cols), BF16),
        compiler_params=_cp(("parallel", "parallel"), VMEM_BIG))(g, g, from_sib)


def _add_chips(x, own, name):
    n, rows, cols = x.shape
    tr = _tile(rows, (256, 240))

    def body(x_ref, own_ref, o_ref):
        me = 2 * lax.axis_index("x") + lax.axis_index("y")
        v = [jnp.where(me == k, own_ref[...], x_ref[k]).astype(F32) for k in range(N_CHIPS)]
        o_ref[...] = ((v[0] + v[1]) + v[2]) + v[3]

    return pl.pallas_call(
        body, name=name, grid=(rows // tr,),
        in_specs=[pl.BlockSpec((n, tr, cols), lambda i: (0, i, 0)), pl.BlockSpec((tr, cols), lambda i: (i, 0))],
        out_specs=pl.BlockSpec((tr, cols), lambda i: (i, 0)),
        out_shape=jax.ShapeDtypeStruct((rows, cols), F32), compiler_params=_cp(("parallel",), VMEM_BIG))(x, own)


def _axes():
    return lax.axis_index("x"), lax.axis_index("y"), lax.axis_index("c")


def _other_chips(x, y):
    return [(1 - x, y), (x, 1 - y), (1 - x, 1 - y)]


ANY = pl.BlockSpec(memory_space=pl.ANY)


def _rcopy(src, dst, send_sems, recv_sems, k, to):
    return pltpu.make_async_remote_copy(src_ref=src, dst_ref=dst, send_sem=send_sems.at[k], recv_sem=recv_sems.at[k],
                                        device_id=to, device_id_type=MESH)


def _gather_weights(shards, meta):
    n = len(shards)

    def body(*refs):
        srcs, meta_ref = refs[:n], refs[n]
        outs, mout_ref = refs[n + 1:2 * n + 1], refs[2 * n + 1]
        send_sems, recv_sems = refs[2 * n + 2:]
        x, y, c = _axes()
        me = 2 * x + y
        sib = (x, y, 1 - c)
        chips = _other_chips(x, y)

        def half(t, chip_idx, cc):
            hr = shards[t].shape[0] // 2
            return outs[t].at[chip_idx, pl.ds(cc * hr, hr), :]

        first = []
        for j, (px, py) in enumerate(chips):
            for t in range(n):
                hr = shards[t].shape[0] // 2
                first.append(_rcopy(srcs[t].at[pl.ds(c * hr, hr), :], half(t, me, c), send_sems, recv_sems,
                                    3 * t + j, (px, py, c)))
            first.append(_rcopy(meta_ref, mout_ref.at[me], send_sems, recv_sems, 3 * n + j, (px, py, c)))
        for cp in first:
            cp.start()
        passed = []
        for j, (px, py) in enumerate(chips):
            src_chip = 2 * px + py
            for t in range(n):
                _rcopy(half(t, src_chip, c), half(t, src_chip, c), send_sems, recv_sems, 3 * t + j, sib).wait_recv()
                fwd = _rcopy(half(t, src_chip, c), half(t, src_chip, c), send_sems, recv_sems, 3 * (n + 1 + t) + j, sib)
                fwd.start()
                passed.append(fwd)
            _rcopy(mout_ref.at[src_chip], mout_ref.at[src_chip], send_sems, recv_sems, 3 * n + j, sib).wait_recv()
        for j, (px, py) in enumerate(chips):
            src_chip = 2 * px + py
            for t in range(n):
                _rcopy(half(t, src_chip, 1 - c), half(t, src_chip, 1 - c), send_sems, recv_sems,
                       3 * (n + 1 + t) + j, sib).wait_recv()
        for cp in first + passed:
            cp.wait_send()

    nsem = 3 * (2 * n + 1)
    return pl.pallas_call(
        body, name="gather_weights", in_specs=[ANY] * (n + 1), out_specs=[ANY] * (n + 1),
        out_shape=[jax.ShapeDtypeStruct((N_CHIPS,) + s.shape, s.dtype) for s in shards]
        + [jax.ShapeDtypeStruct((N_CHIPS,) + meta.shape, meta.dtype)],
        scratch_shapes=[pltpu.SemaphoreType.DMA((nsem,)), pltpu.SemaphoreType.DMA((nsem,))])(*shards, meta)


def _gather_late(shard):
    rows, cols = shard.shape
    hr = rows // 2
    src = jax.new_ref(shard, memory_space=pltpu.MemorySpace.HBM)
    out = jax.empty_ref(jax.ShapeDtypeStruct((N_CHIPS, rows, cols), shard.dtype), memory_space=pltpu.MemorySpace.HBM)

    @pl.kernel(mesh=plsc.ScalarSubcoreMesh(axis_name="seq", num_cores=1), name="gather_late",
               scratch_types=(pltpu.SemaphoreType.DMA((6,)), pltpu.SemaphoreType.DMA((6,))),
               compiler_params=pltpu.CompilerParams(collective_id=1))
    def launch(send_sems, recv_sems):
        x, y, c = _axes()
        me = 2 * x + y
        sib = (x, y, 1 - c)
        chips = _other_chips(x, y)
        barrier = pltpu.get_barrier_semaphore()
        for px, py in chips:
            pl.semaphore_signal(barrier, inc=1, device_id=(px, py, c), device_id_type=MESH)
        pl.semaphore_signal(barrier, inc=1, device_id=sib, device_id_type=MESH)
        pl.semaphore_wait(barrier, 4)

        def half(chip_idx, cc):
            return out.at[chip_idx, pl.ds(cc * hr, hr), :]

        first = [_rcopy(src.at[pl.ds(c * hr, hr), :], half(me, c), send_sems, recv_sems, j, (px, py, c))
                 for j, (px, py) in enumerate(chips)]
        for cp in first:
            cp.start()
        passed = []
        for j, (px, py) in enumerate(chips):
            land = half(2 * px + py, c)
            _rcopy(land, land, send_sems, recv_sems, j, sib).wait_recv()
            fwd = _rcopy(land, land, send_sems, recv_sems, 3 + j, sib)
            fwd.start()
            passed.append(fwd)
        for j, (px, py) in enumerate(chips):
            land = half(2 * px + py, 1 - c)
            _rcopy(land, land, send_sems, recv_sems, 3 + j, sib).wait_recv()
        for cp in first + passed:
            cp.wait_send()

    launch()
    return out[...]


def _swap_halves(gs):
    n = len(gs)
    ncopies = sum(g.shape[0] for g in gs)

    def body(*refs):
        srcs, outs = refs[:n], refs[n:2 * n]
        send_sems, recv_sems = refs[2 * n:]
        x, y, c = _axes()
        cps = []
        for t in range(n):
            hr = gs[t].shape[1] // 2
            for j in range(gs[t].shape[0]):
                cps.append(_rcopy(srcs[t].at[j, pl.ds((1 - c) * hr, hr), :], outs[t].at[j], send_sems, recv_sems,
                                  len(cps), (x, y, 1 - c)))
        for cp in cps:
            cp.start()
        for cp in cps:
            cp.wait()

    return pl.pallas_call(
        body, name="swap_halves", in_specs=[ANY] * n, out_specs=[ANY] * n,
        out_shape=[jax.ShapeDtypeStruct((g.shape[0], g.shape[1] // 2, g.shape[2]), g.dtype) for g in gs],
        scratch_shapes=[pltpu.SemaphoreType.DMA((ncopies,)), pltpu.SemaphoreType.DMA((ncopies,))])(*gs)


def _scatter_chips(parts):
    n = len(parts)
    srcs = [jax.new_ref(p, memory_space=pltpu.MemorySpace.HBM) for p in parts]
    outs = [jax.empty_ref(jax.ShapeDtypeStruct(p.shape, p.dtype), memory_space=pltpu.MemorySpace.HBM) for p in parts]

    @pl.kernel(mesh=plsc.ScalarSubcoreMesh(axis_name="seq", num_cores=1), name="scatter_chips",
               scratch_types=(pltpu.SemaphoreType.DMA((3 * n,)), pltpu.SemaphoreType.DMA((3 * n,))),
               compiler_params=pltpu.CompilerParams(collective_id=0))
    def launch(send_sems, recv_sems):
        x, y, c = _axes()
        me = 2 * x + y
        chips = _other_chips(x, y)
        barrier = pltpu.get_barrier_semaphore()
        for px, py in chips:
            pl.semaphore_signal(barrier, inc=1, device_id=(px, py, c), device_id_type=MESH)
        pl.semaphore_wait(barrier, 3)
        cps = []
        for j, (px, py) in enumerate(chips):
            for t in range(n):
                cps.append(_rcopy(srcs[t].at[2 * px + py], outs[t].at[me], send_sems, recv_sems, 3 * t + j,
                                  (px, py, c)))
        for cp in cps:
            cp.start()
        for cp in cps:
            cp.wait()

    launch()
    return [o[...] for o in outs]


def _swap_reduced(rs):
    n = len(rs)

    def body(*refs):
        srcs, outs = refs[:n], refs[n:2 * n]
        send_sems, recv_sems = refs[2 * n:]
        x, y, c = _axes()
        cps = [_rcopy(srcs[t], outs[t], send_sems, recv_sems, t, (x, y, 1 - c)) for t in range(n)]
        for cp in cps:
            cp.start()
        for cp in cps:
            cp.wait()

    return pl.pallas_call(
        body, name="swap_reduced", in_specs=[ANY] * n, out_specs=[ANY] * n,
        out_shape=[jax.ShapeDtypeStruct(r.shape, r.dtype) for r in rs],
        scratch_shapes=[pltpu.SemaphoreType.DMA((n,)), pltpu.SemaphoreType.DMA((n,))])(*rs)


SMALL_ROWS = 24 + 128


def _allreduce_small(vec):
    def body(v_ref, out_ref, slots, send_sems, recv_sems):
        x, y, c = _axes()
        me = 4 * x + 2 * y + c
        slots[me] = v_ref[...]
        cps = []
        for k in range(1, 8):
            kx, ky, kc = (k >> 2) & 1, (k >> 1) & 1, k & 1
            peer = (1 - x if kx else x, 1 - y if ky else y, 1 - c if kc else c)
            cps.append(_rcopy(v_ref, slots.at[me], send_sems, recv_sems, k - 1, peer))
        for cp in cps:
            cp.start()
        for cp in cps:
            cp.wait()
        tot = slots[0]
        for k in range(1, 8):
            tot = tot + slots[k]
        out_ref[...] = tot

    return pl.pallas_call(
        body, name="allreduce_small",
        in_specs=[pl.BlockSpec(memory_space=pltpu.VMEM)], out_specs=pl.BlockSpec(memory_space=pltpu.VMEM),
        out_shape=jax.ShapeDtypeStruct((SMALL_ROWS, 128), F32),
        scratch_shapes=[pltpu.VMEM((8, SMALL_ROWS, 128), F32), pltpu.SemaphoreType.DMA((7,)),
                        pltpu.SemaphoreType.DMA((7,))])(vec)


def _pack_p2(w_uq, w_ukv, w_br_mla, w_br_fox, w_out, dtype):
    parts = [w_uq.reshape(96, D_MODEL), w_ukv.reshape(64, D_MODEL), w_br_mla, w_br_fox, w_out]
    return jnp.concatenate([p.astype(dtype) for p in parts], axis=0)


def _unpack_p2(pk):
    return pk[0:96].reshape(256, 384), pk[96:160].reshape(128, 512), pk[160:416], pk[416:672], pk[672:928]


def _uq_arrange(w):
    w3 = w.reshape(256, HEADS, 96)
    nope = w3[:, :, :64].reshape(256, PAIRS, 128)
    pe = w3[:, :, 64:].reshape(256, PAIRS, 64)
    return jnp.concatenate([nope, pe, jnp.zeros((256, PAIRS, 64), w.dtype)], axis=2).reshape(256, PAIRS * 256)


def _uq_restore(g):
    g3 = g.reshape(256, PAIRS, 256)
    nope = g3[:, :, :128].reshape(256, HEADS, 64)
    pe = g3[:, :, 128:192].reshape(256, HEADS, 32)
    return jnp.concatenate([nope, pe], axis=2).reshape(256, HEADS * 96)


def _ukv_arrange(w):
    w3 = w.reshape(128, HEADS, 128)
    return jnp.concatenate([w3[:, :, :64].reshape(128, 1024), w3[:, :, 64:].reshape(128, 1024)], axis=1)


def _ukv_restore(g):
    kn = g[:, :1024].reshape(128, HEADS, 64)
    vv = g[:, 1024:].reshape(128, HEADS, 64)
    return jnp.concatenate([kn, vv], axis=2).reshape(128, HEADS * 128)


def _rope_tables(lp):
    r = np.arange(lp)
    pos = np.where(r < N_META, r, np.where(r >= PAD, r - PAD + N_META, 0)).astype(np.float32)
    half = MLA_ROPE // 2
    inv_freq = np.float32(ROPE_THETA) ** (-np.arange(half, dtype=np.float32) / np.float32(half))
    ang = (pos[:, None] * inv_freq[None, :]).astype(np.float32)
    cos, sin = np.cos(ang).astype(np.float32), np.sin(ang).astype(np.float32)
    one, zero = np.ones((lp, 64), np.float32), np.zeros((lp, 64), np.float32)
    return (jnp.asarray(np.concatenate([cos, cos, cos, cos, one], axis=1)),
            jnp.asarray(np.concatenate([-sin, sin, -sin, sin, zero], axis=1)))


def _pad_lanes(v, n=128):
    return jnp.pad(v, ((0, 0), (0, n - v.shape[1])))


def _in_cols(slabs, a, b):
    out = []
    for j in range(N_CHIPS):
        lo, hi = max(a, W_IN_SHARD * j), min(b, W_IN_SHARD * (j + 1))
        if lo < hi:
            out.append(slabs[j][:, lo - W_IN_SHARD * j:hi - W_IN_SHARD * j])
    return out


def _local_step(x2, tgt2, meta_f, w_small, w_attn, w_gate, w_uq_f, w_ukv_f, w_bm, w_bf, w_o, pre_norm_g,
                post_norm_g, mla_q_norm_g, mla_kv_norm_g, fox_forget_b, start_exchange=None):
    s_rows = x2.shape[0]
    lp = PAD + s_rows
    w_uq_a = _uq_arrange(w_uq_f)
    w_ukv_a = _ukv_arrange(w_ukv_f)

    ctab, stab = _rope_tables(lp)
    ii = jnp.arange(BLK)
    tri_lo = (ii[:, None] >= ii[None, :]).astype(BF16)
    tri_up = (ii[:, None] <= ii[None, :]).astype(BF16)
    fb128 = _pad_lanes(fox_forget_b)

    u = _rms_pre(x2, meta_f, pre_norm_g)
    small = _mm(u, w_small, mode="nn", out_dtype=F32, name="proj_small")
    attn = _mm(u, w_attn, mode="nn", out_dtype=BF16, name="proj_attn")
    gate = _mm(u, w_gate, mode="nn", out_dtype=BF16, name="proj_gate")
    qn, kvn, kr, ncum = _small_prep(small, mla_q_norm_g, mla_kv_norm_g, fb128, ctab, stab, tri_lo)
    qcat = _mm(qn, w_uq_a, mode="nn", out_dtype=BF16, name="mla_q", epilogue=_rope_pairs, row_ins=(ctab, stab))
    kv = _mm(kvn, w_ukv_a, mode="nn", out_dtype=BF16, name="mla_kv")
    nbrep = jnp.broadcast_to(ncum[:, :HEADS].T[:, :, None], (HEADS, lp, LANES))

    mla_cols = dict(qcol=0, kcol=lambda p: p, vcol=lambda p: PAIRS + p)
    fox_cols = dict(qcol=0, kcol=lambda p: PAIRS + p, vcol=lambda p: 2 * PAIRS + p)
    o_mla, lse_mla = _attn_fwd(qcat, kv, kv, kr=kr, scale=MLA_SCALE, name="mla_fwd", **mla_cols)
    o_fox, lse_fox = _attn_fwd(attn, attn, attn, nbrep=nbrep, scale=FOX_SCALE, name="fox_fwd", **fox_cols)

    a_mla, a_fox = _gate_fwd(o_mla, o_fox, gate)
    y_mla = _mm(a_mla, w_bm, mode="nn", out_dtype=BF16, name="br_mla")
    y_fox = _mm(a_fox, w_bf, mode="nn", out_dtype=BF16, name="br_fox")
    mg = _merge_fwd(gate, y_mla, y_fox)
    mixed = _mm(mg, w_o, mode="nn", out_dtype=F32, name="out_proj")
    dmixed, dy, loss_p, dg_post = _tail(x2, mixed, tgt2, post_norm_g)

    d_w_out = _mm(mg, dmixed, mode="tn", out_dtype=F32, name="d_w_out")
    dm = _mm(dmixed, w_o, mode="nt", out_dtype=BF16, name="d_merge")
    dy_mla, dy_fox, dgate_ab = _merge_bwd(dm, gate, y_mla, y_fox)
    d_w_bm = _mm(a_mla, dy_mla, mode="tn", out_dtype=F32, name="d_w_br_mla")
    d_w_bf = _mm(a_fox, dy_fox, mode="tn", out_dtype=F32, name="d_w_br_fox")
    da_mla = _mm(dy_mla, w_bm, mode="nt", out_dtype=BF16, name="d_a_mla")
    da_fox = _mm(dy_fox, w_bf, mode="nt", out_dtype=BF16, name="d_a_fox")
    do_mla, do_fox, dgate_z = _gate_bwd(da_mla, da_fox, o_mla, o_fox, gate)

    dq_a, dkn, dvm, dkr = _attn_bwd(qcat, kv, kv, do_mla, o_mla, lse_mla, kr=kr, rtabs=(ctab, stab), scale=MLA_SCALE,
                                    name="mla_bwd", **mla_cols)
    dfq, dfk, dfv, dcol, drow = _attn_bwd(attn, attn, attn, do_fox, o_fox, lse_fox, nbrep=nbrep, scale=FOX_SCALE,
                                          name="fox_bwd", **fox_cols)

    d_w_uq_a = _mm(qn, dq_a, mode="tn", out_dtype=F32, name="d_w_uq")
    dqn = _mm(dq_a, w_uq_a, mode="nt", out_dtype=F32, name="d_qn")
    d_w_ukv_a = jnp.concatenate([_mm(kvn, dkn, mode="tn", out_dtype=F32, name="d_w_uk"),
                                 _mm(kvn, dvm, mode="tn", out_dtype=F32, name="d_w_uv")], axis=1)
    dkvn = _mm(dkn, w_ukv_a[:, :1024], mode="nt", out_dtype=F32, name="d_kvn_k")
    dkvn = _mm(dvm, w_ukv_a[:, 1024:], mode="nt", out_dtype=F32, name="d_kvn_v", acc=dkvn)
    dsmall, dg_q, dg_kv, dfb = _small_bwd(small, dqn, dkvn, dkr, dcol, drow, mla_q_norm_g, mla_kv_norm_g,
                                          fb128, ctab, stab, tri_up)

    dw_small = _mm(u, dsmall, mode="tn", out_dtype=F32, name="d_w_small")
    dw_fq = _mm(u, dfq, mode="tn", out_dtype=F32, name="d_w_fq")
    dw_fk = _mm(u, dfk, mode="tn", out_dtype=F32, name="d_w_fk")
    dw_fv = _mm(u, dfv, mode="tn", out_dtype=F32, name="d_w_fv")
    dw_z = _mm(u, dgate_z, mode="tn", out_dtype=F32, name="d_w_z")
    dw_g = _mm(u, dgate_ab, mode="tn", out_dtype=F32, name="d_w_g")
    d_w_in = (dw_small, dw_z, dw_fq, dw_fk, dw_fv, dw_g)
    d_w_uq = _uq_restore(d_w_uq_a)
    d_w_ukv = _ukv_restore(d_w_ukv_a)
    token = start_exchange(d_w_in, d_w_uq, d_w_ukv, d_w_bm, d_w_bf, d_w_out) if start_exchange else None
    du = _mm_sum_nt([(dsmall, w_small), (dfq, w_attn[:, 0:1024]), (dfk, w_attn[:, 1024:2048]),
                     (dfv, w_attn[:, 2048:3072]), (dgate_z, w_gate[:, 0:2048]), (dgate_ab, w_gate[:, 2048:4096])],
                    name="d_u", after=token)
    dx, dmeta, dg_pre = _pre_bwd(du, x2, meta_f, dy, pre_norm_g)
    return (loss_p, dx, dmeta, d_w_in, d_w_uq, d_w_ukv, d_w_bm, d_w_bf, d_w_out, dg_pre, dg_post, dg_q, dg_kv, dfb)


def _w_in_slabs(pieces):
    dw_small, dw_z, dw_fq, dw_fk, dw_fv, dw_g = pieces
    runs = [(dw_small[:, 0:416], C_CQ), (dw_z[:, 0:1024], C_ZMLA), (dw_fq, C_FQ), (dw_fk, C_FK), (dw_fv, C_FV),
            (dw_small[:, 512:528], C_FL), (dw_z[:, 1024:2048], C_ZFOX), (dw_g, C_GA)]
    slabs = []
    for j in range(N_CHIPS):
        lo, hi = W_IN_SHARD * j, W_IN_SHARD * (j + 1)
        cols = [a[:, max(lo, c0) - c0:min(hi, c0 + a.shape[1]) - c0] for a, c0 in runs
                if max(lo, c0) < min(hi, c0 + a.shape[1])]
        slabs.append(jnp.concatenate(cols, axis=1))
    return jnp.stack(slabs, axis=0)


def kernel(x, meta_tokens, pre_norm_g, w_in, fox_forget_b, mla_q_norm_g, mla_kv_norm_g, w_uq, w_ukv, w_br_mla, w_br_fox, w_out, post_norm_g, loss_target, m_meta_tokens, m_pre_norm_g, m_w_in, m_fox_forget_b, m_mla_q_norm_g, m_mla_kv_norm_g, m_w_uq, m_w_ukv, m_w_br_mla, m_w_br_fox, m_w_out, m_post_norm_g, v_meta_tokens, v_pre_norm_g, v_w_in, v_fox_forget_b, v_mla_q_norm_g, v_mla_kv_norm_g, v_w_uq, v_w_ukv, v_w_br_mla, v_w_br_fox, v_w_out, v_post_norm_g):
    me = 2 * lax.axis_index("x") + lax.axis_index("y")
    core = lax.axis_index("c")
    w_in_b = w_in.astype(BF16).reshape(D_MODEL, W_IN_SHARD)
    p2 = _pack_p2(w_uq[0], w_ukv[0], w_br_mla[0], w_br_fox[0], w_out[0], BF16)
    w_in_g, meta_g = _gather_weights([w_in_b], meta_tokens)
    p2_g = _gather_late(lax.optimization_barrier((p2, w_in_g))[0])
    slabs = [jnp.where(me == j, w_in_b, w_in_g[j]) for j in range(N_CHIPS)]
    pieces = [_unpack_p2(jnp.where(me == j, p2, p2_g[j])) for j in range(N_CHIPS)]
    w_uq_f = jnp.concatenate([p[0] for p in pieces], axis=1)
    w_ukv_f = jnp.concatenate([p[1] for p in pieces], axis=1)
    w_bm = jnp.concatenate([p[2] for p in pieces], axis=0)
    w_bf = jnp.concatenate([p[3] for p in pieces], axis=0)
    w_o = jnp.concatenate([p[4] for p in pieces], axis=0)
    meta_f = jnp.concatenate([jnp.where(me == j, meta_tokens, meta_g[j]) for j in range(N_CHIPS)], axis=1)
    kpe = _in_cols(slabs, C_KPE, C_ZMLA)
    w_small = jnp.concatenate(_in_cols(slabs, C_CQ, C_KPE) + kpe + kpe + [jnp.zeros((D_MODEL, 64), BF16)]
                              + _in_cols(slabs, C_FL, C_ZFOX) + [jnp.zeros((D_MODEL, 112), BF16)], axis=1)
    w_attn = jnp.concatenate(_in_cols(slabs, C_FQ, C_FL), axis=1)
    w_gate = jnp.concatenate(_in_cols(slabs, C_ZMLA, C_FQ) + _in_cols(slabs, C_ZFOX, C_END), axis=1)

    exchange = {}

    def start_exchange(d_w_in, d_w_uq, d_w_ukv, d_w_bm, d_w_bf, d_w_out):
        g2 = jnp.stack([_pack_p2(d_w_uq[:, 384 * j:384 * (j + 1)], d_w_ukv[:, 512 * j:512 * (j + 1)],
                                 d_w_bm[256 * j:256 * (j + 1)], d_w_bf[256 * j:256 * (j + 1)],
                                 d_w_out[256 * j:256 * (j + 1)], F32) for j in range(N_CHIPS)], axis=0)
        pieces = [p[None] for p in d_w_in]
        from_sib = _swap_halves(pieces + [g2])
        halves = [_add_cores(p, s, "add_cores_" + nm)[0]
                  for p, s, nm in zip(pieces, from_sib, ("small", "z", "fq", "fk", "fv", "g"))]
        parts = [_w_in_slabs(halves), _add_cores(g2, from_sib[-1], "add_cores_rest")]
        exchange.update(parts=parts, landed=_scatter_chips(parts))
        return parts[0][0, 0:16, 0:LANES]

    (loss_p, dx, dmeta, _, _, _, _, _, _, dg_pre, dg_post, dg_q, dg_kv,
     dfb) = _local_step(x[0], loss_target[0], meta_f, w_small, w_attn, w_gate, w_uq_f, w_ukv_f, w_bm, w_bf, w_o,
                        pre_norm_g, post_norm_g, mla_q_norm_g, mla_kv_norm_g, fox_forget_b, start_exchange)

    mine = [_add_chips(l, lax.dynamic_index_in_dim(p, me, 0, keepdims=False), nm)
            for l, p, nm in zip(exchange["landed"], exchange["parts"], ("add_chips_w_in", "add_chips_rest"))]
    theirs = _swap_reduced(mine)
    g_w_in, g_p2 = [jnp.concatenate([jnp.where(core == 0, a, b), jnp.where(core == 0, b, a)], axis=0)
                    for a, b in zip(mine, theirs)]
    g_w_uq, g_w_ukv, g_w_bm, g_w_bf, g_w_out = _unpack_p2(g_p2)
    g_w_in = g_w_in[None]

    vec = jnp.concatenate([dg_pre.reshape(8, 128), dg_post.reshape(8, 128), dg_q.reshape(2, 128), dg_kv,
                           dfb, _pad_lanes(loss_p), jnp.zeros((3, 128), F32), dmeta.reshape(128, 128)], axis=0)
    tot = _allreduce_small(vec)
    loss = tot[20, 0]
    g_meta = lax.dynamic_slice_in_dim(tot[24:].reshape(N_META, D_MODEL), 256 * me, 256, axis=1)

    def small_pack(pre, post, gq_, gkv_, fb_):
        return jnp.concatenate([pre.reshape(8, 128), post.reshape(8, 128), gq_.reshape(2, 128), gkv_,
                                _pad_lanes(fb_), jnp.zeros((4, 128), F32)], axis=0)

    def small_unpack(t):
        return (t[0:8].reshape(1, 1024), t[8:16].reshape(1, 1024), t[16:18].reshape(1, 256), t[18:19],
                t[19:20, 0:HEADS])

    g_small = jnp.concatenate([tot[0:20], jnp.zeros((4, 128), F32)], axis=0)
    sm = _adamw(small_pack(pre_norm_g, post_norm_g, mla_q_norm_g, mla_kv_norm_g, fox_forget_b), g_small,
                small_pack(m_pre_norm_g, m_post_norm_g, m_mla_q_norm_g, m_mla_kv_norm_g, m_fox_forget_b),
                small_pack(v_pre_norm_g, v_post_norm_g, v_mla_q_norm_g, v_mla_kv_norm_g, v_fox_forget_b),
                "adamw_small")
    g_pre, g_post, g_q, g_kv, g_fb = small_unpack(g_small)
    (d_pre, d_post, d_q, d_kv, d_fb), (nm_pre, nm_post, nm_q, nm_kv, nm_fb), (nv_pre, nv_post, nv_q, nv_kv, nv_fb) = (
        small_unpack(t) for t in sm)

    d_meta, nm_meta, nv_meta = _adamw(meta_tokens, g_meta, m_meta_tokens, v_meta_tokens, "adamw_meta")
    d_win, nm_win, nv_win = (t.T[None] for t in _adamw(w_in[0].T, g_w_in[0].T, m_w_in[0].T, v_w_in[0].T,
                                                       "adamw_w_in"))
    d_wuq, nm_wuq, nv_wuq = _adamw(w_uq[0], g_w_uq, m_w_uq[0], v_w_uq[0], "adamw_w_uq")
    d_wukv, nm_wukv, nv_wukv = _adamw(w_ukv[0], g_w_ukv, m_w_ukv[0], v_w_ukv[0], "adamw_w_ukv")
    d_wbm, nm_wbm, nv_wbm = _adamw(w_br_mla[0], g_w_bm, m_w_br_mla[0], v_w_br_mla[0], "adamw_w_br_mla")
    d_wbf, nm_wbf, nv_wbf = _adamw(w_br_fox[0], g_w_bf, m_w_br_fox[0], v_w_br_fox[0], "adamw_w_br_fox")
    d_wo, nm_wo, nv_wo = _adamw(w_out[0], g_w_out, m_w_out[0], v_w_out[0], "adamw_w_out")

    def group(meta_, pre, win, fb_, q_, kv_, wuq, wukv, wbm, wbf, wo, post):
        return (meta_, pre, win, fb_, q_, kv_, wuq[None], wukv[None], wbm[None], wbf[None], wo[None], post)

    grads = group(g_meta, g_pre, g_w_in, g_fb, g_q, g_kv, g_w_uq, g_w_ukv, g_w_bm, g_w_bf, g_w_out, g_post)
    deltas = group(d_meta, d_pre, d_win, d_fb, d_q, d_kv, d_wuq, d_wukv, d_wbm, d_wbf, d_wo, d_post)
    new_m = group(nm_meta, nm_pre, nm_win, nm_fb, nm_q, nm_kv, nm_wuq, nm_wukv, nm_wbm, nm_wbf, nm_wo, nm_post)
    new_v = group(nv_meta, nv_pre, nv_win, nv_fb, nv_q, nv_kv, nv_wuq, nv_wukv, nv_wbm, nv_wbf, nv_wo, nv_post)
    return (loss, dx[None], *grads, *deltas, *new_m, *new_v)
```

```python
import math

import jax
import jax.numpy as jnp
import numpy as np
from jax import lax
from jax.experimental import pallas as pl
from jax.experimental.pallas import tpu as pltpu
from jax.experimental.pallas import tpu_sc as plsc

F32 = jnp.float32
BF16 = jnp.bfloat16

D_MODEL = 1024
N_META = 16
RMS_EPS = 1e-6
HEADS = 16
PAIRS = HEADS // 2
HEAD_DIM = 64
LANES = 128
MLA_ROPE = 32
MLA_SCALE = 1.0 / math.sqrt(64 + 32)
FOX_SCALE = 1.0 / math.sqrt(64)
ROPE_THETA = 10000.0

PAD = 256
BLK = 256
QB = 512
UNROLL = 4
NEG = -1e30

C_CQ, C_CKV, C_KPE, C_ZMLA, C_FQ, C_FK, C_FV, C_FL, C_ZFOX, C_GA, C_GB, C_END = (
    0, 256, 384, 416, 1440, 2464, 3488, 4512, 4528, 5552, 6576, 7600)
SMALL_W = 640
W_IN_SHARD = 1900

P2_ROWS = 928
N_CHIPS = 4

ADAM_LR = 0.001
ADAM_B1 = 0.9
ADAM_B2 = 0.999
ADAM_EPS = 1e-08
ADAM_WD = 0.01
ADAM_STEP = 10

VMEM_BIG = 56 * 1024 * 1024
MM_VMEM_BUDGET = 44 * 1024 * 1024
MESH = pl.DeviceIdType.MESH


def _cp(dims, vmem=None):
    return pltpu.CompilerParams(dimension_semantics=dims, vmem_limit_bytes=vmem)


def _dot(a, b, ca, cb):
    return lax.dot_general(a, b, (((ca,), (cb,)), ((), ())), preferred_element_type=F32)


def _sigmoid(x):
    return 1.0 / (1.0 + jnp.exp(-x))


def _tile(n, cands):
    for c in cands:
        if n % c == 0:
            return c
    return n


def _mm(a, b, *, mode, out_dtype, name, acc=None, epilogue=None, row_ins=(), after=None):
    if mode == "nn":
        (M, K), N = a.shape, b.shape[1]
    elif mode == "nt":
        (M, K), N = a.shape, b.shape[0]
    else:
        (K, M), N = a.shape, b.shape[1]
    tm = _tile(M, (1088, 1024)) if M > 1024 else M
    tn = _tile(N, (1024,)) if N > 1024 else N
    nk = 1
    while True:
        tk = K // nk
        need = 2 * tk * (tm * a.dtype.itemsize + tn * b.dtype.itemsize) + tm * tn * (
            2 * jnp.dtype(out_dtype).itemsize + (8 if acc is not None else 0) + (4 if nk > 1 else 0))
        if need <= MM_VMEM_BUDGET or (tk // 2) % (16 if mode == "tn" else LANES) or tk <= 512:
            break
        nk *= 2
    ca, cb = {"nn": (1, 0), "nt": (1, 1), "tn": (0, 0)}[mode]
    a_spec = (pl.BlockSpec((tk, tm), lambda j, i, k: (k, i)) if mode == "tn"
              else pl.BlockSpec((tm, tk), lambda j, i, k: (i, k)))
    b_spec = (pl.BlockSpec((tn, tk), lambda j, i, k: (j, k)) if mode == "nt"
              else pl.BlockSpec((tk, tn), lambda j, i, k: (k, j)))
    o_spec = pl.BlockSpec((tm, tn), lambda j, i, k: (i, j))
    has_acc = acc is not None

    nrow = len(row_ins)

    def body(*refs):
        a_ref, b_ref = refs[0], refs[1]
        acc_ref = refs[2] if has_acc else None
        rows = refs[2 + has_acc:2 + has_acc + nrow]
        o_ref = refs[2 + has_acc + nrow + (after is not None)]

        def store(tile):
            if epilogue is not None:
                tile = epilogue(tile, *[r[...] for r in rows])
            o_ref[...] = tile.astype(out_dtype)

        part = _dot(a_ref[...].astype(BF16), b_ref[...].astype(BF16), ca, cb)
        if nk == 1:
            store(part + acc_ref[...] if has_acc else part)
        else:
            sc = refs[-1]
            k = pl.program_id(2)

            @pl.when(k == 0)
            def _():
                sc[...] = part + acc_ref[...] if has_acc else part

            @pl.when(k > 0)
            def _():
                sc[...] += part

            @pl.when(k == nk - 1)
            def _():
                store(sc[...])

    ins = [a, b] + ([acc] if has_acc else []) + list(row_ins)
    in_specs = ([a_spec, b_spec] + ([o_spec] if has_acc else [])
                + [pl.BlockSpec((tm, r.shape[1]), lambda j, i, k: (i, 0)) for r in row_ins])
    if after is not None:
        ins.append(after)
        in_specs.append(pl.BlockSpec(after.shape, lambda j, i, k: (0,) * after.ndim))
    return pl.pallas_call(
        body, name=name, grid=(N // tn, M // tm, nk), in_specs=in_specs, out_specs=o_spec,
        out_shape=jax.ShapeDtypeStruct((M, N), out_dtype),
        scratch_shapes=[pltpu.VMEM((tm, tn), F32)] if nk > 1 else [],
        compiler_params=_cp(("parallel", "parallel", "arbitrary"), VMEM_BIG))(*ins)


def _mm_sum_nt(pairs, *, name, after=None):
    n = len(pairs)
    M, N = pairs[0][0].shape[0], pairs[0][1].shape[0]
    tm = _tile(M, (272,))

    def body(*refs):
        o_ref = refs[2 * n + (after is not None)]
        tot = _dot(refs[0][...].astype(BF16), refs[n][...].astype(BF16), 1, 1)
        for i in range(1, n):
            tot = tot + _dot(refs[i][...].astype(BF16), refs[n + i][...].astype(BF16), 1, 1)
        o_ref[...] = tot

    ins = [a for a, _ in pairs] + [b for _, b in pairs]
    in_specs = ([pl.BlockSpec((tm, a.shape[1]), lambda i: (i, 0)) for a, _ in pairs]
                + [pl.BlockSpec(b.shape, lambda i: (0, 0)) for _, b in pairs])
    if after is not None:
        ins.append(after)
        in_specs.append(pl.BlockSpec(after.shape, lambda i: (0,) * after.ndim))
    return pl.pallas_call(
        body, name=name, grid=(M // tm,), in_specs=in_specs, out_specs=pl.BlockSpec((tm, N), lambda i: (i, 0)),
        out_shape=jax.ShapeDtypeStruct((M, N), F32), compiler_params=_cp(("parallel",), VMEM_BIG))(*ins)


def _row(w):
    return pl.BlockSpec((BLK, w), lambda i: (i, 0))


def _rowc(w, c):
    return pl.BlockSpec((BLK, w), lambda i: (i, c))


def _full(shape):
    return pl.BlockSpec(shape, lambda i: tuple(0 for _ in shape))


def _rope(x, c, s):
    lane = lax.broadcasted_iota(jnp.int32, x.shape, 1)
    is_x1 = ((lane >> 4) & 1) == 0
    partner = jnp.where(is_x1, pltpu.roll(x, LANES - 16, 1), pltpu.roll(x, 16, 1))
    return x * c + partner * s


def _row_valid(i):
    rows = i * BLK + lax.broadcasted_iota(jnp.int32, (BLK, 1), 0)
    return (rows < N_META) | (rows >= PAD)


def _shift_rows(w):
    return pl.BlockSpec((BLK, w), lambda i: (jnp.maximum(i - 1, 0), 0))


def _h_block(i, x_ref, meta_ref):
    head = jnp.concatenate([meta_ref[...], jnp.zeros((BLK - N_META, D_MODEL), F32)], axis=0)
    return jnp.where(i == 0, head, x_ref[...])


def _rms_pre(x2, meta, g):
    lp = PAD + x2.shape[0]

    def body(x_ref, meta_ref, g_ref, u_ref):
        hv = _h_block(pl.program_id(0), x_ref, meta_ref)
        r = lax.rsqrt(jnp.mean(hv * hv, axis=-1, keepdims=True) + RMS_EPS)
        u_ref[...] = (hv * r * g_ref[...]).astype(BF16)

    return pl.pallas_call(
        body, name="rms_pre", grid=(lp // BLK,),
        in_specs=[_shift_rows(D_MODEL), _full((N_META, D_MODEL)), _full((1, D_MODEL))], out_specs=_row(D_MODEL),
        out_shape=jax.ShapeDtypeStruct((lp, D_MODEL), BF16),
        compiler_params=_cp(("parallel",)))(x2, meta, g)


def _split3(x):
    hi = x.astype(BF16)
    r1 = x - hi.astype(F32)
    mid = r1.astype(BF16)
    lo = (r1 - mid.astype(F32)).astype(BF16)
    return hi, mid, lo


def _small_prep(small, gq, gkv, fb, ctab, stab, tri):
    lp = small.shape[0]

    def body(sm_ref, gq_ref, gkv_ref, fb_ref, c_ref, s_ref, tri_ref, qn_ref, kvn_ref, kr_ref, ncum_ref, carry):
        i = pl.program_id(0)

        @pl.when(i == 0)
        def _():
            carry[...] = jnp.zeros_like(carry)

        cq = sm_ref[:, 0:256]
        r = lax.rsqrt(jnp.mean(cq * cq, axis=-1, keepdims=True) + RMS_EPS)
        qn_ref[...] = (cq * r * gq_ref[...]).astype(BF16)
        ckv = sm_ref[:, 256:384]
        r = lax.rsqrt(jnp.mean(ckv * ckv, axis=-1, keepdims=True) + RMS_EPS)
        kvn_ref[...] = (ckv * r * gkv_ref[...]).astype(BF16)
        kr_ref[...] = _rope(sm_ref[:, 384:512], c_ref[...], s_ref[...]).astype(BF16)
        fl = sm_ref[:, 512:640] + fb_ref[...]
        lf = jnp.minimum(fl, 0.0) - jnp.log(1.0 + jnp.exp(-jnp.abs(fl)))
        lf = jnp.where(_row_valid(i), lf, 0.0)
        hi, mid, lo = _split3(lf)
        t = tri_ref[...]
        cum = (_dot(t, hi, 1, 0) + _dot(t, mid, 1, 0)) + _dot(t, lo, 1, 0) + carry[...]
        ncum_ref[...] = -cum
        carry[...] = -ncum_ref[BLK - 1:BLK, :]

    return pl.pallas_call(
        body, name="small_prep", grid=(lp // BLK,),
        in_specs=[_row(SMALL_W), _full((1, 256)), _full((1, 128)), _full((1, 128)), _row(128), _row(128),
                  _full((BLK, BLK))],
        out_specs=[_row(256), _row(128), _row(128), _row(128)],
        out_shape=[jax.ShapeDtypeStruct((lp, 256), BF16), jax.ShapeDtypeStruct((lp, 128), BF16),
                   jax.ShapeDtypeStruct((lp, 128), BF16), jax.ShapeDtypeStruct((lp, 128), F32)],
        scratch_shapes=[pltpu.VMEM((1, 128), F32)],
        compiler_params=_cp(("arbitrary",)))(small, gq, gkv, fb, ctab, stab, tri)


def _rope_pairs(tile, c, s):
    out = []
    for lo in range(0, tile.shape[1], 256):
        out += [tile[:, lo:lo + 128], _rope(tile[:, lo + 128:lo + 256], c, s)]
    return jnp.concatenate(out, axis=1)


def _gate_fwd(o_mla, o_fox, gate):
    lp = o_mla.shape[0]

    def body(om_ref, of_ref, zm_ref, zf_ref, am_ref, af_ref):
        zm = zm_ref[...].astype(F32)
        am_ref[...] = (om_ref[...] * (zm * _sigmoid(zm))).astype(BF16)
        zf = zf_ref[...].astype(F32)
        af_ref[...] = (of_ref[...] * (zf * _sigmoid(zf))).astype(BF16)

    return pl.pallas_call(
        body, name="gate_fwd", grid=(lp // BLK,),
        in_specs=[_row(D_MODEL), _row(D_MODEL), _rowc(D_MODEL, 0), _rowc(D_MODEL, 1)],
        out_specs=[_row(D_MODEL), _row(D_MODEL)],
        out_shape=[jax.ShapeDtypeStruct((lp, D_MODEL), BF16)] * 2,
        compiler_params=_cp(("parallel",)))(o_mla, o_fox, gate, gate)


def _merge_fwd(gate, y_mla, y_fox):
    lp = y_mla.shape[0]

    def body(ga_ref, gb_ref, ym_ref, yf_ref, m_ref):
        sa = _sigmoid(ga_ref[...].astype(F32))
        sb = _sigmoid(gb_ref[...].astype(F32))
        m_ref[...] = (sa * ym_ref[...] + sb * yf_ref[...]).astype(BF16)

    return pl.pallas_call(
        body, name="merge_fwd", grid=(lp // BLK,),
        in_specs=[_rowc(D_MODEL, 2), _rowc(D_MODEL, 3), _row(D_MODEL), _row(D_MODEL)],
        out_specs=_row(D_MODEL), out_shape=jax.ShapeDtypeStruct((lp, D_MODEL), BF16),
        compiler_params=_cp(("parallel",)))(gate, gate, y_mla, y_fox)


def _tail(x2, mixed, tgt, gpost):
    lp = mixed.shape[0]
    shift = _shift_rows(D_MODEL)

    def body(h_ref, mx_ref, t_ref, g_ref, dmx_ref, dy_ref, loss_ref, dg_ref):
        i = pl.program_id(0)

        @pl.when(i == 0)
        def _():
            loss_ref[...] = jnp.zeros_like(loss_ref)
            dg_ref[...] = jnp.zeros_like(dg_ref)
            dmx_ref[...] = jnp.zeros_like(dmx_ref)
            dy_ref[...] = jnp.zeros_like(dy_ref)

        @pl.when(i > 0)
        def _():
            mx = mx_ref[...]
            g = g_ref[...]
            r = lax.rsqrt(jnp.mean(mx * mx, axis=-1, keepdims=True) + RMS_EPS)
            nrm = mx * r
            e = (h_ref[...] + nrm * g) - t_ref[...]
            loss_ref[...] += jnp.sum(0.5 * jnp.sum(e * e, axis=-1, keepdims=True) * (1.0 / D_MODEL),
                                     axis=0, keepdims=True)
            dy = e * (1.0 / D_MODEL)
            dy_ref[...] = dy
            dg_ref[...] += jnp.sum(dy * nrm, axis=0, keepdims=True)
            w = dy * g
            dot = jnp.mean(w * mx, axis=-1, keepdims=True)
            dmx_ref[...] = (r * w - mx * (r * r * r * dot)).astype(BF16)

    return pl.pallas_call(
        body, name="tail", grid=(lp // BLK,),
        in_specs=[shift, _row(D_MODEL), shift, _full((1, D_MODEL))],
        out_specs=[_row(D_MODEL), _row(D_MODEL), _full((1, 1)), _full((1, D_MODEL))],
        out_shape=[jax.ShapeDtypeStruct((lp, D_MODEL), BF16), jax.ShapeDtypeStruct((lp, D_MODEL), F32),
                   jax.ShapeDtypeStruct((1, 1), F32), jax.ShapeDtypeStruct((1, D_MODEL), F32)],
        compiler_params=_cp(("arbitrary",)))(x2, mixed, tgt, gpost)


def _merge_bwd(dm, gate, y_mla, y_fox):
    lp = dm.shape[0]

    def body(dm_ref, ga_ref, gb_ref, ym_ref, yf_ref, dym_ref, dyf_ref, dg_ref):
        dm_v = dm_ref[...].astype(F32)
        sa = _sigmoid(ga_ref[...].astype(F32))
        sb = _sigmoid(gb_ref[...].astype(F32))
        dym_ref[...] = (dm_v * sa).astype(BF16)
        dyf_ref[...] = (dm_v * sb).astype(BF16)
        dg_ref[:, 0:D_MODEL] = (dm_v * ym_ref[...] * (sa * (1.0 - sa))).astype(BF16)
        dg_ref[:, D_MODEL:2 * D_MODEL] = (dm_v * yf_ref[...] * (sb * (1.0 - sb))).astype(BF16)

    return pl.pallas_call(
        body, name="merge_bwd", grid=(lp // BLK,),
        in_specs=[_row(D_MODEL), _rowc(D_MODEL, 2), _rowc(D_MODEL, 3), _row(D_MODEL), _row(D_MODEL)],
        out_specs=[_row(D_MODEL), _row(D_MODEL), _row(2 * D_MODEL)],
        out_shape=[jax.ShapeDtypeStruct((lp, D_MODEL), BF16), jax.ShapeDtypeStruct((lp, D_MODEL), BF16),
                   jax.ShapeDtypeStruct((lp, 2 * D_MODEL), BF16)],
        compiler_params=_cp(("parallel",)))(dm, gate, gate, y_mla, y_fox)


def _gate_bwd(da_mla, da_fox, o_mla, o_fox, gate):
    lp = da_mla.shape[0]

    def one(da, o, z, head_of_col):
        sg = _sigmoid(z)
        do = (da * (z * sg)).astype(BF16)
        dz = da * o * (sg * (1.0 + z * (1.0 - sg)))
        delta = sum(_dot(part, head_of_col, 1, 0) for part in _split3(do.astype(F32) * o))
        return do, dz.astype(BF16), delta

    def body(dam_ref, daf_ref, om_ref, of_ref, zm_ref, zf_ref, dom_ref, dof_ref, dz_ref, dlm_ref, dlf_ref):
        f32 = lambda r: r[...].astype(F32)
        head_of_col = (lax.broadcasted_iota(jnp.int32, (D_MODEL, LANES), 0) // HEAD_DIM
                       == lax.broadcasted_iota(jnp.int32, (D_MODEL, LANES), 1)).astype(BF16)
        dom_ref[...], dz_ref[:, 0:D_MODEL], dlm_ref[...] = one(f32(dam_ref), f32(om_ref), f32(zm_ref), head_of_col)
        dof_ref[...], dz_ref[:, D_MODEL:2 * D_MODEL], dlf_ref[...] = one(f32(daf_ref), f32(of_ref), f32(zf_ref),
                                                                        head_of_col)

    return pl.pallas_call(
        body, name="gate_bwd", grid=(lp // BLK,),
        in_specs=[_row(D_MODEL)] * 4 + [_rowc(D_MODEL, 0), _rowc(D_MODEL, 1)],
        out_specs=[_row(D_MODEL), _row(D_MODEL), _row(2 * D_MODEL), _row(LANES), _row(LANES)],
        out_shape=[jax.ShapeDtypeStruct((lp, D_MODEL), BF16), jax.ShapeDtypeStruct((lp, D_MODEL), BF16),
                   jax.ShapeDtypeStruct((lp, 2 * D_MODEL), BF16), jax.ShapeDtypeStruct((lp, LANES), F32),
                   jax.ShapeDtypeStruct((lp, LANES), F32)],
        compiler_params=_cp(("parallel",)))(da_mla, da_fox, o_mla, o_fox, gate, gate)


def _small_bwd(small, dqn, dkvn, dkr, dcol_t, drow_t, gq, gkv, fb, ctab, stab, triu):
    lp = small.shape[0]
    nb = lp // BLK

    def rrow(w):
        return pl.BlockSpec((BLK, w), lambda i: (nb - 1 - i, 0))

    def body(sm_ref, dqn_ref, dkvn_ref, dkr_ref, dcol_ref, drow_ref, gq_ref, gkv_ref, fb_ref, c_ref, s_ref, tri_ref,
             ds_ref, dgq_ref, dgkv_ref, dfb_ref, carry):
        i = pl.program_id(0)

        @pl.when(i == 0)
        def _():
            carry[...] = jnp.zeros_like(carry)
            dgq_ref[...] = jnp.zeros_like(dgq_ref)
            dgkv_ref[...] = jnp.zeros_like(dgkv_ref)
            dfb_ref[...] = jnp.zeros_like(dfb_ref)

        def norm_bwd(x, dn, g, dg_ref):
            r = lax.rsqrt(jnp.mean(x * x, axis=-1, keepdims=True) + RMS_EPS)
            dg_ref[...] += jnp.sum(dn * (x * r), axis=0, keepdims=True)
            w = dn * g
            dot = jnp.mean(w * x, axis=-1, keepdims=True)
            return r * w - x * (r * r * r * dot)

        ds_ref[:, 0:256] = norm_bwd(sm_ref[:, 0:256], dqn_ref[...], gq_ref[...], dgq_ref).astype(BF16)
        ds_ref[:, 256:384] = norm_bwd(sm_ref[:, 256:384], dkvn_ref[...], gkv_ref[...], dgkv_ref).astype(BF16)

        dk = dkr_ref[0]
        for p in range(1, PAIRS):
            dk = dk + dkr_ref[p]
        dk = _rope(dk, c_ref[...], -s_ref[...])
        lane = lax.broadcasted_iota(jnp.int32, dk.shape, 1)
        dk = jnp.where(lane < MLA_ROPE, dk + pltpu.roll(dk, LANES - MLA_ROPE, 1), 0.0)
        ds_ref[:, 384:512] = dk.astype(BF16)

        dcol = dcol_ref[0]
        for p in range(1, PAIRS):
            dcol = dcol + pltpu.roll(dcol_ref[p], 2 * p, 1)
        rows16 = jnp.concatenate([drow_ref[p, h:h + 1, :] for p in range(PAIRS) for h in range(2)], axis=0)
        eye = (lax.broadcasted_iota(jnp.int32, (HEADS, LANES), 0)
               == lax.broadcasted_iota(jnp.int32, (HEADS, LANES), 1)).astype(BF16)
        drow = sum(_dot(part, eye, 0, 0) for part in _split3(rows16))
        dcr = dcol - drow
        hi, mid, lo = _split3(dcr)
        t = tri_ref[...]
        suf = (_dot(t, hi, 1, 0) + _dot(t, mid, 1, 0)) + _dot(t, lo, 1, 0) + carry[...]
        fl = sm_ref[:, 512:640] + fb_ref[...]
        dfl = jnp.where(_row_valid(nb - 1 - i), -suf * _sigmoid(-fl), 0.0)
        ds_ref[:, 512:640] = dfl.astype(BF16)
        dfb_ref[...] += jnp.sum(dfl, axis=0, keepdims=True)
        carry[...] += jnp.sum(dcr, axis=0, keepdims=True)

    return pl.pallas_call(
        body, name="small_bwd", grid=(nb,),
        in_specs=[rrow(SMALL_W), rrow(256), rrow(128),
                  pl.BlockSpec((PAIRS, BLK, 128), lambda i: (0, nb - 1 - i, 0)),
                  pl.BlockSpec((PAIRS, BLK, 128), lambda i: (0, nb - 1 - i, 0)),
                  pl.BlockSpec((PAIRS, 2, BLK), lambda i: (0, 0, nb - 1 - i)),
                  _full((1, 256)), _full((1, 128)), _full((1, 128)), rrow(128), rrow(128), _full((BLK, BLK))],
        out_specs=[rrow(SMALL_W), _full((1, 256)), _full((1, 128)), _full((1, 128))],
        out_shape=[jax.ShapeDtypeStruct((lp, SMALL_W), BF16), jax.ShapeDtypeStruct((1, 256), F32),
                   jax.ShapeDtypeStruct((1, 128), F32), jax.ShapeDtypeStruct((1, 128), F32)],
        scratch_shapes=[pltpu.VMEM((1, 128), F32)],
        compiler_params=_cp(("arbitrary",)))(small, dqn, dkvn, dkr, dcol_t, drow_t, gq, gkv, fb, ctab, stab, triu)


def _pre_bwd(du, x2, meta, dy, gpre):
    s_rows = x2.shape[0]
    lp = PAD + s_rows
    shift = _shift_rows(D_MODEL)

    def body(du_ref, x_ref, meta_ref, dy_ref, g_ref, dx_ref, dmeta_ref, dg_ref):
        i = pl.program_id(0)

        @pl.when(i == 0)
        def _():
            dg_ref[...] = jnp.zeros_like(dg_ref)

        hv = _h_block(i, x_ref, meta_ref)
        duv = du_ref[...]
        r = lax.rsqrt(jnp.mean(hv * hv, axis=-1, keepdims=True) + RMS_EPS)
        dg_ref[...] += jnp.sum(duv * (hv * r), axis=0, keepdims=True)
        w = duv * g_ref[...]
        dot = jnp.mean(w * hv, axis=-1, keepdims=True)
        dh = dy_ref[...] + (r * w - hv * (r * r * r * dot))
        dx_ref[...] = dh

        @pl.when(i == 0)
        def _():
            dmeta_ref[...] = dh[0:N_META, :]

    return pl.pallas_call(
        body, name="pre_bwd", grid=(lp // BLK,),
        in_specs=[_row(D_MODEL), shift, _full((N_META, D_MODEL)), _row(D_MODEL), _full((1, D_MODEL))],
        out_specs=[shift, _full((N_META, D_MODEL)), _full((1, D_MODEL))],
        out_shape=[jax.ShapeDtypeStruct((s_rows, D_MODEL), F32), jax.ShapeDtypeStruct((N_META, D_MODEL), F32),
                   jax.ShapeDtypeStruct((1, D_MODEL), F32)],
        compiler_params=_cp(("arbitrary",)))(du, x2, meta, dy, gpre)


def _pair_masks(rope):
    lane = lax.broadcasted_iota(jnp.int32, (1, LANES), 1)
    mas = [lane < HEAD_DIM, lane >= HEAD_DIM]
    if not rope:
        return mas, mas
    wide = lax.broadcasted_iota(jnp.int32, (1, 2 * LANES), 1)
    rope_lo = LANES + MLA_ROPE
    return mas, [(wide < HEAD_DIM) | ((wide >= LANES) & (wide < rope_lo)),
                 ((wide >= HEAD_DIM) & (wide < LANES)) | ((wide >= rope_lo) & (wide < rope_lo + MLA_ROPE))]


def _mask2(x, masks):
    return [jnp.where(m, x, jnp.zeros_like(x)) for m in masks]


def _attn_fwd(q, k, v, *, kr=None, nbrep=None, scale, qcol, kcol, vcol, name):
    lp = q.shape[0]
    nq = 1 + (lp - PAD) // QB
    rope = kr is not None
    bias = nbrep is not None
    qw = 256 if rope else 128

    def body(*refs):
        it = iter(refs)
        q_ref, k_ref, v_ref = next(it), next(it), next(it)
        kr_ref = next(it) if rope else None
        nb_ref = next(it) if bias else None
        o_ref, lse_ref = next(it), next(it)
        i = pl.program_id(1)
        r0 = pl.multiple_of(jnp.where(i == 0, 0, PAD + QB * (i - 1)), BLK)
        b0 = r0 // BLK
        mas, hmask = _pair_masks(rope)
        qh = _mask2(q_ref[pl.ds(r0, QB), :], hmask)
        if bias:
            qh = [x * scale for x in qh]

        def causal(kc, n):
            key = kc * BLK + lax.broadcasted_iota(jnp.int32, (n, QB), 0)
            return (key <= r0 + lax.broadcasted_iota(jnp.int32, (n, QB), 1)) & ((kc > 0) | (n == N_META))

        def update(kcs, carry, masks, n=BLK):
            stats, acc = carry[:4], carry[4]
            k0s = [pl.multiple_of(kc * BLK, BLK) for kc in kcs]
            kks = [k_ref[pl.ds(k0, n), :] for k0 in k0s]
            if rope:
                kks = [jnp.concatenate([kk, kr_ref[pl.ds(k0, n), :]], axis=1) for kk, k0 in zip(kks, k0s)]
            new_stats, alphas, ps = [], [], [[] for _ in kcs]
            for h in range(2):
                m_prev, l_prev = stats[2 * h], stats[2 * h + 1]
                ss = []
                for kk, k0, mask in zip(kks, k0s, masks):
                    s = _dot(kk, qh[h], 1, 1)
                    if rope:
                        s = s * scale
                    if bias:
                        nbc = nb_ref[h, pl.ds(k0, n), :]
                        s = s + jnp.concatenate([nbc] * (QB // LANES), axis=1)
                    if mask is not None:
                        s = jnp.where(mask, s, NEG)
                    ss.append(s)
                m_new = m_prev
                for s in ss:
                    m_new = jnp.maximum(m_new, jnp.max(s, axis=0, keepdims=True))
                alpha = jnp.exp(m_prev - m_new)
                l_new = alpha * l_prev
                for j, s in enumerate(ss):
                    p = jnp.exp(s - m_new)
                    l_new = l_new + jnp.sum(p, axis=0, keepdims=True)
                    ps[j].append(p.astype(BF16))
                new_stats += [m_new, l_new]
                alphas.append(alpha)
            vcat = jnp.concatenate([x for k0 in k0s for x in _mask2(v_ref[pl.ds(k0, n), :], mas)], axis=0)
            pv = _dot(vcat, jnp.concatenate([p for pj in ps for p in pj], axis=0), 0, 0)
            a_full = jnp.concatenate([jnp.broadcast_to(a, (HEAD_DIM, QB)) for a in alphas], axis=0)
            return (*new_stats, a_full * acc + pv)

        neg = jnp.full((1, QB), NEG, F32)
        zero = jnp.zeros((1, QB), F32)
        c = update([0], (neg, zero, neg, zero, jnp.zeros((LANES, QB), F32)), [causal(0, N_META)], N_META)
        n_mid = jnp.maximum(b0 - 1, 0)
        c = lax.fori_loop(0, n_mid // 4, lambda t, cr: update([4 * t + u for u in (1, 2, 3, 4)], cr, [None] * 4), c)
        c = lax.fori_loop(0, (n_mid % 4) // 2, lambda t, cr: update([n_mid - 1, n_mid], cr, [None, None]), c)
        c = update([b0, b0 + 1], c, [causal(b0, BLK), causal(b0 + 1, BLK)])
        inv =jnp.concatenate([jnp.broadcast_to(1.0 / c[1], (HEAD_DIM, QB)),
                               jnp.broadcast_to(1.0 / c[3], (HEAD_DIM, QB))], axis=0)
        o_t = (c[4] * inv).T.astype(BF16)
        lses = [c[0] + jnp.log(c[1]), c[2] + jnp.log(c[3])]
        o_ref[pl.ds(r0, BLK), :] = o_t[0:BLK]
        for h in range(2):
            lse_ref[0, h:h + 1, pl.ds(r0, BLK)] = lses[h][:, 0:BLK]

        @pl.when(i > 0)
        def _():
            r1 = pl.multiple_of(r0 + BLK, BLK)
            o_ref[pl.ds(r1, QB - BLK), :] = o_t[BLK:QB]
            for h in range(2):
                lse_ref[0, h:h + 1, pl.ds(r1, QB - BLK)] = lses[h][:, BLK:QB]

    in_specs = [pl.BlockSpec((lp, qw), lambda p, i: (0, qcol + p)),
                pl.BlockSpec((lp, 128), lambda p, i: (0, kcol(p))),
                pl.BlockSpec((lp, 128), lambda p, i: (0, vcol(p)))]
    ins = [q, k, v]
    if rope:
        in_specs.append(pl.BlockSpec((lp, 128), lambda p, i: (0, 0)))
        ins.append(kr)
    if bias:
        in_specs.append(pl.BlockSpec((2, lp, 128), lambda p, i: (p, 0, 0)))
        ins.append(nbrep)
    return pl.pallas_call(
        body, name=name, grid=(PAIRS, nq), in_specs=in_specs,
        out_specs=[pl.BlockSpec((lp, 128), lambda p, i: (0, p)),
                   pl.BlockSpec((1, 2, lp), lambda p, i: (p, 0, 0))],
        out_shape=[jax.ShapeDtypeStruct((lp, D_MODEL), BF16), jax.ShapeDtypeStruct((PAIRS, 2, lp), F32)],
        compiler_params=_cp(("parallel", "arbitrary"), VMEM_BIG))(*ins)


def _attn_bwd(q, k, v, do, delta, lse, *, kr=None, rtabs=None, nbrep=None, scale, qcol, kcol, vcol, name):
    lp = q.shape[0]
    nb = lp // BLK
    rope = kr is not None
    bias = nbrep is not None
    qw = 256 if rope else 128

    def body(*refs):
        it = iter(refs)
        q_ref, k_ref, v_ref = next(it), next(it), next(it)
        kr_ref = next(it) if rope else None
        nb_ref = next(it) if bias else None
        do_ref, dl_ref, lse_ref = next(it), next(it), next(it)
        ct_ref, st_ref = (next(it), next(it)) if rope else (None, None)
        dq_out, dk_ref, dv_ref = next(it), next(it), next(it)
        x_ref = next(it)
        drow_ref = next(it) if bias else None
        dq_ref = next(it)
        kb = pl.program_id(1)
        mas, hmask = _pair_masks(rope)
        lane = lax.broadcasted_iota(jnp.int32, (1, LANES), 1)

        @pl.when(kb == 0)
        def _():
            dq_ref[...] = jnp.zeros_like(dq_ref)
            if bias:
                drow_ref[...] = jnp.zeros_like(drow_ref)

        def key_pass(n, w):
            kk = k_ref[0:n, :]
            if rope:
                kk = jnp.concatenate([kk, kr_ref[0:n, :]], axis=1)
            vh = _mask2(v_ref[0:n, :], mas)
            kcat = jnp.concatenate(_mask2(kk, hmask), axis=0)
            if bias:
                kcat = kcat * scale
                nbc = [jnp.concatenate([nb_ref[h, 0:n, :]] * (w // LANES), axis=1) for h in range(2)]
            diag_mask = (lax.broadcasted_iota(jnp.int32, (n, w), 0) <= lax.broadcasted_iota(jnp.int32, (n, w), 1))

            def chunk(qc, carry, mask):
                carry = list(carry)
                q0 = qc * w if isinstance(qc, int) else pl.multiple_of(qc * w, w)
                dov = do_ref[pl.ds(q0, w), :]
                doh = _mask2(dov, mas)
                qh = _mask2(q_ref[pl.ds(q0, w), :], hmask)
                if bias:
                    qh = [x * scale for x in qh]
                pbs, dss = [], []
                for h in range(2):
                    s = _dot(kk, qh[h], 1, 1)
                    if rope:
                        s = s * scale
                    if bias:
                        s = s + nbc[h]
                    p = jnp.exp(s - lse_ref[0, h:h + 1, pl.ds(q0, w)])
                    if mask is not None:
                        p = jnp.where(mask, p, 0.0)
                    ds = p * (_dot(vh[h], dov, 1, 1) - dl_ref[0, h:h + 1, pl.ds(q0, w)])
                    if bias:
                        drow_ref[0, h:h + 1, pl.ds(q0, w)] += jnp.sum(ds, axis=0, keepdims=True)
                        carry[2 + h] = carry[2 + h] + jnp.sum(ds, axis=1, keepdims=True)
                    else:
                        ds = ds * scale
                    pbs.append(p.astype(BF16))
                    dss.append(ds.astype(BF16))
                ds_lanes = jnp.concatenate(dss, axis=1)
                ds_rows = jnp.concatenate(dss, axis=0)
                carry[0] = carry[0] + _dot(ds_lanes, jnp.concatenate(qh, axis=0), 1, 0)
                carry[1] = carry[1] + _dot(jnp.concatenate(pbs, axis=1), jnp.concatenate(doh, axis=0), 1, 0)
                dq_ref[pl.ds(q0, w), :] += _dot(ds_rows, kcat, 0, 0)
                return tuple(carry)

            c = [jnp.zeros((n, qw), F32), jnp.zeros((n, LANES), F32)]
            if bias:
                c += [jnp.zeros((n, 1), F32), jnp.zeros((n, 1), F32)]
            c = tuple(c)
            if w != BLK:
                for qc in range(lp // w):
                    c = chunk(qc, c, diag_mask if qc == 0 else None)
            else:
                groups = (nb - kb) // UNROLL

                def several(t, cr):
                    for u in range(UNROLL):
                        cr = chunk(kb + UNROLL * t + u, cr, (diag_mask | (t > 0)) if u == 0 else None)
                    return cr

                c = lax.fori_loop(0, groups, several, c)
                start = kb + UNROLL * groups
                pairs = (nb - start) // 2

                def two(t, cr):
                    qc = start + 2 * t
                    return chunk(qc + 1, chunk(qc, cr, diag_mask | (qc > kb)), None)

                c = lax.fori_loop(0, pairs, two, c)
                c = lax.fori_loop(start + 2 * pairs, nb, lambda qc, cr: chunk(qc, cr, diag_mask | (qc > kb)), c)

            def rows(a, dtype):
                a = a.astype(dtype)
                return a if n == BLK else jnp.concatenate([a, jnp.zeros((BLK - n, a.shape[1]), dtype)], axis=0)

            dk_ref[...] = rows(c[0][:, 0:LANES], BF16)
            dv_ref[...] = rows(c[1], BF16)
            if rope:
                x_ref[0] = rows(c[0][:, LANES:2 * LANES], F32)
            if bias:
                x_ref[0] = rows(jnp.where(lane == 0, c[2], jnp.where(lane == 1, c[3], 0.0)), F32)

        @pl.when(kb == 0)
        def _():
            key_pass(N_META, lp // 2)

        @pl.when(kb > 0)
        def _():
            key_pass(BLK, BLK)

        @pl.when(kb == nb - 1)
        def _():
            def fin(c, carry):
                r0 = pl.multiple_of(c * BLK, BLK)
                dq = dq_ref[pl.ds(r0, BLK), :]
                if rope:
                    back = _rope(dq[:, LANES:2 * LANES], ct_ref[pl.ds(r0, BLK), :], -st_ref[pl.ds(r0, BLK), :])
                    dq = jnp.concatenate([dq[:, 0:LANES], back], axis=1)
                dq_out[pl.ds(r0, BLK), :] = dq.astype(BF16)
                return carry

            lax.fori_loop(0, nb, fin, 0)

    in_specs = [pl.BlockSpec((lp, qw), lambda p, j: (0, qcol + p)),
                pl.BlockSpec((BLK, 128), lambda p, j: (j, kcol(p))),
                pl.BlockSpec((BLK, 128), lambda p, j: (j, vcol(p)))]
    ins = [q, k, v]
    if rope:
        in_specs.append(pl.BlockSpec((BLK, 128), lambda p, j: (j, 0)))
        ins.append(kr)
    if bias:
        in_specs.append(pl.BlockSpec((2, BLK, 128), lambda p, j: (p, j, 0)))
        ins.append(nbrep)
    in_specs += [pl.BlockSpec((lp, 128), lambda p, j: (0, p)), pl.BlockSpec((1, 2, lp), lambda p, j: (p, 0, 0)),
                 pl.BlockSpec((1, 2, lp), lambda p, j: (p, 0, 0))]
    ins += [do, delta, lse]
    if rope:
        in_specs += [pl.BlockSpec((lp, 128), lambda p, j: (0, 0))] * 2
        ins += list(rtabs)
    out_specs = [pl.BlockSpec((lp, qw), lambda p, j: (0, p)),
                 pl.BlockSpec((BLK, 128), lambda p, j: (j, p)),
                 pl.BlockSpec((BLK, 128), lambda p, j: (j, p)),
                 pl.BlockSpec((1, BLK, 128), lambda p, j: (p, j, 0))]
    out_shape = [jax.ShapeDtypeStruct((lp, PAIRS * qw), BF16), jax.ShapeDtypeStruct((lp, D_MODEL), BF16),
                 jax.ShapeDtypeStruct((lp, D_MODEL), BF16), jax.ShapeDtypeStruct((PAIRS, lp, 128), F32)]
    if bias:
        out_specs.append(pl.BlockSpec((1, 2, lp), lambda p, j: (p, 0, 0)))
        out_shape.append(jax.ShapeDtypeStruct((PAIRS, 2, lp), F32))
    return pl.pallas_call(
        body, name=name, grid=(PAIRS, nb), in_specs=in_specs, out_specs=out_specs, out_shape=out_shape,
        scratch_shapes=[pltpu.VMEM((lp, qw), F32)],
        compiler_params=_cp(("parallel", "arbitrary"), VMEM_BIG))(*ins)


def _adamw(w, g, m, v, name):
    lead = w.ndim - 2
    rows, cols = w.shape[lead:]
    big = rows * cols > 512 * 1024
    tr = 128 if big and rows % 128 == 0 else rows
    tc = 256 if big and tr == rows else cols

    def body(w_ref, g_ref, m_ref, v_ref, d_ref, nm_ref, nv_ref):
        gv = g_ref[...]
        nm = ADAM_B1 * m_ref[...] + (1.0 - ADAM_B1) * gv
        nv = ADAM_B2 * v_ref[...] + (1.0 - ADAM_B2) * (gv * gv)
        m_hat = nm / (1.0 - ADAM_B1 ** ADAM_STEP)
        v_hat = nv / (1.0 - ADAM_B2 ** ADAM_STEP)
        d_ref[...] = -ADAM_LR * (m_hat / (jnp.sqrt(v_hat) + ADAM_EPS) + ADAM_WD * w_ref[...])
        nm_ref[...] = nm
        nv_ref[...] = nv

    spec = pl.BlockSpec((1,) * lead + (tr, tc), lambda i, j: (0,) * lead + (i, j))
    return pl.pallas_call(
        body, name=name, grid=(rows // tr, cols // tc), in_specs=[spec] * 4, out_specs=[spec] * 3,
        out_shape=[jax.ShapeDtypeStruct(w.shape, F32)] * 3,
        compiler_params=_cp(("parallel", "parallel"), VMEM_BIG))(w, g, m, v)


def _add_cores(g, from_sib, name):
    n, rows, cols = g.shape
    half = rows // 2
    tr = _tile(half, (256, 240))
    nt = half // tr

    def body(lo_ref, hi_ref, s_ref, o_ref):
        mine = jnp.where(lax.axis_index("c") == 0, lo_ref[0], hi_ref[0])
        o_ref[0] = (mine + s_ref[0]).astype(BF16)

    return pl.pallas_call(
        body, name=name, grid=(n, nt),
        in_specs=[pl.BlockSpec((1, tr, cols), lambda j, i: (j, i, 0)),
                  pl.BlockSpec((1, tr, cols), lambda j, i: (j, nt + i, 0)),
                  pl.BlockSpec((1, tr, cols), lambda j, i: (j, i, 0))],
        out_specs=pl.BlockSpec((1, tr, cols), lambda j, i: (j, i, 0)),
        out_shape=jax.ShapeDtypeStruct((n, half, cols), BF16),
        compiler_params=_cp(("parallel", "parallel"), VMEM_BIG))(g, g, from_sib)


def _add_chips(x, own, name):
    n, rows, cols = x.shape
    tr = _tile(rows, (256, 240))

    def body(x_ref, own_ref, o_ref):
        me = 2 * lax.axis_index("x") + lax.axis_index("y")
        v = [jnp.where(me == k, own_ref[...], x_ref[k]).astype(F32) for k in range(N_CHIPS)]
        o_ref[...] = ((v[0] + v[1]) + v[2]) + v[3]

    return pl.pallas_call(
        body, name=name, grid=(rows // tr,),
        in_specs=[pl.BlockSpec((n, tr, cols), lambda i: (0, i, 0)), pl.BlockSpec((tr, cols), lambda i: (i, 0))],
        out_specs=pl.BlockSpec((tr, cols), lambda i: (i, 0)),
        out_shape=jax.ShapeDtypeStruct((rows, cols), F32), compiler_params=_cp(("parallel",), VMEM_BIG))(x, own)


def _axes():
    return lax.axis_index("x"), lax.axis_index("y"), lax.axis_index("c")


def _other_chips(x, y):
    return [(1 - x, y), (x, 1 - y), (1 - x, 1 - y)]


ANY = pl.BlockSpec(memory_space=pl.ANY)


def _rcopy(src, dst, send_sems, recv_sems, k, to):
    return pltpu.make_async_remote_copy(src_ref=src, dst_ref=dst, send_sem=send_sems.at[k], recv_sem=recv_sems.at[k],
                                        device_id=to, device_id_type=MESH)


def _gather_weights(shards, meta):
    n = len(shards)

    def body(*refs):
        srcs, meta_ref = refs[:n], refs[n]
        outs, mout_ref = refs[n + 1:2 * n + 1], refs[2 * n + 1]
        send_sems, recv_sems = refs[2 * n + 2:]
        x, y, c = _axes()
        me = 2 * x + y
        sib = (x, y, 1 - c)
        chips = _other_chips(x, y)

        def half(t, chip_idx, cc):
            hr = shards[t].shape[0] // 2
            return outs[t].at[chip_idx, pl.ds(cc * hr, hr), :]

        first = []
        for j, (px, py) in enumerate(chips):
            for t in range(n):
                hr = shards[t].shape[0] // 2
                first.append(_rcopy(srcs[t].at[pl.ds(c * hr, hr), :], half(t, me, c), send_sems, recv_sems,
                                    3 * t + j, (px, py, c)))
            first.append(_rcopy(meta_ref, mout_ref.at[me], send_sems, recv_sems, 3 * n + j, (px, py, c)))
        for cp in first:
            cp.start()
        passed = []
        for j, (px, py) in enumerate(chips):
            src_chip = 2 * px + py
            for t in range(n):
                _rcopy(half(t, src_chip, c), half(t, src_chip, c), send_sems, recv_sems, 3 * t + j, sib).wait_recv()
                fwd = _rcopy(half(t, src_chip, c), half(t, src_chip, c), send_sems, recv_sems, 3 * (n + 1 + t) + j, sib)
                fwd.start()
                passed.append(fwd)
            _rcopy(mout_ref.at[src_chip], mout_ref.at[src_chip], send_sems, recv_sems, 3 * n + j, sib).wait_recv()
        for j, (px, py) in enumerate(chips):
            src_chip = 2 * px + py
            for t in range(n):
                _rcopy(half(t, src_chip, 1 - c), half(t, src_chip, 1 - c), send_sems, recv_sems,
                       3 * (n + 1 + t) + j, sib).wait_recv()
        for cp in first + passed:
            cp.wait_send()

    nsem = 3 * (2 * n + 1)
    return pl.pallas_call(
        body, name="gather_weights", in_specs=[ANY] * (n + 1), out_specs=[ANY] * (n + 1),
        out_shape=[jax.ShapeDtypeStruct((N_CHIPS,) + s.shape, s.dtype) for s in shards]
        + [jax.ShapeDtypeStruct((N_CHIPS,) + meta.shape, meta.dtype)],
        scratch_shapes=[pltpu.SemaphoreType.DMA((nsem,)), pltpu.SemaphoreType.DMA((nsem,))])(*shards, meta)


def _gather_late(shard):
    rows, cols = shard.shape
    hr = rows // 2
    src = jax.new_ref(shard, memory_space=pltpu.MemorySpace.HBM)
    out = jax.empty_ref(jax.ShapeDtypeStruct((N_CHIPS, rows, cols), shard.dtype), memory_space=pltpu.MemorySpace.HBM)

    @pl.kernel(mesh=plsc.ScalarSubcoreMesh(axis_name="seq", num_cores=1), name="gather_late",
               scratch_types=(pltpu.SemaphoreType.DMA((6,)), pltpu.SemaphoreType.DMA((6,))),
               compiler_params=pltpu.CompilerParams(collective_id=1))
    def launch(send_sems, recv_sems):
        x, y, c = _axes()
        me = 2 * x + y
        sib = (x, y, 1 - c)
        chips = _other_chips(x, y)
        barrier = pltpu.get_barrier_semaphore()
        for px, py in chips:
            pl.semaphore_signal(barrier, inc=1, device_id=(px, py, c), device_id_type=MESH)
        pl.semaphore_signal(barrier, inc=1, device_id=sib, device_id_type=MESH)
        pl.semaphore_wait(barrier, 4)

        def half(chip_idx, cc):
            return out.at[chip_idx, pl.ds(cc * hr, hr), :]

        first = [_rcopy(src.at[pl.ds(c * hr, hr), :], half(me, c), send_sems, recv_sems, j, (px, py, c))
                 for j, (px, py) in enumerate(chips)]
        for cp in first:
            cp.start()
        passed = []
        for j, (px, py) in enumerate(chips):
            land = half(2 * px + py, c)
            _rcopy(land, land, send_sems, recv_sems, j, sib).wait_recv()
            fwd = _rcopy(land, land, send_sems, recv_sems, 3 + j, sib)
            fwd.start()
            passed.append(fwd)
        for j, (px, py) in enumerate(chips):
            land = half(2 * px + py, 1 - c)
            _rcopy(land, land, send_sems, recv_sems, 3 + j, sib).wait_recv()
        for cp in first + passed:
            cp.wait_send()

    launch()
    return out[...]


def _swap_halves(gs):
    n = len(gs)
    ncopies = sum(g.shape[0] for g in gs)

    def body(*refs):
        srcs, outs = refs[:n], refs[n:2 * n]
        send_sems, recv_sems = refs[2 * n:]
        x, y, c = _axes()
        cps = []
        for t in range(n):
            hr = gs[t].shape[1] // 2
            for j in range(gs[t].shape[0]):
                cps.append(_rcopy(srcs[t].at[j, pl.ds((1 - c) * hr, hr), :], outs[t].at[j], send_sems, recv_sems,
                                  len(cps), (x, y, 1 - c)))
        for cp in cps:
            cp.start()
        for cp in cps:
            cp.wait()

    return pl.pallas_call(
        body, name="swap_halves", in_specs=[ANY] * n, out_specs=[ANY] * n,
        out_shape=[jax.ShapeDtypeStruct((g.shape[0], g.shape[1] // 2, g.shape[2]), g.dtype) for g in gs],
        scratch_shapes=[pltpu.SemaphoreType.DMA((ncopies,)), pltpu.SemaphoreType.DMA((ncopies,))])(*gs)


def _scatter_chips(parts):
    n = len(parts)
    srcs = [jax.new_ref(p, memory_space=pltpu.MemorySpace.HBM) for p in parts]
    outs = [jax.empty_ref(jax.ShapeDtypeStruct(p.shape, p.dtype), memory_space=pltpu.MemorySpace.HBM) for p in parts]

    @pl.kernel(mesh=plsc.ScalarSubcoreMesh(axis_name="seq", num_cores=1), name="scatter_chips",
               scratch_types=(pltpu.SemaphoreType.DMA((3 * n,)), pltpu.SemaphoreType.DMA((3 * n,))),
               compiler_params=pltpu.CompilerParams(collective_id=0))
    def launch(send_sems, recv_sems):
        x, y, c = _axes()
        me = 2 * x + y
        chips = _other_chips(x, y)
        barrier = pltpu.get_barrier_semaphore()
        for px, py in chips:
            pl.semaphore_signal(barrier, inc=1, device_id=(px, py, c), device_id_type=MESH)
        pl.semaphore_wait(barrier, 3)
        cps = []
        for j, (px, py) in enumerate(chips):
            for t in range(n):
                cps.append(_rcopy(srcs[t].at[2 * px + py], outs[t].at[me], send_sems, recv_sems, 3 * t + j,
                                  (px, py, c)))
        for cp in cps:
            cp.start()
        for cp in cps:
            cp.wait()

    launch()
    return [o[...] for o in outs]


def _swap_reduced(rs):
    n = len(rs)

    def body(*refs):
        srcs, outs = refs[:n], refs[n:2 * n]
        send_sems, recv_sems = refs[2 * n:]
        x, y, c = _axes()
        cps = [_rcopy(srcs[t], outs[t], send_sems, recv_sems, t, (x, y, 1 - c)) for t in range(n)]
        for cp in cps:
            cp.start()
        for cp in cps:
            cp.wait()

    return pl.pallas_call(
        body, name="swap_reduced", in_specs=[ANY] * n, out_specs=[ANY] * n,
        out_shape=[jax.ShapeDtypeStruct(r.shape, r.dtype) for r in rs],
        scratch_shapes=[pltpu.SemaphoreType.DMA((n,)), pltpu.SemaphoreType.DMA((n,))])(*rs)


SMALL_ROWS = 24 + 128


def _allreduce_small(vec):
    def body(v_ref, out_ref, slots, send_sems, recv_sems):
        x, y, c = _axes()
        me = 4 * x + 2 * y + c
        slots[me] = v_ref[...]
        cps = []
        for k in range(1, 8):
            kx, ky, kc = (k >> 2) & 1, (k >> 1) & 1, k & 1
            peer = (1 - x if kx else x, 1 - y if ky else y, 1 - c if kc else c)
            cps.append(_rcopy(v_ref, slots.at[me], send_sems, recv_sems, k - 1, peer))
        for cp in cps:
            cp.start()
        for cp in cps:
            cp.wait()
        tot = slots[0]
        for k in range(1, 8):
            tot = tot + slots[k]
        out_ref[...] = tot

    return pl.pallas_call(
        body, name="allreduce_small",
        in_specs=[pl.BlockSpec(memory_space=pltpu.VMEM)], out_specs=pl.BlockSpec(memory_space=pltpu.VMEM),
        out_shape=jax.ShapeDtypeStruct((SMALL_ROWS, 128), F32),
        scratch_shapes=[pltpu.VMEM((8, SMALL_ROWS, 128), F32), pltpu.SemaphoreType.DMA((7,)),
                        pltpu.SemaphoreType.DMA((7,))])(vec)


def _pack_p2(w_uq, w_ukv, w_br_mla, w_br_fox, w_out, dtype):
    parts = [w_uq.reshape(96, D_MODEL), w_ukv.reshape(64, D_MODEL), w_br_mla, w_br_fox, w_out]
    return jnp.concatenate([p.astype(dtype) for p in parts], axis=0)


def _unpack_p2(pk):
    return pk[0:96].reshape(256, 384), pk[96:160].reshape(128, 512), pk[160:416], pk[416:672], pk[672:928]


def _uq_arrange(w):
    w3 = w.reshape(256, HEADS, 96)
    nope = w3[:, :, :64].reshape(256, PAIRS, 128)
    pe = w3[:, :, 64:].reshape(256, PAIRS, 64)
    return jnp.concatenate([nope, pe, jnp.zeros((256, PAIRS, 64), w.dtype)], axis=2).reshape(256, PAIRS * 256)


def _uq_restore(g):
    g3 = g.reshape(256, PAIRS, 256)
    nope = g3[:, :, :128].reshape(256, HEADS, 64)
    pe = g3[:, :, 128:192].reshape(256, HEADS, 32)
    return jnp.concatenate([nope, pe], axis=2).reshape(256, HEADS * 96)


def _ukv_arrange(w):
    w3 = w.reshape(128, HEADS, 128)
    return jnp.concatenate([w3[:, :, :64].reshape(128, 1024), w3[:, :, 64:].reshape(128, 1024)], axis=1)


def _ukv_restore(g):
    kn = g[:, :1024].reshape(128, HEADS, 64)
    vv = g[:, 1024:].reshape(128, HEADS, 64)
    return jnp.concatenate([kn, vv], axis=2).reshape(128, HEADS * 128)


def _rope_tables(lp):
    r = np.arange(lp)
    pos = np.where(r < N_META, r, np.where(r >= PAD, r - PAD + N_META, 0)).astype(np.float32)
    half = MLA_ROPE // 2
    inv_freq = np.float32(ROPE_THETA) ** (-np.arange(half, dtype=np.float32) / np.float32(half))
    ang = (pos[:, None] * inv_freq[None, :]).astype(np.float32)
    cos, sin = np.cos(ang).astype(np.float32), np.sin(ang).astype(np.float32)
    one, zero = np.ones((lp, 64), np.float32), np.zeros((lp, 64), np.float32)
    return (jnp.asarray(np.concatenate([cos, cos, cos, cos, one], axis=1)),
            jnp.asarray(np.concatenate([-sin, sin, -sin, sin, zero], axis=1)))


def _pad_lanes(v, n=128):
    return jnp.pad(v, ((0, 0), (0, n - v.shape[1])))


def _in_cols(slabs, a, b):
    out = []
    for j in range(N_CHIPS):
        lo, hi = max(a, W_IN_SHARD * j), min(b, W_IN_SHARD * (j + 1))
        if lo < hi:
            out.append(slabs[j][:, lo - W_IN_SHARD * j:hi - W_IN_SHARD * j])
    return out


def _local_step(x2, tgt2, meta_f, w_small, w_attn, w_gate, w_uq_f, w_ukv_f, w_bm, w_bf, w_o, pre_norm_g,
                post_norm_g, mla_q_norm_g, mla_kv_norm_g, fox_forget_b, start_exchange=None):
    s_rows = x2.shape[0]
    lp = PAD + s_rows
    w_uq_a = _uq_arrange(w_uq_f)
    w_ukv_a = _ukv_arrange(w_ukv_f)

    ctab, stab = _rope_tables(lp)
    ii = jnp.arange(BLK)
    tri_lo = (ii[:, None] >= ii[None, :]).astype(BF16)
    tri_up = (ii[:, None] <= ii[None, :]).astype(BF16)
    fb128 = _pad_lanes(fox_forget_b)

    u = _rms_pre(x2, meta_f, pre_norm_g)
    small = _mm(u, w_small, mode="nn", out_dtype=F32, name="proj_small")
    attn = _mm(u, w_attn, mode="nn", out_dtype=BF16, name="proj_attn")
    gate = _mm(u, w_gate, mode="nn", out_dtype=BF16, name="proj_gate")
    qn, kvn, kr, ncum = _small_prep(small, mla_q_norm_g, mla_kv_norm_g, fb128, ctab, stab, tri_lo)
    qcat = _mm(qn, w_uq_a, mode="nn", out_dtype=BF16, name="mla_q", epilogue=_rope_pairs, row_ins=(ctab, stab))
    kv = _mm(kvn, w_ukv_a, mode="nn", out_dtype=BF16, name="mla_kv")
    nbrep = jnp.broadcast_to(ncum[:, :HEADS].T[:, :, None], (HEADS, lp, LANES))

    mla_cols = dict(qcol=0, kcol=lambda p: p, vcol=lambda p: PAIRS + p)
    fox_cols = dict(qcol=0, kcol=lambda p: PAIRS + p, vcol=lambda p: 2 * PAIRS + p)
    o_mla, lse_mla = _attn_fwd(qcat, kv, kv, kr=kr, scale=MLA_SCALE, name="mla_fwd", **mla_cols)
    o_fox, lse_fox = _attn_fwd(attn, attn, attn, nbrep=nbrep, scale=FOX_SCALE, name="fox_fwd", **fox_cols)

    a_mla, a_fox = _gate_fwd(o_mla, o_fox, gate)
    y_mla = _mm(a_mla, w_bm, mode="nn", out_dtype=BF16, name="br_mla")
    y_fox = _mm(a_fox, w_bf, mode="nn", out_dtype=BF16, name="br_fox")
    mg = _merge_fwd(gate, y_mla, y_fox)
    mixed = _mm(mg, w_o, mode="nn", out_dtype=F32, name="out_proj")
    dmixed, dy, loss_p, dg_post = _tail(x2, mixed, tgt2, post_norm_g)

    d_w_out = _mm(mg, dmixed, mode="tn", out_dtype=F32, name="d_w_out")
    dm = _mm(dmixed, w_o, mode="nt", out_dtype=BF16, name="d_merge")
    dy_mla, dy_fox, dgate_ab = _merge_bwd(dm, gate, y_mla, y_fox)
    d_w_bm = _mm(a_mla, dy_mla, mode="tn", out_dtype=F32, name="d_w_br_mla")
    d_w_bf = _mm(a_fox, dy_fox, mode="tn", out_dtype=F32, name="d_w_br_fox")
    da_mla = _mm(dy_mla, w_bm, mode="nt", out_dtype=BF16, name="d_a_mla")
    da_fox = _mm(dy_fox, w_bf, mode="nt", out_dtype=BF16, name="d_a_fox")
    do_mla, do_fox, dgate_z, dl_mla, dl_fox = _gate_bwd(da_mla, da_fox, o_mla, o_fox, gate)
    dl_mla, dl_fox = (d[:, :HEADS].T.reshape(PAIRS, 2, lp) for d in (dl_mla, dl_fox))

    dq_a, dkn, dvm, dkr = _attn_bwd(qcat, kv, kv, do_mla, dl_mla, lse_mla, kr=kr, rtabs=(ctab, stab),
                                    scale=MLA_SCALE, name="mla_bwd", **mla_cols)
    dfq, dfk, dfv, dcol, drow = _attn_bwd(attn, attn, attn, do_fox, dl_fox, lse_fox, nbrep=nbrep, scale=FOX_SCALE,
                                          name="fox_bwd", **fox_cols)

    d_w_uq_a = _mm(qn, dq_a, mode="tn", out_dtype=F32, name="d_w_uq")
    dqn = _mm(dq_a, w_uq_a, mode="nt", out_dtype=F32, name="d_qn")
    d_w_ukv_a = jnp.concatenate([_mm(kvn, dkn, mode="tn", out_dtype=F32, name="d_w_uk"),
                                 _mm(kvn, dvm, mode="tn", out_dtype=F32, name="d_w_uv")], axis=1)
    dkvn = _mm(dkn, w_ukv_a[:, :1024], mode="nt", out_dtype=F32, name="d_kvn_k")
    dkvn = _mm(dvm, w_ukv_a[:, 1024:], mode="nt", out_dtype=F32, name="d_kvn_v", acc=dkvn)
    dsmall, dg_q, dg_kv, dfb = _small_bwd(small, dqn, dkvn, dkr, dcol, drow, mla_q_norm_g, mla_kv_norm_g,
                                          fb128, ctab, stab, tri_up)

    dw_small = _mm(u, dsmall, mode="tn", out_dtype=F32, name="d_w_small")
    dw_fq = _mm(u, dfq, mode="tn", out_dtype=F32, name="d_w_fq")
    dw_fk = _mm(u, dfk, mode="tn", out_dtype=F32, name="d_w_fk")
    dw_fv = _mm(u, dfv, mode="tn", out_dtype=F32, name="d_w_fv")
    dw_z = _mm(u, dgate_z, mode="tn", out_dtype=F32, name="d_w_z")
    dw_g = _mm(u, dgate_ab, mode="tn", out_dtype=F32, name="d_w_g")
    d_w_in = (dw_small, dw_z, dw_fq, dw_fk, dw_fv, dw_g)
    d_w_uq = _uq_restore(d_w_uq_a)
    d_w_ukv = _ukv_restore(d_w_ukv_a)
    token = start_exchange(d_w_in, d_w_uq, d_w_ukv, d_w_bm, d_w_bf, d_w_out) if start_exchange else None
    du = _mm_sum_nt([(dsmall, w_small), (dfq, w_attn[:, 0:1024]), (dfk, w_attn[:, 1024:2048]),
                     (dfv, w_attn[:, 2048:3072]), (dgate_z, w_gate[:, 0:2048]), (dgate_ab, w_gate[:, 2048:4096])],
                    name="d_u", after=token)
    dx, dmeta, dg_pre = _pre_bwd(du, x2, meta_f, dy, pre_norm_g)
    return (loss_p, dx, dmeta, d_w_in, d_w_uq, d_w_ukv, d_w_bm, d_w_bf, d_w_out, dg_pre, dg_post, dg_q, dg_kv, dfb)


def _w_in_slabs(pieces):
    dw_small, dw_z, dw_fq, dw_fk, dw_fv, dw_g = pieces
    runs = [(dw_small[:, 0:416], C_CQ), (dw_z[:, 0:1024], C_ZMLA), (dw_fq, C_FQ), (dw_fk, C_FK), (dw_fv, C_FV),
            (dw_small[:, 512:528], C_FL), (dw_z[:, 1024:2048], C_ZFOX), (dw_g, C_GA)]
    slabs = []
    for j in range(N_CHIPS):
        lo, hi = W_IN_SHARD * j, W_IN_SHARD * (j + 1)
        cols = [a[:, max(lo, c0) - c0:min(hi, c0 + a.shape[1]) - c0] for a, c0 in runs
                if max(lo, c0) < min(hi, c0 + a.shape[1])]
        slabs.append(jnp.concatenate(cols, axis=1))
    return jnp.stack(slabs, axis=0)


def kernel(x, meta_tokens, pre_norm_g, w_in, fox_forget_b, mla_q_norm_g, mla_kv_norm_g, w_uq, w_ukv, w_br_mla, w_br_fox, w_out, post_norm_g, loss_target, m_meta_tokens, m_pre_norm_g, m_w_in, m_fox_forget_b, m_mla_q_norm_g, m_mla_kv_norm_g, m_w_uq, m_w_ukv, m_w_br_mla, m_w_br_fox, m_w_out, m_post_norm_g, v_meta_tokens, v_pre_norm_g, v_w_in, v_fox_forget_b, v_mla_q_norm_g, v_mla_kv_norm_g, v_w_uq, v_w_ukv, v_w_br_mla, v_w_br_fox, v_w_out, v_post_norm_g):
    me = 2 * lax.axis_index("x") + lax.axis_index("y")
    core = lax.axis_index("c")
    w_in_b = w_in.astype(BF16).reshape(D_MODEL, W_IN_SHARD)
    p2 = _pack_p2(w_uq[0], w_ukv[0], w_br_mla[0], w_br_fox[0], w_out[0], BF16)
    w_in_g, meta_g = _gather_weights([w_in_b], meta_tokens)
    p2_g = _gather_late(lax.optimization_barrier((p2, w_in_g))[0])
    slabs = [jnp.where(me == j, w_in_b, w_in_g[j]) for j in range(N_CHIPS)]
    pieces = [_unpack_p2(jnp.where(me == j, p2, p2_g[j])) for j in range(N_CHIPS)]
    w_uq_f = jnp.concatenate([p[0] for p in pieces], axis=1)
    w_ukv_f = jnp.concatenate([p[1] for p in pieces], axis=1)
    w_bm = jnp.concatenate([p[2] for p in pieces], axis=0)
    w_bf = jnp.concatenate([p[3] for p in pieces], axis=0)
    w_o = jnp.concatenate([p[4] for p in pieces], axis=0)
    meta_f = jnp.concatenate([jnp.where(me == j, meta_tokens, meta_g[j]) for j in range(N_CHIPS)], axis=1)
    kpe = _in_cols(slabs, C_KPE, C_ZMLA)
    w_small = jnp.concatenate(_in_cols(slabs, C_CQ, C_KPE) + kpe + kpe + [jnp.zeros((D_MODEL, 64), BF16)]
                              + _in_cols(slabs, C_FL, C_ZFOX) + [jnp.zeros((D_MODEL, 112), BF16)], axis=1)
    w_attn = jnp.concatenate(_in_cols(slabs, C_FQ, C_FL), axis=1)
    w_gate = jnp.concatenate(_in_cols(slabs, C_ZMLA, C_FQ) + _in_cols(slabs, C_ZFOX, C_END), axis=1)

    exchange = {}

    def start_exchange(d_w_in, d_w_uq, d_w_ukv, d_w_bm, d_w_bf, d_w_out):
        g2 = jnp.stack([_pack_p2(d_w_uq[:, 384 * j:384 * (j + 1)], d_w_ukv[:, 512 * j:512 * (j + 1)],
                                 d_w_bm[256 * j:256 * (j + 1)], d_w_bf[256 * j:256 * (j + 1)],
                                 d_w_out[256 * j:256 * (j + 1)], F32) for j in range(N_CHIPS)], axis=0)
        pieces = [p[None] for p in d_w_in]
        from_sib = _swap_halves(pieces + [g2])
        halves = [_add_cores(p, s, "add_cores_" + nm)[0]
                  for p, s, nm in zip(pieces, from_sib, ("small", "z", "fq", "fk", "fv", "g"))]
        parts = [_w_in_slabs(halves), _add_cores(g2, from_sib[-1], "add_cores_rest")]
        exchange.update(parts=parts, landed=_scatter_chips(parts))
        return parts[0][0, 0:16, 0:LANES]

    (loss_p, dx, dmeta, _, _, _, _, _, _, dg_pre, dg_post, dg_q, dg_kv,
     dfb) = _local_step(x[0], loss_target[0], meta_f, w_small, w_attn, w_gate, w_uq_f, w_ukv_f, w_bm, w_bf, w_o,
                        pre_norm_g, post_norm_g, mla_q_norm_g, mla_kv_norm_g, fox_forget_b, start_exchange)

    mine = [_add_chips(l, lax.dynamic_index_in_dim(p, me, 0, keepdims=False), nm)
            for l, p, nm in zip(exchange["landed"], exchange["parts"], ("add_chips_w_in", "add_chips_rest"))]
    theirs = _swap_reduced(mine)
    g_w_in, g_p2 = [jnp.concatenate([jnp.where(core == 0, a, b), jnp.where(core == 0, b, a)], axis=0)
                    for a, b in zip(mine, theirs)]
    g_w_uq, g_w_ukv, g_w_bm, g_w_bf, g_w_out = _unpack_p2(g_p2)
    g_w_in = g_w_in[None]

    vec = jnp.concatenate([dg_pre.reshape(8, 128), dg_post.reshape(8, 128), dg_q.reshape(2, 128), dg_kv,
                           dfb, _pad_lanes(loss_p), jnp.zeros((3, 128), F32), dmeta.reshape(128, 128)], axis=0)
    tot = _allreduce_small(vec)
    loss = tot[20, 0]
    g_meta = lax.dynamic_slice_in_dim(tot[24:].reshape(N_META, D_MODEL), 256 * me, 256, axis=1)

    def small_pack(pre, post, gq_, gkv_, fb_):
        return jnp.concatenate([pre.reshape(8, 128), post.reshape(8, 128), gq_.reshape(2, 128), gkv_,
                                _pad_lanes(fb_), jnp.zeros((4, 128), F32)], axis=0)

    def small_unpack(t):
        return (t[0:8].reshape(1, 1024), t[8:16].reshape(1, 1024), t[16:18].reshape(1, 256), t[18:19],
                t[19:20, 0:HEADS])

    g_small = jnp.concatenate([tot[0:20], jnp.zeros((4, 128), F32)], axis=0)
    sm = _adamw(small_pack(pre_norm_g, post_norm_g, mla_q_norm_g, mla_kv_norm_g, fox_forget_b), g_small,
                small_pack(m_pre_norm_g, m_post_norm_g, m_mla_q_norm_g, m_mla_kv_norm_g, m_fox_forget_b),
                small_pack(v_pre_norm_g, v_post_norm_g, v_mla_q_norm_g, v_mla_kv_norm_g, v_fox_forget_b),
                "adamw_small")
    g_pre, g_post, g_q, g_kv, g_fb = small_unpack(g_small)
    (d_pre, d_post, d_q, d_kv, d_fb), (nm_pre, nm_post, nm_q, nm_kv, nm_fb), (nv_pre, nv_post, nv_q, nv_kv, nv_fb) = (
        small_unpack(t) for t in sm)

    d_meta, nm_meta, nv_meta = _adamw(meta_tokens, g_meta, m_meta_tokens, v_meta_tokens, "adamw_meta")
    d_win, nm_win, nv_win = (t.T[None] for t in _adamw(w_in[0].T, g_w_in[0].T, m_w_in[0].T, v_w_in[0].T,
                                                       "adamw_w_in"))
    d_wuq, nm_wuq, nv_wuq = _adamw(w_uq[0], g_w_uq, m_w_uq[0], v_w_uq[0], "adamw_w_uq")
    d_wukv, nm_wukv, nv_wukv = _adamw(w_ukv[0], g_w_ukv, m_w_ukv[0], v_w_ukv[0], "adamw_w_ukv")
    d_wbm, nm_wbm, nv_wbm = _adamw(w_br_mla[0], g_w_bm, m_w_br_mla[0], v_w_br_mla[0], "adamw_w_br_mla")
    d_wbf, nm_wbf, nv_wbf = _adamw(w_br_fox[0], g_w_bf, m_w_br_fox[0], v_w_br_fox[0], "adamw_w_br_fox")
    d_wo, nm_wo, nv_wo = _adamw(w_out[0], g_w_out, m_w_out[0], v_w_out[0], "adamw_w_out")

    def group(meta_, pre, win, fb_, q_, kv_, wuq, wukv, wbm, wbf, wo, post):
        return (meta_, pre, win, fb_, q_, kv_, wuq[None], wukv[None], wbm[None], wbf[None], wo[None], post)

    grads = group(g_meta, g_pre, g_w_in, g_fb, g_q, g_kv, g_w_uq, g_w_ukv, g_w_bm, g_w_bf, g_w_out, g_post)
    deltas = group(d_meta, d_pre, d_win, d_fb, d_q, d_kv, d_wuq, d_wukv, d_wbm, d_wbf, d_wo, d_post)
    new_m = group(nm_meta, nm_pre, nm_win, nm_fb, nm_q, nm_kv, nm_wuq, nm_wukv, nm_wbm, nm_wbf, nm_wo, nm_post)
    new_v = group(nv_meta, nv_pre, nv_win, nv_fb, nv_q, nv_kv, nv_wuq, nv_wukv, nv_wbm, nv_wbf, nv_wo, nv_post)
    return (loss, dx[None], *grads, *deltas, *new_m, *new_v)
```

```python
import math

import jax
import jax.numpy as jnp
import numpy as np
from jax import lax
from jax.experimental import pallas as pl
from jax.experimental.pallas import tpu as pltpu
from jax.experimental.pallas import tpu_sc as plsc

F32 = jnp.float32
BF16 = jnp.bfloat16

D_MODEL = 1024
N_META = 16
RMS_EPS = 1e-6
HEADS = 16
PAIRS = HEADS // 2
HEAD_DIM = 64
LANES = 128
MLA_ROPE = 32
MLA_SCALE = 1.0 / math.sqrt(64 + 32)
FOX_SCALE = 1.0 / math.sqrt(64)
ROPE_THETA = 10000.0

PAD = 256
BLK = 256
QB = 512
UNROLL = 4
NEG = -1e30

C_CQ, C_CKV, C_KPE, C_ZMLA, C_FQ, C_FK, C_FV, C_FL, C_ZFOX, C_GA, C_GB, C_END = (
    0, 256, 384, 416, 1440, 2464, 3488, 4512, 4528, 5552, 6576, 7600)
SMALL_W = 640
W_IN_SHARD = 1900

P2_ROWS = 928
N_CHIPS = 4

ADAM_LR = 0.001
ADAM_B1 = 0.9
ADAM_B2 = 0.999
ADAM_EPS = 1e-08
ADAM_WD = 0.01
ADAM_STEP = 10

VMEM_BIG = 56 * 1024 * 1024
MM_VMEM_BUDGET = 44 * 1024 * 1024
MESH = pl.DeviceIdType.MESH


def _cp(dims, vmem=None):
    return pltpu.CompilerParams(dimension_semantics=dims, vmem_limit_bytes=vmem)


def _dot(a, b, ca, cb):
    return lax.dot_general(a, b, (((ca,), (cb,)), ((), ())), preferred_element_type=F32)


def _sigmoid(x):
    return 1.0 / (1.0 + jnp.exp(-x))


def _tile(n, cands):
    for c in cands:
        if n % c == 0:
            return c
    return n


def _mm(a, b, *, mode, out_dtype, name, acc=None, epilogue=None, row_ins=(), after=None):
    if mode == "nn":
        (M, K), N = a.shape, b.shape[1]
    elif mode == "nt":
        (M, K), N = a.shape, b.shape[0]
    else:
        (K, M), N = a.shape, b.shape[1]
    tm = _tile(M, (1088, 1024)) if M > 1024 else M
    tn = _tile(N, (1024,)) if N > 1024 else N
    nk = 1
    while True:
        tk = K // nk
        need = 2 * tk * (tm * a.dtype.itemsize + tn * b.dtype.itemsize) + tm * tn * (
            2 * jnp.dtype(out_dtype).itemsize + (8 if acc is not None else 0) + (4 if nk > 1 else 0))
        if need <= MM_VMEM_BUDGET or (tk // 2) % (16 if mode == "tn" else LANES) or tk <= 512:
            break
        nk *= 2
    ca, cb = {"nn": (1, 0), "nt": (1, 1), "tn": (0, 0)}[mode]
    a_spec = (pl.BlockSpec((tk, tm), lambda j, i, k: (k, i)) if mode == "tn"
              else pl.BlockSpec((tm, tk), lambda j, i, k: (i, k)))
    b_spec = (pl.BlockSpec((tn, tk), lambda j, i, k: (j, k)) if mode == "nt"
              else pl.BlockSpec((tk, tn), lambda j, i, k: (k, j)))
    o_spec = pl.BlockSpec((tm, tn), lambda j, i, k: (i, j))
    has_acc = acc is not None

    nrow = len(row_ins)

    def body(*refs):
        a_ref, b_ref = refs[0], refs[1]
        acc_ref = refs[2] if has_acc else None
        rows = refs[2 + has_acc:2 + has_acc + nrow]
        o_ref = refs[2 + has_acc + nrow + (after is not None)]

        def store(tile):
            if epilogue is not None:
                tile = epilogue(tile, *[r[...] for r in rows])
            o_ref[...] = tile.astype(out_dtype)

        part = _dot(a_ref[...].astype(BF16), b_ref[...].astype(BF16), ca, cb)
        if nk == 1:
            store(part + acc_ref[...] if has_acc else part)
        else:
            sc = refs[-1]
            k = pl.program_id(2)

            @pl.when(k == 0)
            def _():
                sc[...] = part + acc_ref[...] if has_acc else part

            @pl.when(k > 0)
            def _():
                sc[...] += part

            @pl.when(k == nk - 1)
            def _():
                store(sc[...])

    ins = [a, b] + ([acc] if has_acc else []) + list(row_ins)
    in_specs = ([a_spec, b_spec] + ([o_spec] if has_acc else [])
                + [pl.BlockSpec((tm, r.shape[1]), lambda j, i, k: (i, 0)) for r in row_ins])
    if after is not None:
        ins.append(after)
        in_specs.append(pl.BlockSpec(after.shape, lambda j, i, k: (0,) * after.ndim))
    return pl.pallas_call(
        body, name=name, grid=(N // tn, M // tm, nk), in_specs=in_specs, out_specs=o_spec,
        out_shape=jax.ShapeDtypeStruct((M, N), out_dtype),
        scratch_shapes=[pltpu.VMEM((tm, tn), F32)] if nk > 1 else [],
        compiler_params=_cp(("parallel", "parallel", "arbitrary"), VMEM_BIG))(*ins)


def _mm_sum_nt(pairs, *, name, after=None):
    n = len(pairs)
    M, N = pairs[0][0].shape[0], pairs[0][1].shape[0]
    tm = _tile(M, (272,))

    def body(*refs):
        o_ref = refs[2 * n + (after is not None)]
        tot = _dot(refs[0][...].astype(BF16), refs[n][...].astype(BF16), 1, 1)
        for i in range(1, n):
            tot = tot + _dot(refs[i][...].astype(BF16), refs[n + i][...].astype(BF16), 1, 1)
        o_ref[...] = tot

    ins = [a for a, _ in pairs] + [b for _, b in pairs]
    in_specs = ([pl.BlockSpec((tm, a.shape[1]), lambda i: (i, 0)) for a, _ in pairs]
                + [pl.BlockSpec(b.shape, lambda i: (0, 0)) for _, b in pairs])
    if after is not None:
        ins.append(after)
        in_specs.append(pl.BlockSpec(after.shape, lambda i: (0,) * after.ndim))
    return pl.pallas_call(
        body, name=name, grid=(M // tm,), in_specs=in_specs, out_specs=pl.BlockSpec((tm, N), lambda i: (i, 0)),
        out_shape=jax.ShapeDtypeStruct((M, N), F32), compiler_params=_cp(("parallel",), VMEM_BIG))(*ins)


def _row(w):
    return pl.BlockSpec((BLK, w), lambda i: (i, 0))


def _rowc(w, c):
    return pl.BlockSpec((BLK, w), lambda i: (i, c))


def _full(shape):
    return pl.BlockSpec(shape, lambda i: tuple(0 for _ in shape))


def _rope(x, c, s):
    lane = lax.broadcasted_iota(jnp.int32, x.shape, 1)
    is_x1 = ((lane >> 4) & 1) == 0
    partner = jnp.where(is_x1, pltpu.roll(x, LANES - 16, 1), pltpu.roll(x, 16, 1))
    return x * c + partner * s


def _row_valid(i):
    rows = i * BLK + lax.broadcasted_iota(jnp.int32, (BLK, 1), 0)
    return (rows < N_META) | (rows >= PAD)


def _shift_rows(w):
    return pl.BlockSpec((BLK, w), lambda i: (jnp.maximum(i - 1, 0), 0))


def _h_block(i, x_ref, meta_ref):
    head = jnp.concatenate([meta_ref[...], jnp.zeros((BLK - N_META, D_MODEL), F32)], axis=0)
    return jnp.where(i == 0, head, x_ref[...])


def _rms_pre(x2, meta, g):
    lp = PAD + x2.shape[0]

    def body(x_ref, meta_ref, g_ref, u_ref):
        hv = _h_block(pl.program_id(0), x_ref, meta_ref)
        r = lax.rsqrt(jnp.mean(hv * hv, axis=-1, keepdims=True) + RMS_EPS)
        u_ref[...] = (hv * r * g_ref[...]).astype(BF16)

    return pl.pallas_call(
        body, name="rms_pre", grid=(lp // BLK,),
        in_specs=[_shift_rows(D_MODEL), _full((N_META, D_MODEL)), _full((1, D_MODEL))], out_specs=_row(D_MODEL),
        out_shape=jax.ShapeDtypeStruct((lp, D_MODEL), BF16),
        compiler_params=_cp(("parallel",)))(x2, meta, g)


def _split3(x):
    hi = x.astype(BF16)
    r1 = x - hi.astype(F32)
    mid = r1.astype(BF16)
    lo = (r1 - mid.astype(F32)).astype(BF16)
    return hi, mid, lo


def _small_prep(small, gq, gkv, fb, ctab, stab, tri):
    lp = small.shape[0]

    def body(sm_ref, gq_ref, gkv_ref, fb_ref, c_ref, s_ref, tri_ref, qn_ref, kvn_ref, kr_ref, ncum_ref, carry):
        i = pl.program_id(0)

        @pl.when(i == 0)
        def _():
            carry[...] = jnp.zeros_like(carry)

        cq = sm_ref[:, 0:256]
        r = lax.rsqrt(jnp.mean(cq * cq, axis=-1, keepdims=True) + RMS_EPS)
        qn_ref[...] = (cq * r * gq_ref[...]).astype(BF16)
        ckv = sm_ref[:, 256:384]
        r = lax.rsqrt(jnp.mean(ckv * ckv, axis=-1, keepdims=True) + RMS_EPS)
        kvn_ref[...] = (ckv * r * gkv_ref[...]).astype(BF16)
        kr_ref[...] = _rope(sm_ref[:, 384:512], c_ref[...], s_ref[...]).astype(BF16)
        fl = sm_ref[:, 512:640] + fb_ref[...]
        lf = jnp.minimum(fl, 0.0) - jnp.log(1.0 + jnp.exp(-jnp.abs(fl)))
        lf = jnp.where(_row_valid(i), lf, 0.0)
        hi, mid, lo = _split3(lf)
        t = tri_ref[...]
        cum = (_dot(t, hi, 1, 0) + _dot(t, mid, 1, 0)) + _dot(t, lo, 1, 0) + carry[...]
        ncum_ref[...] = -cum
        carry[...] = -ncum_ref[BLK - 1:BLK, :]

    return pl.pallas_call(
        body, name="small_prep", grid=(lp // BLK,),
        in_specs=[_row(SMALL_W), _full((1, 256)), _full((1, 128)), _full((1, 128)), _row(128), _row(128),
                  _full((BLK, BLK))],
        out_specs=[_row(256), _row(128), _row(128), _row(128)],
        out_shape=[jax.ShapeDtypeStruct((lp, 256), BF16), jax.ShapeDtypeStruct((lp, 128), BF16),
                   jax.ShapeDtypeStruct((lp, 128), BF16), jax.ShapeDtypeStruct((lp, 128), F32)],
        scratch_shapes=[pltpu.VMEM((1, 128), F32)],
        compiler_params=_cp(("arbitrary",)))(small, gq, gkv, fb, ctab, stab, tri)


def _rope_pairs(tile, c, s):
    out = []
    for lo in range(0, tile.shape[1], 256):
        out += [tile[:, lo:lo + 128], _rope(tile[:, lo + 128:lo + 256], c, s)]
    return jnp.concatenate(out, axis=1)


def _gate_fwd(o_mla, o_fox, gate):
    lp = o_mla.shape[0]

    def body(om_ref, of_ref, zm_ref, zf_ref, am_ref, af_ref):
        zm = zm_ref[...].astype(F32)
        am_ref[...] = (om_ref[...] * (zm * _sigmoid(zm))).astype(BF16)
        zf = zf_ref[...].astype(F32)
        af_ref[...] = (of_ref[...] * (zf * _sigmoid(zf))).astype(BF16)

    return pl.pallas_call(
        body, name="gate_fwd", grid=(lp // BLK,),
        in_specs=[_row(D_MODEL), _row(D_MODEL), _rowc(D_MODEL, 0), _rowc(D_MODEL, 1)],
        out_specs=[_row(D_MODEL), _row(D_MODEL)],
        out_shape=[jax.ShapeDtypeStruct((lp, D_MODEL), BF16)] * 2,
        compiler_params=_cp(("parallel",)))(o_mla, o_fox, gate, gate)


def _merge_fwd(gate, y_mla, y_fox):
    lp = y_mla.shape[0]

    def body(ga_ref, gb_ref, ym_ref, yf_ref, m_ref):
        sa = _sigmoid(ga_ref[...].astype(F32))
        sb = _sigmoid(gb_ref[...].astype(F32))
        m_ref[...] = (sa * ym_ref[...] + sb * yf_ref[...]).astype(BF16)

    return pl.pallas_call(
        body, name="merge_fwd", grid=(lp // BLK,),
        in_specs=[_rowc(D_MODEL, 2), _rowc(D_MODEL, 3), _row(D_MODEL), _row(D_MODEL)],
        out_specs=_row(D_MODEL), out_shape=jax.ShapeDtypeStruct((lp, D_MODEL), BF16),
        compiler_params=_cp(("parallel",)))(gate, gate, y_mla, y_fox)


def _tail(x2, mixed, tgt, gpost):
    lp = mixed.shape[0]
    shift = _shift_rows(D_MODEL)

    def body(h_ref, mx_ref, t_ref, g_ref, dmx_ref, dy_ref, loss_ref, dg_ref):
        i = pl.program_id(0)

        @pl.when(i == 0)
        def _():
            loss_ref[...] = jnp.zeros_like(loss_ref)
            dg_ref[...] = jnp.zeros_like(dg_ref)
            dmx_ref[...] = jnp.zeros_like(dmx_ref)
            dy_ref[...] = jnp.zeros_like(dy_ref)

        @pl.when(i > 0)
        def _():
            mx = mx_ref[...]
            g = g_ref[...]
            r = lax.rsqrt(jnp.mean(mx * mx, axis=-1, keepdims=True) + RMS_EPS)
            nrm = mx * r
            e = (h_ref[...] + nrm * g) - t_ref[...]
            loss_ref[...] += jnp.sum(0.5 * jnp.sum(e * e, axis=-1, keepdims=True) * (1.0 / D_MODEL),
                                     axis=0, keepdims=True)
            dy = e * (1.0 / D_MODEL)
            dy_ref[...] = dy
            dg_ref[...] += jnp.sum(dy * nrm, axis=0, keepdims=True)
            w = dy * g
            dot = jnp.mean(w * mx, axis=-1, keepdims=True)
            dmx_ref[...] = (r * w - mx * (r * r * r * dot)).astype(BF16)

    return pl.pallas_call(
        body, name="tail", grid=(lp // BLK,),
        in_specs=[shift, _row(D_MODEL), shift, _full((1, D_MODEL))],
        out_specs=[_row(D_MODEL), _row(D_MODEL), _full((1, 1)), _full((1, D_MODEL))],
        out_shape=[jax.ShapeDtypeStruct((lp, D_MODEL), BF16), jax.ShapeDtypeStruct((lp, D_MODEL), F32),
                   jax.ShapeDtypeStruct((1, 1), F32), jax.ShapeDtypeStruct((1, D_MODEL), F32)],
        compiler_params=_cp(("arbitrary",)))(x2, mixed, tgt, gpost)


def _merge_bwd(dm, gate, y_mla, y_fox):
    lp = dm.shape[0]

    def body(dm_ref, ga_ref, gb_ref, ym_ref, yf_ref, dym_ref, dyf_ref, dg_ref):
        dm_v = dm_ref[...].astype(F32)
        sa = _sigmoid(ga_ref[...].astype(F32))
        sb = _sigmoid(gb_ref[...].astype(F32))
        dym_ref[...] = (dm_v * sa).astype(BF16)
        dyf_ref[...] = (dm_v * sb).astype(BF16)
        dg_ref[:, 0:D_MODEL] = (dm_v * ym_ref[...] * (sa * (1.0 - sa))).astype(BF16)
        dg_ref[:, D_MODEL:2 * D_MODEL] = (dm_v * yf_ref[...] * (sb * (1.0 - sb))).astype(BF16)

    return pl.pallas_call(
        body, name="merge_bwd", grid=(lp // BLK,),
        in_specs=[_row(D_MODEL), _rowc(D_MODEL, 2), _rowc(D_MODEL, 3), _row(D_MODEL), _row(D_MODEL)],
        out_specs=[_row(D_MODEL), _row(D_MODEL), _row(2 * D_MODEL)],
        out_shape=[jax.ShapeDtypeStruct((lp, D_MODEL), BF16), jax.ShapeDtypeStruct((lp, D_MODEL), BF16),
                   jax.ShapeDtypeStruct((lp, 2 * D_MODEL), BF16)],
        compiler_params=_cp(("parallel",)))(dm, gate, gate, y_mla, y_fox)


def _gate_bwd(da_mla, da_fox, o_mla, o_fox, gate):
    lp = da_mla.shape[0]

    def one(da, o, z, head_of_col):
        sg = _sigmoid(z)
        do = (da * (z * sg)).astype(BF16)
        dz = da * o * (sg * (1.0 + z * (1.0 - sg)))
        delta = sum(_dot(part, head_of_col, 1, 0) for part in _split3(do.astype(F32) * o))
        return do, dz.astype(BF16), delta

    def body(dam_ref, daf_ref, om_ref, of_ref, zm_ref, zf_ref, dom_ref, dof_ref, dz_ref, dlm_ref, dlf_ref):
        f32 = lambda r: r[...].astype(F32)
        head_of_col = (lax.broadcasted_iota(jnp.int32, (D_MODEL, LANES), 0) // HEAD_DIM
                       == lax.broadcasted_iota(jnp.int32, (D_MODEL, LANES), 1)).astype(BF16)
        dom_ref[...], dz_ref[:, 0:D_MODEL], dlm_ref[...] = one(f32(dam_ref), f32(om_ref), f32(zm_ref), head_of_col)
        dof_ref[...], dz_ref[:, D_MODEL:2 * D_MODEL], dlf_ref[...] = one(f32(daf_ref), f32(of_ref), f32(zf_ref),
                                                                        head_of_col)

    return pl.pallas_call(
        body, name="gate_bwd", grid=(lp // BLK,),
        in_specs=[_row(D_MODEL)] * 4 + [_rowc(D_MODEL, 0), _rowc(D_MODEL, 1)],
        out_specs=[_row(D_MODEL), _row(D_MODEL), _row(2 * D_MODEL), _row(LANES), _row(LANES)],
        out_shape=[jax.ShapeDtypeStruct((lp, D_MODEL), BF16), jax.ShapeDtypeStruct((lp, D_MODEL), BF16),
                   jax.ShapeDtypeStruct((lp, 2 * D_MODEL), BF16), jax.ShapeDtypeStruct((lp, LANES), F32),
                   jax.ShapeDtypeStruct((lp, LANES), F32)],
        compiler_params=_cp(("parallel",)))(da_mla, da_fox, o_mla, o_fox, gate, gate)


def _small_bwd(small, dqn, dkvn, dkr, dcol_t, drow_t, gq, gkv, fb, ctab, stab, triu):
    lp = small.shape[0]
    nb = lp // BLK

    def rrow(w):
        return pl.BlockSpec((BLK, w), lambda i: (nb - 1 - i, 0))

    def body(sm_ref, dqn_ref, dkvn_ref, dkr_ref, dcol_ref, drow_ref, gq_ref, gkv_ref, fb_ref, c_ref, s_ref, tri_ref,
             ds_ref, dgq_ref, dgkv_ref, dfb_ref, carry):
        i = pl.program_id(0)

        @pl.when(i == 0)
        def _():
            carry[...] = jnp.zeros_like(carry)
            dgq_ref[...] = jnp.zeros_like(dgq_ref)
            dgkv_ref[...] = jnp.zeros_like(dgkv_ref)
            dfb_ref[...] = jnp.zeros_like(dfb_ref)

        def norm_bwd(x, dn, g, dg_ref):
            r = lax.rsqrt(jnp.mean(x * x, axis=-1, keepdims=True) + RMS_EPS)
            dg_ref[...] += jnp.sum(dn * (x * r), axis=0, keepdims=True)
            w = dn * g
            dot = jnp.mean(w * x, axis=-1, keepdims=True)
            return r * w - x * (r * r * r * dot)

        ds_ref[:, 0:256] = norm_bwd(sm_ref[:, 0:256], dqn_ref[...], gq_ref[...], dgq_ref).astype(BF16)
        ds_ref[:, 256:384] = norm_bwd(sm_ref[:, 256:384], dkvn_ref[...], gkv_ref[...], dgkv_ref).astype(BF16)

        dk = dkr_ref[0]
        for p in range(1, PAIRS):
            dk = dk + dkr_ref[p]
        dk = _rope(dk, c_ref[...], -s_ref[...])
        lane = lax.broadcasted_iota(jnp.int32, dk.shape, 1)
        dk = jnp.where(lane < MLA_ROPE, dk + pltpu.roll(dk, LANES - MLA_ROPE, 1), 0.0)
        ds_ref[:, 384:512] = dk.astype(BF16)

        dcol = dcol_ref[0]
        for p in range(1, PAIRS):
            dcol = dcol + pltpu.roll(dcol_ref[p], 2 * p, 1)
        rows16 = jnp.concatenate([drow_ref[p, h:h + 1, :] for p in range(PAIRS) for h in range(2)], axis=0)
        eye = (lax.broadcasted_iota(jnp.int32, (HEADS, LANES), 0)
               == lax.broadcasted_iota(jnp.int32, (HEADS, LANES), 1)).astype(BF16)
        drow = sum(_dot(part, eye, 0, 0) for part in _split3(rows16))
        dcr = dcol - drow
        hi, mid, lo = _split3(dcr)
        t = tri_ref[...]
        suf = (_dot(t, hi, 1, 0) + _dot(t, mid, 1, 0)) + _dot(t, lo, 1, 0) + carry[...]
        fl = sm_ref[:, 512:640] + fb_ref[...]
        dfl = jnp.where(_row_valid(nb - 1 - i), -suf * _sigmoid(-fl), 0.0)
        ds_ref[:, 512:640] = dfl.astype(BF16)
        dfb_ref[...] += jnp.sum(dfl, axis=0, keepdims=True)
        carry[...] += jnp.sum(dcr, axis=0, keepdims=True)

    return pl.pallas_call(
        body, name="small_bwd", grid=(nb,),
        in_specs=[rrow(SMALL_W), rrow(256), rrow(128),
                  pl.BlockSpec((PAIRS, BLK, 128), lambda i: (0, nb - 1 - i, 0)),
                  pl.BlockSpec((PAIRS, BLK, 128), lambda i: (0, nb - 1 - i, 0)),
                  pl.BlockSpec((PAIRS, 2, BLK), lambda i: (0, 0, nb - 1 - i)),
                  _full((1, 256)), _full((1, 128)), _full((1, 128)), rrow(128), rrow(128), _full((BLK, BLK))],
        out_specs=[rrow(SMALL_W), _full((1, 256)), _full((1, 128)), _full((1, 128))],
        out_shape=[jax.ShapeDtypeStruct((lp, SMALL_W), BF16), jax.ShapeDtypeStruct((1, 256), F32),
                   jax.ShapeDtypeStruct((1, 128), F32), jax.ShapeDtypeStruct((1, 128), F32)],
        scratch_shapes=[pltpu.VMEM((1, 128), F32)],
        compiler_params=_cp(("arbitrary",)))(small, dqn, dkvn, dkr, dcol_t, drow_t, gq, gkv, fb, ctab, stab, triu)


def _pre_bwd(du, x2, meta, dy, gpre):
    s_rows = x2.shape[0]
    lp = PAD + s_rows
    shift = _shift_rows(D_MODEL)

    def body(du_ref, x_ref, meta_ref, dy_ref, g_ref, dx_ref, dmeta_ref, dg_ref):
        i = pl.program_id(0)

        @pl.when(i == 0)
        def _():
            dg_ref[...] = jnp.zeros_like(dg_ref)

        hv = _h_block(i, x_ref, meta_ref)
        duv = du_ref[...]
        r = lax.rsqrt(jnp.mean(hv * hv, axis=-1, keepdims=True) + RMS_EPS)
        dg_ref[...] += jnp.sum(duv * (hv * r), axis=0, keepdims=True)
        w = duv * g_ref[...]
        dot = jnp.mean(w * hv, axis=-1, keepdims=True)
        dh = dy_ref[...] + (r * w - hv * (r * r * r * dot))
        dx_ref[...] = dh

        @pl.when(i == 0)
        def _():
            dmeta_ref[...] = dh[0:N_META, :]

    return pl.pallas_call(
        body, name="pre_bwd", grid=(lp // BLK,),
        in_specs=[_row(D_MODEL), shift, _full((N_META, D_MODEL)), _row(D_MODEL), _full((1, D_MODEL))],
        out_specs=[shift, _full((N_META, D_MODEL)), _full((1, D_MODEL))],
        out_shape=[jax.ShapeDtypeStruct((s_rows, D_MODEL), F32), jax.ShapeDtypeStruct((N_META, D_MODEL), F32),
                   jax.ShapeDtypeStruct((1, D_MODEL), F32)],
        compiler_params=_cp(("arbitrary",)))(du, x2, meta, dy, gpre)


def _pair_masks(rope):
    lane = lax.broadcasted_iota(jnp.int32, (1, LANES), 1)
    mas = [lane < HEAD_DIM, lane >= HEAD_DIM]
    if not rope:
        return mas, mas
    wide = lax.broadcasted_iota(jnp.int32, (1, 2 * LANES), 1)
    rope_lo = LANES + MLA_ROPE
    return mas, [(wide < HEAD_DIM) | ((wide >= LANES) & (wide < rope_lo)),
                 ((wide >= HEAD_DIM) & (wide < LANES)) | ((wide >= rope_lo) & (wide < rope_lo + MLA_ROPE))]


def _mask2(x, masks):
    return [jnp.where(m, x, jnp.zeros_like(x)) for m in masks]


def _attn_fwd(q, k, v, *, kr=None, nbrep=None, scale, qcol, kcol, vcol, name):
    lp = q.shape[0]
    nq = 1 + (lp - PAD) // QB
    rope = kr is not None
    bias = nbrep is not None
    qw = 256 if rope else 128

    def body(*refs):
        it = iter(refs)
        q_ref, k_ref, v_ref = next(it), next(it), next(it)
        kr_ref = next(it) if rope else None
        nb_ref = next(it) if bias else None
        o_ref, lse_ref = next(it), next(it)
        i = pl.program_id(1)
        r0 = pl.multiple_of(jnp.where(i == 0, 0, PAD + QB * (i - 1)), BLK)
        b0 = r0 // BLK
        mas, hmask = _pair_masks(rope)
        qh = _mask2(q_ref[pl.ds(r0, QB), :], hmask)
        if bias:
            qh = [x * scale for x in qh]

        def causal(kc, n):
            key = kc * BLK + lax.broadcasted_iota(jnp.int32, (n, QB), 0)
            return (key <= r0 + lax.broadcasted_iota(jnp.int32, (n, QB), 1)) & ((kc > 0) | (n == N_META))

        def update(kcs, carry, masks, n=BLK, q_lo=0, wq=QB):
            stats, acc = carry[:4], carry[4]
            qs = [x[q_lo:q_lo + wq] for x in qh]
            k0s = [pl.multiple_of(kc * BLK, BLK) for kc in kcs]
            kks = [k_ref[pl.ds(k0, n), :] for k0 in k0s]
            if rope:
                kks = [jnp.concatenate([kk, kr_ref[pl.ds(k0, n), :]], axis=1) for kk, k0 in zip(kks, k0s)]
            new_stats, alphas, ps = [], [], [[] for _ in kcs]
            for h in range(2):
                m_prev, l_prev = stats[2 * h], stats[2 * h + 1]
                ss = []
                for kk, k0, mask in zip(kks, k0s, masks):
                    s = _dot(kk, qs[h], 1, 1)
                    if rope:
                        s = s * scale
                    if bias:
                        nbc = nb_ref[h, pl.ds(k0, n), :]
                        s = s + jnp.concatenate([nbc] * (wq // LANES), axis=1)
                    if mask is not None:
                        s = jnp.where(mask, s, NEG)
                    ss.append(s)
                m_new = m_prev
                for s in ss:
                    m_new = jnp.maximum(m_new, jnp.max(s, axis=0, keepdims=True))
                alpha = jnp.exp(m_prev - m_new)
                l_new = alpha * l_prev
                for j, s in enumerate(ss):
                    p = jnp.exp(s - m_new)
                    l_new = l_new + jnp.sum(p, axis=0, keepdims=True)
                    ps[j].append(p.astype(BF16))
                new_stats += [m_new, l_new]
                alphas.append(alpha)
            vcat = jnp.concatenate([x for k0 in k0s for x in _mask2(v_ref[pl.ds(k0, n), :], mas)], axis=0)
            pv = _dot(vcat, jnp.concatenate([p for pj in ps for p in pj], axis=0), 0, 0)
            a_full = jnp.concatenate([jnp.broadcast_to(a, (HEAD_DIM, wq)) for a in alphas], axis=0)
            return (*new_stats, a_full * acc + pv)

        neg = jnp.full((1, QB), NEG, F32)
        zero = jnp.zeros((1, QB), F32)
        c = update([0], (neg, zero, neg, zero, jnp.zeros((LANES, QB), F32)), [causal(0, N_META)], N_META)
        n_mid = jnp.maximum(b0 - 1, 0)
        c = lax.fori_loop(0, n_mid // 4, lambda t, cr: update([4 * t + u for u in (1, 2, 3, 4)], cr, [None] * 4), c)
        c = lax.fori_loop(0, (n_mid % 4) // 2, lambda t, cr: update([n_mid - 1, n_mid], cr, [None, None]), c)
        tri = ((lax.broadcasted_iota(jnp.int32, (BLK, BLK), 0) <= lax.broadcasted_iota(jnp.int32, (BLK, BLK), 1))
               & (b0 > 0))
        lo = update([b0], tuple(a[:, 0:BLK] for a in c), [tri], q_lo=0, wq=BLK)
        hi = update([b0, b0 + 1], tuple(a[:, BLK:QB] for a in c), [None, tri], q_lo=BLK, wq=QB - BLK)
        c = tuple(jnp.concatenate([a, b], axis=1) for a, b in zip(lo, hi))
        inv =jnp.concatenate([jnp.broadcast_to(1.0 / c[1], (HEAD_DIM, QB)),
                               jnp.broadcast_to(1.0 / c[3], (HEAD_DIM, QB))], axis=0)
        o_t = (c[4] * inv).T.astype(BF16)
        lses = [c[0] + jnp.log(c[1]), c[2] + jnp.log(c[3])]
        o_ref[pl.ds(r0, BLK), :] = o_t[0:BLK]
        for h in range(2):
            lse_ref[0, h:h + 1, pl.ds(r0, BLK)] = lses[h][:, 0:BLK]

        @pl.when(i > 0)
        def _():
            r1 = pl.multiple_of(r0 + BLK, BLK)
            o_ref[pl.ds(r1, QB - BLK), :] = o_t[BLK:QB]
            for h in range(2):
                lse_ref[0, h:h + 1, pl.ds(r1, QB - BLK)] = lses[h][:, BLK:QB]

    in_specs = [pl.BlockSpec((lp, qw), lambda p, i: (0, qcol + p)),
                pl.BlockSpec((lp, 128), lambda p, i: (0, kcol(p))),
                pl.BlockSpec((lp, 128), lambda p, i: (0, vcol(p)))]
    ins = [q, k, v]
    if rope:
        in_specs.append(pl.BlockSpec((lp, 128), lambda p, i: (0, 0)))
        ins.append(kr)
    if bias:
        in_specs.append(pl.BlockSpec((2, lp, 128), lambda p, i: (p, 0, 0)))
        ins.append(nbrep)
    return pl.pallas_call(
        body, name=name, grid=(PAIRS, nq), in_specs=in_specs,
        out_specs=[pl.BlockSpec((lp, 128), lambda p, i: (0, p)),
                   pl.BlockSpec((1, 2, lp), lambda p, i: (p, 0, 0))],
        out_shape=[jax.ShapeDtypeStruct((lp, D_MODEL), BF16), jax.ShapeDtypeStruct((PAIRS, 2, lp), F32)],
        compiler_params=_cp(("parallel", "arbitrary"), VMEM_BIG))(*ins)


def _attn_bwd(q, k, v, do, delta, lse, *, kr=None, rtabs=None, nbrep=None, scale, qcol, kcol, vcol, name):
    lp = q.shape[0]
    nb = lp // BLK
    rope = kr is not None
    bias = nbrep is not None
    qw = 256 if rope else 128

    def body(*refs):
        it = iter(refs)
        q_ref, k_ref, v_ref = next(it), next(it), next(it)
        kr_ref = next(it) if rope else None
        nb_ref = next(it) if bias else None
        do_ref, dl_ref, lse_ref = next(it), next(it), next(it)
        ct_ref, st_ref = (next(it), next(it)) if rope else (None, None)
        dq_out, dk_ref, dv_ref = next(it), next(it), next(it)
        x_ref = next(it)
        drow_ref = next(it) if bias else None
        dq_ref = next(it)
        kb = pl.program_id(1)
        mas, hmask = _pair_masks(rope)
        lane = lax.broadcasted_iota(jnp.int32, (1, LANES), 1)

        @pl.when(kb == 0)
        def _():
            dq_ref[...] = jnp.zeros_like(dq_ref)
            if bias:
                drow_ref[...] = jnp.zeros_like(drow_ref)

        def key_pass(n, w):
            kk = k_ref[0:n, :]
            if rope:
                kk = jnp.concatenate([kk, kr_ref[0:n, :]], axis=1)
            vh = _mask2(v_ref[0:n, :], mas)
            kcat = jnp.concatenate(_mask2(kk, hmask), axis=0)
            if bias:
                kcat = kcat * scale
            diag_mask = (lax.broadcasted_iota(jnp.int32, (n, w), 0) <= lax.broadcasted_iota(jnp.int32, (n, w), 1))

            def chunk(qc, carry, mask):
                carry = list(carry)
                q0 = qc * w if isinstance(qc, int) else pl.multiple_of(qc * w, w)
                dov = do_ref[pl.ds(q0, w), :]
                doh = _mask2(dov, mas)
                qh = _mask2(q_ref[pl.ds(q0, w), :], hmask)
                if bias:
                    qh = [x * scale for x in qh]
                pbs, dss = [], []
                for h in range(2):
                    s = _dot(kk, qh[h], 1, 1)
                    if rope:
                        s = s * scale
                    if bias:
                        s = s + jnp.concatenate([nb_ref[h, 0:n, :]] * (w // LANES), axis=1)
                    p = jnp.exp(s - lse_ref[0, h:h + 1, pl.ds(q0, w)])
                    if mask is not None:
                        p = jnp.where(mask, p, 0.0)
                    ds = p * (_dot(vh[h], dov, 1, 1) - dl_ref[0, h:h + 1, pl.ds(q0, w)])
                    if bias:
                        drow_ref[0, h:h + 1, pl.ds(q0, w)] += jnp.sum(ds, axis=0, keepdims=True)
                        carry[2 + h] = carry[2 + h] + jnp.sum(ds, axis=1, keepdims=True)
                    else:
                        ds = ds * scale
                    pbs.append(p.astype(BF16))
                    dss.append(ds.astype(BF16))
                ds_lanes = jnp.concatenate(dss, axis=1)
                ds_rows = jnp.concatenate(dss, axis=0)
                carry[0] = carry[0] + _dot(ds_lanes, jnp.concatenate(qh, axis=0), 1, 0)
                carry[1] = carry[1] + _dot(jnp.concatenate(pbs, axis=1), jnp.concatenate(doh, axis=0), 1, 0)
                dq_ref[pl.ds(q0, w), :] += _dot(ds_rows, kcat, 0, 0)
                return tuple(carry)

            c = [jnp.zeros((n, qw), F32), jnp.zeros((n, LANES), F32)]
            if bias:
                c += [jnp.zeros((n, 1), F32), jnp.zeros((n, 1), F32)]
            c = tuple(c)
            if w != BLK:
                for qc in range(lp // w):
                    c = chunk(qc, c, diag_mask if qc == 0 else None)
            else:
                groups = (nb - kb) // UNROLL

                def several(t, cr):
                    for u in range(UNROLL):
                        cr = chunk(kb + UNROLL * t + u, cr, (diag_mask | (t > 0)) if u == 0 else None)
                    return cr

                c = lax.fori_loop(0, groups, several, c)
                start = kb + UNROLL * groups
                pairs = (nb - start) // 2

                def two(t, cr):
                    qc = start + 2 * t
                    return chunk(qc + 1, chunk(qc, cr, diag_mask | (qc > kb)), None)

                c = lax.fori_loop(0, pairs, two, c)
                c = lax.fori_loop(start + 2 * pairs, nb, lambda qc, cr: chunk(qc, cr, diag_mask | (qc > kb)), c)

            def rows(a, dtype):
                a = a.astype(dtype)
                return a if n == BLK else jnp.concatenate([a, jnp.zeros((BLK - n, a.shape[1]), dtype)], axis=0)

            dk_ref[...] = rows(c[0][:, 0:LANES], BF16)
            dv_ref[...] = rows(c[1], BF16)
            if rope:
                x_ref[0] = rows(c[0][:, LANES:2 * LANES], F32)
            if bias:
                x_ref[0] = rows(jnp.where(lane == 0, c[2], jnp.where(lane == 1, c[3], 0.0)), F32)

        @pl.when(kb == 0)
        def _():
            key_pass(N_META, lp // 2)

        @pl.when(kb > 0)
        def _():
            key_pass(BLK, BLK)

        @pl.when(kb == nb - 1)
        def _():
            def fin(c, carry):
                r0 = pl.multiple_of(c * BLK, BLK)
                dq = dq_ref[pl.ds(r0, BLK), :]
                if rope:
                    back = _rope(dq[:, LANES:2 * LANES], ct_ref[pl.ds(r0, BLK), :], -st_ref[pl.ds(r0, BLK), :])
                    dq = jnp.concatenate([dq[:, 0:LANES], back], axis=1)
                dq_out[pl.ds(r0, BLK), :] = dq.astype(BF16)
                return carry

            lax.fori_loop(0, nb, fin, 0)

    in_specs = [pl.BlockSpec((lp, qw), lambda p, j: (0, qcol + p)),
                pl.BlockSpec((BLK, 128), lambda p, j: (j, kcol(p))),
                pl.BlockSpec((BLK, 128), lambda p, j: (j, vcol(p)))]
    ins = [q, k, v]
    if rope:
        in_specs.append(pl.BlockSpec((BLK, 128), lambda p, j: (j, 0)))
        ins.append(kr)
    if bias:
        in_specs.append(pl.BlockSpec((2, BLK, 128), lambda p, j: (p, j, 0)))
        ins.append(nbrep)
    in_specs += [pl.BlockSpec((lp, 128), lambda p, j: (0, p)), pl.BlockSpec((1, 2, lp), lambda p, j: (p, 0, 0)),
                 pl.BlockSpec((1, 2, lp), lambda p, j: (p, 0, 0))]
    ins += [do, delta, lse]
    if rope:
        in_specs += [pl.BlockSpec((lp, 128), lambda p, j: (0, 0))] * 2
        ins += list(rtabs)
    out_specs = [pl.BlockSpec((lp, qw), lambda p, j: (0, p)),
                 pl.BlockSpec((BLK, 128), lambda p, j: (j, p)),
                 pl.BlockSpec((BLK, 128), lambda p, j: (j, p)),
                 pl.BlockSpec((1, BLK, 128), lambda p, j: (p, j, 0))]
    out_shape = [jax.ShapeDtypeStruct((lp, PAIRS * qw), BF16), jax.ShapeDtypeStruct((lp, D_MODEL), BF16),
                 jax.ShapeDtypeStruct((lp, D_MODEL), BF16), jax.ShapeDtypeStruct((PAIRS, lp, 128), F32)]
    if bias:
        out_specs.append(pl.BlockSpec((1, 2, lp), lambda p, j: (p, 0, 0)))
        out_shape.append(jax.ShapeDtypeStruct((PAIRS, 2, lp), F32))
    return pl.pallas_call(
        body, name=name, grid=(PAIRS, nb), in_specs=in_specs, out_specs=out_specs, out_shape=out_shape,
        scratch_shapes=[pltpu.VMEM((lp, qw), F32)],
        compiler_params=_cp(("parallel", "arbitrary"), VMEM_BIG))(*ins)


def _adamw(w, g, m, v, name):
    lead = w.ndim - 2
    rows, cols = w.shape[lead:]
    big = rows * cols > 512 * 1024
    tr = 128 if big and rows % 128 == 0 else rows
    tc = 256 if big and tr == rows else cols

    def body(w_ref, g_ref, m_ref, v_ref, d_ref, nm_ref, nv_ref):
        gv = g_ref[...]
        nm = ADAM_B1 * m_ref[...] + (1.0 - ADAM_B1) * gv
        nv = ADAM_B2 * v_ref[...] + (1.0 - ADAM_B2) * (gv * gv)
        m_hat = nm / (1.0 - ADAM_B1 ** ADAM_STEP)
        v_hat = nv / (1.0 - ADAM_B2 ** ADAM_STEP)
        d_ref[...] = -ADAM_LR * (m_hat / (jnp.sqrt(v_hat) + ADAM_EPS) + ADAM_WD * w_ref[...])
        nm_ref[...] = nm
        nv_ref[...] = nv

    spec = pl.BlockSpec((1,) * lead + (tr, tc), lambda i, j: (0,) * lead + (i, j))
    return pl.pallas_call(
        body, name=name, grid=(rows // tr, cols // tc), in_specs=[spec] * 4, out_specs=[spec] * 3,
        out_shape=[jax.ShapeDtypeStruct(w.shape, F32)] * 3,
        compiler_params=_cp(("parallel", "parallel"), VMEM_BIG))(w, g, m, v)


def _add_cores(g, from_sib, name):
    n, rows, cols = g.shape
    half = rows // 2
    tr = _tile(half, (256, 240))
    nt = half // tr

    def body(lo_ref, hi_ref, s_ref, o_ref):
        mine = jnp.where(lax.axis_index("c") == 0, lo_ref[0], hi_ref[0])
        o_ref[0] = (mine + s_ref[0]).astype(BF16)

    return pl.pallas_call(
        body, name=name, grid=(n, nt),
        in_specs=[pl.BlockSpec((1, tr, cols), lambda j, i: (j, i, 0)),
                  pl.BlockSpec((1, tr, cols), lambda j, i: (j, nt + i, 0)),
                  pl.BlockSpec((1, tr, cols), lambda j, i: (j, i, 0))],
        out_specs=pl.BlockSpec((1, tr, cols), lambda j, i: (j, i, 0)),
        out_shape=jax.ShapeDtypeStruct((n, half, cols), BF16),
        compiler_params=_cp(("parallel", "parallel"), VMEM_BIG))(g, g, from_sib)


def _add_chips(x, own, name):
    n, rows, cols = x.shape
    tr = _tile(rows, (256, 240))

    def body(x_ref, own_ref, o_ref):
        me = 2 * lax.axis_index("x") + lax.axis_index("y")
        v = [jnp.where(me == k, own_ref[...], x_ref[k]).astype(F32) for k in range(N_CHIPS)]
        o_ref[...] = ((v[0] + v[1]) + v[2]) + v[3]

    return pl.pallas_call(
        body, name=name, grid=(rows // tr,),
        in_specs=[pl.BlockSpec((n, tr, cols), lambda i: (0, i, 0)), pl.BlockSpec((tr, cols), lambda i: (i, 0))],
        out_specs=pl.BlockSpec((tr, cols), lambda i: (i, 0)),
        out_shape=jax.ShapeDtypeStruct((rows, cols), F32), compiler_params=_cp(("parallel",), VMEM_BIG))(x, own)


def _axes():
    return lax.axis_index("x"), lax.axis_index("y"), lax.axis_index("c")


def _other_chips(x, y):
    return [(1 - x, y), (x, 1 - y), (1 - x, 1 - y)]


ANY = pl.BlockSpec(memory_space=pl.ANY)


def _rcopy(src, dst, send_sems, recv_sems, k, to):
    return pltpu.make_async_remote_copy(src_ref=src, dst_ref=dst, send_sem=send_sems.at[k], recv_sem=recv_sems.at[k],
                                        device_id=to, device_id_type=MESH)


def _gather_weights(shards, meta):
    n = len(shards)

    def body(*refs):
        srcs, meta_ref = refs[:n], refs[n]
        outs, mout_ref = refs[n + 1:2 * n + 1], refs[2 * n + 1]
        send_sems, recv_sems = refs[2 * n + 2:]
        x, y, c = _axes()
        me = 2 * x + y
        sib = (x, y, 1 - c)
        chips = _other_chips(x, y)

        def half(t, chip_idx, cc):
            hr = shards[t].shape[0] // 2
            return outs[t].at[chip_idx, pl.ds(cc * hr, hr), :]

        first = []
        for j, (px, py) in enumerate(chips):
            for t in range(n):
                hr = shards[t].shape[0] // 2
                first.append(_rcopy(srcs[t].at[pl.ds(c * hr, hr), :], half(t, me, c), send_sems, recv_sems,
                                    3 * t + j, (px, py, c)))
            first.append(_rcopy(meta_ref, mout_ref.at[me], send_sems, recv_sems, 3 * n + j, (px, py, c)))
        for cp in first:
            cp.start()
        passed = []
        for j, (px, py) in enumerate(chips):
            src_chip = 2 * px + py
            for t in range(n):
                _rcopy(half(t, src_chip, c), half(t, src_chip, c), send_sems, recv_sems, 3 * t + j, sib).wait_recv()
                fwd = _rcopy(half(t, src_chip, c), half(t, src_chip, c), send_sems, recv_sems, 3 * (n + 1 + t) + j, sib)
                fwd.start()
                passed.append(fwd)
            _rcopy(mout_ref.at[src_chip], mout_ref.at[src_chip], send_sems, recv_sems, 3 * n + j, sib).wait_recv()
        for j, (px, py) in enumerate(chips):
            src_chip = 2 * px + py
            for t in range(n):
                _rcopy(half(t, src_chip, 1 - c), half(t, src_chip, 1 - c), send_sems, recv_sems,
                       3 * (n + 1 + t) + j, sib).wait_recv()
        for cp in first + passed:
            cp.wait_send()

    nsem = 3 * (2 * n + 1)
    return pl.pallas_call(
        body, name="gather_weights", in_specs=[ANY] * (n + 1), out_specs=[ANY] * (n + 1),
        out_shape=[jax.ShapeDtypeStruct((N_CHIPS,) + s.shape, s.dtype) for s in shards]
        + [jax.ShapeDtypeStruct((N_CHIPS,) + meta.shape, meta.dtype)],
        scratch_shapes=[pltpu.SemaphoreType.DMA((nsem,)), pltpu.SemaphoreType.DMA((nsem,))])(*shards, meta)


def _gather_late(shard):
    rows, cols = shard.shape
    hr = rows // 2
    src = jax.new_ref(shard, memory_space=pltpu.MemorySpace.HBM)
    out = jax.empty_ref(jax.ShapeDtypeStruct((N_CHIPS, rows, cols), shard.dtype), memory_space=pltpu.MemorySpace.HBM)

    @pl.kernel(mesh=plsc.ScalarSubcoreMesh(axis_name="seq", num_cores=1), name="gather_late",
               scratch_types=(pltpu.SemaphoreType.DMA((6,)), pltpu.SemaphoreType.DMA((6,))),
               compiler_params=pltpu.CompilerParams(collective_id=1))
    def launch(send_sems, recv_sems):
        x, y, c = _axes()
        me = 2 * x + y
        sib = (x, y, 1 - c)
        chips = _other_chips(x, y)
        barrier = pltpu.get_barrier_semaphore()
        for px, py in chips:
            pl.semaphore_signal(barrier, inc=1, device_id=(px, py, c), device_id_type=MESH)
        pl.semaphore_signal(barrier, inc=1, device_id=sib, device_id_type=MESH)
        pl.semaphore_wait(barrier, 4)

        def half(chip_idx, cc):
            return out.at[chip_idx, pl.ds(cc * hr, hr), :]

        first = [_rcopy(src.at[pl.ds(c * hr, hr), :], half(me, c), send_sems, recv_sems, j, (px, py, c))
                 for j, (px, py) in enumerate(chips)]
        for cp in first:
            cp.start()
        passed = []
        for j, (px, py) in enumerate(chips):
            land = half(2 * px + py, c)
            _rcopy(land, land, send_sems, recv_sems, j, sib).wait_recv()
            fwd = _rcopy(land, land, send_sems, recv_sems, 3 + j, sib)
            fwd.start()
            passed.append(fwd)
        for j, (px, py) in enumerate(chips):
            land = half(2 * px + py, 1 - c)
            _rcopy(land, land, send_sems, recv_sems, 3 + j, sib).wait_recv()
        for cp in first + passed:
            cp.wait_send()

    launch()
    return out[...]


def _swap_halves(gs):
    n = len(gs)
    ncopies = sum(g.shape[0] for g in gs)

    def body(*refs):
        srcs, outs = refs[:n], refs[n:2 * n]
        send_sems, recv_sems = refs[2 * n:]
        x, y, c = _axes()
        cps = []
        for t in range(n):
            hr = gs[t].shape[1] // 2
            for j in range(gs[t].shape[0]):
                cps.append(_rcopy(srcs[t].at[j, pl.ds((1 - c) * hr, hr), :], outs[t].at[j], send_sems, recv_sems,
                                  len(cps), (x, y, 1 - c)))
        for cp in cps:
            cp.start()
        for cp in cps:
            cp.wait()

    return pl.pallas_call(
        body, name="swap_halves", in_specs=[ANY] * n, out_specs=[ANY] * n,
        out_shape=[jax.ShapeDtypeStruct((g.shape[0], g.shape[1] // 2, g.shape[2]), g.dtype) for g in gs],
        scratch_shapes=[pltpu.SemaphoreType.DMA((ncopies,)), pltpu.SemaphoreType.DMA((ncopies,))])(*gs)


def _scatter_chips(parts):
    n = len(parts)
    srcs = [jax.new_ref(p, memory_space=pltpu.MemorySpace.HBM) for p in parts]
    outs = [jax.empty_ref(jax.ShapeDtypeStruct(p.shape, p.dtype), memory_space=pltpu.MemorySpace.HBM) for p in parts]

    @pl.kernel(mesh=plsc.ScalarSubcoreMesh(axis_name="seq", num_cores=1), name="scatter_chips",
               scratch_types=(pltpu.SemaphoreType.DMA((3 * n,)), pltpu.SemaphoreType.DMA((3 * n,))),
               compiler_params=pltpu.CompilerParams(collective_id=0))
    def launch(send_sems, recv_sems):
        x, y, c = _axes()
        me = 2 * x + y
        chips = _other_chips(x, y)
        barrier = pltpu.get_barrier_semaphore()
        for px, py in chips:
            pl.semaphore_signal(barrier, inc=1, device_id=(px, py, c), device_id_type=MESH)
        pl.semaphore_wait(barrier, 3)
        cps = []
        for j, (px, py) in enumerate(chips):
            for t in range(n):
                cps.append(_rcopy(srcs[t].at[2 * px + py], outs[t].at[me], send_sems, recv_sems, 3 * t + j,
                                  (px, py, c)))
        for cp in cps:
            cp.start()
        for cp in cps:
            cp.wait()

    launch()
    return [o[...] for o in outs]


def _swap_reduced(rs):
    n = len(rs)

    def body(*refs):
        srcs, outs = refs[:n], refs[n:2 * n]
        send_sems, recv_sems = refs[2 * n:]
        x, y, c = _axes()
        cps = [_rcopy(srcs[t], outs[t], send_sems, recv_sems, t, (x, y, 1 - c)) for t in range(n)]
        for cp in cps:
            cp.start()
        for cp in cps:
            cp.wait()

    return pl.pallas_call(
        body, name="swap_reduced", in_specs=[ANY] * n, out_specs=[ANY] * n,
        out_shape=[jax.ShapeDtypeStruct(r.shape, r.dtype) for r in rs],
        scratch_shapes=[pltpu.SemaphoreType.DMA((n,)), pltpu.SemaphoreType.DMA((n,))])(*rs)


SMALL_ROWS = 24 + 128


def _allreduce_small(vec):
    def body(v_ref, out_ref, slots, send_sems, recv_sems):
        x, y, c = _axes()
        me = 4 * x + 2 * y + c
        slots[me] = v_ref[...]
        cps = []
        for k in range(1, 8):
            kx, ky, kc = (k >> 2) & 1, (k >> 1) & 1, k & 1
            peer = (1 - x if kx else x, 1 - y if ky else y, 1 - c if kc else c)
            cps.append(_rcopy(v_ref, slots.at[me], send_sems, recv_sems, k - 1, peer))
        for cp in cps:
            cp.start()
        for cp in cps:
            cp.wait()
        tot = slots[0]
        for k in range(1, 8):
            tot = tot + slots[k]
        out_ref[...] = tot

    return pl.pallas_call(
        body, name="allreduce_small",
        in_specs=[pl.BlockSpec(memory_space=pltpu.VMEM)], out_specs=pl.BlockSpec(memory_space=pltpu.VMEM),
        out_shape=jax.ShapeDtypeStruct((SMALL_ROWS, 128), F32),
        scratch_shapes=[pltpu.VMEM((8, SMALL_ROWS, 128), F32), pltpu.SemaphoreType.DMA((7,)),
                        pltpu.SemaphoreType.DMA((7,))])(vec)


def _pack_p2(w_uq, w_ukv, w_br_mla, w_br_fox, w_out, dtype):
    parts = [w_uq.reshape(96, D_MODEL), w_ukv.reshape(64, D_MODEL), w_br_mla, w_br_fox, w_out]
    return jnp.concatenate([p.astype(dtype) for p in parts], axis=0)


def _unpack_p2(pk):
    return pk[0:96].reshape(256, 384), pk[96:160].reshape(128, 512), pk[160:416], pk[416:672], pk[672:928]


def _uq_arrange(w):
    w3 = w.reshape(256, HEADS, 96)
    nope = w3[:, :, :64].reshape(256, PAIRS, 128)
    pe = w3[:, :, 64:].reshape(256, PAIRS, 64)
    return jnp.concatenate([nope, pe, jnp.zeros((256, PAIRS, 64), w.dtype)], axis=2).reshape(256, PAIRS * 256)


def _uq_restore(g):
    g3 = g.reshape(256, PAIRS, 256)
    nope = g3[:, :, :128].reshape(256, HEADS, 64)
    pe = g3[:, :, 128:192].reshape(256, HEADS, 32)
    return jnp.concatenate([nope, pe], axis=2).reshape(256, HEADS * 96)


def _ukv_arrange(w):
    w3 = w.reshape(128, HEADS, 128)
    return jnp.concatenate([w3[:, :, :64].reshape(128, 1024), w3[:, :, 64:].reshape(128, 1024)], axis=1)


def _ukv_restore(g):
    kn = g[:, :1024].reshape(128, HEADS, 64)
    vv = g[:, 1024:].reshape(128, HEADS, 64)
    return jnp.concatenate([kn, vv], axis=2).reshape(128, HEADS * 128)


def _rope_tables(lp):
    r = np.arange(lp)
    pos = np.where(r < N_META, r, np.where(r >= PAD, r - PAD + N_META, 0)).astype(np.float32)
    half = MLA_ROPE // 2
    inv_freq = np.float32(ROPE_THETA) ** (-np.arange(half, dtype=np.float32) / np.float32(half))
    ang = (pos[:, None] * inv_freq[None, :]).astype(np.float32)
    cos, sin = np.cos(ang).astype(np.float32), np.sin(ang).astype(np.float32)
    one, zero = np.ones((lp, 64), np.float32), np.zeros((lp, 64), np.float32)
    return (jnp.asarray(np.concatenate([cos, cos, cos, cos, one], axis=1)),
            jnp.asarray(np.concatenate([-sin, sin, -sin, sin, zero], axis=1)))


def _pad_lanes(v, n=128):
    return jnp.pad(v, ((0, 0), (0, n - v.shape[1])))


def _in_cols(slabs, a, b):
    out = []
    for j in range(N_CHIPS):
        lo, hi = max(a, W_IN_SHARD * j), min(b, W_IN_SHARD * (j + 1))
        if lo < hi:
            out.append(slabs[j][:, lo - W_IN_SHARD * j:hi - W_IN_SHARD * j])
    return out


def _local_step(x2, tgt2, meta_f, w_small, w_attn, w_gate, w_uq_f, w_ukv_f, w_bm, w_bf, w_o, pre_norm_g,
                post_norm_g, mla_q_norm_g, mla_kv_norm_g, fox_forget_b, start_exchange=None):
    s_rows = x2.shape[0]
    lp = PAD + s_rows
    w_uq_a = _uq_arrange(w_uq_f)
    w_ukv_a = _ukv_arrange(w_ukv_f)

    ctab, stab = _rope_tables(lp)
    ii = jnp.arange(BLK)
    tri_lo = (ii[:, None] >= ii[None, :]).astype(BF16)
    tri_up = (ii[:, None] <= ii[None, :]).astype(BF16)
    fb128 = _pad_lanes(fox_forget_b)

    u = _rms_pre(x2, meta_f, pre_norm_g)
    small = _mm(u, w_small, mode="nn", out_dtype=F32, name="proj_small")
    attn = _mm(u, w_attn, mode="nn", out_dtype=BF16, name="proj_attn")
    gate = _mm(u, w_gate, mode="nn", out_dtype=BF16, name="proj_gate")
    qn, kvn, kr, ncum = _small_prep(small, mla_q_norm_g, mla_kv_norm_g, fb128, ctab, stab, tri_lo)
    qcat = _mm(qn, w_uq_a, mode="nn", out_dtype=BF16, name="mla_q", epilogue=_rope_pairs, row_ins=(ctab, stab))
    kv = _mm(kvn, w_ukv_a, mode="nn", out_dtype=BF16, name="mla_kv")
    nbrep = jnp.broadcast_to(ncum[:, :HEADS].T[:, :, None], (HEADS, lp, LANES))

    mla_cols = dict(qcol=0, kcol=lambda p: p, vcol=lambda p: PAIRS + p)
    fox_cols = dict(qcol=0, kcol=lambda p: PAIRS + p, vcol=lambda p: 2 * PAIRS + p)
    o_mla, lse_mla = _attn_fwd(qcat, kv, kv, kr=kr, scale=MLA_SCALE, name="mla_fwd", **mla_cols)
    o_fox, lse_fox = _attn_fwd(attn, attn, attn, nbrep=nbrep, scale=FOX_SCALE, name="fox_fwd", **fox_cols)

    a_mla, a_fox = _gate_fwd(o_mla, o_fox, gate)
    y_mla = _mm(a_mla, w_bm, mode="nn", out_dtype=BF16, name="br_mla")
    y_fox = _mm(a_fox, w_bf, mode="nn", out_dtype=BF16, name="br_fox")
    mg = _merge_fwd(gate, y_mla, y_fox)
    mixed = _mm(mg, w_o, mode="nn", out_dtype=F32, name="out_proj")
    dmixed, dy, loss_p, dg_post = _tail(x2, mixed, tgt2, post_norm_g)

    d_w_out = _mm(mg, dmixed, mode="tn", out_dtype=F32, name="d_w_out")
    dm = _mm(dmixed, w_o, mode="nt", out_dtype=BF16, name="d_merge")
    dy_mla, dy_fox, dgate_ab = _merge_bwd(dm, gate, y_mla, y_fox)
    d_w_bm = _mm(a_mla, dy_mla, mode="tn", out_dtype=F32, name="d_w_br_mla")
    d_w_bf = _mm(a_fox, dy_fox, mode="tn", out_dtype=F32, name="d_w_br_fox")
    da_mla = _mm(dy_mla, w_bm, mode="nt", out_dtype=BF16, name="d_a_mla")
    da_fox = _mm(dy_fox, w_bf, mode="nt", out_dtype=BF16, name="d_a_fox")
    do_mla, do_fox, dgate_z, dl_mla, dl_fox = _gate_bwd(da_mla, da_fox, o_mla, o_fox, gate)
    dl_mla, dl_fox = (d[:, :HEADS].T.reshape(PAIRS, 2, lp) for d in (dl_mla, dl_fox))

    dq_a, dkn, dvm, dkr = _attn_bwd(qcat, kv, kv, do_mla, dl_mla, lse_mla, kr=kr, rtabs=(ctab, stab),
                                    scale=MLA_SCALE, name="mla_bwd", **mla_cols)
    dfq, dfk, dfv, dcol, drow = _attn_bwd(attn, attn, attn, do_fox, dl_fox, lse_fox, nbrep=nbrep, scale=FOX_SCALE,
                                          name="fox_bwd", **fox_cols)

    d_w_uq_a = _mm(qn, dq_a, mode="tn", out_dtype=F32, name="d_w_uq")
    dqn = _mm(dq_a, w_uq_a, mode="nt", out_dtype=F32, name="d_qn")
    d_w_ukv_a = jnp.concatenate([_mm(kvn, dkn, mode="tn", out_dtype=F32, name="d_w_uk"),
                                 _mm(kvn, dvm, mode="tn", out_dtype=F32, name="d_w_uv")], axis=1)
    dkvn = _mm(dkn, w_ukv_a[:, :1024], mode="nt", out_dtype=F32, name="d_kvn_k")
    dkvn = _mm(dvm, w_ukv_a[:, 1024:], mode="nt", out_dtype=F32, name="d_kvn_v", acc=dkvn)
    dsmall, dg_q, dg_kv, dfb = _small_bwd(small, dqn, dkvn, dkr, dcol, drow, mla_q_norm_g, mla_kv_norm_g,
                                          fb128, ctab, stab, tri_up)

    dw_small = _mm(u, dsmall, mode="tn", out_dtype=F32, name="d_w_small")
    dw_fq = _mm(u, dfq, mode="tn", out_dtype=F32, name="d_w_fq")
    dw_fk = _mm(u, dfk, mode="tn", out_dtype=F32, name="d_w_fk")
    dw_fv = _mm(u, dfv, mode="tn", out_dtype=F32, name="d_w_fv")
    dw_z = _mm(u, dgate_z, mode="tn", out_dtype=F32, name="d_w_z")
    dw_g = _mm(u, dgate_ab, mode="tn", out_dtype=F32, name="d_w_g")
    d_w_in = (dw_small, dw_z, dw_fq, dw_fk, dw_fv, dw_g)
    d_w_uq = _uq_restore(d_w_uq_a)
    d_w_ukv = _ukv_restore(d_w_ukv_a)
    token = start_exchange(d_w_in, d_w_uq, d_w_ukv, d_w_bm, d_w_bf, d_w_out) if start_exchange else None
    du = _mm_sum_nt([(dsmall, w_small), (dfq, w_attn[:, 0:1024]), (dfk, w_attn[:, 1024:2048]),
                     (dfv, w_attn[:, 2048:3072]), (dgate_z, w_gate[:, 0:2048]), (dgate_ab, w_gate[:, 2048:4096])],
                    name="d_u", after=token)
    dx, dmeta, dg_pre = _pre_bwd(du, x2, meta_f, dy, pre_norm_g)
    return (loss_p, dx, dmeta, d_w_in, d_w_uq, d_w_ukv, d_w_bm, d_w_bf, d_w_out, dg_pre, dg_post, dg_q, dg_kv, dfb)


def _w_in_slabs(pieces):
    dw_small, dw_z, dw_fq, dw_fk, dw_fv, dw_g = pieces
    runs = [(dw_small[:, 0:416], C_CQ), (dw_z[:, 0:1024], C_ZMLA), (dw_fq, C_FQ), (dw_fk, C_FK), (dw_fv, C_FV),
            (dw_small[:, 512:528], C_FL), (dw_z[:, 1024:2048], C_ZFOX), (dw_g, C_GA)]
    slabs = []
    for j in range(N_CHIPS):
        lo, hi = W_IN_SHARD * j, W_IN_SHARD * (j + 1)
        cols = [a[:, max(lo, c0) - c0:min(hi, c0 + a.shape[1]) - c0] for a, c0 in runs
                if max(lo, c0) < min(hi, c0 + a.shape[1])]
        slabs.append(jnp.concatenate(cols, axis=1))
    return jnp.stack(slabs, axis=0)


def kernel(x, meta_tokens, pre_norm_g, w_in, fox_forget_b, mla_q_norm_g, mla_kv_norm_g, w_uq, w_ukv, w_br_mla, w_br_fox, w_out, post_norm_g, loss_target, m_meta_tokens, m_pre_norm_g, m_w_in, m_fox_forget_b, m_mla_q_norm_g, m_mla_kv_norm_g, m_w_uq, m_w_ukv, m_w_br_mla, m_w_br_fox, m_w_out, m_post_norm_g, v_meta_tokens, v_pre_norm_g, v_w_in, v_fox_forget_b, v_mla_q_norm_g, v_mla_kv_norm_g, v_w_uq, v_w_ukv, v_w_br_mla, v_w_br_fox, v_w_out, v_post_norm_g):
    me = 2 * lax.axis_index("x") + lax.axis_index("y")
    core = lax.axis_index("c")
    w_in_b = w_in.astype(BF16).reshape(D_MODEL, W_IN_SHARD)
    p2 = _pack_p2(w_uq[0], w_ukv[0], w_br_mla[0], w_br_fox[0], w_out[0], BF16)
    w_in_g, meta_g = _gather_weights([w_in_b], meta_tokens)
    p2_g = _gather_late(lax.optimization_barrier((p2, w_in_g))[0])
    slabs = [jnp.where(me == j, w_in_b, w_in_g[j]) for j in range(N_CHIPS)]
    pieces = [_unpack_p2(jnp.where(me == j, p2, p2_g[j])) for j in range(N_CHIPS)]
    w_uq_f = jnp.concatenate([p[0] for p in pieces], axis=1)
    w_ukv_f = jnp.concatenate([p[1] for p in pieces], axis=1)
    w_bm = jnp.concatenate([p[2] for p in pieces], axis=0)
    w_bf = jnp.concatenate([p[3] for p in pieces], axis=0)
    w_o = jnp.concatenate([p[4] for p in pieces], axis=0)
    meta_f = jnp.concatenate([jnp.where(me == j, meta_tokens, meta_g[j]) for j in range(N_CHIPS)], axis=1)
    kpe = _in_cols(slabs, C_KPE, C_ZMLA)
    w_small = jnp.concatenate(_in_cols(slabs, C_CQ, C_KPE) + kpe + kpe + [jnp.zeros((D_MODEL, 64), BF16)]
                              + _in_cols(slabs, C_FL, C_ZFOX) + [jnp.zeros((D_MODEL, 112), BF16)], axis=1)
    w_attn = jnp.concatenate(_in_cols(slabs, C_FQ, C_FL), axis=1)
    w_gate = jnp.concatenate(_in_cols(slabs, C_ZMLA, C_FQ) + _in_cols(slabs, C_ZFOX, C_END), axis=1)

    exchange = {}

    def start_exchange(d_w_in, d_w_uq, d_w_ukv, d_w_bm, d_w_bf, d_w_out):
        g2 = jnp.stack([_pack_p2(d_w_uq[:, 384 * j:384 * (j + 1)], d_w_ukv[:, 512 * j:512 * (j + 1)],
                                 d_w_bm[256 * j:256 * (j + 1)], d_w_bf[256 * j:256 * (j + 1)],
                                 d_w_out[256 * j:256 * (j + 1)], F32) for j in range(N_CHIPS)], axis=0)
        pieces = [p[None] for p in d_w_in]
        from_sib = _swap_halves(pieces + [g2])
        halves = [_add_cores(p, s, "add_cores_" + nm)[0]
                  for p, s, nm in zip(pieces, from_sib, ("small", "z", "fq", "fk", "fv", "g"))]
        parts = [_w_in_slabs(halves), _add_cores(g2, from_sib[-1], "add_cores_rest")]
        exchange.update(parts=parts, landed=_scatter_chips(parts))
        return parts[0][0, 0:16, 0:LANES]

    (loss_p, dx, dmeta, _, _, _, _, _, _, dg_pre, dg_post, dg_q, dg_kv,
     dfb) = _local_step(x[0], loss_target[0], meta_f, w_small, w_attn, w_gate, w_uq_f, w_ukv_f, w_bm, w_bf, w_o,
                        pre_norm_g, post_norm_g, mla_q_norm_g, mla_kv_norm_g, fox_forget_b, start_exchange)

    mine = [_add_chips(l, lax.dynamic_index_in_dim(p, me, 0, keepdims=False), nm)
            for l, p, nm in zip(exchange["landed"], exchange["parts"], ("add_chips_w_in", "add_chips_rest"))]
    theirs = _swap_reduced(mine)
    g_w_in, g_p2 = [jnp.concatenate([jnp.where(core == 0, a, b), jnp.where(core == 0, b, a)], axis=0)
                    for a, b in zip(mine, theirs)]
    g_w_uq, g_w_ukv, g_w_bm, g_w_bf, g_w_out = _unpack_p2(g_p2)
    g_w_in = g_w_in[None]

    vec = jnp.concatenate([dg_pre.reshape(8, 128), dg_post.reshape(8, 128), dg_q.reshape(2, 128), dg_kv,
                           dfb, _pad_lanes(loss_p), jnp.zeros((3, 128), F32), dmeta.reshape(128, 128)], axis=0)
    tot = _allreduce_small(vec)
    loss = tot[20, 0]
    g_meta = lax.dynamic_slice_in_dim(tot[24:].reshape(N_META, D_MODEL), 256 * me, 256, axis=1)

    def small_pack(pre, post, gq_, gkv_, fb_):
        return jnp.concatenate([pre.reshape(8, 128), post.reshape(8, 128), gq_.reshape(2, 128), gkv_,
                                _pad_lanes(fb_), jnp.zeros((4, 128), F32)], axis=0)

    def small_unpack(t):
        return (t[0:8].reshape(1, 1024), t[8:16].reshape(1, 1024), t[16:18].reshape(1, 256), t[18:19],
                t[19:20, 0:HEADS])

    g_small = jnp.concatenate([tot[0:20], jnp.zeros((4, 128), F32)], axis=0)
    sm = _adamw(small_pack(pre_norm_g, post_norm_g, mla_q_norm_g, mla_kv_norm_g, fox_forget_b), g_small,
                small_pack(m_pre_norm_g, m_post_norm_g, m_mla_q_norm_g, m_mla_kv_norm_g, m_fox_forget_b),
                small_pack(v_pre_norm_g, v_post_norm_g, v_mla_q_norm_g, v_mla_kv_norm_g, v_fox_forget_b),
                "adamw_small")
    g_pre, g_post, g_q, g_kv, g_fb = small_unpack(g_small)
    (d_pre, d_post, d_q, d_kv, d_fb), (nm_pre, nm_post, nm_q, nm_kv, nm_fb), (nv_pre, nv_post, nv_q, nv_kv, nv_fb) = (
        small_unpack(t) for t in sm)

    d_meta, nm_meta, nv_meta = _adamw(meta_tokens, g_meta, m_meta_tokens, v_meta_tokens, "adamw_meta")
    d_win, nm_win, nv_win = (t.T[None] for t in _adamw(w_in[0].T, g_w_in[0].T, m_w_in[0].T, v_w_in[0].T,
                                                       "adamw_w_in"))
    d_wuq, nm_wuq, nv_wuq = _adamw(w_uq[0], g_w_uq, m_w_uq[0], v_w_uq[0], "adamw_w_uq")
    d_wukv, nm_wukv, nv_wukv = _adamw(w_ukv[0], g_w_ukv, m_w_ukv[0], v_w_ukv[0], "adamw_w_ukv")
    d_wbm, nm_wbm, nv_wbm = _adamw(w_br_mla[0], g_w_bm, m_w_br_mla[0], v_w_br_mla[0], "adamw_w_br_mla")
    d_wbf, nm_wbf, nv_wbf = _adamw(w_br_fox[0], g_w_bf, m_w_br_fox[0], v_w_br_fox[0], "adamw_w_br_fox")
    d_wo, nm_wo, nv_wo = _adamw(w_out[0], g_w_out, m_w_out[0], v_w_out[0], "adamw_w_out")

    def group(meta_, pre, win, fb_, q_, kv_, wuq, wukv, wbm, wbf, wo, post):
        return (meta_, pre, win, fb_, q_, kv_, wuq[None], wukv[None], wbm[None], wbf[None], wo[None], post)

    grads = group(g_meta, g_pre, g_w_in, g_fb, g_q, g_kv, g_w_uq, g_w_ukv, g_w_bm, g_w_bf, g_w_out, g_post)
    deltas = group(d_meta, d_pre, d_win, d_fb, d_q, d_kv, d_wuq, d_wukv, d_wbm, d_wbf, d_wo, d_post)
    new_m = group(nm_meta, nm_pre, nm_win, nm_fb, nm_q, nm_kv, nm_wuq, nm_wukv, nm_wbm, nm_wbf, nm_wo, nm_post)
    new_v = group(nv_meta, nv_pre, nv_win, nv_fb, nv_q, nv_kv, nv_wuq, nv_wukv, nv_wbm, nv_wbf, nv_wo, nv_post)
    return (loss, dx[None], *grads, *deltas, *new_m, *new_v)
```

```python
import math

import jax
import jax.numpy as jnp
import numpy as np
from jax import lax
from jax.experimental import pallas as pl
from jax.experimental.pallas import tpu as pltpu
from jax.experimental.pallas import tpu_sc as plsc

F32 = jnp.float32
BF16 = jnp.bfloat16

D_MODEL = 1024
N_META = 16
RMS_EPS = 1e-6
HEADS = 16
PAIRS = HEADS // 2
HEAD_DIM = 64
LANES = 128
MLA_ROPE = 32
MLA_SCALE = 1.0 / math.sqrt(64 + 32)
FOX_SCALE = 1.0 / math.sqrt(64)
ROPE_THETA = 10000.0

PAD = 256
BLK = 256
QB = 512
UNROLL = 4
NEG = -1e30

C_CQ, C_CKV, C_KPE, C_ZMLA, C_FQ, C_FK, C_FV, C_FL, C_ZFOX, C_GA, C_GB, C_END = (
    0, 256, 384, 416, 1440, 2464, 3488, 4512, 4528, 5552, 6576, 7600)
SMALL_W = 640
W_IN_SHARD = 1900

P2_ROWS = 928
N_CHIPS = 4

ADAM_LR = 0.001
ADAM_B1 = 0.9
ADAM_B2 = 0.999
ADAM_EPS = 1e-08
ADAM_WD = 0.01
ADAM_STEP = 10

VMEM_BIG = 56 * 1024 * 1024
MM_VMEM_BUDGET = 44 * 1024 * 1024
MESH = pl.DeviceIdType.MESH


def _cp(dims, vmem=None):
    return pltpu.CompilerParams(dimension_semantics=dims, vmem_limit_bytes=vmem)


def _dot(a, b, ca, cb):
    return lax.dot_general(a, b, (((ca,), (cb,)), ((), ())), preferred_element_type=F32)


def _sigmoid(x):
    return 1.0 / (1.0 + jnp.exp(-x))


def _tile(n, cands):
    for c in cands:
        if n % c == 0:
            return c
    return n


def _mm(a, b, *, mode, out_dtype, name, acc=None, epilogue=None, row_ins=(), after=None):
    if mode == "nn":
        (M, K), N = a.shape, b.shape[1]
    elif mode == "nt":
        (M, K), N = a.shape, b.shape[0]
    else:
        (K, M), N = a.shape, b.shape[1]
    tm = _tile(M, (1088, 1024)) if M > 1024 else M
    tn = _tile(N, (1024,)) if N > 1024 else N
    nk = 1
    while True:
        tk = K // nk
        need = 2 * tk * (tm * a.dtype.itemsize + tn * b.dtype.itemsize) + tm * tn * (
            2 * jnp.dtype(out_dtype).itemsize + (8 if acc is not None else 0) + (4 if nk > 1 else 0))
        if need <= MM_VMEM_BUDGET or (tk // 2) % (16 if mode == "tn" else LANES) or tk <= 512:
            break
        nk *= 2
    ca, cb = {"nn": (1, 0), "nt": (1, 1), "tn": (0, 0)}[mode]
    a_spec = (pl.BlockSpec((tk, tm), lambda j, i, k: (k, i)) if mode == "tn"
              else pl.BlockSpec((tm, tk), lambda j, i, k: (i, k)))
    b_spec = (pl.BlockSpec((tn, tk), lambda j, i, k: (j, k)) if mode == "nt"
              else pl.BlockSpec((tk, tn), lambda j, i, k: (k, j)))
    o_spec = pl.BlockSpec((tm, tn), lambda j, i, k: (i, j))
    has_acc = acc is not None

    nrow = len(row_ins)

    def body(*refs):
        a_ref, b_ref = refs[0], refs[1]
        acc_ref = refs[2] if has_acc else None
        rows = refs[2 + has_acc:2 + has_acc + nrow]
        o_ref = refs[2 + has_acc + nrow + (after is not None)]

        def store(tile):
            if epilogue is not None:
                tile = epilogue(tile, *[r[...] for r in rows])
            o_ref[...] = tile.astype(out_dtype)

        part = _dot(a_ref[...].astype(BF16), b_ref[...].astype(BF16), ca, cb)
        if nk == 1:
            store(part + acc_ref[...] if has_acc else part)
        else:
            sc = refs[-1]
            k = pl.program_id(2)

            @pl.when(k == 0)
            def _():
                sc[...] = part + acc_ref[...] if has_acc else part

            @pl.when(k > 0)
            def _():
                sc[...] += part

            @pl.when(k == nk - 1)
            def _():
                store(sc[...])

    ins = [a, b] + ([acc] if has_acc else []) + list(row_ins)
    in_specs = ([a_spec, b_spec] + ([o_spec] if has_acc else [])
                + [pl.BlockSpec((tm, r.shape[1]), lambda j, i, k: (i, 0)) for r in row_ins])
    if after is not None:
        ins.append(after)
        in_specs.append(pl.BlockSpec(after.shape, lambda j, i, k: (0,) * after.ndim))
    return pl.pallas_call(
        body, name=name, grid=(N // tn, M // tm, nk), in_specs=in_specs, out_specs=o_spec,
        out_shape=jax.ShapeDtypeStruct((M, N), out_dtype),
        scratch_shapes=[pltpu.VMEM((tm, tn), F32)] if nk > 1 else [],
        compiler_params=_cp(("parallel", "parallel", "arbitrary"), VMEM_BIG))(*ins)


def _mm_sum_nt(pairs, *, name, after=None):
    n = len(pairs)
    M, N = pairs[0][0].shape[0], pairs[0][1].shape[0]
    tm = _tile(M, (272,))

    def body(*refs):
        o_ref = refs[2 * n + (after is not None)]
        tot = _dot(refs[0][...].astype(BF16), refs[n][...].astype(BF16), 1, 1)
        for i in range(1, n):
            tot = tot + _dot(refs[i][...].astype(BF16), refs[n + i][...].astype(BF16), 1, 1)
        o_ref[...] = tot

    ins = [a for a, _ in pairs] + [b for _, b in pairs]
    in_specs = ([pl.BlockSpec((tm, a.shape[1]), lambda i: (i, 0)) for a, _ in pairs]
                + [pl.BlockSpec(b.shape, lambda i: (0, 0)) for _, b in pairs])
    if after is not None:
        ins.append(after)
        in_specs.append(pl.BlockSpec(after.shape, lambda i: (0,) * after.ndim))
    return pl.pallas_call(
        body, name=name, grid=(M // tm,), in_specs=in_specs, out_specs=pl.BlockSpec((tm, N), lambda i: (i, 0)),
        out_shape=jax.ShapeDtypeStruct((M, N), F32), compiler_params=_cp(("parallel",), VMEM_BIG))(*ins)


def _row(w):
    return pl.BlockSpec((BLK, w), lambda i: (i, 0))


def _rowc(w, c):
    return pl.BlockSpec((BLK, w), lambda i: (i, c))


def _full(shape):
    return pl.BlockSpec(shape, lambda i: tuple(0 for _ in shape))


def _rope(x, c, s):
    lane = lax.broadcasted_iota(jnp.int32, x.shape, 1)
    is_x1 = ((lane >> 4) & 1) == 0
    partner = jnp.where(is_x1, pltpu.roll(x, LANES - 16, 1), pltpu.roll(x, 16, 1))
    return x * c + partner * s


def _row_valid(i):
    rows = i * BLK + lax.broadcasted_iota(jnp.int32, (BLK, 1), 0)
    return (rows < N_META) | (rows >= PAD)


def _shift_rows(w):
    return pl.BlockSpec((BLK, w), lambda i: (jnp.maximum(i - 1, 0), 0))


def _h_block(i, x_ref, meta_ref):
    head = jnp.concatenate([meta_ref[...], jnp.zeros((BLK - N_META, D_MODEL), F32)], axis=0)
    return jnp.where(i == 0, head, x_ref[...])


def _rms_pre(x2, meta, g):
    lp = PAD + x2.shape[0]

    def body(x_ref, meta_ref, g_ref, u_ref):
        hv = _h_block(pl.program_id(0), x_ref, meta_ref)
        r = lax.rsqrt(jnp.mean(hv * hv, axis=-1, keepdims=True) + RMS_EPS)
        u_ref[...] = (hv * r * g_ref[...]).astype(BF16)

    return pl.pallas_call(
        body, name="rms_pre", grid=(lp // BLK,),
        in_specs=[_shift_rows(D_MODEL), _full((N_META, D_MODEL)), _full((1, D_MODEL))], out_specs=_row(D_MODEL),
        out_shape=jax.ShapeDtypeStruct((lp, D_MODEL), BF16),
        compiler_params=_cp(("parallel",)))(x2, meta, g)


def _split3(x):
    hi = x.astype(BF16)
    r1 = x - hi.astype(F32)
    mid = r1.astype(BF16)
    lo = (r1 - mid.astype(F32)).astype(BF16)
    return hi, mid, lo


def _small_prep(small, gq, gkv, fb, ctab, stab, tri):
    lp = small.shape[0]

    def body(sm_ref, gq_ref, gkv_ref, fb_ref, c_ref, s_ref, tri_ref, qn_ref, kvn_ref, kr_ref, ncum_ref, carry):
        i = pl.program_id(0)

        @pl.when(i == 0)
        def _():
            carry[...] = jnp.zeros_like(carry)

        cq = sm_ref[:, 0:256]
        r = lax.rsqrt(jnp.mean(cq * cq, axis=-1, keepdims=True) + RMS_EPS)
        qn_ref[...] = (cq * r * gq_ref[...]).astype(BF16)
        ckv = sm_ref[:, 256:384]
        r = lax.rsqrt(jnp.mean(ckv * ckv, axis=-1, keepdims=True) + RMS_EPS)
        kvn_ref[...] = (ckv * r * gkv_ref[...]).astype(BF16)
        kr_ref[...] = _rope(sm_ref[:, 384:512], c_ref[...], s_ref[...]).astype(BF16)
        fl = sm_ref[:, 512:640] + fb_ref[...]
        lf = jnp.minimum(fl, 0.0) - jnp.log(1.0 + jnp.exp(-jnp.abs(fl)))
        lf = jnp.where(_row_valid(i), lf, 0.0)
        hi, mid, lo = _split3(lf)
        t = tri_ref[...]
        cum = (_dot(t, hi, 1, 0) + _dot(t, mid, 1, 0)) + _dot(t, lo, 1, 0) + carry[...]
        ncum_ref[...] = -cum
        carry[...] = -ncum_ref[BLK - 1:BLK, :]

    return pl.pallas_call(
        body, name="small_prep", grid=(lp // BLK,),
        in_specs=[_row(SMALL_W), _full((1, 256)), _full((1, 128)), _full((1, 128)), _row(128), _row(128),
                  _full((BLK, BLK))],
        out_specs=[_row(256), _row(128), _row(128), _row(128)],
        out_shape=[jax.ShapeDtypeStruct((lp, 256), BF16), jax.ShapeDtypeStruct((lp, 128), BF16),
                   jax.ShapeDtypeStruct((lp, 128), BF16), jax.ShapeDtypeStruct((lp, 128), F32)],
        scratch_shapes=[pltpu.VMEM((1, 128), F32)],
        compiler_params=_cp(("arbitrary",)))(small, gq, gkv, fb, ctab, stab, tri)


def _rope_pairs(tile, c, s):
    out = []
    for lo in range(0, tile.shape[1], 256):
        out += [tile[:, lo:lo + 128], _rope(tile[:, lo + 128:lo + 256], c, s)]
    return jnp.concatenate(out, axis=1)


def _gate_fwd(o_mla, o_fox, gate):
    lp = o_mla.shape[0]

    def body(om_ref, of_ref, zm_ref, zf_ref, am_ref, af_ref):
        zm = zm_ref[...].astype(F32)
        am_ref[...] = (om_ref[...] * (zm * _sigmoid(zm))).astype(BF16)
        zf = zf_ref[...].astype(F32)
        af_ref[...] = (of_ref[...] * (zf * _sigmoid(zf))).astype(BF16)

    return pl.pallas_call(
        body, name="gate_fwd", grid=(lp // BLK,),
        in_specs=[_row(D_MODEL), _row(D_MODEL), _rowc(D_MODEL, 0), _rowc(D_MODEL, 1)],
        out_specs=[_row(D_MODEL), _row(D_MODEL)],
        out_shape=[jax.ShapeDtypeStruct((lp, D_MODEL), BF16)] * 2,
        compiler_params=_cp(("parallel",)))(o_mla, o_fox, gate, gate)


def _merge_fwd(gate, y_mla, y_fox):
    lp = y_mla.shape[0]

    def body(ga_ref, gb_ref, ym_ref, yf_ref, m_ref):
        sa = _sigmoid(ga_ref[...].astype(F32))
        sb = _sigmoid(gb_ref[...].astype(F32))
        m_ref[...] = (sa * ym_ref[...] + sb * yf_ref[...]).astype(BF16)

    return pl.pallas_call(
        body, name="merge_fwd", grid=(lp // BLK,),
        in_specs=[_rowc(D_MODEL, 2), _rowc(D_MODEL, 3), _row(D_MODEL), _row(D_MODEL)],
        out_specs=_row(D_MODEL), out_shape=jax.ShapeDtypeStruct((lp, D_MODEL), BF16),
        compiler_params=_cp(("parallel",)))(gate, gate, y_mla, y_fox)


def _tail(x2, mixed, tgt, gpost):
    lp = mixed.shape[0]
    shift = _shift_rows(D_MODEL)

    def body(h_ref, mx_ref, t_ref, g_ref, dmx_ref, dy_ref, loss_ref, dg_ref):
        i = pl.program_id(0)

        @pl.when(i == 0)
        def _():
            loss_ref[...] = jnp.zeros_like(loss_ref)
            dg_ref[...] = jnp.zeros_like(dg_ref)
            dmx_ref[...] = jnp.zeros_like(dmx_ref)
            dy_ref[...] = jnp.zeros_like(dy_ref)

        @pl.when(i > 0)
        def _():
            mx = mx_ref[...]
            g = g_ref[...]
            r = lax.rsqrt(jnp.mean(mx * mx, axis=-1, keepdims=True) + RMS_EPS)
            nrm = mx * r
            e = (h_ref[...] + nrm * g) - t_ref[...]
            loss_ref[...] += jnp.sum(0.5 * jnp.sum(e * e, axis=-1, keepdims=True) * (1.0 / D_MODEL),
                                     axis=0, keepdims=True)
            dy = e * (1.0 / D_MODEL)
            dy_ref[...] = dy
            dg_ref[...] += jnp.sum(dy * nrm, axis=0, keepdims=True)
            w = dy * g
            dot = jnp.mean(w * mx, axis=-1, keepdims=True)
            dmx_ref[...] = (r * w - mx * (r * r * r * dot)).astype(BF16)

    return pl.pallas_call(
        body, name="tail", grid=(lp // BLK,),
        in_specs=[shift, _row(D_MODEL), shift, _full((1, D_MODEL))],
        out_specs=[_row(D_MODEL), _row(D_MODEL), _full((1, 1)), _full((1, D_MODEL))],
        out_shape=[jax.ShapeDtypeStruct((lp, D_MODEL), BF16), jax.ShapeDtypeStruct((lp, D_MODEL), F32),
                   jax.ShapeDtypeStruct((1, 1), F32), jax.ShapeDtypeStruct((1, D_MODEL), F32)],
        compiler_params=_cp(("arbitrary",)))(x2, mixed, tgt, gpost)


def _merge_bwd(dm, gate, y_mla, y_fox):
    lp = dm.shape[0]

    def body(dm_ref, ga_ref, gb_ref, ym_ref, yf_ref, dym_ref, dyf_ref, dg_ref):
        dm_v = dm_ref[...].astype(F32)
        sa = _sigmoid(ga_ref[...].astype(F32))
        sb = _sigmoid(gb_ref[...].astype(F32))
        dym_ref[...] = (dm_v * sa).astype(BF16)
        dyf_ref[...] = (dm_v * sb).astype(BF16)
        dg_ref[:, 0:D_MODEL] = (dm_v * ym_ref[...] * (sa * (1.0 - sa))).astype(BF16)
        dg_ref[:, D_MODEL:2 * D_MODEL] = (dm_v * yf_ref[...] * (sb * (1.0 - sb))).astype(BF16)

    return pl.pallas_call(
        body, name="merge_bwd", grid=(lp // BLK,),
        in_specs=[_row(D_MODEL), _rowc(D_MODEL, 2), _rowc(D_MODEL, 3), _row(D_MODEL), _row(D_MODEL)],
        out_specs=[_row(D_MODEL), _row(D_MODEL), _row(2 * D_MODEL)],
        out_shape=[jax.ShapeDtypeStruct((lp, D_MODEL), BF16), jax.ShapeDtypeStruct((lp, D_MODEL), BF16),
                   jax.ShapeDtypeStruct((lp, 2 * D_MODEL), BF16)],
        compiler_params=_cp(("parallel",)))(dm, gate, gate, y_mla, y_fox)


def _gate_bwd(da_mla, da_fox, o_mla, o_fox, gate):
    lp = da_mla.shape[0]

    def one(da, o, z, head_of_col):
        sg = _sigmoid(z)
        do = (da * (z * sg)).astype(BF16)
        dz = da * o * (sg * (1.0 + z * (1.0 - sg)))
        delta = sum(_dot(part, head_of_col, 1, 0) for part in _split3(do.astype(F32) * o))
        return do, dz.astype(BF16), delta

    def body(dam_ref, daf_ref, om_ref, of_ref, zm_ref, zf_ref, dom_ref, dof_ref, dz_ref, dlm_ref, dlf_ref):
        f32 = lambda r: r[...].astype(F32)
        head_of_col = (lax.broadcasted_iota(jnp.int32, (D_MODEL, LANES), 0) // HEAD_DIM
                       == lax.broadcasted_iota(jnp.int32, (D_MODEL, LANES), 1)).astype(BF16)
        dom_ref[...], dz_ref[:, 0:D_MODEL], dlm_ref[...] = one(f32(dam_ref), f32(om_ref), f32(zm_ref), head_of_col)
        dof_ref[...], dz_ref[:, D_MODEL:2 * D_MODEL], dlf_ref[...] = one(f32(daf_ref), f32(of_ref), f32(zf_ref),
                                                                        head_of_col)

    return pl.pallas_call(
        body, name="gate_bwd", grid=(lp // BLK,),
        in_specs=[_row(D_MODEL)] * 4 + [_rowc(D_MODEL, 0), _rowc(D_MODEL, 1)],
        out_specs=[_row(D_MODEL), _row(D_MODEL), _row(2 * D_MODEL), _row(LANES), _row(LANES)],
        out_shape=[jax.ShapeDtypeStruct((lp, D_MODEL), BF16), jax.ShapeDtypeStruct((lp, D_MODEL), BF16),
                   jax.ShapeDtypeStruct((lp, 2 * D_MODEL), BF16), jax.ShapeDtypeStruct((lp, LANES), F32),
                   jax.ShapeDtypeStruct((lp, LANES), F32)],
        compiler_params=_cp(("parallel",)))(da_mla, da_fox, o_mla, o_fox, gate, gate)


def _small_bwd(small, dqn, dkvn, dkr, dcol_t, drow_t, gq, gkv, fb, ctab, stab, triu):
    lp = small.shape[0]
    nb = lp // BLK

    def rrow(w):
        return pl.BlockSpec((BLK, w), lambda i: (nb - 1 - i, 0))

    def body(sm_ref, dqn_ref, dkvn_ref, dkr_ref, dcol_ref, drow_ref, gq_ref, gkv_ref, fb_ref, c_ref, s_ref, tri_ref,
             ds_ref, dgq_ref, dgkv_ref, dfb_ref, carry):
        i = pl.program_id(0)

        @pl.when(i == 0)
        def _():
            carry[...] = jnp.zeros_like(carry)
            dgq_ref[...] = jnp.zeros_like(dgq_ref)
            dgkv_ref[...] = jnp.zeros_like(dgkv_ref)
            dfb_ref[...] = jnp.zeros_like(dfb_ref)

        def norm_bwd(x, dn, g, dg_ref):
            r = lax.rsqrt(jnp.mean(x * x, axis=-1, keepdims=True) + RMS_EPS)
            dg_ref[...] += jnp.sum(dn * (x * r), axis=0, keepdims=True)
            w = dn * g
            dot = jnp.mean(w * x, axis=-1, keepdims=True)
            return r * w - x * (r * r * r * dot)

        ds_ref[:, 0:256] = norm_bwd(sm_ref[:, 0:256], dqn_ref[...], gq_ref[...], dgq_ref).astype(BF16)
        ds_ref[:, 256:384] = norm_bwd(sm_ref[:, 256:384], dkvn_ref[...], gkv_ref[...], dgkv_ref).astype(BF16)

        dk = dkr_ref[0]
        for p in range(1, PAIRS):
            dk = dk + dkr_ref[p]
        dk = _rope(dk, c_ref[...], -s_ref[...])
        lane = lax.broadcasted_iota(jnp.int32, dk.shape, 1)
        dk = jnp.where(lane < MLA_ROPE, dk + pltpu.roll(dk, LANES - MLA_ROPE, 1), 0.0)
        ds_ref[:, 384:512] = dk.astype(BF16)

        dcol = dcol_ref[0]
        for p in range(1, PAIRS):
            dcol = dcol + pltpu.roll(dcol_ref[p], 2 * p, 1)
        rows16 = jnp.concatenate([drow_ref[p, h:h + 1, :] for p in range(PAIRS) for h in range(2)], axis=0)
        eye = (lax.broadcasted_iota(jnp.int32, (HEADS, LANES), 0)
               == lax.broadcasted_iota(jnp.int32, (HEADS, LANES), 1)).astype(BF16)
        drow = sum(_dot(part, eye, 0, 0) for part in _split3(rows16))
        dcr = dcol - drow
        hi, mid, lo = _split3(dcr)
        t = tri_ref[...]
        suf = (_dot(t, hi, 1, 0) + _dot(t, mid, 1, 0)) + _dot(t, lo, 1, 0) + carry[...]
        fl = sm_ref[:, 512:640] + fb_ref[...]
        dfl = jnp.where(_row_valid(nb - 1 - i), -suf * _sigmoid(-fl), 0.0)
        ds_ref[:, 512:640] = dfl.astype(BF16)
        dfb_ref[...] += jnp.sum(dfl, axis=0, keepdims=True)
        carry[...] += jnp.sum(dcr, axis=0, keepdims=True)

    return pl.pallas_call(
        body, name="small_bwd", grid=(nb,),
        in_specs=[rrow(SMALL_W), rrow(256), rrow(128),
                  pl.BlockSpec((PAIRS, BLK, 128), lambda i: (0, nb - 1 - i, 0)),
                  pl.BlockSpec((PAIRS, BLK, 128), lambda i: (0, nb - 1 - i, 0)),
                  pl.BlockSpec((PAIRS, 2, BLK), lambda i: (0, 0, nb - 1 - i)),
                  _full((1, 256)), _full((1, 128)), _full((1, 128)), rrow(128), rrow(128), _full((BLK, BLK))],
        out_specs=[rrow(SMALL_W), _full((1, 256)), _full((1, 128)), _full((1, 128))],
        out_shape=[jax.ShapeDtypeStruct((lp, SMALL_W), BF16), jax.ShapeDtypeStruct((1, 256), F32),
                   jax.ShapeDtypeStruct((1, 128), F32), jax.ShapeDtypeStruct((1, 128), F32)],
        scratch_shapes=[pltpu.VMEM((1, 128), F32)],
        compiler_params=_cp(("arbitrary",)))(small, dqn, dkvn, dkr, dcol_t, drow_t, gq, gkv, fb, ctab, stab, triu)


def _pre_bwd(du, x2, meta, dy, gpre):
    s_rows = x2.shape[0]
    lp = PAD + s_rows
    shift = _shift_rows(D_MODEL)

    def body(du_ref, x_ref, meta_ref, dy_ref, g_ref, dx_ref, dmeta_ref, dg_ref):
        i = pl.program_id(0)

        @pl.when(i == 0)
        def _():
            dg_ref[...] = jnp.zeros_like(dg_ref)

        hv = _h_block(i, x_ref, meta_ref)
        duv = du_ref[...]
        r = lax.rsqrt(jnp.mean(hv * hv, axis=-1, keepdims=True) + RMS_EPS)
        dg_ref[...] += jnp.sum(duv * (hv * r), axis=0, keepdims=True)
        w = duv * g_ref[...]
        dot = jnp.mean(w * hv, axis=-1, keepdims=True)
        dh = dy_ref[...] + (r * w - hv * (r * r * r * dot))
        dx_ref[...] = dh

        @pl.when(i == 0)
        def _():
            dmeta_ref[...] = dh[0:N_META, :]

    return pl.pallas_call(
        body, name="pre_bwd", grid=(lp // BLK,),
        in_specs=[_row(D_MODEL), shift, _full((N_META, D_MODEL)), _row(D_MODEL), _full((1, D_MODEL))],
        out_specs=[shift, _full((N_META, D_MODEL)), _full((1, D_MODEL))],
        out_shape=[jax.ShapeDtypeStruct((s_rows, D_MODEL), F32), jax.ShapeDtypeStruct((N_META, D_MODEL), F32),
                   jax.ShapeDtypeStruct((1, D_MODEL), F32)],
        compiler_params=_cp(("arbitrary",)))(du, x2, meta, dy, gpre)


def _pair_masks(rope):
    lane = lax.broadcasted_iota(jnp.int32, (1, LANES), 1)
    mas = [lane < HEAD_DIM, lane >= HEAD_DIM]
    if not rope:
        return mas, mas
    wide = lax.broadcasted_iota(jnp.int32, (1, 2 * LANES), 1)
    rope_lo = LANES + MLA_ROPE
    return mas, [(wide < HEAD_DIM) | ((wide >= LANES) & (wide < rope_lo)),
                 ((wide >= HEAD_DIM) & (wide < LANES)) | ((wide >= rope_lo) & (wide < rope_lo + MLA_ROPE))]


def _mask2(x, masks):
    return [jnp.where(m, x, jnp.zeros_like(x)) for m in masks]


def _attn_fwd(q, k, v, *, kr=None, nbrep=None, scale, qcol, kcol, vcol, name):
    lp = q.shape[0]
    nq = 1 + (lp - PAD) // QB
    rope = kr is not None
    bias = nbrep is not None
    qw = 256 if rope else 128

    def body(*refs):
        it = iter(refs)
        q_ref, k_ref, v_ref = next(it), next(it), next(it)
        kr_ref = next(it) if rope else None
        nb_ref = next(it) if bias else None
        o_ref, lse_ref = next(it), next(it)
        i = pl.program_id(1)
        r0 = pl.multiple_of(jnp.where(i == 0, 0, PAD + QB * (i - 1)), BLK)
        b0 = r0 // BLK
        mas, hmask = _pair_masks(rope)
        qh = _mask2(q_ref[pl.ds(r0, QB), :], hmask)
        if bias:
            qh = [x * scale for x in qh]

        def causal(kc, n):
            key = kc * BLK + lax.broadcasted_iota(jnp.int32, (n, QB), 0)
            return (key <= r0 + lax.broadcasted_iota(jnp.int32, (n, QB), 1)) & ((kc > 0) | (n == N_META))

        def update(kcs, carry, masks, n=BLK, q_lo=0, wq=QB):
            stats, acc = carry[:4], carry[4]
            qs = [x[q_lo:q_lo + wq] for x in qh]
            k0s = [pl.multiple_of(kc * BLK, BLK) for kc in kcs]
            kks = [k_ref[pl.ds(k0, n), :] for k0 in k0s]
            if rope:
                kks = [jnp.concatenate([kk, kr_ref[pl.ds(k0, n), :]], axis=1) for kk, k0 in zip(kks, k0s)]
            new_stats, alphas, ps = [], [], [[] for _ in kcs]
            for h in range(2):
                m_prev, l_prev = stats[2 * h], stats[2 * h + 1]
                ss = []
                for kk, k0, mask in zip(kks, k0s, masks):
                    s = _dot(kk, qs[h], 1, 1)
                    if rope:
                        s = s * scale
                    if bias:
                        nbc = nb_ref[h, pl.ds(k0, n), :]
                        s = s + jnp.concatenate([nbc] * (wq // LANES), axis=1)
                    if mask is not None:
                        s = jnp.where(mask, s, NEG)
                    ss.append(s)
                m_new = m_prev
                for s in ss:
                    m_new = jnp.maximum(m_new, jnp.max(s, axis=0, keepdims=True))
                alpha = jnp.exp(m_prev - m_new)
                l_new = alpha * l_prev
                for j, s in enumerate(ss):
                    p = jnp.exp(s - m_new)
                    l_new = l_new + jnp.sum(p, axis=0, keepdims=True)
                    ps[j].append(p.astype(BF16))
                new_stats += [m_new, l_new]
                alphas.append(alpha)
            vcat = jnp.concatenate([x for k0 in k0s for x in _mask2(v_ref[pl.ds(k0, n), :], mas)], axis=0)
            pv = _dot(vcat, jnp.concatenate([p for pj in ps for p in pj], axis=0), 0, 0)
            a_full = jnp.concatenate([jnp.broadcast_to(a, (HEAD_DIM, wq)) for a in alphas], axis=0)
            return (*new_stats, a_full * acc + pv)

        neg = jnp.full((1, QB), NEG, F32)
        zero = jnp.zeros((1, QB), F32)
        c = update([0], (neg, zero, neg, zero, jnp.zeros((LANES, QB), F32)), [causal(0, N_META)], N_META)
        n_mid = jnp.maximum(b0 - 1, 0)
        c = lax.fori_loop(0, n_mid // 4, lambda t, cr: update([4 * t + u for u in (1, 2, 3, 4)], cr, [None] * 4), c)
        c = lax.fori_loop(0, (n_mid % 4) // 2, lambda t, cr: update([n_mid - 1, n_mid], cr, [None, None]), c)
        tri = ((lax.broadcasted_iota(jnp.int32, (BLK, BLK), 0) <= lax.broadcasted_iota(jnp.int32, (BLK, BLK), 1))
               & (b0 > 0))
        lo = update([b0], tuple(a[:, 0:BLK] for a in c), [tri], q_lo=0, wq=BLK)
        hi = update([b0, b0 + 1], tuple(a[:, BLK:QB] for a in c), [None, tri], q_lo=BLK, wq=QB - BLK)
        c = tuple(jnp.concatenate([a, b], axis=1) for a, b in zip(lo, hi))
        inv =jnp.concatenate([jnp.broadcast_to(1.0 / c[1], (HEAD_DIM, QB)),
                               jnp.broadcast_to(1.0 / c[3], (HEAD_DIM, QB))], axis=0)
        o_t = (c[4] * inv).T.astype(BF16)
        lses = [c[0] + jnp.log(c[1]), c[2] + jnp.log(c[3])]
        o_ref[pl.ds(r0, BLK), :] = o_t[0:BLK]
        for h in range(2):
            lse_ref[0, h:h + 1, pl.ds(r0, BLK)] = lses[h][:, 0:BLK]

        @pl.when(i > 0)
        def _():
            r1 = pl.multiple_of(r0 + BLK, BLK)
            o_ref[pl.ds(r1, QB - BLK), :] = o_t[BLK:QB]
            for h in range(2):
                lse_ref[0, h:h + 1, pl.ds(r1, QB - BLK)] = lses[h][:, BLK:QB]

    in_specs = [pl.BlockSpec((lp, qw), lambda p, i: (0, qcol + p)),
                pl.BlockSpec((lp, 128), lambda p, i: (0, kcol(p))),
                pl.BlockSpec((lp, 128), lambda p, i: (0, vcol(p)))]
    ins = [q, k, v]
    if rope:
        in_specs.append(pl.BlockSpec((lp, 128), lambda p, i: (0, 0)))
        ins.append(kr)
    if bias:
        in_specs.append(pl.BlockSpec((2, lp, 128), lambda p, i: (p, 0, 0)))
        ins.append(nbrep)
    return pl.pallas_call(
        body, name=name, grid=(PAIRS, nq), in_specs=in_specs,
        out_specs=[pl.BlockSpec((lp, 128), lambda p, i: (0, p)),
                   pl.BlockSpec((1, 2, lp), lambda p, i: (p, 0, 0))],
        out_shape=[jax.ShapeDtypeStruct((lp, D_MODEL), BF16), jax.ShapeDtypeStruct((PAIRS, 2, lp), F32)],
        compiler_params=_cp(("parallel", "arbitrary"), VMEM_BIG))(*ins)


def _attn_bwd(q, k, v, do, delta, lse, *, kr=None, rtabs=None, nbrep=None, scale, qcol, kcol, vcol, name):
    lp = q.shape[0]
    nb = lp // BLK
    rope = kr is not None
    bias = nbrep is not None
    qw = 256 if rope else 128

    def body(*refs):
        it = iter(refs)
        q_ref, k_ref, v_ref = next(it), next(it), next(it)
        kr_ref = next(it) if rope else None
        nb_ref = next(it) if bias else None
        do_ref, dl_ref, lse_ref = next(it), next(it), next(it)
        ct_ref, st_ref = (next(it), next(it)) if rope else (None, None)
        dq_out, dk_ref, dv_ref = next(it), next(it), next(it)
        x_ref = next(it)
        drow_ref = next(it) if bias else None
        dq_ref = next(it)
        kb = pl.program_id(1)
        mas, hmask = _pair_masks(rope)
        lane = lax.broadcasted_iota(jnp.int32, (1, LANES), 1)

        @pl.when(kb == 0)
        def _():
            dq_ref[...] = jnp.zeros_like(dq_ref)
            if bias:
                drow_ref[...] = jnp.zeros_like(drow_ref)

        def key_pass(n, w):
            kk = k_ref[0:n, :]
            if rope:
                kk = jnp.concatenate([kk, kr_ref[0:n, :]], axis=1)
            vh = _mask2(v_ref[0:n, :], mas)
            kcat = jnp.concatenate(_mask2(kk, hmask), axis=0)
            if bias:
                kcat = kcat * scale
            diag_mask = (lax.broadcasted_iota(jnp.int32, (n, w), 0) <= lax.broadcasted_iota(jnp.int32, (n, w), 1))

            def chunk(qc, carry, mask):
                carry = list(carry)
                q0 = qc * w if isinstance(qc, int) else pl.multiple_of(qc * w, w)
                dov = do_ref[pl.ds(q0, w), :]
                doh = _mask2(dov, mas)
                qh = _mask2(q_ref[pl.ds(q0, w), :], hmask)
                if bias:
                    qh = [x * scale for x in qh]
                pbs, dss = [], []
                for h in range(2):
                    s = _dot(kk, qh[h], 1, 1)
                    if rope:
                        s = s * scale
                    if bias:
                        s = s + jnp.concatenate([nb_ref[h, 0:n, :]] * (w // LANES), axis=1)
                    p = jnp.exp(s - lse_ref[0, h:h + 1, pl.ds(q0, w)])
                    if mask is not None:
                        p = jnp.where(mask, p, 0.0)
                    ds = p * (_dot(vh[h], dov, 1, 1) - dl_ref[0, h:h + 1, pl.ds(q0, w)])
                    if bias:
                        drow_ref[0, h:h + 1, pl.ds(q0, w)] += jnp.sum(ds, axis=0, keepdims=True)
                        carry[2 + h] = carry[2 + h] + jnp.sum(ds, axis=1, keepdims=True)
                    else:
                        ds = ds * scale
                    pbs.append(p.astype(BF16))
                    dss.append(ds.astype(BF16))
                ds_lanes = jnp.concatenate(dss, axis=1)
                ds_rows = jnp.concatenate(dss, axis=0)
                carry[0] = carry[0] + _dot(ds_lanes, jnp.concatenate(qh, axis=0), 1, 0)
                carry[1] = carry[1] + _dot(jnp.concatenate(pbs, axis=1), jnp.concatenate(doh, axis=0), 1, 0)
                dq_ref[pl.ds(q0, w), :] += _dot(ds_rows, kcat, 0, 0)
                return tuple(carry)

            c = [jnp.zeros((n, qw), F32), jnp.zeros((n, LANES), F32)]
            if bias:
                c += [jnp.zeros((n, 1), F32), jnp.zeros((n, 1), F32)]
            c = tuple(c)
            if w != BLK:
                for qc in range(lp // w):
                    c = chunk(qc, c, diag_mask if qc == 0 else None)
            else:
                groups = (nb - kb) // UNROLL

                def several(t, cr):
                    for u in range(UNROLL):
                        cr = chunk(kb + UNROLL * t + u, cr, (diag_mask | (t > 0)) if u == 0 else None)
                    return cr

                c = lax.fori_loop(0, groups, several, c)
                start = kb + UNROLL * groups
                pairs = (nb - start) // 2

                def two(t, cr):
                    qc = start + 2 * t
                    return chunk(qc + 1, chunk(qc, cr, diag_mask | (qc > kb)), None)

                c = lax.fori_loop(0, pairs, two, c)
                c = lax.fori_loop(start + 2 * pairs, nb, lambda qc, cr: chunk(qc, cr, diag_mask | (qc > kb)), c)

            def rows(a, dtype):
                a = a.astype(dtype)
                return a if n == BLK else jnp.concatenate([a, jnp.zeros((BLK - n, a.shape[1]), dtype)], axis=0)

            dk_ref[...] = rows(c[0][:, 0:LANES], BF16)
            dv_ref[...] = rows(c[1], BF16)
            if rope:
                x_ref[0] = rows(c[0][:, LANES:2 * LANES], F32)
            if bias:
                x_ref[0] = rows(jnp.where(lane == 0, c[2], jnp.where(lane == 1, c[3], 0.0)), F32)

        @pl.when(kb == 0)
        def _():
            key_pass(N_META, lp // 2)

        @pl.when(kb > 0)
        def _():
            key_pass(BLK, BLK)

        @pl.when(kb == nb - 1)
        def _():
            def fin(c, carry):
                r0 = pl.multiple_of(c * BLK, BLK)
                dq = dq_ref[pl.ds(r0, BLK), :]
                if rope:
                    back = _rope(dq[:, LANES:2 * LANES], ct_ref[pl.ds(r0, BLK), :], -st_ref[pl.ds(r0, BLK), :])
                    dq = jnp.concatenate([dq[:, 0:LANES], back], axis=1)
                dq_out[pl.ds(r0, BLK), :] = dq.astype(BF16)
                return carry

            lax.fori_loop(0, nb, fin, 0)

    in_specs = [pl.BlockSpec((lp, qw), lambda p, j: (0, qcol + p)),
                pl.BlockSpec((BLK, 128), lambda p, j: (j, kcol(p))),
                pl.BlockSpec((BLK, 128), lambda p, j: (j, vcol(p)))]
    ins = [q, k, v]
    if rope:
        in_specs.append(pl.BlockSpec((BLK, 128), lambda p, j: (j, 0)))
        ins.append(kr)
    if bias:
        in_specs.append(pl.BlockSpec((2, BLK, 128), lambda p, j: (p, j, 0)))
        ins.append(nbrep)
    in_specs += [pl.BlockSpec((lp, 128), lambda p, j: (0, p)), pl.BlockSpec((1, 2, lp), lambda p, j: (p, 0, 0)),
                 pl.BlockSpec((1, 2, lp), lambda p, j: (p, 0, 0))]
    ins += [do, delta, lse]
    if rope:
        in_specs += [pl.BlockSpec((lp, 128), lambda p, j: (0, 0))] * 2
        ins += list(rtabs)
    out_specs = [pl.BlockSpec((lp, qw), lambda p, j: (0, p)),
                 pl.BlockSpec((BLK, 128), lambda p, j: (j, p)),
                 pl.BlockSpec((BLK, 128), lambda p, j: (j, p)),
                 pl.BlockSpec((1, BLK, 128), lambda p, j: (p, j, 0))]
    out_shape = [jax.ShapeDtypeStruct((lp, PAIRS * qw), BF16), jax.ShapeDtypeStruct((lp, D_MODEL), BF16),
                 jax.ShapeDtypeStruct((lp, D_MODEL), BF16), jax.ShapeDtypeStruct((PAIRS, lp, 128), F32)]
    if bias:
        out_specs.append(pl.BlockSpec((1, 2, lp), lambda p, j: (p, 0, 0)))
        out_shape.append(jax.ShapeDtypeStruct((PAIRS, 2, lp), F32))
    return pl.pallas_call(
        body, name=name, grid=(PAIRS, nb), in_specs=in_specs, out_specs=out_specs, out_shape=out_shape,
        scratch_shapes=[pltpu.VMEM((lp, qw), F32)],
        compiler_params=_cp(("parallel", "arbitrary"), VMEM_BIG))(*ins)


def _adamw(w, g, m, v, name):
    lead = w.ndim - 2
    rows, cols = w.shape[lead:]
    big = rows * cols > 512 * 1024
    tr = 128 if big and rows % 128 == 0 else rows
    tc = 256 if big and tr == rows else cols

    def body(w_ref, g_ref, m_ref, v_ref, d_ref, nm_ref, nv_ref):
        gv = g_ref[...]
        nm = ADAM_B1 * m_ref[...] + (1.0 - ADAM_B1) * gv
        nv = ADAM_B2 * v_ref[...] + (1.0 - ADAM_B2) * (gv * gv)
        m_hat = nm / (1.0 - ADAM_B1 ** ADAM_STEP)
        v_hat = nv / (1.0 - ADAM_B2 ** ADAM_STEP)
        d_ref[...] = -ADAM_LR * (m_hat / (jnp.sqrt(v_hat) + ADAM_EPS) + ADAM_WD * w_ref[...])
        nm_ref[...] = nm
        nv_ref[...] = nv

    spec = pl.BlockSpec((1,) * lead + (tr, tc), lambda i, j: (0,) * lead + (i, j))
    return pl.pallas_call(
        body, name=name, grid=(rows // tr, cols // tc), in_specs=[spec] * 4, out_specs=[spec] * 3,
        out_shape=[jax.ShapeDtypeStruct(w.shape, F32)] * 3,
        compiler_params=_cp(("parallel", "parallel"), VMEM_BIG))(w, g, m, v)


def _add_cores(g, from_sib, name):
    n, rows, cols = g.shape
    half = rows // 2
    tr = _tile(half, (256, 240))
    nt = half // tr

    def body(lo_ref, hi_ref, s_ref, o_ref):
        mine = jnp.where(lax.axis_index("c") == 0, lo_ref[0], hi_ref[0])
        o_ref[0] = (mine + s_ref[0]).astype(BF16)

    return pl.pallas_call(
        body, name=name, grid=(n, nt),
        in_specs=[pl.BlockSpec((1, tr, cols), lambda j, i: (j, i, 0)),
                  pl.BlockSpec((1, tr, cols), lambda j, i: (j, nt + i, 0)),
                  pl.BlockSpec((1, tr, cols), lambda j, i: (j, i, 0))],
        out_specs=pl.BlockSpec((1, tr, cols), lambda j, i: (j, i, 0)),
        out_shape=jax.ShapeDtypeStruct((n, half, cols), BF16),
        compiler_params=_cp(("parallel", "parallel"), VMEM_BIG))(g, g, from_sib)


def _add_chips(x, own, name):
    n, rows, cols = x.shape
    tr = _tile(rows, (256, 240))

    def body(x_ref, own_ref, o_ref):
        me = 2 * lax.axis_index("x") + lax.axis_index("y")
        v = [jnp.where(me == k, own_ref[...], x_ref[k]).astype(F32) for k in range(N_CHIPS)]
        o_ref[...] = ((v[0] + v[1]) + v[2]) + v[3]

    return pl.pallas_call(
        body, name=name, grid=(rows // tr,),
        in_specs=[pl.BlockSpec((n, tr, cols), lambda i: (0, i, 0)), pl.BlockSpec((tr, cols), lambda i: (i, 0))],
        out_specs=pl.BlockSpec((tr, cols), lambda i: (i, 0)),
        out_shape=jax.ShapeDtypeStruct((rows, cols), F32), compiler_params=_cp(("parallel",), VMEM_BIG))(x, own)


def _axes():
    return lax.axis_index("x"), lax.axis_index("y"), lax.axis_index("c")


def _other_chips(x, y):
    return [(1 - x, y), (x, 1 - y), (1 - x, 1 - y)]


ANY = pl.BlockSpec(memory_space=pl.ANY)


def _rcopy(src, dst, send_sems, recv_sems, k, to):
    return pltpu.make_async_remote_copy(src_ref=src, dst_ref=dst, send_sem=send_sems.at[k], recv_sem=recv_sems.at[k],
                                        device_id=to, device_id_type=MESH)


def _gather_weights(shards, meta):
    n = len(shards)

    def body(*refs):
        srcs, meta_ref = refs[:n], refs[n]
        outs, mout_ref = refs[n + 1:2 * n + 1], refs[2 * n + 1]
        send_sems, recv_sems = refs[2 * n + 2:]
        x, y, c = _axes()
        me = 2 * x + y
        sib = (x, y, 1 - c)
        chips = _other_chips(x, y)

        def half(t, chip_idx, cc):
            hr = shards[t].shape[0] // 2
            return outs[t].at[chip_idx, pl.ds(cc * hr, hr), :]

        first = []
        for j, (px, py) in enumerate(chips):
            for t in range(n):
                hr = shards[t].shape[0] // 2
                first.append(_rcopy(srcs[t].at[pl.ds(c * hr, hr), :], half(t, me, c), send_sems, recv_sems,
                                    3 * t + j, (px, py, c)))
            first.append(_rcopy(meta_ref, mout_ref.at[me], send_sems, recv_sems, 3 * n + j, (px, py, c)))
        for cp in first:
            cp.start()
        passed = []
        for j, (px, py) in enumerate(chips):
            src_chip = 2 * px + py
            for t in range(n):
                _rcopy(half(t, src_chip, c), half(t, src_chip, c), send_sems, recv_sems, 3 * t + j, sib).wait_recv()
                fwd = _rcopy(half(t, src_chip, c), half(t, src_chip, c), send_sems, recv_sems, 3 * (n + 1 + t) + j, sib)
                fwd.start()
                passed.append(fwd)
            _rcopy(mout_ref.at[src_chip], mout_ref.at[src_chip], send_sems, recv_sems, 3 * n + j, sib).wait_recv()
        for j, (px, py) in enumerate(chips):
            src_chip = 2 * px + py
            for t in range(n):
                _rcopy(half(t, src_chip, 1 - c), half(t, src_chip, 1 - c), send_sems, recv_sems,
                       3 * (n + 1 + t) + j, sib).wait_recv()
        for cp in first + passed:
            cp.wait_send()

    nsem = 3 * (2 * n + 1)
    return pl.pallas_call(
        body, name="gather_weights", in_specs=[ANY] * (n + 1), out_specs=[ANY] * (n + 1),
        out_shape=[jax.ShapeDtypeStruct((N_CHIPS,) + s.shape, s.dtype) for s in shards]
        + [jax.ShapeDtypeStruct((N_CHIPS,) + meta.shape, meta.dtype)],
        scratch_shapes=[pltpu.SemaphoreType.DMA((nsem,)), pltpu.SemaphoreType.DMA((nsem,))])(*shards, meta)


def _gather_late(shard):
    rows, cols = shard.shape
    hr = rows // 2
    src = jax.new_ref(shard, memory_space=pltpu.MemorySpace.HBM)
    out = jax.empty_ref(jax.ShapeDtypeStruct((N_CHIPS, rows, cols), shard.dtype), memory_space=pltpu.MemorySpace.HBM)

    @pl.kernel(mesh=plsc.ScalarSubcoreMesh(axis_name="seq", num_cores=1), name="gather_late",
               scratch_types=(pltpu.SemaphoreType.DMA((6,)), pltpu.SemaphoreType.DMA((6,))),
               compiler_params=pltpu.CompilerParams(collective_id=1))
    def launch(send_sems, recv_sems):
        x, y, c = _axes()
        me = 2 * x + y
        sib = (x, y, 1 - c)
        chips = _other_chips(x, y)
        barrier = pltpu.get_barrier_semaphore()
        for px, py in chips:
            pl.semaphore_signal(barrier, inc=1, device_id=(px, py, c), device_id_type=MESH)
        pl.semaphore_signal(barrier, inc=1, device_id=sib, device_id_type=MESH)
        pl.semaphore_wait(barrier, 4)

        def half(chip_idx, cc):
            return out.at[chip_idx, pl.ds(cc * hr, hr), :]

        first = [_rcopy(src.at[pl.ds(c * hr, hr), :], half(me, c), send_sems, recv_sems, j, (px, py, c))
                 for j, (px, py) in enumerate(chips)]
        for cp in first:
            cp.start()
        passed = []
        for j, (px, py) in enumerate(chips):
            land = half(2 * px + py, c)
            _rcopy(land, land, send_sems, recv_sems, j, sib).wait_recv()
            fwd = _rcopy(land, land, send_sems, recv_sems, 3 + j, sib)
            fwd.start()
            passed.append(fwd)
        for j, (px, py) in enumerate(chips):
            land = half(2 * px + py, 1 - c)
            _rcopy(land, land, send_sems, recv_sems, 3 + j, sib).wait_recv()
        for cp in first + passed:
            cp.wait_send()

    launch()
    return out[...]


def _swap_halves(gs):
    n = len(gs)
    ncopies = sum(g.shape[0] for g in gs)

    def body(*refs):
        srcs, outs = refs[:n], refs[n:2 * n]
        send_sems, recv_sems = refs[2 * n:]
        x, y, c = _axes()
        cps = []
        for t in range(n):
            hr = gs[t].shape[1] // 2
            for j in range(gs[t].shape[0]):
                cps.append(_rcopy(srcs[t].at[j, pl.ds((1 - c) * hr, hr), :], outs[t].at[j], send_sems, recv_sems,
                                  len(cps), (x, y, 1 - c)))
        for cp in cps:
            cp.start()
        for cp in cps:
            cp.wait()

    return pl.pallas_call(
        body, name="swap_halves", in_specs=[ANY] * n, out_specs=[ANY] * n,
        out_shape=[jax.ShapeDtypeStruct((g.shape[0], g.shape[1] // 2, g.shape[2]), g.dtype) for g in gs],
        scratch_shapes=[pltpu.SemaphoreType.DMA((ncopies,)), pltpu.SemaphoreType.DMA((ncopies,))])(*gs)


def _scatter_chips(parts):
    n = len(parts)
    srcs = [jax.new_ref(p, memory_space=pltpu.MemorySpace.HBM) for p in parts]
    outs = [jax.empty_ref(jax.ShapeDtypeStruct(p.shape, p.dtype), memory_space=pltpu.MemorySpace.HBM) for p in parts]

    @pl.kernel(mesh=plsc.ScalarSubcoreMesh(axis_name="seq", num_cores=1), name="scatter_chips",
               scratch_types=(pltpu.SemaphoreType.DMA((3 * n,)), pltpu.SemaphoreType.DMA((3 * n,))),
               compiler_params=pltpu.CompilerParams(collective_id=0))
    def launch(send_sems, recv_sems):
        x, y, c = _axes()
        me = 2 * x + y
        chips = _other_chips(x, y)
        barrier = pltpu.get_barrier_semaphore()
        for px, py in chips:
            pl.semaphore_signal(barrier, inc=1, device_id=(px, py, c), device_id_type=MESH)
        pl.semaphore_wait(barrier, 3)
        cps = []
        for j, (px, py) in enumerate(chips):
            for t in range(n):
                cps.append(_rcopy(srcs[t].at[2 * px + py], outs[t].at[me], send_sems, recv_sems, 3 * t + j,
                                  (px, py, c)))
        for cp in cps:
            cp.start()
        for cp in cps:
            cp.wait()

    launch()
    return [o[...] for o in outs]


def _swap_reduced(rs):
    n = len(rs)

    def body(*refs):
        srcs, outs = refs[:n], refs[n:2 * n]
        send_sems, recv_sems = refs[2 * n:]
        x, y, c = _axes()
        cps = [_rcopy(srcs[t], outs[t], send_sems, recv_sems, t, (x, y, 1 - c)) for t in range(n)]
        for cp in cps:
            cp.start()
        for cp in cps:
            cp.wait()

    return pl.pallas_call(
        body, name="swap_reduced", in_specs=[ANY] * n, out_specs=[ANY] * n,
        out_shape=[jax.ShapeDtypeStruct(r.shape, r.dtype) for r in rs],
        scratch_shapes=[pltpu.SemaphoreType.DMA((n,)), pltpu.SemaphoreType.DMA((n,))])(*rs)


SMALL_ROWS = 24 + 128


def _allreduce_small(vec):
    def body(v_ref, out_ref, slots, send_sems, recv_sems):
        x, y, c = _axes()
        me = 4 * x + 2 * y + c
        slots[me] = v_ref[...]
        cps = []
        for k in range(1, 8):
            kx, ky, kc = (k >> 2) & 1, (k >> 1) & 1, k & 1
            peer = (1 - x if kx else x, 1 - y if ky else y, 1 - c if kc else c)
            cps.append(_rcopy(v_ref, slots.at[me], send_sems, recv_sems, k - 1, peer))
        for cp in cps:
            cp.start()
        for cp in cps:
            cp.wait()
        tot = slots[0]
        for k in range(1, 8):
            tot = tot + slots[k]
        out_ref[...] = tot

    return pl.pallas_call(
        body, name="allreduce_small",
        in_specs=[pl.BlockSpec(memory_space=pltpu.VMEM)], out_specs=pl.BlockSpec(memory_space=pltpu.VMEM),
        out_shape=jax.ShapeDtypeStruct((SMALL_ROWS, 128), F32),
        scratch_shapes=[pltpu.VMEM((8, SMALL_ROWS, 128), F32), pltpu.SemaphoreType.DMA((7,)),
                        pltpu.SemaphoreType.DMA((7,))])(vec)


def _pack_p2(w_uq, w_ukv, w_br_mla, w_br_fox, w_out, dtype):
    parts = [w_uq.reshape(96, D_MODEL), w_ukv.reshape(64, D_MODEL), w_br_mla, w_br_fox, w_out]
    return jnp.concatenate([p.astype(dtype) for p in parts], axis=0)


def _unpack_p2(pk):
    return pk[0:96].reshape(256, 384), pk[96:160].reshape(128, 512), pk[160:416], pk[416:672], pk[672:928]


def _uq_arrange(w):
    w3 = w.reshape(256, HEADS, 96)
    nope = w3[:, :, :64].reshape(256, PAIRS, 128)
    pe = w3[:, :, 64:].reshape(256, PAIRS, 64)
    return jnp.concatenate([nope, pe, jnp.zeros((256, PAIRS, 64), w.dtype)], axis=2).reshape(256, PAIRS * 256)


def _uq_restore(g):
    g3 = g.reshape(256, PAIRS, 256)
    nope = g3[:, :, :128].reshape(256, HEADS, 64)
    pe = g3[:, :, 128:192].reshape(256, HEADS, 32)
    return jnp.concatenate([nope, pe], axis=2).reshape(256, HEADS * 96)


def _ukv_arrange(w):
    w3 = w.reshape(128, HEADS, 128)
    return jnp.concatenate([w3[:, :, :64].reshape(128, 1024), w3[:, :, 64:].reshape(128, 1024)], axis=1)


def _ukv_restore(g):
    kn = g[:, :1024].reshape(128, HEADS, 64)
    vv = g[:, 1024:].reshape(128, HEADS, 64)
    return jnp.concatenate([kn, vv], axis=2).reshape(128, HEADS * 128)


def _rope_tables(lp):
    r = np.arange(lp)
    pos = np.where(r < N_META, r, np.where(r >= PAD, r - PAD + N_META, 0)).astype(np.float32)
    half = MLA_ROPE // 2
    inv_freq = np.float32(ROPE_THETA) ** (-np.arange(half, dtype=np.float32) / np.float32(half))
    ang = (pos[:, None] * inv_freq[None, :]).astype(np.float32)
    cos, sin = np.cos(ang).astype(np.float32), np.sin(ang).astype(np.float32)
    one, zero = np.ones((lp, 64), np.float32), np.zeros((lp, 64), np.float32)
    return (jnp.asarray(np.concatenate([cos, cos, cos, cos, one], axis=1)),
            jnp.asarray(np.concatenate([-sin, sin, -sin, sin, zero], axis=1)))


def _pad_lanes(v, n=128):
    return jnp.pad(v, ((0, 0), (0, n - v.shape[1])))


def _in_cols(slabs, a, b):
    out = []
    for j in range(N_CHIPS):
        lo, hi = max(a, W_IN_SHARD * j), min(b, W_IN_SHARD * (j + 1))
        if lo < hi:
            out.append(slabs[j][:, lo - W_IN_SHARD * j:hi - W_IN_SHARD * j])
    return out


def _local_step(x2, tgt2, meta_f, w_small, w_attn, w_gate, w_uq_f, w_ukv_f, w_bm, w_bf, w_o, pre_norm_g,
                post_norm_g, mla_q_norm_g, mla_kv_norm_g, fox_forget_b, start_exchange=None):
    s_rows = x2.shape[0]
    lp = PAD + s_rows
    w_uq_a = _uq_arrange(w_uq_f)
    w_ukv_a = _ukv_arrange(w_ukv_f)

    ctab, stab = _rope_tables(lp)
    ii = jnp.arange(BLK)
    tri_lo = (ii[:, None] >= ii[None, :]).astype(BF16)
    tri_up = (ii[:, None] <= ii[None, :]).astype(BF16)
    fb128 = _pad_lanes(fox_forget_b)

    u = _rms_pre(x2, meta_f, pre_norm_g)
    small = _mm(u, w_small, mode="nn", out_dtype=F32, name="proj_small")
    attn = _mm(u, w_attn, mode="nn", out_dtype=BF16, name="proj_attn")
    gate = _mm(u, w_gate, mode="nn", out_dtype=BF16, name="proj_gate")
    qn, kvn, kr, ncum = _small_prep(small, mla_q_norm_g, mla_kv_norm_g, fb128, ctab, stab, tri_lo)
    qcat = _mm(qn, w_uq_a, mode="nn", out_dtype=BF16, name="mla_q", epilogue=_rope_pairs, row_ins=(ctab, stab))
    kv = _mm(kvn, w_ukv_a, mode="nn", out_dtype=BF16, name="mla_kv")
    nbrep = jnp.broadcast_to(ncum[:, :HEADS].T[:, :, None], (HEADS, lp, LANES))

    mla_cols = dict(qcol=0, kcol=lambda p: p, vcol=lambda p: PAIRS + p)
    fox_cols = dict(qcol=0, kcol=lambda p: PAIRS + p, vcol=lambda p: 2 * PAIRS + p)
    o_mla, lse_mla = _attn_fwd(qcat, kv, kv, kr=kr, scale=MLA_SCALE, name="mla_fwd", **mla_cols)
    o_fox, lse_fox = _attn_fwd(attn, attn, attn, nbrep=nbrep, scale=FOX_SCALE, name="fox_fwd", **fox_cols)

    a_mla, a_fox = _gate_fwd(o_mla, o_fox, gate)
    y_mla = _mm(a_mla, w_bm, mode="nn", out_dtype=BF16, name="br_mla")
    y_fox = _mm(a_fox, w_bf, mode="nn", out_dtype=BF16, name="br_fox")
    mg = _merge_fwd(gate, y_mla, y_fox)
    mixed = _mm(mg, w_o, mode="nn", out_dtype=F32, name="out_proj")
    dmixed, dy, loss_p, dg_post = _tail(x2, mixed, tgt2, post_norm_g)

    d_w_out = _mm(mg, dmixed, mode="tn", out_dtype=F32, name="d_w_out")
    dm = _mm(dmixed, w_o, mode="nt", out_dtype=BF16, name="d_merge")
    dy_mla, dy_fox, dgate_ab = _merge_bwd(dm, gate, y_mla, y_fox)
    d_w_bm = _mm(a_mla, dy_mla, mode="tn", out_dtype=F32, name="d_w_br_mla")
    d_w_bf = _mm(a_fox, dy_fox, mode="tn", out_dtype=F32, name="d_w_br_fox")
    da_mla = _mm(dy_mla, w_bm, mode="nt", out_dtype=BF16, name="d_a_mla")
    da_fox = _mm(dy_fox, w_bf, mode="nt", out_dtype=BF16, name="d_a_fox")
    do_mla, do_fox, dgate_z, dl_mla, dl_fox = _gate_bwd(da_mla, da_fox, o_mla, o_fox, gate)
    dl_mla, dl_fox = (d[:, :HEADS].T.reshape(PAIRS, 2, lp) for d in (dl_mla, dl_fox))

    dq_a, dkn, dvm, dkr = _attn_bwd(qcat, kv, kv, do_mla, dl_mla, lse_mla, kr=kr, rtabs=(ctab, stab),
                                    scale=MLA_SCALE, name="mla_bwd", **mla_cols)
    dfq, dfk, dfv, dcol, drow = _attn_bwd(attn, attn, attn, do_fox, dl_fox, lse_fox, nbrep=nbrep, scale=FOX_SCALE,
                                          name="fox_bwd", **fox_cols)

    d_w_uq_a = _mm(qn, dq_a, mode="tn", out_dtype=F32, name="d_w_uq")
    dqn = _mm(dq_a, w_uq_a, mode="nt", out_dtype=F32, name="d_qn")
    d_w_ukv_a = jnp.concatenate([_mm(kvn, dkn, mode="tn", out_dtype=F32, name="d_w_uk"),
                                 _mm(kvn, dvm, mode="tn", out_dtype=F32, name="d_w_uv")], axis=1)
    dkvn = _mm(dkn, w_ukv_a[:, :1024], mode="nt", out_dtype=F32, name="d_kvn_k")
    dkvn = _mm(dvm, w_ukv_a[:, 1024:], mode="nt", out_dtype=F32, name="d_kvn_v", acc=dkvn)
    dsmall, dg_q, dg_kv, dfb = _small_bwd(small, dqn, dkvn, dkr, dcol, drow, mla_q_norm_g, mla_kv_norm_g,
                                          fb128, ctab, stab, tri_up)

    dw_small = _mm(u, dsmall, mode="tn", out_dtype=F32, name="d_w_small")
    dw_fq = _mm(u, dfq, mode="tn", out_dtype=F32, name="d_w_fq")
    dw_fk = _mm(u, dfk, mode="tn", out_dtype=F32, name="d_w_fk")
    dw_fv = _mm(u, dfv, mode="tn", out_dtype=F32, name="d_w_fv")
    dw_z = _mm(u, dgate_z, mode="tn", out_dtype=F32, name="d_w_z")
    dw_g = _mm(u, dgate_ab, mode="tn", out_dtype=F32, name="d_w_g")
    d_w_in = (dw_small, dw_z, dw_fq, dw_fk, dw_fv, dw_g)
    d_w_uq = _uq_restore(d_w_uq_a)
    d_w_ukv = _ukv_restore(d_w_ukv_a)
    token = start_exchange(d_w_in, d_w_uq, d_w_ukv, d_w_bm, d_w_bf, d_w_out) if start_exchange else None
    du = _mm_sum_nt([(dsmall, w_small), (dfq, w_attn[:, 0:1024]), (dfk, w_attn[:, 1024:2048]),
                     (dfv, w_attn[:, 2048:3072]), (dgate_z, w_gate[:, 0:2048]), (dgate_ab, w_gate[:, 2048:4096])],
                    name="d_u", after=token)
    dx, dmeta, dg_pre = _pre_bwd(du, x2, meta_f, dy, pre_norm_g)
    return (loss_p, dx, dmeta, d_w_in, d_w_uq, d_w_ukv, d_w_bm, d_w_bf, d_w_out, dg_pre, dg_post, dg_q, dg_kv, dfb)


def _w_in_slabs(pieces):
    dw_small, dw_z, dw_fq, dw_fk, dw_fv, dw_g = pieces
    runs = [(dw_small[:, 0:416], C_CQ), (dw_z[:, 0:1024], C_ZMLA), (dw_fq, C_FQ), (dw_fk, C_FK), (dw_fv, C_FV),
            (dw_small[:, 512:528], C_FL), (dw_z[:, 1024:2048], C_ZFOX), (dw_g, C_GA)]
    slabs = []
    for j in range(N_CHIPS):
        lo, hi = W_IN_SHARD * j, W_IN_SHARD * (j + 1)
        cols = [a[:, max(lo, c0) - c0:min(hi, c0 + a.shape[1]) - c0] for a, c0 in runs
                if max(lo, c0) < min(hi, c0 + a.shape[1])]
        slabs.append(jnp.concatenate(cols, axis=1))
    return jnp.stack(slabs, axis=0)


def kernel(x, meta_tokens, pre_norm_g, w_in, fox_forget_b, mla_q_norm_g, mla_kv_norm_g, w_uq, w_ukv, w_br_mla, w_br_fox, w_out, post_norm_g, loss_target, m_meta_tokens, m_pre_norm_g, m_w_in, m_fox_forget_b, m_mla_q_norm_g, m_mla_kv_norm_g, m_w_uq, m_w_ukv, m_w_br_mla, m_w_br_fox, m_w_out, m_post_norm_g, v_meta_tokens, v_pre_norm_g, v_w_in, v_fox_forget_b, v_mla_q_norm_g, v_mla_kv_norm_g, v_w_uq, v_w_ukv, v_w_br_mla, v_w_br_fox, v_w_out, v_post_norm_g):
    me = 2 * lax.axis_index("x") + lax.axis_index("y")
    core = lax.axis_index("c")
    w_in_b = w_in.astype(BF16).reshape(D_MODEL, W_IN_SHARD)
    p2 = _pack_p2(w_uq[0], w_ukv[0], w_br_mla[0], w_br_fox[0], w_out[0], BF16)
    w_in_g, meta_g = _gather_weights([w_in_b], meta_tokens)
    p2_g = _gather_late(lax.optimization_barrier((p2, w_in_g))[0])
    slabs = [jnp.where(me == j, w_in_b, w_in_g[j]) for j in range(N_CHIPS)]
    chip = lax.broadcasted_iota(jnp.int32, (N_CHIPS, 1, 1), 0)
    p2_all = jnp.where(chip == me, p2[None], p2_g)
    w_uq_f = p2_all[:, 0:96].reshape(N_CHIPS, 256, 384).transpose(1, 0, 2).reshape(256, 1536)
    w_ukv_f = p2_all[:, 96:160].reshape(N_CHIPS, 128, 512).transpose(1, 0, 2).reshape(128, 2048)
    w_bm, w_bf, w_o = (p2_all[:, lo:lo + 256].reshape(D_MODEL, D_MODEL) for lo in (160, 416, 672))
    meta_f = jnp.where(chip == me, meta_tokens[None], meta_g).transpose(1, 0, 2).reshape(N_META, D_MODEL)
    kpe = _in_cols(slabs, C_KPE, C_ZMLA)
    w_small = jnp.concatenate(_in_cols(slabs, C_CQ, C_KPE) + kpe + kpe + [jnp.zeros((D_MODEL, 64), BF16)]
                              + _in_cols(slabs, C_FL, C_ZFOX) + [jnp.zeros((D_MODEL, 112), BF16)], axis=1)
    w_attn = jnp.concatenate(_in_cols(slabs, C_FQ, C_FL), axis=1)
    w_gate = jnp.concatenate(_in_cols(slabs, C_ZMLA, C_FQ) + _in_cols(slabs, C_ZFOX, C_END), axis=1)

    exchange = {}

    def start_exchange(d_w_in, d_w_uq, d_w_ukv, d_w_bm, d_w_bf, d_w_out):
        g2 = jnp.concatenate(
            [d_w_uq.reshape(256, N_CHIPS, 384).transpose(1, 0, 2).reshape(N_CHIPS, 96, D_MODEL),
             d_w_ukv.reshape(128, N_CHIPS, 512).transpose(1, 0, 2).reshape(N_CHIPS, 64, D_MODEL)]
            + [g.reshape(N_CHIPS, 256, D_MODEL) for g in (d_w_bm, d_w_bf, d_w_out)], axis=1)
        pieces = [p[None] for p in d_w_in]
        from_sib = _swap_halves(pieces + [g2])
        halves = [_add_cores(p, s, "add_cores_" + nm)[0]
                  for p, s, nm in zip(pieces, from_sib, ("small", "z", "fq", "fk", "fv", "g"))]
        parts = [_w_in_slabs(halves), _add_cores(g2, from_sib[-1], "add_cores_rest")]
        exchange.update(parts=parts, landed=_scatter_chips(parts))
        return parts[0][0, 0:16, 0:LANES]

    (loss_p, dx, dmeta, _, _, _, _, _, _, dg_pre, dg_post, dg_q, dg_kv,
     dfb) = _local_step(x[0], loss_target[0], meta_f, w_small, w_attn, w_gate, w_uq_f, w_ukv_f, w_bm, w_bf, w_o,
                        pre_norm_g, post_norm_g, mla_q_norm_g, mla_kv_norm_g, fox_forget_b, start_exchange)

    mine = [_add_chips(l, lax.dynamic_index_in_dim(p, me, 0, keepdims=False), nm)
            for l, p, nm in zip(exchange["landed"], exchange["parts"], ("add_chips_w_in", "add_chips_rest"))]
    theirs = _swap_reduced(mine)
    g_w_in, g_p2 = [jnp.concatenate([jnp.where(core == 0, a, b), jnp.where(core == 0, b, a)], axis=0)
                    for a, b in zip(mine, theirs)]
    g_w_uq, g_w_ukv, g_w_bm, g_w_bf, g_w_out = _unpack_p2(g_p2)
    g_w_in = g_w_in[None]

    vec = jnp.concatenate([dg_pre.reshape(8, 128), dg_post.reshape(8, 128), dg_q.reshape(2, 128), dg_kv,
                           dfb, _pad_lanes(loss_p), jnp.zeros((3, 128), F32), dmeta.reshape(128, 128)], axis=0)
    tot = _allreduce_small(vec)
    loss = tot[20, 0]
    g_meta = lax.dynamic_slice_in_dim(tot[24:].reshape(N_META, D_MODEL), 256 * me, 256, axis=1)

    def small_pack(pre, post, gq_, gkv_, fb_):
        return jnp.concatenate([pre.reshape(8, 128), post.reshape(8, 128), gq_.reshape(2, 128), gkv_,
                                _pad_lanes(fb_), jnp.zeros((4, 128), F32)], axis=0)

    def small_unpack(t):
        return (t[0:8].reshape(1, 1024), t[8:16].reshape(1, 1024), t[16:18].reshape(1, 256), t[18:19],
                t[19:20, 0:HEADS])

    g_small = jnp.concatenate([tot[0:20], jnp.zeros((4, 128), F32)], axis=0)
    sm = _adamw(small_pack(pre_norm_g, post_norm_g, mla_q_norm_g, mla_kv_norm_g, fox_forget_b), g_small,
                small_pack(m_pre_norm_g, m_post_norm_g, m_mla_q_norm_g, m_mla_kv_norm_g, m_fox_forget_b),
                small_pack(v_pre_norm_g, v_post_norm_g, v_mla_q_norm_g, v_mla_kv_norm_g, v_fox_forget_b),
                "adamw_small")
    g_pre, g_post, g_q, g_kv, g_fb = small_unpack(g_small)
    (d_pre, d_post, d_q, d_kv, d_fb), (nm_pre, nm_post, nm_q, nm_kv, nm_fb), (nv_pre, nv_post, nv_q, nv_kv, nv_fb) = (
        small_unpack(t) for t in sm)

    d_meta, nm_meta, nv_meta = _adamw(meta_tokens, g_meta, m_meta_tokens, v_meta_tokens, "adamw_meta")
    d_win, nm_win, nv_win = (t.T[None] for t in _adamw(w_in[0].T, g_w_in[0].T, m_w_in[0].T, v_w_in[0].T,
                                                       "adamw_w_in"))
    d_wuq, nm_wuq, nv_wuq = _adamw(w_uq[0], g_w_uq, m_w_uq[0], v_w_uq[0], "adamw_w_uq")
    d_wukv, nm_wukv, nv_wukv = _adamw(w_ukv[0], g_w_ukv, m_w_ukv[0], v_w_ukv[0], "adamw_w_ukv")
    d_wbm, nm_wbm, nv_wbm = _adamw(w_br_mla[0], g_w_bm, m_w_br_mla[0], v_w_br_mla[0], "adamw_w_br_mla")
    d_wbf, nm_wbf, nv_wbf = _adamw(w_br_fox[0], g_w_bf, m_w_br_fox[0], v_w_br_fox[0], "adamw_w_br_fox")
    d_wo, nm_wo, nv_wo = _adamw(w_out[0], g_w_out, m_w_out[0], v_w_out[0], "adamw_w_out")

    def group(meta_, pre, win, fb_, q_, kv_, wuq, wukv, wbm, wbf, wo, post):
        return (meta_, pre, win, fb_, q_, kv_, wuq[None], wukv[None], wbm[None], wbf[None], wo[None], post)

    grads = group(g_meta, g_pre, g_w_in, g_fb, g_q, g_kv, g_w_uq, g_w_ukv, g_w_bm, g_w_bf, g_w_out, g_post)
    deltas = group(d_meta, d_pre, d_win, d_fb, d_q, d_kv, d_wuq, d_wukv, d_wbm, d_wbf, d_wo, d_post)
    new_m = group(nm_meta, nm_pre, nm_win, nm_fb, nm_q, nm_kv, nm_wuq, nm_wukv, nm_wbm, nm_wbf, nm_wo, nm_post)
    new_v = group(nv_meta, nv_pre, nv_win, nv_fb, nv_q, nv_kv, nv_wuq, nv_wukv, nv_wbm, nv_wbf, nv_wo, nv_post)
    return (loss, dx[None], *grads, *deltas, *new_m, *new_v)
```

```python
import math

import jax
import jax.numpy as jnp
import numpy as np
from jax import lax
from jax.experimental import pallas as pl
from jax.experimental.pallas import tpu as pltpu
from jax.experimental.pallas import tpu_sc as plsc

F32 = jnp.float32
BF16 = jnp.bfloat16

D_MODEL = 1024
N_META = 16
RMS_EPS = 1e-6
HEADS = 16
PAIRS = HEADS // 2
HEAD_DIM = 64
LANES = 128
MLA_ROPE = 32
MLA_SCALE = 1.0 / math.sqrt(64 + 32)
FOX_SCALE = 1.0 / math.sqrt(64)
ROPE_THETA = 10000.0

PAD = 256
BLK = 256
QB = 512
UNROLL = 4
NEG = -1e30

C_CQ, C_CKV, C_KPE, C_ZMLA, C_FQ, C_FK, C_FV, C_FL, C_ZFOX, C_GA, C_GB, C_END = (
    0, 256, 384, 416, 1440, 2464, 3488, 4512, 4528, 5552, 6576, 7600)
SMALL_W = 640
W_IN_SHARD = 1900

P2_ROWS = 928
N_CHIPS = 4

ADAM_LR = 0.001
ADAM_B1 = 0.9
ADAM_B2 = 0.999
ADAM_EPS = 1e-08
ADAM_WD = 0.01
ADAM_STEP = 10

VMEM_BIG = 56 * 1024 * 1024
MM_VMEM_BUDGET = 44 * 1024 * 1024
MESH = pl.DeviceIdType.MESH


def _cp(dims, vmem=None):
    return pltpu.CompilerParams(dimension_semantics=dims, vmem_limit_bytes=vmem)


def _dot(a, b, ca, cb):
    return lax.dot_general(a, b, (((ca,), (cb,)), ((), ())), preferred_element_type=F32)


def _sigmoid(x):
    return 1.0 / (1.0 + jnp.exp(-x))


def _tile(n, cands):
    for c in cands:
        if n % c == 0:
            return c
    return n


def _mm(a, b, *, mode, out_dtype, name, acc=None, epilogue=None, row_ins=(), after=None):
    if mode == "nn":
        (M, K), N = a.shape, b.shape[1]
    elif mode == "nt":
        (M, K), N = a.shape, b.shape[0]
    else:
        (K, M), N = a.shape, b.shape[1]
    tm = _tile(M, (1088, 1024)) if M > 1024 else M
    tn = _tile(N, (1024,)) if N > 1024 else N
    nk = 1
    while True:
        tk = K // nk
        need = 2 * tk * (tm * a.dtype.itemsize + tn * b.dtype.itemsize) + tm * tn * (
            2 * jnp.dtype(out_dtype).itemsize + (8 if acc is not None else 0) + (4 if nk > 1 else 0))
        if need <= MM_VMEM_BUDGET or (tk // 2) % (16 if mode == "tn" else LANES) or tk <= 512:
            break
        nk *= 2
    ca, cb = {"nn": (1, 0), "nt": (1, 1), "tn": (0, 0)}[mode]
    a_spec = (pl.BlockSpec((tk, tm), lambda j, i, k: (k, i)) if mode == "tn"
              else pl.BlockSpec((tm, tk), lambda j, i, k: (i, k)))
    b_spec = (pl.BlockSpec((tn, tk), lambda j, i, k: (j, k)) if mode == "nt"
              else pl.BlockSpec((tk, tn), lambda j, i, k: (k, j)))
    o_spec = pl.BlockSpec((tm, tn), lambda j, i, k: (i, j))
    has_acc = acc is not None

    nrow = len(row_ins)

    def body(*refs):
        a_ref, b_ref = refs[0], refs[1]
        acc_ref = refs[2] if has_acc else None
        rows = refs[2 + has_acc:2 + has_acc + nrow]
        o_ref = refs[2 + has_acc + nrow + (after is not None)]

        def store(tile):
            if epilogue is not None:
                tile = epilogue(tile, *[r[...] for r in rows])
            o_ref[...] = tile.astype(out_dtype)

        part = _dot(a_ref[...].astype(BF16), b_ref[...].astype(BF16), ca, cb)
        if nk == 1:
            store(part + acc_ref[...] if has_acc else part)
        else:
            sc = refs[-1]
            k = pl.program_id(2)

            @pl.when(k == 0)
            def _():
                sc[...] = part + acc_ref[...] if has_acc else part

            @pl.when(k > 0)
            def _():
                sc[...] += part

            @pl.when(k == nk - 1)
            def _():
                store(sc[...])

    ins = [a, b] + ([acc] if has_acc else []) + list(row_ins)
    in_specs = ([a_spec, b_spec] + ([o_spec] if has_acc else [])
                + [pl.BlockSpec((tm, r.shape[1]), lambda j, i, k: (i, 0)) for r in row_ins])
    if after is not None:
        ins.append(after)
        in_specs.append(pl.BlockSpec(after.shape, lambda j, i, k: (0,) * after.ndim))
    return pl.pallas_call(
        body, name=name, grid=(N // tn, M // tm, nk), in_specs=in_specs, out_specs=o_spec,
        out_shape=jax.ShapeDtypeStruct((M, N), out_dtype),
        scratch_shapes=[pltpu.VMEM((tm, tn), F32)] if nk > 1 else [],
        compiler_params=_cp(("parallel", "parallel", "arbitrary"), VMEM_BIG))(*ins)


def _mm_sum_nt(pairs, *, name, after=None):
    n = len(pairs)
    M, N = pairs[0][0].shape[0], pairs[0][1].shape[0]
    tm = _tile(M, (272,))

    def body(*refs):
        o_ref = refs[2 * n + (after is not None)]
        tot = _dot(refs[0][...].astype(BF16), refs[n][...].astype(BF16), 1, 1)
        for i in range(1, n):
            tot = tot + _dot(refs[i][...].astype(BF16), refs[n + i][...].astype(BF16), 1, 1)
        o_ref[...] = tot

    ins = [a for a, _ in pairs] + [b for _, b in pairs]
    in_specs = ([pl.BlockSpec((tm, a.shape[1]), lambda i: (i, 0)) for a, _ in pairs]
                + [pl.BlockSpec(b.shape, lambda i: (0, 0)) for _, b in pairs])
    if after is not None:
        ins.append(after)
        in_specs.append(pl.BlockSpec(after.shape, lambda i: (0,) * after.ndim))
    return pl.pallas_call(
        body, name=name, grid=(M // tm,), in_specs=in_specs, out_specs=pl.BlockSpec((tm, N), lambda i: (i, 0)),
        out_shape=jax.ShapeDtypeStruct((M, N), F32), compiler_params=_cp(("parallel",), VMEM_BIG))(*ins)


def _row(w):
    return pl.BlockSpec((BLK, w), lambda i: (i, 0))


def _rowc(w, c):
    return pl.BlockSpec((BLK, w), lambda i: (i, c))


def _full(shape):
    return pl.BlockSpec(shape, lambda i: tuple(0 for _ in shape))


def _rope(x, c, s):
    lane = lax.broadcasted_iota(jnp.int32, x.shape, 1)
    is_x1 = ((lane >> 4) & 1) == 0
    partner = jnp.where(is_x1, pltpu.roll(x, LANES - 16, 1), pltpu.roll(x, 16, 1))
    return x * c + partner * s


def _row_valid(i):
    rows = i * BLK + lax.broadcasted_iota(jnp.int32, (BLK, 1), 0)
    return (rows < N_META) | (rows >= PAD)


def _shift_rows(w):
    return pl.BlockSpec((BLK, w), lambda i: (jnp.maximum(i - 1, 0), 0))


def _h_block(i, x_ref, meta_ref):
    head = jnp.concatenate([meta_ref[...], jnp.zeros((BLK - N_META, D_MODEL), F32)], axis=0)
    return jnp.where(i == 0, head, x_ref[...])


def _rms_pre(x2, meta, g):
    lp = PAD + x2.shape[0]

    def body(x_ref, meta_ref, g_ref, u_ref):
        hv = _h_block(pl.program_id(0), x_ref, meta_ref)
        r = lax.rsqrt(jnp.mean(hv * hv, axis=-1, keepdims=True) + RMS_EPS)
        u_ref[...] = (hv * r * g_ref[...]).astype(BF16)

    return pl.pallas_call(
        body, name="rms_pre", grid=(lp // BLK,),
        in_specs=[_shift_rows(D_MODEL), _full((N_META, D_MODEL)), _full((1, D_MODEL))], out_specs=_row(D_MODEL),
        out_shape=jax.ShapeDtypeStruct((lp, D_MODEL), BF16),
        compiler_params=_cp(("parallel",)))(x2, meta, g)


def _split3(x):
    hi = x.astype(BF16)
    r1 = x - hi.astype(F32)
    mid = r1.astype(BF16)
    lo = (r1 - mid.astype(F32)).astype(BF16)
    return hi, mid, lo


def _small_prep(small, gq, gkv, fb, ctab, stab, tri):
    lp = small.shape[0]

    def body(sm_ref, gq_ref, gkv_ref, fb_ref, c_ref, s_ref, tri_ref, qn_ref, kvn_ref, kr_ref, ncum_ref, carry):
        i = pl.program_id(0)

        @pl.when(i == 0)
        def _():
            carry[...] = jnp.zeros_like(carry)

        cq = sm_ref[:, 0:256]
        r = lax.rsqrt(jnp.mean(cq * cq, axis=-1, keepdims=True) + RMS_EPS)
        qn_ref[...] = (cq * r * gq_ref[...]).astype(BF16)
        ckv = sm_ref[:, 256:384]
        r = lax.rsqrt(jnp.mean(ckv * ckv, axis=-1, keepdims=True) + RMS_EPS)
        kvn_ref[...] = (ckv * r * gkv_ref[...]).astype(BF16)
        kr_ref[...] = _rope(sm_ref[:, 384:512], c_ref[...], s_ref[...]).astype(BF16)
        fl = sm_ref[:, 512:640] + fb_ref[...]
        lf = jnp.minimum(fl, 0.0) - jnp.log(1.0 + jnp.exp(-jnp.abs(fl)))
        lf = jnp.where(_row_valid(i), lf, 0.0)
        hi, mid, lo = _split3(lf)
        t = tri_ref[...]
        cum = (_dot(t, hi, 1, 0) + _dot(t, mid, 1, 0)) + _dot(t, lo, 1, 0) + carry[...]
        ncum_ref[...] = -cum
        carry[...] = -ncum_ref[BLK - 1:BLK, :]

    return pl.pallas_call(
        body, name="small_prep", grid=(lp // BLK,),
        in_specs=[_row(SMALL_W), _full((1, 256)), _full((1, 128)), _full((1, 128)), _row(128), _row(128),
                  _full((BLK, BLK))],
        out_specs=[_row(256), _row(128), _row(128), _row(128)],
        out_shape=[jax.ShapeDtypeStruct((lp, 256), BF16), jax.ShapeDtypeStruct((lp, 128), BF16),
                   jax.ShapeDtypeStruct((lp, 128), BF16), jax.ShapeDtypeStruct((lp, 128), F32)],
        scratch_shapes=[pltpu.VMEM((1, 128), F32)],
        compiler_params=_cp(("arbitrary",)))(small, gq, gkv, fb, ctab, stab, tri)


def _rope_pairs(tile, c, s):
    out = []
    for lo in range(0, tile.shape[1], 256):
        out += [tile[:, lo:lo + 128], _rope(tile[:, lo + 128:lo + 256], c, s)]
    return jnp.concatenate(out, axis=1)


def _gate_fwd(o_mla, o_fox, gate):
    lp = o_mla.shape[0]

    def body(om_ref, of_ref, zm_ref, zf_ref, am_ref, af_ref):
        zm = zm_ref[...].astype(F32)
        am_ref[...] = (om_ref[...] * (zm * _sigmoid(zm))).astype(BF16)
        zf = zf_ref[...].astype(F32)
        af_ref[...] = (of_ref[...] * (zf * _sigmoid(zf))).astype(BF16)

    return pl.pallas_call(
        body, name="gate_fwd", grid=(lp // BLK,),
        in_specs=[_row(D_MODEL), _row(D_MODEL), _rowc(D_MODEL, 0), _rowc(D_MODEL, 1)],
        out_specs=[_row(D_MODEL), _row(D_MODEL)],
        out_shape=[jax.ShapeDtypeStruct((lp, D_MODEL), BF16)] * 2,
        compiler_params=_cp(("parallel",)))(o_mla, o_fox, gate, gate)


def _merge_fwd(gate, y_mla, y_fox):
    lp = y_mla.shape[0]

    def body(ga_ref, gb_ref, ym_ref, yf_ref, m_ref):
        sa = _sigmoid(ga_ref[...].astype(F32))
        sb = _sigmoid(gb_ref[...].astype(F32))
        m_ref[...] = (sa * ym_ref[...] + sb * yf_ref[...]).astype(BF16)

    return pl.pallas_call(
        body, name="merge_fwd", grid=(lp // BLK,),
        in_specs=[_rowc(D_MODEL, 2), _rowc(D_MODEL, 3), _row(D_MODEL), _row(D_MODEL)],
        out_specs=_row(D_MODEL), out_shape=jax.ShapeDtypeStruct((lp, D_MODEL), BF16),
        compiler_params=_cp(("parallel",)))(gate, gate, y_mla, y_fox)


def _tail(x2, mixed, tgt, gpost):
    lp = mixed.shape[0]
    shift = _shift_rows(D_MODEL)

    def body(h_ref, mx_ref, t_ref, g_ref, dmx_ref, dy_ref, loss_ref, dg_ref):
        i = pl.program_id(0)

        @pl.when(i == 0)
        def _():
            loss_ref[...] = jnp.zeros_like(loss_ref)
            dg_ref[...] = jnp.zeros_like(dg_ref)
            dmx_ref[...] = jnp.zeros_like(dmx_ref)
            dy_ref[...] = jnp.zeros_like(dy_ref)

        @pl.when(i > 0)
        def _():
            mx = mx_ref[...]
            g = g_ref[...]
            r = lax.rsqrt(jnp.mean(mx * mx, axis=-1, keepdims=True) + RMS_EPS)
            nrm = mx * r
            e = (h_ref[...] + nrm * g) - t_ref[...]
            loss_ref[...] += jnp.sum(0.5 * jnp.sum(e * e, axis=-1, keepdims=True) * (1.0 / D_MODEL),
                                     axis=0, keepdims=True)
            dy = e * (1.0 / D_MODEL)
            dy_ref[...] = dy
            dg_ref[...] += jnp.sum(dy * nrm, axis=0, keepdims=True)
            w = dy * g
            dot = jnp.mean(w * mx, axis=-1, keepdims=True)
            dmx_ref[...] = (r * w - mx * (r * r * r * dot)).astype(BF16)

    return pl.pallas_call(
        body, name="tail", grid=(lp // BLK,),
        in_specs=[shift, _row(D_MODEL), shift, _full((1, D_MODEL))],
        out_specs=[_row(D_MODEL), _row(D_MODEL), _full((1, 1)), _full((1, D_MODEL))],
        out_shape=[jax.ShapeDtypeStruct((lp, D_MODEL), BF16), jax.ShapeDtypeStruct((lp, D_MODEL), F32),
                   jax.ShapeDtypeStruct((1, 1), F32), jax.ShapeDtypeStruct((1, D_MODEL), F32)],
        compiler_params=_cp(("arbitrary",)))(x2, mixed, tgt, gpost)


def _merge_bwd(dm, gate, y_mla, y_fox):
    lp = dm.shape[0]

    def body(dm_ref, ga_ref, gb_ref, ym_ref, yf_ref, dym_ref, dyf_ref, dg_ref):
        dm_v = dm_ref[...].astype(F32)
        sa = _sigmoid(ga_ref[...].astype(F32))
        sb = _sigmoid(gb_ref[...].astype(F32))
        dym_ref[...] = (dm_v * sa).astype(BF16)
        dyf_ref[...] = (dm_v * sb).astype(BF16)
        dg_ref[:, 0:D_MODEL] = (dm_v * ym_ref[...] * (sa * (1.0 - sa))).astype(BF16)
        dg_ref[:, D_MODEL:2 * D_MODEL] = (dm_v * yf_ref[...] * (sb * (1.0 - sb))).astype(BF16)

    return pl.pallas_call(
        body, name="merge_bwd", grid=(lp // BLK,),
        in_specs=[_row(D_MODEL), _rowc(D_MODEL, 2), _rowc(D_MODEL, 3), _row(D_MODEL), _row(D_MODEL)],
        out_specs=[_row(D_MODEL), _row(D_MODEL), _row(2 * D_MODEL)],
        out_shape=[jax.ShapeDtypeStruct((lp, D_MODEL), BF16), jax.ShapeDtypeStruct((lp, D_MODEL), BF16),
                   jax.ShapeDtypeStruct((lp, 2 * D_MODEL), BF16)],
        compiler_params=_cp(("parallel",)))(dm, gate, gate, y_mla, y_fox)


def _gate_bwd(da_mla, da_fox, o_mla, o_fox, gate):
    lp = da_mla.shape[0]

    def one(da, o, z, head_of_col):
        sg = _sigmoid(z)
        do = (da * (z * sg)).astype(BF16)
        dz = da * o * (sg * (1.0 + z * (1.0 - sg)))
        delta = sum(_dot(part, head_of_col, 1, 0) for part in _split3(do.astype(F32) * o))
        return do, dz.astype(BF16), delta

    def body(dam_ref, daf_ref, om_ref, of_ref, zm_ref, zf_ref, dom_ref, dof_ref, dz_ref, dlm_ref, dlf_ref):
        f32 = lambda r: r[...].astype(F32)
        head_of_col = (lax.broadcasted_iota(jnp.int32, (D_MODEL, LANES), 0) // HEAD_DIM
                       == lax.broadcasted_iota(jnp.int32, (D_MODEL, LANES), 1)).astype(BF16)
        dom_ref[...], dz_ref[:, 0:D_MODEL], dlm_ref[...] = one(f32(dam_ref), f32(om_ref), f32(zm_ref), head_of_col)
        dof_ref[...], dz_ref[:, D_MODEL:2 * D_MODEL], dlf_ref[...] = one(f32(daf_ref), f32(of_ref), f32(zf_ref),
                                                                        head_of_col)

    return pl.pallas_call(
        body, name="gate_bwd", grid=(lp // BLK,),
        in_specs=[_row(D_MODEL)] * 4 + [_rowc(D_MODEL, 0), _rowc(D_MODEL, 1)],
        out_specs=[_row(D_MODEL), _row(D_MODEL), _row(2 * D_MODEL), _row(LANES), _row(LANES)],
        out_shape=[jax.ShapeDtypeStruct((lp, D_MODEL), BF16), jax.ShapeDtypeStruct((lp, D_MODEL), BF16),
                   jax.ShapeDtypeStruct((lp, 2 * D_MODEL), BF16), jax.ShapeDtypeStruct((lp, LANES), F32),
                   jax.ShapeDtypeStruct((lp, LANES), F32)],
        compiler_params=_cp(("parallel",)))(da_mla, da_fox, o_mla, o_fox, gate, gate)


def _small_bwd(small, dqn, dkvn, dkr, dcol_t, drow_t, gq, gkv, fb, ctab, stab, triu):
    lp = small.shape[0]
    nb = lp // BLK

    def rrow(w):
        return pl.BlockSpec((BLK, w), lambda i: (nb - 1 - i, 0))

    def body(sm_ref, dqn_ref, dkvn_ref, dkr_ref, dcol_ref, drow_ref, gq_ref, gkv_ref, fb_ref, c_ref, s_ref, tri_ref,
             ds_ref, dgq_ref, dgkv_ref, dfb_ref, carry):
        i = pl.program_id(0)

        @pl.when(i == 0)
        def _():
            carry[...] = jnp.zeros_like(carry)
            dgq_ref[...] = jnp.zeros_like(dgq_ref)
            dgkv_ref[...] = jnp.zeros_like(dgkv_ref)
            dfb_ref[...] = jnp.zeros_like(dfb_ref)

        def norm_bwd(x, dn, g, dg_ref):
            r = lax.rsqrt(jnp.mean(x * x, axis=-1, keepdims=True) + RMS_EPS)
            dg_ref[...] += jnp.sum(dn * (x * r), axis=0, keepdims=True)
            w = dn * g
            dot = jnp.mean(w * x, axis=-1, keepdims=True)
            return r * w - x * (r * r * r * dot)

        ds_ref[:, 0:256] = norm_bwd(sm_ref[:, 0:256], dqn_ref[...], gq_ref[...], dgq_ref).astype(BF16)
        ds_ref[:, 256:384] = norm_bwd(sm_ref[:, 256:384], dkvn_ref[...], gkv_ref[...], dgkv_ref).astype(BF16)

        dk = dkr_ref[0]
        for p in range(1, PAIRS):
            dk = dk + dkr_ref[p]
        dk = _rope(dk, c_ref[...], -s_ref[...])
        lane = lax.broadcasted_iota(jnp.int32, dk.shape, 1)
        dk = jnp.where(lane < MLA_ROPE, dk + pltpu.roll(dk, LANES - MLA_ROPE, 1), 0.0)
        ds_ref[:, 384:512] = dk.astype(BF16)

        dcol = dcol_ref[0]
        for p in range(1, PAIRS):
            dcol = dcol + pltpu.roll(dcol_ref[p], 2 * p, 1)
        rows16 = jnp.concatenate([drow_ref[p, h:h + 1, :] for p in range(PAIRS) for h in range(2)], axis=0)
        eye = (lax.broadcasted_iota(jnp.int32, (HEADS, LANES), 0)
               == lax.broadcasted_iota(jnp.int32, (HEADS, LANES), 1)).astype(BF16)
        drow = sum(_dot(part, eye, 0, 0) for part in _split3(rows16))
        dcr = dcol - drow
        hi, mid, lo = _split3(dcr)
        t = tri_ref[...]
        suf = (_dot(t, hi, 1, 0) + _dot(t, mid, 1, 0)) + _dot(t, lo, 1, 0) + carry[...]
        fl = sm_ref[:, 512:640] + fb_ref[...]
        dfl = jnp.where(_row_valid(nb - 1 - i), -suf * _sigmoid(-fl), 0.0)
        ds_ref[:, 512:640] = dfl.astype(BF16)
        dfb_ref[...] += jnp.sum(dfl, axis=0, keepdims=True)
        carry[...] += jnp.sum(dcr, axis=0, keepdims=True)

    return pl.pallas_call(
        body, name="small_bwd", grid=(nb,),
        in_specs=[rrow(SMALL_W), rrow(256), rrow(128),
                  pl.BlockSpec((PAIRS, BLK, 128), lambda i: (0, nb - 1 - i, 0)),
                  pl.BlockSpec((PAIRS, BLK, 128), lambda i: (0, nb - 1 - i, 0)),
                  pl.BlockSpec((PAIRS, 2, BLK), lambda i: (0, 0, nb - 1 - i)),
                  _full((1, 256)), _full((1, 128)), _full((1, 128)), rrow(128), rrow(128), _full((BLK, BLK))],
        out_specs=[rrow(SMALL_W), _full((1, 256)), _full((1, 128)), _full((1, 128))],
        out_shape=[jax.ShapeDtypeStruct((lp, SMALL_W), BF16), jax.ShapeDtypeStruct((1, 256), F32),
                   jax.ShapeDtypeStruct((1, 128), F32), jax.ShapeDtypeStruct((1, 128), F32)],
        scratch_shapes=[pltpu.VMEM((1, 128), F32)],
        compiler_params=_cp(("arbitrary",)))(small, dqn, dkvn, dkr, dcol_t, drow_t, gq, gkv, fb, ctab, stab, triu)


def _pre_bwd(du, x2, meta, dy, gpre):
    s_rows = x2.shape[0]
    lp = PAD + s_rows
    shift = _shift_rows(D_MODEL)

    def body(du_ref, x_ref, meta_ref, dy_ref, g_ref, dx_ref, dmeta_ref, dg_ref):
        i = pl.program_id(0)

        @pl.when(i == 0)
        def _():
            dg_ref[...] = jnp.zeros_like(dg_ref)

        hv = _h_block(i, x_ref, meta_ref)
        duv = du_ref[...]
        r = lax.rsqrt(jnp.mean(hv * hv, axis=-1, keepdims=True) + RMS_EPS)
        dg_ref[...] += jnp.sum(duv * (hv * r), axis=0, keepdims=True)
        w = duv * g_ref[...]
        dot = jnp.mean(w * hv, axis=-1, keepdims=True)
        dh = dy_ref[...] + (r * w - hv * (r * r * r * dot))
        dx_ref[...] = dh

        @pl.when(i == 0)
        def _():
            dmeta_ref[...] = dh[0:N_META, :]

    return pl.pallas_call(
        body, name="pre_bwd", grid=(lp // BLK,),
        in_specs=[_row(D_MODEL), shift, _full((N_META, D_MODEL)), _row(D_MODEL), _full((1, D_MODEL))],
        out_specs=[shift, _full((N_META, D_MODEL)), _full((1, D_MODEL))],
        out_shape=[jax.ShapeDtypeStruct((s_rows, D_MODEL), F32), jax.ShapeDtypeStruct((N_META, D_MODEL), F32),
                   jax.ShapeDtypeStruct((1, D_MODEL), F32)],
        compiler_params=_cp(("arbitrary",)))(du, x2, meta, dy, gpre)


def _pair_masks(rope):
    lane = lax.broadcasted_iota(jnp.int32, (1, LANES), 1)
    mas = [lane < HEAD_DIM, lane >= HEAD_DIM]
    if not rope:
        return mas, mas
    wide = lax.broadcasted_iota(jnp.int32, (1, 2 * LANES), 1)
    rope_lo = LANES + MLA_ROPE
    return mas, [(wide < HEAD_DIM) | ((wide >= LANES) & (wide < rope_lo)),
                 ((wide >= HEAD_DIM) & (wide < LANES)) | ((wide >= rope_lo) & (wide < rope_lo + MLA_ROPE))]


def _mask2(x, masks):
    return [jnp.where(m, x, jnp.zeros_like(x)) for m in masks]


def _attn_fwd(q, k, v, *, kr=None, nbrep=None, scale, qcol, kcol, vcol, name):
    lp = q.shape[0]
    nq = 1 + (lp - PAD) // QB
    rope = kr is not None
    bias = nbrep is not None
    qw = 256 if rope else 128

    def body(*refs):
        it = iter(refs)
        q_ref, k_ref, v_ref = next(it), next(it), next(it)
        kr_ref = next(it) if rope else None
        nb_ref = next(it) if bias else None
        o_ref, lse_ref = next(it), next(it)
        i = pl.program_id(1)
        r0 = pl.multiple_of(jnp.where(i == 0, 0, PAD + QB * (i - 1)), BLK)
        b0 = r0 // BLK
        mas, hmask = _pair_masks(rope)
        qh = _mask2(q_ref[pl.ds(r0, QB), :], hmask)
        if bias:
            qh = [x * scale for x in qh]

        def causal(kc, n):
            key = kc * BLK + lax.broadcasted_iota(jnp.int32, (n, QB), 0)
            return (key <= r0 + lax.broadcasted_iota(jnp.int32, (n, QB), 1)) & ((kc > 0) | (n == N_META))

        def update(kcs, carry, masks, n=BLK, q_lo=0, wq=QB):
            stats, acc = carry[:4], carry[4]
            qs = [x[q_lo:q_lo + wq] for x in qh]
            k0s = [pl.multiple_of(kc * BLK, BLK) for kc in kcs]
            kks = [k_ref[pl.ds(k0, n), :] for k0 in k0s]
            if rope:
                kks = [jnp.concatenate([kk, kr_ref[pl.ds(k0, n), :]], axis=1) for kk, k0 in zip(kks, k0s)]
            new_stats, alphas, ps = [], [], [[] for _ in kcs]
            for h in range(2):
                m_prev, l_prev = stats[2 * h], stats[2 * h + 1]
                ss = []
                for kk, k0, mask in zip(kks, k0s, masks):
                    s = _dot(kk, qs[h], 1, 1)
                    if rope:
                        s = s * scale
                    if bias:
                        nbc = nb_ref[h, pl.ds(k0, n), :]
                        s = s + jnp.concatenate([nbc] * (wq // LANES), axis=1)
                    if mask is not None:
                        s = jnp.where(mask, s, NEG)
                    ss.append(s)
                m_new = m_prev
                for s in ss:
                    m_new = jnp.maximum(m_new, jnp.max(s, axis=0, keepdims=True))
                alpha = jnp.exp(m_prev - m_new)
                l_new = alpha * l_prev
                for j, s in enumerate(ss):
                    p = jnp.exp(s - m_new)
                    l_new = l_new + jnp.sum(p, axis=0, keepdims=True)
                    ps[j].append(p.astype(BF16))
                new_stats += [m_new, l_new]
                alphas.append(alpha)
            vcat = jnp.concatenate([x for k0 in k0s for x in _mask2(v_ref[pl.ds(k0, n), :], mas)], axis=0)
            pv = _dot(vcat, jnp.concatenate([p for pj in ps for p in pj], axis=0), 0, 0)
            a_full = jnp.concatenate([jnp.broadcast_to(a, (HEAD_DIM, wq)) for a in alphas], axis=0)
            return (*new_stats, a_full * acc + pv)

        neg = jnp.full((1, QB), NEG, F32)
        zero = jnp.zeros((1, QB), F32)
        c = update([0], (neg, zero, neg, zero, jnp.zeros((LANES, QB), F32)), [causal(0, N_META)], N_META)
        n_mid = jnp.maximum(b0 - 1, 0)
        c = lax.fori_loop(0, n_mid // 4, lambda t, cr: update([4 * t + u for u in (1, 2, 3, 4)], cr, [None] * 4), c)
        c = lax.fori_loop(0, (n_mid % 4) // 2, lambda t, cr: update([n_mid - 1, n_mid], cr, [None, None]), c)
        tri = ((lax.broadcasted_iota(jnp.int32, (BLK, BLK), 0) <= lax.broadcasted_iota(jnp.int32, (BLK, BLK), 1))
               & (b0 > 0))
        lo = update([b0], tuple(a[:, 0:BLK] for a in c), [tri], q_lo=0, wq=BLK)
        hi = update([b0, b0 + 1], tuple(a[:, BLK:QB] for a in c), [None, tri], q_lo=BLK, wq=QB - BLK)
        c = tuple(jnp.concatenate([a, b], axis=1) for a, b in zip(lo, hi))
        inv =jnp.concatenate([jnp.broadcast_to(1.0 / c[1], (HEAD_DIM, QB)),
                               jnp.broadcast_to(1.0 / c[3], (HEAD_DIM, QB))], axis=0)
        o_t = (c[4] * inv).T.astype(BF16)
        lses = [c[0] + jnp.log(c[1]), c[2] + jnp.log(c[3])]
        o_ref[pl.ds(r0, BLK), :] = o_t[0:BLK]
        for h in range(2):
            lse_ref[0, h:h + 1, pl.ds(r0, BLK)] = lses[h][:, 0:BLK]

        @pl.when(i > 0)
        def _():
            r1 = pl.multiple_of(r0 + BLK, BLK)
            o_ref[pl.ds(r1, QB - BLK), :] = o_t[BLK:QB]
            for h in range(2):
                lse_ref[0, h:h + 1, pl.ds(r1, QB - BLK)] = lses[h][:, BLK:QB]

    in_specs = [pl.BlockSpec((lp, qw), lambda p, i: (0, qcol + p)),
                pl.BlockSpec((lp, 128), lambda p, i: (0, kcol(p))),
                pl.BlockSpec((lp, 128), lambda p, i: (0, vcol(p)))]
    ins = [q, k, v]
    if rope:
        in_specs.append(pl.BlockSpec((lp, 128), lambda p, i: (0, 0)))
        ins.append(kr)
    if bias:
        in_specs.append(pl.BlockSpec((2, lp, 128), lambda p, i: (p, 0, 0)))
        ins.append(nbrep)
    return pl.pallas_call(
        body, name=name, grid=(PAIRS, nq), in_specs=in_specs,
        out_specs=[pl.BlockSpec((lp, 128), lambda p, i: (0, p)),
                   pl.BlockSpec((1, 2, lp), lambda p, i: (p, 0, 0))],
        out_shape=[jax.ShapeDtypeStruct((lp, D_MODEL), BF16), jax.ShapeDtypeStruct((PAIRS, 2, lp), F32)],
        compiler_params=_cp(("parallel", "arbitrary"), VMEM_BIG))(*ins)


def _attn_bwd(q, k, v, do, delta, lse, *, kr=None, rtabs=None, nbrep=None, scale, qcol, kcol, vcol, name):
    lp = q.shape[0]
    nb = lp // BLK
    rope = kr is not None
    bias = nbrep is not None
    qw = 256 if rope else 128

    def body(*refs):
        it = iter(refs)
        q_ref, k_ref, v_ref = next(it), next(it), next(it)
        kr_ref = next(it) if rope else None
        nb_ref = next(it) if bias else None
        do_ref, dl_ref, lse_ref = next(it), next(it), next(it)
        ct_ref, st_ref = (next(it), next(it)) if rope else (None, None)
        dq_out, dk_ref, dv_ref = next(it), next(it), next(it)
        x_ref = next(it)
        drow_ref = next(it) if bias else None
        dq_ref = next(it)
        kb = pl.program_id(1)
        mas, hmask = _pair_masks(rope)
        lane = lax.broadcasted_iota(jnp.int32, (1, LANES), 1)

        @pl.when(kb == 0)
        def _():
            dq_ref[...] = jnp.zeros_like(dq_ref)
            if bias:
                drow_ref[...] = jnp.zeros_like(drow_ref)

        def key_pass(n, w):
            kk = k_ref[0:n, :]
            if rope:
                kk = jnp.concatenate([kk, kr_ref[0:n, :]], axis=1)
            vh = _mask2(v_ref[0:n, :], mas)
            kcat = jnp.concatenate(_mask2(kk, hmask), axis=0)
            if bias:
                kcat = kcat * scale
            diag_mask = (lax.broadcasted_iota(jnp.int32, (n, w), 0) <= lax.broadcasted_iota(jnp.int32, (n, w), 1))

            def chunk(qc, carry, mask):
                carry = list(carry)
                q0 = qc * w if isinstance(qc, int) else pl.multiple_of(qc * w, w)
                dov = do_ref[pl.ds(q0, w), :]
                doh = _mask2(dov, mas)
                qh = _mask2(q_ref[pl.ds(q0, w), :], hmask)
                if bias:
                    qh = [x * scale for x in qh]
                pbs, dss = [], []
                for h in range(2):
                    s = _dot(kk, qh[h], 1, 1)
                    if rope:
                        s = s * scale
                    if bias:
                        s = s + jnp.concatenate([nb_ref[h, 0:n, :]] * (w // LANES), axis=1)
                    p = jnp.exp(s - lse_ref[0, h:h + 1, pl.ds(q0, w)])
                    if mask is not None:
                        p = jnp.where(mask, p, 0.0)
                    ds = p * (_dot(vh[h], dov, 1, 1) - dl_ref[0, h:h + 1, pl.ds(q0, w)])
                    if bias:
                        drow_ref[0, h:h + 1, pl.ds(q0, w)] += jnp.sum(ds, axis=0, keepdims=True)
                        carry[2 + h] = carry[2 + h] + jnp.sum(ds, axis=1, keepdims=True)
                    else:
                        ds = ds * scale
                    pbs.append(p.astype(BF16))
                    dss.append(ds.astype(BF16))
                ds_lanes = jnp.concatenate(dss, axis=1)
                ds_rows = jnp.concatenate(dss, axis=0)
                carry[0] = carry[0] + _dot(ds_lanes, jnp.concatenate(qh, axis=0), 1, 0)
                carry[1] = carry[1] + _dot(jnp.concatenate(pbs, axis=1), jnp.concatenate(doh, axis=0), 1, 0)
                dq_ref[pl.ds(q0, w), :] += _dot(ds_rows, kcat, 0, 0)
                return tuple(carry)

            c = [jnp.zeros((n, qw), F32), jnp.zeros((n, LANES), F32)]
            if bias:
                c += [jnp.zeros((n, 1), F32), jnp.zeros((n, 1), F32)]
            c = tuple(c)
            if w != BLK:
                for qc in range(lp // w):
                    c = chunk(qc, c, diag_mask if qc == 0 else None)
            else:
                groups = (nb - kb) // UNROLL

                def several(t, cr):
                    for u in range(UNROLL):
                        cr = chunk(kb + UNROLL * t + u, cr, (diag_mask | (t > 0)) if u == 0 else None)
                    return cr

                c = lax.fori_loop(0, groups, several, c)
                start = kb + UNROLL * groups
                pairs = (nb - start) // 2

                def two(t, cr):
                    qc = start + 2 * t
                    return chunk(qc + 1, chunk(qc, cr, diag_mask | (qc > kb)), None)

                c = lax.fori_loop(0, pairs, two, c)
                c = lax.fori_loop(start + 2 * pairs, nb, lambda qc, cr: chunk(qc, cr, diag_mask | (qc > kb)), c)

            def rows(a, dtype):
                a = a.astype(dtype)
                return a if n == BLK else jnp.concatenate([a, jnp.zeros((BLK - n, a.shape[1]), dtype)], axis=0)

            dk_ref[...] = rows(c[0][:, 0:LANES], BF16)
            dv_ref[...] = rows(c[1], BF16)
            if rope:
                x_ref[0] = rows(c[0][:, LANES:2 * LANES], F32)
            if bias:
                x_ref[0] = rows(jnp.where(lane == 0, c[2], jnp.where(lane == 1, c[3], 0.0)), F32)

        @pl.when(kb == 0)
        def _():
            key_pass(N_META, lp // 2)

        @pl.when(kb > 0)
        def _():
            key_pass(BLK, BLK)

        @pl.when(kb == nb - 1)
        def _():
            def fin(c, carry):
                r0 = pl.multiple_of(c * BLK, BLK)
                dq = dq_ref[pl.ds(r0, BLK), :]
                if rope:
                    back = _rope(dq[:, LANES:2 * LANES], ct_ref[pl.ds(r0, BLK), :], -st_ref[pl.ds(r0, BLK), :])
                    dq = jnp.concatenate([dq[:, 0:LANES], back], axis=1)
                dq_out[pl.ds(r0, BLK), :] = dq.astype(BF16)
                return carry

            lax.fori_loop(0, nb, fin, 0)

    in_specs = [pl.BlockSpec((lp, qw), lambda p, j: (0, qcol + p)),
                pl.BlockSpec((BLK, 128), lambda p, j: (j, kcol(p))),
                pl.BlockSpec((BLK, 128), lambda p, j: (j, vcol(p)))]
    ins = [q, k, v]
    if rope:
        in_specs.append(pl.BlockSpec((BLK, 128), lambda p, j: (j, 0)))
        ins.append(kr)
    if bias:
        in_specs.append(pl.BlockSpec((2, BLK, 128), lambda p, j: (p, j, 0)))
        ins.append(nbrep)
    in_specs += [pl.BlockSpec((lp, 128), lambda p, j: (0, p)), pl.BlockSpec((1, 2, lp), lambda p, j: (p, 0, 0)),
                 pl.BlockSpec((1, 2, lp), lambda p, j: (p, 0, 0))]
    ins += [do, delta, lse]
    if rope:
        in_specs += [pl.BlockSpec((lp, 128), lambda p, j: (0, 0))] * 2
        ins += list(rtabs)
    out_specs = [pl.BlockSpec((lp, qw), lambda p, j: (0, p)),
                 pl.BlockSpec((BLK, 128), lambda p, j: (j, p)),
                 pl.BlockSpec((BLK, 128), lambda p, j: (j, p)),
                 pl.BlockSpec((1, BLK, 128), lambda p, j: (p, j, 0))]
    out_shape = [jax.ShapeDtypeStruct((lp, PAIRS * qw), BF16), jax.ShapeDtypeStruct((lp, D_MODEL), BF16),
                 jax.ShapeDtypeStruct((lp, D_MODEL), BF16), jax.ShapeDtypeStruct((PAIRS, lp, 128), F32)]
    if bias:
        out_specs.append(pl.BlockSpec((1, 2, lp), lambda p, j: (p, 0, 0)))
        out_shape.append(jax.ShapeDtypeStruct((PAIRS, 2, lp), F32))
    return pl.pallas_call(
        body, name=name, grid=(PAIRS, nb), in_specs=in_specs, out_specs=out_specs, out_shape=out_shape,
        scratch_shapes=[pltpu.VMEM((lp, qw), F32)],
        compiler_params=_cp(("parallel", "arbitrary"), VMEM_BIG))(*ins)


def _adamw(w, g, m, v, name):
    lead = w.ndim - 2
    rows, cols = w.shape[lead:]
    big = rows * cols > 512 * 1024
    tr = 128 if big and rows % 128 == 0 else rows
    tc = 256 if big and tr == rows else cols

    def body(w_ref, g_ref, m_ref, v_ref, d_ref, nm_ref, nv_ref):
        gv = g_ref[...]
        nm = ADAM_B1 * m_ref[...] + (1.0 - ADAM_B1) * gv
        nv = ADAM_B2 * v_ref[...] + (1.0 - ADAM_B2) * (gv * gv)
        m_hat = nm / (1.0 - ADAM_B1 ** ADAM_STEP)
        v_hat = nv / (1.0 - ADAM_B2 ** ADAM_STEP)
        d_ref[...] = -ADAM_LR * (m_hat / (jnp.sqrt(v_hat) + ADAM_EPS) + ADAM_WD * w_ref[...])
        nm_ref[...] = nm
        nv_ref[...] = nv

    spec = pl.BlockSpec((1,) * lead + (tr, tc), lambda i, j: (0,) * lead + (i, j))
    return pl.pallas_call(
        body, name=name, grid=(rows // tr, cols // tc), in_specs=[spec] * 4, out_specs=[spec] * 3,
        out_shape=[jax.ShapeDtypeStruct(w.shape, F32)] * 3,
        compiler_params=_cp(("parallel", "parallel"), VMEM_BIG))(w, g, m, v)


def _add_cores(g, from_sib, name):
    n, rows, cols = g.shape
    half = rows // 2
    tr = _tile(half, (256, 240))
    nt = half // tr

    def body(lo_ref, hi_ref, s_ref, o_ref):
        mine = jnp.where(lax.axis_index("c") == 0, lo_ref[0], hi_ref[0])
        o_ref[0] = (mine.astype(F32) + s_ref[0].astype(F32)).astype(BF16)

    return pl.pallas_call(
        body, name=name, grid=(n, nt),
        in_specs=[pl.BlockSpec((1, tr, cols), lambda j, i: (j, i, 0)),
                  pl.BlockSpec((1, tr, cols), lambda j, i: (j, nt + i, 0)),
                  pl.BlockSpec((1, tr, cols), lambda j, i: (j, i, 0))],
        out_specs=pl.BlockSpec((1, tr, cols), lambda j, i: (j, i, 0)),
        out_shape=jax.ShapeDtypeStruct((n, half, cols), BF16),
        compiler_params=_cp(("parallel", "parallel"), VMEM_BIG))(g, g, from_sib)


def _add_chips(x, own, name):
    n, rows, cols = x.shape
    tr = _tile(rows, (256, 240))

    def body(x_ref, own_ref, o_ref):
        me = 2 * lax.axis_index("x") + lax.axis_index("y")
        v = [jnp.where(me == k, own_ref[...], x_ref[k]).astype(F32) for k in range(N_CHIPS)]
        o_ref[...] = ((v[0] + v[1]) + v[2]) + v[3]

    return pl.pallas_call(
        body, name=name, grid=(rows // tr,),
        in_specs=[pl.BlockSpec((n, tr, cols), lambda i: (0, i, 0)), pl.BlockSpec((tr, cols), lambda i: (i, 0))],
        out_specs=pl.BlockSpec((tr, cols), lambda i: (i, 0)),
        out_shape=jax.ShapeDtypeStruct((rows, cols), F32), compiler_params=_cp(("parallel",), VMEM_BIG))(x, own)


def _axes():
    return lax.axis_index("x"), lax.axis_index("y"), lax.axis_index("c")


def _other_chips(x, y):
    return [(1 - x, y), (x, 1 - y), (1 - x, 1 - y)]


ANY = pl.BlockSpec(memory_space=pl.ANY)


def _rcopy(src, dst, send_sems, recv_sems, k, to):
    return pltpu.make_async_remote_copy(src_ref=src, dst_ref=dst, send_sem=send_sems.at[k], recv_sem=recv_sems.at[k],
                                        device_id=to, device_id_type=MESH)


def _gather_weights(shards, meta):
    n = len(shards)

    def body(*refs):
        srcs, meta_ref = refs[:n], refs[n]
        outs, mout_ref = refs[n + 1:2 * n + 1], refs[2 * n + 1]
        send_sems, recv_sems = refs[2 * n + 2:]
        x, y, c = _axes()
        me = 2 * x + y
        sib = (x, y, 1 - c)
        chips = _other_chips(x, y)

        def half(t, chip_idx, cc):
            hr = shards[t].shape[0] // 2
            return outs[t].at[chip_idx, pl.ds(cc * hr, hr), :]

        first = []
        for j, (px, py) in enumerate(chips):
            for t in range(n):
                hr = shards[t].shape[0] // 2
                first.append(_rcopy(srcs[t].at[pl.ds(c * hr, hr), :], half(t, me, c), send_sems, recv_sems,
                                    3 * t + j, (px, py, c)))
            first.append(_rcopy(meta_ref, mout_ref.at[me], send_sems, recv_sems, 3 * n + j, (px, py, c)))
        for cp in first:
            cp.start()
        passed = []
        for j, (px, py) in enumerate(chips):
            src_chip = 2 * px + py
            for t in range(n):
                _rcopy(half(t, src_chip, c), half(t, src_chip, c), send_sems, recv_sems, 3 * t + j, sib).wait_recv()
                fwd = _rcopy(half(t, src_chip, c), half(t, src_chip, c), send_sems, recv_sems, 3 * (n + 1 + t) + j, sib)
                fwd.start()
                passed.append(fwd)
            _rcopy(mout_ref.at[src_chip], mout_ref.at[src_chip], send_sems, recv_sems, 3 * n + j, sib).wait_recv()
        for j, (px, py) in enumerate(chips):
            src_chip = 2 * px + py
            for t in range(n):
                _rcopy(half(t, src_chip, 1 - c), half(t, src_chip, 1 - c), send_sems, recv_sems,
                       3 * (n + 1 + t) + j, sib).wait_recv()
        for cp in first + passed:
            cp.wait_send()

    nsem = 3 * (2 * n + 1)
    return pl.pallas_call(
        body, name="gather_weights", in_specs=[ANY] * (n + 1), out_specs=[ANY] * (n + 1),
        out_shape=[jax.ShapeDtypeStruct((N_CHIPS,) + s.shape, s.dtype) for s in shards]
        + [jax.ShapeDtypeStruct((N_CHIPS,) + meta.shape, meta.dtype)],
        scratch_shapes=[pltpu.SemaphoreType.DMA((nsem,)), pltpu.SemaphoreType.DMA((nsem,))])(*shards, meta)


def _gather_late(shard):
    rows, cols = shard.shape
    hr = rows // 2
    src = jax.new_ref(shard, memory_space=pltpu.MemorySpace.HBM)
    out = jax.empty_ref(jax.ShapeDtypeStruct((N_CHIPS, rows, cols), shard.dtype), memory_space=pltpu.MemorySpace.HBM)

    @pl.kernel(mesh=plsc.ScalarSubcoreMesh(axis_name="seq", num_cores=1), name="gather_late",
               scratch_types=(pltpu.SemaphoreType.DMA((6,)), pltpu.SemaphoreType.DMA((6,))),
               compiler_params=pltpu.CompilerParams(collective_id=1))
    def launch(send_sems, recv_sems):
        x, y, c = _axes()
        me = 2 * x + y
        sib = (x, y, 1 - c)
        chips = _other_chips(x, y)
        barrier = pltpu.get_barrier_semaphore()
        for px, py in chips:
            pl.semaphore_signal(barrier, inc=1, device_id=(px, py, c), device_id_type=MESH)
        pl.semaphore_signal(barrier, inc=1, device_id=sib, device_id_type=MESH)
        pl.semaphore_wait(barrier, 4)

        def half(chip_idx, cc):
            return out.at[chip_idx, pl.ds(cc * hr, hr), :]

        first = [_rcopy(src.at[pl.ds(c * hr, hr), :], half(me, c), send_sems, recv_sems, j, (px, py, c))
                 for j, (px, py) in enumerate(chips)]
        for cp in first:
            cp.start()
        passed = []
        for j, (px, py) in enumerate(chips):
            land = half(2 * px + py, c)
            _rcopy(land, land, send_sems, recv_sems, j, sib).wait_recv()
            fwd = _rcopy(land, land, send_sems, recv_sems, 3 + j, sib)
            fwd.start()
            passed.append(fwd)
        for j, (px, py) in enumerate(chips):
            land = half(2 * px + py, 1 - c)
            _rcopy(land, land, send_sems, recv_sems, 3 + j, sib).wait_recv()
        for cp in first + passed:
            cp.wait_send()

    launch()
    return out[...]


def _swap_halves(gs):
    n = len(gs)
    ncopies = sum(g.shape[0] for g in gs)

    def body(*refs):
        srcs, outs = refs[:n], refs[n:2 * n]
        send_sems, recv_sems = refs[2 * n:]
        x, y, c = _axes()
        cps = []
        for t in range(n):
            hr = gs[t].shape[1] // 2
            for j in range(gs[t].shape[0]):
                cps.append(_rcopy(srcs[t].at[j, pl.ds((1 - c) * hr, hr), :], outs[t].at[j], send_sems, recv_sems,
                                  len(cps), (x, y, 1 - c)))
        for cp in cps:
            cp.start()
        for cp in cps:
            cp.wait()

    return pl.pallas_call(
        body, name="swap_halves", in_specs=[ANY] * n, out_specs=[ANY] * n,
        out_shape=[jax.ShapeDtypeStruct((g.shape[0], g.shape[1] // 2, g.shape[2]), g.dtype) for g in gs],
        scratch_shapes=[pltpu.SemaphoreType.DMA((ncopies,)), pltpu.SemaphoreType.DMA((ncopies,))])(*gs)


def _scatter_chips(parts):
    n = len(parts)
    srcs = [jax.new_ref(p, memory_space=pltpu.MemorySpace.HBM) for p in parts]
    outs = [jax.empty_ref(jax.ShapeDtypeStruct(p.shape, p.dtype), memory_space=pltpu.MemorySpace.HBM) for p in parts]

    @pl.kernel(mesh=plsc.ScalarSubcoreMesh(axis_name="seq", num_cores=1), name="scatter_chips",
               scratch_types=(pltpu.SemaphoreType.DMA((3 * n,)), pltpu.SemaphoreType.DMA((3 * n,))),
               compiler_params=pltpu.CompilerParams(collective_id=0))
    def launch(send_sems, recv_sems):
        x, y, c = _axes()
        me = 2 * x + y
        chips = _other_chips(x, y)
        barrier = pltpu.get_barrier_semaphore()
        for px, py in chips:
            pl.semaphore_signal(barrier, inc=1, device_id=(px, py, c), device_id_type=MESH)
        pl.semaphore_wait(barrier, 3)
        cps = []
        for j, (px, py) in enumerate(chips):
            for t in range(n):
                cps.append(_rcopy(srcs[t].at[2 * px + py], outs[t].at[me], send_sems, recv_sems, 3 * t + j,
                                  (px, py, c)))
        for cp in cps:
            cp.start()
        for cp in cps:
            cp.wait()

    launch()
    return [o[...] for o in outs]


def _swap_reduced(rs):
    n = len(rs)

    def body(*refs):
        srcs, outs = refs[:n], refs[n:2 * n]
        send_sems, recv_sems = refs[2 * n:]
        x, y, c = _axes()
        cps = [_rcopy(srcs[t], outs[t], send_sems, recv_sems, t, (x, y, 1 - c)) for t in range(n)]
        for cp in cps:
            cp.start()
        for cp in cps:
            cp.wait()

    return pl.pallas_call(
        body, name="swap_reduced", in_specs=[ANY] * n, out_specs=[ANY] * n,
        out_shape=[jax.ShapeDtypeStruct(r.shape, r.dtype) for r in rs],
        scratch_shapes=[pltpu.SemaphoreType.DMA((n,)), pltpu.SemaphoreType.DMA((n,))])(*rs)


SMALL_ROWS = 24 + 128


def _allreduce_small(vec):
    def body(v_ref, out_ref, slots, send_sems, recv_sems):
        x, y, c = _axes()
        me = 4 * x + 2 * y + c
        slots[me] = v_ref[...]
        cps = []
        for k in range(1, 8):
            kx, ky, kc = (k >> 2) & 1, (k >> 1) & 1, k & 1
            peer = (1 - x if kx else x, 1 - y if ky else y, 1 - c if kc else c)
            cps.append(_rcopy(v_ref, slots.at[me], send_sems, recv_sems, k - 1, peer))
        for cp in cps:
            cp.start()
        for cp in cps:
            cp.wait()
        tot = slots[0]
        for k in range(1, 8):
            tot = tot + slots[k]
        out_ref[...] = tot

    return pl.pallas_call(
        body, name="allreduce_small",
        in_specs=[pl.BlockSpec(memory_space=pltpu.VMEM)], out_specs=pl.BlockSpec(memory_space=pltpu.VMEM),
        out_shape=jax.ShapeDtypeStruct((SMALL_ROWS, 128), F32),
        scratch_shapes=[pltpu.VMEM((8, SMALL_ROWS, 128), F32), pltpu.SemaphoreType.DMA((7,)),
                        pltpu.SemaphoreType.DMA((7,))])(vec)


def _pack_p2(w_uq, w_ukv, w_br_mla, w_br_fox, w_out, dtype):
    parts = [w_uq.reshape(96, D_MODEL), w_ukv.reshape(64, D_MODEL), w_br_mla, w_br_fox, w_out]
    return jnp.concatenate([p.astype(dtype) for p in parts], axis=0)


def _unpack_p2(pk):
    return pk[0:96].reshape(256, 384), pk[96:160].reshape(128, 512), pk[160:416], pk[416:672], pk[672:928]


def _uq_arrange(w):
    w3 = w.reshape(256, HEADS, 96)
    nope = w3[:, :, :64].reshape(256, PAIRS, 128)
    pe = w3[:, :, 64:].reshape(256, PAIRS, 64)
    return jnp.concatenate([nope, pe, jnp.zeros((256, PAIRS, 64), w.dtype)], axis=2).reshape(256, PAIRS * 256)


def _uq_restore(g):
    g3 = g.reshape(256, PAIRS, 256)
    nope = g3[:, :, :128].reshape(256, HEADS, 64)
    pe = g3[:, :, 128:192].reshape(256, HEADS, 32)
    return jnp.concatenate([nope, pe], axis=2).reshape(256, HEADS * 96)


def _ukv_arrange(w):
    w3 = w.reshape(128, HEADS, 128)
    return jnp.concatenate([w3[:, :, :64].reshape(128, 1024), w3[:, :, 64:].reshape(128, 1024)], axis=1)


def _ukv_restore(g):
    kn = g[:, :1024].reshape(128, HEADS, 64)
    vv = g[:, 1024:].reshape(128, HEADS, 64)
    return jnp.concatenate([kn, vv], axis=2).reshape(128, HEADS * 128)


def _rope_tables(lp):
    r = np.arange(lp)
    pos = np.where(r < N_META, r, np.where(r >= PAD, r - PAD + N_META, 0)).astype(np.float32)
    half = MLA_ROPE // 2
    inv_freq = np.float32(ROPE_THETA) ** (-np.arange(half, dtype=np.float32) / np.float32(half))
    ang = (pos[:, None] * inv_freq[None, :]).astype(np.float32)
    cos, sin = np.cos(ang).astype(np.float32), np.sin(ang).astype(np.float32)
    one, zero = np.ones((lp, 64), np.float32), np.zeros((lp, 64), np.float32)
    return (jnp.asarray(np.concatenate([cos, cos, cos, cos, one], axis=1)),
            jnp.asarray(np.concatenate([-sin, sin, -sin, sin, zero], axis=1)))


def _pad_lanes(v, n=128):
    return jnp.pad(v, ((0, 0), (0, n - v.shape[1])))


def _in_cols(slabs, a, b):
    out = []
    for j in range(N_CHIPS):
        lo, hi = max(a, W_IN_SHARD * j), min(b, W_IN_SHARD * (j + 1))
        if lo < hi:
            out.append(slabs[j][:, lo - W_IN_SHARD * j:hi - W_IN_SHARD * j])
    return out


def _local_step(x2, tgt2, meta_f, w_small, w_attn, w_gate, w_uq_f, w_ukv_f, w_bm, w_bf, w_o, pre_norm_g,
                post_norm_g, mla_q_norm_g, mla_kv_norm_g, fox_forget_b, start_exchange=None):
    s_rows = x2.shape[0]
    lp = PAD + s_rows
    w_uq_a = _uq_arrange(w_uq_f)
    w_ukv_a = _ukv_arrange(w_ukv_f)

    ctab, stab = _rope_tables(lp)
    ii = jnp.arange(BLK)
    tri_lo = (ii[:, None] >= ii[None, :]).astype(BF16)
    tri_up = (ii[:, None] <= ii[None, :]).astype(BF16)
    fb128 = _pad_lanes(fox_forget_b)

    u = _rms_pre(x2, meta_f, pre_norm_g)
    small = _mm(u, w_small, mode="nn", out_dtype=F32, name="proj_small")
    attn = _mm(u, w_attn, mode="nn", out_dtype=BF16, name="proj_attn")
    gate = _mm(u, w_gate, mode="nn", out_dtype=BF16, name="proj_gate")
    qn, kvn, kr, ncum = _small_prep(small, mla_q_norm_g, mla_kv_norm_g, fb128, ctab, stab, tri_lo)
    qcat = _mm(qn, w_uq_a, mode="nn", out_dtype=BF16, name="mla_q", epilogue=_rope_pairs, row_ins=(ctab, stab))
    kv = _mm(kvn, w_ukv_a, mode="nn", out_dtype=BF16, name="mla_kv")
    nbrep = jnp.broadcast_to(ncum[:, :HEADS].T[:, :, None], (HEADS, lp, LANES))

    mla_cols = dict(qcol=0, kcol=lambda p: p, vcol=lambda p: PAIRS + p)
    fox_cols = dict(qcol=0, kcol=lambda p: PAIRS + p, vcol=lambda p: 2 * PAIRS + p)
    o_mla, lse_mla = _attn_fwd(qcat, kv, kv, kr=kr, scale=MLA_SCALE, name="mla_fwd", **mla_cols)
    o_fox, lse_fox = _attn_fwd(attn, attn, attn, nbrep=nbrep, scale=FOX_SCALE, name="fox_fwd", **fox_cols)

    a_mla, a_fox = _gate_fwd(o_mla, o_fox, gate)
    y_mla = _mm(a_mla, w_bm, mode="nn", out_dtype=BF16, name="br_mla")
    y_fox = _mm(a_fox, w_bf, mode="nn", out_dtype=BF16, name="br_fox")
    mg = _merge_fwd(gate, y_mla, y_fox)
    mixed = _mm(mg, w_o, mode="nn", out_dtype=F32, name="out_proj")
    dmixed, dy, loss_p, dg_post = _tail(x2, mixed, tgt2, post_norm_g)

    d_w_out = _mm(mg, dmixed, mode="tn", out_dtype=F32, name="d_w_out")
    dm = _mm(dmixed, w_o, mode="nt", out_dtype=BF16, name="d_merge")
    dy_mla, dy_fox, dgate_ab = _merge_bwd(dm, gate, y_mla, y_fox)
    d_w_bm = _mm(a_mla, dy_mla, mode="tn", out_dtype=F32, name="d_w_br_mla")
    d_w_bf = _mm(a_fox, dy_fox, mode="tn", out_dtype=F32, name="d_w_br_fox")
    da_mla = _mm(dy_mla, w_bm, mode="nt", out_dtype=BF16, name="d_a_mla")
    da_fox = _mm(dy_fox, w_bf, mode="nt", out_dtype=BF16, name="d_a_fox")
    do_mla, do_fox, dgate_z, dl_mla, dl_fox = _gate_bwd(da_mla, da_fox, o_mla, o_fox, gate)
    dl_mla, dl_fox = (d[:, :HEADS].T.reshape(PAIRS, 2, lp) for d in (dl_mla, dl_fox))

    dq_a, dkn, dvm, dkr = _attn_bwd(qcat, kv, kv, do_mla, dl_mla, lse_mla, kr=kr, rtabs=(ctab, stab),
                                    scale=MLA_SCALE, name="mla_bwd", **mla_cols)
    dfq, dfk, dfv, dcol, drow = _attn_bwd(attn, attn, attn, do_fox, dl_fox, lse_fox, nbrep=nbrep, scale=FOX_SCALE,
                                          name="fox_bwd", **fox_cols)

    d_w_uq_a = _mm(qn, dq_a, mode="tn", out_dtype=F32, name="d_w_uq")
    dqn = _mm(dq_a, w_uq_a, mode="nt", out_dtype=F32, name="d_qn")
    d_w_ukv_a = jnp.concatenate([_mm(kvn, dkn, mode="tn", out_dtype=F32, name="d_w_uk"),
                                 _mm(kvn, dvm, mode="tn", out_dtype=F32, name="d_w_uv")], axis=1)
    dkvn = _mm(dkn, w_ukv_a[:, :1024], mode="nt", out_dtype=F32, name="d_kvn_k")
    dkvn = _mm(dvm, w_ukv_a[:, 1024:], mode="nt", out_dtype=F32, name="d_kvn_v", acc=dkvn)
    dsmall, dg_q, dg_kv, dfb = _small_bwd(small, dqn, dkvn, dkr, dcol, drow, mla_q_norm_g, mla_kv_norm_g,
                                          fb128, ctab, stab, tri_up)

    dw_small = _mm(u, dsmall, mode="tn", out_dtype=BF16, name="d_w_small")
    dw_fq = _mm(u, dfq, mode="tn", out_dtype=BF16, name="d_w_fq")
    dw_fk = _mm(u, dfk, mode="tn", out_dtype=BF16, name="d_w_fk")
    dw_fv = _mm(u, dfv, mode="tn", out_dtype=BF16, name="d_w_fv")
    dw_z = _mm(u, dgate_z, mode="tn", out_dtype=BF16, name="d_w_z")
    dw_g = _mm(u, dgate_ab, mode="tn", out_dtype=BF16, name="d_w_g")
    d_w_in = (dw_small, dw_z, dw_fq, dw_fk, dw_fv, dw_g)
    d_w_uq = _uq_restore(d_w_uq_a)
    d_w_ukv = _ukv_restore(d_w_ukv_a)
    token = start_exchange(d_w_in, d_w_uq, d_w_ukv, d_w_bm, d_w_bf, d_w_out) if start_exchange else None
    du = _mm_sum_nt([(dsmall, w_small), (dfq, w_attn[:, 0:1024]), (dfk, w_attn[:, 1024:2048]),
                     (dfv, w_attn[:, 2048:3072]), (dgate_z, w_gate[:, 0:2048]), (dgate_ab, w_gate[:, 2048:4096])],
                    name="d_u", after=token)
    dx, dmeta, dg_pre = _pre_bwd(du, x2, meta_f, dy, pre_norm_g)
    return (loss_p, dx, dmeta, d_w_in, d_w_uq, d_w_ukv, d_w_bm, d_w_bf, d_w_out, dg_pre, dg_post, dg_q, dg_kv, dfb)


def _w_in_slabs(pieces):
    dw_small, dw_z, dw_fq, dw_fk, dw_fv, dw_g = pieces
    runs = [(dw_small[:, 0:416], C_CQ), (dw_z[:, 0:1024], C_ZMLA), (dw_fq, C_FQ), (dw_fk, C_FK), (dw_fv, C_FV),
            (dw_small[:, 512:528], C_FL), (dw_z[:, 1024:2048], C_ZFOX), (dw_g, C_GA)]
    slabs = []
    for j in range(N_CHIPS):
        lo, hi = W_IN_SHARD * j, W_IN_SHARD * (j + 1)
        cols = [a[:, max(lo, c0) - c0:min(hi, c0 + a.shape[1]) - c0] for a, c0 in runs
                if max(lo, c0) < min(hi, c0 + a.shape[1])]
        slabs.append(jnp.concatenate(cols, axis=1))
    return jnp.stack(slabs, axis=0)


def kernel(x, meta_tokens, pre_norm_g, w_in, fox_forget_b, mla_q_norm_g, mla_kv_norm_g, w_uq, w_ukv, w_br_mla, w_br_fox, w_out, post_norm_g, loss_target, m_meta_tokens, m_pre_norm_g, m_w_in, m_fox_forget_b, m_mla_q_norm_g, m_mla_kv_norm_g, m_w_uq, m_w_ukv, m_w_br_mla, m_w_br_fox, m_w_out, m_post_norm_g, v_meta_tokens, v_pre_norm_g, v_w_in, v_fox_forget_b, v_mla_q_norm_g, v_mla_kv_norm_g, v_w_uq, v_w_ukv, v_w_br_mla, v_w_br_fox, v_w_out, v_post_norm_g):
    me = 2 * lax.axis_index("x") + lax.axis_index("y")
    core = lax.axis_index("c")
    w_in_b = w_in.astype(BF16).reshape(D_MODEL, W_IN_SHARD)
    p2 = _pack_p2(w_uq[0], w_ukv[0], w_br_mla[0], w_br_fox[0], w_out[0], BF16)
    w_in_g, meta_g = _gather_weights([w_in_b], meta_tokens)
    p2_g = _gather_late(lax.optimization_barrier((p2, w_in_g))[0])
    slabs = [jnp.where(me == j, w_in_b, w_in_g[j]) for j in range(N_CHIPS)]
    chip = lax.broadcasted_iota(jnp.int32, (N_CHIPS, 1, 1), 0)
    p2_all = jnp.where(chip == me, p2[None], p2_g)
    w_uq_f = p2_all[:, 0:96].reshape(N_CHIPS, 256, 384).transpose(1, 0, 2).reshape(256, 1536)
    w_ukv_f = p2_all[:, 96:160].reshape(N_CHIPS, 128, 512).transpose(1, 0, 2).reshape(128, 2048)
    w_bm, w_bf, w_o = (p2_all[:, lo:lo + 256].reshape(D_MODEL, D_MODEL) for lo in (160, 416, 672))
    meta_f = jnp.where(chip == me, meta_tokens[None], meta_g).transpose(1, 0, 2).reshape(N_META, D_MODEL)
    kpe = _in_cols(slabs, C_KPE, C_ZMLA)
    w_small = jnp.concatenate(_in_cols(slabs, C_CQ, C_KPE) + kpe + kpe + [jnp.zeros((D_MODEL, 64), BF16)]
                              + _in_cols(slabs, C_FL, C_ZFOX) + [jnp.zeros((D_MODEL, 112), BF16)], axis=1)
    w_attn = jnp.concatenate(_in_cols(slabs, C_FQ, C_FL), axis=1)
    w_gate = jnp.concatenate(_in_cols(slabs, C_ZMLA, C_FQ) + _in_cols(slabs, C_ZFOX, C_END), axis=1)

    exchange = {}

    def start_exchange(d_w_in, d_w_uq, d_w_ukv, d_w_bm, d_w_bf, d_w_out):
        g2 = jnp.concatenate(
            [d_w_uq.reshape(256, N_CHIPS, 384).transpose(1, 0, 2).reshape(N_CHIPS, 96, D_MODEL),
             d_w_ukv.reshape(128, N_CHIPS, 512).transpose(1, 0, 2).reshape(N_CHIPS, 64, D_MODEL)]
            + [g.reshape(N_CHIPS, 256, D_MODEL) for g in (d_w_bm, d_w_bf, d_w_out)], axis=1)
        pieces = [p[None] for p in d_w_in]
        from_sib = _swap_halves(pieces + [g2])
        halves = [_add_cores(p, s, "add_cores_" + nm)[0]
                  for p, s, nm in zip(pieces, from_sib, ("small", "z", "fq", "fk", "fv", "g"))]
        parts = [_w_in_slabs(halves), _add_cores(g2, from_sib[-1], "add_cores_rest")]
        exchange.update(parts=parts, landed=_scatter_chips(parts))
        return parts[0][0, 0:16, 0:LANES]

    (loss_p, dx, dmeta, _, _, _, _, _, _, dg_pre, dg_post, dg_q, dg_kv,
     dfb) = _local_step(x[0], loss_target[0], meta_f, w_small, w_attn, w_gate, w_uq_f, w_ukv_f, w_bm, w_bf, w_o,
                        pre_norm_g, post_norm_g, mla_q_norm_g, mla_kv_norm_g, fox_forget_b, start_exchange)

    mine = [_add_chips(l, lax.dynamic_index_in_dim(p, me, 0, keepdims=False), nm)
            for l, p, nm in zip(exchange["landed"], exchange["parts"], ("add_chips_w_in", "add_chips_rest"))]
    theirs = _swap_reduced(mine)
    g_w_in, g_p2 = [jnp.concatenate([jnp.where(core == 0, a, b), jnp.where(core == 0, b, a)], axis=0)
                    for a, b in zip(mine, theirs)]
    g_w_uq, g_w_ukv, g_w_bm, g_w_bf, g_w_out = _unpack_p2(g_p2)
    g_w_in = g_w_in[None]

    vec = jnp.concatenate([dg_pre.reshape(8, 128), dg_post.reshape(8, 128), dg_q.reshape(2, 128), dg_kv,
                           dfb, _pad_lanes(loss_p), jnp.zeros((3, 128), F32), dmeta.reshape(128, 128)], axis=0)
    tot = _allreduce_small(vec)
    loss = tot[20, 0]
    g_meta = lax.dynamic_slice_in_dim(tot[24:].reshape(N_META, D_MODEL), 256 * me, 256, axis=1)

    def small_pack(pre, post, gq_, gkv_, fb_):
        return jnp.concatenate([pre.reshape(8, 128), post.reshape(8, 128), gq_.reshape(2, 128), gkv_,
                                _pad_lanes(fb_), jnp.zeros((4, 128), F32)], axis=0)

    def small_unpack(t):
        return (t[0:8].reshape(1, 1024), t[8:16].reshape(1, 1024), t[16:18].reshape(1, 256), t[18:19],
                t[19:20, 0:HEADS])

    g_small = jnp.concatenate([tot[0:20], jnp.zeros((4, 128), F32)], axis=0)
    sm = _adamw(small_pack(pre_norm_g, post_norm_g, mla_q_norm_g, mla_kv_norm_g, fox_forget_b), g_small,
                small_pack(m_pre_norm_g, m_post_norm_g, m_mla_q_norm_g, m_mla_kv_norm_g, m_fox_forget_b),
                small_pack(v_pre_norm_g, v_post_norm_g, v_mla_q_norm_g, v_mla_kv_norm_g, v_fox_forget_b),
                "adamw_small")
    g_pre, g_post, g_q, g_kv, g_fb = small_unpack(g_small)
    (d_pre, d_post, d_q, d_kv, d_fb), (nm_pre, nm_post, nm_q, nm_kv, nm_fb), (nv_pre, nv_post, nv_q, nv_kv, nv_fb) = (
        small_unpack(t) for t in sm)

    d_meta, nm_meta, nv_meta = _adamw(meta_tokens, g_meta, m_meta_tokens, v_meta_tokens, "adamw_meta")
    d_win, nm_win, nv_win = (t.T[None] for t in _adamw(w_in[0].T, g_w_in[0].T, m_w_in[0].T, v_w_in[0].T,
                                                       "adamw_w_in"))
    d_wuq, nm_wuq, nv_wuq = _adamw(w_uq[0], g_w_uq, m_w_uq[0], v_w_uq[0], "adamw_w_uq")
    d_wukv, nm_wukv, nv_wukv = _adamw(w_ukv[0], g_w_ukv, m_w_ukv[0], v_w_ukv[0], "adamw_w_ukv")
    d_wbm, nm_wbm, nv_wbm = _adamw(w_br_mla[0], g_w_bm, m_w_br_mla[0], v_w_br_mla[0], "adamw_w_br_mla")
    d_wbf, nm_wbf, nv_wbf = _adamw(w_br_fox[0], g_w_bf, m_w_br_fox[0], v_w_br_fox[0], "adamw_w_br_fox")
    d_wo, nm_wo, nv_wo = _adamw(w_out[0], g_w_out, m_w_out[0], v_w_out[0], "adamw_w_out")

    def group(meta_, pre, win, fb_, q_, kv_, wuq, wukv, wbm, wbf, wo, post):
        return (meta_, pre, win, fb_, q_, kv_, wuq[None], wukv[None], wbm[None], wbf[None], wo[None], post)

    grads = group(g_meta, g_pre, g_w_in, g_fb, g_q, g_kv, g_w_uq, g_w_ukv, g_w_bm, g_w_bf, g_w_out, g_post)
    deltas = group(d_meta, d_pre, d_win, d_fb, d_q, d_kv, d_wuq, d_wukv, d_wbm, d_wbf, d_wo, d_post)
    new_m = group(nm_meta, nm_pre, nm_win, nm_fb, nm_q, nm_kv, nm_wuq, nm_wukv, nm_wbm, nm_wbf, nm_wo, nm_post)
    new_v = group(nv_meta, nv_pre, nv_win, nv_fb, nv_q, nv_kv, nv_wuq, nv_wukv, nv_wbm, nv_wbf, nv_wo, nv_post)
    return (loss, dx[None], *grads, *deltas, *new_m, *new_v)
```

```python
import math

import jax
import jax.numpy as jnp
import numpy as np
from jax import lax
from jax.experimental import pallas as pl
from jax.experimental.pallas import tpu as pltpu
from jax.experimental.pallas import tpu_sc as plsc

F32 = jnp.float32
BF16 = jnp.bfloat16

D_MODEL = 1024
N_META = 16
RMS_EPS = 1e-6
HEADS = 16
PAIRS = HEADS // 2
HEAD_DIM = 64
LANES = 128
MLA_ROPE = 32
MLA_SCALE = 1.0 / math.sqrt(64 + 32)
FOX_SCALE = 1.0 / math.sqrt(64)
ROPE_THETA = 10000.0

PAD = 256
BLK = 256
QB = 512
UNROLL = 4
NEG = -1e30

C_CQ, C_CKV, C_KPE, C_ZMLA, C_FQ, C_FK, C_FV, C_FL, C_ZFOX, C_GA, C_GB, C_END = (
    0, 256, 384, 416, 1440, 2464, 3488, 4512, 4528, 5552, 6576, 7600)
SMALL_W = 640
W_IN_SHARD = 1900

P2_ROWS = 928
N_CHIPS = 4

ADAM_LR = 0.001
ADAM_B1 = 0.9
ADAM_B2 = 0.999
ADAM_EPS = 1e-08
ADAM_WD = 0.01
ADAM_STEP = 10

VMEM_BIG = 56 * 1024 * 1024
MM_VMEM_BUDGET = 44 * 1024 * 1024
MESH = pl.DeviceIdType.MESH


def _cp(dims, vmem=None):
    return pltpu.CompilerParams(dimension_semantics=dims, vmem_limit_bytes=vmem)


def _dot(a, b, ca, cb):
    return lax.dot_general(a, b, (((ca,), (cb,)), ((), ())), preferred_element_type=F32)


def _sigmoid(x):
    return 1.0 / (1.0 + jnp.exp(-x))


def _tile(n, cands):
    for c in cands:
        if n % c == 0:
            return c
    return n


def _mm(a, b, *, mode, out_dtype, name, acc=None, epilogue=None, row_ins=(), after=None):
    if mode == "nn":
        (M, K), N = a.shape, b.shape[1]
    elif mode == "nt":
        (M, K), N = a.shape, b.shape[0]
    else:
        (K, M), N = a.shape, b.shape[1]
    tm = _tile(M, (1088, 1024)) if M > 1024 else M
    tn = _tile(N, (1024,)) if N > 1024 else N
    nk = 1
    while True:
        tk = K // nk
        need = 2 * tk * (tm * a.dtype.itemsize + tn * b.dtype.itemsize) + tm * tn * (
            2 * jnp.dtype(out_dtype).itemsize + (8 if acc is not None else 0) + (4 if nk > 1 else 0))
        if need <= MM_VMEM_BUDGET or (tk // 2) % (16 if mode == "tn" else LANES) or tk <= 512:
            break
        nk *= 2
    ca, cb = {"nn": (1, 0), "nt": (1, 1), "tn": (0, 0)}[mode]
    a_spec = (pl.BlockSpec((tk, tm), lambda j, i, k: (k, i)) if mode == "tn"
              else pl.BlockSpec((tm, tk), lambda j, i, k: (i, k)))
    b_spec = (pl.BlockSpec((tn, tk), lambda j, i, k: (j, k)) if mode == "nt"
              else pl.BlockSpec((tk, tn), lambda j, i, k: (k, j)))
    o_spec = pl.BlockSpec((tm, tn), lambda j, i, k: (i, j))
    has_acc = acc is not None

    nrow = len(row_ins)

    def body(*refs):
        a_ref, b_ref = refs[0], refs[1]
        acc_ref = refs[2] if has_acc else None
        rows = refs[2 + has_acc:2 + has_acc + nrow]
        o_ref = refs[2 + has_acc + nrow + (after is not None)]

        def store(tile):
            if epilogue is not None:
                tile = epilogue(tile, *[r[...] for r in rows])
            o_ref[...] = tile.astype(out_dtype)

        part = _dot(a_ref[...].astype(BF16), b_ref[...].astype(BF16), ca, cb)
        if nk == 1:
            store(part + acc_ref[...] if has_acc else part)
        else:
            sc = refs[-1]
            k = pl.program_id(2)

            @pl.when(k == 0)
            def _():
                sc[...] = part + acc_ref[...] if has_acc else part

            @pl.when(k > 0)
            def _():
                sc[...] += part

            @pl.when(k == nk - 1)
            def _():
                store(sc[...])

    ins = [a, b] + ([acc] if has_acc else []) + list(row_ins)
    in_specs = ([a_spec, b_spec] + ([o_spec] if has_acc else [])
                + [pl.BlockSpec((tm, r.shape[1]), lambda j, i, k: (i, 0)) for r in row_ins])
    if after is not None:
        ins.append(after)
        in_specs.append(pl.BlockSpec(after.shape, lambda j, i, k: (0,) * after.ndim))
    return pl.pallas_call(
        body, name=name, grid=(N // tn, M // tm, nk), in_specs=in_specs, out_specs=o_spec,
        out_shape=jax.ShapeDtypeStruct((M, N), out_dtype),
        scratch_shapes=[pltpu.VMEM((tm, tn), F32)] if nk > 1 else [],
        compiler_params=_cp(("parallel", "parallel", "arbitrary"), VMEM_BIG))(*ins)


def _mm_sum_nt(pairs, *, name, after=None):
    n = len(pairs)
    M, N = pairs[0][0].shape[0], pairs[0][1].shape[0]
    tm = _tile(M, (272,))

    def body(*refs):
        o_ref = refs[2 * n + (after is not None)]
        tot = _dot(refs[0][...].astype(BF16), refs[n][...].astype(BF16), 1, 1)
        for i in range(1, n):
            tot = tot + _dot(refs[i][...].astype(BF16), refs[n + i][...].astype(BF16), 1, 1)
        o_ref[...] = tot

    ins = [a for a, _ in pairs] + [b for _, b in pairs]
    in_specs = ([pl.BlockSpec((tm, a.shape[1]), lambda i: (i, 0)) for a, _ in pairs]
                + [pl.BlockSpec(b.shape, lambda i: (0, 0)) for _, b in pairs])
    if after is not None:
        ins.append(after)
        in_specs.append(pl.BlockSpec(after.shape, lambda i: (0,) * after.ndim))
    return pl.pallas_call(
        body, name=name, grid=(M // tm,), in_specs=in_specs, out_specs=pl.BlockSpec((tm, N), lambda i: (i, 0)),
        out_shape=jax.ShapeDtypeStruct((M, N), F32), compiler_params=_cp(("parallel",), VMEM_BIG))(*ins)


def _row(w):
    return pl.BlockSpec((BLK, w), lambda i: (i, 0))


def _rowc(w, c):
    return pl.BlockSpec((BLK, w), lambda i: (i, c))


def _full(shape):
    return pl.BlockSpec(shape, lambda i: tuple(0 for _ in shape))


def _rope(x, c, s):
    lane = lax.broadcasted_iota(jnp.int32, x.shape, 1)
    is_x1 = ((lane >> 4) & 1) == 0
    partner = jnp.where(is_x1, pltpu.roll(x, LANES - 16, 1), pltpu.roll(x, 16, 1))
    return x * c + partner * s


def _row_valid(i):
    rows = i * BLK + lax.broadcasted_iota(jnp.int32, (BLK, 1), 0)
    return (rows < N_META) | (rows >= PAD)


def _shift_rows(w):
    return pl.BlockSpec((BLK, w), lambda i: (jnp.maximum(i - 1, 0), 0))


def _h_block(i, x_ref, meta_ref):
    head = jnp.concatenate([meta_ref[...], jnp.zeros((BLK - N_META, D_MODEL), F32)], axis=0)
    return jnp.where(i == 0, head, x_ref[...])


def _rms_pre(x2, meta, g):
    lp = PAD + x2.shape[0]

    def body(x_ref, meta_ref, g_ref, u_ref):
        hv = _h_block(pl.program_id(0), x_ref, meta_ref)
        r = lax.rsqrt(jnp.mean(hv * hv, axis=-1, keepdims=True) + RMS_EPS)
        u_ref[...] = (hv * r * g_ref[...]).astype(BF16)

    return pl.pallas_call(
        body, name="rms_pre", grid=(lp // BLK,),
        in_specs=[_shift_rows(D_MODEL), _full((N_META, D_MODEL)), _full((1, D_MODEL))], out_specs=_row(D_MODEL),
        out_shape=jax.ShapeDtypeStruct((lp, D_MODEL), BF16),
        compiler_params=_cp(("parallel",)))(x2, meta, g)


def _split3(x):
    hi = x.astype(BF16)
    r1 = x - hi.astype(F32)
    mid = r1.astype(BF16)
    lo = (r1 - mid.astype(F32)).astype(BF16)
    return hi, mid, lo


def _small_prep(small, gq, gkv, fb, ctab, stab, tri):
    lp = small.shape[0]

    def body(sm_ref, gq_ref, gkv_ref, fb_ref, c_ref, s_ref, tri_ref, qn_ref, kvn_ref, kr_ref, ncum_ref, carry):
        i = pl.program_id(0)

        @pl.when(i == 0)
        def _():
            carry[...] = jnp.zeros_like(carry)

        cq = sm_ref[:, 0:256]
        r = lax.rsqrt(jnp.mean(cq * cq, axis=-1, keepdims=True) + RMS_EPS)
        qn_ref[...] = (cq * r * gq_ref[...]).astype(BF16)
        ckv = sm_ref[:, 256:384]
        r = lax.rsqrt(jnp.mean(ckv * ckv, axis=-1, keepdims=True) + RMS_EPS)
        kvn_ref[...] = (ckv * r * gkv_ref[...]).astype(BF16)
        kr_ref[...] = _rope(sm_ref[:, 384:512], c_ref[...], s_ref[...]).astype(BF16)
        fl = sm_ref[:, 512:640] + fb_ref[...]
        lf = jnp.minimum(fl, 0.0) - jnp.log(1.0 + jnp.exp(-jnp.abs(fl)))
        lf = jnp.where(_row_valid(i), lf, 0.0)
        hi, mid, lo = _split3(lf)
        t = tri_ref[...]
        cum = (_dot(t, hi, 1, 0) + _dot(t, mid, 1, 0)) + _dot(t, lo, 1, 0) + carry[...]
        ncum_ref[...] = -cum
        carry[...] = -ncum_ref[BLK - 1:BLK, :]

    return pl.pallas_call(
        body, name="small_prep", grid=(lp // BLK,),
        in_specs=[_row(SMALL_W), _full((1, 256)), _full((1, 128)), _full((1, 128)), _row(128), _row(128),
                  _full((BLK, BLK))],
        out_specs=[_row(256), _row(128), _row(128), _row(128)],
        out_shape=[jax.ShapeDtypeStruct((lp, 256), BF16), jax.ShapeDtypeStruct((lp, 128), BF16),
                   jax.ShapeDtypeStruct((lp, 128), BF16), jax.ShapeDtypeStruct((lp, 128), F32)],
        scratch_shapes=[pltpu.VMEM((1, 128), F32)],
        compiler_params=_cp(("arbitrary",)))(small, gq, gkv, fb, ctab, stab, tri)


def _rope_pairs(tile, c, s):
    out = []
    for lo in range(0, tile.shape[1], 256):
        out += [tile[:, lo:lo + 128], _rope(tile[:, lo + 128:lo + 256], c, s)]
    return jnp.concatenate(out, axis=1)


def _gate_fwd(o_mla, o_fox, gate):
    lp = o_mla.shape[0]

    def body(om_ref, of_ref, zm_ref, zf_ref, am_ref, af_ref):
        zm = zm_ref[...].astype(F32)
        am_ref[...] = (om_ref[...] * (zm * _sigmoid(zm))).astype(BF16)
        zf = zf_ref[...].astype(F32)
        af_ref[...] = (of_ref[...] * (zf * _sigmoid(zf))).astype(BF16)

    return pl.pallas_call(
        body, name="gate_fwd", grid=(lp // BLK,),
        in_specs=[_row(D_MODEL), _row(D_MODEL), _rowc(D_MODEL, 0), _rowc(D_MODEL, 1)],
        out_specs=[_row(D_MODEL), _row(D_MODEL)],
        out_shape=[jax.ShapeDtypeStruct((lp, D_MODEL), BF16)] * 2,
        compiler_params=_cp(("parallel",)))(o_mla, o_fox, gate, gate)


def _merge_fwd(gate, y_mla, y_fox):
    lp = y_mla.shape[0]

    def body(ga_ref, gb_ref, ym_ref, yf_ref, m_ref):
        sa = _sigmoid(ga_ref[...].astype(F32))
        sb = _sigmoid(gb_ref[...].astype(F32))
        m_ref[...] = (sa * ym_ref[...] + sb * yf_ref[...]).astype(BF16)

    return pl.pallas_call(
        body, name="merge_fwd", grid=(lp // BLK,),
        in_specs=[_rowc(D_MODEL, 2), _rowc(D_MODEL, 3), _row(D_MODEL), _row(D_MODEL)],
        out_specs=_row(D_MODEL), out_shape=jax.ShapeDtypeStruct((lp, D_MODEL), BF16),
        compiler_params=_cp(("parallel",)))(gate, gate, y_mla, y_fox)


def _tail(x2, mixed, tgt, gpost):
    lp = mixed.shape[0]
    shift = _shift_rows(D_MODEL)

    def body(h_ref, mx_ref, t_ref, g_ref, dmx_ref, dy_ref, loss_ref, dg_ref):
        i = pl.program_id(0)

        @pl.when(i == 0)
        def _():
            loss_ref[...] = jnp.zeros_like(loss_ref)
            dg_ref[...] = jnp.zeros_like(dg_ref)
            dmx_ref[...] = jnp.zeros_like(dmx_ref)
            dy_ref[...] = jnp.zeros_like(dy_ref)

        @pl.when(i > 0)
        def _():
            mx = mx_ref[...]
            g = g_ref[...]
            r = lax.rsqrt(jnp.mean(mx * mx, axis=-1, keepdims=True) + RMS_EPS)
            nrm = mx * r
            e = (h_ref[...] + nrm * g) - t_ref[...]
            loss_ref[...] += jnp.sum(0.5 * jnp.sum(e * e, axis=-1, keepdims=True) * (1.0 / D_MODEL),
                                     axis=0, keepdims=True)
            dy = e * (1.0 / D_MODEL)
            dy_ref[...] = dy
            dg_ref[...] += jnp.sum(dy * nrm, axis=0, keepdims=True)
            w = dy * g
            dot = jnp.mean(w * mx, axis=-1, keepdims=True)
            dmx_ref[...] = (r * w - mx * (r * r * r * dot)).astype(BF16)

    return pl.pallas_call(
        body, name="tail", grid=(lp // BLK,),
        in_specs=[shift, _row(D_MODEL), shift, _full((1, D_MODEL))],
        out_specs=[_row(D_MODEL), _row(D_MODEL), _full((1, 1)), _full((1, D_MODEL))],
        out_shape=[jax.ShapeDtypeStruct((lp, D_MODEL), BF16), jax.ShapeDtypeStruct((lp, D_MODEL), F32),
                   jax.ShapeDtypeStruct((1, 1), F32), jax.ShapeDtypeStruct((1, D_MODEL), F32)],
        compiler_params=_cp(("arbitrary",)))(x2, mixed, tgt, gpost)


def _merge_bwd(dm, gate, y_mla, y_fox):
    lp = dm.shape[0]

    def body(dm_ref, ga_ref, gb_ref, ym_ref, yf_ref, dym_ref, dyf_ref, dg_ref):
        dm_v = dm_ref[...].astype(F32)
        sa = _sigmoid(ga_ref[...].astype(F32))
        sb = _sigmoid(gb_ref[...].astype(F32))
        dym_ref[...] = (dm_v * sa).astype(BF16)
        dyf_ref[...] = (dm_v * sb).astype(BF16)
        dg_ref[:, 0:D_MODEL] = (dm_v * ym_ref[...] * (sa * (1.0 - sa))).astype(BF16)
        dg_ref[:, D_MODEL:2 * D_MODEL] = (dm_v * yf_ref[...] * (sb * (1.0 - sb))).astype(BF16)

    return pl.pallas_call(
        body, name="merge_bwd", grid=(lp // BLK,),
        in_specs=[_row(D_MODEL), _rowc(D_MODEL, 2), _rowc(D_MODEL, 3), _row(D_MODEL), _row(D_MODEL)],
        out_specs=[_row(D_MODEL), _row(D_MODEL), _row(2 * D_MODEL)],
        out_shape=[jax.ShapeDtypeStruct((lp, D_MODEL), BF16), jax.ShapeDtypeStruct((lp, D_MODEL), BF16),
                   jax.ShapeDtypeStruct((lp, 2 * D_MODEL), BF16)],
        compiler_params=_cp(("parallel",)))(dm, gate, gate, y_mla, y_fox)


def _gate_bwd(da_mla, da_fox, o_mla, o_fox, gate):
    lp = da_mla.shape[0]

    def one(da, o, z, head_of_col):
        sg = _sigmoid(z)
        do = (da * (z * sg)).astype(BF16)
        dz = da * o * (sg * (1.0 + z * (1.0 - sg)))
        delta = sum(_dot(part, head_of_col, 1, 0) for part in _split3(do.astype(F32) * o))
        return do, dz.astype(BF16), delta

    def body(dam_ref, daf_ref, om_ref, of_ref, zm_ref, zf_ref, dom_ref, dof_ref, dz_ref, dlm_ref, dlf_ref):
        f32 = lambda r: r[...].astype(F32)
        head_of_col = (lax.broadcasted_iota(jnp.int32, (D_MODEL, LANES), 0) // HEAD_DIM
                       == lax.broadcasted_iota(jnp.int32, (D_MODEL, LANES), 1)).astype(BF16)
        dom_ref[...], dz_ref[:, 0:D_MODEL], dlm_ref[...] = one(f32(dam_ref), f32(om_ref), f32(zm_ref), head_of_col)
        dof_ref[...], dz_ref[:, D_MODEL:2 * D_MODEL], dlf_ref[...] = one(f32(daf_ref), f32(of_ref), f32(zf_ref),
                                                                        head_of_col)

    return pl.pallas_call(
        body, name="gate_bwd", grid=(lp // BLK,),
        in_specs=[_row(D_MODEL)] * 4 + [_rowc(D_MODEL, 0), _rowc(D_MODEL, 1)],
        out_specs=[_row(D_MODEL), _row(D_MODEL), _row(2 * D_MODEL), _row(LANES), _row(LANES)],
        out_shape=[jax.ShapeDtypeStruct((lp, D_MODEL), BF16), jax.ShapeDtypeStruct((lp, D_MODEL), BF16),
                   jax.ShapeDtypeStruct((lp, 2 * D_MODEL), BF16), jax.ShapeDtypeStruct((lp, LANES), F32),
                   jax.ShapeDtypeStruct((lp, LANES), F32)],
        compiler_params=_cp(("parallel",)))(da_mla, da_fox, o_mla, o_fox, gate, gate)


def _small_bwd(small, dqn, dkvn, dkr, dcol_t, drow_t, gq, gkv, fb, ctab, stab, triu):
    lp = small.shape[0]
    nb = lp // BLK

    def rrow(w):
        return pl.BlockSpec((BLK, w), lambda i: (nb - 1 - i, 0))

    def body(sm_ref, dqn_ref, dkvn_ref, dkr_ref, dcol_ref, drow_ref, gq_ref, gkv_ref, fb_ref, c_ref, s_ref, tri_ref,
             ds_ref, dgq_ref, dgkv_ref, dfb_ref, carry):
        i = pl.program_id(0)

        @pl.when(i == 0)
        def _():
            carry[...] = jnp.zeros_like(carry)
            dgq_ref[...] = jnp.zeros_like(dgq_ref)
            dgkv_ref[...] = jnp.zeros_like(dgkv_ref)
            dfb_ref[...] = jnp.zeros_like(dfb_ref)

        def norm_bwd(x, dn, g, dg_ref):
            r = lax.rsqrt(jnp.mean(x * x, axis=-1, keepdims=True) + RMS_EPS)
            dg_ref[...] += jnp.sum(dn * (x * r), axis=0, keepdims=True)
            w = dn * g
            dot = jnp.mean(w * x, axis=-1, keepdims=True)
            return r * w - x * (r * r * r * dot)

        ds_ref[:, 0:256] = norm_bwd(sm_ref[:, 0:256], dqn_ref[...], gq_ref[...], dgq_ref).astype(BF16)
        ds_ref[:, 256:384] = norm_bwd(sm_ref[:, 256:384], dkvn_ref[...], gkv_ref[...], dgkv_ref).astype(BF16)

        dk = dkr_ref[0]
        for p in range(1, PAIRS):
            dk = dk + dkr_ref[p]
        dk = _rope(dk, c_ref[...], -s_ref[...])
        lane = lax.broadcasted_iota(jnp.int32, dk.shape, 1)
        dk = jnp.where(lane < MLA_ROPE, dk + pltpu.roll(dk, LANES - MLA_ROPE, 1), 0.0)
        ds_ref[:, 384:512] = dk.astype(BF16)

        dcol = dcol_ref[0]
        for p in range(1, PAIRS):
            dcol = dcol + pltpu.roll(dcol_ref[p], 2 * p, 1)
        rows16 = jnp.concatenate([drow_ref[p, h:h + 1, :] for p in range(PAIRS) for h in range(2)], axis=0)
        eye = (lax.broadcasted_iota(jnp.int32, (HEADS, LANES), 0)
               == lax.broadcasted_iota(jnp.int32, (HEADS, LANES), 1)).astype(BF16)
        drow = sum(_dot(part, eye, 0, 0) for part in _split3(rows16))
        dcr = dcol - drow
        hi, mid, lo = _split3(dcr)
        t = tri_ref[...]
        suf = (_dot(t, hi, 1, 0) + _dot(t, mid, 1, 0)) + _dot(t, lo, 1, 0) + carry[...]
        fl = sm_ref[:, 512:640] + fb_ref[...]
        dfl = jnp.where(_row_valid(nb - 1 - i), -suf * _sigmoid(-fl), 0.0)
        ds_ref[:, 512:640] = dfl.astype(BF16)
        dfb_ref[...] += jnp.sum(dfl, axis=0, keepdims=True)
        carry[...] += jnp.sum(dcr, axis=0, keepdims=True)

    return pl.pallas_call(
        body, name="small_bwd", grid=(nb,),
        in_specs=[rrow(SMALL_W), rrow(256), rrow(128),
                  pl.BlockSpec((PAIRS, BLK, 128), lambda i: (0, nb - 1 - i, 0)),
                  pl.BlockSpec((PAIRS, BLK, 128), lambda i: (0, nb - 1 - i, 0)),
                  pl.BlockSpec((PAIRS, 2, BLK), lambda i: (0, 0, nb - 1 - i)),
                  _full((1, 256)), _full((1, 128)), _full((1, 128)), rrow(128), rrow(128), _full((BLK, BLK))],
        out_specs=[rrow(SMALL_W), _full((1, 256)), _full((1, 128)), _full((1, 128))],
        out_shape=[jax.ShapeDtypeStruct((lp, SMALL_W), BF16), jax.ShapeDtypeStruct((1, 256), F32),
                   jax.ShapeDtypeStruct((1, 128), F32), jax.ShapeDtypeStruct((1, 128), F32)],
        scratch_shapes=[pltpu.VMEM((1, 128), F32)],
        compiler_params=_cp(("arbitrary",)))(small, dqn, dkvn, dkr, dcol_t, drow_t, gq, gkv, fb, ctab, stab, triu)


def _pre_bwd(du, x2, meta, dy, gpre):
    s_rows = x2.shape[0]
    lp = PAD + s_rows
    shift = _shift_rows(D_MODEL)

    def body(du_ref, x_ref, meta_ref, dy_ref, g_ref, dx_ref, dmeta_ref, dg_ref):
        i = pl.program_id(0)

        @pl.when(i == 0)
        def _():
            dg_ref[...] = jnp.zeros_like(dg_ref)

        hv = _h_block(i, x_ref, meta_ref)
        duv = du_ref[...]
        r = lax.rsqrt(jnp.mean(hv * hv, axis=-1, keepdims=True) + RMS_EPS)
        dg_ref[...] += jnp.sum(duv * (hv * r), axis=0, keepdims=True)
        w = duv * g_ref[...]
        dot = jnp.mean(w * hv, axis=-1, keepdims=True)
        dh = dy_ref[...] + (r * w - hv * (r * r * r * dot))
        dx_ref[...] = dh

        @pl.when(i == 0)
        def _():
            dmeta_ref[...] = dh[0:N_META, :]

    return pl.pallas_call(
        body, name="pre_bwd", grid=(lp // BLK,),
        in_specs=[_row(D_MODEL), shift, _full((N_META, D_MODEL)), _row(D_MODEL), _full((1, D_MODEL))],
        out_specs=[shift, _full((N_META, D_MODEL)), _full((1, D_MODEL))],
        out_shape=[jax.ShapeDtypeStruct((s_rows, D_MODEL), F32), jax.ShapeDtypeStruct((N_META, D_MODEL), F32),
                   jax.ShapeDtypeStruct((1, D_MODEL), F32)],
        compiler_params=_cp(("arbitrary",)))(du, x2, meta, dy, gpre)


def _pair_masks(rope):
    lane = lax.broadcasted_iota(jnp.int32, (1, LANES), 1)
    mas = [lane < HEAD_DIM, lane >= HEAD_DIM]
    if not rope:
        return mas, mas
    wide = lax.broadcasted_iota(jnp.int32, (1, 2 * LANES), 1)
    rope_lo = LANES + MLA_ROPE
    return mas, [(wide < HEAD_DIM) | ((wide >= LANES) & (wide < rope_lo)),
                 ((wide >= HEAD_DIM) & (wide < LANES)) | ((wide >= rope_lo) & (wide < rope_lo + MLA_ROPE))]


def _mask2(x, masks):
    return [jnp.where(m, x, jnp.zeros_like(x)) for m in masks]


def _attn_fwd(q, k, v, *, kr=None, nbrep=None, scale, qcol, kcol, vcol, name):
    lp = q.shape[0]
    nq = 1 + (lp - PAD) // QB
    rope = kr is not None
    bias = nbrep is not None
    qw = 256 if rope else 128

    def body(*refs):
        it = iter(refs)
        q_ref, k_ref, v_ref = next(it), next(it), next(it)
        kr_ref = next(it) if rope else None
        nb_ref = next(it) if bias else None
        o_ref, lse_ref = next(it), next(it)
        i = pl.program_id(1)
        r0 = pl.multiple_of(jnp.where(i == 0, 0, PAD + QB * (i - 1)), BLK)
        b0 = r0 // BLK
        mas, hmask = _pair_masks(rope)
        qh = _mask2(q_ref[pl.ds(r0, QB), :], hmask)
        if bias:
            qh = [x * scale for x in qh]

        def update(kcs, carry, masks, ns=None, q_lo=0, wq=QB):
            ns = ns or [BLK] * len(kcs)
            stats, acc = carry[:4], carry[4]
            qs = [x[q_lo:q_lo + wq] for x in qh]
            k0s = [pl.multiple_of(kc * BLK, BLK) for kc in kcs]
            kks = [k_ref[pl.ds(k0, n), :] for k0, n in zip(k0s, ns)]
            if rope:
                kks = [jnp.concatenate([kk, kr_ref[pl.ds(k0, n), :]], axis=1) for kk, k0, n in zip(kks, k0s, ns)]
            new_stats, alphas, ps = [], [], [[] for _ in kcs]
            for h in range(2):
                m_prev, l_prev = stats[2 * h], stats[2 * h + 1]
                ss = []
                for kk, k0, n, mask in zip(kks, k0s, ns, masks):
                    s = _dot(kk, qs[h], 1, 1)
                    if rope:
                        s = s * scale
                    if bias:
                        nbc = nb_ref[h, pl.ds(k0, n), :]
                        s = s + jnp.concatenate([nbc] * (wq // LANES), axis=1)
                    if mask is not None:
                        s = jnp.where(mask, s, NEG)
                    ss.append(s)
                m_new = m_prev
                for s in ss:
                    m_new = jnp.maximum(m_new, jnp.max(s, axis=0, keepdims=True))
                alpha = jnp.exp(m_prev - m_new)
                l_new = alpha * l_prev
                for j, s in enumerate(ss):
                    p = jnp.exp(s - m_new)
                    l_new = l_new + jnp.sum(p, axis=0, keepdims=True)
                    ps[j].append(p.astype(BF16))
                new_stats += [m_new, l_new]
                alphas.append(alpha)
            vcat = jnp.concatenate([x for k0, n in zip(k0s, ns) for x in _mask2(v_ref[pl.ds(k0, n), :], mas)], axis=0)
            pv = _dot(vcat, jnp.concatenate([p for pj in ps for p in pj], axis=0), 0, 0)
            a_full = jnp.concatenate([jnp.broadcast_to(a, (HEAD_DIM, wq)) for a in alphas], axis=0)
            return (*new_stats, a_full * acc + pv)

        neg = jnp.full((1, QB), NEG, F32)
        zero = jnp.zeros((1, QB), F32)
        c = (neg, zero, neg, zero, jnp.zeros((LANES, QB), F32))
        n_mid = jnp.maximum(b0 - 1, 0)
        c = lax.fori_loop(0, n_mid // 4, lambda t, cr: update([4 * t + u for u in (1, 2, 3, 4)], cr, [None] * 4), c)
        c = lax.fori_loop(0, (n_mid % 4) // 2, lambda t, cr: update([n_mid - 1, n_mid], cr, [None, None]), c)
        key_l = lax.broadcasted_iota(jnp.int32, (BLK, BLK), 0)
        qry_l = lax.broadcasted_iota(jnp.int32, (BLK, BLK), 1)
        tri = (key_l <= qry_l) & (b0 > 0)
        meta_ok = (key_l[0:N_META] <= qry_l[0:N_META]) | (b0 > 0)
        lo = update([0, b0], tuple(a[:, 0:BLK] for a in c), [meta_ok, tri], ns=[N_META, BLK], q_lo=0, wq=BLK)
        hi = update([0, b0, b0 + 1], tuple(a[:, BLK:QB] for a in c), [None, None, tri], ns=[N_META, BLK, BLK],
                    q_lo=BLK, wq=QB - BLK)
        c = tuple(jnp.concatenate([a, b], axis=1) for a, b in zip(lo, hi))
        inv =jnp.concatenate([jnp.broadcast_to(1.0 / c[1], (HEAD_DIM, QB)),
                               jnp.broadcast_to(1.0 / c[3], (HEAD_DIM, QB))], axis=0)
        o_t = (c[4] * inv).T.astype(BF16)
        lses = [c[0] + jnp.log(c[1]), c[2] + jnp.log(c[3])]
        o_ref[pl.ds(r0, BLK), :] = o_t[0:BLK]
        for h in range(2):
            lse_ref[0, h:h + 1, pl.ds(r0, BLK)] = lses[h][:, 0:BLK]

        @pl.when(i > 0)
        def _():
            r1 = pl.multiple_of(r0 + BLK, BLK)
            o_ref[pl.ds(r1, QB - BLK), :] = o_t[BLK:QB]
            for h in range(2):
                lse_ref[0, h:h + 1, pl.ds(r1, QB - BLK)] = lses[h][:, BLK:QB]

    in_specs = [pl.BlockSpec((lp, qw), lambda p, i: (0, qcol + p)),
                pl.BlockSpec((lp, 128), lambda p, i: (0, kcol(p))),
                pl.BlockSpec((lp, 128), lambda p, i: (0, vcol(p)))]
    ins = [q, k, v]
    if rope:
        in_specs.append(pl.BlockSpec((lp, 128), lambda p, i: (0, 0)))
        ins.append(kr)
    if bias:
        in_specs.append(pl.BlockSpec((2, lp, 128), lambda p, i: (p, 0, 0)))
        ins.append(nbrep)
    return pl.pallas_call(
        body, name=name, grid=(PAIRS, nq), in_specs=in_specs,
        out_specs=[pl.BlockSpec((lp, 128), lambda p, i: (0, p)),
                   pl.BlockSpec((1, 2, lp), lambda p, i: (p, 0, 0))],
        out_shape=[jax.ShapeDtypeStruct((lp, D_MODEL), BF16), jax.ShapeDtypeStruct((PAIRS, 2, lp), F32)],
        compiler_params=_cp(("parallel", "arbitrary"), VMEM_BIG))(*ins)


def _attn_bwd(q, k, v, do, delta, lse, *, kr=None, rtabs=None, nbrep=None, scale, qcol, kcol, vcol, name):
    lp = q.shape[0]
    nb = lp // BLK
    rope = kr is not None
    bias = nbrep is not None
    qw = 256 if rope else 128

    def body(*refs):
        it = iter(refs)
        q_ref, k_ref, v_ref = next(it), next(it), next(it)
        kr_ref = next(it) if rope else None
        nb_ref = next(it) if bias else None
        do_ref, dl_ref, lse_ref = next(it), next(it), next(it)
        ct_ref, st_ref = (next(it), next(it)) if rope else (None, None)
        dq_out, dk_ref, dv_ref = next(it), next(it), next(it)
        x_ref = next(it)
        drow_ref = next(it) if bias else None
        dq_ref = next(it)
        kb = pl.program_id(1)
        mas, hmask = _pair_masks(rope)
        lane = lax.broadcasted_iota(jnp.int32, (1, LANES), 1)

        @pl.when(kb == 0)
        def _():
            dq_ref[...] = jnp.zeros_like(dq_ref)
            if bias:
                drow_ref[...] = jnp.zeros_like(drow_ref)

        def key_pass(n, w):
            kk = k_ref[0:n, :]
            if rope:
                kk = jnp.concatenate([kk, kr_ref[0:n, :]], axis=1)
            vh = _mask2(v_ref[0:n, :], mas)
            kcat = jnp.concatenate(_mask2(kk, hmask), axis=0)
            if bias:
                kcat = kcat * scale
            diag_mask = (lax.broadcasted_iota(jnp.int32, (n, w), 0) <= lax.broadcasted_iota(jnp.int32, (n, w), 1))

            def chunk(qc, carry, mask):
                carry = list(carry)
                q0 = qc * w if isinstance(qc, int) else pl.multiple_of(qc * w, w)
                dov = do_ref[pl.ds(q0, w), :]
                doh = _mask2(dov, mas)
                qh = _mask2(q_ref[pl.ds(q0, w), :], hmask)
                if bias:
                    qh = [x * scale for x in qh]
                pbs, dss = [], []
                for h in range(2):
                    s = _dot(kk, qh[h], 1, 1)
                    if rope:
                        s = s * scale
                    if bias:
                        s = s + jnp.concatenate([nb_ref[h, 0:n, :]] * (w // LANES), axis=1)
                    p = jnp.exp(s - lse_ref[0, h:h + 1, pl.ds(q0, w)])
                    if mask is not None:
                        p = jnp.where(mask, p, 0.0)
                    ds = p * (_dot(vh[h], dov, 1, 1) - dl_ref[0, h:h + 1, pl.ds(q0, w)])
                    if bias:
                        drow_ref[0, h:h + 1, pl.ds(q0, w)] += jnp.sum(ds, axis=0, keepdims=True)
                        carry[2 + h] = carry[2 + h] + jnp.sum(ds, axis=1, keepdims=True)
                    else:
                        ds = ds * scale
                    pbs.append(p.astype(BF16))
                    dss.append(ds.astype(BF16))
                ds_lanes = jnp.concatenate(dss, axis=1)
                ds_rows = jnp.concatenate(dss, axis=0)
                carry[0] = carry[0] + _dot(ds_lanes, jnp.concatenate(qh, axis=0), 1, 0)
                carry[1] = carry[1] + _dot(jnp.concatenate(pbs, axis=1), jnp.concatenate(doh, axis=0), 1, 0)
                dq_ref[pl.ds(q0, w), :] += _dot(ds_rows, kcat, 0, 0)
                return tuple(carry)

            c = [jnp.zeros((n, qw), F32), jnp.zeros((n, LANES), F32)]
            if bias:
                c += [jnp.zeros((n, 1), F32), jnp.zeros((n, 1), F32)]
            c = tuple(c)
            if w != BLK:
                for qc in range(lp // w):
                    c = chunk(qc, c, diag_mask if qc == 0 else None)
            else:
                groups = (nb - kb) // UNROLL

                def several(t, cr):
                    for u in range(UNROLL):
                        cr = chunk(kb + UNROLL * t + u, cr, (diag_mask | (t > 0)) if u == 0 else None)
                    return cr

                c = lax.fori_loop(0, groups, several, c)
                start = kb + UNROLL * groups
                pairs = (nb - start) // 2

                def two(t, cr):
                    qc = start + 2 * t
                    return chunk(qc + 1, chunk(qc, cr, diag_mask | (qc > kb)), None)

                c = lax.fori_loop(0, pairs, two, c)
                c = lax.fori_loop(start + 2 * pairs, nb, lambda qc, cr: chunk(qc, cr, diag_mask | (qc > kb)), c)

            def rows(a, dtype):
                a = a.astype(dtype)
                return a if n == BLK else jnp.concatenate([a, jnp.zeros((BLK - n, a.shape[1]), dtype)], axis=0)

            dk_ref[...] = rows(c[0][:, 0:LANES], BF16)
            dv_ref[...] = rows(c[1], BF16)
            if rope:
                x_ref[0] = rows(c[0][:, LANES:2 * LANES], F32)
            if bias:
                x_ref[0] = rows(jnp.where(lane == 0, c[2], jnp.where(lane == 1, c[3], 0.0)), F32)

        @pl.when(kb == 0)
        def _():
            key_pass(N_META, lp // 2)

        @pl.when(kb > 0)
        def _():
            key_pass(BLK, BLK)

        @pl.when(kb == nb - 1)
        def _():
            def fin(c, carry):
                r0 = pl.multiple_of(c * BLK, BLK)
                dq = dq_ref[pl.ds(r0, BLK), :]
                if rope:
                    back = _rope(dq[:, LANES:2 * LANES], ct_ref[pl.ds(r0, BLK), :], -st_ref[pl.ds(r0, BLK), :])
                    dq = jnp.concatenate([dq[:, 0:LANES], back], axis=1)
                dq_out[pl.ds(r0, BLK), :] = dq.astype(BF16)
                return carry

            lax.fori_loop(0, nb, fin, 0)

    in_specs = [pl.BlockSpec((lp, qw), lambda p, j: (0, qcol + p)),
                pl.BlockSpec((BLK, 128), lambda p, j: (j, kcol(p))),
                pl.BlockSpec((BLK, 128), lambda p, j: (j, vcol(p)))]
    ins = [q, k, v]
    if rope:
        in_specs.append(pl.BlockSpec((BLK, 128), lambda p, j: (j, 0)))
        ins.append(kr)
    if bias:
        in_specs.append(pl.BlockSpec((2, BLK, 128), lambda p, j: (p, j, 0)))
        ins.append(nbrep)
    in_specs += [pl.BlockSpec((lp, 128), lambda p, j: (0, p)), pl.BlockSpec((1, 2, lp), lambda p, j: (p, 0, 0)),
                 pl.BlockSpec((1, 2, lp), lambda p, j: (p, 0, 0))]
    ins += [do, delta, lse]
    if rope:
        in_specs += [pl.BlockSpec((lp, 128), lambda p, j: (0, 0))] * 2
        ins += list(rtabs)
    out_specs = [pl.BlockSpec((lp, qw), lambda p, j: (0, p)),
                 pl.BlockSpec((BLK, 128), lambda p, j: (j, p)),
                 pl.BlockSpec((BLK, 128), lambda p, j: (j, p)),
                 pl.BlockSpec((1, BLK, 128), lambda p, j: (p, j, 0))]
    out_shape = [jax.ShapeDtypeStruct((lp, PAIRS * qw), BF16), jax.ShapeDtypeStruct((lp, D_MODEL), BF16),
                 jax.ShapeDtypeStruct((lp, D_MODEL), BF16), jax.ShapeDtypeStruct((PAIRS, lp, 128), F32)]
    if bias:
        out_specs.append(pl.BlockSpec((1, 2, lp), lambda p, j: (p, 0, 0)))
        out_shape.append(jax.ShapeDtypeStruct((PAIRS, 2, lp), F32))
    return pl.pallas_call(
        body, name=name, grid=(PAIRS, nb), in_specs=in_specs, out_specs=out_specs, out_shape=out_shape,
        scratch_shapes=[pltpu.VMEM((lp, qw), F32)],
        compiler_params=_cp(("parallel", "arbitrary"), VMEM_BIG))(*ins)


def _adamw(w, g, m, v, name):
    lead = w.ndim - 2
    rows, cols = w.shape[lead:]
    big = rows * cols > 512 * 1024
    tr = 128 if big and rows % 128 == 0 else rows
    tc = 256 if big and tr == rows else cols

    def body(w_ref, g_ref, m_ref, v_ref, d_ref, nm_ref, nv_ref):
        gv = g_ref[...]
        nm = ADAM_B1 * m_ref[...] + (1.0 - ADAM_B1) * gv
        nv = ADAM_B2 * v_ref[...] + (1.0 - ADAM_B2) * (gv * gv)
        m_hat = nm / (1.0 - ADAM_B1 ** ADAM_STEP)
        v_hat = nv / (1.0 - ADAM_B2 ** ADAM_STEP)
        d_ref[...] = -ADAM_LR * (m_hat / (jnp.sqrt(v_hat) + ADAM_EPS) + ADAM_WD * w_ref[...])
        nm_ref[...] = nm
        nv_ref[...] = nv

    spec = pl.BlockSpec((1,) * lead + (tr, tc), lambda i, j: (0,) * lead + (i, j))
    return pl.pallas_call(
        body, name=name, grid=(rows // tr, cols // tc), in_specs=[spec] * 4, out_specs=[spec] * 3,
        out_shape=[jax.ShapeDtypeStruct(w.shape, F32)] * 3,
        compiler_params=_cp(("parallel", "parallel"), VMEM_BIG))(w, g, m, v)


def _add_cores(g, from_sib, name):
    n, rows, cols = g.shape
    half = rows // 2
    tr = _tile(half, (256, 240))
    nt = half // tr

    def body(lo_ref, hi_ref, s_ref, o_ref):
        mine = jnp.where(lax.axis_index("c") == 0, lo_ref[0], hi_ref[0])
        o_ref[0] = (mine.astype(F32) + s_ref[0].astype(F32)).astype(BF16)

    return pl.pallas_call(
        body, name=name, grid=(n, nt),
        in_specs=[pl.BlockSpec((1, tr, cols), lambda j, i: (j, i, 0)),
                  pl.BlockSpec((1, tr, cols), lambda j, i: (j, nt + i, 0)),
                  pl.BlockSpec((1, tr, cols), lambda j, i: (j, i, 0))],
        out_specs=pl.BlockSpec((1, tr, cols), lambda j, i: (j, i, 0)),
        out_shape=jax.ShapeDtypeStruct((n, half, cols), BF16),
        compiler_params=_cp(("parallel", "parallel"), VMEM_BIG))(g, g, from_sib)


def _add_chips(x, own, name):
    n, rows, cols = x.shape
    tr = _tile(rows, (256, 240))

    def body(x_ref, own_ref, o_ref):
        me = 2 * lax.axis_index("x") + lax.axis_index("y")
        v = [jnp.where(me == k, own_ref[...], x_ref[k]).astype(F32) for k in range(N_CHIPS)]
        o_ref[...] = ((v[0] + v[1]) + v[2]) + v[3]

    return pl.pallas_call(
        body, name=name, grid=(rows // tr,),
        in_specs=[pl.BlockSpec((n, tr, cols), lambda i: (0, i, 0)), pl.BlockSpec((tr, cols), lambda i: (i, 0))],
        out_specs=pl.BlockSpec((tr, cols), lambda i: (i, 0)),
        out_shape=jax.ShapeDtypeStruct((rows, cols), F32), compiler_params=_cp(("parallel",), VMEM_BIG))(x, own)


def _axes():
    return lax.axis_index("x"), lax.axis_index("y"), lax.axis_index("c")


def _other_chips(x, y):
    return [(1 - x, y), (x, 1 - y), (1 - x, 1 - y)]


ANY = pl.BlockSpec(memory_space=pl.ANY)


def _rcopy(src, dst, send_sems, recv_sems, k, to):
    return pltpu.make_async_remote_copy(src_ref=src, dst_ref=dst, send_sem=send_sems.at[k], recv_sem=recv_sems.at[k],
                                        device_id=to, device_id_type=MESH)


def _gather_weights(shards, meta):
    n = len(shards)

    def body(*refs):
        srcs, meta_ref = refs[:n], refs[n]
        outs, mout_ref = refs[n + 1:2 * n + 1], refs[2 * n + 1]
        send_sems, recv_sems = refs[2 * n + 2:]
        x, y, c = _axes()
        me = 2 * x + y
        sib = (x, y, 1 - c)
        chips = _other_chips(x, y)

        def half(t, chip_idx, cc):
            hr = shards[t].shape[0] // 2
            return outs[t].at[chip_idx, pl.ds(cc * hr, hr), :]

        first = []
        for j, (px, py) in enumerate(chips):
            for t in range(n):
                hr = shards[t].shape[0] // 2
                first.append(_rcopy(srcs[t].at[pl.ds(c * hr, hr), :], half(t, me, c), send_sems, recv_sems,
                                    3 * t + j, (px, py, c)))
            first.append(_rcopy(meta_ref, mout_ref.at[me], send_sems, recv_sems, 3 * n + j, (px, py, c)))
        for cp in first:
            cp.start()
        passed = []
        for j, (px, py) in enumerate(chips):
            src_chip = 2 * px + py
            for t in range(n):
                _rcopy(half(t, src_chip, c), half(t, src_chip, c), send_sems, recv_sems, 3 * t + j, sib).wait_recv()
                fwd = _rcopy(half(t, src_chip, c), half(t, src_chip, c), send_sems, recv_sems, 3 * (n + 1 + t) + j, sib)
                fwd.start()
                passed.append(fwd)
            _rcopy(mout_ref.at[src_chip], mout_ref.at[src_chip], send_sems, recv_sems, 3 * n + j, sib).wait_recv()
        for j, (px, py) in enumerate(chips):
            src_chip = 2 * px + py
            for t in range(n):
                _rcopy(half(t, src_chip, 1 - c), half(t, src_chip, 1 - c), send_sems, recv_sems,
                       3 * (n + 1 + t) + j, sib).wait_recv()
        for cp in first + passed:
            cp.wait_send()

    nsem = 3 * (2 * n + 1)
    return pl.pallas_call(
        body, name="gather_weights", in_specs=[ANY] * (n + 1), out_specs=[ANY] * (n + 1),
        out_shape=[jax.ShapeDtypeStruct((N_CHIPS,) + s.shape, s.dtype) for s in shards]
        + [jax.ShapeDtypeStruct((N_CHIPS,) + meta.shape, meta.dtype)],
        scratch_shapes=[pltpu.SemaphoreType.DMA((nsem,)), pltpu.SemaphoreType.DMA((nsem,))])(*shards, meta)


def _gather_late(shard):
    rows, cols = shard.shape
    hr = rows // 2
    src = jax.new_ref(shard, memory_space=pltpu.MemorySpace.HBM)
    out = jax.empty_ref(jax.ShapeDtypeStruct((N_CHIPS, rows, cols), shard.dtype), memory_space=pltpu.MemorySpace.HBM)

    @pl.kernel(mesh=plsc.ScalarSubcoreMesh(axis_name="seq", num_cores=1), name="gather_late",
               scratch_types=(pltpu.SemaphoreType.DMA((6,)), pltpu.SemaphoreType.DMA((6,))),
               compiler_params=pltpu.CompilerParams(collective_id=1))
    def launch(send_sems, recv_sems):
        x, y, c = _axes()
        me = 2 * x + y
        sib = (x, y, 1 - c)
        chips = _other_chips(x, y)
        barrier = pltpu.get_barrier_semaphore()
        for px, py in chips:
            pl.semaphore_signal(barrier, inc=1, device_id=(px, py, c), device_id_type=MESH)
        pl.semaphore_signal(barrier, inc=1, device_id=sib, device_id_type=MESH)
        pl.semaphore_wait(barrier, 4)

        def half(chip_idx, cc):
            return out.at[chip_idx, pl.ds(cc * hr, hr), :]

        first = [_rcopy(src.at[pl.ds(c * hr, hr), :], half(me, c), send_sems, recv_sems, j, (px, py, c))
                 for j, (px, py) in enumerate(chips)]
        for cp in first:
            cp.start()
        passed = []
        for j, (px, py) in enumerate(chips):
            land = half(2 * px + py, c)
            _rcopy(land, land, send_sems, recv_sems, j, sib).wait_recv()
            fwd = _rcopy(land, land, send_sems, recv_sems, 3 + j, sib)
            fwd.start()
            passed.append(fwd)
        for j, (px, py) in enumerate(chips):
            land = half(2 * px + py, 1 - c)
            _rcopy(land, land, send_sems, recv_sems, 3 + j, sib).wait_recv()
        for cp in first + passed:
            cp.wait_send()

    launch()
    return out[...]


def _swap_halves(gs):
    n = len(gs)
    ncopies = sum(g.shape[0] for g in gs)

    def body(*refs):
        srcs, outs = refs[:n], refs[n:2 * n]
        send_sems, recv_sems = refs[2 * n:]
        x, y, c = _axes()
        cps = []
        for t in range(n):
            hr = gs[t].shape[1] // 2
            for j in range(gs[t].shape[0]):
                cps.append(_rcopy(srcs[t].at[j, pl.ds((1 - c) * hr, hr), :], outs[t].at[j], send_sems, recv_sems,
                                  len(cps), (x, y, 1 - c)))
        for cp in cps:
            cp.start()
        for cp in cps:
            cp.wait()

    return pl.pallas_call(
        body, name="swap_halves", in_specs=[ANY] * n, out_specs=[ANY] * n,
        out_shape=[jax.ShapeDtypeStruct((g.shape[0], g.shape[1] // 2, g.shape[2]), g.dtype) for g in gs],
        scratch_shapes=[pltpu.SemaphoreType.DMA((ncopies,)), pltpu.SemaphoreType.DMA((ncopies,))])(*gs)


def _scatter_chips(parts):
    n = len(parts)
    srcs = [jax.new_ref(p, memory_space=pltpu.MemorySpace.HBM) for p in parts]
    outs = [jax.empty_ref(jax.ShapeDtypeStruct(p.shape, p.dtype), memory_space=pltpu.MemorySpace.HBM) for p in parts]

    @pl.kernel(mesh=plsc.ScalarSubcoreMesh(axis_name="seq", num_cores=1), name="scatter_chips",
               scratch_types=(pltpu.SemaphoreType.DMA((3 * n,)), pltpu.SemaphoreType.DMA((3 * n,))),
               compiler_params=pltpu.CompilerParams(collective_id=0))
    def launch(send_sems, recv_sems):
        x, y, c = _axes()
        me = 2 * x + y
        chips = _other_chips(x, y)
        barrier = pltpu.get_barrier_semaphore()
        for px, py in chips:
            pl.semaphore_signal(barrier, inc=1, device_id=(px, py, c), device_id_type=MESH)
        pl.semaphore_wait(barrier, 3)
        cps = []
        for j, (px, py) in enumerate(chips):
            for t in range(n):
                cps.append(_rcopy(srcs[t].at[2 * px + py], outs[t].at[me], send_sems, recv_sems, 3 * t + j,
                                  (px, py, c)))
        for cp in cps:
            cp.start()
        for cp in cps:
            cp.wait()

    launch()
    return [o[...] for o in outs]


def _swap_reduced(rs):
    n = len(rs)

    def body(*refs):
        srcs, outs = refs[:n], refs[n:2 * n]
        send_sems, recv_sems = refs[2 * n:]
        x, y, c = _axes()
        cps = [_rcopy(srcs[t], outs[t], send_sems, recv_sems, t, (x, y, 1 - c)) for t in range(n)]
        for cp in cps:
            cp.start()
        for cp in cps:
            cp.wait()

    return pl.pallas_call(
        body, name="swap_reduced", in_specs=[ANY] * n, out_specs=[ANY] * n,
        out_shape=[jax.ShapeDtypeStruct(r.shape, r.dtype) for r in rs],
        scratch_shapes=[pltpu.SemaphoreType.DMA((n,)), pltpu.SemaphoreType.DMA((n,))])(*rs)


SMALL_ROWS = 24 + 128


def _allreduce_small(vec):
    def body(v_ref, out_ref, slots, send_sems, recv_sems):
        x, y, c = _axes()
        me = 4 * x + 2 * y + c
        slots[me] = v_ref[...]
        cps = []
        for k in range(1, 8):
            kx, ky, kc = (k >> 2) & 1, (k >> 1) & 1, k & 1
            peer = (1 - x if kx else x, 1 - y if ky else y, 1 - c if kc else c)
            cps.append(_rcopy(v_ref, slots.at[me], send_sems, recv_sems, k - 1, peer))
        for cp in cps:
            cp.start()
        for cp in cps:
            cp.wait()
        tot = slots[0]
        for k in range(1, 8):
            tot = tot + slots[k]
        out_ref[...] = tot

    return pl.pallas_call(
        body, name="allreduce_small",
        in_specs=[pl.BlockSpec(memory_space=pltpu.VMEM)], out_specs=pl.BlockSpec(memory_space=pltpu.VMEM),
        out_shape=jax.ShapeDtypeStruct((SMALL_ROWS, 128), F32),
        scratch_shapes=[pltpu.VMEM((8, SMALL_ROWS, 128), F32), pltpu.SemaphoreType.DMA((7,)),
                        pltpu.SemaphoreType.DMA((7,))])(vec)


def _pack_p2(w_uq, w_ukv, w_br_mla, w_br_fox, w_out, dtype):
    parts = [w_uq.reshape(96, D_MODEL), w_ukv.reshape(64, D_MODEL), w_br_mla, w_br_fox, w_out]
    return jnp.concatenate([p.astype(dtype) for p in parts], axis=0)


def _unpack_p2(pk):
    return pk[0:96].reshape(256, 384), pk[96:160].reshape(128, 512), pk[160:416], pk[416:672], pk[672:928]


def _uq_arrange(w):
    w3 = w.reshape(256, HEADS, 96)
    nope = w3[:, :, :64].reshape(256, PAIRS, 128)
    pe = w3[:, :, 64:].reshape(256, PAIRS, 64)
    return jnp.concatenate([nope, pe, jnp.zeros((256, PAIRS, 64), w.dtype)], axis=2).reshape(256, PAIRS * 256)


def _uq_restore(g):
    g3 = g.reshape(256, PAIRS, 256)
    nope = g3[:, :, :128].reshape(256, HEADS, 64)
    pe = g3[:, :, 128:192].reshape(256, HEADS, 32)
    return jnp.concatenate([nope, pe], axis=2).reshape(256, HEADS * 96)


def _ukv_arrange(w):
    w3 = w.reshape(128, HEADS, 128)
    return jnp.concatenate([w3[:, :, :64].reshape(128, 1024), w3[:, :, 64:].reshape(128, 1024)], axis=1)


def _ukv_restore(g):
    kn = g[:, :1024].reshape(128, HEADS, 64)
    vv = g[:, 1024:].reshape(128, HEADS, 64)
    return jnp.concatenate([kn, vv], axis=2).reshape(128, HEADS * 128)


def _rope_tables(lp):
    r = np.arange(lp)
    pos = np.where(r < N_META, r, np.where(r >= PAD, r - PAD + N_META, 0)).astype(np.float32)
    half = MLA_ROPE // 2
    inv_freq = np.float32(ROPE_THETA) ** (-np.arange(half, dtype=np.float32) / np.float32(half))
    ang = (pos[:, None] * inv_freq[None, :]).astype(np.float32)
    cos, sin = np.cos(ang).astype(np.float32), np.sin(ang).astype(np.float32)
    one, zero = np.ones((lp, 64), np.float32), np.zeros((lp, 64), np.float32)
    return (jnp.asarray(np.concatenate([cos, cos, cos, cos, one], axis=1)),
            jnp.asarray(np.concatenate([-sin, sin, -sin, sin, zero], axis=1)))


def _pad_lanes(v, n=128):
    return jnp.pad(v, ((0, 0), (0, n - v.shape[1])))


def _in_cols(slabs, a, b):
    out = []
    for j in range(N_CHIPS):
        lo, hi = max(a, W_IN_SHARD * j), min(b, W_IN_SHARD * (j + 1))
        if lo < hi:
            out.append(slabs[j][:, lo - W_IN_SHARD * j:hi - W_IN_SHARD * j])
    return out


def _local_step(x2, tgt2, meta_f, w_small, w_attn, w_gate, w_uq_f, w_ukv_f, w_bm, w_bf, w_o, pre_norm_g,
                post_norm_g, mla_q_norm_g, mla_kv_norm_g, fox_forget_b, start_exchange=None):
    s_rows = x2.shape[0]
    lp = PAD + s_rows
    w_uq_a = _uq_arrange(w_uq_f)
    w_ukv_a = _ukv_arrange(w_ukv_f)

    ctab, stab = _rope_tables(lp)
    ii = jnp.arange(BLK)
    tri_lo = (ii[:, None] >= ii[None, :]).astype(BF16)
    tri_up = (ii[:, None] <= ii[None, :]).astype(BF16)
    fb128 = _pad_lanes(fox_forget_b)

    u = _rms_pre(x2, meta_f, pre_norm_g)
    small = _mm(u, w_small, mode="nn", out_dtype=F32, name="proj_small")
    attn = _mm(u, w_attn, mode="nn", out_dtype=BF16, name="proj_attn")
    gate = _mm(u, w_gate, mode="nn", out_dtype=BF16, name="proj_gate")
    qn, kvn, kr, ncum = _small_prep(small, mla_q_norm_g, mla_kv_norm_g, fb128, ctab, stab, tri_lo)
    qcat = _mm(qn, w_uq_a, mode="nn", out_dtype=BF16, name="mla_q", epilogue=_rope_pairs, row_ins=(ctab, stab))
    kv = _mm(kvn, w_ukv_a, mode="nn", out_dtype=BF16, name="mla_kv")
    nbrep = jnp.broadcast_to(ncum[:, :HEADS].T[:, :, None], (HEADS, lp, LANES))

    mla_cols = dict(qcol=0, kcol=lambda p: p, vcol=lambda p: PAIRS + p)
    fox_cols = dict(qcol=0, kcol=lambda p: PAIRS + p, vcol=lambda p: 2 * PAIRS + p)
    o_mla, lse_mla = _attn_fwd(qcat, kv, kv, kr=kr, scale=MLA_SCALE, name="mla_fwd", **mla_cols)
    o_fox, lse_fox = _attn_fwd(attn, attn, attn, nbrep=nbrep, scale=FOX_SCALE, name="fox_fwd", **fox_cols)

    a_mla, a_fox = _gate_fwd(o_mla, o_fox, gate)
    y_mla = _mm(a_mla, w_bm, mode="nn", out_dtype=BF16, name="br_mla")
    y_fox = _mm(a_fox, w_bf, mode="nn", out_dtype=BF16, name="br_fox")
    mg = _merge_fwd(gate, y_mla, y_fox)
    mixed = _mm(mg, w_o, mode="nn", out_dtype=F32, name="out_proj")
    dmixed, dy, loss_p, dg_post = _tail(x2, mixed, tgt2, post_norm_g)

    d_w_out = _mm(mg, dmixed, mode="tn", out_dtype=F32, name="d_w_out")
    dm = _mm(dmixed, w_o, mode="nt", out_dtype=BF16, name="d_merge")
    dy_mla, dy_fox, dgate_ab = _merge_bwd(dm, gate, y_mla, y_fox)
    d_w_bm = _mm(a_mla, dy_mla, mode="tn", out_dtype=F32, name="d_w_br_mla")
    d_w_bf = _mm(a_fox, dy_fox, mode="tn", out_dtype=F32, name="d_w_br_fox")
    da_mla = _mm(dy_mla, w_bm, mode="nt", out_dtype=BF16, name="d_a_mla")
    da_fox = _mm(dy_fox, w_bf, mode="nt", out_dtype=BF16, name="d_a_fox")
    do_mla, do_fox, dgate_z, dl_mla, dl_fox = _gate_bwd(da_mla, da_fox, o_mla, o_fox, gate)
    dl_mla, dl_fox = (d[:, :HEADS].T.reshape(PAIRS, 2, lp) for d in (dl_mla, dl_fox))

    dq_a, dkn, dvm, dkr = _attn_bwd(qcat, kv, kv, do_mla, dl_mla, lse_mla, kr=kr, rtabs=(ctab, stab),
                                    scale=MLA_SCALE, name="mla_bwd", **mla_cols)
    dfq, dfk, dfv, dcol, drow = _attn_bwd(attn, attn, attn, do_fox, dl_fox, lse_fox, nbrep=nbrep, scale=FOX_SCALE,
                                          name="fox_bwd", **fox_cols)

    d_w_uq_a = _mm(qn, dq_a, mode="tn", out_dtype=F32, name="d_w_uq")
    dqn = _mm(dq_a, w_uq_a, mode="nt", out_dtype=F32, name="d_qn")
    d_w_ukv_a = jnp.concatenate([_mm(kvn, dkn, mode="tn", out_dtype=F32, name="d_w_uk"),
                                 _mm(kvn, dvm, mode="tn", out_dtype=F32, name="d_w_uv")], axis=1)
    dkvn = _mm(dkn, w_ukv_a[:, :1024], mode="nt", out_dtype=F32, name="d_kvn_k")
    dkvn = _mm(dvm, w_ukv_a[:, 1024:], mode="nt", out_dtype=F32, name="d_kvn_v", acc=dkvn)
    dsmall, dg_q, dg_kv, dfb = _small_bwd(small, dqn, dkvn, dkr, dcol, drow, mla_q_norm_g, mla_kv_norm_g,
                                          fb128, ctab, stab, tri_up)

    dw_small = _mm(u, dsmall, mode="tn", out_dtype=BF16, name="d_w_small")
    dw_fq = _mm(u, dfq, mode="tn", out_dtype=BF16, name="d_w_fq")
    dw_fk = _mm(u, dfk, mode="tn", out_dtype=BF16, name="d_w_fk")
    dw_fv = _mm(u, dfv, mode="tn", out_dtype=BF16, name="d_w_fv")
    dw_z = _mm(u, dgate_z, mode="tn", out_dtype=BF16, name="d_w_z")
    dw_g = _mm(u, dgate_ab, mode="tn", out_dtype=BF16, name="d_w_g")
    d_w_in = (dw_small, dw_z, dw_fq, dw_fk, dw_fv, dw_g)
    d_w_uq = _uq_restore(d_w_uq_a)
    d_w_ukv = _ukv_restore(d_w_ukv_a)
    token = start_exchange(d_w_in, d_w_uq, d_w_ukv, d_w_bm, d_w_bf, d_w_out) if start_exchange else None
    du = _mm_sum_nt([(dsmall, w_small), (dfq, w_attn[:, 0:1024]), (dfk, w_attn[:, 1024:2048]),
                     (dfv, w_attn[:, 2048:3072]), (dgate_z, w_gate[:, 0:2048]), (dgate_ab, w_gate[:, 2048:4096])],
                    name="d_u", after=token)
    dx, dmeta, dg_pre = _pre_bwd(du, x2, meta_f, dy, pre_norm_g)
    return (loss_p, dx, dmeta, d_w_in, d_w_uq, d_w_ukv, d_w_bm, d_w_bf, d_w_out, dg_pre, dg_post, dg_q, dg_kv, dfb)


def _w_in_slabs(pieces):
    dw_small, dw_z, dw_fq, dw_fk, dw_fv, dw_g = pieces
    runs = [(dw_small[:, 0:416], C_CQ), (dw_z[:, 0:1024], C_ZMLA), (dw_fq, C_FQ), (dw_fk, C_FK), (dw_fv, C_FV),
            (dw_small[:, 512:528], C_FL), (dw_z[:, 1024:2048], C_ZFOX), (dw_g, C_GA)]
    slabs = []
    for j in range(N_CHIPS):
        lo, hi = W_IN_SHARD * j, W_IN_SHARD * (j + 1)
        cols = [a[:, max(lo, c0) - c0:min(hi, c0 + a.shape[1]) - c0] for a, c0 in runs
                if max(lo, c0) < min(hi, c0 + a.shape[1])]
        slabs.append(jnp.concatenate(cols, axis=1))
    return jnp.stack(slabs, axis=0)


def kernel(x, meta_tokens, pre_norm_g, w_in, fox_forget_b, mla_q_norm_g, mla_kv_norm_g, w_uq, w_ukv, w_br_mla, w_br_fox, w_out, post_norm_g, loss_target, m_meta_tokens, m_pre_norm_g, m_w_in, m_fox_forget_b, m_mla_q_norm_g, m_mla_kv_norm_g, m_w_uq, m_w_ukv, m_w_br_mla, m_w_br_fox, m_w_out, m_post_norm_g, v_meta_tokens, v_pre_norm_g, v_w_in, v_fox_forget_b, v_mla_q_norm_g, v_mla_kv_norm_g, v_w_uq, v_w_ukv, v_w_br_mla, v_w_br_fox, v_w_out, v_post_norm_g):
    me = 2 * lax.axis_index("x") + lax.axis_index("y")
    core = lax.axis_index("c")
    w_in_b = w_in.astype(BF16).reshape(D_MODEL, W_IN_SHARD)
    p2 = _pack_p2(w_uq[0], w_ukv[0], w_br_mla[0], w_br_fox[0], w_out[0], BF16)
    w_in_g, meta_g = _gather_weights([w_in_b], meta_tokens)
    p2_g = _gather_late(lax.optimization_barrier((p2, w_in_g))[0])
    slabs = [jnp.where(me == j, w_in_b, w_in_g[j]) for j in range(N_CHIPS)]
    chip = lax.broadcasted_iota(jnp.int32, (N_CHIPS, 1, 1), 0)
    p2_all = jnp.where(chip == me, p2[None], p2_g)
    w_uq_f = p2_all[:, 0:96].reshape(N_CHIPS, 256, 384).transpose(1, 0, 2).reshape(256, 1536)
    w_ukv_f = p2_all[:, 96:160].reshape(N_CHIPS, 128, 512).transpose(1, 0, 2).reshape(128, 2048)
    w_bm, w_bf, w_o = (p2_all[:, lo:lo + 256].reshape(D_MODEL, D_MODEL) for lo in (160, 416, 672))
    meta_f = jnp.where(chip == me, meta_tokens[None], meta_g).transpose(1, 0, 2).reshape(N_META, D_MODEL)
    kpe = _in_cols(slabs, C_KPE, C_ZMLA)
    w_small = jnp.concatenate(_in_cols(slabs, C_CQ, C_KPE) + kpe + kpe + [jnp.zeros((D_MODEL, 64), BF16)]
                              + _in_cols(slabs, C_FL, C_ZFOX) + [jnp.zeros((D_MODEL, 112), BF16)], axis=1)
    w_attn = jnp.concatenate(_in_cols(slabs, C_FQ, C_FL), axis=1)
    w_gate = jnp.concatenate(_in_cols(slabs, C_ZMLA, C_FQ) + _in_cols(slabs, C_ZFOX, C_END), axis=1)

    exchange = {}

    def start_exchange(d_w_in, d_w_uq, d_w_ukv, d_w_bm, d_w_bf, d_w_out):
        g2 = jnp.concatenate(
            [d_w_uq.reshape(256, N_CHIPS, 384).transpose(1, 0, 2).reshape(N_CHIPS, 96, D_MODEL),
             d_w_ukv.reshape(128, N_CHIPS, 512).transpose(1, 0, 2).reshape(N_CHIPS, 64, D_MODEL)]
            + [g.reshape(N_CHIPS, 256, D_MODEL) for g in (d_w_bm, d_w_bf, d_w_out)], axis=1)
        pieces = [p[None] for p in d_w_in]
        from_sib = _swap_halves(pieces + [g2])
        halves = [_add_cores(p, s, "add_cores_" + nm)[0]
                  for p, s, nm in zip(pieces, from_sib, ("small", "z", "fq", "fk", "fv", "g"))]
        parts = [_w_in_slabs(halves), _add_cores(g2, from_sib[-1], "add_cores_rest")]
        exchange.update(parts=parts, landed=_scatter_chips(parts))
        return parts[0][0, 0:16, 0:LANES]

    (loss_p, dx, dmeta, _, _, _, _, _, _, dg_pre, dg_post, dg_q, dg_kv,
     dfb) = _local_step(x[0], loss_target[0], meta_f, w_small, w_attn, w_gate, w_uq_f, w_ukv_f, w_bm, w_bf, w_o,
                        pre_norm_g, post_norm_g, mla_q_norm_g, mla_kv_norm_g, fox_forget_b, start_exchange)

    mine = [_add_chips(l, lax.dynamic_index_in_dim(p, me, 0, keepdims=False), nm)
            for l, p, nm in zip(exchange["landed"], exchange["parts"], ("add_chips_w_in", "add_chips_rest"))]
    theirs = _swap_reduced(mine)
    g_w_in, g_p2 = [jnp.concatenate([jnp.where(core == 0, a, b), jnp.where(core == 0, b, a)], axis=0)
                    for a, b in zip(mine, theirs)]
    g_w_uq, g_w_ukv, g_w_bm, g_w_bf, g_w_out = _unpack_p2(g_p2)
    g_w_in = g_w_in[None]

    vec = jnp.concatenate([dg_pre.reshape(8, 128), dg_post.reshape(8, 128), dg_q.reshape(2, 128), dg_kv,
                           dfb, _pad_lanes(loss_p), jnp.zeros((3, 128), F32), dmeta.reshape(128, 128)], axis=0)
    tot = _allreduce_small(vec)
    loss = tot[20, 0]
    g_meta = lax.dynamic_slice_in_dim(tot[24:].reshape(N_META, D_MODEL), 256 * me, 256, axis=1)

    def small_pack(pre, post, gq_, gkv_, fb_):
        return jnp.concatenate([pre.reshape(8, 128), post.reshape(8, 128), gq_.reshape(2, 128), gkv_,
                                _pad_lanes(fb_), jnp.zeros((4, 128), F32)], axis=0)

    def small_unpack(t):
        return (t[0:8].reshape(1, 1024), t[8:16].reshape(1, 1024), t[16:18].reshape(1, 256), t[18:19],
                t[19:20, 0:HEADS])

    g_small = jnp.concatenate([tot[0:20], jnp.zeros((4, 128), F32)], axis=0)
    sm = _adamw(small_pack(pre_norm_g, post_norm_g, mla_q_norm_g, mla_kv_norm_g, fox_forget_b), g_small,
                small_pack(m_pre_norm_g, m_post_norm_g, m_mla_q_norm_g, m_mla_kv_norm_g, m_fox_forget_b),
                small_pack(v_pre_norm_g, v_post_norm_g, v_mla_q_norm_g, v_mla_kv_norm_g, v_fox_forget_b),
                "adamw_small")
    g_pre, g_post, g_q, g_kv, g_fb = small_unpack(g_small)
    (d_pre, d_post, d_q, d_kv, d_fb), (nm_pre, nm_post, nm_q, nm_kv, nm_fb), (nv_pre, nv_post, nv_q, nv_kv, nv_fb) = (
        small_unpack(t) for t in sm)

    d_meta, nm_meta, nv_meta = _adamw(meta_tokens, g_meta, m_meta_tokens, v_meta_tokens, "adamw_meta")
    d_win, nm_win, nv_win = (t.T[None] for t in _adamw(w_in[0].T, g_w_in[0].T, m_w_in[0].T, v_w_in[0].T,
                                                       "adamw_w_in"))
    d_wuq, nm_wuq, nv_wuq = _adamw(w_uq[0], g_w_uq, m_w_uq[0], v_w_uq[0], "adamw_w_uq")
    d_wukv, nm_wukv, nv_wukv = _adamw(w_ukv[0], g_w_ukv, m_w_ukv[0], v_w_ukv[0], "adamw_w_ukv")
    d_wbm, nm_wbm, nv_wbm = _adamw(w_br_mla[0], g_w_bm, m_w_br_mla[0], v_w_br_mla[0], "adamw_w_br_mla")
    d_wbf, nm_wbf, nv_wbf = _adamw(w_br_fox[0], g_w_bf, m_w_br_fox[0], v_w_br_fox[0], "adamw_w_br_fox")
    d_wo, nm_wo, nv_wo = _adamw(w_out[0], g_w_out, m_w_out[0], v_w_out[0], "adamw_w_out")

    def group(meta_, pre, win, fb_, q_, kv_, wuq, wukv, wbm, wbf, wo, post):
        return (meta_, pre, win, fb_, q_, kv_, wuq[None], wukv[None], wbm[None], wbf[None], wo[None], post)

    grads = group(g_meta, g_pre, g_w_in, g_fb, g_q, g_kv, g_w_uq, g_w_ukv, g_w_bm, g_w_bf, g_w_out, g_post)
    deltas = group(d_meta, d_pre, d_win, d_fb, d_q, d_kv, d_wuq, d_wukv, d_wbm, d_wbf, d_wo, d_post)
    new_m = group(nm_meta, nm_pre, nm_win, nm_fb, nm_q, nm_kv, nm_wuq, nm_wukv, nm_wbm, nm_wbf, nm_wo, nm_post)
    new_v = group(nv_meta, nv_pre, nv_win, nv_fb, nv_q, nv_kv, nv_wuq, nv_wukv, nv_wbm, nv_wbf, nv_wo, nv_post)
    return (loss, dx[None], *grads, *deltas, *new_m, *new_v)
```

```python
import math

import jax
import jax.numpy as jnp
import numpy as np
from jax import lax
from jax.experimental import pallas as pl
from jax.experimental.pallas import tpu as pltpu
from jax.experimental.pallas import tpu_sc as plsc

F32 = jnp.float32
BF16 = jnp.bfloat16

D_MODEL = 1024
N_META = 16
RMS_EPS = 1e-6
HEADS = 16
PAIRS = HEADS // 2
HEAD_DIM = 64
LANES = 128
MLA_ROPE = 32
MLA_SCALE = 1.0 / math.sqrt(64 + 32)
FOX_SCALE = 1.0 / math.sqrt(64)
ROPE_THETA = 10000.0

PAD = 256
BLK = 256
QB = 512
UNROLL = 4
NEG = -1e30

C_CQ, C_CKV, C_KPE, C_ZMLA, C_FQ, C_FK, C_FV, C_FL, C_ZFOX, C_GA, C_GB, C_END = (
    0, 256, 384, 416, 1440, 2464, 3488, 4512, 4528, 5552, 6576, 7600)
SMALL_W = 640
W_IN_SHARD = 1900

P2_ROWS = 928
N_CHIPS = 4

ADAM_LR = 0.001
ADAM_B1 = 0.9
ADAM_B2 = 0.999
ADAM_EPS = 1e-08
ADAM_WD = 0.01
ADAM_STEP = 10

VMEM_BIG = 56 * 1024 * 1024
MM_VMEM_BUDGET = 44 * 1024 * 1024
MESH = pl.DeviceIdType.MESH


def _cp(dims, vmem=None):
    return pltpu.CompilerParams(dimension_semantics=dims, vmem_limit_bytes=vmem)


def _dot(a, b, ca, cb):
    return lax.dot_general(a, b, (((ca,), (cb,)), ((), ())), preferred_element_type=F32)


def _sigmoid(x):
    return 1.0 / (1.0 + jnp.exp(-x))


def _tile(n, cands):
    for c in cands:
        if n % c == 0:
            return c
    return n


def _mm(a, b, *, mode, out_dtype, name, acc=None, epilogue=None, row_ins=(), after=None):
    if mode == "nn":
        (M, K), N = a.shape, b.shape[1]
    elif mode == "nt":
        (M, K), N = a.shape, b.shape[0]
    else:
        (K, M), N = a.shape, b.shape[1]
    tm = _tile(M, (1088, 1024)) if M > 1024 else M
    tn = _tile(N, (1024,)) if N > 1024 else N
    nk = 1
    while True:
        tk = K // nk
        need = 2 * tk * (tm * a.dtype.itemsize + tn * b.dtype.itemsize) + tm * tn * (
            2 * jnp.dtype(out_dtype).itemsize + (8 if acc is not None else 0) + (4 if nk > 1 else 0))
        if need <= MM_VMEM_BUDGET or (tk // 2) % (16 if mode == "tn" else LANES) or tk <= 512:
            break
        nk *= 2
    ca, cb = {"nn": (1, 0), "nt": (1, 1), "tn": (0, 0)}[mode]
    a_spec = (pl.BlockSpec((tk, tm), lambda j, i, k: (k, i)) if mode == "tn"
              else pl.BlockSpec((tm, tk), lambda j, i, k: (i, k)))
    b_spec = (pl.BlockSpec((tn, tk), lambda j, i, k: (j, k)) if mode == "nt"
              else pl.BlockSpec((tk, tn), lambda j, i, k: (k, j)))
    o_spec = pl.BlockSpec((tm, tn), lambda j, i, k: (i, j))
    has_acc = acc is not None

    nrow = len(row_ins)

    def body(*refs):
        a_ref, b_ref = refs[0], refs[1]
        acc_ref = refs[2] if has_acc else None
        rows = refs[2 + has_acc:2 + has_acc + nrow]
        o_ref = refs[2 + has_acc + nrow + (after is not None)]

        def store(tile):
            if epilogue is not None:
                tile = epilogue(tile, *[r[...] for r in rows])
            o_ref[...] = tile.astype(out_dtype)

        part = _dot(a_ref[...].astype(BF16), b_ref[...].astype(BF16), ca, cb)
        if nk == 1:
            store(part + acc_ref[...] if has_acc else part)
        else:
            sc = refs[-1]
            k = pl.program_id(2)

            @pl.when(k == 0)
            def _():
                sc[...] = part + acc_ref[...] if has_acc else part

            @pl.when(k > 0)
            def _():
                sc[...] += part

            @pl.when(k == nk - 1)
            def _():
                store(sc[...])

    ins = [a, b] + ([acc] if has_acc else []) + list(row_ins)
    in_specs = ([a_spec, b_spec] + ([o_spec] if has_acc else [])
                + [pl.BlockSpec((tm, r.shape[1]), lambda j, i, k: (i, 0)) for r in row_ins])
    if after is not None:
        ins.append(after)
        in_specs.append(pl.BlockSpec(after.shape, lambda j, i, k: (0,) * after.ndim))
    return pl.pallas_call(
        body, name=name, grid=(N // tn, M // tm, nk), in_specs=in_specs, out_specs=o_spec,
        out_shape=jax.ShapeDtypeStruct((M, N), out_dtype),
        scratch_shapes=[pltpu.VMEM((tm, tn), F32)] if nk > 1 else [],
        compiler_params=_cp(("parallel", "parallel", "arbitrary"), VMEM_BIG))(*ins)


def _mm_sum_nt(pairs, *, name, after=None):
    n = len(pairs)
    M, N = pairs[0][0].shape[0], pairs[0][1].shape[0]
    tm = _tile(M, (272,))

    def body(*refs):
        o_ref = refs[2 * n + (after is not None)]
        tot = _dot(refs[0][...].astype(BF16), refs[n][...].astype(BF16), 1, 1)
        for i in range(1, n):
            tot = tot + _dot(refs[i][...].astype(BF16), refs[n + i][...].astype(BF16), 1, 1)
        o_ref[...] = tot

    ins = [a for a, _ in pairs] + [b for _, b in pairs]
    in_specs = ([pl.BlockSpec((tm, a.shape[1]), lambda i: (i, 0)) for a, _ in pairs]
                + [pl.BlockSpec(b.shape, lambda i: (0, 0)) for _, b in pairs])
    if after is not None:
        ins.append(after)
        in_specs.append(pl.BlockSpec(after.shape, lambda i: (0,) * after.ndim))
    return pl.pallas_call(
        body, name=name, grid=(M // tm,), in_specs=in_specs, out_specs=pl.BlockSpec((tm, N), lambda i: (i, 0)),
        out_shape=jax.ShapeDtypeStruct((M, N), F32), compiler_params=_cp(("parallel",), VMEM_BIG))(*ins)


def _row(w):
    return pl.BlockSpec((BLK, w), lambda i: (i, 0))


def _rowc(w, c):
    return pl.BlockSpec((BLK, w), lambda i: (i, c))


def _full(shape):
    return pl.BlockSpec(shape, lambda i: tuple(0 for _ in shape))


def _rope(x, c, s):
    lane = lax.broadcasted_iota(jnp.int32, x.shape, 1)
    is_x1 = ((lane >> 4) & 1) == 0
    partner = jnp.where(is_x1, pltpu.roll(x, LANES - 16, 1), pltpu.roll(x, 16, 1))
    return x * c + partner * s


def _row_valid(i):
    rows = i * BLK + lax.broadcasted_iota(jnp.int32, (BLK, 1), 0)
    return (rows < N_META) | (rows >= PAD)


def _shift_rows(w):
    return pl.BlockSpec((BLK, w), lambda i: (jnp.maximum(i - 1, 0), 0))


def _h_block(i, x_ref, meta_ref):
    head = jnp.concatenate([meta_ref[...], jnp.zeros((BLK - N_META, D_MODEL), F32)], axis=0)
    return jnp.where(i == 0, head, x_ref[...])


def _rms_pre(x2, meta, g):
    lp = PAD + x2.shape[0]

    def body(x_ref, meta_ref, g_ref, u_ref, ut_ref):
        hv = _h_block(pl.program_id(0), x_ref, meta_ref)
        r = lax.rsqrt(jnp.mean(hv * hv, axis=-1, keepdims=True) + RMS_EPS)
        u = hv * r * g_ref[...]
        u_ref[...] = u.astype(BF16)
        ut_ref[...] = u.T.astype(BF16)

    return pl.pallas_call(
        body, name="rms_pre", grid=(lp // BLK,),
        in_specs=[_shift_rows(D_MODEL), _full((N_META, D_MODEL)), _full((1, D_MODEL))],
        out_specs=[_row(D_MODEL), pl.BlockSpec((D_MODEL, BLK), lambda i: (0, i))],
        out_shape=[jax.ShapeDtypeStruct((lp, D_MODEL), BF16), jax.ShapeDtypeStruct((D_MODEL, lp), BF16)],
        compiler_params=_cp(("parallel",)))(x2, meta, g)


def _split3(x):
    hi = x.astype(BF16)
    r1 = x - hi.astype(F32)
    mid = r1.astype(BF16)
    lo = (r1 - mid.astype(F32)).astype(BF16)
    return hi, mid, lo


def _small_prep(small, gq, gkv, fb, ctab, stab, tri):
    lp = small.shape[0]

    def body(sm_ref, gq_ref, gkv_ref, fb_ref, c_ref, s_ref, tri_ref, qn_ref, kvn_ref, kr_ref, ncum_ref, carry):
        i = pl.program_id(0)

        @pl.when(i == 0)
        def _():
            carry[...] = jnp.zeros_like(carry)

        cq = sm_ref[:, 0:256]
        r = lax.rsqrt(jnp.mean(cq * cq, axis=-1, keepdims=True) + RMS_EPS)
        qn_ref[...] = (cq * r * gq_ref[...]).astype(BF16)
        ckv = sm_ref[:, 256:384]
        r = lax.rsqrt(jnp.mean(ckv * ckv, axis=-1, keepdims=True) + RMS_EPS)
        kvn_ref[...] = (ckv * r * gkv_ref[...]).astype(BF16)
        kr_ref[...] = _rope(sm_ref[:, 384:512], c_ref[...], s_ref[...]).astype(BF16)
        fl = sm_ref[:, 512:640] + fb_ref[...]
        lf = jnp.minimum(fl, 0.0) - jnp.log(1.0 + jnp.exp(-jnp.abs(fl)))
        lf = jnp.where(_row_valid(i), lf, 0.0)
        hi, mid, lo = _split3(lf)
        t = tri_ref[...]
        cum = (_dot(t, hi, 1, 0) + _dot(t, mid, 1, 0)) + _dot(t, lo, 1, 0) + carry[...]
        ncum_ref[...] = -cum
        carry[...] = -ncum_ref[BLK - 1:BLK, :]

    return pl.pallas_call(
        body, name="small_prep", grid=(lp // BLK,),
        in_specs=[_row(SMALL_W), _full((1, 256)), _full((1, 128)), _full((1, 128)), _row(128), _row(128),
                  _full((BLK, BLK))],
        out_specs=[_row(256), _row(128), _row(128), _row(128)],
        out_shape=[jax.ShapeDtypeStruct((lp, 256), BF16), jax.ShapeDtypeStruct((lp, 128), BF16),
                   jax.ShapeDtypeStruct((lp, 128), BF16), jax.ShapeDtypeStruct((lp, 128), F32)],
        scratch_shapes=[pltpu.VMEM((1, 128), F32)],
        compiler_params=_cp(("arbitrary",)))(small, gq, gkv, fb, ctab, stab, tri)


def _rope_pairs(tile, c, s):
    out = []
    for lo in range(0, tile.shape[1], 256):
        out += [tile[:, lo:lo + 128], _rope(tile[:, lo + 128:lo + 256], c, s)]
    return jnp.concatenate(out, axis=1)


def _gate_fwd(o_mla, o_fox, gate):
    lp = o_mla.shape[0]

    def body(om_ref, of_ref, zm_ref, zf_ref, am_ref, af_ref):
        zm = zm_ref[...].astype(F32)
        am_ref[...] = (om_ref[...] * (zm * _sigmoid(zm))).astype(BF16)
        zf = zf_ref[...].astype(F32)
        af_ref[...] = (of_ref[...] * (zf * _sigmoid(zf))).astype(BF16)

    return pl.pallas_call(
        body, name="gate_fwd", grid=(lp // BLK,),
        in_specs=[_row(D_MODEL), _row(D_MODEL), _rowc(D_MODEL, 0), _rowc(D_MODEL, 1)],
        out_specs=[_row(D_MODEL), _row(D_MODEL)],
        out_shape=[jax.ShapeDtypeStruct((lp, D_MODEL), BF16)] * 2,
        compiler_params=_cp(("parallel",)))(o_mla, o_fox, gate, gate)


def _merge_fwd(gate, y_mla, y_fox):
    lp = y_mla.shape[0]

    def body(ga_ref, gb_ref, ym_ref, yf_ref, m_ref):
        sa = _sigmoid(ga_ref[...].astype(F32))
        sb = _sigmoid(gb_ref[...].astype(F32))
        m_ref[...] = (sa * ym_ref[...] + sb * yf_ref[...]).astype(BF16)

    return pl.pallas_call(
        body, name="merge_fwd", grid=(lp // BLK,),
        in_specs=[_rowc(D_MODEL, 2), _rowc(D_MODEL, 3), _row(D_MODEL), _row(D_MODEL)],
        out_specs=_row(D_MODEL), out_shape=jax.ShapeDtypeStruct((lp, D_MODEL), BF16),
        compiler_params=_cp(("parallel",)))(gate, gate, y_mla, y_fox)


def _tail(x2, mixed, tgt, gpost):
    lp = mixed.shape[0]
    shift = _shift_rows(D_MODEL)

    def body(h_ref, mx_ref, t_ref, g_ref, dmx_ref, dy_ref, loss_ref, dg_ref):
        i = pl.program_id(0)

        @pl.when(i == 0)
        def _():
            loss_ref[...] = jnp.zeros_like(loss_ref)
            dg_ref[...] = jnp.zeros_like(dg_ref)
            dmx_ref[...] = jnp.zeros_like(dmx_ref)
            dy_ref[...] = jnp.zeros_like(dy_ref)

        @pl.when(i > 0)
        def _():
            mx = mx_ref[...]
            g = g_ref[...]
            r = lax.rsqrt(jnp.mean(mx * mx, axis=-1, keepdims=True) + RMS_EPS)
            nrm = mx * r
            e = (h_ref[...] + nrm * g) - t_ref[...]
            loss_ref[...] += jnp.sum(0.5 * jnp.sum(e * e, axis=-1, keepdims=True) * (1.0 / D_MODEL),
                                     axis=0, keepdims=True)
            dy = e * (1.0 / D_MODEL)
            dy_ref[...] = dy
            dg_ref[...] += jnp.sum(dy * nrm, axis=0, keepdims=True)
            w = dy * g
            dot = jnp.mean(w * mx, axis=-1, keepdims=True)
            dmx_ref[...] = (r * w - mx * (r * r * r * dot)).astype(BF16)

    return pl.pallas_call(
        body, name="tail", grid=(lp // BLK,),
        in_specs=[shift, _row(D_MODEL), shift, _full((1, D_MODEL))],
        out_specs=[_row(D_MODEL), _row(D_MODEL), _full((1, 1)), _full((1, D_MODEL))],
        out_shape=[jax.ShapeDtypeStruct((lp, D_MODEL), BF16), jax.ShapeDtypeStruct((lp, D_MODEL), F32),
                   jax.ShapeDtypeStruct((1, 1), F32), jax.ShapeDtypeStruct((1, D_MODEL), F32)],
        compiler_params=_cp(("arbitrary",)))(x2, mixed, tgt, gpost)


def _merge_bwd(dm, gate, y_mla, y_fox):
    lp = dm.shape[0]

    def body(dm_ref, ga_ref, gb_ref, ym_ref, yf_ref, dym_ref, dyf_ref, dg_ref):
        dm_v = dm_ref[...].astype(F32)
        sa = _sigmoid(ga_ref[...].astype(F32))
        sb = _sigmoid(gb_ref[...].astype(F32))
        dym_ref[...] = (dm_v * sa).astype(BF16)
        dyf_ref[...] = (dm_v * sb).astype(BF16)
        dg_ref[:, 0:D_MODEL] = (dm_v * ym_ref[...] * (sa * (1.0 - sa))).astype(BF16)
        dg_ref[:, D_MODEL:2 * D_MODEL] = (dm_v * yf_ref[...] * (sb * (1.0 - sb))).astype(BF16)

    return pl.pallas_call(
        body, name="merge_bwd", grid=(lp // BLK,),
        in_specs=[_row(D_MODEL), _rowc(D_MODEL, 2), _rowc(D_MODEL, 3), _row(D_MODEL), _row(D_MODEL)],
        out_specs=[_row(D_MODEL), _row(D_MODEL), _row(2 * D_MODEL)],
        out_shape=[jax.ShapeDtypeStruct((lp, D_MODEL), BF16), jax.ShapeDtypeStruct((lp, D_MODEL), BF16),
                   jax.ShapeDtypeStruct((lp, 2 * D_MODEL), BF16)],
        compiler_params=_cp(("parallel",)))(dm, gate, gate, y_mla, y_fox)


def _gate_bwd(da_mla, da_fox, o_mla, o_fox, gate):
    lp = da_mla.shape[0]

    def one(da, o, z, head_of_col):
        sg = _sigmoid(z)
        do = (da * (z * sg)).astype(BF16)
        dz = da * o * (sg * (1.0 + z * (1.0 - sg)))
        delta = sum(_dot(part, head_of_col, 1, 0) for part in _split3(do.astype(F32) * o))
        return do, dz.astype(BF16), delta

    def body(dam_ref, daf_ref, om_ref, of_ref, zm_ref, zf_ref, dom_ref, dof_ref, dz_ref, dlm_ref, dlf_ref):
        f32 = lambda r: r[...].astype(F32)
        head_of_col = (lax.broadcasted_iota(jnp.int32, (D_MODEL, LANES), 0) // HEAD_DIM
                       == lax.broadcasted_iota(jnp.int32, (D_MODEL, LANES), 1)).astype(BF16)
        dom_ref[...], dz_ref[:, 0:D_MODEL], dlm_ref[...] = one(f32(dam_ref), f32(om_ref), f32(zm_ref), head_of_col)
        dof_ref[...], dz_ref[:, D_MODEL:2 * D_MODEL], dlf_ref[...] = one(f32(daf_ref), f32(of_ref), f32(zf_ref),
                                                                        head_of_col)

    return pl.pallas_call(
        body, name="gate_bwd", grid=(lp // BLK,),
        in_specs=[_row(D_MODEL)] * 4 + [_rowc(D_MODEL, 0), _rowc(D_MODEL, 1)],
        out_specs=[_row(D_MODEL), _row(D_MODEL), _row(2 * D_MODEL), _row(LANES), _row(LANES)],
        out_shape=[jax.ShapeDtypeStruct((lp, D_MODEL), BF16), jax.ShapeDtypeStruct((lp, D_MODEL), BF16),
                   jax.ShapeDtypeStruct((lp, 2 * D_MODEL), BF16), jax.ShapeDtypeStruct((lp, LANES), F32),
                   jax.ShapeDtypeStruct((lp, LANES), F32)],
        compiler_params=_cp(("parallel",)))(da_mla, da_fox, o_mla, o_fox, gate, gate)


def _small_bwd(small, dqn, dkvn, dkr, dcol_t, drow_t, gq, gkv, fb, ctab, stab, triu):
    lp = small.shape[0]
    nb = lp // BLK

    def rrow(w):
        return pl.BlockSpec((BLK, w), lambda i: (nb - 1 - i, 0))

    def body(sm_ref, dqn_ref, dkvn_ref, dkr_ref, dcol_ref, drow_ref, gq_ref, gkv_ref, fb_ref, c_ref, s_ref, tri_ref,
             ds_ref, dgq_ref, dgkv_ref, dfb_ref, carry):
        i = pl.program_id(0)

        @pl.when(i == 0)
        def _():
            carry[...] = jnp.zeros_like(carry)
            dgq_ref[...] = jnp.zeros_like(dgq_ref)
            dgkv_ref[...] = jnp.zeros_like(dgkv_ref)
            dfb_ref[...] = jnp.zeros_like(dfb_ref)

        def norm_bwd(x, dn, g, dg_ref):
            r = lax.rsqrt(jnp.mean(x * x, axis=-1, keepdims=True) + RMS_EPS)
            dg_ref[...] += jnp.sum(dn * (x * r), axis=0, keepdims=True)
            w = dn * g
            dot = jnp.mean(w * x, axis=-1, keepdims=True)
            return r * w - x * (r * r * r * dot)

        ds_ref[:, 0:256] = norm_bwd(sm_ref[:, 0:256], dqn_ref[...], gq_ref[...], dgq_ref).astype(BF16)
        ds_ref[:, 256:384] = norm_bwd(sm_ref[:, 256:384], dkvn_ref[...], gkv_ref[...], dgkv_ref).astype(BF16)

        dk = dkr_ref[0]
        for p in range(1, PAIRS):
            dk = dk + dkr_ref[p]
        dk = _rope(dk, c_ref[...], -s_ref[...])
        lane = lax.broadcasted_iota(jnp.int32, dk.shape, 1)
        dk = jnp.where(lane < MLA_ROPE, dk + pltpu.roll(dk, LANES - MLA_ROPE, 1), 0.0)
        ds_ref[:, 384:512] = dk.astype(BF16)

        dcol = dcol_ref[0]
        for p in range(1, PAIRS):
            dcol = dcol + pltpu.roll(dcol_ref[p], 2 * p, 1)
        rows16 = jnp.concatenate([drow_ref[p, h:h + 1, :] for p in range(PAIRS) for h in range(2)], axis=0)
        eye = (lax.broadcasted_iota(jnp.int32, (HEADS, LANES), 0)
               == lax.broadcasted_iota(jnp.int32, (HEADS, LANES), 1)).astype(BF16)
        drow = sum(_dot(part, eye, 0, 0) for part in _split3(rows16))
        dcr = dcol - drow
        hi, mid, lo = _split3(dcr)
        t = tri_ref[...]
        suf = (_dot(t, hi, 1, 0) + _dot(t, mid, 1, 0)) + _dot(t, lo, 1, 0) + carry[...]
        fl = sm_ref[:, 512:640] + fb_ref[...]
        dfl = jnp.where(_row_valid(nb - 1 - i), -suf * _sigmoid(-fl), 0.0)
        ds_ref[:, 512:640] = dfl.astype(BF16)
        dfb_ref[...] += jnp.sum(dfl, axis=0, keepdims=True)
        carry[...] += jnp.sum(dcr, axis=0, keepdims=True)

    return pl.pallas_call(
        body, name="small_bwd", grid=(nb,),
        in_specs=[rrow(SMALL_W), rrow(256), rrow(128),
                  pl.BlockSpec((PAIRS, BLK, 128), lambda i: (0, nb - 1 - i, 0)),
                  pl.BlockSpec((PAIRS, BLK, 128), lambda i: (0, nb - 1 - i, 0)),
                  pl.BlockSpec((PAIRS, 2, BLK), lambda i: (0, 0, nb - 1 - i)),
                  _full((1, 256)), _full((1, 128)), _full((1, 128)), rrow(128), rrow(128), _full((BLK, BLK))],
        out_specs=[rrow(SMALL_W), _full((1, 256)), _full((1, 128)), _full((1, 128))],
        out_shape=[jax.ShapeDtypeStruct((lp, SMALL_W), BF16), jax.ShapeDtypeStruct((1, 256), F32),
                   jax.ShapeDtypeStruct((1, 128), F32), jax.ShapeDtypeStruct((1, 128), F32)],
        scratch_shapes=[pltpu.VMEM((1, 128), F32)],
        compiler_params=_cp(("arbitrary",)))(small, dqn, dkvn, dkr, dcol_t, drow_t, gq, gkv, fb, ctab, stab, triu)


def _pre_bwd(du, x2, meta, dy, gpre):
    s_rows = x2.shape[0]
    lp = PAD + s_rows
    shift = _shift_rows(D_MODEL)

    def body(du_ref, x_ref, meta_ref, dy_ref, g_ref, dx_ref, dmeta_ref, dg_ref):
        i = pl.program_id(0)

        @pl.when(i == 0)
        def _():
            dg_ref[...] = jnp.zeros_like(dg_ref)

        hv = _h_block(i, x_ref, meta_ref)
        duv = du_ref[...]
        r = lax.rsqrt(jnp.mean(hv * hv, axis=-1, keepdims=True) + RMS_EPS)
        dg_ref[...] += jnp.sum(duv * (hv * r), axis=0, keepdims=True)
        w = duv * g_ref[...]
        dot = jnp.mean(w * hv, axis=-1, keepdims=True)
        dh = dy_ref[...] + (r * w - hv * (r * r * r * dot))
        dx_ref[...] = dh

        @pl.when(i == 0)
        def _():
            dmeta_ref[...] = dh[0:N_META, :]

    return pl.pallas_call(
        body, name="pre_bwd", grid=(lp // BLK,),
        in_specs=[_row(D_MODEL), shift, _full((N_META, D_MODEL)), _row(D_MODEL), _full((1, D_MODEL))],
        out_specs=[shift, _full((N_META, D_MODEL)), _full((1, D_MODEL))],
        out_shape=[jax.ShapeDtypeStruct((s_rows, D_MODEL), F32), jax.ShapeDtypeStruct((N_META, D_MODEL), F32),
                   jax.ShapeDtypeStruct((1, D_MODEL), F32)],
        compiler_params=_cp(("arbitrary",)))(du, x2, meta, dy, gpre)


def _pair_masks(rope):
    lane = lax.broadcasted_iota(jnp.int32, (1, LANES), 1)
    mas = [lane < HEAD_DIM, lane >= HEAD_DIM]
    if not rope:
        return mas, mas
    wide = lax.broadcasted_iota(jnp.int32, (1, 2 * LANES), 1)
    rope_lo = LANES + MLA_ROPE
    return mas, [(wide < HEAD_DIM) | ((wide >= LANES) & (wide < rope_lo)),
                 ((wide >= HEAD_DIM) & (wide < LANES)) | ((wide >= rope_lo) & (wide < rope_lo + MLA_ROPE))]


def _mask2(x, masks):
    return [jnp.where(m, x, jnp.zeros_like(x)) for m in masks]


def _attn_fwd(q, k, v, *, kr=None, nbrep=None, scale, qcol, kcol, vcol, name):
    lp = q.shape[0]
    nq = 1 + (lp - PAD) // QB
    rope = kr is not None
    bias = nbrep is not None
    qw = 256 if rope else 128

    def body(*refs):
        it = iter(refs)
        q_ref, k_ref, v_ref = next(it), next(it), next(it)
        kr_ref = next(it) if rope else None
        nb_ref = next(it) if bias else None
        o_ref, lse_ref = next(it), next(it)
        i = pl.program_id(1)
        r0 = pl.multiple_of(jnp.where(i == 0, 0, PAD + QB * (i - 1)), BLK)
        b0 = r0 // BLK
        mas, hmask = _pair_masks(rope)
        qh = _mask2(q_ref[pl.ds(r0, QB), :], hmask)
        if bias:
            qh = [x * scale for x in qh]

        def update(kcs, carry, masks, ns=None, q_lo=0, wq=QB):
            ns = ns or [BLK] * len(kcs)
            stats, acc = carry[:4], carry[4]
            qs = [x[q_lo:q_lo + wq] for x in qh]
            k0s = [pl.multiple_of(kc * BLK, BLK) for kc in kcs]
            kks = [k_ref[pl.ds(k0, n), :] for k0, n in zip(k0s, ns)]
            if rope:
                kks = [jnp.concatenate([kk, kr_ref[pl.ds(k0, n), :]], axis=1) for kk, k0, n in zip(kks, k0s, ns)]
            new_stats, alphas, ps = [], [], [[] for _ in kcs]
            for h in range(2):
                m_prev, l_prev = stats[2 * h], stats[2 * h + 1]
                ss = []
                for kk, k0, n, mask in zip(kks, k0s, ns, masks):
                    s = _dot(kk, qs[h], 1, 1)
                    if rope:
                        s = s * scale
                    if bias:
                        nbc = nb_ref[h, pl.ds(k0, n), :]
                        s = s + jnp.concatenate([nbc] * (wq // LANES), axis=1)
                    if mask is not None:
                        s = jnp.where(mask, s, NEG)
                    ss.append(s)
                m_new = m_prev
                for s in ss:
                    m_new = jnp.maximum(m_new, jnp.max(s, axis=0, keepdims=True))
                alpha = jnp.exp(m_prev - m_new)
                l_new = alpha * l_prev
                for j, s in enumerate(ss):
                    p = jnp.exp(s - m_new)
                    l_new = l_new + jnp.sum(p, axis=0, keepdims=True)
                    ps[j].append(p.astype(BF16))
                new_stats += [m_new, l_new]
                alphas.append(alpha)
            vcat = jnp.concatenate([x for k0, n in zip(k0s, ns) for x in _mask2(v_ref[pl.ds(k0, n), :], mas)], axis=0)
            pv = _dot(vcat, jnp.concatenate([p for pj in ps for p in pj], axis=0), 0, 0)
            a_full = jnp.concatenate([jnp.broadcast_to(a, (HEAD_DIM, wq)) for a in alphas], axis=0)
            return (*new_stats, a_full * acc + pv)

        neg = jnp.full((1, QB), NEG, F32)
        zero = jnp.zeros((1, QB), F32)
        c = (neg, zero, neg, zero, jnp.zeros((LANES, QB), F32))
        n_mid = jnp.maximum(b0 - 1, 0)
        c = lax.fori_loop(0, n_mid // 4, lambda t, cr: update([4 * t + u for u in (1, 2, 3, 4)], cr, [None] * 4), c)
        c = lax.fori_loop(0, (n_mid % 4) // 2, lambda t, cr: update([n_mid - 1, n_mid], cr, [None, None]), c)
        key_l = lax.broadcasted_iota(jnp.int32, (BLK, BLK), 0)
        qry_l = lax.broadcasted_iota(jnp.int32, (BLK, BLK), 1)
        tri = (key_l <= qry_l) & (b0 > 0)
        meta_ok = (key_l[0:N_META] <= qry_l[0:N_META]) | (b0 > 0)
        lo = update([0, b0], tuple(a[:, 0:BLK] for a in c), [meta_ok, tri], ns=[N_META, BLK], q_lo=0, wq=BLK)
        hi = update([0, b0, b0 + 1], tuple(a[:, BLK:QB] for a in c), [None, None, tri], ns=[N_META, BLK, BLK],
                    q_lo=BLK, wq=QB - BLK)
        c = tuple(jnp.concatenate([a, b], axis=1) for a, b in zip(lo, hi))
        inv =jnp.concatenate([jnp.broadcast_to(1.0 / c[1], (HEAD_DIM, QB)),
                               jnp.broadcast_to(1.0 / c[3], (HEAD_DIM, QB))], axis=0)
        o_t = (c[4] * inv).T.astype(BF16)
        lses = [c[0] + jnp.log(c[1]), c[2] + jnp.log(c[3])]
        o_ref[pl.ds(r0, BLK), :] = o_t[0:BLK]
        for h in range(2):
            lse_ref[0, h:h + 1, pl.ds(r0, BLK)] = lses[h][:, 0:BLK]

        @pl.when(i > 0)
        def _():
            r1 = pl.multiple_of(r0 + BLK, BLK)
            o_ref[pl.ds(r1, QB - BLK), :] = o_t[BLK:QB]
            for h in range(2):
                lse_ref[0, h:h + 1, pl.ds(r1, QB - BLK)] = lses[h][:, BLK:QB]

    in_specs = [pl.BlockSpec((lp, qw), lambda p, i: (0, qcol + p)),
                pl.BlockSpec((lp, 128), lambda p, i: (0, kcol(p))),
                pl.BlockSpec((lp, 128), lambda p, i: (0, vcol(p)))]
    ins = [q, k, v]
    if rope:
        in_specs.append(pl.BlockSpec((lp, 128), lambda p, i: (0, 0)))
        ins.append(kr)
    if bias:
        in_specs.append(pl.BlockSpec((2, lp, 128), lambda p, i: (p, 0, 0)))
        ins.append(nbrep)
    return pl.pallas_call(
        body, name=name, grid=(PAIRS, nq), in_specs=in_specs,
        out_specs=[pl.BlockSpec((lp, 128), lambda p, i: (0, p)),
                   pl.BlockSpec((1, 2, lp), lambda p, i: (p, 0, 0))],
        out_shape=[jax.ShapeDtypeStruct((lp, D_MODEL), BF16), jax.ShapeDtypeStruct((PAIRS, 2, lp), F32)],
        compiler_params=_cp(("parallel", "arbitrary"), VMEM_BIG))(*ins)


def _attn_bwd(q, k, v, do, delta, lse, *, kr=None, rtabs=None, nbrep=None, scale, qcol, kcol, vcol, name):
    lp = q.shape[0]
    nb = lp // BLK
    rope = kr is not None
    bias = nbrep is not None
    qw = 256 if rope else 128

    def body(*refs):
        it = iter(refs)
        q_ref, k_ref, v_ref = next(it), next(it), next(it)
        kr_ref = next(it) if rope else None
        nb_ref = next(it) if bias else None
        do_ref, dl_ref, lse_ref = next(it), next(it), next(it)
        ct_ref, st_ref = (next(it), next(it)) if rope else (None, None)
        dq_out, dk_ref, dv_ref = next(it), next(it), next(it)
        x_ref = next(it)
        drow_ref = next(it) if bias else None
        dq_ref = next(it)
        kb = pl.program_id(1)
        mas, hmask = _pair_masks(rope)
        lane = lax.broadcasted_iota(jnp.int32, (1, LANES), 1)

        @pl.when(kb == 0)
        def _():
            dq_ref[...] = jnp.zeros_like(dq_ref)
            if bias:
                drow_ref[...] = jnp.zeros_like(drow_ref)

        def key_pass(n, w):
            kk = k_ref[0:n, :]
            if rope:
                kk = jnp.concatenate([kk, kr_ref[0:n, :]], axis=1)
            vh = _mask2(v_ref[0:n, :], mas)
            kcat = jnp.concatenate(_mask2(kk, hmask), axis=0)
            if bias:
                kcat = kcat * scale
            diag_mask = (lax.broadcasted_iota(jnp.int32, (n, w), 0) <= lax.broadcasted_iota(jnp.int32, (n, w), 1))

            def chunk(qc, carry, mask):
                carry = list(carry)
                q0 = qc * w if isinstance(qc, int) else pl.multiple_of(qc * w, w)
                dov = do_ref[pl.ds(q0, w), :]
                doh = _mask2(dov, mas)
                qh = _mask2(q_ref[pl.ds(q0, w), :], hmask)
                if bias:
                    qh = [x * scale for x in qh]
                pbs, dss = [], []
                for h in range(2):
                    s = _dot(kk, qh[h], 1, 1)
                    if rope:
                        s = s * scale
                    if bias:
                        s = s + jnp.concatenate([nb_ref[h, 0:n, :]] * (w // LANES), axis=1)
                    p = jnp.exp(s - lse_ref[0, h:h + 1, pl.ds(q0, w)])
                    if mask is not None:
                        p = jnp.where(mask, p, 0.0)
                    ds = p * (_dot(vh[h], dov, 1, 1) - dl_ref[0, h:h + 1, pl.ds(q0, w)])
                    if bias:
                        drow_ref[0, h:h + 1, pl.ds(q0, w)] += jnp.sum(ds, axis=0, keepdims=True)
                        carry[2 + h] = carry[2 + h] + jnp.sum(ds, axis=1, keepdims=True)
                    else:
                        ds = ds * scale
                    pbs.append(p.astype(BF16))
                    dss.append(ds.astype(BF16))
                ds_lanes = jnp.concatenate(dss, axis=1)
                ds_rows = jnp.concatenate(dss, axis=0)
                carry[0] = carry[0] + _dot(ds_lanes, jnp.concatenate(qh, axis=0), 1, 0)
                carry[1] = carry[1] + _dot(jnp.concatenate(pbs, axis=1), jnp.concatenate(doh, axis=0), 1, 0)
                dq_ref[pl.ds(q0, w), :] += _dot(ds_rows, kcat, 0, 0)
                return tuple(carry)

            c = [jnp.zeros((n, qw), F32), jnp.zeros((n, LANES), F32)]
            if bias:
                c += [jnp.zeros((n, 1), F32), jnp.zeros((n, 1), F32)]
            c = tuple(c)
            if w != BLK:
                for qc in range(lp // w):
                    c = chunk(qc, c, diag_mask if qc == 0 else None)
            else:
                groups = (nb - kb) // UNROLL

                def several(t, cr):
                    for u in range(UNROLL):
                        cr = chunk(kb + UNROLL * t + u, cr, (diag_mask | (t > 0)) if u == 0 else None)
                    return cr

                c = lax.fori_loop(0, groups, several, c)
                start = kb + UNROLL * groups
                pairs = (nb - start) // 2

                def two(t, cr):
                    qc = start + 2 * t
                    return chunk(qc + 1, chunk(qc, cr, diag_mask | (qc > kb)), None)

                c = lax.fori_loop(0, pairs, two, c)
                c = lax.fori_loop(start + 2 * pairs, nb, lambda qc, cr: chunk(qc, cr, diag_mask | (qc > kb)), c)

            def rows(a, dtype):
                a = a.astype(dtype)
                return a if n == BLK else jnp.concatenate([a, jnp.zeros((BLK - n, a.shape[1]), dtype)], axis=0)

            dk_ref[...] = rows(c[0][:, 0:LANES], BF16)
            dv_ref[...] = rows(c[1], BF16)
            if rope:
                x_ref[0] = rows(c[0][:, LANES:2 * LANES], F32)
            if bias:
                x_ref[0] = rows(jnp.where(lane == 0, c[2], jnp.where(lane == 1, c[3], 0.0)), F32)

        @pl.when(kb == 0)
        def _():
            key_pass(N_META, lp // 2)

        @pl.when(kb > 0)
        def _():
            key_pass(BLK, BLK)

        @pl.when(kb == nb - 1)
        def _():
            def fin(c, carry):
                r0 = pl.multiple_of(c * BLK, BLK)
                dq = dq_ref[pl.ds(r0, BLK), :]
                if rope:
                    back = _rope(dq[:, LANES:2 * LANES], ct_ref[pl.ds(r0, BLK), :], -st_ref[pl.ds(r0, BLK), :])
                    dq = jnp.concatenate([dq[:, 0:LANES], back], axis=1)
                dq_out[pl.ds(r0, BLK), :] = dq.astype(BF16)
                return carry

            lax.fori_loop(0, nb, fin, 0)

    in_specs = [pl.BlockSpec((lp, qw), lambda p, j: (0, qcol + p)),
                pl.BlockSpec((BLK, 128), lambda p, j: (j, kcol(p))),
                pl.BlockSpec((BLK, 128), lambda p, j: (j, vcol(p)))]
    ins = [q, k, v]
    if rope:
        in_specs.append(pl.BlockSpec((BLK, 128), lambda p, j: (j, 0)))
        ins.append(kr)
    if bias:
        in_specs.append(pl.BlockSpec((2, BLK, 128), lambda p, j: (p, j, 0)))
        ins.append(nbrep)
    in_specs += [pl.BlockSpec((lp, 128), lambda p, j: (0, p)), pl.BlockSpec((1, 2, lp), lambda p, j: (p, 0, 0)),
                 pl.BlockSpec((1, 2, lp), lambda p, j: (p, 0, 0))]
    ins += [do, delta, lse]
    if rope:
        in_specs += [pl.BlockSpec((lp, 128), lambda p, j: (0, 0))] * 2
        ins += list(rtabs)
    out_specs = [pl.BlockSpec((lp, qw), lambda p, j: (0, p)),
                 pl.BlockSpec((BLK, 128), lambda p, j: (j, p)),
                 pl.BlockSpec((BLK, 128), lambda p, j: (j, p)),
                 pl.BlockSpec((1, BLK, 128), lambda p, j: (p, j, 0))]
    out_shape = [jax.ShapeDtypeStruct((lp, PAIRS * qw), BF16), jax.ShapeDtypeStruct((lp, D_MODEL), BF16),
                 jax.ShapeDtypeStruct((lp, D_MODEL), BF16), jax.ShapeDtypeStruct((PAIRS, lp, 128), F32)]
    if bias:
        out_specs.append(pl.BlockSpec((1, 2, lp), lambda p, j: (p, 0, 0)))
        out_shape.append(jax.ShapeDtypeStruct((PAIRS, 2, lp), F32))
    return pl.pallas_call(
        body, name=name, grid=(PAIRS, nb), in_specs=in_specs, out_specs=out_specs, out_shape=out_shape,
        scratch_shapes=[pltpu.VMEM((lp, qw), F32)],
        compiler_params=_cp(("parallel", "arbitrary"), VMEM_BIG))(*ins)


def _adamw(w, g, m, v, name):
    lead = w.ndim - 2
    rows, cols = w.shape[lead:]
    big = rows * cols > 512 * 1024
    tr = 128 if big and rows % 128 == 0 else rows
    tc = 256 if big and tr == rows else cols

    def body(w_ref, g_ref, m_ref, v_ref, d_ref, nm_ref, nv_ref):
        gv = g_ref[...]
        nm = ADAM_B1 * m_ref[...] + (1.0 - ADAM_B1) * gv
        nv = ADAM_B2 * v_ref[...] + (1.0 - ADAM_B2) * (gv * gv)
        m_hat = nm / (1.0 - ADAM_B1 ** ADAM_STEP)
        v_hat = nv / (1.0 - ADAM_B2 ** ADAM_STEP)
        d_ref[...] = -ADAM_LR * (m_hat / (jnp.sqrt(v_hat) + ADAM_EPS) + ADAM_WD * w_ref[...])
        nm_ref[...] = nm
        nv_ref[...] = nv

    spec = pl.BlockSpec((1,) * lead + (tr, tc), lambda i, j: (0,) * lead + (i, j))
    return pl.pallas_call(
        body, name=name, grid=(rows // tr, cols // tc), in_specs=[spec] * 4, out_specs=[spec] * 3,
        out_shape=[jax.ShapeDtypeStruct(w.shape, F32)] * 3,
        compiler_params=_cp(("parallel", "parallel"), VMEM_BIG))(w, g, m, v)


def _add_cores(g, from_sib, name):
    n, rows, cols = g.shape
    half = rows // 2
    tr = _tile(half, (256, 240))
    nt = half // tr

    def body(lo_ref, hi_ref, s_ref, o_ref):
        mine = jnp.where(lax.axis_index("c") == 0, lo_ref[0], hi_ref[0])
        o_ref[0] = (mine.astype(F32) + s_ref[0].astype(F32)).astype(BF16)

    return pl.pallas_call(
        body, name=name, grid=(n, nt),
        in_specs=[pl.BlockSpec((1, tr, cols), lambda j, i: (j, i, 0)),
                  pl.BlockSpec((1, tr, cols), lambda j, i: (j, nt + i, 0)),
                  pl.BlockSpec((1, tr, cols), lambda j, i: (j, i, 0))],
        out_specs=pl.BlockSpec((1, tr, cols), lambda j, i: (j, i, 0)),
        out_shape=jax.ShapeDtypeStruct((n, half, cols), BF16),
        compiler_params=_cp(("parallel", "parallel"), VMEM_BIG))(g, g, from_sib)


def _add_chips(x, own, name):
    n, rows, cols = x.shape
    tr = _tile(rows, (256, 240))

    def body(x_ref, own_ref, o_ref):
        me = 2 * lax.axis_index("x") + lax.axis_index("y")
        v = [jnp.where(me == k, own_ref[...], x_ref[k]).astype(F32) for k in range(N_CHIPS)]
        o_ref[...] = ((v[0] + v[1]) + v[2]) + v[3]

    return pl.pallas_call(
        body, name=name, grid=(rows // tr,),
        in_specs=[pl.BlockSpec((n, tr, cols), lambda i: (0, i, 0)), pl.BlockSpec((tr, cols), lambda i: (i, 0))],
        out_specs=pl.BlockSpec((tr, cols), lambda i: (i, 0)),
        out_shape=jax.ShapeDtypeStruct((rows, cols), F32), compiler_params=_cp(("parallel",), VMEM_BIG))(x, own)


def _axes():
    return lax.axis_index("x"), lax.axis_index("y"), lax.axis_index("c")


def _other_chips(x, y):
    return [(1 - x, y), (x, 1 - y), (1 - x, 1 - y)]


ANY = pl.BlockSpec(memory_space=pl.ANY)


def _rcopy(src, dst, send_sems, recv_sems, k, to):
    return pltpu.make_async_remote_copy(src_ref=src, dst_ref=dst, send_sem=send_sems.at[k], recv_sem=recv_sems.at[k],
                                        device_id=to, device_id_type=MESH)


def _gather_weights(shards, meta):
    n = len(shards)

    def body(*refs):
        srcs, meta_ref = refs[:n], refs[n]
        outs, mout_ref = refs[n + 1:2 * n + 1], refs[2 * n + 1]
        send_sems, recv_sems = refs[2 * n + 2:]
        x, y, c = _axes()
        me = 2 * x + y
        sib = (x, y, 1 - c)
        chips = _other_chips(x, y)

        def half(t, chip_idx, cc):
            hr = shards[t].shape[0] // 2
            return outs[t].at[chip_idx, pl.ds(cc * hr, hr), :]

        first = []
        for j, (px, py) in enumerate(chips):
            for t in range(n):
                hr = shards[t].shape[0] // 2
                first.append(_rcopy(srcs[t].at[pl.ds(c * hr, hr), :], half(t, me, c), send_sems, recv_sems,
                                    3 * t + j, (px, py, c)))
            first.append(_rcopy(meta_ref, mout_ref.at[me], send_sems, recv_sems, 3 * n + j, (px, py, c)))
        for cp in first:
            cp.start()
        passed = []
        for j, (px, py) in enumerate(chips):
            src_chip = 2 * px + py
            for t in range(n):
                _rcopy(half(t, src_chip, c), half(t, src_chip, c), send_sems, recv_sems, 3 * t + j, sib).wait_recv()
                fwd = _rcopy(half(t, src_chip, c), half(t, src_chip, c), send_sems, recv_sems, 3 * (n + 1 + t) + j, sib)
                fwd.start()
                passed.append(fwd)
            _rcopy(mout_ref.at[src_chip], mout_ref.at[src_chip], send_sems, recv_sems, 3 * n + j, sib).wait_recv()
        for j, (px, py) in enumerate(chips):
            src_chip = 2 * px + py
            for t in range(n):
                _rcopy(half(t, src_chip, 1 - c), half(t, src_chip, 1 - c), send_sems, recv_sems,
                       3 * (n + 1 + t) + j, sib).wait_recv()
        for cp in first + passed:
            cp.wait_send()

    nsem = 3 * (2 * n + 1)
    return pl.pallas_call(
        body, name="gather_weights", in_specs=[ANY] * (n + 1), out_specs=[ANY] * (n + 1),
        out_shape=[jax.ShapeDtypeStruct((N_CHIPS,) + s.shape, s.dtype) for s in shards]
        + [jax.ShapeDtypeStruct((N_CHIPS,) + meta.shape, meta.dtype)],
        scratch_shapes=[pltpu.SemaphoreType.DMA((nsem,)), pltpu.SemaphoreType.DMA((nsem,))])(*shards, meta)


def _gather_late(shard):
    rows, cols = shard.shape
    hr = rows // 2
    src = jax.new_ref(shard, memory_space=pltpu.MemorySpace.HBM)
    out = jax.empty_ref(jax.ShapeDtypeStruct((N_CHIPS, rows, cols), shard.dtype), memory_space=pltpu.MemorySpace.HBM)

    @pl.kernel(mesh=plsc.ScalarSubcoreMesh(axis_name="seq", num_cores=1), name="gather_late",
               scratch_types=(pltpu.SemaphoreType.DMA((6,)), pltpu.SemaphoreType.DMA((6,))),
               compiler_params=pltpu.CompilerParams(collective_id=1))
    def launch(send_sems, recv_sems):
        x, y, c = _axes()
        me = 2 * x + y
        sib = (x, y, 1 - c)
        chips = _other_chips(x, y)
        barrier = pltpu.get_barrier_semaphore()
        for px, py in chips:
            pl.semaphore_signal(barrier, inc=1, device_id=(px, py, c), device_id_type=MESH)
        pl.semaphore_signal(barrier, inc=1, device_id=sib, device_id_type=MESH)
        pl.semaphore_wait(barrier, 4)

        def half(chip_idx, cc):
            return out.at[chip_idx, pl.ds(cc * hr, hr), :]

        first = [_rcopy(src.at[pl.ds(c * hr, hr), :], half(me, c), send_sems, recv_sems, j, (px, py, c))
                 for j, (px, py) in enumerate(chips)]
        for cp in first:
            cp.start()
        passed = []
        for j, (px, py) in enumerate(chips):
            land = half(2 * px + py, c)
            _rcopy(land, land, send_sems, recv_sems, j, sib).wait_recv()
            fwd = _rcopy(land, land, send_sems, recv_sems, 3 + j, sib)
            fwd.start()
            passed.append(fwd)
        for j, (px, py) in enumerate(chips):
            land = half(2 * px + py, 1 - c)
            _rcopy(land, land, send_sems, recv_sems, 3 + j, sib).wait_recv()
        for cp in first + passed:
            cp.wait_send()

    launch()
    return out[...]


def _swap_halves(gs):
    n = len(gs)
    ncopies = sum(g.shape[0] for g in gs)

    def body(*refs):
        srcs, outs = refs[:n], refs[n:2 * n]
        send_sems, recv_sems = refs[2 * n:]
        x, y, c = _axes()
        cps = []
        for t in range(n):
            hr = gs[t].shape[1] // 2
            for j in range(gs[t].shape[0]):
                cps.append(_rcopy(srcs[t].at[j, pl.ds((1 - c) * hr, hr), :], outs[t].at[j], send_sems, recv_sems,
                                  len(cps), (x, y, 1 - c)))
        for cp in cps:
            cp.start()
        for cp in cps:
            cp.wait()

    return pl.pallas_call(
        body, name="swap_halves", in_specs=[ANY] * n, out_specs=[ANY] * n,
        out_shape=[jax.ShapeDtypeStruct((g.shape[0], g.shape[1] // 2, g.shape[2]), g.dtype) for g in gs],
        scratch_shapes=[pltpu.SemaphoreType.DMA((ncopies,)), pltpu.SemaphoreType.DMA((ncopies,))])(*gs)


def _scatter_chips(parts):
    n = len(parts)
    srcs = [jax.new_ref(p, memory_space=pltpu.MemorySpace.HBM) for p in parts]
    outs = [jax.empty_ref(jax.ShapeDtypeStruct(p.shape, p.dtype), memory_space=pltpu.MemorySpace.HBM) for p in parts]

    @pl.kernel(mesh=plsc.ScalarSubcoreMesh(axis_name="seq", num_cores=1), name="scatter_chips",
               scratch_types=(pltpu.SemaphoreType.DMA((3 * n,)), pltpu.SemaphoreType.DMA((3 * n,))),
               compiler_params=pltpu.CompilerParams(collective_id=0))
    def launch(send_sems, recv_sems):
        x, y, c = _axes()
        me = 2 * x + y
        chips = _other_chips(x, y)
        barrier = pltpu.get_barrier_semaphore()
        for px, py in chips:
            pl.semaphore_signal(barrier, inc=1, device_id=(px, py, c), device_id_type=MESH)
        pl.semaphore_wait(barrier, 3)
        cps = []
        for j, (px, py) in enumerate(chips):
            for t in range(n):
                cps.append(_rcopy(srcs[t].at[2 * px + py], outs[t].at[me], send_sems, recv_sems, 3 * t + j,
                                  (px, py, c)))
        for cp in cps:
            cp.start()
        for cp in cps:
            cp.wait()

    launch()
    return [o[...] for o in outs]


def _swap_reduced(rs):
    n = len(rs)

    def body(*refs):
        srcs, outs = refs[:n], refs[n:2 * n]
        send_sems, recv_sems = refs[2 * n:]
        x, y, c = _axes()
        cps = [_rcopy(srcs[t], outs[t], send_sems, recv_sems, t, (x, y, 1 - c)) for t in range(n)]
        for cp in cps:
            cp.start()
        for cp in cps:
            cp.wait()

    return pl.pallas_call(
        body, name="swap_reduced", in_specs=[ANY] * n, out_specs=[ANY] * n,
        out_shape=[jax.ShapeDtypeStruct(r.shape, r.dtype) for r in rs],
        scratch_shapes=[pltpu.SemaphoreType.DMA((n,)), pltpu.SemaphoreType.DMA((n,))])(*rs)


SMALL_ROWS = 24 + 128


def _allreduce_small(vec):
    def body(v_ref, out_ref, slots, send_sems, recv_sems):
        x, y, c = _axes()
        me = 4 * x + 2 * y + c
        slots[me] = v_ref[...]
        cps = []
        for k in range(1, 8):
            kx, ky, kc = (k >> 2) & 1, (k >> 1) & 1, k & 1
            peer = (1 - x if kx else x, 1 - y if ky else y, 1 - c if kc else c)
            cps.append(_rcopy(v_ref, slots.at[me], send_sems, recv_sems, k - 1, peer))
        for cp in cps:
            cp.start()
        for cp in cps:
            cp.wait()
        tot = slots[0]
        for k in range(1, 8):
            tot = tot + slots[k]
        out_ref[...] = tot

    return pl.pallas_call(
        body, name="allreduce_small",
        in_specs=[pl.BlockSpec(memory_space=pltpu.VMEM)], out_specs=pl.BlockSpec(memory_space=pltpu.VMEM),
        out_shape=jax.ShapeDtypeStruct((SMALL_ROWS, 128), F32),
        scratch_shapes=[pltpu.VMEM((8, SMALL_ROWS, 128), F32), pltpu.SemaphoreType.DMA((7,)),
                        pltpu.SemaphoreType.DMA((7,))])(vec)


def _pack_p2(w_uq, w_ukv, w_br_mla, w_br_fox, w_out, dtype):
    parts = [w_uq.reshape(96, D_MODEL), w_ukv.reshape(64, D_MODEL), w_br_mla, w_br_fox, w_out]
    return jnp.concatenate([p.astype(dtype) for p in parts], axis=0)


def _unpack_p2(pk):
    return pk[0:96].reshape(256, 384), pk[96:160].reshape(128, 512), pk[160:416], pk[416:672], pk[672:928]


def _uq_arrange(w):
    w3 = w.reshape(256, HEADS, 96)
    nope = w3[:, :, :64].reshape(256, PAIRS, 128)
    pe = w3[:, :, 64:].reshape(256, PAIRS, 64)
    return jnp.concatenate([nope, pe, jnp.zeros((256, PAIRS, 64), w.dtype)], axis=2).reshape(256, PAIRS * 256)


def _uq_restore(g):
    g3 = g.reshape(256, PAIRS, 256)
    nope = g3[:, :, :128].reshape(256, HEADS, 64)
    pe = g3[:, :, 128:192].reshape(256, HEADS, 32)
    return jnp.concatenate([nope, pe], axis=2).reshape(256, HEADS * 96)


def _ukv_arrange(w):
    w3 = w.reshape(128, HEADS, 128)
    return jnp.concatenate([w3[:, :, :64].reshape(128, 1024), w3[:, :, 64:].reshape(128, 1024)], axis=1)


def _ukv_restore(g):
    kn = g[:, :1024].reshape(128, HEADS, 64)
    vv = g[:, 1024:].reshape(128, HEADS, 64)
    return jnp.concatenate([kn, vv], axis=2).reshape(128, HEADS * 128)


def _rope_tables(lp):
    r = np.arange(lp)
    pos = np.where(r < N_META, r, np.where(r >= PAD, r - PAD + N_META, 0)).astype(np.float32)
    half = MLA_ROPE // 2
    inv_freq = np.float32(ROPE_THETA) ** (-np.arange(half, dtype=np.float32) / np.float32(half))
    ang = (pos[:, None] * inv_freq[None, :]).astype(np.float32)
    cos, sin = np.cos(ang).astype(np.float32), np.sin(ang).astype(np.float32)
    one, zero = np.ones((lp, 64), np.float32), np.zeros((lp, 64), np.float32)
    return (jnp.asarray(np.concatenate([cos, cos, cos, cos, one], axis=1)),
            jnp.asarray(np.concatenate([-sin, sin, -sin, sin, zero], axis=1)))


def _pad_lanes(v, n=128):
    return jnp.pad(v, ((0, 0), (0, n - v.shape[1])))


def _in_cols(slabs, a, b):
    out = []
    for j in range(N_CHIPS):
        lo, hi = max(a, W_IN_SHARD * j), min(b, W_IN_SHARD * (j + 1))
        if lo < hi:
            out.append(slabs[j][:, lo - W_IN_SHARD * j:hi - W_IN_SHARD * j])
    return out


def _local_step(x2, tgt2, meta_f, w_small, w_attn, w_gate, w_uq_f, w_ukv_f, w_bm, w_bf, w_o, pre_norm_g,
                post_norm_g, mla_q_norm_g, mla_kv_norm_g, fox_forget_b, start_exchange=None):
    s_rows = x2.shape[0]
    lp = PAD + s_rows
    w_uq_a = _uq_arrange(w_uq_f)
    w_ukv_a = _ukv_arrange(w_ukv_f)

    ctab, stab = _rope_tables(lp)
    ii = jnp.arange(BLK)
    tri_lo = (ii[:, None] >= ii[None, :]).astype(BF16)
    tri_up = (ii[:, None] <= ii[None, :]).astype(BF16)
    fb128 = _pad_lanes(fox_forget_b)

    u, u_t = _rms_pre(x2, meta_f, pre_norm_g)
    small = _mm(u, w_small, mode="nn", out_dtype=F32, name="proj_small")
    attn = _mm(u, w_attn, mode="nn", out_dtype=BF16, name="proj_attn")
    gate = _mm(u, w_gate, mode="nn", out_dtype=BF16, name="proj_gate")
    qn, kvn, kr, ncum = _small_prep(small, mla_q_norm_g, mla_kv_norm_g, fb128, ctab, stab, tri_lo)
    qcat = _mm(qn, w_uq_a, mode="nn", out_dtype=BF16, name="mla_q", epilogue=_rope_pairs, row_ins=(ctab, stab))
    kv = _mm(kvn, w_ukv_a, mode="nn", out_dtype=BF16, name="mla_kv")
    nbrep = jnp.broadcast_to(ncum[:, :HEADS].T[:, :, None], (HEADS, lp, LANES))

    mla_cols = dict(qcol=0, kcol=lambda p: p, vcol=lambda p: PAIRS + p)
    fox_cols = dict(qcol=0, kcol=lambda p: PAIRS + p, vcol=lambda p: 2 * PAIRS + p)
    o_mla, lse_mla = _attn_fwd(qcat, kv, kv, kr=kr, scale=MLA_SCALE, name="mla_fwd", **mla_cols)
    o_fox, lse_fox = _attn_fwd(attn, attn, attn, nbrep=nbrep, scale=FOX_SCALE, name="fox_fwd", **fox_cols)

    a_mla, a_fox = _gate_fwd(o_mla, o_fox, gate)
    y_mla = _mm(a_mla, w_bm, mode="nn", out_dtype=BF16, name="br_mla")
    y_fox = _mm(a_fox, w_bf, mode="nn", out_dtype=BF16, name="br_fox")
    mg = _merge_fwd(gate, y_mla, y_fox)
    mixed = _mm(mg, w_o, mode="nn", out_dtype=F32, name="out_proj")
    dmixed, dy, loss_p, dg_post = _tail(x2, mixed, tgt2, post_norm_g)

    d_w_out = _mm(mg, dmixed, mode="tn", out_dtype=F32, name="d_w_out")
    dm = _mm(dmixed, w_o, mode="nt", out_dtype=BF16, name="d_merge")
    dy_mla, dy_fox, dgate_ab = _merge_bwd(dm, gate, y_mla, y_fox)
    d_w_bm = _mm(a_mla, dy_mla, mode="tn", out_dtype=F32, name="d_w_br_mla")
    d_w_bf = _mm(a_fox, dy_fox, mode="tn", out_dtype=F32, name="d_w_br_fox")
    da_mla = _mm(dy_mla, w_bm, mode="nt", out_dtype=BF16, name="d_a_mla")
    da_fox = _mm(dy_fox, w_bf, mode="nt", out_dtype=BF16, name="d_a_fox")
    do_mla, do_fox, dgate_z, dl_mla, dl_fox = _gate_bwd(da_mla, da_fox, o_mla, o_fox, gate)
    dl_mla, dl_fox = (d[:, :HEADS].T.reshape(PAIRS, 2, lp) for d in (dl_mla, dl_fox))

    dq_a, dkn, dvm, dkr = _attn_bwd(qcat, kv, kv, do_mla, dl_mla, lse_mla, kr=kr, rtabs=(ctab, stab),
                                    scale=MLA_SCALE, name="mla_bwd", **mla_cols)
    dfq, dfk, dfv, dcol, drow = _attn_bwd(attn, attn, attn, do_fox, dl_fox, lse_fox, nbrep=nbrep, scale=FOX_SCALE,
                                          name="fox_bwd", **fox_cols)

    d_w_uq_a = _mm(qn, dq_a, mode="tn", out_dtype=F32, name="d_w_uq")
    dqn = _mm(dq_a, w_uq_a, mode="nt", out_dtype=F32, name="d_qn")
    d_w_ukv_a = jnp.concatenate([_mm(kvn, dkn, mode="tn", out_dtype=F32, name="d_w_uk"),
                                 _mm(kvn, dvm, mode="tn", out_dtype=F32, name="d_w_uv")], axis=1)
    dkvn = _mm(dkn, w_ukv_a[:, :1024], mode="nt", out_dtype=F32, name="d_kvn_k")
    dkvn = _mm(dvm, w_ukv_a[:, 1024:], mode="nt", out_dtype=F32, name="d_kvn_v", acc=dkvn)
    dsmall, dg_q, dg_kv, dfb = _small_bwd(small, dqn, dkvn, dkr, dcol, drow, mla_q_norm_g, mla_kv_norm_g,
                                          fb128, ctab, stab, tri_up)

    dw_small = _mm(u_t, dsmall, mode="nn", out_dtype=BF16, name="d_w_small")
    dw_fq = _mm(u_t, dfq, mode="nn", out_dtype=BF16, name="d_w_fq")
    dw_fk = _mm(u_t, dfk, mode="nn", out_dtype=BF16, name="d_w_fk")
    dw_fv = _mm(u_t, dfv, mode="nn", out_dtype=BF16, name="d_w_fv")
    dw_z = _mm(u_t, dgate_z, mode="nn", out_dtype=BF16, name="d_w_z")
    dw_g = _mm(u_t, dgate_ab, mode="nn", out_dtype=BF16, name="d_w_g")
    d_w_in = (dw_small, dw_z, dw_fq, dw_fk, dw_fv, dw_g)
    d_w_uq = _uq_restore(d_w_uq_a)
    d_w_ukv = _ukv_restore(d_w_ukv_a)
    token = start_exchange(d_w_in, d_w_uq, d_w_ukv, d_w_bm, d_w_bf, d_w_out) if start_exchange else None
    du = _mm_sum_nt([(dsmall, w_small), (dfq, w_attn[:, 0:1024]), (dfk, w_attn[:, 1024:2048]),
                     (dfv, w_attn[:, 2048:3072]), (dgate_z, w_gate[:, 0:2048]), (dgate_ab, w_gate[:, 2048:4096])],
                    name="d_u", after=token)
    dx, dmeta, dg_pre = _pre_bwd(du, x2, meta_f, dy, pre_norm_g)
    return (loss_p, dx, dmeta, d_w_in, d_w_uq, d_w_ukv, d_w_bm, d_w_bf, d_w_out, dg_pre, dg_post, dg_q, dg_kv, dfb)


def _w_in_slabs(pieces):
    dw_small, dw_z, dw_fq, dw_fk, dw_fv, dw_g = pieces
    runs = [(dw_small[:, 0:416], C_CQ), (dw_z[:, 0:1024], C_ZMLA), (dw_fq, C_FQ), (dw_fk, C_FK), (dw_fv, C_FV),
            (dw_small[:, 512:528], C_FL), (dw_z[:, 1024:2048], C_ZFOX), (dw_g, C_GA)]
    slabs = []
    for j in range(N_CHIPS):
        lo, hi = W_IN_SHARD * j, W_IN_SHARD * (j + 1)
        cols = [a[:, max(lo, c0) - c0:min(hi, c0 + a.shape[1]) - c0] for a, c0 in runs
                if max(lo, c0) < min(hi, c0 + a.shape[1])]
        slabs.append(jnp.concatenate(cols, axis=1))
    return jnp.stack(slabs, axis=0)


def kernel(x, meta_tokens, pre_norm_g, w_in, fox_forget_b, mla_q_norm_g, mla_kv_norm_g, w_uq, w_ukv, w_br_mla, w_br_fox, w_out, post_norm_g, loss_target, m_meta_tokens, m_pre_norm_g, m_w_in, m_fox_forget_b, m_mla_q_norm_g, m_mla_kv_norm_g, m_w_uq, m_w_ukv, m_w_br_mla, m_w_br_fox, m_w_out, m_post_norm_g, v_meta_tokens, v_pre_norm_g, v_w_in, v_fox_forget_b, v_mla_q_norm_g, v_mla_kv_norm_g, v_w_uq, v_w_ukv, v_w_br_mla, v_w_br_fox, v_w_out, v_post_norm_g):
    me = 2 * lax.axis_index("x") + lax.axis_index("y")
    core = lax.axis_index("c")
    w_in_b = w_in.astype(BF16).reshape(D_MODEL, W_IN_SHARD)
    p2 = _pack_p2(w_uq[0], w_ukv[0], w_br_mla[0], w_br_fox[0], w_out[0], BF16)
    w_in_g, meta_g = _gather_weights([w_in_b], meta_tokens)
    p2_g = _gather_late(lax.optimization_barrier((p2, w_in_g))[0])
    slabs = [jnp.where(me == j, w_in_b, w_in_g[j]) for j in range(N_CHIPS)]
    chip = lax.broadcasted_iota(jnp.int32, (N_CHIPS, 1, 1), 0)
    p2_all = jnp.where(chip == me, p2[None], p2_g)
    w_uq_f = p2_all[:, 0:96].reshape(N_CHIPS, 256, 384).transpose(1, 0, 2).reshape(256, 1536)
    w_ukv_f = p2_all[:, 96:160].reshape(N_CHIPS, 128, 512).transpose(1, 0, 2).reshape(128, 2048)
    w_bm, w_bf, w_o = (p2_all[:, lo:lo + 256].reshape(D_MODEL, D_MODEL) for lo in (160, 416, 672))
    meta_f = jnp.where(chip == me, meta_tokens[None], meta_g).transpose(1, 0, 2).reshape(N_META, D_MODEL)
    kpe = _in_cols(slabs, C_KPE, C_ZMLA)
    w_small = jnp.concatenate(_in_cols(slabs, C_CQ, C_KPE) + kpe + kpe + [jnp.zeros((D_MODEL, 64), BF16)]
                              + _in_cols(slabs, C_FL, C_ZFOX) + [jnp.zeros((D_MODEL, 112), BF16)], axis=1)
    w_attn = jnp.concatenate(_in_cols(slabs, C_FQ, C_FL), axis=1)
    w_gate = jnp.concatenate(_in_cols(slabs, C_ZMLA, C_FQ) + _in_cols(slabs, C_ZFOX, C_END), axis=1)

    exchange = {}

    def start_exchange(d_w_in, d_w_uq, d_w_ukv, d_w_bm, d_w_bf, d_w_out):
        g2 = jnp.concatenate(
            [d_w_uq.reshape(256, N_CHIPS, 384).transpose(1, 0, 2).reshape(N_CHIPS, 96, D_MODEL),
             d_w_ukv.reshape(128, N_CHIPS, 512).transpose(1, 0, 2).reshape(N_CHIPS, 64, D_MODEL)]
            + [g.reshape(N_CHIPS, 256, D_MODEL) for g in (d_w_bm, d_w_bf, d_w_out)], axis=1)
        pieces = [p[None] for p in d_w_in]
        from_sib = _swap_halves(pieces + [g2])
        halves = [_add_cores(p, s, "add_cores_" + nm)[0]
                  for p, s, nm in zip(pieces, from_sib, ("small", "z", "fq", "fk", "fv", "g"))]
        parts = [_w_in_slabs(halves), _add_cores(g2, from_sib[-1], "add_cores_rest")]
        exchange.update(parts=parts, landed=_scatter_chips(parts))
        return parts[0][0, 0:16, 0:LANES]

    (loss_p, dx, dmeta, _, _, _, _, _, _, dg_pre, dg_post, dg_q, dg_kv,
     dfb) = _local_step(x[0], loss_target[0], meta_f, w_small, w_attn, w_gate, w_uq_f, w_ukv_f, w_bm, w_bf, w_o,
                        pre_norm_g, post_norm_g, mla_q_norm_g, mla_kv_norm_g, fox_forget_b, start_exchange)

    mine = [_add_chips(l, lax.dynamic_index_in_dim(p, me, 0, keepdims=False), nm)
            for l, p, nm in zip(exchange["landed"], exchange["parts"], ("add_chips_w_in", "add_chips_rest"))]
    theirs = _swap_reduced(mine)
    g_w_in, g_p2 = [jnp.concatenate([jnp.where(core == 0, a, b), jnp.where(core == 0, b, a)], axis=0)
                    for a, b in zip(mine, theirs)]
    g_w_uq, g_w_ukv, g_w_bm, g_w_bf, g_w_out = _unpack_p2(g_p2)
    g_w_in = g_w_in[None]

    vec = jnp.concatenate([dg_pre.reshape(8, 128), dg_post.reshape(8, 128), dg_q.reshape(2, 128), dg_kv,
                           dfb, _pad_lanes(loss_p), jnp.zeros((3, 128), F32), dmeta.reshape(128, 128)], axis=0)
    tot = _allreduce_small(vec)
    loss = tot[20, 0]
    g_meta = lax.dynamic_slice_in_dim(tot[24:].reshape(N_META, D_MODEL), 256 * me, 256, axis=1)

    def small_pack(pre, post, gq_, gkv_, fb_):
        return jnp.concatenate([pre.reshape(8, 128), post.reshape(8, 128), gq_.reshape(2, 128), gkv_,
                                _pad_lanes(fb_), jnp.zeros((4, 128), F32)], axis=0)

    def small_unpack(t):
        return (t[0:8].reshape(1, 1024), t[8:16].reshape(1, 1024), t[16:18].reshape(1, 256), t[18:19],
                t[19:20, 0:HEADS])

    g_small = jnp.concatenate([tot[0:20], jnp.zeros((4, 128), F32)], axis=0)
    sm = _adamw(small_pack(pre_norm_g, post_norm_g, mla_q_norm_g, mla_kv_norm_g, fox_forget_b), g_small,
                small_pack(m_pre_norm_g, m_post_norm_g, m_mla_q_norm_g, m_mla_kv_norm_g, m_fox_forget_b),
                small_pack(v_pre_norm_g, v_post_norm_g, v_mla_q_norm_g, v_mla_kv_norm_g, v_fox_forget_b),
                "adamw_small")
    g_pre, g_post, g_q, g_kv, g_fb = small_unpack(g_small)
    (d_pre, d_post, d_q, d_kv, d_fb), (nm_pre, nm_post, nm_q, nm_kv, nm_fb), (nv_pre, nv_post, nv_q, nv_kv, nv_fb) = (
        small_unpack(t) for t in sm)

    d_meta, nm_meta, nv_meta = _adamw(meta_tokens, g_meta, m_meta_tokens, v_meta_tokens, "adamw_meta")
    d_win, nm_win, nv_win = (t.T[None] for t in _adamw(w_in[0].T, g_w_in[0].T, m_w_in[0].T, v_w_in[0].T,
                                                       "adamw_w_in"))
    d_wuq, nm_wuq, nv_wuq = _adamw(w_uq[0], g_w_uq, m_w_uq[0], v_w_uq[0], "adamw_w_uq")
    d_wukv, nm_wukv, nv_wukv = _adamw(w_ukv[0], g_w_ukv, m_w_ukv[0], v_w_ukv[0], "adamw_w_ukv")
    d_wbm, nm_wbm, nv_wbm = _adamw(w_br_mla[0], g_w_bm, m_w_br_mla[0], v_w_br_mla[0], "adamw_w_br_mla")
    d_wbf, nm_wbf, nv_wbf = _adamw(w_br_fox[0], g_w_bf, m_w_br_fox[0], v_w_br_fox[0], "adamw_w_br_fox")
    d_wo, nm_wo, nv_wo = _adamw(w_out[0], g_w_out, m_w_out[0], v_w_out[0], "adamw_w_out")

    def group(meta_, pre, win, fb_, q_, kv_, wuq, wukv, wbm, wbf, wo, post):
        return (meta_, pre, win, fb_, q_, kv_, wuq[None], wukv[None], wbm[None], wbf[None], wo[None], post)

    grads = group(g_meta, g_pre, g_w_in, g_fb, g_q, g_kv, g_w_uq, g_w_ukv, g_w_bm, g_w_bf, g_w_out, g_post)
    deltas = group(d_meta, d_pre, d_win, d_fb, d_q, d_kv, d_wuq, d_wukv, d_wbm, d_wbf, d_wo, d_post)
    new_m = group(nm_meta, nm_pre, nm_win, nm_fb, nm_q, nm_kv, nm_wuq, nm_wukv, nm_wbm, nm_wbf, nm_wo, nm_post)
    new_v = group(nv_meta, nv_pre, nv_win, nv_fb, nv_q, nv_kv, nv_wuq, nv_wukv, nv_wbm, nv_wbf, nv_wo, nv_post)
    return (loss, dx[None], *grads, *deltas, *new_m, *new_v)
```

```python
import math

import jax
import jax.numpy as jnp
import numpy as np
from jax import lax
from jax.experimental import pallas as pl
from jax.experimental.pallas import tpu as pltpu
from jax.experimental.pallas import tpu_sc as plsc

F32 = jnp.float32
BF16 = jnp.bfloat16

D_MODEL = 1024
N_META = 16
RMS_EPS = 1e-6
HEADS = 16
PAIRS = HEADS // 2
HEAD_DIM = 64
LANES = 128
MLA_ROPE = 32
MLA_SCALE = 1.0 / math.sqrt(64 + 32)
FOX_SCALE = 1.0 / math.sqrt(64)
ROPE_THETA = 10000.0

PAD = 256
BLK = 256
QB = 512
UNROLL = 4
NEG = -1e30

C_CQ, C_CKV, C_KPE, C_ZMLA, C_FQ, C_FK, C_FV, C_FL, C_ZFOX, C_GA, C_GB, C_END = (
    0, 256, 384, 416, 1440, 2464, 3488, 4512, 4528, 5552, 6576, 7600)
SMALL_W = 640
W_IN_SHARD = 1900

P2_ROWS = 928
N_CHIPS = 4

ADAM_LR = 0.001
ADAM_B1 = 0.9
ADAM_B2 = 0.999
ADAM_EPS = 1e-08
ADAM_WD = 0.01
ADAM_STEP = 10

VMEM_BIG = 56 * 1024 * 1024
MM_VMEM_BUDGET = 44 * 1024 * 1024
MESH = pl.DeviceIdType.MESH


def _cp(dims, vmem=None):
    return pltpu.CompilerParams(dimension_semantics=dims, vmem_limit_bytes=vmem)


def _dot(a, b, ca, cb):
    return lax.dot_general(a, b, (((ca,), (cb,)), ((), ())), preferred_element_type=F32)


def _sigmoid(x):
    return 1.0 / (1.0 + jnp.exp(-x))


def _tile(n, cands):
    for c in cands:
        if n % c == 0:
            return c
    return n


def _mm(a, b, *, mode, out_dtype, name, acc=None, epilogue=None, row_ins=(), after=None):
    if mode == "nn":
        (M, K), N = a.shape, b.shape[1]
    elif mode == "nt":
        (M, K), N = a.shape, b.shape[0]
    else:
        (K, M), N = a.shape, b.shape[1]
    tm = _tile(M, (1088, 1024)) if M > 1024 else M
    tn = _tile(N, (1024,)) if N > 1024 else N
    nk = 1
    while True:
        tk = K // nk
        need = 2 * tk * (tm * a.dtype.itemsize + tn * b.dtype.itemsize) + tm * tn * (
            2 * jnp.dtype(out_dtype).itemsize + (8 if acc is not None else 0) + (4 if nk > 1 else 0))
        if need <= MM_VMEM_BUDGET or (tk // 2) % (16 if mode == "tn" else LANES) or tk <= 512:
            break
        nk *= 2
    while (M // tm) * (N // tn) * nk < 4 and tn % 512 == 0:
        tn //= 2
    ca, cb = {"nn": (1, 0), "nt": (1, 1), "tn": (0, 0)}[mode]
    a_spec = (pl.BlockSpec((tk, tm), lambda j, i, k: (k, i)) if mode == "tn"
              else pl.BlockSpec((tm, tk), lambda j, i, k: (i, k)))
    b_spec = (pl.BlockSpec((tn, tk), lambda j, i, k: (j, k)) if mode == "nt"
              else pl.BlockSpec((tk, tn), lambda j, i, k: (k, j)))
    o_spec = pl.BlockSpec((tm, tn), lambda j, i, k: (i, j))
    has_acc = acc is not None

    nrow = len(row_ins)

    def body(*refs):
        a_ref, b_ref = refs[0], refs[1]
        acc_ref = refs[2] if has_acc else None
        rows = refs[2 + has_acc:2 + has_acc + nrow]
        o_ref = refs[2 + has_acc + nrow + (after is not None)]

        def store(tile):
            if epilogue is not None:
                tile = epilogue(tile, *[r[...] for r in rows])
            o_ref[...] = tile.astype(out_dtype)

        part = _dot(a_ref[...].astype(BF16), b_ref[...].astype(BF16), ca, cb)
        if nk == 1:
            store(part + acc_ref[...] if has_acc else part)
        else:
            sc = refs[-1]
            k = pl.program_id(2)

            @pl.when(k == 0)
            def _():
                sc[...] = part + acc_ref[...] if has_acc else part

            @pl.when(k > 0)
            def _():
                sc[...] += part

            @pl.when(k == nk - 1)
            def _():
                store(sc[...])

    ins = [a, b] + ([acc] if has_acc else []) + list(row_ins)
    in_specs = ([a_spec, b_spec] + ([o_spec] if has_acc else [])
                + [pl.BlockSpec((tm, r.shape[1]), lambda j, i, k: (i, 0)) for r in row_ins])
    if after is not None:
        ins.append(after)
        in_specs.append(pl.BlockSpec(after.shape, lambda j, i, k: (0,) * after.ndim))
    return pl.pallas_call(
        body, name=name, grid=(N // tn, M // tm, nk), in_specs=in_specs, out_specs=o_spec,
        out_shape=jax.ShapeDtypeStruct((M, N), out_dtype),
        scratch_shapes=[pltpu.VMEM((tm, tn), F32)] if nk > 1 else [],
        compiler_params=_cp(("parallel", "parallel", "arbitrary"), VMEM_BIG))(*ins)


def _mm_sum_nt(pairs, *, name, after=None):
    n = len(pairs)
    M, N = pairs[0][0].shape[0], pairs[0][1].shape[0]
    tm = _tile(M, (272,))

    def body(*refs):
        o_ref = refs[2 * n + (after is not None)]
        tot = _dot(refs[0][...].astype(BF16), refs[n][...].astype(BF16), 1, 1)
        for i in range(1, n):
            tot = tot + _dot(refs[i][...].astype(BF16), refs[n + i][...].astype(BF16), 1, 1)
        o_ref[...] = tot

    ins = [a for a, _ in pairs] + [b for _, b in pairs]
    in_specs = ([pl.BlockSpec((tm, a.shape[1]), lambda i: (i, 0)) for a, _ in pairs]
                + [pl.BlockSpec(b.shape, lambda i: (0, 0)) for _, b in pairs])
    if after is not None:
        ins.append(after)
        in_specs.append(pl.BlockSpec(after.shape, lambda i: (0,) * after.ndim))
    return pl.pallas_call(
        body, name=name, grid=(M // tm,), in_specs=in_specs, out_specs=pl.BlockSpec((tm, N), lambda i: (i, 0)),
        out_shape=jax.ShapeDtypeStruct((M, N), F32), compiler_params=_cp(("parallel",), VMEM_BIG))(*ins)


def _row(w):
    return pl.BlockSpec((BLK, w), lambda i: (i, 0))


def _rowc(w, c):
    return pl.BlockSpec((BLK, w), lambda i: (i, c))


def _full(shape):
    return pl.BlockSpec(shape, lambda i: tuple(0 for _ in shape))


def _rope(x, c, s):
    lane = lax.broadcasted_iota(jnp.int32, x.shape, 1)
    is_x1 = ((lane >> 4) & 1) == 0
    partner = jnp.where(is_x1, pltpu.roll(x, LANES - 16, 1), pltpu.roll(x, 16, 1))
    return x * c + partner * s


def _row_valid(i):
    rows = i * BLK + lax.broadcasted_iota(jnp.int32, (BLK, 1), 0)
    return (rows < N_META) | (rows >= PAD)


def _shift_rows(w):
    return pl.BlockSpec((BLK, w), lambda i: (jnp.maximum(i - 1, 0), 0))


def _h_block(i, x_ref, meta_ref):
    head = jnp.concatenate([meta_ref[...], jnp.zeros((BLK - N_META, D_MODEL), F32)], axis=0)
    return jnp.where(i == 0, head, x_ref[...])


def _rms_pre(x2, meta, g):
    lp = PAD + x2.shape[0]

    def body(x_ref, meta_ref, g_ref, u_ref):
        hv = _h_block(pl.program_id(0), x_ref, meta_ref)
        r = lax.rsqrt(jnp.mean(hv * hv, axis=-1, keepdims=True) + RMS_EPS)
        u_ref[...] = (hv * r * g_ref[...]).astype(BF16)

    return pl.pallas_call(
        body, name="rms_pre", grid=(lp // BLK,),
        in_specs=[_shift_rows(D_MODEL), _full((N_META, D_MODEL)), _full((1, D_MODEL))], out_specs=_row(D_MODEL),
        out_shape=jax.ShapeDtypeStruct((lp, D_MODEL), BF16),
        compiler_params=_cp(("parallel",)))(x2, meta, g)


def _split3(x):
    hi = x.astype(BF16)
    r1 = x - hi.astype(F32)
    mid = r1.astype(BF16)
    lo = (r1 - mid.astype(F32)).astype(BF16)
    return hi, mid, lo


def _small_prep(small, gq, gkv, fb, ctab, stab, tri):
    lp = small.shape[0]

    def body(sm_ref, gq_ref, gkv_ref, fb_ref, c_ref, s_ref, tri_ref, qn_ref, kvn_ref, kr_ref, ncum_ref, carry):
        i = pl.program_id(0)

        @pl.when(i == 0)
        def _():
            carry[...] = jnp.zeros_like(carry)

        cq = sm_ref[:, 0:256]
        r = lax.rsqrt(jnp.mean(cq * cq, axis=-1, keepdims=True) + RMS_EPS)
        qn_ref[...] = (cq * r * gq_ref[...]).astype(BF16)
        ckv = sm_ref[:, 256:384]
        r = lax.rsqrt(jnp.mean(ckv * ckv, axis=-1, keepdims=True) + RMS_EPS)
        kvn_ref[...] = (ckv * r * gkv_ref[...]).astype(BF16)
        kr_ref[...] = _rope(sm_ref[:, 384:512], c_ref[...], s_ref[...]).astype(BF16)
        fl = sm_ref[:, 512:640] + fb_ref[...]
        lf = jnp.minimum(fl, 0.0) - jnp.log(1.0 + jnp.exp(-jnp.abs(fl)))
        lf = jnp.where(_row_valid(i), lf, 0.0)
        hi, mid, lo = _split3(lf)
        t = tri_ref[...]
        cum = (_dot(t, hi, 1, 0) + _dot(t, mid, 1, 0)) + _dot(t, lo, 1, 0) + carry[...]
        ncum_ref[...] = -cum
        carry[...] = -ncum_ref[BLK - 1:BLK, :]

    return pl.pallas_call(
        body, name="small_prep", grid=(lp // BLK,),
        in_specs=[_row(SMALL_W), _full((1, 256)), _full((1, 128)), _full((1, 128)), _row(128), _row(128),
                  _full((BLK, BLK))],
        out_specs=[_row(256), _row(128), _row(128), _row(128)],
        out_shape=[jax.ShapeDtypeStruct((lp, 256), BF16), jax.ShapeDtypeStruct((lp, 128), BF16),
                   jax.ShapeDtypeStruct((lp, 128), BF16), jax.ShapeDtypeStruct((lp, 128), F32)],
        scratch_shapes=[pltpu.VMEM((1, 128), F32)],
        compiler_params=_cp(("arbitrary",)))(small, gq, gkv, fb, ctab, stab, tri)


def _rope_pairs(tile, c, s):
    out = []
    for lo in range(0, tile.shape[1], 256):
        out += [tile[:, lo:lo + 128], _rope(tile[:, lo + 128:lo + 256], c, s)]
    return jnp.concatenate(out, axis=1)


def _gate_fwd(o_mla, o_fox, gate):
    lp = o_mla.shape[0]

    def body(om_ref, of_ref, zm_ref, zf_ref, am_ref, af_ref):
        zm = zm_ref[...].astype(F32)
        am_ref[...] = (om_ref[...] * (zm * _sigmoid(zm))).astype(BF16)
        zf = zf_ref[...].astype(F32)
        af_ref[...] = (of_ref[...] * (zf * _sigmoid(zf))).astype(BF16)

    return pl.pallas_call(
        body, name="gate_fwd", grid=(lp // BLK,),
        in_specs=[_row(D_MODEL), _row(D_MODEL), _rowc(D_MODEL, 0), _rowc(D_MODEL, 1)],
        out_specs=[_row(D_MODEL), _row(D_MODEL)],
        out_shape=[jax.ShapeDtypeStruct((lp, D_MODEL), BF16)] * 2,
        compiler_params=_cp(("parallel",)))(o_mla, o_fox, gate, gate)


def _merge_fwd(gate, y_mla, y_fox):
    lp = y_mla.shape[0]

    def body(ga_ref, gb_ref, ym_ref, yf_ref, m_ref):
        sa = _sigmoid(ga_ref[...].astype(F32))
        sb = _sigmoid(gb_ref[...].astype(F32))
        m_ref[...] = (sa * ym_ref[...] + sb * yf_ref[...]).astype(BF16)

    return pl.pallas_call(
        body, name="merge_fwd", grid=(lp // BLK,),
        in_specs=[_rowc(D_MODEL, 2), _rowc(D_MODEL, 3), _row(D_MODEL), _row(D_MODEL)],
        out_specs=_row(D_MODEL), out_shape=jax.ShapeDtypeStruct((lp, D_MODEL), BF16),
        compiler_params=_cp(("parallel",)))(gate, gate, y_mla, y_fox)


def _tail(x2, mixed, tgt, gpost):
    lp = mixed.shape[0]
    shift = _shift_rows(D_MODEL)

    def body(h_ref, mx_ref, t_ref, g_ref, dmx_ref, dy_ref, loss_ref, dg_ref):
        i = pl.program_id(0)

        @pl.when(i == 0)
        def _():
            loss_ref[...] = jnp.zeros_like(loss_ref)
            dg_ref[...] = jnp.zeros_like(dg_ref)
            dmx_ref[...] = jnp.zeros_like(dmx_ref)
            dy_ref[...] = jnp.zeros_like(dy_ref)

        @pl.when(i > 0)
        def _():
            mx = mx_ref[...]
            g = g_ref[...]
            r = lax.rsqrt(jnp.mean(mx * mx, axis=-1, keepdims=True) + RMS_EPS)
            nrm = mx * r
            e = (h_ref[...] + nrm * g) - t_ref[...]
            loss_ref[...] += jnp.sum(0.5 * jnp.sum(e * e, axis=-1, keepdims=True) * (1.0 / D_MODEL),
                                     axis=0, keepdims=True)
            dy = e * (1.0 / D_MODEL)
            dy_ref[...] = dy
            dg_ref[...] += jnp.sum(dy * nrm, axis=0, keepdims=True)
            w = dy * g
            dot = jnp.mean(w * mx, axis=-1, keepdims=True)
            dmx_ref[...] = (r * w - mx * (r * r * r * dot)).astype(BF16)

    return pl.pallas_call(
        body, name="tail", grid=(lp // BLK,),
        in_specs=[shift, _row(D_MODEL), shift, _full((1, D_MODEL))],
        out_specs=[_row(D_MODEL), _row(D_MODEL), _full((1, 1)), _full((1, D_MODEL))],
        out_shape=[jax.ShapeDtypeStruct((lp, D_MODEL), BF16), jax.ShapeDtypeStruct((lp, D_MODEL), F32),
                   jax.ShapeDtypeStruct((1, 1), F32), jax.ShapeDtypeStruct((1, D_MODEL), F32)],
        compiler_params=_cp(("arbitrary",)))(x2, mixed, tgt, gpost)


def _merge_bwd(dm, gate, y_mla, y_fox):
    lp = dm.shape[0]

    def body(dm_ref, ga_ref, gb_ref, ym_ref, yf_ref, dym_ref, dyf_ref, dg_ref):
        dm_v = dm_ref[...].astype(F32)
        sa = _sigmoid(ga_ref[...].astype(F32))
        sb = _sigmoid(gb_ref[...].astype(F32))
        dym_ref[...] = (dm_v * sa).astype(BF16)
        dyf_ref[...] = (dm_v * sb).astype(BF16)
        dg_ref[:, 0:D_MODEL] = (dm_v * ym_ref[...] * (sa * (1.0 - sa))).astype(BF16)
        dg_ref[:, D_MODEL:2 * D_MODEL] = (dm_v * yf_ref[...] * (sb * (1.0 - sb))).astype(BF16)

    return pl.pallas_call(
        body, name="merge_bwd", grid=(lp // BLK,),
        in_specs=[_row(D_MODEL), _rowc(D_MODEL, 2), _rowc(D_MODEL, 3), _row(D_MODEL), _row(D_MODEL)],
        out_specs=[_row(D_MODEL), _row(D_MODEL), _row(2 * D_MODEL)],
        out_shape=[jax.ShapeDtypeStruct((lp, D_MODEL), BF16), jax.ShapeDtypeStruct((lp, D_MODEL), BF16),
                   jax.ShapeDtypeStruct((lp, 2 * D_MODEL), BF16)],
        compiler_params=_cp(("parallel",)))(dm, gate, gate, y_mla, y_fox)


def _gate_bwd(da_mla, da_fox, o_mla, o_fox, gate):
    lp = da_mla.shape[0]

    def one(da, o, z, head_of_col):
        sg = _sigmoid(z)
        do = (da * (z * sg)).astype(BF16)
        dz = da * o * (sg * (1.0 + z * (1.0 - sg)))
        delta = sum(_dot(part, head_of_col, 1, 0) for part in _split3(do.astype(F32) * o))
        return do, dz.astype(BF16), delta

    def body(dam_ref, daf_ref, om_ref, of_ref, zm_ref, zf_ref, dom_ref, dof_ref, dz_ref, dlm_ref, dlf_ref):
        f32 = lambda r: r[...].astype(F32)
        head_of_col = (lax.broadcasted_iota(jnp.int32, (D_MODEL, LANES), 0) // HEAD_DIM
                       == lax.broadcasted_iota(jnp.int32, (D_MODEL, LANES), 1)).astype(BF16)
        dom_ref[...], dz_ref[:, 0:D_MODEL], dlm_ref[...] = one(f32(dam_ref), f32(om_ref), f32(zm_ref), head_of_col)
        dof_ref[...], dz_ref[:, D_MODEL:2 * D_MODEL], dlf_ref[...] = one(f32(daf_ref), f32(of_ref), f32(zf_ref),
                                                                        head_of_col)

    return pl.pallas_call(
        body, name="gate_bwd", grid=(lp // BLK,),
        in_specs=[_row(D_MODEL)] * 4 + [_rowc(D_MODEL, 0), _rowc(D_MODEL, 1)],
        out_specs=[_row(D_MODEL), _row(D_MODEL), _row(2 * D_MODEL), _row(LANES), _row(LANES)],
        out_shape=[jax.ShapeDtypeStruct((lp, D_MODEL), BF16), jax.ShapeDtypeStruct((lp, D_MODEL), BF16),
                   jax.ShapeDtypeStruct((lp, 2 * D_MODEL), BF16), jax.ShapeDtypeStruct((lp, LANES), F32),
                   jax.ShapeDtypeStruct((lp, LANES), F32)],
        compiler_params=_cp(("parallel",)))(da_mla, da_fox, o_mla, o_fox, gate, gate)


def _small_bwd(small, dqn, dkvn, dkr, dcol_t, drow_t, gq, gkv, fb, ctab, stab, triu):
    lp = small.shape[0]
    nb = lp // BLK

    def rrow(w):
        return pl.BlockSpec((BLK, w), lambda i: (nb - 1 - i, 0))

    def body(sm_ref, dqn_ref, dkvn_ref, dkr_ref, dcol_ref, drow_ref, gq_ref, gkv_ref, fb_ref, c_ref, s_ref, tri_ref,
             ds_ref, dgq_ref, dgkv_ref, dfb_ref, carry):
        i = pl.program_id(0)

        @pl.when(i == 0)
        def _():
            carry[...] = jnp.zeros_like(carry)
            dgq_ref[...] = jnp.zeros_like(dgq_ref)
            dgkv_ref[...] = jnp.zeros_like(dgkv_ref)
            dfb_ref[...] = jnp.zeros_like(dfb_ref)

        def norm_bwd(x, dn, g, dg_ref):
            r = lax.rsqrt(jnp.mean(x * x, axis=-1, keepdims=True) + RMS_EPS)
            dg_ref[...] += jnp.sum(dn * (x * r), axis=0, keepdims=True)
            w = dn * g
            dot = jnp.mean(w * x, axis=-1, keepdims=True)
            return r * w - x * (r * r * r * dot)

        ds_ref[:, 0:256] = norm_bwd(sm_ref[:, 0:256], dqn_ref[...], gq_ref[...], dgq_ref).astype(BF16)
        ds_ref[:, 256:384] = norm_bwd(sm_ref[:, 256:384], dkvn_ref[...], gkv_ref[...], dgkv_ref).astype(BF16)

        dk = dkr_ref[0]
        for p in range(1, PAIRS):
            dk = dk + dkr_ref[p]
        dk = _rope(dk, c_ref[...], -s_ref[...])
        lane = lax.broadcasted_iota(jnp.int32, dk.shape, 1)
        dk = jnp.where(lane < MLA_ROPE, dk + pltpu.roll(dk, LANES - MLA_ROPE, 1), 0.0)
        ds_ref[:, 384:512] = dk.astype(BF16)

        dcol = dcol_ref[0]
        for p in range(1, PAIRS):
            dcol = dcol + pltpu.roll(dcol_ref[p], 2 * p, 1)
        rows16 = jnp.concatenate([drow_ref[p, h:h + 1, :] for p in range(PAIRS) for h in range(2)], axis=0)
        eye = (lax.broadcasted_iota(jnp.int32, (HEADS, LANES), 0)
               == lax.broadcasted_iota(jnp.int32, (HEADS, LANES), 1)).astype(BF16)
        drow = sum(_dot(part, eye, 0, 0) for part in _split3(rows16))
        dcr = dcol - drow
        hi, mid, lo = _split3(dcr)
        t = tri_ref[...]
        suf = (_dot(t, hi, 1, 0) + _dot(t, mid, 1, 0)) + _dot(t, lo, 1, 0) + carry[...]
        fl = sm_ref[:, 512:640] + fb_ref[...]
        dfl = jnp.where(_row_valid(nb - 1 - i), -suf * _sigmoid(-fl), 0.0)
        ds_ref[:, 512:640] = dfl.astype(BF16)
        dfb_ref[...] += jnp.sum(dfl, axis=0, keepdims=True)
        carry[...] += jnp.sum(dcr, axis=0, keepdims=True)

    return pl.pallas_call(
        body, name="small_bwd", grid=(nb,),
        in_specs=[rrow(SMALL_W), rrow(256), rrow(128),
                  pl.BlockSpec((PAIRS, BLK, 128), lambda i: (0, nb - 1 - i, 0)),
                  pl.BlockSpec((PAIRS, BLK, 128), lambda i: (0, nb - 1 - i, 0)),
                  pl.BlockSpec((PAIRS, 2, BLK), lambda i: (0, 0, nb - 1 - i)),
                  _full((1, 256)), _full((1, 128)), _full((1, 128)), rrow(128), rrow(128), _full((BLK, BLK))],
        out_specs=[rrow(SMALL_W), _full((1, 256)), _full((1, 128)), _full((1, 128))],
        out_shape=[jax.ShapeDtypeStruct((lp, SMALL_W), BF16), jax.ShapeDtypeStruct((1, 256), F32),
                   jax.ShapeDtypeStruct((1, 128), F32), jax.ShapeDtypeStruct((1, 128), F32)],
        scratch_shapes=[pltpu.VMEM((1, 128), F32)],
        compiler_params=_cp(("arbitrary",)))(small, dqn, dkvn, dkr, dcol_t, drow_t, gq, gkv, fb, ctab, stab, triu)


def _pre_bwd(du, x2, meta, dy, gpre):
    s_rows = x2.shape[0]
    lp = PAD + s_rows
    shift = _shift_rows(D_MODEL)

    def body(du_ref, x_ref, meta_ref, dy_ref, g_ref, dx_ref, dmeta_ref, dg_ref):
        i = pl.program_id(0)

        @pl.when(i == 0)
        def _():
            dg_ref[...] = jnp.zeros_like(dg_ref)

        hv = _h_block(i, x_ref, meta_ref)
        duv = du_ref[...]
        r = lax.rsqrt(jnp.mean(hv * hv, axis=-1, keepdims=True) + RMS_EPS)
        dg_ref[...] += jnp.sum(duv * (hv * r), axis=0, keepdims=True)
        w = duv * g_ref[...]
        dot = jnp.mean(w * hv, axis=-1, keepdims=True)
        dh = dy_ref[...] + (r * w - hv * (r * r * r * dot))
        dx_ref[...] = dh

        @pl.when(i == 0)
        def _():
            dmeta_ref[...] = dh[0:N_META, :]

    return pl.pallas_call(
        body, name="pre_bwd", grid=(lp // BLK,),
        in_specs=[_row(D_MODEL), shift, _full((N_META, D_MODEL)), _row(D_MODEL), _full((1, D_MODEL))],
        out_specs=[shift, _full((N_META, D_MODEL)), _full((1, D_MODEL))],
        out_shape=[jax.ShapeDtypeStruct((s_rows, D_MODEL), F32), jax.ShapeDtypeStruct((N_META, D_MODEL), F32),
                   jax.ShapeDtypeStruct((1, D_MODEL), F32)],
        compiler_params=_cp(("arbitrary",)))(du, x2, meta, dy, gpre)


def _pair_masks(rope):
    lane = lax.broadcasted_iota(jnp.int32, (1, LANES), 1)
    mas = [lane < HEAD_DIM, lane >= HEAD_DIM]
    if not rope:
        return mas, mas
    wide = lax.broadcasted_iota(jnp.int32, (1, 2 * LANES), 1)
    rope_lo = LANES + MLA_ROPE
    return mas, [(wide < HEAD_DIM) | ((wide >= LANES) & (wide < rope_lo)),
                 ((wide >= HEAD_DIM) & (wide < LANES)) | ((wide >= rope_lo) & (wide < rope_lo + MLA_ROPE))]


def _mask2(x, masks):
    return [jnp.where(m, x, jnp.zeros_like(x)) for m in masks]


def _attn_fwd(q, k, v, *, kr=None, nbrep=None, scale, qcol, kcol, vcol, name):
    lp = q.shape[0]
    nq = 1 + (lp - PAD) // QB
    rope = kr is not None
    bias = nbrep is not None
    qw = 256 if rope else 128

    def body(*refs):
        it = iter(refs)
        q_ref, k_ref, v_ref = next(it), next(it), next(it)
        kr_ref = next(it) if rope else None
        nb_ref = next(it) if bias else None
        o_ref, lse_ref = next(it), next(it)
        i = pl.program_id(1)
        r0 = pl.multiple_of(jnp.where(i == 0, 0, PAD + QB * (i - 1)), BLK)
        b0 = r0 // BLK
        mas, hmask = _pair_masks(rope)
        qh = _mask2(q_ref[pl.ds(r0, QB), :], hmask)
        if bias:
            qh = [x * scale for x in qh]

        def update(kcs, carry, masks, ns=None, q_lo=0, wq=QB):
            ns = ns or [BLK] * len(kcs)
            stats, acc = carry[:4], carry[4]
            qs = [x[q_lo:q_lo + wq] for x in qh]
            k0s = [pl.multiple_of(kc * BLK, BLK) for kc in kcs]
            kks = [k_ref[pl.ds(k0, n), :] for k0, n in zip(k0s, ns)]
            if rope:
                kks = [jnp.concatenate([kk, kr_ref[pl.ds(k0, n), :]], axis=1) for kk, k0, n in zip(kks, k0s, ns)]
            new_stats, alphas, ps = [], [], [[] for _ in kcs]
            for h in range(2):
                m_prev, l_prev = stats[2 * h], stats[2 * h + 1]
                ss = []
                for kk, k0, n, mask in zip(kks, k0s, ns, masks):
                    s = _dot(kk, qs[h], 1, 1)
                    if rope:
                        s = s * scale
                    if bias:
                        nbc = nb_ref[h, pl.ds(k0, n), :]
                        s = s + jnp.concatenate([nbc] * (wq // LANES), axis=1)
                    if mask is not None:
                        s = jnp.where(mask, s, NEG)
                    ss.append(s)
                m_new = m_prev
                for s in ss:
                    m_new = jnp.maximum(m_new, jnp.max(s, axis=0, keepdims=True))
                alpha = jnp.exp(m_prev - m_new)
                l_new = alpha * l_prev
                for j, s in enumerate(ss):
                    p = jnp.exp(s - m_new)
                    l_new = l_new + jnp.sum(p, axis=0, keepdims=True)
                    ps[j].append(p.astype(BF16))
                new_stats += [m_new, l_new]
                alphas.append(alpha)
            vcat = jnp.concatenate([x for k0, n in zip(k0s, ns) for x in _mask2(v_ref[pl.ds(k0, n), :], mas)], axis=0)
            pv = _dot(vcat, jnp.concatenate([p for pj in ps for p in pj], axis=0), 0, 0)
            a_full = jnp.concatenate([jnp.broadcast_to(a, (HEAD_DIM, wq)) for a in alphas], axis=0)
            return (*new_stats, a_full * acc + pv)

        neg = jnp.full((1, QB), NEG, F32)
        zero = jnp.zeros((1, QB), F32)
        c = (neg, zero, neg, zero, jnp.zeros((LANES, QB), F32))
        n_mid = jnp.maximum(b0 - 1, 0)
        c = lax.fori_loop(0, n_mid // 4, lambda t, cr: update([4 * t + u for u in (1, 2, 3, 4)], cr, [None] * 4), c)
        c = lax.fori_loop(0, (n_mid % 4) // 2, lambda t, cr: update([n_mid - 1, n_mid], cr, [None, None]), c)
        key_l = lax.broadcasted_iota(jnp.int32, (BLK, BLK), 0)
        qry_l = lax.broadcasted_iota(jnp.int32, (BLK, BLK), 1)
        tri = (key_l <= qry_l) & (b0 > 0)
        meta_ok = (key_l[0:N_META] <= qry_l[0:N_META]) | (b0 > 0)
        lo = update([0, b0], tuple(a[:, 0:BLK] for a in c), [meta_ok, tri], ns=[N_META, BLK], q_lo=0, wq=BLK)
        hi = update([0, b0, b0 + 1], tuple(a[:, BLK:QB] for a in c), [None, None, tri], ns=[N_META, BLK, BLK],
                    q_lo=BLK, wq=QB - BLK)
        c = tuple(jnp.concatenate([a, b], axis=1) for a, b in zip(lo, hi))
        inv =jnp.concatenate([jnp.broadcast_to(1.0 / c[1], (HEAD_DIM, QB)),
                               jnp.broadcast_to(1.0 / c[3], (HEAD_DIM, QB))], axis=0)
        o_t = (c[4] * inv).T.astype(BF16)
        lses = [c[0] + jnp.log(c[1]), c[2] + jnp.log(c[3])]
        o_ref[pl.ds(r0, BLK), :] = o_t[0:BLK]
        for h in range(2):
            lse_ref[0, h:h + 1, pl.ds(r0, BLK)] = lses[h][:, 0:BLK]

        @pl.when(i > 0)
        def _():
            r1 = pl.multiple_of(r0 + BLK, BLK)
            o_ref[pl.ds(r1, QB - BLK), :] = o_t[BLK:QB]
            for h in range(2):
                lse_ref[0, h:h + 1, pl.ds(r1, QB - BLK)] = lses[h][:, BLK:QB]

    in_specs = [pl.BlockSpec((lp, qw), lambda p, i: (0, qcol + p)),
                pl.BlockSpec((lp, 128), lambda p, i: (0, kcol(p))),
                pl.BlockSpec((lp, 128), lambda p, i: (0, vcol(p)))]
    ins = [q, k, v]
    if rope:
        in_specs.append(pl.BlockSpec((lp, 128), lambda p, i: (0, 0)))
        ins.append(kr)
    if bias:
        in_specs.append(pl.BlockSpec((2, lp, 128), lambda p, i: (p, 0, 0)))
        ins.append(nbrep)
    return pl.pallas_call(
        body, name=name, grid=(PAIRS, nq), in_specs=in_specs,
        out_specs=[pl.BlockSpec((lp, 128), lambda p, i: (0, p)),
                   pl.BlockSpec((1, 2, lp), lambda p, i: (p, 0, 0))],
        out_shape=[jax.ShapeDtypeStruct((lp, D_MODEL), BF16), jax.ShapeDtypeStruct((PAIRS, 2, lp), F32)],
        compiler_params=_cp(("parallel", "arbitrary"), VMEM_BIG))(*ins)


def _attn_bwd(q, k, v, do, delta, lse, *, kr=None, rtabs=None, nbrep=None, scale, qcol, kcol, vcol, name):
    lp = q.shape[0]
    nb = lp // BLK
    rope = kr is not None
    bias = nbrep is not None
    qw = 256 if rope else 128

    def body(*refs):
        it = iter(refs)
        q_ref, k_ref, v_ref = next(it), next(it), next(it)
        kr_ref = next(it) if rope else None
        nb_ref = next(it) if bias else None
        do_ref, dl_ref, lse_ref = next(it), next(it), next(it)
        ct_ref, st_ref = (next(it), next(it)) if rope else (None, None)
        dq_out, dk_ref, dv_ref = next(it), next(it), next(it)
        x_ref = next(it)
        drow_ref = next(it) if bias else None
        dq_ref = next(it)
        kb = pl.program_id(1)
        mas, hmask = _pair_masks(rope)
        lane = lax.broadcasted_iota(jnp.int32, (1, LANES), 1)

        @pl.when(kb == 0)
        def _():
            dq_ref[...] = jnp.zeros_like(dq_ref)
            if bias:
                drow_ref[...] = jnp.zeros_like(drow_ref)

        def key_pass(n, w):
            kk = k_ref[0:n, :]
            if rope:
                kk = jnp.concatenate([kk, kr_ref[0:n, :]], axis=1)
            vh = _mask2(v_ref[0:n, :], mas)
            kcat = jnp.concatenate(_mask2(kk, hmask), axis=0)
            if bias:
                kcat = kcat * scale
            diag_mask = (lax.broadcasted_iota(jnp.int32, (n, w), 0) <= lax.broadcasted_iota(jnp.int32, (n, w), 1))

            def chunk(qc, carry, mask):
                carry = list(carry)
                q0 = qc * w if isinstance(qc, int) else pl.multiple_of(qc * w, w)
                dov = do_ref[pl.ds(q0, w), :]
                doh = _mask2(dov, mas)
                qh = _mask2(q_ref[pl.ds(q0, w), :], hmask)
                if bias:
                    qh = [x * scale for x in qh]
                pbs, dss = [], []
                for h in range(2):
                    s = _dot(kk, qh[h], 1, 1)
                    if rope:
                        s = s * scale
                    if bias:
                        s = s + jnp.concatenate([nb_ref[h, 0:n, :]] * (w // LANES), axis=1)
                    p = jnp.exp(s - lse_ref[0, h:h + 1, pl.ds(q0, w)])
                    if mask is not None:
                        p = jnp.where(mask, p, 0.0)
                    ds = p * (_dot(vh[h], dov, 1, 1) - dl_ref[0, h:h + 1, pl.ds(q0, w)])
                    if bias:
                        drow_ref[0, h:h + 1, pl.ds(q0, w)] += jnp.sum(ds, axis=0, keepdims=True)
                        carry[2 + h] = carry[2 + h] + jnp.sum(ds, axis=1, keepdims=True)
                    else:
                        ds = ds * scale
                    pbs.append(p.astype(BF16))
                    dss.append(ds.astype(BF16))
                ds_lanes = jnp.concatenate(dss, axis=1)
                ds_rows = jnp.concatenate(dss, axis=0)
                carry[0] = carry[0] + _dot(ds_lanes, jnp.concatenate(qh, axis=0), 1, 0)
                carry[1] = carry[1] + _dot(jnp.concatenate(pbs, axis=1), jnp.concatenate(doh, axis=0), 1, 0)
                dq_ref[pl.ds(q0, w), :] += _dot(ds_rows, kcat, 0, 0)
                return tuple(carry)

            c = [jnp.zeros((n, qw), F32), jnp.zeros((n, LANES), F32)]
            if bias:
                c += [jnp.zeros((n, 1), F32), jnp.zeros((n, 1), F32)]
            c = tuple(c)
            if w != BLK:
                for qc in range(lp // w):
                    c = chunk(qc, c, diag_mask if qc == 0 else None)
            else:
                groups = (nb - kb) // UNROLL

                def several(t, cr):
                    for u in range(UNROLL):
                        cr = chunk(kb + UNROLL * t + u, cr, (diag_mask | (t > 0)) if u == 0 else None)
                    return cr

                c = lax.fori_loop(0, groups, several, c)
                start = kb + UNROLL * groups
                pairs = (nb - start) // 2

                def two(t, cr):
                    qc = start + 2 * t
                    return chunk(qc + 1, chunk(qc, cr, diag_mask | (qc > kb)), None)

                c = lax.fori_loop(0, pairs, two, c)
                c = lax.fori_loop(start + 2 * pairs, nb, lambda qc, cr: chunk(qc, cr, diag_mask | (qc > kb)), c)

            def rows(a, dtype):
                a = a.astype(dtype)
                return a if n == BLK else jnp.concatenate([a, jnp.zeros((BLK - n, a.shape[1]), dtype)], axis=0)

            dk_ref[...] = rows(c[0][:, 0:LANES], BF16)
            dv_ref[...] = rows(c[1], BF16)
            if rope:
                x_ref[0] = rows(c[0][:, LANES:2 * LANES], F32)
            if bias:
                x_ref[0] = rows(jnp.where(lane == 0, c[2], jnp.where(lane == 1, c[3], 0.0)), F32)

        @pl.when(kb == 0)
        def _():
            key_pass(N_META, lp // 2)

        @pl.when(kb > 0)
        def _():
            key_pass(BLK, BLK)

        @pl.when(kb == nb - 1)
        def _():
            def fin(c, carry):
                r0 = pl.multiple_of(c * BLK, BLK)
                dq = dq_ref[pl.ds(r0, BLK), :]
                if rope:
                    back = _rope(dq[:, LANES:2 * LANES], ct_ref[pl.ds(r0, BLK), :], -st_ref[pl.ds(r0, BLK), :])
                    dq = jnp.concatenate([dq[:, 0:LANES], back], axis=1)
                dq_out[pl.ds(r0, BLK), :] = dq.astype(BF16)
                return carry

            lax.fori_loop(0, nb, fin, 0)

    in_specs = [pl.BlockSpec((lp, qw), lambda p, j: (0, qcol + p)),
                pl.BlockSpec((BLK, 128), lambda p, j: (j, kcol(p))),
                pl.BlockSpec((BLK, 128), lambda p, j: (j, vcol(p)))]
    ins = [q, k, v]
    if rope:
        in_specs.append(pl.BlockSpec((BLK, 128), lambda p, j: (j, 0)))
        ins.append(kr)
    if bias:
        in_specs.append(pl.BlockSpec((2, BLK, 128), lambda p, j: (p, j, 0)))
        ins.append(nbrep)
    in_specs += [pl.BlockSpec((lp, 128), lambda p, j: (0, p)), pl.BlockSpec((1, 2, lp), lambda p, j: (p, 0, 0)),
                 pl.BlockSpec((1, 2, lp), lambda p, j: (p, 0, 0))]
    ins += [do, delta, lse]
    if rope:
        in_specs += [pl.BlockSpec((lp, 128), lambda p, j: (0, 0))] * 2
        ins += list(rtabs)
    out_specs = [pl.BlockSpec((lp, qw), lambda p, j: (0, p)),
                 pl.BlockSpec((BLK, 128), lambda p, j: (j, p)),
                 pl.BlockSpec((BLK, 128), lambda p, j: (j, p)),
                 pl.BlockSpec((1, BLK, 128), lambda p, j: (p, j, 0))]
    out_shape = [jax.ShapeDtypeStruct((lp, PAIRS * qw), BF16), jax.ShapeDtypeStruct((lp, D_MODEL), BF16),
                 jax.ShapeDtypeStruct((lp, D_MODEL), BF16), jax.ShapeDtypeStruct((PAIRS, lp, 128), F32)]
    if bias:
        out_specs.append(pl.BlockSpec((1, 2, lp), lambda p, j: (p, 0, 0)))
        out_shape.append(jax.ShapeDtypeStruct((PAIRS, 2, lp), F32))
    return pl.pallas_call(
        body, name=name, grid=(PAIRS, nb), in_specs=in_specs, out_specs=out_specs, out_shape=out_shape,
        scratch_shapes=[pltpu.VMEM((lp, qw), F32)],
        compiler_params=_cp(("parallel", "arbitrary"), VMEM_BIG))(*ins)


def _adamw(w, g, m, v, name):
    lead = w.ndim - 2
    rows, cols = w.shape[lead:]
    big = rows * cols > 512 * 1024
    tr = 128 if big and rows % 128 == 0 else rows
    tc = 256 if big and tr == rows else cols

    def body(w_ref, g_ref, m_ref, v_ref, d_ref, nm_ref, nv_ref):
        gv = g_ref[...]
        nm = ADAM_B1 * m_ref[...] + (1.0 - ADAM_B1) * gv
        nv = ADAM_B2 * v_ref[...] + (1.0 - ADAM_B2) * (gv * gv)
        m_hat = nm / (1.0 - ADAM_B1 ** ADAM_STEP)
        v_hat = nv / (1.0 - ADAM_B2 ** ADAM_STEP)
        d_ref[...] = -ADAM_LR * (m_hat / (jnp.sqrt(v_hat) + ADAM_EPS) + ADAM_WD * w_ref[...])
        nm_ref[...] = nm
        nv_ref[...] = nv

    spec = pl.BlockSpec((1,) * lead + (tr, tc), lambda i, j: (0,) * lead + (i, j))
    return pl.pallas_call(
        body, name=name, grid=(rows // tr, cols // tc), in_specs=[spec] * 4, out_specs=[spec] * 3,
        out_shape=[jax.ShapeDtypeStruct(w.shape, F32)] * 3,
        compiler_params=_cp(("parallel", "parallel"), VMEM_BIG))(w, g, m, v)


def _add_cores(g, from_sib, name):
    n, rows, cols = g.shape
    half = rows // 2
    tr = _tile(half, (256, 240))
    nt = half // tr

    def body(lo_ref, hi_ref, s_ref, o_ref):
        mine = jnp.where(lax.axis_index("c") == 0, lo_ref[0], hi_ref[0])
        o_ref[0] = (mine.astype(F32) + s_ref[0].astype(F32)).astype(BF16)

    return pl.pallas_call(
        body, name=name, grid=(n, nt),
        in_specs=[pl.BlockSpec((1, tr, cols), lambda j, i: (j, i, 0)),
                  pl.BlockSpec((1, tr, cols), lambda j, i: (j, nt + i, 0)),
                  pl.BlockSpec((1, tr, cols), lambda j, i: (j, i, 0))],
        out_specs=pl.BlockSpec((1, tr, cols), lambda j, i: (j, i, 0)),
        out_shape=jax.ShapeDtypeStruct((n, half, cols), BF16),
        compiler_params=_cp(("parallel", "parallel"), VMEM_BIG))(g, g, from_sib)


def _add_chips(x, own, name):
    n, rows, cols = x.shape
    tr = _tile(rows, (256, 240))

    def body(x_ref, own_ref, o_ref):
        me = 2 * lax.axis_index("x") + lax.axis_index("y")
        v = [jnp.where(me == k, own_ref[...], x_ref[k]).astype(F32) for k in range(N_CHIPS)]
        o_ref[...] = ((v[0] + v[1]) + v[2]) + v[3]

    return pl.pallas_call(
        body, name=name, grid=(rows // tr,),
        in_specs=[pl.BlockSpec((n, tr, cols), lambda i: (0, i, 0)), pl.BlockSpec((tr, cols), lambda i: (i, 0))],
        out_specs=pl.BlockSpec((tr, cols), lambda i: (i, 0)),
        out_shape=jax.ShapeDtypeStruct((rows, cols), F32), compiler_params=_cp(("parallel",), VMEM_BIG))(x, own)


def _axes():
    return lax.axis_index("x"), lax.axis_index("y"), lax.axis_index("c")


def _other_chips(x, y):
    return [(1 - x, y), (x, 1 - y), (1 - x, 1 - y)]


ANY = pl.BlockSpec(memory_space=pl.ANY)


def _rcopy(src, dst, send_sems, recv_sems, k, to):
    return pltpu.make_async_remote_copy(src_ref=src, dst_ref=dst, send_sem=send_sems.at[k], recv_sem=recv_sems.at[k],
                                        device_id=to, device_id_type=MESH)


def _gather_weights(shards, meta):
    n = len(shards)

    def body(*refs):
        srcs, meta_ref = refs[:n], refs[n]
        outs, mout_ref = refs[n + 1:2 * n + 1], refs[2 * n + 1]
        send_sems, recv_sems = refs[2 * n + 2:]
        x, y, c = _axes()
        me = 2 * x + y
        sib = (x, y, 1 - c)
        chips = _other_chips(x, y)

        def half(t, chip_idx, cc):
            hr = shards[t].shape[0] // 2
            return outs[t].at[chip_idx, pl.ds(cc * hr, hr), :]

        first = []
        for j, (px, py) in enumerate(chips):
            for t in range(n):
                hr = shards[t].shape[0] // 2
                first.append(_rcopy(srcs[t].at[pl.ds(c * hr, hr), :], half(t, me, c), send_sems, recv_sems,
                                    3 * t + j, (px, py, c)))
            first.append(_rcopy(meta_ref, mout_ref.at[me], send_sems, recv_sems, 3 * n + j, (px, py, c)))
        for cp in first:
            cp.start()
        passed = []
        for j, (px, py) in enumerate(chips):
            src_chip = 2 * px + py
            for t in range(n):
                _rcopy(half(t, src_chip, c), half(t, src_chip, c), send_sems, recv_sems, 3 * t + j, sib).wait_recv()
                fwd = _rcopy(half(t, src_chip, c), half(t, src_chip, c), send_sems, recv_sems, 3 * (n + 1 + t) + j, sib)
                fwd.start()
                passed.append(fwd)
            _rcopy(mout_ref.at[src_chip], mout_ref.at[src_chip], send_sems, recv_sems, 3 * n + j, sib).wait_recv()
        for j, (px, py) in enumerate(chips):
            src_chip = 2 * px + py
            for t in range(n):
                _rcopy(half(t, src_chip, 1 - c), half(t, src_chip, 1 - c), send_sems, recv_sems,
                       3 * (n + 1 + t) + j, sib).wait_recv()
        for cp in first + passed:
            cp.wait_send()

    nsem = 3 * (2 * n + 1)
    return pl.pallas_call(
        body, name="gather_weights", in_specs=[ANY] * (n + 1), out_specs=[ANY] * (n + 1),
        out_shape=[jax.ShapeDtypeStruct((N_CHIPS,) + s.shape, s.dtype) for s in shards]
        + [jax.ShapeDtypeStruct((N_CHIPS,) + meta.shape, meta.dtype)],
        scratch_shapes=[pltpu.SemaphoreType.DMA((nsem,)), pltpu.SemaphoreType.DMA((nsem,))])(*shards, meta)


def _gather_late(shard):
    rows, cols = shard.shape
    hr = rows // 2
    src = jax.new_ref(shard, memory_space=pltpu.MemorySpace.HBM)
    out = jax.empty_ref(jax.ShapeDtypeStruct((N_CHIPS, rows, cols), shard.dtype), memory_space=pltpu.MemorySpace.HBM)

    @pl.kernel(mesh=plsc.ScalarSubcoreMesh(axis_name="seq", num_cores=1), name="gather_late",
               scratch_types=(pltpu.SemaphoreType.DMA((6,)), pltpu.SemaphoreType.DMA((6,))),
               compiler_params=pltpu.CompilerParams(collective_id=1))
    def launch(send_sems, recv_sems):
        x, y, c = _axes()
        me = 2 * x + y
        sib = (x, y, 1 - c)
        chips = _other_chips(x, y)
        barrier = pltpu.get_barrier_semaphore()
        for px, py in chips:
            pl.semaphore_signal(barrier, inc=1, device_id=(px, py, c), device_id_type=MESH)
        pl.semaphore_signal(barrier, inc=1, device_id=sib, device_id_type=MESH)
        pl.semaphore_wait(barrier, 4)

        def half(chip_idx, cc):
            return out.at[chip_idx, pl.ds(cc * hr, hr), :]

        first = [_rcopy(src.at[pl.ds(c * hr, hr), :], half(me, c), send_sems, recv_sems, j, (px, py, c))
                 for j, (px, py) in enumerate(chips)]
        for cp in first:
            cp.start()
        passed = []
        for j, (px, py) in enumerate(chips):
            land = half(2 * px + py, c)
            _rcopy(land, land, send_sems, recv_sems, j, sib).wait_recv()
            fwd = _rcopy(land, land, send_sems, recv_sems, 3 + j, sib)
            fwd.start()
            passed.append(fwd)
        for j, (px, py) in enumerate(chips):
            land = half(2 * px + py, 1 - c)
            _rcopy(land, land, send_sems, recv_sems, 3 + j, sib).wait_recv()
        for cp in first + passed:
            cp.wait_send()

    launch()
    return out[...]


def _swap_halves(gs):
    n = len(gs)
    ncopies = sum(g.shape[0] for g in gs)

    def body(*refs):
        srcs, outs = refs[:n], refs[n:2 * n]
        send_sems, recv_sems = refs[2 * n:]
        x, y, c = _axes()
        cps = []
        for t in range(n):
            hr = gs[t].shape[1] // 2
            for j in range(gs[t].shape[0]):
                cps.append(_rcopy(srcs[t].at[j, pl.ds((1 - c) * hr, hr), :], outs[t].at[j], send_sems, recv_sems,
                                  len(cps), (x, y, 1 - c)))
        for cp in cps:
            cp.start()
        for cp in cps:
            cp.wait()

    return pl.pallas_call(
        body, name="swap_halves", in_specs=[ANY] * n, out_specs=[ANY] * n,
        out_shape=[jax.ShapeDtypeStruct((g.shape[0], g.shape[1] // 2, g.shape[2]), g.dtype) for g in gs],
        scratch_shapes=[pltpu.SemaphoreType.DMA((ncopies,)), pltpu.SemaphoreType.DMA((ncopies,))])(*gs)


def _scatter_chips(parts):
    n = len(parts)
    srcs = [jax.new_ref(p, memory_space=pltpu.MemorySpace.HBM) for p in parts]
    outs = [jax.empty_ref(jax.ShapeDtypeStruct(p.shape, p.dtype), memory_space=pltpu.MemorySpace.HBM) for p in parts]

    @pl.kernel(mesh=plsc.ScalarSubcoreMesh(axis_name="seq", num_cores=1), name="scatter_chips",
               scratch_types=(pltpu.SemaphoreType.DMA((3 * n,)), pltpu.SemaphoreType.DMA((3 * n,))),
               compiler_params=pltpu.CompilerParams(collective_id=0))
    def launch(send_sems, recv_sems):
        x, y, c = _axes()
        me = 2 * x + y
        chips = _other_chips(x, y)
        barrier = pltpu.get_barrier_semaphore()
        for px, py in chips:
            pl.semaphore_signal(barrier, inc=1, device_id=(px, py, c), device_id_type=MESH)
        pl.semaphore_wait(barrier, 3)
        cps = []
        for j, (px, py) in enumerate(chips):
            for t in range(n):
                cps.append(_rcopy(srcs[t].at[2 * px + py], outs[t].at[me], send_sems, recv_sems, 3 * t + j,
                                  (px, py, c)))
        for cp in cps:
            cp.start()
        for cp in cps:
            cp.wait()

    launch()
    return [o[...] for o in outs]


def _swap_reduced(rs):
    n = len(rs)

    def body(*refs):
        srcs, outs = refs[:n], refs[n:2 * n]
        send_sems, recv_sems = refs[2 * n:]
        x, y, c = _axes()
        cps = [_rcopy(srcs[t], outs[t], send_sems, recv_sems, t, (x, y, 1 - c)) for t in range(n)]
        for cp in cps:
            cp.start()
        for cp in cps:
            cp.wait()

    return pl.pallas_call(
        body, name="swap_reduced", in_specs=[ANY] * n, out_specs=[ANY] * n,
        out_shape=[jax.ShapeDtypeStruct(r.shape, r.dtype) for r in rs],
        scratch_shapes=[pltpu.SemaphoreType.DMA((n,)), pltpu.SemaphoreType.DMA((n,))])(*rs)


SMALL_ROWS = 24 + 128


def _allreduce_small(vec):
    def body(v_ref, out_ref, slots, send_sems, recv_sems):
        x, y, c = _axes()
        me = 4 * x + 2 * y + c
        slots[me] = v_ref[...]
        cps = []
        for k in range(1, 8):
            kx, ky, kc = (k >> 2) & 1, (k >> 1) & 1, k & 1
            peer = (1 - x if kx else x, 1 - y if ky else y, 1 - c if kc else c)
            cps.append(_rcopy(v_ref, slots.at[me], send_sems, recv_sems, k - 1, peer))
        for cp in cps:
            cp.start()
        for cp in cps:
            cp.wait()
        tot = slots[0]
        for k in range(1, 8):
            tot = tot + slots[k]
        out_ref[...] = tot

    return pl.pallas_call(
        body, name="allreduce_small",
        in_specs=[pl.BlockSpec(memory_space=pltpu.VMEM)], out_specs=pl.BlockSpec(memory_space=pltpu.VMEM),
        out_shape=jax.ShapeDtypeStruct((SMALL_ROWS, 128), F32),
        scratch_shapes=[pltpu.VMEM((8, SMALL_ROWS, 128), F32), pltpu.SemaphoreType.DMA((7,)),
                        pltpu.SemaphoreType.DMA((7,))])(vec)


def _pack_p2(w_uq, w_ukv, w_br_mla, w_br_fox, w_out, dtype):
    parts = [w_uq.reshape(96, D_MODEL), w_ukv.reshape(64, D_MODEL), w_br_mla, w_br_fox, w_out]
    return jnp.concatenate([p.astype(dtype) for p in parts], axis=0)


def _unpack_p2(pk):
    return pk[0:96].reshape(256, 384), pk[96:160].reshape(128, 512), pk[160:416], pk[416:672], pk[672:928]


def _uq_arrange(w):
    w3 = w.reshape(256, HEADS, 96)
    nope = w3[:, :, :64].reshape(256, PAIRS, 128)
    pe = w3[:, :, 64:].reshape(256, PAIRS, 64)
    return jnp.concatenate([nope, pe, jnp.zeros((256, PAIRS, 64), w.dtype)], axis=2).reshape(256, PAIRS * 256)


def _uq_restore(g):
    g3 = g.reshape(256, PAIRS, 256)
    nope = g3[:, :, :128].reshape(256, HEADS, 64)
    pe = g3[:, :, 128:192].reshape(256, HEADS, 32)
    return jnp.concatenate([nope, pe], axis=2).reshape(256, HEADS * 96)


def _ukv_arrange(w):
    w3 = w.reshape(128, HEADS, 128)
    return jnp.concatenate([w3[:, :, :64].reshape(128, 1024), w3[:, :, 64:].reshape(128, 1024)], axis=1)


def _ukv_restore(g):
    kn = g[:, :1024].reshape(128, HEADS, 64)
    vv = g[:, 1024:].reshape(128, HEADS, 64)
    return jnp.concatenate([kn, vv], axis=2).reshape(128, HEADS * 128)


def _rope_tables(lp):
    r = np.arange(lp)
    pos = np.where(r < N_META, r, np.where(r >= PAD, r - PAD + N_META, 0)).astype(np.float32)
    half = MLA_ROPE // 2
    inv_freq = np.float32(ROPE_THETA) ** (-np.arange(half, dtype=np.float32) / np.float32(half))
    ang = (pos[:, None] * inv_freq[None, :]).astype(np.float32)
    cos, sin = np.cos(ang).astype(np.float32), np.sin(ang).astype(np.float32)
    one, zero = np.ones((lp, 64), np.float32), np.zeros((lp, 64), np.float32)
    return (jnp.asarray(np.concatenate([cos, cos, cos, cos, one], axis=1)),
            jnp.asarray(np.concatenate([-sin, sin, -sin, sin, zero], axis=1)))


def _pad_lanes(v, n=128):
    return jnp.pad(v, ((0, 0), (0, n - v.shape[1])))


def _in_cols(slabs, a, b):
    out = []
    for j in range(N_CHIPS):
        lo, hi = max(a, W_IN_SHARD * j), min(b, W_IN_SHARD * (j + 1))
        if lo < hi:
            out.append(slabs[j][:, lo - W_IN_SHARD * j:hi - W_IN_SHARD * j])
    return out


def _local_step(x2, tgt2, meta_f, w_small, w_attn, w_gate, w_uq_f, w_ukv_f, w_bm, w_bf, w_o, pre_norm_g,
                post_norm_g, mla_q_norm_g, mla_kv_norm_g, fox_forget_b, start_exchange=None):
    s_rows = x2.shape[0]
    lp = PAD + s_rows
    w_uq_a = _uq_arrange(w_uq_f)
    w_ukv_a = _ukv_arrange(w_ukv_f)

    ctab, stab = _rope_tables(lp)
    ii = jnp.arange(BLK)
    tri_lo = (ii[:, None] >= ii[None, :]).astype(BF16)
    tri_up = (ii[:, None] <= ii[None, :]).astype(BF16)
    fb128 = _pad_lanes(fox_forget_b)

    u = _rms_pre(x2, meta_f, pre_norm_g)
    small = _mm(u, w_small, mode="nn", out_dtype=F32, name="proj_small")
    attn = _mm(u, w_attn, mode="nn", out_dtype=BF16, name="proj_attn")
    gate = _mm(u, w_gate, mode="nn", out_dtype=BF16, name="proj_gate")
    qn, kvn, kr, ncum = _small_prep(small, mla_q_norm_g, mla_kv_norm_g, fb128, ctab, stab, tri_lo)
    qcat = _mm(qn, w_uq_a, mode="nn", out_dtype=BF16, name="mla_q", epilogue=_rope_pairs, row_ins=(ctab, stab))
    kv = _mm(kvn, w_ukv_a, mode="nn", out_dtype=BF16, name="mla_kv")
    nbrep = jnp.broadcast_to(ncum[:, :HEADS].T[:, :, None], (HEADS, lp, LANES))

    mla_cols = dict(qcol=0, kcol=lambda p: p, vcol=lambda p: PAIRS + p)
    fox_cols = dict(qcol=0, kcol=lambda p: PAIRS + p, vcol=lambda p: 2 * PAIRS + p)
    o_mla, lse_mla = _attn_fwd(qcat, kv, kv, kr=kr, scale=MLA_SCALE, name="mla_fwd", **mla_cols)
    o_fox, lse_fox = _attn_fwd(attn, attn, attn, nbrep=nbrep, scale=FOX_SCALE, name="fox_fwd", **fox_cols)

    a_mla, a_fox = _gate_fwd(o_mla, o_fox, gate)
    y_mla = _mm(a_mla, w_bm, mode="nn", out_dtype=BF16, name="br_mla")
    y_fox = _mm(a_fox, w_bf, mode="nn", out_dtype=BF16, name="br_fox")
    mg = _merge_fwd(gate, y_mla, y_fox)
    mixed = _mm(mg, w_o, mode="nn", out_dtype=F32, name="out_proj")
    dmixed, dy, loss_p, dg_post = _tail(x2, mixed, tgt2, post_norm_g)

    d_w_out = _mm(mg, dmixed, mode="tn", out_dtype=F32, name="d_w_out")
    dm = _mm(dmixed, w_o, mode="nt", out_dtype=BF16, name="d_merge")
    dy_mla, dy_fox, dgate_ab = _merge_bwd(dm, gate, y_mla, y_fox)
    d_w_bm = _mm(a_mla, dy_mla, mode="tn", out_dtype=F32, name="d_w_br_mla")
    d_w_bf = _mm(a_fox, dy_fox, mode="tn", out_dtype=F32, name="d_w_br_fox")
    da_mla = _mm(dy_mla, w_bm, mode="nt", out_dtype=BF16, name="d_a_mla")
    da_fox = _mm(dy_fox, w_bf, mode="nt", out_dtype=BF16, name="d_a_fox")
    do_mla, do_fox, dgate_z, dl_mla, dl_fox = _gate_bwd(da_mla, da_fox, o_mla, o_fox, gate)
    dl_mla, dl_fox = (d[:, :HEADS].T.reshape(PAIRS, 2, lp) for d in (dl_mla, dl_fox))

    dq_a, dkn, dvm, dkr = _attn_bwd(qcat, kv, kv, do_mla, dl_mla, lse_mla, kr=kr, rtabs=(ctab, stab),
                                    scale=MLA_SCALE, name="mla_bwd", **mla_cols)
    dfq, dfk, dfv, dcol, drow = _attn_bwd(attn, attn, attn, do_fox, dl_fox, lse_fox, nbrep=nbrep, scale=FOX_SCALE,
                                          name="fox_bwd", **fox_cols)

    d_w_uq_a = _mm(qn, dq_a, mode="tn", out_dtype=F32, name="d_w_uq")
    dqn = _mm(dq_a, w_uq_a, mode="nt", out_dtype=F32, name="d_qn")
    d_w_ukv_a = jnp.concatenate([_mm(kvn, dkn, mode="tn", out_dtype=F32, name="d_w_uk"),
                                 _mm(kvn, dvm, mode="tn", out_dtype=F32, name="d_w_uv")], axis=1)
    dkvn = _mm(dkn, w_ukv_a[:, :1024], mode="nt", out_dtype=F32, name="d_kvn_k")
    dkvn = _mm(dvm, w_ukv_a[:, 1024:], mode="nt", out_dtype=F32, name="d_kvn_v", acc=dkvn)
    dsmall, dg_q, dg_kv, dfb = _small_bwd(small, dqn, dkvn, dkr, dcol, drow, mla_q_norm_g, mla_kv_norm_g,
                                          fb128, ctab, stab, tri_up)

    dw_small = _mm(u, dsmall, mode="tn", out_dtype=BF16, name="d_w_small")
    dw_fq = _mm(u, dfq, mode="tn", out_dtype=BF16, name="d_w_fq")
    dw_fk = _mm(u, dfk, mode="tn", out_dtype=BF16, name="d_w_fk")
    dw_fv = _mm(u, dfv, mode="tn", out_dtype=BF16, name="d_w_fv")
    dw_z = _mm(u, dgate_z, mode="tn", out_dtype=BF16, name="d_w_z")
    dw_g = _mm(u, dgate_ab, mode="tn", out_dtype=BF16, name="d_w_g")
    d_w_in = (dw_small, dw_z, dw_fq, dw_fk, dw_fv, dw_g)
    d_w_uq = _uq_restore(d_w_uq_a)
    d_w_ukv = _ukv_restore(d_w_ukv_a)
    token = start_exchange(d_w_in, d_w_uq, d_w_ukv, d_w_bm, d_w_bf, d_w_out) if start_exchange else None
    du = _mm_sum_nt([(dsmall, w_small), (dfq, w_attn[:, 0:1024]), (dfk, w_attn[:, 1024:2048]),
                     (dfv, w_attn[:, 2048:3072]), (dgate_z, w_gate[:, 0:2048]), (dgate_ab, w_gate[:, 2048:4096])],
                    name="d_u", after=token)
    dx, dmeta, dg_pre = _pre_bwd(du, x2, meta_f, dy, pre_norm_g)
    return (loss_p, dx, dmeta, d_w_in, d_w_uq, d_w_ukv, d_w_bm, d_w_bf, d_w_out, dg_pre, dg_post, dg_q, dg_kv, dfb)


def _w_in_slabs(pieces):
    dw_small, dw_z, dw_fq, dw_fk, dw_fv, dw_g = pieces
    runs = [(dw_small[:, 0:416], C_CQ), (dw_z[:, 0:1024], C_ZMLA), (dw_fq, C_FQ), (dw_fk, C_FK), (dw_fv, C_FV),
            (dw_small[:, 512:528], C_FL), (dw_z[:, 1024:2048], C_ZFOX), (dw_g, C_GA)]
    slabs = []
    for j in range(N_CHIPS):
        lo, hi = W_IN_SHARD * j, W_IN_SHARD * (j + 1)
        cols = [a[:, max(lo, c0) - c0:min(hi, c0 + a.shape[1]) - c0] for a, c0 in runs
                if max(lo, c0) < min(hi, c0 + a.shape[1])]
        slabs.append(jnp.concatenate(cols, axis=1))
    return jnp.stack(slabs, axis=0)


def kernel(x, meta_tokens, pre_norm_g, w_in, fox_forget_b, mla_q_norm_g, mla_kv_norm_g, w_uq, w_ukv, w_br_mla, w_br_fox, w_out, post_norm_g, loss_target, m_meta_tokens, m_pre_norm_g, m_w_in, m_fox_forget_b, m_mla_q_norm_g, m_mla_kv_norm_g, m_w_uq, m_w_ukv, m_w_br_mla, m_w_br_fox, m_w_out, m_post_norm_g, v_meta_tokens, v_pre_norm_g, v_w_in, v_fox_forget_b, v_mla_q_norm_g, v_mla_kv_norm_g, v_w_uq, v_w_ukv, v_w_br_mla, v_w_br_fox, v_w_out, v_post_norm_g):
    me = 2 * lax.axis_index("x") + lax.axis_index("y")
    core = lax.axis_index("c")
    w_in_b = w_in.astype(BF16).reshape(D_MODEL, W_IN_SHARD)
    p2 = _pack_p2(w_uq[0], w_ukv[0], w_br_mla[0], w_br_fox[0], w_out[0], BF16)
    w_in_g, meta_g = _gather_weights([w_in_b], meta_tokens)
    p2_g = _gather_late(lax.optimization_barrier((p2, w_in_g))[0])
    slabs = [jnp.where(me == j, w_in_b, w_in_g[j]) for j in range(N_CHIPS)]
    chip = lax.broadcasted_iota(jnp.int32, (N_CHIPS, 1, 1), 0)
    p2_all = jnp.where(chip == me, p2[None], p2_g)
    w_uq_f = p2_all[:, 0:96].reshape(N_CHIPS, 256, 384).transpose(1, 0, 2).reshape(256, 1536)
    w_ukv_f = p2_all[:, 96:160].reshape(N_CHIPS, 128, 512).transpose(1, 0, 2).reshape(128, 2048)
    w_bm, w_bf, w_o = (p2_all[:, lo:lo + 256].reshape(D_MODEL, D_MODEL) for lo in (160, 416, 672))
    meta_f = jnp.where(chip == me, meta_tokens[None], meta_g).transpose(1, 0, 2).reshape(N_META, D_MODEL)
    kpe = _in_cols(slabs, C_KPE, C_ZMLA)
    w_small = jnp.concatenate(_in_cols(slabs, C_CQ, C_KPE) + kpe + kpe + [jnp.zeros((D_MODEL, 64), BF16)]
                              + _in_cols(slabs, C_FL, C_ZFOX) + [jnp.zeros((D_MODEL, 112), BF16)], axis=1)
    w_attn = jnp.concatenate(_in_cols(slabs, C_FQ, C_FL), axis=1)
    w_gate = jnp.concatenate(_in_cols(slabs, C_ZMLA, C_FQ) + _in_cols(slabs, C_ZFOX, C_END), axis=1)

    exchange = {}

    def start_exchange(d_w_in, d_w_uq, d_w_ukv, d_w_bm, d_w_bf, d_w_out):
        g2 = jnp.concatenate(
            [d_w_uq.reshape(256, N_CHIPS, 384).transpose(1, 0, 2).reshape(N_CHIPS, 96, D_MODEL),
             d_w_ukv.reshape(128, N_CHIPS, 512).transpose(1, 0, 2).reshape(N_CHIPS, 64, D_MODEL)]
            + [g.reshape(N_CHIPS, 256, D_MODEL) for g in (d_w_bm, d_w_bf, d_w_out)], axis=1)
        pieces = [p[None] for p in d_w_in]
        from_sib = _swap_halves(pieces + [g2])
        halves = [_add_cores(p, s, "add_cores_" + nm)[0]
                  for p, s, nm in zip(pieces, from_sib, ("small", "z", "fq", "fk", "fv", "g"))]
        parts = [_w_in_slabs(halves), _add_cores(g2, from_sib[-1], "add_cores_rest")]
        exchange.update(parts=parts, landed=_scatter_chips(parts))
        return parts[0][0, 0:16, 0:LANES]

    (loss_p, dx, dmeta, _, _, _, _, _, _, dg_pre, dg_post, dg_q, dg_kv,
     dfb) = _local_step(x[0], loss_target[0], meta_f, w_small, w_attn, w_gate, w_uq_f, w_ukv_f, w_bm, w_bf, w_o,
                        pre_norm_g, post_norm_g, mla_q_norm_g, mla_kv_norm_g, fox_forget_b, start_exchange)

    mine = [_add_chips(l, lax.dynamic_index_in_dim(p, me, 0, keepdims=False), nm)
            for l, p, nm in zip(exchange["landed"], exchange["parts"], ("add_chips_w_in", "add_chips_rest"))]
    theirs = _swap_reduced(mine)
    g_w_in, g_p2 = [jnp.concatenate([jnp.where(core == 0, a, b), jnp.where(core == 0, b, a)], axis=0)
                    for a, b in zip(mine, theirs)]
    g_w_uq, g_w_ukv, g_w_bm, g_w_bf, g_w_out = _unpack_p2(g_p2)
    g_w_in = g_w_in[None]

    vec = jnp.concatenate([dg_pre.reshape(8, 128), dg_post.reshape(8, 128), dg_q.reshape(2, 128), dg_kv,
                           dfb, _pad_lanes(loss_p), jnp.zeros((3, 128), F32), dmeta.reshape(128, 128)], axis=0)
    tot = _allreduce_small(vec)
    loss = tot[20, 0]
    g_meta = lax.dynamic_slice_in_dim(tot[24:].reshape(N_META, D_MODEL), 256 * me, 256, axis=1)

    def small_pack(pre, post, gq_, gkv_, fb_):
        return jnp.concatenate([pre.reshape(8, 128), post.reshape(8, 128), gq_.reshape(2, 128), gkv_,
                                _pad_lanes(fb_), jnp.zeros((4, 128), F32)], axis=0)

    def small_unpack(t):
        return (t[0:8].reshape(1, 1024), t[8:16].reshape(1, 1024), t[16:18].reshape(1, 256), t[18:19],
                t[19:20, 0:HEADS])

    g_small = jnp.concatenate([tot[0:20], jnp.zeros((4, 128), F32)], axis=0)
    sm = _adamw(small_pack(pre_norm_g, post_norm_g, mla_q_norm_g, mla_kv_norm_g, fox_forget_b), g_small,
                small_pack(m_pre_norm_g, m_post_norm_g, m_mla_q_norm_g, m_mla_kv_norm_g, m_fox_forget_b),
                small_pack(v_pre_norm_g, v_post_norm_g, v_mla_q_norm_g, v_mla_kv_norm_g, v_fox_forget_b),
                "adamw_small")
    g_pre, g_post, g_q, g_kv, g_fb = small_unpack(g_small)
    (d_pre, d_post, d_q, d_kv, d_fb), (nm_pre, nm_post, nm_q, nm_kv, nm_fb), (nv_pre, nv_post, nv_q, nv_kv, nv_fb) = (
        small_unpack(t) for t in sm)

    d_meta, nm_meta, nv_meta = _adamw(meta_tokens, g_meta, m_meta_tokens, v_meta_tokens, "adamw_meta")
    d_win, nm_win, nv_win = (t.T[None] for t in _adamw(w_in[0].T, g_w_in[0].T, m_w_in[0].T, v_w_in[0].T,
                                                       "adamw_w_in"))
    d_wuq, nm_wuq, nv_wuq = _adamw(w_uq[0], g_w_uq, m_w_uq[0], v_w_uq[0], "adamw_w_uq")
    d_wukv, nm_wukv, nv_wukv = _adamw(w_ukv[0], g_w_ukv, m_w_ukv[0], v_w_ukv[0], "adamw_w_ukv")
    d_wbm, nm_wbm, nv_wbm = _adamw(w_br_mla[0], g_w_bm, m_w_br_mla[0], v_w_br_mla[0], "adamw_w_br_mla")
    d_wbf, nm_wbf, nv_wbf = _adamw(w_br_fox[0], g_w_bf, m_w_br_fox[0], v_w_br_fox[0], "adamw_w_br_fox")
    d_wo, nm_wo, nv_wo = _adamw(w_out[0], g_w_out, m_w_out[0], v_w_out[0], "adamw_w_out")

    def group(meta_, pre, win, fb_, q_, kv_, wuq, wukv, wbm, wbf, wo, post):
        return (meta_, pre, win, fb_, q_, kv_, wuq[None], wukv[None], wbm[None], wbf[None], wo[None], post)

    grads = group(g_meta, g_pre, g_w_in, g_fb, g_q, g_kv, g_w_uq, g_w_ukv, g_w_bm, g_w_bf, g_w_out, g_post)
    deltas = group(d_meta, d_pre, d_win, d_fb, d_q, d_kv, d_wuq, d_wukv, d_wbm, d_wbf, d_wo, d_post)
    new_m = group(nm_meta, nm_pre, nm_win, nm_fb, nm_q, nm_kv, nm_wuq, nm_wukv, nm_wbm, nm_wbf, nm_wo, nm_post)
    new_v = group(nv_meta, nv_pre, nv_win, nv_fb, nv_q, nv_kv, nv_wuq, nv_wukv, nv_wbm, nv_wbf, nv_wo, nv_post)
    return (loss, dx[None], *grads, *deltas, *new_m, *new_v)
```

```python
import math

import jax
import jax.numpy as jnp
import numpy as np
from jax import lax
from jax.experimental import pallas as pl
from jax.experimental.pallas import tpu as pltpu
from jax.experimental.pallas import tpu_sc as plsc

F32 = jnp.float32
BF16 = jnp.bfloat16

D_MODEL = 1024
N_META = 16
RMS_EPS = 1e-6
HEADS = 16
PAIRS = HEADS // 2
HEAD_DIM = 64
LANES = 128
MLA_ROPE = 32
MLA_SCALE = 1.0 / math.sqrt(64 + 32)
FOX_SCALE = 1.0 / math.sqrt(64)
ROPE_THETA = 10000.0

PAD = 256
BLK = 256
QB = 512
UNROLL = 4
NEG = -1e30

C_CQ, C_CKV, C_KPE, C_ZMLA, C_FQ, C_FK, C_FV, C_FL, C_ZFOX, C_GA, C_GB, C_END = (
    0, 256, 384, 416, 1440, 2464, 3488, 4512, 4528, 5552, 6576, 7600)
SMALL_W = 640
W_IN_SHARD = 1900

P2_ROWS = 928
N_CHIPS = 4

ADAM_LR = 0.001
ADAM_B1 = 0.9
ADAM_B2 = 0.999
ADAM_EPS = 1e-08
ADAM_WD = 0.01
ADAM_STEP = 10

VMEM_BIG = 56 * 1024 * 1024
MM_VMEM_BUDGET = 44 * 1024 * 1024
MESH = pl.DeviceIdType.MESH


def _cp(dims, vmem=None):
    return pltpu.CompilerParams(dimension_semantics=dims, vmem_limit_bytes=vmem)


def _dot(a, b, ca, cb):
    return lax.dot_general(a, b, (((ca,), (cb,)), ((), ())), preferred_element_type=F32)


def _sigmoid(x):
    return 1.0 / (1.0 + jnp.exp(-x))


def _tile(n, cands):
    for c in cands:
        if n % c == 0:
            return c
    return n


def _mm(a, b, *, mode, out_dtype, name, acc=None, epilogue=None, row_ins=(), after=None):
    if mode == "nn":
        (M, K), N = a.shape, b.shape[1]
    elif mode == "nt":
        (M, K), N = a.shape, b.shape[0]
    else:
        (K, M), N = a.shape, b.shape[1]
    tm = _tile(M, (1088, 1024)) if M > 1024 else M
    tn = _tile(N, (1024,)) if N > 1024 else N
    nk = 1
    while True:
        tk = K // nk
        need = 2 * tk * (tm * a.dtype.itemsize + tn * b.dtype.itemsize) + tm * tn * (
            2 * jnp.dtype(out_dtype).itemsize + (8 if acc is not None else 0) + (4 if nk > 1 else 0))
        if need <= MM_VMEM_BUDGET or (tk // 2) % (16 if mode == "tn" else LANES) or tk <= 512:
            break
        nk *= 2
    while (M // tm) * (N // tn) * nk < 4 and tn % 512 == 0:
        tn //= 2
    ca, cb = {"nn": (1, 0), "nt": (1, 1), "tn": (0, 0)}[mode]
    a_spec = (pl.BlockSpec((tk, tm), lambda j, i, k: (k, i)) if mode == "tn"
              else pl.BlockSpec((tm, tk), lambda j, i, k: (i, k)))
    b_spec = (pl.BlockSpec((tn, tk), lambda j, i, k: (j, k)) if mode == "nt"
              else pl.BlockSpec((tk, tn), lambda j, i, k: (k, j)))
    o_spec = pl.BlockSpec((tm, tn), lambda j, i, k: (i, j))
    has_acc = acc is not None

    nrow = len(row_ins)

    def body(*refs):
        a_ref, b_ref = refs[0], refs[1]
        acc_ref = refs[2] if has_acc else None
        rows = refs[2 + has_acc:2 + has_acc + nrow]
        o_ref = refs[2 + has_acc + nrow + (after is not None)]

        def store(tile):
            if epilogue is not None:
                tile = epilogue(tile, *[r[...] for r in rows])
            o_ref[...] = tile.astype(out_dtype)

        part = _dot(a_ref[...].astype(BF16), b_ref[...].astype(BF16), ca, cb)
        if nk == 1:
            store(part + acc_ref[...] if has_acc else part)
        else:
            sc = refs[-1]
            k = pl.program_id(2)

            @pl.when(k == 0)
            def _():
                sc[...] = part + acc_ref[...] if has_acc else part

            @pl.when(k > 0)
            def _():
                sc[...] += part

            @pl.when(k == nk - 1)
            def _():
                store(sc[...])

    ins = [a, b] + ([acc] if has_acc else []) + list(row_ins)
    in_specs = ([a_spec, b_spec] + ([o_spec] if has_acc else [])
                + [pl.BlockSpec((tm, r.shape[1]), lambda j, i, k: (i, 0)) for r in row_ins])
    if after is not None:
        ins.append(after)
        in_specs.append(pl.BlockSpec(after.shape, lambda j, i, k: (0,) * after.ndim))
    return pl.pallas_call(
        body, name=name, grid=(N // tn, M // tm, nk), in_specs=in_specs, out_specs=o_spec,
        out_shape=jax.ShapeDtypeStruct((M, N), out_dtype),
        scratch_shapes=[pltpu.VMEM((tm, tn), F32)] if nk > 1 else [],
        compiler_params=_cp(("parallel", "parallel", "arbitrary"), VMEM_BIG))(*ins)


def _mm_sum_nt(pairs, *, name, after=None):
    n = len(pairs)
    M, N = pairs[0][0].shape[0], pairs[0][1].shape[0]
    tm = _tile(M, (272,))

    def body(*refs):
        o_ref = refs[2 * n + (after is not None)]
        tot = _dot(refs[0][...].astype(BF16), refs[n][...].astype(BF16), 1, 1)
        for i in range(1, n):
            tot = tot + _dot(refs[i][...].astype(BF16), refs[n + i][...].astype(BF16), 1, 1)
        o_ref[...] = tot

    ins = [a for a, _ in pairs] + [b for _, b in pairs]
    in_specs = ([pl.BlockSpec((tm, a.shape[1]), lambda i: (i, 0)) for a, _ in pairs]
                + [pl.BlockSpec(b.shape, lambda i: (0, 0)) for _, b in pairs])
    if after is not None:
        ins.append(after)
        in_specs.append(pl.BlockSpec(after.shape, lambda i: (0,) * after.ndim))
    return pl.pallas_call(
        body, name=name, grid=(M // tm,), in_specs=in_specs, out_specs=pl.BlockSpec((tm, N), lambda i: (i, 0)),
        out_shape=jax.ShapeDtypeStruct((M, N), F32), compiler_params=_cp(("parallel",), VMEM_BIG))(*ins)


def _row(w):
    return pl.BlockSpec((BLK, w), lambda i: (i, 0))


def _rowc(w, c):
    return pl.BlockSpec((BLK, w), lambda i: (i, c))


def _full(shape):
    return pl.BlockSpec(shape, lambda i: tuple(0 for _ in shape))


def _rope(x, c, s):
    lane = lax.broadcasted_iota(jnp.int32, x.shape, 1)
    is_x1 = ((lane >> 4) & 1) == 0
    partner = jnp.where(is_x1, pltpu.roll(x, LANES - 16, 1), pltpu.roll(x, 16, 1))
    return x * c + partner * s


def _row_valid(i):
    rows = i * BLK + lax.broadcasted_iota(jnp.int32, (BLK, 1), 0)
    return (rows < N_META) | (rows >= PAD)


def _shift_rows(w):
    return pl.BlockSpec((BLK, w), lambda i: (jnp.maximum(i - 1, 0), 0))


def _h_block(i, x_ref, meta_ref):
    head = jnp.concatenate([meta_ref[...], jnp.zeros((BLK - N_META, D_MODEL), F32)], axis=0)
    return jnp.where(i == 0, head, x_ref[...])


def _rms_pre(x2, meta, g):
    lp = PAD + x2.shape[0]

    def body(x_ref, meta_ref, g_ref, u_ref):
        hv = _h_block(pl.program_id(0), x_ref, meta_ref)
        r = lax.rsqrt(jnp.mean(hv * hv, axis=-1, keepdims=True) + RMS_EPS)
        u_ref[...] = (hv * r * g_ref[...]).astype(BF16)

    return pl.pallas_call(
        body, name="rms_pre", grid=(lp // BLK,),
        in_specs=[_shift_rows(D_MODEL), _full((N_META, D_MODEL)), _full((1, D_MODEL))], out_specs=_row(D_MODEL),
        out_shape=jax.ShapeDtypeStruct((lp, D_MODEL), BF16),
        compiler_params=_cp(("parallel",)))(x2, meta, g)


def _split3(x):
    hi = x.astype(BF16)
    r1 = x - hi.astype(F32)
    mid = r1.astype(BF16)
    lo = (r1 - mid.astype(F32)).astype(BF16)
    return hi, mid, lo


def _small_prep(small, gq, gkv, fb, ctab, stab, tri):
    lp = small.shape[0]

    def body(sm_ref, gq_ref, gkv_ref, fb_ref, c_ref, s_ref, tri_ref, qn_ref, kvn_ref, kr_ref, ncum_ref, carry):
        i = pl.program_id(0)

        @pl.when(i == 0)
        def _():
            carry[...] = jnp.zeros_like(carry)

        cq = sm_ref[:, 0:256]
        r = lax.rsqrt(jnp.mean(cq * cq, axis=-1, keepdims=True) + RMS_EPS)
        qn_ref[...] = (cq * r * gq_ref[...]).astype(BF16)
        ckv = sm_ref[:, 256:384]
        r = lax.rsqrt(jnp.mean(ckv * ckv, axis=-1, keepdims=True) + RMS_EPS)
        kvn_ref[...] = (ckv * r * gkv_ref[...]).astype(BF16)
        kr_ref[...] = _rope(sm_ref[:, 384:512], c_ref[...], s_ref[...]).astype(BF16)
        fl = sm_ref[:, 512:640] + fb_ref[...]
        lf = jnp.minimum(fl, 0.0) - jnp.log(1.0 + jnp.exp(-jnp.abs(fl)))
        lf = jnp.where(_row_valid(i), lf, 0.0)
        hi, mid, lo = _split3(lf)
        t = tri_ref[...]
        cum = (_dot(t, hi, 1, 0) + _dot(t, mid, 1, 0)) + _dot(t, lo, 1, 0) + carry[...]
        ncum_ref[...] = -cum
        carry[...] = -ncum_ref[BLK - 1:BLK, :]

    return pl.pallas_call(
        body, name="small_prep", grid=(lp // BLK,),
        in_specs=[_row(SMALL_W), _full((1, 256)), _full((1, 128)), _full((1, 128)), _row(128), _row(128),
                  _full((BLK, BLK))],
        out_specs=[_row(256), _row(128), _row(128), _row(128)],
        out_shape=[jax.ShapeDtypeStruct((lp, 256), BF16), jax.ShapeDtypeStruct((lp, 128), BF16),
                   jax.ShapeDtypeStruct((lp, 128), BF16), jax.ShapeDtypeStruct((lp, 128), F32)],
        scratch_shapes=[pltpu.VMEM((1, 128), F32)],
        compiler_params=_cp(("arbitrary",)))(small, gq, gkv, fb, ctab, stab, tri)


def _rope_pairs(tile, c, s):
    out = []
    for lo in range(0, tile.shape[1], 256):
        out += [tile[:, lo:lo + 128], _rope(tile[:, lo + 128:lo + 256], c, s)]
    return jnp.concatenate(out, axis=1)


def _gate_fwd(o_mla, o_fox, gate):
    lp = o_mla.shape[0]

    def body(om_ref, of_ref, zm_ref, zf_ref, am_ref, af_ref):
        zm = zm_ref[...].astype(F32)
        am_ref[...] = (om_ref[...] * (zm * _sigmoid(zm))).astype(BF16)
        zf = zf_ref[...].astype(F32)
        af_ref[...] = (of_ref[...] * (zf * _sigmoid(zf))).astype(BF16)

    return pl.pallas_call(
        body, name="gate_fwd", grid=(lp // BLK,),
        in_specs=[_row(D_MODEL), _row(D_MODEL), _rowc(D_MODEL, 0), _rowc(D_MODEL, 1)],
        out_specs=[_row(D_MODEL), _row(D_MODEL)],
        out_shape=[jax.ShapeDtypeStruct((lp, D_MODEL), BF16)] * 2,
        compiler_params=_cp(("parallel",)))(o_mla, o_fox, gate, gate)


def _merge_fwd(gate, y_mla, y_fox):
    lp = y_mla.shape[0]

    def body(ga_ref, gb_ref, ym_ref, yf_ref, m_ref):
        sa = _sigmoid(ga_ref[...].astype(F32))
        sb = _sigmoid(gb_ref[...].astype(F32))
        m_ref[...] = (sa * ym_ref[...] + sb * yf_ref[...]).astype(BF16)

    return pl.pallas_call(
        body, name="merge_fwd", grid=(lp // BLK,),
        in_specs=[_rowc(D_MODEL, 2), _rowc(D_MODEL, 3), _row(D_MODEL), _row(D_MODEL)],
        out_specs=_row(D_MODEL), out_shape=jax.ShapeDtypeStruct((lp, D_MODEL), BF16),
        compiler_params=_cp(("parallel",)))(gate, gate, y_mla, y_fox)


def _tail(x2, mixed, tgt, gpost):
    lp = mixed.shape[0]
    shift = _shift_rows(D_MODEL)

    def body(h_ref, mx_ref, t_ref, g_ref, dmx_ref, dy_ref, loss_ref, dg_ref):
        i = pl.program_id(0)

        @pl.when(i == 0)
        def _():
            loss_ref[...] = jnp.zeros_like(loss_ref)
            dg_ref[...] = jnp.zeros_like(dg_ref)
            dmx_ref[...] = jnp.zeros_like(dmx_ref)
            dy_ref[...] = jnp.zeros_like(dy_ref)

        @pl.when(i > 0)
        def _():
            mx = mx_ref[...]
            g = g_ref[...]
            r = lax.rsqrt(jnp.mean(mx * mx, axis=-1, keepdims=True) + RMS_EPS)
            nrm = mx * r
            e = (h_ref[...] + nrm * g) - t_ref[...]
            loss_ref[...] += jnp.sum(0.5 * jnp.sum(e * e, axis=-1, keepdims=True) * (1.0 / D_MODEL),
                                     axis=0, keepdims=True)
            dy = e * (1.0 / D_MODEL)
            dy_ref[...] = dy
            dg_ref[...] += jnp.sum(dy * nrm, axis=0, keepdims=True)
            w = dy * g
            dot = jnp.mean(w * mx, axis=-1, keepdims=True)
            dmx_ref[...] = (r * w - mx * (r * r * r * dot)).astype(BF16)

    return pl.pallas_call(
        body, name="tail", grid=(lp // BLK,),
        in_specs=[shift, _row(D_MODEL), shift, _full((1, D_MODEL))],
        out_specs=[_row(D_MODEL), _row(D_MODEL), _full((1, 1)), _full((1, D_MODEL))],
        out_shape=[jax.ShapeDtypeStruct((lp, D_MODEL), BF16), jax.ShapeDtypeStruct((lp, D_MODEL), F32),
                   jax.ShapeDtypeStruct((1, 1), F32), jax.ShapeDtypeStruct((1, D_MODEL), F32)],
        compiler_params=_cp(("arbitrary",)))(x2, mixed, tgt, gpost)


def _merge_bwd(dm, gate, y_mla, y_fox):
    lp = dm.shape[0]

    def body(dm_ref, ga_ref, gb_ref, ym_ref, yf_ref, dym_ref, dyf_ref, dg_ref):
        dm_v = dm_ref[...].astype(F32)
        sa = _sigmoid(ga_ref[...].astype(F32))
        sb = _sigmoid(gb_ref[...].astype(F32))
        dym_ref[...] = (dm_v * sa).astype(BF16)
        dyf_ref[...] = (dm_v * sb).astype(BF16)
        dg_ref[:, 0:D_MODEL] = (dm_v * ym_ref[...] * (sa * (1.0 - sa))).astype(BF16)
        dg_ref[:, D_MODEL:2 * D_MODEL] = (dm_v * yf_ref[...] * (sb * (1.0 - sb))).astype(BF16)

    return pl.pallas_call(
        body, name="merge_bwd", grid=(lp // BLK,),
        in_specs=[_row(D_MODEL), _rowc(D_MODEL, 2), _rowc(D_MODEL, 3), _row(D_MODEL), _row(D_MODEL)],
        out_specs=[_row(D_MODEL), _row(D_MODEL), _row(2 * D_MODEL)],
        out_shape=[jax.ShapeDtypeStruct((lp, D_MODEL), BF16), jax.ShapeDtypeStruct((lp, D_MODEL), BF16),
                   jax.ShapeDtypeStruct((lp, 2 * D_MODEL), BF16)],
        compiler_params=_cp(("parallel",)))(dm, gate, gate, y_mla, y_fox)


def _gate_bwd(da_mla, da_fox, o_mla, o_fox, gate):
    lp = da_mla.shape[0]

    def one(da, o, z, head_of_col):
        sg = _sigmoid(z)
        do = (da * (z * sg)).astype(BF16)
        dz = da * o * (sg * (1.0 + z * (1.0 - sg)))
        delta = sum(_dot(part, head_of_col, 1, 0) for part in _split3(do.astype(F32) * o))
        return do, dz.astype(BF16), delta

    def body(dam_ref, daf_ref, om_ref, of_ref, zm_ref, zf_ref, dom_ref, dof_ref, dz_ref, dlm_ref, dlf_ref):
        f32 = lambda r: r[...].astype(F32)
        head_of_col = (lax.broadcasted_iota(jnp.int32, (D_MODEL, LANES), 0) // HEAD_DIM
                       == lax.broadcasted_iota(jnp.int32, (D_MODEL, LANES), 1)).astype(BF16)
        dom_ref[...], dz_ref[:, 0:D_MODEL], dlm_ref[...] = one(f32(dam_ref), f32(om_ref), f32(zm_ref), head_of_col)
        dof_ref[...], dz_ref[:, D_MODEL:2 * D_MODEL], dlf_ref[...] = one(f32(daf_ref), f32(of_ref), f32(zf_ref),
                                                                        head_of_col)

    return pl.pallas_call(
        body, name="gate_bwd", grid=(lp // BLK,),
        in_specs=[_row(D_MODEL)] * 4 + [_rowc(D_MODEL, 0), _rowc(D_MODEL, 1)],
        out_specs=[_row(D_MODEL), _row(D_MODEL), _row(2 * D_MODEL), _row(LANES), _row(LANES)],
        out_shape=[jax.ShapeDtypeStruct((lp, D_MODEL), BF16), jax.ShapeDtypeStruct((lp, D_MODEL), BF16),
                   jax.ShapeDtypeStruct((lp, 2 * D_MODEL), BF16), jax.ShapeDtypeStruct((lp, LANES), F32),
                   jax.ShapeDtypeStruct((lp, LANES), F32)],
        compiler_params=_cp(("parallel",)))(da_mla, da_fox, o_mla, o_fox, gate, gate)


def _small_bwd(small, dqn, dkvn, dkr, dcol_t, drow_t, gq, gkv, fb, ctab, stab, triu):
    lp = small.shape[0]
    nb = lp // BLK

    def rrow(w):
        return pl.BlockSpec((BLK, w), lambda i: (nb - 1 - i, 0))

    def body(sm_ref, dqn_ref, dkvn_ref, dkr_ref, dcol_ref, drow_ref, gq_ref, gkv_ref, fb_ref, c_ref, s_ref, tri_ref,
             ds_ref, dgq_ref, dgkv_ref, dfb_ref, carry):
        i = pl.program_id(0)

        @pl.when(i == 0)
        def _():
            carry[...] = jnp.zeros_like(carry)
            dgq_ref[...] = jnp.zeros_like(dgq_ref)
            dgkv_ref[...] = jnp.zeros_like(dgkv_ref)
            dfb_ref[...] = jnp.zeros_like(dfb_ref)

        def norm_bwd(x, dn, g, dg_ref):
            r = lax.rsqrt(jnp.mean(x * x, axis=-1, keepdims=True) + RMS_EPS)
            dg_ref[...] += jnp.sum(dn * (x * r), axis=0, keepdims=True)
            w = dn * g
            dot = jnp.mean(w * x, axis=-1, keepdims=True)
            return r * w - x * (r * r * r * dot)

        ds_ref[:, 0:256] = norm_bwd(sm_ref[:, 0:256], dqn_ref[...], gq_ref[...], dgq_ref).astype(BF16)
        ds_ref[:, 256:384] = norm_bwd(sm_ref[:, 256:384], dkvn_ref[...], gkv_ref[...], dgkv_ref).astype(BF16)

        dk = dkr_ref[0]
        for p in range(1, PAIRS):
            dk = dk + dkr_ref[p]
        dk = _rope(dk, c_ref[...], -s_ref[...])
        lane = lax.broadcasted_iota(jnp.int32, dk.shape, 1)
        dk = jnp.where(lane < MLA_ROPE, dk + pltpu.roll(dk, LANES - MLA_ROPE, 1), 0.0)
        ds_ref[:, 384:512] = dk.astype(BF16)

        dcol = dcol_ref[0]
        for p in range(1, PAIRS):
            dcol = dcol + pltpu.roll(dcol_ref[p], 2 * p, 1)
        rows16 = jnp.concatenate([drow_ref[p, h:h + 1, :] for p in range(PAIRS) for h in range(2)], axis=0)
        eye = (lax.broadcasted_iota(jnp.int32, (HEADS, LANES), 0)
               == lax.broadcasted_iota(jnp.int32, (HEADS, LANES), 1)).astype(BF16)
        drow = sum(_dot(part, eye, 0, 0) for part in _split3(rows16))
        dcr = dcol - drow
        hi, mid, lo = _split3(dcr)
        t = tri_ref[...]
        suf = (_dot(t, hi, 1, 0) + _dot(t, mid, 1, 0)) + _dot(t, lo, 1, 0) + carry[...]
        fl = sm_ref[:, 512:640] + fb_ref[...]
        dfl = jnp.where(_row_valid(nb - 1 - i), -suf * _sigmoid(-fl), 0.0)
        ds_ref[:, 512:640] = dfl.astype(BF16)
        dfb_ref[...] += jnp.sum(dfl, axis=0, keepdims=True)
        carry[...] += jnp.sum(dcr, axis=0, keepdims=True)

    return pl.pallas_call(
        body, name="small_bwd", grid=(nb,),
        in_specs=[rrow(SMALL_W), rrow(256), rrow(128),
                  pl.BlockSpec((PAIRS, BLK, 128), lambda i: (0, nb - 1 - i, 0)),
                  pl.BlockSpec((PAIRS, BLK, 128), lambda i: (0, nb - 1 - i, 0)),
                  pl.BlockSpec((PAIRS, 2, BLK), lambda i: (0, 0, nb - 1 - i)),
                  _full((1, 256)), _full((1, 128)), _full((1, 128)), rrow(128), rrow(128), _full((BLK, BLK))],
        out_specs=[rrow(SMALL_W), _full((1, 256)), _full((1, 128)), _full((1, 128))],
        out_shape=[jax.ShapeDtypeStruct((lp, SMALL_W), BF16), jax.ShapeDtypeStruct((1, 256), F32),
                   jax.ShapeDtypeStruct((1, 128), F32), jax.ShapeDtypeStruct((1, 128), F32)],
        scratch_shapes=[pltpu.VMEM((1, 128), F32)],
        compiler_params=_cp(("arbitrary",)))(small, dqn, dkvn, dkr, dcol_t, drow_t, gq, gkv, fb, ctab, stab, triu)


def _pre_bwd(du, x2, meta, dy, gpre):
    s_rows = x2.shape[0]
    lp = PAD + s_rows
    shift = _shift_rows(D_MODEL)

    def body(du_ref, x_ref, meta_ref, dy_ref, g_ref, dx_ref, dmeta_ref, dg_ref):
        i = pl.program_id(0)

        @pl.when(i == 0)
        def _():
            dg_ref[...] = jnp.zeros_like(dg_ref)

        hv = _h_block(i, x_ref, meta_ref)
        duv = du_ref[...]
        r = lax.rsqrt(jnp.mean(hv * hv, axis=-1, keepdims=True) + RMS_EPS)
        dg_ref[...] += jnp.sum(duv * (hv * r), axis=0, keepdims=True)
        w = duv * g_ref[...]
        dot = jnp.mean(w * hv, axis=-1, keepdims=True)
        dh = dy_ref[...] + (r * w - hv * (r * r * r * dot))
        dx_ref[...] = dh

        @pl.when(i == 0)
        def _():
            dmeta_ref[...] = dh[0:N_META, :]

    return pl.pallas_call(
        body, name="pre_bwd", grid=(lp // BLK,),
        in_specs=[_row(D_MODEL), shift, _full((N_META, D_MODEL)), _row(D_MODEL), _full((1, D_MODEL))],
        out_specs=[shift, _full((N_META, D_MODEL)), _full((1, D_MODEL))],
        out_shape=[jax.ShapeDtypeStruct((s_rows, D_MODEL), F32), jax.ShapeDtypeStruct((N_META, D_MODEL), F32),
                   jax.ShapeDtypeStruct((1, D_MODEL), F32)],
        compiler_params=_cp(("arbitrary",)))(du, x2, meta, dy, gpre)


def _pair_masks(rope):
    lane = lax.broadcasted_iota(jnp.int32, (1, LANES), 1)
    mas = [lane < HEAD_DIM, lane >= HEAD_DIM]
    if not rope:
        return mas, mas
    wide = lax.broadcasted_iota(jnp.int32, (1, 2 * LANES), 1)
    rope_lo = LANES + MLA_ROPE
    return mas, [(wide < HEAD_DIM) | ((wide >= LANES) & (wide < rope_lo)),
                 ((wide >= HEAD_DIM) & (wide < LANES)) | ((wide >= rope_lo) & (wide < rope_lo + MLA_ROPE))]


def _mask2(x, masks):
    return [jnp.where(m, x, jnp.zeros_like(x)) for m in masks]


def _attn_fwd(q, k, v, *, kr=None, nbrep=None, scale, qcol, kcol, vcol, name):
    lp = q.shape[0]
    nq = 1 + (lp - PAD) // QB
    rope = kr is not None
    bias = nbrep is not None
    qw = 256 if rope else 128
    exp2_c = scale * math.log2(math.e)

    def body(*refs):
        it = iter(refs)
        q_ref, k_ref, v_ref = next(it), next(it), next(it)
        kr_ref = next(it) if rope else None
        nb_ref = next(it) if bias else None
        o_ref, lse_ref = next(it), next(it)
        i = pl.program_id(1)
        r0 = pl.multiple_of(jnp.where(i == 0, 0, PAD + QB * (i - 1)), BLK)
        b0 = r0 // BLK
        mas, hmask = _pair_masks(rope)
        qh = _mask2(q_ref[pl.ds(r0, QB), :], hmask)
        if bias:
            qh = [x * scale for x in qh]

        def update(kcs, carry, masks, ns=None, q_lo=0, wq=QB):
            ns = ns or [BLK] * len(kcs)
            stats, acc = carry[:4], carry[4]
            qs = [x[q_lo:q_lo + wq] for x in qh]
            k0s = [pl.multiple_of(kc * BLK, BLK) for kc in kcs]
            kks = [k_ref[pl.ds(k0, n), :] for k0, n in zip(k0s, ns)]
            if rope:
                kks = [jnp.concatenate([kk, kr_ref[pl.ds(k0, n), :]], axis=1) for kk, k0, n in zip(kks, k0s, ns)]
            new_stats, alphas, ps = [], [], [[] for _ in kcs]
            for h in range(2):
                m_prev, l_prev = stats[2 * h], stats[2 * h + 1]
                ss = []
                for kk, k0, n, mask in zip(kks, k0s, ns, masks):
                    s = _dot(kk, qs[h], 1, 1)
                    if bias:
                        nbc = nb_ref[h, pl.ds(k0, n), :]
                        s = s + jnp.concatenate([nbc] * (wq // LANES), axis=1)
                    if mask is not None:
                        s = jnp.where(mask, s, NEG)
                    ss.append(s)
                m_new = m_prev
                for s in ss:
                    m_new = jnp.maximum(m_new, jnp.max(s, axis=0, keepdims=True))
                alpha = jnp.exp2((m_prev - m_new) * exp2_c) if rope else jnp.exp(m_prev - m_new)
                l_new = alpha * l_prev
                for j, s in enumerate(ss):
                    p = jnp.exp2((s - m_new) * exp2_c) if rope else jnp.exp(s - m_new)
                    l_new = l_new + jnp.sum(p, axis=0, keepdims=True)
                    ps[j].append(p.astype(BF16))
                new_stats += [m_new, l_new]
                alphas.append(alpha)
            vcat = jnp.concatenate([x for k0, n in zip(k0s, ns) for x in _mask2(v_ref[pl.ds(k0, n), :], mas)], axis=0)
            pv = _dot(vcat, jnp.concatenate([p for pj in ps for p in pj], axis=0), 0, 0)
            a_full = jnp.concatenate([jnp.broadcast_to(a, (HEAD_DIM, wq)) for a in alphas], axis=0)
            return (*new_stats, a_full * acc + pv)

        neg = jnp.full((1, QB), NEG, F32)
        zero = jnp.zeros((1, QB), F32)
        c = (neg, zero, neg, zero, jnp.zeros((LANES, QB), F32))
        n_mid = jnp.maximum(b0 - 1, 0)
        c = lax.fori_loop(0, n_mid // 4, lambda t, cr: update([4 * t + u for u in (1, 2, 3, 4)], cr, [None] * 4), c)
        c = lax.fori_loop(0, (n_mid % 4) // 2, lambda t, cr: update([n_mid - 1, n_mid], cr, [None, None]), c)
        key_l = lax.broadcasted_iota(jnp.int32, (BLK, BLK), 0)
        qry_l = lax.broadcasted_iota(jnp.int32, (BLK, BLK), 1)
        tri = (key_l <= qry_l) & (b0 > 0)
        meta_ok = (key_l[0:N_META] <= qry_l[0:N_META]) | (b0 > 0)
        lo = update([0, b0], tuple(a[:, 0:BLK] for a in c), [meta_ok, tri], ns=[N_META, BLK], q_lo=0, wq=BLK)
        hi = update([0, b0, b0 + 1], tuple(a[:, BLK:QB] for a in c), [None, None, tri], ns=[N_META, BLK, BLK],
                    q_lo=BLK, wq=QB - BLK)
        c = tuple(jnp.concatenate([a, b], axis=1) for a, b in zip(lo, hi))
        inv =jnp.concatenate([jnp.broadcast_to(1.0 / c[1], (HEAD_DIM, QB)),
                               jnp.broadcast_to(1.0 / c[3], (HEAD_DIM, QB))], axis=0)
        o_t = (c[4] * inv).T.astype(BF16)
        lses = [(c[2 * h] * scale if rope else c[2 * h]) + jnp.log(c[2 * h + 1]) for h in range(2)]
        o_ref[pl.ds(r0, BLK), :] = o_t[0:BLK]
        for h in range(2):
            lse_ref[0, h:h + 1, pl.ds(r0, BLK)] = lses[h][:, 0:BLK]

        @pl.when(i > 0)
        def _():
            r1 = pl.multiple_of(r0 + BLK, BLK)
            o_ref[pl.ds(r1, QB - BLK), :] = o_t[BLK:QB]
            for h in range(2):
                lse_ref[0, h:h + 1, pl.ds(r1, QB - BLK)] = lses[h][:, BLK:QB]

    in_specs = [pl.BlockSpec((lp, qw), lambda p, i: (0, qcol + p)),
                pl.BlockSpec((lp, 128), lambda p, i: (0, kcol(p))),
                pl.BlockSpec((lp, 128), lambda p, i: (0, vcol(p)))]
    ins = [q, k, v]
    if rope:
        in_specs.append(pl.BlockSpec((lp, 128), lambda p, i: (0, 0)))
        ins.append(kr)
    if bias:
        in_specs.append(pl.BlockSpec((2, lp, 128), lambda p, i: (p, 0, 0)))
        ins.append(nbrep)
    return pl.pallas_call(
        body, name=name, grid=(PAIRS, nq), in_specs=in_specs,
        out_specs=[pl.BlockSpec((lp, 128), lambda p, i: (0, p)),
                   pl.BlockSpec((1, 2, lp), lambda p, i: (p, 0, 0))],
        out_shape=[jax.ShapeDtypeStruct((lp, D_MODEL), BF16), jax.ShapeDtypeStruct((PAIRS, 2, lp), F32)],
        compiler_params=_cp(("parallel", "arbitrary"), VMEM_BIG))(*ins)


def _attn_bwd(q, k, v, do, delta, lse, *, kr=None, rtabs=None, nbrep=None, scale, qcol, kcol, vcol, name):
    lp = q.shape[0]
    nb = lp // BLK
    rope = kr is not None
    bias = nbrep is not None
    qw = 256 if rope else 128

    def body(*refs):
        it = iter(refs)
        q_ref, k_ref, v_ref = next(it), next(it), next(it)
        kr_ref = next(it) if rope else None
        nb_ref = next(it) if bias else None
        do_ref, dl_ref, lse_ref = next(it), next(it), next(it)
        ct_ref, st_ref = (next(it), next(it)) if rope else (None, None)
        dq_out, dk_ref, dv_ref = next(it), next(it), next(it)
        x_ref = next(it)
        drow_ref = next(it) if bias else None
        dq_ref = next(it)
        kb = pl.program_id(1)
        mas, hmask = _pair_masks(rope)
        lane = lax.broadcasted_iota(jnp.int32, (1, LANES), 1)

        @pl.when(kb == 0)
        def _():
            dq_ref[...] = jnp.zeros_like(dq_ref)
            if bias:
                drow_ref[...] = jnp.zeros_like(drow_ref)

        def key_pass(n, w):
            kk = k_ref[0:n, :]
            if rope:
                kk = jnp.concatenate([kk, kr_ref[0:n, :]], axis=1)
            vh = _mask2(v_ref[0:n, :], mas)
            kcat = jnp.concatenate(_mask2(kk, hmask), axis=0)
            if bias:
                kcat = kcat * scale
            diag_mask = (lax.broadcasted_iota(jnp.int32, (n, w), 0) <= lax.broadcasted_iota(jnp.int32, (n, w), 1))

            def chunk(qc, carry, mask):
                carry = list(carry)
                q0 = qc * w if isinstance(qc, int) else pl.multiple_of(qc * w, w)
                dov = do_ref[pl.ds(q0, w), :]
                doh = _mask2(dov, mas)
                qh = _mask2(q_ref[pl.ds(q0, w), :], hmask)
                if bias:
                    qh = [x * scale for x in qh]
                pbs, dss = [], []
                for h in range(2):
                    s = _dot(kk, qh[h], 1, 1)
                    if rope:
                        s = s * scale
                    if bias:
                        s = s + jnp.concatenate([nb_ref[h, 0:n, :]] * (w // LANES), axis=1)
                    p = jnp.exp(s - lse_ref[0, h:h + 1, pl.ds(q0, w)])
                    if mask is not None:
                        p = jnp.where(mask, p, 0.0)
                    ds = p * (_dot(vh[h], dov, 1, 1) - dl_ref[0, h:h + 1, pl.ds(q0, w)])
                    if bias:
                        drow_ref[0, h:h + 1, pl.ds(q0, w)] += jnp.sum(ds, axis=0, keepdims=True)
                        carry[2 + h] = carry[2 + h] + jnp.sum(ds, axis=1, keepdims=True)
                    else:
                        ds = ds * scale
                    pbs.append(p.astype(BF16))
                    dss.append(ds.astype(BF16))
                ds_lanes = jnp.concatenate(dss, axis=1)
                ds_rows = jnp.concatenate(dss, axis=0)
                carry[0] = carry[0] + _dot(ds_lanes, jnp.concatenate(qh, axis=0), 1, 0)
                carry[1] = carry[1] + _dot(jnp.concatenate(pbs, axis=1), jnp.concatenate(doh, axis=0), 1, 0)
                dq_ref[pl.ds(q0, w), :] += _dot(ds_rows, kcat, 0, 0)
                return tuple(carry)

            c = [jnp.zeros((n, qw), F32), jnp.zeros((n, LANES), F32)]
            if bias:
                c += [jnp.zeros((n, 1), F32), jnp.zeros((n, 1), F32)]
            c = tuple(c)
            if w != BLK:
                for qc in range(lp // w):
                    c = chunk(qc, c, diag_mask if qc == 0 else None)
            else:
                groups = (nb - kb) // UNROLL

                def several(t, cr):
                    for u in range(UNROLL):
                        cr = chunk(kb + UNROLL * t + u, cr, (diag_mask | (t > 0)) if u == 0 else None)
                    return cr

                c = lax.fori_loop(0, groups, several, c)
                start = kb + UNROLL * groups
                pairs = (nb - start) // 2

                def two(t, cr):
                    qc = start + 2 * t
                    return chunk(qc + 1, chunk(qc, cr, diag_mask | (qc > kb)), None)

                c = lax.fori_loop(0, pairs, two, c)
                c = lax.fori_loop(start + 2 * pairs, nb, lambda qc, cr: chunk(qc, cr, diag_mask | (qc > kb)), c)

            def rows(a, dtype):
                a = a.astype(dtype)
                return a if n == BLK else jnp.concatenate([a, jnp.zeros((BLK - n, a.shape[1]), dtype)], axis=0)

            dk_ref[...] = rows(c[0][:, 0:LANES], BF16)
            dv_ref[...] = rows(c[1], BF16)
            if rope:
                x_ref[0] = rows(c[0][:, LANES:2 * LANES], F32)
            if bias:
                x_ref[0] = rows(jnp.where(lane == 0, c[2], jnp.where(lane == 1, c[3], 0.0)), F32)

        @pl.when(kb == 0)
        def _():
            key_pass(N_META, lp // 2)

        @pl.when(kb > 0)
        def _():
            key_pass(BLK, BLK)

        @pl.when(kb == nb - 1)
        def _():
            def fin(c, carry):
                r0 = pl.multiple_of(c * BLK, BLK)
                dq = dq_ref[pl.ds(r0, BLK), :]
                if rope:
                    back = _rope(dq[:, LANES:2 * LANES], ct_ref[pl.ds(r0, BLK), :], -st_ref[pl.ds(r0, BLK), :])
                    dq = jnp.concatenate([dq[:, 0:LANES], back], axis=1)
                dq_out[pl.ds(r0, BLK), :] = dq.astype(BF16)
                return carry

            lax.fori_loop(0, nb, fin, 0)

    in_specs = [pl.BlockSpec((lp, qw), lambda p, j: (0, qcol + p)),
                pl.BlockSpec((BLK, 128), lambda p, j: (j, kcol(p))),
                pl.BlockSpec((BLK, 128), lambda p, j: (j, vcol(p)))]
    ins = [q, k, v]
    if rope:
        in_specs.append(pl.BlockSpec((BLK, 128), lambda p, j: (j, 0)))
        ins.append(kr)
    if bias:
        in_specs.append(pl.BlockSpec((2, BLK, 128), lambda p, j: (p, j, 0)))
        ins.append(nbrep)
    in_specs += [pl.BlockSpec((lp, 128), lambda p, j: (0, p)), pl.BlockSpec((1, 2, lp), lambda p, j: (p, 0, 0)),
                 pl.BlockSpec((1, 2, lp), lambda p, j: (p, 0, 0))]
    ins += [do, delta, lse]
    if rope:
        in_specs += [pl.BlockSpec((lp, 128), lambda p, j: (0, 0))] * 2
        ins += list(rtabs)
    out_specs = [pl.BlockSpec((lp, qw), lambda p, j: (0, p)),
                 pl.BlockSpec((BLK, 128), lambda p, j: (j, p)),
                 pl.BlockSpec((BLK, 128), lambda p, j: (j, p)),
                 pl.BlockSpec((1, BLK, 128), lambda p, j: (p, j, 0))]
    out_shape = [jax.ShapeDtypeStruct((lp, PAIRS * qw), BF16), jax.ShapeDtypeStruct((lp, D_MODEL), BF16),
                 jax.ShapeDtypeStruct((lp, D_MODEL), BF16), jax.ShapeDtypeStruct((PAIRS, lp, 128), F32)]
    if bias:
        out_specs.append(pl.BlockSpec((1, 2, lp), lambda p, j: (p, 0, 0)))
        out_shape.append(jax.ShapeDtypeStruct((PAIRS, 2, lp), F32))
    return pl.pallas_call(
        body, name=name, grid=(PAIRS, nb), in_specs=in_specs, out_specs=out_specs, out_shape=out_shape,
        scratch_shapes=[pltpu.VMEM((lp, qw), F32)],
        compiler_params=_cp(("parallel", "arbitrary"), VMEM_BIG))(*ins)


def _adamw(w, g, m, v, name):
    lead = w.ndim - 2
    rows, cols = w.shape[lead:]
    big = rows * cols > 512 * 1024
    tr = 128 if big and rows % 128 == 0 else rows
    tc = 256 if big and tr == rows else cols

    def body(w_ref, g_ref, m_ref, v_ref, d_ref, nm_ref, nv_ref):
        gv = g_ref[...]
        nm = ADAM_B1 * m_ref[...] + (1.0 - ADAM_B1) * gv
        nv = ADAM_B2 * v_ref[...] + (1.0 - ADAM_B2) * (gv * gv)
        m_hat = nm / (1.0 - ADAM_B1 ** ADAM_STEP)
        v_hat = nv / (1.0 - ADAM_B2 ** ADAM_STEP)
        d_ref[...] = -ADAM_LR * (m_hat / (jnp.sqrt(v_hat) + ADAM_EPS) + ADAM_WD * w_ref[...])
        nm_ref[...] = nm
        nv_ref[...] = nv

    spec = pl.BlockSpec((1,) * lead + (tr, tc), lambda i, j: (0,) * lead + (i, j))
    return pl.pallas_call(
        body, name=name, grid=(rows // tr, cols // tc), in_specs=[spec] * 4, out_specs=[spec] * 3,
        out_shape=[jax.ShapeDtypeStruct(w.shape, F32)] * 3,
        compiler_params=_cp(("parallel", "parallel"), VMEM_BIG))(w, g, m, v)


def _add_cores(g, from_sib, name):
    n, rows, cols = g.shape
    half = rows // 2
    tr = _tile(half, (256, 240))
    nt = half // tr

    def body(lo_ref, hi_ref, s_ref, o_ref):
        mine = jnp.where(lax.axis_index("c") == 0, lo_ref[0], hi_ref[0])
        o_ref[0] = (mine.astype(F32) + s_ref[0].astype(F32)).astype(BF16)

    return pl.pallas_call(
        body, name=name, grid=(n, nt),
        in_specs=[pl.BlockSpec((1, tr, cols), lambda j, i: (j, i, 0)),
                  pl.BlockSpec((1, tr, cols), lambda j, i: (j, nt + i, 0)),
                  pl.BlockSpec((1, tr, cols), lambda j, i: (j, i, 0))],
        out_specs=pl.BlockSpec((1, tr, cols), lambda j, i: (j, i, 0)),
        out_shape=jax.ShapeDtypeStruct((n, half, cols), BF16),
        compiler_params=_cp(("parallel", "parallel"), VMEM_BIG))(g, g, from_sib)


def _add_chips(x, own, name):
    n, rows, cols = x.shape
    tr = _tile(rows, (256, 240))

    def body(x_ref, own_ref, o_ref):
        me = 2 * lax.axis_index("x") + lax.axis_index("y")
        v = [jnp.where(me == k, own_ref[...], x_ref[k]).astype(F32) for k in range(N_CHIPS)]
        o_ref[...] = ((v[0] + v[1]) + v[2]) + v[3]

    return pl.pallas_call(
        body, name=name, grid=(rows // tr,),
        in_specs=[pl.BlockSpec((n, tr, cols), lambda i: (0, i, 0)), pl.BlockSpec((tr, cols), lambda i: (i, 0))],
        out_specs=pl.BlockSpec((tr, cols), lambda i: (i, 0)),
        out_shape=jax.ShapeDtypeStruct((rows, cols), F32), compiler_params=_cp(("parallel",), VMEM_BIG))(x, own)


def _axes():
    return lax.axis_index("x"), lax.axis_index("y"), lax.axis_index("c")


def _other_chips(x, y):
    return [(1 - x, y), (x, 1 - y), (1 - x, 1 - y)]


ANY = pl.BlockSpec(memory_space=pl.ANY)


def _rcopy(src, dst, send_sems, recv_sems, k, to):
    return pltpu.make_async_remote_copy(src_ref=src, dst_ref=dst, send_sem=send_sems.at[k], recv_sem=recv_sems.at[k],
                                        device_id=to, device_id_type=MESH)


def _gather_weights(shards, meta):
    n = len(shards)

    def body(*refs):
        srcs, meta_ref = refs[:n], refs[n]
        outs, mout_ref = refs[n + 1:2 * n + 1], refs[2 * n + 1]
        send_sems, recv_sems = refs[2 * n + 2:]
        x, y, c = _axes()
        me = 2 * x + y
        sib = (x, y, 1 - c)
        chips = _other_chips(x, y)

        def half(t, chip_idx, cc):
            hr = shards[t].shape[0] // 2
            return outs[t].at[chip_idx, pl.ds(cc * hr, hr), :]

        first = []
        for j, (px, py) in enumerate(chips):
            for t in range(n):
                hr = shards[t].shape[0] // 2
                first.append(_rcopy(srcs[t].at[pl.ds(c * hr, hr), :], half(t, me, c), send_sems, recv_sems,
                                    3 * t + j, (px, py, c)))
            first.append(_rcopy(meta_ref, mout_ref.at[me], send_sems, recv_sems, 3 * n + j, (px, py, c)))
        for cp in first:
            cp.start()
        passed = []
        for j, (px, py) in enumerate(chips):
            src_chip = 2 * px + py
            for t in range(n):
                _rcopy(half(t, src_chip, c), half(t, src_chip, c), send_sems, recv_sems, 3 * t + j, sib).wait_recv()
                fwd = _rcopy(half(t, src_chip, c), half(t, src_chip, c), send_sems, recv_sems, 3 * (n + 1 + t) + j, sib)
                fwd.start()
                passed.append(fwd)
            _rcopy(mout_ref.at[src_chip], mout_ref.at[src_chip], send_sems, recv_sems, 3 * n + j, sib).wait_recv()
        for j, (px, py) in enumerate(chips):
            src_chip = 2 * px + py
            for t in range(n):
                _rcopy(half(t, src_chip, 1 - c), half(t, src_chip, 1 - c), send_sems, recv_sems,
                       3 * (n + 1 + t) + j, sib).wait_recv()
        for cp in first + passed:
            cp.wait_send()

    nsem = 3 * (2 * n + 1)
    return pl.pallas_call(
        body, name="gather_weights", in_specs=[ANY] * (n + 1), out_specs=[ANY] * (n + 1),
        out_shape=[jax.ShapeDtypeStruct((N_CHIPS,) + s.shape, s.dtype) for s in shards]
        + [jax.ShapeDtypeStruct((N_CHIPS,) + meta.shape, meta.dtype)],
        scratch_shapes=[pltpu.SemaphoreType.DMA((nsem,)), pltpu.SemaphoreType.DMA((nsem,))])(*shards, meta)


def _gather_late(shard):
    rows, cols = shard.shape
    hr = rows // 2
    src = jax.new_ref(shard, memory_space=pltpu.MemorySpace.HBM)
    out = jax.empty_ref(jax.ShapeDtypeStruct((N_CHIPS, rows, cols), shard.dtype), memory_space=pltpu.MemorySpace.HBM)

    @pl.kernel(mesh=plsc.ScalarSubcoreMesh(axis_name="seq", num_cores=1), name="gather_late",
               scratch_types=(pltpu.SemaphoreType.DMA((6,)), pltpu.SemaphoreType.DMA((6,))),
               compiler_params=pltpu.CompilerParams(collective_id=1))
    def launch(send_sems, recv_sems):
        x, y, c = _axes()
        me = 2 * x + y
        sib = (x, y, 1 - c)
        chips = _other_chips(x, y)
        barrier = pltpu.get_barrier_semaphore()
        for px, py in chips:
            pl.semaphore_signal(barrier, inc=1, device_id=(px, py, c), device_id_type=MESH)
        pl.semaphore_signal(barrier, inc=1, device_id=sib, device_id_type=MESH)
        pl.semaphore_wait(barrier, 4)

        def half(chip_idx, cc):
            return out.at[chip_idx, pl.ds(cc * hr, hr), :]

        first = [_rcopy(src.at[pl.ds(c * hr, hr), :], half(me, c), send_sems, recv_sems, j, (px, py, c))
                 for j, (px, py) in enumerate(chips)]
        for cp in first:
            cp.start()
        passed = []
        for j, (px, py) in enumerate(chips):
            land = half(2 * px + py, c)
            _rcopy(land, land, send_sems, recv_sems, j, sib).wait_recv()
            fwd = _rcopy(land, land, send_sems, recv_sems, 3 + j, sib)
            fwd.start()
            passed.append(fwd)
        for j, (px, py) in enumerate(chips):
            land = half(2 * px + py, 1 - c)
            _rcopy(land, land, send_sems, recv_sems, 3 + j, sib).wait_recv()
        for cp in first + passed:
            cp.wait_send()

    launch()
    return out[...]


def _swap_halves(gs):
    n = len(gs)
    ncopies = sum(g.shape[0] for g in gs)

    def body(*refs):
        srcs, outs = refs[:n], refs[n:2 * n]
        send_sems, recv_sems = refs[2 * n:]
        x, y, c = _axes()
        cps = []
        for t in range(n):
            hr = gs[t].shape[1] // 2
            for j in range(gs[t].shape[0]):
                cps.append(_rcopy(srcs[t].at[j, pl.ds((1 - c) * hr, hr), :], outs[t].at[j], send_sems, recv_sems,
                                  len(cps), (x, y, 1 - c)))
        for cp in cps:
            cp.start()
        for cp in cps:
            cp.wait()

    return pl.pallas_call(
        body, name="swap_halves", in_specs=[ANY] * n, out_specs=[ANY] * n,
        out_shape=[jax.ShapeDtypeStruct((g.shape[0], g.shape[1] // 2, g.shape[2]), g.dtype) for g in gs],
        scratch_shapes=[pltpu.SemaphoreType.DMA((ncopies,)), pltpu.SemaphoreType.DMA((ncopies,))])(*gs)


def _scatter_chips(parts):
    n = len(parts)
    srcs = [jax.new_ref(p, memory_space=pltpu.MemorySpace.HBM) for p in parts]
    outs = [jax.empty_ref(jax.ShapeDtypeStruct(p.shape, p.dtype), memory_space=pltpu.MemorySpace.HBM) for p in parts]

    @pl.kernel(mesh=plsc.ScalarSubcoreMesh(axis_name="seq", num_cores=1), name="scatter_chips",
               scratch_types=(pltpu.SemaphoreType.DMA((3 * n,)), pltpu.SemaphoreType.DMA((3 * n,))),
               compiler_params=pltpu.CompilerParams(collective_id=0))
    def launch(send_sems, recv_sems):
        x, y, c = _axes()
        me = 2 * x + y
        chips = _other_chips(x, y)
        barrier = pltpu.get_barrier_semaphore()
        for px, py in chips:
            pl.semaphore_signal(barrier, inc=1, device_id=(px, py, c), device_id_type=MESH)
        pl.semaphore_wait(barrier, 3)
        cps = []
        for j, (px, py) in enumerate(chips):
            for t in range(n):
                cps.append(_rcopy(srcs[t].at[2 * px + py], outs[t].at[me], send_sems, recv_sems, 3 * t + j,
                                  (px, py, c)))
        for cp in cps:
            cp.start()
        for cp in cps:
            cp.wait()

    launch()
    return [o[...] for o in outs]


def _swap_reduced(rs):
    n = len(rs)

    def body(*refs):
        srcs, outs = refs[:n], refs[n:2 * n]
        send_sems, recv_sems = refs[2 * n:]
        x, y, c = _axes()
        cps = [_rcopy(srcs[t], outs[t], send_sems, recv_sems, t, (x, y, 1 - c)) for t in range(n)]
        for cp in cps:
            cp.start()
        for cp in cps:
            cp.wait()

    return pl.pallas_call(
        body, name="swap_reduced", in_specs=[ANY] * n, out_specs=[ANY] * n,
        out_shape=[jax.ShapeDtypeStruct(r.shape, r.dtype) for r in rs],
        scratch_shapes=[pltpu.SemaphoreType.DMA((n,)), pltpu.SemaphoreType.DMA((n,))])(*rs)


SMALL_ROWS = 24 + 128


def _allreduce_small(vec):
    def body(v_ref, out_ref, slots, send_sems, recv_sems):
        x, y, c = _axes()
        me = 4 * x + 2 * y + c
        slots[me] = v_ref[...]
        cps = []
        for k in range(1, 8):
            kx, ky, kc = (k >> 2) & 1, (k >> 1) & 1, k & 1
            peer = (1 - x if kx else x, 1 - y if ky else y, 1 - c if kc else c)
            cps.append(_rcopy(v_ref, slots.at[me], send_sems, recv_sems, k - 1, peer))
        for cp in cps:
            cp.start()
        for cp in cps:
            cp.wait()
        tot = slots[0]
        for k in range(1, 8):
            tot = tot + slots[k]
        out_ref[...] = tot

    return pl.pallas_call(
        body, name="allreduce_small",
        in_specs=[pl.BlockSpec(memory_space=pltpu.VMEM)], out_specs=pl.BlockSpec(memory_space=pltpu.VMEM),
        out_shape=jax.ShapeDtypeStruct((SMALL_ROWS, 128), F32),
        scratch_shapes=[pltpu.VMEM((8, SMALL_ROWS, 128), F32), pltpu.SemaphoreType.DMA((7,)),
                        pltpu.SemaphoreType.DMA((7,))])(vec)


def _pack_p2(w_uq, w_ukv, w_br_mla, w_br_fox, w_out, dtype):
    parts = [w_uq.reshape(96, D_MODEL), w_ukv.reshape(64, D_MODEL), w_br_mla, w_br_fox, w_out]
    return jnp.concatenate([p.astype(dtype) for p in parts], axis=0)


def _unpack_p2(pk):
    return pk[0:96].reshape(256, 384), pk[96:160].reshape(128, 512), pk[160:416], pk[416:672], pk[672:928]


def _uq_arrange(w):
    w3 = w.reshape(256, HEADS, 96)
    nope = w3[:, :, :64].reshape(256, PAIRS, 128)
    pe = w3[:, :, 64:].reshape(256, PAIRS, 64)
    return jnp.concatenate([nope, pe, jnp.zeros((256, PAIRS, 64), w.dtype)], axis=2).reshape(256, PAIRS * 256)


def _uq_restore(g):
    g3 = g.reshape(256, PAIRS, 256)
    nope = g3[:, :, :128].reshape(256, HEADS, 64)
    pe = g3[:, :, 128:192].reshape(256, HEADS, 32)
    return jnp.concatenate([nope, pe], axis=2).reshape(256, HEADS * 96)


def _ukv_arrange(w):
    w3 = w.reshape(128, HEADS, 128)
    return jnp.concatenate([w3[:, :, :64].reshape(128, 1024), w3[:, :, 64:].reshape(128, 1024)], axis=1)


def _ukv_restore(g):
    kn = g[:, :1024].reshape(128, HEADS, 64)
    vv = g[:, 1024:].reshape(128, HEADS, 64)
    return jnp.concatenate([kn, vv], axis=2).reshape(128, HEADS * 128)


def _rope_tables(lp):
    r = np.arange(lp)
    pos = np.where(r < N_META, r, np.where(r >= PAD, r - PAD + N_META, 0)).astype(np.float32)
    half = MLA_ROPE // 2
    inv_freq = np.float32(ROPE_THETA) ** (-np.arange(half, dtype=np.float32) / np.float32(half))
    ang = (pos[:, None] * inv_freq[None, :]).astype(np.float32)
    cos, sin = np.cos(ang).astype(np.float32), np.sin(ang).astype(np.float32)
    one, zero = np.ones((lp, 64), np.float32), np.zeros((lp, 64), np.float32)
    return (jnp.asarray(np.concatenate([cos, cos, cos, cos, one], axis=1)),
            jnp.asarray(np.concatenate([-sin, sin, -sin, sin, zero], axis=1)))


def _pad_lanes(v, n=128):
    return jnp.pad(v, ((0, 0), (0, n - v.shape[1])))


def _in_cols(slabs, a, b):
    out = []
    for j in range(N_CHIPS):
        lo, hi = max(a, W_IN_SHARD * j), min(b, W_IN_SHARD * (j + 1))
        if lo < hi:
            out.append(slabs[j][:, lo - W_IN_SHARD * j:hi - W_IN_SHARD * j])
    return out


def _local_step(x2, tgt2, meta_f, w_small, w_attn, w_gate, w_uq_f, w_ukv_f, w_bm, w_bf, w_o, pre_norm_g,
                post_norm_g, mla_q_norm_g, mla_kv_norm_g, fox_forget_b, start_exchange=None):
    s_rows = x2.shape[0]
    lp = PAD + s_rows
    w_uq_a = _uq_arrange(w_uq_f)
    w_ukv_a = _ukv_arrange(w_ukv_f)

    ctab, stab = _rope_tables(lp)
    ii = jnp.arange(BLK)
    tri_lo = (ii[:, None] >= ii[None, :]).astype(BF16)
    tri_up = (ii[:, None] <= ii[None, :]).astype(BF16)
    fb128 = _pad_lanes(fox_forget_b)

    u = _rms_pre(x2, meta_f, pre_norm_g)
    small = _mm(u, w_small, mode="nn", out_dtype=F32, name="proj_small")
    attn = _mm(u, w_attn, mode="nn", out_dtype=BF16, name="proj_attn")
    gate = _mm(u, w_gate, mode="nn", out_dtype=BF16, name="proj_gate")
    qn, kvn, kr, ncum = _small_prep(small, mla_q_norm_g, mla_kv_norm_g, fb128, ctab, stab, tri_lo)
    qcat = _mm(qn, w_uq_a, mode="nn", out_dtype=BF16, name="mla_q", epilogue=_rope_pairs, row_ins=(ctab, stab))
    kv = _mm(kvn, w_ukv_a, mode="nn", out_dtype=BF16, name="mla_kv")
    nbrep = jnp.broadcast_to(ncum[:, :HEADS].T[:, :, None], (HEADS, lp, LANES))

    mla_cols = dict(qcol=0, kcol=lambda p: p, vcol=lambda p: PAIRS + p)
    fox_cols = dict(qcol=0, kcol=lambda p: PAIRS + p, vcol=lambda p: 2 * PAIRS + p)
    o_mla, lse_mla = _attn_fwd(qcat, kv, kv, kr=kr, scale=MLA_SCALE, name="mla_fwd", **mla_cols)
    o_fox, lse_fox = _attn_fwd(attn, attn, attn, nbrep=nbrep, scale=FOX_SCALE, name="fox_fwd", **fox_cols)

    a_mla, a_fox = _gate_fwd(o_mla, o_fox, gate)
    y_mla = _mm(a_mla, w_bm, mode="nn", out_dtype=BF16, name="br_mla")
    y_fox = _mm(a_fox, w_bf, mode="nn", out_dtype=BF16, name="br_fox")
    mg = _merge_fwd(gate, y_mla, y_fox)
    mixed = _mm(mg, w_o, mode="nn", out_dtype=F32, name="out_proj")
    dmixed, dy, loss_p, dg_post = _tail(x2, mixed, tgt2, post_norm_g)

    d_w_out = _mm(mg, dmixed, mode="tn", out_dtype=F32, name="d_w_out")
    dm = _mm(dmixed, w_o, mode="nt", out_dtype=BF16, name="d_merge")
    dy_mla, dy_fox, dgate_ab = _merge_bwd(dm, gate, y_mla, y_fox)
    d_w_bm = _mm(a_mla, dy_mla, mode="tn", out_dtype=F32, name="d_w_br_mla")
    d_w_bf = _mm(a_fox, dy_fox, mode="tn", out_dtype=F32, name="d_w_br_fox")
    da_mla = _mm(dy_mla, w_bm, mode="nt", out_dtype=BF16, name="d_a_mla")
    da_fox = _mm(dy_fox, w_bf, mode="nt", out_dtype=BF16, name="d_a_fox")
    do_mla, do_fox, dgate_z, dl_mla, dl_fox = _gate_bwd(da_mla, da_fox, o_mla, o_fox, gate)
    dl_mla, dl_fox = (d[:, :HEADS].T.reshape(PAIRS, 2, lp) for d in (dl_mla, dl_fox))

    dq_a, dkn, dvm, dkr = _attn_bwd(qcat, kv, kv, do_mla, dl_mla, lse_mla, kr=kr, rtabs=(ctab, stab),
                                    scale=MLA_SCALE, name="mla_bwd", **mla_cols)
    dfq, dfk, dfv, dcol, drow = _attn_bwd(attn, attn, attn, do_fox, dl_fox, lse_fox, nbrep=nbrep, scale=FOX_SCALE,
                                          name="fox_bwd", **fox_cols)

    d_w_uq_a = _mm(qn, dq_a, mode="tn", out_dtype=F32, name="d_w_uq")
    dqn = _mm(dq_a, w_uq_a, mode="nt", out_dtype=F32, name="d_qn")
    d_w_ukv_a = jnp.concatenate([_mm(kvn, dkn, mode="tn", out_dtype=F32, name="d_w_uk"),
                                 _mm(kvn, dvm, mode="tn", out_dtype=F32, name="d_w_uv")], axis=1)
    dkvn = _mm(dkn, w_ukv_a[:, :1024], mode="nt", out_dtype=F32, name="d_kvn_k")
    dkvn = _mm(dvm, w_ukv_a[:, 1024:], mode="nt", out_dtype=F32, name="d_kvn_v", acc=dkvn)
    dsmall, dg_q, dg_kv, dfb = _small_bwd(small, dqn, dkvn, dkr, dcol, drow, mla_q_norm_g, mla_kv_norm_g,
                                          fb128, ctab, stab, tri_up)

    dw_small = _mm(u, dsmall, mode="tn", out_dtype=BF16, name="d_w_small")
    dw_fq = _mm(u, dfq, mode="tn", out_dtype=BF16, name="d_w_fq")
    dw_fk = _mm(u, dfk, mode="tn", out_dtype=BF16, name="d_w_fk")
    dw_fv = _mm(u, dfv, mode="tn", out_dtype=BF16, name="d_w_fv")
    dw_z = _mm(u, dgate_z, mode="tn", out_dtype=BF16, name="d_w_z")
    dw_g = _mm(u, dgate_ab, mode="tn", out_dtype=BF16, name="d_w_g")
    d_w_in = (dw_small, dw_z, dw_fq, dw_fk, dw_fv, dw_g)
    d_w_uq = _uq_restore(d_w_uq_a)
    d_w_ukv = _ukv_restore(d_w_ukv_a)
    token = start_exchange(d_w_in, d_w_uq, d_w_ukv, d_w_bm, d_w_bf, d_w_out) if start_exchange else None
    du = _mm_sum_nt([(dsmall, w_small), (dfq, w_attn[:, 0:1024]), (dfk, w_attn[:, 1024:2048]),
                     (dfv, w_attn[:, 2048:3072]), (dgate_z, w_gate[:, 0:2048]), (dgate_ab, w_gate[:, 2048:4096])],
                    name="d_u", after=token)
    dx, dmeta, dg_pre = _pre_bwd(du, x2, meta_f, dy, pre_norm_g)
    return (loss_p, dx, dmeta, d_w_in, d_w_uq, d_w_ukv, d_w_bm, d_w_bf, d_w_out, dg_pre, dg_post, dg_q, dg_kv, dfb)


def _w_in_slabs(pieces):
    dw_small, dw_z, dw_fq, dw_fk, dw_fv, dw_g = pieces
    runs = [(dw_small[:, 0:416], C_CQ), (dw_z[:, 0:1024], C_ZMLA), (dw_fq, C_FQ), (dw_fk, C_FK), (dw_fv, C_FV),
            (dw_small[:, 512:528], C_FL), (dw_z[:, 1024:2048], C_ZFOX), (dw_g, C_GA)]
    slabs = []
    for j in range(N_CHIPS):
        lo, hi = W_IN_SHARD * j, W_IN_SHARD * (j + 1)
        cols = [a[:, max(lo, c0) - c0:min(hi, c0 + a.shape[1]) - c0] for a, c0 in runs
                if max(lo, c0) < min(hi, c0 + a.shape[1])]
        slabs.append(jnp.concatenate(cols, axis=1))
    return jnp.stack(slabs, axis=0)


def kernel(x, meta_tokens, pre_norm_g, w_in, fox_forget_b, mla_q_norm_g, mla_kv_norm_g, w_uq, w_ukv, w_br_mla, w_br_fox, w_out, post_norm_g, loss_target, m_meta_tokens, m_pre_norm_g, m_w_in, m_fox_forget_b, m_mla_q_norm_g, m_mla_kv_norm_g, m_w_uq, m_w_ukv, m_w_br_mla, m_w_br_fox, m_w_out, m_post_norm_g, v_meta_tokens, v_pre_norm_g, v_w_in, v_fox_forget_b, v_mla_q_norm_g, v_mla_kv_norm_g, v_w_uq, v_w_ukv, v_w_br_mla, v_w_br_fox, v_w_out, v_post_norm_g):
    me = 2 * lax.axis_index("x") + lax.axis_index("y")
    core = lax.axis_index("c")
    w_in_b = w_in.astype(BF16).reshape(D_MODEL, W_IN_SHARD)
    p2 = _pack_p2(w_uq[0], w_ukv[0], w_br_mla[0], w_br_fox[0], w_out[0], BF16)
    w_in_g, meta_g = _gather_weights([w_in_b], meta_tokens)
    p2_g = _gather_late(lax.optimization_barrier((p2, w_in_g))[0])
    slabs = [jnp.where(me == j, w_in_b, w_in_g[j]) for j in range(N_CHIPS)]
    chip = lax.broadcasted_iota(jnp.int32, (N_CHIPS, 1, 1), 0)
    p2_all = jnp.where(chip == me, p2[None], p2_g)
    w_uq_f = p2_all[:, 0:96].reshape(N_CHIPS, 256, 384).transpose(1, 0, 2).reshape(256, 1536)
    w_ukv_f = p2_all[:, 96:160].reshape(N_CHIPS, 128, 512).transpose(1, 0, 2).reshape(128, 2048)
    w_bm, w_bf, w_o = (p2_all[:, lo:lo + 256].reshape(D_MODEL, D_MODEL) for lo in (160, 416, 672))
    meta_f = jnp.where(chip == me, meta_tokens[None], meta_g).transpose(1, 0, 2).reshape(N_META, D_MODEL)
    kpe = _in_cols(slabs, C_KPE, C_ZMLA)
    w_small = jnp.concatenate(_in_cols(slabs, C_CQ, C_KPE) + kpe + kpe + [jnp.zeros((D_MODEL, 64), BF16)]
                              + _in_cols(slabs, C_FL, C_ZFOX) + [jnp.zeros((D_MODEL, 112), BF16)], axis=1)
    w_attn = jnp.concatenate(_in_cols(slabs, C_FQ, C_FL), axis=1)
    w_gate = jnp.concatenate(_in_cols(slabs, C_ZMLA, C_FQ) + _in_cols(slabs, C_ZFOX, C_END), axis=1)

    exchange = {}

    def start_exchange(d_w_in, d_w_uq, d_w_ukv, d_w_bm, d_w_bf, d_w_out):
        g2 = jnp.concatenate(
            [d_w_uq.reshape(256, N_CHIPS, 384).transpose(1, 0, 2).reshape(N_CHIPS, 96, D_MODEL),
             d_w_ukv.reshape(128, N_CHIPS, 512).transpose(1, 0, 2).reshape(N_CHIPS, 64, D_MODEL)]
            + [g.reshape(N_CHIPS, 256, D_MODEL) for g in (d_w_bm, d_w_bf, d_w_out)], axis=1)
        pieces = [p[None] for p in d_w_in]
        from_sib = _swap_halves(pieces + [g2])
        halves = [_add_cores(p, s, "add_cores_" + nm)[0]
                  for p, s, nm in zip(pieces, from_sib, ("small", "z", "fq", "fk", "fv", "g"))]
        parts = [_w_in_slabs(halves), _add_cores(g2, from_sib[-1], "add_cores_rest")]
        exchange.update(parts=parts, landed=_scatter_chips(parts))
        return parts[0][0, 0:16, 0:LANES]

    (loss_p, dx, dmeta, _, _, _, _, _, _, dg_pre, dg_post, dg_q, dg_kv,
     dfb) = _local_step(x[0], loss_target[0], meta_f, w_small, w_attn, w_gate, w_uq_f, w_ukv_f, w_bm, w_bf, w_o,
                        pre_norm_g, post_norm_g, mla_q_norm_g, mla_kv_norm_g, fox_forget_b, start_exchange)

    mine = [_add_chips(l, lax.dynamic_index_in_dim(p, me, 0, keepdims=False), nm)
            for l, p, nm in zip(exchange["landed"], exchange["parts"], ("add_chips_w_in", "add_chips_rest"))]
    theirs = _swap_reduced(mine)
    g_w_in, g_p2 = [jnp.concatenate([jnp.where(core == 0, a, b), jnp.where(core == 0, b, a)], axis=0)
                    for a, b in zip(mine, theirs)]
    g_w_uq, g_w_ukv, g_w_bm, g_w_bf, g_w_out = _unpack_p2(g_p2)
    g_w_in = g_w_in[None]

    vec = jnp.concatenate([dg_pre.reshape(8, 128), dg_post.reshape(8, 128), dg_q.reshape(2, 128), dg_kv,
                           dfb, _pad_lanes(loss_p), jnp.zeros((3, 128), F32), dmeta.reshape(128, 128)], axis=0)
    tot = _allreduce_small(vec)
    loss = tot[20, 0]
    g_meta = lax.dynamic_slice_in_dim(tot[24:].reshape(N_META, D_MODEL), 256 * me, 256, axis=1)

    def small_pack(pre, post, gq_, gkv_, fb_):
        return jnp.concatenate([pre.reshape(8, 128), post.reshape(8, 128), gq_.reshape(2, 128), gkv_,
                                _pad_lanes(fb_), jnp.zeros((4, 128), F32)], axis=0)

    def small_unpack(t):
        return (t[0:8].reshape(1, 1024), t[8:16].reshape(1, 1024), t[16:18].reshape(1, 256), t[18:19],
                t[19:20, 0:HEADS])

    g_small = jnp.concatenate([tot[0:20], jnp.zeros((4, 128), F32)], axis=0)
    sm = _adamw(small_pack(pre_norm_g, post_norm_g, mla_q_norm_g, mla_kv_norm_g, fox_forget_b), g_small,
                small_pack(m_pre_norm_g, m_post_norm_g, m_mla_q_norm_g, m_mla_kv_norm_g, m_fox_forget_b),
                small_pack(v_pre_norm_g, v_post_norm_g, v_mla_q_norm_g, v_mla_kv_norm_g, v_fox_forget_b),
                "adamw_small")
    g_pre, g_post, g_q, g_kv, g_fb = small_unpack(g_small)
    (d_pre, d_post, d_q, d_kv, d_fb), (nm_pre, nm_post, nm_q, nm_kv, nm_fb), (nv_pre, nv_post, nv_q, nv_kv, nv_fb) = (
        small_unpack(t) for t in sm)

    d_meta, nm_meta, nv_meta = _adamw(meta_tokens, g_meta, m_meta_tokens, v_meta_tokens, "adamw_meta")
    d_win, nm_win, nv_win = (t.T[None] for t in _adamw(w_in[0].T, g_w_in[0].T, m_w_in[0].T, v_w_in[0].T,
                                                       "adamw_w_in"))
    d_wuq, nm_wuq, nv_wuq = _adamw(w_uq[0], g_w_uq, m_w_uq[0], v_w_uq[0], "adamw_w_uq")
    d_wukv, nm_wukv, nv_wukv = _adamw(w_ukv[0], g_w_ukv, m_w_ukv[0], v_w_ukv[0], "adamw_w_ukv")
    d_wbm, nm_wbm, nv_wbm = _adamw(w_br_mla[0], g_w_bm, m_w_br_mla[0], v_w_br_mla[0], "adamw_w_br_mla")
    d_wbf, nm_wbf, nv_wbf = _adamw(w_br_fox[0], g_w_bf, m_w_br_fox[0], v_w_br_fox[0], "adamw_w_br_fox")
    d_wo, nm_wo, nv_wo = _adamw(w_out[0], g_w_out, m_w_out[0], v_w_out[0], "adamw_w_out")

    def group(meta_, pre, win, fb_, q_, kv_, wuq, wukv, wbm, wbf, wo, post):
        return (meta_, pre, win, fb_, q_, kv_, wuq[None], wukv[None], wbm[None], wbf[None], wo[None], post)

    grads = group(g_meta, g_pre, g_w_in, g_fb, g_q, g_kv, g_w_uq, g_w_ukv, g_w_bm, g_w_bf, g_w_out, g_post)
    deltas = group(d_meta, d_pre, d_win, d_fb, d_q, d_kv, d_wuq, d_wukv, d_wbm, d_wbf, d_wo, d_post)
    new_m = group(nm_meta, nm_pre, nm_win, nm_fb, nm_q, nm_kv, nm_wuq, nm_wukv, nm_wbm, nm_wbf, nm_wo, nm_post)
    new_v = group(nv_meta, nv_pre, nv_win, nv_fb, nv_q, nv_kv, nv_wuq, nv_wukv, nv_wbm, nv_wbf, nv_wo, nv_post)
    return (loss, dx[None], *grads, *deltas, *new_m, *new_v)
```

```python
import math

import jax
import jax.numpy as jnp
import numpy as np
from jax import lax
from jax.experimental import pallas as pl
from jax.experimental.pallas import tpu as pltpu
from jax.experimental.pallas import tpu_sc as plsc

F32 = jnp.float32
BF16 = jnp.bfloat16

D_MODEL = 1024
N_META = 16
RMS_EPS = 1e-6
HEADS = 16
PAIRS = HEADS // 2
HEAD_DIM = 64
LANES = 128
MLA_ROPE = 32
MLA_SCALE = 1.0 / math.sqrt(64 + 32)
FOX_SCALE = 1.0 / math.sqrt(64)
LOG2E = math.log2(math.e)
LN2 = math.log(2.0)
ROPE_THETA = 10000.0

PAD = 256
BLK = 256
QB = 512
UNROLL = 4
NEG = -1e30

C_CQ, C_CKV, C_KPE, C_ZMLA, C_FQ, C_FK, C_FV, C_FL, C_ZFOX, C_GA, C_GB, C_END = (
    0, 256, 384, 416, 1440, 2464, 3488, 4512, 4528, 5552, 6576, 7600)
SMALL_W = 640
W_IN_SHARD = 1900

P2_ROWS = 928
N_CHIPS = 4

ADAM_LR = 0.001
ADAM_B1 = 0.9
ADAM_B2 = 0.999
ADAM_EPS = 1e-08
ADAM_WD = 0.01
ADAM_STEP = 10

VMEM_BIG = 56 * 1024 * 1024
MM_VMEM_BUDGET = 44 * 1024 * 1024
MESH = pl.DeviceIdType.MESH


def _cp(dims, vmem=None):
    return pltpu.CompilerParams(dimension_semantics=dims, vmem_limit_bytes=vmem)


def _dot(a, b, ca, cb):
    return lax.dot_general(a, b, (((ca,), (cb,)), ((), ())), preferred_element_type=F32)


def _sigmoid(x):
    return 1.0 / (1.0 + jnp.exp(-x))


def _tile(n, cands):
    for c in cands:
        if n % c == 0:
            return c
    return n


def _mm(a, b, *, mode, out_dtype, name, acc=None, epilogue=None, row_ins=(), after=None, col_scale=None):
    if mode == "nn":
        (M, K), N = a.shape, b.shape[1]
    elif mode == "nt":
        (M, K), N = a.shape, b.shape[0]
    else:
        (K, M), N = a.shape, b.shape[1]
    tm = _tile(M, (1088, 1024)) if M > 1024 else M
    tn = _tile(N, (1024,)) if N > 1024 else N
    nk = 1
    while True:
        tk = K // nk
        need = 2 * tk * (tm * a.dtype.itemsize + tn * b.dtype.itemsize) + tm * tn * (
            2 * jnp.dtype(out_dtype).itemsize + (8 if acc is not None else 0) + (4 if nk > 1 else 0))
        if need <= MM_VMEM_BUDGET or (tk // 2) % (16 if mode == "tn" else LANES) or tk <= 512:
            break
        nk *= 2
    while (M // tm) * (N // tn) * nk < 4 and tn % 512 == 0:
        tn //= 2
    assert col_scale is None or (nk == 1 and col_scale[0] % tn == 0)
    ca, cb = {"nn": (1, 0), "nt": (1, 1), "tn": (0, 0)}[mode]
    a_spec = (pl.BlockSpec((tk, tm), lambda j, i, k: (k, i)) if mode == "tn"
              else pl.BlockSpec((tm, tk), lambda j, i, k: (i, k)))
    b_spec = (pl.BlockSpec((tn, tk), lambda j, i, k: (j, k)) if mode == "nt"
              else pl.BlockSpec((tk, tn), lambda j, i, k: (k, j)))
    o_spec = pl.BlockSpec((tm, tn), lambda j, i, k: (i, j))
    has_acc = acc is not None

    nrow = len(row_ins)

    def body(*refs):
        a_ref, b_ref = refs[0], refs[1]
        acc_ref = refs[2] if has_acc else None
        rows = refs[2 + has_acc:2 + has_acc + nrow]
        o_ref = refs[2 + has_acc + nrow + (after is not None)]

        def store(tile):
            if epilogue is not None:
                tile = epilogue(tile, *[r[...] for r in rows])
            if col_scale is not None:
                tile = tile * jnp.where(pl.program_id(0) * tn < col_scale[0], col_scale[1], 1.0)
            o_ref[...] = tile.astype(out_dtype)

        part = _dot(a_ref[...].astype(BF16), b_ref[...].astype(BF16), ca, cb)
        if nk == 1:
            store(part + acc_ref[...] if has_acc else part)
        else:
            sc = refs[-1]
            k = pl.program_id(2)

            @pl.when(k == 0)
            def _():
                sc[...] = part + acc_ref[...] if has_acc else part

            @pl.when(k > 0)
            def _():
                sc[...] += part

            @pl.when(k == nk - 1)
            def _():
                store(sc[...])

    ins = [a, b] + ([acc] if has_acc else []) + list(row_ins)
    in_specs = ([a_spec, b_spec] + ([o_spec] if has_acc else [])
                + [pl.BlockSpec((tm, r.shape[1]), lambda j, i, k: (i, 0)) for r in row_ins])
    if after is not None:
        ins.append(after)
        in_specs.append(pl.BlockSpec(after.shape, lambda j, i, k: (0,) * after.ndim))
    return pl.pallas_call(
        body, name=name, grid=(N // tn, M // tm, nk), in_specs=in_specs, out_specs=o_spec,
        out_shape=jax.ShapeDtypeStruct((M, N), out_dtype),
        scratch_shapes=[pltpu.VMEM((tm, tn), F32)] if nk > 1 else [],
        compiler_params=_cp(("parallel", "parallel", "arbitrary"), VMEM_BIG))(*ins)


def _mm_sum_nt(pairs, *, name, after=None):
    n = len(pairs)
    M, N = pairs[0][0].shape[0], pairs[0][1].shape[0]
    tm = _tile(M, (272,))

    def body(*refs):
        o_ref = refs[2 * n + (after is not None)]
        tot = _dot(refs[0][...].astype(BF16), refs[n][...].astype(BF16), 1, 1)
        for i in range(1, n):
            tot = tot + _dot(refs[i][...].astype(BF16), refs[n + i][...].astype(BF16), 1, 1)
        o_ref[...] = tot

    ins = [a for a, _ in pairs] + [b for _, b in pairs]
    in_specs = ([pl.BlockSpec((tm, a.shape[1]), lambda i: (i, 0)) for a, _ in pairs]
                + [pl.BlockSpec(b.shape, lambda i: (0, 0)) for _, b in pairs])
    if after is not None:
        ins.append(after)
        in_specs.append(pl.BlockSpec(after.shape, lambda i: (0,) * after.ndim))
    return pl.pallas_call(
        body, name=name, grid=(M // tm,), in_specs=in_specs, out_specs=pl.BlockSpec((tm, N), lambda i: (i, 0)),
        out_shape=jax.ShapeDtypeStruct((M, N), F32), compiler_params=_cp(("parallel",), VMEM_BIG))(*ins)


def _row(w):
    return pl.BlockSpec((BLK, w), lambda i: (i, 0))


def _rowc(w, c):
    return pl.BlockSpec((BLK, w), lambda i: (i, c))


def _full(shape):
    return pl.BlockSpec(shape, lambda i: tuple(0 for _ in shape))


def _rope(x, c, s):
    lane = lax.broadcasted_iota(jnp.int32, x.shape, 1)
    is_x1 = ((lane >> 4) & 1) == 0
    partner = jnp.where(is_x1, pltpu.roll(x, LANES - 16, 1), pltpu.roll(x, 16, 1))
    return x * c + partner * s


def _row_valid(i):
    rows = i * BLK + lax.broadcasted_iota(jnp.int32, (BLK, 1), 0)
    return (rows < N_META) | (rows >= PAD)


def _shift_rows(w):
    return pl.BlockSpec((BLK, w), lambda i: (jnp.maximum(i - 1, 0), 0))


def _h_block(i, x_ref, meta_ref):
    head = jnp.concatenate([meta_ref[...], jnp.zeros((BLK - N_META, D_MODEL), F32)], axis=0)
    return jnp.where(i == 0, head, x_ref[...])


def _rms_pre(x2, meta, g):
    lp = PAD + x2.shape[0]

    def body(x_ref, meta_ref, g_ref, u_ref):
        hv = _h_block(pl.program_id(0), x_ref, meta_ref)
        r = lax.rsqrt(jnp.mean(hv * hv, axis=-1, keepdims=True) + RMS_EPS)
        u_ref[...] = (hv * r * g_ref[...]).astype(BF16)

    return pl.pallas_call(
        body, name="rms_pre", grid=(lp // BLK,),
        in_specs=[_shift_rows(D_MODEL), _full((N_META, D_MODEL)), _full((1, D_MODEL))], out_specs=_row(D_MODEL),
        out_shape=jax.ShapeDtypeStruct((lp, D_MODEL), BF16),
        compiler_params=_cp(("parallel",)))(x2, meta, g)


def _split3(x):
    hi = x.astype(BF16)
    r1 = x - hi.astype(F32)
    mid = r1.astype(BF16)
    lo = (r1 - mid.astype(F32)).astype(BF16)
    return hi, mid, lo


def _small_prep(small, gq, gkv, fb, ctab, stab, tri):
    lp = small.shape[0]

    def body(sm_ref, gq_ref, gkv_ref, fb_ref, c_ref, s_ref, tri_ref, qn_ref, kvn_ref, kr_ref, ncum_ref, carry):
        i = pl.program_id(0)

        @pl.when(i == 0)
        def _():
            carry[...] = jnp.zeros_like(carry)

        cq = sm_ref[:, 0:256]
        r = lax.rsqrt(jnp.mean(cq * cq, axis=-1, keepdims=True) + RMS_EPS)
        qn_ref[...] = (cq * r * gq_ref[...]).astype(BF16)
        ckv = sm_ref[:, 256:384]
        r = lax.rsqrt(jnp.mean(ckv * ckv, axis=-1, keepdims=True) + RMS_EPS)
        kvn_ref[...] = (ckv * r * gkv_ref[...]).astype(BF16)
        kr_ref[...] = _rope(sm_ref[:, 384:512], c_ref[...], s_ref[...]).astype(BF16)
        fl = sm_ref[:, 512:640] + fb_ref[...]
        lf = jnp.minimum(fl, 0.0) - jnp.log(1.0 + jnp.exp(-jnp.abs(fl)))
        lf = jnp.where(_row_valid(i), lf, 0.0)
        hi, mid, lo = _split3(lf)
        t = tri_ref[...]
        cum = (_dot(t, hi, 1, 0) + _dot(t, mid, 1, 0)) + _dot(t, lo, 1, 0) + carry[...]
        ncum_ref[...] = -cum
        carry[...] = -ncum_ref[BLK - 1:BLK, :]

    return pl.pallas_call(
        body, name="small_prep", grid=(lp // BLK,),
        in_specs=[_row(SMALL_W), _full((1, 256)), _full((1, 128)), _full((1, 128)), _row(128), _row(128),
                  _full((BLK, BLK))],
        out_specs=[_row(256), _row(128), _row(128), _row(128)],
        out_shape=[jax.ShapeDtypeStruct((lp, 256), BF16), jax.ShapeDtypeStruct((lp, 128), BF16),
                   jax.ShapeDtypeStruct((lp, 128), BF16), jax.ShapeDtypeStruct((lp, 128), F32)],
        scratch_shapes=[pltpu.VMEM((1, 128), F32)],
        compiler_params=_cp(("arbitrary",)))(small, gq, gkv, fb, ctab, stab, tri)


def _rope_pairs(tile, c, s):
    out = []
    for lo in range(0, tile.shape[1], 256):
        out += [tile[:, lo:lo + 128], _rope(tile[:, lo + 128:lo + 256], c, s)]
    return jnp.concatenate(out, axis=1)


def _gate_fwd(o_mla, o_fox, gate):
    lp = o_mla.shape[0]

    def body(om_ref, of_ref, zm_ref, zf_ref, am_ref, af_ref):
        zm = zm_ref[...].astype(F32)
        am_ref[...] = (om_ref[...] * (zm * _sigmoid(zm))).astype(BF16)
        zf = zf_ref[...].astype(F32)
        af_ref[...] = (of_ref[...] * (zf * _sigmoid(zf))).astype(BF16)

    return pl.pallas_call(
        body, name="gate_fwd", grid=(lp // BLK,),
        in_specs=[_row(D_MODEL), _row(D_MODEL), _rowc(D_MODEL, 0), _rowc(D_MODEL, 1)],
        out_specs=[_row(D_MODEL), _row(D_MODEL)],
        out_shape=[jax.ShapeDtypeStruct((lp, D_MODEL), BF16)] * 2,
        compiler_params=_cp(("parallel",)))(o_mla, o_fox, gate, gate)


def _merge_fwd(gate, y_mla, y_fox):
    lp = y_mla.shape[0]

    def body(ga_ref, gb_ref, ym_ref, yf_ref, m_ref):
        sa = _sigmoid(ga_ref[...].astype(F32))
        sb = _sigmoid(gb_ref[...].astype(F32))
        m_ref[...] = (sa * ym_ref[...] + sb * yf_ref[...]).astype(BF16)

    return pl.pallas_call(
        body, name="merge_fwd", grid=(lp // BLK,),
        in_specs=[_rowc(D_MODEL, 2), _rowc(D_MODEL, 3), _row(D_MODEL), _row(D_MODEL)],
        out_specs=_row(D_MODEL), out_shape=jax.ShapeDtypeStruct((lp, D_MODEL), BF16),
        compiler_params=_cp(("parallel",)))(gate, gate, y_mla, y_fox)


def _tail(x2, mixed, tgt, gpost):
    lp = mixed.shape[0]
    shift = _shift_rows(D_MODEL)

    def body(h_ref, mx_ref, t_ref, g_ref, dmx_ref, dy_ref, loss_ref, dg_ref):
        i = pl.program_id(0)

        @pl.when(i == 0)
        def _():
            loss_ref[...] = jnp.zeros_like(loss_ref)
            dg_ref[...] = jnp.zeros_like(dg_ref)
            dmx_ref[...] = jnp.zeros_like(dmx_ref)
            dy_ref[...] = jnp.zeros_like(dy_ref)

        @pl.when(i > 0)
        def _():
            mx = mx_ref[...]
            g = g_ref[...]
            r = lax.rsqrt(jnp.mean(mx * mx, axis=-1, keepdims=True) + RMS_EPS)
            nrm = mx * r
            e = (h_ref[...] + nrm * g) - t_ref[...]
            loss_ref[...] += jnp.sum(0.5 * jnp.sum(e * e, axis=-1, keepdims=True) * (1.0 / D_MODEL),
                                     axis=0, keepdims=True)
            dy = e * (1.0 / D_MODEL)
            dy_ref[...] = dy
            dg_ref[...] += jnp.sum(dy * nrm, axis=0, keepdims=True)
            w = dy * g
            dot = jnp.mean(w * mx, axis=-1, keepdims=True)
            dmx_ref[...] = (r * w - mx * (r * r * r * dot)).astype(BF16)

    return pl.pallas_call(
        body, name="tail", grid=(lp // BLK,),
        in_specs=[shift, _row(D_MODEL), shift, _full((1, D_MODEL))],
        out_specs=[_row(D_MODEL), _row(D_MODEL), _full((1, 1)), _full((1, D_MODEL))],
        out_shape=[jax.ShapeDtypeStruct((lp, D_MODEL), BF16), jax.ShapeDtypeStruct((lp, D_MODEL), F32),
                   jax.ShapeDtypeStruct((1, 1), F32), jax.ShapeDtypeStruct((1, D_MODEL), F32)],
        compiler_params=_cp(("arbitrary",)))(x2, mixed, tgt, gpost)


def _merge_bwd(dm, gate, y_mla, y_fox):
    lp = dm.shape[0]

    def body(dm_ref, ga_ref, gb_ref, ym_ref, yf_ref, dym_ref, dyf_ref, dg_ref):
        dm_v = dm_ref[...].astype(F32)
        sa = _sigmoid(ga_ref[...].astype(F32))
        sb = _sigmoid(gb_ref[...].astype(F32))
        dym_ref[...] = (dm_v * sa).astype(BF16)
        dyf_ref[...] = (dm_v * sb).astype(BF16)
        dg_ref[:, 0:D_MODEL] = (dm_v * ym_ref[...] * (sa * (1.0 - sa))).astype(BF16)
        dg_ref[:, D_MODEL:2 * D_MODEL] = (dm_v * yf_ref[...] * (sb * (1.0 - sb))).astype(BF16)

    return pl.pallas_call(
        body, name="merge_bwd", grid=(lp // BLK,),
        in_specs=[_row(D_MODEL), _rowc(D_MODEL, 2), _rowc(D_MODEL, 3), _row(D_MODEL), _row(D_MODEL)],
        out_specs=[_row(D_MODEL), _row(D_MODEL), _row(2 * D_MODEL)],
        out_shape=[jax.ShapeDtypeStruct((lp, D_MODEL), BF16), jax.ShapeDtypeStruct((lp, D_MODEL), BF16),
                   jax.ShapeDtypeStruct((lp, 2 * D_MODEL), BF16)],
        compiler_params=_cp(("parallel",)))(dm, gate, gate, y_mla, y_fox)


def _gate_bwd(da_mla, da_fox, o_mla, o_fox, gate):
    lp = da_mla.shape[0]

    def one(da, o, z, head_of_col):
        sg = _sigmoid(z)
        do = (da * (z * sg)).astype(BF16)
        dz = da * o * (sg * (1.0 + z * (1.0 - sg)))
        delta = sum(_dot(part, head_of_col, 1, 0) for part in _split3(do.astype(F32) * o))
        return do, dz.astype(BF16), delta

    def body(dam_ref, daf_ref, om_ref, of_ref, zm_ref, zf_ref, dom_ref, dof_ref, dz_ref, dlm_ref, dlf_ref):
        f32 = lambda r: r[...].astype(F32)
        head_of_col = (lax.broadcasted_iota(jnp.int32, (D_MODEL, LANES), 0) // HEAD_DIM
                       == lax.broadcasted_iota(jnp.int32, (D_MODEL, LANES), 1)).astype(BF16)
        dom_ref[...], dz_ref[:, 0:D_MODEL], dlm_ref[...] = one(f32(dam_ref), f32(om_ref), f32(zm_ref), head_of_col)
        dof_ref[...], dz_ref[:, D_MODEL:2 * D_MODEL], dlf_ref[...] = one(f32(daf_ref), f32(of_ref), f32(zf_ref),
                                                                        head_of_col)

    return pl.pallas_call(
        body, name="gate_bwd", grid=(lp // BLK,),
        in_specs=[_row(D_MODEL)] * 4 + [_rowc(D_MODEL, 0), _rowc(D_MODEL, 1)],
        out_specs=[_row(D_MODEL), _row(D_MODEL), _row(2 * D_MODEL), _row(LANES), _row(LANES)],
        out_shape=[jax.ShapeDtypeStruct((lp, D_MODEL), BF16), jax.ShapeDtypeStruct((lp, D_MODEL), BF16),
                   jax.ShapeDtypeStruct((lp, 2 * D_MODEL), BF16), jax.ShapeDtypeStruct((lp, LANES), F32),
                   jax.ShapeDtypeStruct((lp, LANES), F32)],
        compiler_params=_cp(("parallel",)))(da_mla, da_fox, o_mla, o_fox, gate, gate)


def _small_bwd(small, dqn, dkvn, dkr, dcol_t, drow_t, gq, gkv, fb, ctab, stab, triu):
    lp = small.shape[0]
    nb = lp // BLK

    def rrow(w):
        return pl.BlockSpec((BLK, w), lambda i: (nb - 1 - i, 0))

    def body(sm_ref, dqn_ref, dkvn_ref, dkr_ref, dcol_ref, drow_ref, gq_ref, gkv_ref, fb_ref, c_ref, s_ref, tri_ref,
             ds_ref, dgq_ref, dgkv_ref, dfb_ref, carry):
        i = pl.program_id(0)

        @pl.when(i == 0)
        def _():
            carry[...] = jnp.zeros_like(carry)
            dgq_ref[...] = jnp.zeros_like(dgq_ref)
            dgkv_ref[...] = jnp.zeros_like(dgkv_ref)
            dfb_ref[...] = jnp.zeros_like(dfb_ref)

        def norm_bwd(x, dn, g, dg_ref):
            r = lax.rsqrt(jnp.mean(x * x, axis=-1, keepdims=True) + RMS_EPS)
            dg_ref[...] += jnp.sum(dn * (x * r), axis=0, keepdims=True)
            w = dn * g
            dot = jnp.mean(w * x, axis=-1, keepdims=True)
            return r * w - x * (r * r * r * dot)

        ds_ref[:, 0:256] = norm_bwd(sm_ref[:, 0:256], dqn_ref[...], gq_ref[...], dgq_ref).astype(BF16)
        ds_ref[:, 256:384] = norm_bwd(sm_ref[:, 256:384], dkvn_ref[...], gkv_ref[...], dgkv_ref).astype(BF16)

        dk = dkr_ref[0]
        for p in range(1, PAIRS):
            dk = dk + dkr_ref[p]
        dk = _rope(dk, c_ref[...], -s_ref[...])
        lane = lax.broadcasted_iota(jnp.int32, dk.shape, 1)
        dk = jnp.where(lane < MLA_ROPE, dk + pltpu.roll(dk, LANES - MLA_ROPE, 1), 0.0)
        ds_ref[:, 384:512] = dk.astype(BF16)

        dcol = dcol_ref[0]
        for p in range(1, PAIRS):
            dcol = dcol + pltpu.roll(dcol_ref[p], 2 * p, 1)
        rows16 = jnp.concatenate([drow_ref[p, h:h + 1, :] for p in range(PAIRS) for h in range(2)], axis=0)
        eye = (lax.broadcasted_iota(jnp.int32, (HEADS, LANES), 0)
               == lax.broadcasted_iota(jnp.int32, (HEADS, LANES), 1)).astype(BF16)
        drow = sum(_dot(part, eye, 0, 0) for part in _split3(rows16))
        dcr = dcol - drow
        hi, mid, lo = _split3(dcr)
        t = tri_ref[...]
        suf = (_dot(t, hi, 1, 0) + _dot(t, mid, 1, 0)) + _dot(t, lo, 1, 0) + carry[...]
        fl = sm_ref[:, 512:640] + fb_ref[...]
        dfl = jnp.where(_row_valid(nb - 1 - i), -suf * _sigmoid(-fl), 0.0)
        ds_ref[:, 512:640] = dfl.astype(BF16)
        dfb_ref[...] += jnp.sum(dfl, axis=0, keepdims=True)
        carry[...] += jnp.sum(dcr, axis=0, keepdims=True)

    return pl.pallas_call(
        body, name="small_bwd", grid=(nb,),
        in_specs=[rrow(SMALL_W), rrow(256), rrow(128),
                  pl.BlockSpec((PAIRS, BLK, 128), lambda i: (0, nb - 1 - i, 0)),
                  pl.BlockSpec((PAIRS, BLK, 128), lambda i: (0, nb - 1 - i, 0)),
                  pl.BlockSpec((PAIRS, 2, BLK), lambda i: (0, 0, nb - 1 - i)),
                  _full((1, 256)), _full((1, 128)), _full((1, 128)), rrow(128), rrow(128), _full((BLK, BLK))],
        out_specs=[rrow(SMALL_W), _full((1, 256)), _full((1, 128)), _full((1, 128))],
        out_shape=[jax.ShapeDtypeStruct((lp, SMALL_W), BF16), jax.ShapeDtypeStruct((1, 256), F32),
                   jax.ShapeDtypeStruct((1, 128), F32), jax.ShapeDtypeStruct((1, 128), F32)],
        scratch_shapes=[pltpu.VMEM((1, 128), F32)],
        compiler_params=_cp(("arbitrary",)))(small, dqn, dkvn, dkr, dcol_t, drow_t, gq, gkv, fb, ctab, stab, triu)


def _pre_bwd(du, x2, meta, dy, gpre):
    s_rows = x2.shape[0]
    lp = PAD + s_rows
    shift = _shift_rows(D_MODEL)

    def body(du_ref, x_ref, meta_ref, dy_ref, g_ref, dx_ref, dmeta_ref, dg_ref):
        i = pl.program_id(0)

        @pl.when(i == 0)
        def _():
            dg_ref[...] = jnp.zeros_like(dg_ref)

        hv = _h_block(i, x_ref, meta_ref)
        duv = du_ref[...]
        r = lax.rsqrt(jnp.mean(hv * hv, axis=-1, keepdims=True) + RMS_EPS)
        dg_ref[...] += jnp.sum(duv * (hv * r), axis=0, keepdims=True)
        w = duv * g_ref[...]
        dot = jnp.mean(w * hv, axis=-1, keepdims=True)
        dh = dy_ref[...] + (r * w - hv * (r * r * r * dot))
        dx_ref[...] = dh

        @pl.when(i == 0)
        def _():
            dmeta_ref[...] = dh[0:N_META, :]

    return pl.pallas_call(
        body, name="pre_bwd", grid=(lp // BLK,),
        in_specs=[_row(D_MODEL), shift, _full((N_META, D_MODEL)), _row(D_MODEL), _full((1, D_MODEL))],
        out_specs=[shift, _full((N_META, D_MODEL)), _full((1, D_MODEL))],
        out_shape=[jax.ShapeDtypeStruct((s_rows, D_MODEL), F32), jax.ShapeDtypeStruct((N_META, D_MODEL), F32),
                   jax.ShapeDtypeStruct((1, D_MODEL), F32)],
        compiler_params=_cp(("arbitrary",)))(du, x2, meta, dy, gpre)


def _pair_masks(rope):
    lane = lax.broadcasted_iota(jnp.int32, (1, LANES), 1)
    mas = [lane < HEAD_DIM, lane >= HEAD_DIM]
    if not rope:
        return mas, mas
    wide = lax.broadcasted_iota(jnp.int32, (1, 2 * LANES), 1)
    rope_lo = LANES + MLA_ROPE
    return mas, [(wide < HEAD_DIM) | ((wide >= LANES) & (wide < rope_lo)),
                 ((wide >= HEAD_DIM) & (wide < LANES)) | ((wide >= rope_lo) & (wide < rope_lo + MLA_ROPE))]


def _mask2(x, masks):
    return [jnp.where(m, x, jnp.zeros_like(x)) for m in masks]


def _attn_fwd(q, k, v, *, kr=None, nbrep=None, qcol, kcol, vcol, name):
    lp = q.shape[0]
    nq = 1 + (lp - PAD) // QB
    rope = kr is not None
    bias = nbrep is not None
    qw = 256 if rope else 128

    def body(*refs):
        it = iter(refs)
        q_ref, k_ref, v_ref = next(it), next(it), next(it)
        kr_ref = next(it) if rope else None
        nb_ref = next(it) if bias else None
        o_ref, lse_ref = next(it), next(it)
        i = pl.program_id(1)
        r0 = pl.multiple_of(jnp.where(i == 0, 0, PAD + QB * (i - 1)), BLK)
        b0 = r0 // BLK
        mas, hmask = _pair_masks(rope)
        qh = _mask2(q_ref[pl.ds(r0, QB), :], hmask)

        def update(kcs, carry, masks, ns=None, q_lo=0, wq=QB):
            ns = ns or [BLK] * len(kcs)
            stats, acc = carry[:4], carry[4]
            qs = [x[q_lo:q_lo + wq] for x in qh]
            k0s = [pl.multiple_of(kc * BLK, BLK) for kc in kcs]
            kks = [k_ref[pl.ds(k0, n), :] for k0, n in zip(k0s, ns)]
            if rope:
                kks = [jnp.concatenate([kk, kr_ref[pl.ds(k0, n), :]], axis=1) for kk, k0, n in zip(kks, k0s, ns)]
            new_stats, alphas, ps = [], [], [[] for _ in kcs]
            for h in range(2):
                m_prev, l_prev = stats[2 * h], stats[2 * h + 1]
                ss = []
                for kk, k0, n, mask in zip(kks, k0s, ns, masks):
                    s = _dot(kk, qs[h], 1, 1)
                    if bias:
                        nbc = nb_ref[h, pl.ds(k0, n), :]
                        s = s + jnp.concatenate([nbc] * (wq // LANES), axis=1)
                    if mask is not None:
                        s = jnp.where(mask, s, NEG)
                    ss.append(s)
                m_new = m_prev
                for s in ss:
                    m_new = jnp.maximum(m_new, jnp.max(s, axis=0, keepdims=True))
                alpha = jnp.exp2(m_prev - m_new)
                l_new = alpha * l_prev
                for j, s in enumerate(ss):
                    p = jnp.exp2(s - m_new)
                    l_new = l_new + jnp.sum(p, axis=0, keepdims=True)
                    ps[j].append(p.astype(BF16))
                new_stats += [m_new, l_new]
                alphas.append(alpha)
            vcat = jnp.concatenate([x for k0, n in zip(k0s, ns) for x in _mask2(v_ref[pl.ds(k0, n), :], mas)], axis=0)
            pv = _dot(vcat, jnp.concatenate([p for pj in ps for p in pj], axis=0), 0, 0)
            a_full = jnp.concatenate([jnp.broadcast_to(a, (HEAD_DIM, wq)) for a in alphas], axis=0)
            return (*new_stats, a_full * acc + pv)

        neg = jnp.full((1, QB), NEG, F32)
        zero = jnp.zeros((1, QB), F32)
        c = (neg, zero, neg, zero, jnp.zeros((LANES, QB), F32))
        n_mid = jnp.maximum(b0 - 1, 0)
        c = lax.fori_loop(0, n_mid // 4, lambda t, cr: update([4 * t + u for u in (1, 2, 3, 4)], cr, [None] * 4), c)
        c = lax.fori_loop(0, (n_mid % 4) // 2, lambda t, cr: update([n_mid - 1, n_mid], cr, [None, None]), c)
        key_l = lax.broadcasted_iota(jnp.int32, (BLK, BLK), 0)
        qry_l = lax.broadcasted_iota(jnp.int32, (BLK, BLK), 1)
        tri = (key_l <= qry_l) & (b0 > 0)
        meta_ok = (key_l[0:N_META] <= qry_l[0:N_META]) | (b0 > 0)
        lo = update([0, b0], tuple(a[:, 0:BLK] for a in c), [meta_ok, tri], ns=[N_META, BLK], q_lo=0, wq=BLK)
        hi = update([0, b0, b0 + 1], tuple(a[:, BLK:QB] for a in c), [None, None, tri], ns=[N_META, BLK, BLK],
                    q_lo=BLK, wq=QB - BLK)
        c = tuple(jnp.concatenate([a, b], axis=1) for a, b in zip(lo, hi))
        inv =jnp.concatenate([jnp.broadcast_to(1.0 / c[1], (HEAD_DIM, QB)),
                               jnp.broadcast_to(1.0 / c[3], (HEAD_DIM, QB))], axis=0)
        o_t = (c[4] * inv).T.astype(BF16)
        lses = [c[2 * h] + jnp.log(c[2 * h + 1]) * LOG2E for h in range(2)]
        o_ref[pl.ds(r0, BLK), :] = o_t[0:BLK]
        for h in range(2):
            lse_ref[0, h:h + 1, pl.ds(r0, BLK)] = lses[h][:, 0:BLK]

        @pl.when(i > 0)
        def _():
            r1 = pl.multiple_of(r0 + BLK, BLK)
            o_ref[pl.ds(r1, QB - BLK), :] = o_t[BLK:QB]
            for h in range(2):
                lse_ref[0, h:h + 1, pl.ds(r1, QB - BLK)] = lses[h][:, BLK:QB]

    in_specs = [pl.BlockSpec((lp, qw), lambda p, i: (0, qcol + p)),
                pl.BlockSpec((lp, 128), lambda p, i: (0, kcol(p))),
                pl.BlockSpec((lp, 128), lambda p, i: (0, vcol(p)))]
    ins = [q, k, v]
    if rope:
        in_specs.append(pl.BlockSpec((lp, 128), lambda p, i: (0, 0)))
        ins.append(kr)
    if bias:
        in_specs.append(pl.BlockSpec((2, lp, 128), lambda p, i: (p, 0, 0)))
        ins.append(nbrep)
    return pl.pallas_call(
        body, name=name, grid=(PAIRS, nq), in_specs=in_specs,
        out_specs=[pl.BlockSpec((lp, 128), lambda p, i: (0, p)),
                   pl.BlockSpec((1, 2, lp), lambda p, i: (p, 0, 0))],
        out_shape=[jax.ShapeDtypeStruct((lp, D_MODEL), BF16), jax.ShapeDtypeStruct((PAIRS, 2, lp), F32)],
        compiler_params=_cp(("parallel", "arbitrary"), VMEM_BIG))(*ins)


def _attn_bwd(q, k, v, do, delta, lse, *, kr=None, rtabs=None, nbrep=None, scale, qcol, kcol, vcol, name):
    lp = q.shape[0]
    nb = lp // BLK
    rope = kr is not None
    bias = nbrep is not None
    qw = 256 if rope else 128

    def body(*refs):
        it = iter(refs)
        q_ref, k_ref, v_ref = next(it), next(it), next(it)
        kr_ref = next(it) if rope else None
        nb_ref = next(it) if bias else None
        do_ref, dl_ref, lse_ref = next(it), next(it), next(it)
        ct_ref, st_ref = (next(it), next(it)) if rope else (None, None)
        dq_out, dk_ref, dv_ref = next(it), next(it), next(it)
        x_ref = next(it)
        drow_ref = next(it) if bias else None
        dq_ref = next(it)
        kb = pl.program_id(1)
        mas, hmask = _pair_masks(rope)
        lane = lax.broadcasted_iota(jnp.int32, (1, LANES), 1)

        @pl.when(kb == 0)
        def _():
            dq_ref[...] = jnp.zeros_like(dq_ref)
            if bias:
                drow_ref[...] = jnp.zeros_like(drow_ref)

        def key_pass(n, w):
            kk = k_ref[0:n, :]
            if rope:
                kk = jnp.concatenate([kk, kr_ref[0:n, :]], axis=1)
            vh = _mask2(v_ref[0:n, :], mas)
            kcat = jnp.concatenate(_mask2(kk, hmask), axis=0)
            diag_mask = (lax.broadcasted_iota(jnp.int32, (n, w), 0) <= lax.broadcasted_iota(jnp.int32, (n, w), 1))

            def chunk(qc, carry, mask):
                carry = list(carry)
                q0 = qc * w if isinstance(qc, int) else pl.multiple_of(qc * w, w)
                dov = do_ref[pl.ds(q0, w), :]
                doh = _mask2(dov, mas)
                qh = _mask2(q_ref[pl.ds(q0, w), :], hmask)
                pbs, dss = [], []
                for h in range(2):
                    s = _dot(kk, qh[h], 1, 1)
                    if bias:
                        s = s + jnp.concatenate([nb_ref[h, 0:n, :]] * (w // LANES), axis=1)
                    p = jnp.exp2(s - lse_ref[0, h:h + 1, pl.ds(q0, w)])
                    if mask is not None:
                        p = jnp.where(mask, p, 0.0)
                    ds = p * (_dot(vh[h], dov, 1, 1) - dl_ref[0, h:h + 1, pl.ds(q0, w)])
                    if bias:
                        drow_ref[0, h:h + 1, pl.ds(q0, w)] += jnp.sum(ds, axis=0, keepdims=True)
                        carry[2 + h] = carry[2 + h] + jnp.sum(ds, axis=1, keepdims=True)
                    pbs.append(p.astype(BF16))
                    dss.append(ds.astype(BF16))
                ds_lanes = jnp.concatenate(dss, axis=1)
                ds_rows = jnp.concatenate(dss, axis=0)
                carry[0] = carry[0] + _dot(ds_lanes, jnp.concatenate(qh, axis=0), 1, 0)
                carry[1] = carry[1] + _dot(jnp.concatenate(pbs, axis=1), jnp.concatenate(doh, axis=0), 1, 0)
                dq_ref[pl.ds(q0, w), :] += _dot(ds_rows, kcat, 0, 0)
                return tuple(carry)

            c = [jnp.zeros((n, qw), F32), jnp.zeros((n, LANES), F32)]
            if bias:
                c += [jnp.zeros((n, 1), F32), jnp.zeros((n, 1), F32)]
            c = tuple(c)
            if w != BLK:
                for qc in range(lp // w):
                    c = chunk(qc, c, diag_mask if qc == 0 else None)
            else:
                groups = (nb - kb) // UNROLL

                def several(t, cr):
                    for u in range(UNROLL):
                        cr = chunk(kb + UNROLL * t + u, cr, (diag_mask | (t > 0)) if u == 0 else None)
                    return cr

                c = lax.fori_loop(0, groups, several, c)
                start = kb + UNROLL * groups
                pairs = (nb - start) // 2

                def two(t, cr):
                    qc = start + 2 * t
                    return chunk(qc + 1, chunk(qc, cr, diag_mask | (qc > kb)), None)

                c = lax.fori_loop(0, pairs, two, c)
                c = lax.fori_loop(start + 2 * pairs, nb, lambda qc, cr: chunk(qc, cr, diag_mask | (qc > kb)), c)

            def rows(a, dtype):
                a = a.astype(dtype)
                return a if n == BLK else jnp.concatenate([a, jnp.zeros((BLK - n, a.shape[1]), dtype)], axis=0)

            dk = c[0] * LN2
            dk_ref[...] = rows(dk[:, 0:LANES], BF16)
            dv_ref[...] = rows(c[1], BF16)
            if rope:
                x_ref[0] = rows(dk[:, LANES:2 * LANES], F32)
            if bias:
                x_ref[0] = rows(jnp.where(lane == 0, c[2], jnp.where(lane == 1, c[3], 0.0)), F32)

        @pl.when(kb == 0)
        def _():
            key_pass(N_META, lp // 2)

        @pl.when(kb > 0)
        def _():
            key_pass(BLK, BLK)

        @pl.when(kb == nb - 1)
        def _():
            def fin(c, carry):
                r0 = pl.multiple_of(c * BLK, BLK)
                dq = dq_ref[pl.ds(r0, BLK), :] * scale
                if rope:
                    back = _rope(dq[:, LANES:2 * LANES], ct_ref[pl.ds(r0, BLK), :], -st_ref[pl.ds(r0, BLK), :])
                    dq = jnp.concatenate([dq[:, 0:LANES], back], axis=1)
                dq_out[pl.ds(r0, BLK), :] = dq.astype(BF16)
                return carry

            lax.fori_loop(0, nb, fin, 0)

    in_specs = [pl.BlockSpec((lp, qw), lambda p, j: (0, qcol + p)),
                pl.BlockSpec((BLK, 128), lambda p, j: (j, kcol(p))),
                pl.BlockSpec((BLK, 128), lambda p, j: (j, vcol(p)))]
    ins = [q, k, v]
    if rope:
        in_specs.append(pl.BlockSpec((BLK, 128), lambda p, j: (j, 0)))
        ins.append(kr)
    if bias:
        in_specs.append(pl.BlockSpec((2, BLK, 128), lambda p, j: (p, j, 0)))
        ins.append(nbrep)
    in_specs += [pl.BlockSpec((lp, 128), lambda p, j: (0, p)), pl.BlockSpec((1, 2, lp), lambda p, j: (p, 0, 0)),
                 pl.BlockSpec((1, 2, lp), lambda p, j: (p, 0, 0))]
    ins += [do, delta, lse]
    if rope:
        in_specs += [pl.BlockSpec((lp, 128), lambda p, j: (0, 0))] * 2
        ins += list(rtabs)
    out_specs = [pl.BlockSpec((lp, qw), lambda p, j: (0, p)),
                 pl.BlockSpec((BLK, 128), lambda p, j: (j, p)),
                 pl.BlockSpec((BLK, 128), lambda p, j: (j, p)),
                 pl.BlockSpec((1, BLK, 128), lambda p, j: (p, j, 0))]
    out_shape = [jax.ShapeDtypeStruct((lp, PAIRS * qw), BF16), jax.ShapeDtypeStruct((lp, D_MODEL), BF16),
                 jax.ShapeDtypeStruct((lp, D_MODEL), BF16), jax.ShapeDtypeStruct((PAIRS, lp, 128), F32)]
    if bias:
        out_specs.append(pl.BlockSpec((1, 2, lp), lambda p, j: (p, 0, 0)))
        out_shape.append(jax.ShapeDtypeStruct((PAIRS, 2, lp), F32))
    return pl.pallas_call(
        body, name=name, grid=(PAIRS, nb), in_specs=in_specs, out_specs=out_specs, out_shape=out_shape,
        scratch_shapes=[pltpu.VMEM((lp, qw), F32)],
        compiler_params=_cp(("parallel", "arbitrary"), VMEM_BIG))(*ins)


def _adamw(w, g, m, v, name):
    lead = w.ndim - 2
    rows, cols = w.shape[lead:]
    big = rows * cols > 512 * 1024
    tr = 128 if big and rows % 128 == 0 else rows
    tc = 256 if big and tr == rows else cols

    def body(w_ref, g_ref, m_ref, v_ref, d_ref, nm_ref, nv_ref):
        gv = g_ref[...]
        nm = ADAM_B1 * m_ref[...] + (1.0 - ADAM_B1) * gv
        nv = ADAM_B2 * v_ref[...] + (1.0 - ADAM_B2) * (gv * gv)
        m_hat = nm / (1.0 - ADAM_B1 ** ADAM_STEP)
        v_hat = nv / (1.0 - ADAM_B2 ** ADAM_STEP)
        d_ref[...] = -ADAM_LR * (m_hat / (jnp.sqrt(v_hat) + ADAM_EPS) + ADAM_WD * w_ref[...])
        nm_ref[...] = nm
        nv_ref[...] = nv

    spec = pl.BlockSpec((1,) * lead + (tr, tc), lambda i, j: (0,) * lead + (i, j))
    return pl.pallas_call(
        body, name=name, grid=(rows // tr, cols // tc), in_specs=[spec] * 4, out_specs=[spec] * 3,
        out_shape=[jax.ShapeDtypeStruct(w.shape, F32)] * 3,
        compiler_params=_cp(("parallel", "parallel"), VMEM_BIG))(w, g, m, v)


def _add_cores(g, from_sib, name):
    n, rows, cols = g.shape
    half = rows // 2
    tr = _tile(half, (256, 240))
    nt = half // tr

    def body(lo_ref, hi_ref, s_ref, o_ref):
        mine = jnp.where(lax.axis_index("c") == 0, lo_ref[0], hi_ref[0])
        o_ref[0] = (mine.astype(F32) + s_ref[0].astype(F32)).astype(BF16)

    return pl.pallas_call(
        body, name=name, grid=(n, nt),
        in_specs=[pl.BlockSpec((1, tr, cols), lambda j, i: (j, i, 0)),
                  pl.BlockSpec((1, tr, cols), lambda j, i: (j, nt + i, 0)),
                  pl.BlockSpec((1, tr, cols), lambda j, i: (j, i, 0))],
        out_specs=pl.BlockSpec((1, tr, cols), lambda j, i: (j, i, 0)),
        out_shape=jax.ShapeDtypeStruct((n, half, cols), BF16),
        compiler_params=_cp(("parallel", "parallel"), VMEM_BIG))(g, g, from_sib)


def _add_chips(x, own, name):
    n, rows, cols = x.shape
    tr = _tile(rows, (256, 240))

    def body(x_ref, own_ref, o_ref):
        me = 2 * lax.axis_index("x") + lax.axis_index("y")
        v = [jnp.where(me == k, own_ref[...], x_ref[k]).astype(F32) for k in range(N_CHIPS)]
        o_ref[...] = ((v[0] + v[1]) + v[2]) + v[3]

    return pl.pallas_call(
        body, name=name, grid=(rows // tr,),
        in_specs=[pl.BlockSpec((n, tr, cols), lambda i: (0, i, 0)), pl.BlockSpec((tr, cols), lambda i: (i, 0))],
        out_specs=pl.BlockSpec((tr, cols), lambda i: (i, 0)),
        out_shape=jax.ShapeDtypeStruct((rows, cols), F32), compiler_params=_cp(("parallel",), VMEM_BIG))(x, own)


def _axes():
    return lax.axis_index("x"), lax.axis_index("y"), lax.axis_index("c")


def _other_chips(x, y):
    return [(1 - x, y), (x, 1 - y), (1 - x, 1 - y)]


ANY = pl.BlockSpec(memory_space=pl.ANY)


def _rcopy(src, dst, send_sems, recv_sems, k, to):
    return pltpu.make_async_remote_copy(src_ref=src, dst_ref=dst, send_sem=send_sems.at[k], recv_sem=recv_sems.at[k],
                                        device_id=to, device_id_type=MESH)


def _gather_weights(shards, meta):
    n = len(shards)

    def body(*refs):
        srcs, meta_ref = refs[:n], refs[n]
        outs, mout_ref = refs[n + 1:2 * n + 1], refs[2 * n + 1]
        send_sems, recv_sems = refs[2 * n + 2:]
        x, y, c = _axes()
        me = 2 * x + y
        sib = (x, y, 1 - c)
        chips = _other_chips(x, y)

        def half(t, chip_idx, cc):
            hr = shards[t].shape[0] // 2
            return outs[t].at[chip_idx, pl.ds(cc * hr, hr), :]

        first = []
        for j, (px, py) in enumerate(chips):
            for t in range(n):
                hr = shards[t].shape[0] // 2
                first.append(_rcopy(srcs[t].at[pl.ds(c * hr, hr), :], half(t, me, c), send_sems, recv_sems,
                                    3 * t + j, (px, py, c)))
            first.append(_rcopy(meta_ref, mout_ref.at[me], send_sems, recv_sems, 3 * n + j, (px, py, c)))
        for cp in first:
            cp.start()
        passed = []
        for j, (px, py) in enumerate(chips):
            src_chip = 2 * px + py
            for t in range(n):
                _rcopy(half(t, src_chip, c), half(t, src_chip, c), send_sems, recv_sems, 3 * t + j, sib).wait_recv()
                fwd = _rcopy(half(t, src_chip, c), half(t, src_chip, c), send_sems, recv_sems, 3 * (n + 1 + t) + j, sib)
                fwd.start()
                passed.append(fwd)
            _rcopy(mout_ref.at[src_chip], mout_ref.at[src_chip], send_sems, recv_sems, 3 * n + j, sib).wait_recv()
        for j, (px, py) in enumerate(chips):
            src_chip = 2 * px + py
            for t in range(n):
                _rcopy(half(t, src_chip, 1 - c), half(t, src_chip, 1 - c), send_sems, recv_sems,
                       3 * (n + 1 + t) + j, sib).wait_recv()
        for cp in first + passed:
            cp.wait_send()

    nsem = 3 * (2 * n + 1)
    return pl.pallas_call(
        body, name="gather_weights", in_specs=[ANY] * (n + 1), out_specs=[ANY] * (n + 1),
        out_shape=[jax.ShapeDtypeStruct((N_CHIPS,) + s.shape, s.dtype) for s in shards]
        + [jax.ShapeDtypeStruct((N_CHIPS,) + meta.shape, meta.dtype)],
        scratch_shapes=[pltpu.SemaphoreType.DMA((nsem,)), pltpu.SemaphoreType.DMA((nsem,))])(*shards, meta)


def _gather_late(shard):
    rows, cols = shard.shape
    hr = rows // 2
    src = jax.new_ref(shard, memory_space=pltpu.MemorySpace.HBM)
    out = jax.empty_ref(jax.ShapeDtypeStruct((N_CHIPS, rows, cols), shard.dtype), memory_space=pltpu.MemorySpace.HBM)

    @pl.kernel(mesh=plsc.ScalarSubcoreMesh(axis_name="seq", num_cores=1), name="gather_late",
               scratch_types=(pltpu.SemaphoreType.DMA((6,)), pltpu.SemaphoreType.DMA((6,))),
               compiler_params=pltpu.CompilerParams(collective_id=1))
    def launch(send_sems, recv_sems):
        x, y, c = _axes()
        me = 2 * x + y
        sib = (x, y, 1 - c)
        chips = _other_chips(x, y)
        barrier = pltpu.get_barrier_semaphore()
        for px, py in chips:
            pl.semaphore_signal(barrier, inc=1, device_id=(px, py, c), device_id_type=MESH)
        pl.semaphore_signal(barrier, inc=1, device_id=sib, device_id_type=MESH)
        pl.semaphore_wait(barrier, 4)

        def half(chip_idx, cc):
            return out.at[chip_idx, pl.ds(cc * hr, hr), :]

        first = [_rcopy(src.at[pl.ds(c * hr, hr), :], half(me, c), send_sems, recv_sems, j, (px, py, c))
                 for j, (px, py) in enumerate(chips)]
        for cp in first:
            cp.start()
        passed = []
        for j, (px, py) in enumerate(chips):
            land = half(2 * px + py, c)
            _rcopy(land, land, send_sems, recv_sems, j, sib).wait_recv()
            fwd = _rcopy(land, land, send_sems, recv_sems, 3 + j, sib)
            fwd.start()
            passed.append(fwd)
        for j, (px, py) in enumerate(chips):
            land = half(2 * px + py, 1 - c)
            _rcopy(land, land, send_sems, recv_sems, 3 + j, sib).wait_recv()
        for cp in first + passed:
            cp.wait_send()

    launch()
    return out[...]


def _swap_halves(gs):
    n = len(gs)
    ncopies = sum(g.shape[0] for g in gs)

    def body(*refs):
        srcs, outs = refs[:n], refs[n:2 * n]
        send_sems, recv_sems = refs[2 * n:]
        x, y, c = _axes()
        cps = []
        for t in range(n):
            hr = gs[t].shape[1] // 2
            for j in range(gs[t].shape[0]):
                cps.append(_rcopy(srcs[t].at[j, pl.ds((1 - c) * hr, hr), :], outs[t].at[j], send_sems, recv_sems,
                                  len(cps), (x, y, 1 - c)))
        for cp in cps:
            cp.start()
        for cp in cps:
            cp.wait()

    return pl.pallas_call(
        body, name="swap_halves", in_specs=[ANY] * n, out_specs=[ANY] * n,
        out_shape=[jax.ShapeDtypeStruct((g.shape[0], g.shape[1] // 2, g.shape[2]), g.dtype) for g in gs],
        scratch_shapes=[pltpu.SemaphoreType.DMA((ncopies,)), pltpu.SemaphoreType.DMA((ncopies,))])(*gs)


def _scatter_chips(parts):
    n = len(parts)
    srcs = [jax.new_ref(p, memory_space=pltpu.MemorySpace.HBM) for p in parts]
    outs = [jax.empty_ref(jax.ShapeDtypeStruct(p.shape, p.dtype), memory_space=pltpu.MemorySpace.HBM) for p in parts]

    @pl.kernel(mesh=plsc.ScalarSubcoreMesh(axis_name="seq", num_cores=1), name="scatter_chips",
               scratch_types=(pltpu.SemaphoreType.DMA((3 * n,)), pltpu.SemaphoreType.DMA((3 * n,))),
               compiler_params=pltpu.CompilerParams(collective_id=0))
    def launch(send_sems, recv_sems):
        x, y, c = _axes()
        me = 2 * x + y
        chips = _other_chips(x, y)
        barrier = pltpu.get_barrier_semaphore()
        for px, py in chips:
            pl.semaphore_signal(barrier, inc=1, device_id=(px, py, c), device_id_type=MESH)
        pl.semaphore_wait(barrier, 3)
        cps = []
        for j, (px, py) in enumerate(chips):
            for t in range(n):
                cps.append(_rcopy(srcs[t].at[2 * px + py], outs[t].at[me], send_sems, recv_sems, 3 * t + j,
                                  (px, py, c)))
        for cp in cps:
            cp.start()
        for cp in cps:
            cp.wait()

    launch()
    return [o[...] for o in outs]


def _swap_reduced(rs):
    n = len(rs)

    def body(*refs):
        srcs, outs = refs[:n], refs[n:2 * n]
        send_sems, recv_sems = refs[2 * n:]
        x, y, c = _axes()
        cps = [_rcopy(srcs[t], outs[t], send_sems, recv_sems, t, (x, y, 1 - c)) for t in range(n)]
        for cp in cps:
            cp.start()
        for cp in cps:
            cp.wait()

    return pl.pallas_call(
        body, name="swap_reduced", in_specs=[ANY] * n, out_specs=[ANY] * n,
        out_shape=[jax.ShapeDtypeStruct(r.shape, r.dtype) for r in rs],
        scratch_shapes=[pltpu.SemaphoreType.DMA((n,)), pltpu.SemaphoreType.DMA((n,))])(*rs)


SMALL_ROWS = 24 + 128


def _allreduce_small(vec):
    def body(v_ref, out_ref, slots, send_sems, recv_sems):
        x, y, c = _axes()
        me = 4 * x + 2 * y + c
        slots[me] = v_ref[...]
        cps = []
        for k in range(1, 8):
            kx, ky, kc = (k >> 2) & 1, (k >> 1) & 1, k & 1
            peer = (1 - x if kx else x, 1 - y if ky else y, 1 - c if kc else c)
            cps.append(_rcopy(v_ref, slots.at[me], send_sems, recv_sems, k - 1, peer))
        for cp in cps:
            cp.start()
        for cp in cps:
            cp.wait()
        tot = slots[0]
        for k in range(1, 8):
            tot = tot + slots[k]
        out_ref[...] = tot

    return pl.pallas_call(
        body, name="allreduce_small",
        in_specs=[pl.BlockSpec(memory_space=pltpu.VMEM)], out_specs=pl.BlockSpec(memory_space=pltpu.VMEM),
        out_shape=jax.ShapeDtypeStruct((SMALL_ROWS, 128), F32),
        scratch_shapes=[pltpu.VMEM((8, SMALL_ROWS, 128), F32), pltpu.SemaphoreType.DMA((7,)),
                        pltpu.SemaphoreType.DMA((7,))])(vec)


def _pack_p2(w_uq, w_ukv, w_br_mla, w_br_fox, w_out, dtype):
    parts = [w_uq.reshape(96, D_MODEL), w_ukv.reshape(64, D_MODEL), w_br_mla, w_br_fox, w_out]
    return jnp.concatenate([p.astype(dtype) for p in parts], axis=0)


def _unpack_p2(pk):
    return pk[0:96].reshape(256, 384), pk[96:160].reshape(128, 512), pk[160:416], pk[416:672], pk[672:928]


def _uq_arrange(w):
    w3 = w.reshape(256, HEADS, 96)
    nope = w3[:, :, :64].reshape(256, PAIRS, 128)
    pe = w3[:, :, 64:].reshape(256, PAIRS, 64)
    return jnp.concatenate([nope, pe, jnp.zeros((256, PAIRS, 64), w.dtype)], axis=2).reshape(256, PAIRS * 256)


def _uq_restore(g):
    g3 = g.reshape(256, PAIRS, 256)
    nope = g3[:, :, :128].reshape(256, HEADS, 64)
    pe = g3[:, :, 128:192].reshape(256, HEADS, 32)
    return jnp.concatenate([nope, pe], axis=2).reshape(256, HEADS * 96)


def _ukv_arrange(w):
    w3 = w.reshape(128, HEADS, 128)
    return jnp.concatenate([w3[:, :, :64].reshape(128, 1024), w3[:, :, 64:].reshape(128, 1024)], axis=1)


def _ukv_restore(g):
    kn = g[:, :1024].reshape(128, HEADS, 64)
    vv = g[:, 1024:].reshape(128, HEADS, 64)
    return jnp.concatenate([kn, vv], axis=2).reshape(128, HEADS * 128)


def _rope_tables(lp):
    r = np.arange(lp)
    pos = np.where(r < N_META, r, np.where(r >= PAD, r - PAD + N_META, 0)).astype(np.float32)
    half = MLA_ROPE // 2
    inv_freq = np.float32(ROPE_THETA) ** (-np.arange(half, dtype=np.float32) / np.float32(half))
    ang = (pos[:, None] * inv_freq[None, :]).astype(np.float32)
    cos, sin = np.cos(ang).astype(np.float32), np.sin(ang).astype(np.float32)
    one, zero = np.ones((lp, 64), np.float32), np.zeros((lp, 64), np.float32)
    return (jnp.asarray(np.concatenate([cos, cos, cos, cos, one], axis=1)),
            jnp.asarray(np.concatenate([-sin, sin, -sin, sin, zero], axis=1)))


def _pad_lanes(v, n=128):
    return jnp.pad(v, ((0, 0), (0, n - v.shape[1])))


def _in_cols(slabs, a, b):
    out = []
    for j in range(N_CHIPS):
        lo, hi = max(a, W_IN_SHARD * j), min(b, W_IN_SHARD * (j + 1))
        if lo < hi:
            out.append(slabs[j][:, lo - W_IN_SHARD * j:hi - W_IN_SHARD * j])
    return out


def _local_step(x2, tgt2, meta_f, w_small, w_attn, w_gate, w_uq_f, w_ukv_f, w_bm, w_bf, w_o, pre_norm_g,
                post_norm_g, mla_q_norm_g, mla_kv_norm_g, fox_forget_b, start_exchange=None):
    s_rows = x2.shape[0]
    lp = PAD + s_rows
    w_uq_a = _uq_arrange(w_uq_f)
    w_ukv_a = _ukv_arrange(w_ukv_f)

    ctab, stab = _rope_tables(lp)
    ii = jnp.arange(BLK)
    tri_lo = (ii[:, None] >= ii[None, :]).astype(BF16)
    tri_up = (ii[:, None] <= ii[None, :]).astype(BF16)
    fb128 = _pad_lanes(fox_forget_b)

    u = _rms_pre(x2, meta_f, pre_norm_g)
    small = _mm(u, w_small, mode="nn", out_dtype=F32, name="proj_small")
    attn = _mm(u, w_attn, mode="nn", out_dtype=BF16, name="proj_attn",
               col_scale=(HEADS * HEAD_DIM, FOX_SCALE * LOG2E))
    gate = _mm(u, w_gate, mode="nn", out_dtype=BF16, name="proj_gate")
    qn, kvn, kr, ncum = _small_prep(small, mla_q_norm_g, mla_kv_norm_g, fb128, ctab, stab, tri_lo)
    qcat = _mm(qn, w_uq_a, mode="nn", out_dtype=BF16, name="mla_q", row_ins=(ctab, stab),
               epilogue=lambda tile, c, s: _rope_pairs(tile, c, s) * (MLA_SCALE * LOG2E))
    kv = _mm(kvn, w_ukv_a, mode="nn", out_dtype=BF16, name="mla_kv")
    nbrep = jnp.broadcast_to((ncum[:, :HEADS].T * LOG2E)[:, :, None], (HEADS, lp, LANES))

    mla_cols = dict(qcol=0, kcol=lambda p: p, vcol=lambda p: PAIRS + p)
    fox_cols = dict(qcol=0, kcol=lambda p: PAIRS + p, vcol=lambda p: 2 * PAIRS + p)
    o_mla, lse_mla = _attn_fwd(qcat, kv, kv, kr=kr, name="mla_fwd", **mla_cols)
    o_fox, lse_fox = _attn_fwd(attn, attn, attn, nbrep=nbrep, name="fox_fwd", **fox_cols)

    a_mla, a_fox = _gate_fwd(o_mla, o_fox, gate)
    y_mla = _mm(a_mla, w_bm, mode="nn", out_dtype=BF16, name="br_mla")
    y_fox = _mm(a_fox, w_bf, mode="nn", out_dtype=BF16, name="br_fox")
    mg = _merge_fwd(gate, y_mla, y_fox)
    mixed = _mm(mg, w_o, mode="nn", out_dtype=F32, name="out_proj")
    dmixed, dy, loss_p, dg_post = _tail(x2, mixed, tgt2, post_norm_g)

    d_w_out = _mm(mg, dmixed, mode="tn", out_dtype=F32, name="d_w_out")
    dm = _mm(dmixed, w_o, mode="nt", out_dtype=BF16, name="d_merge")
    dy_mla, dy_fox, dgate_ab = _merge_bwd(dm, gate, y_mla, y_fox)
    d_w_bm = _mm(a_mla, dy_mla, mode="tn", out_dtype=F32, name="d_w_br_mla")
    d_w_bf = _mm(a_fox, dy_fox, mode="tn", out_dtype=F32, name="d_w_br_fox")
    da_mla = _mm(dy_mla, w_bm, mode="nt", out_dtype=BF16, name="d_a_mla")
    da_fox = _mm(dy_fox, w_bf, mode="nt", out_dtype=BF16, name="d_a_fox")
    do_mla, do_fox, dgate_z, dl_mla, dl_fox = _gate_bwd(da_mla, da_fox, o_mla, o_fox, gate)
    dl_mla, dl_fox = (d[:, :HEADS].T.reshape(PAIRS, 2, lp) for d in (dl_mla, dl_fox))

    dq_a, dkn, dvm, dkr = _attn_bwd(qcat, kv, kv, do_mla, dl_mla, lse_mla, kr=kr, rtabs=(ctab, stab),
                                    scale=MLA_SCALE, name="mla_bwd", **mla_cols)
    dfq, dfk, dfv, dcol, drow = _attn_bwd(attn, attn, attn, do_fox, dl_fox, lse_fox, nbrep=nbrep, scale=FOX_SCALE,
                                          name="fox_bwd", **fox_cols)

    d_w_uq_a = _mm(qn, dq_a, mode="tn", out_dtype=F32, name="d_w_uq")
    dqn = _mm(dq_a, w_uq_a, mode="nt", out_dtype=F32, name="d_qn")
    d_w_ukv_a = jnp.concatenate([_mm(kvn, dkn, mode="tn", out_dtype=F32, name="d_w_uk"),
                                 _mm(kvn, dvm, mode="tn", out_dtype=F32, name="d_w_uv")], axis=1)
    dkvn = _mm(dkn, w_ukv_a[:, :1024], mode="nt", out_dtype=F32, name="d_kvn_k")
    dkvn = _mm(dvm, w_ukv_a[:, 1024:], mode="nt", out_dtype=F32, name="d_kvn_v", acc=dkvn)
    dsmall, dg_q, dg_kv, dfb = _small_bwd(small, dqn, dkvn, dkr, dcol, drow, mla_q_norm_g, mla_kv_norm_g,
                                          fb128, ctab, stab, tri_up)

    dw_small = _mm(u, dsmall, mode="tn", out_dtype=BF16, name="d_w_small")
    dw_fq = _mm(u, dfq, mode="tn", out_dtype=BF16, name="d_w_fq")
    dw_fk = _mm(u, dfk, mode="tn", out_dtype=BF16, name="d_w_fk")
    dw_fv = _mm(u, dfv, mode="tn", out_dtype=BF16, name="d_w_fv")
    dw_z = _mm(u, dgate_z, mode="tn", out_dtype=BF16, name="d_w_z")
    dw_g = _mm(u, dgate_ab, mode="tn", out_dtype=BF16, name="d_w_g")
    d_w_in = (dw_small, dw_z, dw_fq, dw_fk, dw_fv, dw_g)
    d_w_uq = _uq_restore(d_w_uq_a)
    d_w_ukv = _ukv_restore(d_w_ukv_a)
    token = start_exchange(d_w_in, d_w_uq, d_w_ukv, d_w_bm, d_w_bf, d_w_out) if start_exchange else None
    du = _mm_sum_nt([(dsmall, w_small), (dfq, w_attn[:, 0:1024]), (dfk, w_attn[:, 1024:2048]),
                     (dfv, w_attn[:, 2048:3072]), (dgate_z, w_gate[:, 0:2048]), (dgate_ab, w_gate[:, 2048:4096])],
                    name="d_u", after=token)
    dx, dmeta, dg_pre = _pre_bwd(du, x2, meta_f, dy, pre_norm_g)
    return (loss_p, dx, dmeta, d_w_in, d_w_uq, d_w_ukv, d_w_bm, d_w_bf, d_w_out, dg_pre, dg_post, dg_q, dg_kv, dfb)


def _w_in_slabs(pieces):
    dw_small, dw_z, dw_fq, dw_fk, dw_fv, dw_g = pieces
    runs = [(dw_small[:, 0:416], C_CQ), (dw_z[:, 0:1024], C_ZMLA), (dw_fq, C_FQ), (dw_fk, C_FK), (dw_fv, C_FV),
            (dw_small[:, 512:528], C_FL), (dw_z[:, 1024:2048], C_ZFOX), (dw_g, C_GA)]
    slabs = []
    for j in range(N_CHIPS):
        lo, hi = W_IN_SHARD * j, W_IN_SHARD * (j + 1)
        cols = [a[:, max(lo, c0) - c0:min(hi, c0 + a.shape[1]) - c0] for a, c0 in runs
                if max(lo, c0) < min(hi, c0 + a.shape[1])]
        slabs.append(jnp.concatenate(cols, axis=1))
    return jnp.stack(slabs, axis=0)


def kernel(x, meta_tokens, pre_norm_g, w_in, fox_forget_b, mla_q_norm_g, mla_kv_norm_g, w_uq, w_ukv, w_br_mla, w_br_fox, w_out, post_norm_g, loss_target, m_meta_tokens, m_pre_norm_g, m_w_in, m_fox_forget_b, m_mla_q_norm_g, m_mla_kv_norm_g, m_w_uq, m_w_ukv, m_w_br_mla, m_w_br_fox, m_w_out, m_post_norm_g, v_meta_tokens, v_pre_norm_g, v_w_in, v_fox_forget_b, v_mla_q_norm_g, v_mla_kv_norm_g, v_w_uq, v_w_ukv, v_w_br_mla, v_w_br_fox, v_w_out, v_post_norm_g):
    me = 2 * lax.axis_index("x") + lax.axis_index("y")
    core = lax.axis_index("c")
    w_in_b = w_in.astype(BF16).reshape(D_MODEL, W_IN_SHARD)
    p2 = _pack_p2(w_uq[0], w_ukv[0], w_br_mla[0], w_br_fox[0], w_out[0], BF16)
    w_in_g, meta_g = _gather_weights([w_in_b], meta_tokens)
    p2_g = _gather_late(lax.optimization_barrier((p2, w_in_g))[0])
    slabs = [jnp.where(me == j, w_in_b, w_in_g[j]) for j in range(N_CHIPS)]
    chip = lax.broadcasted_iota(jnp.int32, (N_CHIPS, 1, 1), 0)
    p2_all = jnp.where(chip == me, p2[None], p2_g)
    w_uq_f = p2_all[:, 0:96].reshape(N_CHIPS, 256, 384).transpose(1, 0, 2).reshape(256, 1536)
    w_ukv_f = p2_all[:, 96:160].reshape(N_CHIPS, 128, 512).transpose(1, 0, 2).reshape(128, 2048)
    w_bm, w_bf, w_o = (p2_all[:, lo:lo + 256].reshape(D_MODEL, D_MODEL) for lo in (160, 416, 672))
    meta_f = jnp.where(chip == me, meta_tokens[None], meta_g).transpose(1, 0, 2).reshape(N_META, D_MODEL)
    kpe = _in_cols(slabs, C_KPE, C_ZMLA)
    w_small = jnp.concatenate(_in_cols(slabs, C_CQ, C_KPE) + kpe + kpe + [jnp.zeros((D_MODEL, 64), BF16)]
                              + _in_cols(slabs, C_FL, C_ZFOX) + [jnp.zeros((D_MODEL, 112), BF16)], axis=1)
    w_attn = jnp.concatenate(_in_cols(slabs, C_FQ, C_FL), axis=1)
    w_gate = jnp.concatenate(_in_cols(slabs, C_ZMLA, C_FQ) + _in_cols(slabs, C_ZFOX, C_END), axis=1)

    exchange = {}

    def start_exchange(d_w_in, d_w_uq, d_w_ukv, d_w_bm, d_w_bf, d_w_out):
        g2 = jnp.concatenate(
            [d_w_uq.reshape(256, N_CHIPS, 384).transpose(1, 0, 2).reshape(N_CHIPS, 96, D_MODEL),
             d_w_ukv.reshape(128, N_CHIPS, 512).transpose(1, 0, 2).reshape(N_CHIPS, 64, D_MODEL)]
            + [g.reshape(N_CHIPS, 256, D_MODEL) for g in (d_w_bm, d_w_bf, d_w_out)], axis=1)
        pieces = [p[None] for p in d_w_in]
        from_sib = _swap_halves(pieces + [g2])
        halves = [_add_cores(p, s, "add_cores_" + nm)[0]
                  for p, s, nm in zip(pieces, from_sib, ("small", "z", "fq", "fk", "fv", "g"))]
        parts = [_w_in_slabs(halves), _add_cores(g2, from_sib[-1], "add_cores_rest")]
        exchange.update(parts=parts, landed=_scatter_chips(parts))
        return parts[0][0, 0:16, 0:LANES]

    (loss_p, dx, dmeta, _, _, _, _, _, _, dg_pre, dg_post, dg_q, dg_kv,
     dfb) = _local_step(x[0], loss_target[0], meta_f, w_small, w_attn, w_gate, w_uq_f, w_ukv_f, w_bm, w_bf, w_o,
                        pre_norm_g, post_norm_g, mla_q_norm_g, mla_kv_norm_g, fox_forget_b, start_exchange)

    mine = [_add_chips(l, lax.dynamic_index_in_dim(p, me, 0, keepdims=False), nm)
            for l, p, nm in zip(exchange["landed"], exchange["parts"], ("add_chips_w_in", "add_chips_rest"))]
    theirs = _swap_reduced(mine)
    g_w_in, g_p2 = [jnp.concatenate([jnp.where(core == 0, a, b), jnp.where(core == 0, b, a)], axis=0)
                    for a, b in zip(mine, theirs)]
    g_w_uq, g_w_ukv, g_w_bm, g_w_bf, g_w_out = _unpack_p2(g_p2)
    g_w_in = g_w_in[None]

    vec = jnp.concatenate([dg_pre.reshape(8, 128), dg_post.reshape(8, 128), dg_q.reshape(2, 128), dg_kv,
                           dfb, _pad_lanes(loss_p), jnp.zeros((3, 128), F32), dmeta.reshape(128, 128)], axis=0)
    tot = _allreduce_small(vec)
    loss = tot[20, 0]
    g_meta = lax.dynamic_slice_in_dim(tot[24:].reshape(N_META, D_MODEL), 256 * me, 256, axis=1)

    def small_pack(pre, post, gq_, gkv_, fb_):
        return jnp.concatenate([pre.reshape(8, 128), post.reshape(8, 128), gq_.reshape(2, 128), gkv_,
                                _pad_lanes(fb_), jnp.zeros((4, 128), F32)], axis=0)

    def small_unpack(t):
        return (t[0:8].reshape(1, 1024), t[8:16].reshape(1, 1024), t[16:18].reshape(1, 256), t[18:19],
                t[19:20, 0:HEADS])

    g_small = jnp.concatenate([tot[0:20], jnp.zeros((4, 128), F32)], axis=0)
    sm = _adamw(small_pack(pre_norm_g, post_norm_g, mla_q_norm_g, mla_kv_norm_g, fox_forget_b), g_small,
                small_pack(m_pre_norm_g, m_post_norm_g, m_mla_q_norm_g, m_mla_kv_norm_g, m_fox_forget_b),
                small_pack(v_pre_norm_g, v_post_norm_g, v_mla_q_norm_g, v_mla_kv_norm_g, v_fox_forget_b),
                "adamw_small")
    g_pre, g_post, g_q, g_kv, g_fb = small_unpack(g_small)
    (d_pre, d_post, d_q, d_kv, d_fb), (nm_pre, nm_post, nm_q, nm_kv, nm_fb), (nv_pre, nv_post, nv_q, nv_kv, nv_fb) = (
        small_unpack(t) for t in sm)

    d_meta, nm_meta, nv_meta = _adamw(meta_tokens, g_meta, m_meta_tokens, v_meta_tokens, "adamw_meta")
    d_win, nm_win, nv_win = (t.T[None] for t in _adamw(w_in[0].T, g_w_in[0].T, m_w_in[0].T, v_w_in[0].T,
                                                       "adamw_w_in"))
    d_wuq, nm_wuq, nv_wuq = _adamw(w_uq[0], g_w_uq, m_w_uq[0], v_w_uq[0], "adamw_w_uq")
    d_wukv, nm_wukv, nv_wukv = _adamw(w_ukv[0], g_w_ukv, m_w_ukv[0], v_w_ukv[0], "adamw_w_ukv")
    d_wbm, nm_wbm, nv_wbm = _adamw(w_br_mla[0], g_w_bm, m_w_br_mla[0], v_w_br_mla[0], "adamw_w_br_mla")
    d_wbf, nm_wbf, nv_wbf = _adamw(w_br_fox[0], g_w_bf, m_w_br_fox[0], v_w_br_fox[0], "adamw_w_br_fox")
    d_wo, nm_wo, nv_wo = _adamw(w_out[0], g_w_out, m_w_out[0], v_w_out[0], "adamw_w_out")

    def group(meta_, pre, win, fb_, q_, kv_, wuq, wukv, wbm, wbf, wo, post):
        return (meta_, pre, win, fb_, q_, kv_, wuq[None], wukv[None], wbm[None], wbf[None], wo[None], post)

    grads = group(g_meta, g_pre, g_w_in, g_fb, g_q, g_kv, g_w_uq, g_w_ukv, g_w_bm, g_w_bf, g_w_out, g_post)
    deltas = group(d_meta, d_pre, d_win, d_fb, d_q, d_kv, d_wuq, d_wukv, d_wbm, d_wbf, d_wo, d_post)
    new_m = group(nm_meta, nm_pre, nm_win, nm_fb, nm_q, nm_kv, nm_wuq, nm_wukv, nm_wbm, nm_wbf, nm_wo, nm_post)
    new_v = group(nv_meta, nv_pre, nv_win, nv_fb, nv_q, nv_kv, nv_wuq, nv_wukv, nv_wbm, nv_wbf, nv_wo, nv_post)
    return (loss, dx[None], *grads, *deltas, *new_m, *new_v)
```

```python
import math

import jax
import jax.numpy as jnp
import numpy as np
from jax import lax
from jax.experimental import pallas as pl
from jax.experimental.pallas import tpu as pltpu
from jax.experimental.pallas import tpu_sc as plsc

F32 = jnp.float32
BF16 = jnp.bfloat16

D_MODEL = 1024
N_META = 16
RMS_EPS = 1e-6
HEADS = 16
PAIRS = HEADS // 2
HEAD_DIM = 64
LANES = 128
MLA_ROPE = 32
BIAS_PARTS = 3
MLA_SCALE = 1.0 / math.sqrt(64 + 32)
FOX_SCALE = 1.0 / math.sqrt(64)
LOG2E = math.log2(math.e)
LN2 = math.log(2.0)
ROPE_THETA = 10000.0

PAD = 256
BLK = 256
QB = 512
UNROLL = 4
NEG = -1e30

C_CQ, C_CKV, C_KPE, C_ZMLA, C_FQ, C_FK, C_FV, C_FL, C_ZFOX, C_GA, C_GB, C_END = (
    0, 256, 384, 416, 1440, 2464, 3488, 4512, 4528, 5552, 6576, 7600)
SMALL_W = 640
W_IN_SHARD = 1900

P2_ROWS = 928
N_CHIPS = 4

ADAM_LR = 0.001
ADAM_B1 = 0.9
ADAM_B2 = 0.999
ADAM_EPS = 1e-08
ADAM_WD = 0.01
ADAM_STEP = 10

VMEM_BIG = 56 * 1024 * 1024
MM_VMEM_BUDGET = 44 * 1024 * 1024
MESH = pl.DeviceIdType.MESH


def _cp(dims, vmem=None):
    return pltpu.CompilerParams(dimension_semantics=dims, vmem_limit_bytes=vmem)


def _dot(a, b, ca, cb):
    return lax.dot_general(a, b, (((ca,), (cb,)), ((), ())), preferred_element_type=F32)


def _sigmoid(x):
    return 1.0 / (1.0 + jnp.exp(-x))


def _tile(n, cands):
    for c in cands:
        if n % c == 0:
            return c
    return n


def _mm(a, b, *, mode, out_dtype, name, acc=None, epilogue=None, row_ins=(), after=None, col_scale=None):
    if mode == "nn":
        (M, K), N = a.shape, b.shape[1]
    elif mode == "nt":
        (M, K), N = a.shape, b.shape[0]
    else:
        (K, M), N = a.shape, b.shape[1]
    tm = _tile(M, (1088, 1024)) if M > 1024 else M
    tn = _tile(N, (1024,)) if N > 1024 else N
    nk = 1
    while True:
        tk = K // nk
        need = 2 * tk * (tm * a.dtype.itemsize + tn * b.dtype.itemsize) + tm * tn * (
            2 * jnp.dtype(out_dtype).itemsize + (8 if acc is not None else 0) + (4 if nk > 1 else 0))
        if need <= MM_VMEM_BUDGET or (tk // 2) % (16 if mode == "tn" else LANES) or tk <= 512:
            break
        nk *= 2
    while (M // tm) * (N // tn) * nk < 4 and tn % 512 == 0:
        tn //= 2
    assert col_scale is None or (nk == 1 and col_scale[0] % tn == 0)
    ca, cb = {"nn": (1, 0), "nt": (1, 1), "tn": (0, 0)}[mode]
    a_spec = (pl.BlockSpec((tk, tm), lambda j, i, k: (k, i)) if mode == "tn"
              else pl.BlockSpec((tm, tk), lambda j, i, k: (i, k)))
    b_spec = (pl.BlockSpec((tn, tk), lambda j, i, k: (j, k)) if mode == "nt"
              else pl.BlockSpec((tk, tn), lambda j, i, k: (k, j)))
    o_spec = pl.BlockSpec((tm, tn), lambda j, i, k: (i, j))
    has_acc = acc is not None

    nrow = len(row_ins)

    def body(*refs):
        a_ref, b_ref = refs[0], refs[1]
        acc_ref = refs[2] if has_acc else None
        rows = refs[2 + has_acc:2 + has_acc + nrow]
        o_ref = refs[2 + has_acc + nrow + (after is not None)]

        def store(tile):
            if epilogue is not None:
                tile = epilogue(tile, *[r[...] for r in rows])
            if col_scale is not None:
                tile = tile * jnp.where(pl.program_id(0) * tn < col_scale[0], col_scale[1], 1.0)
            o_ref[...] = tile.astype(out_dtype)

        part = _dot(a_ref[...].astype(BF16), b_ref[...].astype(BF16), ca, cb)
        if nk == 1:
            store(part + acc_ref[...] if has_acc else part)
        else:
            sc = refs[-1]
            k = pl.program_id(2)

            @pl.when(k == 0)
            def _():
                sc[...] = part + acc_ref[...] if has_acc else part

            @pl.when(k > 0)
            def _():
                sc[...] += part

            @pl.when(k == nk - 1)
            def _():
                store(sc[...])

    ins = [a, b] + ([acc] if has_acc else []) + list(row_ins)
    in_specs = ([a_spec, b_spec] + ([o_spec] if has_acc else [])
                + [pl.BlockSpec((tm, r.shape[1]), lambda j, i, k: (i, 0)) for r in row_ins])
    if after is not None:
        ins.append(after)
        in_specs.append(pl.BlockSpec(after.shape, lambda j, i, k: (0,) * after.ndim))
    return pl.pallas_call(
        body, name=name, grid=(N // tn, M // tm, nk), in_specs=in_specs, out_specs=o_spec,
        out_shape=jax.ShapeDtypeStruct((M, N), out_dtype),
        scratch_shapes=[pltpu.VMEM((tm, tn), F32)] if nk > 1 else [],
        compiler_params=_cp(("parallel", "parallel", "arbitrary"), VMEM_BIG))(*ins)


def _mm_sum_nt(pairs, *, name, after=None):
    n = len(pairs)
    M, N = pairs[0][0].shape[0], pairs[0][1].shape[0]
    tm = _tile(M, (272,))

    def body(*refs):
        o_ref = refs[2 * n + (after is not None)]
        tot = _dot(refs[0][...].astype(BF16), refs[n][...].astype(BF16), 1, 1)
        for i in range(1, n):
            tot = tot + _dot(refs[i][...].astype(BF16), refs[n + i][...].astype(BF16), 1, 1)
        o_ref[...] = tot

    ins = [a for a, _ in pairs] + [b for _, b in pairs]
    in_specs = ([pl.BlockSpec((tm, a.shape[1]), lambda i: (i, 0)) for a, _ in pairs]
                + [pl.BlockSpec(b.shape, lambda i: (0, 0)) for _, b in pairs])
    if after is not None:
        ins.append(after)
        in_specs.append(pl.BlockSpec(after.shape, lambda i: (0,) * after.ndim))
    return pl.pallas_call(
        body, name=name, grid=(M // tm,), in_specs=in_specs, out_specs=pl.BlockSpec((tm, N), lambda i: (i, 0)),
        out_shape=jax.ShapeDtypeStruct((M, N), F32), compiler_params=_cp(("parallel",), VMEM_BIG))(*ins)


def _row(w):
    return pl.BlockSpec((BLK, w), lambda i: (i, 0))


def _rowc(w, c):
    return pl.BlockSpec((BLK, w), lambda i: (i, c))


def _full(shape):
    return pl.BlockSpec(shape, lambda i: tuple(0 for _ in shape))


def _rope(x, c, s):
    lane = lax.broadcasted_iota(jnp.int32, x.shape, 1)
    is_x1 = ((lane >> 4) & 1) == 0
    partner = jnp.where(is_x1, pltpu.roll(x, LANES - 16, 1), pltpu.roll(x, 16, 1))
    return x * c + partner * s


def _row_valid(i):
    rows = i * BLK + lax.broadcasted_iota(jnp.int32, (BLK, 1), 0)
    return (rows < N_META) | (rows >= PAD)


def _shift_rows(w):
    return pl.BlockSpec((BLK, w), lambda i: (jnp.maximum(i - 1, 0), 0))


def _h_block(i, x_ref, meta_ref):
    head = jnp.concatenate([meta_ref[...], jnp.zeros((BLK - N_META, D_MODEL), F32)], axis=0)
    return jnp.where(i == 0, head, x_ref[...])


def _rms_pre(x2, meta, g):
    lp = PAD + x2.shape[0]

    def body(x_ref, meta_ref, g_ref, u_ref):
        hv = _h_block(pl.program_id(0), x_ref, meta_ref)
        r = lax.rsqrt(jnp.mean(hv * hv, axis=-1, keepdims=True) + RMS_EPS)
        u_ref[...] = (hv * r * g_ref[...]).astype(BF16)

    return pl.pallas_call(
        body, name="rms_pre", grid=(lp // BLK,),
        in_specs=[_shift_rows(D_MODEL), _full((N_META, D_MODEL)), _full((1, D_MODEL))], out_specs=_row(D_MODEL),
        out_shape=jax.ShapeDtypeStruct((lp, D_MODEL), BF16),
        compiler_params=_cp(("parallel",)))(x2, meta, g)


def _split3(x):
    hi = x.astype(BF16)
    r1 = x - hi.astype(F32)
    mid = r1.astype(BF16)
    lo = (r1 - mid.astype(F32)).astype(BF16)
    return hi, mid, lo


def _small_prep(small, gq, gkv, fb, ctab, stab, tri):
    lp = small.shape[0]

    def body(sm_ref, gq_ref, gkv_ref, fb_ref, c_ref, s_ref, tri_ref, qn_ref, kvn_ref, kr_ref, kb_ref, carry):
        i = pl.program_id(0)

        @pl.when(i == 0)
        def _():
            carry[...] = jnp.zeros_like(carry)

        cq = sm_ref[:, 0:256]
        r = lax.rsqrt(jnp.mean(cq * cq, axis=-1, keepdims=True) + RMS_EPS)
        qn_ref[...] = (cq * r * gq_ref[...]).astype(BF16)
        ckv = sm_ref[:, 256:384]
        r = lax.rsqrt(jnp.mean(ckv * ckv, axis=-1, keepdims=True) + RMS_EPS)
        kvn_ref[...] = (ckv * r * gkv_ref[...]).astype(BF16)
        kr_ref[...] = _rope(sm_ref[:, 384:512], c_ref[...], s_ref[...]).astype(BF16)
        fl = sm_ref[:, 512:640] + fb_ref[...]
        lf = jnp.minimum(fl, 0.0) - jnp.log(1.0 + jnp.exp(-jnp.abs(fl)))
        lf = jnp.where(_row_valid(i), lf, 0.0)
        hi, mid, lo = _split3(lf)
        t = tri_ref[...]
        cum = (_dot(t, hi, 1, 0) + _dot(t, mid, 1, 0)) + _dot(t, lo, 1, 0) + carry[...]
        carry[...] = cum[BLK - 1:BLK, :]
        src = lax.broadcasted_iota(jnp.int32, (LANES, LANES), 0)
        dst = lax.broadcasted_iota(jnp.int32, (LANES, LANES), 1)
        kb = jnp.zeros((BLK, LANES), F32)
        for j, part in enumerate(_split3(-cum * LOG2E)):
            spread = ((dst == BIAS_PARTS * src + j) & (src < HEADS)).astype(BF16)
            kb = kb + _dot(part, spread, 1, 0)
        kb_ref[...] = kb.astype(BF16)

    return pl.pallas_call(
        body, name="small_prep", grid=(lp // BLK,),
        in_specs=[_row(SMALL_W), _full((1, 256)), _full((1, 128)), _full((1, 128)), _row(128), _row(128),
                  _full((BLK, BLK))],
        out_specs=[_row(256), _row(128), _row(128), _row(128)],
        out_shape=[jax.ShapeDtypeStruct((lp, 256), BF16), jax.ShapeDtypeStruct((lp, 128), BF16),
                   jax.ShapeDtypeStruct((lp, 128), BF16), jax.ShapeDtypeStruct((lp, 128), BF16)],
        scratch_shapes=[pltpu.VMEM((1, 128), F32)],
        compiler_params=_cp(("arbitrary",)))(small, gq, gkv, fb, ctab, stab, tri)


def _rope_pairs(tile, c, s):
    out = []
    for lo in range(0, tile.shape[1], 256):
        out += [tile[:, lo:lo + 128], _rope(tile[:, lo + 128:lo + 256], c, s)]
    return jnp.concatenate(out, axis=1)


def _gate_fwd(o_mla, o_fox, gate):
    lp = o_mla.shape[0]

    def body(om_ref, of_ref, zm_ref, zf_ref, am_ref, af_ref):
        zm = zm_ref[...].astype(F32)
        am_ref[...] = (om_ref[...] * (zm * _sigmoid(zm))).astype(BF16)
        zf = zf_ref[...].astype(F32)
        af_ref[...] = (of_ref[...] * (zf * _sigmoid(zf))).astype(BF16)

    return pl.pallas_call(
        body, name="gate_fwd", grid=(lp // BLK,),
        in_specs=[_row(D_MODEL), _row(D_MODEL), _rowc(D_MODEL, 0), _rowc(D_MODEL, 1)],
        out_specs=[_row(D_MODEL), _row(D_MODEL)],
        out_shape=[jax.ShapeDtypeStruct((lp, D_MODEL), BF16)] * 2,
        compiler_params=_cp(("parallel",)))(o_mla, o_fox, gate, gate)


def _merge_fwd(gate, y_mla, y_fox):
    lp = y_mla.shape[0]

    def body(ga_ref, gb_ref, ym_ref, yf_ref, m_ref):
        sa = _sigmoid(ga_ref[...].astype(F32))
        sb = _sigmoid(gb_ref[...].astype(F32))
        m_ref[...] = (sa * ym_ref[...] + sb * yf_ref[...]).astype(BF16)

    return pl.pallas_call(
        body, name="merge_fwd", grid=(lp // BLK,),
        in_specs=[_rowc(D_MODEL, 2), _rowc(D_MODEL, 3), _row(D_MODEL), _row(D_MODEL)],
        out_specs=_row(D_MODEL), out_shape=jax.ShapeDtypeStruct((lp, D_MODEL), BF16),
        compiler_params=_cp(("parallel",)))(gate, gate, y_mla, y_fox)


def _tail(x2, mixed, tgt, gpost):
    lp = mixed.shape[0]
    shift = _shift_rows(D_MODEL)

    def body(h_ref, mx_ref, t_ref, g_ref, dmx_ref, dy_ref, loss_ref, dg_ref):
        i = pl.program_id(0)

        @pl.when(i == 0)
        def _():
            loss_ref[...] = jnp.zeros_like(loss_ref)
            dg_ref[...] = jnp.zeros_like(dg_ref)
            dmx_ref[...] = jnp.zeros_like(dmx_ref)
            dy_ref[...] = jnp.zeros_like(dy_ref)

        @pl.when(i > 0)
        def _():
            mx = mx_ref[...]
            g = g_ref[...]
            r = lax.rsqrt(jnp.mean(mx * mx, axis=-1, keepdims=True) + RMS_EPS)
            nrm = mx * r
            e = (h_ref[...] + nrm * g) - t_ref[...]
            loss_ref[...] += jnp.sum(0.5 * jnp.sum(e * e, axis=-1, keepdims=True) * (1.0 / D_MODEL),
                                     axis=0, keepdims=True)
            dy = e * (1.0 / D_MODEL)
            dy_ref[...] = dy
            dg_ref[...] += jnp.sum(dy * nrm, axis=0, keepdims=True)
            w = dy * g
            dot = jnp.mean(w * mx, axis=-1, keepdims=True)
            dmx_ref[...] = (r * w - mx * (r * r * r * dot)).astype(BF16)

    return pl.pallas_call(
        body, name="tail", grid=(lp // BLK,),
        in_specs=[shift, _row(D_MODEL), shift, _full((1, D_MODEL))],
        out_specs=[_row(D_MODEL), _row(D_MODEL), _full((1, 1)), _full((1, D_MODEL))],
        out_shape=[jax.ShapeDtypeStruct((lp, D_MODEL), BF16), jax.ShapeDtypeStruct((lp, D_MODEL), F32),
                   jax.ShapeDtypeStruct((1, 1), F32), jax.ShapeDtypeStruct((1, D_MODEL), F32)],
        compiler_params=_cp(("arbitrary",)))(x2, mixed, tgt, gpost)


def _merge_bwd(dm, gate, y_mla, y_fox):
    lp = dm.shape[0]

    def body(dm_ref, ga_ref, gb_ref, ym_ref, yf_ref, dym_ref, dyf_ref, dg_ref):
        dm_v = dm_ref[...].astype(F32)
        sa = _sigmoid(ga_ref[...].astype(F32))
        sb = _sigmoid(gb_ref[...].astype(F32))
        dym_ref[...] = (dm_v * sa).astype(BF16)
        dyf_ref[...] = (dm_v * sb).astype(BF16)
        dg_ref[:, 0:D_MODEL] = (dm_v * ym_ref[...] * (sa * (1.0 - sa))).astype(BF16)
        dg_ref[:, D_MODEL:2 * D_MODEL] = (dm_v * yf_ref[...] * (sb * (1.0 - sb))).astype(BF16)

    return pl.pallas_call(
        body, name="merge_bwd", grid=(lp // BLK,),
        in_specs=[_row(D_MODEL), _rowc(D_MODEL, 2), _rowc(D_MODEL, 3), _row(D_MODEL), _row(D_MODEL)],
        out_specs=[_row(D_MODEL), _row(D_MODEL), _row(2 * D_MODEL)],
        out_shape=[jax.ShapeDtypeStruct((lp, D_MODEL), BF16), jax.ShapeDtypeStruct((lp, D_MODEL), BF16),
                   jax.ShapeDtypeStruct((lp, 2 * D_MODEL), BF16)],
        compiler_params=_cp(("parallel",)))(dm, gate, gate, y_mla, y_fox)


def _gate_bwd(da_mla, da_fox, o_mla, o_fox, gate):
    lp = da_mla.shape[0]

    def one(da, o, z, head_of_col):
        sg = _sigmoid(z)
        do = (da * (z * sg)).astype(BF16)
        dz = da * o * (sg * (1.0 + z * (1.0 - sg)))
        delta = sum(_dot(part, head_of_col, 1, 0) for part in _split3(do.astype(F32) * o))
        return do, dz.astype(BF16), delta

    def body(dam_ref, daf_ref, om_ref, of_ref, zm_ref, zf_ref, dom_ref, dof_ref, dz_ref, dlm_ref, dlf_ref):
        f32 = lambda r: r[...].astype(F32)
        head_of_col = (lax.broadcasted_iota(jnp.int32, (D_MODEL, LANES), 0) // HEAD_DIM
                       == lax.broadcasted_iota(jnp.int32, (D_MODEL, LANES), 1)).astype(BF16)
        dom_ref[...], dz_ref[:, 0:D_MODEL], dlm_ref[...] = one(f32(dam_ref), f32(om_ref), f32(zm_ref), head_of_col)
        dof_ref[...], dz_ref[:, D_MODEL:2 * D_MODEL], dlf_ref[...] = one(f32(daf_ref), f32(of_ref), f32(zf_ref),
                                                                        head_of_col)

    return pl.pallas_call(
        body, name="gate_bwd", grid=(lp // BLK,),
        in_specs=[_row(D_MODEL)] * 4 + [_rowc(D_MODEL, 0), _rowc(D_MODEL, 1)],
        out_specs=[_row(D_MODEL), _row(D_MODEL), _row(2 * D_MODEL), _row(LANES), _row(LANES)],
        out_shape=[jax.ShapeDtypeStruct((lp, D_MODEL), BF16), jax.ShapeDtypeStruct((lp, D_MODEL), BF16),
                   jax.ShapeDtypeStruct((lp, 2 * D_MODEL), BF16), jax.ShapeDtypeStruct((lp, LANES), F32),
                   jax.ShapeDtypeStruct((lp, LANES), F32)],
        compiler_params=_cp(("parallel",)))(da_mla, da_fox, o_mla, o_fox, gate, gate)


def _small_bwd(small, dqn, dkvn, dkr, dcol_t, drow_t, gq, gkv, fb, ctab, stab, triu):
    lp = small.shape[0]
    nb = lp // BLK

    def rrow(w):
        return pl.BlockSpec((BLK, w), lambda i: (nb - 1 - i, 0))

    def body(sm_ref, dqn_ref, dkvn_ref, dkr_ref, dcol_ref, drow_ref, gq_ref, gkv_ref, fb_ref, c_ref, s_ref, tri_ref,
             ds_ref, dgq_ref, dgkv_ref, dfb_ref, carry):
        i = pl.program_id(0)

        @pl.when(i == 0)
        def _():
            carry[...] = jnp.zeros_like(carry)
            dgq_ref[...] = jnp.zeros_like(dgq_ref)
            dgkv_ref[...] = jnp.zeros_like(dgkv_ref)
            dfb_ref[...] = jnp.zeros_like(dfb_ref)

        def norm_bwd(x, dn, g, dg_ref):
            r = lax.rsqrt(jnp.mean(x * x, axis=-1, keepdims=True) + RMS_EPS)
            dg_ref[...] += jnp.sum(dn * (x * r), axis=0, keepdims=True)
            w = dn * g
            dot = jnp.mean(w * x, axis=-1, keepdims=True)
            return r * w - x * (r * r * r * dot)

        ds_ref[:, 0:256] = norm_bwd(sm_ref[:, 0:256], dqn_ref[...], gq_ref[...], dgq_ref).astype(BF16)
        ds_ref[:, 256:384] = norm_bwd(sm_ref[:, 256:384], dkvn_ref[...], gkv_ref[...], dgkv_ref).astype(BF16)

        dk = dkr_ref[0]
        for p in range(1, PAIRS):
            dk = dk + dkr_ref[p]
        dk = _rope(dk, c_ref[...], -s_ref[...])
        lane = lax.broadcasted_iota(jnp.int32, dk.shape, 1)
        dk = jnp.where(lane < MLA_ROPE, dk + pltpu.roll(dk, LANES - MLA_ROPE, 1), 0.0)
        ds_ref[:, 384:512] = dk.astype(BF16)

        dcol = dcol_ref[0]
        for p in range(1, PAIRS):
            dcol = dcol + pltpu.roll(dcol_ref[p], 2 * p, 1)
        rows16 = jnp.concatenate([drow_ref[p, h:h + 1, :] for p in range(PAIRS) for h in range(2)], axis=0)
        eye = (lax.broadcasted_iota(jnp.int32, (HEADS, LANES), 0)
               == lax.broadcasted_iota(jnp.int32, (HEADS, LANES), 1)).astype(BF16)
        drow = sum(_dot(part, eye, 0, 0) for part in _split3(rows16))
        dcr = dcol - drow
        hi, mid, lo = _split3(dcr)
        t = tri_ref[...]
        suf = (_dot(t, hi, 1, 0) + _dot(t, mid, 1, 0)) + _dot(t, lo, 1, 0) + carry[...]
        fl = sm_ref[:, 512:640] + fb_ref[...]
        dfl = jnp.where(_row_valid(nb - 1 - i), -suf * _sigmoid(-fl), 0.0)
        ds_ref[:, 512:640] = dfl.astype(BF16)
        dfb_ref[...] += jnp.sum(dfl, axis=0, keepdims=True)
        carry[...] += jnp.sum(dcr, axis=0, keepdims=True)

    return pl.pallas_call(
        body, name="small_bwd", grid=(nb,),
        in_specs=[rrow(SMALL_W), rrow(256), rrow(128),
                  pl.BlockSpec((PAIRS, BLK, 128), lambda i: (0, nb - 1 - i, 0)),
                  pl.BlockSpec((PAIRS, BLK, 128), lambda i: (0, nb - 1 - i, 0)),
                  pl.BlockSpec((PAIRS, 2, BLK), lambda i: (0, 0, nb - 1 - i)),
                  _full((1, 256)), _full((1, 128)), _full((1, 128)), rrow(128), rrow(128), _full((BLK, BLK))],
        out_specs=[rrow(SMALL_W), _full((1, 256)), _full((1, 128)), _full((1, 128))],
        out_shape=[jax.ShapeDtypeStruct((lp, SMALL_W), BF16), jax.ShapeDtypeStruct((1, 256), F32),
                   jax.ShapeDtypeStruct((1, 128), F32), jax.ShapeDtypeStruct((1, 128), F32)],
        scratch_shapes=[pltpu.VMEM((1, 128), F32)],
        compiler_params=_cp(("arbitrary",)))(small, dqn, dkvn, dkr, dcol_t, drow_t, gq, gkv, fb, ctab, stab, triu)


def _pre_bwd(du, x2, meta, dy, gpre):
    s_rows = x2.shape[0]
    lp = PAD + s_rows
    shift = _shift_rows(D_MODEL)

    def body(du_ref, x_ref, meta_ref, dy_ref, g_ref, dx_ref, dmeta_ref, dg_ref):
        i = pl.program_id(0)

        @pl.when(i == 0)
        def _():
            dg_ref[...] = jnp.zeros_like(dg_ref)

        hv = _h_block(i, x_ref, meta_ref)
        duv = du_ref[...]
        r = lax.rsqrt(jnp.mean(hv * hv, axis=-1, keepdims=True) + RMS_EPS)
        dg_ref[...] += jnp.sum(duv * (hv * r), axis=0, keepdims=True)
        w = duv * g_ref[...]
        dot = jnp.mean(w * hv, axis=-1, keepdims=True)
        dh = dy_ref[...] + (r * w - hv * (r * r * r * dot))
        dx_ref[...] = dh

        @pl.when(i == 0)
        def _():
            dmeta_ref[...] = dh[0:N_META, :]

    return pl.pallas_call(
        body, name="pre_bwd", grid=(lp // BLK,),
        in_specs=[_row(D_MODEL), shift, _full((N_META, D_MODEL)), _row(D_MODEL), _full((1, D_MODEL))],
        out_specs=[shift, _full((N_META, D_MODEL)), _full((1, D_MODEL))],
        out_shape=[jax.ShapeDtypeStruct((s_rows, D_MODEL), F32), jax.ShapeDtypeStruct((N_META, D_MODEL), F32),
                   jax.ShapeDtypeStruct((1, D_MODEL), F32)],
        compiler_params=_cp(("arbitrary",)))(du, x2, meta, dy, gpre)


def _pair_masks(rope, pair):
    lane = lax.broadcasted_iota(jnp.int32, (1, LANES), 1)
    mas = [lane < HEAD_DIM, lane >= HEAD_DIM]
    wide = lax.broadcasted_iota(jnp.int32, (1, 2 * LANES), 1)
    extra = MLA_ROPE if rope else BIAS_PARTS
    lo = LANES if rope else LANES + 2 * BIAS_PARTS * pair
    mid = lo + extra
    return mas, [(wide < HEAD_DIM) | ((wide >= lo) & (wide < mid)),
                 ((wide >= HEAD_DIM) & (wide < LANES)) | ((wide >= mid) & (wide < mid + extra))]


def _mask2(x, masks):
    return [jnp.where(m, x, jnp.zeros_like(x)) for m in masks]


def _q_heads(q_rows, rope, mas, hmask):
    if rope:
        return _mask2(q_rows, hmask)
    zero = jnp.zeros((q_rows.shape[0], LANES), BF16)
    return [jnp.concatenate([jnp.where(m, q_rows, zero), jnp.where(hm[:, LANES:], zero + 1, zero)], axis=1)
            for m, hm in zip(mas, hmask)]


def _attn_fwd(q, k, v, k2, *, rope, qcol, kcol, vcol, name):
    lp = q.shape[0]
    nq = 1 + (lp - PAD) // QB
    qw = 256 if rope else 128

    def body(q_ref, k_ref, v_ref, k2_ref, o_ref, lse_ref):
        i = pl.program_id(1)
        r0 = pl.multiple_of(jnp.where(i == 0, 0, PAD + QB * (i - 1)), BLK)
        b0 = r0 // BLK
        mas, hmask = _pair_masks(rope, pl.program_id(0))
        qh = _q_heads(q_ref[pl.ds(r0, QB), :], rope, mas, hmask)

        def update(kcs, carry, masks, ns=None, q_lo=0, wq=QB):
            ns = ns or [BLK] * len(kcs)
            stats, acc = carry[:4], carry[4]
            qs = [x[q_lo:q_lo + wq] for x in qh]
            k0s = [pl.multiple_of(kc * BLK, BLK) for kc in kcs]
            kks = [jnp.concatenate([k_ref[pl.ds(k0, n), :], k2_ref[pl.ds(k0, n), :]], axis=1)
                   for k0, n in zip(k0s, ns)]
            new_stats, alphas, ps = [], [], [[] for _ in kcs]
            for h in range(2):
                m_prev, l_prev = stats[2 * h], stats[2 * h + 1]
                ss = []
                for kk, mask in zip(kks, masks):
                    s = _dot(kk, qs[h], 1, 1)
                    if mask is not None:
                        s = jnp.where(mask, s, NEG)
                    ss.append(s)
                m_new = m_prev
                for s in ss:
                    m_new = jnp.maximum(m_new, jnp.max(s, axis=0, keepdims=True))
                alpha = jnp.exp2(m_prev - m_new)
                l_new = alpha * l_prev
                for j, s in enumerate(ss):
                    p = jnp.exp2(s - m_new)
                    l_new = l_new + jnp.sum(p, axis=0, keepdims=True)
                    ps[j].append(p.astype(BF16))
                new_stats += [m_new, l_new]
                alphas.append(alpha)
            vcat = jnp.concatenate([x for k0, n in zip(k0s, ns) for x in _mask2(v_ref[pl.ds(k0, n), :], mas)], axis=0)
            pv = _dot(vcat, jnp.concatenate([p for pj in ps for p in pj], axis=0), 0, 0)
            a_full = jnp.concatenate([jnp.broadcast_to(a, (HEAD_DIM, wq)) for a in alphas], axis=0)
            return (*new_stats, a_full * acc + pv)

        neg = jnp.full((1, QB), NEG, F32)
        zero = jnp.zeros((1, QB), F32)
        c = (neg, zero, neg, zero, jnp.zeros((LANES, QB), F32))
        n_mid = jnp.maximum(b0 - 1, 0)
        c = lax.fori_loop(0, n_mid // 4, lambda t, cr: update([4 * t + u for u in (1, 2, 3, 4)], cr, [None] * 4), c)
        c = lax.fori_loop(0, (n_mid % 4) // 2, lambda t, cr: update([n_mid - 1, n_mid], cr, [None, None]), c)
        key_l = lax.broadcasted_iota(jnp.int32, (BLK, BLK), 0)
        qry_l = lax.broadcasted_iota(jnp.int32, (BLK, BLK), 1)
        tri = (key_l <= qry_l) & (b0 > 0)
        meta_ok = (key_l[0:N_META] <= qry_l[0:N_META]) | (b0 > 0)
        lo = update([0, b0], tuple(a[:, 0:BLK] for a in c), [meta_ok, tri], ns=[N_META, BLK], q_lo=0, wq=BLK)
        hi = update([0, b0, b0 + 1], tuple(a[:, BLK:QB] for a in c), [None, None, tri], ns=[N_META, BLK, BLK],
                    q_lo=BLK, wq=QB - BLK)
        c = tuple(jnp.concatenate([a, b], axis=1) for a, b in zip(lo, hi))
        inv =jnp.concatenate([jnp.broadcast_to(1.0 / c[1], (HEAD_DIM, QB)),
                               jnp.broadcast_to(1.0 / c[3], (HEAD_DIM, QB))], axis=0)
        o_t = (c[4] * inv).T.astype(BF16)
        lses = [c[2 * h] + jnp.log(c[2 * h + 1]) * LOG2E for h in range(2)]
        o_ref[pl.ds(r0, BLK), :] = o_t[0:BLK]
        for h in range(2):
            lse_ref[0, h:h + 1, pl.ds(r0, BLK)] = lses[h][:, 0:BLK]

        @pl.when(i > 0)
        def _():
            r1 = pl.multiple_of(r0 + BLK, BLK)
            o_ref[pl.ds(r1, QB - BLK), :] = o_t[BLK:QB]
            for h in range(2):
                lse_ref[0, h:h + 1, pl.ds(r1, QB - BLK)] = lses[h][:, BLK:QB]

    in_specs = [pl.BlockSpec((lp, qw), lambda p, i: (0, qcol + p)),
                pl.BlockSpec((lp, 128), lambda p, i: (0, kcol(p))),
                pl.BlockSpec((lp, 128), lambda p, i: (0, vcol(p))),
                pl.BlockSpec((lp, 128), lambda p, i: (0, 0))]
    return pl.pallas_call(
        body, name=name, grid=(PAIRS, nq), in_specs=in_specs,
        out_specs=[pl.BlockSpec((lp, 128), lambda p, i: (0, p)),
                   pl.BlockSpec((1, 2, lp), lambda p, i: (p, 0, 0))],
        out_shape=[jax.ShapeDtypeStruct((lp, D_MODEL), BF16), jax.ShapeDtypeStruct((PAIRS, 2, lp), F32)],
        compiler_params=_cp(("parallel", "arbitrary"), VMEM_BIG))(q, k, v, k2)


def _attn_bwd(q, k, v, k2, do, delta, lse, *, rtabs=None, scale, qcol, kcol, vcol, name):
    lp = q.shape[0]
    nb = lp // BLK
    rope = rtabs is not None
    bias = not rope
    qw = 256 if rope else 128

    def body(*refs):
        it = iter(refs)
        q_ref, k_ref, v_ref, k2_ref = next(it), next(it), next(it), next(it)
        do_ref, dl_ref, lse_ref = next(it), next(it), next(it)
        ct_ref, st_ref = (next(it), next(it)) if rope else (None, None)
        dq_out, dk_ref, dv_ref = next(it), next(it), next(it)
        x_ref = next(it)
        drow_ref = next(it) if bias else None
        dq_ref = next(it)
        kb = pl.program_id(1)
        mas, hmask = _pair_masks(rope, pl.program_id(0))
        lane = lax.broadcasted_iota(jnp.int32, (1, LANES), 1)

        @pl.when(kb == 0)
        def _():
            dq_ref[...] = jnp.zeros_like(dq_ref)
            if bias:
                drow_ref[...] = jnp.zeros_like(drow_ref)

        def key_pass(n, w):
            kk = jnp.concatenate([k_ref[0:n, :], k2_ref[0:n, :]], axis=1)
            vh = _mask2(v_ref[0:n, :], mas)
            kcat = jnp.concatenate([x[:, 0:qw] for x in _mask2(kk, hmask)], axis=0)
            diag_mask = (lax.broadcasted_iota(jnp.int32, (n, w), 0) <= lax.broadcasted_iota(jnp.int32, (n, w), 1))

            def chunk(qc, carry, mask):
                carry = list(carry)
                q0 = qc * w if isinstance(qc, int) else pl.multiple_of(qc * w, w)
                dov = do_ref[pl.ds(q0, w), :]
                doh = _mask2(dov, mas)
                qh = _q_heads(q_ref[pl.ds(q0, w), :], rope, mas, hmask)
                pbs, dss = [], []
                for h in range(2):
                    p = jnp.exp2(_dot(kk, qh[h], 1, 1) - lse_ref[0, h:h + 1, pl.ds(q0, w)])
                    if mask is not None:
                        p = jnp.where(mask, p, 0.0)
                    ds = p * (_dot(vh[h], dov, 1, 1) - dl_ref[0, h:h + 1, pl.ds(q0, w)])
                    if bias:
                        drow_ref[0, h:h + 1, pl.ds(q0, w)] += jnp.sum(ds, axis=0, keepdims=True)
                        carry[2 + h] = carry[2 + h] + jnp.sum(ds, axis=1, keepdims=True)
                    pbs.append(p.astype(BF16))
                    dss.append(ds.astype(BF16))
                ds_lanes = jnp.concatenate(dss, axis=1)
                ds_rows = jnp.concatenate(dss, axis=0)
                qcat = jnp.concatenate([x[:, 0:qw] for x in qh], axis=0)
                carry[0] = carry[0] + _dot(ds_lanes, qcat, 1, 0)
                carry[1] = carry[1] + _dot(jnp.concatenate(pbs, axis=1), jnp.concatenate(doh, axis=0), 1, 0)
                dq_ref[pl.ds(q0, w), :] += _dot(ds_rows, kcat, 0, 0)
                return tuple(carry)

            c = [jnp.zeros((n, qw), F32), jnp.zeros((n, LANES), F32)]
            if bias:
                c += [jnp.zeros((n, 1), F32), jnp.zeros((n, 1), F32)]
            c = tuple(c)
            if w != BLK:
                for qc in range(lp // w):
                    c = chunk(qc, c, diag_mask if qc == 0 else None)
            else:
                groups = (nb - kb) // UNROLL

                def several(t, cr):
                    for u in range(UNROLL):
                        cr = chunk(kb + UNROLL * t + u, cr, (diag_mask | (t > 0)) if u == 0 else None)
                    return cr

                c = lax.fori_loop(0, groups, several, c)
                start = kb + UNROLL * groups
                pairs = (nb - start) // 2

                def two(t, cr):
                    qc = start + 2 * t
                    return chunk(qc + 1, chunk(qc, cr, diag_mask | (qc > kb)), None)

                c = lax.fori_loop(0, pairs, two, c)
                c = lax.fori_loop(start + 2 * pairs, nb, lambda qc, cr: chunk(qc, cr, diag_mask | (qc > kb)), c)

            def rows(a, dtype):
                a = a.astype(dtype)
                return a if n == BLK else jnp.concatenate([a, jnp.zeros((BLK - n, a.shape[1]), dtype)], axis=0)

            dk = c[0] * LN2
            dk_ref[...] = rows(dk[:, 0:LANES], BF16)
            dv_ref[...] = rows(c[1], BF16)
            if rope:
                x_ref[0] = rows(dk[:, LANES:2 * LANES], F32)
            if bias:
                x_ref[0] = rows(jnp.where(lane == 0, c[2], jnp.where(lane == 1, c[3], 0.0)), F32)

        @pl.when(kb == 0)
        def _():
            key_pass(N_META, lp // 2)

        @pl.when(kb > 0)
        def _():
            key_pass(BLK, BLK)

        @pl.when(kb == nb - 1)
        def _():
            def fin(c, carry):
                r0 = pl.multiple_of(c * BLK, BLK)
                dq = dq_ref[pl.ds(r0, BLK), :] * scale
                if rope:
                    back = _rope(dq[:, LANES:2 * LANES], ct_ref[pl.ds(r0, BLK), :], -st_ref[pl.ds(r0, BLK), :])
                    dq = jnp.concatenate([dq[:, 0:LANES], back], axis=1)
                dq_out[pl.ds(r0, BLK), :] = dq.astype(BF16)
                return carry

            lax.fori_loop(0, nb, fin, 0)

    in_specs = [pl.BlockSpec((lp, qw), lambda p, j: (0, qcol + p)),
                pl.BlockSpec((BLK, 128), lambda p, j: (j, kcol(p))),
                pl.BlockSpec((BLK, 128), lambda p, j: (j, vcol(p))),
                pl.BlockSpec((BLK, 128), lambda p, j: (j, 0)),
                pl.BlockSpec((lp, 128), lambda p, j: (0, p)), pl.BlockSpec((1, 2, lp), lambda p, j: (p, 0, 0)),
                pl.BlockSpec((1, 2, lp), lambda p, j: (p, 0, 0))]
    ins = [q, k, v, k2, do, delta, lse]
    if rope:
        in_specs += [pl.BlockSpec((lp, 128), lambda p, j: (0, 0))] * 2
        ins += list(rtabs)
    out_specs = [pl.BlockSpec((lp, qw), lambda p, j: (0, p)),
                 pl.BlockSpec((BLK, 128), lambda p, j: (j, p)),
                 pl.BlockSpec((BLK, 128), lambda p, j: (j, p)),
                 pl.BlockSpec((1, BLK, 128), lambda p, j: (p, j, 0))]
    out_shape = [jax.ShapeDtypeStruct((lp, PAIRS * qw), BF16), jax.ShapeDtypeStruct((lp, D_MODEL), BF16),
                 jax.ShapeDtypeStruct((lp, D_MODEL), BF16), jax.ShapeDtypeStruct((PAIRS, lp, 128), F32)]
    if bias:
        out_specs.append(pl.BlockSpec((1, 2, lp), lambda p, j: (p, 0, 0)))
        out_shape.append(jax.ShapeDtypeStruct((PAIRS, 2, lp), F32))
    return pl.pallas_call(
        body, name=name, grid=(PAIRS, nb), in_specs=in_specs, out_specs=out_specs, out_shape=out_shape,
        scratch_shapes=[pltpu.VMEM((lp, qw), F32)],
        compiler_params=_cp(("parallel", "arbitrary"), VMEM_BIG))(*ins)


def _adamw(w, g, m, v, name):
    lead = w.ndim - 2
    rows, cols = w.shape[lead:]
    big = rows * cols > 512 * 1024
    tr = 128 if big and rows % 128 == 0 else rows
    tc = 256 if big and tr == rows else cols

    def body(w_ref, g_ref, m_ref, v_ref, d_ref, nm_ref, nv_ref):
        gv = g_ref[...]
        nm = ADAM_B1 * m_ref[...] + (1.0 - ADAM_B1) * gv
        nv = ADAM_B2 * v_ref[...] + (1.0 - ADAM_B2) * (gv * gv)
        m_hat = nm / (1.0 - ADAM_B1 ** ADAM_STEP)
        v_hat = nv / (1.0 - ADAM_B2 ** ADAM_STEP)
        d_ref[...] = -ADAM_LR * (m_hat / (jnp.sqrt(v_hat) + ADAM_EPS) + ADAM_WD * w_ref[...])
        nm_ref[...] = nm
        nv_ref[...] = nv

    spec = pl.BlockSpec((1,) * lead + (tr, tc), lambda i, j: (0,) * lead + (i, j))
    return pl.pallas_call(
        body, name=name, grid=(rows // tr, cols // tc), in_specs=[spec] * 4, out_specs=[spec] * 3,
        out_shape=[jax.ShapeDtypeStruct(w.shape, F32)] * 3,
        compiler_params=_cp(("parallel", "parallel"), VMEM_BIG))(w, g, m, v)


def _add_cores(g, from_sib, name):
    n, rows, cols = g.shape
    half = rows // 2
    tr = _tile(half, (256, 240))
    nt = half // tr

    def body(lo_ref, hi_ref, s_ref, o_ref):
        mine = jnp.where(lax.axis_index("c") == 0, lo_ref[0], hi_ref[0])
        o_ref[0] = (mine.astype(F32) + s_ref[0].astype(F32)).astype(BF16)

    return pl.pallas_call(
        body, name=name, grid=(n, nt),
        in_specs=[pl.BlockSpec((1, tr, cols), lambda j, i: (j, i, 0)),
                  pl.BlockSpec((1, tr, cols), lambda j, i: (j, nt + i, 0)),
                  pl.BlockSpec((1, tr, cols), lambda j, i: (j, i, 0))],
        out_specs=pl.BlockSpec((1, tr, cols), lambda j, i: (j, i, 0)),
        out_shape=jax.ShapeDtypeStruct((n, half, cols), BF16),
        compiler_params=_cp(("parallel", "parallel"), VMEM_BIG))(g, g, from_sib)


def _add_chips(x, own, name):
    n, rows, cols = x.shape
    tr = _tile(rows, (256, 240))

    def body(x_ref, own_ref, o_ref):
        me = 2 * lax.axis_index("x") + lax.axis_index("y")
        v = [jnp.where(me == k, own_ref[...], x_ref[k]).astype(F32) for k in range(N_CHIPS)]
        o_ref[...] = ((v[0] + v[1]) + v[2]) + v[3]

    return pl.pallas_call(
        body, name=name, grid=(rows // tr,),
        in_specs=[pl.BlockSpec((n, tr, cols), lambda i: (0, i, 0)), pl.BlockSpec((tr, cols), lambda i: (i, 0))],
        out_specs=pl.BlockSpec((tr, cols), lambda i: (i, 0)),
        out_shape=jax.ShapeDtypeStruct((rows, cols), F32), compiler_params=_cp(("parallel",), VMEM_BIG))(x, own)


def _axes():
    return lax.axis_index("x"), lax.axis_index("y"), lax.axis_index("c")


def _other_chips(x, y):
    return [(1 - x, y), (x, 1 - y), (1 - x, 1 - y)]


ANY = pl.BlockSpec(memory_space=pl.ANY)


def _rcopy(src, dst, send_sems, recv_sems, k, to):
    return pltpu.make_async_remote_copy(src_ref=src, dst_ref=dst, send_sem=send_sems.at[k], recv_sem=recv_sems.at[k],
                                        device_id=to, device_id_type=MESH)


def _gather_weights(shards, meta):
    n = len(shards)

    def body(*refs):
        srcs, meta_ref = refs[:n], refs[n]
        outs, mout_ref = refs[n + 1:2 * n + 1], refs[2 * n + 1]
        send_sems, recv_sems = refs[2 * n + 2:]
        x, y, c = _axes()
        me = 2 * x + y
        sib = (x, y, 1 - c)
        chips = _other_chips(x, y)

        def half(t, chip_idx, cc):
            hr = shards[t].shape[0] // 2
            return outs[t].at[chip_idx, pl.ds(cc * hr, hr), :]

        first = []
        for j, (px, py) in enumerate(chips):
            for t in range(n):
                hr = shards[t].shape[0] // 2
                first.append(_rcopy(srcs[t].at[pl.ds(c * hr, hr), :], half(t, me, c), send_sems, recv_sems,
                                    3 * t + j, (px, py, c)))
            first.append(_rcopy(meta_ref, mout_ref.at[me], send_sems, recv_sems, 3 * n + j, (px, py, c)))
        for cp in first:
            cp.start()
        passed = []
        for j, (px, py) in enumerate(chips):
            src_chip = 2 * px + py
            for t in range(n):
                _rcopy(half(t, src_chip, c), half(t, src_chip, c), send_sems, recv_sems, 3 * t + j, sib).wait_recv()
                fwd = _rcopy(half(t, src_chip, c), half(t, src_chip, c), send_sems, recv_sems, 3 * (n + 1 + t) + j, sib)
                fwd.start()
                passed.append(fwd)
            _rcopy(mout_ref.at[src_chip], mout_ref.at[src_chip], send_sems, recv_sems, 3 * n + j, sib).wait_recv()
        for j, (px, py) in enumerate(chips):
            src_chip = 2 * px + py
            for t in range(n):
                _rcopy(half(t, src_chip, 1 - c), half(t, src_chip, 1 - c), send_sems, recv_sems,
                       3 * (n + 1 + t) + j, sib).wait_recv()
        for cp in first + passed:
            cp.wait_send()

    nsem = 3 * (2 * n + 1)
    return pl.pallas_call(
        body, name="gather_weights", in_specs=[ANY] * (n + 1), out_specs=[ANY] * (n + 1),
        out_shape=[jax.ShapeDtypeStruct((N_CHIPS,) + s.shape, s.dtype) for s in shards]
        + [jax.ShapeDtypeStruct((N_CHIPS,) + meta.shape, meta.dtype)],
        scratch_shapes=[pltpu.SemaphoreType.DMA((nsem,)), pltpu.SemaphoreType.DMA((nsem,))])(*shards, meta)


def _gather_late(shard):
    rows, cols = shard.shape
    hr = rows // 2
    src = jax.new_ref(shard, memory_space=pltpu.MemorySpace.HBM)
    out = jax.empty_ref(jax.ShapeDtypeStruct((N_CHIPS, rows, cols), shard.dtype), memory_space=pltpu.MemorySpace.HBM)

    @pl.kernel(mesh=plsc.ScalarSubcoreMesh(axis_name="seq", num_cores=1), name="gather_late",
               scratch_types=(pltpu.SemaphoreType.DMA((6,)), pltpu.SemaphoreType.DMA((6,))),
               compiler_params=pltpu.CompilerParams(collective_id=1))
    def launch(send_sems, recv_sems):
        x, y, c = _axes()
        me = 2 * x + y
        sib = (x, y, 1 - c)
        chips = _other_chips(x, y)
        barrier = pltpu.get_barrier_semaphore()
        for px, py in chips:
            pl.semaphore_signal(barrier, inc=1, device_id=(px, py, c), device_id_type=MESH)
        pl.semaphore_signal(barrier, inc=1, device_id=sib, device_id_type=MESH)
        pl.semaphore_wait(barrier, 4)

        def half(chip_idx, cc):
            return out.at[chip_idx, pl.ds(cc * hr, hr), :]

        first = [_rcopy(src.at[pl.ds(c * hr, hr), :], half(me, c), send_sems, recv_sems, j, (px, py, c))
                 for j, (px, py) in enumerate(chips)]
        for cp in first:
            cp.start()
        passed = []
        for j, (px, py) in enumerate(chips):
            land = half(2 * px + py, c)
            _rcopy(land, land, send_sems, recv_sems, j, sib).wait_recv()
            fwd = _rcopy(land, land, send_sems, recv_sems, 3 + j, sib)
            fwd.start()
            passed.append(fwd)
        for j, (px, py) in enumerate(chips):
            land = half(2 * px + py, 1 - c)
            _rcopy(land, land, send_sems, recv_sems, 3 + j, sib).wait_recv()
        for cp in first + passed:
            cp.wait_send()

    launch()
    return out[...]


def _swap_halves(gs):
    n = len(gs)
    ncopies = sum(g.shape[0] for g in gs)

    def body(*refs):
        srcs, outs = refs[:n], refs[n:2 * n]
        send_sems, recv_sems = refs[2 * n:]
        x, y, c = _axes()
        cps = []
        for t in range(n):
            hr = gs[t].shape[1] // 2
            for j in range(gs[t].shape[0]):
                cps.append(_rcopy(srcs[t].at[j, pl.ds((1 - c) * hr, hr), :], outs[t].at[j], send_sems, recv_sems,
                                  len(cps), (x, y, 1 - c)))
        for cp in cps:
            cp.start()
        for cp in cps:
            cp.wait()

    return pl.pallas_call(
        body, name="swap_halves", in_specs=[ANY] * n, out_specs=[ANY] * n,
        out_shape=[jax.ShapeDtypeStruct((g.shape[0], g.shape[1] // 2, g.shape[2]), g.dtype) for g in gs],
        scratch_shapes=[pltpu.SemaphoreType.DMA((ncopies,)), pltpu.SemaphoreType.DMA((ncopies,))])(*gs)


def _scatter_chips(parts):
    n = len(parts)
    srcs = [jax.new_ref(p, memory_space=pltpu.MemorySpace.HBM) for p in parts]
    outs = [jax.empty_ref(jax.ShapeDtypeStruct(p.shape, p.dtype), memory_space=pltpu.MemorySpace.HBM) for p in parts]

    @pl.kernel(mesh=plsc.ScalarSubcoreMesh(axis_name="seq", num_cores=1), name="scatter_chips",
               scratch_types=(pltpu.SemaphoreType.DMA((3 * n,)), pltpu.SemaphoreType.DMA((3 * n,))),
               compiler_params=pltpu.CompilerParams(collective_id=0))
    def launch(send_sems, recv_sems):
        x, y, c = _axes()
        me = 2 * x + y
        chips = _other_chips(x, y)
        barrier = pltpu.get_barrier_semaphore()
        for px, py in chips:
            pl.semaphore_signal(barrier, inc=1, device_id=(px, py, c), device_id_type=MESH)
        pl.semaphore_wait(barrier, 3)
        cps = []
        for j, (px, py) in enumerate(chips):
            for t in range(n):
                cps.append(_rcopy(srcs[t].at[2 * px + py], outs[t].at[me], send_sems, recv_sems, 3 * t + j,
                                  (px, py, c)))
        for cp in cps:
            cp.start()
        for cp in cps:
            cp.wait()

    launch()
    return [o[...] for o in outs]


def _swap_reduced(rs):
    n = len(rs)

    def body(*refs):
        srcs, outs = refs[:n], refs[n:2 * n]
        send_sems, recv_sems = refs[2 * n:]
        x, y, c = _axes()
        cps = [_rcopy(srcs[t], outs[t], send_sems, recv_sems, t, (x, y, 1 - c)) for t in range(n)]
        for cp in cps:
            cp.start()
        for cp in cps:
            cp.wait()

    return pl.pallas_call(
        body, name="swap_reduced", in_specs=[ANY] * n, out_specs=[ANY] * n,
        out_shape=[jax.ShapeDtypeStruct(r.shape, r.dtype) for r in rs],
        scratch_shapes=[pltpu.SemaphoreType.DMA((n,)), pltpu.SemaphoreType.DMA((n,))])(*rs)


SMALL_ROWS = 24 + 128


def _allreduce_small(vec):
    def body(v_ref, out_ref, slots, send_sems, recv_sems):
        x, y, c = _axes()
        me = 4 * x + 2 * y + c
        slots[me] = v_ref[...]
        cps = []
        for k in range(1, 8):
            kx, ky, kc = (k >> 2) & 1, (k >> 1) & 1, k & 1
            peer = (1 - x if kx else x, 1 - y if ky else y, 1 - c if kc else c)
            cps.append(_rcopy(v_ref, slots.at[me], send_sems, recv_sems, k - 1, peer))
        for cp in cps:
            cp.start()
        for cp in cps:
            cp.wait()
        tot = slots[0]
        for k in range(1, 8):
            tot = tot + slots[k]
        out_ref[...] = tot

    return pl.pallas_call(
        body, name="allreduce_small",
        in_specs=[pl.BlockSpec(memory_space=pltpu.VMEM)], out_specs=pl.BlockSpec(memory_space=pltpu.VMEM),
        out_shape=jax.ShapeDtypeStruct((SMALL_ROWS, 128), F32),
        scratch_shapes=[pltpu.VMEM((8, SMALL_ROWS, 128), F32), pltpu.SemaphoreType.DMA((7,)),
                        pltpu.SemaphoreType.DMA((7,))])(vec)


def _pack_p2(w_uq, w_ukv, w_br_mla, w_br_fox, w_out, dtype):
    parts = [w_uq.reshape(96, D_MODEL), w_ukv.reshape(64, D_MODEL), w_br_mla, w_br_fox, w_out]
    return jnp.concatenate([p.astype(dtype) for p in parts], axis=0)


def _unpack_p2(pk):
    return pk[0:96].reshape(256, 384), pk[96:160].reshape(128, 512), pk[160:416], pk[416:672], pk[672:928]


def _uq_arrange(w):
    w3 = w.reshape(256, HEADS, 96)
    nope = w3[:, :, :64].reshape(256, PAIRS, 128)
    pe = w3[:, :, 64:].reshape(256, PAIRS, 64)
    return jnp.concatenate([nope, pe, jnp.zeros((256, PAIRS, 64), w.dtype)], axis=2).reshape(256, PAIRS * 256)


def _uq_restore(g):
    g3 = g.reshape(256, PAIRS, 256)
    nope = g3[:, :, :128].reshape(256, HEADS, 64)
    pe = g3[:, :, 128:192].reshape(256, HEADS, 32)
    return jnp.concatenate([nope, pe], axis=2).reshape(256, HEADS * 96)


def _ukv_arrange(w):
    w3 = w.reshape(128, HEADS, 128)
    return jnp.concatenate([w3[:, :, :64].reshape(128, 1024), w3[:, :, 64:].reshape(128, 1024)], axis=1)


def _ukv_restore(g):
    kn = g[:, :1024].reshape(128, HEADS, 64)
    vv = g[:, 1024:].reshape(128, HEADS, 64)
    return jnp.concatenate([kn, vv], axis=2).reshape(128, HEADS * 128)


def _rope_tables(lp):
    r = np.arange(lp)
    pos = np.where(r < N_META, r, np.where(r >= PAD, r - PAD + N_META, 0)).astype(np.float32)
    half = MLA_ROPE // 2
    inv_freq = np.float32(ROPE_THETA) ** (-np.arange(half, dtype=np.float32) / np.float32(half))
    ang = (pos[:, None] * inv_freq[None, :]).astype(np.float32)
    cos, sin = np.cos(ang).astype(np.float32), np.sin(ang).astype(np.float32)
    one, zero = np.ones((lp, 64), np.float32), np.zeros((lp, 64), np.float32)
    return (jnp.asarray(np.concatenate([cos, cos, cos, cos, one], axis=1)),
            jnp.asarray(np.concatenate([-sin, sin, -sin, sin, zero], axis=1)))


def _pad_lanes(v, n=128):
    return jnp.pad(v, ((0, 0), (0, n - v.shape[1])))


def _in_cols(slabs, a, b):
    out = []
    for j in range(N_CHIPS):
        lo, hi = max(a, W_IN_SHARD * j), min(b, W_IN_SHARD * (j + 1))
        if lo < hi:
            out.append(slabs[j][:, lo - W_IN_SHARD * j:hi - W_IN_SHARD * j])
    return out


def _local_step(x2, tgt2, meta_f, w_small, w_attn, w_gate, w_uq_f, w_ukv_f, w_bm, w_bf, w_o, pre_norm_g,
                post_norm_g, mla_q_norm_g, mla_kv_norm_g, fox_forget_b, start_exchange=None):
    s_rows = x2.shape[0]
    lp = PAD + s_rows
    w_uq_a = _uq_arrange(w_uq_f)
    w_ukv_a = _ukv_arrange(w_ukv_f)

    ctab, stab = _rope_tables(lp)
    ii = jnp.arange(BLK)
    tri_lo = (ii[:, None] >= ii[None, :]).astype(BF16)
    tri_up = (ii[:, None] <= ii[None, :]).astype(BF16)
    fb128 = _pad_lanes(fox_forget_b)

    u = _rms_pre(x2, meta_f, pre_norm_g)
    small = _mm(u, w_small, mode="nn", out_dtype=F32, name="proj_small")
    attn = _mm(u, w_attn, mode="nn", out_dtype=BF16, name="proj_attn",
               col_scale=(HEADS * HEAD_DIM, FOX_SCALE * LOG2E))
    gate = _mm(u, w_gate, mode="nn", out_dtype=BF16, name="proj_gate")
    qn, kvn, kr, kb = _small_prep(small, mla_q_norm_g, mla_kv_norm_g, fb128, ctab, stab, tri_lo)
    qcat = _mm(qn, w_uq_a, mode="nn", out_dtype=BF16, name="mla_q", row_ins=(ctab, stab),
               epilogue=lambda tile, c, s: _rope_pairs(tile, c, s) * (MLA_SCALE * LOG2E))
    kv = _mm(kvn, w_ukv_a, mode="nn", out_dtype=BF16, name="mla_kv")

    mla_cols = dict(qcol=0, kcol=lambda p: p, vcol=lambda p: PAIRS + p)
    fox_cols = dict(qcol=0, kcol=lambda p: PAIRS + p, vcol=lambda p: 2 * PAIRS + p)
    o_mla, lse_mla = _attn_fwd(qcat, kv, kv, kr, rope=True, name="mla_fwd", **mla_cols)
    o_fox, lse_fox = _attn_fwd(attn, attn, attn, kb, rope=False, name="fox_fwd", **fox_cols)

    a_mla, a_fox = _gate_fwd(o_mla, o_fox, gate)
    y_mla = _mm(a_mla, w_bm, mode="nn", out_dtype=BF16, name="br_mla")
    y_fox = _mm(a_fox, w_bf, mode="nn", out_dtype=BF16, name="br_fox")
    mg = _merge_fwd(gate, y_mla, y_fox)
    mixed = _mm(mg, w_o, mode="nn", out_dtype=F32, name="out_proj")
    dmixed, dy, loss_p, dg_post = _tail(x2, mixed, tgt2, post_norm_g)

    d_w_out = _mm(mg, dmixed, mode="tn", out_dtype=F32, name="d_w_out")
    dm = _mm(dmixed, w_o, mode="nt", out_dtype=BF16, name="d_merge")
    dy_mla, dy_fox, dgate_ab = _merge_bwd(dm, gate, y_mla, y_fox)
    d_w_bm = _mm(a_mla, dy_mla, mode="tn", out_dtype=F32, name="d_w_br_mla")
    d_w_bf = _mm(a_fox, dy_fox, mode="tn", out_dtype=F32, name="d_w_br_fox")
    da_mla = _mm(dy_mla, w_bm, mode="nt", out_dtype=BF16, name="d_a_mla")
    da_fox = _mm(dy_fox, w_bf, mode="nt", out_dtype=BF16, name="d_a_fox")
    do_mla, do_fox, dgate_z, dl_mla, dl_fox = _gate_bwd(da_mla, da_fox, o_mla, o_fox, gate)
    dl_mla, dl_fox = (d[:, :HEADS].T.reshape(PAIRS, 2, lp) for d in (dl_mla, dl_fox))

    dq_a, dkn, dvm, dkr = _attn_bwd(qcat, kv, kv, kr, do_mla, dl_mla, lse_mla, rtabs=(ctab, stab),
                                    scale=MLA_SCALE, name="mla_bwd", **mla_cols)
    dfq, dfk, dfv, dcol, drow = _attn_bwd(attn, attn, attn, kb, do_fox, dl_fox, lse_fox, scale=FOX_SCALE,
                                          name="fox_bwd", **fox_cols)

    d_w_uq_a = _mm(qn, dq_a, mode="tn", out_dtype=F32, name="d_w_uq")
    dqn = _mm(dq_a, w_uq_a, mode="nt", out_dtype=F32, name="d_qn")
    d_w_ukv_a = jnp.concatenate([_mm(kvn, dkn, mode="tn", out_dtype=F32, name="d_w_uk"),
                                 _mm(kvn, dvm, mode="tn", out_dtype=F32, name="d_w_uv")], axis=1)
    dkvn = _mm(dkn, w_ukv_a[:, :1024], mode="nt", out_dtype=F32, name="d_kvn_k")
    dkvn = _mm(dvm, w_ukv_a[:, 1024:], mode="nt", out_dtype=F32, name="d_kvn_v", acc=dkvn)
    dsmall, dg_q, dg_kv, dfb = _small_bwd(small, dqn, dkvn, dkr, dcol, drow, mla_q_norm_g, mla_kv_norm_g,
                                          fb128, ctab, stab, tri_up)

    dw_small = _mm(u, dsmall, mode="tn", out_dtype=BF16, name="d_w_small")
    dw_fq = _mm(u, dfq, mode="tn", out_dtype=BF16, name="d_w_fq")
    dw_fk = _mm(u, dfk, mode="tn", out_dtype=BF16, name="d_w_fk")
    dw_fv = _mm(u, dfv, mode="tn", out_dtype=BF16, name="d_w_fv")
    dw_z = _mm(u, dgate_z, mode="tn", out_dtype=BF16, name="d_w_z")
    dw_g = _mm(u, dgate_ab, mode="tn", out_dtype=BF16, name="d_w_g")
    d_w_in = (dw_small, dw_z, dw_fq, dw_fk, dw_fv, dw_g)
    d_w_uq = _uq_restore(d_w_uq_a)
    d_w_ukv = _ukv_restore(d_w_ukv_a)
    token = start_exchange(d_w_in, d_w_uq, d_w_ukv, d_w_bm, d_w_bf, d_w_out) if start_exchange else None
    du = _mm_sum_nt([(dsmall, w_small), (dfq, w_attn[:, 0:1024]), (dfk, w_attn[:, 1024:2048]),
                     (dfv, w_attn[:, 2048:3072]), (dgate_z, w_gate[:, 0:2048]), (dgate_ab, w_gate[:, 2048:4096])],
                    name="d_u", after=token)
    dx, dmeta, dg_pre = _pre_bwd(du, x2, meta_f, dy, pre_norm_g)
    return (loss_p, dx, dmeta, d_w_in, d_w_uq, d_w_ukv, d_w_bm, d_w_bf, d_w_out, dg_pre, dg_post, dg_q, dg_kv, dfb)


def _w_in_slabs(pieces):
    dw_small, dw_z, dw_fq, dw_fk, dw_fv, dw_g = pieces
    runs = [(dw_small[:, 0:416], C_CQ), (dw_z[:, 0:1024], C_ZMLA), (dw_fq, C_FQ), (dw_fk, C_FK), (dw_fv, C_FV),
            (dw_small[:, 512:528], C_FL), (dw_z[:, 1024:2048], C_ZFOX), (dw_g, C_GA)]
    slabs = []
    for j in range(N_CHIPS):
        lo, hi = W_IN_SHARD * j, W_IN_SHARD * (j + 1)
        cols = [a[:, max(lo, c0) - c0:min(hi, c0 + a.shape[1]) - c0] for a, c0 in runs
                if max(lo, c0) < min(hi, c0 + a.shape[1])]
        slabs.append(jnp.concatenate(cols, axis=1))
    return jnp.stack(slabs, axis=0)


def kernel(x, meta_tokens, pre_norm_g, w_in, fox_forget_b, mla_q_norm_g, mla_kv_norm_g, w_uq, w_ukv, w_br_mla, w_br_fox, w_out, post_norm_g, loss_target, m_meta_tokens, m_pre_norm_g, m_w_in, m_fox_forget_b, m_mla_q_norm_g, m_mla_kv_norm_g, m_w_uq, m_w_ukv, m_w_br_mla, m_w_br_fox, m_w_out, m_post_norm_g, v_meta_tokens, v_pre_norm_g, v_w_in, v_fox_forget_b, v_mla_q_norm_g, v_mla_kv_norm_g, v_w_uq, v_w_ukv, v_w_br_mla, v_w_br_fox, v_w_out, v_post_norm_g):
    me = 2 * lax.axis_index("x") + lax.axis_index("y")
    core = lax.axis_index("c")
    w_in_b = w_in.astype(BF16).reshape(D_MODEL, W_IN_SHARD)
    p2 = _pack_p2(w_uq[0], w_ukv[0], w_br_mla[0], w_br_fox[0], w_out[0], BF16)
    w_in_g, meta_g = _gather_weights([w_in_b], meta_tokens)
    p2_g = _gather_late(lax.optimization_barrier((p2, w_in_g))[0])
    slabs = [jnp.where(me == j, w_in_b, w_in_g[j]) for j in range(N_CHIPS)]
    chip = lax.broadcasted_iota(jnp.int32, (N_CHIPS, 1, 1), 0)
    p2_all = jnp.where(chip == me, p2[None], p2_g)
    w_uq_f = p2_all[:, 0:96].reshape(N_CHIPS, 256, 384).transpose(1, 0, 2).reshape(256, 1536)
    w_ukv_f = p2_all[:, 96:160].reshape(N_CHIPS, 128, 512).transpose(1, 0, 2).reshape(128, 2048)
    w_bm, w_bf, w_o = (p2_all[:, lo:lo + 256].reshape(D_MODEL, D_MODEL) for lo in (160, 416, 672))
    meta_f = jnp.where(chip == me, meta_tokens[None], meta_g).transpose(1, 0, 2).reshape(N_META, D_MODEL)
    kpe = _in_cols(slabs, C_KPE, C_ZMLA)
    w_small = jnp.concatenate(_in_cols(slabs, C_CQ, C_KPE) + kpe + kpe + [jnp.zeros((D_MODEL, 64), BF16)]
                              + _in_cols(slabs, C_FL, C_ZFOX) + [jnp.zeros((D_MODEL, 112), BF16)], axis=1)
    w_attn = jnp.concatenate(_in_cols(slabs, C_FQ, C_FL), axis=1)
    w_gate = jnp.concatenate(_in_cols(slabs, C_ZMLA, C_FQ) + _in_cols(slabs, C_ZFOX, C_END), axis=1)

    exchange = {}

    def start_exchange(d_w_in, d_w_uq, d_w_ukv, d_w_bm, d_w_bf, d_w_out):
        g2 = jnp.concatenate(
            [d_w_uq.reshape(256, N_CHIPS, 384).transpose(1, 0, 2).reshape(N_CHIPS, 96, D_MODEL),
             d_w_ukv.reshape(128, N_CHIPS, 512).transpose(1, 0, 2).reshape(N_CHIPS, 64, D_MODEL)]
            + [g.reshape(N_CHIPS, 256, D_MODEL) for g in (d_w_bm, d_w_bf, d_w_out)], axis=1)
        pieces = [p[None] for p in d_w_in]
        from_sib = _swap_halves(pieces + [g2])
        halves = [_add_cores(p, s, "add_cores_" + nm)[0]
                  for p, s, nm in zip(pieces, from_sib, ("small", "z", "fq", "fk", "fv", "g"))]
        parts = [_w_in_slabs(halves), _add_cores(g2, from_sib[-1], "add_cores_rest")]
        exchange.update(parts=parts, landed=_scatter_chips(parts))
        return parts[0][0, 0:16, 0:LANES]

    (loss_p, dx, dmeta, _, _, _, _, _, _, dg_pre, dg_post, dg_q, dg_kv,
     dfb) = _local_step(x[0], loss_target[0], meta_f, w_small, w_attn, w_gate, w_uq_f, w_ukv_f, w_bm, w_bf, w_o,
                        pre_norm_g, post_norm_g, mla_q_norm_g, mla_kv_norm_g, fox_forget_b, start_exchange)

    mine = [_add_chips(l, lax.dynamic_index_in_dim(p, me, 0, keepdims=False), nm)
            for l, p, nm in zip(exchange["landed"], exchange["parts"], ("add_chips_w_in", "add_chips_rest"))]
    theirs = _swap_reduced(mine)
    g_w_in, g_p2 = [jnp.concatenate([jnp.where(core == 0, a, b), jnp.where(core == 0, b, a)], axis=0)
                    for a, b in zip(mine, theirs)]
    g_w_uq, g_w_ukv, g_w_bm, g_w_bf, g_w_out = _unpack_p2(g_p2)
    g_w_in = g_w_in[None]

    vec = jnp.concatenate([dg_pre.reshape(8, 128), dg_post.reshape(8, 128), dg_q.reshape(2, 128), dg_kv,
                           dfb, _pad_lanes(loss_p), jnp.zeros((3, 128), F32), dmeta.reshape(128, 128)], axis=0)
    tot = _allreduce_small(vec)
    loss = tot[20, 0]
    g_meta = lax.dynamic_slice_in_dim(tot[24:].reshape(N_META, D_MODEL), 256 * me, 256, axis=1)

    def small_pack(pre, post, gq_, gkv_, fb_):
        return jnp.concatenate([pre.reshape(8, 128), post.reshape(8, 128), gq_.reshape(2, 128), gkv_,
                                _pad_lanes(fb_), jnp.zeros((4, 128), F32)], axis=0)

    def small_unpack(t):
        return (t[0:8].reshape(1, 1024), t[8:16].reshape(1, 1024), t[16:18].reshape(1, 256), t[18:19],
                t[19:20, 0:HEADS])

    g_small = jnp.concatenate([tot[0:20], jnp.zeros((4, 128), F32)], axis=0)
    sm = _adamw(small_pack(pre_norm_g, post_norm_g, mla_q_norm_g, mla_kv_norm_g, fox_forget_b), g_small,
                small_pack(m_pre_norm_g, m_post_norm_g, m_mla_q_norm_g, m_mla_kv_norm_g, m_fox_forget_b),
                small_pack(v_pre_norm_g, v_post_norm_g, v_mla_q_norm_g, v_mla_kv_norm_g, v_fox_forget_b),
                "adamw_small")
    g_pre, g_post, g_q, g_kv, g_fb = small_unpack(g_small)
    (d_pre, d_post, d_q, d_kv, d_fb), (nm_pre, nm_post, nm_q, nm_kv, nm_fb), (nv_pre, nv_post, nv_q, nv_kv, nv_fb) = (
        small_unpack(t) for t in sm)

    d_meta, nm_meta, nv_meta = _adamw(meta_tokens, g_meta, m_meta_tokens, v_meta_tokens, "adamw_meta")
    d_win, nm_win, nv_win = (t.T[None] for t in _adamw(w_in[0].T, g_w_in[0].T, m_w_in[0].T, v_w_in[0].T,
                                                       "adamw_w_in"))
    d_wuq, nm_wuq, nv_wuq = _adamw(w_uq[0], g_w_uq, m_w_uq[0], v_w_uq[0], "adamw_w_uq")
    d_wukv, nm_wukv, nv_wukv = _adamw(w_ukv[0], g_w_ukv, m_w_ukv[0], v_w_ukv[0], "adamw_w_ukv")
    d_wbm, nm_wbm, nv_wbm = _adamw(w_br_mla[0], g_w_bm, m_w_br_mla[0], v_w_br_mla[0], "adamw_w_br_mla")
    d_wbf, nm_wbf, nv_wbf = _adamw(w_br_fox[0], g_w_bf, m_w_br_fox[0], v_w_br_fox[0], "adamw_w_br_fox")
    d_wo, nm_wo, nv_wo = _adamw(w_out[0], g_w_out, m_w_out[0], v_w_out[0], "adamw_w_out")

    def group(meta_, pre, win, fb_, q_, kv_, wuq, wukv, wbm, wbf, wo, post):
        return (meta_, pre, win, fb_, q_, kv_, wuq[None], wukv[None], wbm[None], wbf[None], wo[None], post)

    grads = group(g_meta, g_pre, g_w_in, g_fb, g_q, g_kv, g_w_uq, g_w_ukv, g_w_bm, g_w_bf, g_w_out, g_post)
    deltas = group(d_meta, d_pre, d_win, d_fb, d_q, d_kv, d_wuq, d_wukv, d_wbm, d_wbf, d_wo, d_post)
    new_m = group(nm_meta, nm_pre, nm_win, nm_fb, nm_q, nm_kv, nm_wuq, nm_wukv, nm_wbm, nm_wbf, nm_wo, nm_post)
    new_v = group(nv_meta, nv_pre, nv_win, nv_fb, nv_q, nv_kv, nv_wuq, nv_wukv, nv_wbm, nv_wbf, nv_wo, nv_post)
    return (loss, dx[None], *grads, *deltas, *new_m, *new_v)
```

```python
import math

import jax
import jax.numpy as jnp
import numpy as np
from jax import lax
from jax.experimental import pallas as pl
from jax.experimental.pallas import tpu as pltpu
from jax.experimental.pallas import tpu_sc as plsc

F32 = jnp.float32
BF16 = jnp.bfloat16

D_MODEL = 1024
N_META = 16
RMS_EPS = 1e-6
HEADS = 16
PAIRS = HEADS // 2
HEAD_DIM = 64
LANES = 128
MLA_ROPE = 32
AHEAD = 6
BIAS_PARTS = 3
MLA_SCALE = 1.0 / math.sqrt(64 + 32)
FOX_SCALE = 1.0 / math.sqrt(64)
LOG2E = math.log2(math.e)
LN2 = math.log(2.0)
ROPE_THETA = 10000.0

PAD = 256
BLK = 256
QB = 512
UNROLL = 4
NEG = -1e30

C_CQ, C_CKV, C_KPE, C_ZMLA, C_FQ, C_FK, C_FV, C_FL, C_ZFOX, C_GA, C_GB, C_END = (
    0, 256, 384, 416, 1440, 2464, 3488, 4512, 4528, 5552, 6576, 7600)
SMALL_W = 640
W_IN_SHARD = 1900

P2_ROWS = 928
N_CHIPS = 4

ADAM_LR = 0.001
ADAM_B1 = 0.9
ADAM_B2 = 0.999
ADAM_EPS = 1e-08
ADAM_WD = 0.01
ADAM_STEP = 10

VMEM_BIG = 56 * 1024 * 1024
MM_VMEM_BUDGET = 44 * 1024 * 1024
MESH = pl.DeviceIdType.MESH


def _cp(dims, vmem=None):
    return pltpu.CompilerParams(dimension_semantics=dims, vmem_limit_bytes=vmem)


def _dot(a, b, ca, cb):
    return lax.dot_general(a, b, (((ca,), (cb,)), ((), ())), preferred_element_type=F32)


def _sigmoid(x):
    return 1.0 / (1.0 + jnp.exp(-x))


def _tile(n, cands):
    for c in cands:
        if n % c == 0:
            return c
    return n


def _mm(a, b, *, mode, out_dtype, name, acc=None, epilogue=None, row_ins=(), after=None, col_scale=None):
    if mode == "nn":
        (M, K), N = a.shape, b.shape[1]
    elif mode == "nt":
        (M, K), N = a.shape, b.shape[0]
    else:
        (K, M), N = a.shape, b.shape[1]
    tm = _tile(M, (1088, 1024)) if M > 1024 else M
    tn = _tile(N, (1024,)) if N > 1024 else N
    nk = 1
    while True:
        tk = K // nk
        need = 2 * tk * (tm * a.dtype.itemsize + tn * b.dtype.itemsize) + tm * tn * (
            2 * jnp.dtype(out_dtype).itemsize + (8 if acc is not None else 0) + (4 if nk > 1 else 0))
        if need <= MM_VMEM_BUDGET or (tk // 2) % (16 if mode == "tn" else LANES) or tk <= 512:
            break
        nk *= 2
    while (M // tm) * (N // tn) * nk < 4 and tn % 512 == 0:
        tn //= 2
    assert col_scale is None or (nk == 1 and col_scale[0] % tn == 0)
    ca, cb = {"nn": (1, 0), "nt": (1, 1), "tn": (0, 0)}[mode]
    a_spec = (pl.BlockSpec((tk, tm), lambda j, i, k: (k, i)) if mode == "tn"
              else pl.BlockSpec((tm, tk), lambda j, i, k: (i, k)))
    b_spec = (pl.BlockSpec((tn, tk), lambda j, i, k: (j, k)) if mode == "nt"
              else pl.BlockSpec((tk, tn), lambda j, i, k: (k, j)))
    o_spec = pl.BlockSpec((tm, tn), lambda j, i, k: (i, j))
    has_acc = acc is not None

    nrow = len(row_ins)

    def body(*refs):
        a_ref, b_ref = refs[0], refs[1]
        acc_ref = refs[2] if has_acc else None
        rows = refs[2 + has_acc:2 + has_acc + nrow]
        o_ref = refs[2 + has_acc + nrow + (after is not None)]

        def store(tile):
            if epilogue is not None:
                tile = epilogue(tile, *[r[...] for r in rows])
            if col_scale is not None:
                tile = tile * jnp.where(pl.program_id(0) * tn < col_scale[0], col_scale[1], 1.0)
            o_ref[...] = tile.astype(out_dtype)

        part = _dot(a_ref[...].astype(BF16), b_ref[...].astype(BF16), ca, cb)
        if nk == 1:
            store(part + acc_ref[...] if has_acc else part)
        else:
            sc = refs[-1]
            k = pl.program_id(2)

            @pl.when(k == 0)
            def _():
                sc[...] = part + acc_ref[...] if has_acc else part

            @pl.when(k > 0)
            def _():
                sc[...] += part

            @pl.when(k == nk - 1)
            def _():
                store(sc[...])

    ins = [a, b] + ([acc] if has_acc else []) + list(row_ins)
    in_specs = ([a_spec, b_spec] + ([o_spec] if has_acc else [])
                + [pl.BlockSpec((tm, r.shape[1]), lambda j, i, k: (i, 0)) for r in row_ins])
    if after is not None:
        ins.append(after)
        in_specs.append(pl.BlockSpec(after.shape, lambda j, i, k: (0,) * after.ndim))
    return pl.pallas_call(
        body, name=name, grid=(N // tn, M // tm, nk), in_specs=in_specs, out_specs=o_spec,
        out_shape=jax.ShapeDtypeStruct((M, N), out_dtype),
        scratch_shapes=[pltpu.VMEM((tm, tn), F32)] if nk > 1 else [],
        compiler_params=_cp(("parallel", "parallel", "arbitrary"), VMEM_BIG))(*ins)


def _mm_sum_nt(pairs, *, name, after=None):
    n = len(pairs)
    M, N = pairs[0][0].shape[0], pairs[0][1].shape[0]
    tm = _tile(M, (272,))

    def body(*refs):
        o_ref = refs[2 * n + (after is not None)]
        tot = _dot(refs[0][...].astype(BF16), refs[n][...].astype(BF16), 1, 1)
        for i in range(1, n):
            tot = tot + _dot(refs[i][...].astype(BF16), refs[n + i][...].astype(BF16), 1, 1)
        o_ref[...] = tot

    ins = [a for a, _ in pairs] + [b for _, b in pairs]
    in_specs = ([pl.BlockSpec((tm, a.shape[1]), lambda i: (i, 0)) for a, _ in pairs]
                + [pl.BlockSpec(b.shape, lambda i: (0, 0)) for _, b in pairs])
    if after is not None:
        ins.append(after)
        in_specs.append(pl.BlockSpec(after.shape, lambda i: (0,) * after.ndim))
    return pl.pallas_call(
        body, name=name, grid=(M // tm,), in_specs=in_specs, out_specs=pl.BlockSpec((tm, N), lambda i: (i, 0)),
        out_shape=jax.ShapeDtypeStruct((M, N), F32), compiler_params=_cp(("parallel",), VMEM_BIG))(*ins)


def _row(w):
    return pl.BlockSpec((BLK, w), lambda i: (i, 0))


def _rowc(w, c):
    return pl.BlockSpec((BLK, w), lambda i: (i, c))


def _full(shape):
    return pl.BlockSpec(shape, lambda i: tuple(0 for _ in shape))


def _rope(x, c, s):
    lane = lax.broadcasted_iota(jnp.int32, x.shape, 1)
    is_x1 = ((lane >> 4) & 1) == 0
    partner = jnp.where(is_x1, pltpu.roll(x, LANES - 16, 1), pltpu.roll(x, 16, 1))
    return x * c + partner * s


def _row_valid(i):
    rows = i * BLK + lax.broadcasted_iota(jnp.int32, (BLK, 1), 0)
    return (rows < N_META) | (rows >= PAD)


def _shift_rows(w):
    return pl.BlockSpec((BLK, w), lambda i: (jnp.maximum(i - 1, 0), 0))


def _h_block(i, x_ref, meta_ref):
    head = jnp.concatenate([meta_ref[...], jnp.zeros((BLK - N_META, D_MODEL), F32)], axis=0)
    return jnp.where(i == 0, head, x_ref[...])


def _rms_pre(x2, meta, g):
    lp = PAD + x2.shape[0]

    def body(x_ref, meta_ref, g_ref, u_ref):
        hv = _h_block(pl.program_id(0), x_ref, meta_ref)
        r = lax.rsqrt(jnp.mean(hv * hv, axis=-1, keepdims=True) + RMS_EPS)
        u_ref[...] = (hv * r * g_ref[...]).astype(BF16)

    return pl.pallas_call(
        body, name="rms_pre", grid=(lp // BLK,),
        in_specs=[_shift_rows(D_MODEL), _full((N_META, D_MODEL)), _full((1, D_MODEL))], out_specs=_row(D_MODEL),
        out_shape=jax.ShapeDtypeStruct((lp, D_MODEL), BF16),
        compiler_params=_cp(("parallel",)))(x2, meta, g)


def _split3(x):
    hi = x.astype(BF16)
    r1 = x - hi.astype(F32)
    mid = r1.astype(BF16)
    lo = (r1 - mid.astype(F32)).astype(BF16)
    return hi, mid, lo


def _small_prep(small, gq, gkv, fb, ctab, stab, tri):
    lp = small.shape[0]

    def body(sm_ref, gq_ref, gkv_ref, fb_ref, c_ref, s_ref, tri_ref, qn_ref, kvn_ref, kr_ref, kb_ref, carry):
        i = pl.program_id(0)

        @pl.when(i == 0)
        def _():
            carry[...] = jnp.zeros_like(carry)

        cq = sm_ref[:, 0:256]
        r = lax.rsqrt(jnp.mean(cq * cq, axis=-1, keepdims=True) + RMS_EPS)
        qn_ref[...] = (cq * r * gq_ref[...]).astype(BF16)
        ckv = sm_ref[:, 256:384]
        r = lax.rsqrt(jnp.mean(ckv * ckv, axis=-1, keepdims=True) + RMS_EPS)
        kvn_ref[...] = (ckv * r * gkv_ref[...]).astype(BF16)
        kr_ref[...] = _rope(sm_ref[:, 384:512], c_ref[...], s_ref[...]).astype(BF16)
        fl = sm_ref[:, 512:640] + fb_ref[...]
        lf = jnp.minimum(fl, 0.0) - jnp.log(1.0 + jnp.exp(-jnp.abs(fl)))
        lf = jnp.where(_row_valid(i), lf, 0.0)
        hi, mid, lo = _split3(lf)
        t = tri_ref[...]
        cum = (_dot(t, hi, 1, 0) + _dot(t, mid, 1, 0)) + _dot(t, lo, 1, 0) + carry[...]
        carry[...] = cum[BLK - 1:BLK, :]
        src = lax.broadcasted_iota(jnp.int32, (LANES, LANES), 0)
        dst = lax.broadcasted_iota(jnp.int32, (LANES, LANES), 1)
        kb = jnp.zeros((BLK, LANES), F32)
        for j, part in enumerate(_split3(-cum * LOG2E)):
            spread = ((dst == BIAS_PARTS * src + j) & (src < HEADS)).astype(BF16)
            kb = kb + _dot(part, spread, 1, 0)
        kb_ref[...] = kb.astype(BF16)

    return pl.pallas_call(
        body, name="small_prep", grid=(lp // BLK,),
        in_specs=[_row(SMALL_W), _full((1, 256)), _full((1, 128)), _full((1, 128)), _row(128), _row(128),
                  _full((BLK, BLK))],
        out_specs=[_row(256), _row(128), _row(128), _row(128)],
        out_shape=[jax.ShapeDtypeStruct((lp, 256), BF16), jax.ShapeDtypeStruct((lp, 128), BF16),
                   jax.ShapeDtypeStruct((lp, 128), BF16), jax.ShapeDtypeStruct((lp, 128), BF16)],
        scratch_shapes=[pltpu.VMEM((1, 128), F32)],
        compiler_params=_cp(("arbitrary",)))(small, gq, gkv, fb, ctab, stab, tri)


def _rope_pairs(tile, c, s):
    out = []
    for lo in range(0, tile.shape[1], 256):
        out += [tile[:, lo:lo + 128], _rope(tile[:, lo + 128:lo + 256], c, s)]
    return jnp.concatenate(out, axis=1)


def _gate_fwd(o_mla, o_fox, gate):
    lp = o_mla.shape[0]

    def body(om_ref, of_ref, zm_ref, zf_ref, am_ref, af_ref):
        zm = zm_ref[...].astype(F32)
        am_ref[...] = (om_ref[...] * (zm * _sigmoid(zm))).astype(BF16)
        zf = zf_ref[...].astype(F32)
        af_ref[...] = (of_ref[...] * (zf * _sigmoid(zf))).astype(BF16)

    return pl.pallas_call(
        body, name="gate_fwd", grid=(lp // BLK,),
        in_specs=[_row(D_MODEL), _row(D_MODEL), _rowc(D_MODEL, 0), _rowc(D_MODEL, 1)],
        out_specs=[_row(D_MODEL), _row(D_MODEL)],
        out_shape=[jax.ShapeDtypeStruct((lp, D_MODEL), BF16)] * 2,
        compiler_params=_cp(("parallel",)))(o_mla, o_fox, gate, gate)


def _merge_fwd(gate, y_mla, y_fox):
    lp = y_mla.shape[0]

    def body(ga_ref, gb_ref, ym_ref, yf_ref, m_ref):
        sa = _sigmoid(ga_ref[...].astype(F32))
        sb = _sigmoid(gb_ref[...].astype(F32))
        m_ref[...] = (sa * ym_ref[...] + sb * yf_ref[...]).astype(BF16)

    return pl.pallas_call(
        body, name="merge_fwd", grid=(lp // BLK,),
        in_specs=[_rowc(D_MODEL, 2), _rowc(D_MODEL, 3), _row(D_MODEL), _row(D_MODEL)],
        out_specs=_row(D_MODEL), out_shape=jax.ShapeDtypeStruct((lp, D_MODEL), BF16),
        compiler_params=_cp(("parallel",)))(gate, gate, y_mla, y_fox)


def _tail(x2, mixed, tgt, gpost):
    lp = mixed.shape[0]
    shift = _shift_rows(D_MODEL)

    def body(h_ref, mx_ref, t_ref, g_ref, dmx_ref, dy_ref, loss_ref, dg_ref):
        i = pl.program_id(0)

        @pl.when(i == 0)
        def _():
            loss_ref[...] = jnp.zeros_like(loss_ref)
            dg_ref[...] = jnp.zeros_like(dg_ref)
            dmx_ref[...] = jnp.zeros_like(dmx_ref)
            dy_ref[...] = jnp.zeros_like(dy_ref)

        @pl.when(i > 0)
        def _():
            mx = mx_ref[...]
            g = g_ref[...]
            r = lax.rsqrt(jnp.mean(mx * mx, axis=-1, keepdims=True) + RMS_EPS)
            nrm = mx * r
            e = (h_ref[...] + nrm * g) - t_ref[...]
            loss_ref[...] += jnp.sum(0.5 * jnp.sum(e * e, axis=-1, keepdims=True) * (1.0 / D_MODEL),
                                     axis=0, keepdims=True)
            dy = e * (1.0 / D_MODEL)
            dy_ref[...] = dy
            dg_ref[...] += jnp.sum(dy * nrm, axis=0, keepdims=True)
            w = dy * g
            dot = jnp.mean(w * mx, axis=-1, keepdims=True)
            dmx_ref[...] = (r * w - mx * (r * r * r * dot)).astype(BF16)

    return pl.pallas_call(
        body, name="tail", grid=(lp // BLK,),
        in_specs=[shift, _row(D_MODEL), shift, _full((1, D_MODEL))],
        out_specs=[_row(D_MODEL), _row(D_MODEL), _full((1, 1)), _full((1, D_MODEL))],
        out_shape=[jax.ShapeDtypeStruct((lp, D_MODEL), BF16), jax.ShapeDtypeStruct((lp, D_MODEL), F32),
                   jax.ShapeDtypeStruct((1, 1), F32), jax.ShapeDtypeStruct((1, D_MODEL), F32)],
        compiler_params=_cp(("arbitrary",)))(x2, mixed, tgt, gpost)


def _merge_bwd(dm, gate, y_mla, y_fox):
    lp = dm.shape[0]

    def body(dm_ref, ga_ref, gb_ref, ym_ref, yf_ref, dym_ref, dyf_ref, dg_ref):
        dm_v = dm_ref[...].astype(F32)
        sa = _sigmoid(ga_ref[...].astype(F32))
        sb = _sigmoid(gb_ref[...].astype(F32))
        dym_ref[...] = (dm_v * sa).astype(BF16)
        dyf_ref[...] = (dm_v * sb).astype(BF16)
        dg_ref[:, 0:D_MODEL] = (dm_v * ym_ref[...] * (sa * (1.0 - sa))).astype(BF16)
        dg_ref[:, D_MODEL:2 * D_MODEL] = (dm_v * yf_ref[...] * (sb * (1.0 - sb))).astype(BF16)

    return pl.pallas_call(
        body, name="merge_bwd", grid=(lp // BLK,),
        in_specs=[_row(D_MODEL), _rowc(D_MODEL, 2), _rowc(D_MODEL, 3), _row(D_MODEL), _row(D_MODEL)],
        out_specs=[_row(D_MODEL), _row(D_MODEL), _row(2 * D_MODEL)],
        out_shape=[jax.ShapeDtypeStruct((lp, D_MODEL), BF16), jax.ShapeDtypeStruct((lp, D_MODEL), BF16),
                   jax.ShapeDtypeStruct((lp, 2 * D_MODEL), BF16)],
        compiler_params=_cp(("parallel",)))(dm, gate, gate, y_mla, y_fox)


def _gate_bwd(da_mla, da_fox, o_mla, o_fox, gate):
    lp = da_mla.shape[0]

    def one(da, o, z, head_of_col):
        sg = _sigmoid(z)
        do = (da * (z * sg)).astype(BF16)
        dz = da * o * (sg * (1.0 + z * (1.0 - sg)))
        delta = sum(_dot(part, head_of_col, 1, 0) for part in _split3(do.astype(F32) * o))
        return do, dz.astype(BF16), delta

    def body(dam_ref, daf_ref, om_ref, of_ref, zm_ref, zf_ref, dom_ref, dof_ref, dz_ref, dlm_ref, dlf_ref):
        f32 = lambda r: r[...].astype(F32)
        head_of_col = (lax.broadcasted_iota(jnp.int32, (D_MODEL, LANES), 0) // HEAD_DIM
                       == lax.broadcasted_iota(jnp.int32, (D_MODEL, LANES), 1)).astype(BF16)
        dom_ref[...], dz_ref[:, 0:D_MODEL], dlm_ref[...] = one(f32(dam_ref), f32(om_ref), f32(zm_ref), head_of_col)
        dof_ref[...], dz_ref[:, D_MODEL:2 * D_MODEL], dlf_ref[...] = one(f32(daf_ref), f32(of_ref), f32(zf_ref),
                                                                        head_of_col)

    return pl.pallas_call(
        body, name="gate_bwd", grid=(lp // BLK,),
        in_specs=[_row(D_MODEL)] * 4 + [_rowc(D_MODEL, 0), _rowc(D_MODEL, 1)],
        out_specs=[_row(D_MODEL), _row(D_MODEL), _row(2 * D_MODEL), _row(LANES), _row(LANES)],
        out_shape=[jax.ShapeDtypeStruct((lp, D_MODEL), BF16), jax.ShapeDtypeStruct((lp, D_MODEL), BF16),
                   jax.ShapeDtypeStruct((lp, 2 * D_MODEL), BF16), jax.ShapeDtypeStruct((lp, LANES), F32),
                   jax.ShapeDtypeStruct((lp, LANES), F32)],
        compiler_params=_cp(("parallel",)))(da_mla, da_fox, o_mla, o_fox, gate, gate)


def _small_bwd(small, dqn, dkvn, dkr, dcol_t, drow_t, gq, gkv, fb, ctab, stab, triu):
    lp = small.shape[0]
    nb = lp // BLK

    def rrow(w):
        return pl.BlockSpec((BLK, w), lambda i: (nb - 1 - i, 0))

    def body(sm_ref, dqn_ref, dkvn_ref, dkr_ref, dcol_ref, drow_ref, gq_ref, gkv_ref, fb_ref, c_ref, s_ref, tri_ref,
             ds_ref, dgq_ref, dgkv_ref, dfb_ref, carry):
        i = pl.program_id(0)

        @pl.when(i == 0)
        def _():
            carry[...] = jnp.zeros_like(carry)
            dgq_ref[...] = jnp.zeros_like(dgq_ref)
            dgkv_ref[...] = jnp.zeros_like(dgkv_ref)
            dfb_ref[...] = jnp.zeros_like(dfb_ref)

        def norm_bwd(x, dn, g, dg_ref):
            r = lax.rsqrt(jnp.mean(x * x, axis=-1, keepdims=True) + RMS_EPS)
            dg_ref[...] += jnp.sum(dn * (x * r), axis=0, keepdims=True)
            w = dn * g
            dot = jnp.mean(w * x, axis=-1, keepdims=True)
            return r * w - x * (r * r * r * dot)

        ds_ref[:, 0:256] = norm_bwd(sm_ref[:, 0:256], dqn_ref[...], gq_ref[...], dgq_ref).astype(BF16)
        ds_ref[:, 256:384] = norm_bwd(sm_ref[:, 256:384], dkvn_ref[...], gkv_ref[...], dgkv_ref).astype(BF16)

        dk = dkr_ref[0]
        for p in range(1, PAIRS):
            dk = dk + dkr_ref[p]
        dk = _rope(dk, c_ref[...], -s_ref[...])
        lane = lax.broadcasted_iota(jnp.int32, dk.shape, 1)
        dk = jnp.where(lane < MLA_ROPE, dk + pltpu.roll(dk, LANES - MLA_ROPE, 1), 0.0)
        ds_ref[:, 384:512] = dk.astype(BF16)

        dcol = dcol_ref[0]
        for p in range(1, PAIRS):
            dcol = dcol + pltpu.roll(dcol_ref[p], 2 * p, 1)
        rows16 = jnp.concatenate([drow_ref[p, h:h + 1, :] for p in range(PAIRS) for h in range(2)], axis=0)
        eye = (lax.broadcasted_iota(jnp.int32, (HEADS, LANES), 0)
               == lax.broadcasted_iota(jnp.int32, (HEADS, LANES), 1)).astype(BF16)
        drow = sum(_dot(part, eye, 0, 0) for part in _split3(rows16))
        dcr = dcol - drow
        hi, mid, lo = _split3(dcr)
        t = tri_ref[...]
        suf = (_dot(t, hi, 1, 0) + _dot(t, mid, 1, 0)) + _dot(t, lo, 1, 0) + carry[...]
        fl = sm_ref[:, 512:640] + fb_ref[...]
        dfl = jnp.where(_row_valid(nb - 1 - i), -suf * _sigmoid(-fl), 0.0)
        ds_ref[:, 512:640] = dfl.astype(BF16)
        dfb_ref[...] += jnp.sum(dfl, axis=0, keepdims=True)
        carry[...] += jnp.sum(dcr, axis=0, keepdims=True)

    return pl.pallas_call(
        body, name="small_bwd", grid=(nb,),
        in_specs=[rrow(SMALL_W), rrow(256), rrow(128),
                  pl.BlockSpec((PAIRS, BLK, 128), lambda i: (0, nb - 1 - i, 0)),
                  pl.BlockSpec((PAIRS, BLK, 128), lambda i: (0, nb - 1 - i, 0)),
                  pl.BlockSpec((PAIRS, 2, BLK), lambda i: (0, 0, nb - 1 - i)),
                  _full((1, 256)), _full((1, 128)), _full((1, 128)), rrow(128), rrow(128), _full((BLK, BLK))],
        out_specs=[rrow(SMALL_W), _full((1, 256)), _full((1, 128)), _full((1, 128))],
        out_shape=[jax.ShapeDtypeStruct((lp, SMALL_W), BF16), jax.ShapeDtypeStruct((1, 256), F32),
                   jax.ShapeDtypeStruct((1, 128), F32), jax.ShapeDtypeStruct((1, 128), F32)],
        scratch_shapes=[pltpu.VMEM((1, 128), F32)],
        compiler_params=_cp(("arbitrary",)))(small, dqn, dkvn, dkr, dcol_t, drow_t, gq, gkv, fb, ctab, stab, triu)


def _pre_bwd(du, x2, meta, dy, gpre):
    s_rows = x2.shape[0]
    lp = PAD + s_rows
    shift = _shift_rows(D_MODEL)

    def body(du_ref, x_ref, meta_ref, dy_ref, g_ref, dx_ref, dmeta_ref, dg_ref):
        i = pl.program_id(0)

        @pl.when(i == 0)
        def _():
            dg_ref[...] = jnp.zeros_like(dg_ref)

        hv = _h_block(i, x_ref, meta_ref)
        duv = du_ref[...]
        r = lax.rsqrt(jnp.mean(hv * hv, axis=-1, keepdims=True) + RMS_EPS)
        dg_ref[...] += jnp.sum(duv * (hv * r), axis=0, keepdims=True)
        w = duv * g_ref[...]
        dot = jnp.mean(w * hv, axis=-1, keepdims=True)
        dh = dy_ref[...] + (r * w - hv * (r * r * r * dot))
        dx_ref[...] = dh

        @pl.when(i == 0)
        def _():
            dmeta_ref[...] = dh[0:N_META, :]

    return pl.pallas_call(
        body, name="pre_bwd", grid=(lp // BLK,),
        in_specs=[_row(D_MODEL), shift, _full((N_META, D_MODEL)), _row(D_MODEL), _full((1, D_MODEL))],
        out_specs=[shift, _full((N_META, D_MODEL)), _full((1, D_MODEL))],
        out_shape=[jax.ShapeDtypeStruct((s_rows, D_MODEL), F32), jax.ShapeDtypeStruct((N_META, D_MODEL), F32),
                   jax.ShapeDtypeStruct((1, D_MODEL), F32)],
        compiler_params=_cp(("arbitrary",)))(du, x2, meta, dy, gpre)


def _pair_masks(rope, pair):
    lane = lax.broadcasted_iota(jnp.int32, (1, LANES), 1)
    mas = [lane < HEAD_DIM, lane >= HEAD_DIM]
    wide = lax.broadcasted_iota(jnp.int32, (1, 2 * LANES), 1)
    extra = MLA_ROPE if rope else BIAS_PARTS
    lo = LANES if rope else LANES + 2 * BIAS_PARTS * pair
    mid = lo + extra
    return mas, [(wide < HEAD_DIM) | ((wide >= lo) & (wide < mid)),
                 ((wide >= HEAD_DIM) & (wide < LANES)) | ((wide >= mid) & (wide < mid + extra))]


def _mask2(x, masks):
    return [jnp.where(m, x, jnp.zeros_like(x)) for m in masks]


def _q_heads(q_rows, rope, mas, hmask):
    if rope:
        return _mask2(q_rows, hmask)
    zero = jnp.zeros((q_rows.shape[0], LANES), BF16)
    return [jnp.concatenate([jnp.where(m, q_rows, zero), jnp.where(hm[:, LANES:], zero + 1, zero)], axis=1)
            for m, hm in zip(mas, hmask)]


def _attn_fwd(q, k, v, vt, k2, *, rope, qcol, kcol, vcol, name):
    lp = q.shape[0]
    nq = 1 + (lp - PAD) // QB
    qw = 256 if rope else 128

    def body(q_ref, k_ref, v_ref, vt_ref, k2_ref, o_ref, lse_ref):
        i = pl.program_id(1)
        r0 = pl.multiple_of(jnp.where(i == 0, 0, PAD + QB * (i - 1)), BLK)
        b0 = r0 // BLK
        mas, hmask = _pair_masks(rope, pl.program_id(0))
        qh = _q_heads(q_ref[pl.ds(r0, QB), :], rope, mas, hmask)

        def update(kcs, carry, masks, ns=None, q_lo=0, wq=QB):
            ns = ns or [BLK] * len(kcs)
            m, l = [carry[0], carry[2]], [carry[1], carry[3]]
            acc = [carry[4][0:HEAD_DIM], carry[4][HEAD_DIM:LANES]]
            qs = [x[q_lo:q_lo + wq] for x in qh]
            k0s = [pl.multiple_of(kc * BLK, BLK) for kc in kcs]
            kks = [jnp.concatenate([k_ref[pl.ds(k0, n), :], k2_ref[pl.ds(k0, n), :]], axis=1)
                   for k0, n in zip(k0s, ns)]
            tiles = [(j, h) for j in range(len(kcs)) for h in range(2)]
            score = lambda t: _dot(kks[tiles[t][0]], qs[tiles[t][1]], 1, 1)
            ss = [score(t) for t in range(min(AHEAD, len(tiles)))]
            for t, (j, h) in enumerate(tiles):
                if t + AHEAD < len(tiles):
                    ss.append(score(t + AHEAD))
                s = ss[t] if masks[j] is None else jnp.where(masks[j], ss[t], NEG)
                m_new = jnp.maximum(m[h], jnp.max(s, axis=0, keepdims=True))
                alpha = jnp.exp2(m[h] - m_new)
                p = jnp.exp2(s - m_new)
                l[h] = alpha * l[h] + jnp.sum(p, axis=0, keepdims=True)
                m[h] = m_new
                if ns[j] == BLK:
                    pv = _dot(vt_ref[pl.ds(HEAD_DIM * h, HEAD_DIM), pl.ds(k0s[j], BLK)], p.astype(BF16), 1, 0)
                else:
                    vm = jnp.where(mas[h], v_ref[0:ns[j], :], jnp.zeros((), BF16))
                    pv = _dot(vm, p.astype(BF16), 0, 0)[HEAD_DIM * h:HEAD_DIM * (h + 1)]
                acc[h] = alpha * acc[h] + pv
            return (m[0], l[0], m[1], l[1], jnp.concatenate(acc, axis=0))

        neg = jnp.full((1, QB), NEG, F32)
        zero = jnp.zeros((1, QB), F32)
        c = (neg, zero, neg, zero, jnp.zeros((LANES, QB), F32))
        n_mid = jnp.maximum(b0 - 1, 0)
        c = lax.fori_loop(0, n_mid // 4, lambda t, cr: update([4 * t + u for u in (1, 2, 3, 4)], cr, [None] * 4), c)
        c = lax.fori_loop(0, (n_mid % 4) // 2, lambda t, cr: update([n_mid - 1, n_mid], cr, [None, None]), c)
        key_l = lax.broadcasted_iota(jnp.int32, (BLK, BLK), 0)
        qry_l = lax.broadcasted_iota(jnp.int32, (BLK, BLK), 1)
        tri = (key_l <= qry_l) & (b0 > 0)
        meta_ok = (key_l[0:N_META] <= qry_l[0:N_META]) | (b0 > 0)
        lo = update([0, b0], tuple(a[:, 0:BLK] for a in c), [meta_ok, tri], ns=[N_META, BLK], q_lo=0, wq=BLK)
        hi = update([0, b0, b0 + 1], tuple(a[:, BLK:QB] for a in c), [None, None, tri], ns=[N_META, BLK, BLK],
                    q_lo=BLK, wq=QB - BLK)
        c = tuple(jnp.concatenate([a, b], axis=1) for a, b in zip(lo, hi))
        inv =jnp.concatenate([jnp.broadcast_to(1.0 / c[1], (HEAD_DIM, QB)),
                               jnp.broadcast_to(1.0 / c[3], (HEAD_DIM, QB))], axis=0)
        o_t = (c[4] * inv).T.astype(BF16)
        lses = [c[2 * h] + jnp.log(c[2 * h + 1]) * LOG2E for h in range(2)]
        o_ref[pl.ds(r0, BLK), :] = o_t[0:BLK]
        for h in range(2):
            lse_ref[0, h:h + 1, pl.ds(r0, BLK)] = lses[h][:, 0:BLK]

        @pl.when(i > 0)
        def _():
            r1 = pl.multiple_of(r0 + BLK, BLK)
            o_ref[pl.ds(r1, QB - BLK), :] = o_t[BLK:QB]
            for h in range(2):
                lse_ref[0, h:h + 1, pl.ds(r1, QB - BLK)] = lses[h][:, BLK:QB]

    in_specs = [pl.BlockSpec((lp, qw), lambda p, i: (0, qcol + p)),
                pl.BlockSpec((lp, 128), lambda p, i: (0, kcol(p))),
                pl.BlockSpec((BLK, 128), lambda p, i: (0, vcol(p))),
                pl.BlockSpec((128, lp), lambda p, i: (p, 0)),
                pl.BlockSpec((lp, 128), lambda p, i: (0, 0))]
    return pl.pallas_call(
        body, name=name, grid=(PAIRS, nq), in_specs=in_specs,
        out_specs=[pl.BlockSpec((lp, 128), lambda p, i: (0, p)),
                   pl.BlockSpec((1, 2, lp), lambda p, i: (p, 0, 0))],
        out_shape=[jax.ShapeDtypeStruct((lp, D_MODEL), BF16), jax.ShapeDtypeStruct((PAIRS, 2, lp), F32)],
        compiler_params=_cp(("parallel", "arbitrary"), VMEM_BIG))(q, k, v, vt, k2)


def _attn_bwd(q, k, v, k2, do, delta, lse, *, rtabs=None, scale, qcol, kcol, vcol, name):
    lp = q.shape[0]
    nb = lp // BLK
    rope = rtabs is not None
    bias = not rope
    qw = 256 if rope else 128

    def body(*refs):
        it = iter(refs)
        q_ref, k_ref, v_ref, k2_ref = next(it), next(it), next(it), next(it)
        do_ref, dl_ref, lse_ref = next(it), next(it), next(it)
        ct_ref, st_ref = (next(it), next(it)) if rope else (None, None)
        dq_out, dk_ref, dv_ref = next(it), next(it), next(it)
        x_ref = next(it)
        drow_ref = next(it) if bias else None
        dq_ref = next(it)
        kb = pl.program_id(1)
        mas, hmask = _pair_masks(rope, pl.program_id(0))
        lane = lax.broadcasted_iota(jnp.int32, (1, LANES), 1)

        @pl.when(kb == 0)
        def _():
            dq_ref[...] = jnp.zeros_like(dq_ref)
            if bias:
                drow_ref[...] = jnp.zeros_like(drow_ref)

        def key_pass(n, w):
            kk = jnp.concatenate([k_ref[0:n, :], k2_ref[0:n, :]], axis=1)
            vh = _mask2(v_ref[0:n, :], mas)
            kcat = jnp.concatenate([x[:, 0:qw] for x in _mask2(kk, hmask)], axis=0)
            diag_mask = (lax.broadcasted_iota(jnp.int32, (n, w), 0) <= lax.broadcasted_iota(jnp.int32, (n, w), 1))

            def chunk(qc, carry, mask):
                carry = list(carry)
                q0 = qc * w if isinstance(qc, int) else pl.multiple_of(qc * w, w)
                dov = do_ref[pl.ds(q0, w), :]
                doh = _mask2(dov, mas)
                qh = _q_heads(q_ref[pl.ds(q0, w), :], rope, mas, hmask)
                pbs, dss = [], []
                for h in range(2):
                    p = jnp.exp2(_dot(kk, qh[h], 1, 1) - lse_ref[0, h:h + 1, pl.ds(q0, w)])
                    if mask is not None:
                        p = jnp.where(mask, p, 0.0)
                    ds = p * (_dot(vh[h], dov, 1, 1) - dl_ref[0, h:h + 1, pl.ds(q0, w)])
                    if bias:
                        drow_ref[0, h:h + 1, pl.ds(q0, w)] += jnp.sum(ds, axis=0, keepdims=True)
                        carry[2 + h] = carry[2 + h] + jnp.sum(ds, axis=1, keepdims=True)
                    pbs.append(p.astype(BF16))
                    dss.append(ds.astype(BF16))
                ds_lanes = jnp.concatenate(dss, axis=1)
                ds_rows = jnp.concatenate(dss, axis=0)
                qcat = jnp.concatenate([x[:, 0:qw] for x in qh], axis=0)
                carry[0] = carry[0] + _dot(ds_lanes, qcat, 1, 0)
                carry[1] = carry[1] + _dot(jnp.concatenate(pbs, axis=1), jnp.concatenate(doh, axis=0), 1, 0)
                dq_ref[pl.ds(q0, w), :] += _dot(ds_rows, kcat, 0, 0)
                return tuple(carry)

            c = [jnp.zeros((n, qw), F32), jnp.zeros((n, LANES), F32)]
            if bias:
                c += [jnp.zeros((n, 1), F32), jnp.zeros((n, 1), F32)]
            c = tuple(c)
            if w != BLK:
                for qc in range(lp // w):
                    c = chunk(qc, c, diag_mask if qc == 0 else None)
            else:
                groups = (nb - kb) // UNROLL

                def several(t, cr):
                    for u in range(UNROLL):
                        cr = chunk(kb + UNROLL * t + u, cr, (diag_mask | (t > 0)) if u == 0 else None)
                    return cr

                c = lax.fori_loop(0, groups, several, c)
                start = kb + UNROLL * groups
                pairs = (nb - start) // 2

                def two(t, cr):
                    qc = start + 2 * t
                    return chunk(qc + 1, chunk(qc, cr, diag_mask | (qc > kb)), None)

                c = lax.fori_loop(0, pairs, two, c)
                c = lax.fori_loop(start + 2 * pairs, nb, lambda qc, cr: chunk(qc, cr, diag_mask | (qc > kb)), c)

            def rows(a, dtype):
                a = a.astype(dtype)
                return a if n == BLK else jnp.concatenate([a, jnp.zeros((BLK - n, a.shape[1]), dtype)], axis=0)

            dk = c[0] * LN2
            dk_ref[...] = rows(dk[:, 0:LANES], BF16)
            dv_ref[...] = rows(c[1], BF16)
            if rope:
                x_ref[0] = rows(dk[:, LANES:2 * LANES], F32)
            if bias:
                x_ref[0] = rows(jnp.where(lane == 0, c[2], jnp.where(lane == 1, c[3], 0.0)), F32)

        @pl.when(kb == 0)
        def _():
            key_pass(N_META, lp // 2)

        @pl.when(kb > 0)
        def _():
            key_pass(BLK, BLK)

        @pl.when(kb == nb - 1)
        def _():
            def fin(c, carry):
                r0 = pl.multiple_of(c * BLK, BLK)
                dq = dq_ref[pl.ds(r0, BLK), :] * scale
                if rope:
                    back = _rope(dq[:, LANES:2 * LANES], ct_ref[pl.ds(r0, BLK), :], -st_ref[pl.ds(r0, BLK), :])
                    dq = jnp.concatenate([dq[:, 0:LANES], back], axis=1)
                dq_out[pl.ds(r0, BLK), :] = dq.astype(BF16)
                return carry

            lax.fori_loop(0, nb, fin, 0)

    in_specs = [pl.BlockSpec((lp, qw), lambda p, j: (0, qcol + p)),
                pl.BlockSpec((BLK, 128), lambda p, j: (j, kcol(p))),
                pl.BlockSpec((BLK, 128), lambda p, j: (j, vcol(p))),
                pl.BlockSpec((BLK, 128), lambda p, j: (j, 0)),
                pl.BlockSpec((lp, 128), lambda p, j: (0, p)), pl.BlockSpec((1, 2, lp), lambda p, j: (p, 0, 0)),
                pl.BlockSpec((1, 2, lp), lambda p, j: (p, 0, 0))]
    ins = [q, k, v, k2, do, delta, lse]
    if rope:
        in_specs += [pl.BlockSpec((lp, 128), lambda p, j: (0, 0))] * 2
        ins += list(rtabs)
    out_specs = [pl.BlockSpec((lp, qw), lambda p, j: (0, p)),
                 pl.BlockSpec((BLK, 128), lambda p, j: (j, p)),
                 pl.BlockSpec((BLK, 128), lambda p, j: (j, p)),
                 pl.BlockSpec((1, BLK, 128), lambda p, j: (p, j, 0))]
    out_shape = [jax.ShapeDtypeStruct((lp, PAIRS * qw), BF16), jax.ShapeDtypeStruct((lp, D_MODEL), BF16),
                 jax.ShapeDtypeStruct((lp, D_MODEL), BF16), jax.ShapeDtypeStruct((PAIRS, lp, 128), F32)]
    if bias:
        out_specs.append(pl.BlockSpec((1, 2, lp), lambda p, j: (p, 0, 0)))
        out_shape.append(jax.ShapeDtypeStruct((PAIRS, 2, lp), F32))
    return pl.pallas_call(
        body, name=name, grid=(PAIRS, nb), in_specs=in_specs, out_specs=out_specs, out_shape=out_shape,
        scratch_shapes=[pltpu.VMEM((lp, qw), F32)],
        compiler_params=_cp(("parallel", "arbitrary"), VMEM_BIG))(*ins)


def _adamw(w, g, m, v, name):
    lead = w.ndim - 2
    rows, cols = w.shape[lead:]
    big = rows * cols > 512 * 1024
    tr = 128 if big and rows % 128 == 0 else rows
    tc = 256 if big and tr == rows else cols

    def body(w_ref, g_ref, m_ref, v_ref, d_ref, nm_ref, nv_ref):
        gv = g_ref[...]
        nm = ADAM_B1 * m_ref[...] + (1.0 - ADAM_B1) * gv
        nv = ADAM_B2 * v_ref[...] + (1.0 - ADAM_B2) * (gv * gv)
        m_hat = nm / (1.0 - ADAM_B1 ** ADAM_STEP)
        v_hat = nv / (1.0 - ADAM_B2 ** ADAM_STEP)
        d_ref[...] = -ADAM_LR * (m_hat / (jnp.sqrt(v_hat) + ADAM_EPS) + ADAM_WD * w_ref[...])
        nm_ref[...] = nm
        nv_ref[...] = nv

    spec = pl.BlockSpec((1,) * lead + (tr, tc), lambda i, j: (0,) * lead + (i, j))
    return pl.pallas_call(
        body, name=name, grid=(rows // tr, cols // tc), in_specs=[spec] * 4, out_specs=[spec] * 3,
        out_shape=[jax.ShapeDtypeStruct(w.shape, F32)] * 3,
        compiler_params=_cp(("parallel", "parallel"), VMEM_BIG))(w, g, m, v)


def _add_cores(g, from_sib, name):
    n, rows, cols = g.shape
    half = rows // 2
    tr = _tile(half, (256, 240))
    nt = half // tr

    def body(lo_ref, hi_ref, s_ref, o_ref):
        mine = jnp.where(lax.axis_index("c") == 0, lo_ref[0], hi_ref[0])
        o_ref[0] = (mine.astype(F32) + s_ref[0].astype(F32)).astype(BF16)

    return pl.pallas_call(
        body, name=name, grid=(n, nt),
        in_specs=[pl.BlockSpec((1, tr, cols), lambda j, i: (j, i, 0)),
                  pl.BlockSpec((1, tr, cols), lambda j, i: (j, nt + i, 0)),
                  pl.BlockSpec((1, tr, cols), lambda j, i: (j, i, 0))],
        out_specs=pl.BlockSpec((1, tr, cols), lambda j, i: (j, i, 0)),
        out_shape=jax.ShapeDtypeStruct((n, half, cols), BF16),
        compiler_params=_cp(("parallel", "parallel"), VMEM_BIG))(g, g, from_sib)


def _add_chips(x, own, name):
    n, rows, cols = x.shape
    tr = _tile(rows, (256, 240))

    def body(x_ref, own_ref, o_ref):
        me = 2 * lax.axis_index("x") + lax.axis_index("y")
        v = [jnp.where(me == k, own_ref[...], x_ref[k]).astype(F32) for k in range(N_CHIPS)]
        o_ref[...] = ((v[0] + v[1]) + v[2]) + v[3]

    return pl.pallas_call(
        body, name=name, grid=(rows // tr,),
        in_specs=[pl.BlockSpec((n, tr, cols), lambda i: (0, i, 0)), pl.BlockSpec((tr, cols), lambda i: (i, 0))],
        out_specs=pl.BlockSpec((tr, cols), lambda i: (i, 0)),
        out_shape=jax.ShapeDtypeStruct((rows, cols), F32), compiler_params=_cp(("parallel",), VMEM_BIG))(x, own)


def _axes():
    return lax.axis_index("x"), lax.axis_index("y"), lax.axis_index("c")


def _other_chips(x, y):
    return [(1 - x, y), (x, 1 - y), (1 - x, 1 - y)]


ANY = pl.BlockSpec(memory_space=pl.ANY)


def _rcopy(src, dst, send_sems, recv_sems, k, to):
    return pltpu.make_async_remote_copy(src_ref=src, dst_ref=dst, send_sem=send_sems.at[k], recv_sem=recv_sems.at[k],
                                        device_id=to, device_id_type=MESH)


def _gather_weights(shards, meta):
    n = len(shards)

    def body(*refs):
        srcs, meta_ref = refs[:n], refs[n]
        outs, mout_ref = refs[n + 1:2 * n + 1], refs[2 * n + 1]
        send_sems, recv_sems = refs[2 * n + 2:]
        x, y, c = _axes()
        me = 2 * x + y
        sib = (x, y, 1 - c)
        chips = _other_chips(x, y)

        def half(t, chip_idx, cc):
            hr = shards[t].shape[0] // 2
            return outs[t].at[chip_idx, pl.ds(cc * hr, hr), :]

        first = []
        for j, (px, py) in enumerate(chips):
            for t in range(n):
                hr = shards[t].shape[0] // 2
                first.append(_rcopy(srcs[t].at[pl.ds(c * hr, hr), :], half(t, me, c), send_sems, recv_sems,
                                    3 * t + j, (px, py, c)))
            first.append(_rcopy(meta_ref, mout_ref.at[me], send_sems, recv_sems, 3 * n + j, (px, py, c)))
        for cp in first:
            cp.start()
        passed = []
        for j, (px, py) in enumerate(chips):
            src_chip = 2 * px + py
            for t in range(n):
                _rcopy(half(t, src_chip, c), half(t, src_chip, c), send_sems, recv_sems, 3 * t + j, sib).wait_recv()
                fwd = _rcopy(half(t, src_chip, c), half(t, src_chip, c), send_sems, recv_sems, 3 * (n + 1 + t) + j, sib)
                fwd.start()
                passed.append(fwd)
            _rcopy(mout_ref.at[src_chip], mout_ref.at[src_chip], send_sems, recv_sems, 3 * n + j, sib).wait_recv()
        for j, (px, py) in enumerate(chips):
            src_chip = 2 * px + py
            for t in range(n):
                _rcopy(half(t, src_chip, 1 - c), half(t, src_chip, 1 - c), send_sems, recv_sems,
                       3 * (n + 1 + t) + j, sib).wait_recv()
        for cp in first + passed:
            cp.wait_send()

    nsem = 3 * (2 * n + 1)
    return pl.pallas_call(
        body, name="gather_weights", in_specs=[ANY] * (n + 1), out_specs=[ANY] * (n + 1),
        out_shape=[jax.ShapeDtypeStruct((N_CHIPS,) + s.shape, s.dtype) for s in shards]
        + [jax.ShapeDtypeStruct((N_CHIPS,) + meta.shape, meta.dtype)],
        scratch_shapes=[pltpu.SemaphoreType.DMA((nsem,)), pltpu.SemaphoreType.DMA((nsem,))])(*shards, meta)


def _gather_late(shard):
    rows, cols = shard.shape
    hr = rows // 2
    src = jax.new_ref(shard, memory_space=pltpu.MemorySpace.HBM)
    out = jax.empty_ref(jax.ShapeDtypeStruct((N_CHIPS, rows, cols), shard.dtype), memory_space=pltpu.MemorySpace.HBM)

    @pl.kernel(mesh=plsc.ScalarSubcoreMesh(axis_name="seq", num_cores=1), name="gather_late",
               scratch_types=(pltpu.SemaphoreType.DMA((6,)), pltpu.SemaphoreType.DMA((6,))),
               compiler_params=pltpu.CompilerParams(collective_id=1))
    def launch(send_sems, recv_sems):
        x, y, c = _axes()
        me = 2 * x + y
        sib = (x, y, 1 - c)
        chips = _other_chips(x, y)
        barrier = pltpu.get_barrier_semaphore()
        for px, py in chips:
            pl.semaphore_signal(barrier, inc=1, device_id=(px, py, c), device_id_type=MESH)
        pl.semaphore_signal(barrier, inc=1, device_id=sib, device_id_type=MESH)
        pl.semaphore_wait(barrier, 4)

        def half(chip_idx, cc):
            return out.at[chip_idx, pl.ds(cc * hr, hr), :]

        first = [_rcopy(src.at[pl.ds(c * hr, hr), :], half(me, c), send_sems, recv_sems, j, (px, py, c))
                 for j, (px, py) in enumerate(chips)]
        for cp in first:
            cp.start()
        passed = []
        for j, (px, py) in enumerate(chips):
            land = half(2 * px + py, c)
            _rcopy(land, land, send_sems, recv_sems, j, sib).wait_recv()
            fwd = _rcopy(land, land, send_sems, recv_sems, 3 + j, sib)
            fwd.start()
            passed.append(fwd)
        for j, (px, py) in enumerate(chips):
            land = half(2 * px + py, 1 - c)
            _rcopy(land, land, send_sems, recv_sems, 3 + j, sib).wait_recv()
        for cp in first + passed:
            cp.wait_send()

    launch()
    return out[...]


def _swap_halves(gs):
    n = len(gs)
    ncopies = sum(g.shape[0] for g in gs)

    def body(*refs):
        srcs, outs = refs[:n], refs[n:2 * n]
        send_sems, recv_sems = refs[2 * n:]
        x, y, c = _axes()
        cps = []
        for t in range(n):
            hr = gs[t].shape[1] // 2
            for j in range(gs[t].shape[0]):
                cps.append(_rcopy(srcs[t].at[j, pl.ds((1 - c) * hr, hr), :], outs[t].at[j], send_sems, recv_sems,
                                  len(cps), (x, y, 1 - c)))
        for cp in cps:
            cp.start()
        for cp in cps:
            cp.wait()

    return pl.pallas_call(
        body, name="swap_halves", in_specs=[ANY] * n, out_specs=[ANY] * n,
        out_shape=[jax.ShapeDtypeStruct((g.shape[0], g.shape[1] // 2, g.shape[2]), g.dtype) for g in gs],
        scratch_shapes=[pltpu.SemaphoreType.DMA((ncopies,)), pltpu.SemaphoreType.DMA((ncopies,))])(*gs)


def _scatter_chips(parts):
    n = len(parts)
    srcs = [jax.new_ref(p, memory_space=pltpu.MemorySpace.HBM) for p in parts]
    outs = [jax.empty_ref(jax.ShapeDtypeStruct(p.shape, p.dtype), memory_space=pltpu.MemorySpace.HBM) for p in parts]

    @pl.kernel(mesh=plsc.ScalarSubcoreMesh(axis_name="seq", num_cores=1), name="scatter_chips",
               scratch_types=(pltpu.SemaphoreType.DMA((3 * n,)), pltpu.SemaphoreType.DMA((3 * n,))),
               compiler_params=pltpu.CompilerParams(collective_id=0))
    def launch(send_sems, recv_sems):
        x, y, c = _axes()
        me = 2 * x + y
        chips = _other_chips(x, y)
        barrier = pltpu.get_barrier_semaphore()
        for px, py in chips:
            pl.semaphore_signal(barrier, inc=1, device_id=(px, py, c), device_id_type=MESH)
        pl.semaphore_wait(barrier, 3)
        cps = []
        for j, (px, py) in enumerate(chips):
            for t in range(n):
                cps.append(_rcopy(srcs[t].at[2 * px + py], outs[t].at[me], send_sems, recv_sems, 3 * t + j,
                                  (px, py, c)))
        for cp in cps:
            cp.start()
        for cp in cps:
            cp.wait()

    launch()
    return [o[...] for o in outs]


def _swap_reduced(rs):
    n = len(rs)

    def body(*refs):
        srcs, outs = refs[:n], refs[n:2 * n]
        send_sems, recv_sems = refs[2 * n:]
        x, y, c = _axes()
        cps = [_rcopy(srcs[t], outs[t], send_sems, recv_sems, t, (x, y, 1 - c)) for t in range(n)]
        for cp in cps:
            cp.start()
        for cp in cps:
            cp.wait()

    return pl.pallas_call(
        body, name="swap_reduced", in_specs=[ANY] * n, out_specs=[ANY] * n,
        out_shape=[jax.ShapeDtypeStruct(r.shape, r.dtype) for r in rs],
        scratch_shapes=[pltpu.SemaphoreType.DMA((n,)), pltpu.SemaphoreType.DMA((n,))])(*rs)


SMALL_ROWS = 24 + 128


def _allreduce_small(vec):
    def body(v_ref, out_ref, slots, send_sems, recv_sems):
        x, y, c = _axes()
        me = 4 * x + 2 * y + c
        slots[me] = v_ref[...]
        cps = []
        for k in range(1, 8):
            kx, ky, kc = (k >> 2) & 1, (k >> 1) & 1, k & 1
            peer = (1 - x if kx else x, 1 - y if ky else y, 1 - c if kc else c)
            cps.append(_rcopy(v_ref, slots.at[me], send_sems, recv_sems, k - 1, peer))
        for cp in cps:
            cp.start()
        for cp in cps:
            cp.wait()
        tot = slots[0]
        for k in range(1, 8):
            tot = tot + slots[k]
        out_ref[...] = tot

    return pl.pallas_call(
        body, name="allreduce_small",
        in_specs=[pl.BlockSpec(memory_space=pltpu.VMEM)], out_specs=pl.BlockSpec(memory_space=pltpu.VMEM),
        out_shape=jax.ShapeDtypeStruct((SMALL_ROWS, 128), F32),
        scratch_shapes=[pltpu.VMEM((8, SMALL_ROWS, 128), F32), pltpu.SemaphoreType.DMA((7,)),
                        pltpu.SemaphoreType.DMA((7,))])(vec)


def _pack_p2(w_uq, w_ukv, w_br_mla, w_br_fox, w_out, dtype):
    parts = [w_uq.reshape(96, D_MODEL), w_ukv.reshape(64, D_MODEL), w_br_mla, w_br_fox, w_out]
    return jnp.concatenate([p.astype(dtype) for p in parts], axis=0)


def _unpack_p2(pk):
    return pk[0:96].reshape(256, 384), pk[96:160].reshape(128, 512), pk[160:416], pk[416:672], pk[672:928]


def _uq_arrange(w):
    w3 = w.reshape(256, HEADS, 96)
    nope = w3[:, :, :64].reshape(256, PAIRS, 128)
    pe = w3[:, :, 64:].reshape(256, PAIRS, 64)
    return jnp.concatenate([nope, pe, jnp.zeros((256, PAIRS, 64), w.dtype)], axis=2).reshape(256, PAIRS * 256)


def _uq_restore(g):
    g3 = g.reshape(256, PAIRS, 256)
    nope = g3[:, :, :128].reshape(256, HEADS, 64)
    pe = g3[:, :, 128:192].reshape(256, HEADS, 32)
    return jnp.concatenate([nope, pe], axis=2).reshape(256, HEADS * 96)


def _ukv_arrange(w):
    w3 = w.reshape(128, HEADS, 128)
    return jnp.concatenate([w3[:, :, :64].reshape(128, 1024), w3[:, :, 64:].reshape(128, 1024)], axis=1)


def _ukv_restore(g):
    kn = g[:, :1024].reshape(128, HEADS, 64)
    vv = g[:, 1024:].reshape(128, HEADS, 64)
    return jnp.concatenate([kn, vv], axis=2).reshape(128, HEADS * 128)


def _rope_tables(lp):
    r = np.arange(lp)
    pos = np.where(r < N_META, r, np.where(r >= PAD, r - PAD + N_META, 0)).astype(np.float32)
    half = MLA_ROPE // 2
    inv_freq = np.float32(ROPE_THETA) ** (-np.arange(half, dtype=np.float32) / np.float32(half))
    ang = (pos[:, None] * inv_freq[None, :]).astype(np.float32)
    cos, sin = np.cos(ang).astype(np.float32), np.sin(ang).astype(np.float32)
    one, zero = np.ones((lp, 64), np.float32), np.zeros((lp, 64), np.float32)
    return (jnp.asarray(np.concatenate([cos, cos, cos, cos, one], axis=1)),
            jnp.asarray(np.concatenate([-sin, sin, -sin, sin, zero], axis=1)))


def _pad_lanes(v, n=128):
    return jnp.pad(v, ((0, 0), (0, n - v.shape[1])))


def _in_cols(slabs, a, b):
    out = []
    for j in range(N_CHIPS):
        lo, hi = max(a, W_IN_SHARD * j), min(b, W_IN_SHARD * (j + 1))
        if lo < hi:
            out.append(slabs[j][:, lo - W_IN_SHARD * j:hi - W_IN_SHARD * j])
    return out


def _local_step(x2, tgt2, meta_f, w_small, w_attn, w_gate, w_uq_f, w_ukv_f, w_bm, w_bf, w_o, pre_norm_g,
                post_norm_g, mla_q_norm_g, mla_kv_norm_g, fox_forget_b, start_exchange=None):
    s_rows = x2.shape[0]
    lp = PAD + s_rows
    w_uq_a = _uq_arrange(w_uq_f)
    w_ukv_a = _ukv_arrange(w_ukv_f)

    ctab, stab = _rope_tables(lp)
    ii = jnp.arange(BLK)
    tri_lo = (ii[:, None] >= ii[None, :]).astype(BF16)
    tri_up = (ii[:, None] <= ii[None, :]).astype(BF16)
    fb128 = _pad_lanes(fox_forget_b)

    u = _rms_pre(x2, meta_f, pre_norm_g)
    small = _mm(u, w_small, mode="nn", out_dtype=F32, name="proj_small")
    attn = _mm(u, w_attn, mode="nn", out_dtype=BF16, name="proj_attn",
               col_scale=(HEADS * HEAD_DIM, FOX_SCALE * LOG2E))
    gate = _mm(u, w_gate, mode="nn", out_dtype=BF16, name="proj_gate")
    qn, kvn, kr, kb = _small_prep(small, mla_q_norm_g, mla_kv_norm_g, fb128, ctab, stab, tri_lo)
    qcat = _mm(qn, w_uq_a, mode="nn", out_dtype=BF16, name="mla_q", row_ins=(ctab, stab),
               epilogue=lambda tile, c, s: _rope_pairs(tile, c, s) * (MLA_SCALE * LOG2E))
    kv = _mm(kvn, w_ukv_a, mode="nn", out_dtype=BF16, name="mla_kv")

    mla_cols = dict(qcol=0, kcol=lambda p: p, vcol=lambda p: PAIRS + p)
    fox_cols = dict(qcol=0, kcol=lambda p: PAIRS + p, vcol=lambda p: 2 * PAIRS + p)
    o_mla, lse_mla = _attn_fwd(qcat, kv, kv, kv[:, D_MODEL:].T, kr, rope=True, name="mla_fwd", **mla_cols)
    o_fox, lse_fox = _attn_fwd(attn, attn, attn, attn[:, 2 * D_MODEL:].T, kb, rope=False, name="fox_fwd", **fox_cols)

    a_mla, a_fox = _gate_fwd(o_mla, o_fox, gate)
    y_mla = _mm(a_mla, w_bm, mode="nn", out_dtype=BF16, name="br_mla")
    y_fox = _mm(a_fox, w_bf, mode="nn", out_dtype=BF16, name="br_fox")
    mg = _merge_fwd(gate, y_mla, y_fox)
    mixed = _mm(mg, w_o, mode="nn", out_dtype=F32, name="out_proj")
    dmixed, dy, loss_p, dg_post = _tail(x2, mixed, tgt2, post_norm_g)

    d_w_out = _mm(mg, dmixed, mode="tn", out_dtype=F32, name="d_w_out")
    dm = _mm(dmixed, w_o, mode="nt", out_dtype=BF16, name="d_merge")
    dy_mla, dy_fox, dgate_ab = _merge_bwd(dm, gate, y_mla, y_fox)
    d_w_bm = _mm(a_mla, dy_mla, mode="tn", out_dtype=F32, name="d_w_br_mla")
    d_w_bf = _mm(a_fox, dy_fox, mode="tn", out_dtype=F32, name="d_w_br_fox")
    da_mla = _mm(dy_mla, w_bm, mode="nt", out_dtype=BF16, name="d_a_mla")
    da_fox = _mm(dy_fox, w_bf, mode="nt", out_dtype=BF16, name="d_a_fox")
    do_mla, do_fox, dgate_z, dl_mla, dl_fox = _gate_bwd(da_mla, da_fox, o_mla, o_fox, gate)
    dl_mla, dl_fox = (d[:, :HEADS].T.reshape(PAIRS, 2, lp) for d in (dl_mla, dl_fox))

    dq_a, dkn, dvm, dkr = _attn_bwd(qcat, kv, kv, kr, do_mla, dl_mla, lse_mla, rtabs=(ctab, stab),
                                    scale=MLA_SCALE, name="mla_bwd", **mla_cols)
    dfq, dfk, dfv, dcol, drow = _attn_bwd(attn, attn, attn, kb, do_fox, dl_fox, lse_fox, scale=FOX_SCALE,
                                          name="fox_bwd", **fox_cols)

    d_w_uq_a = _mm(qn, dq_a, mode="tn", out_dtype=F32, name="d_w_uq")
    dqn = _mm(dq_a, w_uq_a, mode="nt", out_dtype=F32, name="d_qn")
    d_w_ukv_a = jnp.concatenate([_mm(kvn, dkn, mode="tn", out_dtype=F32, name="d_w_uk"),
                                 _mm(kvn, dvm, mode="tn", out_dtype=F32, name="d_w_uv")], axis=1)
    dkvn = _mm(dkn, w_ukv_a[:, :1024], mode="nt", out_dtype=F32, name="d_kvn_k")
    dkvn = _mm(dvm, w_ukv_a[:, 1024:], mode="nt", out_dtype=F32, name="d_kvn_v", acc=dkvn)
    dsmall, dg_q, dg_kv, dfb = _small_bwd(small, dqn, dkvn, dkr, dcol, drow, mla_q_norm_g, mla_kv_norm_g,
                                          fb128, ctab, stab, tri_up)

    dw_small = _mm(u, dsmall, mode="tn", out_dtype=BF16, name="d_w_small")
    dw_fq = _mm(u, dfq, mode="tn", out_dtype=BF16, name="d_w_fq")
    dw_fk = _mm(u, dfk, mode="tn", out_dtype=BF16, name="d_w_fk")
    dw_fv = _mm(u, dfv, mode="tn", out_dtype=BF16, name="d_w_fv")
    dw_z = _mm(u, dgate_z, mode="tn", out_dtype=BF16, name="d_w_z")
    dw_g = _mm(u, dgate_ab, mode="tn", out_dtype=BF16, name="d_w_g")
    d_w_in = (dw_small, dw_z, dw_fq, dw_fk, dw_fv, dw_g)
    d_w_uq = _uq_restore(d_w_uq_a)
    d_w_ukv = _ukv_restore(d_w_ukv_a)
    token = start_exchange(d_w_in, d_w_uq, d_w_ukv, d_w_bm, d_w_bf, d_w_out) if start_exchange else None
    du = _mm_sum_nt([(dsmall, w_small), (dfq, w_attn[:, 0:1024]), (dfk, w_attn[:, 1024:2048]),
                     (dfv, w_attn[:, 2048:3072]), (dgate_z, w_gate[:, 0:2048]), (dgate_ab, w_gate[:, 2048:4096])],
                    name="d_u", after=token)
    dx, dmeta, dg_pre = _pre_bwd(du, x2, meta_f, dy, pre_norm_g)
    return (loss_p, dx, dmeta, d_w_in, d_w_uq, d_w_ukv, d_w_bm, d_w_bf, d_w_out, dg_pre, dg_post, dg_q, dg_kv, dfb)


def _w_in_slabs(pieces):
    dw_small, dw_z, dw_fq, dw_fk, dw_fv, dw_g = pieces
    runs = [(dw_small[:, 0:416], C_CQ), (dw_z[:, 0:1024], C_ZMLA), (dw_fq, C_FQ), (dw_fk, C_FK), (dw_fv, C_FV),
            (dw_small[:, 512:528], C_FL), (dw_z[:, 1024:2048], C_ZFOX), (dw_g, C_GA)]
    slabs = []
    for j in range(N_CHIPS):
        lo, hi = W_IN_SHARD * j, W_IN_SHARD * (j + 1)
        cols = [a[:, max(lo, c0) - c0:min(hi, c0 + a.shape[1]) - c0] for a, c0 in runs
                if max(lo, c0) < min(hi, c0 + a.shape[1])]
        slabs.append(jnp.concatenate(cols, axis=1))
    return jnp.stack(slabs, axis=0)


def kernel(x, meta_tokens, pre_norm_g, w_in, fox_forget_b, mla_q_norm_g, mla_kv_norm_g, w_uq, w_ukv, w_br_mla, w_br_fox, w_out, post_norm_g, loss_target, m_meta_tokens, m_pre_norm_g, m_w_in, m_fox_forget_b, m_mla_q_norm_g, m_mla_kv_norm_g, m_w_uq, m_w_ukv, m_w_br_mla, m_w_br_fox, m_w_out, m_post_norm_g, v_meta_tokens, v_pre_norm_g, v_w_in, v_fox_forget_b, v_mla_q_norm_g, v_mla_kv_norm_g, v_w_uq, v_w_ukv, v_w_br_mla, v_w_br_fox, v_w_out, v_post_norm_g):
    me = 2 * lax.axis_index("x") + lax.axis_index("y")
    core = lax.axis_index("c")
    w_in_b = w_in.astype(BF16).reshape(D_MODEL, W_IN_SHARD)
    p2 = _pack_p2(w_uq[0], w_ukv[0], w_br_mla[0], w_br_fox[0], w_out[0], BF16)
    w_in_g, meta_g = _gather_weights([w_in_b], meta_tokens)
    p2_g = _gather_late(lax.optimization_barrier((p2, w_in_g))[0])
    slabs = [jnp.where(me == j, w_in_b, w_in_g[j]) for j in range(N_CHIPS)]
    chip = lax.broadcasted_iota(jnp.int32, (N_CHIPS, 1, 1), 0)
    p2_all = jnp.where(chip == me, p2[None], p2_g)
    w_uq_f = p2_all[:, 0:96].reshape(N_CHIPS, 256, 384).transpose(1, 0, 2).reshape(256, 1536)
    w_ukv_f = p2_all[:, 96:160].reshape(N_CHIPS, 128, 512).transpose(1, 0, 2).reshape(128, 2048)
    w_bm, w_bf, w_o = (p2_all[:, lo:lo + 256].reshape(D_MODEL, D_MODEL) for lo in (160, 416, 672))
    meta_f = jnp.where(chip == me, meta_tokens[None], meta_g).transpose(1, 0, 2).reshape(N_META, D_MODEL)
    kpe = _in_cols(slabs, C_KPE, C_ZMLA)
    w_small = jnp.concatenate(_in_cols(slabs, C_CQ, C_KPE) + kpe + kpe + [jnp.zeros((D_MODEL, 64), BF16)]
                              + _in_cols(slabs, C_FL, C_ZFOX) + [jnp.zeros((D_MODEL, 112), BF16)], axis=1)
    w_attn = jnp.concatenate(_in_cols(slabs, C_FQ, C_FL), axis=1)
    w_gate = jnp.concatenate(_in_cols(slabs, C_ZMLA, C_FQ) + _in_cols(slabs, C_ZFOX, C_END), axis=1)

    exchange = {}

    def start_exchange(d_w_in, d_w_uq, d_w_ukv, d_w_bm, d_w_bf, d_w_out):
        g2 = jnp.concatenate(
            [d_w_uq.reshape(256, N_CHIPS, 384).transpose(1, 0, 2).reshape(N_CHIPS, 96, D_MODEL),
             d_w_ukv.reshape(128, N_CHIPS, 512).transpose(1, 0, 2).reshape(N_CHIPS, 64, D_MODEL)]
            + [g.reshape(N_CHIPS, 256, D_MODEL) for g in (d_w_bm, d_w_bf, d_w_out)], axis=1)
        pieces = [p[None] for p in d_w_in]
        from_sib = _swap_halves(pieces + [g2])
        halves = [_add_cores(p, s, "add_cores_" + nm)[0]
                  for p, s, nm in zip(pieces, from_sib, ("small", "z", "fq", "fk", "fv", "g"))]
        parts = [_w_in_slabs(halves), _add_cores(g2, from_sib[-1], "add_cores_rest")]
        exchange.update(parts=parts, landed=_scatter_chips(parts))
        return parts[0][0, 0:16, 0:LANES]

    (loss_p, dx, dmeta, _, _, _, _, _, _, dg_pre, dg_post, dg_q, dg_kv,
     dfb) = _local_step(x[0], loss_target[0], meta_f, w_small, w_attn, w_gate, w_uq_f, w_ukv_f, w_bm, w_bf, w_o,
                        pre_norm_g, post_norm_g, mla_q_norm_g, mla_kv_norm_g, fox_forget_b, start_exchange)

    mine = [_add_chips(l, lax.dynamic_index_in_dim(p, me, 0, keepdims=False), nm)
            for l, p, nm in zip(exchange["landed"], exchange["parts"], ("add_chips_w_in", "add_chips_rest"))]
    theirs = _swap_reduced(mine)
    g_w_in, g_p2 = [jnp.concatenate([jnp.where(core == 0, a, b), jnp.where(core == 0, b, a)], axis=0)
                    for a, b in zip(mine, theirs)]
    g_w_uq, g_w_ukv, g_w_bm, g_w_bf, g_w_out = _unpack_p2(g_p2)
    g_w_in = g_w_in[None]

    vec = jnp.concatenate([dg_pre.reshape(8, 128), dg_post.reshape(8, 128), dg_q.reshape(2, 128), dg_kv,
                           dfb, _pad_lanes(loss_p), jnp.zeros((3, 128), F32), dmeta.reshape(128, 128)], axis=0)
    tot = _allreduce_small(vec)
    loss = tot[20, 0]
    g_meta = lax.dynamic_slice_in_dim(tot[24:].reshape(N_META, D_MODEL), 256 * me, 256, axis=1)

    def small_pack(pre, post, gq_, gkv_, fb_):
        return jnp.concatenate([pre.reshape(8, 128), post.reshape(8, 128), gq_.reshape(2, 128), gkv_,
                                _pad_lanes(fb_), jnp.zeros((4, 128), F32)], axis=0)

    def small_unpack(t):
        return (t[0:8].reshape(1, 1024), t[8:16].reshape(1, 1024), t[16:18].reshape(1, 256), t[18:19],
                t[19:20, 0:HEADS])

    g_small = jnp.concatenate([tot[0:20], jnp.zeros((4, 128), F32)], axis=0)
    sm = _adamw(small_pack(pre_norm_g, post_norm_g, mla_q_norm_g, mla_kv_norm_g, fox_forget_b), g_small,
                small_pack(m_pre_norm_g, m_post_norm_g, m_mla_q_norm_g, m_mla_kv_norm_g, m_fox_forget_b),
                small_pack(v_pre_norm_g, v_post_norm_g, v_mla_q_norm_g, v_mla_kv_norm_g, v_fox_forget_b),
                "adamw_small")
    g_pre, g_post, g_q, g_kv, g_fb = small_unpack(g_small)
    (d_pre, d_post, d_q, d_kv, d_fb), (nm_pre, nm_post, nm_q, nm_kv, nm_fb), (nv_pre, nv_post, nv_q, nv_kv, nv_fb) = (
        small_unpack(t) for t in sm)

    d_meta, nm_meta, nv_meta = _adamw(meta_tokens, g_meta, m_meta_tokens, v_meta_tokens, "adamw_meta")
    d_win, nm_win, nv_win = (t.T[None] for t in _adamw(w_in[0].T, g_w_in[0].T, m_w_in[0].T, v_w_in[0].T,
                                                       "adamw_w_in"))
    d_wuq, nm_wuq, nv_wuq = _adamw(w_uq[0], g_w_uq, m_w_uq[0], v_w_uq[0], "adamw_w_uq")
    d_wukv, nm_wukv, nv_wukv = _adamw(w_ukv[0], g_w_ukv, m_w_ukv[0], v_w_ukv[0], "adamw_w_ukv")
    d_wbm, nm_wbm, nv_wbm = _adamw(w_br_mla[0], g_w_bm, m_w_br_mla[0], v_w_br_mla[0], "adamw_w_br_mla")
    d_wbf, nm_wbf, nv_wbf = _adamw(w_br_fox[0], g_w_bf, m_w_br_fox[0], v_w_br_fox[0], "adamw_w_br_fox")
    d_wo, nm_wo, nv_wo = _adamw(w_out[0], g_w_out, m_w_out[0], v_w_out[0], "adamw_w_out")

    def group(meta_, pre, win, fb_, q_, kv_, wuq, wukv, wbm, wbf, wo, post):
        return (meta_, pre, win, fb_, q_, kv_, wuq[None], wukv[None], wbm[None], wbf[None], wo[None], post)

    grads = group(g_meta, g_pre, g_w_in, g_fb, g_q, g_kv, g_w_uq, g_w_ukv, g_w_bm, g_w_bf, g_w_out, g_post)
    deltas = group(d_meta, d_pre, d_win, d_fb, d_q, d_kv, d_wuq, d_wukv, d_wbm, d_wbf, d_wo, d_post)
    new_m = group(nm_meta, nm_pre, nm_win, nm_fb, nm_q, nm_kv, nm_wuq, nm_wukv, nm_wbm, nm_wbf, nm_wo, nm_post)
    new_v = group(nv_meta, nv_pre, nv_win, nv_fb, nv_q, nv_kv, nv_wuq, nv_wukv, nv_wbm, nv_wbf, nv_wo, nv_post)
    return (loss, dx[None], *grads, *deltas, *new_m, *new_v)
```

```python
import math

import jax
import jax.numpy as jnp
import numpy as np
from jax import lax
from jax.experimental import pallas as pl
from jax.experimental.pallas import tpu as pltpu
from jax.experimental.pallas import tpu_sc as plsc

F32 = jnp.float32
BF16 = jnp.bfloat16

D_MODEL = 1024
N_META = 16
RMS_EPS = 1e-6
HEADS = 16
PAIRS = HEADS // 2
HEAD_DIM = 64
LANES = 128
MLA_ROPE = 32
AHEAD = 6
BIAS_PARTS = 3
MLA_SCALE = 1.0 / math.sqrt(64 + 32)
FOX_SCALE = 1.0 / math.sqrt(64)
LOG2E = math.log2(math.e)
LN2 = math.log(2.0)
ROPE_THETA = 10000.0

PAD = 256
BLK = 256
QB = 512
UNROLL = 4
NEG = -1e30

C_CQ, C_CKV, C_KPE, C_ZMLA, C_FQ, C_FK, C_FV, C_FL, C_ZFOX, C_GA, C_GB, C_END = (
    0, 256, 384, 416, 1440, 2464, 3488, 4512, 4528, 5552, 6576, 7600)
SMALL_W = 640
W_IN_SHARD = 1900

P2_ROWS = 928
N_CHIPS = 4

ADAM_LR = 0.001
ADAM_B1 = 0.9
ADAM_B2 = 0.999
ADAM_EPS = 1e-08
ADAM_WD = 0.01
ADAM_STEP = 10

VMEM_BIG = 56 * 1024 * 1024
MM_VMEM_BUDGET = 44 * 1024 * 1024
MESH = pl.DeviceIdType.MESH


def _cp(dims, vmem=None):
    return pltpu.CompilerParams(dimension_semantics=dims, vmem_limit_bytes=vmem)


def _dot(a, b, ca, cb):
    return lax.dot_general(a, b, (((ca,), (cb,)), ((), ())), preferred_element_type=F32)


def _sigmoid(x):
    return 1.0 / (1.0 + jnp.exp(-x))


def _tile(n, cands):
    for c in cands:
        if n % c == 0:
            return c
    return n


def _mm(a, b, *, mode, out_dtype, name, acc=None, epilogue=None, row_ins=(), after=None, col_scale=None):
    if mode == "nn":
        (M, K), N = a.shape, b.shape[1]
    elif mode == "nt":
        (M, K), N = a.shape, b.shape[0]
    else:
        (K, M), N = a.shape, b.shape[1]
    tm = _tile(M, (1088, 1024)) if M > 1024 else M
    tn = _tile(N, (1024,)) if N > 1024 else N
    nk = 1
    while True:
        tk = K // nk
        need = 2 * tk * (tm * a.dtype.itemsize + tn * b.dtype.itemsize) + tm * tn * (
            2 * jnp.dtype(out_dtype).itemsize + (8 if acc is not None else 0) + (4 if nk > 1 else 0))
        if need <= MM_VMEM_BUDGET or (tk // 2) % (16 if mode == "tn" else LANES) or tk <= 512:
            break
        nk *= 2
    while (M // tm) * (N // tn) * nk < 4 and tn % 512 == 0:
        tn //= 2
    assert col_scale is None or (nk == 1 and col_scale[0] % tn == 0)
    ca, cb = {"nn": (1, 0), "nt": (1, 1), "tn": (0, 0)}[mode]
    a_spec = (pl.BlockSpec((tk, tm), lambda j, i, k: (k, i)) if mode == "tn"
              else pl.BlockSpec((tm, tk), lambda j, i, k: (i, k)))
    b_spec = (pl.BlockSpec((tn, tk), lambda j, i, k: (j, k)) if mode == "nt"
              else pl.BlockSpec((tk, tn), lambda j, i, k: (k, j)))
    o_spec = pl.BlockSpec((tm, tn), lambda j, i, k: (i, j))
    has_acc = acc is not None

    nrow = len(row_ins)

    def body(*refs):
        a_ref, b_ref = refs[0], refs[1]
        acc_ref = refs[2] if has_acc else None
        rows = refs[2 + has_acc:2 + has_acc + nrow]
        o_ref = refs[2 + has_acc + nrow + (after is not None)]

        def store(tile):
            if epilogue is not None:
                tile = epilogue(tile, *[r[...] for r in rows])
            if col_scale is not None:
                tile = tile * jnp.where(pl.program_id(0) * tn < col_scale[0], col_scale[1], 1.0)
            o_ref[...] = tile.astype(out_dtype)

        part = _dot(a_ref[...].astype(BF16), b_ref[...].astype(BF16), ca, cb)
        if nk == 1:
            store(part + acc_ref[...] if has_acc else part)
        else:
            sc = refs[-1]
            k = pl.program_id(2)

            @pl.when(k == 0)
            def _():
                sc[...] = part + acc_ref[...] if has_acc else part

            @pl.when(k > 0)
            def _():
                sc[...] += part

            @pl.when(k == nk - 1)
            def _():
                store(sc[...])

    ins = [a, b] + ([acc] if has_acc else []) + list(row_ins)
    in_specs = ([a_spec, b_spec] + ([o_spec] if has_acc else [])
                + [pl.BlockSpec((tm, r.shape[1]), lambda j, i, k: (i, 0)) for r in row_ins])
    if after is not None:
        ins.append(after)
        in_specs.append(pl.BlockSpec(after.shape, lambda j, i, k: (0,) * after.ndim))
    return pl.pallas_call(
        body, name=name, grid=(N // tn, M // tm, nk), in_specs=in_specs, out_specs=o_spec,
        out_shape=jax.ShapeDtypeStruct((M, N), out_dtype),
        scratch_shapes=[pltpu.VMEM((tm, tn), F32)] if nk > 1 else [],
        compiler_params=_cp(("parallel", "parallel", "arbitrary"), VMEM_BIG))(*ins)


def _mm_sum_nt(pairs, *, name, after=None):
    n = len(pairs)
    M, N = pairs[0][0].shape[0], pairs[0][1].shape[0]
    tm = _tile(M, (272,))

    def body(*refs):
        o_ref = refs[2 * n + (after is not None)]
        tot = _dot(refs[0][...].astype(BF16), refs[n][...].astype(BF16), 1, 1)
        for i in range(1, n):
            tot = tot + _dot(refs[i][...].astype(BF16), refs[n + i][...].astype(BF16), 1, 1)
        o_ref[...] = tot

    ins = [a for a, _ in pairs] + [b for _, b in pairs]
    in_specs = ([pl.BlockSpec((tm, a.shape[1]), lambda i: (i, 0)) for a, _ in pairs]
                + [pl.BlockSpec(b.shape, lambda i: (0, 0)) for _, b in pairs])
    if after is not None:
        ins.append(after)
        in_specs.append(pl.BlockSpec(after.shape, lambda i: (0,) * after.ndim))
    return pl.pallas_call(
        body, name=name, grid=(M // tm,), in_specs=in_specs, out_specs=pl.BlockSpec((tm, N), lambda i: (i, 0)),
        out_shape=jax.ShapeDtypeStruct((M, N), F32), compiler_params=_cp(("parallel",), VMEM_BIG))(*ins)


def _row(w):
    return pl.BlockSpec((BLK, w), lambda i: (i, 0))


def _rowc(w, c):
    return pl.BlockSpec((BLK, w), lambda i: (i, c))


def _full(shape):
    return pl.BlockSpec(shape, lambda i: tuple(0 for _ in shape))


def _rope(x, c, s):
    lane = lax.broadcasted_iota(jnp.int32, x.shape, 1)
    is_x1 = ((lane >> 4) & 1) == 0
    partner = jnp.where(is_x1, pltpu.roll(x, LANES - 16, 1), pltpu.roll(x, 16, 1))
    return x * c + partner * s


def _row_valid(i):
    rows = i * BLK + lax.broadcasted_iota(jnp.int32, (BLK, 1), 0)
    return (rows < N_META) | (rows >= PAD)


def _shift_rows(w):
    return pl.BlockSpec((BLK, w), lambda i: (jnp.maximum(i - 1, 0), 0))


def _h_block(i, x_ref, meta_ref):
    head = jnp.concatenate([meta_ref[...], jnp.zeros((BLK - N_META, D_MODEL), F32)], axis=0)
    return jnp.where(i == 0, head, x_ref[...])


def _rms_pre(x2, meta, g):
    lp = PAD + x2.shape[0]

    def body(x_ref, meta_ref, g_ref, u_ref):
        hv = _h_block(pl.program_id(0), x_ref, meta_ref)
        r = lax.rsqrt(jnp.mean(hv * hv, axis=-1, keepdims=True) + RMS_EPS)
        u_ref[...] = (hv * r * g_ref[...]).astype(BF16)

    return pl.pallas_call(
        body, name="rms_pre", grid=(lp // BLK,),
        in_specs=[_shift_rows(D_MODEL), _full((N_META, D_MODEL)), _full((1, D_MODEL))], out_specs=_row(D_MODEL),
        out_shape=jax.ShapeDtypeStruct((lp, D_MODEL), BF16),
        compiler_params=_cp(("parallel",)))(x2, meta, g)


def _split3(x):
    hi = x.astype(BF16)
    r1 = x - hi.astype(F32)
    mid = r1.astype(BF16)
    lo = (r1 - mid.astype(F32)).astype(BF16)
    return hi, mid, lo


def _small_prep(small, gq, gkv, fb, ctab, stab, tri):
    lp = small.shape[0]

    def body(sm_ref, gq_ref, gkv_ref, fb_ref, c_ref, s_ref, tri_ref, qn_ref, kvn_ref, kr_ref, kb_ref, carry):
        i = pl.program_id(0)

        @pl.when(i == 0)
        def _():
            carry[...] = jnp.zeros_like(carry)

        cq = sm_ref[:, 0:256]
        r = lax.rsqrt(jnp.mean(cq * cq, axis=-1, keepdims=True) + RMS_EPS)
        qn_ref[...] = (cq * r * gq_ref[...]).astype(BF16)
        ckv = sm_ref[:, 256:384]
        r = lax.rsqrt(jnp.mean(ckv * ckv, axis=-1, keepdims=True) + RMS_EPS)
        kvn_ref[...] = (ckv * r * gkv_ref[...]).astype(BF16)
        kr_ref[...] = _rope(sm_ref[:, 384:512], c_ref[...], s_ref[...]).astype(BF16)
        fl = sm_ref[:, 512:640] + fb_ref[...]
        lf = jnp.minimum(fl, 0.0) - jnp.log(1.0 + jnp.exp(-jnp.abs(fl)))
        lf = jnp.where(_row_valid(i), lf, 0.0)
        hi, mid, lo = _split3(lf)
        t = tri_ref[...]
        cum = (_dot(t, hi, 1, 0) + _dot(t, mid, 1, 0)) + _dot(t, lo, 1, 0) + carry[...]
        carry[...] = cum[BLK - 1:BLK, :]
        src = lax.broadcasted_iota(jnp.int32, (LANES, LANES), 0)
        dst = lax.broadcasted_iota(jnp.int32, (LANES, LANES), 1)
        kb = jnp.zeros((BLK, LANES), F32)
        for j, part in enumerate(_split3(-cum * LOG2E)):
            spread = ((dst == BIAS_PARTS * src + j) & (src < HEADS)).astype(BF16)
            kb = kb + _dot(part, spread, 1, 0)
        kb_ref[...] = kb.astype(BF16)

    return pl.pallas_call(
        body, name="small_prep", grid=(lp // BLK,),
        in_specs=[_row(SMALL_W), _full((1, 256)), _full((1, 128)), _full((1, 128)), _row(128), _row(128),
                  _full((BLK, BLK))],
        out_specs=[_row(256), _row(128), _row(128), _row(128)],
        out_shape=[jax.ShapeDtypeStruct((lp, 256), BF16), jax.ShapeDtypeStruct((lp, 128), BF16),
                   jax.ShapeDtypeStruct((lp, 128), BF16), jax.ShapeDtypeStruct((lp, 128), BF16)],
        scratch_shapes=[pltpu.VMEM((1, 128), F32)],
        compiler_params=_cp(("arbitrary",)))(small, gq, gkv, fb, ctab, stab, tri)


def _rope_pairs(tile, c, s):
    out = []
    for lo in range(0, tile.shape[1], 256):
        out += [tile[:, lo:lo + 128], _rope(tile[:, lo + 128:lo + 256], c, s)]
    return jnp.concatenate(out, axis=1)


def _gate_fwd(o_mla, o_fox, gate):
    lp = o_mla.shape[0]

    def body(om_ref, of_ref, zm_ref, zf_ref, am_ref, af_ref):
        zm = zm_ref[...].astype(F32)
        am_ref[...] = (om_ref[...] * (zm * _sigmoid(zm))).astype(BF16)
        zf = zf_ref[...].astype(F32)
        af_ref[...] = (of_ref[...] * (zf * _sigmoid(zf))).astype(BF16)

    return pl.pallas_call(
        body, name="gate_fwd", grid=(lp // BLK,),
        in_specs=[_row(D_MODEL), _row(D_MODEL), _rowc(D_MODEL, 0), _rowc(D_MODEL, 1)],
        out_specs=[_row(D_MODEL), _row(D_MODEL)],
        out_shape=[jax.ShapeDtypeStruct((lp, D_MODEL), BF16)] * 2,
        compiler_params=_cp(("parallel",)))(o_mla, o_fox, gate, gate)


def _merge_fwd(gate, y_mla, y_fox):
    lp = y_mla.shape[0]

    def body(ga_ref, gb_ref, ym_ref, yf_ref, m_ref):
        sa = _sigmoid(ga_ref[...].astype(F32))
        sb = _sigmoid(gb_ref[...].astype(F32))
        m_ref[...] = (sa * ym_ref[...] + sb * yf_ref[...]).astype(BF16)

    return pl.pallas_call(
        body, name="merge_fwd", grid=(lp // BLK,),
        in_specs=[_rowc(D_MODEL, 2), _rowc(D_MODEL, 3), _row(D_MODEL), _row(D_MODEL)],
        out_specs=_row(D_MODEL), out_shape=jax.ShapeDtypeStruct((lp, D_MODEL), BF16),
        compiler_params=_cp(("parallel",)))(gate, gate, y_mla, y_fox)


def _tail(x2, mixed, tgt, gpost):
    lp = mixed.shape[0]
    shift = _shift_rows(D_MODEL)

    def body(h_ref, mx_ref, t_ref, g_ref, dmx_ref, dy_ref, loss_ref, dg_ref):
        i = pl.program_id(0)

        @pl.when(i == 0)
        def _():
            loss_ref[...] = jnp.zeros_like(loss_ref)
            dg_ref[...] = jnp.zeros_like(dg_ref)
            dmx_ref[...] = jnp.zeros_like(dmx_ref)
            dy_ref[...] = jnp.zeros_like(dy_ref)

        @pl.when(i > 0)
        def _():
            mx = mx_ref[...]
            g = g_ref[...]
            r = lax.rsqrt(jnp.mean(mx * mx, axis=-1, keepdims=True) + RMS_EPS)
            nrm = mx * r
            e = (h_ref[...] + nrm * g) - t_ref[...]
            loss_ref[...] += jnp.sum(0.5 * jnp.sum(e * e, axis=-1, keepdims=True) * (1.0 / D_MODEL),
                                     axis=0, keepdims=True)
            dy = e * (1.0 / D_MODEL)
            dy_ref[...] = dy
            dg_ref[...] += jnp.sum(dy * nrm, axis=0, keepdims=True)
            w = dy * g
            dot = jnp.mean(w * mx, axis=-1, keepdims=True)
            dmx_ref[...] = (r * w - mx * (r * r * r * dot)).astype(BF16)

    return pl.pallas_call(
        body, name="tail", grid=(lp // BLK,),
        in_specs=[shift, _row(D_MODEL), shift, _full((1, D_MODEL))],
        out_specs=[_row(D_MODEL), _row(D_MODEL), _full((1, 1)), _full((1, D_MODEL))],
        out_shape=[jax.ShapeDtypeStruct((lp, D_MODEL), BF16), jax.ShapeDtypeStruct((lp, D_MODEL), F32),
                   jax.ShapeDtypeStruct((1, 1), F32), jax.ShapeDtypeStruct((1, D_MODEL), F32)],
        compiler_params=_cp(("arbitrary",)))(x2, mixed, tgt, gpost)


def _merge_bwd(dm, gate, y_mla, y_fox):
    lp = dm.shape[0]

    def body(dm_ref, ga_ref, gb_ref, ym_ref, yf_ref, dym_ref, dyf_ref, dg_ref):
        dm_v = dm_ref[...].astype(F32)
        sa = _sigmoid(ga_ref[...].astype(F32))
        sb = _sigmoid(gb_ref[...].astype(F32))
        dym_ref[...] = (dm_v * sa).astype(BF16)
        dyf_ref[...] = (dm_v * sb).astype(BF16)
        dg_ref[:, 0:D_MODEL] = (dm_v * ym_ref[...] * (sa * (1.0 - sa))).astype(BF16)
        dg_ref[:, D_MODEL:2 * D_MODEL] = (dm_v * yf_ref[...] * (sb * (1.0 - sb))).astype(BF16)

    return pl.pallas_call(
        body, name="merge_bwd", grid=(lp // BLK,),
        in_specs=[_row(D_MODEL), _rowc(D_MODEL, 2), _rowc(D_MODEL, 3), _row(D_MODEL), _row(D_MODEL)],
        out_specs=[_row(D_MODEL), _row(D_MODEL), _row(2 * D_MODEL)],
        out_shape=[jax.ShapeDtypeStruct((lp, D_MODEL), BF16), jax.ShapeDtypeStruct((lp, D_MODEL), BF16),
                   jax.ShapeDtypeStruct((lp, 2 * D_MODEL), BF16)],
        compiler_params=_cp(("parallel",)))(dm, gate, gate, y_mla, y_fox)


def _gate_bwd(da_mla, da_fox, o_mla, o_fox, gate):
    lp = da_mla.shape[0]

    def one(da, o, z, head_of_col):
        sg = _sigmoid(z)
        do = (da * (z * sg)).astype(BF16)
        dz = da * o * (sg * (1.0 + z * (1.0 - sg)))
        delta = sum(_dot(part, head_of_col, 1, 0) for part in _split3(do.astype(F32) * o))
        return do, dz.astype(BF16), delta

    def body(dam_ref, daf_ref, om_ref, of_ref, zm_ref, zf_ref, dom_ref, dof_ref, dz_ref, dlm_ref, dlf_ref):
        f32 = lambda r: r[...].astype(F32)
        head_of_col = (lax.broadcasted_iota(jnp.int32, (D_MODEL, LANES), 0) // HEAD_DIM
                       == lax.broadcasted_iota(jnp.int32, (D_MODEL, LANES), 1)).astype(BF16)
        dom_ref[...], dz_ref[:, 0:D_MODEL], dlm_ref[...] = one(f32(dam_ref), f32(om_ref), f32(zm_ref), head_of_col)
        dof_ref[...], dz_ref[:, D_MODEL:2 * D_MODEL], dlf_ref[...] = one(f32(daf_ref), f32(of_ref), f32(zf_ref),
                                                                        head_of_col)

    return pl.pallas_call(
        body, name="gate_bwd", grid=(lp // BLK,),
        in_specs=[_row(D_MODEL)] * 4 + [_rowc(D_MODEL, 0), _rowc(D_MODEL, 1)],
        out_specs=[_row(D_MODEL), _row(D_MODEL), _row(2 * D_MODEL), _row(LANES), _row(LANES)],
        out_shape=[jax.ShapeDtypeStruct((lp, D_MODEL), BF16), jax.ShapeDtypeStruct((lp, D_MODEL), BF16),
                   jax.ShapeDtypeStruct((lp, 2 * D_MODEL), BF16), jax.ShapeDtypeStruct((lp, LANES), F32),
                   jax.ShapeDtypeStruct((lp, LANES), F32)],
        compiler_params=_cp(("parallel",)))(da_mla, da_fox, o_mla, o_fox, gate, gate)


def _small_bwd(small, dqn, dkvn, dkr, dcol_t, drow_t, gq, gkv, fb, ctab, stab, triu):
    lp = small.shape[0]
    nb = lp // BLK

    def rrow(w):
        return pl.BlockSpec((BLK, w), lambda i: (nb - 1 - i, 0))

    def body(sm_ref, dqn_ref, dkvn_ref, dkr_ref, dcol_ref, drow_ref, gq_ref, gkv_ref, fb_ref, c_ref, s_ref, tri_ref,
             ds_ref, dgq_ref, dgkv_ref, dfb_ref, carry):
        i = pl.program_id(0)

        @pl.when(i == 0)
        def _():
            carry[...] = jnp.zeros_like(carry)
            dgq_ref[...] = jnp.zeros_like(dgq_ref)
            dgkv_ref[...] = jnp.zeros_like(dgkv_ref)
            dfb_ref[...] = jnp.zeros_like(dfb_ref)

        def norm_bwd(x, dn, g, dg_ref):
            r = lax.rsqrt(jnp.mean(x * x, axis=-1, keepdims=True) + RMS_EPS)
            dg_ref[...] += jnp.sum(dn * (x * r), axis=0, keepdims=True)
            w = dn * g
            dot = jnp.mean(w * x, axis=-1, keepdims=True)
            return r * w - x * (r * r * r * dot)

        ds_ref[:, 0:256] = norm_bwd(sm_ref[:, 0:256], dqn_ref[...], gq_ref[...], dgq_ref).astype(BF16)
        ds_ref[:, 256:384] = norm_bwd(sm_ref[:, 256:384], dkvn_ref[...], gkv_ref[...], dgkv_ref).astype(BF16)

        dk = dkr_ref[0]
        for p in range(1, PAIRS):
            dk = dk + dkr_ref[p]
        dk = _rope(dk, c_ref[...], -s_ref[...])
        lane = lax.broadcasted_iota(jnp.int32, dk.shape, 1)
        dk = jnp.where(lane < MLA_ROPE, dk + pltpu.roll(dk, LANES - MLA_ROPE, 1), 0.0)
        ds_ref[:, 384:512] = dk.astype(BF16)

        dcol = dcol_ref[0]
        for p in range(1, PAIRS):
            dcol = dcol + pltpu.roll(dcol_ref[p], 2 * p, 1)
        rows16 = jnp.concatenate([drow_ref[p, h:h + 1, :] for p in range(PAIRS) for h in range(2)], axis=0)
        eye = (lax.broadcasted_iota(jnp.int32, (HEADS, LANES), 0)
               == lax.broadcasted_iota(jnp.int32, (HEADS, LANES), 1)).astype(BF16)
        drow = sum(_dot(part, eye, 0, 0) for part in _split3(rows16))
        dcr = dcol - drow
        hi, mid, lo = _split3(dcr)
        t = tri_ref[...]
        suf = (_dot(t, hi, 1, 0) + _dot(t, mid, 1, 0)) + _dot(t, lo, 1, 0) + carry[...]
        fl = sm_ref[:, 512:640] + fb_ref[...]
        dfl = jnp.where(_row_valid(nb - 1 - i), -suf * _sigmoid(-fl), 0.0)
        ds_ref[:, 512:640] = dfl.astype(BF16)
        dfb_ref[...] += jnp.sum(dfl, axis=0, keepdims=True)
        carry[...] += jnp.sum(dcr, axis=0, keepdims=True)

    return pl.pallas_call(
        body, name="small_bwd", grid=(nb,),
        in_specs=[rrow(SMALL_W), rrow(256), rrow(128),
                  pl.BlockSpec((PAIRS, BLK, 128), lambda i: (0, nb - 1 - i, 0)),
                  pl.BlockSpec((PAIRS, BLK, 128), lambda i: (0, nb - 1 - i, 0)),
                  pl.BlockSpec((PAIRS, 2, BLK), lambda i: (0, 0, nb - 1 - i)),
                  _full((1, 256)), _full((1, 128)), _full((1, 128)), rrow(128), rrow(128), _full((BLK, BLK))],
        out_specs=[rrow(SMALL_W), _full((1, 256)), _full((1, 128)), _full((1, 128))],
        out_shape=[jax.ShapeDtypeStruct((lp, SMALL_W), BF16), jax.ShapeDtypeStruct((1, 256), F32),
                   jax.ShapeDtypeStruct((1, 128), F32), jax.ShapeDtypeStruct((1, 128), F32)],
        scratch_shapes=[pltpu.VMEM((1, 128), F32)],
        compiler_params=_cp(("arbitrary",)))(small, dqn, dkvn, dkr, dcol_t, drow_t, gq, gkv, fb, ctab, stab, triu)


def _pre_bwd(du, x2, meta, dy, gpre):
    s_rows = x2.shape[0]
    lp = PAD + s_rows
    shift = _shift_rows(D_MODEL)

    def body(du_ref, x_ref, meta_ref, dy_ref, g_ref, dx_ref, dmeta_ref, dg_ref):
        i = pl.program_id(0)

        @pl.when(i == 0)
        def _():
            dg_ref[...] = jnp.zeros_like(dg_ref)

        hv = _h_block(i, x_ref, meta_ref)
        duv = du_ref[...]
        r = lax.rsqrt(jnp.mean(hv * hv, axis=-1, keepdims=True) + RMS_EPS)
        dg_ref[...] += jnp.sum(duv * (hv * r), axis=0, keepdims=True)
        w = duv * g_ref[...]
        dot = jnp.mean(w * hv, axis=-1, keepdims=True)
        dh = dy_ref[...] + (r * w - hv * (r * r * r * dot))
        dx_ref[...] = dh

        @pl.when(i == 0)
        def _():
            dmeta_ref[...] = dh[0:N_META, :]

    return pl.pallas_call(
        body, name="pre_bwd", grid=(lp // BLK,),
        in_specs=[_row(D_MODEL), shift, _full((N_META, D_MODEL)), _row(D_MODEL), _full((1, D_MODEL))],
        out_specs=[shift, _full((N_META, D_MODEL)), _full((1, D_MODEL))],
        out_shape=[jax.ShapeDtypeStruct((s_rows, D_MODEL), F32), jax.ShapeDtypeStruct((N_META, D_MODEL), F32),
                   jax.ShapeDtypeStruct((1, D_MODEL), F32)],
        compiler_params=_cp(("arbitrary",)))(du, x2, meta, dy, gpre)


def _pair_masks(rope, pair):
    lane = lax.broadcasted_iota(jnp.int32, (1, LANES), 1)
    mas = [lane < HEAD_DIM, lane >= HEAD_DIM]
    wide = lax.broadcasted_iota(jnp.int32, (1, 2 * LANES), 1)
    extra = MLA_ROPE if rope else BIAS_PARTS
    lo = LANES if rope else LANES + 2 * BIAS_PARTS * pair
    mid = lo + extra
    return mas, [(wide < HEAD_DIM) | ((wide >= lo) & (wide < mid)),
                 ((wide >= HEAD_DIM) & (wide < LANES)) | ((wide >= mid) & (wide < mid + extra))]


def _mask2(x, masks):
    return [jnp.where(m, x, jnp.zeros_like(x)) for m in masks]


def _q_heads(q_rows, rope, mas, hmask):
    if rope:
        return _mask2(q_rows, hmask)
    zero = jnp.zeros((q_rows.shape[0], LANES), BF16)
    return [jnp.concatenate([jnp.where(m, q_rows, zero), jnp.where(hm[:, LANES:], zero + 1, zero)], axis=1)
            for m, hm in zip(mas, hmask)]


def _transposed_cols(x, group, name):
    lp = x.shape[0]

    def body(x_ref, o_ref):
        o_ref[...] = x_ref[...].T

    return pl.pallas_call(
        body, name=name, grid=(lp // BLK,),
        in_specs=[pl.BlockSpec((BLK, D_MODEL), lambda i: (i, group))],
        out_specs=pl.BlockSpec((D_MODEL, BLK), lambda i: (0, i)),
        out_shape=jax.ShapeDtypeStruct((D_MODEL, lp), x.dtype),
        compiler_params=_cp(("parallel",)))(x)


def _attn_fwd(q, k, v, vt, k2, *, rope, qcol, kcol, vcol, name):
    lp = q.shape[0]
    nq = 1 + (lp - PAD) // QB
    qw = 256 if rope else 128

    def body(q_ref, k_ref, v_ref, vt_ref, k2_ref, o_ref, lse_ref):
        i = pl.program_id(1)
        r0 = pl.multiple_of(jnp.where(i == 0, 0, PAD + QB * (i - 1)), BLK)
        b0 = r0 // BLK
        mas, hmask = _pair_masks(rope, pl.program_id(0))
        qh = _q_heads(q_ref[pl.ds(r0, QB), :], rope, mas, hmask)

        def update(chunks, tiles, groups):
            m = [[cr[0], cr[2]] for _, _, cr in groups]
            l = [[cr[1], cr[3]] for _, _, cr in groups]
            acc = [[cr[4][0:HEAD_DIM], cr[4][HEAD_DIM:LANES]] for _, _, cr in groups]
            qs = [[x[q_lo:q_lo + wq] for x in qh] for q_lo, wq, _ in groups]
            k0s = [pl.multiple_of(kc * BLK, BLK) for kc, _ in chunks]
            kks = [jnp.concatenate([k_ref[pl.ds(k0, n), :], k2_ref[pl.ds(k0, n), :]], axis=1)
                   for k0, (_, n) in zip(k0s, chunks)]
            jobs = [(g, ci, mask, h) for g, ci, mask in tiles for h in range(2)]
            score = lambda t: _dot(kks[jobs[t][1]], qs[jobs[t][0]][jobs[t][3]], 1, 1)
            ss = [score(t) for t in range(min(AHEAD, len(jobs)))]
            for t, (g, ci, mask, h) in enumerate(jobs):
                if t + AHEAD < len(jobs):
                    ss.append(score(t + AHEAD))
                s = ss[t] if mask is None else jnp.where(mask, ss[t], NEG)
                m_new = jnp.maximum(m[g][h], jnp.max(s, axis=0, keepdims=True))
                alpha = jnp.exp2(m[g][h] - m_new)
                p = jnp.exp2(s - m_new)
                l[g][h] = alpha * l[g][h] + jnp.sum(p, axis=0, keepdims=True)
                m[g][h] = m_new
                n = chunks[ci][1]
                if n == BLK:
                    pv = _dot(vt_ref[pl.ds(HEAD_DIM * h, HEAD_DIM), pl.ds(k0s[ci], BLK)], p.astype(BF16), 1, 0)
                else:
                    vm = jnp.where(mas[h], v_ref[0:n, :], jnp.zeros((), BF16))
                    pv = _dot(vm, p.astype(BF16), 0, 0)[HEAD_DIM * h:HEAD_DIM * (h + 1)]
                acc[g][h] = alpha * acc[g][h] + pv
            return [(m[g][0], l[g][0], m[g][1], l[g][1], jnp.concatenate(acc[g], axis=0)) for g in range(len(groups))]

        def full_chunks(kcs, carry):
            return update([(kc, BLK) for kc in kcs], [(0, ci, None) for ci in range(len(kcs))], [(0, QB, carry)])[0]

        neg = jnp.full((1, QB), NEG, F32)
        zero = jnp.zeros((1, QB), F32)
        c = (neg, zero, neg, zero, jnp.zeros((LANES, QB), F32))
        n_mid = jnp.maximum(b0 - 1, 0)
        c = lax.fori_loop(0, n_mid // 4, lambda t, cr: full_chunks([4 * t + u for u in (1, 2, 3, 4)], cr), c)
        c = lax.fori_loop(0, (n_mid % 4) // 2, lambda t, cr: full_chunks([n_mid - 1, n_mid], cr), c)
        key_l = lax.broadcasted_iota(jnp.int32, (BLK, BLK), 0)
        qry_l = lax.broadcasted_iota(jnp.int32, (BLK, BLK), 1)
        tri = (key_l <= qry_l) & (b0 > 0)
        meta_ok = (key_l[0:N_META] <= qry_l[0:N_META]) | (b0 > 0)
        lo, hi = update([(0, N_META), (b0, BLK), (b0 + 1, BLK)],
                        [(0, 0, meta_ok), (1, 0, None), (0, 1, tri), (1, 1, None), (1, 2, tri)],
                        [(0, BLK, tuple(a[:, 0:BLK] for a in c)), (BLK, QB - BLK, tuple(a[:, BLK:QB] for a in c))])
        c = tuple(jnp.concatenate([a, b], axis=1) for a, b in zip(lo, hi))
        inv =jnp.concatenate([jnp.broadcast_to(1.0 / c[1], (HEAD_DIM, QB)),
                               jnp.broadcast_to(1.0 / c[3], (HEAD_DIM, QB))], axis=0)
        o_t = (c[4] * inv).T.astype(BF16)
        lses = [c[2 * h] + jnp.log(c[2 * h + 1]) * LOG2E for h in range(2)]
        o_ref[pl.ds(r0, BLK), :] = o_t[0:BLK]
        for h in range(2):
            lse_ref[0, h:h + 1, pl.ds(r0, BLK)] = lses[h][:, 0:BLK]

        @pl.when(i > 0)
        def _():
            r1 = pl.multiple_of(r0 + BLK, BLK)
            o_ref[pl.ds(r1, QB - BLK), :] = o_t[BLK:QB]
            for h in range(2):
                lse_ref[0, h:h + 1, pl.ds(r1, QB - BLK)] = lses[h][:, BLK:QB]

    in_specs = [pl.BlockSpec((lp, qw), lambda p, i: (0, qcol + p)),
                pl.BlockSpec((lp, 128), lambda p, i: (0, kcol(p))),
                pl.BlockSpec((BLK, 128), lambda p, i: (0, vcol(p))),
                pl.BlockSpec((128, lp), lambda p, i: (p, 0)),
                pl.BlockSpec((lp, 128), lambda p, i: (0, 0))]
    return pl.pallas_call(
        body, name=name, grid=(PAIRS, nq), in_specs=in_specs,
        out_specs=[pl.BlockSpec((lp, 128), lambda p, i: (0, p)),
                   pl.BlockSpec((1, 2, lp), lambda p, i: (p, 0, 0))],
        out_shape=[jax.ShapeDtypeStruct((lp, D_MODEL), BF16), jax.ShapeDtypeStruct((PAIRS, 2, lp), F32)],
        compiler_params=_cp(("parallel", "arbitrary"), VMEM_BIG))(q, k, v, vt, k2)


def _attn_bwd(q, k, v, k2, do, delta, lse, *, rtabs=None, scale, qcol, kcol, vcol, name):
    lp = q.shape[0]
    nb = lp // BLK
    rope = rtabs is not None
    bias = not rope
    qw = 256 if rope else 128

    def body(*refs):
        it = iter(refs)
        q_ref, k_ref, v_ref, k2_ref = next(it), next(it), next(it), next(it)
        do_ref, dl_ref, lse_ref = next(it), next(it), next(it)
        ct_ref, st_ref = (next(it), next(it)) if rope else (None, None)
        dq_out, dk_ref, dv_ref = next(it), next(it), next(it)
        x_ref = next(it)
        drow_ref = next(it) if bias else None
        dq_ref = next(it)
        kb = pl.program_id(1)
        mas, hmask = _pair_masks(rope, pl.program_id(0))
        lane = lax.broadcasted_iota(jnp.int32, (1, LANES), 1)

        @pl.when(kb == 0)
        def _():
            dq_ref[...] = jnp.zeros_like(dq_ref)
            if bias:
                drow_ref[...] = jnp.zeros_like(drow_ref)

        def key_pass(n, w):
            kk = jnp.concatenate([k_ref[0:n, :], k2_ref[0:n, :]], axis=1)
            vh = _mask2(v_ref[0:n, :], mas)
            kcat = jnp.concatenate([x[:, 0:qw] for x in _mask2(kk, hmask)], axis=0)
            diag_mask = (lax.broadcasted_iota(jnp.int32, (n, w), 0) <= lax.broadcasted_iota(jnp.int32, (n, w), 1))

            def chunk(qc, carry, mask):
                carry = list(carry)
                q0 = qc * w if isinstance(qc, int) else pl.multiple_of(qc * w, w)
                dov = do_ref[pl.ds(q0, w), :]
                doh = _mask2(dov, mas)
                qh = _q_heads(q_ref[pl.ds(q0, w), :], rope, mas, hmask)
                pbs, dss = [], []
                for h in range(2):
                    p = jnp.exp2(_dot(kk, qh[h], 1, 1) - lse_ref[0, h:h + 1, pl.ds(q0, w)])
                    if mask is not None:
                        p = jnp.where(mask, p, 0.0)
                    ds = p * (_dot(vh[h], dov, 1, 1) - dl_ref[0, h:h + 1, pl.ds(q0, w)])
                    if bias:
                        drow_ref[0, h:h + 1, pl.ds(q0, w)] += jnp.sum(ds, axis=0, keepdims=True)
                        carry[2 + h] = carry[2 + h] + jnp.sum(ds, axis=1, keepdims=True)
                    pbs.append(p.astype(BF16))
                    dss.append(ds.astype(BF16))
                ds_lanes = jnp.concatenate(dss, axis=1)
                ds_rows = jnp.concatenate(dss, axis=0)
                qcat = jnp.concatenate([x[:, 0:qw] for x in qh], axis=0)
                carry[0] = carry[0] + _dot(ds_lanes, qcat, 1, 0)
                carry[1] = carry[1] + _dot(jnp.concatenate(pbs, axis=1), jnp.concatenate(doh, axis=0), 1, 0)
                dq_ref[pl.ds(q0, w), :] += _dot(ds_rows, kcat, 0, 0)
                return tuple(carry)

            c = [jnp.zeros((n, qw), F32), jnp.zeros((n, LANES), F32)]
            if bias:
                c += [jnp.zeros((n, 1), F32), jnp.zeros((n, 1), F32)]
            c = tuple(c)
            if w != BLK:
                for qc in range(lp // w):
                    c = chunk(qc, c, diag_mask if qc == 0 else None)
            else:
                groups = (nb - kb) // UNROLL

                def several(t, cr):
                    for u in range(UNROLL):
                        cr = chunk(kb + UNROLL * t + u, cr, (diag_mask | (t > 0)) if u == 0 else None)
                    return cr

                c = lax.fori_loop(0, groups, several, c)
                start = kb + UNROLL * groups
                pairs = (nb - start) // 2

                def two(t, cr):
                    qc = start + 2 * t
                    return chunk(qc + 1, chunk(qc, cr, diag_mask | (qc > kb)), None)

                c = lax.fori_loop(0, pairs, two, c)
                c = lax.fori_loop(start + 2 * pairs, nb, lambda qc, cr: chunk(qc, cr, diag_mask | (qc > kb)), c)

            def rows(a, dtype):
                a = a.astype(dtype)
                return a if n == BLK else jnp.concatenate([a, jnp.zeros((BLK - n, a.shape[1]), dtype)], axis=0)

            dk = c[0] * LN2
            dk_ref[...] = rows(dk[:, 0:LANES], BF16)
            dv_ref[...] = rows(c[1], BF16)
            if rope:
                x_ref[0] = rows(dk[:, LANES:2 * LANES], F32)
            if bias:
                x_ref[0] = rows(jnp.where(lane == 0, c[2], jnp.where(lane == 1, c[3], 0.0)), F32)

        @pl.when(kb == 0)
        def _():
            key_pass(N_META, lp // 2)

        @pl.when(kb > 0)
        def _():
            key_pass(BLK, BLK)

        @pl.when(kb == nb - 1)
        def _():
            def fin(c, carry):
                r0 = pl.multiple_of(c * BLK, BLK)
                dq = dq_ref[pl.ds(r0, BLK), :] * scale
                if rope:
                    back = _rope(dq[:, LANES:2 * LANES], ct_ref[pl.ds(r0, BLK), :], -st_ref[pl.ds(r0, BLK), :])
                    dq = jnp.concatenate([dq[:, 0:LANES], back], axis=1)
                dq_out[pl.ds(r0, BLK), :] = dq.astype(BF16)
                return carry

            lax.fori_loop(0, nb, fin, 0)

    in_specs = [pl.BlockSpec((lp, qw), lambda p, j: (0, qcol + p)),
                pl.BlockSpec((BLK, 128), lambda p, j: (j, kcol(p))),
                pl.BlockSpec((BLK, 128), lambda p, j: (j, vcol(p))),
                pl.BlockSpec((BLK, 128), lambda p, j: (j, 0)),
                pl.BlockSpec((lp, 128), lambda p, j: (0, p)), pl.BlockSpec((1, 2, lp), lambda p, j: (p, 0, 0)),
                pl.BlockSpec((1, 2, lp), lambda p, j: (p, 0, 0))]
    ins = [q, k, v, k2, do, delta, lse]
    if rope:
        in_specs += [pl.BlockSpec((lp, 128), lambda p, j: (0, 0))] * 2
        ins += list(rtabs)
    out_specs = [pl.BlockSpec((lp, qw), lambda p, j: (0, p)),
                 pl.BlockSpec((BLK, 128), lambda p, j: (j, p)),
                 pl.BlockSpec((BLK, 128), lambda p, j: (j, p)),
                 pl.BlockSpec((1, BLK, 128), lambda p, j: (p, j, 0))]
    out_shape = [jax.ShapeDtypeStruct((lp, PAIRS * qw), BF16), jax.ShapeDtypeStruct((lp, D_MODEL), BF16),
                 jax.ShapeDtypeStruct((lp, D_MODEL), BF16), jax.ShapeDtypeStruct((PAIRS, lp, 128), F32)]
    if bias:
        out_specs.append(pl.BlockSpec((1, 2, lp), lambda p, j: (p, 0, 0)))
        out_shape.append(jax.ShapeDtypeStruct((PAIRS, 2, lp), F32))
    return pl.pallas_call(
        body, name=name, grid=(PAIRS, nb), in_specs=in_specs, out_specs=out_specs, out_shape=out_shape,
        scratch_shapes=[pltpu.VMEM((lp, qw), F32)],
        compiler_params=_cp(("parallel", "arbitrary"), VMEM_BIG))(*ins)


def _adamw(w, g, m, v, name):
    lead = w.ndim - 2
    rows, cols = w.shape[lead:]
    big = rows * cols > 512 * 1024
    tr = 128 if big and rows % 128 == 0 else rows
    tc = 256 if big and tr == rows else cols

    def body(w_ref, g_ref, m_ref, v_ref, d_ref, nm_ref, nv_ref):
        gv = g_ref[...]
        nm = ADAM_B1 * m_ref[...] + (1.0 - ADAM_B1) * gv
        nv = ADAM_B2 * v_ref[...] + (1.0 - ADAM_B2) * (gv * gv)
        m_hat = nm / (1.0 - ADAM_B1 ** ADAM_STEP)
        v_hat = nv / (1.0 - ADAM_B2 ** ADAM_STEP)
        d_ref[...] = -ADAM_LR * (m_hat / (jnp.sqrt(v_hat) + ADAM_EPS) + ADAM_WD * w_ref[...])
        nm_ref[...] = nm
        nv_ref[...] = nv

    spec = pl.BlockSpec((1,) * lead + (tr, tc), lambda i, j: (0,) * lead + (i, j))
    return pl.pallas_call(
        body, name=name, grid=(rows // tr, cols // tc), in_specs=[spec] * 4, out_specs=[spec] * 3,
        out_shape=[jax.ShapeDtypeStruct(w.shape, F32)] * 3,
        compiler_params=_cp(("parallel", "parallel"), VMEM_BIG))(w, g, m, v)


def _add_cores(g, from_sib, name):
    n, rows, cols = g.shape
    half = rows // 2
    tr = _tile(half, (256, 240))
    nt = half // tr

    def body(lo_ref, hi_ref, s_ref, o_ref):
        mine = jnp.where(lax.axis_index("c") == 0, lo_ref[0], hi_ref[0])
        o_ref[0] = (mine.astype(F32) + s_ref[0].astype(F32)).astype(BF16)

    return pl.pallas_call(
        body, name=name, grid=(n, nt),
        in_specs=[pl.BlockSpec((1, tr, cols), lambda j, i: (j, i, 0)),
                  pl.BlockSpec((1, tr, cols), lambda j, i: (j, nt + i, 0)),
                  pl.BlockSpec((1, tr, cols), lambda j, i: (j, i, 0))],
        out_specs=pl.BlockSpec((1, tr, cols), lambda j, i: (j, i, 0)),
        out_shape=jax.ShapeDtypeStruct((n, half, cols), BF16),
        compiler_params=_cp(("parallel", "parallel"), VMEM_BIG))(g, g, from_sib)


def _add_chips(x, own, name):
    n, rows, cols = x.shape
    tr = _tile(rows, (256, 240))

    def body(x_ref, own_ref, o_ref):
        me = 2 * lax.axis_index("x") + lax.axis_index("y")
        v = [jnp.where(me == k, own_ref[...], x_ref[k]).astype(F32) for k in range(N_CHIPS)]
        o_ref[...] = ((v[0] + v[1]) + v[2]) + v[3]

    return pl.pallas_call(
        body, name=name, grid=(rows // tr,),
        in_specs=[pl.BlockSpec((n, tr, cols), lambda i: (0, i, 0)), pl.BlockSpec((tr, cols), lambda i: (i, 0))],
        out_specs=pl.BlockSpec((tr, cols), lambda i: (i, 0)),
        out_shape=jax.ShapeDtypeStruct((rows, cols), F32), compiler_params=_cp(("parallel",), VMEM_BIG))(x, own)


def _axes():
    return lax.axis_index("x"), lax.axis_index("y"), lax.axis_index("c")


def _other_chips(x, y):
    return [(1 - x, y), (x, 1 - y), (1 - x, 1 - y)]


ANY = pl.BlockSpec(memory_space=pl.ANY)


def _rcopy(src, dst, send_sems, recv_sems, k, to):
    return pltpu.make_async_remote_copy(src_ref=src, dst_ref=dst, send_sem=send_sems.at[k], recv_sem=recv_sems.at[k],
                                        device_id=to, device_id_type=MESH)


def _gather_weights(shards, meta):
    n = len(shards)

    def body(*refs):
        srcs, meta_ref = refs[:n], refs[n]
        outs, mout_ref = refs[n + 1:2 * n + 1], refs[2 * n + 1]
        send_sems, recv_sems = refs[2 * n + 2:]
        x, y, c = _axes()
        me = 2 * x + y
        sib = (x, y, 1 - c)
        chips = _other_chips(x, y)

        def half(t, chip_idx, cc):
            hr = shards[t].shape[0] // 2
            return outs[t].at[chip_idx, pl.ds(cc * hr, hr), :]

        first = []
        for j, (px, py) in enumerate(chips):
            for t in range(n):
                hr = shards[t].shape[0] // 2
                first.append(_rcopy(srcs[t].at[pl.ds(c * hr, hr), :], half(t, me, c), send_sems, recv_sems,
                                    3 * t + j, (px, py, c)))
            first.append(_rcopy(meta_ref, mout_ref.at[me], send_sems, recv_sems, 3 * n + j, (px, py, c)))
        for cp in first:
            cp.start()
        passed = []
        for j, (px, py) in enumerate(chips):
            src_chip = 2 * px + py
            for t in range(n):
                _rcopy(half(t, src_chip, c), half(t, src_chip, c), send_sems, recv_sems, 3 * t + j, sib).wait_recv()
                fwd = _rcopy(half(t, src_chip, c), half(t, src_chip, c), send_sems, recv_sems, 3 * (n + 1 + t) + j, sib)
                fwd.start()
                passed.append(fwd)
            _rcopy(mout_ref.at[src_chip], mout_ref.at[src_chip], send_sems, recv_sems, 3 * n + j, sib).wait_recv()
        for j, (px, py) in enumerate(chips):
            src_chip = 2 * px + py
            for t in range(n):
                _rcopy(half(t, src_chip, 1 - c), half(t, src_chip, 1 - c), send_sems, recv_sems,
                       3 * (n + 1 + t) + j, sib).wait_recv()
        for cp in first + passed:
            cp.wait_send()

    nsem = 3 * (2 * n + 1)
    return pl.pallas_call(
        body, name="gather_weights", in_specs=[ANY] * (n + 1), out_specs=[ANY] * (n + 1),
        out_shape=[jax.ShapeDtypeStruct((N_CHIPS,) + s.shape, s.dtype) for s in shards]
        + [jax.ShapeDtypeStruct((N_CHIPS,) + meta.shape, meta.dtype)],
        scratch_shapes=[pltpu.SemaphoreType.DMA((nsem,)), pltpu.SemaphoreType.DMA((nsem,))])(*shards, meta)


def _gather_late(shard):
    rows, cols = shard.shape
    hr = rows // 2
    src = jax.new_ref(shard, memory_space=pltpu.MemorySpace.HBM)
    out = jax.empty_ref(jax.ShapeDtypeStruct((N_CHIPS, rows, cols), shard.dtype), memory_space=pltpu.MemorySpace.HBM)

    @pl.kernel(mesh=plsc.ScalarSubcoreMesh(axis_name="seq", num_cores=1), name="gather_late",
               scratch_types=(pltpu.SemaphoreType.DMA((6,)), pltpu.SemaphoreType.DMA((6,))),
               compiler_params=pltpu.CompilerParams(collective_id=1))
    def launch(send_sems, recv_sems):
        x, y, c = _axes()
        me = 2 * x + y
        sib = (x, y, 1 - c)
        chips = _other_chips(x, y)
        barrier = pltpu.get_barrier_semaphore()
        for px, py in chips:
            pl.semaphore_signal(barrier, inc=1, device_id=(px, py, c), device_id_type=MESH)
        pl.semaphore_signal(barrier, inc=1, device_id=sib, device_id_type=MESH)
        pl.semaphore_wait(barrier, 4)

        def half(chip_idx, cc):
            return out.at[chip_idx, pl.ds(cc * hr, hr), :]

        first = [_rcopy(src.at[pl.ds(c * hr, hr), :], half(me, c), send_sems, recv_sems, j, (px, py, c))
                 for j, (px, py) in enumerate(chips)]
        for cp in first:
            cp.start()
        passed = []
        for j, (px, py) in enumerate(chips):
            land = half(2 * px + py, c)
            _rcopy(land, land, send_sems, recv_sems, j, sib).wait_recv()
            fwd = _rcopy(land, land, send_sems, recv_sems, 3 + j, sib)
            fwd.start()
            passed.append(fwd)
        for j, (px, py) in enumerate(chips):
            land = half(2 * px + py, 1 - c)
            _rcopy(land, land, send_sems, recv_sems, 3 + j, sib).wait_recv()
        for cp in first + passed:
            cp.wait_send()

    launch()
    return out[...]


def _swap_halves(gs):
    n = len(gs)
    ncopies = sum(g.shape[0] for g in gs)

    def body(*refs):
        srcs, outs = refs[:n], refs[n:2 * n]
        send_sems, recv_sems = refs[2 * n:]
        x, y, c = _axes()
        cps = []
        for t in range(n):
            hr = gs[t].shape[1] // 2
            for j in range(gs[t].shape[0]):
                cps.append(_rcopy(srcs[t].at[j, pl.ds((1 - c) * hr, hr), :], outs[t].at[j], send_sems, recv_sems,
                                  len(cps), (x, y, 1 - c)))
        for cp in cps:
            cp.start()
        for cp in cps:
            cp.wait()

    return pl.pallas_call(
        body, name="swap_halves", in_specs=[ANY] * n, out_specs=[ANY] * n,
        out_shape=[jax.ShapeDtypeStruct((g.shape[0], g.shape[1] // 2, g.shape[2]), g.dtype) for g in gs],
        scratch_shapes=[pltpu.SemaphoreType.DMA((ncopies,)), pltpu.SemaphoreType.DMA((ncopies,))])(*gs)


def _scatter_chips(parts):
    n = len(parts)
    srcs = [jax.new_ref(p, memory_space=pltpu.MemorySpace.HBM) for p in parts]
    outs = [jax.empty_ref(jax.ShapeDtypeStruct(p.shape, p.dtype), memory_space=pltpu.MemorySpace.HBM) for p in parts]

    @pl.kernel(mesh=plsc.ScalarSubcoreMesh(axis_name="seq", num_cores=1), name="scatter_chips",
               scratch_types=(pltpu.SemaphoreType.DMA((3 * n,)), pltpu.SemaphoreType.DMA((3 * n,))),
               compiler_params=pltpu.CompilerParams(collective_id=0))
    def launch(send_sems, recv_sems):
        x, y, c = _axes()
        me = 2 * x + y
        chips = _other_chips(x, y)
        barrier = pltpu.get_barrier_semaphore()
        for px, py in chips:
            pl.semaphore_signal(barrier, inc=1, device_id=(px, py, c), device_id_type=MESH)
        pl.semaphore_wait(barrier, 3)
        cps = []
        for j, (px, py) in enumerate(chips):
            for t in range(n):
                cps.append(_rcopy(srcs[t].at[2 * px + py], outs[t].at[me], send_sems, recv_sems, 3 * t + j,
                                  (px, py, c)))
        for cp in cps:
            cp.start()
        for cp in cps:
            cp.wait()

    launch()
    return [o[...] for o in outs]


def _swap_reduced(rs):
    n = len(rs)

    def body(*refs):
        srcs, outs = refs[:n], refs[n:2 * n]
        send_sems, recv_sems = refs[2 * n:]
        x, y, c = _axes()
        cps = [_rcopy(srcs[t], outs[t], send_sems, recv_sems, t, (x, y, 1 - c)) for t in range(n)]
        for cp in cps:
            cp.start()
        for cp in cps:
            cp.wait()

    return pl.pallas_call(
        body, name="swap_reduced", in_specs=[ANY] * n, out_specs=[ANY] * n,
        out_shape=[jax.ShapeDtypeStruct(r.shape, r.dtype) for r in rs],
        scratch_shapes=[pltpu.SemaphoreType.DMA((n,)), pltpu.SemaphoreType.DMA((n,))])(*rs)


SMALL_ROWS = 24 + 128


def _allreduce_small(vec):
    def body(v_ref, out_ref, slots, send_sems, recv_sems):
        x, y, c = _axes()
        me = 4 * x + 2 * y + c
        slots[me] = v_ref[...]
        cps = []
        for k in range(1, 8):
            kx, ky, kc = (k >> 2) & 1, (k >> 1) & 1, k & 1
            peer = (1 - x if kx else x, 1 - y if ky else y, 1 - c if kc else c)
            cps.append(_rcopy(v_ref, slots.at[me], send_sems, recv_sems, k - 1, peer))
        for cp in cps:
            cp.start()
        for cp in cps:
            cp.wait()
        tot = slots[0]
        for k in range(1, 8):
            tot = tot + slots[k]
        out_ref[...] = tot

    return pl.pallas_call(
        body, name="allreduce_small",
        in_specs=[pl.BlockSpec(memory_space=pltpu.VMEM)], out_specs=pl.BlockSpec(memory_space=pltpu.VMEM),
        out_shape=jax.ShapeDtypeStruct((SMALL_ROWS, 128), F32),
        scratch_shapes=[pltpu.VMEM((8, SMALL_ROWS, 128), F32), pltpu.SemaphoreType.DMA((7,)),
                        pltpu.SemaphoreType.DMA((7,))])(vec)


def _pack_p2(w_uq, w_ukv, w_br_mla, w_br_fox, w_out, dtype):
    parts = [w_uq.reshape(96, D_MODEL), w_ukv.reshape(64, D_MODEL), w_br_mla, w_br_fox, w_out]
    return jnp.concatenate([p.astype(dtype) for p in parts], axis=0)


def _unpack_p2(pk):
    return pk[0:96].reshape(256, 384), pk[96:160].reshape(128, 512), pk[160:416], pk[416:672], pk[672:928]


def _uq_arrange(w):
    w3 = w.reshape(256, HEADS, 96)
    nope = w3[:, :, :64].reshape(256, PAIRS, 128)
    pe = w3[:, :, 64:].reshape(256, PAIRS, 64)
    return jnp.concatenate([nope, pe, jnp.zeros((256, PAIRS, 64), w.dtype)], axis=2).reshape(256, PAIRS * 256)


def _uq_restore(g):
    g3 = g.reshape(256, PAIRS, 256)
    nope = g3[:, :, :128].reshape(256, HEADS, 64)
    pe = g3[:, :, 128:192].reshape(256, HEADS, 32)
    return jnp.concatenate([nope, pe], axis=2).reshape(256, HEADS * 96)


def _ukv_arrange(w):
    w3 = w.reshape(128, HEADS, 128)
    return jnp.concatenate([w3[:, :, :64].reshape(128, 1024), w3[:, :, 64:].reshape(128, 1024)], axis=1)


def _ukv_restore(g):
    kn = g[:, :1024].reshape(128, HEADS, 64)
    vv = g[:, 1024:].reshape(128, HEADS, 64)
    return jnp.concatenate([kn, vv], axis=2).reshape(128, HEADS * 128)


def _rope_tables(lp):
    r = np.arange(lp)
    pos = np.where(r < N_META, r, np.where(r >= PAD, r - PAD + N_META, 0)).astype(np.float32)
    half = MLA_ROPE // 2
    inv_freq = np.float32(ROPE_THETA) ** (-np.arange(half, dtype=np.float32) / np.float32(half))
    ang = (pos[:, None] * inv_freq[None, :]).astype(np.float32)
    cos, sin = np.cos(ang).astype(np.float32), np.sin(ang).astype(np.float32)
    one, zero = np.ones((lp, 64), np.float32), np.zeros((lp, 64), np.float32)
    return (jnp.asarray(np.concatenate([cos, cos, cos, cos, one], axis=1)),
            jnp.asarray(np.concatenate([-sin, sin, -sin, sin, zero], axis=1)))


def _pad_lanes(v, n=128):
    return jnp.pad(v, ((0, 0), (0, n - v.shape[1])))


def _in_cols(slabs, a, b):
    out = []
    for j in range(N_CHIPS):
        lo, hi = max(a, W_IN_SHARD * j), min(b, W_IN_SHARD * (j + 1))
        if lo < hi:
            out.append(slabs[j][:, lo - W_IN_SHARD * j:hi - W_IN_SHARD * j])
    return out


def _local_step(x2, tgt2, meta_f, w_small, w_attn, w_gate, w_uq_f, w_ukv_f, w_bm, w_bf, w_o, pre_norm_g,
                post_norm_g, mla_q_norm_g, mla_kv_norm_g, fox_forget_b, start_exchange=None):
    s_rows = x2.shape[0]
    lp = PAD + s_rows
    w_uq_a = _uq_arrange(w_uq_f)
    w_ukv_a = _ukv_arrange(w_ukv_f)

    ctab, stab = _rope_tables(lp)
    ii = jnp.arange(BLK)
    tri_lo = (ii[:, None] >= ii[None, :]).astype(BF16)
    tri_up = (ii[:, None] <= ii[None, :]).astype(BF16)
    fb128 = _pad_lanes(fox_forget_b)

    u = _rms_pre(x2, meta_f, pre_norm_g)
    small = _mm(u, w_small, mode="nn", out_dtype=F32, name="proj_small")
    attn = _mm(u, w_attn, mode="nn", out_dtype=BF16, name="proj_attn",
               col_scale=(HEADS * HEAD_DIM, FOX_SCALE * LOG2E))
    gate = _mm(u, w_gate, mode="nn", out_dtype=BF16, name="proj_gate")
    qn, kvn, kr, kb = _small_prep(small, mla_q_norm_g, mla_kv_norm_g, fb128, ctab, stab, tri_lo)
    qcat = _mm(qn, w_uq_a, mode="nn", out_dtype=BF16, name="mla_q", row_ins=(ctab, stab),
               epilogue=lambda tile, c, s: _rope_pairs(tile, c, s) * (MLA_SCALE * LOG2E))
    kv = _mm(kvn, w_ukv_a, mode="nn", out_dtype=BF16, name="mla_kv")

    mla_cols = dict(qcol=0, kcol=lambda p: p, vcol=lambda p: PAIRS + p)
    fox_cols = dict(qcol=0, kcol=lambda p: PAIRS + p, vcol=lambda p: 2 * PAIRS + p)
    o_mla, lse_mla = _attn_fwd(qcat, kv, kv, _transposed_cols(kv, 1, "mla_vt"), kr, rope=True, name="mla_fwd",
                               **mla_cols)
    o_fox, lse_fox = _attn_fwd(attn, attn, attn, _transposed_cols(attn, 2, "fox_vt"), kb, rope=False,
                               name="fox_fwd", **fox_cols)

    a_mla, a_fox = _gate_fwd(o_mla, o_fox, gate)
    y_mla = _mm(a_mla, w_bm, mode="nn", out_dtype=BF16, name="br_mla")
    y_fox = _mm(a_fox, w_bf, mode="nn", out_dtype=BF16, name="br_fox")
    mg = _merge_fwd(gate, y_mla, y_fox)
    mixed = _mm(mg, w_o, mode="nn", out_dtype=F32, name="out_proj")
    dmixed, dy, loss_p, dg_post = _tail(x2, mixed, tgt2, post_norm_g)

    d_w_out = _mm(mg, dmixed, mode="tn", out_dtype=F32, name="d_w_out")
    dm = _mm(dmixed, w_o, mode="nt", out_dtype=BF16, name="d_merge")
    dy_mla, dy_fox, dgate_ab = _merge_bwd(dm, gate, y_mla, y_fox)
    d_w_bm = _mm(a_mla, dy_mla, mode="tn", out_dtype=F32, name="d_w_br_mla")
    d_w_bf = _mm(a_fox, dy_fox, mode="tn", out_dtype=F32, name="d_w_br_fox")
    da_mla = _mm(dy_mla, w_bm, mode="nt", out_dtype=BF16, name="d_a_mla")
    da_fox = _mm(dy_fox, w_bf, mode="nt", out_dtype=BF16, name="d_a_fox")
    do_mla, do_fox, dgate_z, dl_mla, dl_fox = _gate_bwd(da_mla, da_fox, o_mla, o_fox, gate)
    dl_mla, dl_fox = (d[:, :HEADS].T.reshape(PAIRS, 2, lp) for d in (dl_mla, dl_fox))

    dq_a, dkn, dvm, dkr = _attn_bwd(qcat, kv, kv, kr, do_mla, dl_mla, lse_mla, rtabs=(ctab, stab),
                                    scale=MLA_SCALE, name="mla_bwd", **mla_cols)
    dfq, dfk, dfv, dcol, drow = _attn_bwd(attn, attn, attn, kb, do_fox, dl_fox, lse_fox, scale=FOX_SCALE,
                                          name="fox_bwd", **fox_cols)

    d_w_uq_a = _mm(qn, dq_a, mode="tn", out_dtype=F32, name="d_w_uq")
    dqn = _mm(dq_a, w_uq_a, mode="nt", out_dtype=F32, name="d_qn")
    d_w_ukv_a = jnp.concatenate([_mm(kvn, dkn, mode="tn", out_dtype=F32, name="d_w_uk"),
                                 _mm(kvn, dvm, mode="tn", out_dtype=F32, name="d_w_uv")], axis=1)
    dkvn = _mm(dkn, w_ukv_a[:, :1024], mode="nt", out_dtype=F32, name="d_kvn_k")
    dkvn = _mm(dvm, w_ukv_a[:, 1024:], mode="nt", out_dtype=F32, name="d_kvn_v", acc=dkvn)
    dsmall, dg_q, dg_kv, dfb = _small_bwd(small, dqn, dkvn, dkr, dcol, drow, mla_q_norm_g, mla_kv_norm_g,
                                          fb128, ctab, stab, tri_up)

    dw_small = _mm(u, dsmall, mode="tn", out_dtype=BF16, name="d_w_small")
    dw_fq = _mm(u, dfq, mode="tn", out_dtype=BF16, name="d_w_fq")
    dw_fk = _mm(u, dfk, mode="tn", out_dtype=BF16, name="d_w_fk")
    dw_fv = _mm(u, dfv, mode="tn", out_dtype=BF16, name="d_w_fv")
    dw_z = _mm(u, dgate_z, mode="tn", out_dtype=BF16, name="d_w_z")
    dw_g = _mm(u, dgate_ab, mode="tn", out_dtype=BF16, name="d_w_g")
    d_w_in = (dw_small, dw_z, dw_fq, dw_fk, dw_fv, dw_g)
    d_w_uq = _uq_restore(d_w_uq_a)
    d_w_ukv = _ukv_restore(d_w_ukv_a)
    token = start_exchange(d_w_in, d_w_uq, d_w_ukv, d_w_bm, d_w_bf, d_w_out) if start_exchange else None
    du = _mm_sum_nt([(dsmall, w_small), (dfq, w_attn[:, 0:1024]), (dfk, w_attn[:, 1024:2048]),
                     (dfv, w_attn[:, 2048:3072]), (dgate_z, w_gate[:, 0:2048]), (dgate_ab, w_gate[:, 2048:4096])],
                    name="d_u", after=token)
    dx, dmeta, dg_pre = _pre_bwd(du, x2, meta_f, dy, pre_norm_g)
    return (loss_p, dx, dmeta, d_w_in, d_w_uq, d_w_ukv, d_w_bm, d_w_bf, d_w_out, dg_pre, dg_post, dg_q, dg_kv, dfb)


def _w_in_slabs(pieces):
    dw_small, dw_z, dw_fq, dw_fk, dw_fv, dw_g = pieces
    runs = [(dw_small[:, 0:416], C_CQ), (dw_z[:, 0:1024], C_ZMLA), (dw_fq, C_FQ), (dw_fk, C_FK), (dw_fv, C_FV),
            (dw_small[:, 512:528], C_FL), (dw_z[:, 1024:2048], C_ZFOX), (dw_g, C_GA)]
    slabs = []
    for j in range(N_CHIPS):
        lo, hi = W_IN_SHARD * j, W_IN_SHARD * (j + 1)
        cols = [a[:, max(lo, c0) - c0:min(hi, c0 + a.shape[1]) - c0] for a, c0 in runs
                if max(lo, c0) < min(hi, c0 + a.shape[1])]
        slabs.append(jnp.concatenate(cols, axis=1))
    return jnp.stack(slabs, axis=0)


def kernel(x, meta_tokens, pre_norm_g, w_in, fox_forget_b, mla_q_norm_g, mla_kv_norm_g, w_uq, w_ukv, w_br_mla, w_br_fox, w_out, post_norm_g, loss_target, m_meta_tokens, m_pre_norm_g, m_w_in, m_fox_forget_b, m_mla_q_norm_g, m_mla_kv_norm_g, m_w_uq, m_w_ukv, m_w_br_mla, m_w_br_fox, m_w_out, m_post_norm_g, v_meta_tokens, v_pre_norm_g, v_w_in, v_fox_forget_b, v_mla_q_norm_g, v_mla_kv_norm_g, v_w_uq, v_w_ukv, v_w_br_mla, v_w_br_fox, v_w_out, v_post_norm_g):
    me = 2 * lax.axis_index("x") + lax.axis_index("y")
    core = lax.axis_index("c")
    w_in_b = w_in.astype(BF16).reshape(D_MODEL, W_IN_SHARD)
    p2 = _pack_p2(w_uq[0], w_ukv[0], w_br_mla[0], w_br_fox[0], w_out[0], BF16)
    w_in_g, meta_g = _gather_weights([w_in_b], meta_tokens)
    p2_g = _gather_late(lax.optimization_barrier((p2, w_in_g))[0])
    slabs = [jnp.where(me == j, w_in_b, w_in_g[j]) for j in range(N_CHIPS)]
    chip = lax.broadcasted_iota(jnp.int32, (N_CHIPS, 1, 1), 0)
    p2_all = jnp.where(chip == me, p2[None], p2_g)
    w_uq_f = p2_all[:, 0:96].reshape(N_CHIPS, 256, 384).transpose(1, 0, 2).reshape(256, 1536)
    w_ukv_f = p2_all[:, 96:160].reshape(N_CHIPS, 128, 512).transpose(1, 0, 2).reshape(128, 2048)
    w_bm, w_bf, w_o = (p2_all[:, lo:lo + 256].reshape(D_MODEL, D_MODEL) for lo in (160, 416, 672))
    meta_f = jnp.where(chip == me, meta_tokens[None], meta_g).transpose(1, 0, 2).reshape(N_META, D_MODEL)
    kpe = _in_cols(slabs, C_KPE, C_ZMLA)
    w_small = jnp.concatenate(_in_cols(slabs, C_CQ, C_KPE) + kpe + kpe + [jnp.zeros((D_MODEL, 64), BF16)]
                              + _in_cols(slabs, C_FL, C_ZFOX) + [jnp.zeros((D_MODEL, 112), BF16)], axis=1)
    w_attn = jnp.concatenate(_in_cols(slabs, C_FQ, C_FL), axis=1)
    w_gate = jnp.concatenate(_in_cols(slabs, C_ZMLA, C_FQ) + _in_cols(slabs, C_ZFOX, C_END), axis=1)

    exchange = {}

    def start_exchange(d_w_in, d_w_uq, d_w_ukv, d_w_bm, d_w_bf, d_w_out):
        g2 = jnp.concatenate(
            [d_w_uq.reshape(256, N_CHIPS, 384).transpose(1, 0, 2).reshape(N_CHIPS, 96, D_MODEL),
             d_w_ukv.reshape(128, N_CHIPS, 512).transpose(1, 0, 2).reshape(N_CHIPS, 64, D_MODEL)]
            + [g.reshape(N_CHIPS, 256, D_MODEL) for g in (d_w_bm, d_w_bf, d_w_out)], axis=1)
        pieces = [p[None] for p in d_w_in]
        from_sib = _swap_halves(pieces + [g2])
        halves = [_add_cores(p, s, "add_cores_" + nm)[0]
                  for p, s, nm in zip(pieces, from_sib, ("small", "z", "fq", "fk", "fv", "g"))]
        parts = [_w_in_slabs(halves), _add_cores(g2, from_sib[-1], "add_cores_rest")]
        exchange.update(parts=parts, landed=_scatter_chips(parts))
        return parts[0][0, 0:16, 0:LANES]

    (loss_p, dx, dmeta, _, _, _, _, _, _, dg_pre, dg_post, dg_q, dg_kv,
     dfb) = _local_step(x[0], loss_target[0], meta_f, w_small, w_attn, w_gate, w_uq_f, w_ukv_f, w_bm, w_bf, w_o,
                        pre_norm_g, post_norm_g, mla_q_norm_g, mla_kv_norm_g, fox_forget_b, start_exchange)

    mine = [_add_chips(l, lax.dynamic_index_in_dim(p, me, 0, keepdims=False), nm)
            for l, p, nm in zip(exchange["landed"], exchange["parts"], ("add_chips_w_in", "add_chips_rest"))]
    theirs = _swap_reduced(mine)
    g_w_in, g_p2 = [jnp.concatenate([jnp.where(core == 0, a, b), jnp.where(core == 0, b, a)], axis=0)
                    for a, b in zip(mine, theirs)]
    g_w_uq, g_w_ukv, g_w_bm, g_w_bf, g_w_out = _unpack_p2(g_p2)
    g_w_in = g_w_in[None]

    vec = jnp.concatenate([dg_pre.reshape(8, 128), dg_post.reshape(8, 128), dg_q.reshape(2, 128), dg_kv,
                           dfb, _pad_lanes(loss_p), jnp.zeros((3, 128), F32), dmeta.reshape(128, 128)], axis=0)
    tot = _allreduce_small(vec)
    loss = tot[20, 0]
    g_meta = lax.dynamic_slice_in_dim(tot[24:].reshape(N_META, D_MODEL), 256 * me, 256, axis=1)

    def small_pack(pre, post, gq_, gkv_, fb_):
        return jnp.concatenate([pre.reshape(8, 128), post.reshape(8, 128), gq_.reshape(2, 128), gkv_,
                                _pad_lanes(fb_), jnp.zeros((4, 128), F32)], axis=0)

    def small_unpack(t):
        return (t[0:8].reshape(1, 1024), t[8:16].reshape(1, 1024), t[16:18].reshape(1, 256), t[18:19],
                t[19:20, 0:HEADS])

    g_small = jnp.concatenate([tot[0:20], jnp.zeros((4, 128), F32)], axis=0)
    sm = _adamw(small_pack(pre_norm_g, post_norm_g, mla_q_norm_g, mla_kv_norm_g, fox_forget_b), g_small,
                small_pack(m_pre_norm_g, m_post_norm_g, m_mla_q_norm_g, m_mla_kv_norm_g, m_fox_forget_b),
                small_pack(v_pre_norm_g, v_post_norm_g, v_mla_q_norm_g, v_mla_kv_norm_g, v_fox_forget_b),
                "adamw_small")
    g_pre, g_post, g_q, g_kv, g_fb = small_unpack(g_small)
    (d_pre, d_post, d_q, d_kv, d_fb), (nm_pre, nm_post, nm_q, nm_kv, nm_fb), (nv_pre, nv_post, nv_q, nv_kv, nv_fb) = (
        small_unpack(t) for t in sm)

    d_meta, nm_meta, nv_meta = _adamw(meta_tokens, g_meta, m_meta_tokens, v_meta_tokens, "adamw_meta")
    d_win, nm_win, nv_win = (t.T[None] for t in _adamw(w_in[0].T, g_w_in[0].T, m_w_in[0].T, v_w_in[0].T,
                                                       "adamw_w_in"))
    d_wuq, nm_wuq, nv_wuq = _adamw(w_uq[0], g_w_uq, m_w_uq[0], v_w_uq[0], "adamw_w_uq")
    d_wukv, nm_wukv, nv_wukv = _adamw(w_ukv[0], g_w_ukv, m_w_ukv[0], v_w_ukv[0], "adamw_w_ukv")
    d_wbm, nm_wbm, nv_wbm = _adamw(w_br_mla[0], g_w_bm, m_w_br_mla[0], v_w_br_mla[0], "adamw_w_br_mla")
    d_wbf, nm_wbf, nv_wbf = _adamw(w_br_fox[0], g_w_bf, m_w_br_fox[0], v_w_br_fox[0], "adamw_w_br_fox")
    d_wo, nm_wo, nv_wo = _adamw(w_out[0], g_w_out, m_w_out[0], v_w_out[0], "adamw_w_out")

    def group(meta_, pre, win, fb_, q_, kv_, wuq, wukv, wbm, wbf, wo, post):
        return (meta_, pre, win, fb_, q_, kv_, wuq[None], wukv[None], wbm[None], wbf[None], wo[None], post)

    grads = group(g_meta, g_pre, g_w_in, g_fb, g_q, g_kv, g_w_uq, g_w_ukv, g_w_bm, g_w_bf, g_w_out, g_post)
    deltas = group(d_meta, d_pre, d_win, d_fb, d_q, d_kv, d_wuq, d_wukv, d_wbm, d_wbf, d_wo, d_post)
    new_m = group(nm_meta, nm_pre, nm_win, nm_fb, nm_q, nm_kv, nm_wuq, nm_wukv, nm_wbm, nm_wbf, nm_wo, nm_post)
    new_v = group(nv_meta, nv_pre, nv_win, nv_fb, nv_q, nv_kv, nv_wuq, nv_wukv, nv_wbm, nv_wbf, nv_wo, nv_post)
    return (loss, dx[None], *grads, *deltas, *new_m, *new_v)
```

```python
import math

import jax
import jax.numpy as jnp
import numpy as np
from jax import lax
from jax.experimental import pallas as pl
from jax.experimental.pallas import tpu as pltpu
from jax.experimental.pallas import tpu_sc as plsc

F32 = jnp.float32
BF16 = jnp.bfloat16

D_MODEL = 1024
N_META = 16
RMS_EPS = 1e-6
HEADS = 16
PAIRS = HEADS // 2
HEAD_DIM = 64
LANES = 128
MLA_ROPE = 32
AHEAD = 6
AHEAD_BWD = 1
BIAS_PARTS = 3
MLA_SCALE = 1.0 / math.sqrt(64 + 32)
FOX_SCALE = 1.0 / math.sqrt(64)
LOG2E = math.log2(math.e)
LN2 = math.log(2.0)
ROPE_THETA = 10000.0

PAD = 256
BLK = 256
QB = 512
UNROLL = 4
NEG = -1e30

C_CQ, C_CKV, C_KPE, C_ZMLA, C_FQ, C_FK, C_FV, C_FL, C_ZFOX, C_GA, C_GB, C_END = (
    0, 256, 384, 416, 1440, 2464, 3488, 4512, 4528, 5552, 6576, 7600)
SMALL_W = 640
W_IN_SHARD = 1900

P2_ROWS = 928
N_CHIPS = 4

ADAM_LR = 0.001
ADAM_B1 = 0.9
ADAM_B2 = 0.999
ADAM_EPS = 1e-08
ADAM_WD = 0.01
ADAM_STEP = 10

VMEM_BIG = 56 * 1024 * 1024
MM_VMEM_BUDGET = 44 * 1024 * 1024
MESH = pl.DeviceIdType.MESH


def _cp(dims, vmem=None):
    return pltpu.CompilerParams(dimension_semantics=dims, vmem_limit_bytes=vmem)


def _dot(a, b, ca, cb):
    return lax.dot_general(a, b, (((ca,), (cb,)), ((), ())), preferred_element_type=F32)


def _sigmoid(x):
    return 1.0 / (1.0 + jnp.exp(-x))


def _tile(n, cands):
    for c in cands:
        if n % c == 0:
            return c
    return n


def _mm(a, b, *, mode, out_dtype, name, acc=None, epilogue=None, row_ins=(), after=None, col_scale=None):
    if mode == "nn":
        (M, K), N = a.shape, b.shape[1]
    elif mode == "nt":
        (M, K), N = a.shape, b.shape[0]
    else:
        (K, M), N = a.shape, b.shape[1]
    tm = _tile(M, (1088, 1024)) if M > 1024 else M
    tn = _tile(N, (1024,)) if N > 1024 else N
    nk = 1
    while True:
        tk = K // nk
        need = 2 * tk * (tm * a.dtype.itemsize + tn * b.dtype.itemsize) + tm * tn * (
            2 * jnp.dtype(out_dtype).itemsize + (8 if acc is not None else 0) + (4 if nk > 1 else 0))
        if need <= MM_VMEM_BUDGET or (tk // 2) % (16 if mode == "tn" else LANES) or tk <= 512:
            break
        nk *= 2
    while (M // tm) * (N // tn) * nk < 4 and tn % 512 == 0:
        tn //= 2
    assert col_scale is None or (nk == 1 and col_scale[0] % tn == 0)
    ca, cb = {"nn": (1, 0), "nt": (1, 1), "tn": (0, 0)}[mode]
    a_spec = (pl.BlockSpec((tk, tm), lambda j, i, k: (k, i)) if mode == "tn"
              else pl.BlockSpec((tm, tk), lambda j, i, k: (i, k)))
    b_spec = (pl.BlockSpec((tn, tk), lambda j, i, k: (j, k)) if mode == "nt"
              else pl.BlockSpec((tk, tn), lambda j, i, k: (k, j)))
    o_spec = pl.BlockSpec((tm, tn), lambda j, i, k: (i, j))
    has_acc = acc is not None

    nrow = len(row_ins)

    def body(*refs):
        a_ref, b_ref = refs[0], refs[1]
        acc_ref = refs[2] if has_acc else None
        rows = refs[2 + has_acc:2 + has_acc + nrow]
        o_ref = refs[2 + has_acc + nrow + (after is not None)]

        def store(tile):
            if epilogue is not None:
                tile = epilogue(tile, *[r[...] for r in rows])
            if col_scale is not None:
                tile = tile * jnp.where(pl.program_id(0) * tn < col_scale[0], col_scale[1], 1.0)
            o_ref[...] = tile.astype(out_dtype)

        part = _dot(a_ref[...].astype(BF16), b_ref[...].astype(BF16), ca, cb)
        if nk == 1:
            store(part + acc_ref[...] if has_acc else part)
        else:
            sc = refs[-1]
            k = pl.program_id(2)

            @pl.when(k == 0)
            def _():
                sc[...] = part + acc_ref[...] if has_acc else part

            @pl.when(k > 0)
            def _():
                sc[...] += part

            @pl.when(k == nk - 1)
            def _():
                store(sc[...])

    ins = [a, b] + ([acc] if has_acc else []) + list(row_ins)
    in_specs = ([a_spec, b_spec] + ([o_spec] if has_acc else [])
                + [pl.BlockSpec((tm, r.shape[1]), lambda j, i, k: (i, 0)) for r in row_ins])
    if after is not None:
        ins.append(after)
        in_specs.append(pl.BlockSpec(after.shape, lambda j, i, k: (0,) * after.ndim))
    return pl.pallas_call(
        body, name=name, grid=(N // tn, M // tm, nk), in_specs=in_specs, out_specs=o_spec,
        out_shape=jax.ShapeDtypeStruct((M, N), out_dtype),
        scratch_shapes=[pltpu.VMEM((tm, tn), F32)] if nk > 1 else [],
        compiler_params=_cp(("parallel", "parallel", "arbitrary"), VMEM_BIG))(*ins)


def _mm_sum_nt(pairs, *, name, after=None):
    n = len(pairs)
    M, N = pairs[0][0].shape[0], pairs[0][1].shape[0]
    tm = _tile(M, (272,))

    def body(*refs):
        o_ref = refs[2 * n + (after is not None)]
        tot = _dot(refs[0][...].astype(BF16), refs[n][...].astype(BF16), 1, 1)
        for i in range(1, n):
            tot = tot + _dot(refs[i][...].astype(BF16), refs[n + i][...].astype(BF16), 1, 1)
        o_ref[...] = tot

    ins = [a for a, _ in pairs] + [b for _, b in pairs]
    in_specs = ([pl.BlockSpec((tm, a.shape[1]), lambda i: (i, 0)) for a, _ in pairs]
                + [pl.BlockSpec(b.shape, lambda i: (0, 0)) for _, b in pairs])
    if after is not None:
        ins.append(after)
        in_specs.append(pl.BlockSpec(after.shape, lambda i: (0,) * after.ndim))
    return pl.pallas_call(
        body, name=name, grid=(M // tm,), in_specs=in_specs, out_specs=pl.BlockSpec((tm, N), lambda i: (i, 0)),
        out_shape=jax.ShapeDtypeStruct((M, N), F32), compiler_params=_cp(("parallel",), VMEM_BIG))(*ins)


def _row(w):
    return pl.BlockSpec((BLK, w), lambda i: (i, 0))


def _rowc(w, c):
    return pl.BlockSpec((BLK, w), lambda i: (i, c))


def _full(shape):
    return pl.BlockSpec(shape, lambda i: tuple(0 for _ in shape))


def _rope(x, c, s):
    lane = lax.broadcasted_iota(jnp.int32, x.shape, 1)
    is_x1 = ((lane >> 4) & 1) == 0
    partner = jnp.where(is_x1, pltpu.roll(x, LANES - 16, 1), pltpu.roll(x, 16, 1))
    return x * c + partner * s


def _row_valid(i):
    rows = i * BLK + lax.broadcasted_iota(jnp.int32, (BLK, 1), 0)
    return (rows < N_META) | (rows >= PAD)


def _shift_rows(w):
    return pl.BlockSpec((BLK, w), lambda i: (jnp.maximum(i - 1, 0), 0))


def _h_block(i, x_ref, meta_ref):
    head = jnp.concatenate([meta_ref[...], jnp.zeros((BLK - N_META, D_MODEL), F32)], axis=0)
    return jnp.where(i == 0, head, x_ref[...])


def _rms_pre(x2, meta, g):
    lp = PAD + x2.shape[0]

    def body(x_ref, meta_ref, g_ref, u_ref):
        hv = _h_block(pl.program_id(0), x_ref, meta_ref)
        r = lax.rsqrt(jnp.mean(hv * hv, axis=-1, keepdims=True) + RMS_EPS)
        u_ref[...] = (hv * r * g_ref[...]).astype(BF16)

    return pl.pallas_call(
        body, name="rms_pre", grid=(lp // BLK,),
        in_specs=[_shift_rows(D_MODEL), _full((N_META, D_MODEL)), _full((1, D_MODEL))], out_specs=_row(D_MODEL),
        out_shape=jax.ShapeDtypeStruct((lp, D_MODEL), BF16),
        compiler_params=_cp(("parallel",)))(x2, meta, g)


def _split3(x):
    hi = x.astype(BF16)
    r1 = x - hi.astype(F32)
    mid = r1.astype(BF16)
    lo = (r1 - mid.astype(F32)).astype(BF16)
    return hi, mid, lo


def _small_prep(small, gq, gkv, fb, ctab, stab, tri):
    lp = small.shape[0]

    def body(sm_ref, gq_ref, gkv_ref, fb_ref, c_ref, s_ref, tri_ref, qn_ref, kvn_ref, kr_ref, kb_ref, carry):
        i = pl.program_id(0)

        @pl.when(i == 0)
        def _():
            carry[...] = jnp.zeros_like(carry)

        cq = sm_ref[:, 0:256]
        r = lax.rsqrt(jnp.mean(cq * cq, axis=-1, keepdims=True) + RMS_EPS)
        qn_ref[...] = (cq * r * gq_ref[...]).astype(BF16)
        ckv = sm_ref[:, 256:384]
        r = lax.rsqrt(jnp.mean(ckv * ckv, axis=-1, keepdims=True) + RMS_EPS)
        kvn_ref[...] = (ckv * r * gkv_ref[...]).astype(BF16)
        kr_ref[...] = _rope(sm_ref[:, 384:512], c_ref[...], s_ref[...]).astype(BF16)
        fl = sm_ref[:, 512:640] + fb_ref[...]
        lf = jnp.minimum(fl, 0.0) - jnp.log(1.0 + jnp.exp(-jnp.abs(fl)))
        lf = jnp.where(_row_valid(i), lf, 0.0)
        hi, mid, lo = _split3(lf)
        t = tri_ref[...]
        cum = (_dot(t, hi, 1, 0) + _dot(t, mid, 1, 0)) + _dot(t, lo, 1, 0) + carry[...]
        carry[...] = cum[BLK - 1:BLK, :]
        src = lax.broadcasted_iota(jnp.int32, (LANES, LANES), 0)
        dst = lax.broadcasted_iota(jnp.int32, (LANES, LANES), 1)
        kb = jnp.zeros((BLK, LANES), F32)
        for j, part in enumerate(_split3(-cum * LOG2E)):
            spread = ((dst == BIAS_PARTS * src + j) & (src < HEADS)).astype(BF16)
            kb = kb + _dot(part, spread, 1, 0)
        kb_ref[...] = kb.astype(BF16)

    return pl.pallas_call(
        body, name="small_prep", grid=(lp // BLK,),
        in_specs=[_row(SMALL_W), _full((1, 256)), _full((1, 128)), _full((1, 128)), _row(128), _row(128),
                  _full((BLK, BLK))],
        out_specs=[_row(256), _row(128), _row(128), _row(128)],
        out_shape=[jax.ShapeDtypeStruct((lp, 256), BF16), jax.ShapeDtypeStruct((lp, 128), BF16),
                   jax.ShapeDtypeStruct((lp, 128), BF16), jax.ShapeDtypeStruct((lp, 128), BF16)],
        scratch_shapes=[pltpu.VMEM((1, 128), F32)],
        compiler_params=_cp(("arbitrary",)))(small, gq, gkv, fb, ctab, stab, tri)


def _rope_pairs(tile, c, s):
    out = []
    for lo in range(0, tile.shape[1], 256):
        out += [tile[:, lo:lo + 128], _rope(tile[:, lo + 128:lo + 256], c, s)]
    return jnp.concatenate(out, axis=1)


def _gate_fwd(o_mla, o_fox, gate):
    lp = o_mla.shape[0]

    def body(om_ref, of_ref, zm_ref, zf_ref, am_ref, af_ref):
        zm = zm_ref[...].astype(F32)
        am_ref[...] = (om_ref[...] * (zm * _sigmoid(zm))).astype(BF16)
        zf = zf_ref[...].astype(F32)
        af_ref[...] = (of_ref[...] * (zf * _sigmoid(zf))).astype(BF16)

    return pl.pallas_call(
        body, name="gate_fwd", grid=(lp // BLK,),
        in_specs=[_row(D_MODEL), _row(D_MODEL), _rowc(D_MODEL, 0), _rowc(D_MODEL, 1)],
        out_specs=[_row(D_MODEL), _row(D_MODEL)],
        out_shape=[jax.ShapeDtypeStruct((lp, D_MODEL), BF16)] * 2,
        compiler_params=_cp(("parallel",)))(o_mla, o_fox, gate, gate)


def _merge_fwd(gate, y_mla, y_fox):
    lp = y_mla.shape[0]

    def body(ga_ref, gb_ref, ym_ref, yf_ref, m_ref):
        sa = _sigmoid(ga_ref[...].astype(F32))
        sb = _sigmoid(gb_ref[...].astype(F32))
        m_ref[...] = (sa * ym_ref[...] + sb * yf_ref[...]).astype(BF16)

    return pl.pallas_call(
        body, name="merge_fwd", grid=(lp // BLK,),
        in_specs=[_rowc(D_MODEL, 2), _rowc(D_MODEL, 3), _row(D_MODEL), _row(D_MODEL)],
        out_specs=_row(D_MODEL), out_shape=jax.ShapeDtypeStruct((lp, D_MODEL), BF16),
        compiler_params=_cp(("parallel",)))(gate, gate, y_mla, y_fox)


def _tail(x2, mixed, tgt, gpost):
    lp = mixed.shape[0]
    shift = _shift_rows(D_MODEL)

    def body(h_ref, mx_ref, t_ref, g_ref, dmx_ref, dy_ref, loss_ref, dg_ref):
        i = pl.program_id(0)

        @pl.when(i == 0)
        def _():
            loss_ref[...] = jnp.zeros_like(loss_ref)
            dg_ref[...] = jnp.zeros_like(dg_ref)
            dmx_ref[...] = jnp.zeros_like(dmx_ref)
            dy_ref[...] = jnp.zeros_like(dy_ref)

        @pl.when(i > 0)
        def _():
            mx = mx_ref[...]
            g = g_ref[...]
            r = lax.rsqrt(jnp.mean(mx * mx, axis=-1, keepdims=True) + RMS_EPS)
            nrm = mx * r
            e = (h_ref[...] + nrm * g) - t_ref[...]
            loss_ref[...] += jnp.sum(0.5 * jnp.sum(e * e, axis=-1, keepdims=True) * (1.0 / D_MODEL),
                                     axis=0, keepdims=True)
            dy = e * (1.0 / D_MODEL)
            dy_ref[...] = dy
            dg_ref[...] += jnp.sum(dy * nrm, axis=0, keepdims=True)
            w = dy * g
            dot = jnp.mean(w * mx, axis=-1, keepdims=True)
            dmx_ref[...] = (r * w - mx * (r * r * r * dot)).astype(BF16)

    return pl.pallas_call(
        body, name="tail", grid=(lp // BLK,),
        in_specs=[shift, _row(D_MODEL), shift, _full((1, D_MODEL))],
        out_specs=[_row(D_MODEL), _row(D_MODEL), _full((1, 1)), _full((1, D_MODEL))],
        out_shape=[jax.ShapeDtypeStruct((lp, D_MODEL), BF16), jax.ShapeDtypeStruct((lp, D_MODEL), F32),
                   jax.ShapeDtypeStruct((1, 1), F32), jax.ShapeDtypeStruct((1, D_MODEL), F32)],
        compiler_params=_cp(("arbitrary",)))(x2, mixed, tgt, gpost)


def _merge_bwd(dm, gate, y_mla, y_fox):
    lp = dm.shape[0]

    def body(dm_ref, ga_ref, gb_ref, ym_ref, yf_ref, dym_ref, dyf_ref, dg_ref):
        dm_v = dm_ref[...].astype(F32)
        sa = _sigmoid(ga_ref[...].astype(F32))
        sb = _sigmoid(gb_ref[...].astype(F32))
        dym_ref[...] = (dm_v * sa).astype(BF16)
        dyf_ref[...] = (dm_v * sb).astype(BF16)
        dg_ref[:, 0:D_MODEL] = (dm_v * ym_ref[...] * (sa * (1.0 - sa))).astype(BF16)
        dg_ref[:, D_MODEL:2 * D_MODEL] = (dm_v * yf_ref[...] * (sb * (1.0 - sb))).astype(BF16)

    return pl.pallas_call(
        body, name="merge_bwd", grid=(lp // BLK,),
        in_specs=[_row(D_MODEL), _rowc(D_MODEL, 2), _rowc(D_MODEL, 3), _row(D_MODEL), _row(D_MODEL)],
        out_specs=[_row(D_MODEL), _row(D_MODEL), _row(2 * D_MODEL)],
        out_shape=[jax.ShapeDtypeStruct((lp, D_MODEL), BF16), jax.ShapeDtypeStruct((lp, D_MODEL), BF16),
                   jax.ShapeDtypeStruct((lp, 2 * D_MODEL), BF16)],
        compiler_params=_cp(("parallel",)))(dm, gate, gate, y_mla, y_fox)


def _gate_bwd(da_mla, da_fox, o_mla, o_fox, gate):
    lp = da_mla.shape[0]

    def one(da, o, z, head_of_col):
        sg = _sigmoid(z)
        do = (da * (z * sg)).astype(BF16)
        dz = da * o * (sg * (1.0 + z * (1.0 - sg)))
        delta = sum(_dot(part, head_of_col, 1, 0) for part in _split3(do.astype(F32) * o))
        return do, dz.astype(BF16), delta

    def body(dam_ref, daf_ref, om_ref, of_ref, zm_ref, zf_ref, dom_ref, dof_ref, dz_ref, dlm_ref, dlf_ref):
        f32 = lambda r: r[...].astype(F32)
        head_of_col = (lax.broadcasted_iota(jnp.int32, (D_MODEL, LANES), 0) // HEAD_DIM
                       == lax.broadcasted_iota(jnp.int32, (D_MODEL, LANES), 1)).astype(BF16)
        dom_ref[...], dz_ref[:, 0:D_MODEL], dlm_ref[...] = one(f32(dam_ref), f32(om_ref), f32(zm_ref), head_of_col)
        dof_ref[...], dz_ref[:, D_MODEL:2 * D_MODEL], dlf_ref[...] = one(f32(daf_ref), f32(of_ref), f32(zf_ref),
                                                                        head_of_col)

    return pl.pallas_call(
        body, name="gate_bwd", grid=(lp // BLK,),
        in_specs=[_row(D_MODEL)] * 4 + [_rowc(D_MODEL, 0), _rowc(D_MODEL, 1)],
        out_specs=[_row(D_MODEL), _row(D_MODEL), _row(2 * D_MODEL), _row(LANES), _row(LANES)],
        out_shape=[jax.ShapeDtypeStruct((lp, D_MODEL), BF16), jax.ShapeDtypeStruct((lp, D_MODEL), BF16),
                   jax.ShapeDtypeStruct((lp, 2 * D_MODEL), BF16), jax.ShapeDtypeStruct((lp, LANES), F32),
                   jax.ShapeDtypeStruct((lp, LANES), F32)],
        compiler_params=_cp(("parallel",)))(da_mla, da_fox, o_mla, o_fox, gate, gate)


def _small_bwd(small, dqn, dkvn, dkr, dcol_t, drow_t, gq, gkv, fb, ctab, stab, triu):
    lp = small.shape[0]
    nb = lp // BLK

    def rrow(w):
        return pl.BlockSpec((BLK, w), lambda i: (nb - 1 - i, 0))

    def body(sm_ref, dqn_ref, dkvn_ref, dkr_ref, dcol_ref, drow_ref, gq_ref, gkv_ref, fb_ref, c_ref, s_ref, tri_ref,
             ds_ref, dgq_ref, dgkv_ref, dfb_ref, carry):
        i = pl.program_id(0)

        @pl.when(i == 0)
        def _():
            carry[...] = jnp.zeros_like(carry)
            dgq_ref[...] = jnp.zeros_like(dgq_ref)
            dgkv_ref[...] = jnp.zeros_like(dgkv_ref)
            dfb_ref[...] = jnp.zeros_like(dfb_ref)

        def norm_bwd(x, dn, g, dg_ref):
            r = lax.rsqrt(jnp.mean(x * x, axis=-1, keepdims=True) + RMS_EPS)
            dg_ref[...] += jnp.sum(dn * (x * r), axis=0, keepdims=True)
            w = dn * g
            dot = jnp.mean(w * x, axis=-1, keepdims=True)
            return r * w - x * (r * r * r * dot)

        ds_ref[:, 0:256] = norm_bwd(sm_ref[:, 0:256], dqn_ref[...], gq_ref[...], dgq_ref).astype(BF16)
        ds_ref[:, 256:384] = norm_bwd(sm_ref[:, 256:384], dkvn_ref[...], gkv_ref[...], dgkv_ref).astype(BF16)

        dk = dkr_ref[0]
        for p in range(1, PAIRS):
            dk = dk + dkr_ref[p]
        dk = _rope(dk, c_ref[...], -s_ref[...])
        lane = lax.broadcasted_iota(jnp.int32, dk.shape, 1)
        dk = jnp.where(lane < MLA_ROPE, dk + pltpu.roll(dk, LANES - MLA_ROPE, 1), 0.0)
        ds_ref[:, 384:512] = dk.astype(BF16)

        dcol = dcol_ref[0]
        for p in range(1, PAIRS):
            dcol = dcol + pltpu.roll(dcol_ref[p], 2 * p, 1)
        rows16 = jnp.concatenate([drow_ref[p, h:h + 1, :] for p in range(PAIRS) for h in range(2)], axis=0)
        eye = (lax.broadcasted_iota(jnp.int32, (HEADS, LANES), 0)
               == lax.broadcasted_iota(jnp.int32, (HEADS, LANES), 1)).astype(BF16)
        drow = sum(_dot(part, eye, 0, 0) for part in _split3(rows16))
        dcr = dcol - drow
        hi, mid, lo = _split3(dcr)
        t = tri_ref[...]
        suf = (_dot(t, hi, 1, 0) + _dot(t, mid, 1, 0)) + _dot(t, lo, 1, 0) + carry[...]
        fl = sm_ref[:, 512:640] + fb_ref[...]
        dfl = jnp.where(_row_valid(nb - 1 - i), -suf * _sigmoid(-fl), 0.0)
        ds_ref[:, 512:640] = dfl.astype(BF16)
        dfb_ref[...] += jnp.sum(dfl, axis=0, keepdims=True)
        carry[...] += jnp.sum(dcr, axis=0, keepdims=True)

    return pl.pallas_call(
        body, name="small_bwd", grid=(nb,),
        in_specs=[rrow(SMALL_W), rrow(256), rrow(128),
                  pl.BlockSpec((PAIRS, BLK, 128), lambda i: (0, nb - 1 - i, 0)),
                  pl.BlockSpec((PAIRS, BLK, 128), lambda i: (0, nb - 1 - i, 0)),
                  pl.BlockSpec((PAIRS, 2, BLK), lambda i: (0, 0, nb - 1 - i)),
                  _full((1, 256)), _full((1, 128)), _full((1, 128)), rrow(128), rrow(128), _full((BLK, BLK))],
        out_specs=[rrow(SMALL_W), _full((1, 256)), _full((1, 128)), _full((1, 128))],
        out_shape=[jax.ShapeDtypeStruct((lp, SMALL_W), BF16), jax.ShapeDtypeStruct((1, 256), F32),
                   jax.ShapeDtypeStruct((1, 128), F32), jax.ShapeDtypeStruct((1, 128), F32)],
        scratch_shapes=[pltpu.VMEM((1, 128), F32)],
        compiler_params=_cp(("arbitrary",)))(small, dqn, dkvn, dkr, dcol_t, drow_t, gq, gkv, fb, ctab, stab, triu)


def _pre_bwd(du, x2, meta, dy, gpre):
    s_rows = x2.shape[0]
    lp = PAD + s_rows
    shift = _shift_rows(D_MODEL)

    def body(du_ref, x_ref, meta_ref, dy_ref, g_ref, dx_ref, dmeta_ref, dg_ref):
        i = pl.program_id(0)

        @pl.when(i == 0)
        def _():
            dg_ref[...] = jnp.zeros_like(dg_ref)

        hv = _h_block(i, x_ref, meta_ref)
        duv = du_ref[...]
        r = lax.rsqrt(jnp.mean(hv * hv, axis=-1, keepdims=True) + RMS_EPS)
        dg_ref[...] += jnp.sum(duv * (hv * r), axis=0, keepdims=True)
        w = duv * g_ref[...]
        dot = jnp.mean(w * hv, axis=-1, keepdims=True)
        dh = dy_ref[...] + (r * w - hv * (r * r * r * dot))
        dx_ref[...] = dh

        @pl.when(i == 0)
        def _():
            dmeta_ref[...] = dh[0:N_META, :]

    return pl.pallas_call(
        body, name="pre_bwd", grid=(lp // BLK,),
        in_specs=[_row(D_MODEL), shift, _full((N_META, D_MODEL)), _row(D_MODEL), _full((1, D_MODEL))],
        out_specs=[shift, _full((N_META, D_MODEL)), _full((1, D_MODEL))],
        out_shape=[jax.ShapeDtypeStruct((s_rows, D_MODEL), F32), jax.ShapeDtypeStruct((N_META, D_MODEL), F32),
                   jax.ShapeDtypeStruct((1, D_MODEL), F32)],
        compiler_params=_cp(("arbitrary",)))(du, x2, meta, dy, gpre)


def _pair_masks(rope, pair):
    lane = lax.broadcasted_iota(jnp.int32, (1, LANES), 1)
    mas = [lane < HEAD_DIM, lane >= HEAD_DIM]
    wide = lax.broadcasted_iota(jnp.int32, (1, 2 * LANES), 1)
    extra = MLA_ROPE if rope else BIAS_PARTS
    lo = LANES if rope else LANES + 2 * BIAS_PARTS * pair
    mid = lo + extra
    return mas, [(wide < HEAD_DIM) | ((wide >= lo) & (wide < mid)),
                 ((wide >= HEAD_DIM) & (wide < LANES)) | ((wide >= mid) & (wide < mid + extra))]


def _mask2(x, masks):
    return [jnp.where(m, x, jnp.zeros_like(x)) for m in masks]


def _q_heads(q_rows, rope, mas, hmask):
    if rope:
        return _mask2(q_rows, hmask)
    zero = jnp.zeros((q_rows.shape[0], LANES), BF16)
    return [jnp.concatenate([jnp.where(m, q_rows, zero), jnp.where(hm[:, LANES:], zero + 1, zero)], axis=1)
            for m, hm in zip(mas, hmask)]


def _transposed_cols(x, group, name):
    lp = x.shape[0]

    def body(x_ref, o_ref):
        o_ref[...] = x_ref[...].T

    cols = 2 * BLK
    per_group = D_MODEL // cols
    return pl.pallas_call(
        body, name=name, grid=(per_group,),
        in_specs=[pl.BlockSpec((lp, cols), lambda i: (0, per_group * group + i))],
        out_specs=pl.BlockSpec((cols, lp), lambda i: (i, 0)),
        out_shape=jax.ShapeDtypeStruct((D_MODEL, lp), x.dtype),
        compiler_params=_cp(("parallel",), MM_VMEM_BUDGET))(x)


def _attn_fwd(q, k, v, vt, k2, *, rope, qcol, kcol, vcol, name):
    lp = q.shape[0]
    nq = 1 + (lp - PAD) // QB
    qw = 256 if rope else 128

    def body(q_ref, k_ref, v_ref, vt_ref, k2_ref, o_ref, lse_ref):
        i = pl.program_id(1)
        r0 = pl.multiple_of(jnp.where(i == 0, 0, PAD + QB * (i - 1)), BLK)
        b0 = r0 // BLK
        mas, hmask = _pair_masks(rope, pl.program_id(0))
        qh = _q_heads(q_ref[pl.ds(r0, QB), :], rope, mas, hmask)

        def update(chunks, tiles, groups):
            m = [[cr[0], cr[2]] for _, _, cr in groups]
            l = [[cr[1], cr[3]] for _, _, cr in groups]
            acc = [[cr[4][0:HEAD_DIM], cr[4][HEAD_DIM:LANES]] for _, _, cr in groups]
            qs = [[x[q_lo:q_lo + wq] for x in qh] for q_lo, wq, _ in groups]
            k0s = [pl.multiple_of(kc * BLK, BLK) for kc, _ in chunks]
            kks = [jnp.concatenate([k_ref[pl.ds(k0, n), :], k2_ref[pl.ds(k0, n), :]], axis=1)
                   for k0, (_, n) in zip(k0s, chunks)]
            jobs = [(g, ci, mask, h) for g, ci, mask in tiles for h in range(2)]
            score = lambda t: _dot(kks[jobs[t][1]], qs[jobs[t][0]][jobs[t][3]], 1, 1)
            ss = [score(t) for t in range(min(AHEAD, len(jobs)))]
            for t, (g, ci, mask, h) in enumerate(jobs):
                if t + AHEAD < len(jobs):
                    ss.append(score(t + AHEAD))
                s = ss[t] if mask is None else jnp.where(mask, ss[t], NEG)
                m_new = jnp.maximum(m[g][h], jnp.max(s, axis=0, keepdims=True))
                alpha = jnp.exp2(m[g][h] - m_new)
                p = jnp.exp2(s - m_new)
                l[g][h] = alpha * l[g][h] + jnp.sum(p, axis=0, keepdims=True)
                m[g][h] = m_new
                n = chunks[ci][1]
                if n == BLK:
                    pv = _dot(vt_ref[pl.ds(HEAD_DIM * h, HEAD_DIM), pl.ds(k0s[ci], BLK)], p.astype(BF16), 1, 0)
                else:
                    vm = jnp.where(mas[h], v_ref[0:n, :], jnp.zeros((), BF16))
                    pv = _dot(vm, p.astype(BF16), 0, 0)[HEAD_DIM * h:HEAD_DIM * (h + 1)]
                acc[g][h] = alpha * acc[g][h] + pv
            return [(m[g][0], l[g][0], m[g][1], l[g][1], jnp.concatenate(acc[g], axis=0)) for g in range(len(groups))]

        def full_chunks(kcs, carry):
            return update([(kc, BLK) for kc in kcs], [(0, ci, None) for ci in range(len(kcs))], [(0, QB, carry)])[0]

        neg = jnp.full((1, QB), NEG, F32)
        zero = jnp.zeros((1, QB), F32)
        c = (neg, zero, neg, zero, jnp.zeros((LANES, QB), F32))
        n_mid = jnp.maximum(b0 - 1, 0)
        c = lax.fori_loop(0, n_mid // 4, lambda t, cr: full_chunks([4 * t + u for u in (1, 2, 3, 4)], cr), c)
        c = lax.fori_loop(0, (n_mid % 4) // 2, lambda t, cr: full_chunks([n_mid - 1, n_mid], cr), c)
        key_l = lax.broadcasted_iota(jnp.int32, (BLK, BLK), 0)
        qry_l = lax.broadcasted_iota(jnp.int32, (BLK, BLK), 1)
        tri = (key_l <= qry_l) & (b0 > 0)
        meta_ok = (key_l[0:N_META] <= qry_l[0:N_META]) | (b0 > 0)
        lo, hi = update([(0, N_META), (b0, BLK), (b0 + 1, BLK)],
                        [(0, 0, meta_ok), (1, 0, None), (0, 1, tri), (1, 1, None), (1, 2, tri)],
                        [(0, BLK, tuple(a[:, 0:BLK] for a in c)), (BLK, QB - BLK, tuple(a[:, BLK:QB] for a in c))])
        c = tuple(jnp.concatenate([a, b], axis=1) for a, b in zip(lo, hi))
        inv =jnp.concatenate([jnp.broadcast_to(1.0 / c[1], (HEAD_DIM, QB)),
                               jnp.broadcast_to(1.0 / c[3], (HEAD_DIM, QB))], axis=0)
        o_t = (c[4] * inv).T.astype(BF16)
        lses = [c[2 * h] + jnp.log(c[2 * h + 1]) * LOG2E for h in range(2)]
        o_ref[pl.ds(r0, BLK), :] = o_t[0:BLK]
        for h in range(2):
            lse_ref[0, h:h + 1, pl.ds(r0, BLK)] = lses[h][:, 0:BLK]

        @pl.when(i > 0)
        def _():
            r1 = pl.multiple_of(r0 + BLK, BLK)
            o_ref[pl.ds(r1, QB - BLK), :] = o_t[BLK:QB]
            for h in range(2):
                lse_ref[0, h:h + 1, pl.ds(r1, QB - BLK)] = lses[h][:, BLK:QB]

    in_specs = [pl.BlockSpec((lp, qw), lambda p, i: (0, qcol + p)),
                pl.BlockSpec((lp, 128), lambda p, i: (0, kcol(p))),
                pl.BlockSpec((BLK, 128), lambda p, i: (0, vcol(p))),
                pl.BlockSpec((128, lp), lambda p, i: (p, 0)),
                pl.BlockSpec((lp, 128), lambda p, i: (0, 0))]
    return pl.pallas_call(
        body, name=name, grid=(PAIRS, nq), in_specs=in_specs,
        out_specs=[pl.BlockSpec((lp, 128), lambda p, i: (0, p)),
                   pl.BlockSpec((1, 2, lp), lambda p, i: (p, 0, 0))],
        out_shape=[jax.ShapeDtypeStruct((lp, D_MODEL), BF16), jax.ShapeDtypeStruct((PAIRS, 2, lp), F32)],
        compiler_params=_cp(("parallel", "arbitrary"), VMEM_BIG))(q, k, v, vt, k2)


def _attn_bwd(q, k, v, k2, do, delta, lse, *, rtabs=None, scale, qcol, kcol, vcol, name):
    lp = q.shape[0]
    nb = lp // BLK
    rope = rtabs is not None
    bias = not rope
    qw = 256 if rope else 128

    def body(*refs):
        it = iter(refs)
        q_ref, k_ref, v_ref, k2_ref = next(it), next(it), next(it), next(it)
        do_ref, dl_ref, lse_ref = next(it), next(it), next(it)
        ct_ref, st_ref = (next(it), next(it)) if rope else (None, None)
        dq_out, dk_ref, dv_ref = next(it), next(it), next(it)
        x_ref = next(it)
        drow_ref = next(it) if bias else None
        dq_ref = next(it)
        kb = pl.program_id(1)
        mas, hmask = _pair_masks(rope, pl.program_id(0))
        lane = lax.broadcasted_iota(jnp.int32, (1, LANES), 1)

        @pl.when(kb == 0)
        def _():
            dq_ref[...] = jnp.zeros_like(dq_ref)
            if bias:
                drow_ref[...] = jnp.zeros_like(drow_ref)

        def key_pass(n, w):
            kk = jnp.concatenate([k_ref[0:n, :], k2_ref[0:n, :]], axis=1)
            vh = _mask2(v_ref[0:n, :], mas)
            kcat = jnp.concatenate([x[:, 0:qw] for x in _mask2(kk, hmask)], axis=0)
            diag_mask = (lax.broadcasted_iota(jnp.int32, (n, w), 0) <= lax.broadcasted_iota(jnp.int32, (n, w), 1))

            def front(qc):
                q0 = qc * w if isinstance(qc, int) else pl.multiple_of(qc * w, w)
                dov = do_ref[pl.ds(q0, w), :]
                qh = _q_heads(q_ref[pl.ds(q0, w), :], rope, mas, hmask)
                return (q0, dov, qh, [_dot(kk, qh[h], 1, 1) for h in range(2)],
                        [_dot(vh[h], dov, 1, 1) for h in range(2)])

            def back(fronted, carry, mask):
                carry = list(carry)
                q0, dov, qh, ss, dps = fronted
                doh = _mask2(dov, mas)
                pbs, dss = [], []
                for h in range(2):
                    p = jnp.exp2(ss[h] - lse_ref[0, h:h + 1, pl.ds(q0, w)])
                    if mask is not None:
                        p = jnp.where(mask, p, 0.0)
                    ds = p * (dps[h] - dl_ref[0, h:h + 1, pl.ds(q0, w)])
                    if bias:
                        drow_ref[0, h:h + 1, pl.ds(q0, w)] += jnp.sum(ds, axis=0, keepdims=True)
                        carry[2 + h] = carry[2 + h] + jnp.sum(ds, axis=1, keepdims=True)
                    pbs.append(p.astype(BF16))
                    dss.append(ds.astype(BF16))
                ds_lanes = jnp.concatenate(dss, axis=1)
                ds_rows = jnp.concatenate(dss, axis=0)
                qcat = jnp.concatenate([x[:, 0:qw] for x in qh], axis=0)
                carry[0] = carry[0] + _dot(ds_lanes, qcat, 1, 0)
                carry[1] = carry[1] + _dot(jnp.concatenate(pbs, axis=1), jnp.concatenate(doh, axis=0), 1, 0)
                dq_ref[pl.ds(q0, w), :] += _dot(ds_rows, kcat, 0, 0)
                return tuple(carry)

            def chunks(qcs, carry, masks):
                fronted = [front(qc) for qc in qcs[:AHEAD_BWD]]
                for u, mask in enumerate(masks):
                    if u + AHEAD_BWD < len(qcs):
                        fronted.append(front(qcs[u + AHEAD_BWD]))
                    carry = back(fronted[u], carry, mask)
                return carry

            c = [jnp.zeros((n, qw), F32), jnp.zeros((n, LANES), F32)]
            if bias:
                c += [jnp.zeros((n, 1), F32), jnp.zeros((n, 1), F32)]
            c = tuple(c)
            if w != BLK:
                c = chunks(list(range(lp // w)), c, [diag_mask] + [None] * (lp // w - 1))
            else:
                groups = (nb - kb) // UNROLL

                def several(t, cr):
                    return chunks([kb + UNROLL * t + u for u in range(UNROLL)], cr,
                                  [diag_mask | (t > 0)] + [None] * (UNROLL - 1))

                c = lax.fori_loop(0, groups, several, c)
                start = kb + UNROLL * groups
                pairs = (nb - start) // 2

                def two(t, cr):
                    qc = start + 2 * t
                    return chunks([qc, qc + 1], cr, [diag_mask | (qc > kb), None])

                c = lax.fori_loop(0, pairs, two, c)
                c = lax.fori_loop(start + 2 * pairs, nb, lambda qc, cr: chunks([qc], cr, [diag_mask | (qc > kb)]), c)

            def rows(a, dtype):
                a = a.astype(dtype)
                return a if n == BLK else jnp.concatenate([a, jnp.zeros((BLK - n, a.shape[1]), dtype)], axis=0)

            dk = c[0] * LN2
            dk_ref[...] = rows(dk[:, 0:LANES], BF16)
            dv_ref[...] = rows(c[1], BF16)
            if rope:
                x_ref[0] = rows(dk[:, LANES:2 * LANES], F32)
            if bias:
                x_ref[0] = rows(jnp.where(lane == 0, c[2], jnp.where(lane == 1, c[3], 0.0)), F32)

        @pl.when(kb == 0)
        def _():
            key_pass(N_META, lp // 2)

        @pl.when(kb > 0)
        def _():
            key_pass(BLK, BLK)

        @pl.when(kb == nb - 1)
        def _():
            def fin(c, carry):
                r0 = pl.multiple_of(c * BLK, BLK)
                dq = dq_ref[pl.ds(r0, BLK), :] * scale
                if rope:
                    back = _rope(dq[:, LANES:2 * LANES], ct_ref[pl.ds(r0, BLK), :], -st_ref[pl.ds(r0, BLK), :])
                    dq = jnp.concatenate([dq[:, 0:LANES], back], axis=1)
                dq_out[pl.ds(r0, BLK), :] = dq.astype(BF16)
                return carry

            lax.fori_loop(0, nb, fin, 0)

    in_specs = [pl.BlockSpec((lp, qw), lambda p, j: (0, qcol + p)),
                pl.BlockSpec((BLK, 128), lambda p, j: (j, kcol(p))),
                pl.BlockSpec((BLK, 128), lambda p, j: (j, vcol(p))),
                pl.BlockSpec((BLK, 128), lambda p, j: (j, 0)),
                pl.BlockSpec((lp, 128), lambda p, j: (0, p)), pl.BlockSpec((1, 2, lp), lambda p, j: (p, 0, 0)),
                pl.BlockSpec((1, 2, lp), lambda p, j: (p, 0, 0))]
    ins = [q, k, v, k2, do, delta, lse]
    if rope:
        in_specs += [pl.BlockSpec((lp, 128), lambda p, j: (0, 0))] * 2
        ins += list(rtabs)
    out_specs = [pl.BlockSpec((lp, qw), lambda p, j: (0, p)),
                 pl.BlockSpec((BLK, 128), lambda p, j: (j, p)),
                 pl.BlockSpec((BLK, 128), lambda p, j: (j, p)),
                 pl.BlockSpec((1, BLK, 128), lambda p, j: (p, j, 0))]
    out_shape = [jax.ShapeDtypeStruct((lp, PAIRS * qw), BF16), jax.ShapeDtypeStruct((lp, D_MODEL), BF16),
                 jax.ShapeDtypeStruct((lp, D_MODEL), BF16), jax.ShapeDtypeStruct((PAIRS, lp, 128), F32)]
    if bias:
        out_specs.append(pl.BlockSpec((1, 2, lp), lambda p, j: (p, 0, 0)))
        out_shape.append(jax.ShapeDtypeStruct((PAIRS, 2, lp), F32))
    return pl.pallas_call(
        body, name=name, grid=(PAIRS, nb), in_specs=in_specs, out_specs=out_specs, out_shape=out_shape,
        scratch_shapes=[pltpu.VMEM((lp, qw), F32)],
        compiler_params=_cp(("parallel", "arbitrary"), VMEM_BIG))(*ins)


def _adamw(w, g, m, v, name):
    lead = w.ndim - 2
    rows, cols = w.shape[lead:]
    big = rows * cols > 512 * 1024
    tr = 128 if big and rows % 128 == 0 else rows
    tc = 256 if big and tr == rows else cols

    def body(w_ref, g_ref, m_ref, v_ref, d_ref, nm_ref, nv_ref):
        gv = g_ref[...]
        nm = ADAM_B1 * m_ref[...] + (1.0 - ADAM_B1) * gv
        nv = ADAM_B2 * v_ref[...] + (1.0 - ADAM_B2) * (gv * gv)
        m_hat = nm / (1.0 - ADAM_B1 ** ADAM_STEP)
        v_hat = nv / (1.0 - ADAM_B2 ** ADAM_STEP)
        d_ref[...] = -ADAM_LR * (m_hat / (jnp.sqrt(v_hat) + ADAM_EPS) + ADAM_WD * w_ref[...])
        nm_ref[...] = nm
        nv_ref[...] = nv

    spec = pl.BlockSpec((1,) * lead + (tr, tc), lambda i, j: (0,) * lead + (i, j))
    return pl.pallas_call(
        body, name=name, grid=(rows // tr, cols // tc), in_specs=[spec] * 4, out_specs=[spec] * 3,
        out_shape=[jax.ShapeDtypeStruct(w.shape, F32)] * 3,
        compiler_params=_cp(("parallel", "parallel"), VMEM_BIG))(w, g, m, v)


def _add_cores(g, from_sib, name):
    n, rows, cols = g.shape
    half = rows // 2
    tr = _tile(half, (256, 240))
    nt = half // tr

    def body(lo_ref, hi_ref, s_ref, o_ref):
        mine = jnp.where(lax.axis_index("c") == 0, lo_ref[0], hi_ref[0])
        o_ref[0] = (mine.astype(F32) + s_ref[0].astype(F32)).astype(BF16)

    return pl.pallas_call(
        body, name=name, grid=(n, nt),
        in_specs=[pl.BlockSpec((1, tr, cols), lambda j, i: (j, i, 0)),
                  pl.BlockSpec((1, tr, cols), lambda j, i: (j, nt + i, 0)),
                  pl.BlockSpec((1, tr, cols), lambda j, i: (j, i, 0))],
        out_specs=pl.BlockSpec((1, tr, cols), lambda j, i: (j, i, 0)),
        out_shape=jax.ShapeDtypeStruct((n, half, cols), BF16),
        compiler_params=_cp(("parallel", "parallel"), VMEM_BIG))(g, g, from_sib)


def _add_chips(x, own, name):
    n, rows, cols = x.shape
    tr = _tile(rows, (256, 240))

    def body(x_ref, own_ref, o_ref):
        me = 2 * lax.axis_index("x") + lax.axis_index("y")
        v = [jnp.where(me == k, own_ref[...], x_ref[k]).astype(F32) for k in range(N_CHIPS)]
        o_ref[...] = ((v[0] + v[1]) + v[2]) + v[3]

    return pl.pallas_call(
        body, name=name, grid=(rows // tr,),
        in_specs=[pl.BlockSpec((n, tr, cols), lambda i: (0, i, 0)), pl.BlockSpec((tr, cols), lambda i: (i, 0))],
        out_specs=pl.BlockSpec((tr, cols), lambda i: (i, 0)),
        out_shape=jax.ShapeDtypeStruct((rows, cols), F32), compiler_params=_cp(("parallel",), VMEM_BIG))(x, own)


def _axes():
    return lax.axis_index("x"), lax.axis_index("y"), lax.axis_index("c")


def _other_chips(x, y):
    return [(1 - x, y), (x, 1 - y), (1 - x, 1 - y)]


ANY = pl.BlockSpec(memory_space=pl.ANY)


def _rcopy(src, dst, send_sems, recv_sems, k, to):
    return pltpu.make_async_remote_copy(src_ref=src, dst_ref=dst, send_sem=send_sems.at[k], recv_sem=recv_sems.at[k],
                                        device_id=to, device_id_type=MESH)


def _gather_weights(shards, meta):
    n = len(shards)

    def body(*refs):
        srcs, meta_ref = refs[:n], refs[n]
        outs, mout_ref = refs[n + 1:2 * n + 1], refs[2 * n + 1]
        send_sems, recv_sems = refs[2 * n + 2:]
        x, y, c = _axes()
        me = 2 * x + y
        sib = (x, y, 1 - c)
        chips = _other_chips(x, y)

        def half(t, chip_idx, cc):
            hr = shards[t].shape[0] // 2
            return outs[t].at[chip_idx, pl.ds(cc * hr, hr), :]

        first = []
        for j, (px, py) in enumerate(chips):
            for t in range(n):
                hr = shards[t].shape[0] // 2
                first.append(_rcopy(srcs[t].at[pl.ds(c * hr, hr), :], half(t, me, c), send_sems, recv_sems,
                                    3 * t + j, (px, py, c)))
            first.append(_rcopy(meta_ref, mout_ref.at[me], send_sems, recv_sems, 3 * n + j, (px, py, c)))
        for cp in first:
            cp.start()
        passed = []
        for j, (px, py) in enumerate(chips):
            src_chip = 2 * px + py
            for t in range(n):
                _rcopy(half(t, src_chip, c), half(t, src_chip, c), send_sems, recv_sems, 3 * t + j, sib).wait_recv()
                fwd = _rcopy(half(t, src_chip, c), half(t, src_chip, c), send_sems, recv_sems, 3 * (n + 1 + t) + j, sib)
                fwd.start()
                passed.append(fwd)
            _rcopy(mout_ref.at[src_chip], mout_ref.at[src_chip], send_sems, recv_sems, 3 * n + j, sib).wait_recv()
        for j, (px, py) in enumerate(chips):
            src_chip = 2 * px + py
            for t in range(n):
                _rcopy(half(t, src_chip, 1 - c), half(t, src_chip, 1 - c), send_sems, recv_sems,
                       3 * (n + 1 + t) + j, sib).wait_recv()
        for cp in first + passed:
            cp.wait_send()

    nsem = 3 * (2 * n + 1)
    return pl.pallas_call(
        body, name="gather_weights", in_specs=[ANY] * (n + 1), out_specs=[ANY] * (n + 1),
        out_shape=[jax.ShapeDtypeStruct((N_CHIPS,) + s.shape, s.dtype) for s in shards]
        + [jax.ShapeDtypeStruct((N_CHIPS,) + meta.shape, meta.dtype)],
        scratch_shapes=[pltpu.SemaphoreType.DMA((nsem,)), pltpu.SemaphoreType.DMA((nsem,))])(*shards, meta)


def _gather_late(shard):
    rows, cols = shard.shape
    hr = rows // 2
    src = jax.new_ref(shard, memory_space=pltpu.MemorySpace.HBM)
    out = jax.empty_ref(jax.ShapeDtypeStruct((N_CHIPS, rows, cols), shard.dtype), memory_space=pltpu.MemorySpace.HBM)

    @pl.kernel(mesh=plsc.ScalarSubcoreMesh(axis_name="seq", num_cores=1), name="gather_late",
               scratch_types=(pltpu.SemaphoreType.DMA((6,)), pltpu.SemaphoreType.DMA((6,))),
               compiler_params=pltpu.CompilerParams(collective_id=1))
    def launch(send_sems, recv_sems):
        x, y, c = _axes()
        me = 2 * x + y
        sib = (x, y, 1 - c)
        chips = _other_chips(x, y)
        barrier = pltpu.get_barrier_semaphore()
        for px, py in chips:
            pl.semaphore_signal(barrier, inc=1, device_id=(px, py, c), device_id_type=MESH)
        pl.semaphore_signal(barrier, inc=1, device_id=sib, device_id_type=MESH)
        pl.semaphore_wait(barrier, 4)

        def half(chip_idx, cc):
            return out.at[chip_idx, pl.ds(cc * hr, hr), :]

        first = [_rcopy(src.at[pl.ds(c * hr, hr), :], half(me, c), send_sems, recv_sems, j, (px, py, c))
                 for j, (px, py) in enumerate(chips)]
        for cp in first:
            cp.start()
        passed = []
        for j, (px, py) in enumerate(chips):
            land = half(2 * px + py, c)
            _rcopy(land, land, send_sems, recv_sems, j, sib).wait_recv()
            fwd = _rcopy(land, land, send_sems, recv_sems, 3 + j, sib)
            fwd.start()
            passed.append(fwd)
        for j, (px, py) in enumerate(chips):
            land = half(2 * px + py, 1 - c)
            _rcopy(land, land, send_sems, recv_sems, 3 + j, sib).wait_recv()
        for cp in first + passed:
            cp.wait_send()

    launch()
    return out[...]


def _swap_halves(gs):
    n = len(gs)
    ncopies = sum(g.shape[0] for g in gs)

    def body(*refs):
        srcs, outs = refs[:n], refs[n:2 * n]
        send_sems, recv_sems = refs[2 * n:]
        x, y, c = _axes()
        cps = []
        for t in range(n):
            hr = gs[t].shape[1] // 2
            for j in range(gs[t].shape[0]):
                cps.append(_rcopy(srcs[t].at[j, pl.ds((1 - c) * hr, hr), :], outs[t].at[j], send_sems, recv_sems,
                                  len(cps), (x, y, 1 - c)))
        for cp in cps:
            cp.start()
        for cp in cps:
            cp.wait()

    return pl.pallas_call(
        body, name="swap_halves", in_specs=[ANY] * n, out_specs=[ANY] * n,
        out_shape=[jax.ShapeDtypeStruct((g.shape[0], g.shape[1] // 2, g.shape[2]), g.dtype) for g in gs],
        scratch_shapes=[pltpu.SemaphoreType.DMA((ncopies,)), pltpu.SemaphoreType.DMA((ncopies,))])(*gs)


def _scatter_chips(parts):
    n = len(parts)
    srcs = [jax.new_ref(p, memory_space=pltpu.MemorySpace.HBM) for p in parts]
    outs = [jax.empty_ref(jax.ShapeDtypeStruct(p.shape, p.dtype), memory_space=pltpu.MemorySpace.HBM) for p in parts]

    @pl.kernel(mesh=plsc.ScalarSubcoreMesh(axis_name="seq", num_cores=1), name="scatter_chips",
               scratch_types=(pltpu.SemaphoreType.DMA((3 * n,)), pltpu.SemaphoreType.DMA((3 * n,))),
               compiler_params=pltpu.CompilerParams(collective_id=0))
    def launch(send_sems, recv_sems):
        x, y, c = _axes()
        me = 2 * x + y
        chips = _other_chips(x, y)
        barrier = pltpu.get_barrier_semaphore()
        for px, py in chips:
            pl.semaphore_signal(barrier, inc=1, device_id=(px, py, c), device_id_type=MESH)
        pl.semaphore_wait(barrier, 3)
        cps = []
        for j, (px, py) in enumerate(chips):
            for t in range(n):
                cps.append(_rcopy(srcs[t].at[2 * px + py], outs[t].at[me], send_sems, recv_sems, 3 * t + j,
                                  (px, py, c)))
        for cp in cps:
            cp.start()
        for cp in cps:
            cp.wait()

    launch()
    return [o[...] for o in outs]


def _swap_reduced(rs):
    n = len(rs)

    def body(*refs):
        srcs, outs = refs[:n], refs[n:2 * n]
        send_sems, recv_sems = refs[2 * n:]
        x, y, c = _axes()
        cps = [_rcopy(srcs[t], outs[t], send_sems, recv_sems, t, (x, y, 1 - c)) for t in range(n)]
        for cp in cps:
            cp.start()
        for cp in cps:
            cp.wait()

    return pl.pallas_call(
        body, name="swap_reduced", in_specs=[ANY] * n, out_specs=[ANY] * n,
        out_shape=[jax.ShapeDtypeStruct(r.shape, r.dtype) for r in rs],
        scratch_shapes=[pltpu.SemaphoreType.DMA((n,)), pltpu.SemaphoreType.DMA((n,))])(*rs)


SMALL_ROWS = 24 + 128


def _allreduce_small(vec):
    def body(v_ref, out_ref, slots, send_sems, recv_sems):
        x, y, c = _axes()
        me = 4 * x + 2 * y + c
        slots[me] = v_ref[...]
        cps = []
        for k in range(1, 8):
            kx, ky, kc = (k >> 2) & 1, (k >> 1) & 1, k & 1
            peer = (1 - x if kx else x, 1 - y if ky else y, 1 - c if kc else c)
            cps.append(_rcopy(v_ref, slots.at[me], send_sems, recv_sems, k - 1, peer))
        for cp in cps:
            cp.start()
        for cp in cps:
            cp.wait()
        tot = slots[0]
        for k in range(1, 8):
            tot = tot + slots[k]
        out_ref[...] = tot

    return pl.pallas_call(
        body, name="allreduce_small",
        in_specs=[pl.BlockSpec(memory_space=pltpu.VMEM)], out_specs=pl.BlockSpec(memory_space=pltpu.VMEM),
        out_shape=jax.ShapeDtypeStruct((SMALL_ROWS, 128), F32),
        scratch_shapes=[pltpu.VMEM((8, SMALL_ROWS, 128), F32), pltpu.SemaphoreType.DMA((7,)),
                        pltpu.SemaphoreType.DMA((7,))])(vec)


def _pack_p2(w_uq, w_ukv, w_br_mla, w_br_fox, w_out, dtype):
    parts = [w_uq.reshape(96, D_MODEL), w_ukv.reshape(64, D_MODEL), w_br_mla, w_br_fox, w_out]
    return jnp.concatenate([p.astype(dtype) for p in parts], axis=0)


def _unpack_p2(pk):
    return pk[0:96].reshape(256, 384), pk[96:160].reshape(128, 512), pk[160:416], pk[416:672], pk[672:928]


def _uq_arrange(w):
    w3 = w.reshape(256, HEADS, 96)
    nope = w3[:, :, :64].reshape(256, PAIRS, 128)
    pe = w3[:, :, 64:].reshape(256, PAIRS, 64)
    return jnp.concatenate([nope, pe, jnp.zeros((256, PAIRS, 64), w.dtype)], axis=2).reshape(256, PAIRS * 256)


def _uq_restore(g):
    g3 = g.reshape(256, PAIRS, 256)
    nope = g3[:, :, :128].reshape(256, HEADS, 64)
    pe = g3[:, :, 128:192].reshape(256, HEADS, 32)
    return jnp.concatenate([nope, pe], axis=2).reshape(256, HEADS * 96)


def _ukv_arrange(w):
    w3 = w.reshape(128, HEADS, 128)
    return jnp.concatenate([w3[:, :, :64].reshape(128, 1024), w3[:, :, 64:].reshape(128, 1024)], axis=1)


def _ukv_restore(g):
    kn = g[:, :1024].reshape(128, HEADS, 64)
    vv = g[:, 1024:].reshape(128, HEADS, 64)
    return jnp.concatenate([kn, vv], axis=2).reshape(128, HEADS * 128)


def _rope_tables(lp):
    r = np.arange(lp)
    pos = np.where(r < N_META, r, np.where(r >= PAD, r - PAD + N_META, 0)).astype(np.float32)
    half = MLA_ROPE // 2
    inv_freq = np.float32(ROPE_THETA) ** (-np.arange(half, dtype=np.float32) / np.float32(half))
    ang = (pos[:, None] * inv_freq[None, :]).astype(np.float32)
    cos, sin = np.cos(ang).astype(np.float32), np.sin(ang).astype(np.float32)
    one, zero = np.ones((lp, 64), np.float32), np.zeros((lp, 64), np.float32)
    return (jnp.asarray(np.concatenate([cos, cos, cos, cos, one], axis=1)),
            jnp.asarray(np.concatenate([-sin, sin, -sin, sin, zero], axis=1)))


def _pad_lanes(v, n=128):
    return jnp.pad(v, ((0, 0), (0, n - v.shape[1])))


def _in_cols(slabs, a, b):
    out = []
    for j in range(N_CHIPS):
        lo, hi = max(a, W_IN_SHARD * j), min(b, W_IN_SHARD * (j + 1))
        if lo < hi:
            out.append(slabs[j][:, lo - W_IN_SHARD * j:hi - W_IN_SHARD * j])
    return out


def _local_step(x2, tgt2, meta_f, w_small, w_attn, w_gate, w_uq_f, w_ukv_f, w_bm, w_bf, w_o, pre_norm_g,
                post_norm_g, mla_q_norm_g, mla_kv_norm_g, fox_forget_b, start_exchange=None):
    s_rows = x2.shape[0]
    lp = PAD + s_rows
    w_uq_a = _uq_arrange(w_uq_f)
    w_ukv_a = _ukv_arrange(w_ukv_f)

    ctab, stab = _rope_tables(lp)
    ii = jnp.arange(BLK)
    tri_lo = (ii[:, None] >= ii[None, :]).astype(BF16)
    tri_up = (ii[:, None] <= ii[None, :]).astype(BF16)
    fb128 = _pad_lanes(fox_forget_b)

    u = _rms_pre(x2, meta_f, pre_norm_g)
    small = _mm(u, w_small, mode="nn", out_dtype=F32, name="proj_small")
    attn = _mm(u, w_attn, mode="nn", out_dtype=BF16, name="proj_attn",
               col_scale=(HEADS * HEAD_DIM, FOX_SCALE * LOG2E))
    gate = _mm(u, w_gate, mode="nn", out_dtype=BF16, name="proj_gate")
    qn, kvn, kr, kb = _small_prep(small, mla_q_norm_g, mla_kv_norm_g, fb128, ctab, stab, tri_lo)
    qcat = _mm(qn, w_uq_a, mode="nn", out_dtype=BF16, name="mla_q", row_ins=(ctab, stab),
               epilogue=lambda tile, c, s: _rope_pairs(tile, c, s) * (MLA_SCALE * LOG2E))
    kv = _mm(kvn, w_ukv_a, mode="nn", out_dtype=BF16, name="mla_kv")

    mla_cols = dict(qcol=0, kcol=lambda p: p, vcol=lambda p: PAIRS + p)
    fox_cols = dict(qcol=0, kcol=lambda p: PAIRS + p, vcol=lambda p: 2 * PAIRS + p)
    o_mla, lse_mla = _attn_fwd(qcat, kv, kv, _transposed_cols(kv, 1, "mla_vt"), kr, rope=True, name="mla_fwd",
                               **mla_cols)
    o_fox, lse_fox = _attn_fwd(attn, attn, attn, _transposed_cols(attn, 2, "fox_vt"), kb, rope=False,
                               name="fox_fwd", **fox_cols)

    a_mla, a_fox = _gate_fwd(o_mla, o_fox, gate)
    y_mla = _mm(a_mla, w_bm, mode="nn", out_dtype=BF16, name="br_mla")
    y_fox = _mm(a_fox, w_bf, mode="nn", out_dtype=BF16, name="br_fox")
    mg = _merge_fwd(gate, y_mla, y_fox)
    mixed = _mm(mg, w_o, mode="nn", out_dtype=F32, name="out_proj")
    dmixed, dy, loss_p, dg_post = _tail(x2, mixed, tgt2, post_norm_g)

    d_w_out = _mm(mg, dmixed, mode="tn", out_dtype=F32, name="d_w_out")
    dm = _mm(dmixed, w_o, mode="nt", out_dtype=BF16, name="d_merge")
    dy_mla, dy_fox, dgate_ab = _merge_bwd(dm, gate, y_mla, y_fox)
    d_w_bm = _mm(a_mla, dy_mla, mode="tn", out_dtype=F32, name="d_w_br_mla")
    d_w_bf = _mm(a_fox, dy_fox, mode="tn", out_dtype=F32, name="d_w_br_fox")
    da_mla = _mm(dy_mla, w_bm, mode="nt", out_dtype=BF16, name="d_a_mla")
    da_fox = _mm(dy_fox, w_bf, mode="nt", out_dtype=BF16, name="d_a_fox")
    do_mla, do_fox, dgate_z, dl_mla, dl_fox = _gate_bwd(da_mla, da_fox, o_mla, o_fox, gate)
    dl_mla, dl_fox = (d[:, :HEADS].T.reshape(PAIRS, 2, lp) for d in (dl_mla, dl_fox))

    dq_a, dkn, dvm, dkr = _attn_bwd(qcat, kv, kv, kr, do_mla, dl_mla, lse_mla, rtabs=(ctab, stab),
                                    scale=MLA_SCALE, name="mla_bwd", **mla_cols)
    dfq, dfk, dfv, dcol, drow = _attn_bwd(attn, attn, attn, kb, do_fox, dl_fox, lse_fox, scale=FOX_SCALE,
                                          name="fox_bwd", **fox_cols)

    d_w_uq_a = _mm(qn, dq_a, mode="tn", out_dtype=F32, name="d_w_uq")
    dqn = _mm(dq_a, w_uq_a, mode="nt", out_dtype=F32, name="d_qn")
    d_w_ukv_a = jnp.concatenate([_mm(kvn, dkn, mode="tn", out_dtype=F32, name="d_w_uk"),
                                 _mm(kvn, dvm, mode="tn", out_dtype=F32, name="d_w_uv")], axis=1)
    dkvn = _mm(dkn, w_ukv_a[:, :1024], mode="nt", out_dtype=F32, name="d_kvn_k")
    dkvn = _mm(dvm, w_ukv_a[:, 1024:], mode="nt", out_dtype=F32, name="d_kvn_v", acc=dkvn)
    dsmall, dg_q, dg_kv, dfb = _small_bwd(small, dqn, dkvn, dkr, dcol, drow, mla_q_norm_g, mla_kv_norm_g,
                                          fb128, ctab, stab, tri_up)

    dw_small = _mm(u, dsmall, mode="tn", out_dtype=BF16, name="d_w_small")
    dw_fq = _mm(u, dfq, mode="tn", out_dtype=BF16, name="d_w_fq")
    dw_fk = _mm(u, dfk, mode="tn", out_dtype=BF16, name="d_w_fk")
    dw_fv = _mm(u, dfv, mode="tn", out_dtype=BF16, name="d_w_fv")
    dw_z = _mm(u, dgate_z, mode="tn", out_dtype=BF16, name="d_w_z")
    dw_g = _mm(u, dgate_ab, mode="tn", out_dtype=BF16, name="d_w_g")
    d_w_in = (dw_small, dw_z, dw_fq, dw_fk, dw_fv, dw_g)
    d_w_uq = _uq_restore(d_w_uq_a)
    d_w_ukv = _ukv_restore(d_w_ukv_a)
    token = start_exchange(d_w_in, d_w_uq, d_w_ukv, d_w_bm, d_w_bf, d_w_out) if start_exchange else None
    du = _mm_sum_nt([(dsmall, w_small), (dfq, w_attn[:, 0:1024]), (dfk, w_attn[:, 1024:2048]),
                     (dfv, w_attn[:, 2048:3072]), (dgate_z, w_gate[:, 0:2048]), (dgate_ab, w_gate[:, 2048:4096])],
                    name="d_u", after=token)
    dx, dmeta, dg_pre = _pre_bwd(du, x2, meta_f, dy, pre_norm_g)
    return (loss_p, dx, dmeta, d_w_in, d_w_uq, d_w_ukv, d_w_bm, d_w_bf, d_w_out, dg_pre, dg_post, dg_q, dg_kv, dfb)


def _w_in_slabs(pieces):
    dw_small, dw_z, dw_fq, dw_fk, dw_fv, dw_g = pieces
    runs = [(dw_small[:, 0:416], C_CQ), (dw_z[:, 0:1024], C_ZMLA), (dw_fq, C_FQ), (dw_fk, C_FK), (dw_fv, C_FV),
            (dw_small[:, 512:528], C_FL), (dw_z[:, 1024:2048], C_ZFOX), (dw_g, C_GA)]
    slabs = []
    for j in range(N_CHIPS):
        lo, hi = W_IN_SHARD * j, W_IN_SHARD * (j + 1)
        cols = [a[:, max(lo, c0) - c0:min(hi, c0 + a.shape[1]) - c0] for a, c0 in runs
                if max(lo, c0) < min(hi, c0 + a.shape[1])]
        slabs.append(jnp.concatenate(cols, axis=1))
    return jnp.stack(slabs, axis=0)


def kernel(x, meta_tokens, pre_norm_g, w_in, fox_forget_b, mla_q_norm_g, mla_kv_norm_g, w_uq, w_ukv, w_br_mla, w_br_fox, w_out, post_norm_g, loss_target, m_meta_tokens, m_pre_norm_g, m_w_in, m_fox_forget_b, m_mla_q_norm_g, m_mla_kv_norm_g, m_w_uq, m_w_ukv, m_w_br_mla, m_w_br_fox, m_w_out, m_post_norm_g, v_meta_tokens, v_pre_norm_g, v_w_in, v_fox_forget_b, v_mla_q_norm_g, v_mla_kv_norm_g, v_w_uq, v_w_ukv, v_w_br_mla, v_w_br_fox, v_w_out, v_post_norm_g):
    me = 2 * lax.axis_index("x") + lax.axis_index("y")
    core = lax.axis_index("c")
    w_in_b = w_in.astype(BF16).reshape(D_MODEL, W_IN_SHARD)
    p2 = _pack_p2(w_uq[0], w_ukv[0], w_br_mla[0], w_br_fox[0], w_out[0], BF16)
    w_in_g, meta_g = _gather_weights([w_in_b], meta_tokens)
    p2_g = _gather_late(lax.optimization_barrier((p2, w_in_g))[0])
    slabs = [jnp.where(me == j, w_in_b, w_in_g[j]) for j in range(N_CHIPS)]
    chip = lax.broadcasted_iota(jnp.int32, (N_CHIPS, 1, 1), 0)
    p2_all = jnp.where(chip == me, p2[None], p2_g)
    w_uq_f = p2_all[:, 0:96].reshape(N_CHIPS, 256, 384).transpose(1, 0, 2).reshape(256, 1536)
    w_ukv_f = p2_all[:, 96:160].reshape(N_CHIPS, 128, 512).transpose(1, 0, 2).reshape(128, 2048)
    w_bm, w_bf, w_o = (p2_all[:, lo:lo + 256].reshape(D_MODEL, D_MODEL) for lo in (160, 416, 672))
    meta_f = jnp.where(chip == me, meta_tokens[None], meta_g).transpose(1, 0, 2).reshape(N_META, D_MODEL)
    kpe = _in_cols(slabs, C_KPE, C_ZMLA)
    w_small = jnp.concatenate(_in_cols(slabs, C_CQ, C_KPE) + kpe + kpe + [jnp.zeros((D_MODEL, 64), BF16)]
                              + _in_cols(slabs, C_FL, C_ZFOX) + [jnp.zeros((D_MODEL, 112), BF16)], axis=1)
    w_attn = jnp.concatenate(_in_cols(slabs, C_FQ, C_FL), axis=1)
    w_gate = jnp.concatenate(_in_cols(slabs, C_ZMLA, C_FQ) + _in_cols(slabs, C_ZFOX, C_END), axis=1)

    exchange = {}

    def start_exchange(d_w_in, d_w_uq, d_w_ukv, d_w_bm, d_w_bf, d_w_out):
        g2 = jnp.concatenate(
            [d_w_uq.reshape(256, N_CHIPS, 384).transpose(1, 0, 2).reshape(N_CHIPS, 96, D_MODEL),
             d_w_ukv.reshape(128, N_CHIPS, 512).transpose(1, 0, 2).reshape(N_CHIPS, 64, D_MODEL)]
            + [g.reshape(N_CHIPS, 256, D_MODEL) for g in (d_w_bm, d_w_bf, d_w_out)], axis=1)
        pieces = [p[None] for p in d_w_in]
        from_sib = _swap_halves(pieces + [g2])
        halves = [_add_cores(p, s, "add_cores_" + nm)[0]
                  for p, s, nm in zip(pieces, from_sib, ("small", "z", "fq", "fk", "fv", "g"))]
        parts = [_w_in_slabs(halves), _add_cores(g2, from_sib[-1], "add_cores_rest")]
        exchange.update(parts=parts, landed=_scatter_chips(parts))
        return parts[0][0, 0:16, 0:LANES]

    (loss_p, dx, dmeta, _, _, _, _, _, _, dg_pre, dg_post, dg_q, dg_kv,
     dfb) = _local_step(x[0], loss_target[0], meta_f, w_small, w_attn, w_gate, w_uq_f, w_ukv_f, w_bm, w_bf, w_o,
                        pre_norm_g, post_norm_g, mla_q_norm_g, mla_kv_norm_g, fox_forget_b, start_exchange)

    mine = [_add_chips(l, lax.dynamic_index_in_dim(p, me, 0, keepdims=False), nm)
            for l, p, nm in zip(exchange["landed"], exchange["parts"], ("add_chips_w_in", "add_chips_rest"))]
    theirs = _swap_reduced(mine)
    g_w_in, g_p2 = [jnp.concatenate([jnp.where(core == 0, a, b), jnp.where(core == 0, b, a)], axis=0)
                    for a, b in zip(mine, theirs)]
    g_w_uq, g_w_ukv, g_w_bm, g_w_bf, g_w_out = _unpack_p2(g_p2)
    g_w_in = g_w_in[None]

    vec = jnp.concatenate([dg_pre.reshape(8, 128), dg_post.reshape(8, 128), dg_q.reshape(2, 128), dg_kv,
                           dfb, _pad_lanes(loss_p), jnp.zeros((3, 128), F32), dmeta.reshape(128, 128)], axis=0)
    tot = _allreduce_small(vec)
    loss = tot[20, 0]
    g_meta = lax.dynamic_slice_in_dim(tot[24:].reshape(N_META, D_MODEL), 256 * me, 256, axis=1)

    def small_pack(pre, post, gq_, gkv_, fb_):
        return jnp.concatenate([pre.reshape(8, 128), post.reshape(8, 128), gq_.reshape(2, 128), gkv_,
                                _pad_lanes(fb_), jnp.zeros((4, 128), F32)], axis=0)

    def small_unpack(t):
        return (t[0:8].reshape(1, 1024), t[8:16].reshape(1, 1024), t[16:18].reshape(1, 256), t[18:19],
                t[19:20, 0:HEADS])

    g_small = jnp.concatenate([tot[0:20], jnp.zeros((4, 128), F32)], axis=0)
    sm = _adamw(small_pack(pre_norm_g, post_norm_g, mla_q_norm_g, mla_kv_norm_g, fox_forget_b), g_small,
                small_pack(m_pre_norm_g, m_post_norm_g, m_mla_q_norm_g, m_mla_kv_norm_g, m_fox_forget_b),
                small_pack(v_pre_norm_g, v_post_norm_g, v_mla_q_norm_g, v_mla_kv_norm_g, v_fox_forget_b),
                "adamw_small")
    g_pre, g_post, g_q, g_kv, g_fb = small_unpack(g_small)
    (d_pre, d_post, d_q, d_kv, d_fb), (nm_pre, nm_post, nm_q, nm_kv, nm_fb), (nv_pre, nv_post, nv_q, nv_kv, nv_fb) = (
        small_unpack(t) for t in sm)

    d_meta, nm_meta, nv_meta = _adamw(meta_tokens, g_meta, m_meta_tokens, v_meta_tokens, "adamw_meta")
    d_win, nm_win, nv_win = (t.T[None] for t in _adamw(w_in[0].T, g_w_in[0].T, m_w_in[0].T, v_w_in[0].T,
                                                       "adamw_w_in"))
    d_wuq, nm_wuq, nv_wuq = _adamw(w_uq[0], g_w_uq, m_w_uq[0], v_w_uq[0], "adamw_w_uq")
    d_wukv, nm_wukv, nv_wukv = _adamw(w_ukv[0], g_w_ukv, m_w_ukv[0], v_w_ukv[0], "adamw_w_ukv")
    d_wbm, nm_wbm, nv_wbm = _adamw(w_br_mla[0], g_w_bm, m_w_br_mla[0], v_w_br_mla[0], "adamw_w_br_mla")
    d_wbf, nm_wbf, nv_wbf = _adamw(w_br_fox[0], g_w_bf, m_w_br_fox[0], v_w_br_fox[0], "adamw_w_br_fox")
    d_wo, nm_wo, nv_wo = _adamw(w_out[0], g_w_out, m_w_out[0], v_w_out[0], "adamw_w_out")

    def group(meta_, pre, win, fb_, q_, kv_, wuq, wukv, wbm, wbf, wo, post):
        return (meta_, pre, win, fb_, q_, kv_, wuq[None], wukv[None], wbm[None], wbf[None], wo[None], post)

    grads = group(g_meta, g_pre, g_w_in, g_fb, g_q, g_kv, g_w_uq, g_w_ukv, g_w_bm, g_w_bf, g_w_out, g_post)
    deltas = group(d_meta, d_pre, d_win, d_fb, d_q, d_kv, d_wuq, d_wukv, d_wbm, d_wbf, d_wo, d_post)
    new_m = group(nm_meta, nm_pre, nm_win, nm_fb, nm_q, nm_kv, nm_wuq, nm_wukv, nm_wbm, nm_wbf, nm_wo, nm_post)
    new_v = group(nv_meta, nv_pre, nv_win, nv_fb, nv_q, nv_kv, nv_wuq, nv_wukv, nv_wbm, nv_wbf, nv_wo, nv_post)
    return (loss, dx[None], *grads, *deltas, *new_m, *new_v)
```

```python
import math

import jax
import jax.numpy as jnp
import numpy as np
from jax import lax
from jax.experimental import pallas as pl
from jax.experimental.pallas import tpu as pltpu
from jax.experimental.pallas import tpu_sc as plsc

F32 = jnp.float32
BF16 = jnp.bfloat16

D_MODEL = 1024
N_META = 16
RMS_EPS = 1e-6
HEADS = 16
PAIRS = HEADS // 2
HEAD_DIM = 64
LANES = 128
MLA_ROPE = 32
AHEAD = 6
AHEAD_BWD = 1
BIAS_PARTS = 3
MLA_SCALE = 1.0 / math.sqrt(64 + 32)
FOX_SCALE = 1.0 / math.sqrt(64)
LOG2E = math.log2(math.e)
LN2 = math.log(2.0)
ROPE_THETA = 10000.0

PAD = 256
BLK = 256
QB = 512
UNROLL = 4
NEG = -1e30

C_CQ, C_CKV, C_KPE, C_ZMLA, C_FQ, C_FK, C_FV, C_FL, C_ZFOX, C_GA, C_GB, C_END = (
    0, 256, 384, 416, 1440, 2464, 3488, 4512, 4528, 5552, 6576, 7600)
SMALL_W = 640
W_IN_SHARD = 1900

P2_ROWS = 928
N_CHIPS = 4

ADAM_LR = 0.001
ADAM_B1 = 0.9
ADAM_B2 = 0.999
ADAM_EPS = 1e-08
ADAM_WD = 0.01
ADAM_STEP = 10

VMEM_BIG = 56 * 1024 * 1024
MM_VMEM_BUDGET = 44 * 1024 * 1024
MESH = pl.DeviceIdType.MESH


def _cp(dims, vmem=None):
    return pltpu.CompilerParams(dimension_semantics=dims, vmem_limit_bytes=vmem)


def _dot(a, b, ca, cb):
    return lax.dot_general(a, b, (((ca,), (cb,)), ((), ())), preferred_element_type=F32)


def _sigmoid(x):
    return 1.0 / (1.0 + jnp.exp(-x))


def _tile(n, cands):
    for c in cands:
        if n % c == 0:
            return c
    return n


def _mm(a, b, *, mode, out_dtype, name, acc=None, epilogue=None, row_ins=(), after=None, col_scale=None):
    if mode == "nn":
        (M, K), N = a.shape, b.shape[1]
    elif mode == "nt":
        (M, K), N = a.shape, b.shape[0]
    else:
        (K, M), N = a.shape, b.shape[1]
    tm = _tile(M, (1088, 1024)) if M > 1024 else M
    tn = _tile(N, (1024,)) if N > 1024 else N
    nk = 1
    while True:
        tk = K // nk
        need = 2 * tk * (tm * a.dtype.itemsize + tn * b.dtype.itemsize) + tm * tn * (
            2 * jnp.dtype(out_dtype).itemsize + (8 if acc is not None else 0) + (4 if nk > 1 else 0))
        if need <= MM_VMEM_BUDGET or (tk // 2) % (16 if mode == "tn" else LANES) or tk <= 512:
            break
        nk *= 2
    while (M // tm) * (N // tn) * nk < 4 and tn % 512 == 0:
        tn //= 2
    assert col_scale is None or (nk == 1 and col_scale[0] % tn == 0)
    ca, cb = {"nn": (1, 0), "nt": (1, 1), "tn": (0, 0)}[mode]
    a_spec = (pl.BlockSpec((tk, tm), lambda j, i, k: (k, i)) if mode == "tn"
              else pl.BlockSpec((tm, tk), lambda j, i, k: (i, k)))
    b_spec = (pl.BlockSpec((tn, tk), lambda j, i, k: (j, k)) if mode == "nt"
              else pl.BlockSpec((tk, tn), lambda j, i, k: (k, j)))
    o_spec = pl.BlockSpec((tm, tn), lambda j, i, k: (i, j))
    has_acc = acc is not None

    nrow = len(row_ins)

    def body(*refs):
        a_ref, b_ref = refs[0], refs[1]
        acc_ref = refs[2] if has_acc else None
        rows = refs[2 + has_acc:2 + has_acc + nrow]
        o_ref = refs[2 + has_acc + nrow + (after is not None)]

        def store(tile):
            if epilogue is not None:
                tile = epilogue(tile, *[r[...] for r in rows])
            if col_scale is not None:
                tile = tile * jnp.where(pl.program_id(0) * tn < col_scale[0], col_scale[1], 1.0)
            o_ref[...] = tile.astype(out_dtype)

        part = _dot(a_ref[...].astype(BF16), b_ref[...].astype(BF16), ca, cb)
        if nk == 1:
            store(part + acc_ref[...] if has_acc else part)
        else:
            sc = refs[-1]
            k = pl.program_id(2)

            @pl.when(k == 0)
            def _():
                sc[...] = part + acc_ref[...] if has_acc else part

            @pl.when(k > 0)
            def _():
                sc[...] += part

            @pl.when(k == nk - 1)
            def _():
                store(sc[...])

    ins = [a, b] + ([acc] if has_acc else []) + list(row_ins)
    in_specs = ([a_spec, b_spec] + ([o_spec] if has_acc else [])
                + [pl.BlockSpec((tm, r.shape[1]), lambda j, i, k: (i, 0)) for r in row_ins])
    if after is not None:
        ins.append(after)
        in_specs.append(pl.BlockSpec(after.shape, lambda j, i, k: (0,) * after.ndim))
    return pl.pallas_call(
        body, name=name, grid=(N // tn, M // tm, nk), in_specs=in_specs, out_specs=o_spec,
        out_shape=jax.ShapeDtypeStruct((M, N), out_dtype),
        scratch_shapes=[pltpu.VMEM((tm, tn), F32)] if nk > 1 else [],
        compiler_params=_cp(("parallel", "parallel", "arbitrary"), VMEM_BIG))(*ins)


def _mm_sum_nt(pairs, *, name, after=None):
    n = len(pairs)
    M, N = pairs[0][0].shape[0], pairs[0][1].shape[0]
    tm = _tile(M, (272,))

    def body(*refs):
        o_ref = refs[2 * n + (after is not None)]
        tot = _dot(refs[0][...].astype(BF16), refs[n][...].astype(BF16), 1, 1)
        for i in range(1, n):
            tot = tot + _dot(refs[i][...].astype(BF16), refs[n + i][...].astype(BF16), 1, 1)
        o_ref[...] = tot

    ins = [a for a, _ in pairs] + [b for _, b in pairs]
    in_specs = ([pl.BlockSpec((tm, a.shape[1]), lambda i: (i, 0)) for a, _ in pairs]
                + [pl.BlockSpec(b.shape, lambda i: (0, 0)) for _, b in pairs])
    if after is not None:
        ins.append(after)
        in_specs.append(pl.BlockSpec(after.shape, lambda i: (0,) * after.ndim))
    return pl.pallas_call(
        body, name=name, grid=(M // tm,), in_specs=in_specs, out_specs=pl.BlockSpec((tm, N), lambda i: (i, 0)),
        out_shape=jax.ShapeDtypeStruct((M, N), F32), compiler_params=_cp(("parallel",), VMEM_BIG))(*ins)


def _row(w):
    return pl.BlockSpec((BLK, w), lambda i: (i, 0))


def _rowc(w, c):
    return pl.BlockSpec((BLK, w), lambda i: (i, c))


def _full(shape):
    return pl.BlockSpec(shape, lambda i: tuple(0 for _ in shape))


def _rope(x, c, s):
    lane = lax.broadcasted_iota(jnp.int32, x.shape, 1)
    is_x1 = ((lane >> 4) & 1) == 0
    partner = jnp.where(is_x1, pltpu.roll(x, LANES - 16, 1), pltpu.roll(x, 16, 1))
    return x * c + partner * s


def _row_valid(i):
    rows = i * BLK + lax.broadcasted_iota(jnp.int32, (BLK, 1), 0)
    return (rows < N_META) | (rows >= PAD)


def _shift_rows(w):
    return pl.BlockSpec((BLK, w), lambda i: (jnp.maximum(i - 1, 0), 0))


def _h_block(i, x_ref, meta_ref):
    head = jnp.concatenate([meta_ref[...], jnp.zeros((BLK - N_META, D_MODEL), F32)], axis=0)
    return jnp.where(i == 0, head, x_ref[...])


def _rms_pre(x2, meta, g):
    lp = PAD + x2.shape[0]

    def body(x_ref, meta_ref, g_ref, u_ref):
        hv = _h_block(pl.program_id(0), x_ref, meta_ref)
        r = lax.rsqrt(jnp.mean(hv * hv, axis=-1, keepdims=True) + RMS_EPS)
        u_ref[...] = (hv * r * g_ref[...]).astype(BF16)

    return pl.pallas_call(
        body, name="rms_pre", grid=(lp // BLK,),
        in_specs=[_shift_rows(D_MODEL), _full((N_META, D_MODEL)), _full((1, D_MODEL))], out_specs=_row(D_MODEL),
        out_shape=jax.ShapeDtypeStruct((lp, D_MODEL), BF16),
        compiler_params=_cp(("parallel",)))(x2, meta, g)


def _split3(x):
    hi = x.astype(BF16)
    r1 = x - hi.astype(F32)
    mid = r1.astype(BF16)
    lo = (r1 - mid.astype(F32)).astype(BF16)
    return hi, mid, lo


def _small_prep(small, gq, gkv, fb, ctab, stab, tri):
    lp = small.shape[0]

    def body(sm_ref, gq_ref, gkv_ref, fb_ref, c_ref, s_ref, tri_ref, qn_ref, kvn_ref, kr_ref, kb_ref, carry):
        i = pl.program_id(0)

        @pl.when(i == 0)
        def _():
            carry[...] = jnp.zeros_like(carry)

        cq = sm_ref[:, 0:256]
        r = lax.rsqrt(jnp.mean(cq * cq, axis=-1, keepdims=True) + RMS_EPS)
        qn_ref[...] = (cq * r * gq_ref[...]).astype(BF16)
        ckv = sm_ref[:, 256:384]
        r = lax.rsqrt(jnp.mean(ckv * ckv, axis=-1, keepdims=True) + RMS_EPS)
        kvn_ref[...] = (ckv * r * gkv_ref[...]).astype(BF16)
        kr_ref[...] = _rope(sm_ref[:, 384:512], c_ref[...], s_ref[...]).astype(BF16)
        fl = sm_ref[:, 512:640] + fb_ref[...]
        lf = jnp.minimum(fl, 0.0) - jnp.log(1.0 + jnp.exp(-jnp.abs(fl)))
        lf = jnp.where(_row_valid(i), lf, 0.0)
        hi, mid, lo = _split3(lf)
        t = tri_ref[...]
        cum = (_dot(t, hi, 1, 0) + _dot(t, mid, 1, 0)) + _dot(t, lo, 1, 0) + carry[...]
        carry[...] = cum[BLK - 1:BLK, :]
        src = lax.broadcasted_iota(jnp.int32, (LANES, LANES), 0)
        dst = lax.broadcasted_iota(jnp.int32, (LANES, LANES), 1)
        kb = jnp.zeros((BLK, LANES), F32)
        for j, part in enumerate(_split3(-cum * LOG2E)):
            spread = ((dst == BIAS_PARTS * src + j) & (src < HEADS)).astype(BF16)
            kb = kb + _dot(part, spread, 1, 0)
        kb_ref[...] = kb.astype(BF16)

    return pl.pallas_call(
        body, name="small_prep", grid=(lp // BLK,),
        in_specs=[_row(SMALL_W), _full((1, 256)), _full((1, 128)), _full((1, 128)), _row(128), _row(128),
                  _full((BLK, BLK))],
        out_specs=[_row(256), _row(128), _row(128), _row(128)],
        out_shape=[jax.ShapeDtypeStruct((lp, 256), BF16), jax.ShapeDtypeStruct((lp, 128), BF16),
                   jax.ShapeDtypeStruct((lp, 128), BF16), jax.ShapeDtypeStruct((lp, 128), BF16)],
        scratch_shapes=[pltpu.VMEM((1, 128), F32)],
        compiler_params=_cp(("arbitrary",)))(small, gq, gkv, fb, ctab, stab, tri)


def _rope_pairs(tile, c, s):
    out = []
    for lo in range(0, tile.shape[1], 256):
        out += [tile[:, lo:lo + 128], _rope(tile[:, lo + 128:lo + 256], c, s)]
    return jnp.concatenate(out, axis=1)


def _gate_fwd(o_mla, o_fox, gate):
    lp = o_mla.shape[0]

    def body(om_ref, of_ref, zm_ref, zf_ref, am_ref, af_ref):
        zm = zm_ref[...].astype(F32)
        am_ref[...] = (om_ref[...] * (zm * _sigmoid(zm))).astype(BF16)
        zf = zf_ref[...].astype(F32)
        af_ref[...] = (of_ref[...] * (zf * _sigmoid(zf))).astype(BF16)

    return pl.pallas_call(
        body, name="gate_fwd", grid=(lp // BLK,),
        in_specs=[_row(D_MODEL), _row(D_MODEL), _rowc(D_MODEL, 0), _rowc(D_MODEL, 1)],
        out_specs=[_row(D_MODEL), _row(D_MODEL)],
        out_shape=[jax.ShapeDtypeStruct((lp, D_MODEL), BF16)] * 2,
        compiler_params=_cp(("parallel",)))(o_mla, o_fox, gate, gate)


def _merge_fwd(gate, y_mla, y_fox):
    lp = y_mla.shape[0]

    def body(ga_ref, gb_ref, ym_ref, yf_ref, m_ref):
        sa = _sigmoid(ga_ref[...].astype(F32))
        sb = _sigmoid(gb_ref[...].astype(F32))
        m_ref[...] = (sa * ym_ref[...] + sb * yf_ref[...]).astype(BF16)

    return pl.pallas_call(
        body, name="merge_fwd", grid=(lp // BLK,),
        in_specs=[_rowc(D_MODEL, 2), _rowc(D_MODEL, 3), _row(D_MODEL), _row(D_MODEL)],
        out_specs=_row(D_MODEL), out_shape=jax.ShapeDtypeStruct((lp, D_MODEL), BF16),
        compiler_params=_cp(("parallel",)))(gate, gate, y_mla, y_fox)


def _tail(x2, mixed, tgt, gpost):
    lp = mixed.shape[0]
    shift = _shift_rows(D_MODEL)

    def body(h_ref, mx_ref, t_ref, g_ref, dmx_ref, dy_ref, loss_ref, dg_ref):
        i = pl.program_id(0)

        @pl.when(i == 0)
        def _():
            loss_ref[...] = jnp.zeros_like(loss_ref)
            dg_ref[...] = jnp.zeros_like(dg_ref)
            dmx_ref[...] = jnp.zeros_like(dmx_ref)
            dy_ref[...] = jnp.zeros_like(dy_ref)

        @pl.when(i > 0)
        def _():
            mx = mx_ref[...]
            g = g_ref[...]
            r = lax.rsqrt(jnp.mean(mx * mx, axis=-1, keepdims=True) + RMS_EPS)
            nrm = mx * r
            e = (h_ref[...] + nrm * g) - t_ref[...]
            loss_ref[...] += jnp.sum(0.5 * jnp.sum(e * e, axis=-1, keepdims=True) * (1.0 / D_MODEL),
                                     axis=0, keepdims=True)
            dy = e * (1.0 / D_MODEL)
            dy_ref[...] = dy
            dg_ref[...] += jnp.sum(dy * nrm, axis=0, keepdims=True)
            w = dy * g
            dot = jnp.mean(w * mx, axis=-1, keepdims=True)
            dmx_ref[...] = (r * w - mx * (r * r * r * dot)).astype(BF16)

    return pl.pallas_call(
        body, name="tail", grid=(lp // BLK,),
        in_specs=[shift, _row(D_MODEL), shift, _full((1, D_MODEL))],
        out_specs=[_row(D_MODEL), _row(D_MODEL), _full((1, 1)), _full((1, D_MODEL))],
        out_shape=[jax.ShapeDtypeStruct((lp, D_MODEL), BF16), jax.ShapeDtypeStruct((lp, D_MODEL), F32),
                   jax.ShapeDtypeStruct((1, 1), F32), jax.ShapeDtypeStruct((1, D_MODEL), F32)],
        compiler_params=_cp(("arbitrary",)))(x2, mixed, tgt, gpost)


def _merge_bwd(dm, gate, y_mla, y_fox):
    lp = dm.shape[0]

    def body(dm_ref, ga_ref, gb_ref, ym_ref, yf_ref, dym_ref, dyf_ref, dg_ref):
        dm_v = dm_ref[...].astype(F32)
        sa = _sigmoid(ga_ref[...].astype(F32))
        sb = _sigmoid(gb_ref[...].astype(F32))
        dym_ref[...] = (dm_v * sa).astype(BF16)
        dyf_ref[...] = (dm_v * sb).astype(BF16)
        dg_ref[:, 0:D_MODEL] = (dm_v * ym_ref[...] * (sa * (1.0 - sa))).astype(BF16)
        dg_ref[:, D_MODEL:2 * D_MODEL] = (dm_v * yf_ref[...] * (sb * (1.0 - sb))).astype(BF16)

    return pl.pallas_call(
        body, name="merge_bwd", grid=(lp // BLK,),
        in_specs=[_row(D_MODEL), _rowc(D_MODEL, 2), _rowc(D_MODEL, 3), _row(D_MODEL), _row(D_MODEL)],
        out_specs=[_row(D_MODEL), _row(D_MODEL), _row(2 * D_MODEL)],
        out_shape=[jax.ShapeDtypeStruct((lp, D_MODEL), BF16), jax.ShapeDtypeStruct((lp, D_MODEL), BF16),
                   jax.ShapeDtypeStruct((lp, 2 * D_MODEL), BF16)],
        compiler_params=_cp(("parallel",)))(dm, gate, gate, y_mla, y_fox)


def _gate_bwd(da_mla, da_fox, o_mla, o_fox, gate):
    lp = da_mla.shape[0]

    def one(da, o, z, head_of_col):
        sg = _sigmoid(z)
        do = (da * (z * sg)).astype(BF16)
        dz = da * o * (sg * (1.0 + z * (1.0 - sg)))
        delta = sum(_dot(part, head_of_col, 1, 0) for part in _split3(do.astype(F32) * o))
        return do, dz.astype(BF16), delta

    def body(dam_ref, daf_ref, om_ref, of_ref, zm_ref, zf_ref, dom_ref, dof_ref, dz_ref, dlm_ref, dlf_ref):
        f32 = lambda r: r[...].astype(F32)
        head_of_col = (lax.broadcasted_iota(jnp.int32, (D_MODEL, LANES), 0) // HEAD_DIM
                       == lax.broadcasted_iota(jnp.int32, (D_MODEL, LANES), 1)).astype(BF16)
        dom_ref[...], dz_ref[:, 0:D_MODEL], dlm_ref[...] = one(f32(dam_ref), f32(om_ref), f32(zm_ref), head_of_col)
        dof_ref[...], dz_ref[:, D_MODEL:2 * D_MODEL], dlf_ref[...] = one(f32(daf_ref), f32(of_ref), f32(zf_ref),
                                                                        head_of_col)

    return pl.pallas_call(
        body, name="gate_bwd", grid=(lp // BLK,),
        in_specs=[_row(D_MODEL)] * 4 + [_rowc(D_MODEL, 0), _rowc(D_MODEL, 1)],
        out_specs=[_row(D_MODEL), _row(D_MODEL), _row(2 * D_MODEL), _row(LANES), _row(LANES)],
        out_shape=[jax.ShapeDtypeStruct((lp, D_MODEL), BF16), jax.ShapeDtypeStruct((lp, D_MODEL), BF16),
                   jax.ShapeDtypeStruct((lp, 2 * D_MODEL), BF16), jax.ShapeDtypeStruct((lp, LANES), F32),
                   jax.ShapeDtypeStruct((lp, LANES), F32)],
        compiler_params=_cp(("parallel",)))(da_mla, da_fox, o_mla, o_fox, gate, gate)


def _small_bwd(small, dqn, dkvn, dkr, dcol_t, drow_t, gq, gkv, fb, ctab, stab, triu):
    lp = small.shape[0]
    nb = lp // BLK

    def rrow(w):
        return pl.BlockSpec((BLK, w), lambda i: (nb - 1 - i, 0))

    def body(sm_ref, dqn_ref, dkvn_ref, dkr_ref, dcol_ref, drow_ref, gq_ref, gkv_ref, fb_ref, c_ref, s_ref, tri_ref,
             ds_ref, dgq_ref, dgkv_ref, dfb_ref, carry):
        i = pl.program_id(0)

        @pl.when(i == 0)
        def _():
            carry[...] = jnp.zeros_like(carry)
            dgq_ref[...] = jnp.zeros_like(dgq_ref)
            dgkv_ref[...] = jnp.zeros_like(dgkv_ref)
            dfb_ref[...] = jnp.zeros_like(dfb_ref)

        def norm_bwd(x, dn, g, dg_ref):
            r = lax.rsqrt(jnp.mean(x * x, axis=-1, keepdims=True) + RMS_EPS)
            dg_ref[...] += jnp.sum(dn * (x * r), axis=0, keepdims=True)
            w = dn * g
            dot = jnp.mean(w * x, axis=-1, keepdims=True)
            return r * w - x * (r * r * r * dot)

        ds_ref[:, 0:256] = norm_bwd(sm_ref[:, 0:256], dqn_ref[...], gq_ref[...], dgq_ref).astype(BF16)
        ds_ref[:, 256:384] = norm_bwd(sm_ref[:, 256:384], dkvn_ref[...], gkv_ref[...], dgkv_ref).astype(BF16)

        dk = dkr_ref[0]
        for p in range(1, PAIRS):
            dk = dk + dkr_ref[p]
        dk = _rope(dk, c_ref[...], -s_ref[...])
        lane = lax.broadcasted_iota(jnp.int32, dk.shape, 1)
        dk = jnp.where(lane < MLA_ROPE, dk + pltpu.roll(dk, LANES - MLA_ROPE, 1), 0.0)
        ds_ref[:, 384:512] = dk.astype(BF16)

        dcol = dcol_ref[0]
        for p in range(1, PAIRS):
            dcol = dcol + pltpu.roll(dcol_ref[p], 2 * p, 1)
        rows16 = jnp.concatenate([drow_ref[p, h:h + 1, :] for p in range(PAIRS) for h in range(2)], axis=0)
        eye = (lax.broadcasted_iota(jnp.int32, (HEADS, LANES), 0)
               == lax.broadcasted_iota(jnp.int32, (HEADS, LANES), 1)).astype(BF16)
        drow = sum(_dot(part, eye, 0, 0) for part in _split3(rows16))
        dcr = dcol - drow
        hi, mid, lo = _split3(dcr)
        t = tri_ref[...]
        suf = (_dot(t, hi, 1, 0) + _dot(t, mid, 1, 0)) + _dot(t, lo, 1, 0) + carry[...]
        fl = sm_ref[:, 512:640] + fb_ref[...]
        dfl = jnp.where(_row_valid(nb - 1 - i), -suf * _sigmoid(-fl), 0.0)
        ds_ref[:, 512:640] = dfl.astype(BF16)
        dfb_ref[...] += jnp.sum(dfl, axis=0, keepdims=True)
        carry[...] += jnp.sum(dcr, axis=0, keepdims=True)

    return pl.pallas_call(
        body, name="small_bwd", grid=(nb,),
        in_specs=[rrow(SMALL_W), rrow(256), rrow(128),
                  pl.BlockSpec((PAIRS, BLK, 128), lambda i: (0, nb - 1 - i, 0)),
                  pl.BlockSpec((PAIRS, BLK, 128), lambda i: (0, nb - 1 - i, 0)),
                  pl.BlockSpec((PAIRS, 2, BLK), lambda i: (0, 0, nb - 1 - i)),
                  _full((1, 256)), _full((1, 128)), _full((1, 128)), rrow(128), rrow(128), _full((BLK, BLK))],
        out_specs=[rrow(SMALL_W), _full((1, 256)), _full((1, 128)), _full((1, 128))],
        out_shape=[jax.ShapeDtypeStruct((lp, SMALL_W), BF16), jax.ShapeDtypeStruct((1, 256), F32),
                   jax.ShapeDtypeStruct((1, 128), F32), jax.ShapeDtypeStruct((1, 128), F32)],
        scratch_shapes=[pltpu.VMEM((1, 128), F32)],
        compiler_params=_cp(("arbitrary",)))(small, dqn, dkvn, dkr, dcol_t, drow_t, gq, gkv, fb, ctab, stab, triu)


def _pre_bwd(du, x2, meta, dy, gpre):
    s_rows = x2.shape[0]
    lp = PAD + s_rows
    shift = _shift_rows(D_MODEL)

    def body(du_ref, x_ref, meta_ref, dy_ref, g_ref, dx_ref, dmeta_ref, dg_ref):
        i = pl.program_id(0)

        @pl.when(i == 0)
        def _():
            dg_ref[...] = jnp.zeros_like(dg_ref)

        hv = _h_block(i, x_ref, meta_ref)
        duv = du_ref[...]
        r = lax.rsqrt(jnp.mean(hv * hv, axis=-1, keepdims=True) + RMS_EPS)
        dg_ref[...] += jnp.sum(duv * (hv * r), axis=0, keepdims=True)
        w = duv * g_ref[...]
        dot = jnp.mean(w * hv, axis=-1, keepdims=True)
        dh = dy_ref[...] + (r * w - hv * (r * r * r * dot))
        dx_ref[...] = dh

        @pl.when(i == 0)
        def _():
            dmeta_ref[...] = dh[0:N_META, :]

    return pl.pallas_call(
        body, name="pre_bwd", grid=(lp // BLK,),
        in_specs=[_row(D_MODEL), shift, _full((N_META, D_MODEL)), _row(D_MODEL), _full((1, D_MODEL))],
        out_specs=[shift, _full((N_META, D_MODEL)), _full((1, D_MODEL))],
        out_shape=[jax.ShapeDtypeStruct((s_rows, D_MODEL), F32), jax.ShapeDtypeStruct((N_META, D_MODEL), F32),
                   jax.ShapeDtypeStruct((1, D_MODEL), F32)],
        compiler_params=_cp(("arbitrary",)))(du, x2, meta, dy, gpre)


def _pair_masks(rope, pair):
    lane = lax.broadcasted_iota(jnp.int32, (1, LANES), 1)
    mas = [lane < HEAD_DIM, lane >= HEAD_DIM]
    wide = lax.broadcasted_iota(jnp.int32, (1, 2 * LANES), 1)
    extra = MLA_ROPE if rope else BIAS_PARTS
    lo = LANES if rope else LANES + 2 * BIAS_PARTS * pair
    mid = lo + extra
    return mas, [(wide < HEAD_DIM) | ((wide >= lo) & (wide < mid)),
                 ((wide >= HEAD_DIM) & (wide < LANES)) | ((wide >= mid) & (wide < mid + extra))]


def _mask2(x, masks):
    return [jnp.where(m, x, jnp.zeros_like(x)) for m in masks]


def _q_heads(q_rows, rope, mas, hmask):
    if rope:
        return _mask2(q_rows, hmask)
    zero = jnp.zeros((q_rows.shape[0], LANES), BF16)
    return [jnp.concatenate([jnp.where(m, q_rows, zero), jnp.where(hm[:, LANES:], zero + 1, zero)], axis=1)
            for m, hm in zip(mas, hmask)]


def _transposed_cols(x, group, name):
    lp = x.shape[0]

    def body(x_ref, o_ref):
        o_ref[...] = x_ref[...].T

    cols = 2 * BLK
    per_group = D_MODEL // cols
    return pl.pallas_call(
        body, name=name, grid=(per_group,),
        in_specs=[pl.BlockSpec((lp, cols), lambda i: (0, per_group * group + i))],
        out_specs=pl.BlockSpec((cols, lp), lambda i: (i, 0)),
        out_shape=jax.ShapeDtypeStruct((D_MODEL, lp), x.dtype),
        compiler_params=_cp(("parallel",), MM_VMEM_BUDGET))(x)


def _attn_fwd(q, k, v, vt, k2, *, rope, qcol, kcol, vcol, name):
    lp = q.shape[0]
    nq = 1 + (lp - PAD) // QB
    qw = 256 if rope else 128

    def body(q_ref, k_ref, v_ref, vt_ref, k2_ref, o_ref, lse_ref):
        i = pl.program_id(1)
        r0 = pl.multiple_of(jnp.where(i == 0, 0, PAD + QB * (i - 1)), BLK)
        b0 = r0 // BLK
        mas, hmask = _pair_masks(rope, pl.program_id(0))
        qh = _q_heads(q_ref[pl.ds(r0, QB), :], rope, mas, hmask)

        def update(chunks, tiles, groups):
            m = [[cr[0], cr[2]] for _, _, cr in groups]
            l = [[cr[1], cr[3]] for _, _, cr in groups]
            acc = [[cr[4][0:HEAD_DIM], cr[4][HEAD_DIM:LANES]] for _, _, cr in groups]
            qs = [[x[q_lo:q_lo + wq] for x in qh] for q_lo, wq, _ in groups]
            k0s = [pl.multiple_of(kc * BLK, BLK) for kc, _ in chunks]
            kks = [jnp.concatenate([k_ref[pl.ds(k0, n), :], k2_ref[pl.ds(k0, n), :]], axis=1)
                   for k0, (_, n) in zip(k0s, chunks)]
            jobs = [(g, ci, mask, h) for g, ci, mask in tiles for h in range(2)]
            score = lambda t: _dot(kks[jobs[t][1]], qs[jobs[t][0]][jobs[t][3]], 1, 1)
            ss = [score(t) for t in range(min(AHEAD, len(jobs)))]
            for t, (g, ci, mask, h) in enumerate(jobs):
                if t + AHEAD < len(jobs):
                    ss.append(score(t + AHEAD))
                s = ss[t] if mask is None else jnp.where(mask, ss[t], NEG)
                m_new = jnp.maximum(m[g][h], jnp.max(s, axis=0, keepdims=True))
                alpha = jnp.exp2(m[g][h] - m_new)
                p = jnp.exp2(s - m_new)
                l[g][h] = alpha * l[g][h] + jnp.sum(p, axis=0, keepdims=True)
                m[g][h] = m_new
                n = chunks[ci][1]
                if n == BLK:
                    pv = _dot(vt_ref[pl.ds(HEAD_DIM * h, HEAD_DIM), pl.ds(k0s[ci], BLK)], p.astype(BF16), 1, 0)
                else:
                    vm = jnp.where(mas[h], v_ref[0:n, :], jnp.zeros((), BF16))
                    pv = _dot(vm, p.astype(BF16), 0, 0)[HEAD_DIM * h:HEAD_DIM * (h + 1)]
                acc[g][h] = alpha * acc[g][h] + pv
            return [(m[g][0], l[g][0], m[g][1], l[g][1], jnp.concatenate(acc[g], axis=0)) for g in range(len(groups))]

        def full_chunks(kcs, carry):
            return update([(kc, BLK) for kc in kcs], [(0, ci, None) for ci in range(len(kcs))], [(0, QB, carry)])[0]

        neg = jnp.full((1, QB), NEG, F32)
        zero = jnp.zeros((1, QB), F32)
        c = (neg, zero, neg, zero, jnp.zeros((LANES, QB), F32))
        n_mid = jnp.maximum(b0 - 1, 0)
        c = lax.fori_loop(0, n_mid // 4, lambda t, cr: full_chunks([4 * t + u for u in (1, 2, 3, 4)], cr), c)
        c = lax.fori_loop(0, (n_mid % 4) // 2, lambda t, cr: full_chunks([n_mid - 1, n_mid], cr), c)
        key_l = lax.broadcasted_iota(jnp.int32, (BLK, BLK), 0)
        qry_l = lax.broadcasted_iota(jnp.int32, (BLK, BLK), 1)
        tri = (key_l <= qry_l) & (b0 > 0)
        meta_ok = (key_l[0:N_META] <= qry_l[0:N_META]) | (b0 > 0)
        lo, hi = update([(0, N_META), (b0, BLK), (b0 + 1, BLK)],
                        [(0, 0, meta_ok), (1, 0, None), (0, 1, tri), (1, 1, None), (1, 2, tri)],
                        [(0, BLK, tuple(a[:, 0:BLK] for a in c)), (BLK, QB - BLK, tuple(a[:, BLK:QB] for a in c))])
        c = tuple(jnp.concatenate([a, b], axis=1) for a, b in zip(lo, hi))
        inv =jnp.concatenate([jnp.broadcast_to(1.0 / c[1], (HEAD_DIM, QB)),
                               jnp.broadcast_to(1.0 / c[3], (HEAD_DIM, QB))], axis=0)
        o_t = (c[4] * inv).T.astype(BF16)
        lses = [c[2 * h] + jnp.log(c[2 * h + 1]) * LOG2E for h in range(2)]
        o_ref[pl.ds(r0, BLK), :] = o_t[0:BLK]
        for h in range(2):
            lse_ref[0, h:h + 1, pl.ds(r0, BLK)] = lses[h][:, 0:BLK]

        @pl.when(i > 0)
        def _():
            r1 = pl.multiple_of(r0 + BLK, BLK)
            o_ref[pl.ds(r1, QB - BLK), :] = o_t[BLK:QB]
            for h in range(2):
                lse_ref[0, h:h + 1, pl.ds(r1, QB - BLK)] = lses[h][:, BLK:QB]

    in_specs = [pl.BlockSpec((lp, qw), lambda p, i: (0, qcol + p)),
                pl.BlockSpec((lp, 128), lambda p, i: (0, kcol(p))),
                pl.BlockSpec((BLK, 128), lambda p, i: (0, vcol(p))),
                pl.BlockSpec((128, lp), lambda p, i: (p, 0)),
                pl.BlockSpec((lp, 128), lambda p, i: (0, 0))]
    return pl.pallas_call(
        body, name=name, grid=(PAIRS, nq), in_specs=in_specs,
        out_specs=[pl.BlockSpec((lp, 128), lambda p, i: (0, p)),
                   pl.BlockSpec((1, 2, lp), lambda p, i: (p, 0, 0))],
        out_shape=[jax.ShapeDtypeStruct((lp, D_MODEL), BF16), jax.ShapeDtypeStruct((PAIRS, 2, lp), F32)],
        compiler_params=_cp(("parallel", "arbitrary"), VMEM_BIG))(q, k, v, vt, k2)


def _attn_bwd(q, k, v, k2, do, delta, lse, *, rtabs=None, scale, qcol, kcol, vcol, name):
    lp = q.shape[0]
    nb = lp // BLK
    rope = rtabs is not None
    bias = not rope
    qw = 256 if rope else 128

    def body(*refs):
        it = iter(refs)
        q_ref, k_ref, v_ref, k2_ref = next(it), next(it), next(it), next(it)
        do_ref, dl_ref, lse_ref = next(it), next(it), next(it)
        ct_ref, st_ref = (next(it), next(it)) if rope else (None, None)
        dq_out, dk_ref, dv_ref = next(it), next(it), next(it)
        x_ref = next(it)
        drow_ref = next(it) if bias else None
        dq_ref = next(it)
        kb = pl.program_id(1)
        mas, hmask = _pair_masks(rope, pl.program_id(0))
        lane = lax.broadcasted_iota(jnp.int32, (1, LANES), 1)

        @pl.when(kb == 0)
        def _():
            dq_ref[...] = jnp.zeros_like(dq_ref)
            if bias:
                drow_ref[...] = jnp.zeros_like(drow_ref)

        def key_pass(n, w):
            kk = jnp.concatenate([k_ref[0:n, :], k2_ref[0:n, :]], axis=1)
            vh = _mask2(v_ref[0:n, :], mas)
            kcat = jnp.concatenate([x[:, 0:qw] for x in _mask2(kk, hmask)], axis=0)
            diag_mask = (lax.broadcasted_iota(jnp.int32, (n, w), 0) <= lax.broadcasted_iota(jnp.int32, (n, w), 1))

            def front(qc):
                q0 = qc * w if isinstance(qc, int) else pl.multiple_of(qc * w, w)
                dov = do_ref[pl.ds(q0, w), :]
                qh = _q_heads(q_ref[pl.ds(q0, w), :], rope, mas, hmask)
                return (q0, dov, qh, [_dot(kk, qh[h], 1, 1) for h in range(2)],
                        [_dot(vh[h], dov, 1, 1) for h in range(2)])

            def back(fronted, carry, mask):
                carry = list(carry)
                q0, dov, qh, ss, dps = fronted
                doh = _mask2(dov, mas)
                pbs, dss = [], []
                for h in range(2):
                    p = jnp.exp2(ss[h] - lse_ref[0, h:h + 1, pl.ds(q0, w)])
                    if mask is not None:
                        p = jnp.where(mask, p, 0.0)
                    ds = p * (dps[h] - dl_ref[0, h:h + 1, pl.ds(q0, w)])
                    if bias:
                        drow_ref[0, h:h + 1, pl.ds(q0, w)] += jnp.sum(ds, axis=0, keepdims=True)
                        carry[2 + h] = carry[2 + h] + jnp.sum(ds, axis=1, keepdims=True)
                    pbs.append(p.astype(BF16))
                    dss.append(ds.astype(BF16))
                ds_lanes = jnp.concatenate(dss, axis=1)
                ds_rows = jnp.concatenate(dss, axis=0)
                qcat = jnp.concatenate([x[:, 0:qw] for x in qh], axis=0)
                carry[0] = carry[0] + _dot(ds_lanes, qcat, 1, 0)
                carry[1] = carry[1] + _dot(jnp.concatenate(pbs, axis=1), jnp.concatenate(doh, axis=0), 1, 0)
                dq_ref[pl.ds(q0, w), :] += _dot(ds_rows, kcat, 0, 0)
                return tuple(carry)

            def chunks(qcs, carry, masks):
                ahead = 0 if rope else AHEAD_BWD
                fronted = [front(qc) for qc in qcs[:ahead]]
                for u, mask in enumerate(masks):
                    if u + ahead < len(qcs):
                        fronted.append(front(qcs[u + ahead]))
                    carry = back(fronted[u], carry, mask)
                return carry

            c = [jnp.zeros((n, qw), F32), jnp.zeros((n, LANES), F32)]
            if bias:
                c += [jnp.zeros((n, 1), F32), jnp.zeros((n, 1), F32)]
            c = tuple(c)
            if w != BLK:
                c = chunks(list(range(lp // w)), c, [diag_mask] + [None] * (lp // w - 1))
            else:
                groups = (nb - kb) // UNROLL

                def several(t, cr):
                    return chunks([kb + UNROLL * t + u for u in range(UNROLL)], cr,
                                  [diag_mask | (t > 0)] + [None] * (UNROLL - 1))

                c = lax.fori_loop(0, groups, several, c)
                start = kb + UNROLL * groups
                pairs = (nb - start) // 2

                def two(t, cr):
                    qc = start + 2 * t
                    return chunks([qc, qc + 1], cr, [diag_mask | (qc > kb), None])

                c = lax.fori_loop(0, pairs, two, c)
                c = lax.fori_loop(start + 2 * pairs, nb, lambda qc, cr: chunks([qc], cr, [diag_mask | (qc > kb)]), c)

            def rows(a, dtype):
                a = a.astype(dtype)
                return a if n == BLK else jnp.concatenate([a, jnp.zeros((BLK - n, a.shape[1]), dtype)], axis=0)

            dk = c[0] * LN2
            dk_ref[...] = rows(dk[:, 0:LANES], BF16)
            dv_ref[...] = rows(c[1], BF16)
            if rope:
                x_ref[0] = rows(dk[:, LANES:2 * LANES], F32)
            if bias:
                x_ref[0] = rows(jnp.where(lane == 0, c[2], jnp.where(lane == 1, c[3], 0.0)), F32)

        @pl.when(kb == 0)
        def _():
            key_pass(N_META, lp // 2)

        @pl.when(kb > 0)
        def _():
            key_pass(BLK, BLK)

        @pl.when(kb == nb - 1)
        def _():
            def fin(c, carry):
                r0 = pl.multiple_of(c * BLK, BLK)
                dq = dq_ref[pl.ds(r0, BLK), :] * scale
                if rope:
                    back = _rope(dq[:, LANES:2 * LANES], ct_ref[pl.ds(r0, BLK), :], -st_ref[pl.ds(r0, BLK), :])
                    dq = jnp.concatenate([dq[:, 0:LANES], back], axis=1)
                dq_out[pl.ds(r0, BLK), :] = dq.astype(BF16)
                return carry

            lax.fori_loop(0, nb, fin, 0)

    in_specs = [pl.BlockSpec((lp, qw), lambda p, j: (0, qcol + p)),
                pl.BlockSpec((BLK, 128), lambda p, j: (j, kcol(p))),
                pl.BlockSpec((BLK, 128), lambda p, j: (j, vcol(p))),
                pl.BlockSpec((BLK, 128), lambda p, j: (j, 0)),
                pl.BlockSpec((lp, 128), lambda p, j: (0, p)), pl.BlockSpec((1, 2, lp), lambda p, j: (p, 0, 0)),
                pl.BlockSpec((1, 2, lp), lambda p, j: (p, 0, 0))]
    ins = [q, k, v, k2, do, delta, lse]
    if rope:
        in_specs += [pl.BlockSpec((lp, 128), lambda p, j: (0, 0))] * 2
        ins += list(rtabs)
    out_specs = [pl.BlockSpec((lp, qw), lambda p, j: (0, p)),
                 pl.BlockSpec((BLK, 128), lambda p, j: (j, p)),
                 pl.BlockSpec((BLK, 128), lambda p, j: (j, p)),
                 pl.BlockSpec((1, BLK, 128), lambda p, j: (p, j, 0))]
    out_shape = [jax.ShapeDtypeStruct((lp, PAIRS * qw), BF16), jax.ShapeDtypeStruct((lp, D_MODEL), BF16),
                 jax.ShapeDtypeStruct((lp, D_MODEL), BF16), jax.ShapeDtypeStruct((PAIRS, lp, 128), F32)]
    if bias:
        out_specs.append(pl.BlockSpec((1, 2, lp), lambda p, j: (p, 0, 0)))
        out_shape.append(jax.ShapeDtypeStruct((PAIRS, 2, lp), F32))
    return pl.pallas_call(
        body, name=name, grid=(PAIRS, nb), in_specs=in_specs, out_specs=out_specs, out_shape=out_shape,
        scratch_shapes=[pltpu.VMEM((lp, qw), F32)],
        compiler_params=_cp(("parallel", "arbitrary"), VMEM_BIG))(*ins)


def _adamw(w, g, m, v, name):
    lead = w.ndim - 2
    rows, cols = w.shape[lead:]
    big = rows * cols > 512 * 1024
    tr = 128 if big and rows % 128 == 0 else rows
    tc = 256 if big and tr == rows else cols

    def body(w_ref, g_ref, m_ref, v_ref, d_ref, nm_ref, nv_ref):
        gv = g_ref[...]
        nm = ADAM_B1 * m_ref[...] + (1.0 - ADAM_B1) * gv
        nv = ADAM_B2 * v_ref[...] + (1.0 - ADAM_B2) * (gv * gv)
        m_hat = nm / (1.0 - ADAM_B1 ** ADAM_STEP)
        v_hat = nv / (1.0 - ADAM_B2 ** ADAM_STEP)
        d_ref[...] = -ADAM_LR * (m_hat / (jnp.sqrt(v_hat) + ADAM_EPS) + ADAM_WD * w_ref[...])
        nm_ref[...] = nm
        nv_ref[...] = nv

    spec = pl.BlockSpec((1,) * lead + (tr, tc), lambda i, j: (0,) * lead + (i, j))
    return pl.pallas_call(
        body, name=name, grid=(rows // tr, cols // tc), in_specs=[spec] * 4, out_specs=[spec] * 3,
        out_shape=[jax.ShapeDtypeStruct(w.shape, F32)] * 3,
        compiler_params=_cp(("parallel", "parallel"), VMEM_BIG))(w, g, m, v)


def _add_cores(g, from_sib, name):
    n, rows, cols = g.shape
    half = rows // 2
    tr = _tile(half, (256, 240))
    nt = half // tr

    def body(lo_ref, hi_ref, s_ref, o_ref):
        mine = jnp.where(lax.axis_index("c") == 0, lo_ref[0], hi_ref[0])
        o_ref[0] = (mine.astype(F32) + s_ref[0].astype(F32)).astype(BF16)

    return pl.pallas_call(
        body, name=name, grid=(n, nt),
        in_specs=[pl.BlockSpec((1, tr, cols), lambda j, i: (j, i, 0)),
                  pl.BlockSpec((1, tr, cols), lambda j, i: (j, nt + i, 0)),
                  pl.BlockSpec((1, tr, cols), lambda j, i: (j, i, 0))],
        out_specs=pl.BlockSpec((1, tr, cols), lambda j, i: (j, i, 0)),
        out_shape=jax.ShapeDtypeStruct((n, half, cols), BF16),
        compiler_params=_cp(("parallel", "parallel"), VMEM_BIG))(g, g, from_sib)


def _add_chips(x, own, name):
    n, rows, cols = x.shape
    tr = _tile(rows, (256, 240))

    def body(x_ref, own_ref, o_ref):
        me = 2 * lax.axis_index("x") + lax.axis_index("y")
        v = [jnp.where(me == k, own_ref[...], x_ref[k]).astype(F32) for k in range(N_CHIPS)]
        o_ref[...] = ((v[0] + v[1]) + v[2]) + v[3]

    return pl.pallas_call(
        body, name=name, grid=(rows // tr,),
        in_specs=[pl.BlockSpec((n, tr, cols), lambda i: (0, i, 0)), pl.BlockSpec((tr, cols), lambda i: (i, 0))],
        out_specs=pl.BlockSpec((tr, cols), lambda i: (i, 0)),
        out_shape=jax.ShapeDtypeStruct((rows, cols), F32), compiler_params=_cp(("parallel",), VMEM_BIG))(x, own)


def _axes():
    return lax.axis_index("x"), lax.axis_index("y"), lax.axis_index("c")


def _other_chips(x, y):
    return [(1 - x, y), (x, 1 - y), (1 - x, 1 - y)]


ANY = pl.BlockSpec(memory_space=pl.ANY)


def _rcopy(src, dst, send_sems, recv_sems, k, to):
    return pltpu.make_async_remote_copy(src_ref=src, dst_ref=dst, send_sem=send_sems.at[k], recv_sem=recv_sems.at[k],
                                        device_id=to, device_id_type=MESH)


def _gather_weights(shards, meta):
    n = len(shards)

    def body(*refs):
        srcs, meta_ref = refs[:n], refs[n]
        outs, mout_ref = refs[n + 1:2 * n + 1], refs[2 * n + 1]
        send_sems, recv_sems = refs[2 * n + 2:]
        x, y, c = _axes()
        me = 2 * x + y
        sib = (x, y, 1 - c)
        chips = _other_chips(x, y)

        def half(t, chip_idx, cc):
            hr = shards[t].shape[0] // 2
            return outs[t].at[chip_idx, pl.ds(cc * hr, hr), :]

        first = []
        for j, (px, py) in enumerate(chips):
            for t in range(n):
                hr = shards[t].shape[0] // 2
                first.append(_rcopy(srcs[t].at[pl.ds(c * hr, hr), :], half(t, me, c), send_sems, recv_sems,
                                    3 * t + j, (px, py, c)))
            first.append(_rcopy(meta_ref, mout_ref.at[me], send_sems, recv_sems, 3 * n + j, (px, py, c)))
        for cp in first:
            cp.start()
        passed = []
        for j, (px, py) in enumerate(chips):
            src_chip = 2 * px + py
            for t in range(n):
                _rcopy(half(t, src_chip, c), half(t, src_chip, c), send_sems, recv_sems, 3 * t + j, sib).wait_recv()
                fwd = _rcopy(half(t, src_chip, c), half(t, src_chip, c), send_sems, recv_sems, 3 * (n + 1 + t) + j, sib)
                fwd.start()
                passed.append(fwd)
            _rcopy(mout_ref.at[src_chip], mout_ref.at[src_chip], send_sems, recv_sems, 3 * n + j, sib).wait_recv()
        for j, (px, py) in enumerate(chips):
            src_chip = 2 * px + py
            for t in range(n):
                _rcopy(half(t, src_chip, 1 - c), half(t, src_chip, 1 - c), send_sems, recv_sems,
                       3 * (n + 1 + t) + j, sib).wait_recv()
        for cp in first + passed:
            cp.wait_send()

    nsem = 3 * (2 * n + 1)
    return pl.pallas_call(
        body, name="gather_weights", in_specs=[ANY] * (n + 1), out_specs=[ANY] * (n + 1),
        out_shape=[jax.ShapeDtypeStruct((N_CHIPS,) + s.shape, s.dtype) for s in shards]
        + [jax.ShapeDtypeStruct((N_CHIPS,) + meta.shape, meta.dtype)],
        scratch_shapes=[pltpu.SemaphoreType.DMA((nsem,)), pltpu.SemaphoreType.DMA((nsem,))])(*shards, meta)


def _gather_late(shard):
    rows, cols = shard.shape
    hr = rows // 2
    src = jax.new_ref(shard, memory_space=pltpu.MemorySpace.HBM)
    out = jax.empty_ref(jax.ShapeDtypeStruct((N_CHIPS, rows, cols), shard.dtype), memory_space=pltpu.MemorySpace.HBM)

    @pl.kernel(mesh=plsc.ScalarSubcoreMesh(axis_name="seq", num_cores=1), name="gather_late",
               scratch_types=(pltpu.SemaphoreType.DMA((6,)), pltpu.SemaphoreType.DMA((6,))),
               compiler_params=pltpu.CompilerParams(collective_id=1))
    def launch(send_sems, recv_sems):
        x, y, c = _axes()
        me = 2 * x + y
        sib = (x, y, 1 - c)
        chips = _other_chips(x, y)
        barrier = pltpu.get_barrier_semaphore()
        for px, py in chips:
            pl.semaphore_signal(barrier, inc=1, device_id=(px, py, c), device_id_type=MESH)
        pl.semaphore_signal(barrier, inc=1, device_id=sib, device_id_type=MESH)
        pl.semaphore_wait(barrier, 4)

        def half(chip_idx, cc):
            return out.at[chip_idx, pl.ds(cc * hr, hr), :]

        first = [_rcopy(src.at[pl.ds(c * hr, hr), :], half(me, c), send_sems, recv_sems, j, (px, py, c))
                 for j, (px, py) in enumerate(chips)]
        for cp in first:
            cp.start()
        passed = []
        for j, (px, py) in enumerate(chips):
            land = half(2 * px + py, c)
            _rcopy(land, land, send_sems, recv_sems, j, sib).wait_recv()
            fwd = _rcopy(land, land, send_sems, recv_sems, 3 + j, sib)
            fwd.start()
            passed.append(fwd)
        for j, (px, py) in enumerate(chips):
            land = half(2 * px + py, 1 - c)
            _rcopy(land, land, send_sems, recv_sems, 3 + j, sib).wait_recv()
        for cp in first + passed:
            cp.wait_send()

    launch()
    return out[...]


def _swap_halves(gs):
    n = len(gs)
    ncopies = sum(g.shape[0] for g in gs)

    def body(*refs):
        srcs, outs = refs[:n], refs[n:2 * n]
        send_sems, recv_sems = refs[2 * n:]
        x, y, c = _axes()
        cps = []
        for t in range(n):
            hr = gs[t].shape[1] // 2
            for j in range(gs[t].shape[0]):
                cps.append(_rcopy(srcs[t].at[j, pl.ds((1 - c) * hr, hr), :], outs[t].at[j], send_sems, recv_sems,
                                  len(cps), (x, y, 1 - c)))
        for cp in cps:
            cp.start()
        for cp in cps:
            cp.wait()

    return pl.pallas_call(
        body, name="swap_halves", in_specs=[ANY] * n, out_specs=[ANY] * n,
        out_shape=[jax.ShapeDtypeStruct((g.shape[0], g.shape[1] // 2, g.shape[2]), g.dtype) for g in gs],
        scratch_shapes=[pltpu.SemaphoreType.DMA((ncopies,)), pltpu.SemaphoreType.DMA((ncopies,))])(*gs)


def _scatter_chips(parts):
    n = len(parts)
    srcs = [jax.new_ref(p, memory_space=pltpu.MemorySpace.HBM) for p in parts]
    outs = [jax.empty_ref(jax.ShapeDtypeStruct(p.shape, p.dtype), memory_space=pltpu.MemorySpace.HBM) for p in parts]

    @pl.kernel(mesh=plsc.ScalarSubcoreMesh(axis_name="seq", num_cores=1), name="scatter_chips",
               scratch_types=(pltpu.SemaphoreType.DMA((3 * n,)), pltpu.SemaphoreType.DMA((3 * n,))),
               compiler_params=pltpu.CompilerParams(collective_id=0))
    def launch(send_sems, recv_sems):
        x, y, c = _axes()
        me = 2 * x + y
        chips = _other_chips(x, y)
        barrier = pltpu.get_barrier_semaphore()
        for px, py in chips:
            pl.semaphore_signal(barrier, inc=1, device_id=(px, py, c), device_id_type=MESH)
        pl.semaphore_wait(barrier, 3)
        cps = []
        for j, (px, py) in enumerate(chips):
            for t in range(n):
                cps.append(_rcopy(srcs[t].at[2 * px + py], outs[t].at[me], send_sems, recv_sems, 3 * t + j,
                                  (px, py, c)))
        for cp in cps:
            cp.start()
        for cp in cps:
            cp.wait()

    launch()
    return [o[...] for o in outs]


def _swap_reduced(rs):
    n = len(rs)

    def body(*refs):
        srcs, outs = refs[:n], refs[n:2 * n]
        send_sems, recv_sems = refs[2 * n:]
        x, y, c = _axes()
        cps = [_rcopy(srcs[t], outs[t], send_sems, recv_sems, t, (x, y, 1 - c)) for t in range(n)]
        for cp in cps:
            cp.start()
        for cp in cps:
            cp.wait()

    return pl.pallas_call(
        body, name="swap_reduced", in_specs=[ANY] * n, out_specs=[ANY] * n,
        out_shape=[jax.ShapeDtypeStruct(r.shape, r.dtype) for r in rs],
        scratch_shapes=[pltpu.SemaphoreType.DMA((n,)), pltpu.SemaphoreType.DMA((n,))])(*rs)


SMALL_ROWS = 24 + 128


def _allreduce_small(vec):
    def body(v_ref, out_ref, slots, send_sems, recv_sems):
        x, y, c = _axes()
        me = 4 * x + 2 * y + c
        slots[me] = v_ref[...]
        cps = []
        for k in range(1, 8):
            kx, ky, kc = (k >> 2) & 1, (k >> 1) & 1, k & 1
            peer = (1 - x if kx else x, 1 - y if ky else y, 1 - c if kc else c)
            cps.append(_rcopy(v_ref, slots.at[me], send_sems, recv_sems, k - 1, peer))
        for cp in cps:
            cp.start()
        for cp in cps:
            cp.wait()
        tot = slots[0]
        for k in range(1, 8):
            tot = tot + slots[k]
        out_ref[...] = tot

    return pl.pallas_call(
        body, name="allreduce_small",
        in_specs=[pl.BlockSpec(memory_space=pltpu.VMEM)], out_specs=pl.BlockSpec(memory_space=pltpu.VMEM),
        out_shape=jax.ShapeDtypeStruct((SMALL_ROWS, 128), F32),
        scratch_shapes=[pltpu.VMEM((8, SMALL_ROWS, 128), F32), pltpu.SemaphoreType.DMA((7,)),
                        pltpu.SemaphoreType.DMA((7,))])(vec)


def _pack_p2(w_uq, w_ukv, w_br_mla, w_br_fox, w_out, dtype):
    parts = [w_uq.reshape(96, D_MODEL), w_ukv.reshape(64, D_MODEL), w_br_mla, w_br_fox, w_out]
    return jnp.concatenate([p.astype(dtype) for p in parts], axis=0)


def _unpack_p2(pk):
    return pk[0:96].reshape(256, 384), pk[96:160].reshape(128, 512), pk[160:416], pk[416:672], pk[672:928]


def _uq_arrange(w):
    w3 = w.reshape(256, HEADS, 96)
    nope = w3[:, :, :64].reshape(256, PAIRS, 128)
    pe = w3[:, :, 64:].reshape(256, PAIRS, 64)
    return jnp.concatenate([nope, pe, jnp.zeros((256, PAIRS, 64), w.dtype)], axis=2).reshape(256, PAIRS * 256)


def _uq_restore(g):
    g3 = g.reshape(256, PAIRS, 256)
    nope = g3[:, :, :128].reshape(256, HEADS, 64)
    pe = g3[:, :, 128:192].reshape(256, HEADS, 32)
    return jnp.concatenate([nope, pe], axis=2).reshape(256, HEADS * 96)


def _ukv_arrange(w):
    w3 = w.reshape(128, HEADS, 128)
    return jnp.concatenate([w3[:, :, :64].reshape(128, 1024), w3[:, :, 64:].reshape(128, 1024)], axis=1)


def _ukv_restore(g):
    kn = g[:, :1024].reshape(128, HEADS, 64)
    vv = g[:, 1024:].reshape(128, HEADS, 64)
    return jnp.concatenate([kn, vv], axis=2).reshape(128, HEADS * 128)


def _rope_tables(lp):
    r = np.arange(lp)
    pos = np.where(r < N_META, r, np.where(r >= PAD, r - PAD + N_META, 0)).astype(np.float32)
    half = MLA_ROPE // 2
    inv_freq = np.float32(ROPE_THETA) ** (-np.arange(half, dtype=np.float32) / np.float32(half))
    ang = (pos[:, None] * inv_freq[None, :]).astype(np.float32)
    cos, sin = np.cos(ang).astype(np.float32), np.sin(ang).astype(np.float32)
    one, zero = np.ones((lp, 64), np.float32), np.zeros((lp, 64), np.float32)
    return (jnp.asarray(np.concatenate([cos, cos, cos, cos, one], axis=1)),
            jnp.asarray(np.concatenate([-sin, sin, -sin, sin, zero], axis=1)))


def _pad_lanes(v, n=128):
    return jnp.pad(v, ((0, 0), (0, n - v.shape[1])))


def _in_cols(slabs, a, b):
    out = []
    for j in range(N_CHIPS):
        lo, hi = max(a, W_IN_SHARD * j), min(b, W_IN_SHARD * (j + 1))
        if lo < hi:
            out.append(slabs[j][:, lo - W_IN_SHARD * j:hi - W_IN_SHARD * j])
    return out


def _local_step(x2, tgt2, meta_f, w_small, w_attn, w_gate, w_uq_f, w_ukv_f, w_bm, w_bf, w_o, pre_norm_g,
                post_norm_g, mla_q_norm_g, mla_kv_norm_g, fox_forget_b, start_exchange=None):
    s_rows = x2.shape[0]
    lp = PAD + s_rows
    w_uq_a = _uq_arrange(w_uq_f)
    w_ukv_a = _ukv_arrange(w_ukv_f)

    ctab, stab = _rope_tables(lp)
    ii = jnp.arange(BLK)
    tri_lo = (ii[:, None] >= ii[None, :]).astype(BF16)
    tri_up = (ii[:, None] <= ii[None, :]).astype(BF16)
    fb128 = _pad_lanes(fox_forget_b)

    u = _rms_pre(x2, meta_f, pre_norm_g)
    small = _mm(u, w_small, mode="nn", out_dtype=F32, name="proj_small")
    attn = _mm(u, w_attn, mode="nn", out_dtype=BF16, name="proj_attn",
               col_scale=(HEADS * HEAD_DIM, FOX_SCALE * LOG2E))
    gate = _mm(u, w_gate, mode="nn", out_dtype=BF16, name="proj_gate")
    qn, kvn, kr, kb = _small_prep(small, mla_q_norm_g, mla_kv_norm_g, fb128, ctab, stab, tri_lo)
    qcat = _mm(qn, w_uq_a, mode="nn", out_dtype=BF16, name="mla_q", row_ins=(ctab, stab),
               epilogue=lambda tile, c, s: _rope_pairs(tile, c, s) * (MLA_SCALE * LOG2E))
    kv = _mm(kvn, w_ukv_a, mode="nn", out_dtype=BF16, name="mla_kv")

    mla_cols = dict(qcol=0, kcol=lambda p: p, vcol=lambda p: PAIRS + p)
    fox_cols = dict(qcol=0, kcol=lambda p: PAIRS + p, vcol=lambda p: 2 * PAIRS + p)
    o_mla, lse_mla = _attn_fwd(qcat, kv, kv, _transposed_cols(kv, 1, "mla_vt"), kr, rope=True, name="mla_fwd",
                               **mla_cols)
    o_fox, lse_fox = _attn_fwd(attn, attn, attn, _transposed_cols(attn, 2, "fox_vt"), kb, rope=False,
                               name="fox_fwd", **fox_cols)

    a_mla, a_fox = _gate_fwd(o_mla, o_fox, gate)
    y_mla = _mm(a_mla, w_bm, mode="nn", out_dtype=BF16, name="br_mla")
    y_fox = _mm(a_fox, w_bf, mode="nn", out_dtype=BF16, name="br_fox")
    mg = _merge_fwd(gate, y_mla, y_fox)
    mixed = _mm(mg, w_o, mode="nn", out_dtype=F32, name="out_proj")
    dmixed, dy, loss_p, dg_post = _tail(x2, mixed, tgt2, post_norm_g)

    d_w_out = _mm(mg, dmixed, mode="tn", out_dtype=F32, name="d_w_out")
    dm = _mm(dmixed, w_o, mode="nt", out_dtype=BF16, name="d_merge")
    dy_mla, dy_fox, dgate_ab = _merge_bwd(dm, gate, y_mla, y_fox)
    d_w_bm = _mm(a_mla, dy_mla, mode="tn", out_dtype=F32, name="d_w_br_mla")
    d_w_bf = _mm(a_fox, dy_fox, mode="tn", out_dtype=F32, name="d_w_br_fox")
    da_mla = _mm(dy_mla, w_bm, mode="nt", out_dtype=BF16, name="d_a_mla")
    da_fox = _mm(dy_fox, w_bf, mode="nt", out_dtype=BF16, name="d_a_fox")
    do_mla, do_fox, dgate_z, dl_mla, dl_fox = _gate_bwd(da_mla, da_fox, o_mla, o_fox, gate)
    dl_mla, dl_fox = (d[:, :HEADS].T.reshape(PAIRS, 2, lp) for d in (dl_mla, dl_fox))

    dq_a, dkn, dvm, dkr = _attn_bwd(qcat, kv, kv, kr, do_mla, dl_mla, lse_mla, rtabs=(ctab, stab),
                                    scale=MLA_SCALE, name="mla_bwd", **mla_cols)
    dfq, dfk, dfv, dcol, drow = _attn_bwd(attn, attn, attn, kb, do_fox, dl_fox, lse_fox, scale=FOX_SCALE,
                                          name="fox_bwd", **fox_cols)

    d_w_uq_a = _mm(qn, dq_a, mode="tn", out_dtype=F32, name="d_w_uq")
    dqn = _mm(dq_a, w_uq_a, mode="nt", out_dtype=F32, name="d_qn")
    d_w_ukv_a = jnp.concatenate([_mm(kvn, dkn, mode="tn", out_dtype=F32, name="d_w_uk"),
                                 _mm(kvn, dvm, mode="tn", out_dtype=F32, name="d_w_uv")], axis=1)
    dkvn = _mm(dkn, w_ukv_a[:, :1024], mode="nt", out_dtype=F32, name="d_kvn_k")
    dkvn = _mm(dvm, w_ukv_a[:, 1024:], mode="nt", out_dtype=F32, name="d_kvn_v", acc=dkvn)
    dsmall, dg_q, dg_kv, dfb = _small_bwd(small, dqn, dkvn, dkr, dcol, drow, mla_q_norm_g, mla_kv_norm_g,
                                          fb128, ctab, stab, tri_up)

    dw_small = _mm(u, dsmall, mode="tn", out_dtype=BF16, name="d_w_small")
    dw_fq = _mm(u, dfq, mode="tn", out_dtype=BF16, name="d_w_fq")
    dw_fk = _mm(u, dfk, mode="tn", out_dtype=BF16, name="d_w_fk")
    dw_fv = _mm(u, dfv, mode="tn", out_dtype=BF16, name="d_w_fv")
    dw_z = _mm(u, dgate_z, mode="tn", out_dtype=BF16, name="d_w_z")
    dw_g = _mm(u, dgate_ab, mode="tn", out_dtype=BF16, name="d_w_g")
    d_w_in = (dw_small, dw_z, dw_fq, dw_fk, dw_fv, dw_g)
    d_w_uq = _uq_restore(d_w_uq_a)
    d_w_ukv = _ukv_restore(d_w_ukv_a)
    token = start_exchange(d_w_in, d_w_uq, d_w_ukv, d_w_bm, d_w_bf, d_w_out) if start_exchange else None
    du = _mm_sum_nt([(dsmall, w_small), (dfq, w_attn[:, 0:1024]), (dfk, w_attn[:, 1024:2048]),
                     (dfv, w_attn[:, 2048:3072]), (dgate_z, w_gate[:, 0:2048]), (dgate_ab, w_gate[:, 2048:4096])],
                    name="d_u", after=token)
    dx, dmeta, dg_pre = _pre_bwd(du, x2, meta_f, dy, pre_norm_g)
    return (loss_p, dx, dmeta, d_w_in, d_w_uq, d_w_ukv, d_w_bm, d_w_bf, d_w_out, dg_pre, dg_post, dg_q, dg_kv, dfb)


def _w_in_slabs(pieces):
    dw_small, dw_z, dw_fq, dw_fk, dw_fv, dw_g = pieces
    runs = [(dw_small[:, 0:416], C_CQ), (dw_z[:, 0:1024], C_ZMLA), (dw_fq, C_FQ), (dw_fk, C_FK), (dw_fv, C_FV),
            (dw_small[:, 512:528], C_FL), (dw_z[:, 1024:2048], C_ZFOX), (dw_g, C_GA)]
    slabs = []
    for j in range(N_CHIPS):
        lo, hi = W_IN_SHARD * j, W_IN_SHARD * (j + 1)
        cols = [a[:, max(lo, c0) - c0:min(hi, c0 + a.shape[1]) - c0] for a, c0 in runs
                if max(lo, c0) < min(hi, c0 + a.shape[1])]
        slabs.append(jnp.concatenate(cols, axis=1))
    return jnp.stack(slabs, axis=0)


def kernel(x, meta_tokens, pre_norm_g, w_in, fox_forget_b, mla_q_norm_g, mla_kv_norm_g, w_uq, w_ukv, w_br_mla, w_br_fox, w_out, post_norm_g, loss_target, m_meta_tokens, m_pre_norm_g, m_w_in, m_fox_forget_b, m_mla_q_norm_g, m_mla_kv_norm_g, m_w_uq, m_w_ukv, m_w_br_mla, m_w_br_fox, m_w_out, m_post_norm_g, v_meta_tokens, v_pre_norm_g, v_w_in, v_fox_forget_b, v_mla_q_norm_g, v_mla_kv_norm_g, v_w_uq, v_w_ukv, v_w_br_mla, v_w_br_fox, v_w_out, v_post_norm_g):
    me = 2 * lax.axis_index("x") + lax.axis_index("y")
    core = lax.axis_index("c")
    w_in_b = w_in.astype(BF16).reshape(D_MODEL, W_IN_SHARD)
    p2 = _pack_p2(w_uq[0], w_ukv[0], w_br_mla[0], w_br_fox[0], w_out[0], BF16)
    w_in_g, meta_g = _gather_weights([w_in_b], meta_tokens)
    p2_g = _gather_late(lax.optimization_barrier((p2, w_in_g))[0])
    slabs = [jnp.where(me == j, w_in_b, w_in_g[j]) for j in range(N_CHIPS)]
    chip = lax.broadcasted_iota(jnp.int32, (N_CHIPS, 1, 1), 0)
    p2_all = jnp.where(chip == me, p2[None], p2_g)
    w_uq_f = p2_all[:, 0:96].reshape(N_CHIPS, 256, 384).transpose(1, 0, 2).reshape(256, 1536)
    w_ukv_f = p2_all[:, 96:160].reshape(N_CHIPS, 128, 512).transpose(1, 0, 2).reshape(128, 2048)
    w_bm, w_bf, w_o = (p2_all[:, lo:lo + 256].reshape(D_MODEL, D_MODEL) for lo in (160, 416, 672))
    meta_f = jnp.where(chip == me, meta_tokens[None], meta_g).transpose(1, 0, 2).reshape(N_META, D_MODEL)
    kpe = _in_cols(slabs, C_KPE, C_ZMLA)
    w_small = jnp.concatenate(_in_cols(slabs, C_CQ, C_KPE) + kpe + kpe + [jnp.zeros((D_MODEL, 64), BF16)]
                              + _in_cols(slabs, C_FL, C_ZFOX) + [jnp.zeros((D_MODEL, 112), BF16)], axis=1)
    w_attn = jnp.concatenate(_in_cols(slabs, C_FQ, C_FL), axis=1)
    w_gate = jnp.concatenate(_in_cols(slabs, C_ZMLA, C_FQ) + _in_cols(slabs, C_ZFOX, C_END), axis=1)

    exchange = {}

    def start_exchange(d_w_in, d_w_uq, d_w_ukv, d_w_bm, d_w_bf, d_w_out):
        g2 = jnp.concatenate(
            [d_w_uq.reshape(256, N_CHIPS, 384).transpose(1, 0, 2).reshape(N_CHIPS, 96, D_MODEL),
             d_w_ukv.reshape(128, N_CHIPS, 512).transpose(1, 0, 2).reshape(N_CHIPS, 64, D_MODEL)]
            + [g.reshape(N_CHIPS, 256, D_MODEL) for g in (d_w_bm, d_w_bf, d_w_out)], axis=1)
        pieces = [p[None] for p in d_w_in]
        from_sib = _swap_halves(pieces + [g2])
        halves = [_add_cores(p, s, "add_cores_" + nm)[0]
                  for p, s, nm in zip(pieces, from_sib, ("small", "z", "fq", "fk", "fv", "g"))]
        parts = [_w_in_slabs(halves), _add_cores(g2, from_sib[-1], "add_cores_rest")]
        exchange.update(parts=parts, landed=_scatter_chips(parts))
        return parts[0][0, 0:16, 0:LANES]

    (loss_p, dx, dmeta, _, _, _, _, _, _, dg_pre, dg_post, dg_q, dg_kv,
     dfb) = _local_step(x[0], loss_target[0], meta_f, w_small, w_attn, w_gate, w_uq_f, w_ukv_f, w_bm, w_bf, w_o,
                        pre_norm_g, post_norm_g, mla_q_norm_g, mla_kv_norm_g, fox_forget_b, start_exchange)

    mine = [_add_chips(l, lax.dynamic_index_in_dim(p, me, 0, keepdims=False), nm)
            for l, p, nm in zip(exchange["landed"], exchange["parts"], ("add_chips_w_in", "add_chips_rest"))]
    theirs = _swap_reduced(mine)
    g_w_in, g_p2 = [jnp.concatenate([jnp.where(core == 0, a, b), jnp.where(core == 0, b, a)], axis=0)
                    for a, b in zip(mine, theirs)]
    g_w_uq, g_w_ukv, g_w_bm, g_w_bf, g_w_out = _unpack_p2(g_p2)
    g_w_in = g_w_in[None]

    vec = jnp.concatenate([dg_pre.reshape(8, 128), dg_post.reshape(8, 128), dg_q.reshape(2, 128), dg_kv,
                           dfb, _pad_lanes(loss_p), jnp.zeros((3, 128), F32), dmeta.reshape(128, 128)], axis=0)
    tot = _allreduce_small(vec)
    loss = tot[20, 0]
    g_meta = lax.dynamic_slice_in_dim(tot[24:].reshape(N_META, D_MODEL), 256 * me, 256, axis=1)

    def small_pack(pre, post, gq_, gkv_, fb_):
        return jnp.concatenate([pre.reshape(8, 128), post.reshape(8, 128), gq_.reshape(2, 128), gkv_,
                                _pad_lanes(fb_), jnp.zeros((4, 128), F32)], axis=0)

    def small_unpack(t):
        return (t[0:8].reshape(1, 1024), t[8:16].reshape(1, 1024), t[16:18].reshape(1, 256), t[18:19],
                t[19:20, 0:HEADS])

    g_small = jnp.concatenate([tot[0:20], jnp.zeros((4, 128), F32)], axis=0)
    sm = _adamw(small_pack(pre_norm_g, post_norm_g, mla_q_norm_g, mla_kv_norm_g, fox_forget_b), g_small,
                small_pack(m_pre_norm_g, m_post_norm_g, m_mla_q_norm_g, m_mla_kv_norm_g, m_fox_forget_b),
                small_pack(v_pre_norm_g, v_post_norm_g, v_mla_q_norm_g, v_mla_kv_norm_g, v_fox_forget_b),
                "adamw_small")
    g_pre, g_post, g_q, g_kv, g_fb = small_unpack(g_small)
    (d_pre, d_post, d_q, d_kv, d_fb), (nm_pre, nm_post, nm_q, nm_kv, nm_fb), (nv_pre, nv_post, nv_q, nv_kv, nv_fb) = (
        small_unpack(t) for t in sm)

    d_meta, nm_meta, nv_meta = _adamw(meta_tokens, g_meta, m_meta_tokens, v_meta_tokens, "adamw_meta")
    d_win, nm_win, nv_win = (t.T[None] for t in _adamw(w_in[0].T, g_w_in[0].T, m_w_in[0].T, v_w_in[0].T,
                                                       "adamw_w_in"))
    d_wuq, nm_wuq, nv_wuq = _adamw(w_uq[0], g_w_uq, m_w_uq[0], v_w_uq[0], "adamw_w_uq")
    d_wukv, nm_wukv, nv_wukv = _adamw(w_ukv[0], g_w_ukv, m_w_ukv[0], v_w_ukv[0], "adamw_w_ukv")
    d_wbm, nm_wbm, nv_wbm = _adamw(w_br_mla[0], g_w_bm, m_w_br_mla[0], v_w_br_mla[0], "adamw_w_br_mla")
    d_wbf, nm_wbf, nv_wbf = _adamw(w_br_fox[0], g_w_bf, m_w_br_fox[0], v_w_br_fox[0], "adamw_w_br_fox")
    d_wo, nm_wo, nv_wo = _adamw(w_out[0], g_w_out, m_w_out[0], v_w_out[0], "adamw_w_out")

    def group(meta_, pre, win, fb_, q_, kv_, wuq, wukv, wbm, wbf, wo, post):
        return (meta_, pre, win, fb_, q_, kv_, wuq[None], wukv[None], wbm[None], wbf[None], wo[None], post)

    grads = group(g_meta, g_pre, g_w_in, g_fb, g_q, g_kv, g_w_uq, g_w_ukv, g_w_bm, g_w_bf, g_w_out, g_post)
    deltas = group(d_meta, d_pre, d_win, d_fb, d_q, d_kv, d_wuq, d_wukv, d_wbm, d_wbf, d_wo, d_post)
    new_m = group(nm_meta, nm_pre, nm_win, nm_fb, nm_q, nm_kv, nm_wuq, nm_wukv, nm_wbm, nm_wbf, nm_wo, nm_post)
    new_v = group(nv_meta, nv_pre, nv_win, nv_fb, nv_q, nv_kv, nv_wuq, nv_wukv, nv_wbm, nv_wbf, nv_wo, nv_post)
    return (loss, dx[None], *grads, *deltas, *new_m, *new_v)
```

```python
import math

import jax
import jax.numpy as jnp
import numpy as np
from jax import lax
from jax.experimental import pallas as pl
from jax.experimental.pallas import tpu as pltpu
from jax.experimental.pallas import tpu_sc as plsc

F32 = jnp.float32
BF16 = jnp.bfloat16

D_MODEL = 1024
N_META = 16
RMS_EPS = 1e-6
HEADS = 16
PAIRS = HEADS // 2
HEAD_DIM = 64
LANES = 128
MLA_ROPE = 32
AHEAD = 6
AHEAD_BWD = 1
BIAS_PARTS = 3
MLA_SCALE = 1.0 / math.sqrt(64 + 32)
FOX_SCALE = 1.0 / math.sqrt(64)
LOG2E = math.log2(math.e)
LN2 = math.log(2.0)
ROPE_THETA = 10000.0

PAD = 256
BLK = 256
QB = 512
UNROLL = 4
NEG = -1e30

C_CQ, C_CKV, C_KPE, C_ZMLA, C_FQ, C_FK, C_FV, C_FL, C_ZFOX, C_GA, C_GB, C_END = (
    0, 256, 384, 416, 1440, 2464, 3488, 4512, 4528, 5552, 6576, 7600)
SMALL_W = 640
W_IN_SHARD = 1900

P2_ROWS = 928
N_CHIPS = 4

ADAM_LR = 0.001
ADAM_B1 = 0.9
ADAM_B2 = 0.999
ADAM_EPS = 1e-08
ADAM_WD = 0.01
ADAM_STEP = 10

VMEM_BIG = 56 * 1024 * 1024
MM_VMEM_BUDGET = 44 * 1024 * 1024
MESH = pl.DeviceIdType.MESH


def _cp(dims, vmem=None):
    return pltpu.CompilerParams(dimension_semantics=dims, vmem_limit_bytes=vmem)


def _dot(a, b, ca, cb):
    return lax.dot_general(a, b, (((ca,), (cb,)), ((), ())), preferred_element_type=F32)


def _sigmoid(x):
    return 1.0 / (1.0 + jnp.exp(-x))


def _tile(n, cands):
    for c in cands:
        if n % c == 0:
            return c
    return n


def _mm(a, b, *, mode, out_dtype, name, acc=None, epilogue=None, row_ins=(), after=None, col_scale=None):
    if mode == "nn":
        (M, K), N = a.shape, b.shape[1]
    elif mode == "nt":
        (M, K), N = a.shape, b.shape[0]
    else:
        (K, M), N = a.shape, b.shape[1]
    tm = _tile(M, (1088, 1024)) if M > 1024 else M
    tn = _tile(N, (1024,)) if N > 1024 else N
    nk = 1
    while True:
        tk = K // nk
        need = 2 * tk * (tm * a.dtype.itemsize + tn * b.dtype.itemsize) + tm * tn * (
            2 * jnp.dtype(out_dtype).itemsize + (8 if acc is not None else 0) + (4 if nk > 1 else 0))
        if need <= MM_VMEM_BUDGET or (tk // 2) % (16 if mode == "tn" else LANES) or tk <= 512:
            break
        nk *= 2
    while (M // tm) * (N // tn) * nk < 4 and tn % 512 == 0:
        tn //= 2
    assert col_scale is None or (nk == 1 and col_scale[0] % tn == 0)
    ca, cb = {"nn": (1, 0), "nt": (1, 1), "tn": (0, 0)}[mode]
    a_spec = (pl.BlockSpec((tk, tm), lambda j, i, k: (k, i)) if mode == "tn"
              else pl.BlockSpec((tm, tk), lambda j, i, k: (i, k)))
    b_spec = (pl.BlockSpec((tn, tk), lambda j, i, k: (j, k)) if mode == "nt"
              else pl.BlockSpec((tk, tn), lambda j, i, k: (k, j)))
    o_spec = pl.BlockSpec((tm, tn), lambda j, i, k: (i, j))
    has_acc = acc is not None

    nrow = len(row_ins)

    def body(*refs):
        a_ref, b_ref = refs[0], refs[1]
        acc_ref = refs[2] if has_acc else None
        rows = refs[2 + has_acc:2 + has_acc + nrow]
        o_ref = refs[2 + has_acc + nrow + (after is not None)]

        def store(tile):
            if epilogue is not None:
                tile = epilogue(tile, *[r[...] for r in rows])
            if col_scale is not None:
                tile = tile * jnp.where(pl.program_id(0) * tn < col_scale[0], col_scale[1], 1.0)
            o_ref[...] = tile.astype(out_dtype)

        part = _dot(a_ref[...].astype(BF16), b_ref[...].astype(BF16), ca, cb)
        if nk == 1:
            store(part + acc_ref[...] if has_acc else part)
        else:
            sc = refs[-1]
            k = pl.program_id(2)

            @pl.when(k == 0)
            def _():
                sc[...] = part + acc_ref[...] if has_acc else part

            @pl.when(k > 0)
            def _():
                sc[...] += part

            @pl.when(k == nk - 1)
            def _():
                store(sc[...])

    ins = [a, b] + ([acc] if has_acc else []) + list(row_ins)
    in_specs = ([a_spec, b_spec] + ([o_spec] if has_acc else [])
                + [pl.BlockSpec((tm, r.shape[1]), lambda j, i, k: (i, 0)) for r in row_ins])
    if after is not None:
        ins.append(after)
        in_specs.append(pl.BlockSpec(after.shape, lambda j, i, k: (0,) * after.ndim))
    return pl.pallas_call(
        body, name=name, grid=(N // tn, M // tm, nk), in_specs=in_specs, out_specs=o_spec,
        out_shape=jax.ShapeDtypeStruct((M, N), out_dtype),
        scratch_shapes=[pltpu.VMEM((tm, tn), F32)] if nk > 1 else [],
        compiler_params=_cp(("parallel", "parallel", "arbitrary"), VMEM_BIG))(*ins)


def _mm_sum_nt(pairs, *, name, after=None):
    n = len(pairs)
    M, N = pairs[0][0].shape[0], pairs[0][1].shape[0]
    tm = _tile(M, (272,))

    def body(*refs):
        o_ref = refs[2 * n + (after is not None)]
        tot = _dot(refs[0][...].astype(BF16), refs[n][...].astype(BF16), 1, 1)
        for i in range(1, n):
            tot = tot + _dot(refs[i][...].astype(BF16), refs[n + i][...].astype(BF16), 1, 1)
        o_ref[...] = tot

    ins = [a for a, _ in pairs] + [b for _, b in pairs]
    in_specs = ([pl.BlockSpec((tm, a.shape[1]), lambda i: (i, 0)) for a, _ in pairs]
                + [pl.BlockSpec(b.shape, lambda i: (0, 0)) for _, b in pairs])
    if after is not None:
        ins.append(after)
        in_specs.append(pl.BlockSpec(after.shape, lambda i: (0,) * after.ndim))
    return pl.pallas_call(
        body, name=name, grid=(M // tm,), in_specs=in_specs, out_specs=pl.BlockSpec((tm, N), lambda i: (i, 0)),
        out_shape=jax.ShapeDtypeStruct((M, N), F32), compiler_params=_cp(("parallel",), VMEM_BIG))(*ins)


def _row(w):
    return pl.BlockSpec((BLK, w), lambda i: (i, 0))


def _rowc(w, c):
    return pl.BlockSpec((BLK, w), lambda i: (i, c))


def _full(shape):
    return pl.BlockSpec(shape, lambda i: tuple(0 for _ in shape))


def _rope(x, c, s):
    lane = lax.broadcasted_iota(jnp.int32, x.shape, 1)
    is_x1 = ((lane >> 4) & 1) == 0
    partner = jnp.where(is_x1, pltpu.roll(x, LANES - 16, 1), pltpu.roll(x, 16, 1))
    return x * c + partner * s


def _row_valid(i):
    rows = i * BLK + lax.broadcasted_iota(jnp.int32, (BLK, 1), 0)
    return (rows < N_META) | (rows >= PAD)


def _shift_rows(w):
    return pl.BlockSpec((BLK, w), lambda i: (jnp.maximum(i - 1, 0), 0))


def _h_block(i, x_ref, meta_ref):
    head = jnp.concatenate([meta_ref[...], jnp.zeros((BLK - N_META, D_MODEL), F32)], axis=0)
    return jnp.where(i == 0, head, x_ref[...])


def _rms_pre(x2, meta, g):
    lp = PAD + x2.shape[0]

    def body(x_ref, meta_ref, g_ref, u_ref):
        hv = _h_block(pl.program_id(0), x_ref, meta_ref)
        r = lax.rsqrt(jnp.mean(hv * hv, axis=-1, keepdims=True) + RMS_EPS)
        u_ref[...] = (hv * r * g_ref[...]).astype(BF16)

    return pl.pallas_call(
        body, name="rms_pre", grid=(lp // BLK,),
        in_specs=[_shift_rows(D_MODEL), _full((N_META, D_MODEL)), _full((1, D_MODEL))], out_specs=_row(D_MODEL),
        out_shape=jax.ShapeDtypeStruct((lp, D_MODEL), BF16),
        compiler_params=_cp(("parallel",)))(x2, meta, g)


def _split3(x):
    hi = x.astype(BF16)
    r1 = x - hi.astype(F32)
    mid = r1.astype(BF16)
    lo = (r1 - mid.astype(F32)).astype(BF16)
    return hi, mid, lo


def _small_prep(small, gq, gkv, fb, ctab, stab, tri):
    lp = small.shape[0]

    def body(sm_ref, gq_ref, gkv_ref, fb_ref, c_ref, s_ref, tri_ref, qn_ref, kvn_ref, kr_ref, kb_ref, carry):
        i = pl.program_id(0)

        @pl.when(i == 0)
        def _():
            carry[...] = jnp.zeros_like(carry)

        cq = sm_ref[:, 0:256]
        r = lax.rsqrt(jnp.mean(cq * cq, axis=-1, keepdims=True) + RMS_EPS)
        qn_ref[...] = (cq * r * gq_ref[...]).astype(BF16)
        ckv = sm_ref[:, 256:384]
        r = lax.rsqrt(jnp.mean(ckv * ckv, axis=-1, keepdims=True) + RMS_EPS)
        kvn_ref[...] = (ckv * r * gkv_ref[...]).astype(BF16)
        kr_ref[...] = _rope(sm_ref[:, 384:512], c_ref[...], s_ref[...]).astype(BF16)
        fl = sm_ref[:, 512:640] + fb_ref[...]
        lf = jnp.minimum(fl, 0.0) - jnp.log(1.0 + jnp.exp(-jnp.abs(fl)))
        lf = jnp.where(_row_valid(i), lf, 0.0)
        hi, mid, lo = _split3(lf)
        t = tri_ref[...]
        cum = (_dot(t, hi, 1, 0) + _dot(t, mid, 1, 0)) + _dot(t, lo, 1, 0) + carry[...]
        carry[...] = cum[BLK - 1:BLK, :]
        src = lax.broadcasted_iota(jnp.int32, (LANES, LANES), 0)
        dst = lax.broadcasted_iota(jnp.int32, (LANES, LANES), 1)
        kb = jnp.zeros((BLK, LANES), F32)
        for j, part in enumerate(_split3(-cum * LOG2E)):
            spread = ((dst == BIAS_PARTS * src + j) & (src < HEADS)).astype(BF16)
            kb = kb + _dot(part, spread, 1, 0)
        kb_ref[...] = kb.astype(BF16)

    return pl.pallas_call(
        body, name="small_prep", grid=(lp // BLK,),
        in_specs=[_row(SMALL_W), _full((1, 256)), _full((1, 128)), _full((1, 128)), _row(128), _row(128),
                  _full((BLK, BLK))],
        out_specs=[_row(256), _row(128), _row(128), _row(128)],
        out_shape=[jax.ShapeDtypeStruct((lp, 256), BF16), jax.ShapeDtypeStruct((lp, 128), BF16),
                   jax.ShapeDtypeStruct((lp, 128), BF16), jax.ShapeDtypeStruct((lp, 128), BF16)],
        scratch_shapes=[pltpu.VMEM((1, 128), F32)],
        compiler_params=_cp(("arbitrary",)))(small, gq, gkv, fb, ctab, stab, tri)


def _rope_pairs(tile, c, s):
    out = []
    for lo in range(0, tile.shape[1], 256):
        out += [tile[:, lo:lo + 128], _rope(tile[:, lo + 128:lo + 256], c, s)]
    return jnp.concatenate(out, axis=1)


def _gate_fwd(o_mla, o_fox, gate):
    lp = o_mla.shape[0]

    def body(om_ref, of_ref, zm_ref, zf_ref, am_ref, af_ref):
        zm = zm_ref[...].astype(F32)
        am_ref[...] = (om_ref[...] * (zm * _sigmoid(zm))).astype(BF16)
        zf = zf_ref[...].astype(F32)
        af_ref[...] = (of_ref[...] * (zf * _sigmoid(zf))).astype(BF16)

    return pl.pallas_call(
        body, name="gate_fwd", grid=(lp // BLK,),
        in_specs=[_row(D_MODEL), _row(D_MODEL), _rowc(D_MODEL, 0), _rowc(D_MODEL, 1)],
        out_specs=[_row(D_MODEL), _row(D_MODEL)],
        out_shape=[jax.ShapeDtypeStruct((lp, D_MODEL), BF16)] * 2,
        compiler_params=_cp(("parallel",)))(o_mla, o_fox, gate, gate)


def _merge_fwd(gate, y_mla, y_fox):
    lp = y_mla.shape[0]

    def body(ga_ref, gb_ref, ym_ref, yf_ref, m_ref):
        sa = _sigmoid(ga_ref[...].astype(F32))
        sb = _sigmoid(gb_ref[...].astype(F32))
        m_ref[...] = (sa * ym_ref[...] + sb * yf_ref[...]).astype(BF16)

    return pl.pallas_call(
        body, name="merge_fwd", grid=(lp // BLK,),
        in_specs=[_rowc(D_MODEL, 2), _rowc(D_MODEL, 3), _row(D_MODEL), _row(D_MODEL)],
        out_specs=_row(D_MODEL), out_shape=jax.ShapeDtypeStruct((lp, D_MODEL), BF16),
        compiler_params=_cp(("parallel",)))(gate, gate, y_mla, y_fox)


def _tail(x2, mixed, tgt, gpost):
    lp = mixed.shape[0]
    shift = _shift_rows(D_MODEL)

    def body(h_ref, mx_ref, t_ref, g_ref, dmx_ref, dy_ref, loss_ref, dg_ref):
        i = pl.program_id(0)

        @pl.when(i == 0)
        def _():
            loss_ref[...] = jnp.zeros_like(loss_ref)
            dg_ref[...] = jnp.zeros_like(dg_ref)
            dmx_ref[...] = jnp.zeros_like(dmx_ref)
            dy_ref[...] = jnp.zeros_like(dy_ref)

        @pl.when(i > 0)
        def _():
            mx = mx_ref[...]
            g = g_ref[...]
            r = lax.rsqrt(jnp.mean(mx * mx, axis=-1, keepdims=True) + RMS_EPS)
            nrm = mx * r
            e = (h_ref[...] + nrm * g) - t_ref[...]
            loss_ref[...] += jnp.sum(0.5 * jnp.sum(e * e, axis=-1, keepdims=True) * (1.0 / D_MODEL),
                                     axis=0, keepdims=True)
            dy = e * (1.0 / D_MODEL)
            dy_ref[...] = dy
            dg_ref[...] += jnp.sum(dy * nrm, axis=0, keepdims=True)
            w = dy * g
            dot = jnp.mean(w * mx, axis=-1, keepdims=True)
            dmx_ref[...] = (r * w - mx * (r * r * r * dot)).astype(BF16)

    return pl.pallas_call(
        body, name="tail", grid=(lp // BLK,),
        in_specs=[shift, _row(D_MODEL), shift, _full((1, D_MODEL))],
        out_specs=[_row(D_MODEL), _row(D_MODEL), _full((1, 1)), _full((1, D_MODEL))],
        out_shape=[jax.ShapeDtypeStruct((lp, D_MODEL), BF16), jax.ShapeDtypeStruct((lp, D_MODEL), F32),
                   jax.ShapeDtypeStruct((1, 1), F32), jax.ShapeDtypeStruct((1, D_MODEL), F32)],
        compiler_params=_cp(("arbitrary",)))(x2, mixed, tgt, gpost)


def _merge_bwd(dm, gate, y_mla, y_fox):
    lp = dm.shape[0]

    def body(dm_ref, ga_ref, gb_ref, ym_ref, yf_ref, dym_ref, dyf_ref, dg_ref):
        dm_v = dm_ref[...].astype(F32)
        sa = _sigmoid(ga_ref[...].astype(F32))
        sb = _sigmoid(gb_ref[...].astype(F32))
        dym_ref[...] = (dm_v * sa).astype(BF16)
        dyf_ref[...] = (dm_v * sb).astype(BF16)
        dg_ref[:, 0:D_MODEL] = (dm_v * ym_ref[...] * (sa * (1.0 - sa))).astype(BF16)
        dg_ref[:, D_MODEL:2 * D_MODEL] = (dm_v * yf_ref[...] * (sb * (1.0 - sb))).astype(BF16)

    return pl.pallas_call(
        body, name="merge_bwd", grid=(lp // BLK,),
        in_specs=[_row(D_MODEL), _rowc(D_MODEL, 2), _rowc(D_MODEL, 3), _row(D_MODEL), _row(D_MODEL)],
        out_specs=[_row(D_MODEL), _row(D_MODEL), _row(2 * D_MODEL)],
        out_shape=[jax.ShapeDtypeStruct((lp, D_MODEL), BF16), jax.ShapeDtypeStruct((lp, D_MODEL), BF16),
                   jax.ShapeDtypeStruct((lp, 2 * D_MODEL), BF16)],
        compiler_params=_cp(("parallel",)))(dm, gate, gate, y_mla, y_fox)


def _gate_bwd(da_mla, da_fox, o_mla, o_fox, gate):
    lp = da_mla.shape[0]

    def one(da, o, z, head_of_col):
        sg = _sigmoid(z)
        do = (da * (z * sg)).astype(BF16)
        dz = da * o * (sg * (1.0 + z * (1.0 - sg)))
        delta = sum(_dot(part, head_of_col, 1, 0) for part in _split3(do.astype(F32) * o))
        return do, dz.astype(BF16), delta

    def body(dam_ref, daf_ref, om_ref, of_ref, zm_ref, zf_ref, dom_ref, dof_ref, dz_ref, dlm_ref, dlf_ref):
        f32 = lambda r: r[...].astype(F32)
        head_of_col = (lax.broadcasted_iota(jnp.int32, (D_MODEL, LANES), 0) // HEAD_DIM
                       == lax.broadcasted_iota(jnp.int32, (D_MODEL, LANES), 1)).astype(BF16)
        dom_ref[...], dz_ref[:, 0:D_MODEL], dlm_ref[...] = one(f32(dam_ref), f32(om_ref), f32(zm_ref), head_of_col)
        dof_ref[...], dz_ref[:, D_MODEL:2 * D_MODEL], dlf_ref[...] = one(f32(daf_ref), f32(of_ref), f32(zf_ref),
                                                                        head_of_col)

    return pl.pallas_call(
        body, name="gate_bwd", grid=(lp // BLK,),
        in_specs=[_row(D_MODEL)] * 4 + [_rowc(D_MODEL, 0), _rowc(D_MODEL, 1)],
        out_specs=[_row(D_MODEL), _row(D_MODEL), _row(2 * D_MODEL), _row(LANES), _row(LANES)],
        out_shape=[jax.ShapeDtypeStruct((lp, D_MODEL), BF16), jax.ShapeDtypeStruct((lp, D_MODEL), BF16),
                   jax.ShapeDtypeStruct((lp, 2 * D_MODEL), BF16), jax.ShapeDtypeStruct((lp, LANES), F32),
                   jax.ShapeDtypeStruct((lp, LANES), F32)],
        compiler_params=_cp(("parallel",)))(da_mla, da_fox, o_mla, o_fox, gate, gate)


def _small_bwd(small, dqn, dkvn, dkr, dcol_t, drow_t, gq, gkv, fb, ctab, stab, triu):
    lp = small.shape[0]
    nb = lp // BLK

    def rrow(w):
        return pl.BlockSpec((BLK, w), lambda i: (nb - 1 - i, 0))

    def body(sm_ref, dqn_ref, dkvn_ref, dkr_ref, dcol_ref, drow_ref, gq_ref, gkv_ref, fb_ref, c_ref, s_ref, tri_ref,
             ds_ref, dgq_ref, dgkv_ref, dfb_ref, carry):
        i = pl.program_id(0)

        @pl.when(i == 0)
        def _():
            carry[...] = jnp.zeros_like(carry)
            dgq_ref[...] = jnp.zeros_like(dgq_ref)
            dgkv_ref[...] = jnp.zeros_like(dgkv_ref)
            dfb_ref[...] = jnp.zeros_like(dfb_ref)

        def norm_bwd(x, dn, g, dg_ref):
            r = lax.rsqrt(jnp.mean(x * x, axis=-1, keepdims=True) + RMS_EPS)
            dg_ref[...] += jnp.sum(dn * (x * r), axis=0, keepdims=True)
            w = dn * g
            dot = jnp.mean(w * x, axis=-1, keepdims=True)
            return r * w - x * (r * r * r * dot)

        ds_ref[:, 0:256] = norm_bwd(sm_ref[:, 0:256], dqn_ref[...], gq_ref[...], dgq_ref).astype(BF16)
        ds_ref[:, 256:384] = norm_bwd(sm_ref[:, 256:384], dkvn_ref[...], gkv_ref[...], dgkv_ref).astype(BF16)

        dk = dkr_ref[0]
        for p in range(1, PAIRS):
            dk = dk + dkr_ref[p]
        dk = _rope(dk, c_ref[...], -s_ref[...])
        lane = lax.broadcasted_iota(jnp.int32, dk.shape, 1)
        dk = jnp.where(lane < MLA_ROPE, dk + pltpu.roll(dk, LANES - MLA_ROPE, 1), 0.0)
        ds_ref[:, 384:512] = dk.astype(BF16)

        dcol = dcol_ref[0]
        for p in range(1, PAIRS):
            dcol = dcol + pltpu.roll(dcol_ref[p], 2 * p, 1)
        rows16 = jnp.concatenate([drow_ref[p, h:h + 1, :] for p in range(PAIRS) for h in range(2)], axis=0)
        eye = (lax.broadcasted_iota(jnp.int32, (HEADS, LANES), 0)
               == lax.broadcasted_iota(jnp.int32, (HEADS, LANES), 1)).astype(BF16)
        drow = sum(_dot(part, eye, 0, 0) for part in _split3(rows16))
        dcr = dcol - drow
        hi, mid, lo = _split3(dcr)
        t = tri_ref[...]
        suf = (_dot(t, hi, 1, 0) + _dot(t, mid, 1, 0)) + _dot(t, lo, 1, 0) + carry[...]
        fl = sm_ref[:, 512:640] + fb_ref[...]
        dfl = jnp.where(_row_valid(nb - 1 - i), -suf * _sigmoid(-fl), 0.0)
        ds_ref[:, 512:640] = dfl.astype(BF16)
        dfb_ref[...] += jnp.sum(dfl, axis=0, keepdims=True)
        carry[...] += jnp.sum(dcr, axis=0, keepdims=True)

    return pl.pallas_call(
        body, name="small_bwd", grid=(nb,),
        in_specs=[rrow(SMALL_W), rrow(256), rrow(128),
                  pl.BlockSpec((PAIRS, BLK, 128), lambda i: (0, nb - 1 - i, 0)),
                  pl.BlockSpec((PAIRS, BLK, 128), lambda i: (0, nb - 1 - i, 0)),
                  pl.BlockSpec((PAIRS, 2, BLK), lambda i: (0, 0, nb - 1 - i)),
                  _full((1, 256)), _full((1, 128)), _full((1, 128)), rrow(128), rrow(128), _full((BLK, BLK))],
        out_specs=[rrow(SMALL_W), _full((1, 256)), _full((1, 128)), _full((1, 128))],
        out_shape=[jax.ShapeDtypeStruct((lp, SMALL_W), BF16), jax.ShapeDtypeStruct((1, 256), F32),
                   jax.ShapeDtypeStruct((1, 128), F32), jax.ShapeDtypeStruct((1, 128), F32)],
        scratch_shapes=[pltpu.VMEM((1, 128), F32)],
        compiler_params=_cp(("arbitrary",)))(small, dqn, dkvn, dkr, dcol_t, drow_t, gq, gkv, fb, ctab, stab, triu)


def _pre_bwd(du, x2, meta, dy, gpre):
    s_rows = x2.shape[0]
    lp = PAD + s_rows
    shift = _shift_rows(D_MODEL)

    def body(du_ref, x_ref, meta_ref, dy_ref, g_ref, dx_ref, dmeta_ref, dg_ref):
        i = pl.program_id(0)

        @pl.when(i == 0)
        def _():
            dg_ref[...] = jnp.zeros_like(dg_ref)

        hv = _h_block(i, x_ref, meta_ref)
        duv = du_ref[...]
        r = lax.rsqrt(jnp.mean(hv * hv, axis=-1, keepdims=True) + RMS_EPS)
        dg_ref[...] += jnp.sum(duv * (hv * r), axis=0, keepdims=True)
        w = duv * g_ref[...]
        dot = jnp.mean(w * hv, axis=-1, keepdims=True)
        dh = dy_ref[...] + (r * w - hv * (r * r * r * dot))
        dx_ref[...] = dh

        @pl.when(i == 0)
        def _():
            dmeta_ref[...] = dh[0:N_META, :]

    return pl.pallas_call(
        body, name="pre_bwd", grid=(lp // BLK,),
        in_specs=[_row(D_MODEL), shift, _full((N_META, D_MODEL)), _row(D_MODEL), _full((1, D_MODEL))],
        out_specs=[shift, _full((N_META, D_MODEL)), _full((1, D_MODEL))],
        out_shape=[jax.ShapeDtypeStruct((s_rows, D_MODEL), F32), jax.ShapeDtypeStruct((N_META, D_MODEL), F32),
                   jax.ShapeDtypeStruct((1, D_MODEL), F32)],
        compiler_params=_cp(("arbitrary",)))(du, x2, meta, dy, gpre)


def _pair_masks(rope, pair):
    lane = lax.broadcasted_iota(jnp.int32, (1, LANES), 1)
    mas = [lane < HEAD_DIM, lane >= HEAD_DIM]
    wide = lax.broadcasted_iota(jnp.int32, (1, 2 * LANES), 1)
    extra = MLA_ROPE if rope else BIAS_PARTS
    lo = LANES if rope else LANES + 2 * BIAS_PARTS * pair
    mid = lo + extra
    return mas, [(wide < HEAD_DIM) | ((wide >= lo) & (wide < mid)),
                 ((wide >= HEAD_DIM) & (wide < LANES)) | ((wide >= mid) & (wide < mid + extra))]


def _mask2(x, masks):
    return [jnp.where(m, x, jnp.zeros_like(x)) for m in masks]


def _q_heads(q_rows, rope, mas, hmask):
    if rope:
        return _mask2(q_rows, hmask)
    zero = jnp.zeros((q_rows.shape[0], LANES), BF16)
    return [jnp.concatenate([jnp.where(m, q_rows, zero), jnp.where(hm[:, LANES:], zero + 1, zero)], axis=1)
            for m, hm in zip(mas, hmask)]


def _transposed_cols(x, group, name):
    lp = x.shape[0]

    def body(x_ref, o_ref):
        o_ref[...] = x_ref[...].T

    cols = 2 * BLK
    per_group = D_MODEL // cols
    return pl.pallas_call(
        body, name=name, grid=(per_group,),
        in_specs=[pl.BlockSpec((lp, cols), lambda i: (0, per_group * group + i))],
        out_specs=pl.BlockSpec((cols, lp), lambda i: (i, 0)),
        out_shape=jax.ShapeDtypeStruct((D_MODEL, lp), x.dtype),
        compiler_params=_cp(("parallel",), MM_VMEM_BUDGET))(x)


def _attn_fwd(q, k, v, vt, k2, *, rope, qcol, kcol, vcol, name):
    lp = q.shape[0]
    nq = 1 + (lp - PAD) // QB
    qw = 256 if rope else 128

    def body(q_ref, k_ref, v_ref, vt_ref, k2_ref, o_ref, lse_ref):
        i = pl.program_id(1)
        r0 = pl.multiple_of(jnp.where(i == 0, 0, PAD + QB * (i - 1)), BLK)
        b0 = r0 // BLK
        mas, hmask = _pair_masks(rope, pl.program_id(0))
        qh = _q_heads(q_ref[pl.ds(r0, QB), :], rope, mas, hmask)

        def update(chunks, tiles, groups):
            m = [[cr[0], cr[2]] for _, _, cr in groups]
            l = [[cr[1], cr[3]] for _, _, cr in groups]
            acc = [[cr[4][0:HEAD_DIM], cr[4][HEAD_DIM:LANES]] for _, _, cr in groups]
            qs = [[x[q_lo:q_lo + wq] for x in qh] for q_lo, wq, _ in groups]
            k0s = [pl.multiple_of(kc * BLK, BLK) for kc, _ in chunks]
            kks = [jnp.concatenate([k_ref[pl.ds(k0, n), :], k2_ref[pl.ds(k0, n), :]], axis=1)
                   for k0, (_, n) in zip(k0s, chunks)]
            jobs = [(g, ci, mask, h) for g, ci, mask in tiles for h in range(2)]
            score = lambda t: _dot(kks[jobs[t][1]], qs[jobs[t][0]][jobs[t][3]], 1, 1)
            ss = [score(t) for t in range(min(AHEAD, len(jobs)))]
            for t, (g, ci, mask, h) in enumerate(jobs):
                if t + AHEAD < len(jobs):
                    ss.append(score(t + AHEAD))
                s = ss[t] if mask is None else jnp.where(mask, ss[t], NEG)
                m_new = jnp.maximum(m[g][h], jnp.max(s, axis=0, keepdims=True))
                alpha = jnp.exp2(m[g][h] - m_new)
                p = jnp.exp2(s - m_new)
                l[g][h] = alpha * l[g][h] + jnp.sum(p, axis=0, keepdims=True)
                m[g][h] = m_new
                n = chunks[ci][1]
                if n == BLK:
                    pv = _dot(vt_ref[pl.ds(HEAD_DIM * h, HEAD_DIM), pl.ds(k0s[ci], BLK)], p.astype(BF16), 1, 0)
                else:
                    vm = jnp.where(mas[h], v_ref[0:n, :], jnp.zeros((), BF16))
                    pv = _dot(vm, p.astype(BF16), 0, 0)[HEAD_DIM * h:HEAD_DIM * (h + 1)]
                acc[g][h] = alpha * acc[g][h] + pv
            return [(m[g][0], l[g][0], m[g][1], l[g][1], jnp.concatenate(acc[g], axis=0)) for g in range(len(groups))]

        def full_chunks(kcs, carry):
            return update([(kc, BLK) for kc in kcs], [(0, ci, None) for ci in range(len(kcs))], [(0, QB, carry)])[0]

        neg = jnp.full((1, QB), NEG, F32)
        zero = jnp.zeros((1, QB), F32)
        c = (neg, zero, neg, zero, jnp.zeros((LANES, QB), F32))
        n_mid = jnp.maximum(b0 - 1, 0)
        c = lax.fori_loop(0, n_mid // 4, lambda t, cr: full_chunks([4 * t + u for u in (1, 2, 3, 4)], cr), c)
        c = lax.fori_loop(0, (n_mid % 4) // 2, lambda t, cr: full_chunks([n_mid - 1, n_mid], cr), c)
        key_l = lax.broadcasted_iota(jnp.int32, (BLK, BLK), 0)
        qry_l = lax.broadcasted_iota(jnp.int32, (BLK, BLK), 1)
        tri = (key_l <= qry_l) & (b0 > 0)
        meta_ok = (key_l[0:N_META] <= qry_l[0:N_META]) | (b0 > 0)
        lo, hi = update([(0, N_META), (b0, BLK), (b0 + 1, BLK)],
                        [(0, 0, meta_ok), (1, 0, None), (0, 1, tri), (1, 1, None), (1, 2, tri)],
                        [(0, BLK, tuple(a[:, 0:BLK] for a in c)), (BLK, QB - BLK, tuple(a[:, BLK:QB] for a in c))])
        c = tuple(jnp.concatenate([a, b], axis=1) for a, b in zip(lo, hi))
        inv =jnp.concatenate([jnp.broadcast_to(1.0 / c[1], (HEAD_DIM, QB)),
                               jnp.broadcast_to(1.0 / c[3], (HEAD_DIM, QB))], axis=0)
        o_t = (c[4] * inv).T.astype(BF16)
        lses = [c[2 * h] + jnp.log(c[2 * h + 1]) * LOG2E for h in range(2)]
        o_ref[pl.ds(r0, BLK), :] = o_t[0:BLK]
        for h in range(2):
            lse_ref[0, h:h + 1, pl.ds(r0, BLK)] = lses[h][:, 0:BLK]

        @pl.when(i > 0)
        def _():
            r1 = pl.multiple_of(r0 + BLK, BLK)
            o_ref[pl.ds(r1, QB - BLK), :] = o_t[BLK:QB]
            for h in range(2):
                lse_ref[0, h:h + 1, pl.ds(r1, QB - BLK)] = lses[h][:, BLK:QB]

    in_specs = [pl.BlockSpec((lp, qw), lambda p, i: (0, qcol + p)),
                pl.BlockSpec((lp, 128), lambda p, i: (0, kcol(p))),
                pl.BlockSpec((BLK, 128), lambda p, i: (0, vcol(p))),
                pl.BlockSpec((128, lp), lambda p, i: (p, 0)),
                pl.BlockSpec((lp, 128), lambda p, i: (0, 0))]
    return pl.pallas_call(
        body, name=name, grid=(PAIRS, nq), in_specs=in_specs,
        out_specs=[pl.BlockSpec((lp, 128), lambda p, i: (0, p)),
                   pl.BlockSpec((1, 2, lp), lambda p, i: (p, 0, 0))],
        out_shape=[jax.ShapeDtypeStruct((lp, D_MODEL), BF16), jax.ShapeDtypeStruct((PAIRS, 2, lp), F32)],
        compiler_params=_cp(("parallel", "arbitrary"), VMEM_BIG))(q, k, v, vt, k2)


def _attn_bwd(q, k, v, k2, do, delta, lse, *, rtabs=None, scale, qcol, kcol, vcol, name):
    lp = q.shape[0]
    nb = lp // BLK
    rope = rtabs is not None
    bias = not rope
    qw = 256 if rope else 128

    def body(*refs):
        it = iter(refs)
        q_ref, k_ref, v_ref, k2_ref = next(it), next(it), next(it), next(it)
        do_ref, dl_ref, lse_ref = next(it), next(it), next(it)
        ct_ref, st_ref = (next(it), next(it)) if rope else (None, None)
        dq_out, dk_ref, dv_ref = next(it), next(it), next(it)
        x_ref = next(it)
        drow_ref = next(it) if bias else None
        dq_ref = next(it)
        kb = pl.program_id(1)
        mas, hmask = _pair_masks(rope, pl.program_id(0))
        lane = lax.broadcasted_iota(jnp.int32, (1, LANES), 1)

        @pl.when(kb == 0)
        def _():
            dq_ref[...] = jnp.zeros_like(dq_ref)
            if bias:
                drow_ref[...] = jnp.zeros_like(drow_ref)

        def key_pass(n, w):
            kk = jnp.concatenate([k_ref[0:n, :], k2_ref[0:n, :]], axis=1)
            vh = _mask2(v_ref[0:n, :], mas)
            kcat = jnp.concatenate([x[:, 0:qw] for x in _mask2(kk, hmask)], axis=0)
            diag_mask = (lax.broadcasted_iota(jnp.int32, (n, w), 0) <= lax.broadcasted_iota(jnp.int32, (n, w), 1))

            def front(qc):
                q0 = qc * w if isinstance(qc, int) else pl.multiple_of(qc * w, w)
                dov = do_ref[pl.ds(q0, w), :]
                qh = _q_heads(q_ref[pl.ds(q0, w), :], rope, mas, hmask)
                ss, dps = [], []
                for h in range(2):
                    ss.append(_dot(kk, qh[h], 1, 1))
                    dps.append(_dot(vh[h], dov, 1, 1))
                return q0, dov, qh, ss, dps

            def back(fronted, carry, mask):
                carry = list(carry)
                q0, dov, qh, ss, dps = fronted
                doh = _mask2(dov, mas)
                pbs, dss = [], []
                for h in range(2):
                    p = jnp.exp2(ss[h] - lse_ref[0, h:h + 1, pl.ds(q0, w)])
                    if mask is not None:
                        p = jnp.where(mask, p, 0.0)
                    ds = p * (dps[h] - dl_ref[0, h:h + 1, pl.ds(q0, w)])
                    if bias:
                        drow_ref[0, h:h + 1, pl.ds(q0, w)] += jnp.sum(ds, axis=0, keepdims=True)
                        carry[2 + h] = carry[2 + h] + jnp.sum(ds, axis=1, keepdims=True)
                    pbs.append(p.astype(BF16))
                    dss.append(ds.astype(BF16))
                ds_lanes = jnp.concatenate(dss, axis=1)
                ds_rows = jnp.concatenate(dss, axis=0)
                qcat = jnp.concatenate([x[:, 0:qw] for x in qh], axis=0)
                carry[0] = carry[0] + _dot(ds_lanes, qcat, 1, 0)
                carry[1] = carry[1] + _dot(jnp.concatenate(pbs, axis=1), jnp.concatenate(doh, axis=0), 1, 0)
                dq_ref[pl.ds(q0, w), :] += _dot(ds_rows, kcat, 0, 0)
                return tuple(carry)

            def chunks(qcs, carry, masks):
                ahead = AHEAD_BWD
                fronted = [front(qc) for qc in qcs[:ahead]]
                for u, mask in enumerate(masks):
                    if u + ahead < len(qcs):
                        fronted.append(front(qcs[u + ahead]))
                    carry = back(fronted[u], carry, mask)
                return carry

            c = [jnp.zeros((n, qw), F32), jnp.zeros((n, LANES), F32)]
            if bias:
                c += [jnp.zeros((n, 1), F32), jnp.zeros((n, 1), F32)]
            c = tuple(c)
            if w != BLK:
                c = chunks(list(range(lp // w)), c, [diag_mask] + [None] * (lp // w - 1))
            else:
                groups = (nb - kb) // UNROLL

                def several(t, cr):
                    return chunks([kb + UNROLL * t + u for u in range(UNROLL)], cr,
                                  [diag_mask | (t > 0)] + [None] * (UNROLL - 1))

                c = lax.fori_loop(0, groups, several, c)
                start = kb + UNROLL * groups
                pairs = (nb - start) // 2

                def two(t, cr):
                    qc = start + 2 * t
                    return chunks([qc, qc + 1], cr, [diag_mask | (qc > kb), None])

                c = lax.fori_loop(0, pairs, two, c)
                c = lax.fori_loop(start + 2 * pairs, nb, lambda qc, cr: chunks([qc], cr, [diag_mask | (qc > kb)]), c)

            def rows(a, dtype):
                a = a.astype(dtype)
                return a if n == BLK else jnp.concatenate([a, jnp.zeros((BLK - n, a.shape[1]), dtype)], axis=0)

            dk = c[0] * LN2
            dk_ref[...] = rows(dk[:, 0:LANES], BF16)
            dv_ref[...] = rows(c[1], BF16)
            if rope:
                x_ref[0] = rows(dk[:, LANES:2 * LANES], F32)
            if bias:
                x_ref[0] = rows(jnp.where(lane == 0, c[2], jnp.where(lane == 1, c[3], 0.0)), F32)

        @pl.when(kb == 0)
        def _():
            key_pass(N_META, lp // 2)

        @pl.when(kb > 0)
        def _():
            key_pass(BLK, BLK)

        @pl.when(kb == nb - 1)
        def _():
            def fin(c, carry):
                r0 = pl.multiple_of(c * BLK, BLK)
                dq = dq_ref[pl.ds(r0, BLK), :] * scale
                if rope:
                    back = _rope(dq[:, LANES:2 * LANES], ct_ref[pl.ds(r0, BLK), :], -st_ref[pl.ds(r0, BLK), :])
                    dq = jnp.concatenate([dq[:, 0:LANES], back], axis=1)
                dq_out[pl.ds(r0, BLK), :] = dq.astype(BF16)
                return carry

            lax.fori_loop(0, nb, fin, 0)

    in_specs = [pl.BlockSpec((lp, qw), lambda p, j: (0, qcol + p)),
                pl.BlockSpec((BLK, 128), lambda p, j: (j, kcol(p))),
                pl.BlockSpec((BLK, 128), lambda p, j: (j, vcol(p))),
                pl.BlockSpec((BLK, 128), lambda p, j: (j, 0)),
                pl.BlockSpec((lp, 128), lambda p, j: (0, p)), pl.BlockSpec((1, 2, lp), lambda p, j: (p, 0, 0)),
                pl.BlockSpec((1, 2, lp), lambda p, j: (p, 0, 0))]
    ins = [q, k, v, k2, do, delta, lse]
    if rope:
        in_specs += [pl.BlockSpec((lp, 128), lambda p, j: (0, 0))] * 2
        ins += list(rtabs)
    out_specs = [pl.BlockSpec((lp, qw), lambda p, j: (0, p)),
                 pl.BlockSpec((BLK, 128), lambda p, j: (j, p)),
                 pl.BlockSpec((BLK, 128), lambda p, j: (j, p)),
                 pl.BlockSpec((1, BLK, 128), lambda p, j: (p, j, 0))]
    out_shape = [jax.ShapeDtypeStruct((lp, PAIRS * qw), BF16), jax.ShapeDtypeStruct((lp, D_MODEL), BF16),
                 jax.ShapeDtypeStruct((lp, D_MODEL), BF16), jax.ShapeDtypeStruct((PAIRS, lp, 128), F32)]
    if bias:
        out_specs.append(pl.BlockSpec((1, 2, lp), lambda p, j: (p, 0, 0)))
        out_shape.append(jax.ShapeDtypeStruct((PAIRS, 2, lp), F32))
    return pl.pallas_call(
        body, name=name, grid=(PAIRS, nb), in_specs=in_specs, out_specs=out_specs, out_shape=out_shape,
        scratch_shapes=[pltpu.VMEM((lp, qw), F32)],
        compiler_params=_cp(("parallel", "arbitrary"), VMEM_BIG))(*ins)


def _adamw(w, g, m, v, name):
    lead = w.ndim - 2
    rows, cols = w.shape[lead:]
    big = rows * cols > 512 * 1024
    tr = 128 if big and rows % 128 == 0 else rows
    tc = 256 if big and tr == rows else cols

    def body(w_ref, g_ref, m_ref, v_ref, d_ref, nm_ref, nv_ref):
        gv = g_ref[...]
        nm = ADAM_B1 * m_ref[...] + (1.0 - ADAM_B1) * gv
        nv = ADAM_B2 * v_ref[...] + (1.0 - ADAM_B2) * (gv * gv)
        m_hat = nm / (1.0 - ADAM_B1 ** ADAM_STEP)
        v_hat = nv / (1.0 - ADAM_B2 ** ADAM_STEP)
        d_ref[...] = -ADAM_LR * (m_hat / (jnp.sqrt(v_hat) + ADAM_EPS) + ADAM_WD * w_ref[...])
        nm_ref[...] = nm
        nv_ref[...] = nv

    spec = pl.BlockSpec((1,) * lead + (tr, tc), lambda i, j: (0,) * lead + (i, j))
    return pl.pallas_call(
        body, name=name, grid=(rows // tr, cols // tc), in_specs=[spec] * 4, out_specs=[spec] * 3,
        out_shape=[jax.ShapeDtypeStruct(w.shape, F32)] * 3,
        compiler_params=_cp(("parallel", "parallel"), VMEM_BIG))(w, g, m, v)


def _add_cores(g, from_sib, name):
    n, rows, cols = g.shape
    half = rows // 2
    tr = _tile(half, (256, 240))
    nt = half // tr

    def body(lo_ref, hi_ref, s_ref, o_ref):
        mine = jnp.where(lax.axis_index("c") == 0, lo_ref[0], hi_ref[0])
        o_ref[0] = (mine.astype(F32) + s_ref[0].astype(F32)).astype(BF16)

    return pl.pallas_call(
        body, name=name, grid=(n, nt),
        in_specs=[pl.BlockSpec((1, tr, cols), lambda j, i: (j, i, 0)),
                  pl.BlockSpec((1, tr, cols), lambda j, i: (j, nt + i, 0)),
                  pl.BlockSpec((1, tr, cols), lambda j, i: (j, i, 0))],
        out_specs=pl.BlockSpec((1, tr, cols), lambda j, i: (j, i, 0)),
        out_shape=jax.ShapeDtypeStruct((n, half, cols), BF16),
        compiler_params=_cp(("parallel", "parallel"), VMEM_BIG))(g, g, from_sib)


def _add_chips(x, own, name):
    n, rows, cols = x.shape
    tr = _tile(rows, (256, 240))

    def body(x_ref, own_ref, o_ref):
        me = 2 * lax.axis_index("x") + lax.axis_index("y")
        v = [jnp.where(me == k, own_ref[...], x_ref[k]).astype(F32) for k in range(N_CHIPS)]
        o_ref[...] = ((v[0] + v[1]) + v[2]) + v[3]

    return pl.pallas_call(
        body, name=name, grid=(rows // tr,),
        in_specs=[pl.BlockSpec((n, tr, cols), lambda i: (0, i, 0)), pl.BlockSpec((tr, cols), lambda i: (i, 0))],
        out_specs=pl.BlockSpec((tr, cols), lambda i: (i, 0)),
        out_shape=jax.ShapeDtypeStruct((rows, cols), F32), compiler_params=_cp(("parallel",), VMEM_BIG))(x, own)


def _axes():
    return lax.axis_index("x"), lax.axis_index("y"), lax.axis_index("c")


def _other_chips(x, y):
    return [(1 - x, y), (x, 1 - y), (1 - x, 1 - y)]


ANY = pl.BlockSpec(memory_space=pl.ANY)


def _rcopy(src, dst, send_sems, recv_sems, k, to):
    return pltpu.make_async_remote_copy(src_ref=src, dst_ref=dst, send_sem=send_sems.at[k], recv_sem=recv_sems.at[k],
                                        device_id=to, device_id_type=MESH)


def _gather_weights(shards, meta):
    n = len(shards)

    def body(*refs):
        srcs, meta_ref = refs[:n], refs[n]
        outs, mout_ref = refs[n + 1:2 * n + 1], refs[2 * n + 1]
        send_sems, recv_sems = refs[2 * n + 2:]
        x, y, c = _axes()
        me = 2 * x + y
        sib = (x, y, 1 - c)
        chips = _other_chips(x, y)

        def half(t, chip_idx, cc):
            hr = shards[t].shape[0] // 2
            return outs[t].at[chip_idx, pl.ds(cc * hr, hr), :]

        first = []
        for j, (px, py) in enumerate(chips):
            for t in range(n):
                hr = shards[t].shape[0] // 2
                first.append(_rcopy(srcs[t].at[pl.ds(c * hr, hr), :], half(t, me, c), send_sems, recv_sems,
                                    3 * t + j, (px, py, c)))
            first.append(_rcopy(meta_ref, mout_ref.at[me], send_sems, recv_sems, 3 * n + j, (px, py, c)))
        for cp in first:
            cp.start()
        passed = []
        for j, (px, py) in enumerate(chips):
            src_chip = 2 * px + py
            for t in range(n):
                _rcopy(half(t, src_chip, c), half(t, src_chip, c), send_sems, recv_sems, 3 * t + j, sib).wait_recv()
                fwd = _rcopy(half(t, src_chip, c), half(t, src_chip, c), send_sems, recv_sems, 3 * (n + 1 + t) + j, sib)
                fwd.start()
                passed.append(fwd)
            _rcopy(mout_ref.at[src_chip], mout_ref.at[src_chip], send_sems, recv_sems, 3 * n + j, sib).wait_recv()
        for j, (px, py) in enumerate(chips):
            src_chip = 2 * px + py
            for t in range(n):
                _rcopy(half(t, src_chip, 1 - c), half(t, src_chip, 1 - c), send_sems, recv_sems,
                       3 * (n + 1 + t) + j, sib).wait_recv()
        for cp in first + passed:
            cp.wait_send()

    nsem = 3 * (2 * n + 1)
    return pl.pallas_call(
        body, name="gather_weights", in_specs=[ANY] * (n + 1), out_specs=[ANY] * (n + 1),
        out_shape=[jax.ShapeDtypeStruct((N_CHIPS,) + s.shape, s.dtype) for s in shards]
        + [jax.ShapeDtypeStruct((N_CHIPS,) + meta.shape, meta.dtype)],
        scratch_shapes=[pltpu.SemaphoreType.DMA((nsem,)), pltpu.SemaphoreType.DMA((nsem,))])(*shards, meta)


def _gather_late(shard):
    rows, cols = shard.shape
    hr = rows // 2
    src = jax.new_ref(shard, memory_space=pltpu.MemorySpace.HBM)
    out = jax.empty_ref(jax.ShapeDtypeStruct((N_CHIPS, rows, cols), shard.dtype), memory_space=pltpu.MemorySpace.HBM)

    @pl.kernel(mesh=plsc.ScalarSubcoreMesh(axis_name="seq", num_cores=1), name="gather_late",
               scratch_types=(pltpu.SemaphoreType.DMA((6,)), pltpu.SemaphoreType.DMA((6,))),
               compiler_params=pltpu.CompilerParams(collective_id=1))
    def launch(send_sems, recv_sems):
        x, y, c = _axes()
        me = 2 * x + y
        sib = (x, y, 1 - c)
        chips = _other_chips(x, y)
        barrier = pltpu.get_barrier_semaphore()
        for px, py in chips:
            pl.semaphore_signal(barrier, inc=1, device_id=(px, py, c), device_id_type=MESH)
        pl.semaphore_signal(barrier, inc=1, device_id=sib, device_id_type=MESH)
        pl.semaphore_wait(barrier, 4)

        def half(chip_idx, cc):
            return out.at[chip_idx, pl.ds(cc * hr, hr), :]

        first = [_rcopy(src.at[pl.ds(c * hr, hr), :], half(me, c), send_sems, recv_sems, j, (px, py, c))
                 for j, (px, py) in enumerate(chips)]
        for cp in first:
            cp.start()
        passed = []
        for j, (px, py) in enumerate(chips):
            land = half(2 * px + py, c)
            _rcopy(land, land, send_sems, recv_sems, j, sib).wait_recv()
            fwd = _rcopy(land, land, send_sems, recv_sems, 3 + j, sib)
            fwd.start()
            passed.append(fwd)
        for j, (px, py) in enumerate(chips):
            land = half(2 * px + py, 1 - c)
            _rcopy(land, land, send_sems, recv_sems, 3 + j, sib).wait_recv()
        for cp in first + passed:
            cp.wait_send()

    launch()
    return out[...]


def _swap_halves(gs):
    n = len(gs)
    ncopies = sum(g.shape[0] for g in gs)

    def body(*refs):
        srcs, outs = refs[:n], refs[n:2 * n]
        send_sems, recv_sems = refs[2 * n:]
        x, y, c = _axes()
        cps = []
        for t in range(n):
            hr = gs[t].shape[1] // 2
            for j in range(gs[t].shape[0]):
                cps.append(_rcopy(srcs[t].at[j, pl.ds((1 - c) * hr, hr), :], outs[t].at[j], send_sems, recv_sems,
                                  len(cps), (x, y, 1 - c)))
        for cp in cps:
            cp.start()
        for cp in cps:
            cp.wait()

    return pl.pallas_call(
        body, name="swap_halves", in_specs=[ANY] * n, out_specs=[ANY] * n,
        out_shape=[jax.ShapeDtypeStruct((g.shape[0], g.shape[1] // 2, g.shape[2]), g.dtype) for g in gs],
        scratch_shapes=[pltpu.SemaphoreType.DMA((ncopies,)), pltpu.SemaphoreType.DMA((ncopies,))])(*gs)


def _scatter_chips(parts):
    n = len(parts)
    srcs = [jax.new_ref(p, memory_space=pltpu.MemorySpace.HBM) for p in parts]
    outs = [jax.empty_ref(jax.ShapeDtypeStruct(p.shape, p.dtype), memory_space=pltpu.MemorySpace.HBM) for p in parts]

    @pl.kernel(mesh=plsc.ScalarSubcoreMesh(axis_name="seq", num_cores=1), name="scatter_chips",
               scratch_types=(pltpu.SemaphoreType.DMA((3 * n,)), pltpu.SemaphoreType.DMA((3 * n,))),
               compiler_params=pltpu.CompilerParams(collective_id=0))
    def launch(send_sems, recv_sems):
        x, y, c = _axes()
        me = 2 * x + y
        chips = _other_chips(x, y)
        barrier = pltpu.get_barrier_semaphore()
        for px, py in chips:
            pl.semaphore_signal(barrier, inc=1, device_id=(px, py, c), device_id_type=MESH)
        pl.semaphore_wait(barrier, 3)
        cps = []
        for j, (px, py) in enumerate(chips):
            for t in range(n):
                cps.append(_rcopy(srcs[t].at[2 * px + py], outs[t].at[me], send_sems, recv_sems, 3 * t + j,
                                  (px, py, c)))
        for cp in cps:
            cp.start()
        for cp in cps:
            cp.wait()

    launch()
    return [o[...] for o in outs]


def _swap_reduced(rs):
    n = len(rs)

    def body(*refs):
        srcs, outs = refs[:n], refs[n:2 * n]
        send_sems, recv_sems = refs[2 * n:]
        x, y, c = _axes()
        cps = [_rcopy(srcs[t], outs[t], send_sems, recv_sems, t, (x, y, 1 - c)) for t in range(n)]
        for cp in cps:
            cp.start()
        for cp in cps:
            cp.wait()

    return pl.pallas_call(
        body, name="swap_reduced", in_specs=[ANY] * n, out_specs=[ANY] * n,
        out_shape=[jax.ShapeDtypeStruct(r.shape, r.dtype) for r in rs],
        scratch_shapes=[pltpu.SemaphoreType.DMA((n,)), pltpu.SemaphoreType.DMA((n,))])(*rs)


SMALL_ROWS = 24 + 128


def _allreduce_small(vec):
    def body(v_ref, out_ref, slots, send_sems, recv_sems):
        x, y, c = _axes()
        me = 4 * x + 2 * y + c
        slots[me] = v_ref[...]
        cps = []
        for k in range(1, 8):
            kx, ky, kc = (k >> 2) & 1, (k >> 1) & 1, k & 1
            peer = (1 - x if kx else x, 1 - y if ky else y, 1 - c if kc else c)
            cps.append(_rcopy(v_ref, slots.at[me], send_sems, recv_sems, k - 1, peer))
        for cp in cps:
            cp.start()
        for cp in cps:
            cp.wait()
        tot = slots[0]
        for k in range(1, 8):
            tot = tot + slots[k]
        out_ref[...] = tot

    return pl.pallas_call(
        body, name="allreduce_small",
        in_specs=[pl.BlockSpec(memory_space=pltpu.VMEM)], out_specs=pl.BlockSpec(memory_space=pltpu.VMEM),
        out_shape=jax.ShapeDtypeStruct((SMALL_ROWS, 128), F32),
        scratch_shapes=[pltpu.VMEM((8, SMALL_ROWS, 128), F32), pltpu.SemaphoreType.DMA((7,)),
                        pltpu.SemaphoreType.DMA((7,))])(vec)


def _pack_p2(w_uq, w_ukv, w_br_mla, w_br_fox, w_out, dtype):
    parts = [w_uq.reshape(96, D_MODEL), w_ukv.reshape(64, D_MODEL), w_br_mla, w_br_fox, w_out]
    return jnp.concatenate([p.astype(dtype) for p in parts], axis=0)


def _unpack_p2(pk):
    return pk[0:96].reshape(256, 384), pk[96:160].reshape(128, 512), pk[160:416], pk[416:672], pk[672:928]


def _uq_arrange(w):
    w3 = w.reshape(256, HEADS, 96)
    nope = w3[:, :, :64].reshape(256, PAIRS, 128)
    pe = w3[:, :, 64:].reshape(256, PAIRS, 64)
    return jnp.concatenate([nope, pe, jnp.zeros((256, PAIRS, 64), w.dtype)], axis=2).reshape(256, PAIRS * 256)


def _uq_restore(g):
    g3 = g.reshape(256, PAIRS, 256)
    nope = g3[:, :, :128].reshape(256, HEADS, 64)
    pe = g3[:, :, 128:192].reshape(256, HEADS, 32)
    return jnp.concatenate([nope, pe], axis=2).reshape(256, HEADS * 96)


def _ukv_arrange(w):
    w3 = w.reshape(128, HEADS, 128)
    return jnp.concatenate([w3[:, :, :64].reshape(128, 1024), w3[:, :, 64:].reshape(128, 1024)], axis=1)


def _ukv_restore(g):
    kn = g[:, :1024].reshape(128, HEADS, 64)
    vv = g[:, 1024:].reshape(128, HEADS, 64)
    return jnp.concatenate([kn, vv], axis=2).reshape(128, HEADS * 128)


def _rope_tables(lp):
    r = np.arange(lp)
    pos = np.where(r < N_META, r, np.where(r >= PAD, r - PAD + N_META, 0)).astype(np.float32)
    half = MLA_ROPE // 2
    inv_freq = np.float32(ROPE_THETA) ** (-np.arange(half, dtype=np.float32) / np.float32(half))
    ang = (pos[:, None] * inv_freq[None, :]).astype(np.float32)
    cos, sin = np.cos(ang).astype(np.float32), np.sin(ang).astype(np.float32)
    one, zero = np.ones((lp, 64), np.float32), np.zeros((lp, 64), np.float32)
    return (jnp.asarray(np.concatenate([cos, cos, cos, cos, one], axis=1)),
            jnp.asarray(np.concatenate([-sin, sin, -sin, sin, zero], axis=1)))


def _pad_lanes(v, n=128):
    return jnp.pad(v, ((0, 0), (0, n - v.shape[1])))


def _in_cols(slabs, a, b):
    out = []
    for j in range(N_CHIPS):
        lo, hi = max(a, W_IN_SHARD * j), min(b, W_IN_SHARD * (j + 1))
        if lo < hi:
            out.append(slabs[j][:, lo - W_IN_SHARD * j:hi - W_IN_SHARD * j])
    return out


def _local_step(x2, tgt2, meta_f, w_small, w_attn, w_gate, w_uq_f, w_ukv_f, w_bm, w_bf, w_o, pre_norm_g,
                post_norm_g, mla_q_norm_g, mla_kv_norm_g, fox_forget_b, start_exchange=None):
    s_rows = x2.shape[0]
    lp = PAD + s_rows
    w_uq_a = _uq_arrange(w_uq_f)
    w_ukv_a = _ukv_arrange(w_ukv_f)

    ctab, stab = _rope_tables(lp)
    ii = jnp.arange(BLK)
    tri_lo = (ii[:, None] >= ii[None, :]).astype(BF16)
    tri_up = (ii[:, None] <= ii[None, :]).astype(BF16)
    fb128 = _pad_lanes(fox_forget_b)

    u = _rms_pre(x2, meta_f, pre_norm_g)
    small = _mm(u, w_small, mode="nn", out_dtype=F32, name="proj_small")
    attn = _mm(u, w_attn, mode="nn", out_dtype=BF16, name="proj_attn",
               col_scale=(HEADS * HEAD_DIM, FOX_SCALE * LOG2E))
    gate = _mm(u, w_gate, mode="nn", out_dtype=BF16, name="proj_gate")
    qn, kvn, kr, kb = _small_prep(small, mla_q_norm_g, mla_kv_norm_g, fb128, ctab, stab, tri_lo)
    qcat = _mm(qn, w_uq_a, mode="nn", out_dtype=BF16, name="mla_q", row_ins=(ctab, stab),
               epilogue=lambda tile, c, s: _rope_pairs(tile, c, s) * (MLA_SCALE * LOG2E))
    kv = _mm(kvn, w_ukv_a, mode="nn", out_dtype=BF16, name="mla_kv")

    mla_cols = dict(qcol=0, kcol=lambda p: p, vcol=lambda p: PAIRS + p)
    fox_cols = dict(qcol=0, kcol=lambda p: PAIRS + p, vcol=lambda p: 2 * PAIRS + p)
    o_mla, lse_mla = _attn_fwd(qcat, kv, kv, _transposed_cols(kv, 1, "mla_vt"), kr, rope=True, name="mla_fwd",
                               **mla_cols)
    o_fox, lse_fox = _attn_fwd(attn, attn, attn, _transposed_cols(attn, 2, "fox_vt"), kb, rope=False,
                               name="fox_fwd", **fox_cols)

    a_mla, a_fox = _gate_fwd(o_mla, o_fox, gate)
    y_mla = _mm(a_mla, w_bm, mode="nn", out_dtype=BF16, name="br_mla")
    y_fox = _mm(a_fox, w_bf, mode="nn", out_dtype=BF16, name="br_fox")
    mg = _merge_fwd(gate, y_mla, y_fox)
    mixed = _mm(mg, w_o, mode="nn", out_dtype=F32, name="out_proj")
    dmixed, dy, loss_p, dg_post = _tail(x2, mixed, tgt2, post_norm_g)

    d_w_out = _mm(mg, dmixed, mode="tn", out_dtype=F32, name="d_w_out")
    dm = _mm(dmixed, w_o, mode="nt", out_dtype=BF16, name="d_merge")
    dy_mla, dy_fox, dgate_ab = _merge_bwd(dm, gate, y_mla, y_fox)
    d_w_bm = _mm(a_mla, dy_mla, mode="tn", out_dtype=F32, name="d_w_br_mla")
    d_w_bf = _mm(a_fox, dy_fox, mode="tn", out_dtype=F32, name="d_w_br_fox")
    da_mla = _mm(dy_mla, w_bm, mode="nt", out_dtype=BF16, name="d_a_mla")
    da_fox = _mm(dy_fox, w_bf, mode="nt", out_dtype=BF16, name="d_a_fox")
    do_mla, do_fox, dgate_z, dl_mla, dl_fox = _gate_bwd(da_mla, da_fox, o_mla, o_fox, gate)
    dl_mla, dl_fox = (d[:, :HEADS].T.reshape(PAIRS, 2, lp) for d in (dl_mla, dl_fox))

    dq_a, dkn, dvm, dkr = _attn_bwd(qcat, kv, kv, kr, do_mla, dl_mla, lse_mla, rtabs=(ctab, stab),
                                    scale=MLA_SCALE, name="mla_bwd", **mla_cols)
    dfq, dfk, dfv, dcol, drow = _attn_bwd(attn, attn, attn, kb, do_fox, dl_fox, lse_fox, scale=FOX_SCALE,
                                          name="fox_bwd", **fox_cols)

    d_w_uq_a = _mm(qn, dq_a, mode="tn", out_dtype=F32, name="d_w_uq")
    dqn = _mm(dq_a, w_uq_a, mode="nt", out_dtype=F32, name="d_qn")
    d_w_ukv_a = jnp.concatenate([_mm(kvn, dkn, mode="tn", out_dtype=F32, name="d_w_uk"),
                                 _mm(kvn, dvm, mode="tn", out_dtype=F32, name="d_w_uv")], axis=1)
    dkvn = _mm(dkn, w_ukv_a[:, :1024], mode="nt", out_dtype=F32, name="d_kvn_k")
    dkvn = _mm(dvm, w_ukv_a[:, 1024:], mode="nt", out_dtype=F32, name="d_kvn_v", acc=dkvn)
    dsmall, dg_q, dg_kv, dfb = _small_bwd(small, dqn, dkvn, dkr, dcol, drow, mla_q_norm_g, mla_kv_norm_g,
                                          fb128, ctab, stab, tri_up)

    dw_small = _mm(u, dsmall, mode="tn", out_dtype=BF16, name="d_w_small")
    dw_fq = _mm(u, dfq, mode="tn", out_dtype=BF16, name="d_w_fq")
    dw_fk = _mm(u, dfk, mode="tn", out_dtype=BF16, name="d_w_fk")
    dw_fv = _mm(u, dfv, mode="tn", out_dtype=BF16, name="d_w_fv")
    dw_z = _mm(u, dgate_z, mode="tn", out_dtype=BF16, name="d_w_z")
    dw_g = _mm(u, dgate_ab, mode="tn", out_dtype=BF16, name="d_w_g")
    d_w_in = (dw_small, dw_z, dw_fq, dw_fk, dw_fv, dw_g)
    d_w_uq = _uq_restore(d_w_uq_a)
    d_w_ukv = _ukv_restore(d_w_ukv_a)
    token = start_exchange(d_w_in, d_w_uq, d_w_ukv, d_w_bm, d_w_bf, d_w_out) if start_exchange else None
    du = _mm_sum_nt([(dsmall, w_small), (dfq, w_attn[:, 0:1024]), (dfk, w_attn[:, 1024:2048]),
                     (dfv, w_attn[:, 2048:3072]), (dgate_z, w_gate[:, 0:2048]), (dgate_ab, w_gate[:, 2048:4096])],
                    name="d_u", after=token)
    dx, dmeta, dg_pre = _pre_bwd(du, x2, meta_f, dy, pre_norm_g)
    return (loss_p, dx, dmeta, d_w_in, d_w_uq, d_w_ukv, d_w_bm, d_w_bf, d_w_out, dg_pre, dg_post, dg_q, dg_kv, dfb)


def _w_in_slabs(pieces):
    dw_small, dw_z, dw_fq, dw_fk, dw_fv, dw_g = pieces
    runs = [(dw_small[:, 0:416], C_CQ), (dw_z[:, 0:1024], C_ZMLA), (dw_fq, C_FQ), (dw_fk, C_FK), (dw_fv, C_FV),
            (dw_small[:, 512:528], C_FL), (dw_z[:, 1024:2048], C_ZFOX), (dw_g, C_GA)]
    slabs = []
    for j in range(N_CHIPS):
        lo, hi = W_IN_SHARD * j, W_IN_SHARD * (j + 1)
        cols = [a[:, max(lo, c0) - c0:min(hi, c0 + a.shape[1]) - c0] for a, c0 in runs
                if max(lo, c0) < min(hi, c0 + a.shape[1])]
        slabs.append(jnp.concatenate(cols, axis=1))
    return jnp.stack(slabs, axis=0)


def kernel(x, meta_tokens, pre_norm_g, w_in, fox_forget_b, mla_q_norm_g, mla_kv_norm_g, w_uq, w_ukv, w_br_mla, w_br_fox, w_out, post_norm_g, loss_target, m_meta_tokens, m_pre_norm_g, m_w_in, m_fox_forget_b, m_mla_q_norm_g, m_mla_kv_norm_g, m_w_uq, m_w_ukv, m_w_br_mla, m_w_br_fox, m_w_out, m_post_norm_g, v_meta_tokens, v_pre_norm_g, v_w_in, v_fox_forget_b, v_mla_q_norm_g, v_mla_kv_norm_g, v_w_uq, v_w_ukv, v_w_br_mla, v_w_br_fox, v_w_out, v_post_norm_g):
    me = 2 * lax.axis_index("x") + lax.axis_index("y")
    core = lax.axis_index("c")
    w_in_b = w_in.astype(BF16).reshape(D_MODEL, W_IN_SHARD)
    p2 = _pack_p2(w_uq[0], w_ukv[0], w_br_mla[0], w_br_fox[0], w_out[0], BF16)
    w_in_g, meta_g = _gather_weights([w_in_b], meta_tokens)
    p2_g = _gather_late(lax.optimization_barrier((p2, w_in_g))[0])
    slabs = [jnp.where(me == j, w_in_b, w_in_g[j]) for j in range(N_CHIPS)]
    chip = lax.broadcasted_iota(jnp.int32, (N_CHIPS, 1, 1), 0)
    p2_all = jnp.where(chip == me, p2[None], p2_g)
    w_uq_f = p2_all[:, 0:96].reshape(N_CHIPS, 256, 384).transpose(1, 0, 2).reshape(256, 1536)
    w_ukv_f = p2_all[:, 96:160].reshape(N_CHIPS, 128, 512).transpose(1, 0, 2).reshape(128, 2048)
    w_bm, w_bf, w_o = (p2_all[:, lo:lo + 256].reshape(D_MODEL, D_MODEL) for lo in (160, 416, 672))
    meta_f = jnp.where(chip == me, meta_tokens[None], meta_g).transpose(1, 0, 2).reshape(N_META, D_MODEL)
    kpe = _in_cols(slabs, C_KPE, C_ZMLA)
    w_small = jnp.concatenate(_in_cols(slabs, C_CQ, C_KPE) + kpe + kpe + [jnp.zeros((D_MODEL, 64), BF16)]
                              + _in_cols(slabs, C_FL, C_ZFOX) + [jnp.zeros((D_MODEL, 112), BF16)], axis=1)
    w_attn = jnp.concatenate(_in_cols(slabs, C_FQ, C_FL), axis=1)
    w_gate = jnp.concatenate(_in_cols(slabs, C_ZMLA, C_FQ) + _in_cols(slabs, C_ZFOX, C_END), axis=1)

    exchange = {}

    def start_exchange(d_w_in, d_w_uq, d_w_ukv, d_w_bm, d_w_bf, d_w_out):
        g2 = jnp.concatenate(
            [d_w_uq.reshape(256, N_CHIPS, 384).transpose(1, 0, 2).reshape(N_CHIPS, 96, D_MODEL),
             d_w_ukv.reshape(128, N_CHIPS, 512).transpose(1, 0, 2).reshape(N_CHIPS, 64, D_MODEL)]
            + [g.reshape(N_CHIPS, 256, D_MODEL) for g in (d_w_bm, d_w_bf, d_w_out)], axis=1)
        pieces = [p[None] for p in d_w_in]
        from_sib = _swap_halves(pieces + [g2])
        halves = [_add_cores(p, s, "add_cores_" + nm)[0]
                  for p, s, nm in zip(pieces, from_sib, ("small", "z", "fq", "fk", "fv", "g"))]
        parts = [_w_in_slabs(halves), _add_cores(g2, from_sib[-1], "add_cores_rest")]
        exchange.update(parts=parts, landed=_scatter_chips(parts))
        return parts[0][0, 0:16, 0:LANES]

    (loss_p, dx, dmeta, _, _, _, _, _, _, dg_pre, dg_post, dg_q, dg_kv,
     dfb) = _local_step(x[0], loss_target[0], meta_f, w_small, w_attn, w_gate, w_uq_f, w_ukv_f, w_bm, w_bf, w_o,
                        pre_norm_g, post_norm_g, mla_q_norm_g, mla_kv_norm_g, fox_forget_b, start_exchange)

    mine = [_add_chips(l, lax.dynamic_index_in_dim(p, me, 0, keepdims=False), nm)
            for l, p, nm in zip(exchange["landed"], exchange["parts"], ("add_chips_w_in", "add_chips_rest"))]
    theirs = _swap_reduced(mine)
    g_w_in, g_p2 = [jnp.concatenate([jnp.where(core == 0, a, b), jnp.where(core == 0, b, a)], axis=0)
                    for a, b in zip(mine, theirs)]
    g_w_uq, g_w_ukv, g_w_bm, g_w_bf, g_w_out = _unpack_p2(g_p2)
    g_w_in = g_w_in[None]

    vec = jnp.concatenate([dg_pre.reshape(8, 128), dg_post.reshape(8, 128), dg_q.reshape(2, 128), dg_kv,
                           dfb, _pad_lanes(loss_p), jnp.zeros((3, 128), F32), dmeta.reshape(128, 128)], axis=0)
    tot = _allreduce_small(vec)
    loss = tot[20, 0]
    g_meta = lax.dynamic_slice_in_dim(tot[24:].reshape(N_META, D_MODEL), 256 * me, 256, axis=1)

    def small_pack(pre, post, gq_, gkv_, fb_):
        return jnp.concatenate([pre.reshape(8, 128), post.reshape(8, 128), gq_.reshape(2, 128), gkv_,
                                _pad_lanes(fb_), jnp.zeros((4, 128), F32)], axis=0)

    def small_unpack(t):
        return (t[0:8].reshape(1, 1024), t[8:16].reshape(1, 1024), t[16:18].reshape(1, 256), t[18:19],
                t[19:20, 0:HEADS])

    g_small = jnp.concatenate([tot[0:20], jnp.zeros((4, 128), F32)], axis=0)
    sm = _adamw(small_pack(pre_norm_g, post_norm_g, mla_q_norm_g, mla_kv_norm_g, fox_forget_b), g_small,
                small_pack(m_pre_norm_g, m_post_norm_g, m_mla_q_norm_g, m_mla_kv_norm_g, m_fox_forget_b),
                small_pack(v_pre_norm_g, v_post_norm_g, v_mla_q_norm_g, v_mla_kv_norm_g, v_fox_forget_b),
                "adamw_small")
    g_pre, g_post, g_q, g_kv, g_fb = small_unpack(g_small)
    (d_pre, d_post, d_q, d_kv, d_fb), (nm_pre, nm_post, nm_q, nm_kv, nm_fb), (nv_pre, nv_post, nv_q, nv_kv, nv_fb) = (
        small_unpack(t) for t in sm)

    d_meta, nm_meta, nv_meta = _adamw(meta_tokens, g_meta, m_meta_tokens, v_meta_tokens, "adamw_meta")
    d_win, nm_win, nv_win = (t.T[None] for t in _adamw(w_in[0].T, g_w_in[0].T, m_w_in[0].T, v_w_in[0].T,
                                                       "adamw_w_in"))
    d_wuq, nm_wuq, nv_wuq = _adamw(w_uq[0], g_w_uq, m_w_uq[0], v_w_uq[0], "adamw_w_uq")
    d_wukv, nm_wukv, nv_wukv = _adamw(w_ukv[0], g_w_ukv, m_w_ukv[0], v_w_ukv[0], "adamw_w_ukv")
    d_wbm, nm_wbm, nv_wbm = _adamw(w_br_mla[0], g_w_bm, m_w_br_mla[0], v_w_br_mla[0], "adamw_w_br_mla")
    d_wbf, nm_wbf, nv_wbf = _adamw(w_br_fox[0], g_w_bf, m_w_br_fox[0], v_w_br_fox[0], "adamw_w_br_fox")
    d_wo, nm_wo, nv_wo = _adamw(w_out[0], g_w_out, m_w_out[0], v_w_out[0], "adamw_w_out")

    def group(meta_, pre, win, fb_, q_, kv_, wuq, wukv, wbm, wbf, wo, post):
        return (meta_, pre, win, fb_, q_, kv_, wuq[None], wukv[None], wbm[None], wbf[None], wo[None], post)

    grads = group(g_meta, g_pre, g_w_in, g_fb, g_q, g_kv, g_w_uq, g_w_ukv, g_w_bm, g_w_bf, g_w_out, g_post)
    deltas = group(d_meta, d_pre, d_win, d_fb, d_q, d_kv, d_wuq, d_wukv, d_wbm, d_wbf, d_wo, d_post)
    new_m = group(nm_meta, nm_pre, nm_win, nm_fb, nm_q, nm_kv, nm_wuq, nm_wukv, nm_wbm, nm_wbf, nm_wo, nm_post)
    new_v = group(nv_meta, nv_pre, nv_win, nv_fb, nv_q, nv_kv, nv_wuq, nv_wukv, nv_wbm, nv_wbf, nv_wo, nv_post)
    return (loss, dx[None], *grads, *deltas, *new_m, *new_v)
```

```python
import math

import jax
import jax.numpy as jnp
import numpy as np
from jax import lax
from jax.experimental import pallas as pl
from jax.experimental.pallas import tpu as pltpu
from jax.experimental.pallas import tpu_sc as plsc

F32 = jnp.float32
BF16 = jnp.bfloat16

D_MODEL = 1024
N_META = 16
RMS_EPS = 1e-6
HEADS = 16
PAIRS = HEADS // 2
HEAD_DIM = 64
LANES = 128
MLA_ROPE = 32
AHEAD = 6
AHEAD_BWD = 1
BIAS_PARTS = 3
MLA_SCALE = 1.0 / math.sqrt(64 + 32)
FOX_SCALE = 1.0 / math.sqrt(64)
LOG2E = math.log2(math.e)
LN2 = math.log(2.0)
ROPE_THETA = 10000.0

PAD = 256
BLK = 256
QB = 512
UNROLL = 4
NEG = -1e30

C_CQ, C_CKV, C_KPE, C_ZMLA, C_FQ, C_FK, C_FV, C_FL, C_ZFOX, C_GA, C_GB, C_END = (
    0, 256, 384, 416, 1440, 2464, 3488, 4512, 4528, 5552, 6576, 7600)
SMALL_W = 640
W_IN_SHARD = 1900

P2_ROWS = 928
N_CHIPS = 4

ADAM_LR = 0.001
ADAM_B1 = 0.9
ADAM_B2 = 0.999
ADAM_EPS = 1e-08
ADAM_WD = 0.01
ADAM_STEP = 10

VMEM_BIG = 56 * 1024 * 1024
MM_VMEM_BUDGET = 44 * 1024 * 1024
MESH = pl.DeviceIdType.MESH


def _cp(dims, vmem=None):
    return pltpu.CompilerParams(dimension_semantics=dims, vmem_limit_bytes=vmem)


def _dot(a, b, ca, cb):
    return lax.dot_general(a, b, (((ca,), (cb,)), ((), ())), preferred_element_type=F32)


def _sigmoid(x):
    return 1.0 / (1.0 + jnp.exp(-x))


def _tile(n, cands):
    for c in cands:
        if n % c == 0:
            return c
    return n


def _mm(a, b, *, mode, out_dtype, name, acc=None, epilogue=None, row_ins=(), after=None, col_scale=None):
    if mode == "nn":
        (M, K), N = a.shape, b.shape[1]
    elif mode == "nt":
        (M, K), N = a.shape, b.shape[0]
    else:
        (K, M), N = a.shape, b.shape[1]
    tm = _tile(M, (1088, 1024)) if M > 1024 else M
    tn = _tile(N, (1024,)) if N > 1024 else N
    nk = 1
    while True:
        tk = K // nk
        need = 2 * tk * (tm * a.dtype.itemsize + tn * b.dtype.itemsize) + tm * tn * (
            2 * jnp.dtype(out_dtype).itemsize + (8 if acc is not None else 0) + (4 if nk > 1 else 0))
        if need <= MM_VMEM_BUDGET or (tk // 2) % (16 if mode == "tn" else LANES) or tk <= 512:
            break
        nk *= 2
    while (M // tm) * (N // tn) * nk < 4 and tn % 512 == 0:
        tn //= 2
    assert col_scale is None or (nk == 1 and col_scale[0] % tn == 0)
    ca, cb = {"nn": (1, 0), "nt": (1, 1), "tn": (0, 0)}[mode]
    a_spec = (pl.BlockSpec((tk, tm), lambda j, i, k: (k, i)) if mode == "tn"
              else pl.BlockSpec((tm, tk), lambda j, i, k: (i, k)))
    b_spec = (pl.BlockSpec((tn, tk), lambda j, i, k: (j, k)) if mode == "nt"
              else pl.BlockSpec((tk, tn), lambda j, i, k: (k, j)))
    o_spec = pl.BlockSpec((tm, tn), lambda j, i, k: (i, j))
    has_acc = acc is not None

    nrow = len(row_ins)

    def body(*refs):
        a_ref, b_ref = refs[0], refs[1]
        acc_ref = refs[2] if has_acc else None
        rows = refs[2 + has_acc:2 + has_acc + nrow]
        o_ref = refs[2 + has_acc + nrow + (after is not None)]

        def store(tile):
            if epilogue is not None:
                tile = epilogue(tile, *[r[...] for r in rows])
            if col_scale is not None:
                tile = tile * jnp.where(pl.program_id(0) * tn < col_scale[0], col_scale[1], 1.0)
            o_ref[...] = tile.astype(out_dtype)

        part = _dot(a_ref[...].astype(BF16), b_ref[...].astype(BF16), ca, cb)
        if nk == 1:
            store(part + acc_ref[...] if has_acc else part)
        else:
            sc = refs[-1]
            k = pl.program_id(2)

            @pl.when(k == 0)
            def _():
                sc[...] = part + acc_ref[...] if has_acc else part

            @pl.when(k > 0)
            def _():
                sc[...] += part

            @pl.when(k == nk - 1)
            def _():
                store(sc[...])

    ins = [a, b] + ([acc] if has_acc else []) + list(row_ins)
    in_specs = ([a_spec, b_spec] + ([o_spec] if has_acc else [])
                + [pl.BlockSpec((tm, r.shape[1]), lambda j, i, k: (i, 0)) for r in row_ins])
    if after is not None:
        ins.append(after)
        in_specs.append(pl.BlockSpec(after.shape, lambda j, i, k: (0,) * after.ndim))
    return pl.pallas_call(
        body, name=name, grid=(N // tn, M // tm, nk), in_specs=in_specs, out_specs=o_spec,
        out_shape=jax.ShapeDtypeStruct((M, N), out_dtype),
        scratch_shapes=[pltpu.VMEM((tm, tn), F32)] if nk > 1 else [],
        compiler_params=_cp(("parallel", "parallel", "arbitrary"), VMEM_BIG))(*ins)


def _mm_sum_nt(pairs, *, name, after=None):
    n = len(pairs)
    M, N = pairs[0][0].shape[0], pairs[0][1].shape[0]
    tm = _tile(M, (272,))

    def body(*refs):
        o_ref = refs[2 * n + (after is not None)]
        tot = _dot(refs[0][...].astype(BF16), refs[n][...].astype(BF16), 1, 1)
        for i in range(1, n):
            tot = tot + _dot(refs[i][...].astype(BF16), refs[n + i][...].astype(BF16), 1, 1)
        o_ref[...] = tot

    ins = [a for a, _ in pairs] + [b for _, b in pairs]
    in_specs = ([pl.BlockSpec((tm, a.shape[1]), lambda i: (i, 0)) for a, _ in pairs]
                + [pl.BlockSpec(b.shape, lambda i: (0, 0)) for _, b in pairs])
    if after is not None:
        ins.append(after)
        in_specs.append(pl.BlockSpec(after.shape, lambda i: (0,) * after.ndim))
    return pl.pallas_call(
        body, name=name, grid=(M // tm,), in_specs=in_specs, out_specs=pl.BlockSpec((tm, N), lambda i: (i, 0)),
        out_shape=jax.ShapeDtypeStruct((M, N), F32), compiler_params=_cp(("parallel",), VMEM_BIG))(*ins)


def _row(w):
    return pl.BlockSpec((BLK, w), lambda i: (i, 0))


def _rowc(w, c):
    return pl.BlockSpec((BLK, w), lambda i: (i, c))


def _full(shape):
    return pl.BlockSpec(shape, lambda i: tuple(0 for _ in shape))


def _rope(x, c, s):
    lane = lax.broadcasted_iota(jnp.int32, x.shape, 1)
    is_x1 = ((lane >> 4) & 1) == 0
    partner = jnp.where(is_x1, pltpu.roll(x, LANES - 16, 1), pltpu.roll(x, 16, 1))
    return x * c + partner * s


def _row_valid(i):
    rows = i * BLK + lax.broadcasted_iota(jnp.int32, (BLK, 1), 0)
    return (rows < N_META) | (rows >= PAD)


def _shift_rows(w):
    return pl.BlockSpec((BLK, w), lambda i: (jnp.maximum(i - 1, 0), 0))


def _h_block(i, x_ref, meta_ref):
    head = jnp.concatenate([meta_ref[...], jnp.zeros((BLK - N_META, D_MODEL), F32)], axis=0)
    return jnp.where(i == 0, head, x_ref[...])


def _rms_pre(x2, meta, g):
    lp = PAD + x2.shape[0]

    def body(x_ref, meta_ref, g_ref, u_ref):
        hv = _h_block(pl.program_id(0), x_ref, meta_ref)
        r = lax.rsqrt(jnp.mean(hv * hv, axis=-1, keepdims=True) + RMS_EPS)
        u_ref[...] = (hv * r * g_ref[...]).astype(BF16)

    return pl.pallas_call(
        body, name="rms_pre", grid=(lp // BLK,),
        in_specs=[_shift_rows(D_MODEL), _full((N_META, D_MODEL)), _full((1, D_MODEL))], out_specs=_row(D_MODEL),
        out_shape=jax.ShapeDtypeStruct((lp, D_MODEL), BF16),
        compiler_params=_cp(("parallel",)))(x2, meta, g)


def _split3(x):
    hi = x.astype(BF16)
    r1 = x - hi.astype(F32)
    mid = r1.astype(BF16)
    lo = (r1 - mid.astype(F32)).astype(BF16)
    return hi, mid, lo


def _small_prep(small, gq, gkv, fb, ctab, stab, tri):
    lp = small.shape[0]

    def body(sm_ref, gq_ref, gkv_ref, fb_ref, c_ref, s_ref, tri_ref, qn_ref, kvn_ref, kr_ref, kb_ref, carry):
        i = pl.program_id(0)

        @pl.when(i == 0)
        def _():
            carry[...] = jnp.zeros_like(carry)

        cq = sm_ref[:, 0:256]
        r = lax.rsqrt(jnp.mean(cq * cq, axis=-1, keepdims=True) + RMS_EPS)
        qn_ref[...] = (cq * r * gq_ref[...]).astype(BF16)
        ckv = sm_ref[:, 256:384]
        r = lax.rsqrt(jnp.mean(ckv * ckv, axis=-1, keepdims=True) + RMS_EPS)
        kvn_ref[...] = (ckv * r * gkv_ref[...]).astype(BF16)
        kr_ref[...] = _rope(sm_ref[:, 384:512], c_ref[...], s_ref[...]).astype(BF16)
        fl = sm_ref[:, 512:640] + fb_ref[...]
        lf = jnp.minimum(fl, 0.0) - jnp.log(1.0 + jnp.exp(-jnp.abs(fl)))
        lf = jnp.where(_row_valid(i), lf, 0.0)
        hi, mid, lo = _split3(lf)
        t = tri_ref[...]
        cum = (_dot(t, hi, 1, 0) + _dot(t, mid, 1, 0)) + _dot(t, lo, 1, 0) + carry[...]
        carry[...] = cum[BLK - 1:BLK, :]
        src = lax.broadcasted_iota(jnp.int32, (LANES, LANES), 0)
        dst = lax.broadcasted_iota(jnp.int32, (LANES, LANES), 1)
        kb = jnp.zeros((BLK, LANES), F32)
        for j, part in enumerate(_split3(-cum * LOG2E)):
            spread = ((dst == BIAS_PARTS * src + j) & (src < HEADS)).astype(BF16)
            kb = kb + _dot(part, spread, 1, 0)
        kb_ref[...] = kb.astype(BF16)

    return pl.pallas_call(
        body, name="small_prep", grid=(lp // BLK,),
        in_specs=[_row(SMALL_W), _full((1, 256)), _full((1, 128)), _full((1, 128)), _row(128), _row(128),
                  _full((BLK, BLK))],
        out_specs=[_row(256), _row(128), _row(128), _row(128)],
        out_shape=[jax.ShapeDtypeStruct((lp, 256), BF16), jax.ShapeDtypeStruct((lp, 128), BF16),
                   jax.ShapeDtypeStruct((lp, 128), BF16), jax.ShapeDtypeStruct((lp, 128), BF16)],
        scratch_shapes=[pltpu.VMEM((1, 128), F32)],
        compiler_params=_cp(("arbitrary",)))(small, gq, gkv, fb, ctab, stab, tri)


def _rope_pairs(tile, c, s):
    out = []
    for lo in range(0, tile.shape[1], 256):
        out += [tile[:, lo:lo + 128], _rope(tile[:, lo + 128:lo + 256], c, s)]
    return jnp.concatenate(out, axis=1)


def _gate_fwd(o_mla, o_fox, gate):
    lp = o_mla.shape[0]

    def body(om_ref, of_ref, zm_ref, zf_ref, am_ref, af_ref):
        zm = zm_ref[...].astype(F32)
        am_ref[...] = (om_ref[...] * (zm * _sigmoid(zm))).astype(BF16)
        zf = zf_ref[...].astype(F32)
        af_ref[...] = (of_ref[...] * (zf * _sigmoid(zf))).astype(BF16)

    return pl.pallas_call(
        body, name="gate_fwd", grid=(lp // BLK,),
        in_specs=[_row(D_MODEL), _row(D_MODEL), _rowc(D_MODEL, 0), _rowc(D_MODEL, 1)],
        out_specs=[_row(D_MODEL), _row(D_MODEL)],
        out_shape=[jax.ShapeDtypeStruct((lp, D_MODEL), BF16)] * 2,
        compiler_params=_cp(("parallel",)))(o_mla, o_fox, gate, gate)


def _merge_fwd(gate, y_mla, y_fox):
    lp = y_mla.shape[0]

    def body(ga_ref, gb_ref, ym_ref, yf_ref, m_ref):
        sa = _sigmoid(ga_ref[...].astype(F32))
        sb = _sigmoid(gb_ref[...].astype(F32))
        m_ref[...] = (sa * ym_ref[...] + sb * yf_ref[...]).astype(BF16)

    return pl.pallas_call(
        body, name="merge_fwd", grid=(lp // BLK,),
        in_specs=[_rowc(D_MODEL, 2), _rowc(D_MODEL, 3), _row(D_MODEL), _row(D_MODEL)],
        out_specs=_row(D_MODEL), out_shape=jax.ShapeDtypeStruct((lp, D_MODEL), BF16),
        compiler_params=_cp(("parallel",)))(gate, gate, y_mla, y_fox)


def _tail(x2, mixed, tgt, gpost):
    lp = mixed.shape[0]
    shift = _shift_rows(D_MODEL)

    def body(h_ref, mx_ref, t_ref, g_ref, dmx_ref, dy_ref, loss_ref, dg_ref):
        i = pl.program_id(0)

        @pl.when(i == 0)
        def _():
            loss_ref[...] = jnp.zeros_like(loss_ref)
            dg_ref[...] = jnp.zeros_like(dg_ref)
            dmx_ref[...] = jnp.zeros_like(dmx_ref)
            dy_ref[...] = jnp.zeros_like(dy_ref)

        @pl.when(i > 0)
        def _():
            mx = mx_ref[...]
            g = g_ref[...]
            r = lax.rsqrt(jnp.mean(mx * mx, axis=-1, keepdims=True) + RMS_EPS)
            nrm = mx * r
            e = (h_ref[...] + nrm * g) - t_ref[...]
            loss_ref[...] += jnp.sum(0.5 * jnp.sum(e * e, axis=-1, keepdims=True) * (1.0 / D_MODEL),
                                     axis=0, keepdims=True)
            dy = e * (1.0 / D_MODEL)
            dy_ref[...] = dy
            dg_ref[...] += jnp.sum(dy * nrm, axis=0, keepdims=True)
            w = dy * g
            dot = jnp.mean(w * mx, axis=-1, keepdims=True)
            dmx_ref[...] = (r * w - mx * (r * r * r * dot)).astype(BF16)

    return pl.pallas_call(
        body, name="tail", grid=(lp // BLK,),
        in_specs=[shift, _row(D_MODEL), shift, _full((1, D_MODEL))],
        out_specs=[_row(D_MODEL), _row(D_MODEL), _full((1, 1)), _full((1, D_MODEL))],
        out_shape=[jax.ShapeDtypeStruct((lp, D_MODEL), BF16), jax.ShapeDtypeStruct((lp, D_MODEL), F32),
                   jax.ShapeDtypeStruct((1, 1), F32), jax.ShapeDtypeStruct((1, D_MODEL), F32)],
        compiler_params=_cp(("arbitrary",)))(x2, mixed, tgt, gpost)


def _merge_bwd(dm, gate, y_mla, y_fox):
    lp = dm.shape[0]

    def body(dm_ref, ga_ref, gb_ref, ym_ref, yf_ref, dym_ref, dyf_ref, dg_ref):
        dm_v = dm_ref[...].astype(F32)
        sa = _sigmoid(ga_ref[...].astype(F32))
        sb = _sigmoid(gb_ref[...].astype(F32))
        dym_ref[...] = (dm_v * sa).astype(BF16)
        dyf_ref[...] = (dm_v * sb).astype(BF16)
        dg_ref[:, 0:D_MODEL] = (dm_v * ym_ref[...] * (sa * (1.0 - sa))).astype(BF16)
        dg_ref[:, D_MODEL:2 * D_MODEL] = (dm_v * yf_ref[...] * (sb * (1.0 - sb))).astype(BF16)

    return pl.pallas_call(
        body, name="merge_bwd", grid=(lp // BLK,),
        in_specs=[_row(D_MODEL), _rowc(D_MODEL, 2), _rowc(D_MODEL, 3), _row(D_MODEL), _row(D_MODEL)],
        out_specs=[_row(D_MODEL), _row(D_MODEL), _row(2 * D_MODEL)],
        out_shape=[jax.ShapeDtypeStruct((lp, D_MODEL), BF16), jax.ShapeDtypeStruct((lp, D_MODEL), BF16),
                   jax.ShapeDtypeStruct((lp, 2 * D_MODEL), BF16)],
        compiler_params=_cp(("parallel",)))(dm, gate, gate, y_mla, y_fox)


def _gate_bwd(da_mla, da_fox, o_mla, o_fox, gate):
    lp = da_mla.shape[0]

    def one(da, o, z, head_of_col):
        sg = _sigmoid(z)
        do = (da * (z * sg)).astype(BF16)
        dz = da * o * (sg * (1.0 + z * (1.0 - sg)))
        delta = sum(_dot(part, head_of_col, 1, 0) for part in _split3(do.astype(F32) * o))
        return do, dz.astype(BF16), delta

    def body(dam_ref, daf_ref, om_ref, of_ref, zm_ref, zf_ref, dom_ref, dof_ref, dz_ref, dlm_ref, dlf_ref):
        f32 = lambda r: r[...].astype(F32)
        head_of_col = (lax.broadcasted_iota(jnp.int32, (D_MODEL, LANES), 0) // HEAD_DIM
                       == lax.broadcasted_iota(jnp.int32, (D_MODEL, LANES), 1)).astype(BF16)
        dom_ref[...], dz_ref[:, 0:D_MODEL], dlm_ref[...] = one(f32(dam_ref), f32(om_ref), f32(zm_ref), head_of_col)
        dof_ref[...], dz_ref[:, D_MODEL:2 * D_MODEL], dlf_ref[...] = one(f32(daf_ref), f32(of_ref), f32(zf_ref),
                                                                        head_of_col)

    return pl.pallas_call(
        body, name="gate_bwd", grid=(lp // BLK,),
        in_specs=[_row(D_MODEL)] * 4 + [_rowc(D_MODEL, 0), _rowc(D_MODEL, 1)],
        out_specs=[_row(D_MODEL), _row(D_MODEL), _row(2 * D_MODEL), _row(LANES), _row(LANES)],
        out_shape=[jax.ShapeDtypeStruct((lp, D_MODEL), BF16), jax.ShapeDtypeStruct((lp, D_MODEL), BF16),
                   jax.ShapeDtypeStruct((lp, 2 * D_MODEL), BF16), jax.ShapeDtypeStruct((lp, LANES), F32),
                   jax.ShapeDtypeStruct((lp, LANES), F32)],
        compiler_params=_cp(("parallel",)))(da_mla, da_fox, o_mla, o_fox, gate, gate)


def _small_bwd(small, dqn, dkvn, dkr, dcol_t, drow_t, gq, gkv, fb, ctab, stab, triu):
    lp = small.shape[0]
    nb = lp // BLK

    def rrow(w):
        return pl.BlockSpec((BLK, w), lambda i: (nb - 1 - i, 0))

    def body(sm_ref, dqn_ref, dkvn_ref, dkr_ref, dcol_ref, drow_ref, gq_ref, gkv_ref, fb_ref, c_ref, s_ref, tri_ref,
             ds_ref, dgq_ref, dgkv_ref, dfb_ref, carry):
        i = pl.program_id(0)

        @pl.when(i == 0)
        def _():
            carry[...] = jnp.zeros_like(carry)
            dgq_ref[...] = jnp.zeros_like(dgq_ref)
            dgkv_ref[...] = jnp.zeros_like(dgkv_ref)
            dfb_ref[...] = jnp.zeros_like(dfb_ref)

        def norm_bwd(x, dn, g, dg_ref):
            r = lax.rsqrt(jnp.mean(x * x, axis=-1, keepdims=True) + RMS_EPS)
            dg_ref[...] += jnp.sum(dn * (x * r), axis=0, keepdims=True)
            w = dn * g
            dot = jnp.mean(w * x, axis=-1, keepdims=True)
            return r * w - x * (r * r * r * dot)

        ds_ref[:, 0:256] = norm_bwd(sm_ref[:, 0:256], dqn_ref[...], gq_ref[...], dgq_ref).astype(BF16)
        ds_ref[:, 256:384] = norm_bwd(sm_ref[:, 256:384], dkvn_ref[...], gkv_ref[...], dgkv_ref).astype(BF16)

        dk = dkr_ref[0]
        for p in range(1, PAIRS):
            dk = dk + dkr_ref[p]
        dk = _rope(dk, c_ref[...], -s_ref[...])
        lane = lax.broadcasted_iota(jnp.int32, dk.shape, 1)
        dk = jnp.where(lane < MLA_ROPE, dk + pltpu.roll(dk, LANES - MLA_ROPE, 1), 0.0)
        ds_ref[:, 384:512] = dk.astype(BF16)

        dcol = dcol_ref[0]
        for p in range(1, PAIRS):
            dcol = dcol + pltpu.roll(dcol_ref[p], 2 * p, 1)
        rows16 = jnp.concatenate([drow_ref[p, h:h + 1, :] for p in range(PAIRS) for h in range(2)], axis=0)
        eye = (lax.broadcasted_iota(jnp.int32, (HEADS, LANES), 0)
               == lax.broadcasted_iota(jnp.int32, (HEADS, LANES), 1)).astype(BF16)
        drow = sum(_dot(part, eye, 0, 0) for part in _split3(rows16))
        dcr = dcol - drow
        hi, mid, lo = _split3(dcr)
        t = tri_ref[...]
        suf = (_dot(t, hi, 1, 0) + _dot(t, mid, 1, 0)) + _dot(t, lo, 1, 0) + carry[...]
        fl = sm_ref[:, 512:640] + fb_ref[...]
        dfl = jnp.where(_row_valid(nb - 1 - i), -suf * _sigmoid(-fl), 0.0)
        ds_ref[:, 512:640] = dfl.astype(BF16)
        dfb_ref[...] += jnp.sum(dfl, axis=0, keepdims=True)
        carry[...] += jnp.sum(dcr, axis=0, keepdims=True)

    return pl.pallas_call(
        body, name="small_bwd", grid=(nb,),
        in_specs=[rrow(SMALL_W), rrow(256), rrow(128),
                  pl.BlockSpec((PAIRS, BLK, 128), lambda i: (0, nb - 1 - i, 0)),
                  pl.BlockSpec((PAIRS, BLK, 128), lambda i: (0, nb - 1 - i, 0)),
                  pl.BlockSpec((PAIRS, 2, BLK), lambda i: (0, 0, nb - 1 - i)),
                  _full((1, 256)), _full((1, 128)), _full((1, 128)), rrow(128), rrow(128), _full((BLK, BLK))],
        out_specs=[rrow(SMALL_W), _full((1, 256)), _full((1, 128)), _full((1, 128))],
        out_shape=[jax.ShapeDtypeStruct((lp, SMALL_W), BF16), jax.ShapeDtypeStruct((1, 256), F32),
                   jax.ShapeDtypeStruct((1, 128), F32), jax.ShapeDtypeStruct((1, 128), F32)],
        scratch_shapes=[pltpu.VMEM((1, 128), F32)],
        compiler_params=_cp(("arbitrary",)))(small, dqn, dkvn, dkr, dcol_t, drow_t, gq, gkv, fb, ctab, stab, triu)


def _pre_bwd(du, x2, meta, dy, gpre):
    s_rows = x2.shape[0]
    lp = PAD + s_rows
    shift = _shift_rows(D_MODEL)

    def body(du_ref, x_ref, meta_ref, dy_ref, g_ref, dx_ref, dmeta_ref, dg_ref):
        i = pl.program_id(0)

        @pl.when(i == 0)
        def _():
            dg_ref[...] = jnp.zeros_like(dg_ref)

        hv = _h_block(i, x_ref, meta_ref)
        duv = du_ref[...]
        r = lax.rsqrt(jnp.mean(hv * hv, axis=-1, keepdims=True) + RMS_EPS)
        dg_ref[...] += jnp.sum(duv * (hv * r), axis=0, keepdims=True)
        w = duv * g_ref[...]
        dot = jnp.mean(w * hv, axis=-1, keepdims=True)
        dh = dy_ref[...] + (r * w - hv * (r * r * r * dot))
        dx_ref[...] = dh

        @pl.when(i == 0)
        def _():
            dmeta_ref[...] = dh[0:N_META, :]

    return pl.pallas_call(
        body, name="pre_bwd", grid=(lp // BLK,),
        in_specs=[_row(D_MODEL), shift, _full((N_META, D_MODEL)), _row(D_MODEL), _full((1, D_MODEL))],
        out_specs=[shift, _full((N_META, D_MODEL)), _full((1, D_MODEL))],
        out_shape=[jax.ShapeDtypeStruct((s_rows, D_MODEL), F32), jax.ShapeDtypeStruct((N_META, D_MODEL), F32),
                   jax.ShapeDtypeStruct((1, D_MODEL), F32)],
        compiler_params=_cp(("arbitrary",)))(du, x2, meta, dy, gpre)


def _pair_masks(rope, pair):
    lane = lax.broadcasted_iota(jnp.int32, (1, LANES), 1)
    mas = [lane < HEAD_DIM, lane >= HEAD_DIM]
    wide = lax.broadcasted_iota(jnp.int32, (1, 2 * LANES), 1)
    extra = MLA_ROPE if rope else BIAS_PARTS
    lo = LANES if rope else LANES + 2 * BIAS_PARTS * pair
    mid = lo + extra
    return mas, [(wide < HEAD_DIM) | ((wide >= lo) & (wide < mid)),
                 ((wide >= HEAD_DIM) & (wide < LANES)) | ((wide >= mid) & (wide < mid + extra))]


def _mask2(x, masks):
    return [jnp.where(m, x, jnp.zeros_like(x)) for m in masks]


def _q_heads(q_rows, rope, mas, hmask):
    if rope:
        return _mask2(q_rows, hmask)
    zero = jnp.zeros((q_rows.shape[0], LANES), BF16)
    return [jnp.concatenate([jnp.where(m, q_rows, zero), jnp.where(hm[:, LANES:], zero + 1, zero)], axis=1)
            for m, hm in zip(mas, hmask)]


def _transposed_cols(x, group, name):
    lp = x.shape[0]

    def body(x_ref, o_ref):
        o_ref[...] = x_ref[...].T

    cols = 2 * BLK
    per_group = D_MODEL // cols
    return pl.pallas_call(
        body, name=name, grid=(per_group,),
        in_specs=[pl.BlockSpec((lp, cols), lambda i: (0, per_group * group + i))],
        out_specs=pl.BlockSpec((cols, lp), lambda i: (i, 0)),
        out_shape=jax.ShapeDtypeStruct((D_MODEL, lp), x.dtype),
        compiler_params=_cp(("parallel",), MM_VMEM_BUDGET))(x)


def _attn_fwd(q, k, v, vt, k2, *, rope, qcol, kcol, vcol, name):
    lp = q.shape[0]
    nq = 1 + (lp - PAD) // QB
    qw = 256 if rope else 128

    def body(q_ref, k_ref, v_ref, vt_ref, k2_ref, o_ref, lse_ref):
        i = pl.program_id(1)
        r0 = pl.multiple_of(jnp.where(i == 0, 0, PAD + QB * (i - 1)), BLK)
        b0 = r0 // BLK
        mas, hmask = _pair_masks(rope, pl.program_id(0))
        qh = _q_heads(q_ref[pl.ds(r0, QB), :], rope, mas, hmask)

        def update(chunks, tiles, groups):
            m = [[cr[0], cr[2]] for _, _, cr in groups]
            l = [[cr[1], cr[3]] for _, _, cr in groups]
            acc = [[cr[4][0:HEAD_DIM], cr[4][HEAD_DIM:LANES]] for _, _, cr in groups]
            qs = [[x[q_lo:q_lo + wq] for x in qh] for q_lo, wq, _ in groups]
            k0s = [pl.multiple_of(kc * BLK, BLK) for kc, _ in chunks]
            kks = [jnp.concatenate([k_ref[pl.ds(k0, n), :], k2_ref[pl.ds(k0, n), :]], axis=1)
                   for k0, (_, n) in zip(k0s, chunks)]
            jobs = [(g, ci, mask, h) for g, ci, mask in tiles for h in range(2)]
            score = lambda t: _dot(kks[jobs[t][1]], qs[jobs[t][0]][jobs[t][3]], 1, 1)
            ss = [score(t) for t in range(min(AHEAD, len(jobs)))]
            for t, (g, ci, mask, h) in enumerate(jobs):
                if t + AHEAD < len(jobs):
                    ss.append(score(t + AHEAD))
                s = ss[t] if mask is None else jnp.where(mask, ss[t], NEG)
                m_new = jnp.maximum(m[g][h], jnp.max(s, axis=0, keepdims=True))
                alpha = jnp.exp2(m[g][h] - m_new)
                p = jnp.exp2(s - m_new)
                l[g][h] = alpha * l[g][h] + jnp.sum(p, axis=0, keepdims=True)
                m[g][h] = m_new
                n = chunks[ci][1]
                if n == BLK:
                    pv = _dot(vt_ref[pl.ds(HEAD_DIM * h, HEAD_DIM), pl.ds(k0s[ci], BLK)], p.astype(BF16), 1, 0)
                else:
                    vm = jnp.where(mas[h], v_ref[0:n, :], jnp.zeros((), BF16))
                    pv = _dot(vm, p.astype(BF16), 0, 0)[HEAD_DIM * h:HEAD_DIM * (h + 1)]
                acc[g][h] = alpha * acc[g][h] + pv
            return [(m[g][0], l[g][0], m[g][1], l[g][1], jnp.concatenate(acc[g], axis=0)) for g in range(len(groups))]

        def full_chunks(kcs, carry):
            return update([(kc, BLK) for kc in kcs], [(0, ci, None) for ci in range(len(kcs))], [(0, QB, carry)])[0]

        neg = jnp.full((1, QB), NEG, F32)
        zero = jnp.zeros((1, QB), F32)
        c = (neg, zero, neg, zero, jnp.zeros((LANES, QB), F32))
        n_mid = jnp.maximum(b0 - 1, 0)
        c = lax.fori_loop(0, n_mid // 4, lambda t, cr: full_chunks([4 * t + u for u in (1, 2, 3, 4)], cr), c)
        c = lax.fori_loop(0, (n_mid % 4) // 2, lambda t, cr: full_chunks([n_mid - 1, n_mid], cr), c)
        key_l = lax.broadcasted_iota(jnp.int32, (BLK, BLK), 0)
        qry_l = lax.broadcasted_iota(jnp.int32, (BLK, BLK), 1)
        tri = (key_l <= qry_l) & (b0 > 0)
        meta_ok = (key_l[0:N_META] <= qry_l[0:N_META]) | (b0 > 0)
        lo, hi = update([(0, N_META), (b0, BLK), (b0 + 1, BLK)],
                        [(0, 0, meta_ok), (1, 0, None), (0, 1, tri), (1, 1, None), (1, 2, tri)],
                        [(0, BLK, tuple(a[:, 0:BLK] for a in c)), (BLK, QB - BLK, tuple(a[:, BLK:QB] for a in c))])
        c = tuple(jnp.concatenate([a, b], axis=1) for a, b in zip(lo, hi))
        inv =jnp.concatenate([jnp.broadcast_to(1.0 / c[1], (HEAD_DIM, QB)),
                               jnp.broadcast_to(1.0 / c[3], (HEAD_DIM, QB))], axis=0)
        o_t = (c[4] * inv).T.astype(BF16)
        lses = [c[2 * h] + jnp.log(c[2 * h + 1]) * LOG2E for h in range(2)]
        o_ref[pl.ds(r0, BLK), :] = o_t[0:BLK]
        for h in range(2):
            lse_ref[0, h:h + 1, pl.ds(r0, BLK)] = lses[h][:, 0:BLK]

        @pl.when(i > 0)
        def _():
            r1 = pl.multiple_of(r0 + BLK, BLK)
            o_ref[pl.ds(r1, QB - BLK), :] = o_t[BLK:QB]
            for h in range(2):
                lse_ref[0, h:h + 1, pl.ds(r1, QB - BLK)] = lses[h][:, BLK:QB]

    in_specs = [pl.BlockSpec((lp, qw), lambda p, i: (0, qcol + p)),
                pl.BlockSpec((lp, 128), lambda p, i: (0, kcol(p))),
                pl.BlockSpec((BLK, 128), lambda p, i: (0, vcol(p))),
                pl.BlockSpec((128, lp), lambda p, i: (p, 0)),
                pl.BlockSpec((lp, 128), lambda p, i: (0, 0))]
    return pl.pallas_call(
        body, name=name, grid=(PAIRS, nq), in_specs=in_specs,
        out_specs=[pl.BlockSpec((lp, 128), lambda p, i: (0, p)),
                   pl.BlockSpec((1, 2, lp), lambda p, i: (p, 0, 0))],
        out_shape=[jax.ShapeDtypeStruct((lp, D_MODEL), BF16), jax.ShapeDtypeStruct((PAIRS, 2, lp), F32)],
        compiler_params=_cp(("parallel", "arbitrary"), VMEM_BIG))(q, k, v, vt, k2)


def _attn_bwd(q, k, v, k2, do, delta, lse, *, rtabs=None, scale, qcol, kcol, vcol, name):
    lp = q.shape[0]
    nb = lp // BLK
    rope = rtabs is not None
    bias = not rope
    qw = 256 if rope else 128

    def body(*refs):
        it = iter(refs)
        q_ref, k_ref, v_ref, k2_ref = next(it), next(it), next(it), next(it)
        do_ref, dl_ref, lse_ref = next(it), next(it), next(it)
        ct_ref, st_ref = (next(it), next(it)) if rope else (None, None)
        dq_out, dk_ref, dv_ref = next(it), next(it), next(it)
        x_ref = next(it)
        drow_ref = next(it) if bias else None
        dq_ref = next(it)
        kb = pl.program_id(1)
        mas, hmask = _pair_masks(rope, pl.program_id(0))
        lane = lax.broadcasted_iota(jnp.int32, (1, LANES), 1)

        @pl.when(kb == 0)
        def _():
            dq_ref[...] = jnp.zeros_like(dq_ref)
            if bias:
                drow_ref[...] = jnp.zeros_like(drow_ref)

        def key_pass(n, w):
            kk = jnp.concatenate([k_ref[0:n, :], k2_ref[0:n, :]], axis=1)
            vh = _mask2(v_ref[0:n, :], mas)
            kcat = jnp.concatenate([x[:, 0:qw] for x in _mask2(kk, hmask)], axis=0)
            diag_mask = (lax.broadcasted_iota(jnp.int32, (n, w), 0) <= lax.broadcasted_iota(jnp.int32, (n, w), 1))

            def front(qc):
                q0 = qc * w if isinstance(qc, int) else pl.multiple_of(qc * w, w)
                dov = do_ref[pl.ds(q0, w), :]
                qh = _q_heads(q_ref[pl.ds(q0, w), :], rope, mas, hmask)
                ss, dps = [], []
                for h in range(2):
                    ss.append(_dot(kk, qh[h], 1, 1))
                    dps.append(_dot(vh[h], dov, 1, 1))
                return q0, dov, qh, ss, dps

            def back(fronted, carry, mask):
                carry = list(carry)
                q0, dov, qh, ss, dps = fronted
                doh = _mask2(dov, mas)
                dq = 0.0
                for h in range(2):
                    p = jnp.exp2(ss[h] - lse_ref[0, h:h + 1, pl.ds(q0, w)])
                    if mask is not None:
                        p = jnp.where(mask, p, 0.0)
                    ds = p * (dps[h] - dl_ref[0, h:h + 1, pl.ds(q0, w)])
                    if bias:
                        drow_ref[0, h:h + 1, pl.ds(q0, w)] += jnp.sum(ds, axis=0, keepdims=True)
                        carry[2 + h] = carry[2 + h] + jnp.sum(ds, axis=1, keepdims=True)
                    dsb = ds.astype(BF16)
                    carry[1] = carry[1] + _dot(p.astype(BF16), doh[h], 1, 0)
                    carry[0] = carry[0] + _dot(dsb, qh[h][:, 0:qw], 1, 0)
                    dq = dq + _dot(dsb, kcat[h * n:(h + 1) * n], 0, 0)
                dq_ref[pl.ds(q0, w), :] += dq
                return tuple(carry)

            def chunks(qcs, carry, masks):
                ahead = AHEAD_BWD
                fronted = [front(qc) for qc in qcs[:ahead]]
                for u, mask in enumerate(masks):
                    if u + ahead < len(qcs):
                        fronted.append(front(qcs[u + ahead]))
                    carry = back(fronted[u], carry, mask)
                return carry

            c = [jnp.zeros((n, qw), F32), jnp.zeros((n, LANES), F32)]
            if bias:
                c += [jnp.zeros((n, 1), F32), jnp.zeros((n, 1), F32)]
            c = tuple(c)
            if w != BLK:
                c = chunks(list(range(lp // w)), c, [diag_mask] + [None] * (lp // w - 1))
            else:
                groups = (nb - kb) // UNROLL

                def several(t, cr):
                    return chunks([kb + UNROLL * t + u for u in range(UNROLL)], cr,
                                  [diag_mask | (t > 0)] + [None] * (UNROLL - 1))

                c = lax.fori_loop(0, groups, several, c)
                start = kb + UNROLL * groups
                pairs = (nb - start) // 2

                def two(t, cr):
                    qc = start + 2 * t
                    return chunks([qc, qc + 1], cr, [diag_mask | (qc > kb), None])

                c = lax.fori_loop(0, pairs, two, c)
                c = lax.fori_loop(start + 2 * pairs, nb, lambda qc, cr: chunks([qc], cr, [diag_mask | (qc > kb)]), c)

            def rows(a, dtype):
                a = a.astype(dtype)
                return a if n == BLK else jnp.concatenate([a, jnp.zeros((BLK - n, a.shape[1]), dtype)], axis=0)

            dk = c[0] * LN2
            dk_ref[...] = rows(dk[:, 0:LANES], BF16)
            dv_ref[...] = rows(c[1], BF16)
            if rope:
                x_ref[0] = rows(dk[:, LANES:2 * LANES], F32)
            if bias:
                x_ref[0] = rows(jnp.where(lane == 0, c[2], jnp.where(lane == 1, c[3], 0.0)), F32)

        @pl.when(kb == 0)
        def _():
            key_pass(N_META, lp // 2)

        @pl.when(kb > 0)
        def _():
            key_pass(BLK, BLK)

        @pl.when(kb == nb - 1)
        def _():
            def fin(c, carry):
                r0 = pl.multiple_of(c * BLK, BLK)
                dq = dq_ref[pl.ds(r0, BLK), :] * scale
                if rope:
                    back = _rope(dq[:, LANES:2 * LANES], ct_ref[pl.ds(r0, BLK), :], -st_ref[pl.ds(r0, BLK), :])
                    dq = jnp.concatenate([dq[:, 0:LANES], back], axis=1)
                dq_out[pl.ds(r0, BLK), :] = dq.astype(BF16)
                return carry

            lax.fori_loop(0, nb, fin, 0)

    in_specs = [pl.BlockSpec((lp, qw), lambda p, j: (0, qcol + p)),
                pl.BlockSpec((BLK, 128), lambda p, j: (j, kcol(p))),
                pl.BlockSpec((BLK, 128), lambda p, j: (j, vcol(p))),
                pl.BlockSpec((BLK, 128), lambda p, j: (j, 0)),
                pl.BlockSpec((lp, 128), lambda p, j: (0, p)), pl.BlockSpec((1, 2, lp), lambda p, j: (p, 0, 0)),
                pl.BlockSpec((1, 2, lp), lambda p, j: (p, 0, 0))]
    ins = [q, k, v, k2, do, delta, lse]
    if rope:
        in_specs += [pl.BlockSpec((lp, 128), lambda p, j: (0, 0))] * 2
        ins += list(rtabs)
    out_specs = [pl.BlockSpec((lp, qw), lambda p, j: (0, p)),
                 pl.BlockSpec((BLK, 128), lambda p, j: (j, p)),
                 pl.BlockSpec((BLK, 128), lambda p, j: (j, p)),
                 pl.BlockSpec((1, BLK, 128), lambda p, j: (p, j, 0))]
    out_shape = [jax.ShapeDtypeStruct((lp, PAIRS * qw), BF16), jax.ShapeDtypeStruct((lp, D_MODEL), BF16),
                 jax.ShapeDtypeStruct((lp, D_MODEL), BF16), jax.ShapeDtypeStruct((PAIRS, lp, 128), F32)]
    if bias:
        out_specs.append(pl.BlockSpec((1, 2, lp), lambda p, j: (p, 0, 0)))
        out_shape.append(jax.ShapeDtypeStruct((PAIRS, 2, lp), F32))
    return pl.pallas_call(
        body, name=name, grid=(PAIRS, nb), in_specs=in_specs, out_specs=out_specs, out_shape=out_shape,
        scratch_shapes=[pltpu.VMEM((lp, qw), F32)],
        compiler_params=_cp(("parallel", "arbitrary"), VMEM_BIG))(*ins)


def _adamw(w, g, m, v, name):
    lead = w.ndim - 2
    rows, cols = w.shape[lead:]
    big = rows * cols > 512 * 1024
    tr = 128 if big and rows % 128 == 0 else rows
    tc = 256 if big and tr == rows else cols

    def body(w_ref, g_ref, m_ref, v_ref, d_ref, nm_ref, nv_ref):
        gv = g_ref[...]
        nm = ADAM_B1 * m_ref[...] + (1.0 - ADAM_B1) * gv
        nv = ADAM_B2 * v_ref[...] + (1.0 - ADAM_B2) * (gv * gv)
        m_hat = nm / (1.0 - ADAM_B1 ** ADAM_STEP)
        v_hat = nv / (1.0 - ADAM_B2 ** ADAM_STEP)
        d_ref[...] = -ADAM_LR * (m_hat / (jnp.sqrt(v_hat) + ADAM_EPS) + ADAM_WD * w_ref[...])
        nm_ref[...] = nm
        nv_ref[...] = nv

    spec = pl.BlockSpec((1,) * lead + (tr, tc), lambda i, j: (0,) * lead + (i, j))
    return pl.pallas_call(
        body, name=name, grid=(rows // tr, cols // tc), in_specs=[spec] * 4, out_specs=[spec] * 3,
        out_shape=[jax.ShapeDtypeStruct(w.shape, F32)] * 3,
        compiler_params=_cp(("parallel", "parallel"), VMEM_BIG))(w, g, m, v)


def _add_cores(g, from_sib, name):
    n, rows, cols = g.shape
    half = rows // 2
    tr = _tile(half, (256, 240))
    nt = half // tr

    def body(lo_ref, hi_ref, s_ref, o_ref):
        mine = jnp.where(lax.axis_index("c") == 0, lo_ref[0], hi_ref[0])
        o_ref[0] = (mine.astype(F32) + s_ref[0].astype(F32)).astype(BF16)

    return pl.pallas_call(
        body, name=name, grid=(n, nt),
        in_specs=[pl.BlockSpec((1, tr, cols), lambda j, i: (j, i, 0)),
                  pl.BlockSpec((1, tr, cols), lambda j, i: (j, nt + i, 0)),
                  pl.BlockSpec((1, tr, cols), lambda j, i: (j, i, 0))],
        out_specs=pl.BlockSpec((1, tr, cols), lambda j, i: (j, i, 0)),
        out_shape=jax.ShapeDtypeStruct((n, half, cols), BF16),
        compiler_params=_cp(("parallel", "parallel"), VMEM_BIG))(g, g, from_sib)


def _add_chips(x, own, name):
    n, rows, cols = x.shape
    tr = _tile(rows, (256, 240))

    def body(x_ref, own_ref, o_ref):
        me = 2 * lax.axis_index("x") + lax.axis_index("y")
        v = [jnp.where(me == k, own_ref[...], x_ref[k]).astype(F32) for k in range(N_CHIPS)]
        o_ref[...] = ((v[0] + v[1]) + v[2]) + v[3]

    return pl.pallas_call(
        body, name=name, grid=(rows // tr,),
        in_specs=[pl.BlockSpec((n, tr, cols), lambda i: (0, i, 0)), pl.BlockSpec((tr, cols), lambda i: (i, 0))],
        out_specs=pl.BlockSpec((tr, cols), lambda i: (i, 0)),
        out_shape=jax.ShapeDtypeStruct((rows, cols), F32), compiler_params=_cp(("parallel",), VMEM_BIG))(x, own)


def _axes():
    return lax.axis_index("x"), lax.axis_index("y"), lax.axis_index("c")


def _other_chips(x, y):
    return [(1 - x, y), (x, 1 - y), (1 - x, 1 - y)]


ANY = pl.BlockSpec(memory_space=pl.ANY)


def _rcopy(src, dst, send_sems, recv_sems, k, to):
    return pltpu.make_async_remote_copy(src_ref=src, dst_ref=dst, send_sem=send_sems.at[k], recv_sem=recv_sems.at[k],
                                        device_id=to, device_id_type=MESH)


def _gather_weights(shards, meta):
    n = len(shards)

    def body(*refs):
        srcs, meta_ref = refs[:n], refs[n]
        outs, mout_ref = refs[n + 1:2 * n + 1], refs[2 * n + 1]
        send_sems, recv_sems = refs[2 * n + 2:]
        x, y, c = _axes()
        me = 2 * x + y
        sib = (x, y, 1 - c)
        chips = _other_chips(x, y)

        def half(t, chip_idx, cc):
            hr = shards[t].shape[0] // 2
            return outs[t].at[chip_idx, pl.ds(cc * hr, hr), :]

        first = []
        for j, (px, py) in enumerate(chips):
            for t in range(n):
                hr = shards[t].shape[0] // 2
                first.append(_rcopy(srcs[t].at[pl.ds(c * hr, hr), :], half(t, me, c), send_sems, recv_sems,
                                    3 * t + j, (px, py, c)))
            first.append(_rcopy(meta_ref, mout_ref.at[me], send_sems, recv_sems, 3 * n + j, (px, py, c)))
        for cp in first:
            cp.start()
        passed = []
        for j, (px, py) in enumerate(chips):
            src_chip = 2 * px + py
            for t in range(n):
                _rcopy(half(t, src_chip, c), half(t, src_chip, c), send_sems, recv_sems, 3 * t + j, sib).wait_recv()
                fwd = _rcopy(half(t, src_chip, c), half(t, src_chip, c), send_sems, recv_sems, 3 * (n + 1 + t) + j, sib)
                fwd.start()
                passed.append(fwd)
            _rcopy(mout_ref.at[src_chip], mout_ref.at[src_chip], send_sems, recv_sems, 3 * n + j, sib).wait_recv()
        for j, (px, py) in enumerate(chips):
            src_chip = 2 * px + py
            for t in range(n):
                _rcopy(half(t, src_chip, 1 - c), half(t, src_chip, 1 - c), send_sems, recv_sems,
                       3 * (n + 1 + t) + j, sib).wait_recv()
        for cp in first + passed:
            cp.wait_send()

    nsem = 3 * (2 * n + 1)
    return pl.pallas_call(
        body, name="gather_weights", in_specs=[ANY] * (n + 1), out_specs=[ANY] * (n + 1),
        out_shape=[jax.ShapeDtypeStruct((N_CHIPS,) + s.shape, s.dtype) for s in shards]
        + [jax.ShapeDtypeStruct((N_CHIPS,) + meta.shape, meta.dtype)],
        scratch_shapes=[pltpu.SemaphoreType.DMA((nsem,)), pltpu.SemaphoreType.DMA((nsem,))])(*shards, meta)


def _gather_late(shard):
    rows, cols = shard.shape
    hr = rows // 2
    src = jax.new_ref(shard, memory_space=pltpu.MemorySpace.HBM)
    out = jax.empty_ref(jax.ShapeDtypeStruct((N_CHIPS, rows, cols), shard.dtype), memory_space=pltpu.MemorySpace.HBM)

    @pl.kernel(mesh=plsc.ScalarSubcoreMesh(axis_name="seq", num_cores=1), name="gather_late",
               scratch_types=(pltpu.SemaphoreType.DMA((6,)), pltpu.SemaphoreType.DMA((6,))),
               compiler_params=pltpu.CompilerParams(collective_id=1))
    def launch(send_sems, recv_sems):
        x, y, c = _axes()
        me = 2 * x + y
        sib = (x, y, 1 - c)
        chips = _other_chips(x, y)
        barrier = pltpu.get_barrier_semaphore()
        for px, py in chips:
            pl.semaphore_signal(barrier, inc=1, device_id=(px, py, c), device_id_type=MESH)
        pl.semaphore_signal(barrier, inc=1, device_id=sib, device_id_type=MESH)
        pl.semaphore_wait(barrier, 4)

        def half(chip_idx, cc):
            return out.at[chip_idx, pl.ds(cc * hr, hr), :]

        first = [_rcopy(src.at[pl.ds(c * hr, hr), :], half(me, c), send_sems, recv_sems, j, (px, py, c))
                 for j, (px, py) in enumerate(chips)]
        for cp in first:
            cp.start()
        passed = []
        for j, (px, py) in enumerate(chips):
            land = half(2 * px + py, c)
            _rcopy(land, land, send_sems, recv_sems, j, sib).wait_recv()
            fwd = _rcopy(land, land, send_sems, recv_sems, 3 + j, sib)
            fwd.start()
            passed.append(fwd)
        for j, (px, py) in enumerate(chips):
            land = half(2 * px + py, 1 - c)
            _rcopy(land, land, send_sems, recv_sems, 3 + j, sib).wait_recv()
        for cp in first + passed:
            cp.wait_send()

    launch()
    return out[...]


def _swap_halves(gs):
    n = len(gs)
    ncopies = sum(g.shape[0] for g in gs)

    def body(*refs):
        srcs, outs = refs[:n], refs[n:2 * n]
        send_sems, recv_sems = refs[2 * n:]
        x, y, c = _axes()
        cps = []
        for t in range(n):
            hr = gs[t].shape[1] // 2
            for j in range(gs[t].shape[0]):
                cps.append(_rcopy(srcs[t].at[j, pl.ds((1 - c) * hr, hr), :], outs[t].at[j], send_sems, recv_sems,
                                  len(cps), (x, y, 1 - c)))
        for cp in cps:
            cp.start()
        for cp in cps:
            cp.wait()

    return pl.pallas_call(
        body, name="swap_halves", in_specs=[ANY] * n, out_specs=[ANY] * n,
        out_shape=[jax.ShapeDtypeStruct((g.shape[0], g.shape[1] // 2, g.shape[2]), g.dtype) for g in gs],
        scratch_shapes=[pltpu.SemaphoreType.DMA((ncopies,)), pltpu.SemaphoreType.DMA((ncopies,))])(*gs)


def _scatter_chips(parts):
    n = len(parts)
    srcs = [jax.new_ref(p, memory_space=pltpu.MemorySpace.HBM) for p in parts]
    outs = [jax.empty_ref(jax.ShapeDtypeStruct(p.shape, p.dtype), memory_space=pltpu.MemorySpace.HBM) for p in parts]

    @pl.kernel(mesh=plsc.ScalarSubcoreMesh(axis_name="seq", num_cores=1), name="scatter_chips",
               scratch_types=(pltpu.SemaphoreType.DMA((3 * n,)), pltpu.SemaphoreType.DMA((3 * n,))),
               compiler_params=pltpu.CompilerParams(collective_id=0))
    def launch(send_sems, recv_sems):
        x, y, c = _axes()
        me = 2 * x + y
        chips = _other_chips(x, y)
        barrier = pltpu.get_barrier_semaphore()
        for px, py in chips:
            pl.semaphore_signal(barrier, inc=1, device_id=(px, py, c), device_id_type=MESH)
        pl.semaphore_wait(barrier, 3)
        cps = []
        for j, (px, py) in enumerate(chips):
            for t in range(n):
                cps.append(_rcopy(srcs[t].at[2 * px + py], outs[t].at[me], send_sems, recv_sems, 3 * t + j,
                                  (px, py, c)))
        for cp in cps:
            cp.start()
        for cp in cps:
            cp.wait()

    launch()
    return [o[...] for o in outs]


def _swap_reduced(rs):
    n = len(rs)

    def body(*refs):
        srcs, outs = refs[:n], refs[n:2 * n]
        send_sems, recv_sems = refs[2 * n:]
        x, y, c = _axes()
        cps = [_rcopy(srcs[t], outs[t], send_sems, recv_sems, t, (x, y, 1 - c)) for t in range(n)]
        for cp in cps:
            cp.start()
        for cp in cps:
            cp.wait()

    return pl.pallas_call(
        body, name="swap_reduced", in_specs=[ANY] * n, out_specs=[ANY] * n,
        out_shape=[jax.ShapeDtypeStruct(r.shape, r.dtype) for r in rs],
        scratch_shapes=[pltpu.SemaphoreType.DMA((n,)), pltpu.SemaphoreType.DMA((n,))])(*rs)


SMALL_ROWS = 24 + 128


def _allreduce_small(vec):
    def body(v_ref, out_ref, slots, send_sems, recv_sems):
        x, y, c = _axes()
        me = 4 * x + 2 * y + c
        slots[me] = v_ref[...]
        cps = []
        for k in range(1, 8):
            kx, ky, kc = (k >> 2) & 1, (k >> 1) & 1, k & 1
            peer = (1 - x if kx else x, 1 - y if ky else y, 1 - c if kc else c)
            cps.append(_rcopy(v_ref, slots.at[me], send_sems, recv_sems, k - 1, peer))
        for cp in cps:
            cp.start()
        for cp in cps:
            cp.wait()
        tot = slots[0]
        for k in range(1, 8):
            tot = tot + slots[k]
        out_ref[...] = tot

    return pl.pallas_call(
        body, name="allreduce_small",
        in_specs=[pl.BlockSpec(memory_space=pltpu.VMEM)], out_specs=pl.BlockSpec(memory_space=pltpu.VMEM),
        out_shape=jax.ShapeDtypeStruct((SMALL_ROWS, 128), F32),
        scratch_shapes=[pltpu.VMEM((8, SMALL_ROWS, 128), F32), pltpu.SemaphoreType.DMA((7,)),
                        pltpu.SemaphoreType.DMA((7,))])(vec)


def _pack_p2(w_uq, w_ukv, w_br_mla, w_br_fox, w_out, dtype):
    parts = [w_uq.reshape(96, D_MODEL), w_ukv.reshape(64, D_MODEL), w_br_mla, w_br_fox, w_out]
    return jnp.concatenate([p.astype(dtype) for p in parts], axis=0)


def _unpack_p2(pk):
    return pk[0:96].reshape(256, 384), pk[96:160].reshape(128, 512), pk[160:416], pk[416:672], pk[672:928]


def _uq_arrange(w):
    w3 = w.reshape(256, HEADS, 96)
    nope = w3[:, :, :64].reshape(256, PAIRS, 128)
    pe = w3[:, :, 64:].reshape(256, PAIRS, 64)
    return jnp.concatenate([nope, pe, jnp.zeros((256, PAIRS, 64), w.dtype)], axis=2).reshape(256, PAIRS * 256)


def _uq_restore(g):
    g3 = g.reshape(256, PAIRS, 256)
    nope = g3[:, :, :128].reshape(256, HEADS, 64)
    pe = g3[:, :, 128:192].reshape(256, HEADS, 32)
    return jnp.concatenate([nope, pe], axis=2).reshape(256, HEADS * 96)


def _ukv_arrange(w):
    w3 = w.reshape(128, HEADS, 128)
    return jnp.concatenate([w3[:, :, :64].reshape(128, 1024), w3[:, :, 64:].reshape(128, 1024)], axis=1)


def _ukv_restore(g):
    kn = g[:, :1024].reshape(128, HEADS, 64)
    vv = g[:, 1024:].reshape(128, HEADS, 64)
    return jnp.concatenate([kn, vv], axis=2).reshape(128, HEADS * 128)


def _rope_tables(lp):
    r = np.arange(lp)
    pos = np.where(r < N_META, r, np.where(r >= PAD, r - PAD + N_META, 0)).astype(np.float32)
    half = MLA_ROPE // 2
    inv_freq = np.float32(ROPE_THETA) ** (-np.arange(half, dtype=np.float32) / np.float32(half))
    ang = (pos[:, None] * inv_freq[None, :]).astype(np.float32)
    cos, sin = np.cos(ang).astype(np.float32), np.sin(ang).astype(np.float32)
    one, zero = np.ones((lp, 64), np.float32), np.zeros((lp, 64), np.float32)
    return (jnp.asarray(np.concatenate([cos, cos, cos, cos, one], axis=1)),
            jnp.asarray(np.concatenate([-sin, sin, -sin, sin, zero], axis=1)))


def _pad_lanes(v, n=128):
    return jnp.pad(v, ((0, 0), (0, n - v.shape[1])))


def _in_cols(slabs, a, b):
    out = []
    for j in range(N_CHIPS):
        lo, hi = max(a, W_IN_SHARD * j), min(b, W_IN_SHARD * (j + 1))
        if lo < hi:
            out.append(slabs[j][:, lo - W_IN_SHARD * j:hi - W_IN_SHARD * j])
    return out


def _local_step(x2, tgt2, meta_f, w_small, w_attn, w_gate, w_uq_f, w_ukv_f, w_bm, w_bf, w_o, pre_norm_g,
                post_norm_g, mla_q_norm_g, mla_kv_norm_g, fox_forget_b, start_exchange=None):
    s_rows = x2.shape[0]
    lp = PAD + s_rows
    w_uq_a = _uq_arrange(w_uq_f)
    w_ukv_a = _ukv_arrange(w_ukv_f)

    ctab, stab = _rope_tables(lp)
    ii = jnp.arange(BLK)
    tri_lo = (ii[:, None] >= ii[None, :]).astype(BF16)
    tri_up = (ii[:, None] <= ii[None, :]).astype(BF16)
    fb128 = _pad_lanes(fox_forget_b)

    u = _rms_pre(x2, meta_f, pre_norm_g)
    small = _mm(u, w_small, mode="nn", out_dtype=F32, name="proj_small")
    attn = _mm(u, w_attn, mode="nn", out_dtype=BF16, name="proj_attn",
               col_scale=(HEADS * HEAD_DIM, FOX_SCALE * LOG2E))
    gate = _mm(u, w_gate, mode="nn", out_dtype=BF16, name="proj_gate")
    qn, kvn, kr, kb = _small_prep(small, mla_q_norm_g, mla_kv_norm_g, fb128, ctab, stab, tri_lo)
    qcat = _mm(qn, w_uq_a, mode="nn", out_dtype=BF16, name="mla_q", row_ins=(ctab, stab),
               epilogue=lambda tile, c, s: _rope_pairs(tile, c, s) * (MLA_SCALE * LOG2E))
    kv = _mm(kvn, w_ukv_a, mode="nn", out_dtype=BF16, name="mla_kv")

    mla_cols = dict(qcol=0, kcol=lambda p: p, vcol=lambda p: PAIRS + p)
    fox_cols = dict(qcol=0, kcol=lambda p: PAIRS + p, vcol=lambda p: 2 * PAIRS + p)
    o_mla, lse_mla = _attn_fwd(qcat, kv, kv, _transposed_cols(kv, 1, "mla_vt"), kr, rope=True, name="mla_fwd",
                               **mla_cols)
    o_fox, lse_fox = _attn_fwd(attn, attn, attn, _transposed_cols(attn, 2, "fox_vt"), kb, rope=False,
                               name="fox_fwd", **fox_cols)

    a_mla, a_fox = _gate_fwd(o_mla, o_fox, gate)
    y_mla = _mm(a_mla, w_bm, mode="nn", out_dtype=BF16, name="br_mla")
    y_fox = _mm(a_fox, w_bf, mode="nn", out_dtype=BF16, name="br_fox")
    mg = _merge_fwd(gate, y_mla, y_fox)
    mixed = _mm(mg, w_o, mode="nn", out_dtype=F32, name="out_proj")
    dmixed, dy, loss_p, dg_post = _tail(x2, mixed, tgt2, post_norm_g)

    d_w_out = _mm(mg, dmixed, mode="tn", out_dtype=F32, name="d_w_out")
    dm = _mm(dmixed, w_o, mode="nt", out_dtype=BF16, name="d_merge")
    dy_mla, dy_fox, dgate_ab = _merge_bwd(dm, gate, y_mla, y_fox)
    d_w_bm = _mm(a_mla, dy_mla, mode="tn", out_dtype=F32, name="d_w_br_mla")
    d_w_bf = _mm(a_fox, dy_fox, mode="tn", out_dtype=F32, name="d_w_br_fox")
    da_mla = _mm(dy_mla, w_bm, mode="nt", out_dtype=BF16, name="d_a_mla")
    da_fox = _mm(dy_fox, w_bf, mode="nt", out_dtype=BF16, name="d_a_fox")
    do_mla, do_fox, dgate_z, dl_mla, dl_fox = _gate_bwd(da_mla, da_fox, o_mla, o_fox, gate)
    dl_mla, dl_fox = (d[:, :HEADS].T.reshape(PAIRS, 2, lp) for d in (dl_mla, dl_fox))

    dq_a, dkn, dvm, dkr = _attn_bwd(qcat, kv, kv, kr, do_mla, dl_mla, lse_mla, rtabs=(ctab, stab),
                                    scale=MLA_SCALE, name="mla_bwd", **mla_cols)
    dfq, dfk, dfv, dcol, drow = _attn_bwd(attn, attn, attn, kb, do_fox, dl_fox, lse_fox, scale=FOX_SCALE,
                                          name="fox_bwd", **fox_cols)

    d_w_uq_a = _mm(qn, dq_a, mode="tn", out_dtype=F32, name="d_w_uq")
    dqn = _mm(dq_a, w_uq_a, mode="nt", out_dtype=F32, name="d_qn")
    d_w_ukv_a = jnp.concatenate([_mm(kvn, dkn, mode="tn", out_dtype=F32, name="d_w_uk"),
                                 _mm(kvn, dvm, mode="tn", out_dtype=F32, name="d_w_uv")], axis=1)
    dkvn = _mm(dkn, w_ukv_a[:, :1024], mode="nt", out_dtype=F32, name="d_kvn_k")
    dkvn = _mm(dvm, w_ukv_a[:, 1024:], mode="nt", out_dtype=F32, name="d_kvn_v", acc=dkvn)
    dsmall, dg_q, dg_kv, dfb = _small_bwd(small, dqn, dkvn, dkr, dcol, drow, mla_q_norm_g, mla_kv_norm_g,
                                          fb128, ctab, stab, tri_up)

    dw_small = _mm(u, dsmall, mode="tn", out_dtype=BF16, name="d_w_small")
    dw_fq = _mm(u, dfq, mode="tn", out_dtype=BF16, name="d_w_fq")
    dw_fk = _mm(u, dfk, mode="tn", out_dtype=BF16, name="d_w_fk")
    dw_fv = _mm(u, dfv, mode="tn", out_dtype=BF16, name="d_w_fv")
    dw_z = _mm(u, dgate_z, mode="tn", out_dtype=BF16, name="d_w_z")
    dw_g = _mm(u, dgate_ab, mode="tn", out_dtype=BF16, name="d_w_g")
    d_w_in = (dw_small, dw_z, dw_fq, dw_fk, dw_fv, dw_g)
    d_w_uq = _uq_restore(d_w_uq_a)
    d_w_ukv = _ukv_restore(d_w_ukv_a)
    token = start_exchange(d_w_in, d_w_uq, d_w_ukv, d_w_bm, d_w_bf, d_w_out) if start_exchange else None
    du = _mm_sum_nt([(dsmall, w_small), (dfq, w_attn[:, 0:1024]), (dfk, w_attn[:, 1024:2048]),
                     (dfv, w_attn[:, 2048:3072]), (dgate_z, w_gate[:, 0:2048]), (dgate_ab, w_gate[:, 2048:4096])],
                    name="d_u", after=token)
    dx, dmeta, dg_pre = _pre_bwd(du, x2, meta_f, dy, pre_norm_g)
    return (loss_p, dx, dmeta, d_w_in, d_w_uq, d_w_ukv, d_w_bm, d_w_bf, d_w_out, dg_pre, dg_post, dg_q, dg_kv, dfb)


def _w_in_slabs(pieces):
    dw_small, dw_z, dw_fq, dw_fk, dw_fv, dw_g = pieces
    runs = [(dw_small[:, 0:416], C_CQ), (dw_z[:, 0:1024], C_ZMLA), (dw_fq, C_FQ), (dw_fk, C_FK), (dw_fv, C_FV),
            (dw_small[:, 512:528], C_FL), (dw_z[:, 1024:2048], C_ZFOX), (dw_g, C_GA)]
    slabs = []
    for j in range(N_CHIPS):
        lo, hi = W_IN_SHARD * j, W_IN_SHARD * (j + 1)
        cols = [a[:, max(lo, c0) - c0:min(hi, c0 + a.shape[1]) - c0] for a, c0 in runs
                if max(lo, c0) < min(hi, c0 + a.shape[1])]
        slabs.append(jnp.concatenate(cols, axis=1))
    return jnp.stack(slabs, axis=0)


def kernel(x, meta_tokens, pre_norm_g, w_in, fox_forget_b, mla_q_norm_g, mla_kv_norm_g, w_uq, w_ukv, w_br_mla, w_br_fox, w_out, post_norm_g, loss_target, m_meta_tokens, m_pre_norm_g, m_w_in, m_fox_forget_b, m_mla_q_norm_g, m_mla_kv_norm_g, m_w_uq, m_w_ukv, m_w_br_mla, m_w_br_fox, m_w_out, m_post_norm_g, v_meta_tokens, v_pre_norm_g, v_w_in, v_fox_forget_b, v_mla_q_norm_g, v_mla_kv_norm_g, v_w_uq, v_w_ukv, v_w_br_mla, v_w_br_fox, v_w_out, v_post_norm_g):
    me = 2 * lax.axis_index("x") + lax.axis_index("y")
    core = lax.axis_index("c")
    w_in_b = w_in.astype(BF16).reshape(D_MODEL, W_IN_SHARD)
    p2 = _pack_p2(w_uq[0], w_ukv[0], w_br_mla[0], w_br_fox[0], w_out[0], BF16)
    w_in_g, meta_g = _gather_weights([w_in_b], meta_tokens)
    p2_g = _gather_late(lax.optimization_barrier((p2, w_in_g))[0])
    slabs = [jnp.where(me == j, w_in_b, w_in_g[j]) for j in range(N_CHIPS)]
    chip = lax.broadcasted_iota(jnp.int32, (N_CHIPS, 1, 1), 0)
    p2_all = jnp.where(chip == me, p2[None], p2_g)
    w_uq_f = p2_all[:, 0:96].reshape(N_CHIPS, 256, 384).transpose(1, 0, 2).reshape(256, 1536)
    w_ukv_f = p2_all[:, 96:160].reshape(N_CHIPS, 128, 512).transpose(1, 0, 2).reshape(128, 2048)
    w_bm, w_bf, w_o = (p2_all[:, lo:lo + 256].reshape(D_MODEL, D_MODEL) for lo in (160, 416, 672))
    meta_f = jnp.where(chip == me, meta_tokens[None], meta_g).transpose(1, 0, 2).reshape(N_META, D_MODEL)
    kpe = _in_cols(slabs, C_KPE, C_ZMLA)
    w_small = jnp.concatenate(_in_cols(slabs, C_CQ, C_KPE) + kpe + kpe + [jnp.zeros((D_MODEL, 64), BF16)]
                              + _in_cols(slabs, C_FL, C_ZFOX) + [jnp.zeros((D_MODEL, 112), BF16)], axis=1)
    w_attn = jnp.concatenate(_in_cols(slabs, C_FQ, C_FL), axis=1)
    w_gate = jnp.concatenate(_in_cols(slabs, C_ZMLA, C_FQ) + _in_cols(slabs, C_ZFOX, C_END), axis=1)

    exchange = {}

    def start_exchange(d_w_in, d_w_uq, d_w_ukv, d_w_bm, d_w_bf, d_w_out):
        g2 = jnp.concatenate(
            [d_w_uq.reshape(256, N_CHIPS, 384).transpose(1, 0, 2).reshape(N_CHIPS, 96, D_MODEL),
             d_w_ukv.reshape(128, N_CHIPS, 512).transpose(1, 0, 2).reshape(N_CHIPS, 64, D_MODEL)]
            + [g.reshape(N_CHIPS, 256, D_MODEL) for g in (d_w_bm, d_w_bf, d_w_out)], axis=1)
        pieces = [p[None] for p in d_w_in]
        from_sib = _swap_halves(pieces + [g2])
        halves = [_add_cores(p, s, "add_cores_" + nm)[0]
                  for p, s, nm in zip(pieces, from_sib, ("small", "z", "fq", "fk", "fv", "g"))]
        parts = [_w_in_slabs(halves), _add_cores(g2, from_sib[-1], "add_cores_rest")]
        exchange.update(parts=parts, landed=_scatter_chips(parts))
        return parts[0][0, 0:16, 0:LANES]

    (loss_p, dx, dmeta, _, _, _, _, _, _, dg_pre, dg_post, dg_q, dg_kv,
     dfb) = _local_step(x[0], loss_target[0], meta_f, w_small, w_attn, w_gate, w_uq_f, w_ukv_f, w_bm, w_bf, w_o,
                        pre_norm_g, post_norm_g, mla_q_norm_g, mla_kv_norm_g, fox_forget_b, start_exchange)

    mine = [_add_chips(l, lax.dynamic_index_in_dim(p, me, 0, keepdims=False), nm)
            for l, p, nm in zip(exchange["landed"], exchange["parts"], ("add_chips_w_in", "add_chips_rest"))]
    theirs = _swap_reduced(mine)
    g_w_in, g_p2 = [jnp.concatenate([jnp.where(core == 0, a, b), jnp.where(core == 0, b, a)], axis=0)
                    for a, b in zip(mine, theirs)]
    g_w_uq, g_w_ukv, g_w_bm, g_w_bf, g_w_out = _unpack_p2(g_p2)
    g_w_in = g_w_in[None]

    vec = jnp.concatenate([dg_pre.reshape(8, 128), dg_post.reshape(8, 128), dg_q.reshape(2, 128), dg_kv,
                           dfb, _pad_lanes(loss_p), jnp.zeros((3, 128), F32), dmeta.reshape(128, 128)], axis=0)
    tot = _allreduce_small(vec)
    loss = tot[20, 0]
    g_meta = lax.dynamic_slice_in_dim(tot[24:].reshape(N_META, D_MODEL), 256 * me, 256, axis=1)

    def small_pack(pre, post, gq_, gkv_, fb_):
        return jnp.concatenate([pre.reshape(8, 128), post.reshape(8, 128), gq_.reshape(2, 128), gkv_,
                                _pad_lanes(fb_), jnp.zeros((4, 128), F32)], axis=0)

    def small_unpack(t):
        return (t[0:8].reshape(1, 1024), t[8:16].reshape(1, 1024), t[16:18].reshape(1, 256), t[18:19],
                t[19:20, 0:HEADS])

    g_small = jnp.concatenate([tot[0:20], jnp.zeros((4, 128), F32)], axis=0)
    sm = _adamw(small_pack(pre_norm_g, post_norm_g, mla_q_norm_g, mla_kv_norm_g, fox_forget_b), g_small,
                small_pack(m_pre_norm_g, m_post_norm_g, m_mla_q_norm_g, m_mla_kv_norm_g, m_fox_forget_b),
                small_pack(v_pre_norm_g, v_post_norm_g, v_mla_q_norm_g, v_mla_kv_norm_g, v_fox_forget_b),
                "adamw_small")
    g_pre, g_post, g_q, g_kv, g_fb = small_unpack(g_small)
    (d_pre, d_post, d_q, d_kv, d_fb), (nm_pre, nm_post, nm_q, nm_kv, nm_fb), (nv_pre, nv_post, nv_q, nv_kv, nv_fb) = (
        small_unpack(t) for t in sm)

    d_meta, nm_meta, nv_meta = _adamw(meta_tokens, g_meta, m_meta_tokens, v_meta_tokens, "adamw_meta")
    d_win, nm_win, nv_win = (t.T[None] for t in _adamw(w_in[0].T, g_w_in[0].T, m_w_in[0].T, v_w_in[0].T,
                                                       "adamw_w_in"))
    d_wuq, nm_wuq, nv_wuq = _adamw(w_uq[0], g_w_uq, m_w_uq[0], v_w_uq[0], "adamw_w_uq")
    d_wukv, nm_wukv, nv_wukv = _adamw(w_ukv[0], g_w_ukv, m_w_ukv[0], v_w_ukv[0], "adamw_w_ukv")
    d_wbm, nm_wbm, nv_wbm = _adamw(w_br_mla[0], g_w_bm, m_w_br_mla[0], v_w_br_mla[0], "adamw_w_br_mla")
    d_wbf, nm_wbf, nv_wbf = _adamw(w_br_fox[0], g_w_bf, m_w_br_fox[0], v_w_br_fox[0], "adamw_w_br_fox")
    d_wo, nm_wo, nv_wo = _adamw(w_out[0], g_w_out, m_w_out[0], v_w_out[0], "adamw_w_out")

    def group(meta_, pre, win, fb_, q_, kv_, wuq, wukv, wbm, wbf, wo, post):
        return (meta_, pre, win, fb_, q_, kv_, wuq[None], wukv[None], wbm[None], wbf[None], wo[None], post)

    grads = group(g_meta, g_pre, g_w_in, g_fb, g_q, g_kv, g_w_uq, g_w_ukv, g_w_bm, g_w_bf, g_w_out, g_post)
    deltas = group(d_meta, d_pre, d_win, d_fb, d_q, d_kv, d_wuq, d_wukv, d_wbm, d_wbf, d_wo, d_post)
    new_m = group(nm_meta, nm_pre, nm_win, nm_fb, nm_q, nm_kv, nm_wuq, nm_wukv, nm_wbm, nm_wbf, nm_wo, nm_post)
    new_v = group(nv_meta, nv_pre, nv_win, nv_fb, nv_q, nv_kv, nv_wuq, nv_wukv, nv_wbm, nv_wbf, nv_wo, nv_post)
    return (loss, dx[None], *grads, *deltas, *new_m, *new_v)
```

```python
import math

import jax
import jax.numpy as jnp
import numpy as np
from jax import lax
from jax.experimental import pallas as pl
from jax.experimental.pallas import tpu as pltpu
from jax.experimental.pallas import tpu_sc as plsc

F32 = jnp.float32
BF16 = jnp.bfloat16

D_MODEL = 1024
N_META = 16
RMS_EPS = 1e-6
HEADS = 16
PAIRS = HEADS // 2
HEAD_DIM = 64
LANES = 128
MLA_ROPE = 32
AHEAD = 6
AHEAD_BWD = 1
BIAS_PARTS = 3
MLA_SCALE = 1.0 / math.sqrt(64 + 32)
FOX_SCALE = 1.0 / math.sqrt(64)
LOG2E = math.log2(math.e)
LN2 = math.log(2.0)
ROPE_THETA = 10000.0

PAD = 256
BLK = 256
QB = 512
UNROLLS = (8, 4, 2, 1)
NEG = -1e30

C_CQ, C_CKV, C_KPE, C_ZMLA, C_FQ, C_FK, C_FV, C_FL, C_ZFOX, C_GA, C_GB, C_END = (
    0, 256, 384, 416, 1440, 2464, 3488, 4512, 4528, 5552, 6576, 7600)
SMALL_W = 640
W_IN_SHARD = 1900

P2_ROWS = 928
N_CHIPS = 4

ADAM_LR = 0.001
ADAM_B1 = 0.9
ADAM_B2 = 0.999
ADAM_EPS = 1e-08
ADAM_WD = 0.01
ADAM_STEP = 10

VMEM_BIG = 56 * 1024 * 1024
MM_VMEM_BUDGET = 44 * 1024 * 1024
MESH = pl.DeviceIdType.MESH


def _cp(dims, vmem=None):
    return pltpu.CompilerParams(dimension_semantics=dims, vmem_limit_bytes=vmem)


def _dot(a, b, ca, cb):
    return lax.dot_general(a, b, (((ca,), (cb,)), ((), ())), preferred_element_type=F32)


def _sigmoid(x):
    return 1.0 / (1.0 + jnp.exp(-x))


def _tile(n, cands):
    for c in cands:
        if n % c == 0:
            return c
    return n


def _mm(a, b, *, mode, out_dtype, name, acc=None, epilogue=None, row_ins=(), after=None, col_scale=None):
    if mode == "nn":
        (M, K), N = a.shape, b.shape[1]
    elif mode == "nt":
        (M, K), N = a.shape, b.shape[0]
    else:
        (K, M), N = a.shape, b.shape[1]
    tm = _tile(M, (1088, 1024)) if M > 1024 else M
    tn = _tile(N, (1024,)) if N > 1024 else N
    nk = 1
    while True:
        tk = K // nk
        need = 2 * tk * (tm * a.dtype.itemsize + tn * b.dtype.itemsize) + tm * tn * (
            2 * jnp.dtype(out_dtype).itemsize + (8 if acc is not None else 0) + (4 if nk > 1 else 0))
        if need <= MM_VMEM_BUDGET or (tk // 2) % (16 if mode == "tn" else LANES) or tk <= 512:
            break
        nk *= 2
    while (M // tm) * (N // tn) * nk < 4 and tn % 512 == 0:
        tn //= 2
    assert col_scale is None or (nk == 1 and col_scale[0] % tn == 0)
    ca, cb = {"nn": (1, 0), "nt": (1, 1), "tn": (0, 0)}[mode]
    a_spec = (pl.BlockSpec((tk, tm), lambda j, i, k: (k, i)) if mode == "tn"
              else pl.BlockSpec((tm, tk), lambda j, i, k: (i, k)))
    b_spec = (pl.BlockSpec((tn, tk), lambda j, i, k: (j, k)) if mode == "nt"
              else pl.BlockSpec((tk, tn), lambda j, i, k: (k, j)))
    o_spec = pl.BlockSpec((tm, tn), lambda j, i, k: (i, j))
    has_acc = acc is not None

    nrow = len(row_ins)

    def body(*refs):
        a_ref, b_ref = refs[0], refs[1]
        acc_ref = refs[2] if has_acc else None
        rows = refs[2 + has_acc:2 + has_acc + nrow]
        o_ref = refs[2 + has_acc + nrow + (after is not None)]

        def store(tile):
            if epilogue is not None:
                tile = epilogue(tile, *[r[...] for r in rows])
            if col_scale is not None:
                tile = tile * jnp.where(pl.program_id(0) * tn < col_scale[0], col_scale[1], 1.0)
            o_ref[...] = tile.astype(out_dtype)

        part = _dot(a_ref[...].astype(BF16), b_ref[...].astype(BF16), ca, cb)
        if nk == 1:
            store(part + acc_ref[...] if has_acc else part)
        else:
            sc = refs[-1]
            k = pl.program_id(2)

            @pl.when(k == 0)
            def _():
                sc[...] = part + acc_ref[...] if has_acc else part

            @pl.when(k > 0)
            def _():
                sc[...] += part

            @pl.when(k == nk - 1)
            def _():
                store(sc[...])

    ins = [a, b] + ([acc] if has_acc else []) + list(row_ins)
    in_specs = ([a_spec, b_spec] + ([o_spec] if has_acc else [])
                + [pl.BlockSpec((tm, r.shape[1]), lambda j, i, k: (i, 0)) for r in row_ins])
    if after is not None:
        ins.append(after)
        in_specs.append(pl.BlockSpec(after.shape, lambda j, i, k: (0,) * after.ndim))
    return pl.pallas_call(
        body, name=name, grid=(N // tn, M // tm, nk), in_specs=in_specs, out_specs=o_spec,
        out_shape=jax.ShapeDtypeStruct((M, N), out_dtype),
        scratch_shapes=[pltpu.VMEM((tm, tn), F32)] if nk > 1 else [],
        compiler_params=_cp(("parallel", "parallel", "arbitrary"), VMEM_BIG))(*ins)


def _mm_sum_nt(pairs, *, name, after=None):
    n = len(pairs)
    M, N = pairs[0][0].shape[0], pairs[0][1].shape[0]
    tm = _tile(M, (272,))

    def body(*refs):
        o_ref = refs[2 * n + (after is not None)]
        tot = _dot(refs[0][...].astype(BF16), refs[n][...].astype(BF16), 1, 1)
        for i in range(1, n):
            tot = tot + _dot(refs[i][...].astype(BF16), refs[n + i][...].astype(BF16), 1, 1)
        o_ref[...] = tot

    ins = [a for a, _ in pairs] + [b for _, b in pairs]
    in_specs = ([pl.BlockSpec((tm, a.shape[1]), lambda i: (i, 0)) for a, _ in pairs]
                + [pl.BlockSpec(b.shape, lambda i: (0, 0)) for _, b in pairs])
    if after is not None:
        ins.append(after)
        in_specs.append(pl.BlockSpec(after.shape, lambda i: (0,) * after.ndim))
    return pl.pallas_call(
        body, name=name, grid=(M // tm,), in_specs=in_specs, out_specs=pl.BlockSpec((tm, N), lambda i: (i, 0)),
        out_shape=jax.ShapeDtypeStruct((M, N), F32), compiler_params=_cp(("parallel",), VMEM_BIG))(*ins)


def _row(w):
    return pl.BlockSpec((BLK, w), lambda i: (i, 0))


def _rowc(w, c):
    return pl.BlockSpec((BLK, w), lambda i: (i, c))


def _full(shape):
    return pl.BlockSpec(shape, lambda i: tuple(0 for _ in shape))


def _rope(x, c, s):
    lane = lax.broadcasted_iota(jnp.int32, x.shape, 1)
    is_x1 = ((lane >> 4) & 1) == 0
    partner = jnp.where(is_x1, pltpu.roll(x, LANES - 16, 1), pltpu.roll(x, 16, 1))
    return x * c + partner * s


def _row_valid(i):
    rows = i * BLK + lax.broadcasted_iota(jnp.int32, (BLK, 1), 0)
    return (rows < N_META) | (rows >= PAD)


def _shift_rows(w):
    return pl.BlockSpec((BLK, w), lambda i: (jnp.maximum(i - 1, 0), 0))


def _h_block(i, x_ref, meta_ref):
    head = jnp.concatenate([meta_ref[...], jnp.zeros((BLK - N_META, D_MODEL), F32)], axis=0)
    return jnp.where(i == 0, head, x_ref[...])


def _rms_pre(x2, meta, g):
    lp = PAD + x2.shape[0]

    def body(x_ref, meta_ref, g_ref, u_ref):
        hv = _h_block(pl.program_id(0), x_ref, meta_ref)
        r = lax.rsqrt(jnp.mean(hv * hv, axis=-1, keepdims=True) + RMS_EPS)
        u_ref[...] = (hv * r * g_ref[...]).astype(BF16)

    return pl.pallas_call(
        body, name="rms_pre", grid=(lp // BLK,),
        in_specs=[_shift_rows(D_MODEL), _full((N_META, D_MODEL)), _full((1, D_MODEL))], out_specs=_row(D_MODEL),
        out_shape=jax.ShapeDtypeStruct((lp, D_MODEL), BF16),
        compiler_params=_cp(("parallel",)))(x2, meta, g)


def _split3(x):
    hi = x.astype(BF16)
    r1 = x - hi.astype(F32)
    mid = r1.astype(BF16)
    lo = (r1 - mid.astype(F32)).astype(BF16)
    return hi, mid, lo


def _small_prep(small, gq, gkv, fb, ctab, stab, tri):
    lp = small.shape[0]

    def body(sm_ref, gq_ref, gkv_ref, fb_ref, c_ref, s_ref, tri_ref, qn_ref, kvn_ref, kr_ref, kb_ref, carry):
        i = pl.program_id(0)

        @pl.when(i == 0)
        def _():
            carry[...] = jnp.zeros_like(carry)

        cq = sm_ref[:, 0:256]
        r = lax.rsqrt(jnp.mean(cq * cq, axis=-1, keepdims=True) + RMS_EPS)
        qn_ref[...] = (cq * r * gq_ref[...]).astype(BF16)
        ckv = sm_ref[:, 256:384]
        r = lax.rsqrt(jnp.mean(ckv * ckv, axis=-1, keepdims=True) + RMS_EPS)
        kvn_ref[...] = (ckv * r * gkv_ref[...]).astype(BF16)
        kr_ref[...] = _rope(sm_ref[:, 384:512], c_ref[...], s_ref[...]).astype(BF16)
        fl = sm_ref[:, 512:640] + fb_ref[...]
        lf = jnp.minimum(fl, 0.0) - jnp.log(1.0 + jnp.exp(-jnp.abs(fl)))
        lf = jnp.where(_row_valid(i), lf, 0.0)
        hi, mid, lo = _split3(lf)
        t = tri_ref[...]
        cum = (_dot(t, hi, 1, 0) + _dot(t, mid, 1, 0)) + _dot(t, lo, 1, 0) + carry[...]
        carry[...] = cum[BLK - 1:BLK, :]
        src = lax.broadcasted_iota(jnp.int32, (LANES, LANES), 0)
        dst = lax.broadcasted_iota(jnp.int32, (LANES, LANES), 1)
        kb = jnp.zeros((BLK, LANES), F32)
        for j, part in enumerate(_split3(-cum * LOG2E)):
            spread = ((dst == BIAS_PARTS * src + j) & (src < HEADS)).astype(BF16)
            kb = kb + _dot(part, spread, 1, 0)
        kb_ref[...] = kb.astype(BF16)

    return pl.pallas_call(
        body, name="small_prep", grid=(lp // BLK,),
        in_specs=[_row(SMALL_W), _full((1, 256)), _full((1, 128)), _full((1, 128)), _row(128), _row(128),
                  _full((BLK, BLK))],
        out_specs=[_row(256), _row(128), _row(128), _row(128)],
        out_shape=[jax.ShapeDtypeStruct((lp, 256), BF16), jax.ShapeDtypeStruct((lp, 128), BF16),
                   jax.ShapeDtypeStruct((lp, 128), BF16), jax.ShapeDtypeStruct((lp, 128), BF16)],
        scratch_shapes=[pltpu.VMEM((1, 128), F32)],
        compiler_params=_cp(("arbitrary",)))(small, gq, gkv, fb, ctab, stab, tri)


def _rope_pairs(tile, c, s):
    out = []
    for lo in range(0, tile.shape[1], 256):
        out += [tile[:, lo:lo + 128], _rope(tile[:, lo + 128:lo + 256], c, s)]
    return jnp.concatenate(out, axis=1)


def _gate_fwd(o_mla, o_fox, gate):
    lp = o_mla.shape[0]

    def body(om_ref, of_ref, zm_ref, zf_ref, am_ref, af_ref):
        zm = zm_ref[...].astype(F32)
        am_ref[...] = (om_ref[...] * (zm * _sigmoid(zm))).astype(BF16)
        zf = zf_ref[...].astype(F32)
        af_ref[...] = (of_ref[...] * (zf * _sigmoid(zf))).astype(BF16)

    return pl.pallas_call(
        body, name="gate_fwd", grid=(lp // BLK,),
        in_specs=[_row(D_MODEL), _row(D_MODEL), _rowc(D_MODEL, 0), _rowc(D_MODEL, 1)],
        out_specs=[_row(D_MODEL), _row(D_MODEL)],
        out_shape=[jax.ShapeDtypeStruct((lp, D_MODEL), BF16)] * 2,
        compiler_params=_cp(("parallel",)))(o_mla, o_fox, gate, gate)


def _merge_fwd(gate, y_mla, y_fox):
    lp = y_mla.shape[0]

    def body(ga_ref, gb_ref, ym_ref, yf_ref, m_ref):
        sa = _sigmoid(ga_ref[...].astype(F32))
        sb = _sigmoid(gb_ref[...].astype(F32))
        m_ref[...] = (sa * ym_ref[...] + sb * yf_ref[...]).astype(BF16)

    return pl.pallas_call(
        body, name="merge_fwd", grid=(lp // BLK,),
        in_specs=[_rowc(D_MODEL, 2), _rowc(D_MODEL, 3), _row(D_MODEL), _row(D_MODEL)],
        out_specs=_row(D_MODEL), out_shape=jax.ShapeDtypeStruct((lp, D_MODEL), BF16),
        compiler_params=_cp(("parallel",)))(gate, gate, y_mla, y_fox)


def _tail(x2, mixed, tgt, gpost):
    lp = mixed.shape[0]
    shift = _shift_rows(D_MODEL)

    def body(h_ref, mx_ref, t_ref, g_ref, dmx_ref, dy_ref, loss_ref, dg_ref):
        i = pl.program_id(0)

        @pl.when(i == 0)
        def _():
            loss_ref[...] = jnp.zeros_like(loss_ref)
            dg_ref[...] = jnp.zeros_like(dg_ref)
            dmx_ref[...] = jnp.zeros_like(dmx_ref)
            dy_ref[...] = jnp.zeros_like(dy_ref)

        @pl.when(i > 0)
        def _():
            mx = mx_ref[...]
            g = g_ref[...]
            r = lax.rsqrt(jnp.mean(mx * mx, axis=-1, keepdims=True) + RMS_EPS)
            nrm = mx * r
            e = (h_ref[...] + nrm * g) - t_ref[...]
            loss_ref[...] += jnp.sum(0.5 * jnp.sum(e * e, axis=-1, keepdims=True) * (1.0 / D_MODEL),
                                     axis=0, keepdims=True)
            dy = e * (1.0 / D_MODEL)
            dy_ref[...] = dy
            dg_ref[...] += jnp.sum(dy * nrm, axis=0, keepdims=True)
            w = dy * g
            dot = jnp.mean(w * mx, axis=-1, keepdims=True)
            dmx_ref[...] = (r * w - mx * (r * r * r * dot)).astype(BF16)

    return pl.pallas_call(
        body, name="tail", grid=(lp // BLK,),
        in_specs=[shift, _row(D_MODEL), shift, _full((1, D_MODEL))],
        out_specs=[_row(D_MODEL), _row(D_MODEL), _full((1, 1)), _full((1, D_MODEL))],
        out_shape=[jax.ShapeDtypeStruct((lp, D_MODEL), BF16), jax.ShapeDtypeStruct((lp, D_MODEL), F32),
                   jax.ShapeDtypeStruct((1, 1), F32), jax.ShapeDtypeStruct((1, D_MODEL), F32)],
        compiler_params=_cp(("arbitrary",)))(x2, mixed, tgt, gpost)


def _merge_bwd(dm, gate, y_mla, y_fox):
    lp = dm.shape[0]

    def body(dm_ref, ga_ref, gb_ref, ym_ref, yf_ref, dym_ref, dyf_ref, dg_ref):
        dm_v = dm_ref[...].astype(F32)
        sa = _sigmoid(ga_ref[...].astype(F32))
        sb = _sigmoid(gb_ref[...].astype(F32))
        dym_ref[...] = (dm_v * sa).astype(BF16)
        dyf_ref[...] = (dm_v * sb).astype(BF16)
        dg_ref[:, 0:D_MODEL] = (dm_v * ym_ref[...] * (sa * (1.0 - sa))).astype(BF16)
        dg_ref[:, D_MODEL:2 * D_MODEL] = (dm_v * yf_ref[...] * (sb * (1.0 - sb))).astype(BF16)

    return pl.pallas_call(
        body, name="merge_bwd", grid=(lp // BLK,),
        in_specs=[_row(D_MODEL), _rowc(D_MODEL, 2), _rowc(D_MODEL, 3), _row(D_MODEL), _row(D_MODEL)],
        out_specs=[_row(D_MODEL), _row(D_MODEL), _row(2 * D_MODEL)],
        out_shape=[jax.ShapeDtypeStruct((lp, D_MODEL), BF16), jax.ShapeDtypeStruct((lp, D_MODEL), BF16),
                   jax.ShapeDtypeStruct((lp, 2 * D_MODEL), BF16)],
        compiler_params=_cp(("parallel",)))(dm, gate, gate, y_mla, y_fox)


def _gate_bwd(da_mla, da_fox, o_mla, o_fox, gate):
    lp = da_mla.shape[0]

    def one(da, o, z, head_of_col):
        sg = _sigmoid(z)
        do = (da * (z * sg)).astype(BF16)
        dz = da * o * (sg * (1.0 + z * (1.0 - sg)))
        delta = sum(_dot(part, head_of_col, 1, 0) for part in _split3(do.astype(F32) * o))
        return do, dz.astype(BF16), delta

    def body(dam_ref, daf_ref, om_ref, of_ref, zm_ref, zf_ref, dom_ref, dof_ref, dz_ref, dlm_ref, dlf_ref):
        f32 = lambda r: r[...].astype(F32)
        head_of_col = (lax.broadcasted_iota(jnp.int32, (D_MODEL, LANES), 0) // HEAD_DIM
                       == lax.broadcasted_iota(jnp.int32, (D_MODEL, LANES), 1)).astype(BF16)
        dom_ref[...], dz_ref[:, 0:D_MODEL], dlm_ref[...] = one(f32(dam_ref), f32(om_ref), f32(zm_ref), head_of_col)
        dof_ref[...], dz_ref[:, D_MODEL:2 * D_MODEL], dlf_ref[...] = one(f32(daf_ref), f32(of_ref), f32(zf_ref),
                                                                        head_of_col)

    return pl.pallas_call(
        body, name="gate_bwd", grid=(lp // BLK,),
        in_specs=[_row(D_MODEL)] * 4 + [_rowc(D_MODEL, 0), _rowc(D_MODEL, 1)],
        out_specs=[_row(D_MODEL), _row(D_MODEL), _row(2 * D_MODEL), _row(LANES), _row(LANES)],
        out_shape=[jax.ShapeDtypeStruct((lp, D_MODEL), BF16), jax.ShapeDtypeStruct((lp, D_MODEL), BF16),
                   jax.ShapeDtypeStruct((lp, 2 * D_MODEL), BF16), jax.ShapeDtypeStruct((lp, LANES), F32),
                   jax.ShapeDtypeStruct((lp, LANES), F32)],
        compiler_params=_cp(("parallel",)))(da_mla, da_fox, o_mla, o_fox, gate, gate)


def _small_bwd(small, dqn, dkvn, dkr, dcol_t, drow_t, gq, gkv, fb, ctab, stab, triu):
    lp = small.shape[0]
    nb = lp // BLK

    def rrow(w):
        return pl.BlockSpec((BLK, w), lambda i: (nb - 1 - i, 0))

    def body(sm_ref, dqn_ref, dkvn_ref, dkr_ref, dcol_ref, drow_ref, gq_ref, gkv_ref, fb_ref, c_ref, s_ref, tri_ref,
             ds_ref, dgq_ref, dgkv_ref, dfb_ref, carry):
        i = pl.program_id(0)

        @pl.when(i == 0)
        def _():
            carry[...] = jnp.zeros_like(carry)
            dgq_ref[...] = jnp.zeros_like(dgq_ref)
            dgkv_ref[...] = jnp.zeros_like(dgkv_ref)
            dfb_ref[...] = jnp.zeros_like(dfb_ref)

        def norm_bwd(x, dn, g, dg_ref):
            r = lax.rsqrt(jnp.mean(x * x, axis=-1, keepdims=True) + RMS_EPS)
            dg_ref[...] += jnp.sum(dn * (x * r), axis=0, keepdims=True)
            w = dn * g
            dot = jnp.mean(w * x, axis=-1, keepdims=True)
            return r * w - x * (r * r * r * dot)

        ds_ref[:, 0:256] = norm_bwd(sm_ref[:, 0:256], dqn_ref[...], gq_ref[...], dgq_ref).astype(BF16)
        ds_ref[:, 256:384] = norm_bwd(sm_ref[:, 256:384], dkvn_ref[...], gkv_ref[...], dgkv_ref).astype(BF16)

        dk = dkr_ref[0]
        for p in range(1, PAIRS):
            dk = dk + dkr_ref[p]
        dk = _rope(dk, c_ref[...], -s_ref[...])
        lane = lax.broadcasted_iota(jnp.int32, dk.shape, 1)
        dk = jnp.where(lane < MLA_ROPE, dk + pltpu.roll(dk, LANES - MLA_ROPE, 1), 0.0)
        ds_ref[:, 384:512] = dk.astype(BF16)

        dcol = dcol_ref[0]
        for p in range(1, PAIRS):
            dcol = dcol + pltpu.roll(dcol_ref[p], 2 * p, 1)
        rows16 = jnp.concatenate([drow_ref[p, h:h + 1, :] for p in range(PAIRS) for h in range(2)], axis=0)
        eye = (lax.broadcasted_iota(jnp.int32, (HEADS, LANES), 0)
               == lax.broadcasted_iota(jnp.int32, (HEADS, LANES), 1)).astype(BF16)
        drow = sum(_dot(part, eye, 0, 0) for part in _split3(rows16))
        dcr = dcol - drow
        hi, mid, lo = _split3(dcr)
        t = tri_ref[...]
        suf = (_dot(t, hi, 1, 0) + _dot(t, mid, 1, 0)) + _dot(t, lo, 1, 0) + carry[...]
        fl = sm_ref[:, 512:640] + fb_ref[...]
        dfl = jnp.where(_row_valid(nb - 1 - i), -suf * _sigmoid(-fl), 0.0)
        ds_ref[:, 512:640] = dfl.astype(BF16)
        dfb_ref[...] += jnp.sum(dfl, axis=0, keepdims=True)
        carry[...] += jnp.sum(dcr, axis=0, keepdims=True)

    return pl.pallas_call(
        body, name="small_bwd", grid=(nb,),
        in_specs=[rrow(SMALL_W), rrow(256), rrow(128),
                  pl.BlockSpec((PAIRS, BLK, 128), lambda i: (0, nb - 1 - i, 0)),
                  pl.BlockSpec((PAIRS, BLK, 128), lambda i: (0, nb - 1 - i, 0)),
                  pl.BlockSpec((PAIRS, 2, BLK), lambda i: (0, 0, nb - 1 - i)),
                  _full((1, 256)), _full((1, 128)), _full((1, 128)), rrow(128), rrow(128), _full((BLK, BLK))],
        out_specs=[rrow(SMALL_W), _full((1, 256)), _full((1, 128)), _full((1, 128))],
        out_shape=[jax.ShapeDtypeStruct((lp, SMALL_W), BF16), jax.ShapeDtypeStruct((1, 256), F32),
                   jax.ShapeDtypeStruct((1, 128), F32), jax.ShapeDtypeStruct((1, 128), F32)],
        scratch_shapes=[pltpu.VMEM((1, 128), F32)],
        compiler_params=_cp(("arbitrary",)))(small, dqn, dkvn, dkr, dcol_t, drow_t, gq, gkv, fb, ctab, stab, triu)


def _pre_bwd(du, x2, meta, dy, gpre):
    s_rows = x2.shape[0]
    lp = PAD + s_rows
    shift = _shift_rows(D_MODEL)

    def body(du_ref, x_ref, meta_ref, dy_ref, g_ref, dx_ref, dmeta_ref, dg_ref):
        i = pl.program_id(0)

        @pl.when(i == 0)
        def _():
            dg_ref[...] = jnp.zeros_like(dg_ref)

        hv = _h_block(i, x_ref, meta_ref)
        duv = du_ref[...]
        r = lax.rsqrt(jnp.mean(hv * hv, axis=-1, keepdims=True) + RMS_EPS)
        dg_ref[...] += jnp.sum(duv * (hv * r), axis=0, keepdims=True)
        w = duv * g_ref[...]
        dot = jnp.mean(w * hv, axis=-1, keepdims=True)
        dh = dy_ref[...] + (r * w - hv * (r * r * r * dot))
        dx_ref[...] = dh

        @pl.when(i == 0)
        def _():
            dmeta_ref[...] = dh[0:N_META, :]

    return pl.pallas_call(
        body, name="pre_bwd", grid=(lp // BLK,),
        in_specs=[_row(D_MODEL), shift, _full((N_META, D_MODEL)), _row(D_MODEL), _full((1, D_MODEL))],
        out_specs=[shift, _full((N_META, D_MODEL)), _full((1, D_MODEL))],
        out_shape=[jax.ShapeDtypeStruct((s_rows, D_MODEL), F32), jax.ShapeDtypeStruct((N_META, D_MODEL), F32),
                   jax.ShapeDtypeStruct((1, D_MODEL), F32)],
        compiler_params=_cp(("arbitrary",)))(du, x2, meta, dy, gpre)


def _pair_masks(rope, pair):
    lane = lax.broadcasted_iota(jnp.int32, (1, LANES), 1)
    mas = [lane < HEAD_DIM, lane >= HEAD_DIM]
    wide = lax.broadcasted_iota(jnp.int32, (1, 2 * LANES), 1)
    extra = MLA_ROPE if rope else BIAS_PARTS
    lo = LANES if rope else LANES + 2 * BIAS_PARTS * pair
    mid = lo + extra
    return mas, [(wide < HEAD_DIM) | ((wide >= lo) & (wide < mid)),
                 ((wide >= HEAD_DIM) & (wide < LANES)) | ((wide >= mid) & (wide < mid + extra))]


def _mask2(x, masks):
    return [jnp.where(m, x, jnp.zeros_like(x)) for m in masks]


def _q_heads(q_rows, rope, mas, hmask):
    if rope:
        return _mask2(q_rows, hmask)
    zero = jnp.zeros((q_rows.shape[0], LANES), BF16)
    return [jnp.concatenate([jnp.where(m, q_rows, zero), jnp.where(hm[:, LANES:], zero + 1, zero)], axis=1)
            for m, hm in zip(mas, hmask)]


def _transposed_cols(x, group, name):
    lp = x.shape[0]

    def body(x_ref, o_ref):
        o_ref[...] = x_ref[...].T

    cols = 2 * BLK
    per_group = D_MODEL // cols
    return pl.pallas_call(
        body, name=name, grid=(per_group,),
        in_specs=[pl.BlockSpec((lp, cols), lambda i: (0, per_group * group + i))],
        out_specs=pl.BlockSpec((cols, lp), lambda i: (i, 0)),
        out_shape=jax.ShapeDtypeStruct((D_MODEL, lp), x.dtype),
        compiler_params=_cp(("parallel",), MM_VMEM_BUDGET))(x)


def _attn_fwd(q, k, v, vt, k2, *, rope, qcol, kcol, vcol, name):
    lp = q.shape[0]
    nq = 1 + (lp - PAD) // QB
    qw = 256 if rope else 128

    def body(q_ref, k_ref, v_ref, vt_ref, k2_ref, o_ref, lse_ref):
        i = pl.program_id(1)
        r0 = pl.multiple_of(jnp.where(i == 0, 0, PAD + QB * (i - 1)), BLK)
        b0 = r0 // BLK
        mas, hmask = _pair_masks(rope, pl.program_id(0))
        qh = _q_heads(q_ref[pl.ds(r0, QB), :], rope, mas, hmask)

        def update(chunks, tiles, groups):
            m = [[cr[0], cr[2]] for _, _, cr in groups]
            l = [[cr[1], cr[3]] for _, _, cr in groups]
            acc = [[cr[4][0:HEAD_DIM], cr[4][HEAD_DIM:LANES]] for _, _, cr in groups]
            qs = [[x[q_lo:q_lo + wq] for x in qh] for q_lo, wq, _ in groups]
            k0s = [pl.multiple_of(kc * BLK, BLK) for kc, _ in chunks]
            kks = [jnp.concatenate([k_ref[pl.ds(k0, n), :], k2_ref[pl.ds(k0, n), :]], axis=1)
                   for k0, (_, n) in zip(k0s, chunks)]
            jobs = [(g, ci, mask, h) for g, ci, mask in tiles for h in range(2)]
            score = lambda t: _dot(kks[jobs[t][1]], qs[jobs[t][0]][jobs[t][3]], 1, 1)
            ss = [score(t) for t in range(min(AHEAD, len(jobs)))]
            for t, (g, ci, mask, h) in enumerate(jobs):
                if t + AHEAD < len(jobs):
                    ss.append(score(t + AHEAD))
                s = ss[t] if mask is None else jnp.where(mask, ss[t], NEG)
                m_new = jnp.maximum(m[g][h], jnp.max(s, axis=0, keepdims=True))
                alpha = jnp.exp2(m[g][h] - m_new)
                p = jnp.exp2(s - m_new)
                l[g][h] = alpha * l[g][h] + jnp.sum(p, axis=0, keepdims=True)
                m[g][h] = m_new
                n = chunks[ci][1]
                if n == BLK:
                    pv = _dot(vt_ref[pl.ds(HEAD_DIM * h, HEAD_DIM), pl.ds(k0s[ci], BLK)], p.astype(BF16), 1, 0)
                else:
                    vm = jnp.where(mas[h], v_ref[0:n, :], jnp.zeros((), BF16))
                    pv = _dot(vm, p.astype(BF16), 0, 0)[HEAD_DIM * h:HEAD_DIM * (h + 1)]
                acc[g][h] = alpha * acc[g][h] + pv
            return [(m[g][0], l[g][0], m[g][1], l[g][1], jnp.concatenate(acc[g], axis=0)) for g in range(len(groups))]

        def full_chunks(kcs, carry):
            return update([(kc, BLK) for kc in kcs], [(0, ci, None) for ci in range(len(kcs))], [(0, QB, carry)])[0]

        neg = jnp.full((1, QB), NEG, F32)
        zero = jnp.zeros((1, QB), F32)
        c = (neg, zero, neg, zero, jnp.zeros((LANES, QB), F32))
        n_mid = jnp.maximum(b0 - 1, 0)
        c = lax.fori_loop(0, n_mid // 4, lambda t, cr: full_chunks([4 * t + u for u in (1, 2, 3, 4)], cr), c)
        c = lax.fori_loop(0, (n_mid % 4) // 2, lambda t, cr: full_chunks([n_mid - 1, n_mid], cr), c)
        key_l = lax.broadcasted_iota(jnp.int32, (BLK, BLK), 0)
        qry_l = lax.broadcasted_iota(jnp.int32, (BLK, BLK), 1)
        tri = (key_l <= qry_l) & (b0 > 0)
        meta_ok = (key_l[0:N_META] <= qry_l[0:N_META]) | (b0 > 0)
        lo, hi = update([(0, N_META), (b0, BLK), (b0 + 1, BLK)],
                        [(0, 0, meta_ok), (1, 0, None), (0, 1, tri), (1, 1, None), (1, 2, tri)],
                        [(0, BLK, tuple(a[:, 0:BLK] for a in c)), (BLK, QB - BLK, tuple(a[:, BLK:QB] for a in c))])
        c = tuple(jnp.concatenate([a, b], axis=1) for a, b in zip(lo, hi))
        inv =jnp.concatenate([jnp.broadcast_to(1.0 / c[1], (HEAD_DIM, QB)),
                               jnp.broadcast_to(1.0 / c[3], (HEAD_DIM, QB))], axis=0)
        o_t = (c[4] * inv).T.astype(BF16)
        lses = [c[2 * h] + jnp.log(c[2 * h + 1]) * LOG2E for h in range(2)]
        o_ref[pl.ds(r0, BLK), :] = o_t[0:BLK]
        for h in range(2):
            lse_ref[0, h:h + 1, pl.ds(r0, BLK)] = lses[h][:, 0:BLK]

        @pl.when(i > 0)
        def _():
            r1 = pl.multiple_of(r0 + BLK, BLK)
            o_ref[pl.ds(r1, QB - BLK), :] = o_t[BLK:QB]
            for h in range(2):
                lse_ref[0, h:h + 1, pl.ds(r1, QB - BLK)] = lses[h][:, BLK:QB]

    in_specs = [pl.BlockSpec((lp, qw), lambda p, i: (0, qcol + p)),
                pl.BlockSpec((lp, 128), lambda p, i: (0, kcol(p))),
                pl.BlockSpec((BLK, 128), lambda p, i: (0, vcol(p))),
                pl.BlockSpec((128, lp), lambda p, i: (p, 0)),
                pl.BlockSpec((lp, 128), lambda p, i: (0, 0))]
    return pl.pallas_call(
        body, name=name, grid=(PAIRS, nq), in_specs=in_specs,
        out_specs=[pl.BlockSpec((lp, 128), lambda p, i: (0, p)),
                   pl.BlockSpec((1, 2, lp), lambda p, i: (p, 0, 0))],
        out_shape=[jax.ShapeDtypeStruct((lp, D_MODEL), BF16), jax.ShapeDtypeStruct((PAIRS, 2, lp), F32)],
        compiler_params=_cp(("parallel", "arbitrary"), VMEM_BIG))(q, k, v, vt, k2)


def _attn_bwd(q, k, v, k2, do, delta, lse, *, rtabs=None, scale, qcol, kcol, vcol, name):
    lp = q.shape[0]
    nb = lp // BLK
    rope = rtabs is not None
    bias = not rope
    qw = 256 if rope else 128

    def body(*refs):
        it = iter(refs)
        q_ref, k_ref, v_ref, k2_ref = next(it), next(it), next(it), next(it)
        do_ref, dl_ref, lse_ref = next(it), next(it), next(it)
        ct_ref, st_ref = (next(it), next(it)) if rope else (None, None)
        dq_out, dk_ref, dv_ref = next(it), next(it), next(it)
        x_ref = next(it)
        drow_ref = next(it) if bias else None
        dq_ref = next(it)
        kb = pl.program_id(1)
        mas, hmask = _pair_masks(rope, pl.program_id(0))
        lane = lax.broadcasted_iota(jnp.int32, (1, LANES), 1)

        @pl.when(kb == 0)
        def _():
            dq_ref[...] = jnp.zeros_like(dq_ref)
            if bias:
                drow_ref[...] = jnp.zeros_like(drow_ref)

        def key_pass(n, w):
            kk = jnp.concatenate([k_ref[0:n, :], k2_ref[0:n, :]], axis=1)
            vh = _mask2(v_ref[0:n, :], mas)
            kcat = jnp.concatenate([x[:, 0:qw] for x in _mask2(kk, hmask)], axis=0)
            diag_mask = (lax.broadcasted_iota(jnp.int32, (n, w), 0) <= lax.broadcasted_iota(jnp.int32, (n, w), 1))

            def front(qc):
                q0 = qc * w if isinstance(qc, int) else pl.multiple_of(qc * w, w)
                dov = do_ref[pl.ds(q0, w), :]
                qh = _q_heads(q_ref[pl.ds(q0, w), :], rope, mas, hmask)
                ss, dps = [], []
                for h in range(2):
                    ss.append(_dot(kk, qh[h], 1, 1))
                    dps.append(_dot(vh[h], dov, 1, 1))
                return q0, dov, qh, ss, dps

            def back(fronted, carry, mask):
                carry = list(carry)
                q0, dov, qh, ss, dps = fronted
                doh = _mask2(dov, mas)
                pbs, dss = [], []
                for h in range(2):
                    p = jnp.exp2(ss[h] - lse_ref[0, h:h + 1, pl.ds(q0, w)])
                    if mask is not None:
                        p = jnp.where(mask, p, 0.0)
                    ds = p * (dps[h] - dl_ref[0, h:h + 1, pl.ds(q0, w)])
                    if bias:
                        drow_ref[0, h:h + 1, pl.ds(q0, w)] += jnp.sum(ds, axis=0, keepdims=True)
                        carry[2 + h] = carry[2 + h] + jnp.sum(ds, axis=1, keepdims=True)
                    pbs.append(p.astype(BF16))
                    dss.append(ds.astype(BF16))
                ds_lanes = jnp.concatenate(dss, axis=1)
                ds_rows = jnp.concatenate(dss, axis=0)
                qcat = jnp.concatenate([x[:, 0:qw] for x in qh], axis=0)
                carry[0] = carry[0] + _dot(ds_lanes, qcat, 1, 0)
                carry[1] = carry[1] + _dot(jnp.concatenate(pbs, axis=1), jnp.concatenate(doh, axis=0), 1, 0)
                dq_ref[pl.ds(q0, w), :] += _dot(ds_rows, kcat, 0, 0)
                return tuple(carry)

            def chunks(qcs, carry, masks):
                ahead = AHEAD_BWD
                fronted = [front(qc) for qc in qcs[:ahead]]
                for u, mask in enumerate(masks):
                    if u + ahead < len(qcs):
                        fronted.append(front(qcs[u + ahead]))
                    carry = back(fronted[u], carry, mask)
                return carry

            c = [jnp.zeros((n, qw), F32), jnp.zeros((n, LANES), F32)]
            if bias:
                c += [jnp.zeros((n, 1), F32), jnp.zeros((n, 1), F32)]
            c = tuple(c)
            if w != BLK:
                c = chunks(list(range(lp // w)), c, [diag_mask] + [None] * (lp // w - 1))
            else:
                start = kb
                for width in UNROLLS:

                    def several(t, cr, start=start, width=width):
                        qc = start + width * t
                        return chunks([qc + u for u in range(width)], cr, [diag_mask | (qc > kb)] + [None] * (width - 1))

                    trips = (nb - start) // width
                    c = lax.fori_loop(0, trips, several, c)
                    start = start + width * trips

            def rows(a, dtype):
                a = a.astype(dtype)
                return a if n == BLK else jnp.concatenate([a, jnp.zeros((BLK - n, a.shape[1]), dtype)], axis=0)

            dk = c[0] * LN2
            dk_ref[...] = rows(dk[:, 0:LANES], BF16)
            dv_ref[...] = rows(c[1], BF16)
            if rope:
                x_ref[0] = rows(dk[:, LANES:2 * LANES], F32)
            if bias:
                x_ref[0] = rows(jnp.where(lane == 0, c[2], jnp.where(lane == 1, c[3], 0.0)), F32)

        @pl.when(kb == 0)
        def _():
            key_pass(N_META, lp // 2)

        @pl.when(kb > 0)
        def _():
            key_pass(BLK, BLK)

        @pl.when(kb == nb - 1)
        def _():
            def fin(c, carry):
                r0 = pl.multiple_of(c * BLK, BLK)
                dq = dq_ref[pl.ds(r0, BLK), :] * scale
                if rope:
                    back = _rope(dq[:, LANES:2 * LANES], ct_ref[pl.ds(r0, BLK), :], -st_ref[pl.ds(r0, BLK), :])
                    dq = jnp.concatenate([dq[:, 0:LANES], back], axis=1)
                dq_out[pl.ds(r0, BLK), :] = dq.astype(BF16)
                return carry

            lax.fori_loop(0, nb, fin, 0)

    in_specs = [pl.BlockSpec((lp, qw), lambda p, j: (0, qcol + p)),
                pl.BlockSpec((BLK, 128), lambda p, j: (j, kcol(p))),
                pl.BlockSpec((BLK, 128), lambda p, j: (j, vcol(p))),
                pl.BlockSpec((BLK, 128), lambda p, j: (j, 0)),
                pl.BlockSpec((lp, 128), lambda p, j: (0, p)), pl.BlockSpec((1, 2, lp), lambda p, j: (p, 0, 0)),
                pl.BlockSpec((1, 2, lp), lambda p, j: (p, 0, 0))]
    ins = [q, k, v, k2, do, delta, lse]
    if rope:
        in_specs += [pl.BlockSpec((lp, 128), lambda p, j: (0, 0))] * 2
        ins += list(rtabs)
    out_specs = [pl.BlockSpec((lp, qw), lambda p, j: (0, p)),
                 pl.BlockSpec((BLK, 128), lambda p, j: (j, p)),
                 pl.BlockSpec((BLK, 128), lambda p, j: (j, p)),
                 pl.BlockSpec((1, BLK, 128), lambda p, j: (p, j, 0))]
    out_shape = [jax.ShapeDtypeStruct((lp, PAIRS * qw), BF16), jax.ShapeDtypeStruct((lp, D_MODEL), BF16),
                 jax.ShapeDtypeStruct((lp, D_MODEL), BF16), jax.ShapeDtypeStruct((PAIRS, lp, 128), F32)]
    if bias:
        out_specs.append(pl.BlockSpec((1, 2, lp), lambda p, j: (p, 0, 0)))
        out_shape.append(jax.ShapeDtypeStruct((PAIRS, 2, lp), F32))
    return pl.pallas_call(
        body, name=name, grid=(PAIRS, nb), in_specs=in_specs, out_specs=out_specs, out_shape=out_shape,
        scratch_shapes=[pltpu.VMEM((lp, qw), F32)],
        compiler_params=_cp(("parallel", "arbitrary"), VMEM_BIG))(*ins)


def _adamw(w, g, m, v, name):
    lead = w.ndim - 2
    rows, cols = w.shape[lead:]
    big = rows * cols > 512 * 1024
    tr = 128 if big and rows % 128 == 0 else rows
    tc = 256 if big and tr == rows else cols

    def body(w_ref, g_ref, m_ref, v_ref, d_ref, nm_ref, nv_ref):
        gv = g_ref[...]
        nm = ADAM_B1 * m_ref[...] + (1.0 - ADAM_B1) * gv
        nv = ADAM_B2 * v_ref[...] + (1.0 - ADAM_B2) * (gv * gv)
        m_hat = nm / (1.0 - ADAM_B1 ** ADAM_STEP)
        v_hat = nv / (1.0 - ADAM_B2 ** ADAM_STEP)
        d_ref[...] = -ADAM_LR * (m_hat / (jnp.sqrt(v_hat) + ADAM_EPS) + ADAM_WD * w_ref[...])
        nm_ref[...] = nm
        nv_ref[...] = nv

    spec = pl.BlockSpec((1,) * lead + (tr, tc), lambda i, j: (0,) * lead + (i, j))
    return pl.pallas_call(
        body, name=name, grid=(rows // tr, cols // tc), in_specs=[spec] * 4, out_specs=[spec] * 3,
        out_shape=[jax.ShapeDtypeStruct(w.shape, F32)] * 3,
        compiler_params=_cp(("parallel", "parallel"), VMEM_BIG))(w, g, m, v)


def _add_cores(g, from_sib, name):
    n, rows, cols = g.shape
    half = rows // 2
    tr = _tile(half, (256, 240))
    nt = half // tr

    def body(lo_ref, hi_ref, s_ref, o_ref):
        mine = jnp.where(lax.axis_index("c") == 0, lo_ref[0], hi_ref[0])
        o_ref[0] = (mine.astype(F32) + s_ref[0].astype(F32)).astype(BF16)

    return pl.pallas_call(
        body, name=name, grid=(n, nt),
        in_specs=[pl.BlockSpec((1, tr, cols), lambda j, i: (j, i, 0)),
                  pl.BlockSpec((1, tr, cols), lambda j, i: (j, nt + i, 0)),
                  pl.BlockSpec((1, tr, cols), lambda j, i: (j, i, 0))],
        out_specs=pl.BlockSpec((1, tr, cols), lambda j, i: (j, i, 0)),
        out_shape=jax.ShapeDtypeStruct((n, half, cols), BF16),
        compiler_params=_cp(("parallel", "parallel"), VMEM_BIG))(g, g, from_sib)


def _add_chips(x, own, name):
    n, rows, cols = x.shape
    tr = _tile(rows, (256, 240))

    def body(x_ref, own_ref, o_ref):
        me = 2 * lax.axis_index("x") + lax.axis_index("y")
        v = [jnp.where(me == k, own_ref[...], x_ref[k]).astype(F32) for k in range(N_CHIPS)]
        o_ref[...] = ((v[0] + v[1]) + v[2]) + v[3]

    return pl.pallas_call(
        body, name=name, grid=(rows // tr,),
        in_specs=[pl.BlockSpec((n, tr, cols), lambda i: (0, i, 0)), pl.BlockSpec((tr, cols), lambda i: (i, 0))],
        out_specs=pl.BlockSpec((tr, cols), lambda i: (i, 0)),
        out_shape=jax.ShapeDtypeStruct((rows, cols), F32), compiler_params=_cp(("parallel",), VMEM_BIG))(x, own)


def _axes():
    return lax.axis_index("x"), lax.axis_index("y"), lax.axis_index("c")


def _other_chips(x, y):
    return [(1 - x, y), (x, 1 - y), (1 - x, 1 - y)]


ANY = pl.BlockSpec(memory_space=pl.ANY)


def _rcopy(src, dst, send_sems, recv_sems, k, to):
    return pltpu.make_async_remote_copy(src_ref=src, dst_ref=dst, send_sem=send_sems.at[k], recv_sem=recv_sems.at[k],
                                        device_id=to, device_id_type=MESH)


def _gather_weights(shards, meta):
    n = len(shards)

    def body(*refs):
        srcs, meta_ref = refs[:n], refs[n]
        outs, mout_ref = refs[n + 1:2 * n + 1], refs[2 * n + 1]
        send_sems, recv_sems = refs[2 * n + 2:]
        x, y, c = _axes()
        me = 2 * x + y
        sib = (x, y, 1 - c)
        chips = _other_chips(x, y)

        def half(t, chip_idx, cc):
            hr = shards[t].shape[0] // 2
            return outs[t].at[chip_idx, pl.ds(cc * hr, hr), :]

        first = []
        for j, (px, py) in enumerate(chips):
            for t in range(n):
                hr = shards[t].shape[0] // 2
                first.append(_rcopy(srcs[t].at[pl.ds(c * hr, hr), :], half(t, me, c), send_sems, recv_sems,
                                    3 * t + j, (px, py, c)))
            first.append(_rcopy(meta_ref, mout_ref.at[me], send_sems, recv_sems, 3 * n + j, (px, py, c)))
        for cp in first:
            cp.start()
        passed = []
        for j, (px, py) in enumerate(chips):
            src_chip = 2 * px + py
            for t in range(n):
                _rcopy(half(t, src_chip, c), half(t, src_chip, c), send_sems, recv_sems, 3 * t + j, sib).wait_recv()
                fwd = _rcopy(half(t, src_chip, c), half(t, src_chip, c), send_sems, recv_sems, 3 * (n + 1 + t) + j, sib)
                fwd.start()
                passed.append(fwd)
            _rcopy(mout_ref.at[src_chip], mout_ref.at[src_chip], send_sems, recv_sems, 3 * n + j, sib).wait_recv()
        for j, (px, py) in enumerate(chips):
            src_chip = 2 * px + py
            for t in range(n):
                _rcopy(half(t, src_chip, 1 - c), half(t, src_chip, 1 - c), send_sems, recv_sems,
                       3 * (n + 1 + t) + j, sib).wait_recv()
        for cp in first + passed:
            cp.wait_send()

    nsem = 3 * (2 * n + 1)
    return pl.pallas_call(
        body, name="gather_weights", in_specs=[ANY] * (n + 1), out_specs=[ANY] * (n + 1),
        out_shape=[jax.ShapeDtypeStruct((N_CHIPS,) + s.shape, s.dtype) for s in shards]
        + [jax.ShapeDtypeStruct((N_CHIPS,) + meta.shape, meta.dtype)],
        scratch_shapes=[pltpu.SemaphoreType.DMA((nsem,)), pltpu.SemaphoreType.DMA((nsem,))])(*shards, meta)


def _gather_late(shard):
    rows, cols = shard.shape
    hr = rows // 2
    src = jax.new_ref(shard, memory_space=pltpu.MemorySpace.HBM)
    out = jax.empty_ref(jax.ShapeDtypeStruct((N_CHIPS, rows, cols), shard.dtype), memory_space=pltpu.MemorySpace.HBM)

    @pl.kernel(mesh=plsc.ScalarSubcoreMesh(axis_name="seq", num_cores=1), name="gather_late",
               scratch_types=(pltpu.SemaphoreType.DMA((6,)), pltpu.SemaphoreType.DMA((6,))),
               compiler_params=pltpu.CompilerParams(collective_id=1))
    def launch(send_sems, recv_sems):
        x, y, c = _axes()
        me = 2 * x + y
        sib = (x, y, 1 - c)
        chips = _other_chips(x, y)
        barrier = pltpu.get_barrier_semaphore()
        for px, py in chips:
            pl.semaphore_signal(barrier, inc=1, device_id=(px, py, c), device_id_type=MESH)
        pl.semaphore_signal(barrier, inc=1, device_id=sib, device_id_type=MESH)
        pl.semaphore_wait(barrier, 4)

        def half(chip_idx, cc):
            return out.at[chip_idx, pl.ds(cc * hr, hr), :]

        first = [_rcopy(src.at[pl.ds(c * hr, hr), :], half(me, c), send_sems, recv_sems, j, (px, py, c))
                 for j, (px, py) in enumerate(chips)]
        for cp in first:
            cp.start()
        passed = []
        for j, (px, py) in enumerate(chips):
            land = half(2 * px + py, c)
            _rcopy(land, land, send_sems, recv_sems, j, sib).wait_recv()
            fwd = _rcopy(land, land, send_sems, recv_sems, 3 + j, sib)
            fwd.start()
            passed.append(fwd)
        for j, (px, py) in enumerate(chips):
            land = half(2 * px + py, 1 - c)
            _rcopy(land, land, send_sems, recv_sems, 3 + j, sib).wait_recv()
        for cp in first + passed:
            cp.wait_send()

    launch()
    return out[...]


def _swap_halves(gs):
    n = len(gs)
    ncopies = sum(g.shape[0] for g in gs)

    def body(*refs):
        srcs, outs = refs[:n], refs[n:2 * n]
        send_sems, recv_sems = refs[2 * n:]
        x, y, c = _axes()
        cps = []
        for t in range(n):
            hr = gs[t].shape[1] // 2
            for j in range(gs[t].shape[0]):
                cps.append(_rcopy(srcs[t].at[j, pl.ds((1 - c) * hr, hr), :], outs[t].at[j], send_sems, recv_sems,
                                  len(cps), (x, y, 1 - c)))
        for cp in cps:
            cp.start()
        for cp in cps:
            cp.wait()

    return pl.pallas_call(
        body, name="swap_halves", in_specs=[ANY] * n, out_specs=[ANY] * n,
        out_shape=[jax.ShapeDtypeStruct((g.shape[0], g.shape[1] // 2, g.shape[2]), g.dtype) for g in gs],
        scratch_shapes=[pltpu.SemaphoreType.DMA((ncopies,)), pltpu.SemaphoreType.DMA((ncopies,))])(*gs)


def _scatter_chips(parts):
    n = len(parts)
    srcs = [jax.new_ref(p, memory_space=pltpu.MemorySpace.HBM) for p in parts]
    outs = [jax.empty_ref(jax.ShapeDtypeStruct(p.shape, p.dtype), memory_space=pltpu.MemorySpace.HBM) for p in parts]

    @pl.kernel(mesh=plsc.ScalarSubcoreMesh(axis_name="seq", num_cores=1), name="scatter_chips",
               scratch_types=(pltpu.SemaphoreType.DMA((3 * n,)), pltpu.SemaphoreType.DMA((3 * n,))),
               compiler_params=pltpu.CompilerParams(collective_id=0))
    def launch(send_sems, recv_sems):
        x, y, c = _axes()
        me = 2 * x + y
        chips = _other_chips(x, y)
        barrier = pltpu.get_barrier_semaphore()
        for px, py in chips:
            pl.semaphore_signal(barrier, inc=1, device_id=(px, py, c), device_id_type=MESH)
        pl.semaphore_wait(barrier, 3)
        cps = []
        for j, (px, py) in enumerate(chips):
            for t in range(n):
                cps.append(_rcopy(srcs[t].at[2 * px + py], outs[t].at[me], send_sems, recv_sems, 3 * t + j,
                                  (px, py, c)))
        for cp in cps:
            cp.start()
        for cp in cps:
            cp.wait()

    launch()
    return [o[...] for o in outs]


def _swap_reduced(rs):
    n = len(rs)

    def body(*refs):
        srcs, outs = refs[:n], refs[n:2 * n]
        send_sems, recv_sems = refs[2 * n:]
        x, y, c = _axes()
        cps = [_rcopy(srcs[t], outs[t], send_sems, recv_sems, t, (x, y, 1 - c)) for t in range(n)]
        for cp in cps:
            cp.start()
        for cp in cps:
            cp.wait()

    return pl.pallas_call(
        body, name="swap_reduced", in_specs=[ANY] * n, out_specs=[ANY] * n,
        out_shape=[jax.ShapeDtypeStruct(r.shape, r.dtype) for r in rs],
        scratch_shapes=[pltpu.SemaphoreType.DMA((n,)), pltpu.SemaphoreType.DMA((n,))])(*rs)


SMALL_ROWS = 24 + 128


def _allreduce_small(vec):
    def body(v_ref, out_ref, slots, send_sems, recv_sems):
        x, y, c = _axes()
        me = 4 * x + 2 * y + c
        slots[me] = v_ref[...]
        cps = []
        for k in range(1, 8):
            kx, ky, kc = (k >> 2) & 1, (k >> 1) & 1, k & 1
            peer = (1 - x if kx else x, 1 - y if ky else y, 1 - c if kc else c)
            cps.append(_rcopy(v_ref, slots.at[me], send_sems, recv_sems, k - 1, peer))
        for cp in cps:
            cp.start()
        for cp in cps:
            cp.wait()
        tot = slots[0]
        for k in range(1, 8):
            tot = tot + slots[k]
        out_ref[...] = tot

    return pl.pallas_call(
        body, name="allreduce_small",
        in_specs=[pl.BlockSpec(memory_space=pltpu.VMEM)], out_specs=pl.BlockSpec(memory_space=pltpu.VMEM),
        out_shape=jax.ShapeDtypeStruct((SMALL_ROWS, 128), F32),
        scratch_shapes=[pltpu.VMEM((8, SMALL_ROWS, 128), F32), pltpu.SemaphoreType.DMA((7,)),
                        pltpu.SemaphoreType.DMA((7,))])(vec)


def _pack_p2(w_uq, w_ukv, w_br_mla, w_br_fox, w_out, dtype):
    parts = [w_uq.reshape(96, D_MODEL), w_ukv.reshape(64, D_MODEL), w_br_mla, w_br_fox, w_out]
    return jnp.concatenate([p.astype(dtype) for p in parts], axis=0)


def _unpack_p2(pk):
    return pk[0:96].reshape(256, 384), pk[96:160].reshape(128, 512), pk[160:416], pk[416:672], pk[672:928]


def _uq_arrange(w):
    w3 = w.reshape(256, HEADS, 96)
    nope = w3[:, :, :64].reshape(256, PAIRS, 128)
    pe = w3[:, :, 64:].reshape(256, PAIRS, 64)
    return jnp.concatenate([nope, pe, jnp.zeros((256, PAIRS, 64), w.dtype)], axis=2).reshape(256, PAIRS * 256)


def _uq_restore(g):
    g3 = g.reshape(256, PAIRS, 256)
    nope = g3[:, :, :128].reshape(256, HEADS, 64)
    pe = g3[:, :, 128:192].reshape(256, HEADS, 32)
    return jnp.concatenate([nope, pe], axis=2).reshape(256, HEADS * 96)


def _ukv_arrange(w):
    w3 = w.reshape(128, HEADS, 128)
    return jnp.concatenate([w3[:, :, :64].reshape(128, 1024), w3[:, :, 64:].reshape(128, 1024)], axis=1)


def _ukv_restore(g):
    kn = g[:, :1024].reshape(128, HEADS, 64)
    vv = g[:, 1024:].reshape(128, HEADS, 64)
    return jnp.concatenate([kn, vv], axis=2).reshape(128, HEADS * 128)


def _rope_tables(lp):
    r = np.arange(lp)
    pos = np.where(r < N_META, r, np.where(r >= PAD, r - PAD + N_META, 0)).astype(np.float32)
    half = MLA_ROPE // 2
    inv_freq = np.float32(ROPE_THETA) ** (-np.arange(half, dtype=np.float32) / np.float32(half))
    ang = (pos[:, None] * inv_freq[None, :]).astype(np.float32)
    cos, sin = np.cos(ang).astype(np.float32), np.sin(ang).astype(np.float32)
    one, zero = np.ones((lp, 64), np.float32), np.zeros((lp, 64), np.float32)
    return (jnp.asarray(np.concatenate([cos, cos, cos, cos, one], axis=1)),
            jnp.asarray(np.concatenate([-sin, sin, -sin, sin, zero], axis=1)))


def _pad_lanes(v, n=128):
    return jnp.pad(v, ((0, 0), (0, n - v.shape[1])))


def _in_cols(slabs, a, b):
    out = []
    for j in range(N_CHIPS):
        lo, hi = max(a, W_IN_SHARD * j), min(b, W_IN_SHARD * (j + 1))
        if lo < hi:
            out.append(slabs[j][:, lo - W_IN_SHARD * j:hi - W_IN_SHARD * j])
    return out


def _local_step(x2, tgt2, meta_f, w_small, w_attn, w_gate, w_uq_f, w_ukv_f, w_bm, w_bf, w_o, pre_norm_g,
                post_norm_g, mla_q_norm_g, mla_kv_norm_g, fox_forget_b, start_exchange=None):
    s_rows = x2.shape[0]
    lp = PAD + s_rows
    w_uq_a = _uq_arrange(w_uq_f)
    w_ukv_a = _ukv_arrange(w_ukv_f)

    ctab, stab = _rope_tables(lp)
    ii = jnp.arange(BLK)
    tri_lo = (ii[:, None] >= ii[None, :]).astype(BF16)
    tri_up = (ii[:, None] <= ii[None, :]).astype(BF16)
    fb128 = _pad_lanes(fox_forget_b)

    u = _rms_pre(x2, meta_f, pre_norm_g)
    small = _mm(u, w_small, mode="nn", out_dtype=F32, name="proj_small")
    attn = _mm(u, w_attn, mode="nn", out_dtype=BF16, name="proj_attn",
               col_scale=(HEADS * HEAD_DIM, FOX_SCALE * LOG2E))
    gate = _mm(u, w_gate, mode="nn", out_dtype=BF16, name="proj_gate")
    qn, kvn, kr, kb = _small_prep(small, mla_q_norm_g, mla_kv_norm_g, fb128, ctab, stab, tri_lo)
    qcat = _mm(qn, w_uq_a, mode="nn", out_dtype=BF16, name="mla_q", row_ins=(ctab, stab),
               epilogue=lambda tile, c, s: _rope_pairs(tile, c, s) * (MLA_SCALE * LOG2E))
    kv = _mm(kvn, w_ukv_a, mode="nn", out_dtype=BF16, name="mla_kv")

    mla_cols = dict(qcol=0, kcol=lambda p: p, vcol=lambda p: PAIRS + p)
    fox_cols = dict(qcol=0, kcol=lambda p: PAIRS + p, vcol=lambda p: 2 * PAIRS + p)
    o_mla, lse_mla = _attn_fwd(qcat, kv, kv, _transposed_cols(kv, 1, "mla_vt"), kr, rope=True, name="mla_fwd",
                               **mla_cols)
    o_fox, lse_fox = _attn_fwd(attn, attn, attn, _transposed_cols(attn, 2, "fox_vt"), kb, rope=False,
                               name="fox_fwd", **fox_cols)

    a_mla, a_fox = _gate_fwd(o_mla, o_fox, gate)
    y_mla = _mm(a_mla, w_bm, mode="nn", out_dtype=BF16, name="br_mla")
    y_fox = _mm(a_fox, w_bf, mode="nn", out_dtype=BF16, name="br_fox")
    mg = _merge_fwd(gate, y_mla, y_fox)
    mixed = _mm(mg, w_o, mode="nn", out_dtype=F32, name="out_proj")
    dmixed, dy, loss_p, dg_post = _tail(x2, mixed, tgt2, post_norm_g)

    d_w_out = _mm(mg, dmixed, mode="tn", out_dtype=F32, name="d_w_out")
    dm = _mm(dmixed, w_o, mode="nt", out_dtype=BF16, name="d_merge")
    dy_mla, dy_fox, dgate_ab = _merge_bwd(dm, gate, y_mla, y_fox)
    d_w_bm = _mm(a_mla, dy_mla, mode="tn", out_dtype=F32, name="d_w_br_mla")
    d_w_bf = _mm(a_fox, dy_fox, mode="tn", out_dtype=F32, name="d_w_br_fox")
    da_mla = _mm(dy_mla, w_bm, mode="nt", out_dtype=BF16, name="d_a_mla")
    da_fox = _mm(dy_fox, w_bf, mode="nt", out_dtype=BF16, name="d_a_fox")
    do_mla, do_fox, dgate_z, dl_mla, dl_fox = _gate_bwd(da_mla, da_fox, o_mla, o_fox, gate)
    dl_mla, dl_fox = (d[:, :HEADS].T.reshape(PAIRS, 2, lp) for d in (dl_mla, dl_fox))

    dq_a, dkn, dvm, dkr = _attn_bwd(qcat, kv, kv, kr, do_mla, dl_mla, lse_mla, rtabs=(ctab, stab),
                                    scale=MLA_SCALE, name="mla_bwd", **mla_cols)
    dfq, dfk, dfv, dcol, drow = _attn_bwd(attn, attn, attn, kb, do_fox, dl_fox, lse_fox, scale=FOX_SCALE,
                                          name="fox_bwd", **fox_cols)

    d_w_uq_a = _mm(qn, dq_a, mode="tn", out_dtype=F32, name="d_w_uq")
    dqn = _mm(dq_a, w_uq_a, mode="nt", out_dtype=F32, name="d_qn")
    d_w_ukv_a = jnp.concatenate([_mm(kvn, dkn, mode="tn", out_dtype=F32, name="d_w_uk"),
                                 _mm(kvn, dvm, mode="tn", out_dtype=F32, name="d_w_uv")], axis=1)
    dkvn = _mm(dkn, w_ukv_a[:, :1024], mode="nt", out_dtype=F32, name="d_kvn_k")
    dkvn = _mm(dvm, w_ukv_a[:, 1024:], mode="nt", out_dtype=F32, name="d_kvn_v", acc=dkvn)
    dsmall, dg_q, dg_kv, dfb = _small_bwd(small, dqn, dkvn, dkr, dcol, drow, mla_q_norm_g, mla_kv_norm_g,
                                          fb128, ctab, stab, tri_up)

    dw_small = _mm(u, dsmall, mode="tn", out_dtype=BF16, name="d_w_small")
    dw_fq = _mm(u, dfq, mode="tn", out_dtype=BF16, name="d_w_fq")
    dw_fk = _mm(u, dfk, mode="tn", out_dtype=BF16, name="d_w_fk")
    dw_fv = _mm(u, dfv, mode="tn", out_dtype=BF16, name="d_w_fv")
    dw_z = _mm(u, dgate_z, mode="tn", out_dtype=BF16, name="d_w_z")
    dw_g = _mm(u, dgate_ab, mode="tn", out_dtype=BF16, name="d_w_g")
    d_w_in = (dw_small, dw_z, dw_fq, dw_fk, dw_fv, dw_g)
    d_w_uq = _uq_restore(d_w_uq_a)
    d_w_ukv = _ukv_restore(d_w_ukv_a)
    token = start_exchange(d_w_in, d_w_uq, d_w_ukv, d_w_bm, d_w_bf, d_w_out) if start_exchange else None
    du = _mm_sum_nt([(dsmall, w_small), (dfq, w_attn[:, 0:1024]), (dfk, w_attn[:, 1024:2048]),
                     (dfv, w_attn[:, 2048:3072]), (dgate_z, w_gate[:, 0:2048]), (dgate_ab, w_gate[:, 2048:4096])],
                    name="d_u", after=token)
    dx, dmeta, dg_pre = _pre_bwd(du, x2, meta_f, dy, pre_norm_g)
    return (loss_p, dx, dmeta, d_w_in, d_w_uq, d_w_ukv, d_w_bm, d_w_bf, d_w_out, dg_pre, dg_post, dg_q, dg_kv, dfb)


def _w_in_slabs(pieces):
    dw_small, dw_z, dw_fq, dw_fk, dw_fv, dw_g = pieces
    runs = [(dw_small[:, 0:416], C_CQ), (dw_z[:, 0:1024], C_ZMLA), (dw_fq, C_FQ), (dw_fk, C_FK), (dw_fv, C_FV),
            (dw_small[:, 512:528], C_FL), (dw_z[:, 1024:2048], C_ZFOX), (dw_g, C_GA)]
    slabs = []
    for j in range(N_CHIPS):
        lo, hi = W_IN_SHARD * j, W_IN_SHARD * (j + 1)
        cols = [a[:, max(lo, c0) - c0:min(hi, c0 + a.shape[1]) - c0] for a, c0 in runs
                if max(lo, c0) < min(hi, c0 + a.shape[1])]
        slabs.append(jnp.concatenate(cols, axis=1))
    return jnp.stack(slabs, axis=0)


def kernel(x, meta_tokens, pre_norm_g, w_in, fox_forget_b, mla_q_norm_g, mla_kv_norm_g, w_uq, w_ukv, w_br_mla, w_br_fox, w_out, post_norm_g, loss_target, m_meta_tokens, m_pre_norm_g, m_w_in, m_fox_forget_b, m_mla_q_norm_g, m_mla_kv_norm_g, m_w_uq, m_w_ukv, m_w_br_mla, m_w_br_fox, m_w_out, m_post_norm_g, v_meta_tokens, v_pre_norm_g, v_w_in, v_fox_forget_b, v_mla_q_norm_g, v_mla_kv_norm_g, v_w_uq, v_w_ukv, v_w_br_mla, v_w_br_fox, v_w_out, v_post_norm_g):
    me = 2 * lax.axis_index("x") + lax.axis_index("y")
    core = lax.axis_index("c")
    w_in_b = w_in.astype(BF16).reshape(D_MODEL, W_IN_SHARD)
    p2 = _pack_p2(w_uq[0], w_ukv[0], w_br_mla[0], w_br_fox[0], w_out[0], BF16)
    w_in_g, meta_g = _gather_weights([w_in_b], meta_tokens)
    p2_g = _gather_late(lax.optimization_barrier((p2, w_in_g))[0])
    slabs = [jnp.where(me == j, w_in_b, w_in_g[j]) for j in range(N_CHIPS)]
    chip = lax.broadcasted_iota(jnp.int32, (N_CHIPS, 1, 1), 0)
    p2_all = jnp.where(chip == me, p2[None], p2_g)
    w_uq_f = p2_all[:, 0:96].reshape(N_CHIPS, 256, 384).transpose(1, 0, 2).reshape(256, 1536)
    w_ukv_f = p2_all[:, 96:160].reshape(N_CHIPS, 128, 512).transpose(1, 0, 2).reshape(128, 2048)
    w_bm, w_bf, w_o = (p2_all[:, lo:lo + 256].reshape(D_MODEL, D_MODEL) for lo in (160, 416, 672))
    meta_f = jnp.where(chip == me, meta_tokens[None], meta_g).transpose(1, 0, 2).reshape(N_META, D_MODEL)
    kpe = _in_cols(slabs, C_KPE, C_ZMLA)
    w_small = jnp.concatenate(_in_cols(slabs, C_CQ, C_KPE) + kpe + kpe + [jnp.zeros((D_MODEL, 64), BF16)]
                              + _in_cols(slabs, C_FL, C_ZFOX) + [jnp.zeros((D_MODEL, 112), BF16)], axis=1)
    w_attn = jnp.concatenate(_in_cols(slabs, C_FQ, C_FL), axis=1)
    w_gate = jnp.concatenate(_in_cols(slabs, C_ZMLA, C_FQ) + _in_cols(slabs, C_ZFOX, C_END), axis=1)

    exchange = {}

    def start_exchange(d_w_in, d_w_uq, d_w_ukv, d_w_bm, d_w_bf, d_w_out):
        g2 = jnp.concatenate(
            [d_w_uq.reshape(256, N_CHIPS, 384).transpose(1, 0, 2).reshape(N_CHIPS, 96, D_MODEL),
             d_w_ukv.reshape(128, N_CHIPS, 512).transpose(1, 0, 2).reshape(N_CHIPS, 64, D_MODEL)]
            + [g.reshape(N_CHIPS, 256, D_MODEL) for g in (d_w_bm, d_w_bf, d_w_out)], axis=1)
        pieces = [p[None] for p in d_w_in]
        from_sib = _swap_halves(pieces + [g2])
        halves = [_add_cores(p, s, "add_cores_" + nm)[0]
                  for p, s, nm in zip(pieces, from_sib, ("small", "z", "fq", "fk", "fv", "g"))]
        parts = [_w_in_slabs(halves), _add_cores(g2, from_sib[-1], "add_cores_rest")]
        exchange.update(parts=parts, landed=_scatter_chips(parts))
        return parts[0][0, 0:16, 0:LANES]

    (loss_p, dx, dmeta, _, _, _, _, _, _, dg_pre, dg_post, dg_q, dg_kv,
     dfb) = _local_step(x[0], loss_target[0], meta_f, w_small, w_attn, w_gate, w_uq_f, w_ukv_f, w_bm, w_bf, w_o,
                        pre_norm_g, post_norm_g, mla_q_norm_g, mla_kv_norm_g, fox_forget_b, start_exchange)

    mine = [_add_chips(l, lax.dynamic_index_in_dim(p, me, 0, keepdims=False), nm)
            for l, p, nm in zip(exchange["landed"], exchange["parts"], ("add_chips_w_in", "add_chips_rest"))]
    theirs = _swap_reduced(mine)
    g_w_in, g_p2 = [jnp.concatenate([jnp.where(core == 0, a, b), jnp.where(core == 0, b, a)], axis=0)
                    for a, b in zip(mine, theirs)]
    g_w_uq, g_w_ukv, g_w_bm, g_w_bf, g_w_out = _unpack_p2(g_p2)
    g_w_in = g_w_in[None]

    vec = jnp.concatenate([dg_pre.reshape(8, 128), dg_post.reshape(8, 128), dg_q.reshape(2, 128), dg_kv,
                           dfb, _pad_lanes(loss_p), jnp.zeros((3, 128), F32), dmeta.reshape(128, 128)], axis=0)
    tot = _allreduce_small(vec)
    loss = tot[20, 0]
    g_meta = lax.dynamic_slice_in_dim(tot[24:].reshape(N_META, D_MODEL), 256 * me, 256, axis=1)

    def small_pack(pre, post, gq_, gkv_, fb_):
        return jnp.concatenate([pre.reshape(8, 128), post.reshape(8, 128), gq_.reshape(2, 128), gkv_,
                                _pad_lanes(fb_), jnp.zeros((4, 128), F32)], axis=0)

    def small_unpack(t):
        return (t[0:8].reshape(1, 1024), t[8:16].reshape(1, 1024), t[16:18].reshape(1, 256), t[18:19],
                t[19:20, 0:HEADS])

    g_small = jnp.concatenate([tot[0:20], jnp.zeros((4, 128), F32)], axis=0)
    sm = _adamw(small_pack(pre_norm_g, post_norm_g, mla_q_norm_g, mla_kv_norm_g, fox_forget_b), g_small,
                small_pack(m_pre_norm_g, m_post_norm_g, m_mla_q_norm_g, m_mla_kv_norm_g, m_fox_forget_b),
                small_pack(v_pre_norm_g, v_post_norm_g, v_mla_q_norm_g, v_mla_kv_norm_g, v_fox_forget_b),
                "adamw_small")
    g_pre, g_post, g_q, g_kv, g_fb = small_unpack(g_small)
    (d_pre, d_post, d_q, d_kv, d_fb), (nm_pre, nm_post, nm_q, nm_kv, nm_fb), (nv_pre, nv_post, nv_q, nv_kv, nv_fb) = (
        small_unpack(t) for t in sm)

    d_meta, nm_meta, nv_meta = _adamw(meta_tokens, g_meta, m_meta_tokens, v_meta_tokens, "adamw_meta")
    d_win, nm_win, nv_win = (t.T[None] for t in _adamw(w_in[0].T, g_w_in[0].T, m_w_in[0].T, v_w_in[0].T,
                                                       "adamw_w_in"))
    d_wuq, nm_wuq, nv_wuq = _adamw(w_uq[0], g_w_uq, m_w_uq[0], v_w_uq[0], "adamw_w_uq")
    d_wukv, nm_wukv, nv_wukv = _adamw(w_ukv[0], g_w_ukv, m_w_ukv[0], v_w_ukv[0], "adamw_w_ukv")
    d_wbm, nm_wbm, nv_wbm = _adamw(w_br_mla[0], g_w_bm, m_w_br_mla[0], v_w_br_mla[0], "adamw_w_br_mla")
    d_wbf, nm_wbf, nv_wbf = _adamw(w_br_fox[0], g_w_bf, m_w_br_fox[0], v_w_br_fox[0], "adamw_w_br_fox")
    d_wo, nm_wo, nv_wo = _adamw(w_out[0], g_w_out, m_w_out[0], v_w_out[0], "adamw_w_out")

    def group(meta_, pre, win, fb_, q_, kv_, wuq, wukv, wbm, wbf, wo, post):
        return (meta_, pre, win, fb_, q_, kv_, wuq[None], wukv[None], wbm[None], wbf[None], wo[None], post)

    grads = group(g_meta, g_pre, g_w_in, g_fb, g_q, g_kv, g_w_uq, g_w_ukv, g_w_bm, g_w_bf, g_w_out, g_post)
    deltas = group(d_meta, d_pre, d_win, d_fb, d_q, d_kv, d_wuq, d_wukv, d_wbm, d_wbf, d_wo, d_post)
    new_m = group(nm_meta, nm_pre, nm_win, nm_fb, nm_q, nm_kv, nm_wuq, nm_wukv, nm_wbm, nm_wbf, nm_wo, nm_post)
    new_v = group(nv_meta, nv_pre, nv_win, nv_fb, nv_q, nv_kv, nv_wuq, nv_wukv, nv_wbm, nv_wbf, nv_wo, nv_post)
    return (loss, dx[None], *grads, *deltas, *new_m, *new_v)
```

```python
import math

import jax
import jax.numpy as jnp
import numpy as np
from jax import lax
from jax.experimental import pallas as pl
from jax.experimental.pallas import tpu as pltpu
from jax.experimental.pallas import tpu_sc as plsc

F32 = jnp.float32
BF16 = jnp.bfloat16

D_MODEL = 1024
N_META = 16
RMS_EPS = 1e-6
HEADS = 16
PAIRS = HEADS // 2
HEAD_DIM = 64
LANES = 128
MLA_ROPE = 32
AHEAD = 6
AHEAD_BWD = 1
BIAS_PARTS = 3
MLA_SCALE = 1.0 / math.sqrt(64 + 32)
FOX_SCALE = 1.0 / math.sqrt(64)
LOG2E = math.log2(math.e)
LN2 = math.log(2.0)
ROPE_THETA = 10000.0

PAD = 256
BLK = 256
QB = 512
UNROLLS = (8, 4, 2, 1)
NEG = -1e30

C_CQ, C_CKV, C_KPE, C_ZMLA, C_FQ, C_FK, C_FV, C_FL, C_ZFOX, C_GA, C_GB, C_END = (
    0, 256, 384, 416, 1440, 2464, 3488, 4512, 4528, 5552, 6576, 7600)
SMALL_W = 640
W_IN_SHARD = 1900

P2_ROWS = 928
N_CHIPS = 4

ADAM_LR = 0.001
ADAM_B1 = 0.9
ADAM_B2 = 0.999
ADAM_EPS = 1e-08
ADAM_WD = 0.01
ADAM_STEP = 10

VMEM_BIG = 56 * 1024 * 1024
MM_VMEM_BUDGET = 44 * 1024 * 1024
MESH = pl.DeviceIdType.MESH


def _cp(dims, vmem=None):
    return pltpu.CompilerParams(dimension_semantics=dims, vmem_limit_bytes=vmem)


def _dot(a, b, ca, cb):
    return lax.dot_general(a, b, (((ca,), (cb,)), ((), ())), preferred_element_type=F32)


def _sigmoid(x):
    return 1.0 / (1.0 + jnp.exp(-x))


def _tile(n, cands):
    for c in cands:
        if n % c == 0:
            return c
    return n


def _mm(a, b, *, mode, out_dtype, name, acc=None, epilogue=None, row_ins=(), after=None, col_scale=None):
    if mode == "nn":
        (M, K), N = a.shape, b.shape[1]
    elif mode == "nt":
        (M, K), N = a.shape, b.shape[0]
    else:
        (K, M), N = a.shape, b.shape[1]
    tm = _tile(M, (1088, 1024)) if M > 1024 else M
    tn = _tile(N, (1024,)) if N > 1024 else N
    nk = 1
    while True:
        tk = K // nk
        need = 2 * tk * (tm * a.dtype.itemsize + tn * b.dtype.itemsize) + tm * tn * (
            2 * jnp.dtype(out_dtype).itemsize + (8 if acc is not None else 0) + (4 if nk > 1 else 0))
        if need <= MM_VMEM_BUDGET or (tk // 2) % (16 if mode == "tn" else LANES) or tk <= 512:
            break
        nk *= 2
    while (M // tm) * (N // tn) * nk < 4 and tn % 512 == 0:
        tn //= 2
    assert col_scale is None or (nk == 1 and col_scale[0] % tn == 0)
    ca, cb = {"nn": (1, 0), "nt": (1, 1), "tn": (0, 0)}[mode]
    a_spec = (pl.BlockSpec((tk, tm), lambda j, i, k: (k, i)) if mode == "tn"
              else pl.BlockSpec((tm, tk), lambda j, i, k: (i, k)))
    b_spec = (pl.BlockSpec((tn, tk), lambda j, i, k: (j, k)) if mode == "nt"
              else pl.BlockSpec((tk, tn), lambda j, i, k: (k, j)))
    o_spec = pl.BlockSpec((tm, tn), lambda j, i, k: (i, j))
    has_acc = acc is not None

    nrow = len(row_ins)

    def body(*refs):
        a_ref, b_ref = refs[0], refs[1]
        acc_ref = refs[2] if has_acc else None
        rows = refs[2 + has_acc:2 + has_acc + nrow]
        o_ref = refs[2 + has_acc + nrow + (after is not None)]

        def store(tile):
            if epilogue is not None:
                tile = epilogue(tile, *[r[...] for r in rows])
            if col_scale is not None:
                tile = tile * jnp.where(pl.program_id(0) * tn < col_scale[0], col_scale[1], 1.0)
            o_ref[...] = tile.astype(out_dtype)

        part = _dot(a_ref[...].astype(BF16), b_ref[...].astype(BF16), ca, cb)
        if nk == 1:
            store(part + acc_ref[...] if has_acc else part)
        else:
            sc = refs[-1]
            k = pl.program_id(2)

            @pl.when(k == 0)
            def _():
                sc[...] = part + acc_ref[...] if has_acc else part

            @pl.when(k > 0)
            def _():
                sc[...] += part

            @pl.when(k == nk - 1)
            def _():
                store(sc[...])

    ins = [a, b] + ([acc] if has_acc else []) + list(row_ins)
    in_specs = ([a_spec, b_spec] + ([o_spec] if has_acc else [])
                + [pl.BlockSpec((tm, r.shape[1]), lambda j, i, k: (i, 0)) for r in row_ins])
    if after is not None:
        ins.append(after)
        in_specs.append(pl.BlockSpec(after.shape, lambda j, i, k: (0,) * after.ndim))
    return pl.pallas_call(
        body, name=name, grid=(N // tn, M // tm, nk), in_specs=in_specs, out_specs=o_spec,
        out_shape=jax.ShapeDtypeStruct((M, N), out_dtype),
        scratch_shapes=[pltpu.VMEM((tm, tn), F32)] if nk > 1 else [],
        compiler_params=_cp(("parallel", "parallel", "arbitrary"), VMEM_BIG))(*ins)


def _mm_sum_nt(pairs, *, name, after=None):
    n = len(pairs)
    M, N = pairs[0][0].shape[0], pairs[0][1].shape[0]
    tm = _tile(M, (272,))

    def body(*refs):
        o_ref = refs[2 * n + (after is not None)]
        tot = _dot(refs[0][...].astype(BF16), refs[n][...].astype(BF16), 1, 1)
        for i in range(1, n):
            tot = tot + _dot(refs[i][...].astype(BF16), refs[n + i][...].astype(BF16), 1, 1)
        o_ref[...] = tot

    ins = [a for a, _ in pairs] + [b for _, b in pairs]
    in_specs = ([pl.BlockSpec((tm, a.shape[1]), lambda i: (i, 0)) for a, _ in pairs]
                + [pl.BlockSpec(b.shape, lambda i: (0, 0)) for _, b in pairs])
    if after is not None:
        ins.append(after)
        in_specs.append(pl.BlockSpec(after.shape, lambda i: (0,) * after.ndim))
    return pl.pallas_call(
        body, name=name, grid=(M // tm,), in_specs=in_specs, out_specs=pl.BlockSpec((tm, N), lambda i: (i, 0)),
        out_shape=jax.ShapeDtypeStruct((M, N), F32), compiler_params=_cp(("parallel",), VMEM_BIG))(*ins)


def _row(w):
    return pl.BlockSpec((BLK, w), lambda i: (i, 0))


def _rowc(w, c):
    return pl.BlockSpec((BLK, w), lambda i: (i, c))


def _full(shape):
    return pl.BlockSpec(shape, lambda i: tuple(0 for _ in shape))


def _rope(x, c, s):
    lane = lax.broadcasted_iota(jnp.int32, x.shape, 1)
    is_x1 = ((lane >> 4) & 1) == 0
    partner = jnp.where(is_x1, pltpu.roll(x, LANES - 16, 1), pltpu.roll(x, 16, 1))
    return x * c + partner * s


def _row_valid(i):
    rows = i * BLK + lax.broadcasted_iota(jnp.int32, (BLK, 1), 0)
    return (rows < N_META) | (rows >= PAD)


def _shift_rows(w):
    return pl.BlockSpec((BLK, w), lambda i: (jnp.maximum(i - 1, 0), 0))


def _h_block(i, x_ref, meta_ref):
    head = jnp.concatenate([meta_ref[...], jnp.zeros((BLK - N_META, D_MODEL), F32)], axis=0)
    return jnp.where(i == 0, head, x_ref[...])


def _rms_pre(x2, meta, g):
    lp = PAD + x2.shape[0]

    def body(x_ref, meta_ref, g_ref, u_ref):
        hv = _h_block(pl.program_id(0), x_ref, meta_ref)
        r = lax.rsqrt(jnp.mean(hv * hv, axis=-1, keepdims=True) + RMS_EPS)
        u_ref[...] = (hv * r * g_ref[...]).astype(BF16)

    return pl.pallas_call(
        body, name="rms_pre", grid=(lp // BLK,),
        in_specs=[_shift_rows(D_MODEL), _full((N_META, D_MODEL)), _full((1, D_MODEL))], out_specs=_row(D_MODEL),
        out_shape=jax.ShapeDtypeStruct((lp, D_MODEL), BF16),
        compiler_params=_cp(("parallel",)))(x2, meta, g)


def _split3(x):
    hi = x.astype(BF16)
    r1 = x - hi.astype(F32)
    mid = r1.astype(BF16)
    lo = (r1 - mid.astype(F32)).astype(BF16)
    return hi, mid, lo


def _small_prep(small, gq, gkv, fb, ctab, stab, tri):
    lp = small.shape[0]

    def body(sm_ref, gq_ref, gkv_ref, fb_ref, c_ref, s_ref, tri_ref, qn_ref, kvn_ref, kr_ref, kb_ref, carry):
        i = pl.program_id(0)

        @pl.when(i == 0)
        def _():
            carry[...] = jnp.zeros_like(carry)

        cq = sm_ref[:, 0:256]
        r = lax.rsqrt(jnp.mean(cq * cq, axis=-1, keepdims=True) + RMS_EPS)
        qn_ref[...] = (cq * r * gq_ref[...]).astype(BF16)
        ckv = sm_ref[:, 256:384]
        r = lax.rsqrt(jnp.mean(ckv * ckv, axis=-1, keepdims=True) + RMS_EPS)
        kvn_ref[...] = (ckv * r * gkv_ref[...]).astype(BF16)
        kr_ref[...] = _rope(sm_ref[:, 384:512], c_ref[...], s_ref[...]).astype(BF16)
        fl = sm_ref[:, 512:640] + fb_ref[...]
        lf = jnp.minimum(fl, 0.0) - jnp.log(1.0 + jnp.exp(-jnp.abs(fl)))
        lf = jnp.where(_row_valid(i), lf, 0.0)
        hi, mid, lo = _split3(lf)
        t = tri_ref[...]
        cum = (_dot(t, hi, 1, 0) + _dot(t, mid, 1, 0)) + _dot(t, lo, 1, 0) + carry[...]
        carry[...] = cum[BLK - 1:BLK, :]
        src = lax.broadcasted_iota(jnp.int32, (LANES, LANES), 0)
        dst = lax.broadcasted_iota(jnp.int32, (LANES, LANES), 1)
        kb = jnp.zeros((BLK, LANES), F32)
        for j, part in enumerate(_split3(-cum * LOG2E)):
            spread = ((dst == BIAS_PARTS * src + j) & (src < HEADS)).astype(BF16)
            kb = kb + _dot(part, spread, 1, 0)
        kb_ref[...] = kb.astype(BF16)

    return pl.pallas_call(
        body, name="small_prep", grid=(lp // BLK,),
        in_specs=[_row(SMALL_W), _full((1, 256)), _full((1, 128)), _full((1, 128)), _row(128), _row(128),
                  _full((BLK, BLK))],
        out_specs=[_row(256), _row(128), _row(128), _row(128)],
        out_shape=[jax.ShapeDtypeStruct((lp, 256), BF16), jax.ShapeDtypeStruct((lp, 128), BF16),
                   jax.ShapeDtypeStruct((lp, 128), BF16), jax.ShapeDtypeStruct((lp, 128), BF16)],
        scratch_shapes=[pltpu.VMEM((1, 128), F32)],
        compiler_params=_cp(("arbitrary",)))(small, gq, gkv, fb, ctab, stab, tri)


def _rope_pairs(tile, c, s):
    out = []
    for lo in range(0, tile.shape[1], 256):
        out += [tile[:, lo:lo + 128], _rope(tile[:, lo + 128:lo + 256], c, s)]
    return jnp.concatenate(out, axis=1)


def _gate_fwd(o_mla, o_fox, gate):
    lp = o_mla.shape[0]

    def body(om_ref, of_ref, zm_ref, zf_ref, am_ref, af_ref):
        zm = zm_ref[...].astype(F32)
        am_ref[...] = (om_ref[...] * (zm * _sigmoid(zm))).astype(BF16)
        zf = zf_ref[...].astype(F32)
        af_ref[...] = (of_ref[...] * (zf * _sigmoid(zf))).astype(BF16)

    return pl.pallas_call(
        body, name="gate_fwd", grid=(lp // BLK,),
        in_specs=[_row(D_MODEL), _row(D_MODEL), _rowc(D_MODEL, 0), _rowc(D_MODEL, 1)],
        out_specs=[_row(D_MODEL), _row(D_MODEL)],
        out_shape=[jax.ShapeDtypeStruct((lp, D_MODEL), BF16)] * 2,
        compiler_params=_cp(("parallel",)))(o_mla, o_fox, gate, gate)


def _merge_fwd(gate, y_mla, y_fox):
    lp = y_mla.shape[0]

    def body(ga_ref, gb_ref, ym_ref, yf_ref, m_ref):
        sa = _sigmoid(ga_ref[...].astype(F32))
        sb = _sigmoid(gb_ref[...].astype(F32))
        m_ref[...] = (sa * ym_ref[...] + sb * yf_ref[...]).astype(BF16)

    return pl.pallas_call(
        body, name="merge_fwd", grid=(lp // BLK,),
        in_specs=[_rowc(D_MODEL, 2), _rowc(D_MODEL, 3), _row(D_MODEL), _row(D_MODEL)],
        out_specs=_row(D_MODEL), out_shape=jax.ShapeDtypeStruct((lp, D_MODEL), BF16),
        compiler_params=_cp(("parallel",)))(gate, gate, y_mla, y_fox)


def _tail(x2, mixed, tgt, gpost):
    lp = mixed.shape[0]
    shift = _shift_rows(D_MODEL)

    def body(h_ref, mx_ref, t_ref, g_ref, dmx_ref, dy_ref, loss_ref, dg_ref):
        i = pl.program_id(0)

        @pl.when(i == 0)
        def _():
            loss_ref[...] = jnp.zeros_like(loss_ref)
            dg_ref[...] = jnp.zeros_like(dg_ref)
            dmx_ref[...] = jnp.zeros_like(dmx_ref)
            dy_ref[...] = jnp.zeros_like(dy_ref)

        @pl.when(i > 0)
        def _():
            mx = mx_ref[...]
            g = g_ref[...]
            r = lax.rsqrt(jnp.mean(mx * mx, axis=-1, keepdims=True) + RMS_EPS)
            nrm = mx * r
            e = (h_ref[...] + nrm * g) - t_ref[...]
            loss_ref[...] += jnp.sum(0.5 * jnp.sum(e * e, axis=-1, keepdims=True) * (1.0 / D_MODEL),
                                     axis=0, keepdims=True)
            dy = e * (1.0 / D_MODEL)
            dy_ref[...] = dy
            dg_ref[...] += jnp.sum(dy * nrm, axis=0, keepdims=True)
            w = dy * g
            dot = jnp.mean(w * mx, axis=-1, keepdims=True)
            dmx_ref[...] = (r * w - mx * (r * r * r * dot)).astype(BF16)

    return pl.pallas_call(
        body, name="tail", grid=(lp // BLK,),
        in_specs=[shift, _row(D_MODEL), shift, _full((1, D_MODEL))],
        out_specs=[_row(D_MODEL), _row(D_MODEL), _full((1, 1)), _full((1, D_MODEL))],
        out_shape=[jax.ShapeDtypeStruct((lp, D_MODEL), BF16), jax.ShapeDtypeStruct((lp, D_MODEL), F32),
                   jax.ShapeDtypeStruct((1, 1), F32), jax.ShapeDtypeStruct((1, D_MODEL), F32)],
        compiler_params=_cp(("arbitrary",)))(x2, mixed, tgt, gpost)


def _merge_bwd(dm, gate, y_mla, y_fox):
    lp = dm.shape[0]

    def body(dm_ref, ga_ref, gb_ref, ym_ref, yf_ref, dym_ref, dyf_ref, dg_ref):
        dm_v = dm_ref[...].astype(F32)
        sa = _sigmoid(ga_ref[...].astype(F32))
        sb = _sigmoid(gb_ref[...].astype(F32))
        dym_ref[...] = (dm_v * sa).astype(BF16)
        dyf_ref[...] = (dm_v * sb).astype(BF16)
        dg_ref[:, 0:D_MODEL] = (dm_v * ym_ref[...] * (sa * (1.0 - sa))).astype(BF16)
        dg_ref[:, D_MODEL:2 * D_MODEL] = (dm_v * yf_ref[...] * (sb * (1.0 - sb))).astype(BF16)

    return pl.pallas_call(
        body, name="merge_bwd", grid=(lp // BLK,),
        in_specs=[_row(D_MODEL), _rowc(D_MODEL, 2), _rowc(D_MODEL, 3), _row(D_MODEL), _row(D_MODEL)],
        out_specs=[_row(D_MODEL), _row(D_MODEL), _row(2 * D_MODEL)],
        out_shape=[jax.ShapeDtypeStruct((lp, D_MODEL), BF16), jax.ShapeDtypeStruct((lp, D_MODEL), BF16),
                   jax.ShapeDtypeStruct((lp, 2 * D_MODEL), BF16)],
        compiler_params=_cp(("parallel",)))(dm, gate, gate, y_mla, y_fox)


def _gate_bwd(da_mla, da_fox, o_mla, o_fox, gate):
    lp = da_mla.shape[0]

    def one(da, o, z, head_of_col):
        sg = _sigmoid(z)
        do = (da * (z * sg)).astype(BF16)
        dz = da * o * (sg * (1.0 + z * (1.0 - sg)))
        delta = sum(_dot(part, head_of_col, 1, 0) for part in _split3(do.astype(F32) * o))
        return do, dz.astype(BF16), delta

    def body(dam_ref, daf_ref, om_ref, of_ref, zm_ref, zf_ref, dom_ref, dof_ref, dz_ref, dlm_ref, dlf_ref):
        f32 = lambda r: r[...].astype(F32)
        head_of_col = (lax.broadcasted_iota(jnp.int32, (D_MODEL, LANES), 0) // HEAD_DIM
                       == lax.broadcasted_iota(jnp.int32, (D_MODEL, LANES), 1)).astype(BF16)
        dom_ref[...], dz_ref[:, 0:D_MODEL], dlm_ref[...] = one(f32(dam_ref), f32(om_ref), f32(zm_ref), head_of_col)
        dof_ref[...], dz_ref[:, D_MODEL:2 * D_MODEL], dlf_ref[...] = one(f32(daf_ref), f32(of_ref), f32(zf_ref),
                                                                        head_of_col)

    return pl.pallas_call(
        body, name="gate_bwd", grid=(lp // BLK,),
        in_specs=[_row(D_MODEL)] * 4 + [_rowc(D_MODEL, 0), _rowc(D_MODEL, 1)],
        out_specs=[_row(D_MODEL), _row(D_MODEL), _row(2 * D_MODEL), _row(LANES), _row(LANES)],
        out_shape=[jax.ShapeDtypeStruct((lp, D_MODEL), BF16), jax.ShapeDtypeStruct((lp, D_MODEL), BF16),
                   jax.ShapeDtypeStruct((lp, 2 * D_MODEL), BF16), jax.ShapeDtypeStruct((lp, LANES), F32),
                   jax.ShapeDtypeStruct((lp, LANES), F32)],
        compiler_params=_cp(("parallel",)))(da_mla, da_fox, o_mla, o_fox, gate, gate)


def _small_bwd(small, dqn, dkvn, dkr, dcol_t, drow_t, gq, gkv, fb, ctab, stab, triu):
    lp = small.shape[0]
    nb = lp // BLK

    def rrow(w):
        return pl.BlockSpec((BLK, w), lambda i: (nb - 1 - i, 0))

    def body(sm_ref, dqn_ref, dkvn_ref, dkr_ref, dcol_ref, drow_ref, gq_ref, gkv_ref, fb_ref, c_ref, s_ref, tri_ref,
             ds_ref, dgq_ref, dgkv_ref, dfb_ref, carry):
        i = pl.program_id(0)

        @pl.when(i == 0)
        def _():
            carry[...] = jnp.zeros_like(carry)
            dgq_ref[...] = jnp.zeros_like(dgq_ref)
            dgkv_ref[...] = jnp.zeros_like(dgkv_ref)
            dfb_ref[...] = jnp.zeros_like(dfb_ref)

        def norm_bwd(x, dn, g, dg_ref):
            r = lax.rsqrt(jnp.mean(x * x, axis=-1, keepdims=True) + RMS_EPS)
            dg_ref[...] += jnp.sum(dn * (x * r), axis=0, keepdims=True)
            w = dn * g
            dot = jnp.mean(w * x, axis=-1, keepdims=True)
            return r * w - x * (r * r * r * dot)

        ds_ref[:, 0:256] = norm_bwd(sm_ref[:, 0:256], dqn_ref[...], gq_ref[...], dgq_ref).astype(BF16)
        ds_ref[:, 256:384] = norm_bwd(sm_ref[:, 256:384], dkvn_ref[...], gkv_ref[...], dgkv_ref).astype(BF16)

        dk = dkr_ref[0]
        for p in range(1, PAIRS):
            dk = dk + dkr_ref[p]
        dk = _rope(dk, c_ref[...], -s_ref[...])
        lane = lax.broadcasted_iota(jnp.int32, dk.shape, 1)
        dk = jnp.where(lane < MLA_ROPE, dk + pltpu.roll(dk, LANES - MLA_ROPE, 1), 0.0)
        ds_ref[:, 384:512] = dk.astype(BF16)

        dcol = dcol_ref[0]
        for p in range(1, PAIRS):
            dcol = dcol + pltpu.roll(dcol_ref[p], 2 * p, 1)
        rows16 = jnp.concatenate([drow_ref[p, h:h + 1, :] for p in range(PAIRS) for h in range(2)], axis=0)
        eye = (lax.broadcasted_iota(jnp.int32, (HEADS, LANES), 0)
               == lax.broadcasted_iota(jnp.int32, (HEADS, LANES), 1)).astype(BF16)
        drow = sum(_dot(part, eye, 0, 0) for part in _split3(rows16))
        dcr = dcol - drow
        hi, mid, lo = _split3(dcr)
        t = tri_ref[...]
        suf = (_dot(t, hi, 1, 0) + _dot(t, mid, 1, 0)) + _dot(t, lo, 1, 0) + carry[...]
        fl = sm_ref[:, 512:640] + fb_ref[...]
        dfl = jnp.where(_row_valid(nb - 1 - i), -suf * _sigmoid(-fl), 0.0)
        ds_ref[:, 512:640] = dfl.astype(BF16)
        dfb_ref[...] += jnp.sum(dfl, axis=0, keepdims=True)
        carry[...] += jnp.sum(dcr, axis=0, keepdims=True)

    return pl.pallas_call(
        body, name="small_bwd", grid=(nb,),
        in_specs=[rrow(SMALL_W), rrow(256), rrow(128),
                  pl.BlockSpec((PAIRS, BLK, 128), lambda i: (0, nb - 1 - i, 0)),
                  pl.BlockSpec((PAIRS, BLK, 128), lambda i: (0, nb - 1 - i, 0)),
                  pl.BlockSpec((PAIRS, 2, BLK), lambda i: (0, 0, nb - 1 - i)),
                  _full((1, 256)), _full((1, 128)), _full((1, 128)), rrow(128), rrow(128), _full((BLK, BLK))],
        out_specs=[rrow(SMALL_W), _full((1, 256)), _full((1, 128)), _full((1, 128))],
        out_shape=[jax.ShapeDtypeStruct((lp, SMALL_W), BF16), jax.ShapeDtypeStruct((1, 256), F32),
                   jax.ShapeDtypeStruct((1, 128), F32), jax.ShapeDtypeStruct((1, 128), F32)],
        scratch_shapes=[pltpu.VMEM((1, 128), F32)],
        compiler_params=_cp(("arbitrary",)))(small, dqn, dkvn, dkr, dcol_t, drow_t, gq, gkv, fb, ctab, stab, triu)


def _pre_bwd(du, x2, meta, dy, gpre):
    s_rows = x2.shape[0]
    lp = PAD + s_rows
    shift = _shift_rows(D_MODEL)

    def body(du_ref, x_ref, meta_ref, dy_ref, g_ref, dx_ref, dmeta_ref, dg_ref):
        i = pl.program_id(0)

        @pl.when(i == 0)
        def _():
            dg_ref[...] = jnp.zeros_like(dg_ref)

        hv = _h_block(i, x_ref, meta_ref)
        duv = du_ref[...]
        r = lax.rsqrt(jnp.mean(hv * hv, axis=-1, keepdims=True) + RMS_EPS)
        dg_ref[...] += jnp.sum(duv * (hv * r), axis=0, keepdims=True)
        w = duv * g_ref[...]
        dot = jnp.mean(w * hv, axis=-1, keepdims=True)
        dh = dy_ref[...] + (r * w - hv * (r * r * r * dot))
        dx_ref[...] = dh

        @pl.when(i == 0)
        def _():
            dmeta_ref[...] = dh[0:N_META, :]

    return pl.pallas_call(
        body, name="pre_bwd", grid=(lp // BLK,),
        in_specs=[_row(D_MODEL), shift, _full((N_META, D_MODEL)), _row(D_MODEL), _full((1, D_MODEL))],
        out_specs=[shift, _full((N_META, D_MODEL)), _full((1, D_MODEL))],
        out_shape=[jax.ShapeDtypeStruct((s_rows, D_MODEL), F32), jax.ShapeDtypeStruct((N_META, D_MODEL), F32),
                   jax.ShapeDtypeStruct((1, D_MODEL), F32)],
        compiler_params=_cp(("arbitrary",)))(du, x2, meta, dy, gpre)


def _pair_masks(rope, pair):
    lane = lax.broadcasted_iota(jnp.int32, (1, LANES), 1)
    mas = [lane < HEAD_DIM, lane >= HEAD_DIM]
    wide = lax.broadcasted_iota(jnp.int32, (1, 2 * LANES), 1)
    extra = MLA_ROPE if rope else BIAS_PARTS
    lo = LANES if rope else LANES + 2 * BIAS_PARTS * pair
    mid = lo + extra
    return mas, [(wide < HEAD_DIM) | ((wide >= lo) & (wide < mid)),
                 ((wide >= HEAD_DIM) & (wide < LANES)) | ((wide >= mid) & (wide < mid + extra))]


def _mask2(x, masks):
    return [jnp.where(m, x, jnp.zeros_like(x)) for m in masks]


def _q_heads(q_rows, rope, mas, hmask):
    if rope:
        return _mask2(q_rows, hmask)
    zero = jnp.zeros((q_rows.shape[0], LANES), BF16)
    return [jnp.concatenate([jnp.where(m, q_rows, zero), jnp.where(hm[:, LANES:], zero + 1, zero)], axis=1)
            for m, hm in zip(mas, hmask)]


def _transposed_cols(x, group, name):
    lp = x.shape[0]

    def body(x_ref, o_ref):
        o_ref[...] = x_ref[...].T

    cols = 2 * BLK
    per_group = D_MODEL // cols
    return pl.pallas_call(
        body, name=name, grid=(per_group,),
        in_specs=[pl.BlockSpec((lp, cols), lambda i: (0, per_group * group + i))],
        out_specs=pl.BlockSpec((cols, lp), lambda i: (i, 0)),
        out_shape=jax.ShapeDtypeStruct((D_MODEL, lp), x.dtype),
        compiler_params=_cp(("parallel",), MM_VMEM_BUDGET))(x)


def _attn_fwd(q, k, v, vt, k2, *, rope, qcol, kcol, vcol, name):
    lp = q.shape[0]
    nq = 1 + (lp - PAD) // QB
    qw = 256 if rope else 128

    def body(q_ref, k_ref, v_ref, vt_ref, k2_ref, o_ref, lse_ref):
        i = pl.program_id(1)
        r0 = pl.multiple_of(jnp.where(i == 0, 0, PAD + QB * (i - 1)), BLK)
        b0 = r0 // BLK
        mas, hmask = _pair_masks(rope, pl.program_id(0))
        qh = _q_heads(q_ref[pl.ds(r0, QB), :], rope, mas, hmask)

        def update(chunks, tiles, groups):
            m = [[cr[0], cr[2]] for _, _, cr in groups]
            l = [[cr[1], cr[3]] for _, _, cr in groups]
            acc = [[cr[4][0:HEAD_DIM], cr[4][HEAD_DIM:LANES]] for _, _, cr in groups]
            qs = [[x[q_lo:q_lo + wq] for x in qh] for q_lo, wq, _ in groups]
            k0s = [pl.multiple_of(kc * BLK, BLK) for kc, _ in chunks]
            kks = [jnp.concatenate([k_ref[pl.ds(k0, n), :], k2_ref[pl.ds(k0, n), :]], axis=1)
                   for k0, (_, n) in zip(k0s, chunks)]
            jobs = [(g, ci, mask, h) for g, ci, mask in tiles for h in range(2)]
            score = lambda t: _dot(kks[jobs[t][1]], qs[jobs[t][0]][jobs[t][3]], 1, 1)
            ss = [score(t) for t in range(min(AHEAD, len(jobs)))]
            for t, (g, ci, mask, h) in enumerate(jobs):
                if t + AHEAD < len(jobs):
                    ss.append(score(t + AHEAD))
                s = ss[t] if mask is None else jnp.where(mask, ss[t], NEG)
                m_new = jnp.maximum(m[g][h], jnp.max(s, axis=0, keepdims=True))
                alpha = jnp.exp2(m[g][h] - m_new)
                p = jnp.exp2(s - m_new)
                l[g][h] = alpha * l[g][h] + jnp.sum(p, axis=0, keepdims=True)
                m[g][h] = m_new
                n = chunks[ci][1]
                if n == BLK:
                    pv = _dot(vt_ref[pl.ds(HEAD_DIM * h, HEAD_DIM), pl.ds(k0s[ci], BLK)], p.astype(BF16), 1, 0)
                else:
                    vm = jnp.where(mas[h], v_ref[0:n, :], jnp.zeros((), BF16))
                    pv = _dot(vm, p.astype(BF16), 0, 0)[HEAD_DIM * h:HEAD_DIM * (h + 1)]
                acc[g][h] = alpha * acc[g][h] + pv
            return [(m[g][0], l[g][0], m[g][1], l[g][1], jnp.concatenate(acc[g], axis=0)) for g in range(len(groups))]

        def full_chunks(kcs, carry):
            return update([(kc, BLK) for kc in kcs], [(0, ci, None) for ci in range(len(kcs))], [(0, QB, carry)])[0]

        neg = jnp.full((1, QB), NEG, F32)
        zero = jnp.zeros((1, QB), F32)
        c = (neg, zero, neg, zero, jnp.zeros((LANES, QB), F32))
        n_mid = jnp.maximum(b0 - 1, 0)
        done = 0
        for width in (8, 4, 2):
            trips = (n_mid - done) // width
            c = lax.fori_loop(0, trips, lambda t, cr, first=done + 1, width=width:
                              full_chunks([first + width * t + u for u in range(width)], cr), c)
            done = done + width * trips
        key_l = lax.broadcasted_iota(jnp.int32, (BLK, BLK), 0)
        qry_l = lax.broadcasted_iota(jnp.int32, (BLK, BLK), 1)
        tri = (key_l <= qry_l) & (b0 > 0)
        meta_ok = (key_l[0:N_META] <= qry_l[0:N_META]) | (b0 > 0)
        lo, hi = update([(0, N_META), (b0, BLK), (b0 + 1, BLK)],
                        [(0, 0, meta_ok), (1, 0, None), (0, 1, tri), (1, 1, None), (1, 2, tri)],
                        [(0, BLK, tuple(a[:, 0:BLK] for a in c)), (BLK, QB - BLK, tuple(a[:, BLK:QB] for a in c))])
        c = tuple(jnp.concatenate([a, b], axis=1) for a, b in zip(lo, hi))
        inv =jnp.concatenate([jnp.broadcast_to(1.0 / c[1], (HEAD_DIM, QB)),
                               jnp.broadcast_to(1.0 / c[3], (HEAD_DIM, QB))], axis=0)
        o_t = (c[4] * inv).T.astype(BF16)
        lses = [c[2 * h] + jnp.log(c[2 * h + 1]) * LOG2E for h in range(2)]
        o_ref[pl.ds(r0, BLK), :] = o_t[0:BLK]
        for h in range(2):
            lse_ref[0, h:h + 1, pl.ds(r0, BLK)] = lses[h][:, 0:BLK]

        @pl.when(i > 0)
        def _():
            r1 = pl.multiple_of(r0 + BLK, BLK)
            o_ref[pl.ds(r1, QB - BLK), :] = o_t[BLK:QB]
            for h in range(2):
                lse_ref[0, h:h + 1, pl.ds(r1, QB - BLK)] = lses[h][:, BLK:QB]

    in_specs = [pl.BlockSpec((lp, qw), lambda p, i: (0, qcol + p)),
                pl.BlockSpec((lp, 128), lambda p, i: (0, kcol(p))),
                pl.BlockSpec((BLK, 128), lambda p, i: (0, vcol(p))),
                pl.BlockSpec((128, lp), lambda p, i: (p, 0)),
                pl.BlockSpec((lp, 128), lambda p, i: (0, 0))]
    return pl.pallas_call(
        body, name=name, grid=(PAIRS, nq), in_specs=in_specs,
        out_specs=[pl.BlockSpec((lp, 128), lambda p, i: (0, p)),
                   pl.BlockSpec((1, 2, lp), lambda p, i: (p, 0, 0))],
        out_shape=[jax.ShapeDtypeStruct((lp, D_MODEL), BF16), jax.ShapeDtypeStruct((PAIRS, 2, lp), F32)],
        compiler_params=_cp(("parallel", "arbitrary"), VMEM_BIG))(q, k, v, vt, k2)


def _attn_bwd(q, k, v, k2, do, delta, lse, *, rtabs=None, scale, qcol, kcol, vcol, name):
    lp = q.shape[0]
    nb = lp // BLK
    rope = rtabs is not None
    bias = not rope
    qw = 256 if rope else 128

    def body(*refs):
        it = iter(refs)
        q_ref, k_ref, v_ref, k2_ref = next(it), next(it), next(it), next(it)
        do_ref, dl_ref, lse_ref = next(it), next(it), next(it)
        ct_ref, st_ref = (next(it), next(it)) if rope else (None, None)
        dq_out, dk_ref, dv_ref = next(it), next(it), next(it)
        x_ref = next(it)
        drow_ref = next(it) if bias else None
        dq_ref = next(it)
        kb = pl.program_id(1)
        mas, hmask = _pair_masks(rope, pl.program_id(0))
        lane = lax.broadcasted_iota(jnp.int32, (1, LANES), 1)

        @pl.when(kb == 0)
        def _():
            dq_ref[...] = jnp.zeros_like(dq_ref)
            if bias:
                drow_ref[...] = jnp.zeros_like(drow_ref)

        def key_pass(n, w):
            kk = jnp.concatenate([k_ref[0:n, :], k2_ref[0:n, :]], axis=1)
            vh = _mask2(v_ref[0:n, :], mas)
            kcat = jnp.concatenate([x[:, 0:qw] for x in _mask2(kk, hmask)], axis=0)
            diag_mask = (lax.broadcasted_iota(jnp.int32, (n, w), 0) <= lax.broadcasted_iota(jnp.int32, (n, w), 1))

            def front(qc):
                q0 = qc * w if isinstance(qc, int) else pl.multiple_of(qc * w, w)
                dov = do_ref[pl.ds(q0, w), :]
                qh = _q_heads(q_ref[pl.ds(q0, w), :], rope, mas, hmask)
                ss, dps = [], []
                for h in range(2):
                    ss.append(_dot(kk, qh[h], 1, 1))
                    dps.append(_dot(vh[h], dov, 1, 1))
                return q0, dov, qh, ss, dps

            def back(fronted, carry, mask):
                carry = list(carry)
                q0, dov, qh, ss, dps = fronted
                doh = _mask2(dov, mas)
                pbs, dss = [], []
                for h in range(2):
                    p = jnp.exp2(ss[h] - lse_ref[0, h:h + 1, pl.ds(q0, w)])
                    if mask is not None:
                        p = jnp.where(mask, p, 0.0)
                    ds = p * (dps[h] - dl_ref[0, h:h + 1, pl.ds(q0, w)])
                    if bias:
                        drow_ref[0, h:h + 1, pl.ds(q0, w)] += jnp.sum(ds, axis=0, keepdims=True)
                        carry[2 + h] = carry[2 + h] + jnp.sum(ds, axis=1, keepdims=True)
                    pbs.append(p.astype(BF16))
                    dss.append(ds.astype(BF16))
                ds_lanes = jnp.concatenate(dss, axis=1)
                ds_rows = jnp.concatenate(dss, axis=0)
                qcat = jnp.concatenate([x[:, 0:qw] for x in qh], axis=0)
                carry[0] = carry[0] + _dot(ds_lanes, qcat, 1, 0)
                carry[1] = carry[1] + _dot(jnp.concatenate(pbs, axis=1), jnp.concatenate(doh, axis=0), 1, 0)
                dq_ref[pl.ds(q0, w), :] += _dot(ds_rows, kcat, 0, 0)
                return tuple(carry)

            def chunks(qcs, carry, masks):
                ahead = AHEAD_BWD
                fronted = [front(qc) for qc in qcs[:ahead]]
                for u, mask in enumerate(masks):
                    if u + ahead < len(qcs):
                        fronted.append(front(qcs[u + ahead]))
                    carry = back(fronted[u], carry, mask)
                return carry

            c = [jnp.zeros((n, qw), F32), jnp.zeros((n, LANES), F32)]
            if bias:
                c += [jnp.zeros((n, 1), F32), jnp.zeros((n, 1), F32)]
            c = tuple(c)
            if w != BLK:
                c = chunks(list(range(lp // w)), c, [diag_mask] + [None] * (lp // w - 1))
            else:
                start = kb
                for width in UNROLLS:

                    def several(t, cr, start=start, width=width):
                        qc = start + width * t
                        return chunks([qc + u for u in range(width)], cr, [diag_mask | (qc > kb)] + [None] * (width - 1))

                    trips = (nb - start) // width
                    c = lax.fori_loop(0, trips, several, c)
                    start = start + width * trips

            def rows(a, dtype):
                a = a.astype(dtype)
                return a if n == BLK else jnp.concatenate([a, jnp.zeros((BLK - n, a.shape[1]), dtype)], axis=0)

            dk = c[0] * LN2
            dk_ref[...] = rows(dk[:, 0:LANES], BF16)
            dv_ref[...] = rows(c[1], BF16)
            if rope:
                x_ref[0] = rows(dk[:, LANES:2 * LANES], F32)
            if bias:
                x_ref[0] = rows(jnp.where(lane == 0, c[2], jnp.where(lane == 1, c[3], 0.0)), F32)

        @pl.when(kb == 0)
        def _():
            key_pass(N_META, lp // 2)

        @pl.when(kb > 0)
        def _():
            key_pass(BLK, BLK)

        @pl.when(kb == nb - 1)
        def _():
            def fin(c, carry):
                r0 = pl.multiple_of(c * BLK, BLK)
                dq = dq_ref[pl.ds(r0, BLK), :] * scale
                if rope:
                    back = _rope(dq[:, LANES:2 * LANES], ct_ref[pl.ds(r0, BLK), :], -st_ref[pl.ds(r0, BLK), :])
                    dq = jnp.concatenate([dq[:, 0:LANES], back], axis=1)
                dq_out[pl.ds(r0, BLK), :] = dq.astype(BF16)
                return carry

            lax.fori_loop(0, nb, fin, 0)

    in_specs = [pl.BlockSpec((lp, qw), lambda p, j: (0, qcol + p)),
                pl.BlockSpec((BLK, 128), lambda p, j: (j, kcol(p))),
                pl.BlockSpec((BLK, 128), lambda p, j: (j, vcol(p))),
                pl.BlockSpec((BLK, 128), lambda p, j: (j, 0)),
                pl.BlockSpec((lp, 128), lambda p, j: (0, p)), pl.BlockSpec((1, 2, lp), lambda p, j: (p, 0, 0)),
                pl.BlockSpec((1, 2, lp), lambda p, j: (p, 0, 0))]
    ins = [q, k, v, k2, do, delta, lse]
    if rope:
        in_specs += [pl.BlockSpec((lp, 128), lambda p, j: (0, 0))] * 2
        ins += list(rtabs)
    out_specs = [pl.BlockSpec((lp, qw), lambda p, j: (0, p)),
                 pl.BlockSpec((BLK, 128), lambda p, j: (j, p)),
                 pl.BlockSpec((BLK, 128), lambda p, j: (j, p)),
                 pl.BlockSpec((1, BLK, 128), lambda p, j: (p, j, 0))]
    out_shape = [jax.ShapeDtypeStruct((lp, PAIRS * qw), BF16), jax.ShapeDtypeStruct((lp, D_MODEL), BF16),
                 jax.ShapeDtypeStruct((lp, D_MODEL), BF16), jax.ShapeDtypeStruct((PAIRS, lp, 128), F32)]
    if bias:
        out_specs.append(pl.BlockSpec((1, 2, lp), lambda p, j: (p, 0, 0)))
        out_shape.append(jax.ShapeDtypeStruct((PAIRS, 2, lp), F32))
    return pl.pallas_call(
        body, name=name, grid=(PAIRS, nb), in_specs=in_specs, out_specs=out_specs, out_shape=out_shape,
        scratch_shapes=[pltpu.VMEM((lp, qw), F32)],
        compiler_params=_cp(("parallel", "arbitrary"), VMEM_BIG))(*ins)


def _adamw(w, g, m, v, name):
    lead = w.ndim - 2
    rows, cols = w.shape[lead:]
    big = rows * cols > 512 * 1024
    tr = 128 if big and rows % 128 == 0 else rows
    tc = 256 if big and tr == rows else cols

    def body(w_ref, g_ref, m_ref, v_ref, d_ref, nm_ref, nv_ref):
        gv = g_ref[...]
        nm = ADAM_B1 * m_ref[...] + (1.0 - ADAM_B1) * gv
        nv = ADAM_B2 * v_ref[...] + (1.0 - ADAM_B2) * (gv * gv)
        m_hat = nm / (1.0 - ADAM_B1 ** ADAM_STEP)
        v_hat = nv / (1.0 - ADAM_B2 ** ADAM_STEP)
        d_ref[...] = -ADAM_LR * (m_hat / (jnp.sqrt(v_hat) + ADAM_EPS) + ADAM_WD * w_ref[...])
        nm_ref[...] = nm
        nv_ref[...] = nv

    spec = pl.BlockSpec((1,) * lead + (tr, tc), lambda i, j: (0,) * lead + (i, j))
    return pl.pallas_call(
        body, name=name, grid=(rows // tr, cols // tc), in_specs=[spec] * 4, out_specs=[spec] * 3,
        out_shape=[jax.ShapeDtypeStruct(w.shape, F32)] * 3,
        compiler_params=_cp(("parallel", "parallel"), VMEM_BIG))(w, g, m, v)


def _add_cores(g, from_sib, name):
    n, rows, cols = g.shape
    half = rows // 2
    tr = _tile(half, (256, 240))
    nt = half // tr

    def body(lo_ref, hi_ref, s_ref, o_ref):
        mine = jnp.where(lax.axis_index("c") == 0, lo_ref[0], hi_ref[0])
        o_ref[0] = (mine.astype(F32) + s_ref[0].astype(F32)).astype(BF16)

    return pl.pallas_call(
        body, name=name, grid=(n, nt),
        in_specs=[pl.BlockSpec((1, tr, cols), lambda j, i: (j, i, 0)),
                  pl.BlockSpec((1, tr, cols), lambda j, i: (j, nt + i, 0)),
                  pl.BlockSpec((1, tr, cols), lambda j, i: (j, i, 0))],
        out_specs=pl.BlockSpec((1, tr, cols), lambda j, i: (j, i, 0)),
        out_shape=jax.ShapeDtypeStruct((n, half, cols), BF16),
        compiler_params=_cp(("parallel", "parallel"), VMEM_BIG))(g, g, from_sib)


def _add_chips(x, own, name):
    n, rows, cols = x.shape
    tr = _tile(rows, (256, 240))

    def body(x_ref, own_ref, o_ref):
        me = 2 * lax.axis_index("x") + lax.axis_index("y")
        v = [jnp.where(me == k, own_ref[...], x_ref[k]).astype(F32) for k in range(N_CHIPS)]
        o_ref[...] = ((v[0] + v[1]) + v[2]) + v[3]

    return pl.pallas_call(
        body, name=name, grid=(rows // tr,),
        in_specs=[pl.BlockSpec((n, tr, cols), lambda i: (0, i, 0)), pl.BlockSpec((tr, cols), lambda i: (i, 0))],
        out_specs=pl.BlockSpec((tr, cols), lambda i: (i, 0)),
        out_shape=jax.ShapeDtypeStruct((rows, cols), F32), compiler_params=_cp(("parallel",), VMEM_BIG))(x, own)


def _axes():
    return lax.axis_index("x"), lax.axis_index("y"), lax.axis_index("c")


def _other_chips(x, y):
    return [(1 - x, y), (x, 1 - y), (1 - x, 1 - y)]


ANY = pl.BlockSpec(memory_space=pl.ANY)


def _rcopy(src, dst, send_sems, recv_sems, k, to):
    return pltpu.make_async_remote_copy(src_ref=src, dst_ref=dst, send_sem=send_sems.at[k], recv_sem=recv_sems.at[k],
                                        device_id=to, device_id_type=MESH)


def _gather_weights(shards, meta):
    n = len(shards)

    def body(*refs):
        srcs, meta_ref = refs[:n], refs[n]
        outs, mout_ref = refs[n + 1:2 * n + 1], refs[2 * n + 1]
        send_sems, recv_sems = refs[2 * n + 2:]
        x, y, c = _axes()
        me = 2 * x + y
        sib = (x, y, 1 - c)
        chips = _other_chips(x, y)

        def half(t, chip_idx, cc):
            hr = shards[t].shape[0] // 2
            return outs[t].at[chip_idx, pl.ds(cc * hr, hr), :]

        first = []
        for j, (px, py) in enumerate(chips):
            for t in range(n):
                hr = shards[t].shape[0] // 2
                first.append(_rcopy(srcs[t].at[pl.ds(c * hr, hr), :], half(t, me, c), send_sems, recv_sems,
                                    3 * t + j, (px, py, c)))
            first.append(_rcopy(meta_ref, mout_ref.at[me], send_sems, recv_sems, 3 * n + j, (px, py, c)))
        for cp in first:
            cp.start()
        passed = []
        for j, (px, py) in enumerate(chips):
            src_chip = 2 * px + py
            for t in range(n):
                _rcopy(half(t, src_chip, c), half(t, src_chip, c), send_sems, recv_sems, 3 * t + j, sib).wait_recv()
                fwd = _rcopy(half(t, src_chip, c), half(t, src_chip, c), send_sems, recv_sems, 3 * (n + 1 + t) + j, sib)
                fwd.start()
                passed.append(fwd)
            _rcopy(mout_ref.at[src_chip], mout_ref.at[src_chip], send_sems, recv_sems, 3 * n + j, sib).wait_recv()
        for j, (px, py) in enumerate(chips):
            src_chip = 2 * px + py
            for t in range(n):
                _rcopy(half(t, src_chip, 1 - c), half(t, src_chip, 1 - c), send_sems, recv_sems,
                       3 * (n + 1 + t) + j, sib).wait_recv()
        for cp in first + passed:
            cp.wait_send()

    nsem = 3 * (2 * n + 1)
    return pl.pallas_call(
        body, name="gather_weights", in_specs=[ANY] * (n + 1), out_specs=[ANY] * (n + 1),
        out_shape=[jax.ShapeDtypeStruct((N_CHIPS,) + s.shape, s.dtype) for s in shards]
        + [jax.ShapeDtypeStruct((N_CHIPS,) + meta.shape, meta.dtype)],
        scratch_shapes=[pltpu.SemaphoreType.DMA((nsem,)), pltpu.SemaphoreType.DMA((nsem,))])(*shards, meta)


def _gather_late(shard):
    rows, cols = shard.shape
    hr = rows // 2
    src = jax.new_ref(shard, memory_space=pltpu.MemorySpace.HBM)
    out = jax.empty_ref(jax.ShapeDtypeStruct((N_CHIPS, rows, cols), shard.dtype), memory_space=pltpu.MemorySpace.HBM)

    @pl.kernel(mesh=plsc.ScalarSubcoreMesh(axis_name="seq", num_cores=1), name="gather_late",
               scratch_types=(pltpu.SemaphoreType.DMA((6,)), pltpu.SemaphoreType.DMA((6,))),
               compiler_params=pltpu.CompilerParams(collective_id=1))
    def launch(send_sems, recv_sems):
        x, y, c = _axes()
        me = 2 * x + y
        sib = (x, y, 1 - c)
        chips = _other_chips(x, y)
        barrier = pltpu.get_barrier_semaphore()
        for px, py in chips:
            pl.semaphore_signal(barrier, inc=1, device_id=(px, py, c), device_id_type=MESH)
        pl.semaphore_signal(barrier, inc=1, device_id=sib, device_id_type=MESH)
        pl.semaphore_wait(barrier, 4)

        def half(chip_idx, cc):
            return out.at[chip_idx, pl.ds(cc * hr, hr), :]

        first = [_rcopy(src.at[pl.ds(c * hr, hr), :], half(me, c), send_sems, recv_sems, j, (px, py, c))
                 for j, (px, py) in enumerate(chips)]
        for cp in first:
            cp.start()
        passed = []
        for j, (px, py) in enumerate(chips):
            land = half(2 * px + py, c)
            _rcopy(land, land, send_sems, recv_sems, j, sib).wait_recv()
            fwd = _rcopy(land, land, send_sems, recv_sems, 3 + j, sib)
            fwd.start()
            passed.append(fwd)
        for j, (px, py) in enumerate(chips):
            land = half(2 * px + py, 1 - c)
            _rcopy(land, land, send_sems, recv_sems, 3 + j, sib).wait_recv()
        for cp in first + passed:
            cp.wait_send()

    launch()
    return out[...]


def _swap_halves(gs):
    n = len(gs)
    ncopies = sum(g.shape[0] for g in gs)

    def body(*refs):
        srcs, outs = refs[:n], refs[n:2 * n]
        send_sems, recv_sems = refs[2 * n:]
        x, y, c = _axes()
        cps = []
        for t in range(n):
            hr = gs[t].shape[1] // 2
            for j in range(gs[t].shape[0]):
                cps.append(_rcopy(srcs[t].at[j, pl.ds((1 - c) * hr, hr), :], outs[t].at[j], send_sems, recv_sems,
                                  len(cps), (x, y, 1 - c)))
        for cp in cps:
            cp.start()
        for cp in cps:
            cp.wait()

    return pl.pallas_call(
        body, name="swap_halves", in_specs=[ANY] * n, out_specs=[ANY] * n,
        out_shape=[jax.ShapeDtypeStruct((g.shape[0], g.shape[1] // 2, g.shape[2]), g.dtype) for g in gs],
        scratch_shapes=[pltpu.SemaphoreType.DMA((ncopies,)), pltpu.SemaphoreType.DMA((ncopies,))])(*gs)


def _scatter_chips(parts):
    n = len(parts)
    srcs = [jax.new_ref(p, memory_space=pltpu.MemorySpace.HBM) for p in parts]
    outs = [jax.empty_ref(jax.ShapeDtypeStruct(p.shape, p.dtype), memory_space=pltpu.MemorySpace.HBM) for p in parts]

    @pl.kernel(mesh=plsc.ScalarSubcoreMesh(axis_name="seq", num_cores=1), name="scatter_chips",
               scratch_types=(pltpu.SemaphoreType.DMA((3 * n,)), pltpu.SemaphoreType.DMA((3 * n,))),
               compiler_params=pltpu.CompilerParams(collective_id=0))
    def launch(send_sems, recv_sems):
        x, y, c = _axes()
        me = 2 * x + y
        chips = _other_chips(x, y)
        barrier = pltpu.get_barrier_semaphore()
        for px, py in chips:
            pl.semaphore_signal(barrier, inc=1, device_id=(px, py, c), device_id_type=MESH)
        pl.semaphore_wait(barrier, 3)
        cps = []
        for j, (px, py) in enumerate(chips):
            for t in range(n):
                cps.append(_rcopy(srcs[t].at[2 * px + py], outs[t].at[me], send_sems, recv_sems, 3 * t + j,
                                  (px, py, c)))
        for cp in cps:
            cp.start()
        for cp in cps:
            cp.wait()

    launch()
    return [o[...] for o in outs]


def _swap_reduced(rs):
    n = len(rs)

    def body(*refs):
        srcs, outs = refs[:n], refs[n:2 * n]
        send_sems, recv_sems = refs[2 * n:]
        x, y, c = _axes()
        cps = [_rcopy(srcs[t], outs[t], send_sems, recv_sems, t, (x, y, 1 - c)) for t in range(n)]
        for cp in cps:
            cp.start()
        for cp in cps:
            cp.wait()

    return pl.pallas_call(
        body, name="swap_reduced", in_specs=[ANY] * n, out_specs=[ANY] * n,
        out_shape=[jax.ShapeDtypeStruct(r.shape, r.dtype) for r in rs],
        scratch_shapes=[pltpu.SemaphoreType.DMA((n,)), pltpu.SemaphoreType.DMA((n,))])(*rs)


SMALL_ROWS = 24 + 128


def _allreduce_small(vec):
    def body(v_ref, out_ref, slots, send_sems, recv_sems):
        x, y, c = _axes()
        me = 4 * x + 2 * y + c
        slots[me] = v_ref[...]
        cps = []
        for k in range(1, 8):
            kx, ky, kc = (k >> 2) & 1, (k >> 1) & 1, k & 1
            peer = (1 - x if kx else x, 1 - y if ky else y, 1 - c if kc else c)
            cps.append(_rcopy(v_ref, slots.at[me], send_sems, recv_sems, k - 1, peer))
        for cp in cps:
            cp.start()
        for cp in cps:
            cp.wait()
        tot = slots[0]
        for k in range(1, 8):
            tot = tot + slots[k]
        out_ref[...] = tot

    return pl.pallas_call(
        body, name="allreduce_small",
        in_specs=[pl.BlockSpec(memory_space=pltpu.VMEM)], out_specs=pl.BlockSpec(memory_space=pltpu.VMEM),
        out_shape=jax.ShapeDtypeStruct((SMALL_ROWS, 128), F32),
        scratch_shapes=[pltpu.VMEM((8, SMALL_ROWS, 128), F32), pltpu.SemaphoreType.DMA((7,)),
                        pltpu.SemaphoreType.DMA((7,))])(vec)


def _pack_p2(w_uq, w_ukv, w_br_mla, w_br_fox, w_out, dtype):
    parts = [w_uq.reshape(96, D_MODEL), w_ukv.reshape(64, D_MODEL), w_br_mla, w_br_fox, w_out]
    return jnp.concatenate([p.astype(dtype) for p in parts], axis=0)


def _unpack_p2(pk):
    return pk[0:96].reshape(256, 384), pk[96:160].reshape(128, 512), pk[160:416], pk[416:672], pk[672:928]


def _uq_arrange(w):
    w3 = w.reshape(256, HEADS, 96)
    nope = w3[:, :, :64].reshape(256, PAIRS, 128)
    pe = w3[:, :, 64:].reshape(256, PAIRS, 64)
    return jnp.concatenate([nope, pe, jnp.zeros((256, PAIRS, 64), w.dtype)], axis=2).reshape(256, PAIRS * 256)


def _uq_restore(g):
    g3 = g.reshape(256, PAIRS, 256)
    nope = g3[:, :, :128].reshape(256, HEADS, 64)
    pe = g3[:, :, 128:192].reshape(256, HEADS, 32)
    return jnp.concatenate([nope, pe], axis=2).reshape(256, HEADS * 96)


def _ukv_arrange(w):
    w3 = w.reshape(128, HEADS, 128)
    return jnp.concatenate([w3[:, :, :64].reshape(128, 1024), w3[:, :, 64:].reshape(128, 1024)], axis=1)


def _ukv_restore(g):
    kn = g[:, :1024].reshape(128, HEADS, 64)
    vv = g[:, 1024:].reshape(128, HEADS, 64)
    return jnp.concatenate([kn, vv], axis=2).reshape(128, HEADS * 128)


def _rope_tables(lp):
    r = np.arange(lp)
    pos = np.where(r < N_META, r, np.where(r >= PAD, r - PAD + N_META, 0)).astype(np.float32)
    half = MLA_ROPE // 2
    inv_freq = np.float32(ROPE_THETA) ** (-np.arange(half, dtype=np.float32) / np.float32(half))
    ang = (pos[:, None] * inv_freq[None, :]).astype(np.float32)
    cos, sin = np.cos(ang).astype(np.float32), np.sin(ang).astype(np.float32)
    one, zero = np.ones((lp, 64), np.float32), np.zeros((lp, 64), np.float32)
    return (jnp.asarray(np.concatenate([cos, cos, cos, cos, one], axis=1)),
            jnp.asarray(np.concatenate([-sin, sin, -sin, sin, zero], axis=1)))


def _pad_lanes(v, n=128):
    return jnp.pad(v, ((0, 0), (0, n - v.shape[1])))


def _in_cols(slabs, a, b):
    out = []
    for j in range(N_CHIPS):
        lo, hi = max(a, W_IN_SHARD * j), min(b, W_IN_SHARD * (j + 1))
        if lo < hi:
            out.append(slabs[j][:, lo - W_IN_SHARD * j:hi - W_IN_SHARD * j])
    return out


def _local_step(x2, tgt2, meta_f, w_small, w_attn, w_gate, w_uq_f, w_ukv_f, w_bm, w_bf, w_o, pre_norm_g,
                post_norm_g, mla_q_norm_g, mla_kv_norm_g, fox_forget_b, start_exchange=None):
    s_rows = x2.shape[0]
    lp = PAD + s_rows
    w_uq_a = _uq_arrange(w_uq_f)
    w_ukv_a = _ukv_arrange(w_ukv_f)

    ctab, stab = _rope_tables(lp)
    ii = jnp.arange(BLK)
    tri_lo = (ii[:, None] >= ii[None, :]).astype(BF16)
    tri_up = (ii[:, None] <= ii[None, :]).astype(BF16)
    fb128 = _pad_lanes(fox_forget_b)

    u = _rms_pre(x2, meta_f, pre_norm_g)
    small = _mm(u, w_small, mode="nn", out_dtype=F32, name="proj_small")
    attn = _mm(u, w_attn, mode="nn", out_dtype=BF16, name="proj_attn",
               col_scale=(HEADS * HEAD_DIM, FOX_SCALE * LOG2E))
    gate = _mm(u, w_gate, mode="nn", out_dtype=BF16, name="proj_gate")
    qn, kvn, kr, kb = _small_prep(small, mla_q_norm_g, mla_kv_norm_g, fb128, ctab, stab, tri_lo)
    qcat = _mm(qn, w_uq_a, mode="nn", out_dtype=BF16, name="mla_q", row_ins=(ctab, stab),
               epilogue=lambda tile, c, s: _rope_pairs(tile, c, s) * (MLA_SCALE * LOG2E))
    kv = _mm(kvn, w_ukv_a, mode="nn", out_dtype=BF16, name="mla_kv")

    mla_cols = dict(qcol=0, kcol=lambda p: p, vcol=lambda p: PAIRS + p)
    fox_cols = dict(qcol=0, kcol=lambda p: PAIRS + p, vcol=lambda p: 2 * PAIRS + p)
    o_mla, lse_mla = _attn_fwd(qcat, kv, kv, _transposed_cols(kv, 1, "mla_vt"), kr, rope=True, name="mla_fwd",
                               **mla_cols)
    o_fox, lse_fox = _attn_fwd(attn, attn, attn, _transposed_cols(attn, 2, "fox_vt"), kb, rope=False,
                               name="fox_fwd", **fox_cols)

    a_mla, a_fox = _gate_fwd(o_mla, o_fox, gate)
    y_mla = _mm(a_mla, w_bm, mode="nn", out_dtype=BF16, name="br_mla")
    y_fox = _mm(a_fox, w_bf, mode="nn", out_dtype=BF16, name="br_fox")
    mg = _merge_fwd(gate, y_mla, y_fox)
    mixed = _mm(mg, w_o, mode="nn", out_dtype=F32, name="out_proj")
    dmixed, dy, loss_p, dg_post = _tail(x2, mixed, tgt2, post_norm_g)

    d_w_out = _mm(mg, dmixed, mode="tn", out_dtype=F32, name="d_w_out")
    dm = _mm(dmixed, w_o, mode="nt", out_dtype=BF16, name="d_merge")
    dy_mla, dy_fox, dgate_ab = _merge_bwd(dm, gate, y_mla, y_fox)
    d_w_bm = _mm(a_mla, dy_mla, mode="tn", out_dtype=F32, name="d_w_br_mla")
    d_w_bf = _mm(a_fox, dy_fox, mode="tn", out_dtype=F32, name="d_w_br_fox")
    da_mla = _mm(dy_mla, w_bm, mode="nt", out_dtype=BF16, name="d_a_mla")
    da_fox = _mm(dy_fox, w_bf, mode="nt", out_dtype=BF16, name="d_a_fox")
    do_mla, do_fox, dgate_z, dl_mla, dl_fox = _gate_bwd(da_mla, da_fox, o_mla, o_fox, gate)
    dl_mla, dl_fox = (d[:, :HEADS].T.reshape(PAIRS, 2, lp) for d in (dl_mla, dl_fox))

    dq_a, dkn, dvm, dkr = _attn_bwd(qcat, kv, kv, kr, do_mla, dl_mla, lse_mla, rtabs=(ctab, stab),
                                    scale=MLA_SCALE, name="mla_bwd", **mla_cols)
    dfq, dfk, dfv, dcol, drow = _attn_bwd(attn, attn, attn, kb, do_fox, dl_fox, lse_fox, scale=FOX_SCALE,
                                          name="fox_bwd", **fox_cols)

    d_w_uq_a = _mm(qn, dq_a, mode="tn", out_dtype=F32, name="d_w_uq")
    dqn = _mm(dq_a, w_uq_a, mode="nt", out_dtype=F32, name="d_qn")
    d_w_ukv_a = jnp.concatenate([_mm(kvn, dkn, mode="tn", out_dtype=F32, name="d_w_uk"),
                                 _mm(kvn, dvm, mode="tn", out_dtype=F32, name="d_w_uv")], axis=1)
    dkvn = _mm(dkn, w_ukv_a[:, :1024], mode="nt", out_dtype=F32, name="d_kvn_k")
    dkvn = _mm(dvm, w_ukv_a[:, 1024:], mode="nt", out_dtype=F32, name="d_kvn_v", acc=dkvn)
    dsmall, dg_q, dg_kv, dfb = _small_bwd(small, dqn, dkvn, dkr, dcol, drow, mla_q_norm_g, mla_kv_norm_g,
                                          fb128, ctab, stab, tri_up)

    dw_small = _mm(u, dsmall, mode="tn", out_dtype=BF16, name="d_w_small")
    dw_fq = _mm(u, dfq, mode="tn", out_dtype=BF16, name="d_w_fq")
    dw_fk = _mm(u, dfk, mode="tn", out_dtype=BF16, name="d_w_fk")
    dw_fv = _mm(u, dfv, mode="tn", out_dtype=BF16, name="d_w_fv")
    dw_z = _mm(u, dgate_z, mode="tn", out_dtype=BF16, name="d_w_z")
    dw_g = _mm(u, dgate_ab, mode="tn", out_dtype=BF16, name="d_w_g")
    d_w_in = (dw_small, dw_z, dw_fq, dw_fk, dw_fv, dw_g)
    d_w_uq = _uq_restore(d_w_uq_a)
    d_w_ukv = _ukv_restore(d_w_ukv_a)
    token = start_exchange(d_w_in, d_w_uq, d_w_ukv, d_w_bm, d_w_bf, d_w_out) if start_exchange else None
    du = _mm_sum_nt([(dsmall, w_small), (dfq, w_attn[:, 0:1024]), (dfk, w_attn[:, 1024:2048]),
                     (dfv, w_attn[:, 2048:3072]), (dgate_z, w_gate[:, 0:2048]), (dgate_ab, w_gate[:, 2048:4096])],
                    name="d_u", after=token)
    dx, dmeta, dg_pre = _pre_bwd(du, x2, meta_f, dy, pre_norm_g)
    return (loss_p, dx, dmeta, d_w_in, d_w_uq, d_w_ukv, d_w_bm, d_w_bf, d_w_out, dg_pre, dg_post, dg_q, dg_kv, dfb)


def _w_in_slabs(pieces):
    dw_small, dw_z, dw_fq, dw_fk, dw_fv, dw_g = pieces
    runs = [(dw_small[:, 0:416], C_CQ), (dw_z[:, 0:1024], C_ZMLA), (dw_fq, C_FQ), (dw_fk, C_FK), (dw_fv, C_FV),
            (dw_small[:, 512:528], C_FL), (dw_z[:, 1024:2048], C_ZFOX), (dw_g, C_GA)]
    slabs = []
    for j in range(N_CHIPS):
        lo, hi = W_IN_SHARD * j, W_IN_SHARD * (j + 1)
        cols = [a[:, max(lo, c0) - c0:min(hi, c0 + a.shape[1]) - c0] for a, c0 in runs
                if max(lo, c0) < min(hi, c0 + a.shape[1])]
        slabs.append(jnp.concatenate(cols, axis=1))
    return jnp.stack(slabs, axis=0)


def kernel(x, meta_tokens, pre_norm_g, w_in, fox_forget_b, mla_q_norm_g, mla_kv_norm_g, w_uq, w_ukv, w_br_mla, w_br_fox, w_out, post_norm_g, loss_target, m_meta_tokens, m_pre_norm_g, m_w_in, m_fox_forget_b, m_mla_q_norm_g, m_mla_kv_norm_g, m_w_uq, m_w_ukv, m_w_br_mla, m_w_br_fox, m_w_out, m_post_norm_g, v_meta_tokens, v_pre_norm_g, v_w_in, v_fox_forget_b, v_mla_q_norm_g, v_mla_kv_norm_g, v_w_uq, v_w_ukv, v_w_br_mla, v_w_br_fox, v_w_out, v_post_norm_g):
    me = 2 * lax.axis_index("x") + lax.axis_index("y")
    core = lax.axis_index("c")
    w_in_b = w_in.astype(BF16).reshape(D_MODEL, W_IN_SHARD)
    p2 = _pack_p2(w_uq[0], w_ukv[0], w_br_mla[0], w_br_fox[0], w_out[0], BF16)
    w_in_g, meta_g = _gather_weights([w_in_b], meta_tokens)
    p2_g = _gather_late(lax.optimization_barrier((p2, w_in_g))[0])
    slabs = [jnp.where(me == j, w_in_b, w_in_g[j]) for j in range(N_CHIPS)]
    chip = lax.broadcasted_iota(jnp.int32, (N_CHIPS, 1, 1), 0)
    p2_all = jnp.where(chip == me, p2[None], p2_g)
    w_uq_f = p2_all[:, 0:96].reshape(N_CHIPS, 256, 384).transpose(1, 0, 2).reshape(256, 1536)
    w_ukv_f = p2_all[:, 96:160].reshape(N_CHIPS, 128, 512).transpose(1, 0, 2).reshape(128, 2048)
    w_bm, w_bf, w_o = (p2_all[:, lo:lo + 256].reshape(D_MODEL, D_MODEL) for lo in (160, 416, 672))
    meta_f = jnp.where(chip == me, meta_tokens[None], meta_g).transpose(1, 0, 2).reshape(N_META, D_MODEL)
    kpe = _in_cols(slabs, C_KPE, C_ZMLA)
    w_small = jnp.concatenate(_in_cols(slabs, C_CQ, C_KPE) + kpe + kpe + [jnp.zeros((D_MODEL, 64), BF16)]
                              + _in_cols(slabs, C_FL, C_ZFOX) + [jnp.zeros((D_MODEL, 112), BF16)], axis=1)
    w_attn = jnp.concatenate(_in_cols(slabs, C_FQ, C_FL), axis=1)
    w_gate = jnp.concatenate(_in_cols(slabs, C_ZMLA, C_FQ) + _in_cols(slabs, C_ZFOX, C_END), axis=1)

    exchange = {}

    def start_exchange(d_w_in, d_w_uq, d_w_ukv, d_w_bm, d_w_bf, d_w_out):
        g2 = jnp.concatenate(
            [d_w_uq.reshape(256, N_CHIPS, 384).transpose(1, 0, 2).reshape(N_CHIPS, 96, D_MODEL),
             d_w_ukv.reshape(128, N_CHIPS, 512).transpose(1, 0, 2).reshape(N_CHIPS, 64, D_MODEL)]
            + [g.reshape(N_CHIPS, 256, D_MODEL) for g in (d_w_bm, d_w_bf, d_w_out)], axis=1)
        pieces = [p[None] for p in d_w_in]
        from_sib = _swap_halves(pieces + [g2])
        halves = [_add_cores(p, s, "add_cores_" + nm)[0]
                  for p, s, nm in zip(pieces, from_sib, ("small", "z", "fq", "fk", "fv", "g"))]
        parts = [_w_in_slabs(halves), _add_cores(g2, from_sib[-1], "add_cores_rest")]
        exchange.update(parts=parts, landed=_scatter_chips(parts))
        return parts[0][0, 0:16, 0:LANES]

    (loss_p, dx, dmeta, _, _, _, _, _, _, dg_pre, dg_post, dg_q, dg_kv,
     dfb) = _local_step(x[0], loss_target[0], meta_f, w_small, w_attn, w_gate, w_uq_f, w_ukv_f, w_bm, w_bf, w_o,
                        pre_norm_g, post_norm_g, mla_q_norm_g, mla_kv_norm_g, fox_forget_b, start_exchange)

    mine = [_add_chips(l, lax.dynamic_index_in_dim(p, me, 0, keepdims=False), nm)
            for l, p, nm in zip(exchange["landed"], exchange["parts"], ("add_chips_w_in", "add_chips_rest"))]
    theirs = _swap_reduced(mine)
    g_w_in, g_p2 = [jnp.concatenate([jnp.where(core == 0, a, b), jnp.where(core == 0, b, a)], axis=0)
                    for a, b in zip(mine, theirs)]
    g_w_uq, g_w_ukv, g_w_bm, g_w_bf, g_w_out = _unpack_p2(g_p2)
    g_w_in = g_w_in[None]

    vec = jnp.concatenate([dg_pre.reshape(8, 128), dg_post.reshape(8, 128), dg_q.reshape(2, 128), dg_kv,
                           dfb, _pad_lanes(loss_p), jnp.zeros((3, 128), F32), dmeta.reshape(128, 128)], axis=0)
    tot = _allreduce_small(vec)
    loss = tot[20, 0]
    g_meta = lax.dynamic_slice_in_dim(tot[24:].reshape(N_META, D_MODEL), 256 * me, 256, axis=1)

    def small_pack(pre, post, gq_, gkv_, fb_):
        return jnp.concatenate([pre.reshape(8, 128), post.reshape(8, 128), gq_.reshape(2, 128), gkv_,
                                _pad_lanes(fb_), jnp.zeros((4, 128), F32)], axis=0)

    def small_unpack(t):
        return (t[0:8].reshape(1, 1024), t[8:16].reshape(1, 1024), t[16:18].reshape(1, 256), t[18:19],
                t[19:20, 0:HEADS])

    g_small = jnp.concatenate([tot[0:20], jnp.zeros((4, 128), F32)], axis=0)
    sm = _adamw(small_pack(pre_norm_g, post_norm_g, mla_q_norm_g, mla_kv_norm_g, fox_forget_b), g_small,
                small_pack(m_pre_norm_g, m_post_norm_g, m_mla_q_norm_g, m_mla_kv_norm_g, m_fox_forget_b),
                small_pack(v_pre_norm_g, v_post_norm_g, v_mla_q_norm_g, v_mla_kv_norm_g, v_fox_forget_b),
                "adamw_small")
    g_pre, g_post, g_q, g_kv, g_fb = small_unpack(g_small)
    (d_pre, d_post, d_q, d_kv, d_fb), (nm_pre, nm_post, nm_q, nm_kv, nm_fb), (nv_pre, nv_post, nv_q, nv_kv, nv_fb) = (
        small_unpack(t) for t in sm)

    d_meta, nm_meta, nv_meta = _adamw(meta_tokens, g_meta, m_meta_tokens, v_meta_tokens, "adamw_meta")
    d_win, nm_win, nv_win = (t.T[None] for t in _adamw(w_in[0].T, g_w_in[0].T, m_w_in[0].T, v_w_in[0].T,
                                                       "adamw_w_in"))
    d_wuq, nm_wuq, nv_wuq = _adamw(w_uq[0], g_w_uq, m_w_uq[0], v_w_uq[0], "adamw_w_uq")
    d_wukv, nm_wukv, nv_wukv = _adamw(w_ukv[0], g_w_ukv, m_w_ukv[0], v_w_ukv[0], "adamw_w_ukv")
    d_wbm, nm_wbm, nv_wbm = _adamw(w_br_mla[0], g_w_bm, m_w_br_mla[0], v_w_br_mla[0], "adamw_w_br_mla")
    d_wbf, nm_wbf, nv_wbf = _adamw(w_br_fox[0], g_w_bf, m_w_br_fox[0], v_w_br_fox[0], "adamw_w_br_fox")
    d_wo, nm_wo, nv_wo = _adamw(w_out[0], g_w_out, m_w_out[0], v_w_out[0], "adamw_w_out")

    def group(meta_, pre, win, fb_, q_, kv_, wuq, wukv, wbm, wbf, wo, post):
        return (meta_, pre, win, fb_, q_, kv_, wuq[None], wukv[None], wbm[None], wbf[None], wo[None], post)

    grads = group(g_meta, g_pre, g_w_in, g_fb, g_q, g_kv, g_w_uq, g_w_ukv, g_w_bm, g_w_bf, g_w_out, g_post)
    deltas = group(d_meta, d_pre, d_win, d_fb, d_q, d_kv, d_wuq, d_wukv, d_wbm, d_wbf, d_wo, d_post)
    new_m = group(nm_meta, nm_pre, nm_win, nm_fb, nm_q, nm_kv, nm_wuq, nm_wukv, nm_wbm, nm_wbf, nm_wo, nm_post)
    new_v = group(nv_meta, nv_pre, nv_win, nv_fb, nv_q, nv_kv, nv_wuq, nv_wukv, nv_wbm, nv_wbf, nv_wo, nv_post)
    return (loss, dx[None], *grads, *deltas, *new_m, *new_v)
```

```python
import math

import jax
import jax.numpy as jnp
import numpy as np
from jax import lax
from jax.experimental import pallas as pl
from jax.experimental.pallas import tpu as pltpu
from jax.experimental.pallas import tpu_sc as plsc

F32 = jnp.float32
BF16 = jnp.bfloat16

D_MODEL = 1024
N_META = 16
RMS_EPS = 1e-6
HEADS = 16
PAIRS = HEADS // 2
HEAD_DIM = 64
LANES = 128
MLA_ROPE = 32
AHEAD = 6
AHEAD_BWD = 2
BIAS_PARTS = 3
MLA_SCALE = 1.0 / math.sqrt(64 + 32)
FOX_SCALE = 1.0 / math.sqrt(64)
LOG2E = math.log2(math.e)
LN2 = math.log(2.0)
ROPE_THETA = 10000.0

PAD = 256
BLK = 256
QB = 512
UNROLLS = (8, 4, 2, 1)
NEG = -1e30

C_CQ, C_CKV, C_KPE, C_ZMLA, C_FQ, C_FK, C_FV, C_FL, C_ZFOX, C_GA, C_GB, C_END = (
    0, 256, 384, 416, 1440, 2464, 3488, 4512, 4528, 5552, 6576, 7600)
SMALL_W = 640
W_IN_SHARD = 1900

P2_ROWS = 928
N_CHIPS = 4

ADAM_LR = 0.001
ADAM_B1 = 0.9
ADAM_B2 = 0.999
ADAM_EPS = 1e-08
ADAM_WD = 0.01
ADAM_STEP = 10

VMEM_BIG = 56 * 1024 * 1024
MM_VMEM_BUDGET = 44 * 1024 * 1024
MESH = pl.DeviceIdType.MESH


def _cp(dims, vmem=None):
    return pltpu.CompilerParams(dimension_semantics=dims, vmem_limit_bytes=vmem)


def _dot(a, b, ca, cb):
    return lax.dot_general(a, b, (((ca,), (cb,)), ((), ())), preferred_element_type=F32)


def _sigmoid(x):
    return 1.0 / (1.0 + jnp.exp(-x))


def _tile(n, cands):
    for c in cands:
        if n % c == 0:
            return c
    return n


def _mm(a, b, *, mode, out_dtype, name, acc=None, epilogue=None, row_ins=(), after=None, col_scale=None):
    if mode == "nn":
        (M, K), N = a.shape, b.shape[1]
    elif mode == "nt":
        (M, K), N = a.shape, b.shape[0]
    else:
        (K, M), N = a.shape, b.shape[1]
    tm = _tile(M, (1088, 1024)) if M > 1024 else M
    tn = _tile(N, (1024,)) if N > 1024 else N
    nk = 1
    while True:
        tk = K // nk
        need = 2 * tk * (tm * a.dtype.itemsize + tn * b.dtype.itemsize) + tm * tn * (
            2 * jnp.dtype(out_dtype).itemsize + (8 if acc is not None else 0) + (4 if nk > 1 else 0))
        if need <= MM_VMEM_BUDGET or (tk // 2) % (16 if mode == "tn" else LANES) or tk <= 512:
            break
        nk *= 2
    while (M // tm) * (N // tn) * nk < 4 and tn % 512 == 0:
        tn //= 2
    assert col_scale is None or (nk == 1 and col_scale[0] % tn == 0)
    ca, cb = {"nn": (1, 0), "nt": (1, 1), "tn": (0, 0)}[mode]
    a_spec = (pl.BlockSpec((tk, tm), lambda j, i, k: (k, i)) if mode == "tn"
              else pl.BlockSpec((tm, tk), lambda j, i, k: (i, k)))
    b_spec = (pl.BlockSpec((tn, tk), lambda j, i, k: (j, k)) if mode == "nt"
              else pl.BlockSpec((tk, tn), lambda j, i, k: (k, j)))
    o_spec = pl.BlockSpec((tm, tn), lambda j, i, k: (i, j))
    has_acc = acc is not None

    nrow = len(row_ins)

    def body(*refs):
        a_ref, b_ref = refs[0], refs[1]
        acc_ref = refs[2] if has_acc else None
        rows = refs[2 + has_acc:2 + has_acc + nrow]
        o_ref = refs[2 + has_acc + nrow + (after is not None)]

        def store(tile):
            if epilogue is not None:
                tile = epilogue(tile, *[r[...] for r in rows])
            if col_scale is not None:
                tile = tile * jnp.where(pl.program_id(0) * tn < col_scale[0], col_scale[1], 1.0)
            o_ref[...] = tile.astype(out_dtype)

        part = _dot(a_ref[...].astype(BF16), b_ref[...].astype(BF16), ca, cb)
        if nk == 1:
            store(part + acc_ref[...] if has_acc else part)
        else:
            sc = refs[-1]
            k = pl.program_id(2)

            @pl.when(k == 0)
            def _():
                sc[...] = part + acc_ref[...] if has_acc else part

            @pl.when(k > 0)
            def _():
                sc[...] += part

            @pl.when(k == nk - 1)
            def _():
                store(sc[...])

    ins = [a, b] + ([acc] if has_acc else []) + list(row_ins)
    in_specs = ([a_spec, b_spec] + ([o_spec] if has_acc else [])
                + [pl.BlockSpec((tm, r.shape[1]), lambda j, i, k: (i, 0)) for r in row_ins])
    if after is not None:
        ins.append(after)
        in_specs.append(pl.BlockSpec(after.shape, lambda j, i, k: (0,) * after.ndim))
    return pl.pallas_call(
        body, name=name, grid=(N // tn, M // tm, nk), in_specs=in_specs, out_specs=o_spec,
        out_shape=jax.ShapeDtypeStruct((M, N), out_dtype),
        scratch_shapes=[pltpu.VMEM((tm, tn), F32)] if nk > 1 else [],
        compiler_params=_cp(("parallel", "parallel", "arbitrary"), VMEM_BIG))(*ins)


def _mm_sum_nt(pairs, *, name, after=None):
    n = len(pairs)
    M, N = pairs[0][0].shape[0], pairs[0][1].shape[0]
    tm = _tile(M, (272,))

    def body(*refs):
        o_ref = refs[2 * n + (after is not None)]
        tot = _dot(refs[0][...].astype(BF16), refs[n][...].astype(BF16), 1, 1)
        for i in range(1, n):
            tot = tot + _dot(refs[i][...].astype(BF16), refs[n + i][...].astype(BF16), 1, 1)
        o_ref[...] = tot

    ins = [a for a, _ in pairs] + [b for _, b in pairs]
    in_specs = ([pl.BlockSpec((tm, a.shape[1]), lambda i: (i, 0)) for a, _ in pairs]
                + [pl.BlockSpec(b.shape, lambda i: (0, 0)) for _, b in pairs])
    if after is not None:
        ins.append(after)
        in_specs.append(pl.BlockSpec(after.shape, lambda i: (0,) * after.ndim))
    return pl.pallas_call(
        body, name=name, grid=(M // tm,), in_specs=in_specs, out_specs=pl.BlockSpec((tm, N), lambda i: (i, 0)),
        out_shape=jax.ShapeDtypeStruct((M, N), F32), compiler_params=_cp(("parallel",), VMEM_BIG))(*ins)


def _row(w):
    return pl.BlockSpec((BLK, w), lambda i: (i, 0))


def _rowc(w, c):
    return pl.BlockSpec((BLK, w), lambda i: (i, c))


def _full(shape):
    return pl.BlockSpec(shape, lambda i: tuple(0 for _ in shape))


def _rope(x, c, s):
    lane = lax.broadcasted_iota(jnp.int32, x.shape, 1)
    is_x1 = ((lane >> 4) & 1) == 0
    partner = jnp.where(is_x1, pltpu.roll(x, LANES - 16, 1), pltpu.roll(x, 16, 1))
    return x * c + partner * s


def _row_valid(i):
    rows = i * BLK + lax.broadcasted_iota(jnp.int32, (BLK, 1), 0)
    return (rows < N_META) | (rows >= PAD)


def _shift_rows(w):
    return pl.BlockSpec((BLK, w), lambda i: (jnp.maximum(i - 1, 0), 0))


def _h_block(i, x_ref, meta_ref):
    head = jnp.concatenate([meta_ref[...], jnp.zeros((BLK - N_META, D_MODEL), F32)], axis=0)
    return jnp.where(i == 0, head, x_ref[...])


def _rms_pre(x2, meta, g):
    lp = PAD + x2.shape[0]

    def body(x_ref, meta_ref, g_ref, u_ref):
        hv = _h_block(pl.program_id(0), x_ref, meta_ref)
        r = lax.rsqrt(jnp.mean(hv * hv, axis=-1, keepdims=True) + RMS_EPS)
        u_ref[...] = (hv * r * g_ref[...]).astype(BF16)

    return pl.pallas_call(
        body, name="rms_pre", grid=(lp // BLK,),
        in_specs=[_shift_rows(D_MODEL), _full((N_META, D_MODEL)), _full((1, D_MODEL))], out_specs=_row(D_MODEL),
        out_shape=jax.ShapeDtypeStruct((lp, D_MODEL), BF16),
        compiler_params=_cp(("parallel",)))(x2, meta, g)


def _split3(x):
    hi = x.astype(BF16)
    r1 = x - hi.astype(F32)
    mid = r1.astype(BF16)
    lo = (r1 - mid.astype(F32)).astype(BF16)
    return hi, mid, lo


def _small_prep(small, gq, gkv, fb, ctab, stab, tri):
    lp = small.shape[0]

    def body(sm_ref, gq_ref, gkv_ref, fb_ref, c_ref, s_ref, tri_ref, qn_ref, kvn_ref, kr_ref, kb_ref, carry):
        i = pl.program_id(0)

        @pl.when(i == 0)
        def _():
            carry[...] = jnp.zeros_like(carry)

        cq = sm_ref[:, 0:256]
        r = lax.rsqrt(jnp.mean(cq * cq, axis=-1, keepdims=True) + RMS_EPS)
        qn_ref[...] = (cq * r * gq_ref[...]).astype(BF16)
        ckv = sm_ref[:, 256:384]
        r = lax.rsqrt(jnp.mean(ckv * ckv, axis=-1, keepdims=True) + RMS_EPS)
        kvn_ref[...] = (ckv * r * gkv_ref[...]).astype(BF16)
        kr_ref[...] = _rope(sm_ref[:, 384:512], c_ref[...], s_ref[...]).astype(BF16)
        fl = sm_ref[:, 512:640] + fb_ref[...]
        lf = jnp.minimum(fl, 0.0) - jnp.log(1.0 + jnp.exp(-jnp.abs(fl)))
        lf = jnp.where(_row_valid(i), lf, 0.0)
        hi, mid, lo = _split3(lf)
        t = tri_ref[...]
        cum = (_dot(t, hi, 1, 0) + _dot(t, mid, 1, 0)) + _dot(t, lo, 1, 0) + carry[...]
        carry[...] = cum[BLK - 1:BLK, :]
        src = lax.broadcasted_iota(jnp.int32, (LANES, LANES), 0)
        dst = lax.broadcasted_iota(jnp.int32, (LANES, LANES), 1)
        kb = jnp.zeros((BLK, LANES), F32)
        for j, part in enumerate(_split3(-cum * LOG2E)):
            spread = ((dst == BIAS_PARTS * src + j) & (src < HEADS)).astype(BF16)
            kb = kb + _dot(part, spread, 1, 0)
        kb_ref[...] = kb.astype(BF16)

    return pl.pallas_call(
        body, name="small_prep", grid=(lp // BLK,),
        in_specs=[_row(SMALL_W), _full((1, 256)), _full((1, 128)), _full((1, 128)), _row(128), _row(128),
                  _full((BLK, BLK))],
        out_specs=[_row(256), _row(128), _row(128), _row(128)],
        out_shape=[jax.ShapeDtypeStruct((lp, 256), BF16), jax.ShapeDtypeStruct((lp, 128), BF16),
                   jax.ShapeDtypeStruct((lp, 128), BF16), jax.ShapeDtypeStruct((lp, 128), BF16)],
        scratch_shapes=[pltpu.VMEM((1, 128), F32)],
        compiler_params=_cp(("arbitrary",)))(small, gq, gkv, fb, ctab, stab, tri)


def _rope_pairs(tile, c, s):
    out = []
    for lo in range(0, tile.shape[1], 256):
        out += [tile[:, lo:lo + 128], _rope(tile[:, lo + 128:lo + 256], c, s)]
    return jnp.concatenate(out, axis=1)


def _gate_fwd(o_mla, o_fox, gate):
    lp = o_mla.shape[0]

    def body(om_ref, of_ref, zm_ref, zf_ref, am_ref, af_ref):
        zm = zm_ref[...].astype(F32)
        am_ref[...] = (om_ref[...] * (zm * _sigmoid(zm))).astype(BF16)
        zf = zf_ref[...].astype(F32)
        af_ref[...] = (of_ref[...] * (zf * _sigmoid(zf))).astype(BF16)

    return pl.pallas_call(
        body, name="gate_fwd", grid=(lp // BLK,),
        in_specs=[_row(D_MODEL), _row(D_MODEL), _rowc(D_MODEL, 0), _rowc(D_MODEL, 1)],
        out_specs=[_row(D_MODEL), _row(D_MODEL)],
        out_shape=[jax.ShapeDtypeStruct((lp, D_MODEL), BF16)] * 2,
        compiler_params=_cp(("parallel",)))(o_mla, o_fox, gate, gate)


def _merge_fwd(gate, y_mla, y_fox):
    lp = y_mla.shape[0]

    def body(ga_ref, gb_ref, ym_ref, yf_ref, m_ref):
        sa = _sigmoid(ga_ref[...].astype(F32))
        sb = _sigmoid(gb_ref[...].astype(F32))
        m_ref[...] = (sa * ym_ref[...] + sb * yf_ref[...]).astype(BF16)

    return pl.pallas_call(
        body, name="merge_fwd", grid=(lp // BLK,),
        in_specs=[_rowc(D_MODEL, 2), _rowc(D_MODEL, 3), _row(D_MODEL), _row(D_MODEL)],
        out_specs=_row(D_MODEL), out_shape=jax.ShapeDtypeStruct((lp, D_MODEL), BF16),
        compiler_params=_cp(("parallel",)))(gate, gate, y_mla, y_fox)


def _tail(x2, mixed, tgt, gpost):
    lp = mixed.shape[0]
    shift = _shift_rows(D_MODEL)

    def body(h_ref, mx_ref, t_ref, g_ref, dmx_ref, dy_ref, loss_ref, dg_ref):
        i = pl.program_id(0)

        @pl.when(i == 0)
        def _():
            loss_ref[...] = jnp.zeros_like(loss_ref)
            dg_ref[...] = jnp.zeros_like(dg_ref)
            dmx_ref[...] = jnp.zeros_like(dmx_ref)
            dy_ref[...] = jnp.zeros_like(dy_ref)

        @pl.when(i > 0)
        def _():
            mx = mx_ref[...]
            g = g_ref[...]
            r = lax.rsqrt(jnp.mean(mx * mx, axis=-1, keepdims=True) + RMS_EPS)
            nrm = mx * r
            e = (h_ref[...] + nrm * g) - t_ref[...]
            loss_ref[...] += jnp.sum(0.5 * jnp.sum(e * e, axis=-1, keepdims=True) * (1.0 / D_MODEL),
                                     axis=0, keepdims=True)
            dy = e * (1.0 / D_MODEL)
            dy_ref[...] = dy
            dg_ref[...] += jnp.sum(dy * nrm, axis=0, keepdims=True)
            w = dy * g
            dot = jnp.mean(w * mx, axis=-1, keepdims=True)
            dmx_ref[...] = (r * w - mx * (r * r * r * dot)).astype(BF16)

    return pl.pallas_call(
        body, name="tail", grid=(lp // BLK,),
        in_specs=[shift, _row(D_MODEL), shift, _full((1, D_MODEL))],
        out_specs=[_row(D_MODEL), _row(D_MODEL), _full((1, 1)), _full((1, D_MODEL))],
        out_shape=[jax.ShapeDtypeStruct((lp, D_MODEL), BF16), jax.ShapeDtypeStruct((lp, D_MODEL), F32),
                   jax.ShapeDtypeStruct((1, 1), F32), jax.ShapeDtypeStruct((1, D_MODEL), F32)],
        compiler_params=_cp(("arbitrary",)))(x2, mixed, tgt, gpost)


def _merge_bwd(dm, gate, y_mla, y_fox):
    lp = dm.shape[0]

    def body(dm_ref, ga_ref, gb_ref, ym_ref, yf_ref, dym_ref, dyf_ref, dg_ref):
        dm_v = dm_ref[...].astype(F32)
        sa = _sigmoid(ga_ref[...].astype(F32))
        sb = _sigmoid(gb_ref[...].astype(F32))
        dym_ref[...] = (dm_v * sa).astype(BF16)
        dyf_ref[...] = (dm_v * sb).astype(BF16)
        dg_ref[:, 0:D_MODEL] = (dm_v * ym_ref[...] * (sa * (1.0 - sa))).astype(BF16)
        dg_ref[:, D_MODEL:2 * D_MODEL] = (dm_v * yf_ref[...] * (sb * (1.0 - sb))).astype(BF16)

    return pl.pallas_call(
        body, name="merge_bwd", grid=(lp // BLK,),
        in_specs=[_row(D_MODEL), _rowc(D_MODEL, 2), _rowc(D_MODEL, 3), _row(D_MODEL), _row(D_MODEL)],
        out_specs=[_row(D_MODEL), _row(D_MODEL), _row(2 * D_MODEL)],
        out_shape=[jax.ShapeDtypeStruct((lp, D_MODEL), BF16), jax.ShapeDtypeStruct((lp, D_MODEL), BF16),
                   jax.ShapeDtypeStruct((lp, 2 * D_MODEL), BF16)],
        compiler_params=_cp(("parallel",)))(dm, gate, gate, y_mla, y_fox)


def _gate_bwd(da_mla, da_fox, o_mla, o_fox, gate):
    lp = da_mla.shape[0]

    def one(da, o, z, head_of_col):
        sg = _sigmoid(z)
        do = (da * (z * sg)).astype(BF16)
        dz = da * o * (sg * (1.0 + z * (1.0 - sg)))
        delta = sum(_dot(part, head_of_col, 1, 0) for part in _split3(do.astype(F32) * o))
        return do, dz.astype(BF16), delta

    def body(dam_ref, daf_ref, om_ref, of_ref, zm_ref, zf_ref, dom_ref, dof_ref, dz_ref, dlm_ref, dlf_ref):
        f32 = lambda r: r[...].astype(F32)
        head_of_col = (lax.broadcasted_iota(jnp.int32, (D_MODEL, LANES), 0) // HEAD_DIM
                       == lax.broadcasted_iota(jnp.int32, (D_MODEL, LANES), 1)).astype(BF16)
        dom_ref[...], dz_ref[:, 0:D_MODEL], dlm_ref[...] = one(f32(dam_ref), f32(om_ref), f32(zm_ref), head_of_col)
        dof_ref[...], dz_ref[:, D_MODEL:2 * D_MODEL], dlf_ref[...] = one(f32(daf_ref), f32(of_ref), f32(zf_ref),
                                                                        head_of_col)

    return pl.pallas_call(
        body, name="gate_bwd", grid=(lp // BLK,),
        in_specs=[_row(D_MODEL)] * 4 + [_rowc(D_MODEL, 0), _rowc(D_MODEL, 1)],
        out_specs=[_row(D_MODEL), _row(D_MODEL), _row(2 * D_MODEL), _row(LANES), _row(LANES)],
        out_shape=[jax.ShapeDtypeStruct((lp, D_MODEL), BF16), jax.ShapeDtypeStruct((lp, D_MODEL), BF16),
                   jax.ShapeDtypeStruct((lp, 2 * D_MODEL), BF16), jax.ShapeDtypeStruct((lp, LANES), F32),
                   jax.ShapeDtypeStruct((lp, LANES), F32)],
        compiler_params=_cp(("parallel",)))(da_mla, da_fox, o_mla, o_fox, gate, gate)


def _small_bwd(small, dqn, dkvn, dkr, dcol_t, drow_t, gq, gkv, fb, ctab, stab, triu):
    lp = small.shape[0]
    nb = lp // BLK

    def rrow(w):
        return pl.BlockSpec((BLK, w), lambda i: (nb - 1 - i, 0))

    def body(sm_ref, dqn_ref, dkvn_ref, dkr_ref, dcol_ref, drow_ref, gq_ref, gkv_ref, fb_ref, c_ref, s_ref, tri_ref,
             ds_ref, dgq_ref, dgkv_ref, dfb_ref, carry):
        i = pl.program_id(0)

        @pl.when(i == 0)
        def _():
            carry[...] = jnp.zeros_like(carry)
            dgq_ref[...] = jnp.zeros_like(dgq_ref)
            dgkv_ref[...] = jnp.zeros_like(dgkv_ref)
            dfb_ref[...] = jnp.zeros_like(dfb_ref)

        def norm_bwd(x, dn, g, dg_ref):
            r = lax.rsqrt(jnp.mean(x * x, axis=-1, keepdims=True) + RMS_EPS)
            dg_ref[...] += jnp.sum(dn * (x * r), axis=0, keepdims=True)
            w = dn * g
            dot = jnp.mean(w * x, axis=-1, keepdims=True)
            return r * w - x * (r * r * r * dot)

        ds_ref[:, 0:256] = norm_bwd(sm_ref[:, 0:256], dqn_ref[...], gq_ref[...], dgq_ref).astype(BF16)
        ds_ref[:, 256:384] = norm_bwd(sm_ref[:, 256:384], dkvn_ref[...], gkv_ref[...], dgkv_ref).astype(BF16)

        dk = dkr_ref[0]
        for p in range(1, PAIRS):
            dk = dk + dkr_ref[p]
        dk = _rope(dk, c_ref[...], -s_ref[...])
        lane = lax.broadcasted_iota(jnp.int32, dk.shape, 1)
        dk = jnp.where(lane < MLA_ROPE, dk + pltpu.roll(dk, LANES - MLA_ROPE, 1), 0.0)
        ds_ref[:, 384:512] = dk.astype(BF16)

        dcol = dcol_ref[0]
        for p in range(1, PAIRS):
            dcol = dcol + pltpu.roll(dcol_ref[p], 2 * p, 1)
        rows16 = jnp.concatenate([drow_ref[p, h:h + 1, :] for p in range(PAIRS) for h in range(2)], axis=0)
        eye = (lax.broadcasted_iota(jnp.int32, (HEADS, LANES), 0)
               == lax.broadcasted_iota(jnp.int32, (HEADS, LANES), 1)).astype(BF16)
        drow = sum(_dot(part, eye, 0, 0) for part in _split3(rows16))
        dcr = dcol - drow
        hi, mid, lo = _split3(dcr)
        t = tri_ref[...]
        suf = (_dot(t, hi, 1, 0) + _dot(t, mid, 1, 0)) + _dot(t, lo, 1, 0) + carry[...]
        fl = sm_ref[:, 512:640] + fb_ref[...]
        dfl = jnp.where(_row_valid(nb - 1 - i), -suf * _sigmoid(-fl), 0.0)
        ds_ref[:, 512:640] = dfl.astype(BF16)
        dfb_ref[...] += jnp.sum(dfl, axis=0, keepdims=True)
        carry[...] += jnp.sum(dcr, axis=0, keepdims=True)

    return pl.pallas_call(
        body, name="small_bwd", grid=(nb,),
        in_specs=[rrow(SMALL_W), rrow(256), rrow(128),
                  pl.BlockSpec((PAIRS, BLK, 128), lambda i: (0, nb - 1 - i, 0)),
                  pl.BlockSpec((PAIRS, BLK, 128), lambda i: (0, nb - 1 - i, 0)),
                  pl.BlockSpec((PAIRS, 2, BLK), lambda i: (0, 0, nb - 1 - i)),
                  _full((1, 256)), _full((1, 128)), _full((1, 128)), rrow(128), rrow(128), _full((BLK, BLK))],
        out_specs=[rrow(SMALL_W), _full((1, 256)), _full((1, 128)), _full((1, 128))],
        out_shape=[jax.ShapeDtypeStruct((lp, SMALL_W), BF16), jax.ShapeDtypeStruct((1, 256), F32),
                   jax.ShapeDtypeStruct((1, 128), F32), jax.ShapeDtypeStruct((1, 128), F32)],
        scratch_shapes=[pltpu.VMEM((1, 128), F32)],
        compiler_params=_cp(("arbitrary",)))(small, dqn, dkvn, dkr, dcol_t, drow_t, gq, gkv, fb, ctab, stab, triu)


def _pre_bwd(du, x2, meta, dy, gpre):
    s_rows = x2.shape[0]
    lp = PAD + s_rows
    shift = _shift_rows(D_MODEL)

    def body(du_ref, x_ref, meta_ref, dy_ref, g_ref, dx_ref, dmeta_ref, dg_ref):
        i = pl.program_id(0)

        @pl.when(i == 0)
        def _():
            dg_ref[...] = jnp.zeros_like(dg_ref)

        hv = _h_block(i, x_ref, meta_ref)
        duv = du_ref[...]
        r = lax.rsqrt(jnp.mean(hv * hv, axis=-1, keepdims=True) + RMS_EPS)
        dg_ref[...] += jnp.sum(duv * (hv * r), axis=0, keepdims=True)
        w = duv * g_ref[...]
        dot = jnp.mean(w * hv, axis=-1, keepdims=True)
        dh = dy_ref[...] + (r * w - hv * (r * r * r * dot))
        dx_ref[...] = dh

        @pl.when(i == 0)
        def _():
            dmeta_ref[...] = dh[0:N_META, :]

    return pl.pallas_call(
        body, name="pre_bwd", grid=(lp // BLK,),
        in_specs=[_row(D_MODEL), shift, _full((N_META, D_MODEL)), _row(D_MODEL), _full((1, D_MODEL))],
        out_specs=[shift, _full((N_META, D_MODEL)), _full((1, D_MODEL))],
        out_shape=[jax.ShapeDtypeStruct((s_rows, D_MODEL), F32), jax.ShapeDtypeStruct((N_META, D_MODEL), F32),
                   jax.ShapeDtypeStruct((1, D_MODEL), F32)],
        compiler_params=_cp(("arbitrary",)))(du, x2, meta, dy, gpre)


def _pair_masks(rope, pair):
    lane = lax.broadcasted_iota(jnp.int32, (1, LANES), 1)
    mas = [lane < HEAD_DIM, lane >= HEAD_DIM]
    wide = lax.broadcasted_iota(jnp.int32, (1, 2 * LANES), 1)
    extra = MLA_ROPE if rope else BIAS_PARTS
    lo = LANES if rope else LANES + 2 * BIAS_PARTS * pair
    mid = lo + extra
    return mas, [(wide < HEAD_DIM) | ((wide >= lo) & (wide < mid)),
                 ((wide >= HEAD_DIM) & (wide < LANES)) | ((wide >= mid) & (wide < mid + extra))]


def _mask2(x, masks):
    return [jnp.where(m, x, jnp.zeros_like(x)) for m in masks]


def _q_heads(q_rows, rope, mas, hmask):
    if rope:
        return _mask2(q_rows, hmask)
    zero = jnp.zeros((q_rows.shape[0], LANES), BF16)
    return [jnp.concatenate([jnp.where(m, q_rows, zero), jnp.where(hm[:, LANES:], zero + 1, zero)], axis=1)
            for m, hm in zip(mas, hmask)]


def _transposed_cols(x, group, name):
    lp = x.shape[0]

    def body(x_ref, o_ref):
        o_ref[...] = x_ref[...].T

    cols = 2 * BLK
    per_group = D_MODEL // cols
    return pl.pallas_call(
        body, name=name, grid=(per_group,),
        in_specs=[pl.BlockSpec((lp, cols), lambda i: (0, per_group * group + i))],
        out_specs=pl.BlockSpec((cols, lp), lambda i: (i, 0)),
        out_shape=jax.ShapeDtypeStruct((D_MODEL, lp), x.dtype),
        compiler_params=_cp(("parallel",), MM_VMEM_BUDGET))(x)


def _attn_fwd(q, k, v, vt, k2, *, rope, qcol, kcol, vcol, name):
    lp = q.shape[0]
    nq = 1 + (lp - PAD) // QB
    qw = 256 if rope else 128

    def body(q_ref, k_ref, v_ref, vt_ref, k2_ref, o_ref, lse_ref):
        i = pl.program_id(1)
        r0 = pl.multiple_of(jnp.where(i == 0, 0, PAD + QB * (i - 1)), BLK)
        b0 = r0 // BLK
        mas, hmask = _pair_masks(rope, pl.program_id(0))
        qh = _q_heads(q_ref[pl.ds(r0, QB), :], rope, mas, hmask)

        def update(chunks, tiles, groups):
            m = [[cr[0], cr[2]] for _, _, cr in groups]
            l = [[cr[1], cr[3]] for _, _, cr in groups]
            acc = [[cr[4][0:HEAD_DIM], cr[4][HEAD_DIM:LANES]] for _, _, cr in groups]
            qs = [[x[q_lo:q_lo + wq] for x in qh] for q_lo, wq, _ in groups]
            k0s = [pl.multiple_of(kc * BLK, BLK) for kc, _ in chunks]
            kks = [jnp.concatenate([k_ref[pl.ds(k0, n), :], k2_ref[pl.ds(k0, n), :]], axis=1)
                   for k0, (_, n) in zip(k0s, chunks)]
            jobs = [(g, ci, mask, h) for g, ci, mask in tiles for h in range(2)]
            score = lambda t: _dot(kks[jobs[t][1]], qs[jobs[t][0]][jobs[t][3]], 1, 1)
            ss = [score(t) for t in range(min(AHEAD, len(jobs)))]
            for t, (g, ci, mask, h) in enumerate(jobs):
                if t + AHEAD < len(jobs):
                    ss.append(score(t + AHEAD))
                s = ss[t] if mask is None else jnp.where(mask, ss[t], NEG)
                m_new = jnp.maximum(m[g][h], jnp.max(s, axis=0, keepdims=True))
                alpha = jnp.exp2(m[g][h] - m_new)
                p = jnp.exp2(s - m_new)
                l[g][h] = alpha * l[g][h] + jnp.sum(p, axis=0, keepdims=True)
                m[g][h] = m_new
                n = chunks[ci][1]
                if n == BLK:
                    pv = _dot(vt_ref[pl.ds(HEAD_DIM * h, HEAD_DIM), pl.ds(k0s[ci], BLK)], p.astype(BF16), 1, 0)
                else:
                    vm = jnp.where(mas[h], v_ref[0:n, :], jnp.zeros((), BF16))
                    pv = _dot(vm, p.astype(BF16), 0, 0)[HEAD_DIM * h:HEAD_DIM * (h + 1)]
                acc[g][h] = alpha * acc[g][h] + pv
            return [(m[g][0], l[g][0], m[g][1], l[g][1], jnp.concatenate(acc[g], axis=0)) for g in range(len(groups))]

        def full_chunks(kcs, carry):
            return update([(kc, BLK) for kc in kcs], [(0, ci, None) for ci in range(len(kcs))], [(0, QB, carry)])[0]

        neg = jnp.full((1, QB), NEG, F32)
        zero = jnp.zeros((1, QB), F32)
        c = (neg, zero, neg, zero, jnp.zeros((LANES, QB), F32))
        n_mid = jnp.maximum(b0 - 1, 0)
        c = lax.fori_loop(0, n_mid // 4, lambda t, cr: full_chunks([4 * t + u for u in (1, 2, 3, 4)], cr), c)
        c = lax.fori_loop(0, (n_mid % 4) // 2, lambda t, cr: full_chunks([n_mid - 1, n_mid], cr), c)
        key_l = lax.broadcasted_iota(jnp.int32, (BLK, BLK), 0)
        qry_l = lax.broadcasted_iota(jnp.int32, (BLK, BLK), 1)
        tri = (key_l <= qry_l) & (b0 > 0)
        meta_ok = (key_l[0:N_META] <= qry_l[0:N_META]) | (b0 > 0)
        lo, hi = update([(0, N_META), (b0, BLK), (b0 + 1, BLK)],
                        [(0, 0, meta_ok), (1, 0, None), (0, 1, tri), (1, 1, None), (1, 2, tri)],
                        [(0, BLK, tuple(a[:, 0:BLK] for a in c)), (BLK, QB - BLK, tuple(a[:, BLK:QB] for a in c))])
        c = tuple(jnp.concatenate([a, b], axis=1) for a, b in zip(lo, hi))
        inv =jnp.concatenate([jnp.broadcast_to(1.0 / c[1], (HEAD_DIM, QB)),
                               jnp.broadcast_to(1.0 / c[3], (HEAD_DIM, QB))], axis=0)
        o_t = (c[4] * inv).T.astype(BF16)
        lses = [c[2 * h] + jnp.log(c[2 * h + 1]) * LOG2E for h in range(2)]
        o_ref[pl.ds(r0, BLK), :] = o_t[0:BLK]
        for h in range(2):
            lse_ref[0, h:h + 1, pl.ds(r0, BLK)] = lses[h][:, 0:BLK]

        @pl.when(i > 0)
        def _():
            r1 = pl.multiple_of(r0 + BLK, BLK)
            o_ref[pl.ds(r1, QB - BLK), :] = o_t[BLK:QB]
            for h in range(2):
                lse_ref[0, h:h + 1, pl.ds(r1, QB - BLK)] = lses[h][:, BLK:QB]

    in_specs = [pl.BlockSpec((lp, qw), lambda p, i: (0, qcol + p)),
                pl.BlockSpec((lp, 128), lambda p, i: (0, kcol(p))),
                pl.BlockSpec((BLK, 128), lambda p, i: (0, vcol(p))),
                pl.BlockSpec((128, lp), lambda p, i: (p, 0)),
                pl.BlockSpec((lp, 128), lambda p, i: (0, 0))]
    return pl.pallas_call(
        body, name=name, grid=(PAIRS, nq), in_specs=in_specs,
        out_specs=[pl.BlockSpec((lp, 128), lambda p, i: (0, p)),
                   pl.BlockSpec((1, 2, lp), lambda p, i: (p, 0, 0))],
        out_shape=[jax.ShapeDtypeStruct((lp, D_MODEL), BF16), jax.ShapeDtypeStruct((PAIRS, 2, lp), F32)],
        compiler_params=_cp(("parallel", "arbitrary"), VMEM_BIG))(q, k, v, vt, k2)


def _attn_bwd(q, k, v, k2, do, delta, lse, *, rtabs=None, scale, qcol, kcol, vcol, name):
    lp = q.shape[0]
    nb = lp // BLK
    rope = rtabs is not None
    bias = not rope
    qw = 256 if rope else 128

    def body(*refs):
        it = iter(refs)
        q_ref, k_ref, v_ref, k2_ref = next(it), next(it), next(it), next(it)
        do_ref, dl_ref, lse_ref = next(it), next(it), next(it)
        ct_ref, st_ref = (next(it), next(it)) if rope else (None, None)
        dq_out, dk_ref, dv_ref = next(it), next(it), next(it)
        x_ref = next(it)
        drow_ref = next(it) if bias else None
        dq_ref = next(it)
        kb = pl.program_id(1)
        mas, hmask = _pair_masks(rope, pl.program_id(0))
        lane = lax.broadcasted_iota(jnp.int32, (1, LANES), 1)

        @pl.when(kb == 0)
        def _():
            dq_ref[...] = jnp.zeros_like(dq_ref)
            if bias:
                drow_ref[...] = jnp.zeros_like(drow_ref)

        def key_pass(n, w):
            kk = jnp.concatenate([k_ref[0:n, :], k2_ref[0:n, :]], axis=1)
            vh = _mask2(v_ref[0:n, :], mas)
            kcat = jnp.concatenate([x[:, 0:qw] for x in _mask2(kk, hmask)], axis=0)
            diag_mask = (lax.broadcasted_iota(jnp.int32, (n, w), 0) <= lax.broadcasted_iota(jnp.int32, (n, w), 1))

            def front(qc):
                q0 = qc * w if isinstance(qc, int) else pl.multiple_of(qc * w, w)
                dov = do_ref[pl.ds(q0, w), :]
                qh = _q_heads(q_ref[pl.ds(q0, w), :], rope, mas, hmask)
                ss, dps = [], []
                for h in range(2):
                    ss.append(_dot(kk, qh[h], 1, 1))
                    dps.append(_dot(vh[h], dov, 1, 1))
                return q0, dov, qh, ss, dps

            def back(fronted, carry, mask):
                carry = list(carry)
                q0, dov, qh, ss, dps = fronted
                doh = _mask2(dov, mas)
                pbs, dss = [], []
                for h in range(2):
                    p = jnp.exp2(ss[h] - lse_ref[0, h:h + 1, pl.ds(q0, w)])
                    if mask is not None:
                        p = jnp.where(mask, p, 0.0)
                    ds = p * (dps[h] - dl_ref[0, h:h + 1, pl.ds(q0, w)])
                    if bias:
                        drow_ref[0, h:h + 1, pl.ds(q0, w)] += jnp.sum(ds, axis=0, keepdims=True)
                        carry[2 + h] = carry[2 + h] + jnp.sum(ds, axis=1, keepdims=True)
                    pbs.append(p.astype(BF16))
                    dss.append(ds.astype(BF16))
                ds_lanes = jnp.concatenate(dss, axis=1)
                ds_rows = jnp.concatenate(dss, axis=0)
                qcat = jnp.concatenate([x[:, 0:qw] for x in qh], axis=0)
                carry[0] = carry[0] + _dot(ds_lanes, qcat, 1, 0)
                carry[1] = carry[1] + _dot(jnp.concatenate(pbs, axis=1), jnp.concatenate(doh, axis=0), 1, 0)
                dq_ref[pl.ds(q0, w), :] += _dot(ds_rows, kcat, 0, 0)
                return tuple(carry)

            def chunks(qcs, carry, masks):
                ahead = AHEAD_BWD
                fronted = [front(qc) for qc in qcs[:ahead]]
                for u, mask in enumerate(masks):
                    if u + ahead < len(qcs):
                        fronted.append(front(qcs[u + ahead]))
                    carry = back(fronted[u], carry, mask)
                return carry

            c = [jnp.zeros((n, qw), F32), jnp.zeros((n, LANES), F32)]
            if bias:
                c += [jnp.zeros((n, 1), F32), jnp.zeros((n, 1), F32)]
            c = tuple(c)
            if w != BLK:
                c = chunks(list(range(lp // w)), c, [diag_mask] + [None] * (lp // w - 1))
            else:
                start = kb
                for width in UNROLLS:

                    def several(t, cr, start=start, width=width):
                        qc = start + width * t
                        return chunks([qc + u for u in range(width)], cr, [diag_mask | (qc > kb)] + [None] * (width - 1))

                    trips = (nb - start) // width
                    c = lax.fori_loop(0, trips, several, c)
                    start = start + width * trips

            def rows(a, dtype):
                a = a.astype(dtype)
                return a if n == BLK else jnp.concatenate([a, jnp.zeros((BLK - n, a.shape[1]), dtype)], axis=0)

            dk = c[0] * LN2
            dk_ref[...] = rows(dk[:, 0:LANES], BF16)
            dv_ref[...] = rows(c[1], BF16)
            if rope:
                x_ref[0] = rows(dk[:, LANES:2 * LANES], F32)
            if bias:
                x_ref[0] = rows(jnp.where(lane == 0, c[2], jnp.where(lane == 1, c[3], 0.0)), F32)

        @pl.when(kb == 0)
        def _():
            key_pass(N_META, lp // 2)

        @pl.when(kb > 0)
        def _():
            key_pass(BLK, BLK)

        @pl.when(kb == nb - 1)
        def _():
            def fin(c, carry):
                r0 = pl.multiple_of(c * BLK, BLK)
                dq = dq_ref[pl.ds(r0, BLK), :] * scale
                if rope:
                    back = _rope(dq[:, LANES:2 * LANES], ct_ref[pl.ds(r0, BLK), :], -st_ref[pl.ds(r0, BLK), :])
                    dq = jnp.concatenate([dq[:, 0:LANES], back], axis=1)
                dq_out[pl.ds(r0, BLK), :] = dq.astype(BF16)
                return carry

            lax.fori_loop(0, nb, fin, 0)

    in_specs = [pl.BlockSpec((lp, qw), lambda p, j: (0, qcol + p)),
                pl.BlockSpec((BLK, 128), lambda p, j: (j, kcol(p))),
                pl.BlockSpec((BLK, 128), lambda p, j: (j, vcol(p))),
                pl.BlockSpec((BLK, 128), lambda p, j: (j, 0)),
                pl.BlockSpec((lp, 128), lambda p, j: (0, p)), pl.BlockSpec((1, 2, lp), lambda p, j: (p, 0, 0)),
                pl.BlockSpec((1, 2, lp), lambda p, j: (p, 0, 0))]
    ins = [q, k, v, k2, do, delta, lse]
    if rope:
        in_specs += [pl.BlockSpec((lp, 128), lambda p, j: (0, 0))] * 2
        ins += list(rtabs)
    out_specs = [pl.BlockSpec((lp, qw), lambda p, j: (0, p)),
                 pl.BlockSpec((BLK, 128), lambda p, j: (j, p)),
                 pl.BlockSpec((BLK, 128), lambda p, j: (j, p)),
                 pl.BlockSpec((1, BLK, 128), lambda p, j: (p, j, 0))]
    out_shape = [jax.ShapeDtypeStruct((lp, PAIRS * qw), BF16), jax.ShapeDtypeStruct((lp, D_MODEL), BF16),
                 jax.ShapeDtypeStruct((lp, D_MODEL), BF16), jax.ShapeDtypeStruct((PAIRS, lp, 128), F32)]
    if bias:
        out_specs.append(pl.BlockSpec((1, 2, lp), lambda p, j: (p, 0, 0)))
        out_shape.append(jax.ShapeDtypeStruct((PAIRS, 2, lp), F32))
    return pl.pallas_call(
        body, name=name, grid=(PAIRS, nb), in_specs=in_specs, out_specs=out_specs, out_shape=out_shape,
        scratch_shapes=[pltpu.VMEM((lp, qw), F32)],
        compiler_params=_cp(("parallel", "arbitrary"), VMEM_BIG))(*ins)


def _adamw(w, g, m, v, name):
    lead = w.ndim - 2
    rows, cols = w.shape[lead:]
    big = rows * cols > 512 * 1024
    tr = 128 if big and rows % 128 == 0 else rows
    tc = 256 if big and tr == rows else cols

    def body(w_ref, g_ref, m_ref, v_ref, d_ref, nm_ref, nv_ref):
        gv = g_ref[...]
        nm = ADAM_B1 * m_ref[...] + (1.0 - ADAM_B1) * gv
        nv = ADAM_B2 * v_ref[...] + (1.0 - ADAM_B2) * (gv * gv)
        m_hat = nm / (1.0 - ADAM_B1 ** ADAM_STEP)
        v_hat = nv / (1.0 - ADAM_B2 ** ADAM_STEP)
        d_ref[...] = -ADAM_LR * (m_hat / (jnp.sqrt(v_hat) + ADAM_EPS) + ADAM_WD * w_ref[...])
        nm_ref[...] = nm
        nv_ref[...] = nv

    spec = pl.BlockSpec((1,) * lead + (tr, tc), lambda i, j: (0,) * lead + (i, j))
    return pl.pallas_call(
        body, name=name, grid=(rows // tr, cols // tc), in_specs=[spec] * 4, out_specs=[spec] * 3,
        out_shape=[jax.ShapeDtypeStruct(w.shape, F32)] * 3,
        compiler_params=_cp(("parallel", "parallel"), VMEM_BIG))(w, g, m, v)


def _add_cores(g, from_sib, name):
    n, rows, cols = g.shape
    half = rows // 2
    tr = _tile(half, (256, 240))
    nt = half // tr

    def body(lo_ref, hi_ref, s_ref, o_ref):
        mine = jnp.where(lax.axis_index("c") == 0, lo_ref[0], hi_ref[0])
        o_ref[0] = (mine.astype(F32) + s_ref[0].astype(F32)).astype(BF16)

    return pl.pallas_call(
        body, name=name, grid=(n, nt),
        in_specs=[pl.BlockSpec((1, tr, cols), lambda j, i: (j, i, 0)),
                  pl.BlockSpec((1, tr, cols), lambda j, i: (j, nt + i, 0)),
                  pl.BlockSpec((1, tr, cols), lambda j, i: (j, i, 0))],
        out_specs=pl.BlockSpec((1, tr, cols), lambda j, i: (j, i, 0)),
        out_shape=jax.ShapeDtypeStruct((n, half, cols), BF16),
        compiler_params=_cp(("parallel", "parallel"), VMEM_BIG))(g, g, from_sib)


def _add_chips(x, own, name):
    n, rows, cols = x.shape
    tr = _tile(rows, (256, 240))

    def body(x_ref, own_ref, o_ref):
        me = 2 * lax.axis_index("x") + lax.axis_index("y")
        v = [jnp.where(me == k, own_ref[...], x_ref[k]).astype(F32) for k in range(N_CHIPS)]
        o_ref[...] = ((v[0] + v[1]) + v[2]) + v[3]

    return pl.pallas_call(
        body, name=name, grid=(rows // tr,),
        in_specs=[pl.BlockSpec((n, tr, cols), lambda i: (0, i, 0)), pl.BlockSpec((tr, cols), lambda i: (i, 0))],
        out_specs=pl.BlockSpec((tr, cols), lambda i: (i, 0)),
        out_shape=jax.ShapeDtypeStruct((rows, cols), F32), compiler_params=_cp(("parallel",), VMEM_BIG))(x, own)


def _axes():
    return lax.axis_index("x"), lax.axis_index("y"), lax.axis_index("c")


def _other_chips(x, y):
    return [(1 - x, y), (x, 1 - y), (1 - x, 1 - y)]


ANY = pl.BlockSpec(memory_space=pl.ANY)


def _rcopy(src, dst, send_sems, recv_sems, k, to):
    return pltpu.make_async_remote_copy(src_ref=src, dst_ref=dst, send_sem=send_sems.at[k], recv_sem=recv_sems.at[k],
                                        device_id=to, device_id_type=MESH)


def _gather_weights(shards, meta):
    n = len(shards)

    def body(*refs):
        srcs, meta_ref = refs[:n], refs[n]
        outs, mout_ref = refs[n + 1:2 * n + 1], refs[2 * n + 1]
        send_sems, recv_sems = refs[2 * n + 2:]
        x, y, c = _axes()
        me = 2 * x + y
        sib = (x, y, 1 - c)
        chips = _other_chips(x, y)

        def half(t, chip_idx, cc):
            hr = shards[t].shape[0] // 2
            return outs[t].at[chip_idx, pl.ds(cc * hr, hr), :]

        first = []
        for j, (px, py) in enumerate(chips):
            for t in range(n):
                hr = shards[t].shape[0] // 2
                first.append(_rcopy(srcs[t].at[pl.ds(c * hr, hr), :], half(t, me, c), send_sems, recv_sems,
                                    3 * t + j, (px, py, c)))
            first.append(_rcopy(meta_ref, mout_ref.at[me], send_sems, recv_sems, 3 * n + j, (px, py, c)))
        for cp in first:
            cp.start()
        passed = []
        for j, (px, py) in enumerate(chips):
            src_chip = 2 * px + py
            for t in range(n):
                _rcopy(half(t, src_chip, c), half(t, src_chip, c), send_sems, recv_sems, 3 * t + j, sib).wait_recv()
                fwd = _rcopy(half(t, src_chip, c), half(t, src_chip, c), send_sems, recv_sems, 3 * (n + 1 + t) + j, sib)
                fwd.start()
                passed.append(fwd)
            _rcopy(mout_ref.at[src_chip], mout_ref.at[src_chip], send_sems, recv_sems, 3 * n + j, sib).wait_recv()
        for j, (px, py) in enumerate(chips):
            src_chip = 2 * px + py
            for t in range(n):
                _rcopy(half(t, src_chip, 1 - c), half(t, src_chip, 1 - c), send_sems, recv_sems,
                       3 * (n + 1 + t) + j, sib).wait_recv()
        for cp in first + passed:
            cp.wait_send()

    nsem = 3 * (2 * n + 1)
    return pl.pallas_call(
        body, name="gather_weights", in_specs=[ANY] * (n + 1), out_specs=[ANY] * (n + 1),
        out_shape=[jax.ShapeDtypeStruct((N_CHIPS,) + s.shape, s.dtype) for s in shards]
        + [jax.ShapeDtypeStruct((N_CHIPS,) + meta.shape, meta.dtype)],
        scratch_shapes=[pltpu.SemaphoreType.DMA((nsem,)), pltpu.SemaphoreType.DMA((nsem,))])(*shards, meta)


def _gather_late(shard):
    rows, cols = shard.shape
    hr = rows // 2
    src = jax.new_ref(shard, memory_space=pltpu.MemorySpace.HBM)
    out = jax.empty_ref(jax.ShapeDtypeStruct((N_CHIPS, rows, cols), shard.dtype), memory_space=pltpu.MemorySpace.HBM)

    @pl.kernel(mesh=plsc.ScalarSubcoreMesh(axis_name="seq", num_cores=1), name="gather_late",
               scratch_types=(pltpu.SemaphoreType.DMA((6,)), pltpu.SemaphoreType.DMA((6,))),
               compiler_params=pltpu.CompilerParams(collective_id=1))
    def launch(send_sems, recv_sems):
        x, y, c = _axes()
        me = 2 * x + y
        sib = (x, y, 1 - c)
        chips = _other_chips(x, y)
        barrier = pltpu.get_barrier_semaphore()
        for px, py in chips:
            pl.semaphore_signal(barrier, inc=1, device_id=(px, py, c), device_id_type=MESH)
        pl.semaphore_signal(barrier, inc=1, device_id=sib, device_id_type=MESH)
        pl.semaphore_wait(barrier, 4)

        def half(chip_idx, cc):
            return out.at[chip_idx, pl.ds(cc * hr, hr), :]

        first = [_rcopy(src.at[pl.ds(c * hr, hr), :], half(me, c), send_sems, recv_sems, j, (px, py, c))
                 for j, (px, py) in enumerate(chips)]
        for cp in first:
            cp.start()
        passed = []
        for j, (px, py) in enumerate(chips):
            land = half(2 * px + py, c)
            _rcopy(land, land, send_sems, recv_sems, j, sib).wait_recv()
            fwd = _rcopy(land, land, send_sems, recv_sems, 3 + j, sib)
            fwd.start()
            passed.append(fwd)
        for j, (px, py) in enumerate(chips):
            land = half(2 * px + py, 1 - c)
            _rcopy(land, land, send_sems, recv_sems, 3 + j, sib).wait_recv()
        for cp in first + passed:
            cp.wait_send()

    launch()
    return out[...]


def _swap_halves(gs):
    n = len(gs)
    ncopies = sum(g.shape[0] for g in gs)

    def body(*refs):
        srcs, outs = refs[:n], refs[n:2 * n]
        send_sems, recv_sems = refs[2 * n:]
        x, y, c = _axes()
        cps = []
        for t in range(n):
            hr = gs[t].shape[1] // 2
            for j in range(gs[t].shape[0]):
                cps.append(_rcopy(srcs[t].at[j, pl.ds((1 - c) * hr, hr), :], outs[t].at[j], send_sems, recv_sems,
                                  len(cps), (x, y, 1 - c)))
        for cp in cps:
            cp.start()
        for cp in cps:
            cp.wait()

    return pl.pallas_call(
        body, name="swap_halves", in_specs=[ANY] * n, out_specs=[ANY] * n,
        out_shape=[jax.ShapeDtypeStruct((g.shape[0], g.shape[1] // 2, g.shape[2]), g.dtype) for g in gs],
        scratch_shapes=[pltpu.SemaphoreType.DMA((ncopies,)), pltpu.SemaphoreType.DMA((ncopies,))])(*gs)


def _scatter_chips(parts):
    n = len(parts)
    srcs = [jax.new_ref(p, memory_space=pltpu.MemorySpace.HBM) for p in parts]
    outs = [jax.empty_ref(jax.ShapeDtypeStruct(p.shape, p.dtype), memory_space=pltpu.MemorySpace.HBM) for p in parts]

    @pl.kernel(mesh=plsc.ScalarSubcoreMesh(axis_name="seq", num_cores=1), name="scatter_chips",
               scratch_types=(pltpu.SemaphoreType.DMA((3 * n,)), pltpu.SemaphoreType.DMA((3 * n,))),
               compiler_params=pltpu.CompilerParams(collective_id=0))
    def launch(send_sems, recv_sems):
        x, y, c = _axes()
        me = 2 * x + y
        chips = _other_chips(x, y)
        barrier = pltpu.get_barrier_semaphore()
        for px, py in chips:
            pl.semaphore_signal(barrier, inc=1, device_id=(px, py, c), device_id_type=MESH)
        pl.semaphore_wait(barrier, 3)
        cps = []
        for j, (px, py) in enumerate(chips):
            for t in range(n):
                cps.append(_rcopy(srcs[t].at[2 * px + py], outs[t].at[me], send_sems, recv_sems, 3 * t + j,
                                  (px, py, c)))
        for cp in cps:
            cp.start()
        for cp in cps:
            cp.wait()

    launch()
    return [o[...] for o in outs]


def _swap_reduced(rs):
    n = len(rs)

    def body(*refs):
        srcs, outs = refs[:n], refs[n:2 * n]
        send_sems, recv_sems = refs[2 * n:]
        x, y, c = _axes()
        cps = [_rcopy(srcs[t], outs[t], send_sems, recv_sems, t, (x, y, 1 - c)) for t in range(n)]
        for cp in cps:
            cp.start()
        for cp in cps:
            cp.wait()

    return pl.pallas_call(
        body, name="swap_reduced", in_specs=[ANY] * n, out_specs=[ANY] * n,
        out_shape=[jax.ShapeDtypeStruct(r.shape, r.dtype) for r in rs],
        scratch_shapes=[pltpu.SemaphoreType.DMA((n,)), pltpu.SemaphoreType.DMA((n,))])(*rs)


SMALL_ROWS = 24 + 128


def _allreduce_small(vec):
    def body(v_ref, out_ref, slots, send_sems, recv_sems):
        x, y, c = _axes()
        me = 4 * x + 2 * y + c
        slots[me] = v_ref[...]
        cps = []
        for k in range(1, 8):
            kx, ky, kc = (k >> 2) & 1, (k >> 1) & 1, k & 1
            peer = (1 - x if kx else x, 1 - y if ky else y, 1 - c if kc else c)
            cps.append(_rcopy(v_ref, slots.at[me], send_sems, recv_sems, k - 1, peer))
        for cp in cps:
            cp.start()
        for cp in cps:
            cp.wait()
        tot = slots[0]
        for k in range(1, 8):
            tot = tot + slots[k]
        out_ref[...] = tot

    return pl.pallas_call(
        body, name="allreduce_small",
        in_specs=[pl.BlockSpec(memory_space=pltpu.VMEM)], out_specs=pl.BlockSpec(memory_space=pltpu.VMEM),
        out_shape=jax.ShapeDtypeStruct((SMALL_ROWS, 128), F32),
        scratch_shapes=[pltpu.VMEM((8, SMALL_ROWS, 128), F32), pltpu.SemaphoreType.DMA((7,)),
                        pltpu.SemaphoreType.DMA((7,))])(vec)


def _pack_p2(w_uq, w_ukv, w_br_mla, w_br_fox, w_out, dtype):
    parts = [w_uq.reshape(96, D_MODEL), w_ukv.reshape(64, D_MODEL), w_br_mla, w_br_fox, w_out]
    return jnp.concatenate([p.astype(dtype) for p in parts], axis=0)


def _unpack_p2(pk):
    return pk[0:96].reshape(256, 384), pk[96:160].reshape(128, 512), pk[160:416], pk[416:672], pk[672:928]


def _uq_arrange(w):
    w3 = w.reshape(256, HEADS, 96)
    nope = w3[:, :, :64].reshape(256, PAIRS, 128)
    pe = w3[:, :, 64:].reshape(256, PAIRS, 64)
    return jnp.concatenate([nope, pe, jnp.zeros((256, PAIRS, 64), w.dtype)], axis=2).reshape(256, PAIRS * 256)


def _uq_restore(g):
    g3 = g.reshape(256, PAIRS, 256)
    nope = g3[:, :, :128].reshape(256, HEADS, 64)
    pe = g3[:, :, 128:192].reshape(256, HEADS, 32)
    return jnp.concatenate([nope, pe], axis=2).reshape(256, HEADS * 96)


def _ukv_arrange(w):
    w3 = w.reshape(128, HEADS, 128)
    return jnp.concatenate([w3[:, :, :64].reshape(128, 1024), w3[:, :, 64:].reshape(128, 1024)], axis=1)


def _ukv_restore(g):
    kn = g[:, :1024].reshape(128, HEADS, 64)
    vv = g[:, 1024:].reshape(128, HEADS, 64)
    return jnp.concatenate([kn, vv], axis=2).reshape(128, HEADS * 128)


def _rope_tables(lp):
    r = np.arange(lp)
    pos = np.where(r < N_META, r, np.where(r >= PAD, r - PAD + N_META, 0)).astype(np.float32)
    half = MLA_ROPE // 2
    inv_freq = np.float32(ROPE_THETA) ** (-np.arange(half, dtype=np.float32) / np.float32(half))
    ang = (pos[:, None] * inv_freq[None, :]).astype(np.float32)
    cos, sin = np.cos(ang).astype(np.float32), np.sin(ang).astype(np.float32)
    one, zero = np.ones((lp, 64), np.float32), np.zeros((lp, 64), np.float32)
    return (jnp.asarray(np.concatenate([cos, cos, cos, cos, one], axis=1)),
            jnp.asarray(np.concatenate([-sin, sin, -sin, sin, zero], axis=1)))


def _pad_lanes(v, n=128):
    return jnp.pad(v, ((0, 0), (0, n - v.shape[1])))


def _in_cols(slabs, a, b):
    out = []
    for j in range(N_CHIPS):
        lo, hi = max(a, W_IN_SHARD * j), min(b, W_IN_SHARD * (j + 1))
        if lo < hi:
            out.append(slabs[j][:, lo - W_IN_SHARD * j:hi - W_IN_SHARD * j])
    return out


def _local_step(x2, tgt2, meta_f, w_small, w_attn, w_gate, w_uq_f, w_ukv_f, w_bm, w_bf, w_o, pre_norm_g,
                post_norm_g, mla_q_norm_g, mla_kv_norm_g, fox_forget_b, start_exchange=None):
    s_rows = x2.shape[0]
    lp = PAD + s_rows
    w_uq_a = _uq_arrange(w_uq_f)
    w_ukv_a = _ukv_arrange(w_ukv_f)

    ctab, stab = _rope_tables(lp)
    ii = jnp.arange(BLK)
    tri_lo = (ii[:, None] >= ii[None, :]).astype(BF16)
    tri_up = (ii[:, None] <= ii[None, :]).astype(BF16)
    fb128 = _pad_lanes(fox_forget_b)

    u = _rms_pre(x2, meta_f, pre_norm_g)
    small = _mm(u, w_small, mode="nn", out_dtype=F32, name="proj_small")
    attn = _mm(u, w_attn, mode="nn", out_dtype=BF16, name="proj_attn",
               col_scale=(HEADS * HEAD_DIM, FOX_SCALE * LOG2E))
    gate = _mm(u, w_gate, mode="nn", out_dtype=BF16, name="proj_gate")
    qn, kvn, kr, kb = _small_prep(small, mla_q_norm_g, mla_kv_norm_g, fb128, ctab, stab, tri_lo)
    qcat = _mm(qn, w_uq_a, mode="nn", out_dtype=BF16, name="mla_q", row_ins=(ctab, stab),
               epilogue=lambda tile, c, s: _rope_pairs(tile, c, s) * (MLA_SCALE * LOG2E))
    kv = _mm(kvn, w_ukv_a, mode="nn", out_dtype=BF16, name="mla_kv")

    mla_cols = dict(qcol=0, kcol=lambda p: p, vcol=lambda p: PAIRS + p)
    fox_cols = dict(qcol=0, kcol=lambda p: PAIRS + p, vcol=lambda p: 2 * PAIRS + p)
    o_mla, lse_mla = _attn_fwd(qcat, kv, kv, _transposed_cols(kv, 1, "mla_vt"), kr, rope=True, name="mla_fwd",
                               **mla_cols)
    o_fox, lse_fox = _attn_fwd(attn, attn, attn, _transposed_cols(attn, 2, "fox_vt"), kb, rope=False,
                               name="fox_fwd", **fox_cols)

    a_mla, a_fox = _gate_fwd(o_mla, o_fox, gate)
    y_mla = _mm(a_mla, w_bm, mode="nn", out_dtype=BF16, name="br_mla")
    y_fox = _mm(a_fox, w_bf, mode="nn", out_dtype=BF16, name="br_fox")
    mg = _merge_fwd(gate, y_mla, y_fox)
    mixed = _mm(mg, w_o, mode="nn", out_dtype=F32, name="out_proj")
    dmixed, dy, loss_p, dg_post = _tail(x2, mixed, tgt2, post_norm_g)

    d_w_out = _mm(mg, dmixed, mode="tn", out_dtype=F32, name="d_w_out")
    dm = _mm(dmixed, w_o, mode="nt", out_dtype=BF16, name="d_merge")
    dy_mla, dy_fox, dgate_ab = _merge_bwd(dm, gate, y_mla, y_fox)
    d_w_bm = _mm(a_mla, dy_mla, mode="tn", out_dtype=F32, name="d_w_br_mla")
    d_w_bf = _mm(a_fox, dy_fox, mode="tn", out_dtype=F32, name="d_w_br_fox")
    da_mla = _mm(dy_mla, w_bm, mode="nt", out_dtype=BF16, name="d_a_mla")
    da_fox = _mm(dy_fox, w_bf, mode="nt", out_dtype=BF16, name="d_a_fox")
    do_mla, do_fox, dgate_z, dl_mla, dl_fox = _gate_bwd(da_mla, da_fox, o_mla, o_fox, gate)
    dl_mla, dl_fox = (d[:, :HEADS].T.reshape(PAIRS, 2, lp) for d in (dl_mla, dl_fox))

    dq_a, dkn, dvm, dkr = _attn_bwd(qcat, kv, kv, kr, do_mla, dl_mla, lse_mla, rtabs=(ctab, stab),
                                    scale=MLA_SCALE, name="mla_bwd", **mla_cols)
    dfq, dfk, dfv, dcol, drow = _attn_bwd(attn, attn, attn, kb, do_fox, dl_fox, lse_fox, scale=FOX_SCALE,
                                          name="fox_bwd", **fox_cols)

    d_w_uq_a = _mm(qn, dq_a, mode="tn", out_dtype=F32, name="d_w_uq")
    dqn = _mm(dq_a, w_uq_a, mode="nt", out_dtype=F32, name="d_qn")
    d_w_ukv_a = jnp.concatenate([_mm(kvn, dkn, mode="tn", out_dtype=F32, name="d_w_uk"),
                                 _mm(kvn, dvm, mode="tn", out_dtype=F32, name="d_w_uv")], axis=1)
    dkvn = _mm(dkn, w_ukv_a[:, :1024], mode="nt", out_dtype=F32, name="d_kvn_k")
    dkvn = _mm(dvm, w_ukv_a[:, 1024:], mode="nt", out_dtype=F32, name="d_kvn_v", acc=dkvn)
    dsmall, dg_q, dg_kv, dfb = _small_bwd(small, dqn, dkvn, dkr, dcol, drow, mla_q_norm_g, mla_kv_norm_g,
                                          fb128, ctab, stab, tri_up)

    dw_small = _mm(u, dsmall, mode="tn", out_dtype=BF16, name="d_w_small")
    dw_fq = _mm(u, dfq, mode="tn", out_dtype=BF16, name="d_w_fq")
    dw_fk = _mm(u, dfk, mode="tn", out_dtype=BF16, name="d_w_fk")
    dw_fv = _mm(u, dfv, mode="tn", out_dtype=BF16, name="d_w_fv")
    dw_z = _mm(u, dgate_z, mode="tn", out_dtype=BF16, name="d_w_z")
    dw_g = _mm(u, dgate_ab, mode="tn", out_dtype=BF16, name="d_w_g")
    d_w_in = (dw_small, dw_z, dw_fq, dw_fk, dw_fv, dw_g)
    d_w_uq = _uq_restore(d_w_uq_a)
    d_w_ukv = _ukv_restore(d_w_ukv_a)
    token = start_exchange(d_w_in, d_w_uq, d_w_ukv, d_w_bm, d_w_bf, d_w_out) if start_exchange else None
    du = _mm_sum_nt([(dsmall, w_small), (dfq, w_attn[:, 0:1024]), (dfk, w_attn[:, 1024:2048]),
                     (dfv, w_attn[:, 2048:3072]), (dgate_z, w_gate[:, 0:2048]), (dgate_ab, w_gate[:, 2048:4096])],
                    name="d_u", after=token)
    dx, dmeta, dg_pre = _pre_bwd(du, x2, meta_f, dy, pre_norm_g)
    return (loss_p, dx, dmeta, d_w_in, d_w_uq, d_w_ukv, d_w_bm, d_w_bf, d_w_out, dg_pre, dg_post, dg_q, dg_kv, dfb)


def _w_in_slabs(pieces):
    dw_small, dw_z, dw_fq, dw_fk, dw_fv, dw_g = pieces
    runs = [(dw_small[:, 0:416], C_CQ), (dw_z[:, 0:1024], C_ZMLA), (dw_fq, C_FQ), (dw_fk, C_FK), (dw_fv, C_FV),
            (dw_small[:, 512:528], C_FL), (dw_z[:, 1024:2048], C_ZFOX), (dw_g, C_GA)]
    slabs = []
    for j in range(N_CHIPS):
        lo, hi = W_IN_SHARD * j, W_IN_SHARD * (j + 1)
        cols = [a[:, max(lo, c0) - c0:min(hi, c0 + a.shape[1]) - c0] for a, c0 in runs
                if max(lo, c0) < min(hi, c0 + a.shape[1])]
        slabs.append(jnp.concatenate(cols, axis=1))
    return jnp.stack(slabs, axis=0)


def kernel(x, meta_tokens, pre_norm_g, w_in, fox_forget_b, mla_q_norm_g, mla_kv_norm_g, w_uq, w_ukv, w_br_mla, w_br_fox, w_out, post_norm_g, loss_target, m_meta_tokens, m_pre_norm_g, m_w_in, m_fox_forget_b, m_mla_q_norm_g, m_mla_kv_norm_g, m_w_uq, m_w_ukv, m_w_br_mla, m_w_br_fox, m_w_out, m_post_norm_g, v_meta_tokens, v_pre_norm_g, v_w_in, v_fox_forget_b, v_mla_q_norm_g, v_mla_kv_norm_g, v_w_uq, v_w_ukv, v_w_br_mla, v_w_br_fox, v_w_out, v_post_norm_g):
    me = 2 * lax.axis_index("x") + lax.axis_index("y")
    core = lax.axis_index("c")
    w_in_b = w_in.astype(BF16).reshape(D_MODEL, W_IN_SHARD)
    p2 = _pack_p2(w_uq[0], w_ukv[0], w_br_mla[0], w_br_fox[0], w_out[0], BF16)
    w_in_g, meta_g = _gather_weights([w_in_b], meta_tokens)
    p2_g = _gather_late(lax.optimization_barrier((p2, w_in_g))[0])
    slabs = [jnp.where(me == j, w_in_b, w_in_g[j]) for j in range(N_CHIPS)]
    chip = lax.broadcasted_iota(jnp.int32, (N_CHIPS, 1, 1), 0)
    p2_all = jnp.where(chip == me, p2[None], p2_g)
    w_uq_f = p2_all[:, 0:96].reshape(N_CHIPS, 256, 384).transpose(1, 0, 2).reshape(256, 1536)
    w_ukv_f = p2_all[:, 96:160].reshape(N_CHIPS, 128, 512).transpose(1, 0, 2).reshape(128, 2048)
    w_bm, w_bf, w_o = (p2_all[:, lo:lo + 256].reshape(D_MODEL, D_MODEL) for lo in (160, 416, 672))
    meta_f = jnp.where(chip == me, meta_tokens[None], meta_g).transpose(1, 0, 2).reshape(N_META, D_MODEL)
    kpe = _in_cols(slabs, C_KPE, C_ZMLA)
    w_small = jnp.concatenate(_in_cols(slabs, C_CQ, C_KPE) + kpe + kpe + [jnp.zeros((D_MODEL, 64), BF16)]
                              + _in_cols(slabs, C_FL, C_ZFOX) + [jnp.zeros((D_MODEL, 112), BF16)], axis=1)
    w_attn = jnp.concatenate(_in_cols(slabs, C_FQ, C_FL), axis=1)
    w_gate = jnp.concatenate(_in_cols(slabs, C_ZMLA, C_FQ) + _in_cols(slabs, C_ZFOX, C_END), axis=1)

    exchange = {}

    def start_exchange(d_w_in, d_w_uq, d_w_ukv, d_w_bm, d_w_bf, d_w_out):
        g2 = jnp.concatenate(
            [d_w_uq.reshape(256, N_CHIPS, 384).transpose(1, 0, 2).reshape(N_CHIPS, 96, D_MODEL),
             d_w_ukv.reshape(128, N_CHIPS, 512).transpose(1, 0, 2).reshape(N_CHIPS, 64, D_MODEL)]
            + [g.reshape(N_CHIPS, 256, D_MODEL) for g in (d_w_bm, d_w_bf, d_w_out)], axis=1)
        pieces = [p[None] for p in d_w_in]
        from_sib = _swap_halves(pieces + [g2])
        halves = [_add_cores(p, s, "add_cores_" + nm)[0]
                  for p, s, nm in zip(pieces, from_sib, ("small", "z", "fq", "fk", "fv", "g"))]
        parts = [_w_in_slabs(halves), _add_cores(g2, from_sib[-1], "add_cores_rest")]
        exchange.update(parts=parts, landed=_scatter_chips(parts))
        return parts[0][0, 0:16, 0:LANES]

    (loss_p, dx, dmeta, _, _, _, _, _, _, dg_pre, dg_post, dg_q, dg_kv,
     dfb) = _local_step(x[0], loss_target[0], meta_f, w_small, w_attn, w_gate, w_uq_f, w_ukv_f, w_bm, w_bf, w_o,
                        pre_norm_g, post_norm_g, mla_q_norm_g, mla_kv_norm_g, fox_forget_b, start_exchange)

    mine = [_add_chips(l, lax.dynamic_index_in_dim(p, me, 0, keepdims=False), nm)
            for l, p, nm in zip(exchange["landed"], exchange["parts"], ("add_chips_w_in", "add_chips_rest"))]
    theirs = _swap_reduced(mine)
    g_w_in, g_p2 = [jnp.concatenate([jnp.where(core == 0, a, b), jnp.where(core == 0, b, a)], axis=0)
                    for a, b in zip(mine, theirs)]
    g_w_uq, g_w_ukv, g_w_bm, g_w_bf, g_w_out = _unpack_p2(g_p2)
    g_w_in = g_w_in[None]

    vec = jnp.concatenate([dg_pre.reshape(8, 128), dg_post.reshape(8, 128), dg_q.reshape(2, 128), dg_kv,
                           dfb, _pad_lanes(loss_p), jnp.zeros((3, 128), F32), dmeta.reshape(128, 128)], axis=0)
    tot = _allreduce_small(vec)
    loss = tot[20, 0]
    g_meta = lax.dynamic_slice_in_dim(tot[24:].reshape(N_META, D_MODEL), 256 * me, 256, axis=1)

    def small_pack(pre, post, gq_, gkv_, fb_):
        return jnp.concatenate([pre.reshape(8, 128), post.reshape(8, 128), gq_.reshape(2, 128), gkv_,
                                _pad_lanes(fb_), jnp.zeros((4, 128), F32)], axis=0)

    def small_unpack(t):
        return (t[0:8].reshape(1, 1024), t[8:16].reshape(1, 1024), t[16:18].reshape(1, 256), t[18:19],
                t[19:20, 0:HEADS])

    g_small = jnp.concatenate([tot[0:20], jnp.zeros((4, 128), F32)], axis=0)
    sm = _adamw(small_pack(pre_norm_g, post_norm_g, mla_q_norm_g, mla_kv_norm_g, fox_forget_b), g_small,
                small_pack(m_pre_norm_g, m_post_norm_g, m_mla_q_norm_g, m_mla_kv_norm_g, m_fox_forget_b),
                small_pack(v_pre_norm_g, v_post_norm_g, v_mla_q_norm_g, v_mla_kv_norm_g, v_fox_forget_b),
                "adamw_small")
    g_pre, g_post, g_q, g_kv, g_fb = small_unpack(g_small)
    (d_pre, d_post, d_q, d_kv, d_fb), (nm_pre, nm_post, nm_q, nm_kv, nm_fb), (nv_pre, nv_post, nv_q, nv_kv, nv_fb) = (
        small_unpack(t) for t in sm)

    d_meta, nm_meta, nv_meta = _adamw(meta_tokens, g_meta, m_meta_tokens, v_meta_tokens, "adamw_meta")
    d_win, nm_win, nv_win = (t.T[None] for t in _adamw(w_in[0].T, g_w_in[0].T, m_w_in[0].T, v_w_in[0].T,
                                                       "adamw_w_in"))
    d_wuq, nm_wuq, nv_wuq = _adamw(w_uq[0], g_w_uq, m_w_uq[0], v_w_uq[0], "adamw_w_uq")
    d_wukv, nm_wukv, nv_wukv = _adamw(w_ukv[0], g_w_ukv, m_w_ukv[0], v_w_ukv[0], "adamw_w_ukv")
    d_wbm, nm_wbm, nv_wbm = _adamw(w_br_mla[0], g_w_bm, m_w_br_mla[0], v_w_br_mla[0], "adamw_w_br_mla")
    d_wbf, nm_wbf, nv_wbf = _adamw(w_br_fox[0], g_w_bf, m_w_br_fox[0], v_w_br_fox[0], "adamw_w_br_fox")
    d_wo, nm_wo, nv_wo = _adamw(w_out[0], g_w_out, m_w_out[0], v_w_out[0], "adamw_w_out")

    def group(meta_, pre, win, fb_, q_, kv_, wuq, wukv, wbm, wbf, wo, post):
        return (meta_, pre, win, fb_, q_, kv_, wuq[None], wukv[None], wbm[None], wbf[None], wo[None], post)

    grads = group(g_meta, g_pre, g_w_in, g_fb, g_q, g_kv, g_w_uq, g_w_ukv, g_w_bm, g_w_bf, g_w_out, g_post)
    deltas = group(d_meta, d_pre, d_win, d_fb, d_q, d_kv, d_wuq, d_wukv, d_wbm, d_wbf, d_wo, d_post)
    new_m = group(nm_meta, nm_pre, nm_win, nm_fb, nm_q, nm_kv, nm_wuq, nm_wukv, nm_wbm, nm_wbf, nm_wo, nm_post)
    new_v = group(nv_meta, nv_pre, nv_win, nv_fb, nv_q, nv_kv, nv_wuq, nv_wukv, nv_wbm, nv_wbf, nv_wo, nv_post)
    return (loss, dx[None], *grads, *deltas, *new_m, *new_v)
```

```python
import math

import jax
import jax.numpy as jnp
import numpy as np
from jax import lax
from jax.experimental import pallas as pl
from jax.experimental.pallas import tpu as pltpu
from jax.experimental.pallas import tpu_sc as plsc

F32 = jnp.float32
BF16 = jnp.bfloat16

D_MODEL = 1024
N_META = 16
RMS_EPS = 1e-6
HEADS = 16
PAIRS = HEADS // 2
HEAD_DIM = 64
LANES = 128
MLA_ROPE = 32
AHEAD = 6
AHEAD_BWD = 1
BIAS_PARTS = 3
MLA_SCALE = 1.0 / math.sqrt(64 + 32)
FOX_SCALE = 1.0 / math.sqrt(64)
LOG2E = math.log2(math.e)
LN2 = math.log(2.0)
ROPE_THETA = 10000.0

PAD = 256
BLK = 256
QB = 512
UNROLLS = (8, 4, 2, 1)
NEG = -1e30

C_CQ, C_CKV, C_KPE, C_ZMLA, C_FQ, C_FK, C_FV, C_FL, C_ZFOX, C_GA, C_GB, C_END = (
    0, 256, 384, 416, 1440, 2464, 3488, 4512, 4528, 5552, 6576, 7600)
SMALL_W = 640
W_IN_SHARD = 1900

P2_ROWS = 928
N_CHIPS = 4

ADAM_LR = 0.001
ADAM_B1 = 0.9
ADAM_B2 = 0.999
ADAM_EPS = 1e-08
ADAM_WD = 0.01
ADAM_STEP = 10

VMEM_BIG = 56 * 1024 * 1024
MM_VMEM_BUDGET = 44 * 1024 * 1024
MESH = pl.DeviceIdType.MESH


def _cp(dims, vmem=None):
    return pltpu.CompilerParams(dimension_semantics=dims, vmem_limit_bytes=vmem)


def _dot(a, b, ca, cb):
    return lax.dot_general(a, b, (((ca,), (cb,)), ((), ())), preferred_element_type=F32)


def _sigmoid(x):
    return 1.0 / (1.0 + jnp.exp(-x))


def _tile(n, cands):
    for c in cands:
        if n % c == 0:
            return c
    return n


def _mm(a, b, *, mode, out_dtype, name, acc=None, epilogue=None, row_ins=(), after=None, col_scale=None):
    if mode == "nn":
        (M, K), N = a.shape, b.shape[1]
    elif mode == "nt":
        (M, K), N = a.shape, b.shape[0]
    else:
        (K, M), N = a.shape, b.shape[1]
    tm = _tile(M, (1088, 1024)) if M > 1024 else M
    tn = _tile(N, (1024,)) if N > 1024 else N
    nk = 1
    while True:
        tk = K // nk
        need = 2 * tk * (tm * a.dtype.itemsize + tn * b.dtype.itemsize) + tm * tn * (
            2 * jnp.dtype(out_dtype).itemsize + (8 if acc is not None else 0) + (4 if nk > 1 else 0))
        if need <= MM_VMEM_BUDGET or (tk // 2) % (16 if mode == "tn" else LANES) or tk <= 512:
            break
        nk *= 2
    while (M // tm) * (N // tn) * nk < 4 and tn % 512 == 0:
        tn //= 2
    assert col_scale is None or (nk == 1 and col_scale[0] % tn == 0)
    ca, cb = {"nn": (1, 0), "nt": (1, 1), "tn": (0, 0)}[mode]
    a_spec = (pl.BlockSpec((tk, tm), lambda j, i, k: (k, i)) if mode == "tn"
              else pl.BlockSpec((tm, tk), lambda j, i, k: (i, k)))
    b_spec = (pl.BlockSpec((tn, tk), lambda j, i, k: (j, k)) if mode == "nt"
              else pl.BlockSpec((tk, tn), lambda j, i, k: (k, j)))
    o_spec = pl.BlockSpec((tm, tn), lambda j, i, k: (i, j))
    has_acc = acc is not None

    nrow = len(row_ins)

    def body(*refs):
        a_ref, b_ref = refs[0], refs[1]
        acc_ref = refs[2] if has_acc else None
        rows = refs[2 + has_acc:2 + has_acc + nrow]
        o_ref = refs[2 + has_acc + nrow + (after is not None)]

        def store(tile):
            if epilogue is not None:
                tile = epilogue(tile, *[r[...] for r in rows])
            if col_scale is not None:
                tile = tile * jnp.where(pl.program_id(0) * tn < col_scale[0], col_scale[1], 1.0)
            o_ref[...] = tile.astype(out_dtype)

        part = _dot(a_ref[...].astype(BF16), b_ref[...].astype(BF16), ca, cb)
        if nk == 1:
            store(part + acc_ref[...] if has_acc else part)
        else:
            sc = refs[-1]
            k = pl.program_id(2)

            @pl.when(k == 0)
            def _():
                sc[...] = part + acc_ref[...] if has_acc else part

            @pl.when(k > 0)
            def _():
                sc[...] += part

            @pl.when(k == nk - 1)
            def _():
                store(sc[...])

    ins = [a, b] + ([acc] if has_acc else []) + list(row_ins)
    in_specs = ([a_spec, b_spec] + ([o_spec] if has_acc else [])
                + [pl.BlockSpec((tm, r.shape[1]), lambda j, i, k: (i, 0)) for r in row_ins])
    if after is not None:
        ins.append(after)
        in_specs.append(pl.BlockSpec(after.shape, lambda j, i, k: (0,) * after.ndim))
    return pl.pallas_call(
        body, name=name, grid=(N // tn, M // tm, nk), in_specs=in_specs, out_specs=o_spec,
        out_shape=jax.ShapeDtypeStruct((M, N), out_dtype),
        scratch_shapes=[pltpu.VMEM((tm, tn), F32)] if nk > 1 else [],
        compiler_params=_cp(("parallel", "parallel", "arbitrary"), VMEM_BIG))(*ins)


def _mm_sum_nt(pairs, *, name, after=None):
    n = len(pairs)
    M, N = pairs[0][0].shape[0], pairs[0][1].shape[0]
    tm = _tile(M, (272,))

    def body(*refs):
        o_ref = refs[2 * n + (after is not None)]
        tot = _dot(refs[0][...].astype(BF16), refs[n][...].astype(BF16), 1, 1)
        for i in range(1, n):
            tot = tot + _dot(refs[i][...].astype(BF16), refs[n + i][...].astype(BF16), 1, 1)
        o_ref[...] = tot

    ins = [a for a, _ in pairs] + [b for _, b in pairs]
    in_specs = ([pl.BlockSpec((tm, a.shape[1]), lambda i: (i, 0)) for a, _ in pairs]
                + [pl.BlockSpec(b.shape, lambda i: (0, 0)) for _, b in pairs])
    if after is not None:
        ins.append(after)
        in_specs.append(pl.BlockSpec(after.shape, lambda i: (0,) * after.ndim))
    return pl.pallas_call(
        body, name=name, grid=(M // tm,), in_specs=in_specs, out_specs=pl.BlockSpec((tm, N), lambda i: (i, 0)),
        out_shape=jax.ShapeDtypeStruct((M, N), F32), compiler_params=_cp(("parallel",), VMEM_BIG))(*ins)


def _row(w):
    return pl.BlockSpec((BLK, w), lambda i: (i, 0))


def _rowc(w, c):
    return pl.BlockSpec((BLK, w), lambda i: (i, c))


def _full(shape):
    return pl.BlockSpec(shape, lambda i: tuple(0 for _ in shape))


def _rope(x, c, s):
    lane = lax.broadcasted_iota(jnp.int32, x.shape, 1)
    is_x1 = ((lane >> 4) & 1) == 0
    partner = jnp.where(is_x1, pltpu.roll(x, LANES - 16, 1), pltpu.roll(x, 16, 1))
    return x * c + partner * s


def _row_valid(i):
    rows = i * BLK + lax.broadcasted_iota(jnp.int32, (BLK, 1), 0)
    return (rows < N_META) | (rows >= PAD)


def _shift_rows(w):
    return pl.BlockSpec((BLK, w), lambda i: (jnp.maximum(i - 1, 0), 0))


def _h_block(i, x_ref, meta_ref):
    head = jnp.concatenate([meta_ref[...], jnp.zeros((BLK - N_META, D_MODEL), F32)], axis=0)
    return jnp.where(i == 0, head, x_ref[...])


def _rms_pre(x2, meta, g):
    lp = PAD + x2.shape[0]

    def body(x_ref, meta_ref, g_ref, u_ref):
        hv = _h_block(pl.program_id(0), x_ref, meta_ref)
        r = lax.rsqrt(jnp.mean(hv * hv, axis=-1, keepdims=True) + RMS_EPS)
        u_ref[...] = (hv * r * g_ref[...]).astype(BF16)

    return pl.pallas_call(
        body, name="rms_pre", grid=(lp // BLK,),
        in_specs=[_shift_rows(D_MODEL), _full((N_META, D_MODEL)), _full((1, D_MODEL))], out_specs=_row(D_MODEL),
        out_shape=jax.ShapeDtypeStruct((lp, D_MODEL), BF16),
        compiler_params=_cp(("parallel",)))(x2, meta, g)


def _split3(x):
    hi = x.astype(BF16)
    r1 = x - hi.astype(F32)
    mid = r1.astype(BF16)
    lo = (r1 - mid.astype(F32)).astype(BF16)
    return hi, mid, lo


def _small_prep(small, gq, gkv, fb, ctab, stab, tri):
    lp = small.shape[0]

    def body(sm_ref, gq_ref, gkv_ref, fb_ref, c_ref, s_ref, tri_ref, qn_ref, kvn_ref, kr_ref, kb_ref, carry):
        i = pl.program_id(0)

        @pl.when(i == 0)
        def _():
            carry[...] = jnp.zeros_like(carry)

        cq = sm_ref[:, 0:256]
        r = lax.rsqrt(jnp.mean(cq * cq, axis=-1, keepdims=True) + RMS_EPS)
        qn_ref[...] = (cq * r * gq_ref[...]).astype(BF16)
        ckv = sm_ref[:, 256:384]
        r = lax.rsqrt(jnp.mean(ckv * ckv, axis=-1, keepdims=True) + RMS_EPS)
        kvn_ref[...] = (ckv * r * gkv_ref[...]).astype(BF16)
        kr_ref[...] = _rope(sm_ref[:, 384:512], c_ref[...], s_ref[...]).astype(BF16)
        fl = sm_ref[:, 512:640] + fb_ref[...]
        lf = jnp.minimum(fl, 0.0) - jnp.log(1.0 + jnp.exp(-jnp.abs(fl)))
        lf = jnp.where(_row_valid(i), lf, 0.0)
        hi, mid, lo = _split3(lf)
        t = tri_ref[...]
        cum = (_dot(t, hi, 1, 0) + _dot(t, mid, 1, 0)) + _dot(t, lo, 1, 0) + carry[...]
        carry[...] = cum[BLK - 1:BLK, :]
        src = lax.broadcasted_iota(jnp.int32, (LANES, LANES), 0)
        dst = lax.broadcasted_iota(jnp.int32, (LANES, LANES), 1)
        kb = jnp.zeros((BLK, LANES), F32)
        for j, part in enumerate(_split3(-cum * LOG2E)):
            spread = ((dst == BIAS_PARTS * src + j) & (src < HEADS)).astype(BF16)
            kb = kb + _dot(part, spread, 1, 0)
        kb_ref[...] = kb.astype(BF16)

    return pl.pallas_call(
        body, name="small_prep", grid=(lp // BLK,),
        in_specs=[_row(SMALL_W), _full((1, 256)), _full((1, 128)), _full((1, 128)), _row(128), _row(128),
                  _full((BLK, BLK))],
        out_specs=[_row(256), _row(128), _row(128), _row(128)],
        out_shape=[jax.ShapeDtypeStruct((lp, 256), BF16), jax.ShapeDtypeStruct((lp, 128), BF16),
                   jax.ShapeDtypeStruct((lp, 128), BF16), jax.ShapeDtypeStruct((lp, 128), BF16)],
        scratch_shapes=[pltpu.VMEM((1, 128), F32)],
        compiler_params=_cp(("arbitrary",)))(small, gq, gkv, fb, ctab, stab, tri)


def _rope_pairs(tile, c, s):
    out = []
    for lo in range(0, tile.shape[1], 256):
        out += [tile[:, lo:lo + 128], _rope(tile[:, lo + 128:lo + 256], c, s)]
    return jnp.concatenate(out, axis=1)


def _gate_fwd(o_mla, o_fox, gate):
    lp = o_mla.shape[0]

    def body(om_ref, of_ref, zm_ref, zf_ref, am_ref, af_ref):
        zm = zm_ref[...].astype(F32)
        am_ref[...] = (om_ref[...] * (zm * _sigmoid(zm))).astype(BF16)
        zf = zf_ref[...].astype(F32)
        af_ref[...] = (of_ref[...] * (zf * _sigmoid(zf))).astype(BF16)

    return pl.pallas_call(
        body, name="gate_fwd", grid=(lp // BLK,),
        in_specs=[_row(D_MODEL), _row(D_MODEL), _rowc(D_MODEL, 0), _rowc(D_MODEL, 1)],
        out_specs=[_row(D_MODEL), _row(D_MODEL)],
        out_shape=[jax.ShapeDtypeStruct((lp, D_MODEL), BF16)] * 2,
        compiler_params=_cp(("parallel",)))(o_mla, o_fox, gate, gate)


def _merge_fwd(gate, y_mla, y_fox):
    lp = y_mla.shape[0]

    def body(ga_ref, gb_ref, ym_ref, yf_ref, m_ref):
        sa = _sigmoid(ga_ref[...].astype(F32))
        sb = _sigmoid(gb_ref[...].astype(F32))
        m_ref[...] = (sa * ym_ref[...] + sb * yf_ref[...]).astype(BF16)

    return pl.pallas_call(
        body, name="merge_fwd", grid=(lp // BLK,),
        in_specs=[_rowc(D_MODEL, 2), _rowc(D_MODEL, 3), _row(D_MODEL), _row(D_MODEL)],
        out_specs=_row(D_MODEL), out_shape=jax.ShapeDtypeStruct((lp, D_MODEL), BF16),
        compiler_params=_cp(("parallel",)))(gate, gate, y_mla, y_fox)


def _tail(x2, mixed, tgt, gpost):
    lp = mixed.shape[0]
    shift = _shift_rows(D_MODEL)

    def body(h_ref, mx_ref, t_ref, g_ref, dmx_ref, dy_ref, loss_ref, dg_ref):
        i = pl.program_id(0)

        @pl.when(i == 0)
        def _():
            loss_ref[...] = jnp.zeros_like(loss_ref)
            dg_ref[...] = jnp.zeros_like(dg_ref)
            dmx_ref[...] = jnp.zeros_like(dmx_ref)
            dy_ref[...] = jnp.zeros_like(dy_ref)

        @pl.when(i > 0)
        def _():
            mx = mx_ref[...]
            g = g_ref[...]
            r = lax.rsqrt(jnp.mean(mx * mx, axis=-1, keepdims=True) + RMS_EPS)
            nrm = mx * r
            e = (h_ref[...] + nrm * g) - t_ref[...]
            loss_ref[...] += jnp.sum(0.5 * jnp.sum(e * e, axis=-1, keepdims=True) * (1.0 / D_MODEL),
                                     axis=0, keepdims=True)
            dy = e * (1.0 / D_MODEL)
            dy_ref[...] = dy
            dg_ref[...] += jnp.sum(dy * nrm, axis=0, keepdims=True)
            w = dy * g
            dot = jnp.mean(w * mx, axis=-1, keepdims=True)
            dmx_ref[...] = (r * w - mx * (r * r * r * dot)).astype(BF16)

    return pl.pallas_call(
        body, name="tail", grid=(lp // BLK,),
        in_specs=[shift, _row(D_MODEL), shift, _full((1, D_MODEL))],
        out_specs=[_row(D_MODEL), _row(D_MODEL), _full((1, 1)), _full((1, D_MODEL))],
        out_shape=[jax.ShapeDtypeStruct((lp, D_MODEL), BF16), jax.ShapeDtypeStruct((lp, D_MODEL), F32),
                   jax.ShapeDtypeStruct((1, 1), F32), jax.ShapeDtypeStruct((1, D_MODEL), F32)],
        compiler_params=_cp(("arbitrary",)))(x2, mixed, tgt, gpost)


def _merge_bwd(dm, gate, y_mla, y_fox):
    lp = dm.shape[0]

    def body(dm_ref, ga_ref, gb_ref, ym_ref, yf_ref, dym_ref, dyf_ref, dg_ref):
        dm_v = dm_ref[...].astype(F32)
        sa = _sigmoid(ga_ref[...].astype(F32))
        sb = _sigmoid(gb_ref[...].astype(F32))
        dym_ref[...] = (dm_v * sa).astype(BF16)
        dyf_ref[...] = (dm_v * sb).astype(BF16)
        dg_ref[:, 0:D_MODEL] = (dm_v * ym_ref[...] * (sa * (1.0 - sa))).astype(BF16)
        dg_ref[:, D_MODEL:2 * D_MODEL] = (dm_v * yf_ref[...] * (sb * (1.0 - sb))).astype(BF16)

    return pl.pallas_call(
        body, name="merge_bwd", grid=(lp // BLK,),
        in_specs=[_row(D_MODEL), _rowc(D_MODEL, 2), _rowc(D_MODEL, 3), _row(D_MODEL), _row(D_MODEL)],
        out_specs=[_row(D_MODEL), _row(D_MODEL), _row(2 * D_MODEL)],
        out_shape=[jax.ShapeDtypeStruct((lp, D_MODEL), BF16), jax.ShapeDtypeStruct((lp, D_MODEL), BF16),
                   jax.ShapeDtypeStruct((lp, 2 * D_MODEL), BF16)],
        compiler_params=_cp(("parallel",)))(dm, gate, gate, y_mla, y_fox)


def _gate_bwd(da_mla, da_fox, o_mla, o_fox, gate):
    lp = da_mla.shape[0]

    def one(da, o, z, head_of_col):
        sg = _sigmoid(z)
        do = (da * (z * sg)).astype(BF16)
        dz = da * o * (sg * (1.0 + z * (1.0 - sg)))
        delta = sum(_dot(part, head_of_col, 1, 0) for part in _split3(do.astype(F32) * o))
        return do, dz.astype(BF16), delta

    def body(dam_ref, daf_ref, om_ref, of_ref, zm_ref, zf_ref, dom_ref, dof_ref, dz_ref, dlm_ref, dlf_ref):
        f32 = lambda r: r[...].astype(F32)
        head_of_col = (lax.broadcasted_iota(jnp.int32, (D_MODEL, LANES), 0) // HEAD_DIM
                       == lax.broadcasted_iota(jnp.int32, (D_MODEL, LANES), 1)).astype(BF16)
        dom_ref[...], dz_ref[:, 0:D_MODEL], dlm_ref[...] = one(f32(dam_ref), f32(om_ref), f32(zm_ref), head_of_col)
        dof_ref[...], dz_ref[:, D_MODEL:2 * D_MODEL], dlf_ref[...] = one(f32(daf_ref), f32(of_ref), f32(zf_ref),
                                                                        head_of_col)

    return pl.pallas_call(
        body, name="gate_bwd", grid=(lp // BLK,),
        in_specs=[_row(D_MODEL)] * 4 + [_rowc(D_MODEL, 0), _rowc(D_MODEL, 1)],
        out_specs=[_row(D_MODEL), _row(D_MODEL), _row(2 * D_MODEL), _row(LANES), _row(LANES)],
        out_shape=[jax.ShapeDtypeStruct((lp, D_MODEL), BF16), jax.ShapeDtypeStruct((lp, D_MODEL), BF16),
                   jax.ShapeDtypeStruct((lp, 2 * D_MODEL), BF16), jax.ShapeDtypeStruct((lp, LANES), F32),
                   jax.ShapeDtypeStruct((lp, LANES), F32)],
        compiler_params=_cp(("parallel",)))(da_mla, da_fox, o_mla, o_fox, gate, gate)


def _small_bwd(small, dqn, dkvn, dkr, dcol_t, drow_t, gq, gkv, fb, ctab, stab, triu):
    lp = small.shape[0]
    nb = lp // BLK

    def rrow(w):
        return pl.BlockSpec((BLK, w), lambda i: (nb - 1 - i, 0))

    def body(sm_ref, dqn_ref, dkvn_ref, dkr_ref, dcol_ref, drow_ref, gq_ref, gkv_ref, fb_ref, c_ref, s_ref, tri_ref,
             ds_ref, dgq_ref, dgkv_ref, dfb_ref, carry):
        i = pl.program_id(0)

        @pl.when(i == 0)
        def _():
            carry[...] = jnp.zeros_like(carry)
            dgq_ref[...] = jnp.zeros_like(dgq_ref)
            dgkv_ref[...] = jnp.zeros_like(dgkv_ref)
            dfb_ref[...] = jnp.zeros_like(dfb_ref)

        def norm_bwd(x, dn, g, dg_ref):
            r = lax.rsqrt(jnp.mean(x * x, axis=-1, keepdims=True) + RMS_EPS)
            dg_ref[...] += jnp.sum(dn * (x * r), axis=0, keepdims=True)
            w = dn * g
            dot = jnp.mean(w * x, axis=-1, keepdims=True)
            return r * w - x * (r * r * r * dot)

        ds_ref[:, 0:256] = norm_bwd(sm_ref[:, 0:256], dqn_ref[...], gq_ref[...], dgq_ref).astype(BF16)
        ds_ref[:, 256:384] = norm_bwd(sm_ref[:, 256:384], dkvn_ref[...], gkv_ref[...], dgkv_ref).astype(BF16)

        dk = dkr_ref[0]
        for p in range(1, PAIRS):
            dk = dk + dkr_ref[p]
        dk = _rope(dk, c_ref[...], -s_ref[...])
        lane = lax.broadcasted_iota(jnp.int32, dk.shape, 1)
        dk = jnp.where(lane < MLA_ROPE, dk + pltpu.roll(dk, LANES - MLA_ROPE, 1), 0.0)
        ds_ref[:, 384:512] = dk.astype(BF16)

        dcol = dcol_ref[0]
        for p in range(1, PAIRS):
            dcol = dcol + pltpu.roll(dcol_ref[p], 2 * p, 1)
        rows16 = jnp.concatenate([drow_ref[p, h:h + 1, :] for p in range(PAIRS) for h in range(2)], axis=0)
        eye = (lax.broadcasted_iota(jnp.int32, (HEADS, LANES), 0)
               == lax.broadcasted_iota(jnp.int32, (HEADS, LANES), 1)).astype(BF16)
        drow = sum(_dot(part, eye, 0, 0) for part in _split3(rows16))
        dcr = dcol - drow
        hi, mid, lo = _split3(dcr)
        t = tri_ref[...]
        suf = (_dot(t, hi, 1, 0) + _dot(t, mid, 1, 0)) + _dot(t, lo, 1, 0) + carry[...]
        fl = sm_ref[:, 512:640] + fb_ref[...]
        dfl = jnp.where(_row_valid(nb - 1 - i), -suf * _sigmoid(-fl), 0.0)
        ds_ref[:, 512:640] = dfl.astype(BF16)
        dfb_ref[...] += jnp.sum(dfl, axis=0, keepdims=True)
        carry[...] += jnp.sum(dcr, axis=0, keepdims=True)

    return pl.pallas_call(
        body, name="small_bwd", grid=(nb,),
        in_specs=[rrow(SMALL_W), rrow(256), rrow(128),
                  pl.BlockSpec((PAIRS, BLK, 128), lambda i: (0, nb - 1 - i, 0)),
                  pl.BlockSpec((PAIRS, BLK, 128), lambda i: (0, nb - 1 - i, 0)),
                  pl.BlockSpec((PAIRS, 2, BLK), lambda i: (0, 0, nb - 1 - i)),
                  _full((1, 256)), _full((1, 128)), _full((1, 128)), rrow(128), rrow(128), _full((BLK, BLK))],
        out_specs=[rrow(SMALL_W), _full((1, 256)), _full((1, 128)), _full((1, 128))],
        out_shape=[jax.ShapeDtypeStruct((lp, SMALL_W), BF16), jax.ShapeDtypeStruct((1, 256), F32),
                   jax.ShapeDtypeStruct((1, 128), F32), jax.ShapeDtypeStruct((1, 128), F32)],
        scratch_shapes=[pltpu.VMEM((1, 128), F32)],
        compiler_params=_cp(("arbitrary",)))(small, dqn, dkvn, dkr, dcol_t, drow_t, gq, gkv, fb, ctab, stab, triu)


def _pre_bwd(du, x2, meta, dy, gpre):
    s_rows = x2.shape[0]
    lp = PAD + s_rows
    shift = _shift_rows(D_MODEL)

    def body(du_ref, x_ref, meta_ref, dy_ref, g_ref, dx_ref, dmeta_ref, dg_ref):
        i = pl.program_id(0)

        @pl.when(i == 0)
        def _():
            dg_ref[...] = jnp.zeros_like(dg_ref)

        hv = _h_block(i, x_ref, meta_ref)
        duv = du_ref[...]
        r = lax.rsqrt(jnp.mean(hv * hv, axis=-1, keepdims=True) + RMS_EPS)
        dg_ref[...] += jnp.sum(duv * (hv * r), axis=0, keepdims=True)
        w = duv * g_ref[...]
        dot = jnp.mean(w * hv, axis=-1, keepdims=True)
        dh = dy_ref[...] + (r * w - hv * (r * r * r * dot))
        dx_ref[...] = dh

        @pl.when(i == 0)
        def _():
            dmeta_ref[...] = dh[0:N_META, :]

    return pl.pallas_call(
        body, name="pre_bwd", grid=(lp // BLK,),
        in_specs=[_row(D_MODEL), shift, _full((N_META, D_MODEL)), _row(D_MODEL), _full((1, D_MODEL))],
        out_specs=[shift, _full((N_META, D_MODEL)), _full((1, D_MODEL))],
        out_shape=[jax.ShapeDtypeStruct((s_rows, D_MODEL), F32), jax.ShapeDtypeStruct((N_META, D_MODEL), F32),
                   jax.ShapeDtypeStruct((1, D_MODEL), F32)],
        compiler_params=_cp(("arbitrary",)))(du, x2, meta, dy, gpre)


def _pair_masks(rope, pair):
    lane = lax.broadcasted_iota(jnp.int32, (1, LANES), 1)
    mas = [lane < HEAD_DIM, lane >= HEAD_DIM]
    wide = lax.broadcasted_iota(jnp.int32, (1, 2 * LANES), 1)
    extra = MLA_ROPE if rope else BIAS_PARTS
    lo = LANES if rope else LANES + 2 * BIAS_PARTS * pair
    mid = lo + extra
    return mas, [(wide < HEAD_DIM) | ((wide >= lo) & (wide < mid)),
                 ((wide >= HEAD_DIM) & (wide < LANES)) | ((wide >= mid) & (wide < mid + extra))]


def _mask2(x, masks):
    return [jnp.where(m, x, jnp.zeros_like(x)) for m in masks]


def _q_heads(q_rows, rope, mas, hmask):
    if rope:
        return _mask2(q_rows, hmask)
    zero = jnp.zeros((q_rows.shape[0], LANES), BF16)
    return [jnp.concatenate([jnp.where(m, q_rows, zero), jnp.where(hm[:, LANES:], zero + 1, zero)], axis=1)
            for m, hm in zip(mas, hmask)]


def _transposed_cols(x, group, name):
    lp = x.shape[0]

    def body(x_ref, o_ref):
        o_ref[...] = x_ref[...].T

    cols = 2 * BLK
    per_group = D_MODEL // cols
    return pl.pallas_call(
        body, name=name, grid=(per_group,),
        in_specs=[pl.BlockSpec((lp, cols), lambda i: (0, per_group * group + i))],
        out_specs=pl.BlockSpec((cols, lp), lambda i: (i, 0)),
        out_shape=jax.ShapeDtypeStruct((D_MODEL, lp), x.dtype),
        compiler_params=_cp(("parallel",), MM_VMEM_BUDGET))(x)


def _attn_fwd(q, k, v, vt, k2, *, rope, qcol, kcol, vcol, name):
    lp = q.shape[0]
    nq = 1 + (lp - PAD) // QB
    qw = 256 if rope else 128

    def body(q_ref, k_ref, v_ref, vt_ref, k2_ref, o_ref, lse_ref):
        i = pl.program_id(1)
        r0 = pl.multiple_of(jnp.where(i == 0, 0, PAD + QB * (i - 1)), BLK)
        b0 = r0 // BLK
        mas, hmask = _pair_masks(rope, pl.program_id(0))
        qh = _q_heads(q_ref[pl.ds(r0, QB), :], rope, mas, hmask)

        def update(chunks, tiles, groups):
            m = [[cr[0], cr[2]] for _, _, cr in groups]
            l = [[cr[1], cr[3]] for _, _, cr in groups]
            acc = [[cr[4][0:HEAD_DIM], cr[4][HEAD_DIM:LANES]] for _, _, cr in groups]
            qs = [[x[q_lo:q_lo + wq] for x in qh] for q_lo, wq, _ in groups]
            k0s = [pl.multiple_of(kc * BLK, BLK) for kc, _ in chunks]
            kks = [jnp.concatenate([k_ref[pl.ds(k0, n), :], k2_ref[pl.ds(k0, n), :]], axis=1)
                   for k0, (_, n) in zip(k0s, chunks)]
            jobs = [(g, ci, mask, h) for h in range(2) for g, ci, mask in tiles]
            score = lambda t: _dot(kks[jobs[t][1]], qs[jobs[t][0]][jobs[t][3]], 1, 1)
            ss = [score(t) for t in range(min(AHEAD, len(jobs)))]
            for t, (g, ci, mask, h) in enumerate(jobs):
                if t + AHEAD < len(jobs):
                    ss.append(score(t + AHEAD))
                s = ss[t] if mask is None else jnp.where(mask, ss[t], NEG)
                m_new = jnp.maximum(m[g][h], jnp.max(s, axis=0, keepdims=True))
                alpha = jnp.exp2(m[g][h] - m_new)
                p = jnp.exp2(s - m_new)
                l[g][h] = alpha * l[g][h] + jnp.sum(p, axis=0, keepdims=True)
                m[g][h] = m_new
                n = chunks[ci][1]
                if n == BLK:
                    pv = _dot(vt_ref[pl.ds(HEAD_DIM * h, HEAD_DIM), pl.ds(k0s[ci], BLK)], p.astype(BF16), 1, 0)
                else:
                    vm = jnp.where(mas[h], v_ref[0:n, :], jnp.zeros((), BF16))
                    pv = _dot(vm, p.astype(BF16), 0, 0)[HEAD_DIM * h:HEAD_DIM * (h + 1)]
                acc[g][h] = alpha * acc[g][h] + pv
            return [(m[g][0], l[g][0], m[g][1], l[g][1], jnp.concatenate(acc[g], axis=0)) for g in range(len(groups))]

        def full_chunks(kcs, carry):
            return update([(kc, BLK) for kc in kcs], [(0, ci, None) for ci in range(len(kcs))], [(0, QB, carry)])[0]

        neg = jnp.full((1, QB), NEG, F32)
        zero = jnp.zeros((1, QB), F32)
        c = (neg, zero, neg, zero, jnp.zeros((LANES, QB), F32))
        n_mid = jnp.maximum(b0 - 1, 0)
        c = lax.fori_loop(0, n_mid // 4, lambda t, cr: full_chunks([4 * t + u for u in (1, 2, 3, 4)], cr), c)
        c = lax.fori_loop(0, (n_mid % 4) // 2, lambda t, cr: full_chunks([n_mid - 1, n_mid], cr), c)
        key_l = lax.broadcasted_iota(jnp.int32, (BLK, BLK), 0)
        qry_l = lax.broadcasted_iota(jnp.int32, (BLK, BLK), 1)
        tri = (key_l <= qry_l) & (b0 > 0)
        meta_ok = (key_l[0:N_META] <= qry_l[0:N_META]) | (b0 > 0)
        lo, hi = update([(0, N_META), (b0, BLK), (b0 + 1, BLK)],
                        [(0, 0, meta_ok), (1, 0, None), (0, 1, tri), (1, 1, None), (1, 2, tri)],
                        [(0, BLK, tuple(a[:, 0:BLK] for a in c)), (BLK, QB - BLK, tuple(a[:, BLK:QB] for a in c))])
        c = tuple(jnp.concatenate([a, b], axis=1) for a, b in zip(lo, hi))
        inv =jnp.concatenate([jnp.broadcast_to(1.0 / c[1], (HEAD_DIM, QB)),
                               jnp.broadcast_to(1.0 / c[3], (HEAD_DIM, QB))], axis=0)
        o_t = (c[4] * inv).T.astype(BF16)
        lses = [c[2 * h] + jnp.log(c[2 * h + 1]) * LOG2E for h in range(2)]
        o_ref[pl.ds(r0, BLK), :] = o_t[0:BLK]
        for h in range(2):
            lse_ref[0, h:h + 1, pl.ds(r0, BLK)] = lses[h][:, 0:BLK]

        @pl.when(i > 0)
        def _():
            r1 = pl.multiple_of(r0 + BLK, BLK)
            o_ref[pl.ds(r1, QB - BLK), :] = o_t[BLK:QB]
            for h in range(2):
                lse_ref[0, h:h + 1, pl.ds(r1, QB - BLK)] = lses[h][:, BLK:QB]

    in_specs = [pl.BlockSpec((lp, qw), lambda p, i: (0, qcol + p)),
                pl.BlockSpec((lp, 128), lambda p, i: (0, kcol(p))),
                pl.BlockSpec((BLK, 128), lambda p, i: (0, vcol(p))),
                pl.BlockSpec((128, lp), lambda p, i: (p, 0)),
                pl.BlockSpec((lp, 128), lambda p, i: (0, 0))]
    return pl.pallas_call(
        body, name=name, grid=(PAIRS, nq), in_specs=in_specs,
        out_specs=[pl.BlockSpec((lp, 128), lambda p, i: (0, p)),
                   pl.BlockSpec((1, 2, lp), lambda p, i: (p, 0, 0))],
        out_shape=[jax.ShapeDtypeStruct((lp, D_MODEL), BF16), jax.ShapeDtypeStruct((PAIRS, 2, lp), F32)],
        compiler_params=_cp(("parallel", "arbitrary"), VMEM_BIG))(q, k, v, vt, k2)


def _attn_bwd(q, k, v, k2, do, delta, lse, *, rtabs=None, scale, qcol, kcol, vcol, name):
    lp = q.shape[0]
    nb = lp // BLK
    rope = rtabs is not None
    bias = not rope
    qw = 256 if rope else 128

    def body(*refs):
        it = iter(refs)
        q_ref, k_ref, v_ref, k2_ref = next(it), next(it), next(it), next(it)
        do_ref, dl_ref, lse_ref = next(it), next(it), next(it)
        ct_ref, st_ref = (next(it), next(it)) if rope else (None, None)
        dq_out, dk_ref, dv_ref = next(it), next(it), next(it)
        x_ref = next(it)
        drow_ref = next(it) if bias else None
        dq_ref = next(it)
        kb = pl.program_id(1)
        mas, hmask = _pair_masks(rope, pl.program_id(0))
        lane = lax.broadcasted_iota(jnp.int32, (1, LANES), 1)

        @pl.when(kb == 0)
        def _():
            dq_ref[...] = jnp.zeros_like(dq_ref)
            if bias:
                drow_ref[...] = jnp.zeros_like(drow_ref)

        def key_pass(n, w):
            kk = jnp.concatenate([k_ref[0:n, :], k2_ref[0:n, :]], axis=1)
            vh = _mask2(v_ref[0:n, :], mas)
            kcat = jnp.concatenate([x[:, 0:qw] for x in _mask2(kk, hmask)], axis=0)
            diag_mask = (lax.broadcasted_iota(jnp.int32, (n, w), 0) <= lax.broadcasted_iota(jnp.int32, (n, w), 1))

            def front(qc):
                q0 = qc * w if isinstance(qc, int) else pl.multiple_of(qc * w, w)
                dov = do_ref[pl.ds(q0, w), :]
                qh = _q_heads(q_ref[pl.ds(q0, w), :], rope, mas, hmask)
                ss, dps = [], []
                for h in range(2):
                    ss.append(_dot(kk, qh[h], 1, 1))
                    dps.append(_dot(vh[h], dov, 1, 1))
                return q0, dov, qh, ss, dps

            def back(fronted, carry, mask):
                carry = list(carry)
                q0, dov, qh, ss, dps = fronted
                doh = _mask2(dov, mas)
                pbs, dss = [], []
                for h in range(2):
                    p = jnp.exp2(ss[h] - lse_ref[0, h:h + 1, pl.ds(q0, w)])
                    if mask is not None:
                        p = jnp.where(mask, p, 0.0)
                    ds = p * (dps[h] - dl_ref[0, h:h + 1, pl.ds(q0, w)])
                    if bias:
                        drow_ref[0, h:h + 1, pl.ds(q0, w)] += jnp.sum(ds, axis=0, keepdims=True)
                        carry[2 + h] = carry[2 + h] + jnp.sum(ds, axis=1, keepdims=True)
                    pbs.append(p.astype(BF16))
                    dss.append(ds.astype(BF16))
                ds_lanes = jnp.concatenate(dss, axis=1)
                ds_rows = jnp.concatenate(dss, axis=0)
                qcat = jnp.concatenate([x[:, 0:qw] for x in qh], axis=0)
                carry[0] = carry[0] + _dot(ds_lanes, qcat, 1, 0)
                carry[1] = carry[1] + _dot(jnp.concatenate(pbs, axis=1), jnp.concatenate(doh, axis=0), 1, 0)
                dq_ref[pl.ds(q0, w), :] += _dot(ds_rows, kcat, 0, 0)
                return tuple(carry)

            def chunks(qcs, carry, masks):
                ahead = AHEAD_BWD
                fronted = [front(qc) for qc in qcs[:ahead]]
                for u, mask in enumerate(masks):
                    if u + ahead < len(qcs):
                        fronted.append(front(qcs[u + ahead]))
                    carry = back(fronted[u], carry, mask)
                return carry

            c = [jnp.zeros((n, qw), F32), jnp.zeros((n, LANES), F32)]
            if bias:
                c += [jnp.zeros((n, 1), F32), jnp.zeros((n, 1), F32)]
            c = tuple(c)
            if w != BLK:
                c = chunks(list(range(lp // w)), c, [diag_mask] + [None] * (lp // w - 1))
            else:
                start = kb
                for width in UNROLLS:

                    def several(t, cr, start=start, width=width):
                        qc = start + width * t
                        return chunks([qc + u for u in range(width)], cr, [diag_mask | (qc > kb)] + [None] * (width - 1))

                    trips = (nb - start) // width
                    c = lax.fori_loop(0, trips, several, c)
                    start = start + width * trips

            def rows(a, dtype):
                a = a.astype(dtype)
                return a if n == BLK else jnp.concatenate([a, jnp.zeros((BLK - n, a.shape[1]), dtype)], axis=0)

            dk = c[0] * LN2
            dk_ref[...] = rows(dk[:, 0:LANES], BF16)
            dv_ref[...] = rows(c[1], BF16)
            if rope:
                x_ref[0] = rows(dk[:, LANES:2 * LANES], F32)
            if bias:
                x_ref[0] = rows(jnp.where(lane == 0, c[2], jnp.where(lane == 1, c[3], 0.0)), F32)

        @pl.when(kb == 0)
        def _():
            key_pass(N_META, lp // 2)

        @pl.when(kb > 0)
        def _():
            key_pass(BLK, BLK)

        @pl.when(kb == nb - 1)
        def _():
            def fin(c, carry):
                r0 = pl.multiple_of(c * BLK, BLK)
                dq = dq_ref[pl.ds(r0, BLK), :] * scale
                if rope:
                    back = _rope(dq[:, LANES:2 * LANES], ct_ref[pl.ds(r0, BLK), :], -st_ref[pl.ds(r0, BLK), :])
                    dq = jnp.concatenate([dq[:, 0:LANES], back], axis=1)
                dq_out[pl.ds(r0, BLK), :] = dq.astype(BF16)
                return carry

            lax.fori_loop(0, nb, fin, 0)

    in_specs = [pl.BlockSpec((lp, qw), lambda p, j: (0, qcol + p)),
                pl.BlockSpec((BLK, 128), lambda p, j: (j, kcol(p))),
                pl.BlockSpec((BLK, 128), lambda p, j: (j, vcol(p))),
                pl.BlockSpec((BLK, 128), lambda p, j: (j, 0)),
                pl.BlockSpec((lp, 128), lambda p, j: (0, p)), pl.BlockSpec((1, 2, lp), lambda p, j: (p, 0, 0)),
                pl.BlockSpec((1, 2, lp), lambda p, j: (p, 0, 0))]
    ins = [q, k, v, k2, do, delta, lse]
    if rope:
        in_specs += [pl.BlockSpec((lp, 128), lambda p, j: (0, 0))] * 2
        ins += list(rtabs)
    out_specs = [pl.BlockSpec((lp, qw), lambda p, j: (0, p)),
                 pl.BlockSpec((BLK, 128), lambda p, j: (j, p)),
                 pl.BlockSpec((BLK, 128), lambda p, j: (j, p)),
                 pl.BlockSpec((1, BLK, 128), lambda p, j: (p, j, 0))]
    out_shape = [jax.ShapeDtypeStruct((lp, PAIRS * qw), BF16), jax.ShapeDtypeStruct((lp, D_MODEL), BF16),
                 jax.ShapeDtypeStruct((lp, D_MODEL), BF16), jax.ShapeDtypeStruct((PAIRS, lp, 128), F32)]
    if bias:
        out_specs.append(pl.BlockSpec((1, 2, lp), lambda p, j: (p, 0, 0)))
        out_shape.append(jax.ShapeDtypeStruct((PAIRS, 2, lp), F32))
    return pl.pallas_call(
        body, name=name, grid=(PAIRS, nb), in_specs=in_specs, out_specs=out_specs, out_shape=out_shape,
        scratch_shapes=[pltpu.VMEM((lp, qw), F32)],
        compiler_params=_cp(("parallel", "arbitrary"), VMEM_BIG))(*ins)


def _adamw(w, g, m, v, name):
    lead = w.ndim - 2
    rows, cols = w.shape[lead:]
    big = rows * cols > 512 * 1024
    tr = 128 if big and rows % 128 == 0 else rows
    tc = 256 if big and tr == rows else cols

    def body(w_ref, g_ref, m_ref, v_ref, d_ref, nm_ref, nv_ref):
        gv = g_ref[...]
        nm = ADAM_B1 * m_ref[...] + (1.0 - ADAM_B1) * gv
        nv = ADAM_B2 * v_ref[...] + (1.0 - ADAM_B2) * (gv * gv)
        m_hat = nm / (1.0 - ADAM_B1 ** ADAM_STEP)
        v_hat = nv / (1.0 - ADAM_B2 ** ADAM_STEP)
        d_ref[...] = -ADAM_LR * (m_hat / (jnp.sqrt(v_hat) + ADAM_EPS) + ADAM_WD * w_ref[...])
        nm_ref[...] = nm
        nv_ref[...] = nv

    spec = pl.BlockSpec((1,) * lead + (tr, tc), lambda i, j: (0,) * lead + (i, j))
    return pl.pallas_call(
        body, name=name, grid=(rows // tr, cols // tc), in_specs=[spec] * 4, out_specs=[spec] * 3,
        out_shape=[jax.ShapeDtypeStruct(w.shape, F32)] * 3,
        compiler_params=_cp(("parallel", "parallel"), VMEM_BIG))(w, g, m, v)


def _add_cores(g, from_sib, name):
    n, rows, cols = g.shape
    half = rows // 2
    tr = _tile(half, (256, 240))
    nt = half // tr

    def body(lo_ref, hi_ref, s_ref, o_ref):
        mine = jnp.where(lax.axis_index("c") == 0, lo_ref[0], hi_ref[0])
        o_ref[0] = (mine.astype(F32) + s_ref[0].astype(F32)).astype(BF16)

    return pl.pallas_call(
        body, name=name, grid=(n, nt),
        in_specs=[pl.BlockSpec((1, tr, cols), lambda j, i: (j, i, 0)),
                  pl.BlockSpec((1, tr, cols), lambda j, i: (j, nt + i, 0)),
                  pl.BlockSpec((1, tr, cols), lambda j, i: (j, i, 0))],
        out_specs=pl.BlockSpec((1, tr, cols), lambda j, i: (j, i, 0)),
        out_shape=jax.ShapeDtypeStruct((n, half, cols), BF16),
        compiler_params=_cp(("parallel", "parallel"), VMEM_BIG))(g, g, from_sib)


def _add_chips(x, own, name):
    n, rows, cols = x.shape
    tr = _tile(rows, (256, 240))

    def body(x_ref, own_ref, o_ref):
        me = 2 * lax.axis_index("x") + lax.axis_index("y")
        v = [jnp.where(me == k, own_ref[...], x_ref[k]).astype(F32) for k in range(N_CHIPS)]
        o_ref[...] = ((v[0] + v[1]) + v[2]) + v[3]

    return pl.pallas_call(
        body, name=name, grid=(rows // tr,),
        in_specs=[pl.BlockSpec((n, tr, cols), lambda i: (0, i, 0)), pl.BlockSpec((tr, cols), lambda i: (i, 0))],
        out_specs=pl.BlockSpec((tr, cols), lambda i: (i, 0)),
        out_shape=jax.ShapeDtypeStruct((rows, cols), F32), compiler_params=_cp(("parallel",), VMEM_BIG))(x, own)


def _axes():
    return lax.axis_index("x"), lax.axis_index("y"), lax.axis_index("c")


def _other_chips(x, y):
    return [(1 - x, y), (x, 1 - y), (1 - x, 1 - y)]


ANY = pl.BlockSpec(memory_space=pl.ANY)


def _rcopy(src, dst, send_sems, recv_sems, k, to):
    return pltpu.make_async_remote_copy(src_ref=src, dst_ref=dst, send_sem=send_sems.at[k], recv_sem=recv_sems.at[k],
                                        device_id=to, device_id_type=MESH)


def _gather_weights(shards, meta):
    n = len(shards)

    def body(*refs):
        srcs, meta_ref = refs[:n], refs[n]
        outs, mout_ref = refs[n + 1:2 * n + 1], refs[2 * n + 1]
        send_sems, recv_sems = refs[2 * n + 2:]
        x, y, c = _axes()
        me = 2 * x + y
        sib = (x, y, 1 - c)
        chips = _other_chips(x, y)

        def half(t, chip_idx, cc):
            hr = shards[t].shape[0] // 2
            return outs[t].at[chip_idx, pl.ds(cc * hr, hr), :]

        first = []
        for j, (px, py) in enumerate(chips):
            for t in range(n):
                hr = shards[t].shape[0] // 2
                first.append(_rcopy(srcs[t].at[pl.ds(c * hr, hr), :], half(t, me, c), send_sems, recv_sems,
                                    3 * t + j, (px, py, c)))
            first.append(_rcopy(meta_ref, mout_ref.at[me], send_sems, recv_sems, 3 * n + j, (px, py, c)))
        for cp in first:
            cp.start()
        passed = []
        for j, (px, py) in enumerate(chips):
            src_chip = 2 * px + py
            for t in range(n):
                _rcopy(half(t, src_chip, c), half(t, src_chip, c), send_sems, recv_sems, 3 * t + j, sib).wait_recv()
                fwd = _rcopy(half(t, src_chip, c), half(t, src_chip, c), send_sems, recv_sems, 3 * (n + 1 + t) + j, sib)
                fwd.start()
                passed.append(fwd)
            _rcopy(mout_ref.at[src_chip], mout_ref.at[src_chip], send_sems, recv_sems, 3 * n + j, sib).wait_recv()
        for j, (px, py) in enumerate(chips):
            src_chip = 2 * px + py
            for t in range(n):
                _rcopy(half(t, src_chip, 1 - c), half(t, src_chip, 1 - c), send_sems, recv_sems,
                       3 * (n + 1 + t) + j, sib).wait_recv()
        for cp in first + passed:
            cp.wait_send()

    nsem = 3 * (2 * n + 1)
    return pl.pallas_call(
        body, name="gather_weights", in_specs=[ANY] * (n + 1), out_specs=[ANY] * (n + 1),
        out_shape=[jax.ShapeDtypeStruct((N_CHIPS,) + s.shape, s.dtype) for s in shards]
        + [jax.ShapeDtypeStruct((N_CHIPS,) + meta.shape, meta.dtype)],
        scratch_shapes=[pltpu.SemaphoreType.DMA((nsem,)), pltpu.SemaphoreType.DMA((nsem,))])(*shards, meta)


def _gather_late(shard):
    rows, cols = shard.shape
    hr = rows // 2
    src = jax.new_ref(shard, memory_space=pltpu.MemorySpace.HBM)
    out = jax.empty_ref(jax.ShapeDtypeStruct((N_CHIPS, rows, cols), shard.dtype), memory_space=pltpu.MemorySpace.HBM)

    @pl.kernel(mesh=plsc.ScalarSubcoreMesh(axis_name="seq", num_cores=1), name="gather_late",
               scratch_types=(pltpu.SemaphoreType.DMA((6,)), pltpu.SemaphoreType.DMA((6,))),
               compiler_params=pltpu.CompilerParams(collective_id=1))
    def launch(send_sems, recv_sems):
        x, y, c = _axes()
        me = 2 * x + y
        sib = (x, y, 1 - c)
        chips = _other_chips(x, y)
        barrier = pltpu.get_barrier_semaphore()
        for px, py in chips:
            pl.semaphore_signal(barrier, inc=1, device_id=(px, py, c), device_id_type=MESH)
        pl.semaphore_signal(barrier, inc=1, device_id=sib, device_id_type=MESH)
        pl.semaphore_wait(barrier, 4)

        def half(chip_idx, cc):
            return out.at[chip_idx, pl.ds(cc * hr, hr), :]

        first = [_rcopy(src.at[pl.ds(c * hr, hr), :], half(me, c), send_sems, recv_sems, j, (px, py, c))
                 for j, (px, py) in enumerate(chips)]
        for cp in first:
            cp.start()
        passed = []
        for j, (px, py) in enumerate(chips):
            land = half(2 * px + py, c)
            _rcopy(land, land, send_sems, recv_sems, j, sib).wait_recv()
            fwd = _rcopy(land, land, send_sems, recv_sems, 3 + j, sib)
            fwd.start()
            passed.append(fwd)
        for j, (px, py) in enumerate(chips):
            land = half(2 * px + py, 1 - c)
            _rcopy(land, land, send_sems, recv_sems, 3 + j, sib).wait_recv()
        for cp in first + passed:
            cp.wait_send()

    launch()
    return out[...]


def _swap_halves(gs):
    n = len(gs)
    ncopies = sum(g.shape[0] for g in gs)

    def body(*refs):
        srcs, outs = refs[:n], refs[n:2 * n]
        send_sems, recv_sems = refs[2 * n:]
        x, y, c = _axes()
        cps = []
        for t in range(n):
            hr = gs[t].shape[1] // 2
            for j in range(gs[t].shape[0]):
                cps.append(_rcopy(srcs[t].at[j, pl.ds((1 - c) * hr, hr), :], outs[t].at[j], send_sems, recv_sems,
                                  len(cps), (x, y, 1 - c)))
        for cp in cps:
            cp.start()
        for cp in cps:
            cp.wait()

    return pl.pallas_call(
        body, name="swap_halves", in_specs=[ANY] * n, out_specs=[ANY] * n,
        out_shape=[jax.ShapeDtypeStruct((g.shape[0], g.shape[1] // 2, g.shape[2]), g.dtype) for g in gs],
        scratch_shapes=[pltpu.SemaphoreType.DMA((ncopies,)), pltpu.SemaphoreType.DMA((ncopies,))])(*gs)


def _scatter_chips(parts):
    n = len(parts)
    srcs = [jax.new_ref(p, memory_space=pltpu.MemorySpace.HBM) for p in parts]
    outs = [jax.empty_ref(jax.ShapeDtypeStruct(p.shape, p.dtype), memory_space=pltpu.MemorySpace.HBM) for p in parts]

    @pl.kernel(mesh=plsc.ScalarSubcoreMesh(axis_name="seq", num_cores=1), name="scatter_chips",
               scratch_types=(pltpu.SemaphoreType.DMA((3 * n,)), pltpu.SemaphoreType.DMA((3 * n,))),
               compiler_params=pltpu.CompilerParams(collective_id=0))
    def launch(send_sems, recv_sems):
        x, y, c = _axes()
        me = 2 * x + y
        chips = _other_chips(x, y)
        barrier = pltpu.get_barrier_semaphore()
        for px, py in chips:
            pl.semaphore_signal(barrier, inc=1, device_id=(px, py, c), device_id_type=MESH)
        pl.semaphore_wait(barrier, 3)
        cps = []
        for j, (px, py) in enumerate(chips):
            for t in range(n):
                cps.append(_rcopy(srcs[t].at[2 * px + py], outs[t].at[me], send_sems, recv_sems, 3 * t + j,
                                  (px, py, c)))
        for cp in cps:
            cp.start()
        for cp in cps:
            cp.wait()

    launch()
    return [o[...] for o in outs]


def _swap_reduced(rs):
    n = len(rs)

    def body(*refs):
        srcs, outs = refs[:n], refs[n:2 * n]
        send_sems, recv_sems = refs[2 * n:]
        x, y, c = _axes()
        cps = [_rcopy(srcs[t], outs[t], send_sems, recv_sems, t, (x, y, 1 - c)) for t in range(n)]
        for cp in cps:
            cp.start()
        for cp in cps:
            cp.wait()

    return pl.pallas_call(
        body, name="swap_reduced", in_specs=[ANY] * n, out_specs=[ANY] * n,
        out_shape=[jax.ShapeDtypeStruct(r.shape, r.dtype) for r in rs],
        scratch_shapes=[pltpu.SemaphoreType.DMA((n,)), pltpu.SemaphoreType.DMA((n,))])(*rs)


SMALL_ROWS = 24 + 128


def _allreduce_small(vec):
    def body(v_ref, out_ref, slots, send_sems, recv_sems):
        x, y, c = _axes()
        me = 4 * x + 2 * y + c
        slots[me] = v_ref[...]
        cps = []
        for k in range(1, 8):
            kx, ky, kc = (k >> 2) & 1, (k >> 1) & 1, k & 1
            peer = (1 - x if kx else x, 1 - y if ky else y, 1 - c if kc else c)
            cps.append(_rcopy(v_ref, slots.at[me], send_sems, recv_sems, k - 1, peer))
        for cp in cps:
            cp.start()
        for cp in cps:
            cp.wait()
        tot = slots[0]
        for k in range(1, 8):
            tot = tot + slots[k]
        out_ref[...] = tot

    return pl.pallas_call(
        body, name="allreduce_small",
        in_specs=[pl.BlockSpec(memory_space=pltpu.VMEM)], out_specs=pl.BlockSpec(memory_space=pltpu.VMEM),
        out_shape=jax.ShapeDtypeStruct((SMALL_ROWS, 128), F32),
        scratch_shapes=[pltpu.VMEM((8, SMALL_ROWS, 128), F32), pltpu.SemaphoreType.DMA((7,)),
                        pltpu.SemaphoreType.DMA((7,))])(vec)


def _pack_p2(w_uq, w_ukv, w_br_mla, w_br_fox, w_out, dtype):
    parts = [w_uq.reshape(96, D_MODEL), w_ukv.reshape(64, D_MODEL), w_br_mla, w_br_fox, w_out]
    return jnp.concatenate([p.astype(dtype) for p in parts], axis=0)


def _unpack_p2(pk):
    return pk[0:96].reshape(256, 384), pk[96:160].reshape(128, 512), pk[160:416], pk[416:672], pk[672:928]


def _uq_arrange(w):
    w3 = w.reshape(256, HEADS, 96)
    nope = w3[:, :, :64].reshape(256, PAIRS, 128)
    pe = w3[:, :, 64:].reshape(256, PAIRS, 64)
    return jnp.concatenate([nope, pe, jnp.zeros((256, PAIRS, 64), w.dtype)], axis=2).reshape(256, PAIRS * 256)


def _uq_restore(g):
    g3 = g.reshape(256, PAIRS, 256)
    nope = g3[:, :, :128].reshape(256, HEADS, 64)
    pe = g3[:, :, 128:192].reshape(256, HEADS, 32)
    return jnp.concatenate([nope, pe], axis=2).reshape(256, HEADS * 96)


def _ukv_arrange(w):
    w3 = w.reshape(128, HEADS, 128)
    return jnp.concatenate([w3[:, :, :64].reshape(128, 1024), w3[:, :, 64:].reshape(128, 1024)], axis=1)


def _ukv_restore(g):
    kn = g[:, :1024].reshape(128, HEADS, 64)
    vv = g[:, 1024:].reshape(128, HEADS, 64)
    return jnp.concatenate([kn, vv], axis=2).reshape(128, HEADS * 128)


def _rope_tables(lp):
    r = np.arange(lp)
    pos = np.where(r < N_META, r, np.where(r >= PAD, r - PAD + N_META, 0)).astype(np.float32)
    half = MLA_ROPE // 2
    inv_freq = np.float32(ROPE_THETA) ** (-np.arange(half, dtype=np.float32) / np.float32(half))
    ang = (pos[:, None] * inv_freq[None, :]).astype(np.float32)
    cos, sin = np.cos(ang).astype(np.float32), np.sin(ang).astype(np.float32)
    one, zero = np.ones((lp, 64), np.float32), np.zeros((lp, 64), np.float32)
    return (jnp.asarray(np.concatenate([cos, cos, cos, cos, one], axis=1)),
            jnp.asarray(np.concatenate([-sin, sin, -sin, sin, zero], axis=1)))


def _pad_lanes(v, n=128):
    return jnp.pad(v, ((0, 0), (0, n - v.shape[1])))


def _in_cols(slabs, a, b):
    out = []
    for j in range(N_CHIPS):
        lo, hi = max(a, W_IN_SHARD * j), min(b, W_IN_SHARD * (j + 1))
        if lo < hi:
            out.append(slabs[j][:, lo - W_IN_SHARD * j:hi - W_IN_SHARD * j])
    return out


def _local_step(x2, tgt2, meta_f, w_small, w_attn, w_gate, w_uq_f, w_ukv_f, w_bm, w_bf, w_o, pre_norm_g,
                post_norm_g, mla_q_norm_g, mla_kv_norm_g, fox_forget_b, start_exchange=None):
    s_rows = x2.shape[0]
    lp = PAD + s_rows
    w_uq_a = _uq_arrange(w_uq_f)
    w_ukv_a = _ukv_arrange(w_ukv_f)

    ctab, stab = _rope_tables(lp)
    ii = jnp.arange(BLK)
    tri_lo = (ii[:, None] >= ii[None, :]).astype(BF16)
    tri_up = (ii[:, None] <= ii[None, :]).astype(BF16)
    fb128 = _pad_lanes(fox_forget_b)

    u = _rms_pre(x2, meta_f, pre_norm_g)
    small = _mm(u, w_small, mode="nn", out_dtype=F32, name="proj_small")
    attn = _mm(u, w_attn, mode="nn", out_dtype=BF16, name="proj_attn",
               col_scale=(HEADS * HEAD_DIM, FOX_SCALE * LOG2E))
    gate = _mm(u, w_gate, mode="nn", out_dtype=BF16, name="proj_gate")
    qn, kvn, kr, kb = _small_prep(small, mla_q_norm_g, mla_kv_norm_g, fb128, ctab, stab, tri_lo)
    qcat = _mm(qn, w_uq_a, mode="nn", out_dtype=BF16, name="mla_q", row_ins=(ctab, stab),
               epilogue=lambda tile, c, s: _rope_pairs(tile, c, s) * (MLA_SCALE * LOG2E))
    kv = _mm(kvn, w_ukv_a, mode="nn", out_dtype=BF16, name="mla_kv")

    mla_cols = dict(qcol=0, kcol=lambda p: p, vcol=lambda p: PAIRS + p)
    fox_cols = dict(qcol=0, kcol=lambda p: PAIRS + p, vcol=lambda p: 2 * PAIRS + p)
    o_mla, lse_mla = _attn_fwd(qcat, kv, kv, _transposed_cols(kv, 1, "mla_vt"), kr, rope=True, name="mla_fwd",
                               **mla_cols)
    o_fox, lse_fox = _attn_fwd(attn, attn, attn, _transposed_cols(attn, 2, "fox_vt"), kb, rope=False,
                               name="fox_fwd", **fox_cols)

    a_mla, a_fox = _gate_fwd(o_mla, o_fox, gate)
    y_mla = _mm(a_mla, w_bm, mode="nn", out_dtype=BF16, name="br_mla")
    y_fox = _mm(a_fox, w_bf, mode="nn", out_dtype=BF16, name="br_fox")
    mg = _merge_fwd(gate, y_mla, y_fox)
    mixed = _mm(mg, w_o, mode="nn", out_dtype=F32, name="out_proj")
    dmixed, dy, loss_p, dg_post = _tail(x2, mixed, tgt2, post_norm_g)

    d_w_out = _mm(mg, dmixed, mode="tn", out_dtype=F32, name="d_w_out")
    dm = _mm(dmixed, w_o, mode="nt", out_dtype=BF16, name="d_merge")
    dy_mla, dy_fox, dgate_ab = _merge_bwd(dm, gate, y_mla, y_fox)
    d_w_bm = _mm(a_mla, dy_mla, mode="tn", out_dtype=F32, name="d_w_br_mla")
    d_w_bf = _mm(a_fox, dy_fox, mode="tn", out_dtype=F32, name="d_w_br_fox")
    da_mla = _mm(dy_mla, w_bm, mode="nt", out_dtype=BF16, name="d_a_mla")
    da_fox = _mm(dy_fox, w_bf, mode="nt", out_dtype=BF16, name="d_a_fox")
    do_mla, do_fox, dgate_z, dl_mla, dl_fox = _gate_bwd(da_mla, da_fox, o_mla, o_fox, gate)
    dl_mla, dl_fox = (d[:, :HEADS].T.reshape(PAIRS, 2, lp) for d in (dl_mla, dl_fox))

    dq_a, dkn, dvm, dkr = _attn_bwd(qcat, kv, kv, kr, do_mla, dl_mla, lse_mla, rtabs=(ctab, stab),
                                    scale=MLA_SCALE, name="mla_bwd", **mla_cols)
    dfq, dfk, dfv, dcol, drow = _attn_bwd(attn, attn, attn, kb, do_fox, dl_fox, lse_fox, scale=FOX_SCALE,
                                          name="fox_bwd", **fox_cols)

    d_w_uq_a = _mm(qn, dq_a, mode="tn", out_dtype=F32, name="d_w_uq")
    dqn = _mm(dq_a, w_uq_a, mode="nt", out_dtype=F32, name="d_qn")
    d_w_ukv_a = jnp.concatenate([_mm(kvn, dkn, mode="tn", out_dtype=F32, name="d_w_uk"),
                                 _mm(kvn, dvm, mode="tn", out_dtype=F32, name="d_w_uv")], axis=1)
    dkvn = _mm(dkn, w_ukv_a[:, :1024], mode="nt", out_dtype=F32, name="d_kvn_k")
    dkvn = _mm(dvm, w_ukv_a[:, 1024:], mode="nt", out_dtype=F32, name="d_kvn_v", acc=dkvn)
    dsmall, dg_q, dg_kv, dfb = _small_bwd(small, dqn, dkvn, dkr, dcol, drow, mla_q_norm_g, mla_kv_norm_g,
                                          fb128, ctab, stab, tri_up)

    dw_small = _mm(u, dsmall, mode="tn", out_dtype=BF16, name="d_w_small")
    dw_fq = _mm(u, dfq, mode="tn", out_dtype=BF16, name="d_w_fq")
    dw_fk = _mm(u, dfk, mode="tn", out_dtype=BF16, name="d_w_fk")
    dw_fv = _mm(u, dfv, mode="tn", out_dtype=BF16, name="d_w_fv")
    dw_z = _mm(u, dgate_z, mode="tn", out_dtype=BF16, name="d_w_z")
    dw_g = _mm(u, dgate_ab, mode="tn", out_dtype=BF16, name="d_w_g")
    d_w_in = (dw_small, dw_z, dw_fq, dw_fk, dw_fv, dw_g)
    d_w_uq = _uq_restore(d_w_uq_a)
    d_w_ukv = _ukv_restore(d_w_ukv_a)
    token = start_exchange(d_w_in, d_w_uq, d_w_ukv, d_w_bm, d_w_bf, d_w_out) if start_exchange else None
    du = _mm_sum_nt([(dsmall, w_small), (dfq, w_attn[:, 0:1024]), (dfk, w_attn[:, 1024:2048]),
                     (dfv, w_attn[:, 2048:3072]), (dgate_z, w_gate[:, 0:2048]), (dgate_ab, w_gate[:, 2048:4096])],
                    name="d_u", after=token)
    dx, dmeta, dg_pre = _pre_bwd(du, x2, meta_f, dy, pre_norm_g)
    return (loss_p, dx, dmeta, d_w_in, d_w_uq, d_w_ukv, d_w_bm, d_w_bf, d_w_out, dg_pre, dg_post, dg_q, dg_kv, dfb)


def _w_in_slabs(pieces):
    dw_small, dw_z, dw_fq, dw_fk, dw_fv, dw_g = pieces
    runs = [(dw_small[:, 0:416], C_CQ), (dw_z[:, 0:1024], C_ZMLA), (dw_fq, C_FQ), (dw_fk, C_FK), (dw_fv, C_FV),
            (dw_small[:, 512:528], C_FL), (dw_z[:, 1024:2048], C_ZFOX), (dw_g, C_GA)]
    slabs = []
    for j in range(N_CHIPS):
        lo, hi = W_IN_SHARD * j, W_IN_SHARD * (j + 1)
        cols = [a[:, max(lo, c0) - c0:min(hi, c0 + a.shape[1]) - c0] for a, c0 in runs
                if max(lo, c0) < min(hi, c0 + a.shape[1])]
        slabs.append(jnp.concatenate(cols, axis=1))
    return jnp.stack(slabs, axis=0)


def kernel(x, meta_tokens, pre_norm_g, w_in, fox_forget_b, mla_q_norm_g, mla_kv_norm_g, w_uq, w_ukv, w_br_mla, w_br_fox, w_out, post_norm_g, loss_target, m_meta_tokens, m_pre_norm_g, m_w_in, m_fox_forget_b, m_mla_q_norm_g, m_mla_kv_norm_g, m_w_uq, m_w_ukv, m_w_br_mla, m_w_br_fox, m_w_out, m_post_norm_g, v_meta_tokens, v_pre_norm_g, v_w_in, v_fox_forget_b, v_mla_q_norm_g, v_mla_kv_norm_g, v_w_uq, v_w_ukv, v_w_br_mla, v_w_br_fox, v_w_out, v_post_norm_g):
    me = 2 * lax.axis_index("x") + lax.axis_index("y")
    core = lax.axis_index("c")
    w_in_b = w_in.astype(BF16).reshape(D_MODEL, W_IN_SHARD)
    p2 = _pack_p2(w_uq[0], w_ukv[0], w_br_mla[0], w_br_fox[0], w_out[0], BF16)
    w_in_g, meta_g = _gather_weights([w_in_b], meta_tokens)
    p2_g = _gather_late(lax.optimization_barrier((p2, w_in_g))[0])
    slabs = [jnp.where(me == j, w_in_b, w_in_g[j]) for j in range(N_CHIPS)]
    chip = lax.broadcasted_iota(jnp.int32, (N_CHIPS, 1, 1), 0)
    p2_all = jnp.where(chip == me, p2[None], p2_g)
    w_uq_f = p2_all[:, 0:96].reshape(N_CHIPS, 256, 384).transpose(1, 0, 2).reshape(256, 1536)
    w_ukv_f = p2_all[:, 96:160].reshape(N_CHIPS, 128, 512).transpose(1, 0, 2).reshape(128, 2048)
    w_bm, w_bf, w_o = (p2_all[:, lo:lo + 256].reshape(D_MODEL, D_MODEL) for lo in (160, 416, 672))
    meta_f = jnp.where(chip == me, meta_tokens[None], meta_g).transpose(1, 0, 2).reshape(N_META, D_MODEL)
    kpe = _in_cols(slabs, C_KPE, C_ZMLA)
    w_small = jnp.concatenate(_in_cols(slabs, C_CQ, C_KPE) + kpe + kpe + [jnp.zeros((D_MODEL, 64), BF16)]
                              + _in_cols(slabs, C_FL, C_ZFOX) + [jnp.zeros((D_MODEL, 112), BF16)], axis=1)
    w_attn = jnp.concatenate(_in_cols(slabs, C_FQ, C_FL), axis=1)
    w_gate = jnp.concatenate(_in_cols(slabs, C_ZMLA, C_FQ) + _in_cols(slabs, C_ZFOX, C_END), axis=1)

    exchange = {}

    def start_exchange(d_w_in, d_w_uq, d_w_ukv, d_w_bm, d_w_bf, d_w_out):
        g2 = jnp.concatenate(
            [d_w_uq.reshape(256, N_CHIPS, 384).transpose(1, 0, 2).reshape(N_CHIPS, 96, D_MODEL),
             d_w_ukv.reshape(128, N_CHIPS, 512).transpose(1, 0, 2).reshape(N_CHIPS, 64, D_MODEL)]
            + [g.reshape(N_CHIPS, 256, D_MODEL) for g in (d_w_bm, d_w_bf, d_w_out)], axis=1)
        pieces = [p[None] for p in d_w_in]
        from_sib = _swap_halves(pieces + [g2])
        halves = [_add_cores(p, s, "add_cores_" + nm)[0]
                  for p, s, nm in zip(pieces, from_sib, ("small", "z", "fq", "fk", "fv", "g"))]
        parts = [_w_in_slabs(halves), _add_cores(g2, from_sib[-1], "add_cores_rest")]
        exchange.update(parts=parts, landed=_scatter_chips(parts))
        return parts[0][0, 0:16, 0:LANES]

    (loss_p, dx, dmeta, _, _, _, _, _, _, dg_pre, dg_post, dg_q, dg_kv,
     dfb) = _local_step(x[0], loss_target[0], meta_f, w_small, w_attn, w_gate, w_uq_f, w_ukv_f, w_bm, w_bf, w_o,
                        pre_norm_g, post_norm_g, mla_q_norm_g, mla_kv_norm_g, fox_forget_b, start_exchange)

    mine = [_add_chips(l, lax.dynamic_index_in_dim(p, me, 0, keepdims=False), nm)
            for l, p, nm in zip(exchange["landed"], exchange["parts"], ("add_chips_w_in", "add_chips_rest"))]
    theirs = _swap_reduced(mine)
    g_w_in, g_p2 = [jnp.concatenate([jnp.where(core == 0, a, b), jnp.where(core == 0, b, a)], axis=0)
                    for a, b in zip(mine, theirs)]
    g_w_uq, g_w_ukv, g_w_bm, g_w_bf, g_w_out = _unpack_p2(g_p2)
    g_w_in = g_w_in[None]

    vec = jnp.concatenate([dg_pre.reshape(8, 128), dg_post.reshape(8, 128), dg_q.reshape(2, 128), dg_kv,
                           dfb, _pad_lanes(loss_p), jnp.zeros((3, 128), F32), dmeta.reshape(128, 128)], axis=0)
    tot = _allreduce_small(vec)
    loss = tot[20, 0]
    g_meta = lax.dynamic_slice_in_dim(tot[24:].reshape(N_META, D_MODEL), 256 * me, 256, axis=1)

    def small_pack(pre, post, gq_, gkv_, fb_):
        return jnp.concatenate([pre.reshape(8, 128), post.reshape(8, 128), gq_.reshape(2, 128), gkv_,
                                _pad_lanes(fb_), jnp.zeros((4, 128), F32)], axis=0)

    def small_unpack(t):
        return (t[0:8].reshape(1, 1024), t[8:16].reshape(1, 1024), t[16:18].reshape(1, 256), t[18:19],
                t[19:20, 0:HEADS])

    g_small = jnp.concatenate([tot[0:20], jnp.zeros((4, 128), F32)], axis=0)
    sm = _adamw(small_pack(pre_norm_g, post_norm_g, mla_q_norm_g, mla_kv_norm_g, fox_forget_b), g_small,
                small_pack(m_pre_norm_g, m_post_norm_g, m_mla_q_norm_g, m_mla_kv_norm_g, m_fox_forget_b),
                small_pack(v_pre_norm_g, v_post_norm_g, v_mla_q_norm_g, v_mla_kv_norm_g, v_fox_forget_b),
                "adamw_small")
    g_pre, g_post, g_q, g_kv, g_fb = small_unpack(g_small)
    (d_pre, d_post, d_q, d_kv, d_fb), (nm_pre, nm_post, nm_q, nm_kv, nm_fb), (nv_pre, nv_post, nv_q, nv_kv, nv_fb) = (
        small_unpack(t) for t in sm)

    d_meta, nm_meta, nv_meta = _adamw(meta_tokens, g_meta, m_meta_tokens, v_meta_tokens, "adamw_meta")
    d_win, nm_win, nv_win = (t.T[None] for t in _adamw(w_in[0].T, g_w_in[0].T, m_w_in[0].T, v_w_in[0].T,
                                                       "adamw_w_in"))
    d_wuq, nm_wuq, nv_wuq = _adamw(w_uq[0], g_w_uq, m_w_uq[0], v_w_uq[0], "adamw_w_uq")
    d_wukv, nm_wukv, nv_wukv = _adamw(w_ukv[0], g_w_ukv, m_w_ukv[0], v_w_ukv[0], "adamw_w_ukv")
    d_wbm, nm_wbm, nv_wbm = _adamw(w_br_mla[0], g_w_bm, m_w_br_mla[0], v_w_br_mla[0], "adamw_w_br_mla")
    d_wbf, nm_wbf, nv_wbf = _adamw(w_br_fox[0], g_w_bf, m_w_br_fox[0], v_w_br_fox[0], "adamw_w_br_fox")
    d_wo, nm_wo, nv_wo = _adamw(w_out[0], g_w_out, m_w_out[0], v_w_out[0], "adamw_w_out")

    def group(meta_, pre, win, fb_, q_, kv_, wuq, wukv, wbm, wbf, wo, post):
        return (meta_, pre, win, fb_, q_, kv_, wuq[None], wukv[None], wbm[None], wbf[None], wo[None], post)

    grads = group(g_meta, g_pre, g_w_in, g_fb, g_q, g_kv, g_w_uq, g_w_ukv, g_w_bm, g_w_bf, g_w_out, g_post)
    deltas = group(d_meta, d_pre, d_win, d_fb, d_q, d_kv, d_wuq, d_wukv, d_wbm, d_wbf, d_wo, d_post)
    new_m = group(nm_meta, nm_pre, nm_win, nm_fb, nm_q, nm_kv, nm_wuq, nm_wukv, nm_wbm, nm_wbf, nm_wo, nm_post)
    new_v = group(nv_meta, nv_pre, nv_win, nv_fb, nv_q, nv_kv, nv_wuq, nv_wukv, nv_wbm, nv_wbf, nv_wo, nv_post)
    return (loss, dx[None], *grads, *deltas, *new_m, *new_v)
```
